```python
import math
import jax
import jax.numpy as jnp
from jax import lax
import numpy as np

D_MODEL = 1024
BATCH = 8
SEQ = 2048
DEPTH = 2

RW_HEADS = 8
RW_HEAD_DIM = 64
RW_WIDTH = RW_HEADS * RW_HEAD_DIM
RW_DECAY_RANK = 64
RW_ICLR_RANK = 64
RW_GATE_RANK = 160
RW_GN_EPS = 64e-5
RW_PROJ = 3 * RW_WIDTH + RW_DECAY_RANK + RW_ICLR_RANK + RW_GATE_RANK
SC_WIDTH = 512
SC_CONV = 3
SC_PROJ = 3 * SC_WIDTH
AB_PROJ = RW_PROJ + SC_PROJ
AB_OUT_IN = RW_WIDTH + SC_WIDTH
DIL_PATTERNS = ((128, 1), (512, 4), (2048, 16))
N_GROUPS = 3
DIL_HEADS = 8
DIL_HEAD_DIM = 64
DIL_WIDTH = DIL_HEADS * DIL_HEAD_DIM
DIL_PROJ = N_GROUPS * 3 * DIL_WIDTH
BLOCK = 128
N_BUCKETS = 32
MAX_DISTANCE = 2048
D_FF = 4 * D_MODEL
N_EVEN = (DEPTH + 1) // 2
N_ODD = DEPTH // 2
DEEPNORM_ALPHA = (2 * DEPTH) ** 0.25
DEEPNORM_BETA = (8 * DEPTH) ** -0.25
LN_EPS = 1e-5

kernel_name = 'hybrid_rwkv7_shortconv_dilated_attn_trunk'


def _layer_norm(x, g, b):
    xf = x.astype(jnp.float32)
    mu = jnp.mean(xf, -1, keepdims=True)
    var = jnp.mean(jnp.square(xf - mu), -1, keepdims=True)
    return ((xf - mu) * lax.rsqrt(var + LN_EPS) * g + b).astype(x.dtype)


def _token_shift(p):
    return jnp.pad(p, ((0, 0), (1, 0), (0, 0)))[:, :-1]


def _rwkv7_step(state, inp):
    r, w, k, v, a, b = inp
    sa = jnp.einsum('bhvk,bhk->bhv', state, a)
    state = state * w[:, :, None, :] + sa[..., None] * b[:, :, None, :] + v[..., None] * k[:, :, None, :]
    return state, jnp.einsum('bhvk,bhk->bhv', state, r)


def _causal_depthwise_conv(z, w):
    return lax.conv_general_dilated(z, w[:, None, :].astype(z.dtype), window_strides=(1,),
                                    padding=((SC_CONV - 1, 0),),
                                    dimension_numbers=('NWC', 'WIO', 'NWC'),
                                    feature_group_count=z.shape[-1])


def rwkv_shortconv_mixer(u, w_in, mu, w0, w_up, a0, a_up, g_up, k_k, k_a, r_k, lnx_g, lnx_b, conv_w, w_out):
    f32 = jnp.float32
    B, T, _ = u.shape
    p = u @ w_in
    pa, pb = p[..., :RW_PROJ], p[..., RW_PROJ:]
    pa = pa + mu * (_token_shift(pa) - pa)
    W = RW_WIDTH
    r, k, v, wd, ad, gd = jnp.split(
        pa, [W, 2 * W, 3 * W, 3 * W + RW_DECAY_RANK, 3 * W + RW_DECAY_RANK + RW_ICLR_RANK], axis=-1)
    logw = -jax.nn.softplus(-(w0 + jnp.tanh(wd) @ w_up).astype(f32)) - 0.5
    decay = jnp.exp(-jnp.exp(logw))
    iclr = jax.nn.sigmoid((a0 + ad @ a_up).astype(f32))
    gate = (jax.nn.sigmoid(gd) @ g_up).astype(f32)

    def heads(t):
        return t.astype(f32).reshape(B, T, RW_HEADS, RW_HEAD_DIM)

    kf = k.astype(f32)
    kk = heads(kf * k_k)
    kk = kk / jnp.maximum(jnp.linalg.norm(kk, axis=-1, keepdims=True), 1e-12)
    k_h = heads(kf * (1.0 + (iclr - 1.0) * k_a))
    r_h, v_h, w_h, a_h = heads(r), heads(v), heads(decay), heads(iclr)
    xs = tuple(jnp.moveaxis(t, 1, 0) for t in (r_h, w_h, k_h, v_h, -kk, kk * a_h))
    s0 = jnp.zeros((B, RW_HEADS, RW_HEAD_DIM, RW_HEAD_DIM), f32)
    _, y = lax.scan(_rwkv7_step, s0, xs)
    y = jnp.moveaxis(y, 0, 1)
    mean = jnp.mean(y, -1, keepdims=True)
    var = jnp.mean(jnp.square(y - mean), -1, keepdims=True)
    y = ((y - mean) * lax.rsqrt(var + RW_GN_EPS)).reshape(B, T, W) * lnx_g + lnx_b
    bonus = jnp.sum(r_h * k_h * r_k, -1, keepdims=True) * v_h
    y_a = ((y + bonus.reshape(B, T, W)) * gate).astype(u.dtype)
    h, b_gate, c_gate = jnp.split(pb, 3, axis=-1)
    y_b = b_gate * _causal_depthwise_conv(c_gate * h, conv_w)
    return jnp.concatenate([y_a, y_b], axis=-1) @ w_out


def _t5_bucket(dist):
    exact = N_BUCKETS // 2
    logd = jnp.log(jnp.maximum(dist, 1).astype(jnp.float32) / exact) / math.log(MAX_DISTANCE / exact)
    large = jnp.minimum(exact + (logd * (N_BUCKETS - exact)).astype(jnp.int32), N_BUCKETS - 1)
    return jnp.where(dist < exact, dist, large)


def _to_blocks(x, dil, nb):
    b, t = x.shape[:2]
    rest = x.shape[2:]
    L = t // dil
    x = jnp.moveaxis(x.reshape((b, L, dil) + rest), 2, 1)
    pad = [(0, 0)] * x.ndim
    pad[2] = (0, nb * BLOCK - L)
    return jnp.pad(x, pad).reshape((b, dil, nb, BLOCK) + rest)


def _from_blocks(x, L):
    b, dil, nb, blk = x.shape[:4]
    rest = x.shape[4:]
    x = x.reshape((b, dil, nb * blk) + rest)[:, :, :L]
    return jnp.moveaxis(x, 1, 2).reshape((b, L * dil) + rest)


def _with_prev_block(xb):
    prev = jnp.pad(xb, ((0, 0), (0, 0), (1, 0)) + ((0, 0),) * (xb.ndim - 3))[:, :, :-1]
    return jnp.concatenate([prev, xb], axis=3)


def _dilated_group_attention(q, k, v, bias, band, dil):
    t = q.shape[1]
    L = t // dil
    nb = -(-L // BLOCK)
    qb = _to_blocks(q, dil, nb)
    kc = _with_prev_block(_to_blocks(k, dil, nb))
    vc = _with_prev_block(_to_blocks(v, dil, nb))
    s = jnp.einsum('brnqhd,brnkhd->brnhqk', qb, kc).astype(jnp.float32) * (DIL_HEAD_DIM ** -0.5) + bias
    kidx = jnp.arange(2 * BLOCK)
    not_pad = (jnp.arange(nb) > 0)[:, None, None] | (kidx >= BLOCK)[None, None, :]
    valid = (band[None] & not_pad)[None, None, :, None]
    s = jnp.where(valid, s, -jnp.inf)
    m = jnp.max(s, -1, keepdims=True)
    e = jnp.exp(s - m)
    den = jnp.sum(e, -1, keepdims=True)
    o = jnp.einsum('brnhqk,brnkhd->brnqhd', (e / den).astype(v.dtype), vc)
    lse = jnp.moveaxis((m + jnp.log(den))[..., 0], 3, 4)
    return _from_blocks(o, L), _from_blocks(lse, L)


def dilated_attention_mixer(u, w_qkv, w_out, rel_bias):
    B, T, _ = u.shape
    p = (u @ w_qkv).reshape(B, T, N_GROUPS, 3, DIL_HEADS, DIL_HEAD_DIM)
    qi = jnp.arange(BLOCK)[:, None]
    ki = jnp.arange(2 * BLOCK)[None, :]
    rel = BLOCK + qi - ki
    outs, lses = [], []
    for g, (window, dil) in enumerate(DIL_PATTERNS):
        span = window // dil
        band = (rel >= 0) & (rel <= span)
        bucket = _t5_bucket(jnp.clip(rel, 0, span) * dil)
        bias = jnp.transpose(rel_bias[bucket][..., g * DIL_HEADS:(g + 1) * DIL_HEADS], (2, 0, 1)).astype(jnp.float32)
        o, lse = _dilated_group_attention(p[:, :, g, 0], p[:, :, g, 1], p[:, :, g, 2], bias, band, dil)
        outs.append(o.astype(jnp.float32))
        lses.append(lse)
    wts = jax.nn.softmax(jnp.stack(lses), axis=0)
    o = jnp.sum(wts[..., None] * jnp.stack(outs), axis=0)
    return o.reshape(B, T, DIL_WIDTH).astype(u.dtype) @ w_out


def sq_relu_mlp(u, w1, w2):
    return jnp.square(jax.nn.relu(u @ w1)) @ w2


def _fwd_setup_inputs(seed: int = 0) -> dict:
    key = jax.random.key(seed)
    ks = iter(jax.random.split(key, 32))
    D = D_MODEL

    def nrm(shape, s):
        return s * jax.random.normal(next(ks), shape, jnp.float32)

    return {
        'x': nrm((BATCH, SEQ, D), 1.0),
        'c': nrm((BATCH, D), 1.0),
        'ada_w': nrm((DEPTH, D, 6 * D), 0.1 * D ** -0.5),
        'ada_b': nrm((DEPTH, 6 * D), 0.01),
        'ln_g': 1.0 + nrm((DEPTH, 2, D), 0.02),
        'ln_b': nrm((DEPTH, 2, D), 0.02),
        'ab_w_in': nrm((N_EVEN, D, AB_PROJ), D ** -0.5),
        'rw_mu': jax.random.uniform(next(ks), (N_EVEN, RW_PROJ), jnp.float32),
        'rw_w0': -2.0 + nrm((N_EVEN, RW_WIDTH), 1.0),
        'rw_w_up': nrm((N_EVEN, RW_DECAY_RANK, RW_WIDTH), RW_DECAY_RANK ** -0.5),
        'rw_a0': nrm((N_EVEN, RW_WIDTH), 0.5),
        'rw_a_up': nrm((N_EVEN, RW_ICLR_RANK, RW_WIDTH), RW_ICLR_RANK ** -0.5),
        'rw_g_up': nrm((N_EVEN, RW_GATE_RANK, RW_WIDTH), RW_GATE_RANK ** -0.5),
        'rw_k_k': 0.85 + nrm((N_EVEN, RW_WIDTH), 0.05),
        'rw_k_a': 1.0 + nrm((N_EVEN, RW_WIDTH), 0.05),
        'rw_r_k': nrm((N_EVEN, RW_HEADS, RW_HEAD_DIM), 0.1),
        'rw_lnx_g': 1.0 + nrm((N_EVEN, RW_WIDTH), 0.02),
        'rw_lnx_b': nrm((N_EVEN, RW_WIDTH), 0.02),
        'sc_conv_w': nrm((N_EVEN, SC_CONV, SC_WIDTH), SC_CONV ** -0.5),
        'ab_w_out': nrm((N_EVEN, AB_OUT_IN, D), DEEPNORM_BETA * AB_OUT_IN ** -0.5),
        'dil_w_qkv': nrm((N_ODD, D, DIL_PROJ), D ** -0.5),
        'dil_w_out': nrm((N_ODD, DIL_WIDTH, D), DEEPNORM_BETA * DIL_WIDTH ** -0.5),
        'rel_bias': nrm((N_BUCKETS, N_GROUPS * DIL_HEADS), 0.5),
        'mlp_w1': nrm((DEPTH, D, D_FF), D ** -0.5),
        'mlp_w2': nrm((DEPTH, D_FF, D), DEEPNORM_BETA * D_FF ** -0.5),
    }


def _fwd_reference(x, c, ada_w, ada_b, ln_g, ln_b, ab_w_in, rw_mu, rw_w0, rw_w_up, rw_a0, rw_a_up, rw_g_up,
              rw_k_k, rw_k_a, rw_r_k, rw_lnx_g, rw_lnx_b, sc_conv_w, ab_w_out, dil_w_qkv, dil_w_out,
              rel_bias, mlp_w1, mlp_w2):
    cond = jax.nn.silu(c)
    for i in range(DEPTH):
        mod = (cond @ ada_w[i] + ada_b[i])[:, None, :]
        sh1, sc1, g1, sh2, sc2, g2 = jnp.split(mod, 6, axis=-1)
        j = i // 2
        u = x * (1 + sc1) + sh1
        if i % 2 == 0:
            y = rwkv_shortconv_mixer(u, ab_w_in[j], rw_mu[j], rw_w0[j], rw_w_up[j], rw_a0[j], rw_a_up[j],
                                     rw_g_up[j], rw_k_k[j], rw_k_a[j], rw_r_k[j], rw_lnx_g[j], rw_lnx_b[j],
                                     sc_conv_w[j], ab_w_out[j])
        else:
            y = dilated_attention_mixer(u, dil_w_qkv[j], dil_w_out[j], rel_bias)
        x = _layer_norm(DEEPNORM_ALPHA * x + (1 + g1) * y, ln_g[i, 0], ln_b[i, 0])
        u = x * (1 + sc2) + sh2
        y = sq_relu_mlp(u, mlp_w1[i], mlp_w2[i])
        x = _layer_norm(DEEPNORM_ALPHA * x + (1 + g2) * y, ln_g[i, 1], ln_b[i, 1])
    return x


import jax as _jax
import jax.numpy as _jnp

TWIN_FORMAT = 'train_step'
FWD_PARAMS = ['x', 'c', 'ada_w', 'ada_b', 'ln_g', 'ln_b', 'ab_w_in', 'rw_mu', 'rw_w0', 'rw_w_up', 'rw_a0', 'rw_a_up', 'rw_g_up', 'rw_k_k', 'rw_k_a', 'rw_r_k', 'rw_lnx_g', 'rw_lnx_b', 'sc_conv_w', 'ab_w_out', 'dil_w_qkv', 'dil_w_out', 'rel_bias', 'mlp_w1', 'mlp_w2']
TWIN_WEIGHTS = ['ada_w', 'ada_b', 'ln_g', 'ln_b', 'ab_w_in', 'rw_mu', 'rw_w0', 'rw_w_up', 'rw_a0', 'rw_a_up', 'rw_g_up', 'rw_k_k', 'rw_k_a', 'rw_r_k', 'rw_lnx_g', 'rw_lnx_b', 'sc_conv_w', 'ab_w_out', 'dil_w_qkv', 'dil_w_out', 'rel_bias', 'mlp_w1', 'mlp_w2']
TWIN_DIFF_INPUT = 'x'
TWIN_INPUTS = ['x', 'c', 'ada_w', 'ada_b', 'ln_g', 'ln_b', 'ab_w_in', 'rw_mu', 'rw_w0', 'rw_w_up', 'rw_a0', 'rw_a_up', 'rw_g_up', 'rw_k_k', 'rw_k_a', 'rw_r_k', 'rw_lnx_g', 'rw_lnx_b', 'sc_conv_w', 'ab_w_out', 'dil_w_qkv', 'dil_w_out', 'rel_bias', 'mlp_w1', 'mlp_w2', 'loss_target', 'm_ada_w', 'm_ada_b', 'm_ln_g', 'm_ln_b', 'm_ab_w_in', 'm_rw_mu', 'm_rw_w0', 'm_rw_w_up', 'm_rw_a0', 'm_rw_a_up', 'm_rw_g_up', 'm_rw_k_k', 'm_rw_k_a', 'm_rw_r_k', 'm_rw_lnx_g', 'm_rw_lnx_b', 'm_sc_conv_w', 'm_ab_w_out', 'm_dil_w_qkv', 'm_dil_w_out', 'm_rel_bias', 'm_mlp_w1', 'm_mlp_w2', 'v_ada_w', 'v_ada_b', 'v_ln_g', 'v_ln_b', 'v_ab_w_in', 'v_rw_mu', 'v_rw_w0', 'v_rw_w_up', 'v_rw_a0', 'v_rw_a_up', 'v_rw_g_up', 'v_rw_k_k', 'v_rw_k_a', 'v_rw_r_k', 'v_rw_lnx_g', 'v_rw_lnx_b', 'v_sc_conv_w', 'v_ab_w_out', 'v_dil_w_qkv', 'v_dil_w_out', 'v_rel_bias', 'v_mlp_w1', 'v_mlp_w2']
TWIN_OUTPUTS = ['loss', 'grad_x', 'grad_ada_w', 'grad_ada_b', 'grad_ln_g', 'grad_ln_b', 'grad_ab_w_in', 'grad_rw_mu', 'grad_rw_w0', 'grad_rw_w_up', 'grad_rw_a0', 'grad_rw_a_up', 'grad_rw_g_up', 'grad_rw_k_k', 'grad_rw_k_a', 'grad_rw_r_k', 'grad_rw_lnx_g', 'grad_rw_lnx_b', 'grad_sc_conv_w', 'grad_ab_w_out', 'grad_dil_w_qkv', 'grad_dil_w_out', 'grad_rel_bias', 'grad_mlp_w1', 'grad_mlp_w2', 'delta_ada_w', 'delta_ada_b', 'delta_ln_g', 'delta_ln_b', 'delta_ab_w_in', 'delta_rw_mu', 'delta_rw_w0', 'delta_rw_w_up', 'delta_rw_a0', 'delta_rw_a_up', 'delta_rw_g_up', 'delta_rw_k_k', 'delta_rw_k_a', 'delta_rw_r_k', 'delta_rw_lnx_g', 'delta_rw_lnx_b', 'delta_sc_conv_w', 'delta_ab_w_out', 'delta_dil_w_qkv', 'delta_dil_w_out', 'delta_rel_bias', 'delta_mlp_w1', 'delta_mlp_w2', 'new_m_ada_w', 'new_m_ada_b', 'new_m_ln_g', 'new_m_ln_b', 'new_m_ab_w_in', 'new_m_rw_mu', 'new_m_rw_w0', 'new_m_rw_w_up', 'new_m_rw_a0', 'new_m_rw_a_up', 'new_m_rw_g_up', 'new_m_rw_k_k', 'new_m_rw_k_a', 'new_m_rw_r_k', 'new_m_rw_lnx_g', 'new_m_rw_lnx_b', 'new_m_sc_conv_w', 'new_m_ab_w_out', 'new_m_dil_w_qkv', 'new_m_dil_w_out', 'new_m_rel_bias', 'new_m_mlp_w1', 'new_m_mlp_w2', 'new_v_ada_w', 'new_v_ada_b', 'new_v_ln_g', 'new_v_ln_b', 'new_v_ab_w_in', 'new_v_rw_mu', 'new_v_rw_w0', 'new_v_rw_w_up', 'new_v_rw_a0', 'new_v_rw_a_up', 'new_v_rw_g_up', 'new_v_rw_k_k', 'new_v_rw_k_a', 'new_v_rw_r_k', 'new_v_rw_lnx_g', 'new_v_rw_lnx_b', 'new_v_sc_conv_w', 'new_v_ab_w_out', 'new_v_dil_w_qkv', 'new_v_dil_w_out', 'new_v_rel_bias', 'new_v_mlp_w1', 'new_v_mlp_w2']
TWIN_LEAF_KINDS = {'loss': 'loss', 'grad_x': 'grad_x', 'grad_ada_w': 'grad_w', 'grad_ada_b': 'grad_w', 'grad_ln_g': 'grad_w', 'grad_ln_b': 'grad_w', 'grad_ab_w_in': 'grad_w', 'grad_rw_mu': 'grad_w', 'grad_rw_w0': 'grad_w', 'grad_rw_w_up': 'grad_w', 'grad_rw_a0': 'grad_w', 'grad_rw_a_up': 'grad_w', 'grad_rw_g_up': 'grad_w', 'grad_rw_k_k': 'grad_w', 'grad_rw_k_a': 'grad_w', 'grad_rw_r_k': 'grad_w', 'grad_rw_lnx_g': 'grad_w', 'grad_rw_lnx_b': 'grad_w', 'grad_sc_conv_w': 'grad_w', 'grad_ab_w_out': 'grad_w', 'grad_dil_w_qkv': 'grad_w', 'grad_dil_w_out': 'grad_w', 'grad_rel_bias': 'grad_w', 'grad_mlp_w1': 'grad_w', 'grad_mlp_w2': 'grad_w', 'delta_ada_w': 'delta_w', 'delta_ada_b': 'delta_w', 'delta_ln_g': 'delta_w', 'delta_ln_b': 'delta_w', 'delta_ab_w_in': 'delta_w', 'delta_rw_mu': 'delta_w', 'delta_rw_w0': 'delta_w', 'delta_rw_w_up': 'delta_w', 'delta_rw_a0': 'delta_w', 'delta_rw_a_up': 'delta_w', 'delta_rw_g_up': 'delta_w', 'delta_rw_k_k': 'delta_w', 'delta_rw_k_a': 'delta_w', 'delta_rw_r_k': 'delta_w', 'delta_rw_lnx_g': 'delta_w', 'delta_rw_lnx_b': 'delta_w', 'delta_sc_conv_w': 'delta_w', 'delta_ab_w_out': 'delta_w', 'delta_dil_w_qkv': 'delta_w', 'delta_dil_w_out': 'delta_w', 'delta_rel_bias': 'delta_w', 'delta_mlp_w1': 'delta_w', 'delta_mlp_w2': 'delta_w', 'new_m_ada_w': 'new_m', 'new_m_ada_b': 'new_m', 'new_m_ln_g': 'new_m', 'new_m_ln_b': 'new_m', 'new_m_ab_w_in': 'new_m', 'new_m_rw_mu': 'new_m', 'new_m_rw_w0': 'new_m', 'new_m_rw_w_up': 'new_m', 'new_m_rw_a0': 'new_m', 'new_m_rw_a_up': 'new_m', 'new_m_rw_g_up': 'new_m', 'new_m_rw_k_k': 'new_m', 'new_m_rw_k_a': 'new_m', 'new_m_rw_r_k': 'new_m', 'new_m_rw_lnx_g': 'new_m', 'new_m_rw_lnx_b': 'new_m', 'new_m_sc_conv_w': 'new_m', 'new_m_ab_w_out': 'new_m', 'new_m_dil_w_qkv': 'new_m', 'new_m_dil_w_out': 'new_m', 'new_m_rel_bias': 'new_m', 'new_m_mlp_w1': 'new_m', 'new_m_mlp_w2': 'new_m', 'new_v_ada_w': 'new_v', 'new_v_ada_b': 'new_v', 'new_v_ln_g': 'new_v', 'new_v_ln_b': 'new_v', 'new_v_ab_w_in': 'new_v', 'new_v_rw_mu': 'new_v', 'new_v_rw_w0': 'new_v', 'new_v_rw_w_up': 'new_v', 'new_v_rw_a0': 'new_v', 'new_v_rw_a_up': 'new_v', 'new_v_rw_g_up': 'new_v', 'new_v_rw_k_k': 'new_v', 'new_v_rw_k_a': 'new_v', 'new_v_rw_r_k': 'new_v', 'new_v_rw_lnx_g': 'new_v', 'new_v_rw_lnx_b': 'new_v', 'new_v_sc_conv_w': 'new_v', 'new_v_ab_w_out': 'new_v', 'new_v_dil_w_qkv': 'new_v', 'new_v_dil_w_out': 'new_v', 'new_v_rel_bias': 'new_v', 'new_v_mlp_w1': 'new_v', 'new_v_mlp_w2': 'new_v'}


def _forward(args):
    return _fwd_reference(*[args[k] for k in FWD_PARAMS])


def _output_shape():
    out = _jax.eval_shape(lambda: _forward(_fwd_setup_inputs(0)))
    return out.shape, out.dtype

N_MICROBATCH = 1
ADAM_LR = 0.001
ADAM_B1 = 0.9
ADAM_B2 = 0.999
ADAM_EPS = 1e-08
ADAM_WD = 0.01
ADAM_STEP = 10
PER_EXAMPLE_BATCH_AXIS = {'x': 0, 'c': 0, 'loss_target': 0}
SHARED_INPUTS = []
_WEIGHT_DTYPES = {'ada_w': _jnp.float32, 'ada_b': _jnp.float32, 'ln_g': _jnp.float32, 'ln_b': _jnp.float32, 'ab_w_in': _jnp.float32, 'rw_mu': _jnp.float32, 'rw_w0': _jnp.float32, 'rw_w_up': _jnp.float32, 'rw_a0': _jnp.float32, 'rw_a_up': _jnp.float32, 'rw_g_up': _jnp.float32, 'rw_k_k': _jnp.float32, 'rw_k_a': _jnp.float32, 'rw_r_k': _jnp.float32, 'rw_lnx_g': _jnp.float32, 'rw_lnx_b': _jnp.float32, 'sc_conv_w': _jnp.float32, 'ab_w_out': _jnp.float32, 'dil_w_qkv': _jnp.float32, 'dil_w_out': _jnp.float32, 'rel_bias': _jnp.float32, 'mlp_w1': _jnp.float32, 'mlp_w2': _jnp.float32}
MOMENT_SCALE = {'ada_w': 3.445281e-02, 'ada_b': 6.751240e-02, 'ln_g': 8.047459e+00, 'ln_b': 1.876782e+00, 'ab_w_in': 3.601119e-02, 'rw_mu': 4.286913e-02, 'rw_w0': 1.272660e-02, 'rw_w_up': 2.386006e-03, 'rw_a0': 1.025153e-02, 'rw_a_up': 8.477887e-03, 'rw_g_up': 2.554177e-02, 'rw_k_k': 1.103782e-02, 'rw_k_a': 2.918505e-02, 'rw_r_k': 5.847848e-02, 'rw_lnx_g': 2.747140e-02, 'rw_lnx_b': 4.901343e-02, 'sc_conv_w': 4.574466e-02, 'ab_w_out': 7.330597e-02, 'dil_w_qkv': 7.743605e-03, 'dil_w_out': 2.420302e-02, 'rel_bias': 9.645160e-03, 'mlp_w1': 2.962334e-02, 'mlp_w2': 1.328453e-01}


def _to_microbatches(a, axis):
    t = _jnp.moveaxis(a, axis, 0)
    t = t.reshape((N_MICROBATCH, t.shape[0] // N_MICROBATCH) + t.shape[1:])
    return _jnp.moveaxis(t, 1, axis + 1)


def setup_inputs(seed: int = 0) -> dict:
    inp = _fwd_setup_inputs(seed)
    key = _jax.random.fold_in(_jax.random.key(seed), 7919)
    shape, _ = _output_shape()
    out = dict(inp)
    out["loss_target"] = _jax.random.normal(_jax.random.fold_in(key, 0), shape, _jnp.float32)
    for i, name in enumerate(TWIN_WEIGHTS):
        w = inp[name].astype(_jnp.float32)
        if MOMENT_SCALE is None:
            s = _jnp.sqrt(_jnp.mean(_jnp.square(w)) + 1e-30)
        else:
            s = MOMENT_SCALE[name]
        km, kv = _jax.random.split(_jax.random.fold_in(key, i + 1))
        out[name] = w
        out["m_" + name] = s * _jax.random.normal(km, w.shape, _jnp.float32)
        out["v_" + name] = (s * s) * _jax.random.uniform(kv, w.shape, _jnp.float32, 0.5, 1.5)
    if N_MICROBATCH > 1:
        for name, axis in PER_EXAMPLE_BATCH_AXIS.items():
            out[name] = _to_microbatches(out[name], axis)
    return {'x': out['x'], 'c': out['c'], 'ada_w': out['ada_w'], 'ada_b': out['ada_b'], 'ln_g': out['ln_g'], 'ln_b': out['ln_b'], 'ab_w_in': out['ab_w_in'], 'rw_mu': out['rw_mu'], 'rw_w0': out['rw_w0'], 'rw_w_up': out['rw_w_up'], 'rw_a0': out['rw_a0'], 'rw_a_up': out['rw_a_up'], 'rw_g_up': out['rw_g_up'], 'rw_k_k': out['rw_k_k'], 'rw_k_a': out['rw_k_a'], 'rw_r_k': out['rw_r_k'], 'rw_lnx_g': out['rw_lnx_g'], 'rw_lnx_b': out['rw_lnx_b'], 'sc_conv_w': out['sc_conv_w'], 'ab_w_out': out['ab_w_out'], 'dil_w_qkv': out['dil_w_qkv'], 'dil_w_out': out['dil_w_out'], 'rel_bias': out['rel_bias'], 'mlp_w1': out['mlp_w1'], 'mlp_w2': out['mlp_w2'], 'loss_target': out['loss_target'], 'm_ada_w': out['m_ada_w'], 'm_ada_b': out['m_ada_b'], 'm_ln_g': out['m_ln_g'], 'm_ln_b': out['m_ln_b'], 'm_ab_w_in': out['m_ab_w_in'], 'm_rw_mu': out['m_rw_mu'], 'm_rw_w0': out['m_rw_w0'], 'm_rw_w_up': out['m_rw_w_up'], 'm_rw_a0': out['m_rw_a0'], 'm_rw_a_up': out['m_rw_a_up'], 'm_rw_g_up': out['m_rw_g_up'], 'm_rw_k_k': out['m_rw_k_k'], 'm_rw_k_a': out['m_rw_k_a'], 'm_rw_r_k': out['m_rw_r_k'], 'm_rw_lnx_g': out['m_rw_lnx_g'], 'm_rw_lnx_b': out['m_rw_lnx_b'], 'm_sc_conv_w': out['m_sc_conv_w'], 'm_ab_w_out': out['m_ab_w_out'], 'm_dil_w_qkv': out['m_dil_w_qkv'], 'm_dil_w_out': out['m_dil_w_out'], 'm_rel_bias': out['m_rel_bias'], 'm_mlp_w1': out['m_mlp_w1'], 'm_mlp_w2': out['m_mlp_w2'], 'v_ada_w': out['v_ada_w'], 'v_ada_b': out['v_ada_b'], 'v_ln_g': out['v_ln_g'], 'v_ln_b': out['v_ln_b'], 'v_ab_w_in': out['v_ab_w_in'], 'v_rw_mu': out['v_rw_mu'], 'v_rw_w0': out['v_rw_w0'], 'v_rw_w_up': out['v_rw_w_up'], 'v_rw_a0': out['v_rw_a0'], 'v_rw_a_up': out['v_rw_a_up'], 'v_rw_g_up': out['v_rw_g_up'], 'v_rw_k_k': out['v_rw_k_k'], 'v_rw_k_a': out['v_rw_k_a'], 'v_rw_r_k': out['v_rw_r_k'], 'v_rw_lnx_g': out['v_rw_lnx_g'], 'v_rw_lnx_b': out['v_rw_lnx_b'], 'v_sc_conv_w': out['v_sc_conv_w'], 'v_ab_w_out': out['v_ab_w_out'], 'v_dil_w_qkv': out['v_dil_w_qkv'], 'v_dil_w_out': out['v_dil_w_out'], 'v_rel_bias': out['v_rel_bias'], 'v_mlp_w1': out['v_mlp_w1'], 'v_mlp_w2': out['v_mlp_w2']}


def _loss(weights, diff, rest, loss_target):
    with _jax.named_scope("forward"):
        args = {**rest, TWIN_DIFF_INPUT: diff, **{k: w.astype(_WEIGHT_DTYPES[k]) for k, w in weights.items()}}
        y = _forward(args)
    with _jax.named_scope("loss_head"):
        err = _jnp.square(y.astype(_jnp.float32) - loss_target)
        return 0.5 * _jnp.sum(_jnp.mean(err, axis=-1)) if err.ndim else 0.5 * err


def _adamw(w, g, m, v):
    m = ADAM_B1 * m + (1.0 - ADAM_B1) * g
    v = ADAM_B2 * v + (1.0 - ADAM_B2) * _jnp.square(g)
    m_hat = m / (1.0 - ADAM_B1 ** ADAM_STEP)
    v_hat = v / (1.0 - ADAM_B2 ** ADAM_STEP)
    delta = -ADAM_LR * (m_hat / (_jnp.sqrt(v_hat) + ADAM_EPS) + ADAM_WD * w)
    return delta, m, v


def reference(x, c, ada_w, ada_b, ln_g, ln_b, ab_w_in, rw_mu, rw_w0, rw_w_up, rw_a0, rw_a_up, rw_g_up, rw_k_k, rw_k_a, rw_r_k, rw_lnx_g, rw_lnx_b, sc_conv_w, ab_w_out, dil_w_qkv, dil_w_out, rel_bias, mlp_w1, mlp_w2, loss_target, m_ada_w, m_ada_b, m_ln_g, m_ln_b, m_ab_w_in, m_rw_mu, m_rw_w0, m_rw_w_up, m_rw_a0, m_rw_a_up, m_rw_g_up, m_rw_k_k, m_rw_k_a, m_rw_r_k, m_rw_lnx_g, m_rw_lnx_b, m_sc_conv_w, m_ab_w_out, m_dil_w_qkv, m_dil_w_out, m_rel_bias, m_mlp_w1, m_mlp_w2, v_ada_w, v_ada_b, v_ln_g, v_ln_b, v_ab_w_in, v_rw_mu, v_rw_w0, v_rw_w_up, v_rw_a0, v_rw_a_up, v_rw_g_up, v_rw_k_k, v_rw_k_a, v_rw_r_k, v_rw_lnx_g, v_rw_lnx_b, v_sc_conv_w, v_ab_w_out, v_dil_w_qkv, v_dil_w_out, v_rel_bias, v_mlp_w1, v_mlp_w2):
    given = dict(x=x, c=c, ada_w=ada_w, ada_b=ada_b, ln_g=ln_g, ln_b=ln_b, ab_w_in=ab_w_in, rw_mu=rw_mu, rw_w0=rw_w0, rw_w_up=rw_w_up, rw_a0=rw_a0, rw_a_up=rw_a_up, rw_g_up=rw_g_up, rw_k_k=rw_k_k, rw_k_a=rw_k_a, rw_r_k=rw_r_k, rw_lnx_g=rw_lnx_g, rw_lnx_b=rw_lnx_b, sc_conv_w=sc_conv_w, ab_w_out=ab_w_out, dil_w_qkv=dil_w_qkv, dil_w_out=dil_w_out, rel_bias=rel_bias, mlp_w1=mlp_w1, mlp_w2=mlp_w2, loss_target=loss_target, m_ada_w=m_ada_w, m_ada_b=m_ada_b, m_ln_g=m_ln_g, m_ln_b=m_ln_b, m_ab_w_in=m_ab_w_in, m_rw_mu=m_rw_mu, m_rw_w0=m_rw_w0, m_rw_w_up=m_rw_w_up, m_rw_a0=m_rw_a0, m_rw_a_up=m_rw_a_up, m_rw_g_up=m_rw_g_up, m_rw_k_k=m_rw_k_k, m_rw_k_a=m_rw_k_a, m_rw_r_k=m_rw_r_k, m_rw_lnx_g=m_rw_lnx_g, m_rw_lnx_b=m_rw_lnx_b, m_sc_conv_w=m_sc_conv_w, m_ab_w_out=m_ab_w_out, m_dil_w_qkv=m_dil_w_qkv, m_dil_w_out=m_dil_w_out, m_rel_bias=m_rel_bias, m_mlp_w1=m_mlp_w1, m_mlp_w2=m_mlp_w2, v_ada_w=v_ada_w, v_ada_b=v_ada_b, v_ln_g=v_ln_g, v_ln_b=v_ln_b, v_ab_w_in=v_ab_w_in, v_rw_mu=v_rw_mu, v_rw_w0=v_rw_w0, v_rw_w_up=v_rw_w_up, v_rw_a0=v_rw_a0, v_rw_a_up=v_rw_a_up, v_rw_g_up=v_rw_g_up, v_rw_k_k=v_rw_k_k, v_rw_k_a=v_rw_k_a, v_rw_r_k=v_rw_r_k, v_rw_lnx_g=v_rw_lnx_g, v_rw_lnx_b=v_rw_lnx_b, v_sc_conv_w=v_sc_conv_w, v_ab_w_out=v_ab_w_out, v_dil_w_qkv=v_dil_w_qkv, v_dil_w_out=v_dil_w_out, v_rel_bias=v_rel_bias, v_mlp_w1=v_mlp_w1, v_mlp_w2=v_mlp_w2)
    weights = {n: given[n] for n in TWIN_WEIGHTS}
    shared = {n: given[n] for n in SHARED_INPUTS}
    per_example = {n: given[n] for n in ['x', 'c']}
    grad_fn = _jax.value_and_grad(_loss, argnums=(0, 1))

    def one_microbatch(ex, loss_target):
        ex = dict(ex)
        diff = ex.pop(TWIN_DIFF_INPUT)
        return grad_fn(weights, diff, {**shared, **ex}, loss_target)

    if N_MICROBATCH == 1:
        loss, (grad_w, grad_x) = one_microbatch(per_example, given["loss_target"])
    else:
        def body(carry, xs):
            loss_sum, grad_sum = carry
            l_k, (gw_k, gx_k) = one_microbatch(xs[0], xs[1])
            with _jax.named_scope("update"):
                return (loss_sum + l_k, _jax.tree.map(_jnp.add, grad_sum, gw_k)), gx_k

        init = (_jnp.zeros((), _jnp.float32), _jax.tree.map(_jnp.zeros_like, weights))
        (loss, grad_w), grad_x = _jax.lax.scan(body, init, (per_example, given["loss_target"]))
    with _jax.named_scope("update"):
        delta_w, new_m, new_v = {}, {}, {}
        for n in TWIN_WEIGHTS:
            delta_w[n], new_m[n], new_v[n] = _adamw(weights[n], grad_w[n], given["m_" + n], given["v_" + n])
    return (loss, grad_x, *[grad_w[n] for n in TWIN_WEIGHTS], *[delta_w[n] for n in TWIN_WEIGHTS],
            *[new_m[n] for n in TWIN_WEIGHTS], *[new_v[n] for n in TWIN_WEIGHTS])
```

```python
import functools
import math

import jax
import jax.numpy as jnp
from jax import lax
from jax.experimental import pallas as pl
from jax.experimental.pallas import tpu as pltpu

F32 = jnp.float32
BF16 = jnp.bfloat16
HI = lax.Precision.HIGHEST

NDEV = 8
T = 2048
D = 1024
DFF = 4096
HEADS = 8
HD = 64
RW = 512
PA = 2048
PB = 1536
PAB = PA + PB
QKV = 4608
DILS = (1, 4, 16)
BLK = 128
ALPHA = 4.0 ** 0.25
LN_EPS = 1e-5
GN_EPS = 64e-5
ADAM_LR, ADAM_B1, ADAM_B2, ADAM_EPS, ADAM_WD, ADAM_STEP = 0.001, 0.9, 0.999, 1e-8, 0.01, 10
VMEM_LIMIT = 56 * 1024 * 1024


def _cp(sem):
    return pltpu.CompilerParams(dimension_semantics=sem, vmem_limit_bytes=VMEM_LIMIT)


def _slot(px, py, pc):
    return 4 * px + 2 * py + pc


def _all_gather(x, name):
    R, C = x.shape

    def body(x_ref, out_ref, send_sems, recv_sems, local_sem):
        xi, yi, ci = lax.axis_index("x"), lax.axis_index("y"), lax.axis_index("c")
        me, sibling = (xi, yi, ci), (xi, yi, 1 - ci)
        chips = [(1 - xi, yi), (xi, 1 - yi), (1 - xi, 1 - yi)]

        def rows(px, py, pc):
            return out_ref.at[_slot(px, py, pc)]

        def copy(k, block, to, src=None):
            return pltpu.make_async_remote_copy(
                src_ref=rows(*block) if src is None else src, dst_ref=rows(*block),
                send_sem=send_sems.at[k], recv_sem=recv_sems.at[k],
                device_id=to, device_id_type=pl.DeviceIdType.MESH)

        mine = pltpu.make_async_copy(x_ref, rows(*me), local_sem)
        mine.start()
        first = [copy(0, me, sibling, src=x_ref)]
        first += [copy(1 + j, me, (*chip, ci), src=x_ref) for j, chip in enumerate(chips)]
        for cp in first:
            cp.start()
        passed = [copy(4 + j, (*chip, ci), sibling) for j, chip in enumerate(chips)]
        for j, chip in enumerate(chips):
            copy(1 + j, (*chip, ci), me).wait_recv()
            passed[j].start()
        copy(0, sibling, me).wait_recv()
        for j, chip in enumerate(chips):
            copy(4 + j, (*chip, 1 - ci), me).wait_recv()
        for cp in first + passed:
            cp.wait_send()
        mine.wait()

    return pl.pallas_call(
        body, name=name,
        out_shape=jax.ShapeDtypeStruct((NDEV, R, C), x.dtype),
        in_specs=[pl.BlockSpec(memory_space=pl.ANY)],
        out_specs=pl.BlockSpec(memory_space=pl.ANY),
        scratch_shapes=[pltpu.SemaphoreType.DMA((7,)), pltpu.SemaphoreType.DMA((7,)),
                        pltpu.SemaphoreType.DMA(())],
    )(x)


def _all_to_all(g, name):
    _, R, C = g.shape

    def body(g_ref, out_ref, send_sems, recv_sems, local_sem):
        xi, yi, ci = lax.axis_index("x"), lax.axis_index("y"), lax.axis_index("c")
        my_slot = _slot(xi, yi, ci)
        mine = pltpu.make_async_copy(g_ref.at[my_slot], out_ref.at[my_slot], local_sem)
        mine.start()
        copies = []
        for k in range(1, 8):
            px = 1 - xi if k & 4 else xi
            py = 1 - yi if k & 2 else yi
            pc = 1 - ci if k & 1 else ci
            peer_slot = _slot(px, py, pc)
            copies.append((
                pltpu.make_async_remote_copy(
                    src_ref=g_ref.at[peer_slot], dst_ref=out_ref.at[my_slot],
                    send_sem=send_sems.at[k - 1], recv_sem=recv_sems.at[k - 1],
                    device_id=(px, py, pc), device_id_type=pl.DeviceIdType.MESH),
                pltpu.make_async_remote_copy(
                    src_ref=g_ref.at[peer_slot], dst_ref=out_ref.at[peer_slot],
                    send_sem=send_sems.at[k - 1], recv_sem=recv_sems.at[k - 1],
                    device_id=(px, py, pc), device_id_type=pl.DeviceIdType.MESH)))
        for send, _ in copies:
            send.start()
        for _, recv in copies:
            recv.wait_recv()
        for send, _ in copies:
            send.wait_send()
        mine.wait()

    return pl.pallas_call(
        body, name=name,
        out_shape=jax.ShapeDtypeStruct((NDEV, R, C), g.dtype),
        in_specs=[pl.BlockSpec(memory_space=pl.ANY)],
        out_specs=pl.BlockSpec(memory_space=pl.ANY),
        scratch_shapes=[pltpu.SemaphoreType.DMA((7,)), pltpu.SemaphoreType.DMA((7,)),
                        pltpu.SemaphoreType.DMA(())],
    )(g)


def _mm(name, a, b, tb=False, out=(F32,), epi=None, extras=(), tm=512, tn=512, tk_cap=1792):
    M, K = a.shape
    N = b.shape[0] if tb else b.shape[1]
    tm, tn = min(tm, M), min(tn, N)
    tk = max(t for t in range(128, min(K, tk_cap) + 1, 128) if K % t == 0)
    assert M % tm == 0 and N % tn == 0 and K % tk == 0, (name, M, N, K)
    nk = K // tk
    ne, no = len(extras), len(out)
    dims = (((1,), (1 if tb else 0,)), ((), ()))

    def kern(*refs):
        a_ref, b_ref = refs[:2]
        e_refs = refs[2:2 + ne]
        o_refs = refs[2 + ne:2 + ne + no]
        acc_ref = refs[-1]
        k = pl.program_id(2)

        @pl.when(k == 0)
        def _():
            acc_ref[...] = jnp.zeros_like(acc_ref)

        acc_ref[...] += lax.dot_general(a_ref[...], b_ref[...], dims, preferred_element_type=F32)

        @pl.when(k == nk - 1)
        def _():
            acc = acc_ref[...]
            outs = epi(acc, *[e[...] for e in e_refs]) if epi is not None else (acc,)
            for o_ref, o in zip(o_refs, outs):
                o_ref[...] = o.astype(o_ref.dtype)

    b_spec = (pl.BlockSpec((tn, tk), lambda i, j, k: (j, k)) if tb
              else pl.BlockSpec((tk, tn), lambda i, j, k: (k, j)))
    tile = pl.BlockSpec((tm, tn), lambda i, j, k: (i, j))
    res = pl.pallas_call(
        kern, name=name, grid=(M // tm, N // tn, nk),
        in_specs=[pl.BlockSpec((tm, tk), lambda i, j, k: (i, k)), b_spec] + [tile] * ne,
        out_specs=[tile] * no,
        out_shape=[jax.ShapeDtypeStruct((M, N), dt) for dt in out],
        scratch_shapes=[pltpu.VMEM((tm, tn), F32)],
        compiler_params=_cp(("parallel", "parallel", "arbitrary")),
    )(a, b, *extras)
    return res[0] if no == 1 else res


def _rows(name, body, rows, params, out_rows, out_accs=(), tm=256):
    views = [r if isinstance(r, tuple) else (r, r.shape[1], 0) for r in rows]
    n = views[0][0].shape[0]
    assert n % tm == 0
    nr, npar, nor, noa = len(views), len(params), len(out_rows), len(out_accs)

    def kern(*refs):
        r_refs = refs[:nr]
        p_refs = refs[nr:nr + npar]
        o_refs = refs[nr + npar:nr + npar + nor]
        a_refs = refs[nr + npar + nor:]
        outs, accs = body([r[...] for r in r_refs], [p[...] for p in p_refs])
        assert len(outs) == nor and len(accs) == noa, (name, len(outs), len(accs))
        for o_ref, o in zip(o_refs, outs):
            o_ref[...] = o.astype(o_ref.dtype)
        if noa:
            @pl.when(pl.program_id(0) == 0)
            def _():
                for a_ref in a_refs:
                    a_ref[...] = jnp.zeros_like(a_ref)

            for a_ref, a in zip(a_refs, accs):
                a_ref[...] += a.astype(F32)

    def whole(shape):
        nd = len(shape)
        return pl.BlockSpec(tuple(shape), lambda i, nd=nd: (0,) * nd)

    in_specs = [pl.BlockSpec((tm, w), lambda i, cb=cb: (i, cb)) for _, w, cb in views]
    in_specs += [whole(p.shape) for p in params]
    out_specs = [pl.BlockSpec((tm, c), lambda i: (i, 0)) for c, _ in out_rows]
    out_specs += [whole(s) for s in out_accs]
    out_shape = [jax.ShapeDtypeStruct((n, c), dt) for c, dt in out_rows]
    out_shape += [jax.ShapeDtypeStruct(tuple(s), F32) for s in out_accs]
    res = pl.pallas_call(
        kern, name=name, grid=(n // tm,), in_specs=in_specs, out_specs=out_specs,
        out_shape=out_shape, compiler_params=_cp(("arbitrary",)),
    )(*[v[0] for v in views], *params)
    return res[:nor], res[nor:]


def _softplus(z):
    return jnp.maximum(z, 0.0) + jnp.log(1.0 + jnp.exp(jnp.minimum(z, -z)))


def _post_ln(x, y, g, lng, lnb):
    z = ALPHA * x + (1.0 + g) * y
    mu = jnp.mean(z, axis=-1, keepdims=True)
    zc = z - mu
    var = jnp.mean(zc * zc, axis=-1, keepdims=True)
    return zc * lax.rsqrt(var + LN_EPS) * lng + lnb


def _post_ln_mod(x, y, g, lng, lnb, scn, shn):
    xn = _post_ln(x, y, g, lng, lnb)
    return xn, xn * (1.0 + scn) + shn


def _pre_core(E, r_, k_, v_, wd_, ad_, gd_, r1, k1, v1, wd1, ad1, gd1, h, bg, cg, h1, cg1, h2, cg2,
              mu_r, mu_k, mu_v, mu_wd, mu_ad, mu_gd, w0, w_up, a0, a_up, g_up, k_k, k_a,
              cw0, cw1, cw2):
    def mix(x, x1, mu):
        return x + mu * (x1 - x)

    r, k, v = mix(r_, r1, mu_r), mix(k_, k1, mu_k), mix(v_, v1, mu_v)
    wd, ad, gd = mix(wd_, wd1, mu_wd), mix(ad_, ad1, mu_ad), mix(gd_, gd1, mu_gd)
    logw = -_softplus(-(w0 + jnp.dot(jnp.tanh(wd), w_up, preferred_element_type=F32))) - 0.5
    decay = jnp.exp(-jnp.exp(logw))
    iclr = jax.nn.sigmoid(a0 + jnp.dot(ad, a_up, preferred_element_type=F32))
    gate = jnp.dot(jax.nn.sigmoid(gd), g_up, preferred_element_type=F32)
    kk0 = k * k_k
    nrm = jnp.sqrt(jnp.dot(kk0 * kk0, E, precision=HI, preferred_element_type=F32))
    kk = kk0 / jnp.maximum(nrm, 1e-12)
    kh = k * (1.0 + (iclr - 1.0) * k_a)
    yb = bg * (cw2 * (cg * h) + cw1 * (cg1 * h1) + cw0 * (cg2 * h2))
    return r, decay, kh, v, -kk, kk * iclr, gate, yb


def _post_core(E, y, r, kh, v, gate, lnx_g, lnx_b, rk):
    def seg(t):
        return jnp.dot(t, E, precision=HI, preferred_element_type=F32)

    mean = seg(y) * (1.0 / HD)
    yc = y - mean
    var = seg(yc * yc) * (1.0 / HD)
    gn = yc * lax.rsqrt(var + GN_EPS) * lnx_g + lnx_b
    bonus = seg(r * kh * rk) * v
    return (gn + bonus) * gate


def _merge_core(o0, o1, o2, l0, l1, l2):
    m = jnp.maximum(jnp.maximum(l0, l1), l2)
    e0, e1, e2 = jnp.exp(l0 - m), jnp.exp(l1 - m), jnp.exp(l2 - m)
    den = e0 + e1 + e2
    return (e0 * o0 + e1 * o1 + e2 * o2) / den


CHUNK = 128
HALF = 64


def _scan_fwd(r, w, k, a, b, vT):
    def kern(r_ref, w_ref, k_ref, a_ref, b_ref, v_ref, y_ref, ck_ref, s_ref):
        @pl.when(pl.program_id(0) == 0)
        def _():
            s_ref[...] = jnp.zeros_like(s_ref)

        y_ref[...] = jnp.zeros_like(y_ref)
        lane = lax.broadcasted_iota(jnp.int32, (1, CHUNK), 1)

        def step(t, carry):
            oh = (lane == t).astype(F32)
            for h in range(HEADS):
                S = s_ref[h]
                vcol = jnp.sum(v_ref[h] * oh, axis=1, keepdims=True)
                sa = jnp.sum(S * a_ref[h, pl.ds(t, 1), :], axis=1, keepdims=True)
                S = S * w_ref[h, pl.ds(t, 1), :] + sa * b_ref[h, pl.ds(t, 1), :] \
                    + vcol * k_ref[h, pl.ds(t, 1), :]
                ycol = jnp.sum(S * r_ref[h, pl.ds(t, 1), :], axis=1, keepdims=True)
                s_ref[h] = S
                y_ref[h] = y_ref[h] + ycol * oh
            return carry

        for half in range(CHUNK // HALF):
            ck_ref[half] = s_ref[...]
            lax.fori_loop(half * HALF, (half + 1) * HALF, step, 0)

    rowblk = pl.BlockSpec((HEADS, CHUNK, HD), lambda c: (0, c, 0))
    colblk = pl.BlockSpec((HEADS, HD, CHUNK), lambda c: (0, 0, c))
    return pl.pallas_call(
        kern, name="rwkv_scan_fwd", grid=(T // CHUNK,),
        in_specs=[rowblk] * 5 + [colblk],
        out_specs=[colblk, pl.BlockSpec((CHUNK // HALF, HEADS, HD, HD), lambda c: (c, 0, 0, 0))],
        out_shape=[jax.ShapeDtypeStruct((HEADS, HD, T), F32),
                   jax.ShapeDtypeStruct((T // HALF, HEADS, HD, HD), F32)],
        scratch_shapes=[pltpu.VMEM((HEADS, HD, HD), F32)],
        compiler_params=_cp(("arbitrary",)),
    )(r, w, k, a, b, vT)


def _scan_bwd(r, w, k, a, b, vT, dyT, ck):
    NC = T // CHUNK

    def kern(r_ref, w_ref, k_ref, a_ref, b_ref, v_ref, dy_ref, ck_ref,
             dr_ref, dw_ref, dk_ref, da_ref, db_ref, dv_ref, ds_ref, sb_ref):
        @pl.when(pl.program_id(0) == 0)
        def _():
            ds_ref[...] = jnp.zeros_like(ds_ref)

        dv_ref[...] = jnp.zeros_like(dv_ref)
        lane = lax.broadcasted_iota(jnp.int32, (1, CHUNK), 1)

        for half in reversed(range(CHUNK // HALF)):
            base = half * HALF
            sb_ref[0] = ck_ref[half]

            def replay(i, carry):
                t = base + i
                oh = (lane == t).astype(F32)
                for h in range(HEADS):
                    S = sb_ref[i, h]
                    vcol = jnp.sum(v_ref[h] * oh, axis=1, keepdims=True)
                    sa = jnp.sum(S * a_ref[h, pl.ds(t, 1), :], axis=1, keepdims=True)
                    sb_ref[i + 1, h] = S * w_ref[h, pl.ds(t, 1), :] + sa * b_ref[h, pl.ds(t, 1), :] \
                        + vcol * k_ref[h, pl.ds(t, 1), :]
                return carry

            lax.fori_loop(0, HALF, replay, 0)

            def back(ii, carry):
                i = HALF - 1 - ii
                t = base + i
                oh = (lane == t).astype(F32)
                for h in range(HEADS):
                    Sp, St = sb_ref[i, h], sb_ref[i + 1, h]
                    a_r, b_r = a_ref[h, pl.ds(t, 1), :], b_ref[h, pl.ds(t, 1), :]
                    vcol = jnp.sum(v_ref[h] * oh, axis=1, keepdims=True)
                    dycol = jnp.sum(dy_ref[h] * oh, axis=1, keepdims=True)
                    sa = jnp.sum(Sp * a_r, axis=1, keepdims=True)
                    dS = ds_ref[h] + dycol * r_ref[h, pl.ds(t, 1), :]
                    dr_ref[h, pl.ds(t, 1), :] = jnp.sum(St * dycol, axis=0, keepdims=True)
                    dw_ref[h, pl.ds(t, 1), :] = jnp.sum(dS * Sp, axis=0, keepdims=True)
                    db_ref[h, pl.ds(t, 1), :] = jnp.sum(dS * sa, axis=0, keepdims=True)
                    dk_ref[h, pl.ds(t, 1), :] = jnp.sum(dS * vcol, axis=0, keepdims=True)
                    dsa = jnp.sum(dS * b_r, axis=1, keepdims=True)
                    dvcol = jnp.sum(dS * k_ref[h, pl.ds(t, 1), :], axis=1, keepdims=True)
                    da_ref[h, pl.ds(t, 1), :] = jnp.sum(Sp * dsa, axis=0, keepdims=True)
                    ds_ref[h] = dS * w_ref[h, pl.ds(t, 1), :] + dsa * a_r
                    dv_ref[h] = dv_ref[h] + dvcol * oh
                return carry

            lax.fori_loop(0, HALF, back, 0)

    rowblk = pl.BlockSpec((HEADS, CHUNK, HD), lambda c: (0, NC - 1 - c, 0))
    colblk = pl.BlockSpec((HEADS, HD, CHUNK), lambda c: (0, 0, NC - 1 - c))
    rowshape = jax.ShapeDtypeStruct((HEADS, T, HD), F32)
    return pl.pallas_call(
        kern, name="rwkv_scan_bwd", grid=(NC,),
        in_specs=[rowblk] * 5 + [colblk, colblk,
                                 pl.BlockSpec((CHUNK // HALF, HEADS, HD, HD), lambda c: (NC - 1 - c, 0, 0, 0))],
        out_specs=[rowblk] * 5 + [colblk],
        out_shape=[rowshape] * 5 + [jax.ShapeDtypeStruct((HEADS, HD, T), F32)],
        scratch_shapes=[pltpu.VMEM((HEADS, HD, HD), F32), pltpu.VMEM((HALF + 1, HEADS, HD, HD), F32)],
        compiler_params=_cp(("arbitrary",)),
    )(r, w, k, a, b, vT, dyT, ck)


NBLK = T // BLK


def _blocks_per_segment(g):
    return jnp.where(g == 0, NBLK // DILS[0], jnp.where(g == 1, NBLK // DILS[1], NBLK // DILS[2]))


def _attn_fwd(q, kp, vp, bias):
    def kern(q_ref, k_ref, v_ref, b_ref, o_ref, l_ref):
        nbs = _blocks_per_segment(pl.program_id(0))
        qi = lax.broadcasted_iota(jnp.int32, (BLK, 2 * BLK), 0)
        ki = lax.broadcasted_iota(jnp.int32, (BLK, 2 * BLK), 1)
        band = (ki >= qi) & (ki <= qi + BLK)
        bias_t = b_ref[0, 0]
        for n in range(NBLK):
            lo = jnp.where((n & (nbs - 1)) == 0, BLK, 0)
            valid = band & (ki >= lo)
            qb = q_ref[0, 0, n * BLK:(n + 1) * BLK, :]
            kc = k_ref[0, 0, n * BLK:(n + 2) * BLK, :]
            vc = v_ref[0, 0, n * BLK:(n + 2) * BLK, :]
            s = lax.dot_general(qb, kc, (((1,), (1,)), ((), ())), preferred_element_type=F32)
            s = jnp.where(valid, s * (HD ** -0.5) + bias_t, -jnp.inf)
            m = jnp.max(s, axis=1, keepdims=True)
            e = jnp.exp(s - m)
            den = jnp.sum(e, axis=1, keepdims=True)
            pr = (e / den).astype(BF16)
            o_ref[0, 0, n * BLK:(n + 1) * BLK, :] = jnp.dot(pr, vc, preferred_element_type=F32)
            l_ref[0, 0, n * BLK:(n + 1) * BLK, :] = m + jnp.log(den)

    def blk(rows, cols):
        return pl.BlockSpec((1, 1, rows, cols), lambda g, h: (g, h, 0, 0))

    return pl.pallas_call(
        kern, name="attn_fwd", grid=(3, HEADS),
        in_specs=[blk(T, HD), blk(T + BLK, HD), blk(T + BLK, HD), blk(BLK, 2 * BLK)],
        out_specs=[blk(T, HD), blk(T, 1)],
        out_shape=[jax.ShapeDtypeStruct((3, HEADS, T, HD), F32),
                   jax.ShapeDtypeStruct((3, HEADS, T, 1), F32)],
        compiler_params=_cp(("parallel", "parallel")),
    )(q, kp, vp, bias)


def _attn_bwd(q, kp, vp, bias, biasT, do, lse_c, lse_r, dc_c, dc_r):
    def kern(q_ref, k_ref, v_ref, b_ref, bt_ref, do_ref, lc_ref, lr_ref, dcc_ref, dcr_ref,
             dq_ref, dk_ref, dv_ref, db_ref):
        nbs = _blocks_per_segment(pl.program_id(0))
        qi = lax.broadcasted_iota(jnp.int32, (BLK, 2 * BLK), 0)
        ki = lax.broadcasted_iota(jnp.int32, (BLK, 2 * BLK), 1)
        band = (ki >= qi) & (ki <= qi + BLK)
        kiT = lax.broadcasted_iota(jnp.int32, (2 * BLK, BLK), 0)
        qiT = lax.broadcasted_iota(jnp.int32, (2 * BLK, BLK), 1)
        bandT = (kiT >= qiT) & (kiT <= qiT + BLK)
        bias_t, biasT_t = b_ref[0, 0], bt_ref[0, 0]
        scale = HD ** -0.5
        dk_ref[...] = jnp.zeros_like(dk_ref)
        dv_ref[...] = jnp.zeros_like(dv_ref)
        db_ref[...] = jnp.zeros_like(db_ref)
        nt = (((1,), (1,)), ((), ()))
        for n in range(NBLK):
            lo = jnp.where((n & (nbs - 1)) == 0, BLK, 0)
            qs, ks = slice(n * BLK, (n + 1) * BLK), slice(n * BLK, (n + 2) * BLK)
            qb, kc, vc, dob = q_ref[0, 0, qs, :], k_ref[0, 0, ks, :], v_ref[0, 0, ks, :], do_ref[0, 0, qs, :]
            s = lax.dot_general(qb, kc, nt, preferred_element_type=F32) * scale + bias_t
            p = jnp.where(band & (ki >= lo), jnp.exp(s - lc_ref[0, 0, qs, :]), 0.0)
            dp = lax.dot_general(dob, vc, nt, preferred_element_type=F32)
            ds = p * (dp + dcc_ref[0, 0, qs, :])
            db_ref[0, 0] += ds
            dq_ref[0, 0, qs, :] = jnp.dot((ds * scale).astype(BF16), kc, preferred_element_type=F32)
            sT = lax.dot_general(kc, qb, nt, preferred_element_type=F32) * scale + biasT_t
            pT = jnp.where(bandT & (kiT >= lo), jnp.exp(sT - lr_ref[0, 0, :, qs]), 0.0)
            dpT = lax.dot_general(vc, dob, nt, preferred_element_type=F32)
            dsT = pT * (dpT + dcr_ref[0, 0, :, qs])
            dk_ref[0, 0, ks, :] += jnp.dot((dsT * scale).astype(BF16), qb, preferred_element_type=F32)
            dv_ref[0, 0, ks, :] += jnp.dot(pT.astype(BF16), dob, preferred_element_type=F32)

    def blk(rows, cols):
        return pl.BlockSpec((1, 1, rows, cols), lambda g, h: (g, h, 0, 0))

    return pl.pallas_call(
        kern, name="attn_bwd", grid=(3, HEADS),
        in_specs=[blk(T, HD), blk(T + BLK, HD), blk(T + BLK, HD), blk(BLK, 2 * BLK), blk(2 * BLK, BLK),
                  blk(T, HD), blk(T, 1), blk(1, T), blk(T, 1), blk(1, T)],
        out_specs=[blk(T, HD), blk(T + BLK, HD), blk(T + BLK, HD), blk(BLK, 2 * BLK)],
        out_shape=[jax.ShapeDtypeStruct((3, HEADS, T, HD), F32),
                   jax.ShapeDtypeStruct((3, HEADS, T + BLK, HD), F32),
                   jax.ShapeDtypeStruct((3, HEADS, T + BLK, HD), F32),
                   jax.ShapeDtypeStruct((3, HEADS, BLK, 2 * BLK), F32)],
        compiler_params=_cp(("parallel", "parallel")),
    )(q, kp, vp, bias, biasT, do, lse_c, lse_r, dc_c, dc_r)


def _relbias_grad(db, onehot):
    def kern(db_ref, oh_ref, out_ref):
        x = db_ref[0]
        hi = x.astype(BF16)
        lo = (x - hi.astype(F32)).astype(BF16)
        out_ref[0] = (jnp.dot(hi, oh_ref[0], preferred_element_type=F32)
                      + jnp.dot(lo, oh_ref[0], preferred_element_type=F32))

    return pl.pallas_call(
        kern, name="relbias_grad", grid=(3,),
        in_specs=[pl.BlockSpec((1, HEADS, BLK * 2 * BLK), lambda g: (g, 0, 0)),
                  pl.BlockSpec((1, BLK * 2 * BLK, 32), lambda g: (g, 0, 0))],
        out_specs=pl.BlockSpec((1, HEADS, 32), lambda g: (g, 0, 0)),
        out_shape=jax.ShapeDtypeStruct((3, HEADS, 32), F32),
        compiler_params=_cp(("parallel",)),
    )(db, onehot)


def _adamw(w, g, m, v):
    m2 = ADAM_B1 * m + (1.0 - ADAM_B1) * g
    v2 = ADAM_B2 * v + (1.0 - ADAM_B2) * (g * g)
    m_hat = m2 / (1.0 - ADAM_B1 ** ADAM_STEP)
    v_hat = v2 / (1.0 - ADAM_B2 ** ADAM_STEP)
    return -ADAM_LR * (m_hat / (jnp.sqrt(v_hat) + ADAM_EPS) + ADAM_WD * w), m2, v2


def _ada_mod(c_all, ada_w, ada_b_loc):
    def kern(c_ref, w_ref, b_ref, o_ref):
        c = c_ref[...]
        cond = c * jax.nn.sigmoid(c)
        o_ref[0] = jnp.dot(cond, w_ref[0], precision=HI, preferred_element_type=F32) + b_ref[0]

    ncol = ada_w.shape[2]
    return pl.pallas_call(
        kern, name="ada_mod", grid=(2,),
        in_specs=[pl.BlockSpec((NDEV, D), lambda i: (0, 0)),
                  pl.BlockSpec((1, D, ncol), lambda i: (i, 0, 0)),
                  pl.BlockSpec((1, 1, ncol), lambda i: (i, 0, 0))],
        out_specs=pl.BlockSpec((1, NDEV, ncol), lambda i: (i, 0, 0)),
        out_shape=jax.ShapeDtypeStruct((2, NDEV, ncol), F32),
        compiler_params=_cp(("parallel",)),
    )(c_all, ada_w, ada_b_loc.reshape(2, 1, ncol))


def _ada_grad_adamw(cT_all, dmod_loc, w, m, v):
    ncol = w.shape[2]
    tr = 256

    def kern(c_ref, d_ref, w_ref, m_ref, v_ref, g_ref, dl_ref, m2_ref, v2_ref):
        c = c_ref[...]
        cond = c * jax.nn.sigmoid(c)
        g = jnp.dot(cond, d_ref[0], precision=HI, preferred_element_type=F32)
        dl, m2, v2 = _adamw(w_ref[0], g, m_ref[0], v_ref[0])
        g_ref[0], dl_ref[0], m2_ref[0], v2_ref[0] = g, dl, m2, v2

    big = pl.BlockSpec((1, tr, ncol), lambda i, j: (i, j, 0))
    shp = jax.ShapeDtypeStruct(w.shape, F32)
    return pl.pallas_call(
        kern, name="ada_grad_adamw", grid=(2, D // tr),
        in_specs=[pl.BlockSpec((tr, NDEV), lambda i, j: (j, 0)),
                  pl.BlockSpec((1, NDEV, ncol), lambda i, j: (i, 0, 0)), big, big, big],
        out_specs=[big] * 4, out_shape=[shp] * 4,
        compiler_params=_cp(("parallel", "parallel")),
    )(cT_all, dmod_loc, w, m, v)


def _sum_adamw(recv, w, m, v, name, tr):
    R = w.shape[0]
    assert R % tr == 0

    def kern(r_ref, w_ref, m_ref, v_ref, g_ref, dl_ref, m2_ref, v2_ref):
        g = r_ref[0]
        for s in range(1, NDEV):
            g = g + r_ref[s]
        dl, m2, v2 = _adamw(w_ref[...], g, m_ref[...], v_ref[...])
        g_ref[...], dl_ref[...], m2_ref[...], v2_ref[...] = g, dl, m2, v2

    flat = pl.BlockSpec((tr, 128), lambda i: (i, 0))
    shp = jax.ShapeDtypeStruct((R, 128), F32)
    return pl.pallas_call(
        kern, name=name, grid=(R // tr,),
        in_specs=[pl.BlockSpec((NDEV, tr, 128), lambda i: (0, i, 0)), flat, flat, flat],
        out_specs=[flat] * 4, out_shape=[shp] * 4,
        compiler_params=_cp(("parallel",)),
    )(recv, w, m, v)


def _pack(arrs, dtype, row_mult):
    flat = jnp.concatenate([a.reshape(-1).astype(dtype) for a in arrs])
    flat = jnp.pad(flat, (0, -flat.shape[0] % (128 * row_mult)))
    return flat.reshape(-1, 128)


def _pack8(arrs, dtype, row_mult):
    flat = jnp.concatenate([a.reshape(NDEV, -1).astype(dtype) for a in arrs], axis=1)
    flat = jnp.pad(flat, ((0, 0), (0, -flat.shape[1] % (128 * row_mult))))
    return flat.reshape(NDEV, -1, 128)


def _unpack(buf, shapes, lead=()):
    flat = buf.reshape(lead + (-1,))
    out, off = [], 0
    for s in shapes:
        n = math.prod(s)
        out.append(flat[..., off:off + n].reshape(lead + tuple(s)))
        off += n
    return out


def _to_chunks(full, kind):
    if kind == "col":
        x = full.reshape(full.shape[:-1] + (NDEV, full.shape[-1] // NDEV))
        return jnp.moveaxis(x, -2, 0)
    x = full.reshape(full.shape[:-2] + (NDEV, full.shape[-2] // NDEV, full.shape[-1]))
    return jnp.moveaxis(x, -3, 0)


def _from_chunks(g8, kind):
    if kind == "col":
        x = jnp.moveaxis(g8, 0, -2)
        return x.reshape(x.shape[:-2] + (x.shape[-2] * x.shape[-1],))
    x = jnp.moveaxis(g8, 0, -3)
    return x.reshape(x.shape[:-3] + (x.shape[-3] * x.shape[-2], x.shape[-1]))


def _pad_pa(x):
    z = lambda n: jnp.zeros(x.shape[:-1] + (n,), x.dtype)
    return jnp.concatenate([x[..., :1600], z(64), x[..., 1600:1664], z(64), x[..., 1664:1824], z(96)], -1)


def _unpad_pa(x):
    return jnp.concatenate([x[..., :1600], x[..., 1664:1728], x[..., 1792:1952]], -1)


def _pad_rows(x, n):
    return jnp.pad(x, ((0, n - x.shape[0]), (0, 0)))


def _shift_down(x, n):
    return jnp.pad(x, ((n, 0), (0, 0)))[:-n]


def _shift_up(x, n):
    return jnp.pad(x, ((0, n), (0, 0)))[n:]


def _heads(x):
    return x.reshape(T, HEADS, HD).transpose(1, 0, 2)


def _headsT(x):
    return x.reshape(T, HEADS, HD).transpose(1, 2, 0)


def _unheads(x):
    return x.transpose(1, 0, 2).reshape(T, RW)


def _unheadsT(x):
    return x.transpose(2, 0, 1).reshape(T, RW)


def _perm(x, dil):
    C = x.shape[-1]
    return x.reshape(T // dil, dil, HEADS, C).transpose(2, 1, 0, 3).reshape(HEADS, T, C)


def _unperm(y, dil):
    C = y.shape[-1]
    return y.reshape(HEADS, dil, T // dil, C).transpose(2, 1, 0, 3).reshape(T, HEADS, C)


def _bucket_tables():
    qi = jnp.arange(BLK)[:, None]
    ki = jnp.arange(2 * BLK)[None, :]
    rel = BLK + qi - ki
    tabs = []
    for dil in DILS:
        dist = jnp.clip(rel, 0, BLK) * dil
        logd = jnp.log(jnp.maximum(dist, 1).astype(F32) / 16) / math.log(2048 / 16)
        large = jnp.minimum(16 + (logd * 16).astype(jnp.int32), 31)
        tabs.append(jnp.where(dist < 16, dist, large))
    return jnp.stack(tabs)


SHARDED = (("ln_g", "col"), ("ln_b", "col"), ("ab_w_in", "col"), ("rw_w_up", "col"), ("rw_a_up", "col"),
           ("rw_g_up", "col"), ("sc_conv_w", "col"), ("ab_w_out", "row"), ("dil_w_qkv", "col"),
           ("dil_w_out", "col"), ("mlp_w1", "col"), ("mlp_w2", "row"))
GATHER_BF16 = ("ab_w_in", "ab_w_out", "dil_w_qkv", "dil_w_out", "mlp_w1", "mlp_w2")
GATHER_F32 = ("rw_w_up", "rw_a_up", "rw_g_up", "sc_conv_w", "ln_g", "ln_b")
REPLICATED = ("ada_b", "rw_mu", "rw_w0", "rw_a0", "rw_k_k", "rw_k_a", "rw_r_k", "rw_lnx_g", "rw_lnx_b", "rel_bias")
WEIGHTS = ("ada_w", "ada_b", "ln_g", "ln_b", "ab_w_in", "rw_mu", "rw_w0", "rw_w_up", "rw_a0", "rw_a_up",
           "rw_g_up", "rw_k_k", "rw_k_a", "rw_r_k", "rw_lnx_g", "rw_lnx_b", "sc_conv_w", "ab_w_out",
           "dil_w_qkv", "dil_w_out", "rel_bias", "mlp_w1", "mlp_w2")
FLAT_TILE = 512


def _local_step(x0, tgt, mod, W, P):
    row = lambda a: a.reshape(1, -1)
    m6 = mod.reshape(2, 6, 1, D)
    sc = [m6[0, 1], m6[0, 4], m6[1, 1], m6[1, 4]]
    sh = [m6[0, 0], m6[0, 3], m6[1, 0], m6[1, 3]]
    gt = [m6[0, 2], m6[0, 5], m6[1, 2], m6[1, 5]]
    lng = [row(P["ln_g"][0, 0]), row(P["ln_g"][0, 1]), row(P["ln_g"][1, 0]), row(P["ln_g"][1, 1])]
    lnb = [row(P["ln_b"][0, 0]), row(P["ln_b"][0, 1]), row(P["ln_b"][1, 0]), row(P["ln_b"][1, 1])]
    E = jnp.kron(jnp.eye(HEADS, dtype=F32), jnp.ones((HD, HD), F32))

    def mod_body(r, p):
        return [r[0] * (1.0 + p[0]) + p[1]], []

    (u0,), _ = _rows("modulate", mod_body, [x0], [sc[0], sh[0]], [(D, BF16)])

    def post_fwd_body(r, p):
        xn, un = _post_ln_mod(r[0], r[1], *p)
        return [xn, un], []

    def post_fwd(s, x, y):
        (xn, un), _ = _rows(f"post_ln_{s}", post_fwd_body, [x, y],
                            [gt[s], lng[s], lnb[s], sc[s + 1], sh[s + 1]], [(D, F32), (D, BF16)])
        return xn, un

    def relu2(acc):
        a = jnp.maximum(acc, 0.0)
        return acc, a * a

    def relu2_bwd(acc, h):
        return (acc * (2.0 * jnp.maximum(h, 0.0)),)

    p = _mm("ab_in", u0, W["ab_w_in"])
    p1 = _shift_down(p, 1)
    p2 = _shift_down(p[:, PA:], 2)
    mu = _pad_pa(P["rw_mu"])
    mu_parts = [mu[:, :512], mu[:, 512:1024], mu[:, 1024:1536], mu[:, 1536:1664], mu[:, 1664:1792], mu[:, 1792:]]
    pre_params = mu_parts + [P["rw_w0"], _pad_rows(P["rw_w_up"], 128), P["rw_a0"], _pad_rows(P["rw_a_up"], 128),
                             _pad_rows(P["rw_g_up"], 256), P["rw_k_k"], P["rw_k_a"],
                             P["sc_conv_w"][0:1], P["sc_conv_w"][1:2], P["sc_conv_w"][2:3]]
    pre_rows = [(p, 512, 0), (p, 512, 1), (p, 512, 2), (p, 128, 12), (p, 128, 13), (p, 256, 7),
                (p1, 512, 0), (p1, 512, 1), (p1, 512, 2), (p1, 128, 12), (p1, 128, 13), (p1, 256, 7),
                (p, 512, 4), (p, 512, 5), (p, 512, 6), (p1, 512, 4), (p1, 512, 6), (p2, 512, 0), (p2, 512, 2)]
    NPR = len(pre_rows)

    def pre_fwd_body(r, pp):
        return list(_pre_core(pp[0], *r, *pp[1:])), []

    (r_, w_, kh_, v_, a_, b_, gate_, yb), _ = _rows(
        "rwkv_pre", pre_fwd_body, pre_rows, [E] + pre_params,
        [(RW, F32)] * 7 + [(RW, BF16)], tm=128)
    scan_in = [_heads(t) for t in (r_, w_, kh_, a_, b_)] + [_headsT(v_)]
    yT, ck = _scan_fwd(*scan_in)
    ysc = _unheadsT(yT)
    post_params = [P["rw_lnx_g"], P["rw_lnx_b"], P["rw_r_k"].reshape(1, RW)]

    def postmix_fwd_body(r, pp):
        return [_post_core(pp[0], *r, *pp[1:])], []

    (ya,), _ = _rows("rwkv_post", postmix_fwd_body, [ysc, r_, kh_, v_, gate_], [E] + post_params,
                     [(RW, BF16)], tm=128)
    cat = jnp.concatenate([ya, yb], axis=1)
    y0 = _mm("ab_out", cat, W["ab_w_out"])
    x1, u1 = post_fwd(0, x0, y0)

    h1, a1 = _mm("mlp1_up_0", u1, W["mlp_w1"][0], out=(F32, BF16), epi=relu2)
    y1 = _mm("mlp1_down_0", a1, W["mlp_w2"][0])
    x2, u2 = post_fwd(1, x1, y1)

    pq = _mm("qkv", u2, W["dil_w_qkv"], out=(BF16,))
    pq5 = pq.reshape(T, 3, 3, HEADS, HD)
    q = jnp.stack([_perm(pq5[:, g, 0], DILS[g]) for g in range(3)])
    kp = jnp.pad(jnp.stack([_perm(pq5[:, g, 1], DILS[g]) for g in range(3)]), ((0, 0), (0, 0), (BLK, 0), (0, 0)))
    vp = jnp.pad(jnp.stack([_perm(pq5[:, g, 2], DILS[g]) for g in range(3)]), ((0, 0), (0, 0), (BLK, 0), (0, 0)))
    buckets = _bucket_tables()
    bias = jnp.stack([P["rel_bias"][buckets[g]][..., g * HEADS:(g + 1) * HEADS].transpose(2, 0, 1)
                      for g in range(3)])
    og, lse = _attn_fwd(q, kp, vp, bias)
    R = T * HEADS
    o_nat = [_unperm(og[g], DILS[g]).reshape(R, HD) for g in range(3)]
    l_nat = [_unperm(lse[g], DILS[g]).reshape(R, 1) for g in range(3)]

    def merge_fwd_body(r, pp):
        return [_merge_core(*r)], []

    (om,), _ = _rows("attn_merge", merge_fwd_body, o_nat + l_nat, [], [(HD, BF16)], tm=1024)
    om = om.reshape(T, RW)
    y2 = _mm("dil_out", om, W["dil_w_out"])
    x3, u3 = post_fwd(2, x2, y2)

    h3, a3 = _mm("mlp1_up_1", u3, W["mlp_w1"][1], out=(F32, BF16), epi=relu2)
    y3 = _mm("mlp1_down_1", a3, W["mlp_w2"][1])

    def last_body(r, pp):
        x, y, tg = r
        xn, vjp = jax.vjp(_post_ln, x, y, *pp)
        err = xn - tg
        dx, dy, dg, dlg, dlb = vjp(err * (1.0 / D))
        loss = jnp.full((1, 128), (0.5 / D) * jnp.sum(err * err), F32)
        return [dx, dy], [loss, dg, dlg, dlb]

    (dxp, dy3), (loss_acc, dg3, dlng3, dlnb3) = _rows(
        "final_ln_loss", last_body, [x3, y3, tgt], [gt[3], lng[3], lnb[3]],
        [(D, F32), (D, BF16)], [(1, 128), (1, D), (1, D), (1, D)])

    G = {}
    dsc, dsh, dgt = [None] * 4, [None] * 4, [None] * 4
    dlng, dlnb = [None] * 4, [None] * 4
    dgt[3], dlng[3], dlnb[3] = dg3, dlng3, dlnb3

    def mlp_bwd(i, u, h, a, dy):
        dh = _mm(f"mlp_dh_{i}", dy, W["mlp_w2"][i], tb=True, out=(BF16,), epi=relu2_bwd, extras=(h,))
        gw2 = _mm(f"mlp_dw2_{i}", a.T, dy)
        du = _mm(f"mlp_du_{i}", dh, W["mlp_w1"][i], tb=True)
        gw1 = _mm(f"mlp_dw1_{i}", u.T, dh)
        return du, gw1, gw2

    def post_bwd_body(r, pp):
        x, y, dxn, dun = r
        _, vjp = jax.vjp(_post_ln_mod, x, y, *pp)
        dx, dy, dg, dlg, dlb, dscn, dshn = vjp((dxn, dun))
        return [dx, dy], [dg, dlg, dlb, dscn, dshn]

    def post_bwd(s, x, y, dxn, dun):
        (dx, dy), (dgt[s], dlng[s], dlnb[s], dsc[s + 1], dsh[s + 1]) = _rows(
            f"post_ln_bwd_{s}", post_bwd_body, [x, y, dxn, dun],
            [gt[s], lng[s], lnb[s], sc[s + 1], sh[s + 1]], [(D, F32), (D, BF16)], [(1, D)] * 5)
        return dx, dy

    du3, gw1_1, gw2_1 = mlp_bwd(1, u3, h3, a3, dy3)
    dxp, dy2 = post_bwd(2, x2, y2, dxp, du3)

    G["dil_w_out"] = _mm("dil_out_dw", om.T, dy2)[None]
    do = _mm("dil_out_dx", dy2, W["dil_w_out"], tb=True).reshape(R, HD)

    def merge_bwd_body(r, pp):
        o_l, dout = r[:6], r[6]
        _, vjp = jax.vjp(_merge_core, *o_l)
        d = vjp(dout)
        dcs = [d[3 + g] - jnp.sum(d[g] * o_l[g], axis=1, keepdims=True) for g in range(3)]
        return list(d[:3]) + dcs, []

    mb, _ = _rows("attn_merge_bwd", merge_bwd_body, o_nat + l_nat + [do], [],
                  [(HD, BF16)] * 3 + [(1, F32)] * 3, tm=1024)
    dog = jnp.stack([_perm(mb[g].reshape(T, HEADS, HD), DILS[g]) for g in range(3)])
    dcc = jnp.stack([_perm(mb[3 + g].reshape(T, HEADS, 1), DILS[g]) for g in range(3)])
    dq, dkp, dvp, dbias = _attn_bwd(q, kp, vp, bias, jnp.swapaxes(bias, 2, 3), dog, lse,
                                    lse.reshape(3, HEADS, 1, T), dcc, dcc.reshape(3, HEADS, 1, T))
    dpq = jnp.concatenate(
        [_unperm(t[g], DILS[g]).reshape(T, RW) for g in range(3) for t in (dq, dkp[:, :, BLK:], dvp[:, :, BLK:])],
        axis=1).astype(BF16)
    onehot = (buckets.reshape(3, BLK * 2 * BLK, 1) == jnp.arange(32)[None, None, :]).astype(BF16)
    rb = _relbias_grad(dbias.reshape(3, HEADS, BLK * 2 * BLK), onehot)
    G["rel_bias"] = rb.transpose(2, 0, 1).reshape(32, 3 * HEADS)
    G["dil_w_qkv"] = _mm("qkv_dw", u2.T, dpq)[None]
    du2 = _mm("qkv_dx", dpq, W["dil_w_qkv"], tb=True)
    dxp, dy1 = post_bwd(1, x1, y1, dxp, du2)

    du1, gw1_0, gw2_0 = mlp_bwd(0, u1, h1, a1, dy1)
    G["mlp_w1"] = jnp.stack([gw1_0, gw1_1])
    G["mlp_w2"] = jnp.stack([gw2_0, gw2_1])
    dxp, dy0 = post_bwd(0, x0, y0, dxp, du1)

    G["ab_w_out"] = _mm("ab_out_dw", cat.T, dy0)[None]
    dcat = _mm("ab_out_dx", dy0, W["ab_w_out"], tb=True)

    def postmix_bwd_body(r, pp):
        _, vjp = jax.vjp(functools.partial(_post_core, pp[0]), *r[:5], *pp[1:])
        d = vjp(r[5])
        return list(d[:5]), list(d[5:])

    (dysc, dr1, dkh1, dv1, dgate), (G["rw_lnx_g"], G["rw_lnx_b"], drk) = _rows(
        "rwkv_post_bwd", postmix_bwd_body, [ysc, r_, kh_, v_, gate_, (dcat, 512, 0)], [E] + post_params,
        [(RW, F32)] * 5, [(1, RW)] * 3, tm=128)
    G["rw_r_k"] = drk.reshape(1, HEADS, HD)
    dr2, dw2, dk2, da2, db2, dvT = _scan_bwd(*scan_in, _headsT(dysc), ck)
    dr2, dw2, dk2, da2, db2 = [_unheads(t) for t in (dr2, dw2, dk2, da2, db2)]
    dv2 = _unheadsT(dvT)

    def pre_bwd_body(r, pp):
        prim, ct = r[:NPR], r[NPR:]
        _, vjp = jax.vjp(functools.partial(_pre_core, pp[0]), *prim, *pp[1:])
        cts = (ct[0] + ct[1], ct[2], ct[3] + ct[4], ct[5] + ct[6], ct[7], ct[8], ct[9], ct[10])
        d = vjp(cts)
        z = jnp.zeros_like(d[12])
        dp = jnp.concatenate([d[0], d[1], d[2], d[3], d[4], d[5], d[12], d[13], d[14]], axis=1)
        dp1 = jnp.concatenate([d[6], d[7], d[8], d[9], d[10], d[11], d[15], z, d[16]], axis=1)
        dp2 = jnp.concatenate([d[17], z, d[18]], axis=1)
        return [dp, dp1, dp2], list(d[NPR:])

    acc_shapes = [a.shape for a in pre_params]
    (dp, dp1, dp2), pacc = _rows(
        "rwkv_pre_bwd", pre_bwd_body,
        pre_rows + [dr1, dr2, dw2, dkh1, dk2, dv1, dv2, da2, db2, dgate, (dcat, 512, 1)],
        [E] + pre_params, [(PAB, F32), (PAB, F32), (PB, F32)], acc_shapes, tm=128)
    G["rw_mu"] = _unpad_pa(jnp.concatenate(pacc[:6], axis=1))
    G["rw_w0"], G["rw_a0"], G["rw_k_k"], G["rw_k_a"] = pacc[6], pacc[8], pacc[11], pacc[12]
    G["rw_w_up"] = pacc[7][None, :64]
    G["rw_a_up"] = pacc[9][None, :64]
    G["rw_g_up"] = pacc[10][None, :160]
    G["sc_conv_w"] = jnp.concatenate(pacc[13:16], axis=0)[None]

    def add3_body(r, pp):
        return [r[0] + r[1] + r[2]], []

    (dpt,), _ = _rows("shift_merge", add3_body,
                      [dp, _shift_up(dp1, 1), jnp.pad(_shift_up(dp2, 2), ((0, 0), (PA, 0)))], [], [(PAB, BF16)])
    gin = _mm("ab_in_dw", u0.T, dpt)
    G["ab_w_in"] = jnp.concatenate([_unpad_pa(gin[:, :PA]), gin[:, PA:]], axis=1)[None]
    du0 = _mm("ab_in_dx", dpt, W["ab_w_in"], tb=True)

    def mod_bwd_body(r, pp):
        du, dx, x = r
        return [dx + du * (1.0 + pp[0])], [jnp.sum(du * x, axis=0, keepdims=True), jnp.sum(du, axis=0, keepdims=True)]

    (grad_x,), (dsc[0], dsh[0]) = _rows("modulate_bwd", mod_bwd_body, [du0, dxp, x0], [sc[0]], [(D, F32)],
                                        [(1, D), (1, D)])

    G["ln_g"] = jnp.concatenate(dlng, axis=0).reshape(2, 2, D)
    G["ln_b"] = jnp.concatenate(dlnb, axis=0).reshape(2, 2, D)
    dmod = jnp.concatenate([dsh[0], dsc[0], dgt[0], dsh[1], dsc[1], dgt[1],
                            dsh[2], dsc[2], dgt[2], dsh[3], dsc[3], dgt[3]], axis=1).reshape(2, 6 * D)
    return loss_acc[0, 0], grad_x, dmod, G


def kernel(x, c, ada_w, ada_b, ln_g, ln_b, ab_w_in, rw_mu, rw_w0, rw_w_up, rw_a0, rw_a_up, rw_g_up, rw_k_k, rw_k_a, rw_r_k, rw_lnx_g, rw_lnx_b, sc_conv_w, ab_w_out, dil_w_qkv, dil_w_out, rel_bias, mlp_w1, mlp_w2, loss_target, m_ada_w, m_ada_b, m_ln_g, m_ln_b, m_ab_w_in, m_rw_mu, m_rw_w0, m_rw_w_up, m_rw_a0, m_rw_a_up, m_rw_g_up, m_rw_k_k, m_rw_k_a, m_rw_r_k, m_rw_lnx_g, m_rw_lnx_b, m_sc_conv_w, m_ab_w_out, m_dil_w_qkv, m_dil_w_out, m_rel_bias, m_mlp_w1, m_mlp_w2, v_ada_w, v_ada_b, v_ln_g, v_ln_b, v_ab_w_in, v_rw_mu, v_rw_w0, v_rw_w_up, v_rw_a0, v_rw_a_up, v_rw_g_up, v_rw_k_k, v_rw_k_a, v_rw_r_k, v_rw_lnx_g, v_rw_lnx_b, v_sc_conv_w, v_ab_w_out, v_dil_w_qkv, v_dil_w_out, v_rel_bias, v_mlp_w1, v_mlp_w2):
    w = dict(ada_w=ada_w, ada_b=ada_b, ln_g=ln_g, ln_b=ln_b, ab_w_in=ab_w_in, rw_mu=rw_mu, rw_w0=rw_w0,
             rw_w_up=rw_w_up, rw_a0=rw_a0, rw_a_up=rw_a_up, rw_g_up=rw_g_up, rw_k_k=rw_k_k, rw_k_a=rw_k_a,
             rw_r_k=rw_r_k, rw_lnx_g=rw_lnx_g, rw_lnx_b=rw_lnx_b, sc_conv_w=sc_conv_w, ab_w_out=ab_w_out,
             dil_w_qkv=dil_w_qkv, dil_w_out=dil_w_out, rel_bias=rel_bias, mlp_w1=mlp_w1, mlp_w2=mlp_w2)
    m = dict(ada_w=m_ada_w, ada_b=m_ada_b, ln_g=m_ln_g, ln_b=m_ln_b, ab_w_in=m_ab_w_in, rw_mu=m_rw_mu,
             rw_w0=m_rw_w0, rw_w_up=m_rw_w_up, rw_a0=m_rw_a0, rw_a_up=m_rw_a_up, rw_g_up=m_rw_g_up,
             rw_k_k=m_rw_k_k, rw_k_a=m_rw_k_a, rw_r_k=m_rw_r_k, rw_lnx_g=m_rw_lnx_g, rw_lnx_b=m_rw_lnx_b,
             sc_conv_w=m_sc_conv_w, ab_w_out=m_ab_w_out, dil_w_qkv=m_dil_w_qkv, dil_w_out=m_dil_w_out,
             rel_bias=m_rel_bias, mlp_w1=m_mlp_w1, mlp_w2=m_mlp_w2)
    v = dict(ada_w=v_ada_w, ada_b=v_ada_b, ln_g=v_ln_g, ln_b=v_ln_b, ab_w_in=v_ab_w_in, rw_mu=v_rw_mu,
             rw_w0=v_rw_w0, rw_w_up=v_rw_w_up, rw_a0=v_rw_a0, rw_a_up=v_rw_a_up, rw_g_up=v_rw_g_up,
             rw_k_k=v_rw_k_k, rw_k_a=v_rw_k_a, rw_r_k=v_rw_r_k, rw_lnx_g=v_rw_lnx_g, rw_lnx_b=v_rw_lnx_b,
             sc_conv_w=v_sc_conv_w, ab_w_out=v_ab_w_out, dil_w_qkv=v_dil_w_qkv, dil_w_out=v_dil_w_out,
             rel_bias=v_rel_bias, mlp_w1=v_mlp_w1, mlp_w2=v_mlp_w2)
    kinds = dict(SHARDED)
    me = 4 * lax.axis_index("x") + 2 * lax.axis_index("y") + lax.axis_index("c")
    ncol = ada_w.shape[2]

    small = _all_gather(_pack([c] + [w[n] for n in GATHER_F32], F32, 8), "gather_small")
    parts = _unpack(small, [c.shape] + [w[n].shape for n in GATHER_F32], (NDEV,))
    c_all = parts[0].reshape(NDEV, D)
    P = {n: _from_chunks(t, kinds[n]) for n, t in zip(GATHER_F32, parts[1:])}
    P = {n: (t if n in ("ln_g", "ln_b") else t[0]) for n, t in P.items()}
    for n in REPLICATED[1:]:
        P[n] = w[n]
    big = _all_gather(_pack([w[n] for n in GATHER_BF16], BF16, 16), "gather_weights")
    parts = _unpack(big, [w[n].shape for n in GATHER_BF16], (NDEV,))
    W = {n: _from_chunks(t, kinds[n]) for n, t in zip(GATHER_BF16, parts)}
    W = {n: (t if n in ("mlp_w1", "mlp_w2") else t[0]) for n, t in W.items()}
    W["ab_w_in"] = jnp.concatenate([_pad_pa(W["ab_w_in"][:, :1824]), W["ab_w_in"][:, 1824:]], axis=1)

    ada_b_loc = lax.dynamic_slice(ada_b, (0, ncol * me), (2, ncol))
    mod_part = _ada_mod(c_all, ada_w, ada_b_loc)
    mod_all = _all_gather(mod_part.reshape(-1, 128), "gather_mod").reshape(NDEV, 2, NDEV, ncol)
    mod = lax.dynamic_index_in_dim(mod_all, me, axis=2, keepdims=False)
    mod = mod.transpose(1, 0, 2).reshape(2, 6 * D)

    loss_part, grad_x, dmod, G = _local_step(x[0], loss_target[0], mod, W, P)
    G["ada_b"] = dmod
    loss = lax.psum(loss_part, ("x", "y", "c"))

    rep_shapes = [w[n].shape for n in REPLICATED]
    rep_all = _all_gather(_pack([G[n] for n in REPLICATED], F32, 8), "gather_replicated_grads")
    pk = lambda d: _pack([d[n] for n in REPLICATED], F32, 8)
    rep_out = _sum_adamw(rep_all, pk(w), pk(m), pk(v), "sum_adamw_replicated", 8)
    rep_out = [dict(zip(REPLICATED, _unpack(o, rep_shapes))) for o in rep_out]

    dmod_all = _unpack(rep_all, [(2, 6 * D)], (NDEV,))[0]
    dmod_loc = lax.dynamic_slice(dmod_all, (0, 0, ncol * me), (NDEV, 2, ncol)).transpose(1, 0, 2)
    ada_out = _ada_grad_adamw(c_all.T, dmod_loc, ada_w, m_ada_w, v_ada_w)

    names = [n for n, _ in SHARDED]
    shard_shapes = [w[n].shape for n in names]
    chunks = _pack8([_to_chunks(G[n].astype(F32), kinds[n]) for n in names], F32, FLAT_TILE)
    recv = _all_to_all(chunks, "exchange_grads")
    pk = lambda d: _pack([d[n] for n in names], F32, FLAT_TILE)
    sh_out = _sum_adamw(recv, pk(w), pk(m), pk(v), "sum_adamw_sharded", FLAT_TILE)
    sh_out = [dict(zip(names, _unpack(o, shard_shapes))) for o in sh_out]

    def pick(i, n):
        if n == "ada_w":
            return ada_out[i]
        return rep_out[i][n] if n in REPLICATED else sh_out[i][n]

    outs = [loss, grad_x[None]]
    for i in range(4):
        outs += [pick(i, n) for n in WEIGHTS]
    return tuple(outs)
```

```python
import functools
import math

import jax
import jax.numpy as jnp
from jax import lax
from jax.experimental import pallas as pl
from jax.experimental.pallas import tpu as pltpu

F32 = jnp.float32
BF16 = jnp.bfloat16
HI = lax.Precision.HIGHEST

NDEV = 8
T = 2048
D = 1024
DFF = 4096
HEADS = 8
HD = 64
RW = 512
PA = 2048
PB = 1536
PAB = PA + PB
QKV = 4608
DILS = (1, 4, 16)
BLK = 128
ALPHA = 4.0 ** 0.25
LN_EPS = 1e-5
GN_EPS = 64e-5
ADAM_LR, ADAM_B1, ADAM_B2, ADAM_EPS, ADAM_WD, ADAM_STEP = 0.001, 0.9, 0.999, 1e-8, 0.01, 10
VMEM_LIMIT = 56 * 1024 * 1024


def _cp(sem):
    return pltpu.CompilerParams(dimension_semantics=sem, vmem_limit_bytes=VMEM_LIMIT)


def _slot(px, py, pc):
    return 4 * px + 2 * py + pc


def _all_gather(x, name):
    R, C = x.shape

    def body(x_ref, out_ref, send_sems, recv_sems, local_sem):
        xi, yi, ci = lax.axis_index("x"), lax.axis_index("y"), lax.axis_index("c")
        me, sibling = (xi, yi, ci), (xi, yi, 1 - ci)
        chips = [(1 - xi, yi), (xi, 1 - yi), (1 - xi, 1 - yi)]

        def rows(px, py, pc):
            return out_ref.at[_slot(px, py, pc)]

        def copy(k, block, to, src=None):
            return pltpu.make_async_remote_copy(
                src_ref=rows(*block) if src is None else src, dst_ref=rows(*block),
                send_sem=send_sems.at[k], recv_sem=recv_sems.at[k],
                device_id=to, device_id_type=pl.DeviceIdType.MESH)

        mine = pltpu.make_async_copy(x_ref, rows(*me), local_sem)
        mine.start()
        first = [copy(0, me, sibling, src=x_ref)]
        first += [copy(1 + j, me, (*chip, ci), src=x_ref) for j, chip in enumerate(chips)]
        for cp in first:
            cp.start()
        passed = [copy(4 + j, (*chip, ci), sibling) for j, chip in enumerate(chips)]
        for j, chip in enumerate(chips):
            copy(1 + j, (*chip, ci), me).wait_recv()
            passed[j].start()
        copy(0, sibling, me).wait_recv()
        for j, chip in enumerate(chips):
            copy(4 + j, (*chip, 1 - ci), me).wait_recv()
        for cp in first + passed:
            cp.wait_send()
        mine.wait()

    return pl.pallas_call(
        body, name=name,
        out_shape=jax.ShapeDtypeStruct((NDEV, R, C), x.dtype),
        in_specs=[pl.BlockSpec(memory_space=pl.ANY)],
        out_specs=pl.BlockSpec(memory_space=pl.ANY),
        scratch_shapes=[pltpu.SemaphoreType.DMA((7,)), pltpu.SemaphoreType.DMA((7,)),
                        pltpu.SemaphoreType.DMA(())],
    )(x)


def _all_to_all(g, name):
    _, R, C = g.shape

    def body(g_ref, out_ref, send_sems, recv_sems, local_sem):
        xi, yi, ci = lax.axis_index("x"), lax.axis_index("y"), lax.axis_index("c")
        my_slot = _slot(xi, yi, ci)
        mine = pltpu.make_async_copy(g_ref.at[my_slot], out_ref.at[my_slot], local_sem)
        mine.start()
        copies = []
        for k in range(1, 8):
            px = 1 - xi if k & 4 else xi
            py = 1 - yi if k & 2 else yi
            pc = 1 - ci if k & 1 else ci
            peer_slot = _slot(px, py, pc)
            copies.append((
                pltpu.make_async_remote_copy(
                    src_ref=g_ref.at[peer_slot], dst_ref=out_ref.at[my_slot],
                    send_sem=send_sems.at[k - 1], recv_sem=recv_sems.at[k - 1],
                    device_id=(px, py, pc), device_id_type=pl.DeviceIdType.MESH),
                pltpu.make_async_remote_copy(
                    src_ref=g_ref.at[peer_slot], dst_ref=out_ref.at[peer_slot],
                    send_sem=send_sems.at[k - 1], recv_sem=recv_sems.at[k - 1],
                    device_id=(px, py, pc), device_id_type=pl.DeviceIdType.MESH)))
        for send, _ in copies:
            send.start()
        for _, recv in copies:
            recv.wait_recv()
        for send, _ in copies:
            send.wait_send()
        mine.wait()

    return pl.pallas_call(
        body, name=name,
        out_shape=jax.ShapeDtypeStruct((NDEV, R, C), g.dtype),
        in_specs=[pl.BlockSpec(memory_space=pl.ANY)],
        out_specs=pl.BlockSpec(memory_space=pl.ANY),
        scratch_shapes=[pltpu.SemaphoreType.DMA((7,)), pltpu.SemaphoreType.DMA((7,)),
                        pltpu.SemaphoreType.DMA(())],
    )(g)


def _hbm_call(body, name, ins, out_shapes, n_sems):
    anyspec = pl.BlockSpec(memory_space=pl.ANY)
    return pl.pallas_call(
        body, name=name, out_shape=out_shapes,
        in_specs=[anyspec] * len(ins), out_specs=[anyspec] * len(out_shapes),
        scratch_shapes=[pltpu.SemaphoreType.DMA(s) for s in n_sems],
    )(*ins)


def _all_gather_many(xs, name):
    n = len(xs)

    def body(*refs):
        x_refs, o_refs = refs[:n], refs[n:2 * n]
        send_sems, recv_sems, local_sems = refs[2 * n:]
        xi, yi, ci = lax.axis_index("x"), lax.axis_index("y"), lax.axis_index("c")
        me, sibling = (xi, yi, ci), (xi, yi, 1 - ci)
        chips = [(1 - xi, yi), (xi, 1 - yi), (1 - xi, 1 - yi)]

        def copy(i, k, block, to, src=None):
            dst = o_refs[i].at[_slot(*block)]
            return pltpu.make_async_remote_copy(
                src_ref=dst if src is None else src, dst_ref=dst,
                send_sem=send_sems.at[i, k], recv_sem=recv_sems.at[i, k],
                device_id=to, device_id_type=pl.DeviceIdType.MESH)

        mine = [pltpu.make_async_copy(x_refs[i], o_refs[i].at[_slot(*me)], local_sems.at[i]) for i in range(n)]
        sends = []
        for i in range(n):
            mine[i].start()
            sends += [copy(i, 1 + j, me, (*chip, ci), src=x_refs[i]) for j, chip in enumerate(chips)]
            sends.append(copy(i, 0, me, sibling, src=x_refs[i]))
        for cp in sends:
            cp.start()
        for j, chip in enumerate(chips):
            for i in range(n):
                copy(i, 1 + j, (*chip, ci), me).wait_recv()
                passed = copy(i, 4 + j, (*chip, ci), sibling)
                passed.start()
                sends.append(passed)
        for i in range(n):
            copy(i, 0, sibling, me).wait_recv()
            for j, chip in enumerate(chips):
                copy(i, 4 + j, (*chip, 1 - ci), me).wait_recv()
        for cp in sends:
            cp.wait_send()
        for cp in mine:
            cp.wait()

    return _hbm_call(body, name, xs, [jax.ShapeDtypeStruct((NDEV,) + x.shape, x.dtype) for x in xs],
                     [(n, 7), (n, 7), (n,)])


def _swap_sibling(gs, name):
    n = len(gs)

    def body(*refs):
        g_refs, o_refs = refs[:n], refs[n:2 * n]
        send_sems, recv_sems = refs[2 * n:]
        sibling = (lax.axis_index("x"), lax.axis_index("y"), 1 - lax.axis_index("c"))
        copies = [pltpu.make_async_remote_copy(
            src_ref=g_refs[i], dst_ref=o_refs[i], send_sem=send_sems.at[i], recv_sem=recv_sems.at[i],
            device_id=sibling, device_id_type=pl.DeviceIdType.MESH) for i in range(n)]
        for cp in copies:
            cp.start()
        for cp in copies:
            cp.wait_recv()
        for cp in copies:
            cp.wait_send()

    return _hbm_call(body, name, gs, [jax.ShapeDtypeStruct(g.shape, g.dtype) for g in gs], [(n,), (n,)])


def _exchange_chips(ps, name):
    n = len(ps)

    def body(*refs):
        p_refs, o_refs = refs[:n], refs[n:2 * n]
        send_sems, recv_sems, local_sems = refs[2 * n:]
        xi, yi, ci = lax.axis_index("x"), lax.axis_index("y"), lax.axis_index("c")
        q_me = 2 * xi + yi
        mine = [pltpu.make_async_copy(p_refs[i].at[q_me], o_refs[i].at[q_me], local_sems.at[i]) for i in range(n)]
        for cp in mine:
            cp.start()
        sends, recvs = [], []
        for k in range(1, 4):
            px = 1 - xi if k & 2 else xi
            py = 1 - yi if k & 1 else yi
            q_peer = 2 * px + py
            for i in range(n):
                sends.append(pltpu.make_async_remote_copy(
                    src_ref=p_refs[i].at[q_peer], dst_ref=o_refs[i].at[q_me],
                    send_sem=send_sems.at[i, k - 1], recv_sem=recv_sems.at[i, k - 1],
                    device_id=(px, py, ci), device_id_type=pl.DeviceIdType.MESH))
                recvs.append(pltpu.make_async_remote_copy(
                    src_ref=p_refs[i].at[q_peer], dst_ref=o_refs[i].at[q_peer],
                    send_sem=send_sems.at[i, k - 1], recv_sem=recv_sems.at[i, k - 1],
                    device_id=(px, py, ci), device_id_type=pl.DeviceIdType.MESH))
        for cp in sends:
            cp.start()
        for cp in recvs:
            cp.wait_recv()
        for cp in sends:
            cp.wait_send()
        for cp in mine:
            cp.wait()

    return _hbm_call(body, name, ps, [jax.ShapeDtypeStruct(p.shape, p.dtype) for p in ps],
                     [(n, 3), (n, 3), (n,)])


def _peers(xi, yi, ci):
    return [(1 - xi if k & 4 else xi, 1 - yi if k & 2 else yi, 1 - ci if k & 1 else ci) for k in range(1, 8)]


def _direct_copy(src_refs, land_refs, send_sems, recv_sems, i, k, peer, my_slot, gather):
    src = src_refs[i] if gather else src_refs[i].at[_slot(*peer)]
    return pltpu.make_async_remote_copy(
        src_ref=src, dst_ref=land_refs[i].at[my_slot], send_sem=send_sems.at[7 * i + k], recv_sem=recv_sems.at[7 * i + k],
        device_id=peer, device_id_type=pl.DeviceIdType.MESH)


def _exchange_start(srcs, gather, name):
    n = len(srcs)
    lands = [lax.empty(((NDEV,) + s.shape) if gather else s.shape, s.dtype) for s in srcs]

    def body(*refs):
        s_refs, l_refs = refs[:n], refs[n:2 * n]
        send_sems, recv_sems = refs[2 * n], refs[2 * n + 1]
        token, local_sems = refs[2 * n + 2 + 2 * n], refs[2 * n + 3 + 2 * n]
        xi, yi, ci = lax.axis_index("x"), lax.axis_index("y"), lax.axis_index("c")
        my_slot = _slot(xi, yi, ci)
        for k, peer in enumerate(_peers(xi, yi, ci)):
            for i in range(n):
                _direct_copy(s_refs, l_refs, send_sems, recv_sems, i, k, peer, my_slot, gather).start()
        mine = [pltpu.make_async_copy(s_refs[i] if gather else s_refs[i].at[my_slot], l_refs[i].at[my_slot],
                                      local_sems.at[i]) for i in range(n)]
        for cp in mine:
            cp.start()
        for cp in mine:
            cp.wait()
        token[...] = jnp.zeros_like(token)

    hbm = pl.BlockSpec(memory_space=pltpu.HBM)
    sem = pl.BlockSpec(memory_space=pltpu.SEMAPHORE)
    both = list(srcs) + lands
    return pl.pallas_call(
        body, name=name,
        out_shape=(pltpu.SemaphoreType.DMA((7 * n,)), pltpu.SemaphoreType.DMA((7 * n,)),
                   *[pltpu.HBM(t.shape, t.dtype) for t in both], jax.ShapeDtypeStruct((8, 128), F32)),
        in_specs=[hbm] * (2 * n),
        out_specs=(sem, sem, *[hbm] * (2 * n), pl.BlockSpec(memory_space=pltpu.VMEM)),
        input_output_aliases={i: 2 + i for i in range(2 * n)},
        scratch_shapes=[pltpu.SemaphoreType.DMA((n,))],
        compiler_params=pltpu.CompilerParams(has_side_effects=pltpu.SideEffectType.DATAFLOW_SIDE_EFFECTING),
    )(*[pltpu.with_memory_space_constraint(t, pltpu.HBM) for t in both])


def _exchange_wait(started, gather, after, name):
    send_sems, recv_sems, *thru, _ = started
    n = len(thru) // 2

    def body(*refs):
        s_refs, l_refs = refs[:n], refs[n:2 * n]
        send_sems, recv_sems = refs[2 * n], refs[2 * n + 1]
        xi, yi, ci = lax.axis_index("x"), lax.axis_index("y"), lax.axis_index("c")
        my_slot = _slot(xi, yi, ci)
        for k, peer in enumerate(_peers(xi, yi, ci)):
            for i in range(n):
                _direct_copy(s_refs, l_refs, send_sems, recv_sems, i, k, peer, my_slot, gather).wait_send()
                _direct_copy(s_refs, l_refs, send_sems, recv_sems, i, k, peer, _slot(*peer), gather).wait_recv()

    hbm = pl.BlockSpec(memory_space=pltpu.HBM)
    sem = pl.BlockSpec(memory_space=pltpu.SEMAPHORE)
    outs = pl.pallas_call(
        body, name=name,
        out_shape=tuple(pltpu.HBM(t.shape, t.dtype) for t in thru),
        in_specs=[hbm] * (2 * n) + [sem, sem, pl.BlockSpec(memory_space=pl.ANY)],
        out_specs=tuple([hbm] * (2 * n)),
        input_output_aliases={i: i for i in range(2 * n)},
        compiler_params=pltpu.CompilerParams(has_side_effects=pltpu.SideEffectType.DATAFLOW_SIDE_EFFECTING),
    )(*thru, send_sems, recv_sems, after)
    return outs[n:]


def _mm(name, a, b, tb=False, out=(F32,), epi=None, extras=(), tm=1024, tn=512, tk_cap=2048):
    M, K = a.shape
    N = b.shape[0] if tb else b.shape[1]
    tm, tn = min(tm, M), min(tn, N)
    tk = max(t for t in range(128, min(K, tk_cap) + 1, 128) if K % t == 0)
    assert M % tm == 0 and N % tn == 0 and K % tk == 0, (name, M, N, K)
    nk = K // tk
    ne, no = len(extras), len(out)
    dims = (((1,), (1 if tb else 0,)), ((), ()))

    def kern(*refs):
        a_ref, b_ref = refs[:2]
        e_refs = refs[2:2 + ne]
        o_refs = refs[2 + ne:2 + ne + no]

        def finish(acc):
            outs = epi(acc, *[e[...] for e in e_refs]) if epi is not None else (acc,)
            for o_ref, o in zip(o_refs, outs):
                o_ref[...] = o.astype(o_ref.dtype)

        part = lax.dot_general(a_ref[...], b_ref[...], dims, preferred_element_type=F32)
        if nk == 1:
            finish(part)
            return
        acc_ref = refs[-1]
        k = pl.program_id(2)

        @pl.when(k == 0)
        def _():
            acc_ref[...] = part

        @pl.when(k > 0)
        def _():
            acc_ref[...] += part

        @pl.when(k == nk - 1)
        def _():
            finish(acc_ref[...])

    b_spec = (pl.BlockSpec((tn, tk), lambda i, j, k: (j, k)) if tb
              else pl.BlockSpec((tk, tn), lambda i, j, k: (k, j)))
    tile = pl.BlockSpec((tm, tn), lambda i, j, k: (i, j))
    res = pl.pallas_call(
        kern, name=name, grid=(M // tm, N // tn, nk),
        in_specs=[pl.BlockSpec((tm, tk), lambda i, j, k: (i, k)), b_spec] + [tile] * ne,
        out_specs=[tile] * no,
        out_shape=[jax.ShapeDtypeStruct((M, N), dt) for dt in out],
        scratch_shapes=[pltpu.VMEM((tm, tn), F32)] if nk > 1 else [],
        compiler_params=_cp(("parallel", "parallel", "arbitrary")),
    )(a, b, *extras)
    return res[0] if no == 1 else res


def _rows(name, body, rows, params, out_rows, out_accs=(), tm=256):
    views = [r if isinstance(r, tuple) else (r, r.shape[1], 0) for r in rows]
    n = views[0][0].shape[0]
    assert n % tm == 0
    nr, npar, nor, noa = len(views), len(params), len(out_rows), len(out_accs)

    def kern(*refs):
        r_refs = refs[:nr]
        p_refs = refs[nr:nr + npar]
        o_refs = refs[nr + npar:nr + npar + nor]
        a_refs = refs[nr + npar + nor:]
        outs, accs = body([r[...] for r in r_refs], [p[...] for p in p_refs])
        assert len(outs) == nor and len(accs) == noa, (name, len(outs), len(accs))
        for o_ref, o in zip(o_refs, outs):
            o_ref[...] = o.astype(o_ref.dtype)
        if noa:
            @pl.when(pl.program_id(0) == 0)
            def _():
                for a_ref in a_refs:
                    a_ref[...] = jnp.zeros_like(a_ref)

            for a_ref, a in zip(a_refs, accs):
                a_ref[...] += a.astype(F32)

    def whole(shape):
        nd = len(shape)
        return pl.BlockSpec(tuple(shape), lambda i, nd=nd: (0,) * nd)

    in_specs = [pl.BlockSpec((tm, w), lambda i, cb=cb: (i, cb)) for _, w, cb in views]
    in_specs += [whole(p.shape) for p in params]
    out_specs = [pl.BlockSpec((tm, c), lambda i: (i, 0)) for c, _ in out_rows]
    out_specs += [whole(s) for s in out_accs]
    out_shape = [jax.ShapeDtypeStruct((n, c), dt) for c, dt in out_rows]
    out_shape += [jax.ShapeDtypeStruct(tuple(s), F32) for s in out_accs]
    res = pl.pallas_call(
        kern, name=name, grid=(n // tm,), in_specs=in_specs, out_specs=out_specs,
        out_shape=out_shape, compiler_params=_cp(("arbitrary",)),
    )(*[v[0] for v in views], *params)
    return res[:nor], res[nor:]


def _softplus(z):
    return jnp.maximum(z, 0.0) + jnp.log(1.0 + jnp.exp(jnp.minimum(z, -z)))


def _post_ln(x, y, g, lng, lnb):
    z = ALPHA * x + (1.0 + g) * y
    mu = jnp.mean(z, axis=-1, keepdims=True)
    zc = z - mu
    var = jnp.mean(zc * zc, axis=-1, keepdims=True)
    return zc * lax.rsqrt(var + LN_EPS) * lng + lnb


def _post_ln_mod(x, y, g, lng, lnb, scn, shn):
    xn = _post_ln(x, y, g, lng, lnb)
    return xn, xn * (1.0 + scn) + shn


def _pre_core(E, r_, k_, v_, wd_, ad_, gd_, r1, k1, v1, wd1, ad1, gd1, h, bg, cg, h1, cg1, h2, cg2,
              mu_r, mu_k, mu_v, mu_wd, mu_ad, mu_gd, w0, w_up, a0, a_up, g_up, k_k, k_a,
              cw0, cw1, cw2):
    def mix(x, x1, mu):
        return x + mu * (x1 - x)

    r, k, v = mix(r_, r1, mu_r), mix(k_, k1, mu_k), mix(v_, v1, mu_v)
    wd, ad, gd = mix(wd_, wd1, mu_wd), mix(ad_, ad1, mu_ad), mix(gd_, gd1, mu_gd)
    logw = -_softplus(-(w0 + jnp.dot(jnp.tanh(wd), w_up, preferred_element_type=F32))) - 0.5
    decay = jnp.exp(-jnp.exp(logw))
    iclr = jax.nn.sigmoid(a0 + jnp.dot(ad, a_up, preferred_element_type=F32))
    gate = jnp.dot(jax.nn.sigmoid(gd), g_up, preferred_element_type=F32)
    kk0 = k * k_k
    nrm = jnp.sqrt(jnp.dot(kk0 * kk0, E, precision=HI, preferred_element_type=F32))
    kk = kk0 / jnp.maximum(nrm, 1e-12)
    kh = k * (1.0 + (iclr - 1.0) * k_a)
    yb = bg * (cw2 * (cg * h) + cw1 * (cg1 * h1) + cw0 * (cg2 * h2))
    return r, decay, kh, v, -kk, kk * iclr, gate, yb


def _post_core(E, y, r, kh, v, gate, lnx_g, lnx_b, rk):
    def seg(t):
        return jnp.dot(t, E, precision=HI, preferred_element_type=F32)

    mean = seg(y) * (1.0 / HD)
    yc = y - mean
    var = seg(yc * yc) * (1.0 / HD)
    gn = yc * lax.rsqrt(var + GN_EPS) * lnx_g + lnx_b
    bonus = seg(r * kh * rk) * v
    return (gn + bonus) * gate


def _merge_core(o0, o1, o2, l0, l1, l2):
    m = jnp.maximum(jnp.maximum(l0, l1), l2)
    e0, e1, e2 = jnp.exp(l0 - m), jnp.exp(l1 - m), jnp.exp(l2 - m)
    den = e0 + e1 + e2
    return (e0 * o0 + e1 * o1 + e2 * o2) / den


CHUNK = 128
HALF = 64
HP = HEADS // 2
LW = 2 * HD
NCHUNK = T // CHUNK


def _split3(x):
    hi = x.astype(BF16)
    r1 = x - hi.astype(F32)
    mid = r1.astype(BF16)
    return hi, mid, (r1 - mid.astype(F32)).astype(BF16)


def _pairs(x):
    return x.reshape(T, HP, LW).transpose(1, 0, 2)


def _unpairs(x):
    return x.transpose(1, 0, 2).reshape(T, RW)


def _cols3(parts):
    tr = [p.reshape(NCHUNK, CHUNK, HP, 2, HD).transpose(2, 4, 0, 3, 1) for p in parts]
    return jnp.stack(tr, axis=4).reshape(HP, HD, 6 * T)


def _pick_codes():
    row = lax.broadcasted_iota(jnp.int32, (6 * CHUNK, LW), 0)
    col = lax.broadcasted_iota(jnp.int32, (6 * CHUNK, LW), 1)
    same = (row >= 3 * CHUNK) == (col >= HD)
    return jnp.where(same, row & (CHUNK - 1), -1).astype(BF16)


def _column(block_ref, codes, t):
    pick = jnp.where(codes == t.astype(BF16), jnp.ones((), BF16), jnp.zeros((), BF16))
    return jnp.dot(block_ref[...].reshape(HP * HD, 6 * CHUNK), pick, preferred_element_type=F32)


def _halfsums(x, row, left1):
    row_l = jnp.where(left1, row, 0.0)
    return (jnp.sum(x * row_l, axis=1, keepdims=True), jnp.sum(x * (row - row_l), axis=1, keepdims=True))


def _store_columns(ref, p, t_mask, cols):
    for j, col in enumerate(cols):
        pltpu.store(ref.at[pl.ds(2 * p + j, 1)], jnp.broadcast_to(col[None], (1, HD, CHUNK)), mask=t_mask[None])


def _scan_fwd(r, w, k, a, b, v3):
    def kern(r_ref, w_ref, k_ref, a_ref, b_ref, v_ref, y_ref, ck_ref, s_ref, vb_ref):
        @pl.when(pl.program_id(0) == 0)
        def _():
            s_ref[...] = jnp.zeros_like(s_ref)

        lane = lax.broadcasted_iota(jnp.int32, (HD, CHUNK), 1)
        left = lane < HD
        left1 = lax.broadcasted_iota(jnp.int32, (1, LW), 1) < HD
        codes = _pick_codes()

        def step(t, carry):
            row = lambda ref: [ref[p, pl.ds(t, 1), :] for p in range(HP)]
            S = [s_ref[p] for p in range(HP)]
            sa = [jnp.where(left, *_halfsums(s, a, left1)) for s, a in zip(S, row(a_ref))]
            S = [s * w + c * b + vb_ref[pl.ds(p * HD, HD), :] * k
                 for p, (s, w, c, b, k) in enumerate(zip(S, row(w_ref), sa, row(b_ref), row(k_ref)))]
            for p, s in enumerate(S):
                s_ref[p] = s
            for p, (s, r) in enumerate(zip(S, row(r_ref))):
                _store_columns(y_ref, p, lane == t, _halfsums(s, r, left1))
            vb_ref[...] = _column(v_ref, codes, t + 1)
            return carry

        for half in range(CHUNK // HALF):
            ck_ref[half] = s_ref[...]
            if half == 0:
                vb_ref[...] = _column(v_ref, codes, jnp.int32(0))
            lax.fori_loop(half * HALF, (half + 1) * HALF, step, 0, unroll=2)

    rowblk = pl.BlockSpec((HP, CHUNK, LW), lambda c: (0, c, 0))
    return pl.pallas_call(
        kern, name="rwkv_scan_fwd", grid=(NCHUNK,),
        in_specs=[rowblk] * 5 + [pl.BlockSpec((HP, HD, 6 * CHUNK), lambda c: (0, 0, c))],
        out_specs=[pl.BlockSpec((HEADS, HD, CHUNK), lambda c: (0, 0, c)),
                   pl.BlockSpec((CHUNK // HALF, HP, HD, LW), lambda c: (c, 0, 0, 0))],
        out_shape=[jax.ShapeDtypeStruct((HEADS, HD, T), F32),
                   jax.ShapeDtypeStruct((T // HALF, HP, HD, LW), F32)],
        scratch_shapes=[pltpu.VMEM((HP, HD, LW), F32), pltpu.VMEM((HP * HD, LW), F32)],
        compiler_params=_cp(("arbitrary",)),
    )(r, w, k, a, b, v3)


def _scan_bwd(r, w, k, a, b, v3, dy3, ck):
    NC = T // CHUNK

    def kern(r_ref, w_ref, k_ref, a_ref, b_ref, v_ref, dy_ref, ck_ref,
             dr_ref, dw_ref, dk_ref, da_ref, db_ref, dv_ref, ds_ref, sb_ref, vb_ref, sa_ref, dyb_ref):
        @pl.when(pl.program_id(0) == 0)
        def _():
            ds_ref[...] = jnp.zeros_like(ds_ref)

        lane = lax.broadcasted_iota(jnp.int32, (HD, CHUNK), 1)
        left = lane < HD
        left1 = lax.broadcasted_iota(jnp.int32, (1, LW), 1) < HD
        codes = _pick_codes()

        def rowsum(x):
            return jnp.sum(x, axis=0, keepdims=True)

        for half in reversed(range(CHUNK // HALF)):
            base = half * HALF
            sb_ref[0] = ck_ref[half]

            vb_ref[0] = _column(v_ref, codes, jnp.int32(base))

            def replay(i, carry):
                t = base + i
                row = lambda ref: [ref[p, pl.ds(t, 1), :] for p in range(HP)]
                S = [sb_ref[i, p] for p in range(HP)]
                sa = [jnp.where(left, *_halfsums(s, a, left1)) for s, a in zip(S, row(a_ref))]
                for p, (s, w, c, b, k) in enumerate(zip(S, row(w_ref), sa, row(b_ref), row(k_ref))):
                    sb_ref[i + 1, p] = s * w + c * b + vb_ref[i, pl.ds(p * HD, HD), :] * k
                    sa_ref[i, p] = c
                vb_ref[i + 1] = _column(v_ref, codes, t + 1)
                return carry

            lax.fori_loop(0, HALF, replay, 0, unroll=2)
            dyb_ref[...] = _column(dy_ref, codes, jnp.int32(base + HALF - 1))

            def back(ii, carry):
                i = HALF - 1 - ii
                t = base + i
                row = lambda ref: [ref[p, pl.ds(t, 1), :] for p in range(HP)]
                a_r, b_r, k_r, w_r, r_r = row(a_ref), row(b_ref), row(k_ref), row(w_ref), row(r_ref)
                dys = [dyb_ref[pl.ds(p * HD, HD), :] for p in range(HP)]
                dyb_ref[...] = _column(dy_ref, codes, jnp.maximum(t - 1, 0))
                for p in range(HP):
                    Sp, dy = sb_ref[i, p], dys[p]
                    dS = ds_ref[p] + dy * r_r[p]
                    dr_ref[p, pl.ds(t, 1), :] = rowsum(sb_ref[i + 1, p] * dy)
                    dw_ref[p, pl.ds(t, 1), :] = rowsum(dS * Sp)
                    db_ref[p, pl.ds(t, 1), :] = rowsum(dS * sa_ref[i, p])
                    dk_ref[p, pl.ds(t, 1), :] = rowsum(dS * vb_ref[i, pl.ds(p * HD, HD), :])
                    dsa = jnp.where(left, *_halfsums(dS, b_r[p], left1))
                    _store_columns(dv_ref, p, lane == t, _halfsums(dS, k_r[p], left1))
                    da_ref[p, pl.ds(t, 1), :] = rowsum(Sp * dsa)
                    ds_ref[p] = dS * w_r[p] + dsa * a_r[p]
                return carry

            lax.fori_loop(0, HALF, back, 0, unroll=2)

    rowblk = pl.BlockSpec((HP, CHUNK, LW), lambda c: (0, NC - 1 - c, 0))
    col3blk = pl.BlockSpec((HP, HD, 6 * CHUNK), lambda c: (0, 0, NC - 1 - c))
    rowshape = jax.ShapeDtypeStruct((HP, T, LW), F32)
    return pl.pallas_call(
        kern, name="rwkv_scan_bwd", grid=(NC,),
        in_specs=[rowblk] * 5 + [col3blk, col3blk,
                                 pl.BlockSpec((CHUNK // HALF, HP, HD, LW), lambda c: (NC - 1 - c, 0, 0, 0))],
        out_specs=[rowblk] * 5 + [pl.BlockSpec((HEADS, HD, CHUNK), lambda c: (0, 0, NC - 1 - c))],
        out_shape=[rowshape] * 5 + [jax.ShapeDtypeStruct((HEADS, HD, T), F32)],
        scratch_shapes=[pltpu.VMEM((HP, HD, LW), F32), pltpu.VMEM((HALF + 1, HP, HD, LW), F32),
                        pltpu.VMEM((HALF + 1, HP * HD, LW), F32), pltpu.VMEM((HALF, HP, HD, LW), F32),
                        pltpu.VMEM((HP * HD, LW), F32)],
        compiler_params=_cp(("arbitrary",)),
    )(r, w, k, a, b, v3, dy3, ck)


NBLK = T // BLK


def _blocks_per_segment(g):
    return jnp.where(g == 0, NBLK // DILS[0], jnp.where(g == 1, NBLK // DILS[1], NBLK // DILS[2]))


def _attn_fwd(q, kp, vp, bias):
    def kern(q_ref, k_ref, v_ref, b_ref, o_ref, l_ref):
        nbs = _blocks_per_segment(pl.program_id(0))
        qi = lax.broadcasted_iota(jnp.int32, (BLK, 2 * BLK), 0)
        ki = lax.broadcasted_iota(jnp.int32, (BLK, 2 * BLK), 1)
        band = (ki >= qi) & (ki <= qi + BLK)
        bias_t = b_ref[0, 0]
        for n in range(NBLK):
            lo = jnp.where((n & (nbs - 1)) == 0, BLK, 0)
            valid = band & (ki >= lo)
            qb = q_ref[0, 0, n * BLK:(n + 1) * BLK, :]
            kc = k_ref[0, 0, n * BLK:(n + 2) * BLK, :]
            vc = v_ref[0, 0, n * BLK:(n + 2) * BLK, :]
            s = lax.dot_general(qb, kc, (((1,), (1,)), ((), ())), preferred_element_type=F32)
            s = jnp.where(valid, s * (HD ** -0.5) + bias_t, -jnp.inf)
            m = jnp.max(s, axis=1, keepdims=True)
            e = jnp.exp(s - m)
            den = jnp.sum(e, axis=1, keepdims=True)
            pr = (e / den).astype(BF16)
            o_ref[0, 0, n * BLK:(n + 1) * BLK, :] = jnp.dot(pr, vc, preferred_element_type=F32)
            l_ref[0, 0, n * BLK:(n + 1) * BLK, :] = m + jnp.log(den)

    def blk(rows, cols):
        return pl.BlockSpec((1, 1, rows, cols), lambda g, h: (g, h, 0, 0))

    return pl.pallas_call(
        kern, name="attn_fwd", grid=(3, HEADS),
        in_specs=[blk(T, HD), blk(T + BLK, HD), blk(T + BLK, HD), blk(BLK, 2 * BLK)],
        out_specs=[blk(T, HD), blk(T, 1)],
        out_shape=[jax.ShapeDtypeStruct((3, HEADS, T, HD), F32),
                   jax.ShapeDtypeStruct((3, HEADS, T, 1), F32)],
        compiler_params=_cp(("parallel", "parallel")),
    )(q, kp, vp, bias)


def _attn_bwd(q, kp, vp, bias, biasT, do, lse_c, lse_r, dc_c, dc_r):
    def kern(q_ref, k_ref, v_ref, b_ref, bt_ref, do_ref, lc_ref, lr_ref, dcc_ref, dcr_ref,
             dq_ref, dk_ref, dv_ref, db_ref):
        nbs = _blocks_per_segment(pl.program_id(0))
        qi = lax.broadcasted_iota(jnp.int32, (BLK, 2 * BLK), 0)
        ki = lax.broadcasted_iota(jnp.int32, (BLK, 2 * BLK), 1)
        band = (ki >= qi) & (ki <= qi + BLK)
        kiT = lax.broadcasted_iota(jnp.int32, (2 * BLK, BLK), 0)
        qiT = lax.broadcasted_iota(jnp.int32, (2 * BLK, BLK), 1)
        bandT = (kiT >= qiT) & (kiT <= qiT + BLK)
        bias_t, biasT_t = b_ref[0, 0], bt_ref[0, 0]
        scale = HD ** -0.5
        dk_ref[...] = jnp.zeros_like(dk_ref)
        dv_ref[...] = jnp.zeros_like(dv_ref)
        db_ref[...] = jnp.zeros_like(db_ref)
        nt = (((1,), (1,)), ((), ()))
        for n in range(NBLK):
            lo = jnp.where((n & (nbs - 1)) == 0, BLK, 0)
            qs, ks = slice(n * BLK, (n + 1) * BLK), slice(n * BLK, (n + 2) * BLK)
            qb, kc, vc, dob = q_ref[0, 0, qs, :], k_ref[0, 0, ks, :], v_ref[0, 0, ks, :], do_ref[0, 0, qs, :]
            s = lax.dot_general(qb, kc, nt, preferred_element_type=F32) * scale + bias_t
            p = jnp.where(band & (ki >= lo), jnp.exp(s - lc_ref[0, 0, qs, :]), 0.0)
            dp = lax.dot_general(dob, vc, nt, preferred_element_type=F32)
            ds = p * (dp + dcc_ref[0, 0, qs, :])
            db_ref[0, 0] += ds
            dq_ref[0, 0, qs, :] = jnp.dot((ds * scale).astype(BF16), kc, preferred_element_type=F32)
            sT = lax.dot_general(kc, qb, nt, preferred_element_type=F32) * scale + biasT_t
            pT = jnp.where(bandT & (kiT >= lo), jnp.exp(sT - lr_ref[0, 0, :, qs]), 0.0)
            dpT = lax.dot_general(vc, dob, nt, preferred_element_type=F32)
            dsT = pT * (dpT + dcr_ref[0, 0, :, qs])
            dk_ref[0, 0, ks, :] += jnp.dot((dsT * scale).astype(BF16), qb, preferred_element_type=F32)
            dv_ref[0, 0, ks, :] += jnp.dot(pT.astype(BF16), dob, preferred_element_type=F32)

    def blk(rows, cols):
        return pl.BlockSpec((1, 1, rows, cols), lambda g, h: (g, h, 0, 0))

    return pl.pallas_call(
        kern, name="attn_bwd", grid=(3, HEADS),
        in_specs=[blk(T, HD), blk(T + BLK, HD), blk(T + BLK, HD), blk(BLK, 2 * BLK), blk(2 * BLK, BLK),
                  blk(T, HD), blk(T, 1), blk(1, T), blk(T, 1), blk(1, T)],
        out_specs=[blk(T, HD), blk(T + BLK, HD), blk(T + BLK, HD), blk(BLK, 2 * BLK)],
        out_shape=[jax.ShapeDtypeStruct((3, HEADS, T, HD), F32),
                   jax.ShapeDtypeStruct((3, HEADS, T + BLK, HD), F32),
                   jax.ShapeDtypeStruct((3, HEADS, T + BLK, HD), F32),
                   jax.ShapeDtypeStruct((3, HEADS, BLK, 2 * BLK), F32)],
        compiler_params=_cp(("parallel", "parallel")),
    )(q, kp, vp, bias, biasT, do, lse_c, lse_r, dc_c, dc_r)


NBUCKET = 32
NPAIR = BLK * 2 * BLK


def _relbias_table(rbT, onehotT):
    def kern(rb_ref, oh_ref, out_ref):
        out_ref[0] = sum(jnp.dot(p, oh_ref[0], preferred_element_type=F32) for p in _split3(rb_ref[0]))

    return pl.pallas_call(
        kern, name="relbias_table", grid=(3,),
        in_specs=[pl.BlockSpec((1, HEADS, NBUCKET), lambda g: (g, 0, 0)),
                  pl.BlockSpec((1, NBUCKET, NPAIR), lambda g: (g, 0, 0))],
        out_specs=pl.BlockSpec((1, HEADS, NPAIR), lambda g: (g, 0, 0)),
        out_shape=jax.ShapeDtypeStruct((3, HEADS, NPAIR), F32),
        compiler_params=_cp(("parallel",)),
    )(rbT, onehotT)


def _relbias_grad(db, onehotT):
    nt = (((1,), (1,)), ((), ()))

    def kern(db_ref, oh_ref, out_ref):
        hi, mid, _ = _split3(db_ref[0])
        out_ref[0] = (lax.dot_general(hi, oh_ref[0], nt, preferred_element_type=F32)
                      + lax.dot_general(mid, oh_ref[0], nt, preferred_element_type=F32))

    return pl.pallas_call(
        kern, name="relbias_grad", grid=(3,),
        in_specs=[pl.BlockSpec((1, HEADS, NPAIR), lambda g: (g, 0, 0)),
                  pl.BlockSpec((1, NBUCKET, NPAIR), lambda g: (g, 0, 0))],
        out_specs=pl.BlockSpec((1, HEADS, NBUCKET), lambda g: (g, 0, 0)),
        out_shape=jax.ShapeDtypeStruct((3, HEADS, NBUCKET), F32),
        compiler_params=_cp(("parallel",)),
    )(db, onehotT)


def _adamw(w, g, m, v):
    m2 = ADAM_B1 * m + (1.0 - ADAM_B1) * g
    v2 = ADAM_B2 * v + (1.0 - ADAM_B2) * (g * g)
    m_hat = m2 / (1.0 - ADAM_B1 ** ADAM_STEP)
    v_hat = v2 / (1.0 - ADAM_B2 ** ADAM_STEP)
    return -ADAM_LR * (m_hat / (jnp.sqrt(v_hat) + ADAM_EPS) + ADAM_WD * w), m2, v2


def _ada_mod(c_all, ada_w, ada_b_loc):
    def kern(c_ref, w_ref, b_ref, o_ref):
        c = c_ref[...]
        cond = c * jax.nn.sigmoid(c)
        o_ref[0] = jnp.dot(cond, w_ref[0], precision=HI, preferred_element_type=F32) + b_ref[0]

    ncol = ada_w.shape[2]
    return pl.pallas_call(
        kern, name="ada_mod", grid=(2,),
        in_specs=[pl.BlockSpec((NDEV, D), lambda i: (0, 0)),
                  pl.BlockSpec((1, D, ncol), lambda i: (i, 0, 0)),
                  pl.BlockSpec((1, 1, ncol), lambda i: (i, 0, 0))],
        out_specs=pl.BlockSpec((1, NDEV, ncol), lambda i: (i, 0, 0)),
        out_shape=jax.ShapeDtypeStruct((2, NDEV, ncol), F32),
        compiler_params=_cp(("parallel",)),
    )(c_all, ada_w, ada_b_loc.reshape(2, 1, ncol))


def _ada_grad_adamw(cT_all, dmod_loc, w, m, v):
    ncol = w.shape[2]
    tr = 256

    def kern(c_ref, d_ref, w_ref, m_ref, v_ref, g_ref, dl_ref, m2_ref, v2_ref):
        c = c_ref[...]
        cond = c * jax.nn.sigmoid(c)
        g = jnp.dot(cond, d_ref[0], precision=HI, preferred_element_type=F32)
        dl, m2, v2 = _adamw(w_ref[0], g, m_ref[0], v_ref[0])
        g_ref[0], dl_ref[0], m2_ref[0], v2_ref[0] = g, dl, m2, v2

    big = pl.BlockSpec((1, tr, ncol), lambda i, j: (i, j, 0))
    shp = jax.ShapeDtypeStruct(w.shape, F32)
    return pl.pallas_call(
        kern, name="ada_grad_adamw", grid=(2, D // tr),
        in_specs=[pl.BlockSpec((tr, NDEV), lambda i, j: (j, 0)),
                  pl.BlockSpec((1, NDEV, ncol), lambda i, j: (i, 0, 0)), big, big, big],
        out_specs=[big] * 4, out_shape=[shp] * 4,
        compiler_params=_cp(("parallel", "parallel")),
    )(cT_all, dmod_loc, w, m, v)


def _sum_adamw(recv, w, m, v, name, tr):
    S = recv.shape[0]
    R, C = w.shape
    assert R % tr == 0 and recv.shape[1:] == (R, C)

    def kern(r_ref, w_ref, m_ref, v_ref, g_ref, dl_ref, m2_ref, v2_ref):
        g = r_ref[0].astype(F32)
        for s in range(1, S):
            g = g + r_ref[s].astype(F32)
        dl, m2, v2 = _adamw(w_ref[...], g, m_ref[...], v_ref[...])
        g_ref[...], dl_ref[...], m2_ref[...], v2_ref[...] = g, dl, m2, v2

    flat = pl.BlockSpec((tr, C), lambda i: (i, 0))
    shp = jax.ShapeDtypeStruct((R, C), F32)
    return pl.pallas_call(
        kern, name=name, grid=(R // tr,),
        in_specs=[pl.BlockSpec((S, tr, C), lambda i: (0, i, 0)), flat, flat, flat],
        out_specs=[flat] * 4, out_shape=[shp] * 4,
        compiler_params=_cp(("parallel",)),
    )(recv, w, m, v)


def _pack(arrs, dtype, row_mult):
    flat = jnp.concatenate([a.reshape(-1).astype(dtype) for a in arrs])
    flat = jnp.pad(flat, (0, -flat.shape[0] % (128 * row_mult)))
    return flat.reshape(-1, 128)


def _pack8(arrs, dtype, row_mult):
    flat = jnp.concatenate([a.reshape(NDEV, -1).astype(dtype) for a in arrs], axis=1)
    flat = jnp.pad(flat, ((0, 0), (0, -flat.shape[1] % (128 * row_mult))))
    return flat.reshape(NDEV, -1, 128)


def _unpack(buf, shapes, lead=()):
    flat = buf.reshape(lead + (-1,))
    out, off = [], 0
    for s in shapes:
        n = math.prod(s)
        out.append(flat[..., off:off + n].reshape(lead + tuple(s)))
        off += n
    return out


def _to_chunks(full, kind):
    if kind == "col":
        x = full.reshape(full.shape[:-1] + (NDEV, full.shape[-1] // NDEV))
        return jnp.moveaxis(x, -2, 0)
    x = full.reshape(full.shape[:-2] + (NDEV, full.shape[-2] // NDEV, full.shape[-1]))
    return jnp.moveaxis(x, -3, 0)


def _from_chunks(g8, kind):
    if kind == "col":
        x = jnp.moveaxis(g8, 0, -2)
        return x.reshape(x.shape[:-2] + (x.shape[-2] * x.shape[-1],))
    x = jnp.moveaxis(g8, 0, -3)
    return x.reshape(x.shape[:-3] + (x.shape[-3] * x.shape[-2], x.shape[-1]))


def _pad_pa(x):
    z = lambda n: jnp.zeros(x.shape[:-1] + (n,), x.dtype)
    return jnp.concatenate([x[..., :1600], z(64), x[..., 1600:1664], z(64), x[..., 1664:1824], z(96)], -1)


def _unpad_pa(x):
    return jnp.concatenate([x[..., :1600], x[..., 1664:1728], x[..., 1792:1952]], -1)


def _pad_rows(x, n):
    return jnp.pad(x, ((0, n - x.shape[0]), (0, 0)))


def _shift_down(x, n):
    return jnp.pad(x, ((n, 0), (0, 0)))[:-n]


def _shift_up(x, n):
    return jnp.pad(x, ((0, n), (0, 0)))[n:]


def _unheadsT(x):
    return x.transpose(2, 0, 1).reshape(T, RW)


def _perm(x, dil):
    C = x.shape[-1]
    return x.reshape(T // dil, dil, HEADS, C).transpose(2, 1, 0, 3).reshape(HEADS, T, C)


def _unperm(y, dil):
    C = y.shape[-1]
    return y.reshape(HEADS, dil, T // dil, C).transpose(2, 1, 0, 3).reshape(T, HEADS, C)


def _bucket_tables():
    qi = jnp.arange(BLK)[:, None]
    ki = jnp.arange(2 * BLK)[None, :]
    rel = BLK + qi - ki
    tabs = []
    for dil in DILS:
        dist = jnp.clip(rel, 0, BLK) * dil
        logd = jnp.log(jnp.maximum(dist, 1).astype(F32) / 16) / math.log(2048 / 16)
        large = jnp.minimum(16 + (logd * 16).astype(jnp.int32), 31)
        tabs.append(jnp.where(dist < 16, dist, large))
    return jnp.stack(tabs)


SHARDED = (("ln_g", "col"), ("ln_b", "col"), ("ab_w_in", "col"), ("rw_w_up", "col"), ("rw_a_up", "col"),
           ("rw_g_up", "col"), ("sc_conv_w", "col"), ("ab_w_out", "row"), ("dil_w_qkv", "col"),
           ("dil_w_out", "col"), ("mlp_w1", "col"), ("mlp_w2", "row"))
FIRST = ("ab_w_in", "ab_w_out")
LATER = ("dil_w_qkv", "dil_w_out", "mlp_w1", "mlp_w2")
GATHER_BF16 = FIRST + LATER
GATHER_F32 = ("rw_w_up", "rw_a_up", "rw_g_up", "sc_conv_w", "ln_g", "ln_b")
REPLICATED = ("ada_b", "rw_mu", "rw_w0", "rw_a0", "rw_k_k", "rw_k_a", "rw_r_k", "rw_lnx_g", "rw_lnx_b", "rel_bias")
WEIGHTS = ("ada_w", "ada_b", "ln_g", "ln_b", "ab_w_in", "rw_mu", "rw_w0", "rw_w_up", "rw_a0", "rw_a_up",
           "rw_g_up", "rw_k_k", "rw_k_a", "rw_r_k", "rw_lnx_g", "rw_lnx_b", "sc_conv_w", "ab_w_out",
           "dil_w_qkv", "dil_w_out", "rel_bias", "mlp_w1", "mlp_w2")
FLAT_TILE = 512


def _local_step(x0, tgt, mod, W, P, later_weights, early_grads):
    row = lambda a: a.reshape(1, -1)
    W = dict(W)
    m6 = mod.reshape(2, 6, 1, D)
    sc = [m6[0, 1], m6[0, 4], m6[1, 1], m6[1, 4]]
    sh = [m6[0, 0], m6[0, 3], m6[1, 0], m6[1, 3]]
    gt = [m6[0, 2], m6[0, 5], m6[1, 2], m6[1, 5]]
    lng = [row(P["ln_g"][0, 0]), row(P["ln_g"][0, 1]), row(P["ln_g"][1, 0]), row(P["ln_g"][1, 1])]
    lnb = [row(P["ln_b"][0, 0]), row(P["ln_b"][0, 1]), row(P["ln_b"][1, 0]), row(P["ln_b"][1, 1])]
    E = jnp.kron(jnp.eye(HEADS, dtype=F32), jnp.ones((HD, HD), F32))

    def mod_body(r, p):
        return [r[0] * (1.0 + p[0]) + p[1]], []

    (u0,), _ = _rows("modulate", mod_body, [x0], [sc[0], sh[0]], [(D, BF16)])

    def post_fwd_body(r, p):
        xn, un = _post_ln_mod(r[0], r[1], *p)
        return [xn, un], []

    def post_fwd(s, x, y):
        (xn, un), _ = _rows(f"post_ln_{s}", post_fwd_body, [x, y],
                            [gt[s], lng[s], lnb[s], sc[s + 1], sh[s + 1]], [(D, F32), (D, BF16)])
        return xn, un

    def relu2(acc):
        a = jnp.maximum(acc, 0.0)
        return acc, a * a

    def relu2_bwd(acc, h):
        return (acc * (2.0 * jnp.maximum(h, 0.0)),)

    p = _mm("ab_in", u0, W["ab_w_in"])
    p1 = _shift_down(p, 1)
    p2 = _shift_down(p[:, PA:], 2)
    mu = _pad_pa(P["rw_mu"])
    mu_parts = [mu[:, :512], mu[:, 512:1024], mu[:, 1024:1536], mu[:, 1536:1664], mu[:, 1664:1792], mu[:, 1792:]]
    pre_params = mu_parts + [P["rw_w0"], _pad_rows(P["rw_w_up"], 128), P["rw_a0"], _pad_rows(P["rw_a_up"], 128),
                             _pad_rows(P["rw_g_up"], 256), P["rw_k_k"], P["rw_k_a"],
                             P["sc_conv_w"][0:1], P["sc_conv_w"][1:2], P["sc_conv_w"][2:3]]
    pre_rows = [(p, 512, 0), (p, 512, 1), (p, 512, 2), (p, 128, 12), (p, 128, 13), (p, 256, 7),
                (p1, 512, 0), (p1, 512, 1), (p1, 512, 2), (p1, 128, 12), (p1, 128, 13), (p1, 256, 7),
                (p, 512, 4), (p, 512, 5), (p, 512, 6), (p1, 512, 4), (p1, 512, 6), (p2, 512, 0), (p2, 512, 2)]
    NPR = len(pre_rows)

    def pre_fwd_body(r, pp):
        outs = list(_pre_core(pp[0], *r, *pp[1:]))
        return outs + list(_split3(outs[3])), []

    (r_, w_, kh_, v_, a_, b_, gate_, yb, *v_parts), _ = _rows(
        "rwkv_pre", pre_fwd_body, pre_rows, [E] + pre_params,
        [(RW, F32)] * 7 + [(RW, BF16)] * 4, tm=128)
    scan_in = [_pairs(t) for t in (r_, w_, kh_, a_, b_)] + [_cols3(v_parts)]
    yT, ck = _scan_fwd(*scan_in)
    ysc = _unheadsT(yT)
    post_params = [P["rw_lnx_g"], P["rw_lnx_b"], P["rw_r_k"].reshape(1, RW)]

    def postmix_fwd_body(r, pp):
        return [_post_core(pp[0], *r, *pp[1:])], []

    (ya,), _ = _rows("rwkv_post", postmix_fwd_body, [ysc, r_, kh_, v_, gate_], [E] + post_params,
                     [(RW, BF16)], tm=128)
    cat = jnp.concatenate([ya, yb], axis=1)
    y0 = _mm("ab_out", cat, W["ab_w_out"])
    x1, u1 = post_fwd(0, x0, y0)
    W.update(later_weights(u1))

    h1, a1 = _mm("mlp1_up_0", u1, W["mlp_w1"][0], out=(F32, BF16), epi=relu2)
    y1 = _mm("mlp1_down_0", a1, W["mlp_w2"][0])
    x2, u2 = post_fwd(1, x1, y1)

    pq = _mm("qkv", u2, W["dil_w_qkv"], out=(BF16,))
    pq5 = pq.reshape(T, 3, 3, HEADS, HD)
    q = jnp.stack([_perm(pq5[:, g, 0], DILS[g]) for g in range(3)])
    kp = jnp.pad(jnp.stack([_perm(pq5[:, g, 1], DILS[g]) for g in range(3)]), ((0, 0), (0, 0), (BLK, 0), (0, 0)))
    vp = jnp.pad(jnp.stack([_perm(pq5[:, g, 2], DILS[g]) for g in range(3)]), ((0, 0), (0, 0), (BLK, 0), (0, 0)))
    onehotT = (_bucket_tables().reshape(3, 1, NPAIR) == jnp.arange(NBUCKET).reshape(1, NBUCKET, 1)).astype(BF16)
    rbT = P["rel_bias"].reshape(NBUCKET, 3, HEADS).transpose(1, 2, 0)
    bias = _relbias_table(rbT, onehotT).reshape(3, HEADS, BLK, 2 * BLK)
    og, lse = _attn_fwd(q, kp, vp, bias)
    R = T * HEADS
    o_nat = [_unperm(og[g], DILS[g]).reshape(R, HD) for g in range(3)]
    l_nat = [_unperm(lse[g], DILS[g]).reshape(R, 1) for g in range(3)]

    def merge_fwd_body(r, pp):
        return [_merge_core(*r)], []

    (om,), _ = _rows("attn_merge", merge_fwd_body, o_nat + l_nat, [], [(HD, BF16)], tm=1024)
    om = om.reshape(T, RW)
    y2 = _mm("dil_out", om, W["dil_w_out"])
    x3, u3 = post_fwd(2, x2, y2)

    h3, a3 = _mm("mlp1_up_1", u3, W["mlp_w1"][1], out=(F32, BF16), epi=relu2)
    y3 = _mm("mlp1_down_1", a3, W["mlp_w2"][1])

    def last_body(r, pp):
        x, y, tg = r
        xn, vjp = jax.vjp(_post_ln, x, y, *pp)
        err = xn - tg
        dx, dy, dg, dlg, dlb = vjp(err * (1.0 / D))
        loss = jnp.full((1, 128), (0.5 / D) * jnp.sum(err * err), F32)
        return [dx, dy], [loss, dg, dlg, dlb]

    (dxp, dy3), (loss_acc, dg3, dlng3, dlnb3) = _rows(
        "final_ln_loss", last_body, [x3, y3, tgt], [gt[3], lng[3], lnb[3]],
        [(D, F32), (D, BF16)], [(1, 128), (1, D), (1, D), (1, D)])

    G = {}
    dsc, dsh, dgt = [None] * 4, [None] * 4, [None] * 4
    dlng, dlnb = [None] * 4, [None] * 4
    dgt[3], dlng[3], dlnb[3] = dg3, dlng3, dlnb3

    def mlp_bwd(i, u, h, a, dy):
        dh = _mm(f"mlp_dh_{i}", dy, W["mlp_w2"][i], tb=True, out=(BF16,), epi=relu2_bwd, extras=(h,))
        gw2 = _mm(f"mlp_dw2_{i}", a.T, dy)
        du = _mm(f"mlp_du_{i}", dh, W["mlp_w1"][i], tb=True)
        gw1 = _mm(f"mlp_dw1_{i}", u.T, dh)
        return du, gw1, gw2

    def post_bwd_body(r, pp):
        x, y, dxn, dun = r
        _, vjp = jax.vjp(_post_ln_mod, x, y, *pp)
        dx, dy, dg, dlg, dlb, dscn, dshn = vjp((dxn, dun))
        return [dx, dy], [dg, dlg, dlb, dscn, dshn]

    def post_bwd(s, x, y, dxn, dun):
        (dx, dy), (dgt[s], dlng[s], dlnb[s], dsc[s + 1], dsh[s + 1]) = _rows(
            f"post_ln_bwd_{s}", post_bwd_body, [x, y, dxn, dun],
            [gt[s], lng[s], lnb[s], sc[s + 1], sh[s + 1]], [(D, F32), (D, BF16)], [(1, D)] * 5)
        return dx, dy

    du3, gw1_1, gw2_1 = mlp_bwd(1, u3, h3, a3, dy3)
    dxp, dy2 = post_bwd(2, x2, y2, dxp, du3)

    G["dil_w_out"] = _mm("dil_out_dw", om.T, dy2)[None]
    do = _mm("dil_out_dx", dy2, W["dil_w_out"], tb=True).reshape(R, HD)

    def merge_bwd_body(r, pp):
        o_l, dout = r[:6], r[6]
        _, vjp = jax.vjp(_merge_core, *o_l)
        d = vjp(dout)
        dcs = [d[3 + g] - jnp.sum(d[g] * o_l[g], axis=1, keepdims=True) for g in range(3)]
        return list(d[:3]) + dcs, []

    mb, _ = _rows("attn_merge_bwd", merge_bwd_body, o_nat + l_nat + [do], [],
                  [(HD, BF16)] * 3 + [(1, F32)] * 3, tm=1024)
    dog = jnp.stack([_perm(mb[g].reshape(T, HEADS, HD), DILS[g]) for g in range(3)])
    dcc = jnp.stack([_perm(mb[3 + g].reshape(T, HEADS, 1), DILS[g]) for g in range(3)])
    dq, dkp, dvp, dbias = _attn_bwd(q, kp, vp, bias, jnp.swapaxes(bias, 2, 3), dog, lse,
                                    lse.reshape(3, HEADS, 1, T), dcc, dcc.reshape(3, HEADS, 1, T))
    dpq = jnp.concatenate(
        [_unperm(t[g], DILS[g]).reshape(T, RW) for g in range(3) for t in (dq, dkp[:, :, BLK:], dvp[:, :, BLK:])],
        axis=1).astype(BF16)
    rb = _relbias_grad(dbias.reshape(3, HEADS, NPAIR), onehotT)
    G["rel_bias"] = rb.transpose(2, 0, 1).reshape(NBUCKET, 3 * HEADS)
    G["dil_w_qkv"] = _mm("qkv_dw", u2.T, dpq)[None]
    du2 = _mm("qkv_dx", dpq, W["dil_w_qkv"], tb=True)
    dxp, dy1 = post_bwd(1, x1, y1, dxp, du2)

    du1, gw1_0, gw2_0 = mlp_bwd(0, u1, h1, a1, dy1)
    G["mlp_w1"] = jnp.stack([gw1_0, gw1_1])
    G["mlp_w2"] = jnp.stack([gw2_0, gw2_1])
    gt[0] = gt[0] + early_grads(G)
    dxp, dy0 = post_bwd(0, x0, y0, dxp, du1)

    G["ab_w_out"] = _mm("ab_out_dw", cat.T, dy0)[None]
    dcat = _mm("ab_out_dx", dy0, W["ab_w_out"], tb=True)

    def postmix_bwd_body(r, pp):
        _, vjp = jax.vjp(functools.partial(_post_core, pp[0]), *r[:5], *pp[1:])
        d = vjp(r[5])
        return list(_split3(d[0])) + list(d[1:5]), list(d[5:])

    (*dy_parts, dr1, dkh1, dv1, dgate), (G["rw_lnx_g"], G["rw_lnx_b"], drk) = _rows(
        "rwkv_post_bwd", postmix_bwd_body, [ysc, r_, kh_, v_, gate_, (dcat, 512, 0)], [E] + post_params,
        [(RW, BF16)] * 3 + [(RW, F32)] * 4, [(1, RW)] * 3, tm=128)
    G["rw_r_k"] = drk.reshape(1, HEADS, HD)
    dr2, dw2, dk2, da2, db2, dvT = _scan_bwd(*scan_in, _cols3(dy_parts), ck)
    dr2, dw2, dk2, da2, db2 = [_unpairs(t) for t in (dr2, dw2, dk2, da2, db2)]
    dv2 = _unheadsT(dvT)

    def pre_bwd_body(r, pp):
        prim, ct = r[:NPR], r[NPR:]
        _, vjp = jax.vjp(functools.partial(_pre_core, pp[0]), *prim, *pp[1:])
        cts = (ct[0] + ct[1], ct[2], ct[3] + ct[4], ct[5] + ct[6], ct[7], ct[8], ct[9], ct[10])
        d = vjp(cts)
        z = jnp.zeros_like(d[12])
        dp = jnp.concatenate([d[0], d[1], d[2], d[3], d[4], d[5], d[12], d[13], d[14]], axis=1)
        dp1 = jnp.concatenate([d[6], d[7], d[8], d[9], d[10], d[11], d[15], z, d[16]], axis=1)
        dp2 = jnp.concatenate([d[17], z, d[18]], axis=1)
        return [dp, dp1, dp2], list(d[NPR:])

    acc_shapes = [a.shape for a in pre_params]
    (dp, dp1, dp2), pacc = _rows(
        "rwkv_pre_bwd", pre_bwd_body,
        pre_rows + [dr1, dr2, dw2, dkh1, dk2, dv1, dv2, da2, db2, dgate, (dcat, 512, 1)],
        [E] + pre_params, [(PAB, F32), (PAB, F32), (PB, F32)], acc_shapes, tm=128)
    G["rw_mu"] = _unpad_pa(jnp.concatenate(pacc[:6], axis=1))
    G["rw_w0"], G["rw_a0"], G["rw_k_k"], G["rw_k_a"] = pacc[6], pacc[8], pacc[11], pacc[12]
    G["rw_w_up"] = pacc[7][None, :64]
    G["rw_a_up"] = pacc[9][None, :64]
    G["rw_g_up"] = pacc[10][None, :160]
    G["sc_conv_w"] = jnp.concatenate(pacc[13:16], axis=0)[None]

    def add3_body(r, pp):
        return [r[0] + r[1] + r[2]], []

    (dpt,), _ = _rows("shift_merge", add3_body,
                      [dp, _shift_up(dp1, 1), jnp.pad(_shift_up(dp2, 2), ((0, 0), (PA, 0)))], [], [(PAB, BF16)])
    gin = _mm("ab_in_dw", u0.T, dpt)
    G["ab_w_in"] = jnp.concatenate([_unpad_pa(gin[:, :PA]), gin[:, PA:]], axis=1)[None]
    du0 = _mm("ab_in_dx", dpt, W["ab_w_in"], tb=True)

    def mod_bwd_body(r, pp):
        du, dx, x = r
        return [dx + du * (1.0 + pp[0])], [jnp.sum(du * x, axis=0, keepdims=True), jnp.sum(du, axis=0, keepdims=True)]

    (grad_x,), (dsc[0], dsh[0]) = _rows("modulate_bwd", mod_bwd_body, [du0, dxp, x0], [sc[0]], [(D, F32)],
                                        [(1, D), (1, D)])

    G["ln_g"] = jnp.concatenate(dlng, axis=0).reshape(2, 2, D)
    G["ln_b"] = jnp.concatenate(dlnb, axis=0).reshape(2, 2, D)
    dmod = jnp.concatenate([dsh[0], dsc[0], dgt[0], dsh[1], dsc[1], dgt[1],
                            dsh[2], dsc[2], dgt[2], dsh[3], dsc[3], dgt[3]], axis=1).reshape(2, 6 * D)
    return loss_acc[0, 0], grad_x, dmod, G


def kernel(x, c, ada_w, ada_b, ln_g, ln_b, ab_w_in, rw_mu, rw_w0, rw_w_up, rw_a0, rw_a_up, rw_g_up, rw_k_k, rw_k_a, rw_r_k, rw_lnx_g, rw_lnx_b, sc_conv_w, ab_w_out, dil_w_qkv, dil_w_out, rel_bias, mlp_w1, mlp_w2, loss_target, m_ada_w, m_ada_b, m_ln_g, m_ln_b, m_ab_w_in, m_rw_mu, m_rw_w0, m_rw_w_up, m_rw_a0, m_rw_a_up, m_rw_g_up, m_rw_k_k, m_rw_k_a, m_rw_r_k, m_rw_lnx_g, m_rw_lnx_b, m_sc_conv_w, m_ab_w_out, m_dil_w_qkv, m_dil_w_out, m_rel_bias, m_mlp_w1, m_mlp_w2, v_ada_w, v_ada_b, v_ln_g, v_ln_b, v_ab_w_in, v_rw_mu, v_rw_w0, v_rw_w_up, v_rw_a0, v_rw_a_up, v_rw_g_up, v_rw_k_k, v_rw_k_a, v_rw_r_k, v_rw_lnx_g, v_rw_lnx_b, v_sc_conv_w, v_ab_w_out, v_dil_w_qkv, v_dil_w_out, v_rel_bias, v_mlp_w1, v_mlp_w2):
    w = dict(ada_w=ada_w, ada_b=ada_b, ln_g=ln_g, ln_b=ln_b, ab_w_in=ab_w_in, rw_mu=rw_mu, rw_w0=rw_w0,
             rw_w_up=rw_w_up, rw_a0=rw_a0, rw_a_up=rw_a_up, rw_g_up=rw_g_up, rw_k_k=rw_k_k, rw_k_a=rw_k_a,
             rw_r_k=rw_r_k, rw_lnx_g=rw_lnx_g, rw_lnx_b=rw_lnx_b, sc_conv_w=sc_conv_w, ab_w_out=ab_w_out,
             dil_w_qkv=dil_w_qkv, dil_w_out=dil_w_out, rel_bias=rel_bias, mlp_w1=mlp_w1, mlp_w2=mlp_w2)
    m = dict(ada_w=m_ada_w, ada_b=m_ada_b, ln_g=m_ln_g, ln_b=m_ln_b, ab_w_in=m_ab_w_in, rw_mu=m_rw_mu,
             rw_w0=m_rw_w0, rw_w_up=m_rw_w_up, rw_a0=m_rw_a0, rw_a_up=m_rw_a_up, rw_g_up=m_rw_g_up,
             rw_k_k=m_rw_k_k, rw_k_a=m_rw_k_a, rw_r_k=m_rw_r_k, rw_lnx_g=m_rw_lnx_g, rw_lnx_b=m_rw_lnx_b,
             sc_conv_w=m_sc_conv_w, ab_w_out=m_ab_w_out, dil_w_qkv=m_dil_w_qkv, dil_w_out=m_dil_w_out,
             rel_bias=m_rel_bias, mlp_w1=m_mlp_w1, mlp_w2=m_mlp_w2)
    v = dict(ada_w=v_ada_w, ada_b=v_ada_b, ln_g=v_ln_g, ln_b=v_ln_b, ab_w_in=v_ab_w_in, rw_mu=v_rw_mu,
             rw_w0=v_rw_w0, rw_w_up=v_rw_w_up, rw_a0=v_rw_a0, rw_a_up=v_rw_a_up, rw_g_up=v_rw_g_up,
             rw_k_k=v_rw_k_k, rw_k_a=v_rw_k_a, rw_r_k=v_rw_r_k, rw_lnx_g=v_rw_lnx_g, rw_lnx_b=v_rw_lnx_b,
             sc_conv_w=v_sc_conv_w, ab_w_out=v_ab_w_out, dil_w_qkv=v_dil_w_qkv, dil_w_out=v_dil_w_out,
             rel_bias=v_rel_bias, mlp_w1=v_mlp_w1, mlp_w2=v_mlp_w2)
    kinds = dict(SHARDED)
    me = 4 * lax.axis_index("x") + 2 * lax.axis_index("y") + lax.axis_index("c")
    ncol = ada_w.shape[2]

    small = _all_gather(_pack([c] + [w[n] for n in GATHER_F32], F32, 8), "gather_small")
    parts = _unpack(small, [c.shape] + [w[n].shape for n in GATHER_F32], (NDEV,))
    c_all = parts[0].reshape(NDEV, D)
    P = {n: _from_chunks(t, kinds[n]) for n, t in zip(GATHER_F32, parts[1:])}
    P = {n: (t if n in ("ln_g", "ln_b") else t[0]) for n, t in P.items()}
    for n in REPLICATED[1:]:
        P[n] = w[n]
    def full(n, t):
        t = _from_chunks(t, kinds[n])
        return t if n in ("mlp_w1", "mlp_w2") else t[0]

    parts = _all_gather_many([w[n].astype(BF16) for n in FIRST], "gather_first_weights")
    W = {n: full(n, t) for n, t in zip(FIRST, parts)}
    W["ab_w_in"] = jnp.concatenate([_pad_pa(W["ab_w_in"][:, :1824]), W["ab_w_in"][:, 1824:]], axis=1)
    later = _exchange_start([w[n].astype(BF16) for n in LATER], True, "gather_later_weights_start")

    def later_weights(after):
        lands = _exchange_wait(later, True, after, "gather_later_weights_wait")
        return {n: full(n, t) for n, t in zip(LATER, lands)}

    ada_b_loc = lax.dynamic_slice(ada_b, (0, ncol * me), (2, ncol))
    mod_part = _ada_mod(c_all + later[-1][0, 0], ada_w, ada_b_loc)
    mod_all = _all_gather(mod_part.reshape(-1, 128), "gather_mod").reshape(NDEV, 2, NDEV, ncol)
    mod = lax.dynamic_index_in_dim(mod_all, me, axis=2, keepdims=False)
    mod = mod.transpose(1, 0, 2).reshape(2, 6 * D)

    sent = []

    def early_grads(G):
        sent.append(_exchange_start([_to_chunks(G[n], kinds[n]).astype(BF16) for n in LATER], False,
                                    "exchange_later_grads_start"))
        return sent[0][-1][0, 0]

    loss_part, grad_x, dmod, G = _local_step(x[0], loss_target[0], mod, W, P, later_weights, early_grads)
    G["ada_b"] = dmod
    loss = lax.psum(loss_part, ("x", "y", "c"))

    rep_shapes = [w[n].shape for n in REPLICATED]
    rep_all = _all_gather(_pack([G[n] for n in REPLICATED], F32, 8), "gather_replicated_grads")
    pk = lambda d: _pack([d[n] for n in REPLICATED], F32, 8)
    rep_out = _sum_adamw(rep_all, pk(w), pk(m), pk(v), "sum_adamw_replicated", rep_all.shape[1])
    rep_out = [dict(zip(REPLICATED, _unpack(o, rep_shapes))) for o in rep_out]

    dmod_all = _unpack(rep_all, [(2, 6 * D)], (NDEV,))[0]
    dmod_loc = lax.dynamic_slice(dmod_all, (0, 0, ncol * me), (NDEV, 2, ncol)).transpose(1, 0, 2)
    ada_out = _ada_grad_adamw(c_all.T, dmod_loc, ada_w, m_ada_w, v_ada_w)

    names = [n for n, _ in SHARDED if n not in GATHER_BF16]
    shard_shapes = [w[n].shape for n in names]
    chunks = _pack8([_to_chunks(G[n], kinds[n]) for n in names], F32, 8)
    recv = _all_to_all(chunks, "exchange_small_grads")
    pk = lambda d: _pack([d[n] for n in names], F32, 8)
    sh_out = _sum_adamw(recv, pk(w), pk(m), pk(v), "sum_adamw_small", recv.shape[1])
    sh_out = [dict(zip(names, _unpack(o, shard_shapes))) for o in sh_out]

    big_out = {}

    def update(n, contributions):
        cols = w[n].shape[-1]
        flat = lambda t: t.reshape(-1, cols)
        rows = flat(w[n]).shape[0]
        outs = _sum_adamw(contributions.reshape(-1, rows, cols), flat(w[n]), flat(m[n]), flat(v[n]),
                          f"sum_adamw_{n}", min(rows, 256))
        big_out[n] = [o.reshape(w[n].shape) for o in outs]

    ci = lax.axis_index("c")
    mine_l, sib_l = [], []
    for n in FIRST:
        g8 = _to_chunks(G[n], kinds[n])
        g42 = g8.reshape((4, 2) + g8.shape[1:])
        mine_l.append(lax.dynamic_index_in_dim(g42, ci, 1, keepdims=False))
        sib_l.append(lax.dynamic_index_in_dim(g42, 1 - ci, 1, keepdims=False))
    from_sib = _swap_sibling(sib_l, "swap_sibling_grads")

    def add2_body(r, pp):
        return [r[0] + r[1]], []

    partials = []
    for n, a, b in zip(FIRST, mine_l, from_sib):
        cols = a.shape[-1]
        (p,), _ = _rows(f"pair_sum_{n}", add2_body, [a.reshape(-1, cols), b.reshape(-1, cols)], [],
                        [(cols, BF16)], tm=512)
        partials.append(p.reshape(a.shape))
    for n, r in zip(FIRST, _exchange_chips(partials, "exchange_chip_grads")):
        update(n, r)

    for n, r in zip(LATER, _exchange_wait(sent[0], False, partials[0], "exchange_later_grads_wait")):
        update(n, r)
    sh_out = [{**d, **{n: big_out[n][i] for n in GATHER_BF16}} for i, d in enumerate(sh_out)]

    def pick(i, n):
        if n == "ada_w":
            return ada_out[i]
        return rep_out[i][n] if n in REPLICATED else sh_out[i][n]

    outs = [loss, grad_x[None]]
    for i in range(4):
        outs += [pick(i, n) for n in WEIGHTS]
    return tuple(outs)
```

```python
import functools
import math

import jax
import jax.numpy as jnp
from jax import lax
from jax.experimental import pallas as pl
from jax.experimental.pallas import tpu as pltpu

F32 = jnp.float32
BF16 = jnp.bfloat16
HI = lax.Precision.HIGHEST

NDEV = 8
T = 2048
D = 1024
DFF = 4096
HEADS = 8
HD = 64
RW = 512
PA = 2048
PB = 1536
PAB = PA + PB
QKV = 4608
DILS = (1, 4, 16)
BLK = 128
ALPHA = 4.0 ** 0.25
LN_EPS = 1e-5
GN_EPS = 64e-5
ADAM_LR, ADAM_B1, ADAM_B2, ADAM_EPS, ADAM_WD, ADAM_STEP = 0.001, 0.9, 0.999, 1e-8, 0.01, 10
VMEM_LIMIT = 56 * 1024 * 1024


def _cp(sem):
    return pltpu.CompilerParams(dimension_semantics=sem, vmem_limit_bytes=VMEM_LIMIT)


def _slot(px, py, pc):
    return 4 * px + 2 * py + pc


def _all_gather(x, name):
    R, C = x.shape

    def body(x_ref, out_ref, send_sems, recv_sems, local_sem):
        xi, yi, ci = lax.axis_index("x"), lax.axis_index("y"), lax.axis_index("c")
        me, sibling = (xi, yi, ci), (xi, yi, 1 - ci)
        chips = [(1 - xi, yi), (xi, 1 - yi), (1 - xi, 1 - yi)]

        def rows(px, py, pc):
            return out_ref.at[_slot(px, py, pc)]

        def copy(k, block, to, src=None):
            return pltpu.make_async_remote_copy(
                src_ref=rows(*block) if src is None else src, dst_ref=rows(*block),
                send_sem=send_sems.at[k], recv_sem=recv_sems.at[k],
                device_id=to, device_id_type=pl.DeviceIdType.MESH)

        mine = pltpu.make_async_copy(x_ref, rows(*me), local_sem)
        mine.start()
        first = [copy(0, me, sibling, src=x_ref)]
        first += [copy(1 + j, me, (*chip, ci), src=x_ref) for j, chip in enumerate(chips)]
        for cp in first:
            cp.start()
        passed = [copy(4 + j, (*chip, ci), sibling) for j, chip in enumerate(chips)]
        for j, chip in enumerate(chips):
            copy(1 + j, (*chip, ci), me).wait_recv()
            passed[j].start()
        copy(0, sibling, me).wait_recv()
        for j, chip in enumerate(chips):
            copy(4 + j, (*chip, 1 - ci), me).wait_recv()
        for cp in first + passed:
            cp.wait_send()
        mine.wait()

    return pl.pallas_call(
        body, name=name,
        out_shape=jax.ShapeDtypeStruct((NDEV, R, C), x.dtype),
        in_specs=[pl.BlockSpec(memory_space=pl.ANY)],
        out_specs=pl.BlockSpec(memory_space=pl.ANY),
        scratch_shapes=[pltpu.SemaphoreType.DMA((7,)), pltpu.SemaphoreType.DMA((7,)),
                        pltpu.SemaphoreType.DMA(())],
    )(x)


def _all_to_all(g, name):
    _, R, C = g.shape

    def body(g_ref, out_ref, send_sems, recv_sems, local_sem):
        xi, yi, ci = lax.axis_index("x"), lax.axis_index("y"), lax.axis_index("c")
        my_slot = _slot(xi, yi, ci)
        mine = pltpu.make_async_copy(g_ref.at[my_slot], out_ref.at[my_slot], local_sem)
        mine.start()
        copies = []
        for k in range(1, 8):
            px = 1 - xi if k & 4 else xi
            py = 1 - yi if k & 2 else yi
            pc = 1 - ci if k & 1 else ci
            peer_slot = _slot(px, py, pc)
            copies.append((
                pltpu.make_async_remote_copy(
                    src_ref=g_ref.at[peer_slot], dst_ref=out_ref.at[my_slot],
                    send_sem=send_sems.at[k - 1], recv_sem=recv_sems.at[k - 1],
                    device_id=(px, py, pc), device_id_type=pl.DeviceIdType.MESH),
                pltpu.make_async_remote_copy(
                    src_ref=g_ref.at[peer_slot], dst_ref=out_ref.at[peer_slot],
                    send_sem=send_sems.at[k - 1], recv_sem=recv_sems.at[k - 1],
                    device_id=(px, py, pc), device_id_type=pl.DeviceIdType.MESH)))
        for send, _ in copies:
            send.start()
        for _, recv in copies:
            recv.wait_recv()
        for send, _ in copies:
            send.wait_send()
        mine.wait()

    return pl.pallas_call(
        body, name=name,
        out_shape=jax.ShapeDtypeStruct((NDEV, R, C), g.dtype),
        in_specs=[pl.BlockSpec(memory_space=pl.ANY)],
        out_specs=pl.BlockSpec(memory_space=pl.ANY),
        scratch_shapes=[pltpu.SemaphoreType.DMA((7,)), pltpu.SemaphoreType.DMA((7,)),
                        pltpu.SemaphoreType.DMA(())],
    )(g)


def _hbm_call(body, name, ins, out_shapes, n_sems):
    anyspec = pl.BlockSpec(memory_space=pl.ANY)
    return pl.pallas_call(
        body, name=name, out_shape=out_shapes,
        in_specs=[anyspec] * len(ins), out_specs=[anyspec] * len(out_shapes),
        scratch_shapes=[pltpu.SemaphoreType.DMA(s) for s in n_sems],
    )(*ins)


def _all_gather_many(xs, name):
    n = len(xs)

    def body(*refs):
        x_refs, o_refs = refs[:n], refs[n:2 * n]
        send_sems, recv_sems, local_sems = refs[2 * n:]
        xi, yi, ci = lax.axis_index("x"), lax.axis_index("y"), lax.axis_index("c")
        me, sibling = (xi, yi, ci), (xi, yi, 1 - ci)
        chips = [(1 - xi, yi), (xi, 1 - yi), (1 - xi, 1 - yi)]

        def copy(i, k, block, to, src=None):
            dst = o_refs[i].at[_slot(*block)]
            return pltpu.make_async_remote_copy(
                src_ref=dst if src is None else src, dst_ref=dst,
                send_sem=send_sems.at[i, k], recv_sem=recv_sems.at[i, k],
                device_id=to, device_id_type=pl.DeviceIdType.MESH)

        mine = [pltpu.make_async_copy(x_refs[i], o_refs[i].at[_slot(*me)], local_sems.at[i]) for i in range(n)]
        sends = []
        for i in range(n):
            mine[i].start()
            sends += [copy(i, 1 + j, me, (*chip, ci), src=x_refs[i]) for j, chip in enumerate(chips)]
            sends.append(copy(i, 0, me, sibling, src=x_refs[i]))
        for cp in sends:
            cp.start()
        for j, chip in enumerate(chips):
            for i in range(n):
                copy(i, 1 + j, (*chip, ci), me).wait_recv()
                passed = copy(i, 4 + j, (*chip, ci), sibling)
                passed.start()
                sends.append(passed)
        for i in range(n):
            copy(i, 0, sibling, me).wait_recv()
            for j, chip in enumerate(chips):
                copy(i, 4 + j, (*chip, 1 - ci), me).wait_recv()
        for cp in sends:
            cp.wait_send()
        for cp in mine:
            cp.wait()

    return _hbm_call(body, name, xs, [jax.ShapeDtypeStruct((NDEV,) + x.shape, x.dtype) for x in xs],
                     [(n, 7), (n, 7), (n,)])


def _swap_sibling(gs, name):
    n = len(gs)

    def body(*refs):
        g_refs, o_refs = refs[:n], refs[n:2 * n]
        send_sems, recv_sems = refs[2 * n:]
        sibling = (lax.axis_index("x"), lax.axis_index("y"), 1 - lax.axis_index("c"))
        copies = [pltpu.make_async_remote_copy(
            src_ref=g_refs[i], dst_ref=o_refs[i], send_sem=send_sems.at[i], recv_sem=recv_sems.at[i],
            device_id=sibling, device_id_type=pl.DeviceIdType.MESH) for i in range(n)]
        for cp in copies:
            cp.start()
        for cp in copies:
            cp.wait_recv()
        for cp in copies:
            cp.wait_send()

    return _hbm_call(body, name, gs, [jax.ShapeDtypeStruct(g.shape, g.dtype) for g in gs], [(n,), (n,)])


def _exchange_chips(ps, name):
    n = len(ps)

    def body(*refs):
        p_refs, o_refs = refs[:n], refs[n:2 * n]
        send_sems, recv_sems, local_sems = refs[2 * n:]
        xi, yi, ci = lax.axis_index("x"), lax.axis_index("y"), lax.axis_index("c")
        q_me = 2 * xi + yi
        mine = [pltpu.make_async_copy(p_refs[i].at[q_me], o_refs[i].at[q_me], local_sems.at[i]) for i in range(n)]
        for cp in mine:
            cp.start()
        sends, recvs = [], []
        for k in range(1, 4):
            px = 1 - xi if k & 2 else xi
            py = 1 - yi if k & 1 else yi
            q_peer = 2 * px + py
            for i in range(n):
                sends.append(pltpu.make_async_remote_copy(
                    src_ref=p_refs[i].at[q_peer], dst_ref=o_refs[i].at[q_me],
                    send_sem=send_sems.at[i, k - 1], recv_sem=recv_sems.at[i, k - 1],
                    device_id=(px, py, ci), device_id_type=pl.DeviceIdType.MESH))
                recvs.append(pltpu.make_async_remote_copy(
                    src_ref=p_refs[i].at[q_peer], dst_ref=o_refs[i].at[q_peer],
                    send_sem=send_sems.at[i, k - 1], recv_sem=recv_sems.at[i, k - 1],
                    device_id=(px, py, ci), device_id_type=pl.DeviceIdType.MESH))
        for cp in sends:
            cp.start()
        for cp in recvs:
            cp.wait_recv()
        for cp in sends:
            cp.wait_send()
        for cp in mine:
            cp.wait()

    return _hbm_call(body, name, ps, [jax.ShapeDtypeStruct(p.shape, p.dtype) for p in ps],
                     [(n, 3), (n, 3), (n,)])


def _peers(xi, yi, ci):
    return [(1 - xi if k & 4 else xi, 1 - yi if k & 2 else yi, 1 - ci if k & 1 else ci) for k in range(1, 8)]


def _direct_copy(src_refs, land_refs, send_sems, recv_sems, i, k, peer, my_slot, gather):
    src = src_refs[i] if gather else src_refs[i].at[_slot(*peer)]
    return pltpu.make_async_remote_copy(
        src_ref=src, dst_ref=land_refs[i].at[my_slot], send_sem=send_sems.at[7 * i + k], recv_sem=recv_sems.at[7 * i + k],
        device_id=peer, device_id_type=pl.DeviceIdType.MESH)


def _exchange_start(srcs, gather, name):
    n = len(srcs)
    lands = [lax.empty(((NDEV,) + s.shape) if gather else s.shape, s.dtype) for s in srcs]

    def body(*refs):
        s_refs, l_refs = refs[:n], refs[n:2 * n]
        send_sems, recv_sems = refs[2 * n], refs[2 * n + 1]
        token = refs[2 * n + 2 + 2 * n]
        xi, yi, ci = lax.axis_index("x"), lax.axis_index("y"), lax.axis_index("c")
        my_slot = _slot(xi, yi, ci)
        for k, peer in enumerate(_peers(xi, yi, ci)):
            for i in range(n):
                _direct_copy(s_refs, l_refs, send_sems, recv_sems, i, k, peer, my_slot, gather).start()
        token[...] = jnp.zeros_like(token)

    hbm = pl.BlockSpec(memory_space=pltpu.HBM)
    sem = pl.BlockSpec(memory_space=pltpu.SEMAPHORE)
    both = list(srcs) + lands
    return pl.pallas_call(
        body, name=name,
        out_shape=(pltpu.SemaphoreType.DMA((7 * n,)), pltpu.SemaphoreType.DMA((7 * n,)),
                   *[pltpu.HBM(t.shape, t.dtype) for t in both], jax.ShapeDtypeStruct((8, 128), F32)),
        in_specs=[hbm] * (2 * n),
        out_specs=(sem, sem, *[hbm] * (2 * n), pl.BlockSpec(memory_space=pltpu.VMEM)),
        input_output_aliases={i: 2 + i for i in range(2 * n)},
        compiler_params=pltpu.CompilerParams(has_side_effects=pltpu.SideEffectType.DATAFLOW_SIDE_EFFECTING),
    )(*[pltpu.with_memory_space_constraint(t, pltpu.HBM) for t in both])


def _exchange_wait(started, gather, after, name):
    send_sems, recv_sems, *thru, _ = started
    n = len(thru) // 2

    def body(*refs):
        s_refs, l_refs = refs[:n], refs[n:2 * n]
        send_sems, recv_sems, local_sems = refs[2 * n], refs[2 * n + 1], refs[-1]
        xi, yi, ci = lax.axis_index("x"), lax.axis_index("y"), lax.axis_index("c")
        my_slot = _slot(xi, yi, ci)
        mine = [pltpu.make_async_copy(s_refs[i] if gather else s_refs[i].at[my_slot], l_refs[i].at[my_slot],
                                      local_sems.at[i]) for i in range(n)]
        for cp in mine:
            cp.start()
        for cp in mine:
            cp.wait()
        for k, peer in enumerate(_peers(xi, yi, ci)):
            for i in range(n):
                _direct_copy(s_refs, l_refs, send_sems, recv_sems, i, k, peer, my_slot, gather).wait_send()
                _direct_copy(s_refs, l_refs, send_sems, recv_sems, i, k, peer, _slot(*peer), gather).wait_recv()

    hbm = pl.BlockSpec(memory_space=pltpu.HBM)
    sem = pl.BlockSpec(memory_space=pltpu.SEMAPHORE)
    outs = pl.pallas_call(
        body, name=name,
        out_shape=tuple(pltpu.HBM(t.shape, t.dtype) for t in thru),
        in_specs=[hbm] * (2 * n) + [sem, sem, pl.BlockSpec(memory_space=pl.ANY)],
        out_specs=tuple([hbm] * (2 * n)),
        input_output_aliases={i: i for i in range(2 * n)},
        scratch_shapes=[pltpu.SemaphoreType.DMA((n,))],
        compiler_params=pltpu.CompilerParams(has_side_effects=pltpu.SideEffectType.DATAFLOW_SIDE_EFFECTING),
    )(*thru, send_sems, recv_sems, after)
    return outs[n:]


def _mm(name, a, b, tb=False, out=(F32,), epi=None, extras=(), tm=1024, tn=512, tk_cap=2048):
    M, K = a.shape
    N = b.shape[0] if tb else b.shape[1]
    tm, tn = min(tm, M), min(tn, N)
    tk = max(t for t in range(128, min(K, tk_cap) + 1, 128) if K % t == 0)
    assert M % tm == 0 and N % tn == 0 and K % tk == 0, (name, M, N, K)
    nk = K // tk
    ne, no = len(extras), len(out)
    dims = (((1,), (1 if tb else 0,)), ((), ()))

    def kern(*refs):
        a_ref, b_ref = refs[:2]
        e_refs = refs[2:2 + ne]
        o_refs = refs[2 + ne:2 + ne + no]

        def finish(acc):
            outs = epi(acc, *[e[...] for e in e_refs]) if epi is not None else (acc,)
            for o_ref, o in zip(o_refs, outs):
                o_ref[...] = o.astype(o_ref.dtype)

        part = lax.dot_general(a_ref[...], b_ref[...], dims, preferred_element_type=F32)
        if nk == 1:
            finish(part)
            return
        acc_ref = refs[-1]
        k = pl.program_id(2)

        @pl.when(k == 0)
        def _():
            acc_ref[...] = part

        @pl.when(k > 0)
        def _():
            acc_ref[...] += part

        @pl.when(k == nk - 1)
        def _():
            finish(acc_ref[...])

    b_spec = (pl.BlockSpec((tn, tk), lambda i, j, k: (j, k)) if tb
              else pl.BlockSpec((tk, tn), lambda i, j, k: (k, j)))
    tile = pl.BlockSpec((tm, tn), lambda i, j, k: (i, j))
    res = pl.pallas_call(
        kern, name=name, grid=(M // tm, N // tn, nk),
        in_specs=[pl.BlockSpec((tm, tk), lambda i, j, k: (i, k)), b_spec] + [tile] * ne,
        out_specs=[tile] * no,
        out_shape=[jax.ShapeDtypeStruct((M, N), dt) for dt in out],
        scratch_shapes=[pltpu.VMEM((tm, tn), F32)] if nk > 1 else [],
        compiler_params=_cp(("parallel", "parallel", "arbitrary")),
    )(a, b, *extras)
    return res[0] if no == 1 else res


def _rows(name, body, rows, params, out_rows, out_accs=(), tm=256):
    views = [r if isinstance(r, tuple) else (r, r.shape[1], 0) for r in rows]
    n = views[0][0].shape[0]
    assert n % tm == 0
    nr, npar, nor, noa = len(views), len(params), len(out_rows), len(out_accs)

    def kern(*refs):
        r_refs = refs[:nr]
        p_refs = refs[nr:nr + npar]
        o_refs = refs[nr + npar:nr + npar + nor]
        a_refs = refs[nr + npar + nor:]
        outs, accs = body([r[...] for r in r_refs], [p[...] for p in p_refs])
        assert len(outs) == nor and len(accs) == noa, (name, len(outs), len(accs))
        for o_ref, o in zip(o_refs, outs):
            o_ref[...] = o.astype(o_ref.dtype)
        if noa:
            @pl.when(pl.program_id(0) == 0)
            def _():
                for a_ref in a_refs:
                    a_ref[...] = jnp.zeros_like(a_ref)

            for a_ref, a in zip(a_refs, accs):
                a_ref[...] += a.astype(F32)

    def whole(shape):
        nd = len(shape)
        return pl.BlockSpec(tuple(shape), lambda i, nd=nd: (0,) * nd)

    in_specs = [pl.BlockSpec((tm, w), lambda i, cb=cb: (i, cb)) for _, w, cb in views]
    in_specs += [whole(p.shape) for p in params]
    out_specs = [pl.BlockSpec((tm, c), lambda i: (i, 0)) for c, _ in out_rows]
    out_specs += [whole(s) for s in out_accs]
    out_shape = [jax.ShapeDtypeStruct((n, c), dt) for c, dt in out_rows]
    out_shape += [jax.ShapeDtypeStruct(tuple(s), F32) for s in out_accs]
    res = pl.pallas_call(
        kern, name=name, grid=(n // tm,), in_specs=in_specs, out_specs=out_specs,
        out_shape=out_shape, compiler_params=_cp(("arbitrary",)),
    )(*[v[0] for v in views], *params)
    return res[:nor], res[nor:]


def _softplus(z):
    return jnp.maximum(z, 0.0) + jnp.log(1.0 + jnp.exp(jnp.minimum(z, -z)))


def _post_ln(x, y, g, lng, lnb):
    z = ALPHA * x + (1.0 + g) * y
    mu = jnp.mean(z, axis=-1, keepdims=True)
    zc = z - mu
    var = jnp.mean(zc * zc, axis=-1, keepdims=True)
    return zc * lax.rsqrt(var + LN_EPS) * lng + lnb


def _post_ln_mod(x, y, g, lng, lnb, scn, shn):
    xn = _post_ln(x, y, g, lng, lnb)
    return xn, xn * (1.0 + scn) + shn


def _pre_core(E, r_, k_, v_, wd_, ad_, gd_, r1, k1, v1, wd1, ad1, gd1, h, bg, cg, h1, cg1, h2, cg2,
              mu_r, mu_k, mu_v, mu_wd, mu_ad, mu_gd, w0, w_up, a0, a_up, g_up, k_k, k_a,
              cw0, cw1, cw2):
    def mix(x, x1, mu):
        return x + mu * (x1 - x)

    r, k, v = mix(r_, r1, mu_r), mix(k_, k1, mu_k), mix(v_, v1, mu_v)
    wd, ad, gd = mix(wd_, wd1, mu_wd), mix(ad_, ad1, mu_ad), mix(gd_, gd1, mu_gd)
    logw = -_softplus(-(w0 + jnp.dot(jnp.tanh(wd), w_up, preferred_element_type=F32))) - 0.5
    decay = jnp.exp(-jnp.exp(logw))
    iclr = jax.nn.sigmoid(a0 + jnp.dot(ad, a_up, preferred_element_type=F32))
    gate = jnp.dot(jax.nn.sigmoid(gd), g_up, preferred_element_type=F32)
    kk0 = k * k_k
    nrm = jnp.sqrt(jnp.dot(kk0 * kk0, E, precision=HI, preferred_element_type=F32))
    kk = kk0 / jnp.maximum(nrm, 1e-12)
    kh = k * (1.0 + (iclr - 1.0) * k_a)
    yb = bg * (cw2 * (cg * h) + cw1 * (cg1 * h1) + cw0 * (cg2 * h2))
    return r, decay, kh, v, -kk, kk * iclr, gate, yb


def _post_core(E, y, r, kh, v, gate, lnx_g, lnx_b, rk):
    def seg(t):
        return jnp.dot(t, E, precision=HI, preferred_element_type=F32)

    mean = seg(y) * (1.0 / HD)
    yc = y - mean
    var = seg(yc * yc) * (1.0 / HD)
    gn = yc * lax.rsqrt(var + GN_EPS) * lnx_g + lnx_b
    bonus = seg(r * kh * rk) * v
    return (gn + bonus) * gate


def _merge_core(o0, o1, o2, l0, l1, l2):
    m = jnp.maximum(jnp.maximum(l0, l1), l2)
    e0, e1, e2 = jnp.exp(l0 - m), jnp.exp(l1 - m), jnp.exp(l2 - m)
    den = e0 + e1 + e2
    return (e0 * o0 + e1 * o1 + e2 * o2) / den


CHUNK = 128
HALF = 64
HP = HEADS // 2
LW = 2 * HD
NCHUNK = T // CHUNK


def _split3(x):
    hi = x.astype(BF16)
    r1 = x - hi.astype(F32)
    mid = r1.astype(BF16)
    return hi, mid, (r1 - mid.astype(F32)).astype(BF16)


def _pairs(x):
    return x.reshape(T, HP, LW).transpose(1, 0, 2)


def _unpairs(x):
    return x.transpose(1, 0, 2).reshape(T, RW)


def _cols3(parts):
    tr = [p.reshape(NCHUNK, CHUNK, HP, 2, HD).transpose(2, 4, 0, 3, 1) for p in parts]
    return jnp.stack(tr, axis=4).reshape(HP, HD, 6 * T)


def _pick_codes():
    row = lax.broadcasted_iota(jnp.int32, (6 * CHUNK, LW), 0)
    col = lax.broadcasted_iota(jnp.int32, (6 * CHUNK, LW), 1)
    same = (row >= 3 * CHUNK) == (col >= HD)
    return jnp.where(same, row & (CHUNK - 1), -1).astype(BF16)


def _column(block_ref, codes, t):
    pick = jnp.where(codes == t.astype(BF16), jnp.ones((), BF16), jnp.zeros((), BF16))
    return jnp.dot(block_ref[...].reshape(HP * HD, 6 * CHUNK), pick, preferred_element_type=F32)


def _halfsums(x, row, left1):
    row_l = jnp.where(left1, row, 0.0)
    return (jnp.sum(x * row_l, axis=1, keepdims=True), jnp.sum(x * (row - row_l), axis=1, keepdims=True))


def _store_columns(ref, p, t_mask, cols):
    for j, col in enumerate(cols):
        pltpu.store(ref.at[pl.ds(2 * p + j, 1)], jnp.broadcast_to(col[None], (1, HD, CHUNK)), mask=t_mask[None])


def _scan_fwd(r, w, k, a, b, v3):
    def kern(r_ref, w_ref, k_ref, a_ref, b_ref, v_ref, y_ref, ck_ref, s_ref, vb_ref):
        @pl.when(pl.program_id(0) == 0)
        def _():
            s_ref[...] = jnp.zeros_like(s_ref)

        lane = lax.broadcasted_iota(jnp.int32, (HD, CHUNK), 1)
        left = lane < HD
        left1 = lax.broadcasted_iota(jnp.int32, (1, LW), 1) < HD
        codes = _pick_codes()

        def step(t, carry):
            row = lambda ref: [ref[p, pl.ds(t, 1), :] for p in range(HP)]
            S = [s_ref[p] for p in range(HP)]
            sa = [jnp.where(left, *_halfsums(s, a, left1)) for s, a in zip(S, row(a_ref))]
            S = [s * w + c * b + vb_ref[pl.ds(p * HD, HD), :] * k
                 for p, (s, w, c, b, k) in enumerate(zip(S, row(w_ref), sa, row(b_ref), row(k_ref)))]
            for p, s in enumerate(S):
                s_ref[p] = s
            for p, (s, r) in enumerate(zip(S, row(r_ref))):
                _store_columns(y_ref, p, lane == t, _halfsums(s, r, left1))
            vb_ref[...] = _column(v_ref, codes, t + 1)
            return carry

        for half in range(CHUNK // HALF):
            ck_ref[half] = s_ref[...]
            if half == 0:
                vb_ref[...] = _column(v_ref, codes, jnp.int32(0))
            lax.fori_loop(half * HALF, (half + 1) * HALF, step, 0, unroll=2)

    rowblk = pl.BlockSpec((HP, CHUNK, LW), lambda c: (0, c, 0))
    return pl.pallas_call(
        kern, name="rwkv_scan_fwd", grid=(NCHUNK,),
        in_specs=[rowblk] * 5 + [pl.BlockSpec((HP, HD, 6 * CHUNK), lambda c: (0, 0, c))],
        out_specs=[pl.BlockSpec((HEADS, HD, CHUNK), lambda c: (0, 0, c)),
                   pl.BlockSpec((CHUNK // HALF, HP, HD, LW), lambda c: (c, 0, 0, 0))],
        out_shape=[jax.ShapeDtypeStruct((HEADS, HD, T), F32),
                   jax.ShapeDtypeStruct((T // HALF, HP, HD, LW), F32)],
        scratch_shapes=[pltpu.VMEM((HP, HD, LW), F32), pltpu.VMEM((HP * HD, LW), F32)],
        compiler_params=_cp(("arbitrary",)),
    )(r, w, k, a, b, v3)


def _scan_bwd(r, w, k, a, b, v3, dy3, ck):
    NC = T // CHUNK

    def kern(r_ref, w_ref, k_ref, a_ref, b_ref, v_ref, dy_ref, ck_ref,
             dr_ref, dw_ref, dk_ref, da_ref, db_ref, dv_ref, ds_ref, sb_ref, vb_ref, sa_ref, dyb_ref):
        @pl.when(pl.program_id(0) == 0)
        def _():
            ds_ref[...] = jnp.zeros_like(ds_ref)

        lane = lax.broadcasted_iota(jnp.int32, (HD, CHUNK), 1)
        left = lane < HD
        left1 = lax.broadcasted_iota(jnp.int32, (1, LW), 1) < HD
        codes = _pick_codes()

        def rowsum(x):
            return jnp.sum(x, axis=0, keepdims=True)

        for half in reversed(range(CHUNK // HALF)):
            base = half * HALF
            sb_ref[0] = ck_ref[half]

            vb_ref[0] = _column(v_ref, codes, jnp.int32(base))

            def replay(i, carry):
                t = base + i
                row = lambda ref: [ref[p, pl.ds(t, 1), :] for p in range(HP)]
                S = [sb_ref[i, p] for p in range(HP)]
                sa = [jnp.where(left, *_halfsums(s, a, left1)) for s, a in zip(S, row(a_ref))]
                for p, (s, w, c, b, k) in enumerate(zip(S, row(w_ref), sa, row(b_ref), row(k_ref))):
                    sb_ref[i + 1, p] = s * w + c * b + vb_ref[i, pl.ds(p * HD, HD), :] * k
                    sa_ref[i, p] = c
                vb_ref[i + 1] = _column(v_ref, codes, t + 1)
                return carry

            lax.fori_loop(0, HALF, replay, 0, unroll=2)
            dyb_ref[...] = _column(dy_ref, codes, jnp.int32(base + HALF - 1))

            def back(ii, carry):
                i = HALF - 1 - ii
                t = base + i
                row = lambda ref: [ref[p, pl.ds(t, 1), :] for p in range(HP)]
                a_r, b_r, k_r, w_r, r_r = row(a_ref), row(b_ref), row(k_ref), row(w_ref), row(r_ref)
                dys = [dyb_ref[pl.ds(p * HD, HD), :] for p in range(HP)]
                dyb_ref[...] = _column(dy_ref, codes, jnp.maximum(t - 1, 0))
                for p in range(HP):
                    Sp, dy = sb_ref[i, p], dys[p]
                    dS = ds_ref[p] + dy * r_r[p]
                    dr_ref[p, pl.ds(t, 1), :] = rowsum(sb_ref[i + 1, p] * dy)
                    dw_ref[p, pl.ds(t, 1), :] = rowsum(dS * Sp)
                    db_ref[p, pl.ds(t, 1), :] = rowsum(dS * sa_ref[i, p])
                    dk_ref[p, pl.ds(t, 1), :] = rowsum(dS * vb_ref[i, pl.ds(p * HD, HD), :])
                    dsa = jnp.where(left, *_halfsums(dS, b_r[p], left1))
                    _store_columns(dv_ref, p, lane == t, _halfsums(dS, k_r[p], left1))
                    da_ref[p, pl.ds(t, 1), :] = rowsum(Sp * dsa)
                    ds_ref[p] = dS * w_r[p] + dsa * a_r[p]
                return carry

            lax.fori_loop(0, HALF, back, 0, unroll=2)

    rowblk = pl.BlockSpec((HP, CHUNK, LW), lambda c: (0, NC - 1 - c, 0))
    col3blk = pl.BlockSpec((HP, HD, 6 * CHUNK), lambda c: (0, 0, NC - 1 - c))
    rowshape = jax.ShapeDtypeStruct((HP, T, LW), F32)
    return pl.pallas_call(
        kern, name="rwkv_scan_bwd", grid=(NC,),
        in_specs=[rowblk] * 5 + [col3blk, col3blk,
                                 pl.BlockSpec((CHUNK // HALF, HP, HD, LW), lambda c: (NC - 1 - c, 0, 0, 0))],
        out_specs=[rowblk] * 5 + [pl.BlockSpec((HEADS, HD, CHUNK), lambda c: (0, 0, NC - 1 - c))],
        out_shape=[rowshape] * 5 + [jax.ShapeDtypeStruct((HEADS, HD, T), F32)],
        scratch_shapes=[pltpu.VMEM((HP, HD, LW), F32), pltpu.VMEM((HALF + 1, HP, HD, LW), F32),
                        pltpu.VMEM((HALF + 1, HP * HD, LW), F32), pltpu.VMEM((HALF, HP, HD, LW), F32),
                        pltpu.VMEM((HP * HD, LW), F32)],
        compiler_params=_cp(("arbitrary",)),
    )(r, w, k, a, b, v3, dy3, ck)


NBLK = T // BLK


def _blocks_per_segment(g):
    return jnp.where(g == 0, NBLK // DILS[0], jnp.where(g == 1, NBLK // DILS[1], NBLK // DILS[2]))


def _attn_fwd(q, kp, vp, bias):
    def kern(q_ref, k_ref, v_ref, b_ref, o_ref, l_ref):
        nbs = _blocks_per_segment(pl.program_id(0))
        qi = lax.broadcasted_iota(jnp.int32, (BLK, 2 * BLK), 0)
        ki = lax.broadcasted_iota(jnp.int32, (BLK, 2 * BLK), 1)
        band = (ki >= qi) & (ki <= qi + BLK)
        bias_t = b_ref[0, 0]
        for n in range(NBLK):
            lo = jnp.where((n & (nbs - 1)) == 0, BLK, 0)
            valid = band & (ki >= lo)
            qb = q_ref[0, 0, n * BLK:(n + 1) * BLK, :]
            kc = k_ref[0, 0, n * BLK:(n + 2) * BLK, :]
            vc = v_ref[0, 0, n * BLK:(n + 2) * BLK, :]
            s = lax.dot_general(qb, kc, (((1,), (1,)), ((), ())), preferred_element_type=F32)
            s = jnp.where(valid, s * (HD ** -0.5) + bias_t, -jnp.inf)
            m = jnp.max(s, axis=1, keepdims=True)
            e = jnp.exp(s - m)
            den = jnp.sum(e, axis=1, keepdims=True)
            pr = (e / den).astype(BF16)
            o_ref[0, 0, n * BLK:(n + 1) * BLK, :] = jnp.dot(pr, vc, preferred_element_type=F32)
            l_ref[0, 0, n * BLK:(n + 1) * BLK, :] = m + jnp.log(den)

    def blk(rows, cols):
        return pl.BlockSpec((1, 1, rows, cols), lambda g, h: (g, h, 0, 0))

    return pl.pallas_call(
        kern, name="attn_fwd", grid=(3, HEADS),
        in_specs=[blk(T, HD), blk(T + BLK, HD), blk(T + BLK, HD), blk(BLK, 2 * BLK)],
        out_specs=[blk(T, HD), blk(T, 1)],
        out_shape=[jax.ShapeDtypeStruct((3, HEADS, T, HD), F32),
                   jax.ShapeDtypeStruct((3, HEADS, T, 1), F32)],
        compiler_params=_cp(("parallel", "parallel")),
    )(q, kp, vp, bias)


def _attn_bwd(q, kp, vp, bias, biasT, do, lse_c, lse_r, dc_c, dc_r):
    def kern(q_ref, k_ref, v_ref, b_ref, bt_ref, do_ref, lc_ref, lr_ref, dcc_ref, dcr_ref,
             dq_ref, dk_ref, dv_ref, db_ref):
        nbs = _blocks_per_segment(pl.program_id(0))
        qi = lax.broadcasted_iota(jnp.int32, (BLK, 2 * BLK), 0)
        ki = lax.broadcasted_iota(jnp.int32, (BLK, 2 * BLK), 1)
        band = (ki >= qi) & (ki <= qi + BLK)
        kiT = lax.broadcasted_iota(jnp.int32, (2 * BLK, BLK), 0)
        qiT = lax.broadcasted_iota(jnp.int32, (2 * BLK, BLK), 1)
        bandT = (kiT >= qiT) & (kiT <= qiT + BLK)
        bias_t, biasT_t = b_ref[0, 0], bt_ref[0, 0]
        scale = HD ** -0.5
        dk_ref[...] = jnp.zeros_like(dk_ref)
        dv_ref[...] = jnp.zeros_like(dv_ref)
        db_ref[...] = jnp.zeros_like(db_ref)
        nt = (((1,), (1,)), ((), ()))
        for n in range(NBLK):
            lo = jnp.where((n & (nbs - 1)) == 0, BLK, 0)
            qs, ks = slice(n * BLK, (n + 1) * BLK), slice(n * BLK, (n + 2) * BLK)
            qb, kc, vc, dob = q_ref[0, 0, qs, :], k_ref[0, 0, ks, :], v_ref[0, 0, ks, :], do_ref[0, 0, qs, :]
            s = lax.dot_general(qb, kc, nt, preferred_element_type=F32) * scale + bias_t
            p = jnp.where(band & (ki >= lo), jnp.exp(s - lc_ref[0, 0, qs, :]), 0.0)
            dp = lax.dot_general(dob, vc, nt, preferred_element_type=F32)
            ds = p * (dp + dcc_ref[0, 0, qs, :])
            db_ref[0, 0] += ds
            dq_ref[0, 0, qs, :] = jnp.dot((ds * scale).astype(BF16), kc, preferred_element_type=F32)
            sT = lax.dot_general(kc, qb, nt, preferred_element_type=F32) * scale + biasT_t
            pT = jnp.where(bandT & (kiT >= lo), jnp.exp(sT - lr_ref[0, 0, :, qs]), 0.0)
            dpT = lax.dot_general(vc, dob, nt, preferred_element_type=F32)
            dsT = pT * (dpT + dcr_ref[0, 0, :, qs])
            dk_ref[0, 0, ks, :] += jnp.dot((dsT * scale).astype(BF16), qb, preferred_element_type=F32)
            dv_ref[0, 0, ks, :] += jnp.dot(pT.astype(BF16), dob, preferred_element_type=F32)

    def blk(rows, cols):
        return pl.BlockSpec((1, 1, rows, cols), lambda g, h: (g, h, 0, 0))

    return pl.pallas_call(
        kern, name="attn_bwd", grid=(3, HEADS),
        in_specs=[blk(T, HD), blk(T + BLK, HD), blk(T + BLK, HD), blk(BLK, 2 * BLK), blk(2 * BLK, BLK),
                  blk(T, HD), blk(T, 1), blk(1, T), blk(T, 1), blk(1, T)],
        out_specs=[blk(T, HD), blk(T + BLK, HD), blk(T + BLK, HD), blk(BLK, 2 * BLK)],
        out_shape=[jax.ShapeDtypeStruct((3, HEADS, T, HD), F32),
                   jax.ShapeDtypeStruct((3, HEADS, T + BLK, HD), F32),
                   jax.ShapeDtypeStruct((3, HEADS, T + BLK, HD), F32),
                   jax.ShapeDtypeStruct((3, HEADS, BLK, 2 * BLK), F32)],
        compiler_params=_cp(("parallel", "parallel")),
    )(q, kp, vp, bias, biasT, do, lse_c, lse_r, dc_c, dc_r)


NBUCKET = 32
NPAIR = BLK * 2 * BLK


def _relbias_table(rbT, onehotT):
    def kern(rb_ref, oh_ref, out_ref):
        out_ref[0] = sum(jnp.dot(p, oh_ref[0], preferred_element_type=F32) for p in _split3(rb_ref[0]))

    return pl.pallas_call(
        kern, name="relbias_table", grid=(3,),
        in_specs=[pl.BlockSpec((1, HEADS, NBUCKET), lambda g: (g, 0, 0)),
                  pl.BlockSpec((1, NBUCKET, NPAIR), lambda g: (g, 0, 0))],
        out_specs=pl.BlockSpec((1, HEADS, NPAIR), lambda g: (g, 0, 0)),
        out_shape=jax.ShapeDtypeStruct((3, HEADS, NPAIR), F32),
        compiler_params=_cp(("parallel",)),
    )(rbT, onehotT)


def _relbias_grad(db, onehotT):
    nt = (((1,), (1,)), ((), ()))

    def kern(db_ref, oh_ref, out_ref):
        hi, mid, _ = _split3(db_ref[0])
        out_ref[0] = (lax.dot_general(hi, oh_ref[0], nt, preferred_element_type=F32)
                      + lax.dot_general(mid, oh_ref[0], nt, preferred_element_type=F32))

    return pl.pallas_call(
        kern, name="relbias_grad", grid=(3,),
        in_specs=[pl.BlockSpec((1, HEADS, NPAIR), lambda g: (g, 0, 0)),
                  pl.BlockSpec((1, NBUCKET, NPAIR), lambda g: (g, 0, 0))],
        out_specs=pl.BlockSpec((1, HEADS, NBUCKET), lambda g: (g, 0, 0)),
        out_shape=jax.ShapeDtypeStruct((3, HEADS, NBUCKET), F32),
        compiler_params=_cp(("parallel",)),
    )(db, onehotT)


def _adamw(w, g, m, v):
    m2 = ADAM_B1 * m + (1.0 - ADAM_B1) * g
    v2 = ADAM_B2 * v + (1.0 - ADAM_B2) * (g * g)
    m_hat = m2 / (1.0 - ADAM_B1 ** ADAM_STEP)
    v_hat = v2 / (1.0 - ADAM_B2 ** ADAM_STEP)
    return -ADAM_LR * (m_hat / (jnp.sqrt(v_hat) + ADAM_EPS) + ADAM_WD * w), m2, v2


def _ada_mod(c_all, ada_w, ada_b_loc):
    def kern(c_ref, w_ref, b_ref, o_ref):
        c = c_ref[...]
        cond = c * jax.nn.sigmoid(c)
        o_ref[0] = jnp.dot(cond, w_ref[0], precision=HI, preferred_element_type=F32) + b_ref[0]

    ncol = ada_w.shape[2]
    return pl.pallas_call(
        kern, name="ada_mod", grid=(2,),
        in_specs=[pl.BlockSpec((NDEV, D), lambda i: (0, 0)),
                  pl.BlockSpec((1, D, ncol), lambda i: (i, 0, 0)),
                  pl.BlockSpec((1, 1, ncol), lambda i: (i, 0, 0))],
        out_specs=pl.BlockSpec((1, NDEV, ncol), lambda i: (i, 0, 0)),
        out_shape=jax.ShapeDtypeStruct((2, NDEV, ncol), F32),
        compiler_params=_cp(("parallel",)),
    )(c_all, ada_w, ada_b_loc.reshape(2, 1, ncol))


def _ada_grad_adamw(cT_all, dmod_loc, w, m, v):
    ncol = w.shape[2]
    tr = 256

    def kern(c_ref, d_ref, w_ref, m_ref, v_ref, g_ref, dl_ref, m2_ref, v2_ref):
        c = c_ref[...]
        cond = c * jax.nn.sigmoid(c)
        g = jnp.dot(cond, d_ref[0], precision=HI, preferred_element_type=F32)
        dl, m2, v2 = _adamw(w_ref[0], g, m_ref[0], v_ref[0])
        g_ref[0], dl_ref[0], m2_ref[0], v2_ref[0] = g, dl, m2, v2

    big = pl.BlockSpec((1, tr, ncol), lambda i, j: (i, j, 0))
    shp = jax.ShapeDtypeStruct(w.shape, F32)
    return pl.pallas_call(
        kern, name="ada_grad_adamw", grid=(2, D // tr),
        in_specs=[pl.BlockSpec((tr, NDEV), lambda i, j: (j, 0)),
                  pl.BlockSpec((1, NDEV, ncol), lambda i, j: (i, 0, 0)), big, big, big],
        out_specs=[big] * 4, out_shape=[shp] * 4,
        compiler_params=_cp(("parallel", "parallel")),
    )(cT_all, dmod_loc, w, m, v)


def _sum_adamw(recv, w, m, v, name, tr):
    S = recv.shape[0]
    R, C = w.shape
    assert R % tr == 0 and recv.shape[1:] == (R, C)

    def kern(r_ref, w_ref, m_ref, v_ref, g_ref, dl_ref, m2_ref, v2_ref):
        g = r_ref[0].astype(F32)
        for s in range(1, S):
            g = g + r_ref[s].astype(F32)
        dl, m2, v2 = _adamw(w_ref[...], g, m_ref[...], v_ref[...])
        g_ref[...], dl_ref[...], m2_ref[...], v2_ref[...] = g, dl, m2, v2

    flat = pl.BlockSpec((tr, C), lambda i: (i, 0))
    shp = jax.ShapeDtypeStruct((R, C), F32)
    return pl.pallas_call(
        kern, name=name, grid=(R // tr,),
        in_specs=[pl.BlockSpec((S, tr, C), lambda i: (0, i, 0)), flat, flat, flat],
        out_specs=[flat] * 4, out_shape=[shp] * 4,
        compiler_params=_cp(("parallel",)),
    )(recv, w, m, v)


def _pack(arrs, dtype, row_mult):
    flat = jnp.concatenate([a.reshape(-1).astype(dtype) for a in arrs])
    flat = jnp.pad(flat, (0, -flat.shape[0] % (128 * row_mult)))
    return flat.reshape(-1, 128)


def _pack8(arrs, dtype, row_mult):
    flat = jnp.concatenate([a.reshape(NDEV, -1).astype(dtype) for a in arrs], axis=1)
    flat = jnp.pad(flat, ((0, 0), (0, -flat.shape[1] % (128 * row_mult))))
    return flat.reshape(NDEV, -1, 128)


def _unpack(buf, shapes, lead=()):
    flat = buf.reshape(lead + (-1,))
    out, off = [], 0
    for s in shapes:
        n = math.prod(s)
        out.append(flat[..., off:off + n].reshape(lead + tuple(s)))
        off += n
    return out


def _to_chunks(full, kind):
    if kind == "col":
        x = full.reshape(full.shape[:-1] + (NDEV, full.shape[-1] // NDEV))
        return jnp.moveaxis(x, -2, 0)
    x = full.reshape(full.shape[:-2] + (NDEV, full.shape[-2] // NDEV, full.shape[-1]))
    return jnp.moveaxis(x, -3, 0)


def _from_chunks(g8, kind):
    if kind == "col":
        x = jnp.moveaxis(g8, 0, -2)
        return x.reshape(x.shape[:-2] + (x.shape[-2] * x.shape[-1],))
    x = jnp.moveaxis(g8, 0, -3)
    return x.reshape(x.shape[:-3] + (x.shape[-3] * x.shape[-2], x.shape[-1]))


def _pad_pa(x):
    z = lambda n: jnp.zeros(x.shape[:-1] + (n,), x.dtype)
    return jnp.concatenate([x[..., :1600], z(64), x[..., 1600:1664], z(64), x[..., 1664:1824], z(96)], -1)


def _unpad_pa(x):
    return jnp.concatenate([x[..., :1600], x[..., 1664:1728], x[..., 1792:1952]], -1)


def _pad_rows(x, n):
    return jnp.pad(x, ((0, n - x.shape[0]), (0, 0)))


def _shift_down(x, n):
    return jnp.pad(x, ((n, 0), (0, 0)))[:-n]


def _shift_up(x, n):
    return jnp.pad(x, ((0, n), (0, 0)))[n:]


def _unheadsT(x):
    return x.transpose(2, 0, 1).reshape(T, RW)


def _perm(x, dil):
    C = x.shape[-1]
    return x.reshape(T // dil, dil, HEADS, C).transpose(2, 1, 0, 3).reshape(HEADS, T, C)


def _unperm(y, dil):
    C = y.shape[-1]
    return y.reshape(HEADS, dil, T // dil, C).transpose(2, 1, 0, 3).reshape(T, HEADS, C)


def _bucket_tables():
    qi = jnp.arange(BLK)[:, None]
    ki = jnp.arange(2 * BLK)[None, :]
    rel = BLK + qi - ki
    tabs = []
    for dil in DILS:
        dist = jnp.clip(rel, 0, BLK) * dil
        logd = jnp.log(jnp.maximum(dist, 1).astype(F32) / 16) / math.log(2048 / 16)
        large = jnp.minimum(16 + (logd * 16).astype(jnp.int32), 31)
        tabs.append(jnp.where(dist < 16, dist, large))
    return jnp.stack(tabs)


SHARDED = (("ln_g", "col"), ("ln_b", "col"), ("ab_w_in", "col"), ("rw_w_up", "col"), ("rw_a_up", "col"),
           ("rw_g_up", "col"), ("sc_conv_w", "col"), ("ab_w_out", "row"), ("dil_w_qkv", "col"),
           ("dil_w_out", "col"), ("mlp_w1", "col"), ("mlp_w2", "row"))
FIRST = ("ab_w_in", "ab_w_out")
LATER = ("dil_w_qkv", "dil_w_out", "mlp_w1", "mlp_w2")
GATHER_BF16 = FIRST + LATER
GATHER_F32 = ("rw_w_up", "rw_a_up", "rw_g_up", "sc_conv_w", "ln_g", "ln_b")
REPLICATED = ("ada_b", "rw_mu", "rw_w0", "rw_a0", "rw_k_k", "rw_k_a", "rw_r_k", "rw_lnx_g", "rw_lnx_b", "rel_bias")
WEIGHTS = ("ada_w", "ada_b", "ln_g", "ln_b", "ab_w_in", "rw_mu", "rw_w0", "rw_w_up", "rw_a0", "rw_a_up",
           "rw_g_up", "rw_k_k", "rw_k_a", "rw_r_k", "rw_lnx_g", "rw_lnx_b", "sc_conv_w", "ab_w_out",
           "dil_w_qkv", "dil_w_out", "rel_bias", "mlp_w1", "mlp_w2")
FLAT_TILE = 512


def _local_step(x0, tgt, mod, W, P, later_weights, early_grads):
    row = lambda a: a.reshape(1, -1)
    W = dict(W)
    m6 = mod.reshape(2, 6, 1, D)
    sc = [m6[0, 1], m6[0, 4], m6[1, 1], m6[1, 4]]
    sh = [m6[0, 0], m6[0, 3], m6[1, 0], m6[1, 3]]
    gt = [m6[0, 2], m6[0, 5], m6[1, 2], m6[1, 5]]
    lng = [row(P["ln_g"][0, 0]), row(P["ln_g"][0, 1]), row(P["ln_g"][1, 0]), row(P["ln_g"][1, 1])]
    lnb = [row(P["ln_b"][0, 0]), row(P["ln_b"][0, 1]), row(P["ln_b"][1, 0]), row(P["ln_b"][1, 1])]
    E = jnp.kron(jnp.eye(HEADS, dtype=F32), jnp.ones((HD, HD), F32))

    def mod_body(r, p):
        return [r[0] * (1.0 + p[0]) + p[1]], []

    (u0,), _ = _rows("modulate", mod_body, [x0], [sc[0], sh[0]], [(D, BF16)])

    def post_fwd_body(r, p):
        xn, un = _post_ln_mod(r[0], r[1], *p)
        return [xn, un], []

    def post_fwd(s, x, y):
        (xn, un), _ = _rows(f"post_ln_{s}", post_fwd_body, [x, y],
                            [gt[s], lng[s], lnb[s], sc[s + 1], sh[s + 1]], [(D, F32), (D, BF16)])
        return xn, un

    def relu2(acc):
        a = jnp.maximum(acc, 0.0)
        return acc, a * a

    def relu2_bwd(acc, h):
        return (acc * (2.0 * jnp.maximum(h, 0.0)),)

    p = _mm("ab_in", u0, W["ab_w_in"])
    p1 = _shift_down(p, 1)
    p2 = _shift_down(p[:, PA:], 2)
    mu = _pad_pa(P["rw_mu"])
    mu_parts = [mu[:, :512], mu[:, 512:1024], mu[:, 1024:1536], mu[:, 1536:1664], mu[:, 1664:1792], mu[:, 1792:]]
    pre_params = mu_parts + [P["rw_w0"], _pad_rows(P["rw_w_up"], 128), P["rw_a0"], _pad_rows(P["rw_a_up"], 128),
                             _pad_rows(P["rw_g_up"], 256), P["rw_k_k"], P["rw_k_a"],
                             P["sc_conv_w"][0:1], P["sc_conv_w"][1:2], P["sc_conv_w"][2:3]]
    pre_rows = [(p, 512, 0), (p, 512, 1), (p, 512, 2), (p, 128, 12), (p, 128, 13), (p, 256, 7),
                (p1, 512, 0), (p1, 512, 1), (p1, 512, 2), (p1, 128, 12), (p1, 128, 13), (p1, 256, 7),
                (p, 512, 4), (p, 512, 5), (p, 512, 6), (p1, 512, 4), (p1, 512, 6), (p2, 512, 0), (p2, 512, 2)]
    NPR = len(pre_rows)

    def pre_fwd_body(r, pp):
        outs = list(_pre_core(pp[0], *r, *pp[1:]))
        return outs + list(_split3(outs[3])), []

    (r_, w_, kh_, v_, a_, b_, gate_, yb, *v_parts), _ = _rows(
        "rwkv_pre", pre_fwd_body, pre_rows, [E] + pre_params,
        [(RW, F32)] * 7 + [(RW, BF16)] * 4, tm=128)
    scan_in = [_pairs(t) for t in (r_, w_, kh_, a_, b_)] + [_cols3(v_parts)]
    yT, ck = _scan_fwd(*scan_in)
    ysc = _unheadsT(yT)
    post_params = [P["rw_lnx_g"], P["rw_lnx_b"], P["rw_r_k"].reshape(1, RW)]

    def postmix_fwd_body(r, pp):
        return [_post_core(pp[0], *r, *pp[1:])], []

    (ya,), _ = _rows("rwkv_post", postmix_fwd_body, [ysc, r_, kh_, v_, gate_], [E] + post_params,
                     [(RW, BF16)], tm=128)
    cat = jnp.concatenate([ya, yb], axis=1)
    y0 = _mm("ab_out", cat, W["ab_w_out"])
    x1, u1 = post_fwd(0, x0, y0)
    W.update(later_weights(u1))

    h1, a1 = _mm("mlp1_up_0", u1, W["mlp_w1"][0], out=(F32, BF16), epi=relu2)
    y1 = _mm("mlp1_down_0", a1, W["mlp_w2"][0])
    x2, u2 = post_fwd(1, x1, y1)

    pq = _mm("qkv", u2, W["dil_w_qkv"], out=(BF16,))
    pq5 = pq.reshape(T, 3, 3, HEADS, HD)
    q = jnp.stack([_perm(pq5[:, g, 0], DILS[g]) for g in range(3)])
    kp = jnp.pad(jnp.stack([_perm(pq5[:, g, 1], DILS[g]) for g in range(3)]), ((0, 0), (0, 0), (BLK, 0), (0, 0)))
    vp = jnp.pad(jnp.stack([_perm(pq5[:, g, 2], DILS[g]) for g in range(3)]), ((0, 0), (0, 0), (BLK, 0), (0, 0)))
    onehotT = (_bucket_tables().reshape(3, 1, NPAIR) == jnp.arange(NBUCKET).reshape(1, NBUCKET, 1)).astype(BF16)
    rbT = P["rel_bias"].reshape(NBUCKET, 3, HEADS).transpose(1, 2, 0)
    bias = _relbias_table(rbT, onehotT).reshape(3, HEADS, BLK, 2 * BLK)
    og, lse = _attn_fwd(q, kp, vp, bias)
    R = T * HEADS
    o_nat = [_unperm(og[g], DILS[g]).reshape(R, HD) for g in range(3)]
    l_nat = [_unperm(lse[g], DILS[g]).reshape(R, 1) for g in range(3)]

    def merge_fwd_body(r, pp):
        return [_merge_core(*r)], []

    (om,), _ = _rows("attn_merge", merge_fwd_body, o_nat + l_nat, [], [(HD, BF16)], tm=1024)
    om = om.reshape(T, RW)
    y2 = _mm("dil_out", om, W["dil_w_out"])
    x3, u3 = post_fwd(2, x2, y2)

    h3, a3 = _mm("mlp1_up_1", u3, W["mlp_w1"][1], out=(F32, BF16), epi=relu2)
    y3 = _mm("mlp1_down_1", a3, W["mlp_w2"][1])

    def last_body(r, pp):
        x, y, tg = r
        xn, vjp = jax.vjp(_post_ln, x, y, *pp)
        err = xn - tg
        dx, dy, dg, dlg, dlb = vjp(err * (1.0 / D))
        loss = jnp.full((1, 128), (0.5 / D) * jnp.sum(err * err), F32)
        return [dx, dy], [loss, dg, dlg, dlb]

    (dxp, dy3), (loss_acc, dg3, dlng3, dlnb3) = _rows(
        "final_ln_loss", last_body, [x3, y3, tgt], [gt[3], lng[3], lnb[3]],
        [(D, F32), (D, BF16)], [(1, 128), (1, D), (1, D), (1, D)])

    G = {}
    dsc, dsh, dgt = [None] * 4, [None] * 4, [None] * 4
    dlng, dlnb = [None] * 4, [None] * 4
    dgt[3], dlng[3], dlnb[3] = dg3, dlng3, dlnb3

    def mlp_bwd(i, u, h, a, dy):
        dh = _mm(f"mlp_dh_{i}", dy, W["mlp_w2"][i], tb=True, out=(BF16,), epi=relu2_bwd, extras=(h,))
        gw2 = _mm(f"mlp_dw2_{i}", a.T, dy)
        du = _mm(f"mlp_du_{i}", dh, W["mlp_w1"][i], tb=True)
        gw1 = _mm(f"mlp_dw1_{i}", u.T, dh)
        return du, gw1, gw2

    def post_bwd_body(r, pp):
        x, y, dxn, dun = r
        _, vjp = jax.vjp(_post_ln_mod, x, y, *pp)
        dx, dy, dg, dlg, dlb, dscn, dshn = vjp((dxn, dun))
        return [dx, dy], [dg, dlg, dlb, dscn, dshn]

    def post_bwd(s, x, y, dxn, dun):
        (dx, dy), (dgt[s], dlng[s], dlnb[s], dsc[s + 1], dsh[s + 1]) = _rows(
            f"post_ln_bwd_{s}", post_bwd_body, [x, y, dxn, dun],
            [gt[s], lng[s], lnb[s], sc[s + 1], sh[s + 1]], [(D, F32), (D, BF16)], [(1, D)] * 5)
        return dx, dy

    du3, gw1_1, gw2_1 = mlp_bwd(1, u3, h3, a3, dy3)
    dxp, dy2 = post_bwd(2, x2, y2, dxp, du3)

    G["dil_w_out"] = _mm("dil_out_dw", om.T, dy2)[None]
    do = _mm("dil_out_dx", dy2, W["dil_w_out"], tb=True).reshape(R, HD)

    def merge_bwd_body(r, pp):
        o_l, dout = r[:6], r[6]
        _, vjp = jax.vjp(_merge_core, *o_l)
        d = vjp(dout)
        dcs = [d[3 + g] - jnp.sum(d[g] * o_l[g], axis=1, keepdims=True) for g in range(3)]
        return list(d[:3]) + dcs, []

    mb, _ = _rows("attn_merge_bwd", merge_bwd_body, o_nat + l_nat + [do], [],
                  [(HD, BF16)] * 3 + [(1, F32)] * 3, tm=1024)
    dog = jnp.stack([_perm(mb[g].reshape(T, HEADS, HD), DILS[g]) for g in range(3)])
    dcc = jnp.stack([_perm(mb[3 + g].reshape(T, HEADS, 1), DILS[g]) for g in range(3)])
    dq, dkp, dvp, dbias = _attn_bwd(q, kp, vp, bias, jnp.swapaxes(bias, 2, 3), dog, lse,
                                    lse.reshape(3, HEADS, 1, T), dcc, dcc.reshape(3, HEADS, 1, T))
    dpq = jnp.concatenate(
        [_unperm(t[g], DILS[g]).reshape(T, RW) for g in range(3) for t in (dq, dkp[:, :, BLK:], dvp[:, :, BLK:])],
        axis=1).astype(BF16)
    rb = _relbias_grad(dbias.reshape(3, HEADS, NPAIR), onehotT)
    G["rel_bias"] = rb.transpose(2, 0, 1).reshape(NBUCKET, 3 * HEADS)
    G["dil_w_qkv"] = _mm("qkv_dw", u2.T, dpq)[None]
    du2 = _mm("qkv_dx", dpq, W["dil_w_qkv"], tb=True)
    dxp, dy1 = post_bwd(1, x1, y1, dxp, du2)

    du1, gw1_0, gw2_0 = mlp_bwd(0, u1, h1, a1, dy1)
    G["mlp_w1"] = jnp.stack([gw1_0, gw1_1])
    G["mlp_w2"] = jnp.stack([gw2_0, gw2_1])
    gt[0] = gt[0] + early_grads(G)
    dxp, dy0 = post_bwd(0, x0, y0, dxp, du1)

    G["ab_w_out"] = _mm("ab_out_dw", cat.T, dy0)[None]
    dcat = _mm("ab_out_dx", dy0, W["ab_w_out"], tb=True)

    def postmix_bwd_body(r, pp):
        _, vjp = jax.vjp(functools.partial(_post_core, pp[0]), *r[:5], *pp[1:])
        d = vjp(r[5])
        return list(_split3(d[0])) + list(d[1:5]), list(d[5:])

    (*dy_parts, dr1, dkh1, dv1, dgate), (G["rw_lnx_g"], G["rw_lnx_b"], drk) = _rows(
        "rwkv_post_bwd", postmix_bwd_body, [ysc, r_, kh_, v_, gate_, (dcat, 512, 0)], [E] + post_params,
        [(RW, BF16)] * 3 + [(RW, F32)] * 4, [(1, RW)] * 3, tm=128)
    G["rw_r_k"] = drk.reshape(1, HEADS, HD)
    dr2, dw2, dk2, da2, db2, dvT = _scan_bwd(*scan_in, _cols3(dy_parts), ck)
    dr2, dw2, dk2, da2, db2 = [_unpairs(t) for t in (dr2, dw2, dk2, da2, db2)]
    dv2 = _unheadsT(dvT)

    def pre_bwd_body(r, pp):
        prim, ct = r[:NPR], r[NPR:]
        _, vjp = jax.vjp(functools.partial(_pre_core, pp[0]), *prim, *pp[1:])
        cts = (ct[0] + ct[1], ct[2], ct[3] + ct[4], ct[5] + ct[6], ct[7], ct[8], ct[9], ct[10])
        d = vjp(cts)
        z = jnp.zeros_like(d[12])
        dp = jnp.concatenate([d[0], d[1], d[2], d[3], d[4], d[5], d[12], d[13], d[14]], axis=1)
        dp1 = jnp.concatenate([d[6], d[7], d[8], d[9], d[10], d[11], d[15], z, d[16]], axis=1)
        dp2 = jnp.concatenate([d[17], z, d[18]], axis=1)
        return [dp, dp1, dp2], list(d[NPR:])

    acc_shapes = [a.shape for a in pre_params]
    (dp, dp1, dp2), pacc = _rows(
        "rwkv_pre_bwd", pre_bwd_body,
        pre_rows + [dr1, dr2, dw2, dkh1, dk2, dv1, dv2, da2, db2, dgate, (dcat, 512, 1)],
        [E] + pre_params, [(PAB, F32), (PAB, F32), (PB, F32)], acc_shapes, tm=128)
    G["rw_mu"] = _unpad_pa(jnp.concatenate(pacc[:6], axis=1))
    G["rw_w0"], G["rw_a0"], G["rw_k_k"], G["rw_k_a"] = pacc[6], pacc[8], pacc[11], pacc[12]
    G["rw_w_up"] = pacc[7][None, :64]
    G["rw_a_up"] = pacc[9][None, :64]
    G["rw_g_up"] = pacc[10][None, :160]
    G["sc_conv_w"] = jnp.concatenate(pacc[13:16], axis=0)[None]

    def add3_body(r, pp):
        return [r[0] + r[1] + r[2]], []

    (dpt,), _ = _rows("shift_merge", add3_body,
                      [dp, _shift_up(dp1, 1), jnp.pad(_shift_up(dp2, 2), ((0, 0), (PA, 0)))], [], [(PAB, BF16)])
    gin = _mm("ab_in_dw", u0.T, dpt)
    G["ab_w_in"] = jnp.concatenate([_unpad_pa(gin[:, :PA]), gin[:, PA:]], axis=1)[None]
    du0 = _mm("ab_in_dx", dpt, W["ab_w_in"], tb=True)

    def mod_bwd_body(r, pp):
        du, dx, x = r
        return [dx + du * (1.0 + pp[0])], [jnp.sum(du * x, axis=0, keepdims=True), jnp.sum(du, axis=0, keepdims=True)]

    (grad_x,), (dsc[0], dsh[0]) = _rows("modulate_bwd", mod_bwd_body, [du0, dxp, x0], [sc[0]], [(D, F32)],
                                        [(1, D), (1, D)])

    G["ln_g"] = jnp.concatenate(dlng, axis=0).reshape(2, 2, D)
    G["ln_b"] = jnp.concatenate(dlnb, axis=0).reshape(2, 2, D)
    dmod = jnp.concatenate([dsh[0], dsc[0], dgt[0], dsh[1], dsc[1], dgt[1],
                            dsh[2], dsc[2], dgt[2], dsh[3], dsc[3], dgt[3]], axis=1).reshape(2, 6 * D)
    return loss_acc[0, 0], grad_x, dmod, G


def kernel(x, c, ada_w, ada_b, ln_g, ln_b, ab_w_in, rw_mu, rw_w0, rw_w_up, rw_a0, rw_a_up, rw_g_up, rw_k_k, rw_k_a, rw_r_k, rw_lnx_g, rw_lnx_b, sc_conv_w, ab_w_out, dil_w_qkv, dil_w_out, rel_bias, mlp_w1, mlp_w2, loss_target, m_ada_w, m_ada_b, m_ln_g, m_ln_b, m_ab_w_in, m_rw_mu, m_rw_w0, m_rw_w_up, m_rw_a0, m_rw_a_up, m_rw_g_up, m_rw_k_k, m_rw_k_a, m_rw_r_k, m_rw_lnx_g, m_rw_lnx_b, m_sc_conv_w, m_ab_w_out, m_dil_w_qkv, m_dil_w_out, m_rel_bias, m_mlp_w1, m_mlp_w2, v_ada_w, v_ada_b, v_ln_g, v_ln_b, v_ab_w_in, v_rw_mu, v_rw_w0, v_rw_w_up, v_rw_a0, v_rw_a_up, v_rw_g_up, v_rw_k_k, v_rw_k_a, v_rw_r_k, v_rw_lnx_g, v_rw_lnx_b, v_sc_conv_w, v_ab_w_out, v_dil_w_qkv, v_dil_w_out, v_rel_bias, v_mlp_w1, v_mlp_w2):
    w = dict(ada_w=ada_w, ada_b=ada_b, ln_g=ln_g, ln_b=ln_b, ab_w_in=ab_w_in, rw_mu=rw_mu, rw_w0=rw_w0,
             rw_w_up=rw_w_up, rw_a0=rw_a0, rw_a_up=rw_a_up, rw_g_up=rw_g_up, rw_k_k=rw_k_k, rw_k_a=rw_k_a,
             rw_r_k=rw_r_k, rw_lnx_g=rw_lnx_g, rw_lnx_b=rw_lnx_b, sc_conv_w=sc_conv_w, ab_w_out=ab_w_out,
             dil_w_qkv=dil_w_qkv, dil_w_out=dil_w_out, rel_bias=rel_bias, mlp_w1=mlp_w1, mlp_w2=mlp_w2)
    m = dict(ada_w=m_ada_w, ada_b=m_ada_b, ln_g=m_ln_g, ln_b=m_ln_b, ab_w_in=m_ab_w_in, rw_mu=m_rw_mu,
             rw_w0=m_rw_w0, rw_w_up=m_rw_w_up, rw_a0=m_rw_a0, rw_a_up=m_rw_a_up, rw_g_up=m_rw_g_up,
             rw_k_k=m_rw_k_k, rw_k_a=m_rw_k_a, rw_r_k=m_rw_r_k, rw_lnx_g=m_rw_lnx_g, rw_lnx_b=m_rw_lnx_b,
             sc_conv_w=m_sc_conv_w, ab_w_out=m_ab_w_out, dil_w_qkv=m_dil_w_qkv, dil_w_out=m_dil_w_out,
             rel_bias=m_rel_bias, mlp_w1=m_mlp_w1, mlp_w2=m_mlp_w2)
    v = dict(ada_w=v_ada_w, ada_b=v_ada_b, ln_g=v_ln_g, ln_b=v_ln_b, ab_w_in=v_ab_w_in, rw_mu=v_rw_mu,
             rw_w0=v_rw_w0, rw_w_up=v_rw_w_up, rw_a0=v_rw_a0, rw_a_up=v_rw_a_up, rw_g_up=v_rw_g_up,
             rw_k_k=v_rw_k_k, rw_k_a=v_rw_k_a, rw_r_k=v_rw_r_k, rw_lnx_g=v_rw_lnx_g, rw_lnx_b=v_rw_lnx_b,
             sc_conv_w=v_sc_conv_w, ab_w_out=v_ab_w_out, dil_w_qkv=v_dil_w_qkv, dil_w_out=v_dil_w_out,
             rel_bias=v_rel_bias, mlp_w1=v_mlp_w1, mlp_w2=v_mlp_w2)
    kinds = dict(SHARDED)
    me = 4 * lax.axis_index("x") + 2 * lax.axis_index("y") + lax.axis_index("c")
    ncol = ada_w.shape[2]

    small = _all_gather(_pack([c] + [w[n] for n in GATHER_F32], F32, 8), "gather_small")
    parts = _unpack(small, [c.shape] + [w[n].shape for n in GATHER_F32], (NDEV,))
    c_all = parts[0].reshape(NDEV, D)
    P = {n: _from_chunks(t, kinds[n]) for n, t in zip(GATHER_F32, parts[1:])}
    P = {n: (t if n in ("ln_g", "ln_b") else t[0]) for n, t in P.items()}
    for n in REPLICATED[1:]:
        P[n] = w[n]
    def full(n, t):
        t = _from_chunks(t, kinds[n])
        return t if n in ("mlp_w1", "mlp_w2") else t[0]

    parts = _all_gather_many([w[n].astype(BF16) for n in FIRST], "gather_first_weights")
    W = {n: full(n, t) for n, t in zip(FIRST, parts)}
    W["ab_w_in"] = jnp.concatenate([_pad_pa(W["ab_w_in"][:, :1824]), W["ab_w_in"][:, 1824:]], axis=1)

    ada_b_loc = lax.dynamic_slice(ada_b, (0, ncol * me), (2, ncol))
    mod_part = _ada_mod(c_all, ada_w, ada_b_loc)
    mod_all = _all_gather(mod_part.reshape(-1, 128), "gather_mod").reshape(NDEV, 2, NDEV, ncol)
    mod = lax.dynamic_index_in_dim(mod_all, me, axis=2, keepdims=False)
    mod = mod.transpose(1, 0, 2).reshape(2, 6 * D)

    behind = (mod[0, 0] * 0.0).astype(BF16)
    later = _exchange_start([w[n].astype(BF16) + (behind if n == LATER[0] else 0) for n in LATER], True,
                            "gather_later_weights_start")
    mod = mod + later[-1][0, 0]

    def later_weights(after):
        lands = _exchange_wait(later, True, after, "gather_later_weights_wait")
        return {n: full(n, t) for n, t in zip(LATER, lands)}

    sent = []

    def early_grads(G):
        sent.append(_exchange_start([_to_chunks(G[n], kinds[n]).astype(BF16) for n in LATER], False,
                                    "exchange_later_grads_start"))
        return sent[0][-1][0, 0]

    loss_part, grad_x, dmod, G = _local_step(x[0], loss_target[0], mod, W, P, later_weights, early_grads)
    G["ada_b"] = dmod
    loss = lax.psum(loss_part, ("x", "y", "c"))

    rep_shapes = [w[n].shape for n in REPLICATED]
    rep_all = _all_gather(_pack([G[n] for n in REPLICATED], F32, 8), "gather_replicated_grads")
    pk = lambda d: _pack([d[n] for n in REPLICATED], F32, 8)
    rep_out = _sum_adamw(rep_all, pk(w), pk(m), pk(v), "sum_adamw_replicated", rep_all.shape[1])
    rep_out = [dict(zip(REPLICATED, _unpack(o, rep_shapes))) for o in rep_out]

    dmod_all = _unpack(rep_all, [(2, 6 * D)], (NDEV,))[0]
    dmod_loc = lax.dynamic_slice(dmod_all, (0, 0, ncol * me), (NDEV, 2, ncol)).transpose(1, 0, 2)
    ada_out = _ada_grad_adamw(c_all.T, dmod_loc, ada_w, m_ada_w, v_ada_w)

    names = [n for n, _ in SHARDED if n not in GATHER_BF16]
    shard_shapes = [w[n].shape for n in names]
    chunks = _pack8([_to_chunks(G[n], kinds[n]) for n in names], F32, 8)
    recv = _all_to_all(chunks, "exchange_small_grads")
    pk = lambda d: _pack([d[n] for n in names], F32, 8)
    sh_out = _sum_adamw(recv, pk(w), pk(m), pk(v), "sum_adamw_small", recv.shape[1])
    sh_out = [dict(zip(names, _unpack(o, shard_shapes))) for o in sh_out]

    big_out = {}

    def update(n, contributions):
        cols = w[n].shape[-1]
        flat = lambda t: t.reshape(-1, cols)
        rows = flat(w[n]).shape[0]
        outs = _sum_adamw(contributions.reshape(-1, rows, cols), flat(w[n]), flat(m[n]), flat(v[n]),
                          f"sum_adamw_{n}", min(rows, 256))
        big_out[n] = [o.reshape(w[n].shape) for o in outs]

    ci = lax.axis_index("c")
    mine_l, sib_l = [], []
    for n in FIRST:
        g8 = _to_chunks(G[n], kinds[n])
        g42 = g8.reshape((4, 2) + g8.shape[1:])
        mine_l.append(lax.dynamic_index_in_dim(g42, ci, 1, keepdims=False))
        sib_l.append(lax.dynamic_index_in_dim(g42, 1 - ci, 1, keepdims=False))
    from_sib = _swap_sibling(sib_l, "swap_sibling_grads")

    def add2_body(r, pp):
        return [r[0] + r[1]], []

    partials = []
    for n, a, b in zip(FIRST, mine_l, from_sib):
        cols = a.shape[-1]
        (p,), _ = _rows(f"pair_sum_{n}", add2_body, [a.reshape(-1, cols), b.reshape(-1, cols)], [],
                        [(cols, BF16)], tm=512)
        partials.append(p.reshape(a.shape))
    for n, r in zip(FIRST, _exchange_chips(partials, "exchange_chip_grads")):
        update(n, r)

    for n, r in zip(LATER, _exchange_wait(sent[0], False, partials[0], "exchange_later_grads_wait")):
        update(n, r)
    sh_out = [{**d, **{n: big_out[n][i] for n in GATHER_BF16}} for i, d in enumerate(sh_out)]

    def pick(i, n):
        if n == "ada_w":
            return ada_out[i]
        return rep_out[i][n] if n in REPLICATED else sh_out[i][n]

    outs = [loss, grad_x[None]]
    for i in range(4):
        outs += [pick(i, n) for n in WEIGHTS]
    return tuple(outs)
```

```python
import functools
import math

import jax
import jax.numpy as jnp
from jax import lax
from jax.experimental import pallas as pl
from jax.experimental.pallas import tpu as pltpu

F32 = jnp.float32
BF16 = jnp.bfloat16
HI = lax.Precision.HIGHEST

NDEV = 8
T = 2048
D = 1024
DFF = 4096
HEADS = 8
HD = 64
RW = 512
PA = 2048
PB = 1536
PAB = PA + PB
QKV = 4608
DILS = (1, 4, 16)
BLK = 128
ALPHA = 4.0 ** 0.25
LN_EPS = 1e-5
GN_EPS = 64e-5
ADAM_LR, ADAM_B1, ADAM_B2, ADAM_EPS, ADAM_WD, ADAM_STEP = 0.001, 0.9, 0.999, 1e-8, 0.01, 10
VMEM_LIMIT = 56 * 1024 * 1024


def _cp(sem):
    return pltpu.CompilerParams(dimension_semantics=sem, vmem_limit_bytes=VMEM_LIMIT)


def _slot(px, py, pc):
    return 4 * px + 2 * py + pc


def _all_gather(x, name):
    R, C = x.shape

    def body(x_ref, out_ref, send_sems, recv_sems, local_sem):
        xi, yi, ci = lax.axis_index("x"), lax.axis_index("y"), lax.axis_index("c")
        me, sibling = (xi, yi, ci), (xi, yi, 1 - ci)
        chips = [(1 - xi, yi), (xi, 1 - yi), (1 - xi, 1 - yi)]

        def rows(px, py, pc):
            return out_ref.at[_slot(px, py, pc)]

        def copy(k, block, to, src=None):
            return pltpu.make_async_remote_copy(
                src_ref=rows(*block) if src is None else src, dst_ref=rows(*block),
                send_sem=send_sems.at[k], recv_sem=recv_sems.at[k],
                device_id=to, device_id_type=pl.DeviceIdType.MESH)

        mine = pltpu.make_async_copy(x_ref, rows(*me), local_sem)
        mine.start()
        first = [copy(0, me, sibling, src=x_ref)]
        first += [copy(1 + j, me, (*chip, ci), src=x_ref) for j, chip in enumerate(chips)]
        for cp in first:
            cp.start()
        passed = [copy(4 + j, (*chip, ci), sibling) for j, chip in enumerate(chips)]
        for j, chip in enumerate(chips):
            copy(1 + j, (*chip, ci), me).wait_recv()
            passed[j].start()
        copy(0, sibling, me).wait_recv()
        for j, chip in enumerate(chips):
            copy(4 + j, (*chip, 1 - ci), me).wait_recv()
        for cp in first + passed:
            cp.wait_send()
        mine.wait()

    return pl.pallas_call(
        body, name=name,
        out_shape=jax.ShapeDtypeStruct((NDEV, R, C), x.dtype),
        in_specs=[pl.BlockSpec(memory_space=pl.ANY)],
        out_specs=pl.BlockSpec(memory_space=pl.ANY),
        scratch_shapes=[pltpu.SemaphoreType.DMA((7,)), pltpu.SemaphoreType.DMA((7,)),
                        pltpu.SemaphoreType.DMA(())],
    )(x)


def _all_to_all(g, name):
    _, R, C = g.shape

    def body(g_ref, out_ref, send_sems, recv_sems, local_sem):
        xi, yi, ci = lax.axis_index("x"), lax.axis_index("y"), lax.axis_index("c")
        my_slot = _slot(xi, yi, ci)
        mine = pltpu.make_async_copy(g_ref.at[my_slot], out_ref.at[my_slot], local_sem)
        mine.start()
        copies = []
        for k in range(1, 8):
            px = 1 - xi if k & 4 else xi
            py = 1 - yi if k & 2 else yi
            pc = 1 - ci if k & 1 else ci
            peer_slot = _slot(px, py, pc)
            copies.append((
                pltpu.make_async_remote_copy(
                    src_ref=g_ref.at[peer_slot], dst_ref=out_ref.at[my_slot],
                    send_sem=send_sems.at[k - 1], recv_sem=recv_sems.at[k - 1],
                    device_id=(px, py, pc), device_id_type=pl.DeviceIdType.MESH),
                pltpu.make_async_remote_copy(
                    src_ref=g_ref.at[peer_slot], dst_ref=out_ref.at[peer_slot],
                    send_sem=send_sems.at[k - 1], recv_sem=recv_sems.at[k - 1],
                    device_id=(px, py, pc), device_id_type=pl.DeviceIdType.MESH)))
        for send, _ in copies:
            send.start()
        for _, recv in copies:
            recv.wait_recv()
        for send, _ in copies:
            send.wait_send()
        mine.wait()

    return pl.pallas_call(
        body, name=name,
        out_shape=jax.ShapeDtypeStruct((NDEV, R, C), g.dtype),
        in_specs=[pl.BlockSpec(memory_space=pl.ANY)],
        out_specs=pl.BlockSpec(memory_space=pl.ANY),
        scratch_shapes=[pltpu.SemaphoreType.DMA((7,)), pltpu.SemaphoreType.DMA((7,)),
                        pltpu.SemaphoreType.DMA(())],
    )(g)


def _my_slot():
    return _slot(lax.axis_index("x"), lax.axis_index("y"), lax.axis_index("c"))


def _put_own(buf, own, slot):
    return lax.dynamic_update_index_in_dim(buf, own, slot, 0)


def _hbm_call(body, name, ins, out_shapes, n_sems):
    anyspec = pl.BlockSpec(memory_space=pl.ANY)
    return pl.pallas_call(
        body, name=name, out_shape=out_shapes,
        in_specs=[anyspec] * len(ins), out_specs=[anyspec] * len(out_shapes),
        scratch_shapes=[pltpu.SemaphoreType.DMA(s) for s in n_sems],
    )(*ins)


def _all_gather_many(xs, name):
    n = len(xs)

    def body(*refs):
        x_refs, o_refs = refs[:n], refs[n:2 * n]
        send_sems, recv_sems = refs[2 * n:]
        xi, yi, ci = lax.axis_index("x"), lax.axis_index("y"), lax.axis_index("c")
        me, sibling = (xi, yi, ci), (xi, yi, 1 - ci)
        chips = [(1 - xi, yi), (xi, 1 - yi), (1 - xi, 1 - yi)]

        def copy(i, k, block, to, src=None):
            dst = o_refs[i].at[_slot(*block)]
            return pltpu.make_async_remote_copy(
                src_ref=dst if src is None else src, dst_ref=dst,
                send_sem=send_sems.at[i, k], recv_sem=recv_sems.at[i, k],
                device_id=to, device_id_type=pl.DeviceIdType.MESH)

        sends = []
        for i in range(n):
            sends += [copy(i, 1 + j, me, (*chip, ci), src=x_refs[i]) for j, chip in enumerate(chips)]
            sends.append(copy(i, 0, me, sibling, src=x_refs[i]))
        for cp in sends:
            cp.start()
        for j, chip in enumerate(chips):
            for i in range(n):
                copy(i, 1 + j, (*chip, ci), me).wait_recv()
                passed = copy(i, 4 + j, (*chip, ci), sibling)
                passed.start()
                sends.append(passed)
        for i in range(n):
            copy(i, 0, sibling, me).wait_recv()
            for j, chip in enumerate(chips):
                copy(i, 4 + j, (*chip, 1 - ci), me).wait_recv()
        for cp in sends:
            cp.wait_send()

    outs = _hbm_call(body, name, xs, [jax.ShapeDtypeStruct((NDEV,) + x.shape, x.dtype) for x in xs],
                     [(n, 7), (n, 7)])
    return [_put_own(o, x[None], _my_slot()) for o, x in zip(outs, xs)]


def _swap_sibling(gs, name):
    n = len(gs)

    def body(*refs):
        g_refs, o_refs = refs[:n], refs[n:2 * n]
        send_sems, recv_sems = refs[2 * n:]
        sibling = (lax.axis_index("x"), lax.axis_index("y"), 1 - lax.axis_index("c"))
        copies = [pltpu.make_async_remote_copy(
            src_ref=g_refs[i], dst_ref=o_refs[i], send_sem=send_sems.at[i], recv_sem=recv_sems.at[i],
            device_id=sibling, device_id_type=pl.DeviceIdType.MESH) for i in range(n)]
        for cp in copies:
            cp.start()
        for cp in copies:
            cp.wait_recv()
        for cp in copies:
            cp.wait_send()

    return _hbm_call(body, name, gs, [jax.ShapeDtypeStruct(g.shape, g.dtype) for g in gs], [(n,), (n,)])


def _exchange_chips(ps, name):
    n = len(ps)

    def body(*refs):
        p_refs, o_refs = refs[:n], refs[n:2 * n]
        send_sems, recv_sems = refs[2 * n:]
        xi, yi, ci = lax.axis_index("x"), lax.axis_index("y"), lax.axis_index("c")
        q_me = 2 * xi + yi
        sends, recvs = [], []
        for k in range(1, 4):
            px = 1 - xi if k & 2 else xi
            py = 1 - yi if k & 1 else yi
            q_peer = 2 * px + py
            for i in range(n):
                sends.append(pltpu.make_async_remote_copy(
                    src_ref=p_refs[i].at[q_peer], dst_ref=o_refs[i].at[q_me],
                    send_sem=send_sems.at[i, k - 1], recv_sem=recv_sems.at[i, k - 1],
                    device_id=(px, py, ci), device_id_type=pl.DeviceIdType.MESH))
                recvs.append(pltpu.make_async_remote_copy(
                    src_ref=p_refs[i].at[q_peer], dst_ref=o_refs[i].at[q_peer],
                    send_sem=send_sems.at[i, k - 1], recv_sem=recv_sems.at[i, k - 1],
                    device_id=(px, py, ci), device_id_type=pl.DeviceIdType.MESH))
        for cp in sends:
            cp.start()
        for cp in recvs:
            cp.wait_recv()
        for cp in sends:
            cp.wait_send()

    outs = _hbm_call(body, name, ps, [jax.ShapeDtypeStruct(p.shape, p.dtype) for p in ps], [(n, 3), (n, 3)])
    q_me = 2 * lax.axis_index("x") + lax.axis_index("y")
    return [_put_own(o, lax.dynamic_index_in_dim(p, q_me, 0, keepdims=True), q_me) for o, p in zip(outs, ps)]


def _peers(xi, yi, ci):
    return [(1 - xi if k & 4 else xi, 1 - yi if k & 2 else yi, 1 - ci if k & 1 else ci) for k in range(1, 8)]


def _direct_copy(src_refs, land_refs, send_sems, recv_sems, i, k, peer, my_slot, gather):
    src = src_refs[i] if gather else src_refs[i].at[_slot(*peer)]
    return pltpu.make_async_remote_copy(
        src_ref=src, dst_ref=land_refs[i].at[my_slot], send_sem=send_sems.at[7 * i + k], recv_sem=recv_sems.at[7 * i + k],
        device_id=peer, device_id_type=pl.DeviceIdType.MESH)


def _exchange_start(srcs, gather, name):
    n = len(srcs)
    lands = [lax.empty(((NDEV,) + s.shape) if gather else s.shape, s.dtype) for s in srcs]

    def body(*refs):
        s_refs, l_refs = refs[:n], refs[n:2 * n]
        send_sems, recv_sems = refs[2 * n], refs[2 * n + 1]
        token = refs[2 * n + 2 + 2 * n]
        xi, yi, ci = lax.axis_index("x"), lax.axis_index("y"), lax.axis_index("c")
        my_slot = _slot(xi, yi, ci)
        for k, peer in enumerate(_peers(xi, yi, ci)):
            for i in range(n):
                _direct_copy(s_refs, l_refs, send_sems, recv_sems, i, k, peer, my_slot, gather).start()
        token[...] = jnp.zeros_like(token)

    hbm = pl.BlockSpec(memory_space=pltpu.HBM)
    sem = pl.BlockSpec(memory_space=pltpu.SEMAPHORE)
    both = list(srcs) + lands
    return pl.pallas_call(
        body, name=name,
        out_shape=(pltpu.SemaphoreType.DMA((7 * n,)), pltpu.SemaphoreType.DMA((7 * n,)),
                   *[pltpu.HBM(t.shape, t.dtype) for t in both], jax.ShapeDtypeStruct((8, 128), F32)),
        in_specs=[hbm] * (2 * n),
        out_specs=(sem, sem, *[hbm] * (2 * n), pl.BlockSpec(memory_space=pltpu.VMEM)),
        input_output_aliases={i: 2 + i for i in range(2 * n)},
        compiler_params=pltpu.CompilerParams(has_side_effects=pltpu.SideEffectType.DATAFLOW_SIDE_EFFECTING),
    )(*[pltpu.with_memory_space_constraint(t, pltpu.HBM) for t in both])


def _exchange_wait(started, gather, after, name):
    send_sems, recv_sems, *thru, _ = started
    n = len(thru) // 2

    def body(*refs):
        s_refs, l_refs = refs[:n], refs[n:2 * n]
        send_sems, recv_sems = refs[2 * n], refs[2 * n + 1]
        xi, yi, ci = lax.axis_index("x"), lax.axis_index("y"), lax.axis_index("c")
        my_slot = _slot(xi, yi, ci)
        for k, peer in enumerate(_peers(xi, yi, ci)):
            for i in range(n):
                _direct_copy(s_refs, l_refs, send_sems, recv_sems, i, k, peer, my_slot, gather).wait_send()
                _direct_copy(s_refs, l_refs, send_sems, recv_sems, i, k, peer, _slot(*peer), gather).wait_recv()

    hbm = pl.BlockSpec(memory_space=pltpu.HBM)
    sem = pl.BlockSpec(memory_space=pltpu.SEMAPHORE)
    outs = pl.pallas_call(
        body, name=name,
        out_shape=tuple(pltpu.HBM(t.shape, t.dtype) for t in thru),
        in_specs=[hbm] * (2 * n) + [sem, sem, pl.BlockSpec(memory_space=pl.ANY)],
        out_specs=tuple([hbm] * (2 * n)),
        input_output_aliases={i: i for i in range(2 * n)},
        compiler_params=pltpu.CompilerParams(has_side_effects=pltpu.SideEffectType.DATAFLOW_SIDE_EFFECTING),
    )(*thru, send_sems, recv_sems, after)
    slot = _my_slot()
    own = [s[None] if gather else lax.dynamic_index_in_dim(s, slot, 0, keepdims=True) for s in outs[:n]]
    return [_put_own(land, o, slot) for land, o in zip(outs[n:], own)]


def _mm(name, a, b, tb=False, out=(F32,), epi=None, extras=(), tm=1024, tn=512, tk_cap=2048):
    M, K = a.shape
    N = b.shape[0] if tb else b.shape[1]
    tm, tn = min(tm, M), min(tn, N)
    tk = max(t for t in range(128, min(K, tk_cap) + 1, 128) if K % t == 0)
    assert M % tm == 0 and N % tn == 0 and K % tk == 0, (name, M, N, K)
    nk = K // tk
    ne, no = len(extras), len(out)
    dims = (((1,), (1 if tb else 0,)), ((), ()))

    def kern(*refs):
        a_ref, b_ref = refs[:2]
        e_refs = refs[2:2 + ne]
        o_refs = refs[2 + ne:2 + ne + no]

        def finish(acc):
            outs = epi(acc, *[e[...] for e in e_refs]) if epi is not None else (acc,)
            for o_ref, o in zip(o_refs, outs):
                o_ref[...] = o.astype(o_ref.dtype)

        part = lax.dot_general(a_ref[...], b_ref[...], dims, preferred_element_type=F32)
        if nk == 1:
            finish(part)
            return
        acc_ref = refs[-1]
        k = pl.program_id(2)

        @pl.when(k == 0)
        def _():
            acc_ref[...] = part

        @pl.when(k > 0)
        def _():
            acc_ref[...] += part

        @pl.when(k == nk - 1)
        def _():
            finish(acc_ref[...])

    b_spec = (pl.BlockSpec((tn, tk), lambda i, j, k: (j, k)) if tb
              else pl.BlockSpec((tk, tn), lambda i, j, k: (k, j)))
    tile = pl.BlockSpec((tm, tn), lambda i, j, k: (i, j))
    res = pl.pallas_call(
        kern, name=name, grid=(M // tm, N // tn, nk),
        in_specs=[pl.BlockSpec((tm, tk), lambda i, j, k: (i, k)), b_spec] + [tile] * ne,
        out_specs=[tile] * no,
        out_shape=[jax.ShapeDtypeStruct((M, N), dt) for dt in out],
        scratch_shapes=[pltpu.VMEM((tm, tn), F32)] if nk > 1 else [],
        compiler_params=_cp(("parallel", "parallel", "arbitrary")),
    )(a, b, *extras)
    return res[0] if no == 1 else res


def _rows(name, body, rows, params, out_rows, out_accs=(), tm=256):
    views = [r if isinstance(r, tuple) else (r, r.shape[1], 0) for r in rows]
    n = views[0][0].shape[0]
    assert n % tm == 0
    nr, npar, nor, noa = len(views), len(params), len(out_rows), len(out_accs)

    def kern(*refs):
        r_refs = refs[:nr]
        p_refs = refs[nr:nr + npar]
        o_refs = refs[nr + npar:nr + npar + nor]
        a_refs = refs[nr + npar + nor:]
        outs, accs = body([r[...] for r in r_refs], [p[...] for p in p_refs])
        assert len(outs) == nor and len(accs) == noa, (name, len(outs), len(accs))
        for o_ref, o in zip(o_refs, outs):
            o_ref[...] = o.astype(o_ref.dtype)
        if noa:
            @pl.when(pl.program_id(0) == 0)
            def _():
                for a_ref in a_refs:
                    a_ref[...] = jnp.zeros_like(a_ref)

            for a_ref, a in zip(a_refs, accs):
                a_ref[...] += a.astype(F32)

    def whole(shape):
        nd = len(shape)
        return pl.BlockSpec(tuple(shape), lambda i, nd=nd: (0,) * nd)

    in_specs = [pl.BlockSpec((tm, w), lambda i, cb=cb: (i, cb)) for _, w, cb in views]
    in_specs += [whole(p.shape) for p in params]
    out_specs = [pl.BlockSpec((tm, c), lambda i: (i, 0)) for c, _ in out_rows]
    out_specs += [whole(s) for s in out_accs]
    out_shape = [jax.ShapeDtypeStruct((n, c), dt) for c, dt in out_rows]
    out_shape += [jax.ShapeDtypeStruct(tuple(s), F32) for s in out_accs]
    res = pl.pallas_call(
        kern, name=name, grid=(n // tm,), in_specs=in_specs, out_specs=out_specs,
        out_shape=out_shape, compiler_params=_cp(("arbitrary",)),
    )(*[v[0] for v in views], *params)
    return res[:nor], res[nor:]


def _softplus(z):
    return jnp.maximum(z, 0.0) + jnp.log(1.0 + jnp.exp(jnp.minimum(z, -z)))


def _post_ln(x, y, g, lng, lnb):
    z = ALPHA * x + (1.0 + g) * y
    mu = jnp.mean(z, axis=-1, keepdims=True)
    zc = z - mu
    var = jnp.mean(zc * zc, axis=-1, keepdims=True)
    return zc * lax.rsqrt(var + LN_EPS) * lng + lnb


def _post_ln_mod(x, y, g, lng, lnb, scn, shn):
    xn = _post_ln(x, y, g, lng, lnb)
    return xn, xn * (1.0 + scn) + shn


def _pre_core(E, r_, k_, v_, wd_, ad_, gd_, r1, k1, v1, wd1, ad1, gd1, h, bg, cg, h1, cg1, h2, cg2,
              mu_r, mu_k, mu_v, mu_wd, mu_ad, mu_gd, w0, w_up, a0, a_up, g_up, k_k, k_a,
              cw0, cw1, cw2):
    def mix(x, x1, mu):
        return x + mu * (x1 - x)

    r, k, v = mix(r_, r1, mu_r), mix(k_, k1, mu_k), mix(v_, v1, mu_v)
    wd, ad, gd = mix(wd_, wd1, mu_wd), mix(ad_, ad1, mu_ad), mix(gd_, gd1, mu_gd)
    logw = -_softplus(-(w0 + jnp.dot(jnp.tanh(wd), w_up, preferred_element_type=F32))) - 0.5
    decay = jnp.exp(-jnp.exp(logw))
    iclr = jax.nn.sigmoid(a0 + jnp.dot(ad, a_up, preferred_element_type=F32))
    gate = jnp.dot(jax.nn.sigmoid(gd), g_up, preferred_element_type=F32)
    kk0 = k * k_k
    nrm = jnp.sqrt(jnp.dot(kk0 * kk0, E, precision=HI, preferred_element_type=F32))
    kk = kk0 / jnp.maximum(nrm, 1e-12)
    kh = k * (1.0 + (iclr - 1.0) * k_a)
    yb = bg * (cw2 * (cg * h) + cw1 * (cg1 * h1) + cw0 * (cg2 * h2))
    return r, decay, kh, v, -kk, kk * iclr, gate, yb


def _post_core(E, y, r, kh, v, gate, lnx_g, lnx_b, rk):
    def seg(t):
        return jnp.dot(t, E, precision=HI, preferred_element_type=F32)

    mean = seg(y) * (1.0 / HD)
    yc = y - mean
    var = seg(yc * yc) * (1.0 / HD)
    gn = yc * lax.rsqrt(var + GN_EPS) * lnx_g + lnx_b
    bonus = seg(r * kh * rk) * v
    return (gn + bonus) * gate


def _merge_core(o0, o1, o2, l0, l1, l2):
    m = jnp.maximum(jnp.maximum(l0, l1), l2)
    e0, e1, e2 = jnp.exp(l0 - m), jnp.exp(l1 - m), jnp.exp(l2 - m)
    den = e0 + e1 + e2
    return (e0 * o0 + e1 * o1 + e2 * o2) / den


CHUNK = 128
HALF = 64
HP = HEADS // 2
LW = 2 * HD
NCHUNK = T // CHUNK


def _split3(x):
    hi = x.astype(BF16)
    r1 = x - hi.astype(F32)
    mid = r1.astype(BF16)
    return hi, mid, (r1 - mid.astype(F32)).astype(BF16)


def _pairs(x):
    return x.reshape(T, HP, LW).transpose(1, 0, 2)


def _unpairs(x):
    return x.transpose(1, 0, 2).reshape(T, RW)


def _cols3(parts):
    tr = [p.reshape(NCHUNK, CHUNK, HP, 2, HD).transpose(2, 4, 0, 3, 1) for p in parts]
    return jnp.stack(tr, axis=4).reshape(HP, HD, 6 * T)


def _pick_codes():
    row = lax.broadcasted_iota(jnp.int32, (6 * CHUNK, LW), 0)
    col = lax.broadcasted_iota(jnp.int32, (6 * CHUNK, LW), 1)
    same = (row >= 3 * CHUNK) == (col >= HD)
    return jnp.where(same, row & (CHUNK - 1), -1).astype(BF16)


def _column(block_ref, codes, t):
    pick = jnp.where(codes == t.astype(BF16), jnp.ones((), BF16), jnp.zeros((), BF16))
    return jnp.dot(block_ref[...].reshape(HP * HD, 6 * CHUNK), pick, preferred_element_type=F32)


def _halfsums(x, row, left1):
    row_l = jnp.where(left1, row, 0.0)
    return (jnp.sum(x * row_l, axis=1, keepdims=True), jnp.sum(x * (row - row_l), axis=1, keepdims=True))


def _store_columns(ref, p, t_mask, cols):
    for j, col in enumerate(cols):
        pltpu.store(ref.at[pl.ds(2 * p + j, 1)], jnp.broadcast_to(col[None], (1, HD, CHUNK)), mask=t_mask[None])


def _scan_fwd(r, w, k, a, b, v3):
    def kern(r_ref, w_ref, k_ref, a_ref, b_ref, v_ref, y_ref, ck_ref, s_ref, vb_ref):
        @pl.when(pl.program_id(0) == 0)
        def _():
            s_ref[...] = jnp.zeros_like(s_ref)

        lane = lax.broadcasted_iota(jnp.int32, (HD, CHUNK), 1)
        left = lane < HD
        left1 = lax.broadcasted_iota(jnp.int32, (1, LW), 1) < HD
        codes = _pick_codes()

        def step(t, carry):
            row = lambda ref: [ref[p, pl.ds(t, 1), :] for p in range(HP)]
            S = [s_ref[p] for p in range(HP)]
            sa = [jnp.where(left, *_halfsums(s, a, left1)) for s, a in zip(S, row(a_ref))]
            S = [s * w + c * b + vb_ref[pl.ds(p * HD, HD), :] * k
                 for p, (s, w, c, b, k) in enumerate(zip(S, row(w_ref), sa, row(b_ref), row(k_ref)))]
            for p, s in enumerate(S):
                s_ref[p] = s
            for p, (s, r) in enumerate(zip(S, row(r_ref))):
                _store_columns(y_ref, p, lane == t, _halfsums(s, r, left1))
            vb_ref[...] = _column(v_ref, codes, t + 1)
            return carry

        for half in range(CHUNK // HALF):
            ck_ref[half] = s_ref[...]
            if half == 0:
                vb_ref[...] = _column(v_ref, codes, jnp.int32(0))
            lax.fori_loop(half * HALF, (half + 1) * HALF, step, 0, unroll=2)

    rowblk = pl.BlockSpec((HP, CHUNK, LW), lambda c: (0, c, 0))
    return pl.pallas_call(
        kern, name="rwkv_scan_fwd", grid=(NCHUNK,),
        in_specs=[rowblk] * 5 + [pl.BlockSpec((HP, HD, 6 * CHUNK), lambda c: (0, 0, c))],
        out_specs=[pl.BlockSpec((HEADS, HD, CHUNK), lambda c: (0, 0, c)),
                   pl.BlockSpec((CHUNK // HALF, HP, HD, LW), lambda c: (c, 0, 0, 0))],
        out_shape=[jax.ShapeDtypeStruct((HEADS, HD, T), F32),
                   jax.ShapeDtypeStruct((T // HALF, HP, HD, LW), F32)],
        scratch_shapes=[pltpu.VMEM((HP, HD, LW), F32), pltpu.VMEM((HP * HD, LW), F32)],
        compiler_params=_cp(("arbitrary",)),
    )(r, w, k, a, b, v3)


def _scan_bwd(r, w, k, a, b, v3, dy3, ck):
    NC = T // CHUNK

    def kern(r_ref, w_ref, k_ref, a_ref, b_ref, v_ref, dy_ref, ck_ref,
             dr_ref, dw_ref, dk_ref, da_ref, db_ref, dv_ref, ds_ref, sb_ref, vb_ref, sa_ref, dyb_ref):
        @pl.when(pl.program_id(0) == 0)
        def _():
            ds_ref[...] = jnp.zeros_like(ds_ref)

        lane = lax.broadcasted_iota(jnp.int32, (HD, CHUNK), 1)
        left = lane < HD
        left1 = lax.broadcasted_iota(jnp.int32, (1, LW), 1) < HD
        codes = _pick_codes()

        def rowsum(x):
            return jnp.sum(x, axis=0, keepdims=True)

        for half in reversed(range(CHUNK // HALF)):
            base = half * HALF
            sb_ref[0] = ck_ref[half]

            vb_ref[0] = _column(v_ref, codes, jnp.int32(base))

            def replay(i, carry):
                t = base + i
                row = lambda ref: [ref[p, pl.ds(t, 1), :] for p in range(HP)]
                S = [sb_ref[i, p] for p in range(HP)]
                sa = [jnp.where(left, *_halfsums(s, a, left1)) for s, a in zip(S, row(a_ref))]
                for p, (s, w, c, b, k) in enumerate(zip(S, row(w_ref), sa, row(b_ref), row(k_ref))):
                    sb_ref[i + 1, p] = s * w + c * b + vb_ref[i, pl.ds(p * HD, HD), :] * k
                    sa_ref[i, p] = c
                vb_ref[i + 1] = _column(v_ref, codes, t + 1)
                return carry

            lax.fori_loop(0, HALF, replay, 0, unroll=2)
            dyb_ref[...] = _column(dy_ref, codes, jnp.int32(base + HALF - 1))

            def back(ii, carry):
                i = HALF - 1 - ii
                t = base + i
                row = lambda ref: [ref[p, pl.ds(t, 1), :] for p in range(HP)]
                a_r, b_r, k_r, w_r, r_r = row(a_ref), row(b_ref), row(k_ref), row(w_ref), row(r_ref)
                dys = [dyb_ref[pl.ds(p * HD, HD), :] for p in range(HP)]
                dyb_ref[...] = _column(dy_ref, codes, jnp.maximum(t - 1, 0))
                for p in range(HP):
                    Sp, dy = sb_ref[i, p], dys[p]
                    dS = ds_ref[p] + dy * r_r[p]
                    dr_ref[p, pl.ds(t, 1), :] = rowsum(sb_ref[i + 1, p] * dy)
                    dw_ref[p, pl.ds(t, 1), :] = rowsum(dS * Sp)
                    db_ref[p, pl.ds(t, 1), :] = rowsum(dS * sa_ref[i, p])
                    dk_ref[p, pl.ds(t, 1), :] = rowsum(dS * vb_ref[i, pl.ds(p * HD, HD), :])
                    dsa = jnp.where(left, *_halfsums(dS, b_r[p], left1))
                    _store_columns(dv_ref, p, lane == t, _halfsums(dS, k_r[p], left1))
                    da_ref[p, pl.ds(t, 1), :] = rowsum(Sp * dsa)
                    ds_ref[p] = dS * w_r[p] + dsa * a_r[p]
                return carry

            lax.fori_loop(0, HALF, back, 0, unroll=2)

    rowblk = pl.BlockSpec((HP, CHUNK, LW), lambda c: (0, NC - 1 - c, 0))
    col3blk = pl.BlockSpec((HP, HD, 6 * CHUNK), lambda c: (0, 0, NC - 1 - c))
    rowshape = jax.ShapeDtypeStruct((HP, T, LW), F32)
    return pl.pallas_call(
        kern, name="rwkv_scan_bwd", grid=(NC,),
        in_specs=[rowblk] * 5 + [col3blk, col3blk,
                                 pl.BlockSpec((CHUNK // HALF, HP, HD, LW), lambda c: (NC - 1 - c, 0, 0, 0))],
        out_specs=[rowblk] * 5 + [pl.BlockSpec((HEADS, HD, CHUNK), lambda c: (0, 0, NC - 1 - c))],
        out_shape=[rowshape] * 5 + [jax.ShapeDtypeStruct((HEADS, HD, T), F32)],
        scratch_shapes=[pltpu.VMEM((HP, HD, LW), F32), pltpu.VMEM((HALF + 1, HP, HD, LW), F32),
                        pltpu.VMEM((HALF + 1, HP * HD, LW), F32), pltpu.VMEM((HALF, HP, HD, LW), F32),
                        pltpu.VMEM((HP * HD, LW), F32)],
        compiler_params=_cp(("arbitrary",)),
    )(r, w, k, a, b, v3, dy3, ck)


NBLK = T // BLK


def _blocks_per_segment(g):
    return jnp.where(g == 0, NBLK // DILS[0], jnp.where(g == 1, NBLK // DILS[1], NBLK // DILS[2]))


def _attn_fwd(q, kp, vp, bias):
    def kern(q_ref, k_ref, v_ref, b_ref, o_ref, l_ref):
        nbs = _blocks_per_segment(pl.program_id(0))
        qi = lax.broadcasted_iota(jnp.int32, (BLK, 2 * BLK), 0)
        ki = lax.broadcasted_iota(jnp.int32, (BLK, 2 * BLK), 1)
        band = (ki >= qi) & (ki <= qi + BLK)
        bias_t = b_ref[0, 0]
        for n in range(NBLK):
            lo = jnp.where((n & (nbs - 1)) == 0, BLK, 0)
            valid = band & (ki >= lo)
            qb = q_ref[0, 0, n * BLK:(n + 1) * BLK, :]
            kc = k_ref[0, 0, n * BLK:(n + 2) * BLK, :]
            vc = v_ref[0, 0, n * BLK:(n + 2) * BLK, :]
            s = lax.dot_general(qb, kc, (((1,), (1,)), ((), ())), preferred_element_type=F32)
            s = jnp.where(valid, s * (HD ** -0.5) + bias_t, -jnp.inf)
            m = jnp.max(s, axis=1, keepdims=True)
            e = jnp.exp(s - m)
            den = jnp.sum(e, axis=1, keepdims=True)
            pr = (e / den).astype(BF16)
            o_ref[0, 0, n * BLK:(n + 1) * BLK, :] = jnp.dot(pr, vc, preferred_element_type=F32)
            l_ref[0, 0, n * BLK:(n + 1) * BLK, :] = m + jnp.log(den)

    def blk(rows, cols):
        return pl.BlockSpec((1, 1, rows, cols), lambda g, h: (g, h, 0, 0))

    return pl.pallas_call(
        kern, name="attn_fwd", grid=(3, HEADS),
        in_specs=[blk(T, HD), blk(T + BLK, HD), blk(T + BLK, HD), blk(BLK, 2 * BLK)],
        out_specs=[blk(T, HD), blk(T, 1)],
        out_shape=[jax.ShapeDtypeStruct((3, HEADS, T, HD), F32),
                   jax.ShapeDtypeStruct((3, HEADS, T, 1), F32)],
        compiler_params=_cp(("parallel", "parallel")),
    )(q, kp, vp, bias)


def _attn_bwd(q, kp, vp, bias, biasT, do, lse_c, lse_r, dc_c, dc_r):
    def kern(q_ref, k_ref, v_ref, b_ref, bt_ref, do_ref, lc_ref, lr_ref, dcc_ref, dcr_ref,
             dq_ref, dk_ref, dv_ref, db_ref):
        nbs = _blocks_per_segment(pl.program_id(0))
        qi = lax.broadcasted_iota(jnp.int32, (BLK, 2 * BLK), 0)
        ki = lax.broadcasted_iota(jnp.int32, (BLK, 2 * BLK), 1)
        band = (ki >= qi) & (ki <= qi + BLK)
        kiT = lax.broadcasted_iota(jnp.int32, (2 * BLK, BLK), 0)
        qiT = lax.broadcasted_iota(jnp.int32, (2 * BLK, BLK), 1)
        bandT = (kiT >= qiT) & (kiT <= qiT + BLK)
        bias_t, biasT_t = b_ref[0, 0], bt_ref[0, 0]
        scale = HD ** -0.5
        dk_ref[...] = jnp.zeros_like(dk_ref)
        dv_ref[...] = jnp.zeros_like(dv_ref)
        db_ref[...] = jnp.zeros_like(db_ref)
        nt = (((1,), (1,)), ((), ()))
        for n in range(NBLK):
            lo = jnp.where((n & (nbs - 1)) == 0, BLK, 0)
            qs, ks = slice(n * BLK, (n + 1) * BLK), slice(n * BLK, (n + 2) * BLK)
            qb, kc, vc, dob = q_ref[0, 0, qs, :], k_ref[0, 0, ks, :], v_ref[0, 0, ks, :], do_ref[0, 0, qs, :]
            s = lax.dot_general(qb, kc, nt, preferred_element_type=F32) * scale + bias_t
            p = jnp.where(band & (ki >= lo), jnp.exp(s - lc_ref[0, 0, qs, :]), 0.0)
            dp = lax.dot_general(dob, vc, nt, preferred_element_type=F32)
            ds = p * (dp + dcc_ref[0, 0, qs, :])
            db_ref[0, 0] += ds
            dq_ref[0, 0, qs, :] = jnp.dot((ds * scale).astype(BF16), kc, preferred_element_type=F32)
            sT = lax.dot_general(kc, qb, nt, preferred_element_type=F32) * scale + biasT_t
            pT = jnp.where(bandT & (kiT >= lo), jnp.exp(sT - lr_ref[0, 0, :, qs]), 0.0)
            dpT = lax.dot_general(vc, dob, nt, preferred_element_type=F32)
            dsT = pT * (dpT + dcr_ref[0, 0, :, qs])
            dk_ref[0, 0, ks, :] += jnp.dot((dsT * scale).astype(BF16), qb, preferred_element_type=F32)
            dv_ref[0, 0, ks, :] += jnp.dot(pT.astype(BF16), dob, preferred_element_type=F32)

    def blk(rows, cols):
        return pl.BlockSpec((1, 1, rows, cols), lambda g, h: (g, h, 0, 0))

    return pl.pallas_call(
        kern, name="attn_bwd", grid=(3, HEADS),
        in_specs=[blk(T, HD), blk(T + BLK, HD), blk(T + BLK, HD), blk(BLK, 2 * BLK), blk(2 * BLK, BLK),
                  blk(T, HD), blk(T, 1), blk(1, T), blk(T, 1), blk(1, T)],
        out_specs=[blk(T, HD), blk(T + BLK, HD), blk(T + BLK, HD), blk(BLK, 2 * BLK)],
        out_shape=[jax.ShapeDtypeStruct((3, HEADS, T, HD), F32),
                   jax.ShapeDtypeStruct((3, HEADS, T + BLK, HD), F32),
                   jax.ShapeDtypeStruct((3, HEADS, T + BLK, HD), F32),
                   jax.ShapeDtypeStruct((3, HEADS, BLK, 2 * BLK), F32)],
        compiler_params=_cp(("parallel", "parallel")),
    )(q, kp, vp, bias, biasT, do, lse_c, lse_r, dc_c, dc_r)


NBUCKET = 32
NPAIR = BLK * 2 * BLK


def _relbias_table(rbT, onehotT):
    def kern(rb_ref, oh_ref, out_ref):
        out_ref[0] = sum(jnp.dot(p, oh_ref[0], preferred_element_type=F32) for p in _split3(rb_ref[0]))

    return pl.pallas_call(
        kern, name="relbias_table", grid=(3,),
        in_specs=[pl.BlockSpec((1, HEADS, NBUCKET), lambda g: (g, 0, 0)),
                  pl.BlockSpec((1, NBUCKET, NPAIR), lambda g: (g, 0, 0))],
        out_specs=pl.BlockSpec((1, HEADS, NPAIR), lambda g: (g, 0, 0)),
        out_shape=jax.ShapeDtypeStruct((3, HEADS, NPAIR), F32),
        compiler_params=_cp(("parallel",)),
    )(rbT, onehotT)


def _relbias_grad(db, onehotT):
    nt = (((1,), (1,)), ((), ()))

    def kern(db_ref, oh_ref, out_ref):
        hi, mid, _ = _split3(db_ref[0])
        out_ref[0] = (lax.dot_general(hi, oh_ref[0], nt, preferred_element_type=F32)
                      + lax.dot_general(mid, oh_ref[0], nt, preferred_element_type=F32))

    return pl.pallas_call(
        kern, name="relbias_grad", grid=(3,),
        in_specs=[pl.BlockSpec((1, HEADS, NPAIR), lambda g: (g, 0, 0)),
                  pl.BlockSpec((1, NBUCKET, NPAIR), lambda g: (g, 0, 0))],
        out_specs=pl.BlockSpec((1, HEADS, NBUCKET), lambda g: (g, 0, 0)),
        out_shape=jax.ShapeDtypeStruct((3, HEADS, NBUCKET), F32),
        compiler_params=_cp(("parallel",)),
    )(db, onehotT)


def _adamw(w, g, m, v):
    m2 = ADAM_B1 * m + (1.0 - ADAM_B1) * g
    v2 = ADAM_B2 * v + (1.0 - ADAM_B2) * (g * g)
    m_hat = m2 / (1.0 - ADAM_B1 ** ADAM_STEP)
    v_hat = v2 / (1.0 - ADAM_B2 ** ADAM_STEP)
    return -ADAM_LR * (m_hat / (jnp.sqrt(v_hat) + ADAM_EPS) + ADAM_WD * w), m2, v2


def _ada_mod(c_all, ada_w, ada_b_loc):
    def kern(c_ref, w_ref, b_ref, o_ref):
        c = c_ref[...]
        cond = c * jax.nn.sigmoid(c)
        o_ref[0] = jnp.dot(cond, w_ref[0], precision=HI, preferred_element_type=F32) + b_ref[0]

    ncol = ada_w.shape[2]
    return pl.pallas_call(
        kern, name="ada_mod", grid=(2,),
        in_specs=[pl.BlockSpec((NDEV, D), lambda i: (0, 0)),
                  pl.BlockSpec((1, D, ncol), lambda i: (i, 0, 0)),
                  pl.BlockSpec((1, 1, ncol), lambda i: (i, 0, 0))],
        out_specs=pl.BlockSpec((1, NDEV, ncol), lambda i: (i, 0, 0)),
        out_shape=jax.ShapeDtypeStruct((2, NDEV, ncol), F32),
        compiler_params=_cp(("parallel",)),
    )(c_all, ada_w, ada_b_loc.reshape(2, 1, ncol))


def _ada_grad_adamw(cT_all, dmod_loc, w, m, v):
    ncol = w.shape[2]
    tr = 256

    def kern(c_ref, d_ref, w_ref, m_ref, v_ref, g_ref, dl_ref, m2_ref, v2_ref):
        c = c_ref[...]
        cond = c * jax.nn.sigmoid(c)
        g = jnp.dot(cond, d_ref[0], precision=HI, preferred_element_type=F32)
        dl, m2, v2 = _adamw(w_ref[0], g, m_ref[0], v_ref[0])
        g_ref[0], dl_ref[0], m2_ref[0], v2_ref[0] = g, dl, m2, v2

    big = pl.BlockSpec((1, tr, ncol), lambda i, j: (i, j, 0))
    shp = jax.ShapeDtypeStruct(w.shape, F32)
    return pl.pallas_call(
        kern, name="ada_grad_adamw", grid=(2, D // tr),
        in_specs=[pl.BlockSpec((tr, NDEV), lambda i, j: (j, 0)),
                  pl.BlockSpec((1, NDEV, ncol), lambda i, j: (i, 0, 0)), big, big, big],
        out_specs=[big] * 4, out_shape=[shp] * 4,
        compiler_params=_cp(("parallel", "parallel")),
    )(cT_all, dmod_loc, w, m, v)


def _sum_adamw(recv, w, m, v, name, tr):
    S = recv.shape[0]
    R, C = w.shape
    assert R % tr == 0 and recv.shape[1:] == (R, C)

    def kern(r_ref, w_ref, m_ref, v_ref, g_ref, dl_ref, m2_ref, v2_ref):
        g = r_ref[0].astype(F32)
        for s in range(1, S):
            g = g + r_ref[s].astype(F32)
        dl, m2, v2 = _adamw(w_ref[...], g, m_ref[...], v_ref[...])
        g_ref[...], dl_ref[...], m2_ref[...], v2_ref[...] = g, dl, m2, v2

    flat = pl.BlockSpec((tr, C), lambda i: (i, 0))
    shp = jax.ShapeDtypeStruct((R, C), F32)
    return pl.pallas_call(
        kern, name=name, grid=(R // tr,),
        in_specs=[pl.BlockSpec((S, tr, C), lambda i: (0, i, 0)), flat, flat, flat],
        out_specs=[flat] * 4, out_shape=[shp] * 4,
        compiler_params=_cp(("parallel",)),
    )(recv, w, m, v)


def _pack(arrs, dtype, row_mult):
    flat = jnp.concatenate([a.reshape(-1).astype(dtype) for a in arrs])
    flat = jnp.pad(flat, (0, -flat.shape[0] % (128 * row_mult)))
    return flat.reshape(-1, 128)


def _pack8(arrs, dtype, row_mult):
    flat = jnp.concatenate([a.reshape(NDEV, -1).astype(dtype) for a in arrs], axis=1)
    flat = jnp.pad(flat, ((0, 0), (0, -flat.shape[1] % (128 * row_mult))))
    return flat.reshape(NDEV, -1, 128)


def _unpack(buf, shapes, lead=()):
    flat = buf.reshape(lead + (-1,))
    out, off = [], 0
    for s in shapes:
        n = math.prod(s)
        out.append(flat[..., off:off + n].reshape(lead + tuple(s)))
        off += n
    return out


def _to_chunks(full, kind):
    if kind == "col":
        x = full.reshape(full.shape[:-1] + (NDEV, full.shape[-1] // NDEV))
        return jnp.moveaxis(x, -2, 0)
    x = full.reshape(full.shape[:-2] + (NDEV, full.shape[-2] // NDEV, full.shape[-1]))
    return jnp.moveaxis(x, -3, 0)


def _from_chunks(g8, kind):
    if kind == "col":
        x = jnp.moveaxis(g8, 0, -2)
        return x.reshape(x.shape[:-2] + (x.shape[-2] * x.shape[-1],))
    x = jnp.moveaxis(g8, 0, -3)
    return x.reshape(x.shape[:-3] + (x.shape[-3] * x.shape[-2], x.shape[-1]))


def _pad_pa(x):
    z = lambda n: jnp.zeros(x.shape[:-1] + (n,), x.dtype)
    return jnp.concatenate([x[..., :1600], z(64), x[..., 1600:1664], z(64), x[..., 1664:1824], z(96)], -1)


def _unpad_pa(x):
    return jnp.concatenate([x[..., :1600], x[..., 1664:1728], x[..., 1792:1952]], -1)


def _pad_rows(x, n):
    return jnp.pad(x, ((0, n - x.shape[0]), (0, 0)))


def _shift_down(x, n):
    return jnp.pad(x, ((n, 0), (0, 0)))[:-n]


def _shift_up(x, n):
    return jnp.pad(x, ((0, n), (0, 0)))[n:]


def _unheadsT(x):
    return x.transpose(2, 0, 1).reshape(T, RW)


def _perm(x, dil):
    C = x.shape[-1]
    return x.reshape(T // dil, dil, HEADS, C).transpose(2, 1, 0, 3).reshape(HEADS, T, C)


def _unperm(y, dil):
    C = y.shape[-1]
    return y.reshape(HEADS, dil, T // dil, C).transpose(2, 1, 0, 3).reshape(T, HEADS, C)


def _bucket_tables():
    qi = jnp.arange(BLK)[:, None]
    ki = jnp.arange(2 * BLK)[None, :]
    rel = BLK + qi - ki
    tabs = []
    for dil in DILS:
        dist = jnp.clip(rel, 0, BLK) * dil
        logd = jnp.log(jnp.maximum(dist, 1).astype(F32) / 16) / math.log(2048 / 16)
        large = jnp.minimum(16 + (logd * 16).astype(jnp.int32), 31)
        tabs.append(jnp.where(dist < 16, dist, large))
    return jnp.stack(tabs)


SHARDED = (("ln_g", "col"), ("ln_b", "col"), ("ab_w_in", "col"), ("rw_w_up", "col"), ("rw_a_up", "col"),
           ("rw_g_up", "col"), ("sc_conv_w", "col"), ("ab_w_out", "row"), ("dil_w_qkv", "col"),
           ("dil_w_out", "col"), ("mlp_w1", "col"), ("mlp_w2", "row"))
FIRST = ("ab_w_in", "ab_w_out")
LATER = ("dil_w_qkv", "dil_w_out", "mlp_w1", "mlp_w2")
GATHER_BF16 = FIRST + LATER
GATHER_F32 = ("rw_w_up", "rw_a_up", "rw_g_up", "sc_conv_w", "ln_g", "ln_b")
REPLICATED = ("ada_b", "rw_mu", "rw_w0", "rw_a0", "rw_k_k", "rw_k_a", "rw_r_k", "rw_lnx_g", "rw_lnx_b", "rel_bias")
WEIGHTS = ("ada_w", "ada_b", "ln_g", "ln_b", "ab_w_in", "rw_mu", "rw_w0", "rw_w_up", "rw_a0", "rw_a_up",
           "rw_g_up", "rw_k_k", "rw_k_a", "rw_r_k", "rw_lnx_g", "rw_lnx_b", "sc_conv_w", "ab_w_out",
           "dil_w_qkv", "dil_w_out", "rel_bias", "mlp_w1", "mlp_w2")
FLAT_TILE = 512


def _local_step(x0, tgt, mod, W, P, later_weights, early_grads):
    row = lambda a: a.reshape(1, -1)
    W = dict(W)
    m6 = mod.reshape(2, 6, 1, D)
    sc = [m6[0, 1], m6[0, 4], m6[1, 1], m6[1, 4]]
    sh = [m6[0, 0], m6[0, 3], m6[1, 0], m6[1, 3]]
    gt = [m6[0, 2], m6[0, 5], m6[1, 2], m6[1, 5]]
    lng = [row(P["ln_g"][0, 0]), row(P["ln_g"][0, 1]), row(P["ln_g"][1, 0]), row(P["ln_g"][1, 1])]
    lnb = [row(P["ln_b"][0, 0]), row(P["ln_b"][0, 1]), row(P["ln_b"][1, 0]), row(P["ln_b"][1, 1])]
    E = jnp.kron(jnp.eye(HEADS, dtype=F32), jnp.ones((HD, HD), F32))

    def mod_body(r, p):
        return [r[0] * (1.0 + p[0]) + p[1]], []

    (u0,), _ = _rows("modulate", mod_body, [x0], [sc[0], sh[0]], [(D, BF16)])

    def post_fwd_body(r, p):
        xn, un = _post_ln_mod(r[0], r[1], *p)
        return [xn, un], []

    def post_fwd(s, x, y):
        (xn, un), _ = _rows(f"post_ln_{s}", post_fwd_body, [x, y],
                            [gt[s], lng[s], lnb[s], sc[s + 1], sh[s + 1]], [(D, F32), (D, BF16)])
        return xn, un

    def relu2(acc):
        a = jnp.maximum(acc, 0.0)
        return acc, a * a

    def relu2_bwd(acc, h):
        return (acc * (2.0 * jnp.maximum(h, 0.0)),)

    p = _mm("ab_in", u0, W["ab_w_in"])
    p1 = _shift_down(p, 1)
    p2 = _shift_down(p[:, PA:], 2)
    mu = _pad_pa(P["rw_mu"])
    mu_parts = [mu[:, :512], mu[:, 512:1024], mu[:, 1024:1536], mu[:, 1536:1664], mu[:, 1664:1792], mu[:, 1792:]]
    pre_params = mu_parts + [P["rw_w0"], _pad_rows(P["rw_w_up"], 128), P["rw_a0"], _pad_rows(P["rw_a_up"], 128),
                             _pad_rows(P["rw_g_up"], 256), P["rw_k_k"], P["rw_k_a"],
                             P["sc_conv_w"][0:1], P["sc_conv_w"][1:2], P["sc_conv_w"][2:3]]
    pre_rows = [(p, 512, 0), (p, 512, 1), (p, 512, 2), (p, 128, 12), (p, 128, 13), (p, 256, 7),
                (p1, 512, 0), (p1, 512, 1), (p1, 512, 2), (p1, 128, 12), (p1, 128, 13), (p1, 256, 7),
                (p, 512, 4), (p, 512, 5), (p, 512, 6), (p1, 512, 4), (p1, 512, 6), (p2, 512, 0), (p2, 512, 2)]
    NPR = len(pre_rows)

    def pre_fwd_body(r, pp):
        outs = list(_pre_core(pp[0], *r, *pp[1:]))
        return outs + list(_split3(outs[3])), []

    (r_, w_, kh_, v_, a_, b_, gate_, yb, *v_parts), _ = _rows(
        "rwkv_pre", pre_fwd_body, pre_rows, [E] + pre_params,
        [(RW, F32)] * 7 + [(RW, BF16)] * 4, tm=128)
    scan_in = [_pairs(t) for t in (r_, w_, kh_, a_, b_)] + [_cols3(v_parts)]
    yT, ck = _scan_fwd(*scan_in)
    ysc = _unheadsT(yT)
    post_params = [P["rw_lnx_g"], P["rw_lnx_b"], P["rw_r_k"].reshape(1, RW)]

    def postmix_fwd_body(r, pp):
        return [_post_core(pp[0], *r, *pp[1:])], []

    (ya,), _ = _rows("rwkv_post", postmix_fwd_body, [ysc, r_, kh_, v_, gate_], [E] + post_params,
                     [(RW, BF16)], tm=128)
    cat = jnp.concatenate([ya, yb], axis=1)
    y0 = _mm("ab_out", cat, W["ab_w_out"])
    x1, u1 = post_fwd(0, x0, y0)
    W.update(later_weights(u1))

    h1, a1 = _mm("mlp1_up_0", u1, W["mlp_w1"][0], out=(F32, BF16), epi=relu2)
    y1 = _mm("mlp1_down_0", a1, W["mlp_w2"][0])
    x2, u2 = post_fwd(1, x1, y1)

    pq = _mm("qkv", u2, W["dil_w_qkv"], out=(BF16,))
    pq5 = pq.reshape(T, 3, 3, HEADS, HD)
    q = jnp.stack([_perm(pq5[:, g, 0], DILS[g]) for g in range(3)])
    kp = jnp.pad(jnp.stack([_perm(pq5[:, g, 1], DILS[g]) for g in range(3)]), ((0, 0), (0, 0), (BLK, 0), (0, 0)))
    vp = jnp.pad(jnp.stack([_perm(pq5[:, g, 2], DILS[g]) for g in range(3)]), ((0, 0), (0, 0), (BLK, 0), (0, 0)))
    onehotT = (_bucket_tables().reshape(3, 1, NPAIR) == jnp.arange(NBUCKET).reshape(1, NBUCKET, 1)).astype(BF16)
    rbT = P["rel_bias"].reshape(NBUCKET, 3, HEADS).transpose(1, 2, 0)
    bias = _relbias_table(rbT, onehotT).reshape(3, HEADS, BLK, 2 * BLK)
    og, lse = _attn_fwd(q, kp, vp, bias)
    R = T * HEADS
    o_nat = [_unperm(og[g], DILS[g]).reshape(R, HD) for g in range(3)]
    l_nat = [_unperm(lse[g], DILS[g]).reshape(R, 1) for g in range(3)]

    def merge_fwd_body(r, pp):
        return [_merge_core(*r)], []

    (om,), _ = _rows("attn_merge", merge_fwd_body, o_nat + l_nat, [], [(HD, BF16)], tm=1024)
    om = om.reshape(T, RW)
    y2 = _mm("dil_out", om, W["dil_w_out"])
    x3, u3 = post_fwd(2, x2, y2)

    h3, a3 = _mm("mlp1_up_1", u3, W["mlp_w1"][1], out=(F32, BF16), epi=relu2)
    y3 = _mm("mlp1_down_1", a3, W["mlp_w2"][1])

    def last_body(r, pp):
        x, y, tg = r
        xn, vjp = jax.vjp(_post_ln, x, y, *pp)
        err = xn - tg
        dx, dy, dg, dlg, dlb = vjp(err * (1.0 / D))
        loss = jnp.full((1, 128), (0.5 / D) * jnp.sum(err * err), F32)
        return [dx, dy], [loss, dg, dlg, dlb]

    (dxp, dy3), (loss_acc, dg3, dlng3, dlnb3) = _rows(
        "final_ln_loss", last_body, [x3, y3, tgt], [gt[3], lng[3], lnb[3]],
        [(D, F32), (D, BF16)], [(1, 128), (1, D), (1, D), (1, D)])

    G = {}
    dsc, dsh, dgt = [None] * 4, [None] * 4, [None] * 4
    dlng, dlnb = [None] * 4, [None] * 4
    dgt[3], dlng[3], dlnb[3] = dg3, dlng3, dlnb3

    def mlp_bwd(i, u, h, a, dy):
        dh = _mm(f"mlp_dh_{i}", dy, W["mlp_w2"][i], tb=True, out=(BF16,), epi=relu2_bwd, extras=(h,))
        gw2 = _mm(f"mlp_dw2_{i}", a.T, dy)
        du = _mm(f"mlp_du_{i}", dh, W["mlp_w1"][i], tb=True)
        gw1 = _mm(f"mlp_dw1_{i}", u.T, dh)
        return du, gw1, gw2

    def post_bwd_body(r, pp):
        x, y, dxn, dun = r
        _, vjp = jax.vjp(_post_ln_mod, x, y, *pp)
        dx, dy, dg, dlg, dlb, dscn, dshn = vjp((dxn, dun))
        return [dx, dy], [dg, dlg, dlb, dscn, dshn]

    def post_bwd(s, x, y, dxn, dun):
        (dx, dy), (dgt[s], dlng[s], dlnb[s], dsc[s + 1], dsh[s + 1]) = _rows(
            f"post_ln_bwd_{s}", post_bwd_body, [x, y, dxn, dun],
            [gt[s], lng[s], lnb[s], sc[s + 1], sh[s + 1]], [(D, F32), (D, BF16)], [(1, D)] * 5)
        return dx, dy

    du3, gw1_1, gw2_1 = mlp_bwd(1, u3, h3, a3, dy3)
    dxp, dy2 = post_bwd(2, x2, y2, dxp, du3)

    G["dil_w_out"] = _mm("dil_out_dw", om.T, dy2)[None]
    do = _mm("dil_out_dx", dy2, W["dil_w_out"], tb=True).reshape(R, HD)

    def merge_bwd_body(r, pp):
        o_l, dout = r[:6], r[6]
        _, vjp = jax.vjp(_merge_core, *o_l)
        d = vjp(dout)
        dcs = [d[3 + g] - jnp.sum(d[g] * o_l[g], axis=1, keepdims=True) for g in range(3)]
        return list(d[:3]) + dcs, []

    mb, _ = _rows("attn_merge_bwd", merge_bwd_body, o_nat + l_nat + [do], [],
                  [(HD, BF16)] * 3 + [(1, F32)] * 3, tm=1024)
    dog = jnp.stack([_perm(mb[g].reshape(T, HEADS, HD), DILS[g]) for g in range(3)])
    dcc = jnp.stack([_perm(mb[3 + g].reshape(T, HEADS, 1), DILS[g]) for g in range(3)])
    dq, dkp, dvp, dbias = _attn_bwd(q, kp, vp, bias, jnp.swapaxes(bias, 2, 3), dog, lse,
                                    lse.reshape(3, HEADS, 1, T), dcc, dcc.reshape(3, HEADS, 1, T))
    dpq = jnp.concatenate(
        [_unperm(t[g], DILS[g]).reshape(T, RW) for g in range(3) for t in (dq, dkp[:, :, BLK:], dvp[:, :, BLK:])],
        axis=1).astype(BF16)
    rb = _relbias_grad(dbias.reshape(3, HEADS, NPAIR), onehotT)
    G["rel_bias"] = rb.transpose(2, 0, 1).reshape(NBUCKET, 3 * HEADS)
    G["dil_w_qkv"] = _mm("qkv_dw", u2.T, dpq)[None]
    du2 = _mm("qkv_dx", dpq, W["dil_w_qkv"], tb=True)
    dxp, dy1 = post_bwd(1, x1, y1, dxp, du2)

    du1, gw1_0, gw2_0 = mlp_bwd(0, u1, h1, a1, dy1)
    G["mlp_w1"] = jnp.stack([gw1_0, gw1_1])
    G["mlp_w2"] = jnp.stack([gw2_0, gw2_1])
    gt[0] = gt[0] + early_grads(G)
    dxp, dy0 = post_bwd(0, x0, y0, dxp, du1)

    G["ab_w_out"] = _mm("ab_out_dw", cat.T, dy0)[None]
    dcat = _mm("ab_out_dx", dy0, W["ab_w_out"], tb=True)

    def postmix_bwd_body(r, pp):
        _, vjp = jax.vjp(functools.partial(_post_core, pp[0]), *r[:5], *pp[1:])
        d = vjp(r[5])
        return list(_split3(d[0])) + list(d[1:5]), list(d[5:])

    (*dy_parts, dr1, dkh1, dv1, dgate), (G["rw_lnx_g"], G["rw_lnx_b"], drk) = _rows(
        "rwkv_post_bwd", postmix_bwd_body, [ysc, r_, kh_, v_, gate_, (dcat, 512, 0)], [E] + post_params,
        [(RW, BF16)] * 3 + [(RW, F32)] * 4, [(1, RW)] * 3, tm=128)
    G["rw_r_k"] = drk.reshape(1, HEADS, HD)
    dr2, dw2, dk2, da2, db2, dvT = _scan_bwd(*scan_in, _cols3(dy_parts), ck)
    dr2, dw2, dk2, da2, db2 = [_unpairs(t) for t in (dr2, dw2, dk2, da2, db2)]
    dv2 = _unheadsT(dvT)

    def pre_bwd_body(r, pp):
        prim, ct = r[:NPR], r[NPR:]
        _, vjp = jax.vjp(functools.partial(_pre_core, pp[0]), *prim, *pp[1:])
        cts = (ct[0] + ct[1], ct[2], ct[3] + ct[4], ct[5] + ct[6], ct[7], ct[8], ct[9], ct[10])
        d = vjp(cts)
        z = jnp.zeros_like(d[12])
        dp = jnp.concatenate([d[0], d[1], d[2], d[3], d[4], d[5], d[12], d[13], d[14]], axis=1)
        dp1 = jnp.concatenate([d[6], d[7], d[8], d[9], d[10], d[11], d[15], z, d[16]], axis=1)
        dp2 = jnp.concatenate([d[17], z, d[18]], axis=1)
        return [dp, dp1, dp2], list(d[NPR:])

    acc_shapes = [a.shape for a in pre_params]
    (dp, dp1, dp2), pacc = _rows(
        "rwkv_pre_bwd", pre_bwd_body,
        pre_rows + [dr1, dr2, dw2, dkh1, dk2, dv1, dv2, da2, db2, dgate, (dcat, 512, 1)],
        [E] + pre_params, [(PAB, F32), (PAB, F32), (PB, F32)], acc_shapes, tm=128)
    G["rw_mu"] = _unpad_pa(jnp.concatenate(pacc[:6], axis=1))
    G["rw_w0"], G["rw_a0"], G["rw_k_k"], G["rw_k_a"] = pacc[6], pacc[8], pacc[11], pacc[12]
    G["rw_w_up"] = pacc[7][None, :64]
    G["rw_a_up"] = pacc[9][None, :64]
    G["rw_g_up"] = pacc[10][None, :160]
    G["sc_conv_w"] = jnp.concatenate(pacc[13:16], axis=0)[None]

    def add3_body(r, pp):
        return [r[0] + r[1] + r[2]], []

    (dpt,), _ = _rows("shift_merge", add3_body,
                      [dp, _shift_up(dp1, 1), jnp.pad(_shift_up(dp2, 2), ((0, 0), (PA, 0)))], [], [(PAB, BF16)])
    gin = _mm("ab_in_dw", u0.T, dpt)
    G["ab_w_in"] = jnp.concatenate([_unpad_pa(gin[:, :PA]), gin[:, PA:]], axis=1)[None]
    du0 = _mm("ab_in_dx", dpt, W["ab_w_in"], tb=True)

    def mod_bwd_body(r, pp):
        du, dx, x = r
        return [dx + du * (1.0 + pp[0])], [jnp.sum(du * x, axis=0, keepdims=True), jnp.sum(du, axis=0, keepdims=True)]

    (grad_x,), (dsc[0], dsh[0]) = _rows("modulate_bwd", mod_bwd_body, [du0, dxp, x0], [sc[0]], [(D, F32)],
                                        [(1, D), (1, D)])

    G["ln_g"] = jnp.concatenate(dlng, axis=0).reshape(2, 2, D)
    G["ln_b"] = jnp.concatenate(dlnb, axis=0).reshape(2, 2, D)
    dmod = jnp.concatenate([dsh[0], dsc[0], dgt[0], dsh[1], dsc[1], dgt[1],
                            dsh[2], dsc[2], dgt[2], dsh[3], dsc[3], dgt[3]], axis=1).reshape(2, 6 * D)
    return loss_acc[0, 0], grad_x, dmod, G


def kernel(x, c, ada_w, ada_b, ln_g, ln_b, ab_w_in, rw_mu, rw_w0, rw_w_up, rw_a0, rw_a_up, rw_g_up, rw_k_k, rw_k_a, rw_r_k, rw_lnx_g, rw_lnx_b, sc_conv_w, ab_w_out, dil_w_qkv, dil_w_out, rel_bias, mlp_w1, mlp_w2, loss_target, m_ada_w, m_ada_b, m_ln_g, m_ln_b, m_ab_w_in, m_rw_mu, m_rw_w0, m_rw_w_up, m_rw_a0, m_rw_a_up, m_rw_g_up, m_rw_k_k, m_rw_k_a, m_rw_r_k, m_rw_lnx_g, m_rw_lnx_b, m_sc_conv_w, m_ab_w_out, m_dil_w_qkv, m_dil_w_out, m_rel_bias, m_mlp_w1, m_mlp_w2, v_ada_w, v_ada_b, v_ln_g, v_ln_b, v_ab_w_in, v_rw_mu, v_rw_w0, v_rw_w_up, v_rw_a0, v_rw_a_up, v_rw_g_up, v_rw_k_k, v_rw_k_a, v_rw_r_k, v_rw_lnx_g, v_rw_lnx_b, v_sc_conv_w, v_ab_w_out, v_dil_w_qkv, v_dil_w_out, v_rel_bias, v_mlp_w1, v_mlp_w2):
    w = dict(ada_w=ada_w, ada_b=ada_b, ln_g=ln_g, ln_b=ln_b, ab_w_in=ab_w_in, rw_mu=rw_mu, rw_w0=rw_w0,
             rw_w_up=rw_w_up, rw_a0=rw_a0, rw_a_up=rw_a_up, rw_g_up=rw_g_up, rw_k_k=rw_k_k, rw_k_a=rw_k_a,
             rw_r_k=rw_r_k, rw_lnx_g=rw_lnx_g, rw_lnx_b=rw_lnx_b, sc_conv_w=sc_conv_w, ab_w_out=ab_w_out,
             dil_w_qkv=dil_w_qkv, dil_w_out=dil_w_out, rel_bias=rel_bias, mlp_w1=mlp_w1, mlp_w2=mlp_w2)
    m = dict(ada_w=m_ada_w, ada_b=m_ada_b, ln_g=m_ln_g, ln_b=m_ln_b, ab_w_in=m_ab_w_in, rw_mu=m_rw_mu,
             rw_w0=m_rw_w0, rw_w_up=m_rw_w_up, rw_a0=m_rw_a0, rw_a_up=m_rw_a_up, rw_g_up=m_rw_g_up,
             rw_k_k=m_rw_k_k, rw_k_a=m_rw_k_a, rw_r_k=m_rw_r_k, rw_lnx_g=m_rw_lnx_g, rw_lnx_b=m_rw_lnx_b,
             sc_conv_w=m_sc_conv_w, ab_w_out=m_ab_w_out, dil_w_qkv=m_dil_w_qkv, dil_w_out=m_dil_w_out,
             rel_bias=m_rel_bias, mlp_w1=m_mlp_w1, mlp_w2=m_mlp_w2)
    v = dict(ada_w=v_ada_w, ada_b=v_ada_b, ln_g=v_ln_g, ln_b=v_ln_b, ab_w_in=v_ab_w_in, rw_mu=v_rw_mu,
             rw_w0=v_rw_w0, rw_w_up=v_rw_w_up, rw_a0=v_rw_a0, rw_a_up=v_rw_a_up, rw_g_up=v_rw_g_up,
             rw_k_k=v_rw_k_k, rw_k_a=v_rw_k_a, rw_r_k=v_rw_r_k, rw_lnx_g=v_rw_lnx_g, rw_lnx_b=v_rw_lnx_b,
             sc_conv_w=v_sc_conv_w, ab_w_out=v_ab_w_out, dil_w_qkv=v_dil_w_qkv, dil_w_out=v_dil_w_out,
             rel_bias=v_rel_bias, mlp_w1=v_mlp_w1, mlp_w2=v_mlp_w2)
    kinds = dict(SHARDED)
    me = 4 * lax.axis_index("x") + 2 * lax.axis_index("y") + lax.axis_index("c")
    ncol = ada_w.shape[2]

    small = _all_gather(_pack([c] + [w[n] for n in GATHER_F32], F32, 8), "gather_small")
    parts = _unpack(small, [c.shape] + [w[n].shape for n in GATHER_F32], (NDEV,))
    c_all = parts[0].reshape(NDEV, D)
    P = {n: _from_chunks(t, kinds[n]) for n, t in zip(GATHER_F32, parts[1:])}
    P = {n: (t if n in ("ln_g", "ln_b") else t[0]) for n, t in P.items()}
    for n in REPLICATED[1:]:
        P[n] = w[n]
    def full(n, t):
        t = _from_chunks(t, kinds[n])
        return t if n in ("mlp_w1", "mlp_w2") else t[0]

    parts = _all_gather_many([w[n].astype(BF16) for n in FIRST], "gather_first_weights")
    W = {n: full(n, t) for n, t in zip(FIRST, parts)}
    W["ab_w_in"] = jnp.concatenate([_pad_pa(W["ab_w_in"][:, :1824]), W["ab_w_in"][:, 1824:]], axis=1)

    ada_b_loc = lax.dynamic_slice(ada_b, (0, ncol * me), (2, ncol))
    mod_part = _ada_mod(c_all, ada_w, ada_b_loc)
    mod_all = _all_gather(mod_part.reshape(-1, 128), "gather_mod").reshape(NDEV, 2, NDEV, ncol)
    mod = lax.dynamic_index_in_dim(mod_all, me, axis=2, keepdims=False)
    mod = mod.transpose(1, 0, 2).reshape(2, 6 * D)

    behind = (mod[0, 0] * 0.0).astype(BF16)
    later = _exchange_start([w[n].astype(BF16) + (behind if n == LATER[0] else 0) for n in LATER], True,
                            "gather_later_weights_start")
    mod = mod + later[-1][0, 0]

    def later_weights(after):
        lands = _exchange_wait(later, True, after, "gather_later_weights_wait")
        return {n: full(n, t) for n, t in zip(LATER, lands)}

    sent = []

    def early_grads(G):
        sent.append(_exchange_start([_to_chunks(G[n], kinds[n]).astype(BF16) for n in LATER], False,
                                    "exchange_later_grads_start"))
        return sent[0][-1][0, 0]

    loss_part, grad_x, dmod, G = _local_step(x[0], loss_target[0], mod, W, P, later_weights, early_grads)
    G["ada_b"] = dmod
    loss = lax.psum(loss_part, ("x", "y", "c"))

    rep_shapes = [w[n].shape for n in REPLICATED]
    rep_all = _all_gather(_pack([G[n] for n in REPLICATED], F32, 8), "gather_replicated_grads")
    pk = lambda d: _pack([d[n] for n in REPLICATED], F32, 8)
    rep_out = _sum_adamw(rep_all, pk(w), pk(m), pk(v), "sum_adamw_replicated", rep_all.shape[1])
    rep_out = [dict(zip(REPLICATED, _unpack(o, rep_shapes))) for o in rep_out]

    dmod_all = _unpack(rep_all, [(2, 6 * D)], (NDEV,))[0]
    dmod_loc = lax.dynamic_slice(dmod_all, (0, 0, ncol * me), (NDEV, 2, ncol)).transpose(1, 0, 2)
    ada_out = _ada_grad_adamw(c_all.T, dmod_loc, ada_w, m_ada_w, v_ada_w)

    names = [n for n, _ in SHARDED if n not in GATHER_BF16]
    shard_shapes = [w[n].shape for n in names]
    chunks = _pack8([_to_chunks(G[n], kinds[n]) for n in names], F32, 8)
    recv = _all_to_all(chunks, "exchange_small_grads")
    pk = lambda d: _pack([d[n] for n in names], F32, 8)
    sh_out = _sum_adamw(recv, pk(w), pk(m), pk(v), "sum_adamw_small", recv.shape[1])
    sh_out = [dict(zip(names, _unpack(o, shard_shapes))) for o in sh_out]

    big_out = {}

    def update(n, contributions):
        cols = w[n].shape[-1]
        flat = lambda t: t.reshape(-1, cols)
        rows = flat(w[n]).shape[0]
        outs = _sum_adamw(contributions.reshape(-1, rows, cols), flat(w[n]), flat(m[n]), flat(v[n]),
                          f"sum_adamw_{n}", min(rows, 256))
        big_out[n] = [o.reshape(w[n].shape) for o in outs]

    ci = lax.axis_index("c")
    mine_l, sib_l = [], []
    for n in FIRST:
        g8 = _to_chunks(G[n], kinds[n])
        g42 = g8.reshape((4, 2) + g8.shape[1:])
        mine_l.append(lax.dynamic_index_in_dim(g42, ci, 1, keepdims=False))
        sib_l.append(lax.dynamic_index_in_dim(g42, 1 - ci, 1, keepdims=False))
    from_sib = _swap_sibling(sib_l, "swap_sibling_grads")

    def add2_body(r, pp):
        return [r[0] + r[1]], []

    partials = []
    for n, a, b in zip(FIRST, mine_l, from_sib):
        cols = a.shape[-1]
        (p,), _ = _rows(f"pair_sum_{n}", add2_body, [a.reshape(-1, cols), b.reshape(-1, cols)], [],
                        [(cols, BF16)], tm=512)
        partials.append(p.reshape(a.shape))
    for n, r in zip(FIRST, _exchange_chips(partials, "exchange_chip_grads")):
        update(n, r)

    for n, r in zip(LATER, _exchange_wait(sent[0], False, partials[0], "exchange_later_grads_wait")):
        update(n, r)
    sh_out = [{**d, **{n: big_out[n][i] for n in GATHER_BF16}} for i, d in enumerate(sh_out)]

    def pick(i, n):
        if n == "ada_w":
            return ada_out[i]
        return rep_out[i][n] if n in REPLICATED else sh_out[i][n]

    outs = [loss, grad_x[None]]
    for i in range(4):
        outs += [pick(i, n) for n in WEIGHTS]
    return tuple(outs)
```

```python
import functools
import math

import jax
import jax.numpy as jnp
from jax import lax
from jax.experimental import pallas as pl
from jax.experimental.pallas import tpu as pltpu

F32 = jnp.float32
BF16 = jnp.bfloat16
HI = lax.Precision.HIGHEST

NDEV = 8
T = 2048
D = 1024
DFF = 4096
HEADS = 8
HD = 64
RW = 512
PA = 2048
PB = 1536
PAB = PA + PB
QKV = 4608
DILS = (1, 4, 16)
BLK = 128
ALPHA = 4.0 ** 0.25
LN_EPS = 1e-5
GN_EPS = 64e-5
ADAM_LR, ADAM_B1, ADAM_B2, ADAM_EPS, ADAM_WD, ADAM_STEP = 0.001, 0.9, 0.999, 1e-8, 0.01, 10
VMEM_LIMIT = 56 * 1024 * 1024


def _cp(sem):
    return pltpu.CompilerParams(dimension_semantics=sem, vmem_limit_bytes=VMEM_LIMIT)


def _slot(px, py, pc):
    return 4 * px + 2 * py + pc


def _all_gather(x, name):
    R, C = x.shape

    def body(x_ref, out_ref, send_sems, recv_sems, local_sem):
        xi, yi, ci = lax.axis_index("x"), lax.axis_index("y"), lax.axis_index("c")
        me, sibling = (xi, yi, ci), (xi, yi, 1 - ci)
        chips = [(1 - xi, yi), (xi, 1 - yi), (1 - xi, 1 - yi)]

        def rows(px, py, pc):
            return out_ref.at[_slot(px, py, pc)]

        def copy(k, block, to, src=None):
            return pltpu.make_async_remote_copy(
                src_ref=rows(*block) if src is None else src, dst_ref=rows(*block),
                send_sem=send_sems.at[k], recv_sem=recv_sems.at[k],
                device_id=to, device_id_type=pl.DeviceIdType.MESH)

        mine = pltpu.make_async_copy(x_ref, rows(*me), local_sem)
        mine.start()
        first = [copy(0, me, sibling, src=x_ref)]
        first += [copy(1 + j, me, (*chip, ci), src=x_ref) for j, chip in enumerate(chips)]
        for cp in first:
            cp.start()
        passed = [copy(4 + j, (*chip, ci), sibling) for j, chip in enumerate(chips)]
        for j, chip in enumerate(chips):
            copy(1 + j, (*chip, ci), me).wait_recv()
            passed[j].start()
        copy(0, sibling, me).wait_recv()
        for j, chip in enumerate(chips):
            copy(4 + j, (*chip, 1 - ci), me).wait_recv()
        for cp in first + passed:
            cp.wait_send()
        mine.wait()

    return pl.pallas_call(
        body, name=name,
        out_shape=jax.ShapeDtypeStruct((NDEV, R, C), x.dtype),
        in_specs=[pl.BlockSpec(memory_space=pl.ANY)],
        out_specs=pl.BlockSpec(memory_space=pl.ANY),
        scratch_shapes=[pltpu.SemaphoreType.DMA((7,)), pltpu.SemaphoreType.DMA((7,)),
                        pltpu.SemaphoreType.DMA(())],
    )(x)


def _all_to_all(g, name):
    _, R, C = g.shape

    def body(g_ref, out_ref, send_sems, recv_sems, local_sem):
        xi, yi, ci = lax.axis_index("x"), lax.axis_index("y"), lax.axis_index("c")
        my_slot = _slot(xi, yi, ci)
        mine = pltpu.make_async_copy(g_ref.at[my_slot], out_ref.at[my_slot], local_sem)
        mine.start()
        copies = []
        for k in range(1, 8):
            px = 1 - xi if k & 4 else xi
            py = 1 - yi if k & 2 else yi
            pc = 1 - ci if k & 1 else ci
            peer_slot = _slot(px, py, pc)
            copies.append((
                pltpu.make_async_remote_copy(
                    src_ref=g_ref.at[peer_slot], dst_ref=out_ref.at[my_slot],
                    send_sem=send_sems.at[k - 1], recv_sem=recv_sems.at[k - 1],
                    device_id=(px, py, pc), device_id_type=pl.DeviceIdType.MESH),
                pltpu.make_async_remote_copy(
                    src_ref=g_ref.at[peer_slot], dst_ref=out_ref.at[peer_slot],
                    send_sem=send_sems.at[k - 1], recv_sem=recv_sems.at[k - 1],
                    device_id=(px, py, pc), device_id_type=pl.DeviceIdType.MESH)))
        for send, _ in copies:
            send.start()
        for _, recv in copies:
            recv.wait_recv()
        for send, _ in copies:
            send.wait_send()
        mine.wait()

    return pl.pallas_call(
        body, name=name,
        out_shape=jax.ShapeDtypeStruct((NDEV, R, C), g.dtype),
        in_specs=[pl.BlockSpec(memory_space=pl.ANY)],
        out_specs=pl.BlockSpec(memory_space=pl.ANY),
        scratch_shapes=[pltpu.SemaphoreType.DMA((7,)), pltpu.SemaphoreType.DMA((7,)),
                        pltpu.SemaphoreType.DMA(())],
    )(g)


def _my_slot():
    return _slot(lax.axis_index("x"), lax.axis_index("y"), lax.axis_index("c"))


def _put_own(buf, own, slot):
    return lax.dynamic_update_index_in_dim(buf, own, slot, 0)


def _hbm_call(body, name, ins, out_shapes, n_sems):
    anyspec = pl.BlockSpec(memory_space=pl.ANY)
    return pl.pallas_call(
        body, name=name, out_shape=out_shapes,
        in_specs=[anyspec] * len(ins), out_specs=[anyspec] * len(out_shapes),
        scratch_shapes=[pltpu.SemaphoreType.DMA(s) for s in n_sems],
    )(*ins)


def _all_gather_many(xs, name):
    n = len(xs)

    def body(*refs):
        x_refs, o_refs = refs[:n], refs[n:2 * n]
        send_sems, recv_sems = refs[2 * n:]
        xi, yi, ci = lax.axis_index("x"), lax.axis_index("y"), lax.axis_index("c")
        me, sibling = (xi, yi, ci), (xi, yi, 1 - ci)
        chips = [(1 - xi, yi), (xi, 1 - yi), (1 - xi, 1 - yi)]

        def copy(i, k, block, to, src=None):
            dst = o_refs[i].at[_slot(*block)]
            return pltpu.make_async_remote_copy(
                src_ref=dst if src is None else src, dst_ref=dst,
                send_sem=send_sems.at[i, k], recv_sem=recv_sems.at[i, k],
                device_id=to, device_id_type=pl.DeviceIdType.MESH)

        sends = []
        for i in range(n):
            sends += [copy(i, 1 + j, me, (*chip, ci), src=x_refs[i]) for j, chip in enumerate(chips)]
            sends.append(copy(i, 0, me, sibling, src=x_refs[i]))
        for cp in sends:
            cp.start()
        for j, chip in enumerate(chips):
            for i in range(n):
                copy(i, 1 + j, (*chip, ci), me).wait_recv()
                passed = copy(i, 4 + j, (*chip, ci), sibling)
                passed.start()
                sends.append(passed)
        for i in range(n):
            copy(i, 0, sibling, me).wait_recv()
            for j, chip in enumerate(chips):
                copy(i, 4 + j, (*chip, 1 - ci), me).wait_recv()
        for cp in sends:
            cp.wait_send()

    outs = _hbm_call(body, name, xs, [jax.ShapeDtypeStruct((NDEV,) + x.shape, x.dtype) for x in xs],
                     [(n, 7), (n, 7)])
    return [_put_own(o, x[None], _my_slot()) for o, x in zip(outs, xs)]


def _swap_sibling(gs, name):
    n = len(gs)

    def body(*refs):
        g_refs, o_refs = refs[:n], refs[n:2 * n]
        send_sems, recv_sems = refs[2 * n:]
        sibling = (lax.axis_index("x"), lax.axis_index("y"), 1 - lax.axis_index("c"))
        copies = [pltpu.make_async_remote_copy(
            src_ref=g_refs[i], dst_ref=o_refs[i], send_sem=send_sems.at[i], recv_sem=recv_sems.at[i],
            device_id=sibling, device_id_type=pl.DeviceIdType.MESH) for i in range(n)]
        for cp in copies:
            cp.start()
        for cp in copies:
            cp.wait_recv()
        for cp in copies:
            cp.wait_send()

    return _hbm_call(body, name, gs, [jax.ShapeDtypeStruct(g.shape, g.dtype) for g in gs], [(n,), (n,)])


def _exchange_chips(ps, name):
    n = len(ps)

    def body(*refs):
        p_refs, o_refs = refs[:n], refs[n:2 * n]
        send_sems, recv_sems = refs[2 * n:]
        xi, yi, ci = lax.axis_index("x"), lax.axis_index("y"), lax.axis_index("c")
        q_me = 2 * xi + yi
        sends, recvs = [], []
        for k in range(1, 4):
            px = 1 - xi if k & 2 else xi
            py = 1 - yi if k & 1 else yi
            q_peer = 2 * px + py
            for i in range(n):
                sends.append(pltpu.make_async_remote_copy(
                    src_ref=p_refs[i].at[q_peer], dst_ref=o_refs[i].at[q_me],
                    send_sem=send_sems.at[i, k - 1], recv_sem=recv_sems.at[i, k - 1],
                    device_id=(px, py, ci), device_id_type=pl.DeviceIdType.MESH))
                recvs.append(pltpu.make_async_remote_copy(
                    src_ref=p_refs[i].at[q_peer], dst_ref=o_refs[i].at[q_peer],
                    send_sem=send_sems.at[i, k - 1], recv_sem=recv_sems.at[i, k - 1],
                    device_id=(px, py, ci), device_id_type=pl.DeviceIdType.MESH))
        for cp in sends:
            cp.start()
        for cp in recvs:
            cp.wait_recv()
        for cp in sends:
            cp.wait_send()

    outs = _hbm_call(body, name, ps, [jax.ShapeDtypeStruct(p.shape, p.dtype) for p in ps], [(n, 3), (n, 3)])
    q_me = 2 * lax.axis_index("x") + lax.axis_index("y")
    return [_put_own(o, lax.dynamic_index_in_dim(p, q_me, 0, keepdims=True), q_me) for o, p in zip(outs, ps)]


def _peers(xi, yi, ci):
    return [(1 - xi if k & 4 else xi, 1 - yi if k & 2 else yi, 1 - ci if k & 1 else ci) for k in range(1, 8)]


def _direct_copy(src_refs, land_refs, send_sems, recv_sems, i, k, peer, my_slot, gather):
    src = src_refs[i] if gather else src_refs[i].at[_slot(*peer)]
    return pltpu.make_async_remote_copy(
        src_ref=src, dst_ref=land_refs[i].at[my_slot], send_sem=send_sems.at[7 * i + k], recv_sem=recv_sems.at[7 * i + k],
        device_id=peer, device_id_type=pl.DeviceIdType.MESH)


def _exchange_start(srcs, gather, name):
    n = len(srcs)
    lands = [lax.empty(((NDEV,) + s.shape) if gather else s.shape, s.dtype) for s in srcs]

    def body(*refs):
        s_refs, l_refs = refs[:n], refs[n:2 * n]
        send_sems, recv_sems = refs[2 * n], refs[2 * n + 1]
        token = refs[2 * n + 2 + 2 * n]
        xi, yi, ci = lax.axis_index("x"), lax.axis_index("y"), lax.axis_index("c")
        my_slot = _slot(xi, yi, ci)
        for k, peer in enumerate(_peers(xi, yi, ci)):
            for i in range(n):
                _direct_copy(s_refs, l_refs, send_sems, recv_sems, i, k, peer, my_slot, gather).start()
        token[...] = jnp.zeros_like(token)

    hbm = pl.BlockSpec(memory_space=pltpu.HBM)
    sem = pl.BlockSpec(memory_space=pltpu.SEMAPHORE)
    both = list(srcs) + lands
    return pl.pallas_call(
        body, name=name,
        out_shape=(pltpu.SemaphoreType.DMA((7 * n,)), pltpu.SemaphoreType.DMA((7 * n,)),
                   *[pltpu.HBM(t.shape, t.dtype) for t in both], jax.ShapeDtypeStruct((8, 128), F32)),
        in_specs=[hbm] * (2 * n),
        out_specs=(sem, sem, *[hbm] * (2 * n), pl.BlockSpec(memory_space=pltpu.VMEM)),
        input_output_aliases={i: 2 + i for i in range(2 * n)},
        compiler_params=pltpu.CompilerParams(has_side_effects=pltpu.SideEffectType.DATAFLOW_SIDE_EFFECTING),
    )(*[pltpu.with_memory_space_constraint(t, pltpu.HBM) for t in both])


def _exchange_wait(started, gather, after, name):
    send_sems, recv_sems, *thru, _ = started
    n = len(thru) // 2

    def body(*refs):
        s_refs, l_refs = refs[:n], refs[n:2 * n]
        send_sems, recv_sems = refs[2 * n], refs[2 * n + 1]
        xi, yi, ci = lax.axis_index("x"), lax.axis_index("y"), lax.axis_index("c")
        my_slot = _slot(xi, yi, ci)
        for k, peer in enumerate(_peers(xi, yi, ci)):
            for i in range(n):
                _direct_copy(s_refs, l_refs, send_sems, recv_sems, i, k, peer, my_slot, gather).wait_send()
                _direct_copy(s_refs, l_refs, send_sems, recv_sems, i, k, peer, _slot(*peer), gather).wait_recv()

    hbm = pl.BlockSpec(memory_space=pltpu.HBM)
    sem = pl.BlockSpec(memory_space=pltpu.SEMAPHORE)
    outs = pl.pallas_call(
        body, name=name,
        out_shape=tuple(pltpu.HBM(t.shape, t.dtype) for t in thru),
        in_specs=[hbm] * (2 * n) + [sem, sem, pl.BlockSpec(memory_space=pl.ANY)],
        out_specs=tuple([hbm] * (2 * n)),
        input_output_aliases={i: i for i in range(2 * n)},
        compiler_params=pltpu.CompilerParams(has_side_effects=pltpu.SideEffectType.DATAFLOW_SIDE_EFFECTING),
    )(*thru, send_sems, recv_sems, after)
    slot = _my_slot()
    own = [s[None] if gather else lax.dynamic_index_in_dim(s, slot, 0, keepdims=True) for s in outs[:n]]
    return [_put_own(land, o, slot) for land, o in zip(outs[n:], own)]


def _mm(name, a, b, tb=False, out=(F32,), epi=None, extras=(), tm=1024, tn=512, tk_cap=2048):
    M, K = a.shape
    N = b.shape[0] if tb else b.shape[1]
    tm, tn = min(tm, M), min(tn, N)
    tk = max(t for t in range(128, min(K, tk_cap) + 1, 128) if K % t == 0)
    assert M % tm == 0 and N % tn == 0 and K % tk == 0, (name, M, N, K)
    nk = K // tk
    ne, no = len(extras), len(out)
    dims = (((1,), (1 if tb else 0,)), ((), ()))

    def kern(*refs):
        a_ref, b_ref = refs[:2]
        e_refs = refs[2:2 + ne]
        o_refs = refs[2 + ne:2 + ne + no]

        def finish(acc):
            outs = epi(acc, *[e[...] for e in e_refs]) if epi is not None else (acc,)
            for o_ref, o in zip(o_refs, outs):
                o_ref[...] = o.astype(o_ref.dtype)

        part = lax.dot_general(a_ref[...], b_ref[...], dims, preferred_element_type=F32)
        if nk == 1:
            finish(part)
            return
        acc_ref = refs[-1]
        k = pl.program_id(2)

        @pl.when(k == 0)
        def _():
            acc_ref[...] = part

        @pl.when(k > 0)
        def _():
            acc_ref[...] += part

        @pl.when(k == nk - 1)
        def _():
            finish(acc_ref[...])

    b_spec = (pl.BlockSpec((tn, tk), lambda i, j, k: (j, k)) if tb
              else pl.BlockSpec((tk, tn), lambda i, j, k: (k, j)))
    tile = pl.BlockSpec((tm, tn), lambda i, j, k: (i, j))
    res = pl.pallas_call(
        kern, name=name, grid=(M // tm, N // tn, nk),
        in_specs=[pl.BlockSpec((tm, tk), lambda i, j, k: (i, k)), b_spec] + [tile] * ne,
        out_specs=[tile] * no,
        out_shape=[jax.ShapeDtypeStruct((M, N), dt) for dt in out],
        scratch_shapes=[pltpu.VMEM((tm, tn), F32)] if nk > 1 else [],
        compiler_params=_cp(("parallel", "parallel", "arbitrary")),
    )(a, b, *extras)
    return res[0] if no == 1 else res


def _rows(name, body, rows, params, out_rows, out_accs=(), tm=256):
    views = [r if isinstance(r, tuple) else (r, r.shape[1], 0) for r in rows]
    n = views[0][0].shape[0]
    assert n % tm == 0
    nr, npar, nor, noa = len(views), len(params), len(out_rows), len(out_accs)

    def kern(*refs):
        r_refs = refs[:nr]
        p_refs = refs[nr:nr + npar]
        o_refs = refs[nr + npar:nr + npar + nor]
        a_refs = refs[nr + npar + nor:]
        outs, accs = body([r[...] for r in r_refs], [p[...] for p in p_refs])
        assert len(outs) == nor and len(accs) == noa, (name, len(outs), len(accs))
        for o_ref, o in zip(o_refs, outs):
            o_ref[...] = o.astype(o_ref.dtype)
        if noa:
            @pl.when(pl.program_id(0) == 0)
            def _():
                for a_ref in a_refs:
                    a_ref[...] = jnp.zeros_like(a_ref)

            for a_ref, a in zip(a_refs, accs):
                a_ref[...] += a.astype(F32)

    def whole(shape):
        nd = len(shape)
        return pl.BlockSpec(tuple(shape), lambda i, nd=nd: (0,) * nd)

    in_specs = [pl.BlockSpec((tm, w), lambda i, cb=cb: (i, cb)) for _, w, cb in views]
    in_specs += [whole(p.shape) for p in params]
    out_specs = [pl.BlockSpec((tm, c), lambda i: (i, 0)) for c, _ in out_rows]
    out_specs += [whole(s) for s in out_accs]
    out_shape = [jax.ShapeDtypeStruct((n, c), dt) for c, dt in out_rows]
    out_shape += [jax.ShapeDtypeStruct(tuple(s), F32) for s in out_accs]
    res = pl.pallas_call(
        kern, name=name, grid=(n // tm,), in_specs=in_specs, out_specs=out_specs,
        out_shape=out_shape, compiler_params=_cp(("arbitrary",)),
    )(*[v[0] for v in views], *params)
    return res[:nor], res[nor:]


@jax.custom_vjp
def _headsum(x, e):
    return sum(jnp.dot(p, e, preferred_element_type=F32) for p in _split3(x))


_headsum.defvjp(lambda x, e: (_headsum(x, e), e), lambda e, ct: (_headsum(ct, e), None))


def _softplus(z):
    return jnp.maximum(z, 0.0) + jnp.log(1.0 + jnp.exp(jnp.minimum(z, -z)))


def _post_ln(x, y, g, lng, lnb):
    z = ALPHA * x + (1.0 + g) * y
    mu = jnp.mean(z, axis=-1, keepdims=True)
    zc = z - mu
    var = jnp.mean(zc * zc, axis=-1, keepdims=True)
    return zc * lax.rsqrt(var + LN_EPS) * lng + lnb


def _post_ln_mod(x, y, g, lng, lnb, scn, shn):
    xn = _post_ln(x, y, g, lng, lnb)
    return xn, xn * (1.0 + scn) + shn


def _pre_core(E, r_, k_, v_, wd_, ad_, gd_, r1, k1, v1, wd1, ad1, gd1, h, bg, cg, h1, cg1, h2, cg2,
              mu_r, mu_k, mu_v, mu_wd, mu_ad, mu_gd, w0, w_up, a0, a_up, g_up, k_k, k_a,
              cw0, cw1, cw2):
    def mix(x, x1, mu):
        return x + mu * (x1 - x)

    r, k, v = mix(r_, r1, mu_r), mix(k_, k1, mu_k), mix(v_, v1, mu_v)
    wd, ad, gd = mix(wd_, wd1, mu_wd), mix(ad_, ad1, mu_ad), mix(gd_, gd1, mu_gd)
    logw = -_softplus(-(w0 + jnp.dot(jnp.tanh(wd), w_up, preferred_element_type=F32))) - 0.5
    decay = jnp.exp(-jnp.exp(logw))
    iclr = jax.nn.sigmoid(a0 + jnp.dot(ad, a_up, preferred_element_type=F32))
    gate = jnp.dot(jax.nn.sigmoid(gd), g_up, preferred_element_type=F32)
    kk0 = k * k_k
    nrm = jnp.sqrt(_headsum(kk0 * kk0, E))
    kk = kk0 / jnp.maximum(nrm, 1e-12)
    kh = k * (1.0 + (iclr - 1.0) * k_a)
    yb = bg * (cw2 * (cg * h) + cw1 * (cg1 * h1) + cw0 * (cg2 * h2))
    return r, decay, kh, v, -kk, kk * iclr, gate, yb


def _post_core(E, y, r, kh, v, gate, lnx_g, lnx_b, rk):
    def seg(t):
        return _headsum(t, E)

    mean = seg(y) * (1.0 / HD)
    yc = y - mean
    var = seg(yc * yc) * (1.0 / HD)
    gn = yc * lax.rsqrt(var + GN_EPS) * lnx_g + lnx_b
    bonus = seg(r * kh * rk) * v
    return (gn + bonus) * gate


def _merge_core(o0, o1, o2, l0, l1, l2):
    m = jnp.maximum(jnp.maximum(l0, l1), l2)
    e0, e1, e2 = jnp.exp(l0 - m), jnp.exp(l1 - m), jnp.exp(l2 - m)
    den = e0 + e1 + e2
    return (e0 * o0 + e1 * o1 + e2 * o2) / den


CHUNK = 128
HALF = 64
HP = HEADS // 2
LW = 2 * HD
NCHUNK = T // CHUNK


def _split3(x):
    hi = x.astype(BF16)
    r1 = x - hi.astype(F32)
    mid = r1.astype(BF16)
    return hi, mid, (r1 - mid.astype(F32)).astype(BF16)


def _cols3(parts):
    tr = [p.reshape(NCHUNK, CHUNK, HP, 2, HD).transpose(2, 4, 0, 3, 1) for p in parts]
    return jnp.stack(tr, axis=4).reshape(HP, HD, 6 * T)


def _pick_codes():
    row = lax.broadcasted_iota(jnp.int32, (6 * CHUNK, LW), 0)
    col = lax.broadcasted_iota(jnp.int32, (6 * CHUNK, LW), 1)
    same = (row >= 3 * CHUNK) == (col >= HD)
    return jnp.where(same, row & (CHUNK - 1), -1).astype(BF16)


def _column(block_ref, codes, t):
    pick = jnp.where(codes == t.astype(BF16), jnp.ones((), BF16), jnp.zeros((), BF16))
    return jnp.dot(block_ref[...].reshape(HP * HD, 6 * CHUNK), pick, preferred_element_type=F32)


def _halfsums(x, row, left1):
    row_l = jnp.where(left1, row, 0.0)
    return (jnp.sum(x * row_l, axis=1, keepdims=True), jnp.sum(x * (row - row_l), axis=1, keepdims=True))


def _pair_rows(row):
    return [row[:, p * LW:(p + 1) * LW] for p in range(HP)]


def _store_columns(ref, p, t_mask, cols):
    for j, col in enumerate(cols):
        pltpu.store(ref.at[pl.ds(2 * p + j, 1)], jnp.broadcast_to(col[None], (1, HD, CHUNK)), mask=t_mask[None])


def _scan_fwd(r, w, k, a, b, v3):
    def kern(r_ref, w_ref, k_ref, a_ref, b_ref, v_ref, y_ref, ck_ref, s_ref, vb_ref):
        @pl.when(pl.program_id(0) == 0)
        def _():
            s_ref[...] = jnp.zeros_like(s_ref)

        lane = lax.broadcasted_iota(jnp.int32, (HD, CHUNK), 1)
        left = lane < HD
        left1 = lax.broadcasted_iota(jnp.int32, (1, LW), 1) < HD
        codes = _pick_codes()

        def step(t, carry):
            row = lambda ref: _pair_rows(ref[pl.ds(t, 1), :])
            S = [s_ref[p] for p in range(HP)]
            sa = [jnp.where(left, *_halfsums(s, a, left1)) for s, a in zip(S, row(a_ref))]
            S = [s * w + c * b + vb_ref[pl.ds(p * HD, HD), :] * k
                 for p, (s, w, c, b, k) in enumerate(zip(S, row(w_ref), sa, row(b_ref), row(k_ref)))]
            for p, s in enumerate(S):
                s_ref[p] = s
            for p, (s, r) in enumerate(zip(S, row(r_ref))):
                _store_columns(y_ref, p, lane == t, _halfsums(s, r, left1))
            vb_ref[...] = _column(v_ref, codes, t + 1)
            return carry

        for half in range(CHUNK // HALF):
            ck_ref[half] = s_ref[...]
            if half == 0:
                vb_ref[...] = _column(v_ref, codes, jnp.int32(0))
            lax.fori_loop(half * HALF, (half + 1) * HALF, step, 0, unroll=2)

    rowblk = pl.BlockSpec((CHUNK, RW), lambda c: (c, 0))
    return pl.pallas_call(
        kern, name="rwkv_scan_fwd", grid=(NCHUNK,),
        in_specs=[rowblk] * 5 + [pl.BlockSpec((HP, HD, 6 * CHUNK), lambda c: (0, 0, c))],
        out_specs=[pl.BlockSpec((HEADS, HD, CHUNK), lambda c: (0, 0, c)),
                   pl.BlockSpec((CHUNK // HALF, HP, HD, LW), lambda c: (c, 0, 0, 0))],
        out_shape=[jax.ShapeDtypeStruct((HEADS, HD, T), F32),
                   jax.ShapeDtypeStruct((T // HALF, HP, HD, LW), F32)],
        scratch_shapes=[pltpu.VMEM((HP, HD, LW), F32), pltpu.VMEM((HP * HD, LW), F32)],
        compiler_params=_cp(("arbitrary",)),
    )(r, w, k, a, b, v3)


def _scan_bwd(r, w, k, a, b, v3, dy3, ck):
    NC = T // CHUNK

    def kern(r_ref, w_ref, k_ref, a_ref, b_ref, v_ref, dy_ref, ck_ref,
             dr_ref, dw_ref, dk_ref, da_ref, db_ref, dv_ref, ds_ref, sb_ref, vb_ref, sa_ref, dyb_ref):
        @pl.when(pl.program_id(0) == 0)
        def _():
            ds_ref[...] = jnp.zeros_like(ds_ref)

        lane = lax.broadcasted_iota(jnp.int32, (HD, CHUNK), 1)
        left = lane < HD
        left1 = lax.broadcasted_iota(jnp.int32, (1, LW), 1) < HD
        codes = _pick_codes()

        def rowsum(x):
            return jnp.sum(x, axis=0, keepdims=True)

        for half in reversed(range(CHUNK // HALF)):
            base = half * HALF
            sb_ref[0] = ck_ref[half]

            vb_ref[0] = _column(v_ref, codes, jnp.int32(base))

            def replay(i, carry):
                t = base + i
                row = lambda ref: _pair_rows(ref[pl.ds(t, 1), :])
                S = [sb_ref[i, p] for p in range(HP)]
                sa = [jnp.where(left, *_halfsums(s, a, left1)) for s, a in zip(S, row(a_ref))]
                for p, (s, w, c, b, k) in enumerate(zip(S, row(w_ref), sa, row(b_ref), row(k_ref))):
                    sb_ref[i + 1, p] = s * w + c * b + vb_ref[i, pl.ds(p * HD, HD), :] * k
                    sa_ref[i, p] = c
                vb_ref[i + 1] = _column(v_ref, codes, t + 1)
                return carry

            lax.fori_loop(0, HALF, replay, 0, unroll=2)
            dyb_ref[...] = _column(dy_ref, codes, jnp.int32(base + HALF - 1))

            def back(ii, carry):
                i = HALF - 1 - ii
                t = base + i
                row = lambda ref: _pair_rows(ref[pl.ds(t, 1), :])
                a_r, b_r, k_r, w_r, r_r = row(a_ref), row(b_ref), row(k_ref), row(w_ref), row(r_ref)
                dys = [dyb_ref[pl.ds(p * HD, HD), :] for p in range(HP)]
                dyb_ref[...] = _column(dy_ref, codes, jnp.maximum(t - 1, 0))
                dr, dw, db, dk, da = [], [], [], [], []
                for p in range(HP):
                    Sp, dy = sb_ref[i, p], dys[p]
                    dS = ds_ref[p] + dy * r_r[p]
                    dr.append(rowsum(sb_ref[i + 1, p] * dy))
                    dw.append(rowsum(dS * Sp))
                    db.append(rowsum(dS * sa_ref[i, p]))
                    dk.append(rowsum(dS * vb_ref[i, pl.ds(p * HD, HD), :]))
                    dsa = jnp.where(left, *_halfsums(dS, b_r[p], left1))
                    _store_columns(dv_ref, p, lane == t, _halfsums(dS, k_r[p], left1))
                    da.append(rowsum(Sp * dsa))
                    ds_ref[p] = dS * w_r[p] + dsa * a_r[p]
                for ref, pieces in ((dr_ref, dr), (dw_ref, dw), (db_ref, db), (dk_ref, dk), (da_ref, da)):
                    ref[pl.ds(t, 1), :] = jnp.concatenate(pieces, axis=1)
                return carry

            lax.fori_loop(0, HALF, back, 0, unroll=2)

    rowblk = pl.BlockSpec((CHUNK, RW), lambda c: (NC - 1 - c, 0))
    col3blk = pl.BlockSpec((HP, HD, 6 * CHUNK), lambda c: (0, 0, NC - 1 - c))
    rowshape = jax.ShapeDtypeStruct((T, RW), F32)
    return pl.pallas_call(
        kern, name="rwkv_scan_bwd", grid=(NC,),
        in_specs=[rowblk] * 5 + [col3blk, col3blk,
                                 pl.BlockSpec((CHUNK // HALF, HP, HD, LW), lambda c: (NC - 1 - c, 0, 0, 0))],
        out_specs=[rowblk] * 5 + [pl.BlockSpec((HEADS, HD, CHUNK), lambda c: (0, 0, NC - 1 - c))],
        out_shape=[rowshape] * 5 + [jax.ShapeDtypeStruct((HEADS, HD, T), F32)],
        scratch_shapes=[pltpu.VMEM((HP, HD, LW), F32), pltpu.VMEM((HALF + 1, HP, HD, LW), F32),
                        pltpu.VMEM((HALF + 1, HP * HD, LW), F32), pltpu.VMEM((HALF, HP, HD, LW), F32),
                        pltpu.VMEM((HP * HD, LW), F32)],
        compiler_params=_cp(("arbitrary",)),
    )(r, w, k, a, b, v3, dy3, ck)


NBLK = T // BLK


def _blocks_per_segment(g):
    return jnp.where(g == 0, NBLK // DILS[0], jnp.where(g == 1, NBLK // DILS[1], NBLK // DILS[2]))


def _attn_fwd(q, kp, vp, bias):
    def kern(q_ref, k_ref, v_ref, b_ref, o_ref, l_ref):
        nbs = _blocks_per_segment(pl.program_id(0))
        qi = lax.broadcasted_iota(jnp.int32, (BLK, 2 * BLK), 0)
        ki = lax.broadcasted_iota(jnp.int32, (BLK, 2 * BLK), 1)
        band = (ki >= qi) & (ki <= qi + BLK)
        bias_t = b_ref[0, 0]
        for n in range(NBLK):
            lo = jnp.where((n & (nbs - 1)) == 0, BLK, 0)
            valid = band & (ki >= lo)
            qb = q_ref[0, 0, n * BLK:(n + 1) * BLK, :]
            kc = k_ref[0, 0, n * BLK:(n + 2) * BLK, :]
            vc = v_ref[0, 0, n * BLK:(n + 2) * BLK, :]
            s = lax.dot_general(qb, kc, (((1,), (1,)), ((), ())), preferred_element_type=F32)
            s = jnp.where(valid, s * (HD ** -0.5) + bias_t, -jnp.inf)
            m = jnp.max(s, axis=1, keepdims=True)
            e = jnp.exp(s - m)
            den = jnp.sum(e, axis=1, keepdims=True)
            pr = (e / den).astype(BF16)
            o_ref[0, 0, n * BLK:(n + 1) * BLK, :] = jnp.dot(pr, vc, preferred_element_type=F32)
            l_ref[0, 0, n * BLK:(n + 1) * BLK, :] = m + jnp.log(den)

    def blk(rows, cols):
        return pl.BlockSpec((1, 1, rows, cols), lambda g, h: (g, h, 0, 0))

    return pl.pallas_call(
        kern, name="attn_fwd", grid=(3, HEADS),
        in_specs=[blk(T, HD), blk(T + BLK, HD), blk(T + BLK, HD), blk(BLK, 2 * BLK)],
        out_specs=[blk(T, HD), blk(T, 1)],
        out_shape=[jax.ShapeDtypeStruct((3, HEADS, T, HD), F32),
                   jax.ShapeDtypeStruct((3, HEADS, T, 1), F32)],
        compiler_params=_cp(("parallel", "parallel")),
    )(q, kp, vp, bias)


def _attn_bwd(q, kp, vp, bias, biasT, do, lse_c, lse_r, dc_c, dc_r):
    def kern(q_ref, k_ref, v_ref, b_ref, bt_ref, do_ref, lc_ref, lr_ref, dcc_ref, dcr_ref,
             dq_ref, dk_ref, dv_ref, db_ref):
        nbs = _blocks_per_segment(pl.program_id(0))
        qi = lax.broadcasted_iota(jnp.int32, (BLK, 2 * BLK), 0)
        ki = lax.broadcasted_iota(jnp.int32, (BLK, 2 * BLK), 1)
        band = (ki >= qi) & (ki <= qi + BLK)
        kiT = lax.broadcasted_iota(jnp.int32, (2 * BLK, BLK), 0)
        qiT = lax.broadcasted_iota(jnp.int32, (2 * BLK, BLK), 1)
        bandT = (kiT >= qiT) & (kiT <= qiT + BLK)
        bias_t, biasT_t = b_ref[0, 0], bt_ref[0, 0]
        scale = HD ** -0.5
        dk_ref[...] = jnp.zeros_like(dk_ref)
        dv_ref[...] = jnp.zeros_like(dv_ref)
        db_ref[...] = jnp.zeros_like(db_ref)
        nt = (((1,), (1,)), ((), ()))
        for n in range(NBLK):
            lo = jnp.where((n & (nbs - 1)) == 0, BLK, 0)
            qs, ks = slice(n * BLK, (n + 1) * BLK), slice(n * BLK, (n + 2) * BLK)
            qb, kc, vc, dob = q_ref[0, 0, qs, :], k_ref[0, 0, ks, :], v_ref[0, 0, ks, :], do_ref[0, 0, qs, :]
            s = lax.dot_general(qb, kc, nt, preferred_element_type=F32) * scale + bias_t
            p = jnp.where(band & (ki >= lo), jnp.exp(s - lc_ref[0, 0, qs, :]), 0.0)
            dp = lax.dot_general(dob, vc, nt, preferred_element_type=F32)
            ds = p * (dp + dcc_ref[0, 0, qs, :])
            db_ref[0, 0] += ds
            dq_ref[0, 0, qs, :] = jnp.dot((ds * scale).astype(BF16), kc, preferred_element_type=F32)
            sT = lax.dot_general(kc, qb, nt, preferred_element_type=F32) * scale + biasT_t
            pT = jnp.where(bandT & (kiT >= lo), jnp.exp(sT - lr_ref[0, 0, :, qs]), 0.0)
            dpT = lax.dot_general(vc, dob, nt, preferred_element_type=F32)
            dsT = pT * (dpT + dcr_ref[0, 0, :, qs])
            dk_ref[0, 0, ks, :] += jnp.dot((dsT * scale).astype(BF16), qb, preferred_element_type=F32)
            dv_ref[0, 0, ks, :] += jnp.dot(pT.astype(BF16), dob, preferred_element_type=F32)

    def blk(rows, cols):
        return pl.BlockSpec((1, 1, rows, cols), lambda g, h: (g, h, 0, 0))

    return pl.pallas_call(
        kern, name="attn_bwd", grid=(3, HEADS),
        in_specs=[blk(T, HD), blk(T + BLK, HD), blk(T + BLK, HD), blk(BLK, 2 * BLK), blk(2 * BLK, BLK),
                  blk(T, HD), blk(T, 1), blk(1, T), blk(T, 1), blk(1, T)],
        out_specs=[blk(T, HD), blk(T + BLK, HD), blk(T + BLK, HD), blk(BLK, 2 * BLK)],
        out_shape=[jax.ShapeDtypeStruct((3, HEADS, T, HD), F32),
                   jax.ShapeDtypeStruct((3, HEADS, T + BLK, HD), F32),
                   jax.ShapeDtypeStruct((3, HEADS, T + BLK, HD), F32),
                   jax.ShapeDtypeStruct((3, HEADS, BLK, 2 * BLK), F32)],
        compiler_params=_cp(("parallel", "parallel")),
    )(q, kp, vp, bias, biasT, do, lse_c, lse_r, dc_c, dc_r)


NBUCKET = 32
NPAIR = BLK * 2 * BLK


def _relbias_table(rbT, onehotT):
    def kern(rb_ref, oh_ref, out_ref):
        out_ref[0] = sum(jnp.dot(p, oh_ref[0], preferred_element_type=F32) for p in _split3(rb_ref[0]))

    return pl.pallas_call(
        kern, name="relbias_table", grid=(3,),
        in_specs=[pl.BlockSpec((1, HEADS, NBUCKET), lambda g: (g, 0, 0)),
                  pl.BlockSpec((1, NBUCKET, NPAIR), lambda g: (g, 0, 0))],
        out_specs=pl.BlockSpec((1, HEADS, NPAIR), lambda g: (g, 0, 0)),
        out_shape=jax.ShapeDtypeStruct((3, HEADS, NPAIR), F32),
        compiler_params=_cp(("parallel",)),
    )(rbT, onehotT)


def _relbias_grad(db, onehotT):
    nt = (((1,), (1,)), ((), ()))

    def kern(db_ref, oh_ref, out_ref):
        hi, mid, _ = _split3(db_ref[0])
        out_ref[0] = (lax.dot_general(hi, oh_ref[0], nt, preferred_element_type=F32)
                      + lax.dot_general(mid, oh_ref[0], nt, preferred_element_type=F32))

    return pl.pallas_call(
        kern, name="relbias_grad", grid=(3,),
        in_specs=[pl.BlockSpec((1, HEADS, NPAIR), lambda g: (g, 0, 0)),
                  pl.BlockSpec((1, NBUCKET, NPAIR), lambda g: (g, 0, 0))],
        out_specs=pl.BlockSpec((1, HEADS, NBUCKET), lambda g: (g, 0, 0)),
        out_shape=jax.ShapeDtypeStruct((3, HEADS, NBUCKET), F32),
        compiler_params=_cp(("parallel",)),
    )(db, onehotT)


def _adamw(w, g, m, v):
    m2 = ADAM_B1 * m + (1.0 - ADAM_B1) * g
    v2 = ADAM_B2 * v + (1.0 - ADAM_B2) * (g * g)
    m_hat = m2 / (1.0 - ADAM_B1 ** ADAM_STEP)
    v_hat = v2 / (1.0 - ADAM_B2 ** ADAM_STEP)
    return -ADAM_LR * (m_hat / (jnp.sqrt(v_hat) + ADAM_EPS) + ADAM_WD * w), m2, v2


def _ada_mod(c_all, ada_w, ada_b_loc):
    def kern(c_ref, w_ref, b_ref, o_ref):
        c = c_ref[...]
        cond = c * jax.nn.sigmoid(c)
        o_ref[0] = jnp.dot(cond, w_ref[0], precision=HI, preferred_element_type=F32) + b_ref[0]

    ncol = ada_w.shape[2]
    return pl.pallas_call(
        kern, name="ada_mod", grid=(2,),
        in_specs=[pl.BlockSpec((NDEV, D), lambda i: (0, 0)),
                  pl.BlockSpec((1, D, ncol), lambda i: (i, 0, 0)),
                  pl.BlockSpec((1, 1, ncol), lambda i: (i, 0, 0))],
        out_specs=pl.BlockSpec((1, NDEV, ncol), lambda i: (i, 0, 0)),
        out_shape=jax.ShapeDtypeStruct((2, NDEV, ncol), F32),
        compiler_params=_cp(("parallel",)),
    )(c_all, ada_w, ada_b_loc.reshape(2, 1, ncol))


def _ada_grad_adamw(cT_all, dmod_loc, w, m, v):
    ncol = w.shape[2]
    tr = 256

    def kern(c_ref, d_ref, w_ref, m_ref, v_ref, g_ref, dl_ref, m2_ref, v2_ref):
        c = c_ref[...]
        cond = c * jax.nn.sigmoid(c)
        g = jnp.dot(cond, d_ref[0], precision=HI, preferred_element_type=F32)
        dl, m2, v2 = _adamw(w_ref[0], g, m_ref[0], v_ref[0])
        g_ref[0], dl_ref[0], m2_ref[0], v2_ref[0] = g, dl, m2, v2

    big = pl.BlockSpec((1, tr, ncol), lambda i, j: (i, j, 0))
    shp = jax.ShapeDtypeStruct(w.shape, F32)
    return pl.pallas_call(
        kern, name="ada_grad_adamw", grid=(2, D // tr),
        in_specs=[pl.BlockSpec((tr, NDEV), lambda i, j: (j, 0)),
                  pl.BlockSpec((1, NDEV, ncol), lambda i, j: (i, 0, 0)), big, big, big],
        out_specs=[big] * 4, out_shape=[shp] * 4,
        compiler_params=_cp(("parallel", "parallel")),
    )(cT_all, dmod_loc, w, m, v)


def _sum_adamw(recv, w, m, v, name, tr):
    S = recv.shape[0]
    R, C = w.shape
    assert R % tr == 0 and recv.shape[1:] == (R, C)

    def kern(r_ref, w_ref, m_ref, v_ref, g_ref, dl_ref, m2_ref, v2_ref):
        g = r_ref[0].astype(F32)
        for s in range(1, S):
            g = g + r_ref[s].astype(F32)
        dl, m2, v2 = _adamw(w_ref[...], g, m_ref[...], v_ref[...])
        g_ref[...], dl_ref[...], m2_ref[...], v2_ref[...] = g, dl, m2, v2

    flat = pl.BlockSpec((tr, C), lambda i: (i, 0))
    shp = jax.ShapeDtypeStruct((R, C), F32)
    return pl.pallas_call(
        kern, name=name, grid=(R // tr,),
        in_specs=[pl.BlockSpec((S, tr, C), lambda i: (0, i, 0)), flat, flat, flat],
        out_specs=[flat] * 4, out_shape=[shp] * 4,
        compiler_params=_cp(("parallel",)),
    )(recv, w, m, v)


def _pack(arrs, dtype, row_mult):
    flat = jnp.concatenate([a.reshape(-1).astype(dtype) for a in arrs])
    flat = jnp.pad(flat, (0, -flat.shape[0] % (128 * row_mult)))
    return flat.reshape(-1, 128)


def _pack8(arrs, dtype, row_mult):
    flat = jnp.concatenate([a.reshape(NDEV, -1).astype(dtype) for a in arrs], axis=1)
    flat = jnp.pad(flat, ((0, 0), (0, -flat.shape[1] % (128 * row_mult))))
    return flat.reshape(NDEV, -1, 128)


def _unpack(buf, shapes, lead=()):
    flat = buf.reshape(lead + (-1,))
    out, off = [], 0
    for s in shapes:
        n = math.prod(s)
        out.append(flat[..., off:off + n].reshape(lead + tuple(s)))
        off += n
    return out


def _to_chunks(full, kind):
    if kind == "col":
        x = full.reshape(full.shape[:-1] + (NDEV, full.shape[-1] // NDEV))
        return jnp.moveaxis(x, -2, 0)
    x = full.reshape(full.shape[:-2] + (NDEV, full.shape[-2] // NDEV, full.shape[-1]))
    return jnp.moveaxis(x, -3, 0)


def _from_chunks(g8, kind):
    if kind == "col":
        x = jnp.moveaxis(g8, 0, -2)
        return x.reshape(x.shape[:-2] + (x.shape[-2] * x.shape[-1],))
    x = jnp.moveaxis(g8, 0, -3)
    return x.reshape(x.shape[:-3] + (x.shape[-3] * x.shape[-2], x.shape[-1]))


def _pad_pa(x):
    z = lambda n: jnp.zeros(x.shape[:-1] + (n,), x.dtype)
    return jnp.concatenate([x[..., :1600], z(64), x[..., 1600:1664], z(64), x[..., 1664:1824], z(96)], -1)


def _unpad_pa(x):
    return jnp.concatenate([x[..., :1600], x[..., 1664:1728], x[..., 1792:1952]], -1)


def _pad_rows(x, n):
    return jnp.pad(x, ((0, n - x.shape[0]), (0, 0)))


def _shift_down(x, n):
    return jnp.pad(x, ((n, 0), (0, 0)))[:-n]


def _shift_up(x, n):
    return jnp.pad(x, ((0, n), (0, 0)))[n:]


def _unheadsT(x):
    return x.transpose(2, 0, 1).reshape(T, RW)


def _perm(x, dil):
    C = x.shape[-1]
    return x.reshape(T // dil, dil, HEADS, C).transpose(2, 1, 0, 3).reshape(HEADS, T, C)


def _unperm(y, dil):
    C = y.shape[-1]
    return y.reshape(HEADS, dil, T // dil, C).transpose(2, 1, 0, 3).reshape(T, HEADS, C)


def _bucket_tables():
    qi = jnp.arange(BLK)[:, None]
    ki = jnp.arange(2 * BLK)[None, :]
    rel = BLK + qi - ki
    tabs = []
    for dil in DILS:
        dist = jnp.clip(rel, 0, BLK) * dil
        logd = jnp.log(jnp.maximum(dist, 1).astype(F32) / 16) / math.log(2048 / 16)
        large = jnp.minimum(16 + (logd * 16).astype(jnp.int32), 31)
        tabs.append(jnp.where(dist < 16, dist, large))
    return jnp.stack(tabs)


SHARDED = (("ln_g", "col"), ("ln_b", "col"), ("ab_w_in", "col"), ("rw_w_up", "col"), ("rw_a_up", "col"),
           ("rw_g_up", "col"), ("sc_conv_w", "col"), ("ab_w_out", "row"), ("dil_w_qkv", "col"),
           ("dil_w_out", "col"), ("mlp_w1", "col"), ("mlp_w2", "row"))
FIRST = ("ab_w_in", "ab_w_out")
LATER = ("dil_w_qkv", "dil_w_out", "mlp_w1", "mlp_w2")
GATHER_BF16 = FIRST + LATER
GATHER_F32 = ("rw_w_up", "rw_a_up", "rw_g_up", "sc_conv_w", "ln_g", "ln_b")
REPLICATED = ("ada_b", "rw_mu", "rw_w0", "rw_a0", "rw_k_k", "rw_k_a", "rw_r_k", "rw_lnx_g", "rw_lnx_b", "rel_bias")
WEIGHTS = ("ada_w", "ada_b", "ln_g", "ln_b", "ab_w_in", "rw_mu", "rw_w0", "rw_w_up", "rw_a0", "rw_a_up",
           "rw_g_up", "rw_k_k", "rw_k_a", "rw_r_k", "rw_lnx_g", "rw_lnx_b", "sc_conv_w", "ab_w_out",
           "dil_w_qkv", "dil_w_out", "rel_bias", "mlp_w1", "mlp_w2")
FLAT_TILE = 512


def _local_step(x0, tgt, mod, W, P, later_weights, early_grads):
    row = lambda a: a.reshape(1, -1)
    W = dict(W)
    m6 = mod.reshape(2, 6, 1, D)
    sc = [m6[0, 1], m6[0, 4], m6[1, 1], m6[1, 4]]
    sh = [m6[0, 0], m6[0, 3], m6[1, 0], m6[1, 3]]
    gt = [m6[0, 2], m6[0, 5], m6[1, 2], m6[1, 5]]
    lng = [row(P["ln_g"][0, 0]), row(P["ln_g"][0, 1]), row(P["ln_g"][1, 0]), row(P["ln_g"][1, 1])]
    lnb = [row(P["ln_b"][0, 0]), row(P["ln_b"][0, 1]), row(P["ln_b"][1, 0]), row(P["ln_b"][1, 1])]
    E = jnp.kron(jnp.eye(HEADS, dtype=BF16), jnp.ones((HD, HD), BF16))

    def mod_body(r, p):
        return [r[0] * (1.0 + p[0]) + p[1]], []

    (u0,), _ = _rows("modulate", mod_body, [x0], [sc[0], sh[0]], [(D, BF16)])

    def post_fwd_body(r, p):
        xn, un = _post_ln_mod(r[0], r[1], *p)
        return [xn, un], []

    def post_fwd(s, x, y):
        (xn, un), _ = _rows(f"post_ln_{s}", post_fwd_body, [x, y],
                            [gt[s], lng[s], lnb[s], sc[s + 1], sh[s + 1]], [(D, F32), (D, BF16)])
        return xn, un

    def relu2(acc):
        a = jnp.maximum(acc, 0.0)
        return acc, a * a

    def relu2_bwd(acc, h):
        return (acc * (2.0 * jnp.maximum(h, 0.0)),)

    p = _mm("ab_in", u0, W["ab_w_in"])
    p1 = _shift_down(p, 1)
    p2 = _shift_down(p[:, PA:], 2)
    mu = _pad_pa(P["rw_mu"])
    mu_parts = [mu[:, :512], mu[:, 512:1024], mu[:, 1024:1536], mu[:, 1536:1664], mu[:, 1664:1792], mu[:, 1792:]]
    pre_params = mu_parts + [P["rw_w0"], _pad_rows(P["rw_w_up"], 128), P["rw_a0"], _pad_rows(P["rw_a_up"], 128),
                             _pad_rows(P["rw_g_up"], 256), P["rw_k_k"], P["rw_k_a"],
                             P["sc_conv_w"][0:1], P["sc_conv_w"][1:2], P["sc_conv_w"][2:3]]
    pre_rows = [(p, 512, 0), (p, 512, 1), (p, 512, 2), (p, 128, 12), (p, 128, 13), (p, 256, 7),
                (p1, 512, 0), (p1, 512, 1), (p1, 512, 2), (p1, 128, 12), (p1, 128, 13), (p1, 256, 7),
                (p, 512, 4), (p, 512, 5), (p, 512, 6), (p1, 512, 4), (p1, 512, 6), (p2, 512, 0), (p2, 512, 2)]
    NPR = len(pre_rows)

    def pre_fwd_body(r, pp):
        outs = list(_pre_core(pp[0], *r, *pp[1:]))
        return outs + list(_split3(outs[3])), []

    (r_, w_, kh_, v_, a_, b_, gate_, yb, *v_parts), _ = _rows(
        "rwkv_pre", pre_fwd_body, pre_rows, [E] + pre_params,
        [(RW, F32)] * 7 + [(RW, BF16)] * 4, tm=256)
    scan_in = [r_, w_, kh_, a_, b_, _cols3(v_parts)]
    yT, ck = _scan_fwd(*scan_in)
    ysc = _unheadsT(yT)
    post_params = [P["rw_lnx_g"], P["rw_lnx_b"], P["rw_r_k"].reshape(1, RW)]

    def postmix_fwd_body(r, pp):
        return [_post_core(pp[0], *r, *pp[1:])], []

    (ya,), _ = _rows("rwkv_post", postmix_fwd_body, [ysc, r_, kh_, v_, gate_], [E] + post_params,
                     [(RW, BF16)], tm=256)
    cat = jnp.concatenate([ya, yb], axis=1)
    y0 = _mm("ab_out", cat, W["ab_w_out"])
    x1, u1 = post_fwd(0, x0, y0)
    W.update(later_weights(u1))

    h1, a1 = _mm("mlp1_up_0", u1, W["mlp_w1"][0], out=(F32, BF16), epi=relu2)
    y1 = _mm("mlp1_down_0", a1, W["mlp_w2"][0])
    x2, u2 = post_fwd(1, x1, y1)

    pq = _mm("qkv", u2, W["dil_w_qkv"], out=(BF16,))
    pq5 = pq.reshape(T, 3, 3, HEADS, HD)
    q = jnp.stack([_perm(pq5[:, g, 0], DILS[g]) for g in range(3)])
    kp = jnp.pad(jnp.stack([_perm(pq5[:, g, 1], DILS[g]) for g in range(3)]), ((0, 0), (0, 0), (BLK, 0), (0, 0)))
    vp = jnp.pad(jnp.stack([_perm(pq5[:, g, 2], DILS[g]) for g in range(3)]), ((0, 0), (0, 0), (BLK, 0), (0, 0)))
    onehotT = (_bucket_tables().reshape(3, 1, NPAIR) == jnp.arange(NBUCKET).reshape(1, NBUCKET, 1)).astype(BF16)
    rbT = P["rel_bias"].reshape(NBUCKET, 3, HEADS).transpose(1, 2, 0)
    bias = _relbias_table(rbT, onehotT).reshape(3, HEADS, BLK, 2 * BLK)
    og, lse = _attn_fwd(q, kp, vp, bias)
    R = T * HEADS
    o_nat = [_unperm(og[g], DILS[g]).reshape(R, HD) for g in range(3)]
    l_nat = [_unperm(lse[g], DILS[g]).reshape(R, 1) for g in range(3)]

    def merge_fwd_body(r, pp):
        return [_merge_core(*r)], []

    (om,), _ = _rows("attn_merge", merge_fwd_body, o_nat + l_nat, [], [(HD, BF16)], tm=1024)
    om = om.reshape(T, RW)
    y2 = _mm("dil_out", om, W["dil_w_out"])
    x3, u3 = post_fwd(2, x2, y2)

    h3, a3 = _mm("mlp1_up_1", u3, W["mlp_w1"][1], out=(F32, BF16), epi=relu2)
    y3 = _mm("mlp1_down_1", a3, W["mlp_w2"][1])

    def last_body(r, pp):
        x, y, tg = r
        xn, vjp = jax.vjp(_post_ln, x, y, *pp)
        err = xn - tg
        dx, dy, dg, dlg, dlb = vjp(err * (1.0 / D))
        loss = jnp.full((1, 128), (0.5 / D) * jnp.sum(err * err), F32)
        return [dx, dy], [loss, dg, dlg, dlb]

    (dxp, dy3), (loss_acc, dg3, dlng3, dlnb3) = _rows(
        "final_ln_loss", last_body, [x3, y3, tgt], [gt[3], lng[3], lnb[3]],
        [(D, F32), (D, BF16)], [(1, 128), (1, D), (1, D), (1, D)])

    G = {}
    dsc, dsh, dgt = [None] * 4, [None] * 4, [None] * 4
    dlng, dlnb = [None] * 4, [None] * 4
    dgt[3], dlng[3], dlnb[3] = dg3, dlng3, dlnb3

    def mlp_bwd(i, u, h, a, dy):
        dh = _mm(f"mlp_dh_{i}", dy, W["mlp_w2"][i], tb=True, out=(BF16,), epi=relu2_bwd, extras=(h,))
        gw2 = _mm(f"mlp_dw2_{i}", a.T, dy)
        du = _mm(f"mlp_du_{i}", dh, W["mlp_w1"][i], tb=True)
        gw1 = _mm(f"mlp_dw1_{i}", u.T, dh)
        return du, gw1, gw2

    def post_bwd_body(r, pp):
        x, y, dxn, dun = r
        _, vjp = jax.vjp(_post_ln_mod, x, y, *pp)
        dx, dy, dg, dlg, dlb, dscn, dshn = vjp((dxn, dun))
        return [dx, dy], [dg, dlg, dlb, dscn, dshn]

    def post_bwd(s, x, y, dxn, dun):
        (dx, dy), (dgt[s], dlng[s], dlnb[s], dsc[s + 1], dsh[s + 1]) = _rows(
            f"post_ln_bwd_{s}", post_bwd_body, [x, y, dxn, dun],
            [gt[s], lng[s], lnb[s], sc[s + 1], sh[s + 1]], [(D, F32), (D, BF16)], [(1, D)] * 5)
        return dx, dy

    du3, gw1_1, gw2_1 = mlp_bwd(1, u3, h3, a3, dy3)
    dxp, dy2 = post_bwd(2, x2, y2, dxp, du3)

    G["dil_w_out"] = _mm("dil_out_dw", om.T, dy2)[None]
    do = _mm("dil_out_dx", dy2, W["dil_w_out"], tb=True).reshape(R, HD)

    def merge_bwd_body(r, pp):
        o_l, dout = r[:6], r[6]
        _, vjp = jax.vjp(_merge_core, *o_l)
        d = vjp(dout)
        dcs = [d[3 + g] - jnp.sum(d[g] * o_l[g], axis=1, keepdims=True) for g in range(3)]
        return list(d[:3]) + dcs, []

    mb, _ = _rows("attn_merge_bwd", merge_bwd_body, o_nat + l_nat + [do], [],
                  [(HD, BF16)] * 3 + [(1, F32)] * 3, tm=1024)
    dog = jnp.stack([_perm(mb[g].reshape(T, HEADS, HD), DILS[g]) for g in range(3)])
    dcc = jnp.stack([_perm(mb[3 + g].reshape(T, HEADS, 1), DILS[g]) for g in range(3)])
    dq, dkp, dvp, dbias = _attn_bwd(q, kp, vp, bias, jnp.swapaxes(bias, 2, 3), dog, lse,
                                    lse.reshape(3, HEADS, 1, T), dcc, dcc.reshape(3, HEADS, 1, T))
    dpq = jnp.concatenate(
        [_unperm(t[g], DILS[g]).reshape(T, RW) for g in range(3) for t in (dq, dkp[:, :, BLK:], dvp[:, :, BLK:])],
        axis=1).astype(BF16)
    rb = _relbias_grad(dbias.reshape(3, HEADS, NPAIR), onehotT)
    G["rel_bias"] = rb.transpose(2, 0, 1).reshape(NBUCKET, 3 * HEADS)
    G["dil_w_qkv"] = _mm("qkv_dw", u2.T, dpq)[None]
    du2 = _mm("qkv_dx", dpq, W["dil_w_qkv"], tb=True)
    dxp, dy1 = post_bwd(1, x1, y1, dxp, du2)

    du1, gw1_0, gw2_0 = mlp_bwd(0, u1, h1, a1, dy1)
    G["mlp_w1"] = jnp.stack([gw1_0, gw1_1])
    G["mlp_w2"] = jnp.stack([gw2_0, gw2_1])
    gt[0] = gt[0] + early_grads(G)
    dxp, dy0 = post_bwd(0, x0, y0, dxp, du1)

    G["ab_w_out"] = _mm("ab_out_dw", cat.T, dy0)[None]
    dcat = _mm("ab_out_dx", dy0, W["ab_w_out"], tb=True)

    def postmix_bwd_body(r, pp):
        _, vjp = jax.vjp(functools.partial(_post_core, pp[0]), *r[:5], *pp[1:])
        d = vjp(r[5])
        return list(_split3(d[0])) + list(d[1:5]), list(d[5:])

    (*dy_parts, dr1, dkh1, dv1, dgate), (G["rw_lnx_g"], G["rw_lnx_b"], drk) = _rows(
        "rwkv_post_bwd", postmix_bwd_body, [ysc, r_, kh_, v_, gate_, (dcat, 512, 0)], [E] + post_params,
        [(RW, BF16)] * 3 + [(RW, F32)] * 4, [(1, RW)] * 3, tm=256)
    G["rw_r_k"] = drk.reshape(1, HEADS, HD)
    dr2, dw2, dk2, da2, db2, dvT = _scan_bwd(*scan_in, _cols3(dy_parts), ck)
    dv2 = _unheadsT(dvT)

    def pre_bwd_body(r, pp):
        prim, ct = r[:NPR], r[NPR:]
        _, vjp = jax.vjp(functools.partial(_pre_core, pp[0]), *prim, *pp[1:])
        cts = (ct[0] + ct[1], ct[2], ct[3] + ct[4], ct[5] + ct[6], ct[7], ct[8], ct[9], ct[10])
        d = vjp(cts)
        z = jnp.zeros_like(d[12])
        dp = jnp.concatenate([d[0], d[1], d[2], d[3], d[4], d[5], d[12], d[13], d[14]], axis=1)
        dp1 = jnp.concatenate([d[6], d[7], d[8], d[9], d[10], d[11], d[15], z, d[16]], axis=1)
        dp2 = jnp.concatenate([d[17], z, d[18]], axis=1)
        return [dp, dp1, dp2], list(d[NPR:])

    acc_shapes = [a.shape for a in pre_params]
    (dp, dp1, dp2), pacc = _rows(
        "rwkv_pre_bwd", pre_bwd_body,
        pre_rows + [dr1, dr2, dw2, dkh1, dk2, dv1, dv2, da2, db2, dgate, (dcat, 512, 1)],
        [E] + pre_params, [(PAB, F32), (PAB, F32), (PB, F32)], acc_shapes, tm=256)
    G["rw_mu"] = _unpad_pa(jnp.concatenate(pacc[:6], axis=1))
    G["rw_w0"], G["rw_a0"], G["rw_k_k"], G["rw_k_a"] = pacc[6], pacc[8], pacc[11], pacc[12]
    G["rw_w_up"] = pacc[7][None, :64]
    G["rw_a_up"] = pacc[9][None, :64]
    G["rw_g_up"] = pacc[10][None, :160]
    G["sc_conv_w"] = jnp.concatenate(pacc[13:16], axis=0)[None]

    def add3_body(r, pp):
        return [r[0] + r[1] + r[2]], []

    (dpt,), _ = _rows("shift_merge", add3_body,
                      [dp, _shift_up(dp1, 1), jnp.pad(_shift_up(dp2, 2), ((0, 0), (PA, 0)))], [], [(PAB, BF16)])
    gin = _mm("ab_in_dw", u0.T, dpt)
    G["ab_w_in"] = jnp.concatenate([_unpad_pa(gin[:, :PA]), gin[:, PA:]], axis=1)[None]
    du0 = _mm("ab_in_dx", dpt, W["ab_w_in"], tb=True)

    def mod_bwd_body(r, pp):
        du, dx, x = r
        return [dx + du * (1.0 + pp[0])], [jnp.sum(du * x, axis=0, keepdims=True), jnp.sum(du, axis=0, keepdims=True)]

    (grad_x,), (dsc[0], dsh[0]) = _rows("modulate_bwd", mod_bwd_body, [du0, dxp, x0], [sc[0]], [(D, F32)],
                                        [(1, D), (1, D)])

    G["ln_g"] = jnp.concatenate(dlng, axis=0).reshape(2, 2, D)
    G["ln_b"] = jnp.concatenate(dlnb, axis=0).reshape(2, 2, D)
    dmod = jnp.concatenate([dsh[0], dsc[0], dgt[0], dsh[1], dsc[1], dgt[1],
                            dsh[2], dsc[2], dgt[2], dsh[3], dsc[3], dgt[3]], axis=1).reshape(2, 6 * D)
    return loss_acc[0, 0], grad_x, dmod, G


def kernel(x, c, ada_w, ada_b, ln_g, ln_b, ab_w_in, rw_mu, rw_w0, rw_w_up, rw_a0, rw_a_up, rw_g_up, rw_k_k, rw_k_a, rw_r_k, rw_lnx_g, rw_lnx_b, sc_conv_w, ab_w_out, dil_w_qkv, dil_w_out, rel_bias, mlp_w1, mlp_w2, loss_target, m_ada_w, m_ada_b, m_ln_g, m_ln_b, m_ab_w_in, m_rw_mu, m_rw_w0, m_rw_w_up, m_rw_a0, m_rw_a_up, m_rw_g_up, m_rw_k_k, m_rw_k_a, m_rw_r_k, m_rw_lnx_g, m_rw_lnx_b, m_sc_conv_w, m_ab_w_out, m_dil_w_qkv, m_dil_w_out, m_rel_bias, m_mlp_w1, m_mlp_w2, v_ada_w, v_ada_b, v_ln_g, v_ln_b, v_ab_w_in, v_rw_mu, v_rw_w0, v_rw_w_up, v_rw_a0, v_rw_a_up, v_rw_g_up, v_rw_k_k, v_rw_k_a, v_rw_r_k, v_rw_lnx_g, v_rw_lnx_b, v_sc_conv_w, v_ab_w_out, v_dil_w_qkv, v_dil_w_out, v_rel_bias, v_mlp_w1, v_mlp_w2):
    w = dict(ada_w=ada_w, ada_b=ada_b, ln_g=ln_g, ln_b=ln_b, ab_w_in=ab_w_in, rw_mu=rw_mu, rw_w0=rw_w0,
             rw_w_up=rw_w_up, rw_a0=rw_a0, rw_a_up=rw_a_up, rw_g_up=rw_g_up, rw_k_k=rw_k_k, rw_k_a=rw_k_a,
             rw_r_k=rw_r_k, rw_lnx_g=rw_lnx_g, rw_lnx_b=rw_lnx_b, sc_conv_w=sc_conv_w, ab_w_out=ab_w_out,
             dil_w_qkv=dil_w_qkv, dil_w_out=dil_w_out, rel_bias=rel_bias, mlp_w1=mlp_w1, mlp_w2=mlp_w2)
    m = dict(ada_w=m_ada_w, ada_b=m_ada_b, ln_g=m_ln_g, ln_b=m_ln_b, ab_w_in=m_ab_w_in, rw_mu=m_rw_mu,
             rw_w0=m_rw_w0, rw_w_up=m_rw_w_up, rw_a0=m_rw_a0, rw_a_up=m_rw_a_up, rw_g_up=m_rw_g_up,
             rw_k_k=m_rw_k_k, rw_k_a=m_rw_k_a, rw_r_k=m_rw_r_k, rw_lnx_g=m_rw_lnx_g, rw_lnx_b=m_rw_lnx_b,
             sc_conv_w=m_sc_conv_w, ab_w_out=m_ab_w_out, dil_w_qkv=m_dil_w_qkv, dil_w_out=m_dil_w_out,
             rel_bias=m_rel_bias, mlp_w1=m_mlp_w1, mlp_w2=m_mlp_w2)
    v = dict(ada_w=v_ada_w, ada_b=v_ada_b, ln_g=v_ln_g, ln_b=v_ln_b, ab_w_in=v_ab_w_in, rw_mu=v_rw_mu,
             rw_w0=v_rw_w0, rw_w_up=v_rw_w_up, rw_a0=v_rw_a0, rw_a_up=v_rw_a_up, rw_g_up=v_rw_g_up,
             rw_k_k=v_rw_k_k, rw_k_a=v_rw_k_a, rw_r_k=v_rw_r_k, rw_lnx_g=v_rw_lnx_g, rw_lnx_b=v_rw_lnx_b,
             sc_conv_w=v_sc_conv_w, ab_w_out=v_ab_w_out, dil_w_qkv=v_dil_w_qkv, dil_w_out=v_dil_w_out,
             rel_bias=v_rel_bias, mlp_w1=v_mlp_w1, mlp_w2=v_mlp_w2)
    kinds = dict(SHARDED)
    me = 4 * lax.axis_index("x") + 2 * lax.axis_index("y") + lax.axis_index("c")
    ncol = ada_w.shape[2]

    small = _all_gather(_pack([c] + [w[n] for n in GATHER_F32], F32, 8), "gather_small")
    parts = _unpack(small, [c.shape] + [w[n].shape for n in GATHER_F32], (NDEV,))
    c_all = parts[0].reshape(NDEV, D)
    P = {n: _from_chunks(t, kinds[n]) for n, t in zip(GATHER_F32, parts[1:])}
    P = {n: (t if n in ("ln_g", "ln_b") else t[0]) for n, t in P.items()}
    for n in REPLICATED[1:]:
        P[n] = w[n]
    def full(n, t):
        t = _from_chunks(t, kinds[n])
        return t if n in ("mlp_w1", "mlp_w2") else t[0]

    parts = _all_gather_many([w[n].astype(BF16) for n in FIRST], "gather_first_weights")
    W = {n: full(n, t) for n, t in zip(FIRST, parts)}
    W["ab_w_in"] = jnp.concatenate([_pad_pa(W["ab_w_in"][:, :1824]), W["ab_w_in"][:, 1824:]], axis=1)

    ada_b_loc = lax.dynamic_slice(ada_b, (0, ncol * me), (2, ncol))
    mod_part = _ada_mod(c_all, ada_w, ada_b_loc)
    mod_all = _all_gather(mod_part.reshape(-1, 128), "gather_mod").reshape(NDEV, 2, NDEV, ncol)
    mod = lax.dynamic_index_in_dim(mod_all, me, axis=2, keepdims=False)
    mod = mod.transpose(1, 0, 2).reshape(2, 6 * D)

    behind = (mod[0, 0] * 0.0).astype(BF16)
    later = _exchange_start([w[n].astype(BF16) + (behind if n == LATER[0] else 0) for n in LATER], True,
                            "gather_later_weights_start")
    mod = mod + later[-1][0, 0]

    def later_weights(after):
        lands = _exchange_wait(later, True, after, "gather_later_weights_wait")
        return {n: full(n, t) for n, t in zip(LATER, lands)}

    sent = []

    def early_grads(G):
        sent.append(_exchange_start([_to_chunks(G[n], kinds[n]).astype(BF16) for n in LATER], False,
                                    "exchange_later_grads_start"))
        return sent[0][-1][0, 0]

    loss_part, grad_x, dmod, G = _local_step(x[0], loss_target[0], mod, W, P, later_weights, early_grads)
    G["ada_b"] = dmod
    loss = lax.psum(loss_part, ("x", "y", "c"))

    rep_shapes = [w[n].shape for n in REPLICATED]
    rep_all = _all_gather(_pack([G[n] for n in REPLICATED], F32, 8), "gather_replicated_grads")
    pk = lambda d: _pack([d[n] for n in REPLICATED], F32, 8)
    rep_out = _sum_adamw(rep_all, pk(w), pk(m), pk(v), "sum_adamw_replicated", rep_all.shape[1])
    rep_out = [dict(zip(REPLICATED, _unpack(o, rep_shapes))) for o in rep_out]

    dmod_all = _unpack(rep_all, [(2, 6 * D)], (NDEV,))[0]
    dmod_loc = lax.dynamic_slice(dmod_all, (0, 0, ncol * me), (NDEV, 2, ncol)).transpose(1, 0, 2)
    ada_out = _ada_grad_adamw(c_all.T, dmod_loc, ada_w, m_ada_w, v_ada_w)

    names = [n for n, _ in SHARDED if n not in GATHER_BF16]
    shard_shapes = [w[n].shape for n in names]
    chunks = _pack8([_to_chunks(G[n], kinds[n]) for n in names], F32, 8)
    recv = _all_to_all(chunks, "exchange_small_grads")
    pk = lambda d: _pack([d[n] for n in names], F32, 8)
    sh_out = _sum_adamw(recv, pk(w), pk(m), pk(v), "sum_adamw_small", recv.shape[1])
    sh_out = [dict(zip(names, _unpack(o, shard_shapes))) for o in sh_out]

    big_out = {}

    def update(n, contributions):
        cols = w[n].shape[-1]
        flat = lambda t: t.reshape(-1, cols)
        rows = flat(w[n]).shape[0]
        outs = _sum_adamw(contributions.reshape(-1, rows, cols), flat(w[n]), flat(m[n]), flat(v[n]),
                          f"sum_adamw_{n}", min(rows, 256))
        big_out[n] = [o.reshape(w[n].shape) for o in outs]

    ci = lax.axis_index("c")
    mine_l, sib_l = [], []
    for n in FIRST:
        g8 = _to_chunks(G[n], kinds[n])
        g42 = g8.reshape((4, 2) + g8.shape[1:])
        mine_l.append(lax.dynamic_index_in_dim(g42, ci, 1, keepdims=False))
        sib_l.append(lax.dynamic_index_in_dim(g42, 1 - ci, 1, keepdims=False))
    from_sib = _swap_sibling(sib_l, "swap_sibling_grads")

    def add2_body(r, pp):
        return [r[0] + r[1]], []

    partials = []
    for n, a, b in zip(FIRST, mine_l, from_sib):
        cols = a.shape[-1]
        (p,), _ = _rows(f"pair_sum_{n}", add2_body, [a.reshape(-1, cols), b.reshape(-1, cols)], [],
                        [(cols, BF16)], tm=512)
        partials.append(p.reshape(a.shape))
    for n, r in zip(FIRST, _exchange_chips(partials, "exchange_chip_grads")):
        update(n, r)

    for n, r in zip(LATER, _exchange_wait(sent[0], False, partials[0], "exchange_later_grads_wait")):
        update(n, r)
    sh_out = [{**d, **{n: big_out[n][i] for n in GATHER_BF16}} for i, d in enumerate(sh_out)]

    def pick(i, n):
        if n == "ada_w":
            return ada_out[i]
        return rep_out[i][n] if n in REPLICATED else sh_out[i][n]

    outs = [loss, grad_x[None]]
    for i in range(4):
        outs += [pick(i, n) for n in WEIGHTS]
    return tuple(outs)
```

```python
import functools
import math

import jax
import jax.numpy as jnp
from jax import lax
from jax.experimental import pallas as pl
from jax.experimental.pallas import tpu as pltpu

F32 = jnp.float32
BF16 = jnp.bfloat16
HI = lax.Precision.HIGHEST

NDEV = 8
T = 2048
D = 1024
DFF = 4096
HEADS = 8
HD = 64
RW = 512
PA = 2048
PB = 1536
PAB = PA + PB
QKV = 4608
DILS = (1, 4, 16)
BLK = 128
ALPHA = 4.0 ** 0.25
LN_EPS = 1e-5
GN_EPS = 64e-5
ADAM_LR, ADAM_B1, ADAM_B2, ADAM_EPS, ADAM_WD, ADAM_STEP = 0.001, 0.9, 0.999, 1e-8, 0.01, 10
VMEM_LIMIT = 56 * 1024 * 1024


def _cp(sem):
    return pltpu.CompilerParams(dimension_semantics=sem, vmem_limit_bytes=VMEM_LIMIT)


def _slot(px, py, pc):
    return 4 * px + 2 * py + pc


def _all_gather(x, name):
    R, C = x.shape

    def body(x_ref, out_ref, send_sems, recv_sems, local_sem):
        xi, yi, ci = lax.axis_index("x"), lax.axis_index("y"), lax.axis_index("c")
        me, sibling = (xi, yi, ci), (xi, yi, 1 - ci)
        chips = [(1 - xi, yi), (xi, 1 - yi), (1 - xi, 1 - yi)]

        def rows(px, py, pc):
            return out_ref.at[_slot(px, py, pc)]

        def copy(k, block, to, src=None):
            return pltpu.make_async_remote_copy(
                src_ref=rows(*block) if src is None else src, dst_ref=rows(*block),
                send_sem=send_sems.at[k], recv_sem=recv_sems.at[k],
                device_id=to, device_id_type=pl.DeviceIdType.MESH)

        mine = pltpu.make_async_copy(x_ref, rows(*me), local_sem)
        mine.start()
        first = [copy(0, me, sibling, src=x_ref)]
        first += [copy(1 + j, me, (*chip, ci), src=x_ref) for j, chip in enumerate(chips)]
        for cp in first:
            cp.start()
        passed = [copy(4 + j, (*chip, ci), sibling) for j, chip in enumerate(chips)]
        for j, chip in enumerate(chips):
            copy(1 + j, (*chip, ci), me).wait_recv()
            passed[j].start()
        copy(0, sibling, me).wait_recv()
        for j, chip in enumerate(chips):
            copy(4 + j, (*chip, 1 - ci), me).wait_recv()
        for cp in first + passed:
            cp.wait_send()
        mine.wait()

    return pl.pallas_call(
        body, name=name,
        out_shape=jax.ShapeDtypeStruct((NDEV, R, C), x.dtype),
        in_specs=[pl.BlockSpec(memory_space=pl.ANY)],
        out_specs=pl.BlockSpec(memory_space=pl.ANY),
        scratch_shapes=[pltpu.SemaphoreType.DMA((7,)), pltpu.SemaphoreType.DMA((7,)),
                        pltpu.SemaphoreType.DMA(())],
    )(x)


def _all_to_all(g, name):
    _, R, C = g.shape

    def body(g_ref, out_ref, send_sems, recv_sems, local_sem):
        xi, yi, ci = lax.axis_index("x"), lax.axis_index("y"), lax.axis_index("c")
        my_slot = _slot(xi, yi, ci)
        mine = pltpu.make_async_copy(g_ref.at[my_slot], out_ref.at[my_slot], local_sem)
        mine.start()
        copies = []
        for k in range(1, 8):
            px = 1 - xi if k & 4 else xi
            py = 1 - yi if k & 2 else yi
            pc = 1 - ci if k & 1 else ci
            peer_slot = _slot(px, py, pc)
            copies.append((
                pltpu.make_async_remote_copy(
                    src_ref=g_ref.at[peer_slot], dst_ref=out_ref.at[my_slot],
                    send_sem=send_sems.at[k - 1], recv_sem=recv_sems.at[k - 1],
                    device_id=(px, py, pc), device_id_type=pl.DeviceIdType.MESH),
                pltpu.make_async_remote_copy(
                    src_ref=g_ref.at[peer_slot], dst_ref=out_ref.at[peer_slot],
                    send_sem=send_sems.at[k - 1], recv_sem=recv_sems.at[k - 1],
                    device_id=(px, py, pc), device_id_type=pl.DeviceIdType.MESH)))
        for send, _ in copies:
            send.start()
        for _, recv in copies:
            recv.wait_recv()
        for send, _ in copies:
            send.wait_send()
        mine.wait()

    return pl.pallas_call(
        body, name=name,
        out_shape=jax.ShapeDtypeStruct((NDEV, R, C), g.dtype),
        in_specs=[pl.BlockSpec(memory_space=pl.ANY)],
        out_specs=pl.BlockSpec(memory_space=pl.ANY),
        scratch_shapes=[pltpu.SemaphoreType.DMA((7,)), pltpu.SemaphoreType.DMA((7,)),
                        pltpu.SemaphoreType.DMA(())],
    )(g)


def _my_slot():
    return _slot(lax.axis_index("x"), lax.axis_index("y"), lax.axis_index("c"))


def _put_own(buf, own, slot):
    return lax.dynamic_update_index_in_dim(buf, own, slot, 0)


def _hbm_call(body, name, ins, out_shapes, n_sems):
    anyspec = pl.BlockSpec(memory_space=pl.ANY)
    return pl.pallas_call(
        body, name=name, out_shape=out_shapes,
        in_specs=[anyspec] * len(ins), out_specs=[anyspec] * len(out_shapes),
        scratch_shapes=[pltpu.SemaphoreType.DMA(s) for s in n_sems],
    )(*ins)


def _all_gather_many(xs, name):
    n = len(xs)

    def body(*refs):
        x_refs, o_refs = refs[:n], refs[n:2 * n]
        send_sems, recv_sems = refs[2 * n:]
        xi, yi, ci = lax.axis_index("x"), lax.axis_index("y"), lax.axis_index("c")
        me, sibling = (xi, yi, ci), (xi, yi, 1 - ci)
        chips = [(1 - xi, yi), (xi, 1 - yi), (1 - xi, 1 - yi)]

        def copy(i, k, block, to, src=None):
            dst = o_refs[i].at[_slot(*block)]
            return pltpu.make_async_remote_copy(
                src_ref=dst if src is None else src, dst_ref=dst,
                send_sem=send_sems.at[i, k], recv_sem=recv_sems.at[i, k],
                device_id=to, device_id_type=pl.DeviceIdType.MESH)

        sends = []
        for i in range(n):
            sends += [copy(i, 1 + j, me, (*chip, ci), src=x_refs[i]) for j, chip in enumerate(chips)]
            sends.append(copy(i, 0, me, sibling, src=x_refs[i]))
        for cp in sends:
            cp.start()
        for j, chip in enumerate(chips):
            for i in range(n):
                copy(i, 1 + j, (*chip, ci), me).wait_recv()
                passed = copy(i, 4 + j, (*chip, ci), sibling)
                passed.start()
                sends.append(passed)
        for i in range(n):
            copy(i, 0, sibling, me).wait_recv()
            for j, chip in enumerate(chips):
                copy(i, 4 + j, (*chip, 1 - ci), me).wait_recv()
        for cp in sends:
            cp.wait_send()

    outs = _hbm_call(body, name, xs, [jax.ShapeDtypeStruct((NDEV,) + x.shape, x.dtype) for x in xs],
                     [(n, 7), (n, 7)])
    return [_put_own(o, x[None], _my_slot()) for o, x in zip(outs, xs)]


def _swap_sibling(gs, name):
    n = len(gs)

    def body(*refs):
        g_refs, o_refs = refs[:n], refs[n:2 * n]
        send_sems, recv_sems = refs[2 * n:]
        sibling = (lax.axis_index("x"), lax.axis_index("y"), 1 - lax.axis_index("c"))
        copies = [pltpu.make_async_remote_copy(
            src_ref=g_refs[i], dst_ref=o_refs[i], send_sem=send_sems.at[i], recv_sem=recv_sems.at[i],
            device_id=sibling, device_id_type=pl.DeviceIdType.MESH) for i in range(n)]
        for cp in copies:
            cp.start()
        for cp in copies:
            cp.wait_recv()
        for cp in copies:
            cp.wait_send()

    return _hbm_call(body, name, gs, [jax.ShapeDtypeStruct(g.shape, g.dtype) for g in gs], [(n,), (n,)])


def _exchange_chips(ps, name):
    n = len(ps)

    def body(*refs):
        p_refs, o_refs = refs[:n], refs[n:2 * n]
        send_sems, recv_sems = refs[2 * n:]
        xi, yi, ci = lax.axis_index("x"), lax.axis_index("y"), lax.axis_index("c")
        q_me = 2 * xi + yi
        sends, recvs = [], []
        for k in range(1, 4):
            px = 1 - xi if k & 2 else xi
            py = 1 - yi if k & 1 else yi
            q_peer = 2 * px + py
            for i in range(n):
                sends.append(pltpu.make_async_remote_copy(
                    src_ref=p_refs[i].at[q_peer], dst_ref=o_refs[i].at[q_me],
                    send_sem=send_sems.at[i, k - 1], recv_sem=recv_sems.at[i, k - 1],
                    device_id=(px, py, ci), device_id_type=pl.DeviceIdType.MESH))
                recvs.append(pltpu.make_async_remote_copy(
                    src_ref=p_refs[i].at[q_peer], dst_ref=o_refs[i].at[q_peer],
                    send_sem=send_sems.at[i, k - 1], recv_sem=recv_sems.at[i, k - 1],
                    device_id=(px, py, ci), device_id_type=pl.DeviceIdType.MESH))
        for cp in sends:
            cp.start()
        for cp in recvs:
            cp.wait_recv()
        for cp in sends:
            cp.wait_send()

    outs = _hbm_call(body, name, ps, [jax.ShapeDtypeStruct(p.shape, p.dtype) for p in ps], [(n, 3), (n, 3)])
    q_me = 2 * lax.axis_index("x") + lax.axis_index("y")
    return [_put_own(o, lax.dynamic_index_in_dim(p, q_me, 0, keepdims=True), q_me) for o, p in zip(outs, ps)]


def _peers(xi, yi, ci):
    return [(1 - xi if k & 4 else xi, 1 - yi if k & 2 else yi, 1 - ci if k & 1 else ci) for k in range(1, 8)]


def _direct_copy(src_refs, land_refs, send_sems, recv_sems, i, k, peer, my_slot, gather):
    src = src_refs[i] if gather else src_refs[i].at[_slot(*peer)]
    return pltpu.make_async_remote_copy(
        src_ref=src, dst_ref=land_refs[i].at[my_slot], send_sem=send_sems.at[7 * i + k], recv_sem=recv_sems.at[7 * i + k],
        device_id=peer, device_id_type=pl.DeviceIdType.MESH)


def _exchange_start(srcs, gather, name):
    n = len(srcs)
    lands = [lax.empty(((NDEV,) + s.shape) if gather else s.shape, s.dtype) for s in srcs]

    def body(*refs):
        s_refs, l_refs = refs[:n], refs[n:2 * n]
        send_sems, recv_sems = refs[2 * n], refs[2 * n + 1]
        token = refs[2 * n + 2 + 2 * n]
        xi, yi, ci = lax.axis_index("x"), lax.axis_index("y"), lax.axis_index("c")
        my_slot = _slot(xi, yi, ci)
        for k, peer in enumerate(_peers(xi, yi, ci)):
            for i in range(n):
                _direct_copy(s_refs, l_refs, send_sems, recv_sems, i, k, peer, my_slot, gather).start()
        token[...] = jnp.zeros_like(token)

    hbm = pl.BlockSpec(memory_space=pltpu.HBM)
    sem = pl.BlockSpec(memory_space=pltpu.SEMAPHORE)
    both = list(srcs) + lands
    return pl.pallas_call(
        body, name=name,
        out_shape=(pltpu.SemaphoreType.DMA((7 * n,)), pltpu.SemaphoreType.DMA((7 * n,)),
                   *[pltpu.HBM(t.shape, t.dtype) for t in both], jax.ShapeDtypeStruct((8, 128), F32)),
        in_specs=[hbm] * (2 * n),
        out_specs=(sem, sem, *[hbm] * (2 * n), pl.BlockSpec(memory_space=pltpu.VMEM)),
        input_output_aliases={i: 2 + i for i in range(2 * n)},
        compiler_params=pltpu.CompilerParams(has_side_effects=pltpu.SideEffectType.DATAFLOW_SIDE_EFFECTING),
    )(*[pltpu.with_memory_space_constraint(t, pltpu.HBM) for t in both])


def _exchange_wait(started, gather, after, name):
    send_sems, recv_sems, *thru, _ = started
    n = len(thru) // 2

    def body(*refs):
        s_refs, l_refs = refs[:n], refs[n:2 * n]
        send_sems, recv_sems = refs[2 * n], refs[2 * n + 1]
        xi, yi, ci = lax.axis_index("x"), lax.axis_index("y"), lax.axis_index("c")
        my_slot = _slot(xi, yi, ci)
        for k, peer in enumerate(_peers(xi, yi, ci)):
            for i in range(n):
                _direct_copy(s_refs, l_refs, send_sems, recv_sems, i, k, peer, my_slot, gather).wait_send()
                _direct_copy(s_refs, l_refs, send_sems, recv_sems, i, k, peer, _slot(*peer), gather).wait_recv()

    hbm = pl.BlockSpec(memory_space=pltpu.HBM)
    sem = pl.BlockSpec(memory_space=pltpu.SEMAPHORE)
    outs = pl.pallas_call(
        body, name=name,
        out_shape=tuple(pltpu.HBM(t.shape, t.dtype) for t in thru),
        in_specs=[hbm] * (2 * n) + [sem, sem, pl.BlockSpec(memory_space=pl.ANY)],
        out_specs=tuple([hbm] * (2 * n)),
        input_output_aliases={i: i for i in range(2 * n)},
        compiler_params=pltpu.CompilerParams(has_side_effects=pltpu.SideEffectType.DATAFLOW_SIDE_EFFECTING),
    )(*thru, send_sems, recv_sems, after)
    slot = _my_slot()
    own = [s[None] if gather else lax.dynamic_index_in_dim(s, slot, 0, keepdims=True) for s in outs[:n]]
    return [_put_own(land, o, slot) for land, o in zip(outs[n:], own)]


def _mm(name, a, b, tb=False, out=(F32,), epi=None, extras=(), tm=1024, tn=512, tk_cap=2048):
    M, K = a.shape
    N = b.shape[0] if tb else b.shape[1]
    tm, tn = min(tm, M), min(tn, N)
    tk = max(t for t in range(128, min(K, tk_cap) + 1, 128) if K % t == 0)
    assert M % tm == 0 and N % tn == 0 and K % tk == 0, (name, M, N, K)
    nk = K // tk
    ne, no = len(extras), len(out)
    dims = (((1,), (1 if tb else 0,)), ((), ()))

    def kern(*refs):
        a_ref, b_ref = refs[:2]
        e_refs = refs[2:2 + ne]
        o_refs = refs[2 + ne:2 + ne + no]

        def finish(acc):
            outs = epi(acc, *[e[...] for e in e_refs]) if epi is not None else (acc,)
            for o_ref, o in zip(o_refs, outs):
                o_ref[...] = o.astype(o_ref.dtype)

        part = lax.dot_general(a_ref[...], b_ref[...], dims, preferred_element_type=F32)
        if nk == 1:
            finish(part)
            return
        acc_ref = refs[-1]
        k = pl.program_id(2)

        @pl.when(k == 0)
        def _():
            acc_ref[...] = part

        @pl.when(k > 0)
        def _():
            acc_ref[...] += part

        @pl.when(k == nk - 1)
        def _():
            finish(acc_ref[...])

    b_spec = (pl.BlockSpec((tn, tk), lambda i, j, k: (j, k)) if tb
              else pl.BlockSpec((tk, tn), lambda i, j, k: (k, j)))
    tile = pl.BlockSpec((tm, tn), lambda i, j, k: (i, j))
    res = pl.pallas_call(
        kern, name=name, grid=(M // tm, N // tn, nk),
        in_specs=[pl.BlockSpec((tm, tk), lambda i, j, k: (i, k)), b_spec] + [tile] * ne,
        out_specs=[tile] * no,
        out_shape=[jax.ShapeDtypeStruct((M, N), dt) for dt in out],
        scratch_shapes=[pltpu.VMEM((tm, tn), F32)] if nk > 1 else [],
        compiler_params=_cp(("parallel", "parallel", "arbitrary")),
    )(a, b, *extras)
    return res[0] if no == 1 else res


def _rows(name, body, rows, params, out_rows, out_accs=(), tm=256):
    views = [r if isinstance(r, tuple) else (r, r.shape[1], 0) for r in rows]
    n = views[0][0].shape[0]
    assert n % tm == 0
    nr, npar, nor, noa = len(views), len(params), len(out_rows), len(out_accs)

    def kern(*refs):
        r_refs = refs[:nr]
        p_refs = refs[nr:nr + npar]
        o_refs = refs[nr + npar:nr + npar + nor]
        a_refs = refs[nr + npar + nor:]
        outs, accs = body([r[...] for r in r_refs], [p[...] for p in p_refs])
        assert len(outs) == nor and len(accs) == noa, (name, len(outs), len(accs))
        for o_ref, o in zip(o_refs, outs):
            o_ref[...] = o.astype(o_ref.dtype)
        if noa:
            @pl.when(pl.program_id(0) == 0)
            def _():
                for a_ref in a_refs:
                    a_ref[...] = jnp.zeros_like(a_ref)

            for a_ref, a in zip(a_refs, accs):
                a_ref[...] += a.astype(F32)

    def whole(shape):
        nd = len(shape)
        return pl.BlockSpec(tuple(shape), lambda i, nd=nd: (0,) * nd)

    in_specs = [pl.BlockSpec((tm, w), lambda i, cb=cb: (i, cb)) for _, w, cb in views]
    in_specs += [whole(p.shape) for p in params]
    out_specs = [pl.BlockSpec((tm, c), lambda i: (i, 0)) for c, _ in out_rows]
    out_specs += [whole(s) for s in out_accs]
    out_shape = [jax.ShapeDtypeStruct((n, c), dt) for c, dt in out_rows]
    out_shape += [jax.ShapeDtypeStruct(tuple(s), F32) for s in out_accs]
    res = pl.pallas_call(
        kern, name=name, grid=(n // tm,), in_specs=in_specs, out_specs=out_specs,
        out_shape=out_shape, compiler_params=_cp(("arbitrary",)),
    )(*[v[0] for v in views], *params)
    return res[:nor], res[nor:]


@jax.custom_vjp
def _headsum(x, e):
    return sum(jnp.dot(p, e, preferred_element_type=F32) for p in _split3(x))


_headsum.defvjp(lambda x, e: (_headsum(x, e), e), lambda e, ct: (_headsum(ct, e), None))


def _softplus(z):
    return jnp.maximum(z, 0.0) + jnp.log(1.0 + jnp.exp(jnp.minimum(z, -z)))


def _post_ln(x, y, g, lng, lnb):
    z = ALPHA * x + (1.0 + g) * y
    mu = jnp.mean(z, axis=-1, keepdims=True)
    zc = z - mu
    var = jnp.mean(zc * zc, axis=-1, keepdims=True)
    return zc * lax.rsqrt(var + LN_EPS) * lng + lnb


def _post_ln_mod(x, y, g, lng, lnb, scn, shn):
    xn = _post_ln(x, y, g, lng, lnb)
    return xn, xn * (1.0 + scn) + shn


def _pre_core(E, r_, k_, v_, wd_, ad_, gd_, r1, k1, v1, wd1, ad1, gd1, h, bg, cg, h1, cg1, h2, cg2,
              mu_r, mu_k, mu_v, mu_wd, mu_ad, mu_gd, w0, w_up, a0, a_up, g_up, k_k, k_a,
              cw0, cw1, cw2):
    def mix(x, x1, mu):
        return x + mu * (x1 - x)

    r, k, v = mix(r_, r1, mu_r), mix(k_, k1, mu_k), mix(v_, v1, mu_v)
    wd, ad, gd = mix(wd_, wd1, mu_wd), mix(ad_, ad1, mu_ad), mix(gd_, gd1, mu_gd)
    logw = -_softplus(-(w0 + jnp.dot(jnp.tanh(wd), w_up, preferred_element_type=F32))) - 0.5
    decay = jnp.exp(-jnp.exp(logw))
    iclr = jax.nn.sigmoid(a0 + jnp.dot(ad, a_up, preferred_element_type=F32))
    gate = jnp.dot(jax.nn.sigmoid(gd), g_up, preferred_element_type=F32)
    kk0 = k * k_k
    nrm = jnp.sqrt(_headsum(kk0 * kk0, E))
    kk = kk0 / jnp.maximum(nrm, 1e-12)
    kh = k * (1.0 + (iclr - 1.0) * k_a)
    yb = bg * (cw2 * (cg * h) + cw1 * (cg1 * h1) + cw0 * (cg2 * h2))
    return r, decay, kh, v, -kk, kk * iclr, gate, yb


def _post_core(E, y, r, kh, v, gate, lnx_g, lnx_b, rk):
    def seg(t):
        return _headsum(t, E)

    mean = seg(y) * (1.0 / HD)
    yc = y - mean
    var = seg(yc * yc) * (1.0 / HD)
    gn = yc * lax.rsqrt(var + GN_EPS) * lnx_g + lnx_b
    bonus = seg(r * kh * rk) * v
    return (gn + bonus) * gate


def _merge_core(o0, o1, o2, l0, l1, l2):
    m = jnp.maximum(jnp.maximum(l0, l1), l2)
    e0, e1, e2 = jnp.exp(l0 - m), jnp.exp(l1 - m), jnp.exp(l2 - m)
    den = e0 + e1 + e2
    return (e0 * o0 + e1 * o1 + e2 * o2) / den


CHUNK = 128
HALF = 64
HP = HEADS // 2
LW = 2 * HD
NCHUNK = T // CHUNK


def _split3(x):
    hi = x.astype(BF16)
    r1 = x - hi.astype(F32)
    mid = r1.astype(BF16)
    return hi, mid, (r1 - mid.astype(F32)).astype(BF16)


def _cols3(parts):
    tr = [p.reshape(T // HALF, HALF, HP, 2, HD).transpose(2, 4, 0, 3, 1) for p in parts]
    return jnp.stack(tr, axis=4).reshape(HP, HD, 6 * T)


def _pick_codes():
    row = lax.broadcasted_iota(jnp.int32, (6 * HALF, LW), 0)
    col = lax.broadcasted_iota(jnp.int32, (6 * HALF, LW), 1)
    same = (row >= 3 * HALF) == (col >= HD)
    return jnp.where(same, row & (HALF - 1), -1).astype(BF16)


def _column(block_ref, codes, half, i):
    pick = jnp.where(codes == i.astype(BF16), jnp.ones((), BF16), jnp.zeros((), BF16))
    block = block_ref[:, :, half * 6 * HALF:(half + 1) * 6 * HALF].reshape(HP * HD, 6 * HALF)
    return jnp.dot(block, pick, preferred_element_type=F32)


def _halfsums(x, row, left1):
    row_l = jnp.where(left1, row, 0.0)
    return (jnp.sum(x * row_l, axis=1, keepdims=True), jnp.sum(x * (row - row_l), axis=1, keepdims=True))


def _pair_rows(row):
    return [row[:, p * LW:(p + 1) * LW] for p in range(HP)]


def _store_columns(ref, p, t_mask, cols):
    for j, col in enumerate(cols):
        pltpu.store(ref.at[pl.ds(2 * p + j, 1)], jnp.broadcast_to(col[None], (1, HD, CHUNK)), mask=t_mask[None])


def _scan_fwd(r, w, k, a, b, v3):
    def kern(r_ref, w_ref, k_ref, a_ref, b_ref, v_ref, y_ref, ck_ref, s_ref, vb_ref):
        @pl.when(pl.program_id(0) == 0)
        def _():
            s_ref[...] = jnp.zeros_like(s_ref)

        lane = lax.broadcasted_iota(jnp.int32, (HD, CHUNK), 1)
        left = lane < HD
        left1 = lax.broadcasted_iota(jnp.int32, (1, LW), 1) < HD
        codes = _pick_codes()

        for half in range(CHUNK // HALF):
            ck_ref[half] = s_ref[...]
            vb_ref[...] = _column(v_ref, codes, half, jnp.int32(0))

            def step(i, carry):
                t = half * HALF + i
                row = lambda ref: _pair_rows(ref[pl.ds(t, 1), :])
                S = [s_ref[p] for p in range(HP)]
                sa = [jnp.where(left, *_halfsums(s, a, left1)) for s, a in zip(S, row(a_ref))]
                S = [s * w + c * b + vb_ref[pl.ds(p * HD, HD), :] * k
                     for p, (s, w, c, b, k) in enumerate(zip(S, row(w_ref), sa, row(b_ref), row(k_ref)))]
                for p, s in enumerate(S):
                    s_ref[p] = s
                for p, (s, r) in enumerate(zip(S, row(r_ref))):
                    _store_columns(y_ref, p, lane == t, _halfsums(s, r, left1))
                vb_ref[...] = _column(v_ref, codes, half, i + 1)
                return carry

            lax.fori_loop(0, HALF, step, 0, unroll=8)

    rowblk = pl.BlockSpec((CHUNK, RW), lambda c: (c, 0))
    return pl.pallas_call(
        kern, name="rwkv_scan_fwd", grid=(NCHUNK,),
        in_specs=[rowblk] * 5 + [pl.BlockSpec((HP, HD, 6 * CHUNK), lambda c: (0, 0, c))],
        out_specs=[pl.BlockSpec((HEADS, HD, CHUNK), lambda c: (0, 0, c)),
                   pl.BlockSpec((CHUNK // HALF, HP, HD, LW), lambda c: (c, 0, 0, 0))],
        out_shape=[jax.ShapeDtypeStruct((HEADS, HD, T), F32),
                   jax.ShapeDtypeStruct((T // HALF, HP, HD, LW), F32)],
        scratch_shapes=[pltpu.VMEM((HP, HD, LW), F32), pltpu.VMEM((HP * HD, LW), F32)],
        compiler_params=_cp(("arbitrary",)),
    )(r, w, k, a, b, v3)


def _scan_bwd(r, w, k, a, b, v3, dy3, ck):
    NC = T // CHUNK

    def kern(r_ref, w_ref, k_ref, a_ref, b_ref, v_ref, dy_ref, ck_ref,
             dr_ref, dw_ref, dk_ref, da_ref, db_ref, dv_ref, ds_ref, sb_ref, vb_ref, sa_ref, dyb_ref):
        @pl.when(pl.program_id(0) == 0)
        def _():
            ds_ref[...] = jnp.zeros_like(ds_ref)

        lane = lax.broadcasted_iota(jnp.int32, (HD, CHUNK), 1)
        left = lane < HD
        left1 = lax.broadcasted_iota(jnp.int32, (1, LW), 1) < HD
        codes = _pick_codes()

        def rowsum(x):
            return jnp.sum(x, axis=0, keepdims=True)

        for half in reversed(range(CHUNK // HALF)):
            base = half * HALF
            sb_ref[0] = ck_ref[half]

            vb_ref[0] = _column(v_ref, codes, half, jnp.int32(0))

            def replay(i, carry):
                t = base + i
                row = lambda ref: _pair_rows(ref[pl.ds(t, 1), :])
                S = [sb_ref[i, p] for p in range(HP)]
                sa = [jnp.where(left, *_halfsums(s, a, left1)) for s, a in zip(S, row(a_ref))]
                for p, (s, w, c, b, k) in enumerate(zip(S, row(w_ref), sa, row(b_ref), row(k_ref))):
                    sb_ref[i + 1, p] = s * w + c * b + vb_ref[i, pl.ds(p * HD, HD), :] * k
                    sa_ref[i, p] = c
                vb_ref[i + 1] = _column(v_ref, codes, half, i + 1)
                return carry

            lax.fori_loop(0, HALF, replay, 0, unroll=4)
            dyb_ref[...] = _column(dy_ref, codes, half, jnp.int32(HALF - 1))

            def back(ii, carry):
                i = HALF - 1 - ii
                t = base + i
                row = lambda ref: _pair_rows(ref[pl.ds(t, 1), :])
                a_r, b_r, k_r, w_r, r_r = row(a_ref), row(b_ref), row(k_ref), row(w_ref), row(r_ref)
                dys = [dyb_ref[pl.ds(p * HD, HD), :] for p in range(HP)]
                dyb_ref[...] = _column(dy_ref, codes, half, jnp.maximum(i - 1, 0))
                dr, dw, db, dk, da = [], [], [], [], []
                for p in range(HP):
                    Sp, dy = sb_ref[i, p], dys[p]
                    dS = ds_ref[p] + dy * r_r[p]
                    dr.append(rowsum(sb_ref[i + 1, p] * dy))
                    dw.append(rowsum(dS * Sp))
                    db.append(rowsum(dS * sa_ref[i, p]))
                    dk.append(rowsum(dS * vb_ref[i, pl.ds(p * HD, HD), :]))
                    dsa = jnp.where(left, *_halfsums(dS, b_r[p], left1))
                    _store_columns(dv_ref, p, lane == t, _halfsums(dS, k_r[p], left1))
                    da.append(rowsum(Sp * dsa))
                    ds_ref[p] = dS * w_r[p] + dsa * a_r[p]
                for ref, pieces in ((dr_ref, dr), (dw_ref, dw), (db_ref, db), (dk_ref, dk), (da_ref, da)):
                    ref[pl.ds(t, 1), :] = jnp.concatenate(pieces, axis=1)
                return carry

            lax.fori_loop(0, HALF, back, 0, unroll=4)

    rowblk = pl.BlockSpec((CHUNK, RW), lambda c: (NC - 1 - c, 0))
    col3blk = pl.BlockSpec((HP, HD, 6 * CHUNK), lambda c: (0, 0, NC - 1 - c))
    rowshape = jax.ShapeDtypeStruct((T, RW), F32)
    return pl.pallas_call(
        kern, name="rwkv_scan_bwd", grid=(NC,),
        in_specs=[rowblk] * 5 + [col3blk, col3blk,
                                 pl.BlockSpec((CHUNK // HALF, HP, HD, LW), lambda c: (NC - 1 - c, 0, 0, 0))],
        out_specs=[rowblk] * 5 + [pl.BlockSpec((HEADS, HD, CHUNK), lambda c: (0, 0, NC - 1 - c))],
        out_shape=[rowshape] * 5 + [jax.ShapeDtypeStruct((HEADS, HD, T), F32)],
        scratch_shapes=[pltpu.VMEM((HP, HD, LW), F32), pltpu.VMEM((HALF + 1, HP, HD, LW), F32),
                        pltpu.VMEM((HALF + 1, HP * HD, LW), F32), pltpu.VMEM((HALF, HP, HD, LW), F32),
                        pltpu.VMEM((HP * HD, LW), F32)],
        compiler_params=_cp(("arbitrary",)),
    )(r, w, k, a, b, v3, dy3, ck)


NBLK = T // BLK


def _blocks_per_segment(g):
    return jnp.where(g == 0, NBLK // DILS[0], jnp.where(g == 1, NBLK // DILS[1], NBLK // DILS[2]))


def _attn_fwd(q, kp, vp, bias):
    def kern(q_ref, k_ref, v_ref, b_ref, o_ref, l_ref):
        nbs = _blocks_per_segment(pl.program_id(0))
        qi = lax.broadcasted_iota(jnp.int32, (BLK, 2 * BLK), 0)
        ki = lax.broadcasted_iota(jnp.int32, (BLK, 2 * BLK), 1)
        band = (ki >= qi) & (ki <= qi + BLK)
        bias_t = b_ref[0, 0]
        for n in range(NBLK):
            lo = jnp.where((n & (nbs - 1)) == 0, BLK, 0)
            valid = band & (ki >= lo)
            qb = q_ref[0, 0, n * BLK:(n + 1) * BLK, :]
            kc = k_ref[0, 0, n * BLK:(n + 2) * BLK, :]
            vc = v_ref[0, 0, n * BLK:(n + 2) * BLK, :]
            s = lax.dot_general(qb, kc, (((1,), (1,)), ((), ())), preferred_element_type=F32)
            s = jnp.where(valid, s * (HD ** -0.5) + bias_t, -jnp.inf)
            m = jnp.max(s, axis=1, keepdims=True)
            e = jnp.exp(s - m)
            den = jnp.sum(e, axis=1, keepdims=True)
            pr = (e / den).astype(BF16)
            o_ref[0, 0, n * BLK:(n + 1) * BLK, :] = jnp.dot(pr, vc, preferred_element_type=F32)
            l_ref[0, 0, n * BLK:(n + 1) * BLK, :] = m + jnp.log(den)

    def blk(rows, cols):
        return pl.BlockSpec((1, 1, rows, cols), lambda g, h: (g, h, 0, 0))

    return pl.pallas_call(
        kern, name="attn_fwd", grid=(3, HEADS),
        in_specs=[blk(T, HD), blk(T + BLK, HD), blk(T + BLK, HD), blk(BLK, 2 * BLK)],
        out_specs=[blk(T, HD), blk(T, 1)],
        out_shape=[jax.ShapeDtypeStruct((3, HEADS, T, HD), F32),
                   jax.ShapeDtypeStruct((3, HEADS, T, 1), F32)],
        compiler_params=_cp(("parallel", "parallel")),
    )(q, kp, vp, bias)


def _attn_bwd(q, kp, vp, bias, biasT, do, lse_c, lse_r, dc_c, dc_r):
    def kern(q_ref, k_ref, v_ref, b_ref, bt_ref, do_ref, lc_ref, lr_ref, dcc_ref, dcr_ref,
             dq_ref, dk_ref, dv_ref, db_ref):
        nbs = _blocks_per_segment(pl.program_id(0))
        qi = lax.broadcasted_iota(jnp.int32, (BLK, 2 * BLK), 0)
        ki = lax.broadcasted_iota(jnp.int32, (BLK, 2 * BLK), 1)
        band = (ki >= qi) & (ki <= qi + BLK)
        kiT = lax.broadcasted_iota(jnp.int32, (2 * BLK, BLK), 0)
        qiT = lax.broadcasted_iota(jnp.int32, (2 * BLK, BLK), 1)
        bandT = (kiT >= qiT) & (kiT <= qiT + BLK)
        bias_t, biasT_t = b_ref[0, 0], bt_ref[0, 0]
        scale = HD ** -0.5
        dk_ref[...] = jnp.zeros_like(dk_ref)
        dv_ref[...] = jnp.zeros_like(dv_ref)
        db_ref[...] = jnp.zeros_like(db_ref)
        nt = (((1,), (1,)), ((), ()))
        for n in range(NBLK):
            lo = jnp.where((n & (nbs - 1)) == 0, BLK, 0)
            qs, ks = slice(n * BLK, (n + 1) * BLK), slice(n * BLK, (n + 2) * BLK)
            qb, kc, vc, dob = q_ref[0, 0, qs, :], k_ref[0, 0, ks, :], v_ref[0, 0, ks, :], do_ref[0, 0, qs, :]
            s = lax.dot_general(qb, kc, nt, preferred_element_type=F32) * scale + bias_t
            p = jnp.where(band & (ki >= lo), jnp.exp(s - lc_ref[0, 0, qs, :]), 0.0)
            dp = lax.dot_general(dob, vc, nt, preferred_element_type=F32)
            ds = p * (dp + dcc_ref[0, 0, qs, :])
            db_ref[0, 0] += ds
            dq_ref[0, 0, qs, :] = jnp.dot((ds * scale).astype(BF16), kc, preferred_element_type=F32)
            sT = lax.dot_general(kc, qb, nt, preferred_element_type=F32) * scale + biasT_t
            pT = jnp.where(bandT & (kiT >= lo), jnp.exp(sT - lr_ref[0, 0, :, qs]), 0.0)
            dpT = lax.dot_general(vc, dob, nt, preferred_element_type=F32)
            dsT = pT * (dpT + dcr_ref[0, 0, :, qs])
            dk_ref[0, 0, ks, :] += jnp.dot((dsT * scale).astype(BF16), qb, preferred_element_type=F32)
            dv_ref[0, 0, ks, :] += jnp.dot(pT.astype(BF16), dob, preferred_element_type=F32)

    def blk(rows, cols):
        return pl.BlockSpec((1, 1, rows, cols), lambda g, h: (g, h, 0, 0))

    return pl.pallas_call(
        kern, name="attn_bwd", grid=(3, HEADS),
        in_specs=[blk(T, HD), blk(T + BLK, HD), blk(T + BLK, HD), blk(BLK, 2 * BLK), blk(2 * BLK, BLK),
                  blk(T, HD), blk(T, 1), blk(1, T), blk(T, 1), blk(1, T)],
        out_specs=[blk(T, HD), blk(T + BLK, HD), blk(T + BLK, HD), blk(BLK, 2 * BLK)],
        out_shape=[jax.ShapeDtypeStruct((3, HEADS, T, HD), F32),
                   jax.ShapeDtypeStruct((3, HEADS, T + BLK, HD), F32),
                   jax.ShapeDtypeStruct((3, HEADS, T + BLK, HD), F32),
                   jax.ShapeDtypeStruct((3, HEADS, BLK, 2 * BLK), F32)],
        compiler_params=_cp(("parallel", "parallel")),
    )(q, kp, vp, bias, biasT, do, lse_c, lse_r, dc_c, dc_r)


NBUCKET = 32
NPAIR = BLK * 2 * BLK


def _relbias_table(rbT, onehotT):
    def kern(rb_ref, oh_ref, out_ref):
        out_ref[0] = sum(jnp.dot(p, oh_ref[0], preferred_element_type=F32) for p in _split3(rb_ref[0]))

    return pl.pallas_call(
        kern, name="relbias_table", grid=(3,),
        in_specs=[pl.BlockSpec((1, HEADS, NBUCKET), lambda g: (g, 0, 0)),
                  pl.BlockSpec((1, NBUCKET, NPAIR), lambda g: (g, 0, 0))],
        out_specs=pl.BlockSpec((1, HEADS, NPAIR), lambda g: (g, 0, 0)),
        out_shape=jax.ShapeDtypeStruct((3, HEADS, NPAIR), F32),
        compiler_params=_cp(("parallel",)),
    )(rbT, onehotT)


def _relbias_grad(db, onehotT):
    nt = (((1,), (1,)), ((), ()))

    def kern(db_ref, oh_ref, out_ref):
        hi, mid, _ = _split3(db_ref[0])
        out_ref[0] = (lax.dot_general(hi, oh_ref[0], nt, preferred_element_type=F32)
                      + lax.dot_general(mid, oh_ref[0], nt, preferred_element_type=F32))

    return pl.pallas_call(
        kern, name="relbias_grad", grid=(3,),
        in_specs=[pl.BlockSpec((1, HEADS, NPAIR), lambda g: (g, 0, 0)),
                  pl.BlockSpec((1, NBUCKET, NPAIR), lambda g: (g, 0, 0))],
        out_specs=pl.BlockSpec((1, HEADS, NBUCKET), lambda g: (g, 0, 0)),
        out_shape=jax.ShapeDtypeStruct((3, HEADS, NBUCKET), F32),
        compiler_params=_cp(("parallel",)),
    )(db, onehotT)


def _adamw(w, g, m, v):
    m2 = ADAM_B1 * m + (1.0 - ADAM_B1) * g
    v2 = ADAM_B2 * v + (1.0 - ADAM_B2) * (g * g)
    m_hat = m2 / (1.0 - ADAM_B1 ** ADAM_STEP)
    v_hat = v2 / (1.0 - ADAM_B2 ** ADAM_STEP)
    return -ADAM_LR * (m_hat / (jnp.sqrt(v_hat) + ADAM_EPS) + ADAM_WD * w), m2, v2


def _ada_mod(c_all, ada_w, ada_b_loc):
    def kern(c_ref, w_ref, b_ref, o_ref):
        c = c_ref[...]
        cond = c * jax.nn.sigmoid(c)
        o_ref[0] = jnp.dot(cond, w_ref[0], precision=HI, preferred_element_type=F32) + b_ref[0]

    ncol = ada_w.shape[2]
    return pl.pallas_call(
        kern, name="ada_mod", grid=(2,),
        in_specs=[pl.BlockSpec((NDEV, D), lambda i: (0, 0)),
                  pl.BlockSpec((1, D, ncol), lambda i: (i, 0, 0)),
                  pl.BlockSpec((1, 1, ncol), lambda i: (i, 0, 0))],
        out_specs=pl.BlockSpec((1, NDEV, ncol), lambda i: (i, 0, 0)),
        out_shape=jax.ShapeDtypeStruct((2, NDEV, ncol), F32),
        compiler_params=_cp(("parallel",)),
    )(c_all, ada_w, ada_b_loc.reshape(2, 1, ncol))


def _ada_grad_adamw(cT_all, dmod_loc, w, m, v):
    ncol = w.shape[2]
    tr = 256

    def kern(c_ref, d_ref, w_ref, m_ref, v_ref, g_ref, dl_ref, m2_ref, v2_ref):
        c = c_ref[...]
        cond = c * jax.nn.sigmoid(c)
        g = jnp.dot(cond, d_ref[0], precision=HI, preferred_element_type=F32)
        dl, m2, v2 = _adamw(w_ref[0], g, m_ref[0], v_ref[0])
        g_ref[0], dl_ref[0], m2_ref[0], v2_ref[0] = g, dl, m2, v2

    big = pl.BlockSpec((1, tr, ncol), lambda i, j: (i, j, 0))
    shp = jax.ShapeDtypeStruct(w.shape, F32)
    return pl.pallas_call(
        kern, name="ada_grad_adamw", grid=(2, D // tr),
        in_specs=[pl.BlockSpec((tr, NDEV), lambda i, j: (j, 0)),
                  pl.BlockSpec((1, NDEV, ncol), lambda i, j: (i, 0, 0)), big, big, big],
        out_specs=[big] * 4, out_shape=[shp] * 4,
        compiler_params=_cp(("parallel", "parallel")),
    )(cT_all, dmod_loc, w, m, v)


def _sum_adamw(recv, w, m, v, name, tr):
    S = recv.shape[0]
    R, C = w.shape
    assert R % tr == 0 and recv.shape[1:] == (R, C)

    def kern(r_ref, w_ref, m_ref, v_ref, g_ref, dl_ref, m2_ref, v2_ref):
        g = r_ref[0].astype(F32)
        for s in range(1, S):
            g = g + r_ref[s].astype(F32)
        dl, m2, v2 = _adamw(w_ref[...], g, m_ref[...], v_ref[...])
        g_ref[...], dl_ref[...], m2_ref[...], v2_ref[...] = g, dl, m2, v2

    flat = pl.BlockSpec((tr, C), lambda i: (i, 0))
    shp = jax.ShapeDtypeStruct((R, C), F32)
    return pl.pallas_call(
        kern, name=name, grid=(R // tr,),
        in_specs=[pl.BlockSpec((S, tr, C), lambda i: (0, i, 0)), flat, flat, flat],
        out_specs=[flat] * 4, out_shape=[shp] * 4,
        compiler_params=_cp(("parallel",)),
    )(recv, w, m, v)


def _pack(arrs, dtype, row_mult):
    flat = jnp.concatenate([a.reshape(-1).astype(dtype) for a in arrs])
    flat = jnp.pad(flat, (0, -flat.shape[0] % (128 * row_mult)))
    return flat.reshape(-1, 128)


def _pack8(arrs, dtype, row_mult):
    flat = jnp.concatenate([a.reshape(NDEV, -1).astype(dtype) for a in arrs], axis=1)
    flat = jnp.pad(flat, ((0, 0), (0, -flat.shape[1] % (128 * row_mult))))
    return flat.reshape(NDEV, -1, 128)


def _unpack(buf, shapes, lead=()):
    flat = buf.reshape(lead + (-1,))
    out, off = [], 0
    for s in shapes:
        n = math.prod(s)
        out.append(flat[..., off:off + n].reshape(lead + tuple(s)))
        off += n
    return out


def _to_chunks(full, kind):
    if kind == "col":
        x = full.reshape(full.shape[:-1] + (NDEV, full.shape[-1] // NDEV))
        return jnp.moveaxis(x, -2, 0)
    x = full.reshape(full.shape[:-2] + (NDEV, full.shape[-2] // NDEV, full.shape[-1]))
    return jnp.moveaxis(x, -3, 0)


def _from_chunks(g8, kind):
    if kind == "col":
        x = jnp.moveaxis(g8, 0, -2)
        return x.reshape(x.shape[:-2] + (x.shape[-2] * x.shape[-1],))
    x = jnp.moveaxis(g8, 0, -3)
    return x.reshape(x.shape[:-3] + (x.shape[-3] * x.shape[-2], x.shape[-1]))


def _pad_pa(x):
    z = lambda n: jnp.zeros(x.shape[:-1] + (n,), x.dtype)
    return jnp.concatenate([x[..., :1600], z(64), x[..., 1600:1664], z(64), x[..., 1664:1824], z(96)], -1)


def _unpad_pa(x):
    return jnp.concatenate([x[..., :1600], x[..., 1664:1728], x[..., 1792:1952]], -1)


def _pad_rows(x, n):
    return jnp.pad(x, ((0, n - x.shape[0]), (0, 0)))


def _shift_down(x, n):
    return jnp.pad(x, ((n, 0), (0, 0)))[:-n]


def _shift_up(x, n):
    return jnp.pad(x, ((0, n), (0, 0)))[n:]


def _unheadsT(x):
    return x.transpose(2, 0, 1).reshape(T, RW)


def _perm(x, dil):
    C = x.shape[-1]
    return x.reshape(T // dil, dil, HEADS, C).transpose(2, 1, 0, 3).reshape(HEADS, T, C)


def _unperm(y, dil):
    C = y.shape[-1]
    return y.reshape(HEADS, dil, T // dil, C).transpose(2, 1, 0, 3).reshape(T, HEADS, C)


def _bucket_tables():
    qi = jnp.arange(BLK)[:, None]
    ki = jnp.arange(2 * BLK)[None, :]
    rel = BLK + qi - ki
    tabs = []
    for dil in DILS:
        dist = jnp.clip(rel, 0, BLK) * dil
        logd = jnp.log(jnp.maximum(dist, 1).astype(F32) / 16) / math.log(2048 / 16)
        large = jnp.minimum(16 + (logd * 16).astype(jnp.int32), 31)
        tabs.append(jnp.where(dist < 16, dist, large))
    return jnp.stack(tabs)


SHARDED = (("ln_g", "col"), ("ln_b", "col"), ("ab_w_in", "col"), ("rw_w_up", "col"), ("rw_a_up", "col"),
           ("rw_g_up", "col"), ("sc_conv_w", "col"), ("ab_w_out", "row"), ("dil_w_qkv", "col"),
           ("dil_w_out", "col"), ("mlp_w1", "col"), ("mlp_w2", "row"))
FIRST = ("ab_w_in", "ab_w_out")
LATER = ("dil_w_qkv", "dil_w_out", "mlp_w1", "mlp_w2")
GATHER_BF16 = FIRST + LATER
GATHER_F32 = ("rw_w_up", "rw_a_up", "rw_g_up", "sc_conv_w", "ln_g", "ln_b")
REPLICATED = ("ada_b", "rw_mu", "rw_w0", "rw_a0", "rw_k_k", "rw_k_a", "rw_r_k", "rw_lnx_g", "rw_lnx_b", "rel_bias")
WEIGHTS = ("ada_w", "ada_b", "ln_g", "ln_b", "ab_w_in", "rw_mu", "rw_w0", "rw_w_up", "rw_a0", "rw_a_up",
           "rw_g_up", "rw_k_k", "rw_k_a", "rw_r_k", "rw_lnx_g", "rw_lnx_b", "sc_conv_w", "ab_w_out",
           "dil_w_qkv", "dil_w_out", "rel_bias", "mlp_w1", "mlp_w2")
FLAT_TILE = 512


def _local_step(x0, tgt, mod, W, P, later_weights, early_grads):
    row = lambda a: a.reshape(1, -1)
    W = dict(W)
    m6 = mod.reshape(2, 6, 1, D)
    sc = [m6[0, 1], m6[0, 4], m6[1, 1], m6[1, 4]]
    sh = [m6[0, 0], m6[0, 3], m6[1, 0], m6[1, 3]]
    gt = [m6[0, 2], m6[0, 5], m6[1, 2], m6[1, 5]]
    lng = [row(P["ln_g"][0, 0]), row(P["ln_g"][0, 1]), row(P["ln_g"][1, 0]), row(P["ln_g"][1, 1])]
    lnb = [row(P["ln_b"][0, 0]), row(P["ln_b"][0, 1]), row(P["ln_b"][1, 0]), row(P["ln_b"][1, 1])]
    E = jnp.kron(jnp.eye(HEADS, dtype=BF16), jnp.ones((HD, HD), BF16))

    def mod_body(r, p):
        return [r[0] * (1.0 + p[0]) + p[1]], []

    (u0,), _ = _rows("modulate", mod_body, [x0], [sc[0], sh[0]], [(D, BF16)])

    def post_fwd_body(r, p):
        xn, un = _post_ln_mod(r[0], r[1], *p)
        return [xn, un], []

    def post_fwd(s, x, y):
        (xn, un), _ = _rows(f"post_ln_{s}", post_fwd_body, [x, y],
                            [gt[s], lng[s], lnb[s], sc[s + 1], sh[s + 1]], [(D, F32), (D, BF16)])
        return xn, un

    def relu2(acc):
        a = jnp.maximum(acc, 0.0)
        return acc, a * a

    def relu2_bwd(acc, h):
        return (acc * (2.0 * jnp.maximum(h, 0.0)),)

    p = _mm("ab_in", u0, W["ab_w_in"])
    p1 = _shift_down(p, 1)
    p2 = _shift_down(p[:, PA:], 2)
    mu = _pad_pa(P["rw_mu"])
    mu_parts = [mu[:, :512], mu[:, 512:1024], mu[:, 1024:1536], mu[:, 1536:1664], mu[:, 1664:1792], mu[:, 1792:]]
    pre_params = mu_parts + [P["rw_w0"], _pad_rows(P["rw_w_up"], 128), P["rw_a0"], _pad_rows(P["rw_a_up"], 128),
                             _pad_rows(P["rw_g_up"], 256), P["rw_k_k"], P["rw_k_a"],
                             P["sc_conv_w"][0:1], P["sc_conv_w"][1:2], P["sc_conv_w"][2:3]]
    pre_rows = [(p, 512, 0), (p, 512, 1), (p, 512, 2), (p, 128, 12), (p, 128, 13), (p, 256, 7),
                (p1, 512, 0), (p1, 512, 1), (p1, 512, 2), (p1, 128, 12), (p1, 128, 13), (p1, 256, 7),
                (p, 512, 4), (p, 512, 5), (p, 512, 6), (p1, 512, 4), (p1, 512, 6), (p2, 512, 0), (p2, 512, 2)]
    NPR = len(pre_rows)

    def pre_fwd_body(r, pp):
        outs = list(_pre_core(pp[0], *r, *pp[1:]))
        return outs + list(_split3(outs[3])), []

    (r_, w_, kh_, v_, a_, b_, gate_, yb, *v_parts), _ = _rows(
        "rwkv_pre", pre_fwd_body, pre_rows, [E] + pre_params,
        [(RW, F32)] * 7 + [(RW, BF16)] * 4, tm=256)
    scan_in = [r_, w_, kh_, a_, b_, _cols3(v_parts)]
    yT, ck = _scan_fwd(*scan_in)
    ysc = _unheadsT(yT)
    post_params = [P["rw_lnx_g"], P["rw_lnx_b"], P["rw_r_k"].reshape(1, RW)]

    def postmix_fwd_body(r, pp):
        return [_post_core(pp[0], *r, *pp[1:])], []

    (ya,), _ = _rows("rwkv_post", postmix_fwd_body, [ysc, r_, kh_, v_, gate_], [E] + post_params,
                     [(RW, BF16)], tm=256)
    cat = jnp.concatenate([ya, yb], axis=1)
    y0 = _mm("ab_out", cat, W["ab_w_out"])
    x1, u1 = post_fwd(0, x0, y0)
    W.update(later_weights(u1))

    h1, a1 = _mm("mlp1_up_0", u1, W["mlp_w1"][0], out=(F32, BF16), epi=relu2)
    y1 = _mm("mlp1_down_0", a1, W["mlp_w2"][0])
    x2, u2 = post_fwd(1, x1, y1)

    pq = _mm("qkv", u2, W["dil_w_qkv"], out=(BF16,))
    pq5 = pq.reshape(T, 3, 3, HEADS, HD)
    q = jnp.stack([_perm(pq5[:, g, 0], DILS[g]) for g in range(3)])
    kp = jnp.pad(jnp.stack([_perm(pq5[:, g, 1], DILS[g]) for g in range(3)]), ((0, 0), (0, 0), (BLK, 0), (0, 0)))
    vp = jnp.pad(jnp.stack([_perm(pq5[:, g, 2], DILS[g]) for g in range(3)]), ((0, 0), (0, 0), (BLK, 0), (0, 0)))
    onehotT = (_bucket_tables().reshape(3, 1, NPAIR) == jnp.arange(NBUCKET).reshape(1, NBUCKET, 1)).astype(BF16)
    rbT = P["rel_bias"].reshape(NBUCKET, 3, HEADS).transpose(1, 2, 0)
    bias = _relbias_table(rbT, onehotT).reshape(3, HEADS, BLK, 2 * BLK)
    og, lse = _attn_fwd(q, kp, vp, bias)
    R = T * HEADS
    o_nat = [_unperm(og[g], DILS[g]).reshape(R, HD) for g in range(3)]
    l_nat = [_unperm(lse[g], DILS[g]).reshape(R, 1) for g in range(3)]

    def merge_fwd_body(r, pp):
        return [_merge_core(*r)], []

    (om,), _ = _rows("attn_merge", merge_fwd_body, o_nat + l_nat, [], [(HD, BF16)], tm=1024)
    om = om.reshape(T, RW)
    y2 = _mm("dil_out", om, W["dil_w_out"])
    x3, u3 = post_fwd(2, x2, y2)

    h3, a3 = _mm("mlp1_up_1", u3, W["mlp_w1"][1], out=(F32, BF16), epi=relu2)
    y3 = _mm("mlp1_down_1", a3, W["mlp_w2"][1])

    def last_body(r, pp):
        x, y, tg = r
        xn, vjp = jax.vjp(_post_ln, x, y, *pp)
        err = xn - tg
        dx, dy, dg, dlg, dlb = vjp(err * (1.0 / D))
        loss = jnp.full((1, 128), (0.5 / D) * jnp.sum(err * err), F32)
        return [dx, dy], [loss, dg, dlg, dlb]

    (dxp, dy3), (loss_acc, dg3, dlng3, dlnb3) = _rows(
        "final_ln_loss", last_body, [x3, y3, tgt], [gt[3], lng[3], lnb[3]],
        [(D, F32), (D, BF16)], [(1, 128), (1, D), (1, D), (1, D)])

    G = {}
    dsc, dsh, dgt = [None] * 4, [None] * 4, [None] * 4
    dlng, dlnb = [None] * 4, [None] * 4
    dgt[3], dlng[3], dlnb[3] = dg3, dlng3, dlnb3

    def mlp_bwd(i, u, h, a, dy):
        dh = _mm(f"mlp_dh_{i}", dy, W["mlp_w2"][i], tb=True, out=(BF16,), epi=relu2_bwd, extras=(h,))
        gw2 = _mm(f"mlp_dw2_{i}", a.T, dy)
        du = _mm(f"mlp_du_{i}", dh, W["mlp_w1"][i], tb=True)
        gw1 = _mm(f"mlp_dw1_{i}", u.T, dh)
        return du, gw1, gw2

    def post_bwd_body(r, pp):
        x, y, dxn, dun = r
        _, vjp = jax.vjp(_post_ln_mod, x, y, *pp)
        dx, dy, dg, dlg, dlb, dscn, dshn = vjp((dxn, dun))
        return [dx, dy], [dg, dlg, dlb, dscn, dshn]

    def post_bwd(s, x, y, dxn, dun):
        (dx, dy), (dgt[s], dlng[s], dlnb[s], dsc[s + 1], dsh[s + 1]) = _rows(
            f"post_ln_bwd_{s}", post_bwd_body, [x, y, dxn, dun],
            [gt[s], lng[s], lnb[s], sc[s + 1], sh[s + 1]], [(D, F32), (D, BF16)], [(1, D)] * 5)
        return dx, dy

    du3, gw1_1, gw2_1 = mlp_bwd(1, u3, h3, a3, dy3)
    dxp, dy2 = post_bwd(2, x2, y2, dxp, du3)

    G["dil_w_out"] = _mm("dil_out_dw", om.T, dy2)[None]
    do = _mm("dil_out_dx", dy2, W["dil_w_out"], tb=True).reshape(R, HD)

    def merge_bwd_body(r, pp):
        o_l, dout = r[:6], r[6]
        _, vjp = jax.vjp(_merge_core, *o_l)
        d = vjp(dout)
        dcs = [d[3 + g] - jnp.sum(d[g] * o_l[g], axis=1, keepdims=True) for g in range(3)]
        return list(d[:3]) + dcs, []

    mb, _ = _rows("attn_merge_bwd", merge_bwd_body, o_nat + l_nat + [do], [],
                  [(HD, BF16)] * 3 + [(1, F32)] * 3, tm=1024)
    dog = jnp.stack([_perm(mb[g].reshape(T, HEADS, HD), DILS[g]) for g in range(3)])
    dcc = jnp.stack([_perm(mb[3 + g].reshape(T, HEADS, 1), DILS[g]) for g in range(3)])
    dq, dkp, dvp, dbias = _attn_bwd(q, kp, vp, bias, jnp.swapaxes(bias, 2, 3), dog, lse,
                                    lse.reshape(3, HEADS, 1, T), dcc, dcc.reshape(3, HEADS, 1, T))
    dpq = jnp.concatenate(
        [_unperm(t[g], DILS[g]).reshape(T, RW) for g in range(3) for t in (dq, dkp[:, :, BLK:], dvp[:, :, BLK:])],
        axis=1).astype(BF16)
    rb = _relbias_grad(dbias.reshape(3, HEADS, NPAIR), onehotT)
    G["rel_bias"] = rb.transpose(2, 0, 1).reshape(NBUCKET, 3 * HEADS)
    G["dil_w_qkv"] = _mm("qkv_dw", u2.T, dpq)[None]
    du2 = _mm("qkv_dx", dpq, W["dil_w_qkv"], tb=True)
    dxp, dy1 = post_bwd(1, x1, y1, dxp, du2)

    du1, gw1_0, gw2_0 = mlp_bwd(0, u1, h1, a1, dy1)
    G["mlp_w1"] = jnp.stack([gw1_0, gw1_1])
    G["mlp_w2"] = jnp.stack([gw2_0, gw2_1])
    gt[0] = gt[0] + early_grads(G)
    dxp, dy0 = post_bwd(0, x0, y0, dxp, du1)

    G["ab_w_out"] = _mm("ab_out_dw", cat.T, dy0)[None]
    dcat = _mm("ab_out_dx", dy0, W["ab_w_out"], tb=True)

    def postmix_bwd_body(r, pp):
        _, vjp = jax.vjp(functools.partial(_post_core, pp[0]), *r[:5], *pp[1:])
        d = vjp(r[5])
        return list(_split3(d[0])) + list(d[1:5]), list(d[5:])

    (*dy_parts, dr1, dkh1, dv1, dgate), (G["rw_lnx_g"], G["rw_lnx_b"], drk) = _rows(
        "rwkv_post_bwd", postmix_bwd_body, [ysc, r_, kh_, v_, gate_, (dcat, 512, 0)], [E] + post_params,
        [(RW, BF16)] * 3 + [(RW, F32)] * 4, [(1, RW)] * 3, tm=256)
    G["rw_r_k"] = drk.reshape(1, HEADS, HD)
    dr2, dw2, dk2, da2, db2, dvT = _scan_bwd(*scan_in, _cols3(dy_parts), ck)
    dv2 = _unheadsT(dvT)

    def pre_bwd_body(r, pp):
        prim, ct = r[:NPR], r[NPR:]
        _, vjp = jax.vjp(functools.partial(_pre_core, pp[0]), *prim, *pp[1:])
        cts = (ct[0] + ct[1], ct[2], ct[3] + ct[4], ct[5] + ct[6], ct[7], ct[8], ct[9], ct[10])
        d = vjp(cts)
        z = jnp.zeros_like(d[12])
        dp = jnp.concatenate([d[0], d[1], d[2], d[3], d[4], d[5], d[12], d[13], d[14]], axis=1)
        dp1 = jnp.concatenate([d[6], d[7], d[8], d[9], d[10], d[11], d[15], z, d[16]], axis=1)
        dp2 = jnp.concatenate([d[17], z, d[18]], axis=1)
        return [dp, dp1, dp2], list(d[NPR:])

    acc_shapes = [a.shape for a in pre_params]
    (dp, dp1, dp2), pacc = _rows(
        "rwkv_pre_bwd", pre_bwd_body,
        pre_rows + [dr1, dr2, dw2, dkh1, dk2, dv1, dv2, da2, db2, dgate, (dcat, 512, 1)],
        [E] + pre_params, [(PAB, F32), (PAB, F32), (PB, F32)], acc_shapes, tm=256)
    G["rw_mu"] = _unpad_pa(jnp.concatenate(pacc[:6], axis=1))
    G["rw_w0"], G["rw_a0"], G["rw_k_k"], G["rw_k_a"] = pacc[6], pacc[8], pacc[11], pacc[12]
    G["rw_w_up"] = pacc[7][None, :64]
    G["rw_a_up"] = pacc[9][None, :64]
    G["rw_g_up"] = pacc[10][None, :160]
    G["sc_conv_w"] = jnp.concatenate(pacc[13:16], axis=0)[None]

    def add3_body(r, pp):
        return [r[0] + r[1] + r[2]], []

    (dpt,), _ = _rows("shift_merge", add3_body,
                      [dp, _shift_up(dp1, 1), jnp.pad(_shift_up(dp2, 2), ((0, 0), (PA, 0)))], [], [(PAB, BF16)])
    gin = _mm("ab_in_dw", u0.T, dpt)
    G["ab_w_in"] = jnp.concatenate([_unpad_pa(gin[:, :PA]), gin[:, PA:]], axis=1)[None]
    du0 = _mm("ab_in_dx", dpt, W["ab_w_in"], tb=True)

    def mod_bwd_body(r, pp):
        du, dx, x = r
        return [dx + du * (1.0 + pp[0])], [jnp.sum(du * x, axis=0, keepdims=True), jnp.sum(du, axis=0, keepdims=True)]

    (grad_x,), (dsc[0], dsh[0]) = _rows("modulate_bwd", mod_bwd_body, [du0, dxp, x0], [sc[0]], [(D, F32)],
                                        [(1, D), (1, D)])

    G["ln_g"] = jnp.concatenate(dlng, axis=0).reshape(2, 2, D)
    G["ln_b"] = jnp.concatenate(dlnb, axis=0).reshape(2, 2, D)
    dmod = jnp.concatenate([dsh[0], dsc[0], dgt[0], dsh[1], dsc[1], dgt[1],
                            dsh[2], dsc[2], dgt[2], dsh[3], dsc[3], dgt[3]], axis=1).reshape(2, 6 * D)
    return loss_acc[0, 0], grad_x, dmod, G


def kernel(x, c, ada_w, ada_b, ln_g, ln_b, ab_w_in, rw_mu, rw_w0, rw_w_up, rw_a0, rw_a_up, rw_g_up, rw_k_k, rw_k_a, rw_r_k, rw_lnx_g, rw_lnx_b, sc_conv_w, ab_w_out, dil_w_qkv, dil_w_out, rel_bias, mlp_w1, mlp_w2, loss_target, m_ada_w, m_ada_b, m_ln_g, m_ln_b, m_ab_w_in, m_rw_mu, m_rw_w0, m_rw_w_up, m_rw_a0, m_rw_a_up, m_rw_g_up, m_rw_k_k, m_rw_k_a, m_rw_r_k, m_rw_lnx_g, m_rw_lnx_b, m_sc_conv_w, m_ab_w_out, m_dil_w_qkv, m_dil_w_out, m_rel_bias, m_mlp_w1, m_mlp_w2, v_ada_w, v_ada_b, v_ln_g, v_ln_b, v_ab_w_in, v_rw_mu, v_rw_w0, v_rw_w_up, v_rw_a0, v_rw_a_up, v_rw_g_up, v_rw_k_k, v_rw_k_a, v_rw_r_k, v_rw_lnx_g, v_rw_lnx_b, v_sc_conv_w, v_ab_w_out, v_dil_w_qkv, v_dil_w_out, v_rel_bias, v_mlp_w1, v_mlp_w2):
    w = dict(ada_w=ada_w, ada_b=ada_b, ln_g=ln_g, ln_b=ln_b, ab_w_in=ab_w_in, rw_mu=rw_mu, rw_w0=rw_w0,
             rw_w_up=rw_w_up, rw_a0=rw_a0, rw_a_up=rw_a_up, rw_g_up=rw_g_up, rw_k_k=rw_k_k, rw_k_a=rw_k_a,
             rw_r_k=rw_r_k, rw_lnx_g=rw_lnx_g, rw_lnx_b=rw_lnx_b, sc_conv_w=sc_conv_w, ab_w_out=ab_w_out,
             dil_w_qkv=dil_w_qkv, dil_w_out=dil_w_out, rel_bias=rel_bias, mlp_w1=mlp_w1, mlp_w2=mlp_w2)
    m = dict(ada_w=m_ada_w, ada_b=m_ada_b, ln_g=m_ln_g, ln_b=m_ln_b, ab_w_in=m_ab_w_in, rw_mu=m_rw_mu,
             rw_w0=m_rw_w0, rw_w_up=m_rw_w_up, rw_a0=m_rw_a0, rw_a_up=m_rw_a_up, rw_g_up=m_rw_g_up,
             rw_k_k=m_rw_k_k, rw_k_a=m_rw_k_a, rw_r_k=m_rw_r_k, rw_lnx_g=m_rw_lnx_g, rw_lnx_b=m_rw_lnx_b,
             sc_conv_w=m_sc_conv_w, ab_w_out=m_ab_w_out, dil_w_qkv=m_dil_w_qkv, dil_w_out=m_dil_w_out,
             rel_bias=m_rel_bias, mlp_w1=m_mlp_w1, mlp_w2=m_mlp_w2)
    v = dict(ada_w=v_ada_w, ada_b=v_ada_b, ln_g=v_ln_g, ln_b=v_ln_b, ab_w_in=v_ab_w_in, rw_mu=v_rw_mu,
             rw_w0=v_rw_w0, rw_w_up=v_rw_w_up, rw_a0=v_rw_a0, rw_a_up=v_rw_a_up, rw_g_up=v_rw_g_up,
             rw_k_k=v_rw_k_k, rw_k_a=v_rw_k_a, rw_r_k=v_rw_r_k, rw_lnx_g=v_rw_lnx_g, rw_lnx_b=v_rw_lnx_b,
             sc_conv_w=v_sc_conv_w, ab_w_out=v_ab_w_out, dil_w_qkv=v_dil_w_qkv, dil_w_out=v_dil_w_out,
             rel_bias=v_rel_bias, mlp_w1=v_mlp_w1, mlp_w2=v_mlp_w2)
    kinds = dict(SHARDED)
    me = 4 * lax.axis_index("x") + 2 * lax.axis_index("y") + lax.axis_index("c")
    ncol = ada_w.shape[2]

    small = _all_gather(_pack([c] + [w[n] for n in GATHER_F32], F32, 8), "gather_small")
    parts = _unpack(small, [c.shape] + [w[n].shape for n in GATHER_F32], (NDEV,))
    c_all = parts[0].reshape(NDEV, D)
    P = {n: _from_chunks(t, kinds[n]) for n, t in zip(GATHER_F32, parts[1:])}
    P = {n: (t if n in ("ln_g", "ln_b") else t[0]) for n, t in P.items()}
    for n in REPLICATED[1:]:
        P[n] = w[n]
    def full(n, t):
        t = _from_chunks(t, kinds[n])
        return t if n in ("mlp_w1", "mlp_w2") else t[0]

    parts = _all_gather_many([w[n].astype(BF16) for n in FIRST], "gather_first_weights")
    W = {n: full(n, t) for n, t in zip(FIRST, parts)}
    W["ab_w_in"] = jnp.concatenate([_pad_pa(W["ab_w_in"][:, :1824]), W["ab_w_in"][:, 1824:]], axis=1)

    ada_b_loc = lax.dynamic_slice(ada_b, (0, ncol * me), (2, ncol))
    mod_part = _ada_mod(c_all, ada_w, ada_b_loc)
    mod_all = _all_gather(mod_part.reshape(-1, 128), "gather_mod").reshape(NDEV, 2, NDEV, ncol)
    mod = lax.dynamic_index_in_dim(mod_all, me, axis=2, keepdims=False)
    mod = mod.transpose(1, 0, 2).reshape(2, 6 * D)

    behind = (mod[0, 0] * 0.0).astype(BF16)
    later = _exchange_start([w[n].astype(BF16) + (behind if n == LATER[0] else 0) for n in LATER], True,
                            "gather_later_weights_start")
    mod = mod + later[-1][0, 0]

    def later_weights(after):
        lands = _exchange_wait(later, True, after, "gather_later_weights_wait")
        return {n: full(n, t) for n, t in zip(LATER, lands)}

    sent = []

    def early_grads(G):
        sent.append(_exchange_start([_to_chunks(G[n], kinds[n]).astype(BF16) for n in LATER], False,
                                    "exchange_later_grads_start"))
        return sent[0][-1][0, 0]

    loss_part, grad_x, dmod, G = _local_step(x[0], loss_target[0], mod, W, P, later_weights, early_grads)
    G["ada_b"] = dmod
    loss = lax.psum(loss_part, ("x", "y", "c"))

    rep_shapes = [w[n].shape for n in REPLICATED]
    rep_all = _all_gather(_pack([G[n] for n in REPLICATED], F32, 8), "gather_replicated_grads")
    pk = lambda d: _pack([d[n] for n in REPLICATED], F32, 8)
    rep_out = _sum_adamw(rep_all, pk(w), pk(m), pk(v), "sum_adamw_replicated", rep_all.shape[1])
    rep_out = [dict(zip(REPLICATED, _unpack(o, rep_shapes))) for o in rep_out]

    dmod_all = _unpack(rep_all, [(2, 6 * D)], (NDEV,))[0]
    dmod_loc = lax.dynamic_slice(dmod_all, (0, 0, ncol * me), (NDEV, 2, ncol)).transpose(1, 0, 2)
    ada_out = _ada_grad_adamw(c_all.T, dmod_loc, ada_w, m_ada_w, v_ada_w)

    names = [n for n, _ in SHARDED if n not in GATHER_BF16]
    shard_shapes = [w[n].shape for n in names]
    chunks = _pack8([_to_chunks(G[n], kinds[n]) for n in names], F32, 8)
    recv = _all_to_all(chunks, "exchange_small_grads")
    pk = lambda d: _pack([d[n] for n in names], F32, 8)
    sh_out = _sum_adamw(recv, pk(w), pk(m), pk(v), "sum_adamw_small", recv.shape[1])
    sh_out = [dict(zip(names, _unpack(o, shard_shapes))) for o in sh_out]

    big_out = {}

    def update(n, contributions):
        cols = w[n].shape[-1]
        flat = lambda t: t.reshape(-1, cols)
        rows = flat(w[n]).shape[0]
        outs = _sum_adamw(contributions.reshape(-1, rows, cols), flat(w[n]), flat(m[n]), flat(v[n]),
                          f"sum_adamw_{n}", min(rows, 256))
        big_out[n] = [o.reshape(w[n].shape) for o in outs]

    ci = lax.axis_index("c")
    mine_l, sib_l = [], []
    for n in FIRST:
        g8 = _to_chunks(G[n], kinds[n])
        g42 = g8.reshape((4, 2) + g8.shape[1:])
        mine_l.append(lax.dynamic_index_in_dim(g42, ci, 1, keepdims=False))
        sib_l.append(lax.dynamic_index_in_dim(g42, 1 - ci, 1, keepdims=False))
    from_sib = _swap_sibling(sib_l, "swap_sibling_grads")

    def add2_body(r, pp):
        return [r[0] + r[1]], []

    partials = []
    for n, a, b in zip(FIRST, mine_l, from_sib):
        cols = a.shape[-1]
        (p,), _ = _rows(f"pair_sum_{n}", add2_body, [a.reshape(-1, cols), b.reshape(-1, cols)], [],
                        [(cols, BF16)], tm=512)
        partials.append(p.reshape(a.shape))
    for n, r in zip(FIRST, _exchange_chips(partials, "exchange_chip_grads")):
        update(n, r)

    for n, r in zip(LATER, _exchange_wait(sent[0], False, partials[0], "exchange_later_grads_wait")):
        update(n, r)
    sh_out = [{**d, **{n: big_out[n][i] for n in GATHER_BF16}} for i, d in enumerate(sh_out)]

    def pick(i, n):
        if n == "ada_w":
            return ada_out[i]
        return rep_out[i][n] if n in REPLICATED else sh_out[i][n]

    outs = [loss, grad_x[None]]
    for i in range(4):
        outs += [pick(i, n) for n in WEIGHTS]
    return tuple(outs)
```

```python
import functools
import math

import jax
import jax.numpy as jnp
from jax import lax
from jax.experimental import pallas as pl
from jax.experimental.pallas import tpu as pltpu

F32 = jnp.float32
BF16 = jnp.bfloat16
HI = lax.Precision.HIGHEST

NDEV = 8
T = 2048
D = 1024
DFF = 4096
HEADS = 8
HD = 64
RW = 512
PA = 2048
PB = 1536
PAB = PA + PB
QKV = 4608
DILS = (1, 4, 16)
BLK = 128
ALPHA = 4.0 ** 0.25
LN_EPS = 1e-5
GN_EPS = 64e-5
ADAM_LR, ADAM_B1, ADAM_B2, ADAM_EPS, ADAM_WD, ADAM_STEP = 0.001, 0.9, 0.999, 1e-8, 0.01, 10
VMEM_LIMIT = 56 * 1024 * 1024


def _cp(sem):
    return pltpu.CompilerParams(dimension_semantics=sem, vmem_limit_bytes=VMEM_LIMIT)


def _slot(px, py, pc):
    return 4 * px + 2 * py + pc


def _all_gather(x, name):
    R, C = x.shape

    def body(x_ref, out_ref, send_sems, recv_sems, local_sem):
        xi, yi, ci = lax.axis_index("x"), lax.axis_index("y"), lax.axis_index("c")
        me, sibling = (xi, yi, ci), (xi, yi, 1 - ci)
        chips = [(1 - xi, yi), (xi, 1 - yi), (1 - xi, 1 - yi)]

        def rows(px, py, pc):
            return out_ref.at[_slot(px, py, pc)]

        def copy(k, block, to, src=None):
            return pltpu.make_async_remote_copy(
                src_ref=rows(*block) if src is None else src, dst_ref=rows(*block),
                send_sem=send_sems.at[k], recv_sem=recv_sems.at[k],
                device_id=to, device_id_type=pl.DeviceIdType.MESH)

        mine = pltpu.make_async_copy(x_ref, rows(*me), local_sem)
        mine.start()
        first = [copy(0, me, sibling, src=x_ref)]
        first += [copy(1 + j, me, (*chip, ci), src=x_ref) for j, chip in enumerate(chips)]
        for cp in first:
            cp.start()
        passed = [copy(4 + j, (*chip, ci), sibling) for j, chip in enumerate(chips)]
        for j, chip in enumerate(chips):
            copy(1 + j, (*chip, ci), me).wait_recv()
            passed[j].start()
        copy(0, sibling, me).wait_recv()
        for j, chip in enumerate(chips):
            copy(4 + j, (*chip, 1 - ci), me).wait_recv()
        for cp in first + passed:
            cp.wait_send()
        mine.wait()

    return pl.pallas_call(
        body, name=name,
        out_shape=jax.ShapeDtypeStruct((NDEV, R, C), x.dtype),
        in_specs=[pl.BlockSpec(memory_space=pl.ANY)],
        out_specs=pl.BlockSpec(memory_space=pl.ANY),
        scratch_shapes=[pltpu.SemaphoreType.DMA((7,)), pltpu.SemaphoreType.DMA((7,)),
                        pltpu.SemaphoreType.DMA(())],
    )(x)


def _all_to_all(g, name):
    _, R, C = g.shape

    def body(g_ref, out_ref, send_sems, recv_sems, local_sem):
        xi, yi, ci = lax.axis_index("x"), lax.axis_index("y"), lax.axis_index("c")
        my_slot = _slot(xi, yi, ci)
        mine = pltpu.make_async_copy(g_ref.at[my_slot], out_ref.at[my_slot], local_sem)
        mine.start()
        copies = []
        for k in range(1, 8):
            px = 1 - xi if k & 4 else xi
            py = 1 - yi if k & 2 else yi
            pc = 1 - ci if k & 1 else ci
            peer_slot = _slot(px, py, pc)
            copies.append((
                pltpu.make_async_remote_copy(
                    src_ref=g_ref.at[peer_slot], dst_ref=out_ref.at[my_slot],
                    send_sem=send_sems.at[k - 1], recv_sem=recv_sems.at[k - 1],
                    device_id=(px, py, pc), device_id_type=pl.DeviceIdType.MESH),
                pltpu.make_async_remote_copy(
                    src_ref=g_ref.at[peer_slot], dst_ref=out_ref.at[peer_slot],
                    send_sem=send_sems.at[k - 1], recv_sem=recv_sems.at[k - 1],
                    device_id=(px, py, pc), device_id_type=pl.DeviceIdType.MESH)))
        for send, _ in copies:
            send.start()
        for _, recv in copies:
            recv.wait_recv()
        for send, _ in copies:
            send.wait_send()
        mine.wait()

    return pl.pallas_call(
        body, name=name,
        out_shape=jax.ShapeDtypeStruct((NDEV, R, C), g.dtype),
        in_specs=[pl.BlockSpec(memory_space=pl.ANY)],
        out_specs=pl.BlockSpec(memory_space=pl.ANY),
        scratch_shapes=[pltpu.SemaphoreType.DMA((7,)), pltpu.SemaphoreType.DMA((7,)),
                        pltpu.SemaphoreType.DMA(())],
    )(g)


def _my_slot():
    return _slot(lax.axis_index("x"), lax.axis_index("y"), lax.axis_index("c"))


def _put_own(buf, own, slot):
    return lax.dynamic_update_index_in_dim(buf, own, slot, 0)


def _hbm_call(body, name, ins, out_shapes, n_sems):
    anyspec = pl.BlockSpec(memory_space=pl.ANY)
    return pl.pallas_call(
        body, name=name, out_shape=out_shapes,
        in_specs=[anyspec] * len(ins), out_specs=[anyspec] * len(out_shapes),
        scratch_shapes=[pltpu.SemaphoreType.DMA(s) for s in n_sems],
    )(*ins)


def _all_gather_many(xs, name):
    n = len(xs)

    def body(*refs):
        x_refs, o_refs = refs[:n], refs[n:2 * n]
        send_sems, recv_sems = refs[2 * n:]
        xi, yi, ci = lax.axis_index("x"), lax.axis_index("y"), lax.axis_index("c")
        me, sibling = (xi, yi, ci), (xi, yi, 1 - ci)
        chips = [(1 - xi, yi), (xi, 1 - yi), (1 - xi, 1 - yi)]

        def copy(i, k, block, to, src=None):
            dst = o_refs[i].at[_slot(*block)]
            return pltpu.make_async_remote_copy(
                src_ref=dst if src is None else src, dst_ref=dst,
                send_sem=send_sems.at[i, k], recv_sem=recv_sems.at[i, k],
                device_id=to, device_id_type=pl.DeviceIdType.MESH)

        sends = []
        for i in range(n):
            sends += [copy(i, 1 + j, me, (*chip, ci), src=x_refs[i]) for j, chip in enumerate(chips)]
            sends.append(copy(i, 0, me, sibling, src=x_refs[i]))
        for cp in sends:
            cp.start()
        for j, chip in enumerate(chips):
            for i in range(n):
                copy(i, 1 + j, (*chip, ci), me).wait_recv()
                passed = copy(i, 4 + j, (*chip, ci), sibling)
                passed.start()
                sends.append(passed)
        for i in range(n):
            copy(i, 0, sibling, me).wait_recv()
            for j, chip in enumerate(chips):
                copy(i, 4 + j, (*chip, 1 - ci), me).wait_recv()
        for cp in sends:
            cp.wait_send()

    outs = _hbm_call(body, name, xs, [jax.ShapeDtypeStruct((NDEV,) + x.shape, x.dtype) for x in xs],
                     [(n, 7), (n, 7)])
    return [_put_own(o, x[None], _my_slot()) for o, x in zip(outs, xs)]


def _swap_sibling(gs, name):
    n = len(gs)

    def body(*refs):
        g_refs, o_refs = refs[:n], refs[n:2 * n]
        send_sems, recv_sems = refs[2 * n:]
        sibling = (lax.axis_index("x"), lax.axis_index("y"), 1 - lax.axis_index("c"))
        copies = [pltpu.make_async_remote_copy(
            src_ref=g_refs[i], dst_ref=o_refs[i], send_sem=send_sems.at[i], recv_sem=recv_sems.at[i],
            device_id=sibling, device_id_type=pl.DeviceIdType.MESH) for i in range(n)]
        for cp in copies:
            cp.start()
        for cp in copies:
            cp.wait_recv()
        for cp in copies:
            cp.wait_send()

    return _hbm_call(body, name, gs, [jax.ShapeDtypeStruct(g.shape, g.dtype) for g in gs], [(n,), (n,)])


def _exchange_chips(ps, name):
    n = len(ps)

    def body(*refs):
        p_refs, o_refs = refs[:n], refs[n:2 * n]
        send_sems, recv_sems = refs[2 * n:]
        xi, yi, ci = lax.axis_index("x"), lax.axis_index("y"), lax.axis_index("c")
        q_me = 2 * xi + yi
        sends, recvs = [], []
        for k in range(1, 4):
            px = 1 - xi if k & 2 else xi
            py = 1 - yi if k & 1 else yi
            q_peer = 2 * px + py
            for i in range(n):
                sends.append(pltpu.make_async_remote_copy(
                    src_ref=p_refs[i].at[q_peer], dst_ref=o_refs[i].at[q_me],
                    send_sem=send_sems.at[i, k - 1], recv_sem=recv_sems.at[i, k - 1],
                    device_id=(px, py, ci), device_id_type=pl.DeviceIdType.MESH))
                recvs.append(pltpu.make_async_remote_copy(
                    src_ref=p_refs[i].at[q_peer], dst_ref=o_refs[i].at[q_peer],
                    send_sem=send_sems.at[i, k - 1], recv_sem=recv_sems.at[i, k - 1],
                    device_id=(px, py, ci), device_id_type=pl.DeviceIdType.MESH))
        for cp in sends:
            cp.start()
        for cp in recvs:
            cp.wait_recv()
        for cp in sends:
            cp.wait_send()

    outs = _hbm_call(body, name, ps, [jax.ShapeDtypeStruct(p.shape, p.dtype) for p in ps], [(n, 3), (n, 3)])
    q_me = 2 * lax.axis_index("x") + lax.axis_index("y")
    return [_put_own(o, lax.dynamic_index_in_dim(p, q_me, 0, keepdims=True), q_me) for o, p in zip(outs, ps)]


def _peers(xi, yi, ci):
    return [(1 - xi if k & 4 else xi, 1 - yi if k & 2 else yi, 1 - ci if k & 1 else ci) for k in range(1, 8)]


def _direct_copy(src_refs, land_refs, send_sems, recv_sems, i, k, peer, my_slot, gather):
    src = src_refs[i] if gather else src_refs[i].at[_slot(*peer)]
    return pltpu.make_async_remote_copy(
        src_ref=src, dst_ref=land_refs[i].at[my_slot], send_sem=send_sems.at[7 * i + k], recv_sem=recv_sems.at[7 * i + k],
        device_id=peer, device_id_type=pl.DeviceIdType.MESH)


def _exchange_start(srcs, gather, name):
    n = len(srcs)
    lands = [lax.empty(((NDEV,) + s.shape) if gather else s.shape, s.dtype) for s in srcs]

    def body(*refs):
        s_refs, l_refs = refs[:n], refs[n:2 * n]
        send_sems, recv_sems = refs[2 * n], refs[2 * n + 1]
        token = refs[2 * n + 2 + 2 * n]
        xi, yi, ci = lax.axis_index("x"), lax.axis_index("y"), lax.axis_index("c")
        my_slot = _slot(xi, yi, ci)
        for k, peer in enumerate(_peers(xi, yi, ci)):
            for i in range(n):
                _direct_copy(s_refs, l_refs, send_sems, recv_sems, i, k, peer, my_slot, gather).start()
        token[...] = jnp.zeros_like(token)

    hbm = pl.BlockSpec(memory_space=pltpu.HBM)
    sem = pl.BlockSpec(memory_space=pltpu.SEMAPHORE)
    both = list(srcs) + lands
    return pl.pallas_call(
        body, name=name,
        out_shape=(pltpu.SemaphoreType.DMA((7 * n,)), pltpu.SemaphoreType.DMA((7 * n,)),
                   *[pltpu.HBM(t.shape, t.dtype) for t in both], jax.ShapeDtypeStruct((8, 128), F32)),
        in_specs=[hbm] * (2 * n),
        out_specs=(sem, sem, *[hbm] * (2 * n), pl.BlockSpec(memory_space=pltpu.VMEM)),
        input_output_aliases={i: 2 + i for i in range(2 * n)},
        compiler_params=pltpu.CompilerParams(has_side_effects=pltpu.SideEffectType.DATAFLOW_SIDE_EFFECTING),
    )(*[pltpu.with_memory_space_constraint(t, pltpu.HBM) for t in both])


def _exchange_wait(started, gather, after, name):
    send_sems, recv_sems, *thru, _ = started
    n = len(thru) // 2

    def body(*refs):
        s_refs, l_refs = refs[:n], refs[n:2 * n]
        send_sems, recv_sems = refs[2 * n], refs[2 * n + 1]
        xi, yi, ci = lax.axis_index("x"), lax.axis_index("y"), lax.axis_index("c")
        my_slot = _slot(xi, yi, ci)
        for k, peer in enumerate(_peers(xi, yi, ci)):
            for i in range(n):
                _direct_copy(s_refs, l_refs, send_sems, recv_sems, i, k, peer, my_slot, gather).wait_send()
                _direct_copy(s_refs, l_refs, send_sems, recv_sems, i, k, peer, _slot(*peer), gather).wait_recv()

    hbm = pl.BlockSpec(memory_space=pltpu.HBM)
    sem = pl.BlockSpec(memory_space=pltpu.SEMAPHORE)
    outs = pl.pallas_call(
        body, name=name,
        out_shape=tuple(pltpu.HBM(t.shape, t.dtype) for t in thru),
        in_specs=[hbm] * (2 * n) + [sem, sem, pl.BlockSpec(memory_space=pl.ANY)],
        out_specs=tuple([hbm] * (2 * n)),
        input_output_aliases={i: i for i in range(2 * n)},
        compiler_params=pltpu.CompilerParams(has_side_effects=pltpu.SideEffectType.DATAFLOW_SIDE_EFFECTING),
    )(*thru, send_sems, recv_sems, after)
    slot = _my_slot()
    own = [s[None] if gather else lax.dynamic_index_in_dim(s, slot, 0, keepdims=True) for s in outs[:n]]
    return [_put_own(land, o, slot) for land, o in zip(outs[n:], own)]


def _mm(name, a, b, tb=False, out=(F32,), epi=None, extras=(), tm=1024, tn=512, tk_cap=2048):
    M, K = a.shape
    N = b.shape[0] if tb else b.shape[1]
    tm, tn = min(tm, M), min(tn, N)
    tk = max(t for t in range(128, min(K, tk_cap) + 1, 128) if K % t == 0)
    assert M % tm == 0 and N % tn == 0 and K % tk == 0, (name, M, N, K)
    nk = K // tk
    ne, no = len(extras), len(out)
    dims = (((1,), (1 if tb else 0,)), ((), ()))

    def kern(*refs):
        a_ref, b_ref = refs[:2]
        e_refs = refs[2:2 + ne]
        o_refs = refs[2 + ne:2 + ne + no]

        def finish(acc):
            outs = epi(acc, *[e[...] for e in e_refs]) if epi is not None else (acc,)
            for o_ref, o in zip(o_refs, outs):
                o_ref[...] = o.astype(o_ref.dtype)

        part = lax.dot_general(a_ref[...], b_ref[...], dims, preferred_element_type=F32)
        if nk == 1:
            finish(part)
            return
        acc_ref = refs[-1]
        k = pl.program_id(2)

        @pl.when(k == 0)
        def _():
            acc_ref[...] = part

        @pl.when(k > 0)
        def _():
            acc_ref[...] += part

        @pl.when(k == nk - 1)
        def _():
            finish(acc_ref[...])

    b_spec = (pl.BlockSpec((tn, tk), lambda i, j, k: (j, k)) if tb
              else pl.BlockSpec((tk, tn), lambda i, j, k: (k, j)))
    tile = pl.BlockSpec((tm, tn), lambda i, j, k: (i, j))
    res = pl.pallas_call(
        kern, name=name, grid=(M // tm, N // tn, nk),
        in_specs=[pl.BlockSpec((tm, tk), lambda i, j, k: (i, k)), b_spec] + [tile] * ne,
        out_specs=[tile] * no,
        out_shape=[jax.ShapeDtypeStruct((M, N), dt) for dt in out],
        scratch_shapes=[pltpu.VMEM((tm, tn), F32)] if nk > 1 else [],
        compiler_params=_cp(("parallel", "parallel", "arbitrary")),
    )(a, b, *extras)
    return res[0] if no == 1 else res


def _rows(name, body, rows, params, out_rows, out_accs=(), tm=256):
    views = [r if isinstance(r, tuple) else (r, r.shape[1], 0) for r in rows]
    n = views[0][0].shape[0]
    assert n % tm == 0
    nr, npar, nor, noa = len(views), len(params), len(out_rows), len(out_accs)

    def kern(*refs):
        r_refs = refs[:nr]
        p_refs = refs[nr:nr + npar]
        o_refs = refs[nr + npar:nr + npar + nor]
        a_refs = refs[nr + npar + nor:]
        outs, accs = body([r[...] for r in r_refs], [p[...] for p in p_refs])
        assert len(outs) == nor and len(accs) == noa, (name, len(outs), len(accs))
        for o_ref, o in zip(o_refs, outs):
            o_ref[...] = o.astype(o_ref.dtype)
        if noa:
            @pl.when(pl.program_id(0) == 0)
            def _():
                for a_ref in a_refs:
                    a_ref[...] = jnp.zeros_like(a_ref)

            for a_ref, a in zip(a_refs, accs):
                a_ref[...] += a.astype(F32)

    def whole(shape):
        nd = len(shape)
        return pl.BlockSpec(tuple(shape), lambda i, nd=nd: (0,) * nd)

    in_specs = [pl.BlockSpec((tm, w), lambda i, cb=cb: (i, cb)) for _, w, cb in views]
    in_specs += [whole(p.shape) for p in params]
    out_specs = [pl.BlockSpec((tm, c), lambda i: (i, 0)) for c, _ in out_rows]
    out_specs += [whole(s) for s in out_accs]
    out_shape = [jax.ShapeDtypeStruct((n, c), dt) for c, dt in out_rows]
    out_shape += [jax.ShapeDtypeStruct(tuple(s), F32) for s in out_accs]
    res = pl.pallas_call(
        kern, name=name, grid=(n // tm,), in_specs=in_specs, out_specs=out_specs,
        out_shape=out_shape, compiler_params=_cp(("arbitrary",)),
    )(*[v[0] for v in views], *params)
    return res[:nor], res[nor:]


@jax.custom_vjp
def _headsum(x, e):
    return sum(jnp.dot(p, e, preferred_element_type=F32) for p in _split3(x))


_headsum.defvjp(lambda x, e: (_headsum(x, e), e), lambda e, ct: (_headsum(ct, e), None))


def _softplus(z):
    return jnp.maximum(z, 0.0) + jnp.log(1.0 + jnp.exp(jnp.minimum(z, -z)))


def _post_ln(x, y, g, lng, lnb):
    z = ALPHA * x + (1.0 + g) * y
    mu = jnp.mean(z, axis=-1, keepdims=True)
    zc = z - mu
    var = jnp.mean(zc * zc, axis=-1, keepdims=True)
    return zc * lax.rsqrt(var + LN_EPS) * lng + lnb


def _post_ln_mod(x, y, g, lng, lnb, scn, shn):
    xn = _post_ln(x, y, g, lng, lnb)
    return xn, xn * (1.0 + scn) + shn


def _pre_core(E, r_, k_, v_, wd_, ad_, gd_, r1, k1, v1, wd1, ad1, gd1, h, bg, cg, h1, cg1, h2, cg2,
              mu_r, mu_k, mu_v, mu_wd, mu_ad, mu_gd, w0, w_up, a0, a_up, g_up, k_k, k_a,
              cw0, cw1, cw2):
    def mix(x, x1, mu):
        return x + mu * (x1 - x)

    r, k, v = mix(r_, r1, mu_r), mix(k_, k1, mu_k), mix(v_, v1, mu_v)
    wd, ad, gd = mix(wd_, wd1, mu_wd), mix(ad_, ad1, mu_ad), mix(gd_, gd1, mu_gd)
    logw = -_softplus(-(w0 + jnp.dot(jnp.tanh(wd), w_up, preferred_element_type=F32))) - 0.5
    decay = jnp.exp(-jnp.exp(logw))
    iclr = jax.nn.sigmoid(a0 + jnp.dot(ad, a_up, preferred_element_type=F32))
    gate = jnp.dot(jax.nn.sigmoid(gd), g_up, preferred_element_type=F32)
    kk0 = k * k_k
    nrm = jnp.sqrt(_headsum(kk0 * kk0, E))
    kk = kk0 / jnp.maximum(nrm, 1e-12)
    kh = k * (1.0 + (iclr - 1.0) * k_a)
    yb = bg * (cw2 * (cg * h) + cw1 * (cg1 * h1) + cw0 * (cg2 * h2))
    return r, decay, kh, v, -kk, kk * iclr, gate, yb


def _post_core(E, y, r, kh, v, gate, lnx_g, lnx_b, rk):
    def seg(t):
        return _headsum(t, E)

    mean = seg(y) * (1.0 / HD)
    yc = y - mean
    var = seg(yc * yc) * (1.0 / HD)
    gn = yc * lax.rsqrt(var + GN_EPS) * lnx_g + lnx_b
    bonus = seg(r * kh * rk) * v
    return (gn + bonus) * gate


def _merge_core(o0, o1, o2, l0, l1, l2):
    m = jnp.maximum(jnp.maximum(l0, l1), l2)
    e0, e1, e2 = jnp.exp(l0 - m), jnp.exp(l1 - m), jnp.exp(l2 - m)
    den = e0 + e1 + e2
    return (e0 * o0 + e1 * o1 + e2 * o2) / den


CHUNK = 128
HALF = 64
HP = HEADS // 2
LW = 2 * HD
NCHUNK = T // CHUNK


def _split3(x):
    hi = x.astype(BF16)
    r1 = x - hi.astype(F32)
    mid = r1.astype(BF16)
    return hi, mid, (r1 - mid.astype(F32)).astype(BF16)


def _cols3(parts):
    tr = [p.reshape(T // HALF, HALF, HP, 2, HD).transpose(2, 4, 0, 3, 1) for p in parts]
    return jnp.stack(tr, axis=4).reshape(HP, HD, 6 * T)


def _pick_codes():
    row = lax.broadcasted_iota(jnp.int32, (6 * HALF, LW), 0)
    col = lax.broadcasted_iota(jnp.int32, (6 * HALF, LW), 1)
    same = (row >= 3 * HALF) == (col >= HD)
    return jnp.where(same, row & (HALF - 1), -1).astype(BF16)


def _column(block_ref, codes, half, i):
    pick = jnp.where(codes == i.astype(BF16), jnp.ones((), BF16), jnp.zeros((), BF16))
    block = block_ref[:, :, half * 6 * HALF:(half + 1) * 6 * HALF].reshape(HP * HD, 6 * HALF)
    return jnp.dot(block, pick, preferred_element_type=F32)


def _halfsums(x, row, left1):
    row_l = jnp.where(left1, row, 0.0)
    return (jnp.sum(x * row_l, axis=1, keepdims=True), jnp.sum(x * (row - row_l), axis=1, keepdims=True))


def _pair_rows(row):
    return [row[:, p * LW:(p + 1) * LW] for p in range(HP)]


def _store_columns(ref, p, t_mask, cols):
    for j, col in enumerate(cols):
        pltpu.store(ref.at[pl.ds(2 * p + j, 1)], jnp.broadcast_to(col[None], (1, HD, CHUNK)), mask=t_mask[None])


def _scan_fwd(r, w, k, a, b, v3):
    def kern(r_ref, w_ref, k_ref, a_ref, b_ref, v_ref, y_ref, ck_ref, s_ref, vb_ref):
        @pl.when(pl.program_id(0) == 0)
        def _():
            s_ref[...] = jnp.zeros_like(s_ref)

        lane = lax.broadcasted_iota(jnp.int32, (HD, CHUNK), 1)
        left = lane < HD
        left1 = lax.broadcasted_iota(jnp.int32, (1, LW), 1) < HD
        codes = _pick_codes()

        for half in range(CHUNK // HALF):
            ck_ref[half] = s_ref[...]
            vb_ref[...] = _column(v_ref, codes, half, jnp.int32(0))

            def step(i, carry):
                t = half * HALF + i
                row = lambda ref: _pair_rows(ref[pl.ds(t, 1), :])
                S = [s_ref[p] for p in range(HP)]
                sa = [jnp.where(left, *_halfsums(s, a, left1)) for s, a in zip(S, row(a_ref))]
                S = [s * w + c * b + vb_ref[pl.ds(p * HD, HD), :] * k
                     for p, (s, w, c, b, k) in enumerate(zip(S, row(w_ref), sa, row(b_ref), row(k_ref)))]
                for p, s in enumerate(S):
                    s_ref[p] = s
                for p, (s, r) in enumerate(zip(S, row(r_ref))):
                    _store_columns(y_ref, p, lane == t, _halfsums(s, r, left1))
                vb_ref[...] = _column(v_ref, codes, half, i + 1)
                return carry

            lax.fori_loop(0, HALF, step, 0, unroll=8)

    rowblk = pl.BlockSpec((CHUNK, RW), lambda c: (c, 0))
    return pl.pallas_call(
        kern, name="rwkv_scan_fwd", grid=(NCHUNK,),
        in_specs=[rowblk] * 5 + [pl.BlockSpec((HP, HD, 6 * CHUNK), lambda c: (0, 0, c))],
        out_specs=[pl.BlockSpec((HEADS, HD, CHUNK), lambda c: (0, 0, c)),
                   pl.BlockSpec((CHUNK // HALF, HP, HD, LW), lambda c: (c, 0, 0, 0))],
        out_shape=[jax.ShapeDtypeStruct((HEADS, HD, T), F32),
                   jax.ShapeDtypeStruct((T // HALF, HP, HD, LW), F32)],
        scratch_shapes=[pltpu.VMEM((HP, HD, LW), F32), pltpu.VMEM((HP * HD, LW), F32)],
        compiler_params=_cp(("arbitrary",)),
    )(r, w, k, a, b, v3)


def _scan_bwd(r, w, k, a, b, v3, dy3, ck):
    NC = T // CHUNK

    def kern(r_ref, w_ref, k_ref, a_ref, b_ref, v_ref, dy_ref, ck_ref,
             dr_ref, dw_ref, dk_ref, da_ref, db_ref, dv_ref, ds_ref, sb_ref, vb_ref, sa_ref, dyb_ref):
        @pl.when(pl.program_id(0) == 0)
        def _():
            ds_ref[...] = jnp.zeros_like(ds_ref)

        lane = lax.broadcasted_iota(jnp.int32, (HD, CHUNK), 1)
        left = lane < HD
        left1 = lax.broadcasted_iota(jnp.int32, (1, LW), 1) < HD
        codes = _pick_codes()

        def rowsum(x):
            return jnp.sum(x, axis=0, keepdims=True)

        for half in reversed(range(CHUNK // HALF)):
            base = half * HALF
            sb_ref[0] = ck_ref[half]

            vb_ref[0] = _column(v_ref, codes, half, jnp.int32(0))

            def replay(i, carry):
                t = base + i
                row = lambda ref: _pair_rows(ref[pl.ds(t, 1), :])
                S = [sb_ref[i, p] for p in range(HP)]
                sa = [jnp.where(left, *_halfsums(s, a, left1)) for s, a in zip(S, row(a_ref))]
                for p, (s, w, c, b, k) in enumerate(zip(S, row(w_ref), sa, row(b_ref), row(k_ref))):
                    sb_ref[i + 1, p] = s * w + c * b + vb_ref[i, pl.ds(p * HD, HD), :] * k
                    sa_ref[i, p] = c
                vb_ref[i + 1] = _column(v_ref, codes, half, i + 1)
                return carry

            lax.fori_loop(0, HALF, replay, 0, unroll=4)
            dyb_ref[...] = _column(dy_ref, codes, half, jnp.int32(HALF - 1))

            def back(ii, carry):
                i = HALF - 1 - ii
                t = base + i
                row = lambda ref: _pair_rows(ref[pl.ds(t, 1), :])
                a_r, b_r, k_r, w_r, r_r = row(a_ref), row(b_ref), row(k_ref), row(w_ref), row(r_ref)
                dys = [dyb_ref[pl.ds(p * HD, HD), :] for p in range(HP)]
                dyb_ref[...] = _column(dy_ref, codes, half, jnp.maximum(i - 1, 0))
                dr, dw, db, dk, da = [], [], [], [], []
                for p in range(HP):
                    Sp, dy = sb_ref[i, p], dys[p]
                    dS = ds_ref[p] + dy * r_r[p]
                    dr.append(rowsum(sb_ref[i + 1, p] * dy))
                    dw.append(rowsum(dS * Sp))
                    db.append(rowsum(dS * sa_ref[i, p]))
                    dk.append(rowsum(dS * vb_ref[i, pl.ds(p * HD, HD), :]))
                    dsa = jnp.where(left, *_halfsums(dS, b_r[p], left1))
                    _store_columns(dv_ref, p, lane == t, _halfsums(dS, k_r[p], left1))
                    da.append(rowsum(Sp * dsa))
                    ds_ref[p] = dS * w_r[p] + dsa * a_r[p]
                for ref, pieces in ((dr_ref, dr), (dw_ref, dw), (db_ref, db), (dk_ref, dk), (da_ref, da)):
                    ref[pl.ds(t, 1), :] = jnp.concatenate(pieces, axis=1)
                return carry

            lax.fori_loop(0, HALF, back, 0, unroll=4)

    rowblk = pl.BlockSpec((CHUNK, RW), lambda c: (NC - 1 - c, 0))
    col3blk = pl.BlockSpec((HP, HD, 6 * CHUNK), lambda c: (0, 0, NC - 1 - c))
    rowshape = jax.ShapeDtypeStruct((T, RW), F32)
    return pl.pallas_call(
        kern, name="rwkv_scan_bwd", grid=(NC,),
        in_specs=[rowblk] * 5 + [col3blk, col3blk,
                                 pl.BlockSpec((CHUNK // HALF, HP, HD, LW), lambda c: (NC - 1 - c, 0, 0, 0))],
        out_specs=[rowblk] * 5 + [pl.BlockSpec((HEADS, HD, CHUNK), lambda c: (0, 0, NC - 1 - c))],
        out_shape=[rowshape] * 5 + [jax.ShapeDtypeStruct((HEADS, HD, T), F32)],
        scratch_shapes=[pltpu.VMEM((HP, HD, LW), F32), pltpu.VMEM((HALF + 1, HP, HD, LW), F32),
                        pltpu.VMEM((HALF + 1, HP * HD, LW), F32), pltpu.VMEM((HALF, HP, HD, LW), F32),
                        pltpu.VMEM((HP * HD, LW), F32)],
        compiler_params=_cp(("arbitrary",)),
    )(r, w, k, a, b, v3, dy3, ck)


NT = (((1,), (1,)), ((), ()))
TN = (((0,), (0,)), ((), ()))
SCALE = HD ** -0.5
QKV_G = 3 * RW


def _attn_setup(g):
    dil = DILS[g]
    L = T // dil

    def cols(width_blocks, first):
        return lambda hp, r: (0, r * width_blocks + first + hp)

    qkv = [pl.BlockSpec((L, LW), cols(QKV // LW, g * (QKV_G // LW) + s * HP)) for s in range(3)]
    tile = pl.BlockSpec((L, LW), cols(HP, 0))
    bias = pl.BlockSpec((2, BLK, 2 * BLK), lambda hp, r: (hp, 0, 0))
    return dil, L, qkv, tile, bias


def _band(n):
    qi = lax.broadcasted_iota(jnp.int32, (BLK, 2 * BLK), 0)
    ki = lax.broadcasted_iota(jnp.int32, (BLK, 2 * BLK), 1)
    band = (ki >= qi) & (ki <= qi + BLK)
    return band if n else band[:, BLK:]


def _head_masks():
    lane = lax.broadcasted_iota(jnp.int32, (BLK, LW), 1)
    return lane < HD, [(lane < HD).astype(BF16), (lane >= HD).astype(BF16)]


def _attn_fwd(pq, bias, g):
    dil, L, qkv, tile, bias_spec = _attn_setup(g)

    def kern(q_ref, k_ref, v_ref, b_ref, o_ref, l_ref):
        left, masks = _head_masks()
        for n in range(L // BLK):
            rows = slice(n * BLK, (n + 1) * BLK)
            keys = slice((n - 1) * BLK, (n + 1) * BLK) if n else rows
            qb, kc, vc, valid = q_ref[rows, :], k_ref[keys, :], v_ref[keys, :], _band(n)
            o, lse = [], []
            for j in range(2):
                bias_j = b_ref[j] if n else b_ref[j][:, BLK:]
                s = lax.dot_general(qb * masks[j], kc, NT, preferred_element_type=F32) * SCALE + bias_j
                s = jnp.where(valid, s, -jnp.inf)
                m = jnp.max(s, axis=1, keepdims=True)
                e = jnp.exp(s - m)
                den = jnp.sum(e, axis=1, keepdims=True)
                o.append(jnp.dot((e / den).astype(BF16), vc, preferred_element_type=F32))
                lse.append(m + jnp.log(den))
            o_ref[rows, :] = jnp.where(left, o[0], o[1])
            l_ref[rows, :] = jnp.where(left, lse[0], lse[1])

    shape = jax.ShapeDtypeStruct((L, dil * RW), F32)
    o, lse = pl.pallas_call(
        kern, name=f"attn_fwd_{g}", grid=(HP, dil),
        in_specs=qkv + [bias_spec], out_specs=[tile, tile], out_shape=[shape, shape],
        compiler_params=_cp(("parallel", "parallel")),
    )(*[pq.reshape(L, dil * QKV)] * 3, bias)
    return o.reshape(T, RW), lse.reshape(T, RW)


def _attn_bwd(pq, bias, do, o, lse, dlse, g):
    dil, L, qkv, tile, bias_spec = _attn_setup(g)

    def kern(q_ref, k_ref, v_ref, b_ref, do_ref, o_ref, l_ref, dl_ref, dq_ref, dk_ref, dv_ref, db_ref):
        left, masks = _head_masks()
        lane = lax.broadcasted_iota(jnp.int32, (BLK, LW), 1)
        dk_ref[...] = jnp.zeros_like(dk_ref)
        dv_ref[...] = jnp.zeros_like(dv_ref)

        @pl.when(pl.program_id(1) == 0)
        def _():
            db_ref[...] = jnp.zeros_like(db_ref)

        def column(tile_, j):
            return jnp.sum(jnp.where(lane == j * HD, tile_, 0.0), axis=1, keepdims=True)

        for n in range(L // BLK):
            rows = slice(n * BLK, (n + 1) * BLK)
            keys = slice((n - 1) * BLK, (n + 1) * BLK) if n else rows
            qb, kc, vc, dob, valid = q_ref[rows, :], k_ref[keys, :], v_ref[keys, :], do_ref[rows, :], _band(n)
            prod = dob.astype(F32) * o_ref[rows, :]
            dq = []
            for j in range(2):
                bias_j = b_ref[j] if n else b_ref[j][:, BLK:]
                delta = jnp.sum(prod * masks[j].astype(F32), axis=1, keepdims=True)
                qm, dom = qb * masks[j], dob * masks[j]
                s = lax.dot_general(qm, kc, NT, preferred_element_type=F32) * SCALE + bias_j
                p = jnp.where(valid, jnp.exp(s - column(l_ref[rows, :], j)), 0.0)
                dp = lax.dot_general(dom, vc, NT, preferred_element_type=F32)
                ds = p * (dp + (column(dl_ref[rows, :], j) - delta))
                if n:
                    db_ref[j] += ds
                else:
                    db_ref[j, :, BLK:] += ds
                dsb = (ds * SCALE).astype(BF16)
                dq.append(jnp.dot(dsb, kc, preferred_element_type=F32))
                dk_ref[keys, :] += lax.dot_general(dsb, qm, TN, preferred_element_type=F32)
                dv_ref[keys, :] += lax.dot_general(p.astype(BF16), dom, TN, preferred_element_type=F32)
            dq_ref[rows, :] = jnp.where(left, dq[0], dq[1])

    shape = jax.ShapeDtypeStruct((L, dil * RW), F32)
    view = lambda t: t.reshape(L, dil * RW)
    dq, dk, dv, db = pl.pallas_call(
        kern, name=f"attn_bwd_{g}", grid=(HP, dil),
        in_specs=qkv + [bias_spec] + [tile] * 4, out_specs=[tile] * 3 + [bias_spec],
        out_shape=[shape] * 3 + [jax.ShapeDtypeStruct((HEADS, BLK, 2 * BLK), F32)],
        compiler_params=_cp(("parallel", "arbitrary")),
    )(*[pq.reshape(L, dil * QKV)] * 3, bias, view(do), view(o), view(lse), view(dlse))
    return dq.reshape(T, RW), dk.reshape(T, RW), dv.reshape(T, RW), db


NBUCKET = 32
NPAIR = BLK * 2 * BLK


def _relbias_table(rbT, onehotT):
    def kern(rb_ref, oh_ref, out_ref):
        out_ref[0] = sum(jnp.dot(p, oh_ref[0], preferred_element_type=F32) for p in _split3(rb_ref[0]))

    return pl.pallas_call(
        kern, name="relbias_table", grid=(3,),
        in_specs=[pl.BlockSpec((1, HEADS, NBUCKET), lambda g: (g, 0, 0)),
                  pl.BlockSpec((1, NBUCKET, NPAIR), lambda g: (g, 0, 0))],
        out_specs=pl.BlockSpec((1, HEADS, NPAIR), lambda g: (g, 0, 0)),
        out_shape=jax.ShapeDtypeStruct((3, HEADS, NPAIR), F32),
        compiler_params=_cp(("parallel",)),
    )(rbT, onehotT)


def _relbias_grad(db, onehotT):
    nt = (((1,), (1,)), ((), ()))

    def kern(db_ref, oh_ref, out_ref):
        hi, mid, _ = _split3(db_ref[0])
        out_ref[0] = (lax.dot_general(hi, oh_ref[0], nt, preferred_element_type=F32)
                      + lax.dot_general(mid, oh_ref[0], nt, preferred_element_type=F32))

    return pl.pallas_call(
        kern, name="relbias_grad", grid=(3,),
        in_specs=[pl.BlockSpec((1, HEADS, NPAIR), lambda g: (g, 0, 0)),
                  pl.BlockSpec((1, NBUCKET, NPAIR), lambda g: (g, 0, 0))],
        out_specs=pl.BlockSpec((1, HEADS, NBUCKET), lambda g: (g, 0, 0)),
        out_shape=jax.ShapeDtypeStruct((3, HEADS, NBUCKET), F32),
        compiler_params=_cp(("parallel",)),
    )(db, onehotT)


def _adamw(w, g, m, v):
    m2 = ADAM_B1 * m + (1.0 - ADAM_B1) * g
    v2 = ADAM_B2 * v + (1.0 - ADAM_B2) * (g * g)
    m_hat = m2 / (1.0 - ADAM_B1 ** ADAM_STEP)
    v_hat = v2 / (1.0 - ADAM_B2 ** ADAM_STEP)
    return -ADAM_LR * (m_hat / (jnp.sqrt(v_hat) + ADAM_EPS) + ADAM_WD * w), m2, v2


def _ada_mod(c_all, ada_w, ada_b_loc):
    def kern(c_ref, w_ref, b_ref, o_ref):
        c = c_ref[...]
        cond = c * jax.nn.sigmoid(c)
        o_ref[0] = jnp.dot(cond, w_ref[0], precision=HI, preferred_element_type=F32) + b_ref[0]

    ncol = ada_w.shape[2]
    return pl.pallas_call(
        kern, name="ada_mod", grid=(2,),
        in_specs=[pl.BlockSpec((NDEV, D), lambda i: (0, 0)),
                  pl.BlockSpec((1, D, ncol), lambda i: (i, 0, 0)),
                  pl.BlockSpec((1, 1, ncol), lambda i: (i, 0, 0))],
        out_specs=pl.BlockSpec((1, NDEV, ncol), lambda i: (i, 0, 0)),
        out_shape=jax.ShapeDtypeStruct((2, NDEV, ncol), F32),
        compiler_params=_cp(("parallel",)),
    )(c_all, ada_w, ada_b_loc.reshape(2, 1, ncol))


def _ada_grad_adamw(cT_all, dmod_loc, w, m, v):
    ncol = w.shape[2]
    tr = 256

    def kern(c_ref, d_ref, w_ref, m_ref, v_ref, g_ref, dl_ref, m2_ref, v2_ref):
        c = c_ref[...]
        cond = c * jax.nn.sigmoid(c)
        g = jnp.dot(cond, d_ref[0], precision=HI, preferred_element_type=F32)
        dl, m2, v2 = _adamw(w_ref[0], g, m_ref[0], v_ref[0])
        g_ref[0], dl_ref[0], m2_ref[0], v2_ref[0] = g, dl, m2, v2

    big = pl.BlockSpec((1, tr, ncol), lambda i, j: (i, j, 0))
    shp = jax.ShapeDtypeStruct(w.shape, F32)
    return pl.pallas_call(
        kern, name="ada_grad_adamw", grid=(2, D // tr),
        in_specs=[pl.BlockSpec((tr, NDEV), lambda i, j: (j, 0)),
                  pl.BlockSpec((1, NDEV, ncol), lambda i, j: (i, 0, 0)), big, big, big],
        out_specs=[big] * 4, out_shape=[shp] * 4,
        compiler_params=_cp(("parallel", "parallel")),
    )(cT_all, dmod_loc, w, m, v)


def _sum_adamw(recv, w, m, v, name, tr):
    S = recv.shape[0]
    R, C = w.shape
    assert R % tr == 0 and recv.shape[1:] == (R, C)

    def kern(r_ref, w_ref, m_ref, v_ref, g_ref, dl_ref, m2_ref, v2_ref):
        g = r_ref[0].astype(F32)
        for s in range(1, S):
            g = g + r_ref[s].astype(F32)
        dl, m2, v2 = _adamw(w_ref[...], g, m_ref[...], v_ref[...])
        g_ref[...], dl_ref[...], m2_ref[...], v2_ref[...] = g, dl, m2, v2

    flat = pl.BlockSpec((tr, C), lambda i: (i, 0))
    shp = jax.ShapeDtypeStruct((R, C), F32)
    return pl.pallas_call(
        kern, name=name, grid=(R // tr,),
        in_specs=[pl.BlockSpec((S, tr, C), lambda i: (0, i, 0)), flat, flat, flat],
        out_specs=[flat] * 4, out_shape=[shp] * 4,
        compiler_params=_cp(("parallel",)),
    )(recv, w, m, v)


def _pack(arrs, dtype, row_mult):
    flat = jnp.concatenate([a.reshape(-1).astype(dtype) for a in arrs])
    flat = jnp.pad(flat, (0, -flat.shape[0] % (128 * row_mult)))
    return flat.reshape(-1, 128)


def _pack8(arrs, dtype, row_mult):
    flat = jnp.concatenate([a.reshape(NDEV, -1).astype(dtype) for a in arrs], axis=1)
    flat = jnp.pad(flat, ((0, 0), (0, -flat.shape[1] % (128 * row_mult))))
    return flat.reshape(NDEV, -1, 128)


def _unpack(buf, shapes, lead=()):
    flat = buf.reshape(lead + (-1,))
    out, off = [], 0
    for s in shapes:
        n = math.prod(s)
        out.append(flat[..., off:off + n].reshape(lead + tuple(s)))
        off += n
    return out


def _to_chunks(full, kind):
    if kind == "col":
        x = full.reshape(full.shape[:-1] + (NDEV, full.shape[-1] // NDEV))
        return jnp.moveaxis(x, -2, 0)
    x = full.reshape(full.shape[:-2] + (NDEV, full.shape[-2] // NDEV, full.shape[-1]))
    return jnp.moveaxis(x, -3, 0)


def _from_chunks(g8, kind):
    if kind == "col":
        x = jnp.moveaxis(g8, 0, -2)
        return x.reshape(x.shape[:-2] + (x.shape[-2] * x.shape[-1],))
    x = jnp.moveaxis(g8, 0, -3)
    return x.reshape(x.shape[:-3] + (x.shape[-3] * x.shape[-2], x.shape[-1]))


def _pad_pa(x):
    z = lambda n: jnp.zeros(x.shape[:-1] + (n,), x.dtype)
    return jnp.concatenate([x[..., :1600], z(64), x[..., 1600:1664], z(64), x[..., 1664:1824], z(96)], -1)


def _unpad_pa(x):
    return jnp.concatenate([x[..., :1600], x[..., 1664:1728], x[..., 1792:1952]], -1)


def _pad_rows(x, n):
    return jnp.pad(x, ((0, n - x.shape[0]), (0, 0)))


def _shift_down(x, n):
    return jnp.pad(x, ((n, 0), (0, 0)))[:-n]


def _shift_up(x, n):
    return jnp.pad(x, ((0, n), (0, 0)))[n:]


def _unheadsT(x):
    return x.transpose(2, 0, 1).reshape(T, RW)


def _bucket_tables():
    qi = jnp.arange(BLK)[:, None]
    ki = jnp.arange(2 * BLK)[None, :]
    rel = BLK + qi - ki
    tabs = []
    for dil in DILS:
        dist = jnp.clip(rel, 0, BLK) * dil
        logd = jnp.log(jnp.maximum(dist, 1).astype(F32) / 16) / math.log(2048 / 16)
        large = jnp.minimum(16 + (logd * 16).astype(jnp.int32), 31)
        tabs.append(jnp.where(dist < 16, dist, large))
    return jnp.stack(tabs)


SHARDED = (("ln_g", "col"), ("ln_b", "col"), ("ab_w_in", "col"), ("rw_w_up", "col"), ("rw_a_up", "col"),
           ("rw_g_up", "col"), ("sc_conv_w", "col"), ("ab_w_out", "row"), ("dil_w_qkv", "col"),
           ("dil_w_out", "col"), ("mlp_w1", "col"), ("mlp_w2", "row"))
FIRST = ("ab_w_in", "ab_w_out")
LATER = ("dil_w_qkv", "dil_w_out", "mlp_w1", "mlp_w2")
GATHER_BF16 = FIRST + LATER
GATHER_F32 = ("rw_w_up", "rw_a_up", "rw_g_up", "sc_conv_w", "ln_g", "ln_b")
REPLICATED = ("ada_b", "rw_mu", "rw_w0", "rw_a0", "rw_k_k", "rw_k_a", "rw_r_k", "rw_lnx_g", "rw_lnx_b", "rel_bias")
WEIGHTS = ("ada_w", "ada_b", "ln_g", "ln_b", "ab_w_in", "rw_mu", "rw_w0", "rw_w_up", "rw_a0", "rw_a_up",
           "rw_g_up", "rw_k_k", "rw_k_a", "rw_r_k", "rw_lnx_g", "rw_lnx_b", "sc_conv_w", "ab_w_out",
           "dil_w_qkv", "dil_w_out", "rel_bias", "mlp_w1", "mlp_w2")
FLAT_TILE = 512


def _local_step(x0, tgt, mod, W, P, later_weights, early_grads):
    row = lambda a: a.reshape(1, -1)
    W = dict(W)
    m6 = mod.reshape(2, 6, 1, D)
    sc = [m6[0, 1], m6[0, 4], m6[1, 1], m6[1, 4]]
    sh = [m6[0, 0], m6[0, 3], m6[1, 0], m6[1, 3]]
    gt = [m6[0, 2], m6[0, 5], m6[1, 2], m6[1, 5]]
    lng = [row(P["ln_g"][0, 0]), row(P["ln_g"][0, 1]), row(P["ln_g"][1, 0]), row(P["ln_g"][1, 1])]
    lnb = [row(P["ln_b"][0, 0]), row(P["ln_b"][0, 1]), row(P["ln_b"][1, 0]), row(P["ln_b"][1, 1])]
    E = jnp.kron(jnp.eye(HEADS, dtype=BF16), jnp.ones((HD, HD), BF16))

    def mod_body(r, p):
        return [r[0] * (1.0 + p[0]) + p[1]], []

    (u0,), _ = _rows("modulate", mod_body, [x0], [sc[0], sh[0]], [(D, BF16)])

    def post_fwd_body(r, p):
        xn, un = _post_ln_mod(r[0], r[1], *p)
        return [xn, un], []

    def post_fwd(s, x, y):
        (xn, un), _ = _rows(f"post_ln_{s}", post_fwd_body, [x, y],
                            [gt[s], lng[s], lnb[s], sc[s + 1], sh[s + 1]], [(D, F32), (D, BF16)])
        return xn, un

    def relu2(acc):
        a = jnp.maximum(acc, 0.0)
        return acc, a * a

    def relu2_bwd(acc, h):
        return (acc * (2.0 * jnp.maximum(h, 0.0)),)

    p = _mm("ab_in", u0, W["ab_w_in"])
    p1 = _shift_down(p, 1)
    p2 = _shift_down(p[:, PA:], 2)
    mu = _pad_pa(P["rw_mu"])
    mu_parts = [mu[:, :512], mu[:, 512:1024], mu[:, 1024:1536], mu[:, 1536:1664], mu[:, 1664:1792], mu[:, 1792:]]
    pre_params = mu_parts + [P["rw_w0"], _pad_rows(P["rw_w_up"], 128), P["rw_a0"], _pad_rows(P["rw_a_up"], 128),
                             _pad_rows(P["rw_g_up"], 256), P["rw_k_k"], P["rw_k_a"],
                             P["sc_conv_w"][0:1], P["sc_conv_w"][1:2], P["sc_conv_w"][2:3]]
    pre_rows = [(p, 512, 0), (p, 512, 1), (p, 512, 2), (p, 128, 12), (p, 128, 13), (p, 256, 7),
                (p1, 512, 0), (p1, 512, 1), (p1, 512, 2), (p1, 128, 12), (p1, 128, 13), (p1, 256, 7),
                (p, 512, 4), (p, 512, 5), (p, 512, 6), (p1, 512, 4), (p1, 512, 6), (p2, 512, 0), (p2, 512, 2)]
    NPR = len(pre_rows)

    def pre_fwd_body(r, pp):
        outs = list(_pre_core(pp[0], *r, *pp[1:]))
        return outs + list(_split3(outs[3])), []

    (r_, w_, kh_, v_, a_, b_, gate_, yb, *v_parts), _ = _rows(
        "rwkv_pre", pre_fwd_body, pre_rows, [E] + pre_params,
        [(RW, F32)] * 7 + [(RW, BF16)] * 4, tm=256)
    scan_in = [r_, w_, kh_, a_, b_, _cols3(v_parts)]
    yT, ck = _scan_fwd(*scan_in)
    ysc = _unheadsT(yT)
    post_params = [P["rw_lnx_g"], P["rw_lnx_b"], P["rw_r_k"].reshape(1, RW)]

    def postmix_fwd_body(r, pp):
        return [_post_core(pp[0], *r, *pp[1:])], []

    (ya,), _ = _rows("rwkv_post", postmix_fwd_body, [ysc, r_, kh_, v_, gate_], [E] + post_params,
                     [(RW, BF16)], tm=256)
    cat = jnp.concatenate([ya, yb], axis=1)
    y0 = _mm("ab_out", cat, W["ab_w_out"])
    x1, u1 = post_fwd(0, x0, y0)
    W.update(later_weights(u1))

    h1, a1 = _mm("mlp1_up_0", u1, W["mlp_w1"][0], out=(F32, BF16), epi=relu2)
    y1 = _mm("mlp1_down_0", a1, W["mlp_w2"][0])
    x2, u2 = post_fwd(1, x1, y1)

    pq = _mm("qkv", u2, W["dil_w_qkv"], out=(BF16,))
    onehotT = (_bucket_tables().reshape(3, 1, NPAIR) == jnp.arange(NBUCKET).reshape(1, NBUCKET, 1)).astype(BF16)
    rbT = P["rel_bias"].reshape(NBUCKET, 3, HEADS).transpose(1, 2, 0)
    bias = _relbias_table(rbT, onehotT).reshape(3, HEADS, BLK, 2 * BLK)
    og, lse = zip(*[_attn_fwd(pq, bias[g], g) for g in range(3)])

    def merge_fwd_body(r, pp):
        return [_merge_core(*r)], []

    (om,), _ = _rows("attn_merge", merge_fwd_body, list(og + lse), [], [(RW, BF16)])
    y2 = _mm("dil_out", om, W["dil_w_out"])
    x3, u3 = post_fwd(2, x2, y2)

    h3, a3 = _mm("mlp1_up_1", u3, W["mlp_w1"][1], out=(F32, BF16), epi=relu2)
    y3 = _mm("mlp1_down_1", a3, W["mlp_w2"][1])

    def last_body(r, pp):
        x, y, tg = r
        xn, vjp = jax.vjp(_post_ln, x, y, *pp)
        err = xn - tg
        dx, dy, dg, dlg, dlb = vjp(err * (1.0 / D))
        loss = jnp.full((1, 128), (0.5 / D) * jnp.sum(err * err), F32)
        return [dx, dy], [loss, dg, dlg, dlb]

    (dxp, dy3), (loss_acc, dg3, dlng3, dlnb3) = _rows(
        "final_ln_loss", last_body, [x3, y3, tgt], [gt[3], lng[3], lnb[3]],
        [(D, F32), (D, BF16)], [(1, 128), (1, D), (1, D), (1, D)])

    G = {}
    dsc, dsh, dgt = [None] * 4, [None] * 4, [None] * 4
    dlng, dlnb = [None] * 4, [None] * 4
    dgt[3], dlng[3], dlnb[3] = dg3, dlng3, dlnb3

    def mlp_bwd(i, u, h, a, dy):
        dh = _mm(f"mlp_dh_{i}", dy, W["mlp_w2"][i], tb=True, out=(BF16,), epi=relu2_bwd, extras=(h,))
        gw2 = _mm(f"mlp_dw2_{i}", a.T, dy)
        du = _mm(f"mlp_du_{i}", dh, W["mlp_w1"][i], tb=True)
        gw1 = _mm(f"mlp_dw1_{i}", u.T, dh)
        return du, gw1, gw2

    def post_bwd_body(r, pp):
        x, y, dxn, dun = r
        _, vjp = jax.vjp(_post_ln_mod, x, y, *pp)
        dx, dy, dg, dlg, dlb, dscn, dshn = vjp((dxn, dun))
        return [dx, dy], [dg, dlg, dlb, dscn, dshn]

    def post_bwd(s, x, y, dxn, dun):
        (dx, dy), (dgt[s], dlng[s], dlnb[s], dsc[s + 1], dsh[s + 1]) = _rows(
            f"post_ln_bwd_{s}", post_bwd_body, [x, y, dxn, dun],
            [gt[s], lng[s], lnb[s], sc[s + 1], sh[s + 1]], [(D, F32), (D, BF16)], [(1, D)] * 5)
        return dx, dy

    du3, gw1_1, gw2_1 = mlp_bwd(1, u3, h3, a3, dy3)
    dxp, dy2 = post_bwd(2, x2, y2, dxp, du3)

    G["dil_w_out"] = _mm("dil_out_dw", om.T, dy2)[None]
    do = _mm("dil_out_dx", dy2, W["dil_w_out"], tb=True)

    def merge_bwd_body(r, pp):
        _, vjp = jax.vjp(_merge_core, *r[:6])
        d = vjp(r[6])
        return list(d[:3]) + [_headsum(d[3 + g], pp[0]) for g in range(3)], []

    mb, _ = _rows("attn_merge_bwd", merge_bwd_body, list(og + lse) + [do], [E],
                  [(RW, BF16)] * 3 + [(RW, F32)] * 3)
    back = [_attn_bwd(pq, bias[g], mb[g], og[g], lse[g], mb[3 + g], g) for g in range(3)]
    dpq = jnp.concatenate([t for dq, dk, dv, _ in back for t in (dq, dk, dv)], axis=1).astype(BF16)
    rb = _relbias_grad(jnp.stack([b[3] for b in back]).reshape(3, HEADS, NPAIR), onehotT)
    G["rel_bias"] = rb.transpose(2, 0, 1).reshape(NBUCKET, 3 * HEADS)
    G["dil_w_qkv"] = _mm("qkv_dw", u2.T, dpq)[None]
    du2 = _mm("qkv_dx", dpq, W["dil_w_qkv"], tb=True)
    dxp, dy1 = post_bwd(1, x1, y1, dxp, du2)

    du1, gw1_0, gw2_0 = mlp_bwd(0, u1, h1, a1, dy1)
    G["mlp_w1"] = jnp.stack([gw1_0, gw1_1])
    G["mlp_w2"] = jnp.stack([gw2_0, gw2_1])
    gt[0] = gt[0] + early_grads(G)
    dxp, dy0 = post_bwd(0, x0, y0, dxp, du1)

    G["ab_w_out"] = _mm("ab_out_dw", cat.T, dy0)[None]
    dcat = _mm("ab_out_dx", dy0, W["ab_w_out"], tb=True)

    def postmix_bwd_body(r, pp):
        _, vjp = jax.vjp(functools.partial(_post_core, pp[0]), *r[:5], *pp[1:])
        d = vjp(r[5])
        return list(_split3(d[0])) + list(d[1:5]), list(d[5:])

    (*dy_parts, dr1, dkh1, dv1, dgate), (G["rw_lnx_g"], G["rw_lnx_b"], drk) = _rows(
        "rwkv_post_bwd", postmix_bwd_body, [ysc, r_, kh_, v_, gate_, (dcat, 512, 0)], [E] + post_params,
        [(RW, BF16)] * 3 + [(RW, F32)] * 4, [(1, RW)] * 3, tm=256)
    G["rw_r_k"] = drk.reshape(1, HEADS, HD)
    dr2, dw2, dk2, da2, db2, dvT = _scan_bwd(*scan_in, _cols3(dy_parts), ck)
    dv2 = _unheadsT(dvT)

    def pre_bwd_body(r, pp):
        prim, ct = r[:NPR], r[NPR:]
        _, vjp = jax.vjp(functools.partial(_pre_core, pp[0]), *prim, *pp[1:])
        cts = (ct[0] + ct[1], ct[2], ct[3] + ct[4], ct[5] + ct[6], ct[7], ct[8], ct[9], ct[10])
        d = vjp(cts)
        z = jnp.zeros_like(d[12])
        dp = jnp.concatenate([d[0], d[1], d[2], d[3], d[4], d[5], d[12], d[13], d[14]], axis=1)
        dp1 = jnp.concatenate([d[6], d[7], d[8], d[9], d[10], d[11], d[15], z, d[16]], axis=1)
        dp2 = jnp.concatenate([d[17], z, d[18]], axis=1)
        return [dp, dp1, dp2], list(d[NPR:])

    acc_shapes = [a.shape for a in pre_params]
    (dp, dp1, dp2), pacc = _rows(
        "rwkv_pre_bwd", pre_bwd_body,
        pre_rows + [dr1, dr2, dw2, dkh1, dk2, dv1, dv2, da2, db2, dgate, (dcat, 512, 1)],
        [E] + pre_params, [(PAB, F32), (PAB, F32), (PB, F32)], acc_shapes, tm=256)
    G["rw_mu"] = _unpad_pa(jnp.concatenate(pacc[:6], axis=1))
    G["rw_w0"], G["rw_a0"], G["rw_k_k"], G["rw_k_a"] = pacc[6], pacc[8], pacc[11], pacc[12]
    G["rw_w_up"] = pacc[7][None, :64]
    G["rw_a_up"] = pacc[9][None, :64]
    G["rw_g_up"] = pacc[10][None, :160]
    G["sc_conv_w"] = jnp.concatenate(pacc[13:16], axis=0)[None]

    def add3_body(r, pp):
        return [r[0] + r[1] + r[2]], []

    (dpt,), _ = _rows("shift_merge", add3_body,
                      [dp, _shift_up(dp1, 1), jnp.pad(_shift_up(dp2, 2), ((0, 0), (PA, 0)))], [], [(PAB, BF16)])
    gin = _mm("ab_in_dw", u0.T, dpt)
    G["ab_w_in"] = jnp.concatenate([_unpad_pa(gin[:, :PA]), gin[:, PA:]], axis=1)[None]
    du0 = _mm("ab_in_dx", dpt, W["ab_w_in"], tb=True)

    def mod_bwd_body(r, pp):
        du, dx, x = r
        return [dx + du * (1.0 + pp[0])], [jnp.sum(du * x, axis=0, keepdims=True), jnp.sum(du, axis=0, keepdims=True)]

    (grad_x,), (dsc[0], dsh[0]) = _rows("modulate_bwd", mod_bwd_body, [du0, dxp, x0], [sc[0]], [(D, F32)],
                                        [(1, D), (1, D)])

    G["ln_g"] = jnp.concatenate(dlng, axis=0).reshape(2, 2, D)
    G["ln_b"] = jnp.concatenate(dlnb, axis=0).reshape(2, 2, D)
    dmod = jnp.concatenate([dsh[0], dsc[0], dgt[0], dsh[1], dsc[1], dgt[1],
                            dsh[2], dsc[2], dgt[2], dsh[3], dsc[3], dgt[3]], axis=1).reshape(2, 6 * D)
    return loss_acc[0, 0], grad_x, dmod, G


def kernel(x, c, ada_w, ada_b, ln_g, ln_b, ab_w_in, rw_mu, rw_w0, rw_w_up, rw_a0, rw_a_up, rw_g_up, rw_k_k, rw_k_a, rw_r_k, rw_lnx_g, rw_lnx_b, sc_conv_w, ab_w_out, dil_w_qkv, dil_w_out, rel_bias, mlp_w1, mlp_w2, loss_target, m_ada_w, m_ada_b, m_ln_g, m_ln_b, m_ab_w_in, m_rw_mu, m_rw_w0, m_rw_w_up, m_rw_a0, m_rw_a_up, m_rw_g_up, m_rw_k_k, m_rw_k_a, m_rw_r_k, m_rw_lnx_g, m_rw_lnx_b, m_sc_conv_w, m_ab_w_out, m_dil_w_qkv, m_dil_w_out, m_rel_bias, m_mlp_w1, m_mlp_w2, v_ada_w, v_ada_b, v_ln_g, v_ln_b, v_ab_w_in, v_rw_mu, v_rw_w0, v_rw_w_up, v_rw_a0, v_rw_a_up, v_rw_g_up, v_rw_k_k, v_rw_k_a, v_rw_r_k, v_rw_lnx_g, v_rw_lnx_b, v_sc_conv_w, v_ab_w_out, v_dil_w_qkv, v_dil_w_out, v_rel_bias, v_mlp_w1, v_mlp_w2):
    w = dict(ada_w=ada_w, ada_b=ada_b, ln_g=ln_g, ln_b=ln_b, ab_w_in=ab_w_in, rw_mu=rw_mu, rw_w0=rw_w0,
             rw_w_up=rw_w_up, rw_a0=rw_a0, rw_a_up=rw_a_up, rw_g_up=rw_g_up, rw_k_k=rw_k_k, rw_k_a=rw_k_a,
             rw_r_k=rw_r_k, rw_lnx_g=rw_lnx_g, rw_lnx_b=rw_lnx_b, sc_conv_w=sc_conv_w, ab_w_out=ab_w_out,
             dil_w_qkv=dil_w_qkv, dil_w_out=dil_w_out, rel_bias=rel_bias, mlp_w1=mlp_w1, mlp_w2=mlp_w2)
    m = dict(ada_w=m_ada_w, ada_b=m_ada_b, ln_g=m_ln_g, ln_b=m_ln_b, ab_w_in=m_ab_w_in, rw_mu=m_rw_mu,
             rw_w0=m_rw_w0, rw_w_up=m_rw_w_up, rw_a0=m_rw_a0, rw_a_up=m_rw_a_up, rw_g_up=m_rw_g_up,
             rw_k_k=m_rw_k_k, rw_k_a=m_rw_k_a, rw_r_k=m_rw_r_k, rw_lnx_g=m_rw_lnx_g, rw_lnx_b=m_rw_lnx_b,
             sc_conv_w=m_sc_conv_w, ab_w_out=m_ab_w_out, dil_w_qkv=m_dil_w_qkv, dil_w_out=m_dil_w_out,
             rel_bias=m_rel_bias, mlp_w1=m_mlp_w1, mlp_w2=m_mlp_w2)
    v = dict(ada_w=v_ada_w, ada_b=v_ada_b, ln_g=v_ln_g, ln_b=v_ln_b, ab_w_in=v_ab_w_in, rw_mu=v_rw_mu,
             rw_w0=v_rw_w0, rw_w_up=v_rw_w_up, rw_a0=v_rw_a0, rw_a_up=v_rw_a_up, rw_g_up=v_rw_g_up,
             rw_k_k=v_rw_k_k, rw_k_a=v_rw_k_a, rw_r_k=v_rw_r_k, rw_lnx_g=v_rw_lnx_g, rw_lnx_b=v_rw_lnx_b,
             sc_conv_w=v_sc_conv_w, ab_w_out=v_ab_w_out, dil_w_qkv=v_dil_w_qkv, dil_w_out=v_dil_w_out,
             rel_bias=v_rel_bias, mlp_w1=v_mlp_w1, mlp_w2=v_mlp_w2)
    kinds = dict(SHARDED)
    me = 4 * lax.axis_index("x") + 2 * lax.axis_index("y") + lax.axis_index("c")
    ncol = ada_w.shape[2]

    small = _all_gather(_pack([c] + [w[n] for n in GATHER_F32], F32, 8), "gather_small")
    parts = _unpack(small, [c.shape] + [w[n].shape for n in GATHER_F32], (NDEV,))
    c_all = parts[0].reshape(NDEV, D)
    P = {n: _from_chunks(t, kinds[n]) for n, t in zip(GATHER_F32, parts[1:])}
    P = {n: (t if n in ("ln_g", "ln_b") else t[0]) for n, t in P.items()}
    for n in REPLICATED[1:]:
        P[n] = w[n]
    def full(n, t):
        t = _from_chunks(t, kinds[n])
        return t if n in ("mlp_w1", "mlp_w2") else t[0]

    parts = _all_gather_many([w[n].astype(BF16) for n in FIRST], "gather_first_weights")
    W = {n: full(n, t) for n, t in zip(FIRST, parts)}
    W["ab_w_in"] = jnp.concatenate([_pad_pa(W["ab_w_in"][:, :1824]), W["ab_w_in"][:, 1824:]], axis=1)

    ada_b_loc = lax.dynamic_slice(ada_b, (0, ncol * me), (2, ncol))
    mod_part = _ada_mod(c_all, ada_w, ada_b_loc)
    mod_all = _all_gather(mod_part.reshape(-1, 128), "gather_mod").reshape(NDEV, 2, NDEV, ncol)
    mod = lax.dynamic_index_in_dim(mod_all, me, axis=2, keepdims=False)
    mod = mod.transpose(1, 0, 2).reshape(2, 6 * D)

    behind = (mod[0, 0] * 0.0).astype(BF16)
    later = _exchange_start([w[n].astype(BF16) + (behind if n == LATER[0] else 0) for n in LATER], True,
                            "gather_later_weights_start")
    mod = mod + later[-1][0, 0]

    def later_weights(after):
        lands = _exchange_wait(later, True, after, "gather_later_weights_wait")
        return {n: full(n, t) for n, t in zip(LATER, lands)}

    sent = []

    def early_grads(G):
        sent.append(_exchange_start([_to_chunks(G[n], kinds[n]).astype(BF16) for n in LATER], False,
                                    "exchange_later_grads_start"))
        return sent[0][-1][0, 0]

    loss_part, grad_x, dmod, G = _local_step(x[0], loss_target[0], mod, W, P, later_weights, early_grads)
    G["ada_b"] = dmod
    loss = lax.psum(loss_part, ("x", "y", "c"))

    rep_shapes = [w[n].shape for n in REPLICATED]
    rep_all = _all_gather(_pack([G[n] for n in REPLICATED], F32, 8), "gather_replicated_grads")
    pk = lambda d: _pack([d[n] for n in REPLICATED], F32, 8)
    rep_out = _sum_adamw(rep_all, pk(w), pk(m), pk(v), "sum_adamw_replicated", rep_all.shape[1])
    rep_out = [dict(zip(REPLICATED, _unpack(o, rep_shapes))) for o in rep_out]

    dmod_all = _unpack(rep_all, [(2, 6 * D)], (NDEV,))[0]
    dmod_loc = lax.dynamic_slice(dmod_all, (0, 0, ncol * me), (NDEV, 2, ncol)).transpose(1, 0, 2)
    ada_out = _ada_grad_adamw(c_all.T, dmod_loc, ada_w, m_ada_w, v_ada_w)

    names = [n for n, _ in SHARDED if n not in GATHER_BF16]
    shard_shapes = [w[n].shape for n in names]
    chunks = _pack8([_to_chunks(G[n], kinds[n]) for n in names], F32, 8)
    recv = _all_to_all(chunks, "exchange_small_grads")
    pk = lambda d: _pack([d[n] for n in names], F32, 8)
    sh_out = _sum_adamw(recv, pk(w), pk(m), pk(v), "sum_adamw_small", recv.shape[1])
    sh_out = [dict(zip(names, _unpack(o, shard_shapes))) for o in sh_out]

    big_out = {}

    def update(n, contributions):
        cols = w[n].shape[-1]
        flat = lambda t: t.reshape(-1, cols)
        rows = flat(w[n]).shape[0]
        outs = _sum_adamw(contributions.reshape(-1, rows, cols), flat(w[n]), flat(m[n]), flat(v[n]),
                          f"sum_adamw_{n}", min(rows, 256))
        big_out[n] = [o.reshape(w[n].shape) for o in outs]

    ci = lax.axis_index("c")
    mine_l, sib_l = [], []
    for n in FIRST:
        g8 = _to_chunks(G[n], kinds[n])
        g42 = g8.reshape((4, 2) + g8.shape[1:])
        mine_l.append(lax.dynamic_index_in_dim(g42, ci, 1, keepdims=False))
        sib_l.append(lax.dynamic_index_in_dim(g42, 1 - ci, 1, keepdims=False))
    from_sib = _swap_sibling(sib_l, "swap_sibling_grads")

    def add2_body(r, pp):
        return [r[0] + r[1]], []

    partials = []
    for n, a, b in zip(FIRST, mine_l, from_sib):
        cols = a.shape[-1]
        (p,), _ = _rows(f"pair_sum_{n}", add2_body, [a.reshape(-1, cols), b.reshape(-1, cols)], [],
                        [(cols, BF16)], tm=512)
        partials.append(p.reshape(a.shape))
    for n, r in zip(FIRST, _exchange_chips(partials, "exchange_chip_grads")):
        update(n, r)

    for n, r in zip(LATER, _exchange_wait(sent[0], False, partials[0], "exchange_later_grads_wait")):
        update(n, r)
    sh_out = [{**d, **{n: big_out[n][i] for n in GATHER_BF16}} for i, d in enumerate(sh_out)]

    def pick(i, n):
        if n == "ada_w":
            return ada_out[i]
        return rep_out[i][n] if n in REPLICATED else sh_out[i][n]

    outs = [loss, grad_x[None]]
    for i in range(4):
        outs += [pick(i, n) for n in WEIGHTS]
    return tuple(outs)
```

```python
import functools
import math

import jax
import jax.numpy as jnp
from jax import lax
from jax.experimental import pallas as pl
from jax.experimental.pallas import tpu as pltpu

F32 = jnp.float32
BF16 = jnp.bfloat16
HI = lax.Precision.HIGHEST

NDEV = 8
T = 2048
D = 1024
DFF = 4096
HEADS = 8
HD = 64
RW = 512
PA = 2048
PB = 1536
PAB = PA + PB
QKV = 4608
DILS = (1, 4, 16)
BLK = 128
ALPHA = 4.0 ** 0.25
LN_EPS = 1e-5
GN_EPS = 64e-5
ADAM_LR, ADAM_B1, ADAM_B2, ADAM_EPS, ADAM_WD, ADAM_STEP = 0.001, 0.9, 0.999, 1e-8, 0.01, 10
VMEM_LIMIT = 56 * 1024 * 1024


def _cp(sem):
    return pltpu.CompilerParams(dimension_semantics=sem, vmem_limit_bytes=VMEM_LIMIT)


def _slot(px, py, pc):
    return 4 * px + 2 * py + pc


def _all_gather(x, name):
    R, C = x.shape

    def body(x_ref, out_ref, send_sems, recv_sems, local_sem):
        xi, yi, ci = lax.axis_index("x"), lax.axis_index("y"), lax.axis_index("c")
        me, sibling = (xi, yi, ci), (xi, yi, 1 - ci)
        chips = [(1 - xi, yi), (xi, 1 - yi), (1 - xi, 1 - yi)]

        def rows(px, py, pc):
            return out_ref.at[_slot(px, py, pc)]

        def copy(k, block, to, src=None):
            return pltpu.make_async_remote_copy(
                src_ref=rows(*block) if src is None else src, dst_ref=rows(*block),
                send_sem=send_sems.at[k], recv_sem=recv_sems.at[k],
                device_id=to, device_id_type=pl.DeviceIdType.MESH)

        mine = pltpu.make_async_copy(x_ref, rows(*me), local_sem)
        mine.start()
        first = [copy(0, me, sibling, src=x_ref)]
        first += [copy(1 + j, me, (*chip, ci), src=x_ref) for j, chip in enumerate(chips)]
        for cp in first:
            cp.start()
        passed = [copy(4 + j, (*chip, ci), sibling) for j, chip in enumerate(chips)]
        for j, chip in enumerate(chips):
            copy(1 + j, (*chip, ci), me).wait_recv()
            passed[j].start()
        copy(0, sibling, me).wait_recv()
        for j, chip in enumerate(chips):
            copy(4 + j, (*chip, 1 - ci), me).wait_recv()
        for cp in first + passed:
            cp.wait_send()
        mine.wait()

    return pl.pallas_call(
        body, name=name,
        out_shape=jax.ShapeDtypeStruct((NDEV, R, C), x.dtype),
        in_specs=[pl.BlockSpec(memory_space=pl.ANY)],
        out_specs=pl.BlockSpec(memory_space=pl.ANY),
        scratch_shapes=[pltpu.SemaphoreType.DMA((7,)), pltpu.SemaphoreType.DMA((7,)),
                        pltpu.SemaphoreType.DMA(())],
    )(x)


def _all_to_all(g, name):
    _, R, C = g.shape

    def body(g_ref, out_ref, send_sems, recv_sems, local_sem):
        xi, yi, ci = lax.axis_index("x"), lax.axis_index("y"), lax.axis_index("c")
        my_slot = _slot(xi, yi, ci)
        mine = pltpu.make_async_copy(g_ref.at[my_slot], out_ref.at[my_slot], local_sem)
        mine.start()
        copies = []
        for k in range(1, 8):
            px = 1 - xi if k & 4 else xi
            py = 1 - yi if k & 2 else yi
            pc = 1 - ci if k & 1 else ci
            peer_slot = _slot(px, py, pc)
            copies.append((
                pltpu.make_async_remote_copy(
                    src_ref=g_ref.at[peer_slot], dst_ref=out_ref.at[my_slot],
                    send_sem=send_sems.at[k - 1], recv_sem=recv_sems.at[k - 1],
                    device_id=(px, py, pc), device_id_type=pl.DeviceIdType.MESH),
                pltpu.make_async_remote_copy(
                    src_ref=g_ref.at[peer_slot], dst_ref=out_ref.at[peer_slot],
                    send_sem=send_sems.at[k - 1], recv_sem=recv_sems.at[k - 1],
                    device_id=(px, py, pc), device_id_type=pl.DeviceIdType.MESH)))
        for send, _ in copies:
            send.start()
        for _, recv in copies:
            recv.wait_recv()
        for send, _ in copies:
            send.wait_send()
        mine.wait()

    return pl.pallas_call(
        body, name=name,
        out_shape=jax.ShapeDtypeStruct((NDEV, R, C), g.dtype),
        in_specs=[pl.BlockSpec(memory_space=pl.ANY)],
        out_specs=pl.BlockSpec(memory_space=pl.ANY),
        scratch_shapes=[pltpu.SemaphoreType.DMA((7,)), pltpu.SemaphoreType.DMA((7,)),
                        pltpu.SemaphoreType.DMA(())],
    )(g)


def _my_slot():
    return _slot(lax.axis_index("x"), lax.axis_index("y"), lax.axis_index("c"))


def _put_own(buf, own, slot):
    return lax.dynamic_update_index_in_dim(buf, own, slot, 0)


def _hbm_call(body, name, ins, out_shapes, n_sems):
    anyspec = pl.BlockSpec(memory_space=pl.ANY)
    return pl.pallas_call(
        body, name=name, out_shape=out_shapes,
        in_specs=[anyspec] * len(ins), out_specs=[anyspec] * len(out_shapes),
        scratch_shapes=[pltpu.SemaphoreType.DMA(s) for s in n_sems],
    )(*ins)


def _all_gather_many(xs, name):
    n = len(xs)

    def body(*refs):
        x_refs, o_refs = refs[:n], refs[n:2 * n]
        send_sems, recv_sems = refs[2 * n:]
        xi, yi, ci = lax.axis_index("x"), lax.axis_index("y"), lax.axis_index("c")
        me, sibling = (xi, yi, ci), (xi, yi, 1 - ci)
        chips = [(1 - xi, yi), (xi, 1 - yi), (1 - xi, 1 - yi)]

        def copy(i, k, block, to, src=None):
            dst = o_refs[i].at[_slot(*block)]
            return pltpu.make_async_remote_copy(
                src_ref=dst if src is None else src, dst_ref=dst,
                send_sem=send_sems.at[i, k], recv_sem=recv_sems.at[i, k],
                device_id=to, device_id_type=pl.DeviceIdType.MESH)

        sends = []
        for i in range(n):
            sends += [copy(i, 1 + j, me, (*chip, ci), src=x_refs[i]) for j, chip in enumerate(chips)]
            sends.append(copy(i, 0, me, sibling, src=x_refs[i]))
        for cp in sends:
            cp.start()
        for j, chip in enumerate(chips):
            for i in range(n):
                copy(i, 1 + j, (*chip, ci), me).wait_recv()
                passed = copy(i, 4 + j, (*chip, ci), sibling)
                passed.start()
                sends.append(passed)
        for i in range(n):
            copy(i, 0, sibling, me).wait_recv()
            for j, chip in enumerate(chips):
                copy(i, 4 + j, (*chip, 1 - ci), me).wait_recv()
        for cp in sends:
            cp.wait_send()

    outs = _hbm_call(body, name, xs, [jax.ShapeDtypeStruct((NDEV,) + x.shape, x.dtype) for x in xs],
                     [(n, 7), (n, 7)])
    return [_put_own(o, x[None], _my_slot()) for o, x in zip(outs, xs)]


def _swap_sibling(gs, name):
    n = len(gs)

    def body(*refs):
        g_refs, o_refs = refs[:n], refs[n:2 * n]
        send_sems, recv_sems = refs[2 * n:]
        sibling = (lax.axis_index("x"), lax.axis_index("y"), 1 - lax.axis_index("c"))
        copies = [pltpu.make_async_remote_copy(
            src_ref=g_refs[i], dst_ref=o_refs[i], send_sem=send_sems.at[i], recv_sem=recv_sems.at[i],
            device_id=sibling, device_id_type=pl.DeviceIdType.MESH) for i in range(n)]
        for cp in copies:
            cp.start()
        for cp in copies:
            cp.wait_recv()
        for cp in copies:
            cp.wait_send()

    return _hbm_call(body, name, gs, [jax.ShapeDtypeStruct(g.shape, g.dtype) for g in gs], [(n,), (n,)])


def _exchange_chips(ps, name):
    n = len(ps)

    def body(*refs):
        p_refs, o_refs = refs[:n], refs[n:2 * n]
        send_sems, recv_sems = refs[2 * n:]
        xi, yi, ci = lax.axis_index("x"), lax.axis_index("y"), lax.axis_index("c")
        q_me = 2 * xi + yi
        sends, recvs = [], []
        for k in range(1, 4):
            px = 1 - xi if k & 2 else xi
            py = 1 - yi if k & 1 else yi
            q_peer = 2 * px + py
            for i in range(n):
                sends.append(pltpu.make_async_remote_copy(
                    src_ref=p_refs[i].at[q_peer], dst_ref=o_refs[i].at[q_me],
                    send_sem=send_sems.at[i, k - 1], recv_sem=recv_sems.at[i, k - 1],
                    device_id=(px, py, ci), device_id_type=pl.DeviceIdType.MESH))
                recvs.append(pltpu.make_async_remote_copy(
                    src_ref=p_refs[i].at[q_peer], dst_ref=o_refs[i].at[q_peer],
                    send_sem=send_sems.at[i, k - 1], recv_sem=recv_sems.at[i, k - 1],
                    device_id=(px, py, ci), device_id_type=pl.DeviceIdType.MESH))
        for cp in sends:
            cp.start()
        for cp in recvs:
            cp.wait_recv()
        for cp in sends:
            cp.wait_send()

    outs = _hbm_call(body, name, ps, [jax.ShapeDtypeStruct(p.shape, p.dtype) for p in ps], [(n, 3), (n, 3)])
    q_me = 2 * lax.axis_index("x") + lax.axis_index("y")
    return [_put_own(o, lax.dynamic_index_in_dim(p, q_me, 0, keepdims=True), q_me) for o, p in zip(outs, ps)]


def _peers(xi, yi, ci):
    return [(1 - xi if k & 4 else xi, 1 - yi if k & 2 else yi, 1 - ci if k & 1 else ci) for k in range(1, 8)]


def _direct_copy(src_refs, land_refs, send_sems, recv_sems, i, k, peer, my_slot, gather):
    src = src_refs[i] if gather else src_refs[i].at[_slot(*peer)]
    return pltpu.make_async_remote_copy(
        src_ref=src, dst_ref=land_refs[i].at[my_slot], send_sem=send_sems.at[7 * i + k], recv_sem=recv_sems.at[7 * i + k],
        device_id=peer, device_id_type=pl.DeviceIdType.MESH)


def _exchange_start(srcs, gather, name):
    n = len(srcs)
    lands = [lax.empty(((NDEV,) + s.shape) if gather else s.shape, s.dtype) for s in srcs]

    def body(*refs):
        s_refs, l_refs = refs[:n], refs[n:2 * n]
        send_sems, recv_sems = refs[2 * n], refs[2 * n + 1]
        token = refs[2 * n + 2 + 2 * n]
        xi, yi, ci = lax.axis_index("x"), lax.axis_index("y"), lax.axis_index("c")
        my_slot = _slot(xi, yi, ci)
        for k, peer in enumerate(_peers(xi, yi, ci)):
            for i in range(n):
                _direct_copy(s_refs, l_refs, send_sems, recv_sems, i, k, peer, my_slot, gather).start()
        token[...] = jnp.zeros_like(token)

    hbm = pl.BlockSpec(memory_space=pltpu.HBM)
    sem = pl.BlockSpec(memory_space=pltpu.SEMAPHORE)
    both = list(srcs) + lands
    return pl.pallas_call(
        body, name=name,
        out_shape=(pltpu.SemaphoreType.DMA((7 * n,)), pltpu.SemaphoreType.DMA((7 * n,)),
                   *[pltpu.HBM(t.shape, t.dtype) for t in both], jax.ShapeDtypeStruct((8, 128), F32)),
        in_specs=[hbm] * (2 * n),
        out_specs=(sem, sem, *[hbm] * (2 * n), pl.BlockSpec(memory_space=pltpu.VMEM)),
        input_output_aliases={i: 2 + i for i in range(2 * n)},
        compiler_params=pltpu.CompilerParams(has_side_effects=pltpu.SideEffectType.DATAFLOW_SIDE_EFFECTING),
    )(*[pltpu.with_memory_space_constraint(t, pltpu.HBM) for t in both])


def _exchange_wait(started, gather, after, name):
    send_sems, recv_sems, *thru, _ = started
    n = len(thru) // 2

    def body(*refs):
        s_refs, l_refs = refs[:n], refs[n:2 * n]
        send_sems, recv_sems = refs[2 * n], refs[2 * n + 1]
        xi, yi, ci = lax.axis_index("x"), lax.axis_index("y"), lax.axis_index("c")
        my_slot = _slot(xi, yi, ci)
        for k, peer in enumerate(_peers(xi, yi, ci)):
            for i in range(n):
                _direct_copy(s_refs, l_refs, send_sems, recv_sems, i, k, peer, my_slot, gather).wait_send()
                _direct_copy(s_refs, l_refs, send_sems, recv_sems, i, k, peer, _slot(*peer), gather).wait_recv()

    hbm = pl.BlockSpec(memory_space=pltpu.HBM)
    sem = pl.BlockSpec(memory_space=pltpu.SEMAPHORE)
    outs = pl.pallas_call(
        body, name=name,
        out_shape=tuple(pltpu.HBM(t.shape, t.dtype) for t in thru),
        in_specs=[hbm] * (2 * n) + [sem, sem, pl.BlockSpec(memory_space=pl.ANY)],
        out_specs=tuple([hbm] * (2 * n)),
        input_output_aliases={i: i for i in range(2 * n)},
        compiler_params=pltpu.CompilerParams(has_side_effects=pltpu.SideEffectType.DATAFLOW_SIDE_EFFECTING),
    )(*thru, send_sems, recv_sems, after)
    slot = _my_slot()
    own = [s[None] if gather else lax.dynamic_index_in_dim(s, slot, 0, keepdims=True) for s in outs[:n]]
    return [_put_own(land, o, slot) for land, o in zip(outs[n:], own)]


def _mm(name, a, b, tb=False, out=(F32,), epi=None, extras=(), tm=1024, tn=512, tk_cap=2048):
    M, K = a.shape
    N = b.shape[0] if tb else b.shape[1]
    tm, tn = min(tm, M), min(tn, N)
    tk = max(t for t in range(128, min(K, tk_cap) + 1, 128) if K % t == 0)
    assert M % tm == 0 and N % tn == 0 and K % tk == 0, (name, M, N, K)
    nk = K // tk
    ne, no = len(extras), len(out)
    dims = (((1,), (1 if tb else 0,)), ((), ()))

    def kern(*refs):
        a_ref, b_ref = refs[:2]
        e_refs = refs[2:2 + ne]
        o_refs = refs[2 + ne:2 + ne + no]

        def finish(acc):
            outs = epi(acc, *[e[...] for e in e_refs]) if epi is not None else (acc,)
            for o_ref, o in zip(o_refs, outs):
                o_ref[...] = o.astype(o_ref.dtype)

        part = lax.dot_general(a_ref[...], b_ref[...], dims, preferred_element_type=F32)
        if nk == 1:
            finish(part)
            return
        acc_ref = refs[-1]
        k = pl.program_id(2)

        @pl.when(k == 0)
        def _():
            acc_ref[...] = part

        @pl.when(k > 0)
        def _():
            acc_ref[...] += part

        @pl.when(k == nk - 1)
        def _():
            finish(acc_ref[...])

    b_spec = (pl.BlockSpec((tn, tk), lambda i, j, k: (j, k)) if tb
              else pl.BlockSpec((tk, tn), lambda i, j, k: (k, j)))
    tile = pl.BlockSpec((tm, tn), lambda i, j, k: (i, j))
    res = pl.pallas_call(
        kern, name=name, grid=(M // tm, N // tn, nk),
        in_specs=[pl.BlockSpec((tm, tk), lambda i, j, k: (i, k)), b_spec] + [tile] * ne,
        out_specs=[tile] * no,
        out_shape=[jax.ShapeDtypeStruct((M, N), dt) for dt in out],
        scratch_shapes=[pltpu.VMEM((tm, tn), F32)] if nk > 1 else [],
        compiler_params=_cp(("parallel", "parallel", "arbitrary")),
    )(a, b, *extras)
    return res[0] if no == 1 else res


def _rows(name, body, rows, params, out_rows, out_accs=(), tm=256):
    views = [r if isinstance(r, tuple) else (r, r.shape[1], 0) for r in rows]
    n = views[0][0].shape[0]
    assert n % tm == 0
    nr, npar, nor, noa = len(views), len(params), len(out_rows), len(out_accs)

    def kern(*refs):
        r_refs = refs[:nr]
        p_refs = refs[nr:nr + npar]
        o_refs = refs[nr + npar:nr + npar + nor]
        a_refs = refs[nr + npar + nor:]
        outs, accs = body([r[...] for r in r_refs], [p[...] for p in p_refs])
        assert len(outs) == nor and len(accs) == noa, (name, len(outs), len(accs))
        for o_ref, o in zip(o_refs, outs):
            o_ref[...] = o.astype(o_ref.dtype)
        if noa:
            @pl.when(pl.program_id(0) == 0)
            def _():
                for a_ref in a_refs:
                    a_ref[...] = jnp.zeros_like(a_ref)

            for a_ref, a in zip(a_refs, accs):
                a_ref[...] += a.astype(F32)

    def whole(shape):
        nd = len(shape)
        return pl.BlockSpec(tuple(shape), lambda i, nd=nd: (0,) * nd)

    in_specs = [pl.BlockSpec((tm, w), lambda i, cb=cb: (i, cb)) for _, w, cb in views]
    in_specs += [whole(p.shape) for p in params]
    out_specs = [pl.BlockSpec((tm, c), lambda i: (i, 0)) for c, _ in out_rows]
    out_specs += [whole(s) for s in out_accs]
    out_shape = [jax.ShapeDtypeStruct((n, c), dt) for c, dt in out_rows]
    out_shape += [jax.ShapeDtypeStruct(tuple(s), F32) for s in out_accs]
    res = pl.pallas_call(
        kern, name=name, grid=(n // tm,), in_specs=in_specs, out_specs=out_specs,
        out_shape=out_shape, compiler_params=_cp(("arbitrary",)),
    )(*[v[0] for v in views], *params)
    return res[:nor], res[nor:]


@jax.custom_vjp
def _headsum(x, e):
    return sum(jnp.dot(p, e, preferred_element_type=F32) for p in _split3(x))


_headsum.defvjp(lambda x, e: (_headsum(x, e), e), lambda e, ct: (_headsum(ct, e), None))


def _softplus(z):
    return jnp.maximum(z, 0.0) + jnp.log(1.0 + jnp.exp(jnp.minimum(z, -z)))


def _post_ln(x, y, g, lng, lnb):
    z = ALPHA * x + (1.0 + g) * y
    mu = jnp.mean(z, axis=-1, keepdims=True)
    zc = z - mu
    var = jnp.mean(zc * zc, axis=-1, keepdims=True)
    return zc * lax.rsqrt(var + LN_EPS) * lng + lnb


def _post_ln_mod(x, y, g, lng, lnb, scn, shn):
    xn = _post_ln(x, y, g, lng, lnb)
    return xn, xn * (1.0 + scn) + shn


def _pre_core(E, r_, k_, v_, wd_, ad_, gd_, r1, k1, v1, wd1, ad1, gd1, h, bg, cg, h1, cg1, h2, cg2,
              mu_r, mu_k, mu_v, mu_wd, mu_ad, mu_gd, w0, w_up, a0, a_up, g_up, k_k, k_a,
              cw0, cw1, cw2):
    def mix(x, x1, mu):
        return x + mu * (x1 - x)

    r, k, v = mix(r_, r1, mu_r), mix(k_, k1, mu_k), mix(v_, v1, mu_v)
    wd, ad, gd = mix(wd_, wd1, mu_wd), mix(ad_, ad1, mu_ad), mix(gd_, gd1, mu_gd)
    logw = -_softplus(-(w0 + jnp.dot(jnp.tanh(wd), w_up, preferred_element_type=F32))) - 0.5
    decay = jnp.exp(-jnp.exp(logw))
    iclr = jax.nn.sigmoid(a0 + jnp.dot(ad, a_up, preferred_element_type=F32))
    gate = jnp.dot(jax.nn.sigmoid(gd), g_up, preferred_element_type=F32)
    kk0 = k * k_k
    nrm = jnp.sqrt(_headsum(kk0 * kk0, E))
    kk = kk0 / jnp.maximum(nrm, 1e-12)
    kh = k * (1.0 + (iclr - 1.0) * k_a)
    yb = bg * (cw2 * (cg * h) + cw1 * (cg1 * h1) + cw0 * (cg2 * h2))
    return r, decay, kh, v, -kk, kk * iclr, gate, yb


def _post_core(E, y, r, kh, v, gate, lnx_g, lnx_b, rk):
    def seg(t):
        return _headsum(t, E)

    mean = seg(y) * (1.0 / HD)
    yc = y - mean
    var = seg(yc * yc) * (1.0 / HD)
    gn = yc * lax.rsqrt(var + GN_EPS) * lnx_g + lnx_b
    bonus = seg(r * kh * rk) * v
    return (gn + bonus) * gate


def _merge_core(o0, o1, o2, l0, l1, l2):
    m = jnp.maximum(jnp.maximum(l0, l1), l2)
    e0, e1, e2 = jnp.exp(l0 - m), jnp.exp(l1 - m), jnp.exp(l2 - m)
    den = e0 + e1 + e2
    return (e0 * o0 + e1 * o1 + e2 * o2) / den


CHUNK = 128
HALF = 64
HP = HEADS // 2
LW = 2 * HD
NCHUNK = T // CHUNK


def _split3(x):
    hi = x.astype(BF16)
    r1 = x - hi.astype(F32)
    mid = r1.astype(BF16)
    return hi, mid, (r1 - mid.astype(F32)).astype(BF16)


def _cols3(parts):
    tr = [p.reshape(T // HALF, HALF, HP, 2, HD).transpose(2, 4, 0, 3, 1) for p in parts]
    return jnp.stack(tr, axis=4).reshape(HP, HD, 6 * T)


def _pick_codes():
    row = lax.broadcasted_iota(jnp.int32, (6 * HALF, LW), 0)
    col = lax.broadcasted_iota(jnp.int32, (6 * HALF, LW), 1)
    same = (row >= 3 * HALF) == (col >= HD)
    return jnp.where(same, row & (HALF - 1), -1).astype(BF16)


def _column(block_ref, codes, half, i):
    pick = jnp.where(codes == i.astype(BF16), jnp.ones((), BF16), jnp.zeros((), BF16))
    block = block_ref[:, :, half * 6 * HALF:(half + 1) * 6 * HALF].reshape(HP * HD, 6 * HALF)
    return jnp.dot(block, pick, preferred_element_type=F32)


def _halfsums(x, row, left1):
    row_l = jnp.where(left1, row, 0.0)
    return (jnp.sum(x * row_l, axis=1, keepdims=True), jnp.sum(x * (row - row_l), axis=1, keepdims=True))


def _pair_rows(row):
    return [row[:, p * LW:(p + 1) * LW] for p in range(HP)]


def _store_columns(ref, p, t_mask, cols):
    for j, col in enumerate(cols):
        pltpu.store(ref.at[pl.ds(2 * p + j, 1)], jnp.broadcast_to(col[None], (1, HD, CHUNK)), mask=t_mask[None])


def _scan_fwd(r, w, k, a, b, v3):
    def kern(r_ref, w_ref, k_ref, a_ref, b_ref, v_ref, y_ref, ck_ref, s_ref, vb_ref):
        @pl.when(pl.program_id(0) == 0)
        def _():
            s_ref[...] = jnp.zeros_like(s_ref)

        lane = lax.broadcasted_iota(jnp.int32, (HD, CHUNK), 1)
        left = lane < HD
        left1 = lax.broadcasted_iota(jnp.int32, (1, LW), 1) < HD
        codes = _pick_codes()

        for half in range(CHUNK // HALF):
            ck_ref[half] = s_ref[...]
            vb_ref[...] = _column(v_ref, codes, half, jnp.int32(0))

            def step(i, carry):
                t = half * HALF + i
                row = lambda ref: _pair_rows(ref[pl.ds(t, 1), :])
                S = [s_ref[p] for p in range(HP)]
                sa = [jnp.where(left, *_halfsums(s, a, left1)) for s, a in zip(S, row(a_ref))]
                S = [s * w + c * b + vb_ref[pl.ds(p * HD, HD), :] * k
                     for p, (s, w, c, b, k) in enumerate(zip(S, row(w_ref), sa, row(b_ref), row(k_ref)))]
                for p, s in enumerate(S):
                    s_ref[p] = s
                for p, (s, r) in enumerate(zip(S, row(r_ref))):
                    _store_columns(y_ref, p, lane == t, _halfsums(s, r, left1))
                vb_ref[...] = _column(v_ref, codes, half, i + 1)
                return carry

            lax.fori_loop(0, HALF, step, 0, unroll=8)

    rowblk = pl.BlockSpec((CHUNK, RW), lambda c: (c, 0))
    return pl.pallas_call(
        kern, name="rwkv_scan_fwd", grid=(NCHUNK,),
        in_specs=[rowblk] * 5 + [pl.BlockSpec((HP, HD, 6 * CHUNK), lambda c: (0, 0, c))],
        out_specs=[pl.BlockSpec((HEADS, HD, CHUNK), lambda c: (0, 0, c)),
                   pl.BlockSpec((CHUNK // HALF, HP, HD, LW), lambda c: (c, 0, 0, 0))],
        out_shape=[jax.ShapeDtypeStruct((HEADS, HD, T), F32),
                   jax.ShapeDtypeStruct((T // HALF, HP, HD, LW), F32)],
        scratch_shapes=[pltpu.VMEM((HP, HD, LW), F32), pltpu.VMEM((HP * HD, LW), F32)],
        compiler_params=_cp(("arbitrary",)),
    )(r, w, k, a, b, v3)


def _scan_bwd(r, w, k, a, b, v3, dy3, ck):
    NC = T // CHUNK

    def kern(r_ref, w_ref, k_ref, a_ref, b_ref, v_ref, dy_ref, ck_ref,
             dr_ref, dw_ref, dk_ref, da_ref, db_ref, dv_ref, ds_ref, sb_ref, vb_ref, sa_ref, dyb_ref):
        @pl.when(pl.program_id(0) == 0)
        def _():
            ds_ref[...] = jnp.zeros_like(ds_ref)

        lane = lax.broadcasted_iota(jnp.int32, (HD, CHUNK), 1)
        left = lane < HD
        left1 = lax.broadcasted_iota(jnp.int32, (1, LW), 1) < HD
        codes = _pick_codes()

        def rowsum(x):
            return jnp.sum(x, axis=0, keepdims=True)

        for half in reversed(range(CHUNK // HALF)):
            base = half * HALF
            sb_ref[0] = ck_ref[half]

            vb_ref[0] = _column(v_ref, codes, half, jnp.int32(0))

            def replay(i, carry):
                t = base + i
                row = lambda ref: _pair_rows(ref[pl.ds(t, 1), :])
                S = [sb_ref[i, p] for p in range(HP)]
                sa = [jnp.where(left, *_halfsums(s, a, left1)) for s, a in zip(S, row(a_ref))]
                for p, (s, w, c, b, k) in enumerate(zip(S, row(w_ref), sa, row(b_ref), row(k_ref))):
                    sb_ref[i + 1, p] = s * w + c * b + vb_ref[i, pl.ds(p * HD, HD), :] * k
                    sa_ref[i, p] = c
                vb_ref[i + 1] = _column(v_ref, codes, half, i + 1)
                return carry

            lax.fori_loop(0, HALF, replay, 0, unroll=4)
            dyb_ref[...] = _column(dy_ref, codes, half, jnp.int32(HALF - 1))

            def back(ii, carry):
                i = HALF - 1 - ii
                t = base + i
                row = lambda ref: _pair_rows(ref[pl.ds(t, 1), :])
                a_r, b_r, k_r, w_r, r_r = row(a_ref), row(b_ref), row(k_ref), row(w_ref), row(r_ref)
                dys = [dyb_ref[pl.ds(p * HD, HD), :] for p in range(HP)]
                dyb_ref[...] = _column(dy_ref, codes, half, jnp.maximum(i - 1, 0))
                dr, dw, db, dk, da = [], [], [], [], []
                for p in range(HP):
                    Sp, dy = sb_ref[i, p], dys[p]
                    dS = ds_ref[p] + dy * r_r[p]
                    dr.append(rowsum(sb_ref[i + 1, p] * dy))
                    dw.append(rowsum(dS * Sp))
                    db.append(rowsum(dS * sa_ref[i, p]))
                    dk.append(rowsum(dS * vb_ref[i, pl.ds(p * HD, HD), :]))
                    dsa = jnp.where(left, *_halfsums(dS, b_r[p], left1))
                    _store_columns(dv_ref, p, lane == t, _halfsums(dS, k_r[p], left1))
                    da.append(rowsum(Sp * dsa))
                    ds_ref[p] = dS * w_r[p] + dsa * a_r[p]
                for ref, pieces in ((dr_ref, dr), (dw_ref, dw), (db_ref, db), (dk_ref, dk), (da_ref, da)):
                    ref[pl.ds(t, 1), :] = jnp.concatenate(pieces, axis=1)
                return carry

            lax.fori_loop(0, HALF, back, 0, unroll=4)

    rowblk = pl.BlockSpec((CHUNK, RW), lambda c: (NC - 1 - c, 0))
    col3blk = pl.BlockSpec((HP, HD, 6 * CHUNK), lambda c: (0, 0, NC - 1 - c))
    rowshape = jax.ShapeDtypeStruct((T, RW), F32)
    return pl.pallas_call(
        kern, name="rwkv_scan_bwd", grid=(NC,),
        in_specs=[rowblk] * 5 + [col3blk, col3blk,
                                 pl.BlockSpec((CHUNK // HALF, HP, HD, LW), lambda c: (NC - 1 - c, 0, 0, 0))],
        out_specs=[rowblk] * 5 + [pl.BlockSpec((HEADS, HD, CHUNK), lambda c: (0, 0, NC - 1 - c))],
        out_shape=[rowshape] * 5 + [jax.ShapeDtypeStruct((HEADS, HD, T), F32)],
        scratch_shapes=[pltpu.VMEM((HP, HD, LW), F32), pltpu.VMEM((HALF + 1, HP, HD, LW), F32),
                        pltpu.VMEM((HALF + 1, HP * HD, LW), F32), pltpu.VMEM((HALF, HP, HD, LW), F32),
                        pltpu.VMEM((HP * HD, LW), F32)],
        compiler_params=_cp(("arbitrary",)),
    )(r, w, k, a, b, v3, dy3, ck)


NT = (((1,), (1,)), ((), ()))
TN = (((0,), (0,)), ((), ()))
SCALE = HD ** -0.5
QKV_G = 3 * RW


def _attn_setup(g):
    dil = DILS[g]
    qkv = [pl.BlockSpec((T, LW), lambda hp, c=(g * QKV_G + s * RW) // LW: (0, c + hp)) for s in range(3)]
    tile = pl.BlockSpec((T, LW), lambda hp: (0, hp))
    bias = pl.BlockSpec((2, BLK, 2 * BLK), lambda hp: (hp, 0, 0))

    def blocks():
        for r in range(dil):
            for n in range(T // dil // BLK):
                rows = pl.ds(n * BLK * dil + r, BLK, stride=dil)
                keys = pl.ds((n - 1) * BLK * dil + r, 2 * BLK, stride=dil) if n else rows
                yield n, rows, keys

    return qkv, tile, bias, blocks


def _band(n):
    qi = lax.broadcasted_iota(jnp.int32, (BLK, 2 * BLK), 0)
    ki = lax.broadcasted_iota(jnp.int32, (BLK, 2 * BLK), 1)
    band = (ki >= qi) & (ki <= qi + BLK)
    return band if n else band[:, BLK:]


def _head_masks():
    lane = lax.broadcasted_iota(jnp.int32, (BLK, LW), 1)
    return lane < HD, [(lane < HD).astype(BF16), (lane >= HD).astype(BF16)]


def _attn_fwd(pq, bias, g):
    qkv, tile, bias_spec, blocks = _attn_setup(g)

    def kern(q_ref, k_ref, v_ref, b_ref, o_ref, l_ref):
        left, masks = _head_masks()
        for n, rows, keys in blocks():
            qb, kc, vc = q_ref[rows, :].astype(BF16), k_ref[keys, :].astype(BF16), v_ref[keys, :].astype(BF16)
            valid = _band(n)
            o, lse = [], []
            for j in range(2):
                bias_j = b_ref[j] if n else b_ref[j][:, BLK:]
                s = lax.dot_general(qb * masks[j], kc, NT, preferred_element_type=F32) * SCALE + bias_j
                s = jnp.where(valid, s, -jnp.inf)
                m = jnp.max(s, axis=1, keepdims=True)
                e = jnp.exp(s - m)
                den = jnp.sum(e, axis=1, keepdims=True)
                o.append(jnp.dot((e / den).astype(BF16), vc, preferred_element_type=F32))
                lse.append(m + jnp.log(den))
            o_ref[rows, :] = jnp.where(left, o[0], o[1])
            l_ref[rows, :] = jnp.where(left, lse[0], lse[1])

    shape = jax.ShapeDtypeStruct((T, RW), F32)
    return pl.pallas_call(
        kern, name=f"attn_fwd_{g}", grid=(HP,),
        in_specs=qkv + [bias_spec], out_specs=[tile, tile], out_shape=[shape, shape],
        compiler_params=_cp(("parallel",)),
    )(pq, pq, pq, bias)


def _attn_bwd(pq, bias, do, o, lse, dlse, g):
    qkv, tile, bias_spec, blocks = _attn_setup(g)

    def kern(q_ref, k_ref, v_ref, b_ref, do_ref, o_ref, l_ref, dl_ref, dq_ref, dk_ref, dv_ref, db_ref):
        left, masks = _head_masks()
        lane = lax.broadcasted_iota(jnp.int32, (BLK, LW), 1)
        dk_ref[...] = jnp.zeros_like(dk_ref)
        dv_ref[...] = jnp.zeros_like(dv_ref)
        db_ref[...] = jnp.zeros_like(db_ref)

        def column(tile_, j):
            return jnp.sum(jnp.where(lane == j * HD, tile_, 0.0), axis=1, keepdims=True)

        for n, rows, keys in blocks():
            qb, kc, vc = q_ref[rows, :].astype(BF16), k_ref[keys, :].astype(BF16), v_ref[keys, :].astype(BF16)
            dof, valid = do_ref[rows, :], _band(n)
            dob, prod = dof.astype(BF16), dof * o_ref[rows, :]
            dq = []
            for j in range(2):
                bias_j = b_ref[j] if n else b_ref[j][:, BLK:]
                delta = jnp.sum(prod * masks[j].astype(F32), axis=1, keepdims=True)
                qm, dom = qb * masks[j], dob * masks[j]
                s = lax.dot_general(qm, kc, NT, preferred_element_type=F32) * SCALE + bias_j
                p = jnp.where(valid, jnp.exp(s - column(l_ref[rows, :], j)), 0.0)
                dp = lax.dot_general(dom, vc, NT, preferred_element_type=F32)
                ds = p * (dp + (column(dl_ref[rows, :], j) - delta))
                if n:
                    db_ref[j] += ds
                else:
                    db_ref[j, :, BLK:] += ds
                dsb = (ds * SCALE).astype(BF16)
                dq.append(jnp.dot(dsb, kc, preferred_element_type=F32))
                dk_ref[keys, :] += lax.dot_general(dsb, qm, TN, preferred_element_type=F32)
                dv_ref[keys, :] += lax.dot_general(p.astype(BF16), dom, TN, preferred_element_type=F32)
            dq_ref[rows, :] = jnp.where(left, dq[0], dq[1])

    shape = jax.ShapeDtypeStruct((T, RW), F32)
    return pl.pallas_call(
        kern, name=f"attn_bwd_{g}", grid=(HP,),
        in_specs=qkv + [bias_spec] + [tile] * 4, out_specs=[tile] * 3 + [bias_spec],
        out_shape=[shape] * 3 + [jax.ShapeDtypeStruct((HEADS, BLK, 2 * BLK), F32)],
        compiler_params=_cp(("parallel",)),
    )(pq, pq, pq, bias, do, o, lse, dlse)


NBUCKET = 32
NPAIR = BLK * 2 * BLK


def _relbias_table(rbT, onehotT):
    def kern(rb_ref, oh_ref, out_ref):
        out_ref[0] = sum(jnp.dot(p, oh_ref[0], preferred_element_type=F32) for p in _split3(rb_ref[0]))

    return pl.pallas_call(
        kern, name="relbias_table", grid=(3,),
        in_specs=[pl.BlockSpec((1, HEADS, NBUCKET), lambda g: (g, 0, 0)),
                  pl.BlockSpec((1, NBUCKET, NPAIR), lambda g: (g, 0, 0))],
        out_specs=pl.BlockSpec((1, HEADS, NPAIR), lambda g: (g, 0, 0)),
        out_shape=jax.ShapeDtypeStruct((3, HEADS, NPAIR), F32),
        compiler_params=_cp(("parallel",)),
    )(rbT, onehotT)


def _relbias_grad(db, onehotT):
    nt = (((1,), (1,)), ((), ()))

    def kern(db_ref, oh_ref, out_ref):
        hi, mid, _ = _split3(db_ref[0])
        out_ref[0] = (lax.dot_general(hi, oh_ref[0], nt, preferred_element_type=F32)
                      + lax.dot_general(mid, oh_ref[0], nt, preferred_element_type=F32))

    return pl.pallas_call(
        kern, name="relbias_grad", grid=(3,),
        in_specs=[pl.BlockSpec((1, HEADS, NPAIR), lambda g: (g, 0, 0)),
                  pl.BlockSpec((1, NBUCKET, NPAIR), lambda g: (g, 0, 0))],
        out_specs=pl.BlockSpec((1, HEADS, NBUCKET), lambda g: (g, 0, 0)),
        out_shape=jax.ShapeDtypeStruct((3, HEADS, NBUCKET), F32),
        compiler_params=_cp(("parallel",)),
    )(db, onehotT)


def _adamw(w, g, m, v):
    m2 = ADAM_B1 * m + (1.0 - ADAM_B1) * g
    v2 = ADAM_B2 * v + (1.0 - ADAM_B2) * (g * g)
    m_hat = m2 / (1.0 - ADAM_B1 ** ADAM_STEP)
    v_hat = v2 / (1.0 - ADAM_B2 ** ADAM_STEP)
    return -ADAM_LR * (m_hat / (jnp.sqrt(v_hat) + ADAM_EPS) + ADAM_WD * w), m2, v2


def _ada_mod(c_all, ada_w, ada_b_loc):
    def kern(c_ref, w_ref, b_ref, o_ref):
        c = c_ref[...]
        cond = c * jax.nn.sigmoid(c)
        o_ref[0] = jnp.dot(cond, w_ref[0], precision=HI, preferred_element_type=F32) + b_ref[0]

    ncol = ada_w.shape[2]
    return pl.pallas_call(
        kern, name="ada_mod", grid=(2,),
        in_specs=[pl.BlockSpec((NDEV, D), lambda i: (0, 0)),
                  pl.BlockSpec((1, D, ncol), lambda i: (i, 0, 0)),
                  pl.BlockSpec((1, 1, ncol), lambda i: (i, 0, 0))],
        out_specs=pl.BlockSpec((1, NDEV, ncol), lambda i: (i, 0, 0)),
        out_shape=jax.ShapeDtypeStruct((2, NDEV, ncol), F32),
        compiler_params=_cp(("parallel",)),
    )(c_all, ada_w, ada_b_loc.reshape(2, 1, ncol))


def _ada_grad_adamw(cT_all, dmod_loc, w, m, v):
    ncol = w.shape[2]
    tr = 256

    def kern(c_ref, d_ref, w_ref, m_ref, v_ref, g_ref, dl_ref, m2_ref, v2_ref):
        c = c_ref[...]
        cond = c * jax.nn.sigmoid(c)
        g = jnp.dot(cond, d_ref[0], precision=HI, preferred_element_type=F32)
        dl, m2, v2 = _adamw(w_ref[0], g, m_ref[0], v_ref[0])
        g_ref[0], dl_ref[0], m2_ref[0], v2_ref[0] = g, dl, m2, v2

    big = pl.BlockSpec((1, tr, ncol), lambda i, j: (i, j, 0))
    shp = jax.ShapeDtypeStruct(w.shape, F32)
    return pl.pallas_call(
        kern, name="ada_grad_adamw", grid=(2, D // tr),
        in_specs=[pl.BlockSpec((tr, NDEV), lambda i, j: (j, 0)),
                  pl.BlockSpec((1, NDEV, ncol), lambda i, j: (i, 0, 0)), big, big, big],
        out_specs=[big] * 4, out_shape=[shp] * 4,
        compiler_params=_cp(("parallel", "parallel")),
    )(cT_all, dmod_loc, w, m, v)


def _sum_adamw(recv, w, m, v, name, tr):
    S = recv.shape[0]
    R, C = w.shape
    assert R % tr == 0 and recv.shape[1:] == (R, C)

    def kern(r_ref, w_ref, m_ref, v_ref, g_ref, dl_ref, m2_ref, v2_ref):
        g = r_ref[0].astype(F32)
        for s in range(1, S):
            g = g + r_ref[s].astype(F32)
        dl, m2, v2 = _adamw(w_ref[...], g, m_ref[...], v_ref[...])
        g_ref[...], dl_ref[...], m2_ref[...], v2_ref[...] = g, dl, m2, v2

    flat = pl.BlockSpec((tr, C), lambda i: (i, 0))
    shp = jax.ShapeDtypeStruct((R, C), F32)
    return pl.pallas_call(
        kern, name=name, grid=(R // tr,),
        in_specs=[pl.BlockSpec((S, tr, C), lambda i: (0, i, 0)), flat, flat, flat],
        out_specs=[flat] * 4, out_shape=[shp] * 4,
        compiler_params=_cp(("parallel",)),
    )(recv, w, m, v)


def _pack(arrs, dtype, row_mult):
    flat = jnp.concatenate([a.reshape(-1).astype(dtype) for a in arrs])
    flat = jnp.pad(flat, (0, -flat.shape[0] % (128 * row_mult)))
    return flat.reshape(-1, 128)


def _pack8(arrs, dtype, row_mult):
    flat = jnp.concatenate([a.reshape(NDEV, -1).astype(dtype) for a in arrs], axis=1)
    flat = jnp.pad(flat, ((0, 0), (0, -flat.shape[1] % (128 * row_mult))))
    return flat.reshape(NDEV, -1, 128)


def _unpack(buf, shapes, lead=()):
    flat = buf.reshape(lead + (-1,))
    out, off = [], 0
    for s in shapes:
        n = math.prod(s)
        out.append(flat[..., off:off + n].reshape(lead + tuple(s)))
        off += n
    return out


def _to_chunks(full, kind):
    if kind == "col":
        x = full.reshape(full.shape[:-1] + (NDEV, full.shape[-1] // NDEV))
        return jnp.moveaxis(x, -2, 0)
    x = full.reshape(full.shape[:-2] + (NDEV, full.shape[-2] // NDEV, full.shape[-1]))
    return jnp.moveaxis(x, -3, 0)


def _from_chunks(g8, kind):
    if kind == "col":
        x = jnp.moveaxis(g8, 0, -2)
        return x.reshape(x.shape[:-2] + (x.shape[-2] * x.shape[-1],))
    x = jnp.moveaxis(g8, 0, -3)
    return x.reshape(x.shape[:-3] + (x.shape[-3] * x.shape[-2], x.shape[-1]))


def _pad_pa(x):
    z = lambda n: jnp.zeros(x.shape[:-1] + (n,), x.dtype)
    return jnp.concatenate([x[..., :1600], z(64), x[..., 1600:1664], z(64), x[..., 1664:1824], z(96)], -1)


def _unpad_pa(x):
    return jnp.concatenate([x[..., :1600], x[..., 1664:1728], x[..., 1792:1952]], -1)


def _pad_rows(x, n):
    return jnp.pad(x, ((0, n - x.shape[0]), (0, 0)))


def _shift_down(x, n):
    return jnp.pad(x, ((n, 0), (0, 0)))[:-n]


def _shift_up(x, n):
    return jnp.pad(x, ((0, n), (0, 0)))[n:]


def _unheadsT(x):
    return x.transpose(2, 0, 1).reshape(T, RW)


def _bucket_tables():
    qi = jnp.arange(BLK)[:, None]
    ki = jnp.arange(2 * BLK)[None, :]
    rel = BLK + qi - ki
    tabs = []
    for dil in DILS:
        dist = jnp.clip(rel, 0, BLK) * dil
        logd = jnp.log(jnp.maximum(dist, 1).astype(F32) / 16) / math.log(2048 / 16)
        large = jnp.minimum(16 + (logd * 16).astype(jnp.int32), 31)
        tabs.append(jnp.where(dist < 16, dist, large))
    return jnp.stack(tabs)


SHARDED = (("ln_g", "col"), ("ln_b", "col"), ("ab_w_in", "col"), ("rw_w_up", "col"), ("rw_a_up", "col"),
           ("rw_g_up", "col"), ("sc_conv_w", "col"), ("ab_w_out", "row"), ("dil_w_qkv", "col"),
           ("dil_w_out", "col"), ("mlp_w1", "col"), ("mlp_w2", "row"))
FIRST = ("ab_w_in", "ab_w_out")
LATER = ("dil_w_qkv", "dil_w_out", "mlp_w1", "mlp_w2")
GATHER_BF16 = FIRST + LATER
GATHER_F32 = ("rw_w_up", "rw_a_up", "rw_g_up", "sc_conv_w", "ln_g", "ln_b")
REPLICATED = ("ada_b", "rw_mu", "rw_w0", "rw_a0", "rw_k_k", "rw_k_a", "rw_r_k", "rw_lnx_g", "rw_lnx_b", "rel_bias")
WEIGHTS = ("ada_w", "ada_b", "ln_g", "ln_b", "ab_w_in", "rw_mu", "rw_w0", "rw_w_up", "rw_a0", "rw_a_up",
           "rw_g_up", "rw_k_k", "rw_k_a", "rw_r_k", "rw_lnx_g", "rw_lnx_b", "sc_conv_w", "ab_w_out",
           "dil_w_qkv", "dil_w_out", "rel_bias", "mlp_w1", "mlp_w2")
FLAT_TILE = 512


def _local_step(x0, tgt, mod, W, P, later_weights, early_grads):
    row = lambda a: a.reshape(1, -1)
    W = dict(W)
    m6 = mod.reshape(2, 6, 1, D)
    sc = [m6[0, 1], m6[0, 4], m6[1, 1], m6[1, 4]]
    sh = [m6[0, 0], m6[0, 3], m6[1, 0], m6[1, 3]]
    gt = [m6[0, 2], m6[0, 5], m6[1, 2], m6[1, 5]]
    lng = [row(P["ln_g"][0, 0]), row(P["ln_g"][0, 1]), row(P["ln_g"][1, 0]), row(P["ln_g"][1, 1])]
    lnb = [row(P["ln_b"][0, 0]), row(P["ln_b"][0, 1]), row(P["ln_b"][1, 0]), row(P["ln_b"][1, 1])]
    E = jnp.kron(jnp.eye(HEADS, dtype=BF16), jnp.ones((HD, HD), BF16))

    def mod_body(r, p):
        return [r[0] * (1.0 + p[0]) + p[1]], []

    (u0,), _ = _rows("modulate", mod_body, [x0], [sc[0], sh[0]], [(D, BF16)])

    def post_fwd_body(r, p):
        xn, un = _post_ln_mod(r[0], r[1], *p)
        return [xn, un], []

    def post_fwd(s, x, y):
        (xn, un), _ = _rows(f"post_ln_{s}", post_fwd_body, [x, y],
                            [gt[s], lng[s], lnb[s], sc[s + 1], sh[s + 1]], [(D, F32), (D, BF16)])
        return xn, un

    def relu2(acc):
        a = jnp.maximum(acc, 0.0)
        return acc, a * a

    def relu2_bwd(acc, h):
        return (acc * (2.0 * jnp.maximum(h, 0.0)),)

    p = _mm("ab_in", u0, W["ab_w_in"])
    p1 = _shift_down(p, 1)
    p2 = _shift_down(p[:, PA:], 2)
    mu = _pad_pa(P["rw_mu"])
    mu_parts = [mu[:, :512], mu[:, 512:1024], mu[:, 1024:1536], mu[:, 1536:1664], mu[:, 1664:1792], mu[:, 1792:]]
    pre_params = mu_parts + [P["rw_w0"], _pad_rows(P["rw_w_up"], 128), P["rw_a0"], _pad_rows(P["rw_a_up"], 128),
                             _pad_rows(P["rw_g_up"], 256), P["rw_k_k"], P["rw_k_a"],
                             P["sc_conv_w"][0:1], P["sc_conv_w"][1:2], P["sc_conv_w"][2:3]]
    pre_rows = [(p, 512, 0), (p, 512, 1), (p, 512, 2), (p, 128, 12), (p, 128, 13), (p, 256, 7),
                (p1, 512, 0), (p1, 512, 1), (p1, 512, 2), (p1, 128, 12), (p1, 128, 13), (p1, 256, 7),
                (p, 512, 4), (p, 512, 5), (p, 512, 6), (p1, 512, 4), (p1, 512, 6), (p2, 512, 0), (p2, 512, 2)]
    NPR = len(pre_rows)

    def pre_fwd_body(r, pp):
        outs = list(_pre_core(pp[0], *r, *pp[1:]))
        return outs + list(_split3(outs[3])), []

    (r_, w_, kh_, v_, a_, b_, gate_, yb, *v_parts), _ = _rows(
        "rwkv_pre", pre_fwd_body, pre_rows, [E] + pre_params,
        [(RW, F32)] * 7 + [(RW, BF16)] * 4, tm=256)
    scan_in = [r_, w_, kh_, a_, b_, _cols3(v_parts)]
    yT, ck = _scan_fwd(*scan_in)
    ysc = _unheadsT(yT)
    post_params = [P["rw_lnx_g"], P["rw_lnx_b"], P["rw_r_k"].reshape(1, RW)]

    def postmix_fwd_body(r, pp):
        return [_post_core(pp[0], *r, *pp[1:])], []

    (ya,), _ = _rows("rwkv_post", postmix_fwd_body, [ysc, r_, kh_, v_, gate_], [E] + post_params,
                     [(RW, BF16)], tm=256)
    cat = jnp.concatenate([ya, yb], axis=1)
    y0 = _mm("ab_out", cat, W["ab_w_out"])
    x1, u1 = post_fwd(0, x0, y0)
    W.update(later_weights(u1))

    h1, a1 = _mm("mlp1_up_0", u1, W["mlp_w1"][0], out=(F32, BF16), epi=relu2)
    y1 = _mm("mlp1_down_0", a1, W["mlp_w2"][0])
    x2, u2 = post_fwd(1, x1, y1)

    pq = _mm("qkv", u2, W["dil_w_qkv"])
    onehotT = (_bucket_tables().reshape(3, 1, NPAIR) == jnp.arange(NBUCKET).reshape(1, NBUCKET, 1)).astype(BF16)
    rbT = P["rel_bias"].reshape(NBUCKET, 3, HEADS).transpose(1, 2, 0)
    bias = _relbias_table(rbT, onehotT).reshape(3, HEADS, BLK, 2 * BLK)
    og, lse = zip(*[_attn_fwd(pq, bias[g], g) for g in range(3)])

    def merge_fwd_body(r, pp):
        return [_merge_core(*r)], []

    (om,), _ = _rows("attn_merge", merge_fwd_body, list(og + lse), [], [(RW, BF16)])
    y2 = _mm("dil_out", om, W["dil_w_out"])
    x3, u3 = post_fwd(2, x2, y2)

    h3, a3 = _mm("mlp1_up_1", u3, W["mlp_w1"][1], out=(F32, BF16), epi=relu2)
    y3 = _mm("mlp1_down_1", a3, W["mlp_w2"][1])

    def last_body(r, pp):
        x, y, tg = r
        xn, vjp = jax.vjp(_post_ln, x, y, *pp)
        err = xn - tg
        dx, dy, dg, dlg, dlb = vjp(err * (1.0 / D))
        loss = jnp.full((1, 128), (0.5 / D) * jnp.sum(err * err), F32)
        return [dx, dy], [loss, dg, dlg, dlb]

    (dxp, dy3), (loss_acc, dg3, dlng3, dlnb3) = _rows(
        "final_ln_loss", last_body, [x3, y3, tgt], [gt[3], lng[3], lnb[3]],
        [(D, F32), (D, BF16)], [(1, 128), (1, D), (1, D), (1, D)])

    G = {}
    dsc, dsh, dgt = [None] * 4, [None] * 4, [None] * 4
    dlng, dlnb = [None] * 4, [None] * 4
    dgt[3], dlng[3], dlnb[3] = dg3, dlng3, dlnb3

    def mlp_bwd(i, u, h, a, dy):
        dh = _mm(f"mlp_dh_{i}", dy, W["mlp_w2"][i], tb=True, out=(BF16,), epi=relu2_bwd, extras=(h,))
        gw2 = _mm(f"mlp_dw2_{i}", a.T, dy)
        du = _mm(f"mlp_du_{i}", dh, W["mlp_w1"][i], tb=True)
        gw1 = _mm(f"mlp_dw1_{i}", u.T, dh)
        return du, gw1, gw2

    def post_bwd_body(r, pp):
        x, y, dxn, dun = r
        _, vjp = jax.vjp(_post_ln_mod, x, y, *pp)
        dx, dy, dg, dlg, dlb, dscn, dshn = vjp((dxn, dun))
        return [dx, dy], [dg, dlg, dlb, dscn, dshn]

    def post_bwd(s, x, y, dxn, dun):
        (dx, dy), (dgt[s], dlng[s], dlnb[s], dsc[s + 1], dsh[s + 1]) = _rows(
            f"post_ln_bwd_{s}", post_bwd_body, [x, y, dxn, dun],
            [gt[s], lng[s], lnb[s], sc[s + 1], sh[s + 1]], [(D, F32), (D, BF16)], [(1, D)] * 5)
        return dx, dy

    du3, gw1_1, gw2_1 = mlp_bwd(1, u3, h3, a3, dy3)
    dxp, dy2 = post_bwd(2, x2, y2, dxp, du3)

    G["dil_w_out"] = _mm("dil_out_dw", om.T, dy2)[None]
    do = _mm("dil_out_dx", dy2, W["dil_w_out"], tb=True)

    def merge_bwd_body(r, pp):
        _, vjp = jax.vjp(_merge_core, *r[:6])
        d = vjp(r[6])
        return list(d[:3]) + [_headsum(d[3 + g], pp[0]) for g in range(3)], []

    mb, _ = _rows("attn_merge_bwd", merge_bwd_body, list(og + lse) + [do], [E],
                  [(RW, F32)] * 6)
    back = [_attn_bwd(pq, bias[g], mb[g], og[g], lse[g], mb[3 + g], g) for g in range(3)]
    dpq = jnp.concatenate([t for dq, dk, dv, _ in back for t in (dq, dk, dv)], axis=1).astype(BF16)
    rb = _relbias_grad(jnp.stack([b[3] for b in back]).reshape(3, HEADS, NPAIR), onehotT)
    G["rel_bias"] = rb.transpose(2, 0, 1).reshape(NBUCKET, 3 * HEADS)
    G["dil_w_qkv"] = _mm("qkv_dw", u2.T, dpq)[None]
    du2 = _mm("qkv_dx", dpq, W["dil_w_qkv"], tb=True)
    dxp, dy1 = post_bwd(1, x1, y1, dxp, du2)

    du1, gw1_0, gw2_0 = mlp_bwd(0, u1, h1, a1, dy1)
    G["mlp_w1"] = jnp.stack([gw1_0, gw1_1])
    G["mlp_w2"] = jnp.stack([gw2_0, gw2_1])
    gt[0] = gt[0] + early_grads(G)
    dxp, dy0 = post_bwd(0, x0, y0, dxp, du1)

    G["ab_w_out"] = _mm("ab_out_dw", cat.T, dy0)[None]
    dcat = _mm("ab_out_dx", dy0, W["ab_w_out"], tb=True)

    def postmix_bwd_body(r, pp):
        _, vjp = jax.vjp(functools.partial(_post_core, pp[0]), *r[:5], *pp[1:])
        d = vjp(r[5])
        return list(_split3(d[0])) + list(d[1:5]), list(d[5:])

    (*dy_parts, dr1, dkh1, dv1, dgate), (G["rw_lnx_g"], G["rw_lnx_b"], drk) = _rows(
        "rwkv_post_bwd", postmix_bwd_body, [ysc, r_, kh_, v_, gate_, (dcat, 512, 0)], [E] + post_params,
        [(RW, BF16)] * 3 + [(RW, F32)] * 4, [(1, RW)] * 3, tm=256)
    G["rw_r_k"] = drk.reshape(1, HEADS, HD)
    dr2, dw2, dk2, da2, db2, dvT = _scan_bwd(*scan_in, _cols3(dy_parts), ck)
    dv2 = _unheadsT(dvT)

    def pre_bwd_body(r, pp):
        prim, ct = r[:NPR], r[NPR:]
        _, vjp = jax.vjp(functools.partial(_pre_core, pp[0]), *prim, *pp[1:])
        cts = (ct[0] + ct[1], ct[2], ct[3] + ct[4], ct[5] + ct[6], ct[7], ct[8], ct[9], ct[10])
        d = vjp(cts)
        z = jnp.zeros_like(d[12])
        dp = jnp.concatenate([d[0], d[1], d[2], d[3], d[4], d[5], d[12], d[13], d[14]], axis=1)
        dp1 = jnp.concatenate([d[6], d[7], d[8], d[9], d[10], d[11], d[15], z, d[16]], axis=1)
        dp2 = jnp.concatenate([d[17], z, d[18]], axis=1)
        return [dp, dp1, dp2], list(d[NPR:])

    acc_shapes = [a.shape for a in pre_params]
    (dp, dp1, dp2), pacc = _rows(
        "rwkv_pre_bwd", pre_bwd_body,
        pre_rows + [dr1, dr2, dw2, dkh1, dk2, dv1, dv2, da2, db2, dgate, (dcat, 512, 1)],
        [E] + pre_params, [(PAB, F32), (PAB, F32), (PB, F32)], acc_shapes, tm=256)
    G["rw_mu"] = _unpad_pa(jnp.concatenate(pacc[:6], axis=1))
    G["rw_w0"], G["rw_a0"], G["rw_k_k"], G["rw_k_a"] = pacc[6], pacc[8], pacc[11], pacc[12]
    G["rw_w_up"] = pacc[7][None, :64]
    G["rw_a_up"] = pacc[9][None, :64]
    G["rw_g_up"] = pacc[10][None, :160]
    G["sc_conv_w"] = jnp.concatenate(pacc[13:16], axis=0)[None]

    def add3_body(r, pp):
        return [r[0] + r[1] + r[2]], []

    (dpt,), _ = _rows("shift_merge", add3_body,
                      [dp, _shift_up(dp1, 1), jnp.pad(_shift_up(dp2, 2), ((0, 0), (PA, 0)))], [], [(PAB, BF16)])
    gin = _mm("ab_in_dw", u0.T, dpt)
    G["ab_w_in"] = jnp.concatenate([_unpad_pa(gin[:, :PA]), gin[:, PA:]], axis=1)[None]
    du0 = _mm("ab_in_dx", dpt, W["ab_w_in"], tb=True)

    def mod_bwd_body(r, pp):
        du, dx, x = r
        return [dx + du * (1.0 + pp[0])], [jnp.sum(du * x, axis=0, keepdims=True), jnp.sum(du, axis=0, keepdims=True)]

    (grad_x,), (dsc[0], dsh[0]) = _rows("modulate_bwd", mod_bwd_body, [du0, dxp, x0], [sc[0]], [(D, F32)],
                                        [(1, D), (1, D)])

    G["ln_g"] = jnp.concatenate(dlng, axis=0).reshape(2, 2, D)
    G["ln_b"] = jnp.concatenate(dlnb, axis=0).reshape(2, 2, D)
    dmod = jnp.concatenate([dsh[0], dsc[0], dgt[0], dsh[1], dsc[1], dgt[1],
                            dsh[2], dsc[2], dgt[2], dsh[3], dsc[3], dgt[3]], axis=1).reshape(2, 6 * D)
    return loss_acc[0, 0], grad_x, dmod, G


def kernel(x, c, ada_w, ada_b, ln_g, ln_b, ab_w_in, rw_mu, rw_w0, rw_w_up, rw_a0, rw_a_up, rw_g_up, rw_k_k, rw_k_a, rw_r_k, rw_lnx_g, rw_lnx_b, sc_conv_w, ab_w_out, dil_w_qkv, dil_w_out, rel_bias, mlp_w1, mlp_w2, loss_target, m_ada_w, m_ada_b, m_ln_g, m_ln_b, m_ab_w_in, m_rw_mu, m_rw_w0, m_rw_w_up, m_rw_a0, m_rw_a_up, m_rw_g_up, m_rw_k_k, m_rw_k_a, m_rw_r_k, m_rw_lnx_g, m_rw_lnx_b, m_sc_conv_w, m_ab_w_out, m_dil_w_qkv, m_dil_w_out, m_rel_bias, m_mlp_w1, m_mlp_w2, v_ada_w, v_ada_b, v_ln_g, v_ln_b, v_ab_w_in, v_rw_mu, v_rw_w0, v_rw_w_up, v_rw_a0, v_rw_a_up, v_rw_g_up, v_rw_k_k, v_rw_k_a, v_rw_r_k, v_rw_lnx_g, v_rw_lnx_b, v_sc_conv_w, v_ab_w_out, v_dil_w_qkv, v_dil_w_out, v_rel_bias, v_mlp_w1, v_mlp_w2):
    w = dict(ada_w=ada_w, ada_b=ada_b, ln_g=ln_g, ln_b=ln_b, ab_w_in=ab_w_in, rw_mu=rw_mu, rw_w0=rw_w0,
             rw_w_up=rw_w_up, rw_a0=rw_a0, rw_a_up=rw_a_up, rw_g_up=rw_g_up, rw_k_k=rw_k_k, rw_k_a=rw_k_a,
             rw_r_k=rw_r_k, rw_lnx_g=rw_lnx_g, rw_lnx_b=rw_lnx_b, sc_conv_w=sc_conv_w, ab_w_out=ab_w_out,
             dil_w_qkv=dil_w_qkv, dil_w_out=dil_w_out, rel_bias=rel_bias, mlp_w1=mlp_w1, mlp_w2=mlp_w2)
    m = dict(ada_w=m_ada_w, ada_b=m_ada_b, ln_g=m_ln_g, ln_b=m_ln_b, ab_w_in=m_ab_w_in, rw_mu=m_rw_mu,
             rw_w0=m_rw_w0, rw_w_up=m_rw_w_up, rw_a0=m_rw_a0, rw_a_up=m_rw_a_up, rw_g_up=m_rw_g_up,
             rw_k_k=m_rw_k_k, rw_k_a=m_rw_k_a, rw_r_k=m_rw_r_k, rw_lnx_g=m_rw_lnx_g, rw_lnx_b=m_rw_lnx_b,
             sc_conv_w=m_sc_conv_w, ab_w_out=m_ab_w_out, dil_w_qkv=m_dil_w_qkv, dil_w_out=m_dil_w_out,
             rel_bias=m_rel_bias, mlp_w1=m_mlp_w1, mlp_w2=m_mlp_w2)
    v = dict(ada_w=v_ada_w, ada_b=v_ada_b, ln_g=v_ln_g, ln_b=v_ln_b, ab_w_in=v_ab_w_in, rw_mu=v_rw_mu,
             rw_w0=v_rw_w0, rw_w_up=v_rw_w_up, rw_a0=v_rw_a0, rw_a_up=v_rw_a_up, rw_g_up=v_rw_g_up,
             rw_k_k=v_rw_k_k, rw_k_a=v_rw_k_a, rw_r_k=v_rw_r_k, rw_lnx_g=v_rw_lnx_g, rw_lnx_b=v_rw_lnx_b,
             sc_conv_w=v_sc_conv_w, ab_w_out=v_ab_w_out, dil_w_qkv=v_dil_w_qkv, dil_w_out=v_dil_w_out,
             rel_bias=v_rel_bias, mlp_w1=v_mlp_w1, mlp_w2=v_mlp_w2)
    kinds = dict(SHARDED)
    me = 4 * lax.axis_index("x") + 2 * lax.axis_index("y") + lax.axis_index("c")
    ncol = ada_w.shape[2]

    small = _all_gather(_pack([c] + [w[n] for n in GATHER_F32], F32, 8), "gather_small")
    parts = _unpack(small, [c.shape] + [w[n].shape for n in GATHER_F32], (NDEV,))
    c_all = parts[0].reshape(NDEV, D)
    P = {n: _from_chunks(t, kinds[n]) for n, t in zip(GATHER_F32, parts[1:])}
    P = {n: (t if n in ("ln_g", "ln_b") else t[0]) for n, t in P.items()}
    for n in REPLICATED[1:]:
        P[n] = w[n]
    def full(n, t):
        t = _from_chunks(t, kinds[n])
        return t if n in ("mlp_w1", "mlp_w2") else t[0]

    parts = _all_gather_many([w[n].astype(BF16) for n in FIRST], "gather_first_weights")
    W = {n: full(n, t) for n, t in zip(FIRST, parts)}
    W["ab_w_in"] = jnp.concatenate([_pad_pa(W["ab_w_in"][:, :1824]), W["ab_w_in"][:, 1824:]], axis=1)

    ada_b_loc = lax.dynamic_slice(ada_b, (0, ncol * me), (2, ncol))
    mod_part = _ada_mod(c_all, ada_w, ada_b_loc)
    mod_all = _all_gather(mod_part.reshape(-1, 128), "gather_mod").reshape(NDEV, 2, NDEV, ncol)
    mod = lax.dynamic_index_in_dim(mod_all, me, axis=2, keepdims=False)
    mod = mod.transpose(1, 0, 2).reshape(2, 6 * D)

    behind = (mod[0, 0] * 0.0).astype(BF16)
    later = _exchange_start([w[n].astype(BF16) + (behind if n == LATER[0] else 0) for n in LATER], True,
                            "gather_later_weights_start")
    mod = mod + later[-1][0, 0]

    def later_weights(after):
        lands = _exchange_wait(later, True, after, "gather_later_weights_wait")
        return {n: full(n, t) for n, t in zip(LATER, lands)}

    sent = []

    def early_grads(G):
        sent.append(_exchange_start([_to_chunks(G[n], kinds[n]).astype(BF16) for n in LATER], False,
                                    "exchange_later_grads_start"))
        return sent[0][-1][0, 0]

    loss_part, grad_x, dmod, G = _local_step(x[0], loss_target[0], mod, W, P, later_weights, early_grads)
    G["ada_b"] = dmod
    loss = lax.psum(loss_part, ("x", "y", "c"))

    rep_shapes = [w[n].shape for n in REPLICATED]
    rep_all = _all_gather(_pack([G[n] for n in REPLICATED], F32, 8), "gather_replicated_grads")
    pk = lambda d: _pack([d[n] for n in REPLICATED], F32, 8)
    rep_out = _sum_adamw(rep_all, pk(w), pk(m), pk(v), "sum_adamw_replicated", rep_all.shape[1])
    rep_out = [dict(zip(REPLICATED, _unpack(o, rep_shapes))) for o in rep_out]

    dmod_all = _unpack(rep_all, [(2, 6 * D)], (NDEV,))[0]
    dmod_loc = lax.dynamic_slice(dmod_all, (0, 0, ncol * me), (NDEV, 2, ncol)).transpose(1, 0, 2)
    ada_out = _ada_grad_adamw(c_all.T, dmod_loc, ada_w, m_ada_w, v_ada_w)

    names = [n for n, _ in SHARDED if n not in GATHER_BF16]
    shard_shapes = [w[n].shape for n in names]
    chunks = _pack8([_to_chunks(G[n], kinds[n]) for n in names], F32, 8)
    recv = _all_to_all(chunks, "exchange_small_grads")
    pk = lambda d: _pack([d[n] for n in names], F32, 8)
    sh_out = _sum_adamw(recv, pk(w), pk(m), pk(v), "sum_adamw_small", recv.shape[1])
    sh_out = [dict(zip(names, _unpack(o, shard_shapes))) for o in sh_out]

    big_out = {}

    def update(n, contributions):
        cols = w[n].shape[-1]
        flat = lambda t: t.reshape(-1, cols)
        rows = flat(w[n]).shape[0]
        outs = _sum_adamw(contributions.reshape(-1, rows, cols), flat(w[n]), flat(m[n]), flat(v[n]),
                          f"sum_adamw_{n}", min(rows, 256))
        big_out[n] = [o.reshape(w[n].shape) for o in outs]

    ci = lax.axis_index("c")
    mine_l, sib_l = [], []
    for n in FIRST:
        g8 = _to_chunks(G[n], kinds[n])
        g42 = g8.reshape((4, 2) + g8.shape[1:])
        mine_l.append(lax.dynamic_index_in_dim(g42, ci, 1, keepdims=False))
        sib_l.append(lax.dynamic_index_in_dim(g42, 1 - ci, 1, keepdims=False))
    from_sib = _swap_sibling(sib_l, "swap_sibling_grads")

    def add2_body(r, pp):
        return [r[0] + r[1]], []

    partials = []
    for n, a, b in zip(FIRST, mine_l, from_sib):
        cols = a.shape[-1]
        (p,), _ = _rows(f"pair_sum_{n}", add2_body, [a.reshape(-1, cols), b.reshape(-1, cols)], [],
                        [(cols, BF16)], tm=512)
        partials.append(p.reshape(a.shape))
    for n, r in zip(FIRST, _exchange_chips(partials, "exchange_chip_grads")):
        update(n, r)

    for n, r in zip(LATER, _exchange_wait(sent[0], False, partials[0], "exchange_later_grads_wait")):
        update(n, r)
    sh_out = [{**d, **{n: big_out[n][i] for n in GATHER_BF16}} for i, d in enumerate(sh_out)]

    def pick(i, n):
        if n == "ada_w":
            return ada_out[i]
        return rep_out[i][n] if n in REPLICATED else sh_out[i][n]

    outs = [loss, grad_x[None]]
    for i in range(4):
        outs += [pick(i, n) for n in WEIGHTS]
    return tuple(outs)
```

```python
import functools
import math

import jax
import jax.numpy as jnp
from jax import lax
from jax.experimental import pallas as pl
from jax.experimental.pallas import tpu as pltpu

F32 = jnp.float32
BF16 = jnp.bfloat16
HI = lax.Precision.HIGHEST

NDEV = 8
T = 2048
D = 1024
DFF = 4096
HEADS = 8
HD = 64
RW = 512
PA = 2048
PB = 1536
PAB = PA + PB
QKV = 4608
DILS = (1, 4, 16)
BLK = 128
ALPHA = 4.0 ** 0.25
LN_EPS = 1e-5
GN_EPS = 64e-5
ADAM_LR, ADAM_B1, ADAM_B2, ADAM_EPS, ADAM_WD, ADAM_STEP = 0.001, 0.9, 0.999, 1e-8, 0.01, 10
VMEM_LIMIT = 56 * 1024 * 1024


def _cp(sem):
    return pltpu.CompilerParams(dimension_semantics=sem, vmem_limit_bytes=VMEM_LIMIT)


def _slot(px, py, pc):
    return 4 * px + 2 * py + pc


def _all_gather(x, name):
    R, C = x.shape

    def body(x_ref, out_ref, send_sems, recv_sems, local_sem):
        xi, yi, ci = lax.axis_index("x"), lax.axis_index("y"), lax.axis_index("c")
        me, sibling = (xi, yi, ci), (xi, yi, 1 - ci)
        chips = [(1 - xi, yi), (xi, 1 - yi), (1 - xi, 1 - yi)]

        def rows(px, py, pc):
            return out_ref.at[_slot(px, py, pc)]

        def copy(k, block, to, src=None):
            return pltpu.make_async_remote_copy(
                src_ref=rows(*block) if src is None else src, dst_ref=rows(*block),
                send_sem=send_sems.at[k], recv_sem=recv_sems.at[k],
                device_id=to, device_id_type=pl.DeviceIdType.MESH)

        mine = pltpu.make_async_copy(x_ref, rows(*me), local_sem)
        mine.start()
        first = [copy(0, me, sibling, src=x_ref)]
        first += [copy(1 + j, me, (*chip, ci), src=x_ref) for j, chip in enumerate(chips)]
        for cp in first:
            cp.start()
        passed = [copy(4 + j, (*chip, ci), sibling) for j, chip in enumerate(chips)]
        for j, chip in enumerate(chips):
            copy(1 + j, (*chip, ci), me).wait_recv()
            passed[j].start()
        copy(0, sibling, me).wait_recv()
        for j, chip in enumerate(chips):
            copy(4 + j, (*chip, 1 - ci), me).wait_recv()
        for cp in first + passed:
            cp.wait_send()
        mine.wait()

    return pl.pallas_call(
        body, name=name,
        out_shape=jax.ShapeDtypeStruct((NDEV, R, C), x.dtype),
        in_specs=[pl.BlockSpec(memory_space=pl.ANY)],
        out_specs=pl.BlockSpec(memory_space=pl.ANY),
        scratch_shapes=[pltpu.SemaphoreType.DMA((7,)), pltpu.SemaphoreType.DMA((7,)),
                        pltpu.SemaphoreType.DMA(())],
    )(x)


def _all_to_all(g, name):
    _, R, C = g.shape

    def body(g_ref, out_ref, send_sems, recv_sems, local_sem):
        xi, yi, ci = lax.axis_index("x"), lax.axis_index("y"), lax.axis_index("c")
        my_slot = _slot(xi, yi, ci)
        mine = pltpu.make_async_copy(g_ref.at[my_slot], out_ref.at[my_slot], local_sem)
        mine.start()
        copies = []
        for k in range(1, 8):
            px = 1 - xi if k & 4 else xi
            py = 1 - yi if k & 2 else yi
            pc = 1 - ci if k & 1 else ci
            peer_slot = _slot(px, py, pc)
            copies.append((
                pltpu.make_async_remote_copy(
                    src_ref=g_ref.at[peer_slot], dst_ref=out_ref.at[my_slot],
                    send_sem=send_sems.at[k - 1], recv_sem=recv_sems.at[k - 1],
                    device_id=(px, py, pc), device_id_type=pl.DeviceIdType.MESH),
                pltpu.make_async_remote_copy(
                    src_ref=g_ref.at[peer_slot], dst_ref=out_ref.at[peer_slot],
                    send_sem=send_sems.at[k - 1], recv_sem=recv_sems.at[k - 1],
                    device_id=(px, py, pc), device_id_type=pl.DeviceIdType.MESH)))
        for send, _ in copies:
            send.start()
        for _, recv in copies:
            recv.wait_recv()
        for send, _ in copies:
            send.wait_send()
        mine.wait()

    return pl.pallas_call(
        body, name=name,
        out_shape=jax.ShapeDtypeStruct((NDEV, R, C), g.dtype),
        in_specs=[pl.BlockSpec(memory_space=pl.ANY)],
        out_specs=pl.BlockSpec(memory_space=pl.ANY),
        scratch_shapes=[pltpu.SemaphoreType.DMA((7,)), pltpu.SemaphoreType.DMA((7,)),
                        pltpu.SemaphoreType.DMA(())],
    )(g)


def _my_slot():
    return _slot(lax.axis_index("x"), lax.axis_index("y"), lax.axis_index("c"))


def _put_own(buf, own, slot):
    return lax.dynamic_update_index_in_dim(buf, own, slot, 0)


def _hbm_call(body, name, ins, out_shapes, n_sems):
    anyspec = pl.BlockSpec(memory_space=pl.ANY)
    return pl.pallas_call(
        body, name=name, out_shape=out_shapes,
        in_specs=[anyspec] * len(ins), out_specs=[anyspec] * len(out_shapes),
        scratch_shapes=[pltpu.SemaphoreType.DMA(s) for s in n_sems],
    )(*ins)


def _all_gather_many(xs, name):
    n = len(xs)

    def body(*refs):
        x_refs, o_refs = refs[:n], refs[n:2 * n]
        send_sems, recv_sems = refs[2 * n:]
        xi, yi, ci = lax.axis_index("x"), lax.axis_index("y"), lax.axis_index("c")
        me, sibling = (xi, yi, ci), (xi, yi, 1 - ci)
        chips = [(1 - xi, yi), (xi, 1 - yi), (1 - xi, 1 - yi)]

        def copy(i, k, block, to, src=None):
            dst = o_refs[i].at[_slot(*block)]
            return pltpu.make_async_remote_copy(
                src_ref=dst if src is None else src, dst_ref=dst,
                send_sem=send_sems.at[i, k], recv_sem=recv_sems.at[i, k],
                device_id=to, device_id_type=pl.DeviceIdType.MESH)

        sends = []
        for i in range(n):
            sends += [copy(i, 1 + j, me, (*chip, ci), src=x_refs[i]) for j, chip in enumerate(chips)]
            sends.append(copy(i, 0, me, sibling, src=x_refs[i]))
        for cp in sends:
            cp.start()
        for j, chip in enumerate(chips):
            for i in range(n):
                copy(i, 1 + j, (*chip, ci), me).wait_recv()
                passed = copy(i, 4 + j, (*chip, ci), sibling)
                passed.start()
                sends.append(passed)
        for i in range(n):
            copy(i, 0, sibling, me).wait_recv()
            for j, chip in enumerate(chips):
                copy(i, 4 + j, (*chip, 1 - ci), me).wait_recv()
        for cp in sends:
            cp.wait_send()

    outs = _hbm_call(body, name, xs, [jax.ShapeDtypeStruct((NDEV,) + x.shape, x.dtype) for x in xs],
                     [(n, 7), (n, 7)])
    return [_put_own(o, x[None], _my_slot()) for o, x in zip(outs, xs)]


def _swap_sibling(gs, name):
    n = len(gs)

    def body(*refs):
        g_refs, o_refs = refs[:n], refs[n:2 * n]
        send_sems, recv_sems = refs[2 * n:]
        sibling = (lax.axis_index("x"), lax.axis_index("y"), 1 - lax.axis_index("c"))
        copies = [pltpu.make_async_remote_copy(
            src_ref=g_refs[i], dst_ref=o_refs[i], send_sem=send_sems.at[i], recv_sem=recv_sems.at[i],
            device_id=sibling, device_id_type=pl.DeviceIdType.MESH) for i in range(n)]
        for cp in copies:
            cp.start()
        for cp in copies:
            cp.wait_recv()
        for cp in copies:
            cp.wait_send()

    return _hbm_call(body, name, gs, [jax.ShapeDtypeStruct(g.shape, g.dtype) for g in gs], [(n,), (n,)])


def _exchange_chips(ps, name):
    n = len(ps)

    def body(*refs):
        p_refs, o_refs = refs[:n], refs[n:2 * n]
        send_sems, recv_sems = refs[2 * n:]
        xi, yi, ci = lax.axis_index("x"), lax.axis_index("y"), lax.axis_index("c")
        q_me = 2 * xi + yi
        sends, recvs = [], []
        for k in range(1, 4):
            px = 1 - xi if k & 2 else xi
            py = 1 - yi if k & 1 else yi
            q_peer = 2 * px + py
            for i in range(n):
                sends.append(pltpu.make_async_remote_copy(
                    src_ref=p_refs[i].at[q_peer], dst_ref=o_refs[i].at[q_me],
                    send_sem=send_sems.at[i, k - 1], recv_sem=recv_sems.at[i, k - 1],
                    device_id=(px, py, ci), device_id_type=pl.DeviceIdType.MESH))
                recvs.append(pltpu.make_async_remote_copy(
                    src_ref=p_refs[i].at[q_peer], dst_ref=o_refs[i].at[q_peer],
                    send_sem=send_sems.at[i, k - 1], recv_sem=recv_sems.at[i, k - 1],
                    device_id=(px, py, ci), device_id_type=pl.DeviceIdType.MESH))
        for cp in sends:
            cp.start()
        for cp in recvs:
            cp.wait_recv()
        for cp in sends:
            cp.wait_send()

    outs = _hbm_call(body, name, ps, [jax.ShapeDtypeStruct(p.shape, p.dtype) for p in ps], [(n, 3), (n, 3)])
    q_me = 2 * lax.axis_index("x") + lax.axis_index("y")
    return [_put_own(o, lax.dynamic_index_in_dim(p, q_me, 0, keepdims=True), q_me) for o, p in zip(outs, ps)]


def _peers(xi, yi, ci):
    return [(1 - xi if k & 4 else xi, 1 - yi if k & 2 else yi, 1 - ci if k & 1 else ci) for k in range(1, 8)]


def _direct_copy(src_refs, land_refs, send_sems, recv_sems, i, k, peer, my_slot, gather):
    src = src_refs[i] if gather else src_refs[i].at[_slot(*peer)]
    return pltpu.make_async_remote_copy(
        src_ref=src, dst_ref=land_refs[i].at[my_slot], send_sem=send_sems.at[7 * i + k], recv_sem=recv_sems.at[7 * i + k],
        device_id=peer, device_id_type=pl.DeviceIdType.MESH)


def _exchange_start(srcs, gather, name):
    n = len(srcs)
    lands = [lax.empty(((NDEV,) + s.shape) if gather else s.shape, s.dtype) for s in srcs]

    def body(*refs):
        s_refs, l_refs = refs[:n], refs[n:2 * n]
        send_sems, recv_sems = refs[2 * n], refs[2 * n + 1]
        token = refs[2 * n + 2 + 2 * n]
        xi, yi, ci = lax.axis_index("x"), lax.axis_index("y"), lax.axis_index("c")
        my_slot = _slot(xi, yi, ci)
        for k, peer in enumerate(_peers(xi, yi, ci)):
            for i in range(n):
                _direct_copy(s_refs, l_refs, send_sems, recv_sems, i, k, peer, my_slot, gather).start()
        token[...] = jnp.zeros_like(token)

    hbm = pl.BlockSpec(memory_space=pltpu.HBM)
    sem = pl.BlockSpec(memory_space=pltpu.SEMAPHORE)
    both = list(srcs) + lands
    return pl.pallas_call(
        body, name=name,
        out_shape=(pltpu.SemaphoreType.DMA((7 * n,)), pltpu.SemaphoreType.DMA((7 * n,)),
                   *[pltpu.HBM(t.shape, t.dtype) for t in both], jax.ShapeDtypeStruct((8, 128), F32)),
        in_specs=[hbm] * (2 * n),
        out_specs=(sem, sem, *[hbm] * (2 * n), pl.BlockSpec(memory_space=pltpu.VMEM)),
        input_output_aliases={i: 2 + i for i in range(2 * n)},
        compiler_params=pltpu.CompilerParams(has_side_effects=pltpu.SideEffectType.DATAFLOW_SIDE_EFFECTING),
    )(*[pltpu.with_memory_space_constraint(t, pltpu.HBM) for t in both])


def _exchange_wait(started, gather, after, name):
    send_sems, recv_sems, *thru, _ = started
    n = len(thru) // 2

    def body(*refs):
        s_refs, l_refs = refs[:n], refs[n:2 * n]
        send_sems, recv_sems = refs[2 * n], refs[2 * n + 1]
        xi, yi, ci = lax.axis_index("x"), lax.axis_index("y"), lax.axis_index("c")
        my_slot = _slot(xi, yi, ci)
        for k, peer in enumerate(_peers(xi, yi, ci)):
            for i in range(n):
                _direct_copy(s_refs, l_refs, send_sems, recv_sems, i, k, peer, my_slot, gather).wait_send()
                _direct_copy(s_refs, l_refs, send_sems, recv_sems, i, k, peer, _slot(*peer), gather).wait_recv()

    hbm = pl.BlockSpec(memory_space=pltpu.HBM)
    sem = pl.BlockSpec(memory_space=pltpu.SEMAPHORE)
    outs = pl.pallas_call(
        body, name=name,
        out_shape=tuple(pltpu.HBM(t.shape, t.dtype) for t in thru),
        in_specs=[hbm] * (2 * n) + [sem, sem, pl.BlockSpec(memory_space=pl.ANY)],
        out_specs=tuple([hbm] * (2 * n)),
        input_output_aliases={i: i for i in range(2 * n)},
        compiler_params=pltpu.CompilerParams(has_side_effects=pltpu.SideEffectType.DATAFLOW_SIDE_EFFECTING),
    )(*thru, send_sems, recv_sems, after)
    slot = _my_slot()
    own = [s[None] if gather else lax.dynamic_index_in_dim(s, slot, 0, keepdims=True) for s in outs[:n]]
    return [_put_own(land, o, slot) for land, o in zip(outs[n:], own)]


def _mm(name, a, b, tb=False, out=(F32,), epi=None, extras=(), tm=1024, tn=512, tk_cap=2048):
    M, K = a.shape
    N = b.shape[0] if tb else b.shape[1]
    tm, tn = min(tm, M), min(tn, N)
    tk = max(t for t in range(128, min(K, tk_cap) + 1, 128) if K % t == 0)
    assert M % tm == 0 and N % tn == 0 and K % tk == 0, (name, M, N, K)
    nk = K // tk
    ne, no = len(extras), len(out)
    dims = (((1,), (1 if tb else 0,)), ((), ()))

    def kern(*refs):
        a_ref, b_ref = refs[:2]
        e_refs = refs[2:2 + ne]
        o_refs = refs[2 + ne:2 + ne + no]

        def finish(acc):
            outs = epi(acc, *[e[...] for e in e_refs]) if epi is not None else (acc,)
            for o_ref, o in zip(o_refs, outs):
                o_ref[...] = o.astype(o_ref.dtype)

        part = lax.dot_general(a_ref[...], b_ref[...], dims, preferred_element_type=F32)
        if nk == 1:
            finish(part)
            return
        acc_ref = refs[-1]
        k = pl.program_id(2)

        @pl.when(k == 0)
        def _():
            acc_ref[...] = part

        @pl.when(k > 0)
        def _():
            acc_ref[...] += part

        @pl.when(k == nk - 1)
        def _():
            finish(acc_ref[...])

    b_spec = (pl.BlockSpec((tn, tk), lambda i, j, k: (j, k)) if tb
              else pl.BlockSpec((tk, tn), lambda i, j, k: (k, j)))
    tile = pl.BlockSpec((tm, tn), lambda i, j, k: (i, j))
    res = pl.pallas_call(
        kern, name=name, grid=(M // tm, N // tn, nk),
        in_specs=[pl.BlockSpec((tm, tk), lambda i, j, k: (i, k)), b_spec] + [tile] * ne,
        out_specs=[tile] * no,
        out_shape=[jax.ShapeDtypeStruct((M, N), dt) for dt in out],
        scratch_shapes=[pltpu.VMEM((tm, tn), F32)] if nk > 1 else [],
        compiler_params=_cp(("parallel", "parallel", "arbitrary")),
    )(a, b, *extras)
    return res[0] if no == 1 else res


HALO = 8


def _rows(name, body, rows, params, out_rows, out_accs=(), tm=256):
    views = [r if isinstance(r, tuple) else (r, r.shape[1], 0) for r in rows]
    n = views[0][0].shape[0]
    assert n % tm == 0 and tm % HALO == 0
    nr, npar, nor, noa = len(views), len(params), len(out_rows), len(out_accs)

    def row_spec(width, cb, halo=None):
        per, last = tm // HALO, n // HALO - 1
        if halo == "prev":
            return pl.BlockSpec((HALO, width), lambda i: (jnp.maximum(i * per - 1, 0), cb))
        if halo == "next":
            return pl.BlockSpec((HALO, width), lambda i: (jnp.minimum((i + 1) * per, last), cb))
        return pl.BlockSpec((tm, width), lambda i: (i, cb))

    def kern(*refs):
        r_refs = refs[:nr]
        p_refs = refs[nr:nr + npar]
        o_refs = refs[nr + npar:nr + npar + nor]
        a_refs = refs[nr + npar + nor:]
        outs, accs = body([r[...] for r in r_refs], [p[...] for p in p_refs])
        assert len(outs) == nor and len(accs) == noa, (name, len(outs), len(accs))
        for o_ref, o in zip(o_refs, outs):
            o_ref[...] = o.astype(o_ref.dtype)
        if noa:
            @pl.when(pl.program_id(0) == 0)
            def _():
                for a_ref in a_refs:
                    a_ref[...] = jnp.zeros_like(a_ref)

            for a_ref, a in zip(a_refs, accs):
                a_ref[...] += a.astype(F32)

    def whole(shape):
        nd = len(shape)
        return pl.BlockSpec(tuple(shape), lambda i, nd=nd: (0,) * nd)

    in_specs = [row_spec(*v[1:]) for v in views]
    in_specs += [whole(p.shape) for p in params]
    out_specs = [pl.BlockSpec((tm, c), lambda i: (i, 0)) for c, _ in out_rows]
    out_specs += [whole(s) for s in out_accs]
    out_shape = [jax.ShapeDtypeStruct((n, c), dt) for c, dt in out_rows]
    out_shape += [jax.ShapeDtypeStruct(tuple(s), F32) for s in out_accs]
    res = pl.pallas_call(
        kern, name=name, grid=(n // tm,), in_specs=in_specs, out_specs=out_specs,
        out_shape=out_shape, compiler_params=_cp(("arbitrary",)),
    )(*[v[0] for v in views], *params)
    return res[:nor], res[nor:]


def _shift_down(x, prev, k):
    head = jnp.where(pl.program_id(0) == 0, 0.0, pltpu.roll(prev, k, axis=0))
    row = lax.broadcasted_iota(jnp.int32, x.shape, 0)
    return jnp.where(row < k, jnp.tile(head, (x.shape[0] // HALO, 1)), pltpu.roll(x, k, axis=0))


def _shift_up(x, nxt, k):
    n = x.shape[0]
    tail = jnp.where(pl.program_id(0) == pl.num_programs(0) - 1, 0.0, pltpu.roll(nxt, HALO - k, axis=0))
    row = lax.broadcasted_iota(jnp.int32, x.shape, 0)
    return jnp.where(row >= n - k, jnp.tile(tail, (n // HALO, 1)), pltpu.roll(x, n - k, axis=0))


@jax.custom_vjp
def _headsum(x, e):
    return sum(jnp.dot(p, e, preferred_element_type=F32) for p in _split3(x))


_headsum.defvjp(lambda x, e: (_headsum(x, e), e), lambda e, ct: (_headsum(ct, e), None))


def _softplus(z):
    return jnp.maximum(z, 0.0) + jnp.log(1.0 + jnp.exp(jnp.minimum(z, -z)))


def _post_ln(x, y, g, lng, lnb):
    z = ALPHA * x + (1.0 + g) * y
    mu = jnp.mean(z, axis=-1, keepdims=True)
    zc = z - mu
    var = jnp.mean(zc * zc, axis=-1, keepdims=True)
    return zc * lax.rsqrt(var + LN_EPS) * lng + lnb


def _post_ln_mod(x, y, g, lng, lnb, scn, shn):
    xn = _post_ln(x, y, g, lng, lnb)
    return xn, xn * (1.0 + scn) + shn


def _pre_core(E, r_, k_, v_, wd_, ad_, gd_, r1, k1, v1, wd1, ad1, gd1, h, bg, cg, h1, cg1, h2, cg2,
              mu_r, mu_k, mu_v, mu_wd, mu_ad, mu_gd, w0, w_up, a0, a_up, g_up, k_k, k_a,
              cw0, cw1, cw2):
    def mix(x, x1, mu):
        return x + mu * (x1 - x)

    r, k, v = mix(r_, r1, mu_r), mix(k_, k1, mu_k), mix(v_, v1, mu_v)
    wd, ad, gd = mix(wd_, wd1, mu_wd), mix(ad_, ad1, mu_ad), mix(gd_, gd1, mu_gd)
    logw = -_softplus(-(w0 + jnp.dot(jnp.tanh(wd), w_up, preferred_element_type=F32))) - 0.5
    decay = jnp.exp(-jnp.exp(logw))
    iclr = jax.nn.sigmoid(a0 + jnp.dot(ad, a_up, preferred_element_type=F32))
    gate = jnp.dot(jax.nn.sigmoid(gd), g_up, preferred_element_type=F32)
    kk0 = k * k_k
    nrm = jnp.sqrt(_headsum(kk0 * kk0, E))
    kk = kk0 / jnp.maximum(nrm, 1e-12)
    kh = k * (1.0 + (iclr - 1.0) * k_a)
    yb = bg * (cw2 * (cg * h) + cw1 * (cg1 * h1) + cw0 * (cg2 * h2))
    return r, decay, kh, v, -kk, kk * iclr, gate, yb


def _post_core(E, y, r, kh, v, gate, lnx_g, lnx_b, rk):
    def seg(t):
        return _headsum(t, E)

    mean = seg(y) * (1.0 / HD)
    yc = y - mean
    var = seg(yc * yc) * (1.0 / HD)
    gn = yc * lax.rsqrt(var + GN_EPS) * lnx_g + lnx_b
    bonus = seg(r * kh * rk) * v
    return (gn + bonus) * gate


def _merge_core(o0, o1, o2, l0, l1, l2):
    m = jnp.maximum(jnp.maximum(l0, l1), l2)
    e0, e1, e2 = jnp.exp(l0 - m), jnp.exp(l1 - m), jnp.exp(l2 - m)
    den = e0 + e1 + e2
    return (e0 * o0 + e1 * o1 + e2 * o2) / den


CHUNK = 128
HALF = 64
HP = HEADS // 2
LW = 2 * HD
NCHUNK = T // CHUNK


def _split3(x):
    hi = x.astype(BF16)
    r1 = x - hi.astype(F32)
    mid = r1.astype(BF16)
    return hi, mid, (r1 - mid.astype(F32)).astype(BF16)


def _cols3(x, name):
    def kern(x_ref, o_ref):
        xt = x_ref[...].T
        left = lax.broadcasted_iota(jnp.int32, (HD, CHUNK), 1) < HALF
        for p in range(HP):
            a, b = xt[p * LW:p * LW + HD], xt[p * LW + HD:(p + 1) * LW]
            halves = [jnp.where(left, a, pltpu.roll(b, HALF, axis=1)), jnp.where(left, pltpu.roll(a, HALF, axis=1), b)]
            for h, tile in enumerate(halves):
                for j, part in enumerate(_split3(tile)):
                    o_ref[p, :, (3 * h + j) * LW:(3 * h + j + 1) * LW] = part

    return pl.pallas_call(
        kern, name=name, grid=(NCHUNK,),
        in_specs=[pl.BlockSpec((CHUNK, RW), lambda c: (c, 0))],
        out_specs=pl.BlockSpec((HP, HD, 6 * CHUNK), lambda c: (0, 0, c)),
        out_shape=jax.ShapeDtypeStruct((HP, HD, 6 * T), BF16),
        compiler_params=_cp(("parallel",)),
    )(x)


def _pick_codes():
    row = lax.broadcasted_iota(jnp.int32, (6 * HALF, LW), 0)
    col = lax.broadcasted_iota(jnp.int32, (6 * HALF, LW), 1)
    same = ((row & (LW - 1)) >= HALF) == (col >= HD)
    return jnp.where(same, row & (HALF - 1), -1).astype(BF16)


def _column(block_ref, codes, half, i):
    pick = jnp.where(codes == i.astype(BF16), jnp.ones((), BF16), jnp.zeros((), BF16))
    block = block_ref[:, :, half * 6 * HALF:(half + 1) * 6 * HALF].reshape(HP * HD, 6 * HALF)
    return jnp.dot(block, pick, preferred_element_type=F32)


def _halfsums(x, row, left1):
    row_l = jnp.where(left1, row, 0.0)
    return (jnp.sum(x * row_l, axis=1, keepdims=True), jnp.sum(x * (row - row_l), axis=1, keepdims=True))


def _pair_rows(row):
    return [row[:, p * LW:(p + 1) * LW] for p in range(HP)]


def _store_columns(ref, p, t_mask, cols):
    for j, col in enumerate(cols):
        pltpu.store(ref.at[pl.ds(2 * p + j, 1)], jnp.broadcast_to(col[None], (1, HD, CHUNK)), mask=t_mask[None])


def _columns_to_rows(cols_ref, rows_ref):
    for p in range(HP):
        rows_ref[:, p * LW:(p + 1) * LW] = cols_ref[2 * p:2 * p + 2].reshape(LW, CHUNK).T


def _scan_fwd(r, w, k, a, b, v3):
    def kern(r_ref, w_ref, k_ref, a_ref, b_ref, v_ref, y_ref, ck_ref, s_ref, vb_ref, yc_ref):
        @pl.when(pl.program_id(0) == 0)
        def _():
            s_ref[...] = jnp.zeros_like(s_ref)

        lane = lax.broadcasted_iota(jnp.int32, (HD, CHUNK), 1)
        left = lane < HD
        left1 = lax.broadcasted_iota(jnp.int32, (1, LW), 1) < HD
        codes = _pick_codes()

        for half in range(CHUNK // HALF):
            ck_ref[half] = s_ref[...]
            vb_ref[...] = _column(v_ref, codes, half, jnp.int32(0))

            def step(i, carry):
                t = half * HALF + i
                row = lambda ref: _pair_rows(ref[pl.ds(t, 1), :])
                S = [s_ref[p] for p in range(HP)]
                sa = [jnp.where(left, *_halfsums(s, a, left1)) for s, a in zip(S, row(a_ref))]
                S = [s * w + c * b + vb_ref[pl.ds(p * HD, HD), :] * k
                     for p, (s, w, c, b, k) in enumerate(zip(S, row(w_ref), sa, row(b_ref), row(k_ref)))]
                for p, s in enumerate(S):
                    s_ref[p] = s
                for p, (s, r) in enumerate(zip(S, row(r_ref))):
                    _store_columns(yc_ref, p, lane == t, _halfsums(s, r, left1))
                vb_ref[...] = _column(v_ref, codes, half, i + 1)
                return carry

            lax.fori_loop(0, HALF, step, 0, unroll=8)
        _columns_to_rows(yc_ref, y_ref)

    rowblk = pl.BlockSpec((CHUNK, RW), lambda c: (c, 0))
    return pl.pallas_call(
        kern, name="rwkv_scan_fwd", grid=(NCHUNK,),
        in_specs=[rowblk] * 5 + [pl.BlockSpec((HP, HD, 6 * CHUNK), lambda c: (0, 0, c))],
        out_specs=[rowblk, pl.BlockSpec((CHUNK // HALF, HP, HD, LW), lambda c: (c, 0, 0, 0))],
        out_shape=[jax.ShapeDtypeStruct((T, RW), F32), jax.ShapeDtypeStruct((T // HALF, HP, HD, LW), F32)],
        scratch_shapes=[pltpu.VMEM((HP, HD, LW), F32), pltpu.VMEM((HP * HD, LW), F32),
                        pltpu.VMEM((HEADS, HD, CHUNK), F32)],
        compiler_params=_cp(("arbitrary",)),
    )(r, w, k, a, b, v3)


def _scan_bwd(r, w, k, a, b, v3, dy3, ck):
    NC = T // CHUNK

    def kern(r_ref, w_ref, k_ref, a_ref, b_ref, v_ref, dy_ref, ck_ref,
             dr_ref, dw_ref, dk_ref, da_ref, db_ref, dv_ref, ds_ref, sb_ref, vb_ref, sa_ref, dyb_ref, dvc_ref):
        @pl.when(pl.program_id(0) == 0)
        def _():
            ds_ref[...] = jnp.zeros_like(ds_ref)

        lane = lax.broadcasted_iota(jnp.int32, (HD, CHUNK), 1)
        left = lane < HD
        left1 = lax.broadcasted_iota(jnp.int32, (1, LW), 1) < HD
        codes = _pick_codes()

        def rowsum(x):
            return jnp.sum(x, axis=0, keepdims=True)

        for half in reversed(range(CHUNK // HALF)):
            base = half * HALF
            sb_ref[0] = ck_ref[half]

            vb_ref[0] = _column(v_ref, codes, half, jnp.int32(0))

            def replay(i, carry):
                t = base + i
                row = lambda ref: _pair_rows(ref[pl.ds(t, 1), :])
                S = [sb_ref[i, p] for p in range(HP)]
                sa = [jnp.where(left, *_halfsums(s, a, left1)) for s, a in zip(S, row(a_ref))]
                for p, (s, w, c, b, k) in enumerate(zip(S, row(w_ref), sa, row(b_ref), row(k_ref))):
                    sb_ref[i + 1, p] = s * w + c * b + vb_ref[i, pl.ds(p * HD, HD), :] * k
                    sa_ref[i, p] = c
                vb_ref[i + 1] = _column(v_ref, codes, half, i + 1)
                return carry

            lax.fori_loop(0, HALF, replay, 0, unroll=4)
            dyb_ref[...] = _column(dy_ref, codes, half, jnp.int32(HALF - 1))

            def back(ii, carry):
                i = HALF - 1 - ii
                t = base + i
                row = lambda ref: _pair_rows(ref[pl.ds(t, 1), :])
                a_r, b_r, k_r, w_r, r_r = row(a_ref), row(b_ref), row(k_ref), row(w_ref), row(r_ref)
                dys = [dyb_ref[pl.ds(p * HD, HD), :] for p in range(HP)]
                dyb_ref[...] = _column(dy_ref, codes, half, jnp.maximum(i - 1, 0))
                dr, dw, db, dk, da = [], [], [], [], []
                for p in range(HP):
                    Sp, dy = sb_ref[i, p], dys[p]
                    dS = ds_ref[p] + dy * r_r[p]
                    dr.append(rowsum(sb_ref[i + 1, p] * dy))
                    dw.append(rowsum(dS * Sp))
                    db.append(rowsum(dS * sa_ref[i, p]))
                    dk.append(rowsum(dS * vb_ref[i, pl.ds(p * HD, HD), :]))
                    dsa = jnp.where(left, *_halfsums(dS, b_r[p], left1))
                    _store_columns(dvc_ref, p, lane == t, _halfsums(dS, k_r[p], left1))
                    da.append(rowsum(Sp * dsa))
                    ds_ref[p] = dS * w_r[p] + dsa * a_r[p]
                for ref, pieces in ((dr_ref, dr), (dw_ref, dw), (db_ref, db), (dk_ref, dk), (da_ref, da)):
                    ref[pl.ds(t, 1), :] = jnp.concatenate(pieces, axis=1)
                return carry

            lax.fori_loop(0, HALF, back, 0, unroll=4)
        _columns_to_rows(dvc_ref, dv_ref)

    rowblk = pl.BlockSpec((CHUNK, RW), lambda c: (NC - 1 - c, 0))
    col3blk = pl.BlockSpec((HP, HD, 6 * CHUNK), lambda c: (0, 0, NC - 1 - c))
    rowshape = jax.ShapeDtypeStruct((T, RW), F32)
    return pl.pallas_call(
        kern, name="rwkv_scan_bwd", grid=(NC,),
        in_specs=[rowblk] * 5 + [col3blk, col3blk,
                                 pl.BlockSpec((CHUNK // HALF, HP, HD, LW), lambda c: (NC - 1 - c, 0, 0, 0))],
        out_specs=[rowblk] * 6, out_shape=[rowshape] * 6,
        scratch_shapes=[pltpu.VMEM((HP, HD, LW), F32), pltpu.VMEM((HALF + 1, HP, HD, LW), F32),
                        pltpu.VMEM((HALF + 1, HP * HD, LW), F32), pltpu.VMEM((HALF, HP, HD, LW), F32),
                        pltpu.VMEM((HP * HD, LW), F32), pltpu.VMEM((HEADS, HD, CHUNK), F32)],
        compiler_params=_cp(("arbitrary",)),
    )(r, w, k, a, b, v3, dy3, ck)


NT = (((1,), (1,)), ((), ()))
TN = (((0,), (0,)), ((), ()))
SCALE = HD ** -0.5
QKV_G = 3 * RW


def _attn_setup(g):
    dil = DILS[g]
    qkv = [pl.BlockSpec((T, LW), lambda hp, c=(g * QKV_G + s * RW) // LW: (0, c + hp)) for s in range(3)]
    tile = pl.BlockSpec((T, LW), lambda hp: (0, hp))
    bias = pl.BlockSpec((2, BLK, 2 * BLK), lambda hp: (hp, 0, 0))

    def blocks():
        for r in range(dil):
            for n in range(T // dil // BLK):
                rows = pl.ds(n * BLK * dil + r, BLK, stride=dil)
                keys = pl.ds((n - 1) * BLK * dil + r, 2 * BLK, stride=dil) if n else rows
                yield n, rows, keys

    return qkv, tile, bias, blocks


def _band(n):
    qi = lax.broadcasted_iota(jnp.int32, (BLK, 2 * BLK), 0)
    ki = lax.broadcasted_iota(jnp.int32, (BLK, 2 * BLK), 1)
    band = (ki >= qi) & (ki <= qi + BLK)
    return band if n else band[:, BLK:]


def _head_masks():
    lane = lax.broadcasted_iota(jnp.int32, (BLK, LW), 1)
    return lane < HD, [(lane < HD).astype(BF16), (lane >= HD).astype(BF16)]


def _attn_fwd(pq, bias, g):
    qkv, tile, bias_spec, blocks = _attn_setup(g)

    def kern(q_ref, k_ref, v_ref, b_ref, o_ref, l_ref):
        left, masks = _head_masks()
        for n, rows, keys in blocks():
            qb, kc, vc = q_ref[rows, :].astype(BF16), k_ref[keys, :].astype(BF16), v_ref[keys, :].astype(BF16)
            valid = _band(n)
            o, lse = [], []
            for j in range(2):
                bias_j = b_ref[j] if n else b_ref[j][:, BLK:]
                s = lax.dot_general(qb * masks[j], kc, NT, preferred_element_type=F32) * SCALE + bias_j
                s = jnp.where(valid, s, -jnp.inf)
                m = jnp.max(s, axis=1, keepdims=True)
                e = jnp.exp(s - m)
                den = jnp.sum(e, axis=1, keepdims=True)
                o.append(jnp.dot((e / den).astype(BF16), vc, preferred_element_type=F32))
                lse.append(m + jnp.log(den))
            o_ref[rows, :] = jnp.where(left, o[0], o[1])
            l_ref[rows, :] = jnp.where(left, lse[0], lse[1])

    shape = jax.ShapeDtypeStruct((T, RW), F32)
    return pl.pallas_call(
        kern, name=f"attn_fwd_{g}", grid=(HP,),
        in_specs=qkv + [bias_spec], out_specs=[tile, tile], out_shape=[shape, shape],
        compiler_params=_cp(("parallel",)),
    )(pq, pq, pq, bias)


def _attn_bwd(pq, bias, do, o, lse, dlse, g):
    qkv, tile, bias_spec, blocks = _attn_setup(g)

    def kern(q_ref, k_ref, v_ref, b_ref, do_ref, o_ref, l_ref, dl_ref, dq_ref, dk_ref, dv_ref, db_ref):
        left, masks = _head_masks()
        lane = lax.broadcasted_iota(jnp.int32, (BLK, LW), 1)
        dk_ref[...] = jnp.zeros_like(dk_ref)
        dv_ref[...] = jnp.zeros_like(dv_ref)
        db_ref[...] = jnp.zeros_like(db_ref)

        def column(tile_, j):
            return jnp.sum(jnp.where(lane == j * HD, tile_, 0.0), axis=1, keepdims=True)

        for n, rows, keys in blocks():
            qb, kc, vc = q_ref[rows, :].astype(BF16), k_ref[keys, :].astype(BF16), v_ref[keys, :].astype(BF16)
            dof, valid = do_ref[rows, :], _band(n)
            dob, prod = dof.astype(BF16), dof * o_ref[rows, :]
            dq = []
            for j in range(2):
                bias_j = b_ref[j] if n else b_ref[j][:, BLK:]
                delta = jnp.sum(prod * masks[j].astype(F32), axis=1, keepdims=True)
                qm, dom = qb * masks[j], dob * masks[j]
                s = lax.dot_general(qm, kc, NT, preferred_element_type=F32) * SCALE + bias_j
                p = jnp.where(valid, jnp.exp(s - column(l_ref[rows, :], j)), 0.0)
                dp = lax.dot_general(dom, vc, NT, preferred_element_type=F32)
                ds = p * (dp + (column(dl_ref[rows, :], j) - delta))
                if n:
                    db_ref[j] += ds
                else:
                    db_ref[j, :, BLK:] += ds
                dsb = (ds * SCALE).astype(BF16)
                dq.append(jnp.dot(dsb, kc, preferred_element_type=F32))
                dk_ref[keys, :] += lax.dot_general(dsb, qm, TN, preferred_element_type=F32)
                dv_ref[keys, :] += lax.dot_general(p.astype(BF16), dom, TN, preferred_element_type=F32)
            dq_ref[rows, :] = jnp.where(left, dq[0], dq[1])

    shape = jax.ShapeDtypeStruct((T, RW), F32)
    return pl.pallas_call(
        kern, name=f"attn_bwd_{g}", grid=(HP,),
        in_specs=qkv + [bias_spec] + [tile] * 4, out_specs=[tile] * 3 + [bias_spec],
        out_shape=[shape] * 3 + [jax.ShapeDtypeStruct((HEADS, BLK, 2 * BLK), F32)],
        compiler_params=_cp(("parallel",)),
    )(pq, pq, pq, bias, do, o, lse, dlse)


NBUCKET = 32
NPAIR = BLK * 2 * BLK


def _relbias_table(rbT, onehotT):
    def kern(rb_ref, oh_ref, out_ref):
        out_ref[0] = sum(jnp.dot(p, oh_ref[0], preferred_element_type=F32) for p in _split3(rb_ref[0]))

    return pl.pallas_call(
        kern, name="relbias_table", grid=(3,),
        in_specs=[pl.BlockSpec((1, HEADS, NBUCKET), lambda g: (g, 0, 0)),
                  pl.BlockSpec((1, NBUCKET, NPAIR), lambda g: (g, 0, 0))],
        out_specs=pl.BlockSpec((1, HEADS, NPAIR), lambda g: (g, 0, 0)),
        out_shape=jax.ShapeDtypeStruct((3, HEADS, NPAIR), F32),
        compiler_params=_cp(("parallel",)),
    )(rbT, onehotT)


def _relbias_grad(db, onehotT):
    nt = (((1,), (1,)), ((), ()))

    def kern(db_ref, oh_ref, out_ref):
        hi, mid, _ = _split3(db_ref[0])
        out_ref[0] = (lax.dot_general(hi, oh_ref[0], nt, preferred_element_type=F32)
                      + lax.dot_general(mid, oh_ref[0], nt, preferred_element_type=F32))

    return pl.pallas_call(
        kern, name="relbias_grad", grid=(3,),
        in_specs=[pl.BlockSpec((1, HEADS, NPAIR), lambda g: (g, 0, 0)),
                  pl.BlockSpec((1, NBUCKET, NPAIR), lambda g: (g, 0, 0))],
        out_specs=pl.BlockSpec((1, HEADS, NBUCKET), lambda g: (g, 0, 0)),
        out_shape=jax.ShapeDtypeStruct((3, HEADS, NBUCKET), F32),
        compiler_params=_cp(("parallel",)),
    )(db, onehotT)


def _adamw(w, g, m, v):
    m2 = ADAM_B1 * m + (1.0 - ADAM_B1) * g
    v2 = ADAM_B2 * v + (1.0 - ADAM_B2) * (g * g)
    m_hat = m2 / (1.0 - ADAM_B1 ** ADAM_STEP)
    v_hat = v2 / (1.0 - ADAM_B2 ** ADAM_STEP)
    return -ADAM_LR * (m_hat / (jnp.sqrt(v_hat) + ADAM_EPS) + ADAM_WD * w), m2, v2


def _ada_mod(c_all, ada_w, ada_b_loc):
    def kern(c_ref, w_ref, b_ref, o_ref):
        c = c_ref[...]
        cond = c * jax.nn.sigmoid(c)
        o_ref[0] = jnp.dot(cond, w_ref[0], precision=HI, preferred_element_type=F32) + b_ref[0]

    ncol = ada_w.shape[2]
    return pl.pallas_call(
        kern, name="ada_mod", grid=(2,),
        in_specs=[pl.BlockSpec((NDEV, D), lambda i: (0, 0)),
                  pl.BlockSpec((1, D, ncol), lambda i: (i, 0, 0)),
                  pl.BlockSpec((1, 1, ncol), lambda i: (i, 0, 0))],
        out_specs=pl.BlockSpec((1, NDEV, ncol), lambda i: (i, 0, 0)),
        out_shape=jax.ShapeDtypeStruct((2, NDEV, ncol), F32),
        compiler_params=_cp(("parallel",)),
    )(c_all, ada_w, ada_b_loc.reshape(2, 1, ncol))


def _ada_grad_adamw(cT_all, dmod_loc, w, m, v):
    ncol = w.shape[2]
    tr = 256

    def kern(c_ref, d_ref, w_ref, m_ref, v_ref, g_ref, dl_ref, m2_ref, v2_ref):
        c = c_ref[...]
        cond = c * jax.nn.sigmoid(c)
        g = jnp.dot(cond, d_ref[0], precision=HI, preferred_element_type=F32)
        dl, m2, v2 = _adamw(w_ref[0], g, m_ref[0], v_ref[0])
        g_ref[0], dl_ref[0], m2_ref[0], v2_ref[0] = g, dl, m2, v2

    big = pl.BlockSpec((1, tr, ncol), lambda i, j: (i, j, 0))
    shp = jax.ShapeDtypeStruct(w.shape, F32)
    return pl.pallas_call(
        kern, name="ada_grad_adamw", grid=(2, D // tr),
        in_specs=[pl.BlockSpec((tr, NDEV), lambda i, j: (j, 0)),
                  pl.BlockSpec((1, NDEV, ncol), lambda i, j: (i, 0, 0)), big, big, big],
        out_specs=[big] * 4, out_shape=[shp] * 4,
        compiler_params=_cp(("parallel", "parallel")),
    )(cT_all, dmod_loc, w, m, v)


def _sum_adamw(recv, w, m, v, name, tr):
    S = recv.shape[0]
    R, C = w.shape
    assert R % tr == 0 and recv.shape[1:] == (R, C)

    def kern(r_ref, w_ref, m_ref, v_ref, g_ref, dl_ref, m2_ref, v2_ref):
        g = r_ref[0].astype(F32)
        for s in range(1, S):
            g = g + r_ref[s].astype(F32)
        dl, m2, v2 = _adamw(w_ref[...], g, m_ref[...], v_ref[...])
        g_ref[...], dl_ref[...], m2_ref[...], v2_ref[...] = g, dl, m2, v2

    flat = pl.BlockSpec((tr, C), lambda i: (i, 0))
    shp = jax.ShapeDtypeStruct((R, C), F32)
    return pl.pallas_call(
        kern, name=name, grid=(R // tr,),
        in_specs=[pl.BlockSpec((S, tr, C), lambda i: (0, i, 0)), flat, flat, flat],
        out_specs=[flat] * 4, out_shape=[shp] * 4,
        compiler_params=_cp(("parallel",)),
    )(recv, w, m, v)


def _pack(arrs, dtype, row_mult):
    flat = jnp.concatenate([a.reshape(-1).astype(dtype) for a in arrs])
    flat = jnp.pad(flat, (0, -flat.shape[0] % (128 * row_mult)))
    return flat.reshape(-1, 128)


def _pack8(arrs, dtype, row_mult):
    flat = jnp.concatenate([a.reshape(NDEV, -1).astype(dtype) for a in arrs], axis=1)
    flat = jnp.pad(flat, ((0, 0), (0, -flat.shape[1] % (128 * row_mult))))
    return flat.reshape(NDEV, -1, 128)


def _unpack(buf, shapes, lead=()):
    flat = buf.reshape(lead + (-1,))
    out, off = [], 0
    for s in shapes:
        n = math.prod(s)
        out.append(flat[..., off:off + n].reshape(lead + tuple(s)))
        off += n
    return out


def _to_chunks(full, kind):
    if kind == "col":
        x = full.reshape(full.shape[:-1] + (NDEV, full.shape[-1] // NDEV))
        return jnp.moveaxis(x, -2, 0)
    x = full.reshape(full.shape[:-2] + (NDEV, full.shape[-2] // NDEV, full.shape[-1]))
    return jnp.moveaxis(x, -3, 0)


def _from_chunks(g8, kind):
    if kind == "col":
        x = jnp.moveaxis(g8, 0, -2)
        return x.reshape(x.shape[:-2] + (x.shape[-2] * x.shape[-1],))
    x = jnp.moveaxis(g8, 0, -3)
    return x.reshape(x.shape[:-3] + (x.shape[-3] * x.shape[-2], x.shape[-1]))


def _pad_pa(x):
    z = lambda n: jnp.zeros(x.shape[:-1] + (n,), x.dtype)
    return jnp.concatenate([x[..., :1600], z(64), x[..., 1600:1664], z(64), x[..., 1664:1824], z(96)], -1)


def _unpad_pa(x):
    return jnp.concatenate([x[..., :1600], x[..., 1664:1728], x[..., 1792:1952]], -1)


def _pad_rows(x, n):
    return jnp.pad(x, ((0, n - x.shape[0]), (0, 0)))


def _bucket_tables():
    qi = jnp.arange(BLK)[:, None]
    ki = jnp.arange(2 * BLK)[None, :]
    rel = BLK + qi - ki
    tabs = []
    for dil in DILS:
        dist = jnp.clip(rel, 0, BLK) * dil
        logd = jnp.log(jnp.maximum(dist, 1).astype(F32) / 16) / math.log(2048 / 16)
        large = jnp.minimum(16 + (logd * 16).astype(jnp.int32), 31)
        tabs.append(jnp.where(dist < 16, dist, large))
    return jnp.stack(tabs)


SHARDED = (("ln_g", "col"), ("ln_b", "col"), ("ab_w_in", "col"), ("rw_w_up", "col"), ("rw_a_up", "col"),
           ("rw_g_up", "col"), ("sc_conv_w", "col"), ("ab_w_out", "row"), ("dil_w_qkv", "col"),
           ("dil_w_out", "col"), ("mlp_w1", "col"), ("mlp_w2", "row"))
FIRST = ("ab_w_in", "ab_w_out")
LATER = ("dil_w_qkv", "dil_w_out", "mlp_w1", "mlp_w2")
GATHER_BF16 = FIRST + LATER
GATHER_F32 = ("rw_w_up", "rw_a_up", "rw_g_up", "sc_conv_w", "ln_g", "ln_b")
REPLICATED = ("ada_b", "rw_mu", "rw_w0", "rw_a0", "rw_k_k", "rw_k_a", "rw_r_k", "rw_lnx_g", "rw_lnx_b", "rel_bias")
WEIGHTS = ("ada_w", "ada_b", "ln_g", "ln_b", "ab_w_in", "rw_mu", "rw_w0", "rw_w_up", "rw_a0", "rw_a_up",
           "rw_g_up", "rw_k_k", "rw_k_a", "rw_r_k", "rw_lnx_g", "rw_lnx_b", "sc_conv_w", "ab_w_out",
           "dil_w_qkv", "dil_w_out", "rel_bias", "mlp_w1", "mlp_w2")
FLAT_TILE = 512


def _local_step(x0, tgt, mod, W, P, later_weights, early_grads):
    row = lambda a: a.reshape(1, -1)
    W = dict(W)
    m6 = mod.reshape(2, 6, 1, D)
    sc = [m6[0, 1], m6[0, 4], m6[1, 1], m6[1, 4]]
    sh = [m6[0, 0], m6[0, 3], m6[1, 0], m6[1, 3]]
    gt = [m6[0, 2], m6[0, 5], m6[1, 2], m6[1, 5]]
    lng = [row(P["ln_g"][0, 0]), row(P["ln_g"][0, 1]), row(P["ln_g"][1, 0]), row(P["ln_g"][1, 1])]
    lnb = [row(P["ln_b"][0, 0]), row(P["ln_b"][0, 1]), row(P["ln_b"][1, 0]), row(P["ln_b"][1, 1])]
    E = jnp.kron(jnp.eye(HEADS, dtype=BF16), jnp.ones((HD, HD), BF16))

    def mod_body(r, p):
        return [r[0] * (1.0 + p[0]) + p[1]], []

    (u0,), _ = _rows("modulate", mod_body, [x0], [sc[0], sh[0]], [(D, BF16)])

    def post_fwd_body(r, p):
        xn, un = _post_ln_mod(r[0], r[1], *p)
        return [xn, un], []

    def post_fwd(s, x, y):
        (xn, un), _ = _rows(f"post_ln_{s}", post_fwd_body, [x, y],
                            [gt[s], lng[s], lnb[s], sc[s + 1], sh[s + 1]], [(D, F32), (D, BF16)])
        return xn, un

    def relu2(acc):
        a = jnp.maximum(acc, 0.0)
        return acc, a * a

    def relu2_bwd(acc, h):
        return (acc * (2.0 * jnp.maximum(h, 0.0)),)

    p = _mm("ab_in", u0, W["ab_w_in"])
    mu = _pad_pa(P["rw_mu"])
    mu_parts = [mu[:, :512], mu[:, 512:1024], mu[:, 1024:1536], mu[:, 1536:1664], mu[:, 1664:1792], mu[:, 1792:]]
    pre_params = mu_parts + [P["rw_w0"], _pad_rows(P["rw_w_up"], 128), P["rw_a0"], _pad_rows(P["rw_a_up"], 128),
                             _pad_rows(P["rw_g_up"], 256), P["rw_k_k"], P["rw_k_a"],
                             P["sc_conv_w"][0:1], P["sc_conv_w"][1:2], P["sc_conv_w"][2:3]]
    pieces = [(p, 512, 0), (p, 512, 1), (p, 512, 2), (p, 128, 12), (p, 128, 13), (p, 256, 7),
              (p, 512, 4), (p, 512, 5), (p, 512, 6)]
    shifted = [0, 1, 2, 3, 4, 5, 6, 8]
    pre_rows = pieces + [pieces[i] + ("prev",) for i in shifted]
    NPR = 19

    def pre_args(r):
        x, prev = r[:9], dict(zip(shifted, r[9:17]))
        down = lambda i, k: _shift_down(x[i], prev[i], k)
        return x[:6] + [down(i, 1) for i in range(6)] + x[6:9] + [down(6, 1), down(8, 1), down(6, 2), down(8, 2)]

    def pre_fwd_body(r, pp):
        return list(_pre_core(pp[0], *pre_args(r), *pp[1:])), []

    (r_, w_, kh_, v_, a_, b_, gate_, yb), _ = _rows(
        "rwkv_pre", pre_fwd_body, pre_rows, [E] + pre_params, [(RW, F32)] * 7 + [(RW, BF16)], tm=256)
    scan_in = [r_, w_, kh_, a_, b_, _cols3(v_, "rwkv_v_columns")]
    ysc, ck = _scan_fwd(*scan_in)
    post_params = [P["rw_lnx_g"], P["rw_lnx_b"], P["rw_r_k"].reshape(1, RW)]

    def postmix_fwd_body(r, pp):
        return [_post_core(pp[0], *r, *pp[1:])], []

    (ya,), _ = _rows("rwkv_post", postmix_fwd_body, [ysc, r_, kh_, v_, gate_], [E] + post_params,
                     [(RW, BF16)], tm=256)
    cat = jnp.concatenate([ya, yb], axis=1)
    y0 = _mm("ab_out", cat, W["ab_w_out"])
    x1, u1 = post_fwd(0, x0, y0)
    W.update(later_weights(u1))

    h1, a1 = _mm("mlp1_up_0", u1, W["mlp_w1"][0], out=(F32, BF16), epi=relu2)
    y1 = _mm("mlp1_down_0", a1, W["mlp_w2"][0])
    x2, u2 = post_fwd(1, x1, y1)

    pq = _mm("qkv", u2, W["dil_w_qkv"])
    onehotT = (_bucket_tables().reshape(3, 1, NPAIR) == jnp.arange(NBUCKET).reshape(1, NBUCKET, 1)).astype(BF16)
    rbT = P["rel_bias"].reshape(NBUCKET, 3, HEADS).transpose(1, 2, 0)
    bias = _relbias_table(rbT, onehotT).reshape(3, HEADS, BLK, 2 * BLK)
    og, lse = zip(*[_attn_fwd(pq, bias[g], g) for g in range(3)])

    def merge_fwd_body(r, pp):
        return [_merge_core(*r)], []

    (om,), _ = _rows("attn_merge", merge_fwd_body, list(og + lse), [], [(RW, BF16)])
    y2 = _mm("dil_out", om, W["dil_w_out"])
    x3, u3 = post_fwd(2, x2, y2)

    h3, a3 = _mm("mlp1_up_1", u3, W["mlp_w1"][1], out=(F32, BF16), epi=relu2)
    y3 = _mm("mlp1_down_1", a3, W["mlp_w2"][1])

    def last_body(r, pp):
        x, y, tg = r
        xn, vjp = jax.vjp(_post_ln, x, y, *pp)
        err = xn - tg
        dx, dy, dg, dlg, dlb = vjp(err * (1.0 / D))
        loss = jnp.full((1, 128), (0.5 / D) * jnp.sum(err * err), F32)
        return [dx, dy], [loss, dg, dlg, dlb]

    (dxp, dy3), (loss_acc, dg3, dlng3, dlnb3) = _rows(
        "final_ln_loss", last_body, [x3, y3, tgt], [gt[3], lng[3], lnb[3]],
        [(D, F32), (D, BF16)], [(1, 128), (1, D), (1, D), (1, D)])

    G = {}
    dsc, dsh, dgt = [None] * 4, [None] * 4, [None] * 4
    dlng, dlnb = [None] * 4, [None] * 4
    dgt[3], dlng[3], dlnb[3] = dg3, dlng3, dlnb3

    def mlp_bwd(i, u, h, a, dy):
        dh = _mm(f"mlp_dh_{i}", dy, W["mlp_w2"][i], tb=True, out=(BF16,), epi=relu2_bwd, extras=(h,))
        gw2 = _mm(f"mlp_dw2_{i}", a.T, dy)
        du = _mm(f"mlp_du_{i}", dh, W["mlp_w1"][i], tb=True)
        gw1 = _mm(f"mlp_dw1_{i}", u.T, dh)
        return du, gw1, gw2

    def post_bwd_body(r, pp):
        x, y, dxn, dun = r
        _, vjp = jax.vjp(_post_ln_mod, x, y, *pp)
        dx, dy, dg, dlg, dlb, dscn, dshn = vjp((dxn, dun))
        return [dx, dy], [dg, dlg, dlb, dscn, dshn]

    def post_bwd(s, x, y, dxn, dun):
        (dx, dy), (dgt[s], dlng[s], dlnb[s], dsc[s + 1], dsh[s + 1]) = _rows(
            f"post_ln_bwd_{s}", post_bwd_body, [x, y, dxn, dun],
            [gt[s], lng[s], lnb[s], sc[s + 1], sh[s + 1]], [(D, F32), (D, BF16)], [(1, D)] * 5)
        return dx, dy

    du3, gw1_1, gw2_1 = mlp_bwd(1, u3, h3, a3, dy3)
    dxp, dy2 = post_bwd(2, x2, y2, dxp, du3)

    G["dil_w_out"] = _mm("dil_out_dw", om.T, dy2)[None]
    do = _mm("dil_out_dx", dy2, W["dil_w_out"], tb=True)

    def merge_bwd_body(r, pp):
        _, vjp = jax.vjp(_merge_core, *r[:6])
        d = vjp(r[6])
        return list(d[:3]) + [_headsum(d[3 + g], pp[0]) for g in range(3)], []

    mb, _ = _rows("attn_merge_bwd", merge_bwd_body, list(og + lse) + [do], [E],
                  [(RW, F32)] * 6)
    back = [_attn_bwd(pq, bias[g], mb[g], og[g], lse[g], mb[3 + g], g) for g in range(3)]
    dpq = jnp.concatenate([t for dq, dk, dv, _ in back for t in (dq, dk, dv)], axis=1).astype(BF16)
    rb = _relbias_grad(jnp.stack([b[3] for b in back]).reshape(3, HEADS, NPAIR), onehotT)
    G["rel_bias"] = rb.transpose(2, 0, 1).reshape(NBUCKET, 3 * HEADS)
    G["dil_w_qkv"] = _mm("qkv_dw", u2.T, dpq)[None]
    du2 = _mm("qkv_dx", dpq, W["dil_w_qkv"], tb=True)
    dxp, dy1 = post_bwd(1, x1, y1, dxp, du2)

    du1, gw1_0, gw2_0 = mlp_bwd(0, u1, h1, a1, dy1)
    G["mlp_w1"] = jnp.stack([gw1_0, gw1_1])
    G["mlp_w2"] = jnp.stack([gw2_0, gw2_1])
    gt[0] = gt[0] + early_grads(G)
    dxp, dy0 = post_bwd(0, x0, y0, dxp, du1)

    G["ab_w_out"] = _mm("ab_out_dw", cat.T, dy0)[None]
    dcat = _mm("ab_out_dx", dy0, W["ab_w_out"], tb=True)

    def postmix_bwd_body(r, pp):
        _, vjp = jax.vjp(functools.partial(_post_core, pp[0]), *r[:5], *pp[1:])
        d = vjp(r[5])
        return list(d[:5]), list(d[5:])

    (dysc, dr1, dkh1, dv1, dgate), (G["rw_lnx_g"], G["rw_lnx_b"], drk) = _rows(
        "rwkv_post_bwd", postmix_bwd_body, [ysc, r_, kh_, v_, gate_, (dcat, 512, 0)], [E] + post_params,
        [(RW, F32)] * 5, [(1, RW)] * 3, tm=256)
    G["rw_r_k"] = drk.reshape(1, HEADS, HD)
    dr2, dw2, dk2, da2, db2, dv2 = _scan_bwd(*scan_in, _cols3(dysc, "rwkv_dy_columns"), ck)

    def pre_bwd_body(r, pp):
        prim, ct = pre_args(r[:len(pre_rows)]), r[len(pre_rows):]
        _, vjp = jax.vjp(functools.partial(_pre_core, pp[0]), *prim, *pp[1:])
        cts = (ct[0] + ct[1], ct[2], ct[3] + ct[4], ct[5] + ct[6], ct[7], ct[8], ct[9], ct[10])
        d = vjp(cts)
        z = jnp.zeros_like(d[12])
        dp = jnp.concatenate([d[0], d[1], d[2], d[3], d[4], d[5], d[12], d[13], d[14]], axis=1)
        dp1 = jnp.concatenate([d[6], d[7], d[8], d[9], d[10], d[11], d[15], z, d[16]], axis=1)
        dp2 = jnp.concatenate([d[17], z, d[18]], axis=1)
        return [dp, dp1, dp2], list(d[NPR:])

    acc_shapes = [a.shape for a in pre_params]
    (dp, dp1, dp2), pacc = _rows(
        "rwkv_pre_bwd", pre_bwd_body,
        pre_rows + [dr1, dr2, dw2, dkh1, dk2, dv1, dv2, da2, db2, dgate, (dcat, 512, 1)],
        [E] + pre_params, [(PAB, F32), (PAB, F32), (PB, F32)], acc_shapes, tm=256)
    G["rw_mu"] = _unpad_pa(jnp.concatenate(pacc[:6], axis=1))
    G["rw_w0"], G["rw_a0"], G["rw_k_k"], G["rw_k_a"] = pacc[6], pacc[8], pacc[11], pacc[12]
    G["rw_w_up"] = pacc[7][None, :64]
    G["rw_a_up"] = pacc[9][None, :64]
    G["rw_g_up"] = pacc[10][None, :160]
    G["sc_conv_w"] = jnp.concatenate(pacc[13:16], axis=0)[None]

    def shift_merge_body(r, pp):
        d0, d1, d1_next, d2, d2_next = r
        d = d0 + _shift_up(d1, d1_next, 1)
        return [jnp.concatenate([d[:, :PA], d[:, PA:] + _shift_up(d2, d2_next, 2)], axis=1)], []

    (dpt,), _ = _rows("shift_merge", shift_merge_body,
                      [dp, dp1, (dp1, PAB, 0, "next"), dp2, (dp2, PB, 0, "next")], [], [(PAB, BF16)])
    gin = _mm("ab_in_dw", u0.T, dpt)
    G["ab_w_in"] = jnp.concatenate([_unpad_pa(gin[:, :PA]), gin[:, PA:]], axis=1)[None]
    du0 = _mm("ab_in_dx", dpt, W["ab_w_in"], tb=True)

    def mod_bwd_body(r, pp):
        du, dx, x = r
        return [dx + du * (1.0 + pp[0])], [jnp.sum(du * x, axis=0, keepdims=True), jnp.sum(du, axis=0, keepdims=True)]

    (grad_x,), (dsc[0], dsh[0]) = _rows("modulate_bwd", mod_bwd_body, [du0, dxp, x0], [sc[0]], [(D, F32)],
                                        [(1, D), (1, D)])

    G["ln_g"] = jnp.concatenate(dlng, axis=0).reshape(2, 2, D)
    G["ln_b"] = jnp.concatenate(dlnb, axis=0).reshape(2, 2, D)
    dmod = jnp.concatenate([dsh[0], dsc[0], dgt[0], dsh[1], dsc[1], dgt[1],
                            dsh[2], dsc[2], dgt[2], dsh[3], dsc[3], dgt[3]], axis=1).reshape(2, 6 * D)
    return loss_acc[0, 0], grad_x, dmod, G


def kernel(x, c, ada_w, ada_b, ln_g, ln_b, ab_w_in, rw_mu, rw_w0, rw_w_up, rw_a0, rw_a_up, rw_g_up, rw_k_k, rw_k_a, rw_r_k, rw_lnx_g, rw_lnx_b, sc_conv_w, ab_w_out, dil_w_qkv, dil_w_out, rel_bias, mlp_w1, mlp_w2, loss_target, m_ada_w, m_ada_b, m_ln_g, m_ln_b, m_ab_w_in, m_rw_mu, m_rw_w0, m_rw_w_up, m_rw_a0, m_rw_a_up, m_rw_g_up, m_rw_k_k, m_rw_k_a, m_rw_r_k, m_rw_lnx_g, m_rw_lnx_b, m_sc_conv_w, m_ab_w_out, m_dil_w_qkv, m_dil_w_out, m_rel_bias, m_mlp_w1, m_mlp_w2, v_ada_w, v_ada_b, v_ln_g, v_ln_b, v_ab_w_in, v_rw_mu, v_rw_w0, v_rw_w_up, v_rw_a0, v_rw_a_up, v_rw_g_up, v_rw_k_k, v_rw_k_a, v_rw_r_k, v_rw_lnx_g, v_rw_lnx_b, v_sc_conv_w, v_ab_w_out, v_dil_w_qkv, v_dil_w_out, v_rel_bias, v_mlp_w1, v_mlp_w2):
    w = dict(ada_w=ada_w, ada_b=ada_b, ln_g=ln_g, ln_b=ln_b, ab_w_in=ab_w_in, rw_mu=rw_mu, rw_w0=rw_w0,
             rw_w_up=rw_w_up, rw_a0=rw_a0, rw_a_up=rw_a_up, rw_g_up=rw_g_up, rw_k_k=rw_k_k, rw_k_a=rw_k_a,
             rw_r_k=rw_r_k, rw_lnx_g=rw_lnx_g, rw_lnx_b=rw_lnx_b, sc_conv_w=sc_conv_w, ab_w_out=ab_w_out,
             dil_w_qkv=dil_w_qkv, dil_w_out=dil_w_out, rel_bias=rel_bias, mlp_w1=mlp_w1, mlp_w2=mlp_w2)
    m = dict(ada_w=m_ada_w, ada_b=m_ada_b, ln_g=m_ln_g, ln_b=m_ln_b, ab_w_in=m_ab_w_in, rw_mu=m_rw_mu,
             rw_w0=m_rw_w0, rw_w_up=m_rw_w_up, rw_a0=m_rw_a0, rw_a_up=m_rw_a_up, rw_g_up=m_rw_g_up,
             rw_k_k=m_rw_k_k, rw_k_a=m_rw_k_a, rw_r_k=m_rw_r_k, rw_lnx_g=m_rw_lnx_g, rw_lnx_b=m_rw_lnx_b,
             sc_conv_w=m_sc_conv_w, ab_w_out=m_ab_w_out, dil_w_qkv=m_dil_w_qkv, dil_w_out=m_dil_w_out,
             rel_bias=m_rel_bias, mlp_w1=m_mlp_w1, mlp_w2=m_mlp_w2)
    v = dict(ada_w=v_ada_w, ada_b=v_ada_b, ln_g=v_ln_g, ln_b=v_ln_b, ab_w_in=v_ab_w_in, rw_mu=v_rw_mu,
             rw_w0=v_rw_w0, rw_w_up=v_rw_w_up, rw_a0=v_rw_a0, rw_a_up=v_rw_a_up, rw_g_up=v_rw_g_up,
             rw_k_k=v_rw_k_k, rw_k_a=v_rw_k_a, rw_r_k=v_rw_r_k, rw_lnx_g=v_rw_lnx_g, rw_lnx_b=v_rw_lnx_b,
             sc_conv_w=v_sc_conv_w, ab_w_out=v_ab_w_out, dil_w_qkv=v_dil_w_qkv, dil_w_out=v_dil_w_out,
             rel_bias=v_rel_bias, mlp_w1=v_mlp_w1, mlp_w2=v_mlp_w2)
    kinds = dict(SHARDED)
    me = 4 * lax.axis_index("x") + 2 * lax.axis_index("y") + lax.axis_index("c")
    ncol = ada_w.shape[2]

    small = _all_gather(_pack([c] + [w[n] for n in GATHER_F32], F32, 8), "gather_small")
    parts = _unpack(small, [c.shape] + [w[n].shape for n in GATHER_F32], (NDEV,))
    c_all = parts[0].reshape(NDEV, D)
    P = {n: _from_chunks(t, kinds[n]) for n, t in zip(GATHER_F32, parts[1:])}
    P = {n: (t if n in ("ln_g", "ln_b") else t[0]) for n, t in P.items()}
    for n in REPLICATED[1:]:
        P[n] = w[n]
    def full(n, t):
        t = _from_chunks(t, kinds[n])
        return t if n in ("mlp_w1", "mlp_w2") else t[0]

    parts = _all_gather_many([w[n].astype(BF16) for n in FIRST], "gather_first_weights")
    W = {n: full(n, t) for n, t in zip(FIRST, parts)}
    W["ab_w_in"] = jnp.concatenate([_pad_pa(W["ab_w_in"][:, :1824]), W["ab_w_in"][:, 1824:]], axis=1)

    ada_b_loc = lax.dynamic_slice(ada_b, (0, ncol * me), (2, ncol))
    mod_part = _ada_mod(c_all, ada_w, ada_b_loc)
    mod_all = _all_gather(mod_part.reshape(-1, 128), "gather_mod").reshape(NDEV, 2, NDEV, ncol)
    mod = lax.dynamic_index_in_dim(mod_all, me, axis=2, keepdims=False)
    mod = mod.transpose(1, 0, 2).reshape(2, 6 * D)

    behind = (mod[0, 0] * 0.0).astype(BF16)
    later = _exchange_start([w[n].astype(BF16) + (behind if n == LATER[0] else 0) for n in LATER], True,
                            "gather_later_weights_start")
    mod = mod + later[-1][0, 0]

    def later_weights(after):
        lands = _exchange_wait(later, True, after, "gather_later_weights_wait")
        return {n: full(n, t) for n, t in zip(LATER, lands)}

    sent = []

    def early_grads(G):
        sent.append(_exchange_start([_to_chunks(G[n], kinds[n]).astype(BF16) for n in LATER], False,
                                    "exchange_later_grads_start"))
        return sent[0][-1][0, 0]

    loss_part, grad_x, dmod, G = _local_step(x[0], loss_target[0], mod, W, P, later_weights, early_grads)
    G["ada_b"] = dmod
    loss = lax.psum(loss_part, ("x", "y", "c"))

    rep_shapes = [w[n].shape for n in REPLICATED]
    rep_all = _all_gather(_pack([G[n] for n in REPLICATED], F32, 8), "gather_replicated_grads")
    pk = lambda d: _pack([d[n] for n in REPLICATED], F32, 8)
    rep_out = _sum_adamw(rep_all, pk(w), pk(m), pk(v), "sum_adamw_replicated", rep_all.shape[1])
    rep_out = [dict(zip(REPLICATED, _unpack(o, rep_shapes))) for o in rep_out]

    dmod_all = _unpack(rep_all, [(2, 6 * D)], (NDEV,))[0]
    dmod_loc = lax.dynamic_slice(dmod_all, (0, 0, ncol * me), (NDEV, 2, ncol)).transpose(1, 0, 2)
    ada_out = _ada_grad_adamw(c_all.T, dmod_loc, ada_w, m_ada_w, v_ada_w)

    names = [n for n, _ in SHARDED if n not in GATHER_BF16]
    shard_shapes = [w[n].shape for n in names]
    chunks = _pack8([_to_chunks(G[n], kinds[n]) for n in names], F32, 8)
    recv = _all_to_all(chunks, "exchange_small_grads")
    pk = lambda d: _pack([d[n] for n in names], F32, 8)
    sh_out = _sum_adamw(recv, pk(w), pk(m), pk(v), "sum_adamw_small", recv.shape[1])
    sh_out = [dict(zip(names, _unpack(o, shard_shapes))) for o in sh_out]

    big_out = {}

    def update(n, contributions):
        cols = w[n].shape[-1]
        flat = lambda t: t.reshape(-1, cols)
        rows = flat(w[n]).shape[0]
        outs = _sum_adamw(contributions.reshape(-1, rows, cols), flat(w[n]), flat(m[n]), flat(v[n]),
                          f"sum_adamw_{n}", min(rows, 256))
        big_out[n] = [o.reshape(w[n].shape) for o in outs]

    ci = lax.axis_index("c")
    mine_l, sib_l = [], []
    for n in FIRST:
        g8 = _to_chunks(G[n], kinds[n])
        g42 = g8.reshape((4, 2) + g8.shape[1:])
        mine_l.append(lax.dynamic_index_in_dim(g42, ci, 1, keepdims=False))
        sib_l.append(lax.dynamic_index_in_dim(g42, 1 - ci, 1, keepdims=False))
    from_sib = _swap_sibling(sib_l, "swap_sibling_grads")

    def add2_body(r, pp):
        return [r[0] + r[1]], []

    partials = []
    for n, a, b in zip(FIRST, mine_l, from_sib):
        cols = a.shape[-1]
        (p,), _ = _rows(f"pair_sum_{n}", add2_body, [a.reshape(-1, cols), b.reshape(-1, cols)], [],
                        [(cols, BF16)], tm=512)
        partials.append(p.reshape(a.shape))
    for n, r in zip(FIRST, _exchange_chips(partials, "exchange_chip_grads")):
        update(n, r)

    for n, r in zip(LATER, _exchange_wait(sent[0], False, partials[0], "exchange_later_grads_wait")):
        update(n, r)
    sh_out = [{**d, **{n: big_out[n][i] for n in GATHER_BF16}} for i, d in enumerate(sh_out)]

    def pick(i, n):
        if n == "ada_w":
            return ada_out[i]
        return rep_out[i][n] if n in REPLICATED else sh_out[i][n]

    outs = [loss, grad_x[None]]
    for i in range(4):
        outs += [pick(i, n) for n in WEIGHTS]
    return tuple(outs)
```

```python
import functools
import math

import jax
import jax.numpy as jnp
from jax import lax
from jax.experimental import pallas as pl
from jax.experimental.pallas import tpu as pltpu

F32 = jnp.float32
BF16 = jnp.bfloat16
HI = lax.Precision.HIGHEST

NDEV = 8
T = 2048
D = 1024
DFF = 4096
HEADS = 8
HD = 64
RW = 512
PA = 2048
PB = 1536
PAB = PA + PB
QKV = 4608
DILS = (1, 4, 16)
BLK = 128
ALPHA = 4.0 ** 0.25
LN_EPS = 1e-5
GN_EPS = 64e-5
ADAM_LR, ADAM_B1, ADAM_B2, ADAM_EPS, ADAM_WD, ADAM_STEP = 0.001, 0.9, 0.999, 1e-8, 0.01, 10
VMEM_LIMIT = 56 * 1024 * 1024


def _cp(sem):
    return pltpu.CompilerParams(dimension_semantics=sem, vmem_limit_bytes=VMEM_LIMIT)


def _slot(px, py, pc):
    return 4 * px + 2 * py + pc


def _all_gather(x, name):
    R, C = x.shape

    def body(x_ref, out_ref, send_sems, recv_sems, local_sem):
        xi, yi, ci = lax.axis_index("x"), lax.axis_index("y"), lax.axis_index("c")
        me, sibling = (xi, yi, ci), (xi, yi, 1 - ci)
        chips = [(1 - xi, yi), (xi, 1 - yi), (1 - xi, 1 - yi)]

        def rows(px, py, pc):
            return out_ref.at[_slot(px, py, pc)]

        def copy(k, block, to, src=None):
            return pltpu.make_async_remote_copy(
                src_ref=rows(*block) if src is None else src, dst_ref=rows(*block),
                send_sem=send_sems.at[k], recv_sem=recv_sems.at[k],
                device_id=to, device_id_type=pl.DeviceIdType.MESH)

        mine = pltpu.make_async_copy(x_ref, rows(*me), local_sem)
        mine.start()
        first = [copy(0, me, sibling, src=x_ref)]
        first += [copy(1 + j, me, (*chip, ci), src=x_ref) for j, chip in enumerate(chips)]
        for cp in first:
            cp.start()
        passed = [copy(4 + j, (*chip, ci), sibling) for j, chip in enumerate(chips)]
        for j, chip in enumerate(chips):
            copy(1 + j, (*chip, ci), me).wait_recv()
            passed[j].start()
        copy(0, sibling, me).wait_recv()
        for j, chip in enumerate(chips):
            copy(4 + j, (*chip, 1 - ci), me).wait_recv()
        for cp in first + passed:
            cp.wait_send()
        mine.wait()

    return pl.pallas_call(
        body, name=name,
        out_shape=jax.ShapeDtypeStruct((NDEV, R, C), x.dtype),
        in_specs=[pl.BlockSpec(memory_space=pl.ANY)],
        out_specs=pl.BlockSpec(memory_space=pl.ANY),
        scratch_shapes=[pltpu.SemaphoreType.DMA((7,)), pltpu.SemaphoreType.DMA((7,)),
                        pltpu.SemaphoreType.DMA(())],
    )(x)


def _all_to_all(g, name):
    _, R, C = g.shape

    def body(g_ref, out_ref, send_sems, recv_sems, local_sem):
        xi, yi, ci = lax.axis_index("x"), lax.axis_index("y"), lax.axis_index("c")
        my_slot = _slot(xi, yi, ci)
        mine = pltpu.make_async_copy(g_ref.at[my_slot], out_ref.at[my_slot], local_sem)
        mine.start()
        copies = []
        for k in range(1, 8):
            px = 1 - xi if k & 4 else xi
            py = 1 - yi if k & 2 else yi
            pc = 1 - ci if k & 1 else ci
            peer_slot = _slot(px, py, pc)
            copies.append((
                pltpu.make_async_remote_copy(
                    src_ref=g_ref.at[peer_slot], dst_ref=out_ref.at[my_slot],
                    send_sem=send_sems.at[k - 1], recv_sem=recv_sems.at[k - 1],
                    device_id=(px, py, pc), device_id_type=pl.DeviceIdType.MESH),
                pltpu.make_async_remote_copy(
                    src_ref=g_ref.at[peer_slot], dst_ref=out_ref.at[peer_slot],
                    send_sem=send_sems.at[k - 1], recv_sem=recv_sems.at[k - 1],
                    device_id=(px, py, pc), device_id_type=pl.DeviceIdType.MESH)))
        for send, _ in copies:
            send.start()
        for _, recv in copies:
            recv.wait_recv()
        for send, _ in copies:
            send.wait_send()
        mine.wait()

    return pl.pallas_call(
        body, name=name,
        out_shape=jax.ShapeDtypeStruct((NDEV, R, C), g.dtype),
        in_specs=[pl.BlockSpec(memory_space=pl.ANY)],
        out_specs=pl.BlockSpec(memory_space=pl.ANY),
        scratch_shapes=[pltpu.SemaphoreType.DMA((7,)), pltpu.SemaphoreType.DMA((7,)),
                        pltpu.SemaphoreType.DMA(())],
    )(g)


def _my_slot():
    return _slot(lax.axis_index("x"), lax.axis_index("y"), lax.axis_index("c"))


def _put_own(buf, own, slot):
    return lax.dynamic_update_index_in_dim(buf, own, slot, 0)


def _hbm_call(body, name, ins, out_shapes, n_sems):
    anyspec = pl.BlockSpec(memory_space=pl.ANY)
    return pl.pallas_call(
        body, name=name, out_shape=out_shapes,
        in_specs=[anyspec] * len(ins), out_specs=[anyspec] * len(out_shapes),
        scratch_shapes=[pltpu.SemaphoreType.DMA(s) for s in n_sems],
    )(*ins)


def _all_gather_many(xs, name):
    n = len(xs)

    def body(*refs):
        x_refs, o_refs = refs[:n], refs[n:2 * n]
        send_sems, recv_sems = refs[2 * n:]
        xi, yi, ci = lax.axis_index("x"), lax.axis_index("y"), lax.axis_index("c")
        me, sibling = (xi, yi, ci), (xi, yi, 1 - ci)
        chips = [(1 - xi, yi), (xi, 1 - yi), (1 - xi, 1 - yi)]

        def copy(i, k, block, to, src=None):
            dst = o_refs[i].at[_slot(*block)]
            return pltpu.make_async_remote_copy(
                src_ref=dst if src is None else src, dst_ref=dst,
                send_sem=send_sems.at[i, k], recv_sem=recv_sems.at[i, k],
                device_id=to, device_id_type=pl.DeviceIdType.MESH)

        sends = []
        for i in range(n):
            sends += [copy(i, 1 + j, me, (*chip, ci), src=x_refs[i]) for j, chip in enumerate(chips)]
            sends.append(copy(i, 0, me, sibling, src=x_refs[i]))
        for cp in sends:
            cp.start()
        for j, chip in enumerate(chips):
            for i in range(n):
                copy(i, 1 + j, (*chip, ci), me).wait_recv()
                passed = copy(i, 4 + j, (*chip, ci), sibling)
                passed.start()
                sends.append(passed)
        for i in range(n):
            copy(i, 0, sibling, me).wait_recv()
            for j, chip in enumerate(chips):
                copy(i, 4 + j, (*chip, 1 - ci), me).wait_recv()
        for cp in sends:
            cp.wait_send()

    outs = _hbm_call(body, name, xs, [jax.ShapeDtypeStruct((NDEV,) + x.shape, x.dtype) for x in xs],
                     [(n, 7), (n, 7)])
    return [_put_own(o, x[None], _my_slot()) for o, x in zip(outs, xs)]


def _swap_sibling(gs, name):
    n = len(gs)

    def body(*refs):
        g_refs, o_refs = refs[:n], refs[n:2 * n]
        send_sems, recv_sems = refs[2 * n:]
        sibling = (lax.axis_index("x"), lax.axis_index("y"), 1 - lax.axis_index("c"))
        copies = [pltpu.make_async_remote_copy(
            src_ref=g_refs[i], dst_ref=o_refs[i], send_sem=send_sems.at[i], recv_sem=recv_sems.at[i],
            device_id=sibling, device_id_type=pl.DeviceIdType.MESH) for i in range(n)]
        for cp in copies:
            cp.start()
        for cp in copies:
            cp.wait_recv()
        for cp in copies:
            cp.wait_send()

    return _hbm_call(body, name, gs, [jax.ShapeDtypeStruct(g.shape, g.dtype) for g in gs], [(n,), (n,)])


def _exchange_chips(ps, name):
    n = len(ps)

    def body(*refs):
        p_refs, o_refs = refs[:n], refs[n:2 * n]
        send_sems, recv_sems = refs[2 * n:]
        xi, yi, ci = lax.axis_index("x"), lax.axis_index("y"), lax.axis_index("c")
        q_me = 2 * xi + yi
        sends, recvs = [], []
        for k in range(1, 4):
            px = 1 - xi if k & 2 else xi
            py = 1 - yi if k & 1 else yi
            q_peer = 2 * px + py
            for i in range(n):
                sends.append(pltpu.make_async_remote_copy(
                    src_ref=p_refs[i].at[q_peer], dst_ref=o_refs[i].at[q_me],
                    send_sem=send_sems.at[i, k - 1], recv_sem=recv_sems.at[i, k - 1],
                    device_id=(px, py, ci), device_id_type=pl.DeviceIdType.MESH))
                recvs.append(pltpu.make_async_remote_copy(
                    src_ref=p_refs[i].at[q_peer], dst_ref=o_refs[i].at[q_peer],
                    send_sem=send_sems.at[i, k - 1], recv_sem=recv_sems.at[i, k - 1],
                    device_id=(px, py, ci), device_id_type=pl.DeviceIdType.MESH))
        for cp in sends:
            cp.start()
        for cp in recvs:
            cp.wait_recv()
        for cp in sends:
            cp.wait_send()

    outs = _hbm_call(body, name, ps, [jax.ShapeDtypeStruct(p.shape, p.dtype) for p in ps], [(n, 3), (n, 3)])
    q_me = 2 * lax.axis_index("x") + lax.axis_index("y")
    return [_put_own(o, lax.dynamic_index_in_dim(p, q_me, 0, keepdims=True), q_me) for o, p in zip(outs, ps)]


def _peers(xi, yi, ci):
    return [(1 - xi if k & 4 else xi, 1 - yi if k & 2 else yi, 1 - ci if k & 1 else ci) for k in range(1, 8)]


def _direct_copy(src_refs, land_refs, send_sems, recv_sems, i, k, peer, my_slot, gather):
    src = src_refs[i] if gather else src_refs[i].at[_slot(*peer)]
    return pltpu.make_async_remote_copy(
        src_ref=src, dst_ref=land_refs[i].at[my_slot], send_sem=send_sems.at[7 * i + k], recv_sem=recv_sems.at[7 * i + k],
        device_id=peer, device_id_type=pl.DeviceIdType.MESH)


def _exchange_start(srcs, gather, name):
    n = len(srcs)
    lands = [lax.empty(((NDEV,) + s.shape) if gather else s.shape, s.dtype) for s in srcs]

    def body(*refs):
        s_refs, l_refs = refs[:n], refs[n:2 * n]
        send_sems, recv_sems = refs[2 * n], refs[2 * n + 1]
        token = refs[2 * n + 2 + 2 * n]
        xi, yi, ci = lax.axis_index("x"), lax.axis_index("y"), lax.axis_index("c")
        my_slot = _slot(xi, yi, ci)
        for k, peer in enumerate(_peers(xi, yi, ci)):
            for i in range(n):
                _direct_copy(s_refs, l_refs, send_sems, recv_sems, i, k, peer, my_slot, gather).start()
        token[...] = jnp.zeros_like(token)

    hbm = pl.BlockSpec(memory_space=pltpu.HBM)
    sem = pl.BlockSpec(memory_space=pltpu.SEMAPHORE)
    both = list(srcs) + lands
    return pl.pallas_call(
        body, name=name,
        out_shape=(pltpu.SemaphoreType.DMA((7 * n,)), pltpu.SemaphoreType.DMA((7 * n,)),
                   *[pltpu.HBM(t.shape, t.dtype) for t in both], jax.ShapeDtypeStruct((8, 128), F32)),
        in_specs=[hbm] * (2 * n),
        out_specs=(sem, sem, *[hbm] * (2 * n), pl.BlockSpec(memory_space=pltpu.VMEM)),
        input_output_aliases={i: 2 + i for i in range(2 * n)},
        compiler_params=pltpu.CompilerParams(has_side_effects=pltpu.SideEffectType.DATAFLOW_SIDE_EFFECTING),
    )(*[pltpu.with_memory_space_constraint(t, pltpu.HBM) for t in both])


def _exchange_wait(started, gather, after, name):
    send_sems, recv_sems, *thru, _ = started
    n = len(thru) // 2

    def body(*refs):
        s_refs, l_refs = refs[:n], refs[n:2 * n]
        send_sems, recv_sems = refs[2 * n], refs[2 * n + 1]
        xi, yi, ci = lax.axis_index("x"), lax.axis_index("y"), lax.axis_index("c")
        my_slot = _slot(xi, yi, ci)
        for k, peer in enumerate(_peers(xi, yi, ci)):
            for i in range(n):
                _direct_copy(s_refs, l_refs, send_sems, recv_sems, i, k, peer, my_slot, gather).wait_send()
                _direct_copy(s_refs, l_refs, send_sems, recv_sems, i, k, peer, _slot(*peer), gather).wait_recv()

    hbm = pl.BlockSpec(memory_space=pltpu.HBM)
    sem = pl.BlockSpec(memory_space=pltpu.SEMAPHORE)
    outs = pl.pallas_call(
        body, name=name,
        out_shape=tuple(pltpu.HBM(t.shape, t.dtype) for t in thru),
        in_specs=[hbm] * (2 * n) + [sem, sem, pl.BlockSpec(memory_space=pl.ANY)],
        out_specs=tuple([hbm] * (2 * n)),
        input_output_aliases={i: i for i in range(2 * n)},
        compiler_params=pltpu.CompilerParams(has_side_effects=pltpu.SideEffectType.DATAFLOW_SIDE_EFFECTING),
    )(*thru, send_sems, recv_sems, after)
    slot = _my_slot()
    own = [s[None] if gather else lax.dynamic_index_in_dim(s, slot, 0, keepdims=True) for s in outs[:n]]
    return [_put_own(land, o, slot) for land, o in zip(outs[n:], own)]


def _mm(name, a, b, tb=False, out=(F32,), epi=None, extras=(), tm=1024, tn=512, tk_cap=2048):
    M, K = a.shape
    N = b.shape[0] if tb else b.shape[1]
    tm, tn = min(tm, M), min(tn, N)
    tk = max(t for t in range(128, min(K, tk_cap) + 1, 128) if K % t == 0)
    assert M % tm == 0 and N % tn == 0 and K % tk == 0, (name, M, N, K)
    nk = K // tk
    ne, no = len(extras), len(out)
    dims = (((1,), (1 if tb else 0,)), ((), ()))

    def kern(*refs):
        a_ref, b_ref = refs[:2]
        e_refs = refs[2:2 + ne]
        o_refs = refs[2 + ne:2 + ne + no]

        def finish(acc):
            outs = epi(acc, *[e[...] for e in e_refs]) if epi is not None else (acc,)
            for o_ref, o in zip(o_refs, outs):
                o_ref[...] = o.astype(o_ref.dtype)

        part = lax.dot_general(a_ref[...], b_ref[...], dims, preferred_element_type=F32)
        if nk == 1:
            finish(part)
            return
        acc_ref = refs[-1]
        k = pl.program_id(2)

        @pl.when(k == 0)
        def _():
            acc_ref[...] = part

        @pl.when(k > 0)
        def _():
            acc_ref[...] += part

        @pl.when(k == nk - 1)
        def _():
            finish(acc_ref[...])

    b_spec = (pl.BlockSpec((tn, tk), lambda i, j, k: (j, k)) if tb
              else pl.BlockSpec((tk, tn), lambda i, j, k: (k, j)))
    tile = pl.BlockSpec((tm, tn), lambda i, j, k: (i, j))
    res = pl.pallas_call(
        kern, name=name, grid=(M // tm, N // tn, nk),
        in_specs=[pl.BlockSpec((tm, tk), lambda i, j, k: (i, k)), b_spec] + [tile] * ne,
        out_specs=[tile] * no,
        out_shape=[jax.ShapeDtypeStruct((M, N), dt) for dt in out],
        scratch_shapes=[pltpu.VMEM((tm, tn), F32)] if nk > 1 else [],
        compiler_params=_cp(("parallel", "parallel", "arbitrary")),
    )(a, b, *extras)
    return res[0] if no == 1 else res


HALO = 8


def _rows(name, body, rows, params, out_rows, out_accs=(), tm=256):
    views = [r if isinstance(r, tuple) else (r, r.shape[1], 0) for r in rows]
    n = views[0][0].shape[0]
    assert n % tm == 0 and tm % HALO == 0
    nr, npar, nor, noa = len(views), len(params), len(out_rows), len(out_accs)

    def row_spec(width, cb, halo=None):
        per, last = tm // HALO, n // HALO - 1
        if halo == "prev":
            return pl.BlockSpec((HALO, width), lambda i: (jnp.maximum(i * per - 1, 0), cb))
        if halo == "next":
            return pl.BlockSpec((HALO, width), lambda i: (jnp.minimum((i + 1) * per, last), cb))
        return pl.BlockSpec((tm, width), lambda i: (i, cb))

    def kern(*refs):
        r_refs = refs[:nr]
        p_refs = refs[nr:nr + npar]
        o_refs = refs[nr + npar:nr + npar + nor]
        a_refs = refs[nr + npar + nor:]
        outs, accs = body([r[...] for r in r_refs], [p[...] for p in p_refs])
        assert len(outs) == nor and len(accs) == noa, (name, len(outs), len(accs))
        for o_ref, o in zip(o_refs, outs):
            o_ref[...] = o.astype(o_ref.dtype)
        if noa:
            @pl.when(pl.program_id(0) == 0)
            def _():
                for a_ref in a_refs:
                    a_ref[...] = jnp.zeros_like(a_ref)

            for a_ref, a in zip(a_refs, accs):
                a_ref[...] += a.astype(F32)

    def whole(shape):
        nd = len(shape)
        return pl.BlockSpec(tuple(shape), lambda i, nd=nd: (0,) * nd)

    in_specs = [row_spec(*v[1:]) for v in views]
    in_specs += [whole(p.shape) for p in params]
    out_specs = [pl.BlockSpec((tm, c), lambda i: (i, 0)) for c, _ in out_rows]
    out_specs += [whole(s) for s in out_accs]
    out_shape = [jax.ShapeDtypeStruct((n, c), dt) for c, dt in out_rows]
    out_shape += [jax.ShapeDtypeStruct(tuple(s), F32) for s in out_accs]
    res = pl.pallas_call(
        kern, name=name, grid=(n // tm,), in_specs=in_specs, out_specs=out_specs,
        out_shape=out_shape, compiler_params=_cp(("arbitrary",)),
    )(*[v[0] for v in views], *params)
    return res[:nor], res[nor:]


def _shift_down(x, prev, k):
    head = jnp.where(pl.program_id(0) == 0, 0.0, pltpu.roll(prev, k, axis=0))
    row = lax.broadcasted_iota(jnp.int32, x.shape, 0)
    return jnp.where(row < k, jnp.tile(head, (x.shape[0] // HALO, 1)), pltpu.roll(x, k, axis=0))


def _shift_up(x, nxt, k):
    n = x.shape[0]
    tail = jnp.where(pl.program_id(0) == pl.num_programs(0) - 1, 0.0, pltpu.roll(nxt, HALO - k, axis=0))
    row = lax.broadcasted_iota(jnp.int32, x.shape, 0)
    return jnp.where(row >= n - k, jnp.tile(tail, (n // HALO, 1)), pltpu.roll(x, n - k, axis=0))


@jax.custom_vjp
def _headsum(x, e):
    return sum(jnp.dot(p, e, preferred_element_type=F32) for p in _split3(x))


_headsum.defvjp(lambda x, e: (_headsum(x, e), e), lambda e, ct: (_headsum(ct, e), None))


def _softplus(z):
    return jnp.maximum(z, 0.0) + jnp.log(1.0 + jnp.exp(jnp.minimum(z, -z)))


def _post_ln(x, y, g, lng, lnb):
    z = ALPHA * x + (1.0 + g) * y
    mu = jnp.mean(z, axis=-1, keepdims=True)
    zc = z - mu
    var = jnp.mean(zc * zc, axis=-1, keepdims=True)
    return zc * lax.rsqrt(var + LN_EPS) * lng + lnb


def _post_ln_mod(x, y, g, lng, lnb, scn, shn):
    xn = _post_ln(x, y, g, lng, lnb)
    return xn, xn * (1.0 + scn) + shn


def _pre_core(E, r_, k_, v_, wd_, ad_, gd_, r1, k1, v1, wd1, ad1, gd1, h, bg, cg, h1, cg1, h2, cg2,
              mu_r, mu_k, mu_v, mu_wd, mu_ad, mu_gd, w0, w_up, a0, a_up, g_up, k_k, k_a,
              cw0, cw1, cw2):
    def mix(x, x1, mu):
        return x + mu * (x1 - x)

    r, k, v = mix(r_, r1, mu_r), mix(k_, k1, mu_k), mix(v_, v1, mu_v)
    wd, ad, gd = mix(wd_, wd1, mu_wd), mix(ad_, ad1, mu_ad), mix(gd_, gd1, mu_gd)
    logw = -_softplus(-(w0 + jnp.dot(jnp.tanh(wd), w_up, preferred_element_type=F32))) - 0.5
    decay = jnp.exp(-jnp.exp(logw))
    iclr = jax.nn.sigmoid(a0 + jnp.dot(ad, a_up, preferred_element_type=F32))
    gate = jnp.dot(jax.nn.sigmoid(gd), g_up, preferred_element_type=F32)
    kk0 = k * k_k
    nrm = jnp.sqrt(_headsum(kk0 * kk0, E))
    kk = kk0 / jnp.maximum(nrm, 1e-12)
    kh = k * (1.0 + (iclr - 1.0) * k_a)
    yb = bg * (cw2 * (cg * h) + cw1 * (cg1 * h1) + cw0 * (cg2 * h2))
    return r, decay, kh, v, -kk, kk * iclr, gate, yb


def _post_core(E, y, r, kh, v, gate, lnx_g, lnx_b, rk):
    def seg(t):
        return _headsum(t, E)

    mean = seg(y) * (1.0 / HD)
    yc = y - mean
    var = seg(yc * yc) * (1.0 / HD)
    gn = yc * lax.rsqrt(var + GN_EPS) * lnx_g + lnx_b
    bonus = seg(r * kh * rk) * v
    return (gn + bonus) * gate


def _merge_core(o0, o1, o2, l0, l1, l2):
    m = jnp.maximum(jnp.maximum(l0, l1), l2)
    e0, e1, e2 = jnp.exp(l0 - m), jnp.exp(l1 - m), jnp.exp(l2 - m)
    den = e0 + e1 + e2
    return (e0 * o0 + e1 * o1 + e2 * o2) / den


CHUNK = 128
HALF = 64
HP = HEADS // 2
LW = 2 * HD
NCHUNK = T // CHUNK


def _split3(x):
    hi = x.astype(BF16)
    r1 = x - hi.astype(F32)
    mid = r1.astype(BF16)
    return hi, mid, (r1 - mid.astype(F32)).astype(BF16)


def _cols3(x, name):
    def kern(x_ref, o_ref):
        xt = x_ref[...].T
        left = lax.broadcasted_iota(jnp.int32, (HD, CHUNK), 1) < HALF
        for p in range(HP):
            a, b = xt[p * LW:p * LW + HD], xt[p * LW + HD:(p + 1) * LW]
            halves = [jnp.where(left, a, pltpu.roll(b, HALF, axis=1)), jnp.where(left, pltpu.roll(a, HALF, axis=1), b)]
            for h, tile in enumerate(halves):
                for j, part in enumerate(_split3(tile)):
                    o_ref[p, :, (3 * h + j) * LW:(3 * h + j + 1) * LW] = part

    return pl.pallas_call(
        kern, name=name, grid=(NCHUNK,),
        in_specs=[pl.BlockSpec((CHUNK, RW), lambda c: (c, 0))],
        out_specs=pl.BlockSpec((HP, HD, 6 * CHUNK), lambda c: (0, 0, c)),
        out_shape=jax.ShapeDtypeStruct((HP, HD, 6 * T), BF16),
        compiler_params=_cp(("parallel",)),
    )(x)


def _pick_codes():
    row = lax.broadcasted_iota(jnp.int32, (6 * HALF, LW), 0)
    col = lax.broadcasted_iota(jnp.int32, (6 * HALF, LW), 1)
    same = ((row & (LW - 1)) >= HALF) == (col >= HD)
    return jnp.where(same, row & (HALF - 1), -1).astype(BF16)


def _column(block_ref, codes, half, i):
    pick = jnp.where(codes == i.astype(BF16), jnp.ones((), BF16), jnp.zeros((), BF16))
    block = block_ref[:, :, half * 6 * HALF:(half + 1) * 6 * HALF].reshape(HP * HD, 6 * HALF)
    return jnp.dot(block, pick, preferred_element_type=F32)


def _halfsums(x, row, left1):
    row_l = jnp.where(left1, row, 0.0)
    return (jnp.sum(x * row_l, axis=1, keepdims=True), jnp.sum(x * (row - row_l), axis=1, keepdims=True))


def _pair_rows(row):
    return [row[:, p * LW:(p + 1) * LW] for p in range(HP)]


def _store_columns(ref, p, t_mask, cols):
    for j, col in enumerate(cols):
        pltpu.store(ref.at[pl.ds(2 * p + j, 1)], jnp.broadcast_to(col[None], (1, HD, CHUNK)), mask=t_mask[None])


def _columns_to_rows(cols_ref, rows_ref):
    for p in range(HP):
        rows_ref[:, p * LW:(p + 1) * LW] = cols_ref[2 * p:2 * p + 2].reshape(LW, CHUNK).T


def _scan_fwd(r, w, k, a, b, v3):
    def kern(r_ref, w_ref, k_ref, a_ref, b_ref, v_ref, y_ref, ck_ref, s_ref, vb_ref, yc_ref):
        @pl.when(pl.program_id(0) == 0)
        def _():
            s_ref[...] = jnp.zeros_like(s_ref)

        lane = lax.broadcasted_iota(jnp.int32, (HD, CHUNK), 1)
        left = lane < HD
        left1 = lax.broadcasted_iota(jnp.int32, (1, LW), 1) < HD
        codes = _pick_codes()

        for half in range(CHUNK // HALF):
            ck_ref[half] = s_ref[...]
            vb_ref[...] = _column(v_ref, codes, half, jnp.int32(0))

            def step(i, carry):
                t = half * HALF + i
                row = lambda ref: _pair_rows(ref[pl.ds(t, 1), :])
                S = [s_ref[p] for p in range(HP)]
                sa = [jnp.where(left, *_halfsums(s, a, left1)) for s, a in zip(S, row(a_ref))]
                S = [s * w + c * b + vb_ref[pl.ds(p * HD, HD), :] * k
                     for p, (s, w, c, b, k) in enumerate(zip(S, row(w_ref), sa, row(b_ref), row(k_ref)))]
                for p, s in enumerate(S):
                    s_ref[p] = s
                for p, (s, r) in enumerate(zip(S, row(r_ref))):
                    _store_columns(yc_ref, p, lane == t, _halfsums(s, r, left1))
                vb_ref[...] = _column(v_ref, codes, half, i + 1)
                return carry

            lax.fori_loop(0, HALF, step, 0, unroll=8)
        _columns_to_rows(yc_ref, y_ref)

    rowblk = pl.BlockSpec((CHUNK, RW), lambda c: (c, 0))
    return pl.pallas_call(
        kern, name="rwkv_scan_fwd", grid=(NCHUNK,),
        in_specs=[rowblk] * 5 + [pl.BlockSpec((HP, HD, 6 * CHUNK), lambda c: (0, 0, c))],
        out_specs=[rowblk, pl.BlockSpec((CHUNK // HALF, HP, HD, LW), lambda c: (c, 0, 0, 0))],
        out_shape=[jax.ShapeDtypeStruct((T, RW), F32), jax.ShapeDtypeStruct((T // HALF, HP, HD, LW), F32)],
        scratch_shapes=[pltpu.VMEM((HP, HD, LW), F32), pltpu.VMEM((HP * HD, LW), F32),
                        pltpu.VMEM((HEADS, HD, CHUNK), F32)],
        compiler_params=_cp(("arbitrary",)),
    )(r, w, k, a, b, v3)


def _scan_bwd(r, w, k, a, b, v3, dy3, ck):
    NC = T // CHUNK

    def kern(r_ref, w_ref, k_ref, a_ref, b_ref, v_ref, dy_ref, ck_ref,
             dr_ref, dw_ref, dk_ref, da_ref, db_ref, dv_ref, ds_ref, sb_ref, vb_ref, sa_ref, dyb_ref, dvc_ref):
        @pl.when(pl.program_id(0) == 0)
        def _():
            ds_ref[...] = jnp.zeros_like(ds_ref)

        lane = lax.broadcasted_iota(jnp.int32, (HD, CHUNK), 1)
        left = lane < HD
        left1 = lax.broadcasted_iota(jnp.int32, (1, LW), 1) < HD
        codes = _pick_codes()

        def rowsum(x):
            return jnp.sum(x, axis=0, keepdims=True)

        for half in reversed(range(CHUNK // HALF)):
            base = half * HALF
            sb_ref[0] = ck_ref[half]

            vb_ref[0] = _column(v_ref, codes, half, jnp.int32(0))

            def replay(i, carry):
                t = base + i
                row = lambda ref: _pair_rows(ref[pl.ds(t, 1), :])
                S = [sb_ref[i, p] for p in range(HP)]
                sa = [jnp.where(left, *_halfsums(s, a, left1)) for s, a in zip(S, row(a_ref))]
                for p, (s, w, c, b, k) in enumerate(zip(S, row(w_ref), sa, row(b_ref), row(k_ref))):
                    sb_ref[i + 1, p] = s * w + c * b + vb_ref[i, pl.ds(p * HD, HD), :] * k
                    sa_ref[i, p] = c
                vb_ref[i + 1] = _column(v_ref, codes, half, i + 1)
                return carry

            lax.fori_loop(0, HALF, replay, 0, unroll=4)
            dyb_ref[...] = _column(dy_ref, codes, half, jnp.int32(HALF - 1))

            def back(ii, carry):
                i = HALF - 1 - ii
                t = base + i
                row = lambda ref: _pair_rows(ref[pl.ds(t, 1), :])
                a_r, b_r, k_r, w_r, r_r = row(a_ref), row(b_ref), row(k_ref), row(w_ref), row(r_ref)
                dys = [dyb_ref[pl.ds(p * HD, HD), :] for p in range(HP)]
                dyb_ref[...] = _column(dy_ref, codes, half, jnp.maximum(i - 1, 0))
                dr, dw, db, dk, da = [], [], [], [], []
                for p in range(HP):
                    Sp, dy = sb_ref[i, p], dys[p]
                    dS = ds_ref[p] + dy * r_r[p]
                    dr.append(rowsum(sb_ref[i + 1, p] * dy))
                    dw.append(rowsum(dS * Sp))
                    db.append(rowsum(dS * sa_ref[i, p]))
                    dk.append(rowsum(dS * vb_ref[i, pl.ds(p * HD, HD), :]))
                    dsa = jnp.where(left, *_halfsums(dS, b_r[p], left1))
                    _store_columns(dvc_ref, p, lane == t, _halfsums(dS, k_r[p], left1))
                    da.append(rowsum(Sp * dsa))
                    ds_ref[p] = dS * w_r[p] + dsa * a_r[p]
                for ref, pieces in ((dr_ref, dr), (dw_ref, dw), (db_ref, db), (dk_ref, dk), (da_ref, da)):
                    ref[pl.ds(t, 1), :] = jnp.concatenate(pieces, axis=1)
                return carry

            lax.fori_loop(0, HALF, back, 0, unroll=4)
        _columns_to_rows(dvc_ref, dv_ref)

    rowblk = pl.BlockSpec((CHUNK, RW), lambda c: (NC - 1 - c, 0))
    col3blk = pl.BlockSpec((HP, HD, 6 * CHUNK), lambda c: (0, 0, NC - 1 - c))
    rowshape = jax.ShapeDtypeStruct((T, RW), F32)
    return pl.pallas_call(
        kern, name="rwkv_scan_bwd", grid=(NC,),
        in_specs=[rowblk] * 5 + [col3blk, col3blk,
                                 pl.BlockSpec((CHUNK // HALF, HP, HD, LW), lambda c: (NC - 1 - c, 0, 0, 0))],
        out_specs=[rowblk] * 6, out_shape=[rowshape] * 6,
        scratch_shapes=[pltpu.VMEM((HP, HD, LW), F32), pltpu.VMEM((HALF + 1, HP, HD, LW), F32),
                        pltpu.VMEM((HALF + 1, HP * HD, LW), F32), pltpu.VMEM((HALF, HP, HD, LW), F32),
                        pltpu.VMEM((HP * HD, LW), F32), pltpu.VMEM((HEADS, HD, CHUNK), F32)],
        compiler_params=_cp(("arbitrary",)),
    )(r, w, k, a, b, v3, dy3, ck)


NT = (((1,), (1,)), ((), ()))
TN = (((0,), (0,)), ((), ()))
SCALE = HD ** -0.5
QKV_G = 3 * RW


def _attn_setup(g):
    dil = DILS[g]
    qkv = [pl.BlockSpec((T, LW), lambda hp, c=(g * QKV_G + s * RW) // LW: (0, c + hp)) for s in range(3)]
    tile = pl.BlockSpec((T, LW), lambda hp: (0, hp))
    bias = pl.BlockSpec((2, BLK, 2 * BLK), lambda hp: (hp, 0, 0))

    def blocks():
        for r in range(dil):
            for n in range(T // dil // BLK):
                rows = pl.ds(n * BLK * dil + r, BLK, stride=dil)
                keys = pl.ds((n - 1) * BLK * dil + r, 2 * BLK, stride=dil) if n else rows
                yield n, rows, keys

    return qkv, tile, bias, blocks


def _band(n):
    qi = lax.broadcasted_iota(jnp.int32, (BLK, 2 * BLK), 0)
    ki = lax.broadcasted_iota(jnp.int32, (BLK, 2 * BLK), 1)
    band = (ki >= qi) & (ki <= qi + BLK)
    return band if n else band[:, BLK:]


def _head_masks():
    lane = lax.broadcasted_iota(jnp.int32, (BLK, LW), 1)
    return lane < HD, [(lane < HD).astype(BF16), (lane >= HD).astype(BF16)]


def _attn_fwd(pq, bias, g):
    qkv, tile, bias_spec, blocks = _attn_setup(g)

    def kern(q_ref, k_ref, v_ref, b_ref, o_ref, l_ref):
        left, masks = _head_masks()
        for n, rows, keys in blocks():
            qb, kc, vc = q_ref[rows, :].astype(BF16), k_ref[keys, :].astype(BF16), v_ref[keys, :].astype(BF16)
            valid = _band(n)
            o, lse = [], []
            for j in range(2):
                bias_j = b_ref[j] if n else b_ref[j][:, BLK:]
                s = lax.dot_general(qb * masks[j], kc, NT, preferred_element_type=F32) * SCALE + bias_j
                s = jnp.where(valid, s, -jnp.inf)
                m = jnp.max(s, axis=1, keepdims=True)
                e = jnp.exp(s - m)
                den = jnp.sum(e, axis=1, keepdims=True)
                o.append(jnp.dot((e / den).astype(BF16), vc, preferred_element_type=F32))
                lse.append(m + jnp.log(den))
            o_ref[rows, :] = jnp.where(left, o[0], o[1])
            l_ref[rows, :] = jnp.where(left, lse[0], lse[1])

    shape = jax.ShapeDtypeStruct((T, RW), F32)
    return pl.pallas_call(
        kern, name=f"attn_fwd_{g}", grid=(HP,),
        in_specs=qkv + [bias_spec], out_specs=[tile, tile], out_shape=[shape, shape],
        compiler_params=_cp(("parallel",)),
    )(pq, pq, pq, bias)


def _attn_bwd(pq, bias, do, o, lse, dlse, g):
    qkv, tile, bias_spec, blocks = _attn_setup(g)

    def kern(q_ref, k_ref, v_ref, b_ref, do_ref, o_ref, l_ref, dl_ref, dq_ref, dk_ref, dv_ref, db_ref):
        left, masks = _head_masks()
        lane = lax.broadcasted_iota(jnp.int32, (BLK, LW), 1)
        dk_ref[...] = jnp.zeros_like(dk_ref)
        dv_ref[...] = jnp.zeros_like(dv_ref)
        db_ref[...] = jnp.zeros_like(db_ref)

        def column(tile_, j):
            return jnp.sum(jnp.where(lane == j * HD, tile_, 0.0), axis=1, keepdims=True)

        for n, rows, keys in blocks():
            qb, kc, vc = q_ref[rows, :].astype(BF16), k_ref[keys, :].astype(BF16), v_ref[keys, :].astype(BF16)
            dof, valid = do_ref[rows, :], _band(n)
            dob, prod = dof.astype(BF16), dof * o_ref[rows, :]
            dq = []
            for j in range(2):
                bias_j = b_ref[j] if n else b_ref[j][:, BLK:]
                delta = jnp.sum(prod * masks[j].astype(F32), axis=1, keepdims=True)
                qm, dom = qb * masks[j], dob * masks[j]
                s = lax.dot_general(qm, kc, NT, preferred_element_type=F32) * SCALE + bias_j
                p = jnp.where(valid, jnp.exp(s - column(l_ref[rows, :], j)), 0.0)
                dp = lax.dot_general(dom, vc, NT, preferred_element_type=F32)
                ds = p * (dp + (column(dl_ref[rows, :], j) - delta))
                if n:
                    db_ref[j] += ds
                else:
                    db_ref[j, :, BLK:] += ds
                dsb = (ds * SCALE).astype(BF16)
                dq.append(jnp.dot(dsb, kc, preferred_element_type=F32))
                dk_ref[keys, :] += lax.dot_general(dsb, qm, TN, preferred_element_type=F32)
                dv_ref[keys, :] += lax.dot_general(p.astype(BF16), dom, TN, preferred_element_type=F32)
            dq_ref[rows, :] = jnp.where(left, dq[0], dq[1])

    shape = jax.ShapeDtypeStruct((T, RW), F32)
    return pl.pallas_call(
        kern, name=f"attn_bwd_{g}", grid=(HP,),
        in_specs=qkv + [bias_spec] + [tile] * 4, out_specs=[tile] * 3 + [bias_spec],
        out_shape=[shape] * 3 + [jax.ShapeDtypeStruct((HEADS, BLK, 2 * BLK), F32)],
        compiler_params=_cp(("parallel",)),
    )(pq, pq, pq, bias, do, o, lse, dlse)


NBUCKET = 32
NPAIR = BLK * 2 * BLK


def _relbias_table(rbT, onehotT):
    def kern(rb_ref, oh_ref, out_ref):
        out_ref[0] = sum(jnp.dot(p, oh_ref[0], preferred_element_type=F32) for p in _split3(rb_ref[0]))

    return pl.pallas_call(
        kern, name="relbias_table", grid=(3,),
        in_specs=[pl.BlockSpec((1, HEADS, NBUCKET), lambda g: (g, 0, 0)),
                  pl.BlockSpec((1, NBUCKET, NPAIR), lambda g: (g, 0, 0))],
        out_specs=pl.BlockSpec((1, HEADS, NPAIR), lambda g: (g, 0, 0)),
        out_shape=jax.ShapeDtypeStruct((3, HEADS, NPAIR), F32),
        compiler_params=_cp(("parallel",)),
    )(rbT, onehotT)


def _relbias_grad(db, onehotT):
    nt = (((1,), (1,)), ((), ()))

    def kern(db_ref, oh_ref, out_ref):
        hi, mid, _ = _split3(db_ref[0])
        out_ref[0] = (lax.dot_general(hi, oh_ref[0], nt, preferred_element_type=F32)
                      + lax.dot_general(mid, oh_ref[0], nt, preferred_element_type=F32))

    return pl.pallas_call(
        kern, name="relbias_grad", grid=(3,),
        in_specs=[pl.BlockSpec((1, HEADS, NPAIR), lambda g: (g, 0, 0)),
                  pl.BlockSpec((1, NBUCKET, NPAIR), lambda g: (g, 0, 0))],
        out_specs=pl.BlockSpec((1, HEADS, NBUCKET), lambda g: (g, 0, 0)),
        out_shape=jax.ShapeDtypeStruct((3, HEADS, NBUCKET), F32),
        compiler_params=_cp(("parallel",)),
    )(db, onehotT)


def _adamw(w, g, m, v):
    m2 = ADAM_B1 * m + (1.0 - ADAM_B1) * g
    v2 = ADAM_B2 * v + (1.0 - ADAM_B2) * (g * g)
    m_hat = m2 / (1.0 - ADAM_B1 ** ADAM_STEP)
    v_hat = v2 / (1.0 - ADAM_B2 ** ADAM_STEP)
    return -ADAM_LR * (m_hat / (jnp.sqrt(v_hat) + ADAM_EPS) + ADAM_WD * w), m2, v2


def _ada_mod(c_all, ada_w, ada_b_loc):
    def kern(c_ref, w_ref, b_ref, o_ref):
        c = c_ref[...]
        cond = c * jax.nn.sigmoid(c)
        o_ref[0] = jnp.dot(cond, w_ref[0], precision=HI, preferred_element_type=F32) + b_ref[0]

    ncol = ada_w.shape[2]
    return pl.pallas_call(
        kern, name="ada_mod", grid=(2,),
        in_specs=[pl.BlockSpec((NDEV, D), lambda i: (0, 0)),
                  pl.BlockSpec((1, D, ncol), lambda i: (i, 0, 0)),
                  pl.BlockSpec((1, 1, ncol), lambda i: (i, 0, 0))],
        out_specs=pl.BlockSpec((1, NDEV, ncol), lambda i: (i, 0, 0)),
        out_shape=jax.ShapeDtypeStruct((2, NDEV, ncol), F32),
        compiler_params=_cp(("parallel",)),
    )(c_all, ada_w, ada_b_loc.reshape(2, 1, ncol))


def _ada_grad_adamw(cT_all, dmod_loc, w, m, v):
    ncol = w.shape[2]
    tr = 256

    def kern(c_ref, d_ref, w_ref, m_ref, v_ref, g_ref, dl_ref, m2_ref, v2_ref):
        c = c_ref[...]
        cond = c * jax.nn.sigmoid(c)
        g = jnp.dot(cond, d_ref[0], precision=HI, preferred_element_type=F32)
        dl, m2, v2 = _adamw(w_ref[0], g, m_ref[0], v_ref[0])
        g_ref[0], dl_ref[0], m2_ref[0], v2_ref[0] = g, dl, m2, v2

    big = pl.BlockSpec((1, tr, ncol), lambda i, j: (i, j, 0))
    shp = jax.ShapeDtypeStruct(w.shape, F32)
    return pl.pallas_call(
        kern, name="ada_grad_adamw", grid=(2, D // tr),
        in_specs=[pl.BlockSpec((tr, NDEV), lambda i, j: (j, 0)),
                  pl.BlockSpec((1, NDEV, ncol), lambda i, j: (i, 0, 0)), big, big, big],
        out_specs=[big] * 4, out_shape=[shp] * 4,
        compiler_params=_cp(("parallel", "parallel")),
    )(cT_all, dmod_loc, w, m, v)


def _sum_adamw(recv, w, m, v, name, tr):
    S = recv.shape[0]
    R, C = w.shape
    assert R % tr == 0 and recv.shape[1:] == (R, C)

    def kern(r_ref, w_ref, m_ref, v_ref, g_ref, dl_ref, m2_ref, v2_ref):
        g = r_ref[0].astype(F32)
        for s in range(1, S):
            g = g + r_ref[s].astype(F32)
        dl, m2, v2 = _adamw(w_ref[...], g, m_ref[...], v_ref[...])
        g_ref[...], dl_ref[...], m2_ref[...], v2_ref[...] = g, dl, m2, v2

    flat = pl.BlockSpec((tr, C), lambda i: (i, 0))
    shp = jax.ShapeDtypeStruct((R, C), F32)
    return pl.pallas_call(
        kern, name=name, grid=(R // tr,),
        in_specs=[pl.BlockSpec((S, tr, C), lambda i: (0, i, 0)), flat, flat, flat],
        out_specs=[flat] * 4, out_shape=[shp] * 4,
        compiler_params=_cp(("parallel",)),
    )(recv, w, m, v)


def _pack(arrs, dtype, row_mult):
    flat = jnp.concatenate([a.reshape(-1).astype(dtype) for a in arrs])
    flat = jnp.pad(flat, (0, -flat.shape[0] % (128 * row_mult)))
    return flat.reshape(-1, 128)


def _pack8(arrs, dtype, row_mult):
    flat = jnp.concatenate([a.reshape(NDEV, -1).astype(dtype) for a in arrs], axis=1)
    flat = jnp.pad(flat, ((0, 0), (0, -flat.shape[1] % (128 * row_mult))))
    return flat.reshape(NDEV, -1, 128)


def _unpack(buf, shapes, lead=()):
    flat = buf.reshape(lead + (-1,))
    out, off = [], 0
    for s in shapes:
        n = math.prod(s)
        out.append(flat[..., off:off + n].reshape(lead + tuple(s)))
        off += n
    return out


def _to_chunks(full, kind):
    if kind == "col":
        x = full.reshape(full.shape[:-1] + (NDEV, full.shape[-1] // NDEV))
        return jnp.moveaxis(x, -2, 0)
    x = full.reshape(full.shape[:-2] + (NDEV, full.shape[-2] // NDEV, full.shape[-1]))
    return jnp.moveaxis(x, -3, 0)


def _from_chunks(g8, kind):
    if kind == "col":
        x = jnp.moveaxis(g8, 0, -2)
        return x.reshape(x.shape[:-2] + (x.shape[-2] * x.shape[-1],))
    x = jnp.moveaxis(g8, 0, -3)
    return x.reshape(x.shape[:-3] + (x.shape[-3] * x.shape[-2], x.shape[-1]))


def _pad_pa(x):
    z = lambda n: jnp.zeros(x.shape[:-1] + (n,), x.dtype)
    return jnp.concatenate([x[..., :1600], z(64), x[..., 1600:1664], z(64), x[..., 1664:1824], z(96)], -1)


def _unpad_pa(x):
    return jnp.concatenate([x[..., :1600], x[..., 1664:1728], x[..., 1792:1952]], -1)


def _pad_rows(x, n):
    return jnp.pad(x, ((0, n - x.shape[0]), (0, 0)))


def _bucket_tables():
    qi = jnp.arange(BLK)[:, None]
    ki = jnp.arange(2 * BLK)[None, :]
    rel = BLK + qi - ki
    tabs = []
    for dil in DILS:
        dist = jnp.clip(rel, 0, BLK) * dil
        logd = jnp.log(jnp.maximum(dist, 1).astype(F32) / 16) / math.log(2048 / 16)
        large = jnp.minimum(16 + (logd * 16).astype(jnp.int32), 31)
        tabs.append(jnp.where(dist < 16, dist, large))
    return jnp.stack(tabs)


SHARDED = (("ln_g", "col"), ("ln_b", "col"), ("ab_w_in", "col"), ("rw_w_up", "col"), ("rw_a_up", "col"),
           ("rw_g_up", "col"), ("sc_conv_w", "col"), ("ab_w_out", "row"), ("dil_w_qkv", "col"),
           ("dil_w_out", "col"), ("mlp_w1", "col"), ("mlp_w2", "row"))
FIRST = ("ab_w_in",)
LATER = ("ab_w_out", "dil_w_qkv", "dil_w_out", "mlp_w1", "mlp_w2")
GATHER_BF16 = FIRST + LATER
GATHER_F32 = ("rw_w_up", "rw_a_up", "rw_g_up", "sc_conv_w", "ln_g", "ln_b")
REPLICATED = ("ada_b", "rw_mu", "rw_w0", "rw_a0", "rw_k_k", "rw_k_a", "rw_r_k", "rw_lnx_g", "rw_lnx_b", "rel_bias")
WEIGHTS = ("ada_w", "ada_b", "ln_g", "ln_b", "ab_w_in", "rw_mu", "rw_w0", "rw_w_up", "rw_a0", "rw_a_up",
           "rw_g_up", "rw_k_k", "rw_k_a", "rw_r_k", "rw_lnx_g", "rw_lnx_b", "sc_conv_w", "ab_w_out",
           "dil_w_qkv", "dil_w_out", "rel_bias", "mlp_w1", "mlp_w2")
FLAT_TILE = 512


def _local_step(x0, tgt, mod, W, P, later_weights, early_grads):
    row = lambda a: a.reshape(1, -1)
    W = dict(W)
    m6 = mod.reshape(2, 6, 1, D)
    sc = [m6[0, 1], m6[0, 4], m6[1, 1], m6[1, 4]]
    sh = [m6[0, 0], m6[0, 3], m6[1, 0], m6[1, 3]]
    gt = [m6[0, 2], m6[0, 5], m6[1, 2], m6[1, 5]]
    lng = [row(P["ln_g"][0, 0]), row(P["ln_g"][0, 1]), row(P["ln_g"][1, 0]), row(P["ln_g"][1, 1])]
    lnb = [row(P["ln_b"][0, 0]), row(P["ln_b"][0, 1]), row(P["ln_b"][1, 0]), row(P["ln_b"][1, 1])]
    E = jnp.kron(jnp.eye(HEADS, dtype=BF16), jnp.ones((HD, HD), BF16))

    def mod_body(r, p):
        return [r[0] * (1.0 + p[0]) + p[1]], []

    (u0,), _ = _rows("modulate", mod_body, [x0], [sc[0], sh[0]], [(D, BF16)])

    def post_fwd_body(r, p):
        xn, un = _post_ln_mod(r[0], r[1], *p)
        return [xn, un], []

    def post_fwd(s, x, y):
        (xn, un), _ = _rows(f"post_ln_{s}", post_fwd_body, [x, y],
                            [gt[s], lng[s], lnb[s], sc[s + 1], sh[s + 1]], [(D, F32), (D, BF16)])
        return xn, un

    def relu2(acc):
        a = jnp.maximum(acc, 0.0)
        return acc, a * a

    def relu2_bwd(acc, h):
        return (acc * (2.0 * jnp.maximum(h, 0.0)),)

    p = _mm("ab_in", u0, W["ab_w_in"])
    mu = _pad_pa(P["rw_mu"])
    mu_parts = [mu[:, :512], mu[:, 512:1024], mu[:, 1024:1536], mu[:, 1536:1664], mu[:, 1664:1792], mu[:, 1792:]]
    pre_params = mu_parts + [P["rw_w0"], _pad_rows(P["rw_w_up"], 128), P["rw_a0"], _pad_rows(P["rw_a_up"], 128),
                             _pad_rows(P["rw_g_up"], 256), P["rw_k_k"], P["rw_k_a"],
                             P["sc_conv_w"][0:1], P["sc_conv_w"][1:2], P["sc_conv_w"][2:3]]
    pieces = [(p, 512, 0), (p, 512, 1), (p, 512, 2), (p, 128, 12), (p, 128, 13), (p, 256, 7),
              (p, 512, 4), (p, 512, 5), (p, 512, 6)]
    shifted = [0, 1, 2, 3, 4, 5, 6, 8]
    pre_rows = pieces + [pieces[i] + ("prev",) for i in shifted]
    NPR = 19

    def pre_args(r):
        x, prev = r[:9], dict(zip(shifted, r[9:17]))
        down = lambda i, k: _shift_down(x[i], prev[i], k)
        return x[:6] + [down(i, 1) for i in range(6)] + x[6:9] + [down(6, 1), down(8, 1), down(6, 2), down(8, 2)]

    def pre_fwd_body(r, pp):
        return list(_pre_core(pp[0], *pre_args(r), *pp[1:])), []

    (r_, w_, kh_, v_, a_, b_, gate_, yb), _ = _rows(
        "rwkv_pre", pre_fwd_body, pre_rows, [E] + pre_params, [(RW, F32)] * 7 + [(RW, BF16)], tm=256)
    scan_in = [r_, w_, kh_, a_, b_, _cols3(v_, "rwkv_v_columns")]
    ysc, ck = _scan_fwd(*scan_in)
    post_params = [P["rw_lnx_g"], P["rw_lnx_b"], P["rw_r_k"].reshape(1, RW)]

    def postmix_fwd_body(r, pp):
        return [_post_core(pp[0], *r, *pp[1:])], []

    (ya,), _ = _rows("rwkv_post", postmix_fwd_body, [ysc, r_, kh_, v_, gate_], [E] + post_params,
                     [(RW, BF16)], tm=256)
    cat = jnp.concatenate([ya, yb], axis=1)
    W.update(later_weights(cat))
    y0 = _mm("ab_out", cat, W["ab_w_out"])
    x1, u1 = post_fwd(0, x0, y0)

    h1, a1 = _mm("mlp1_up_0", u1, W["mlp_w1"][0], out=(F32, BF16), epi=relu2)
    y1 = _mm("mlp1_down_0", a1, W["mlp_w2"][0])
    x2, u2 = post_fwd(1, x1, y1)

    pq = _mm("qkv", u2, W["dil_w_qkv"])
    onehotT = (_bucket_tables().reshape(3, 1, NPAIR) == jnp.arange(NBUCKET).reshape(1, NBUCKET, 1)).astype(BF16)
    rbT = P["rel_bias"].reshape(NBUCKET, 3, HEADS).transpose(1, 2, 0)
    bias = _relbias_table(rbT, onehotT).reshape(3, HEADS, BLK, 2 * BLK)
    og, lse = zip(*[_attn_fwd(pq, bias[g], g) for g in range(3)])

    def merge_fwd_body(r, pp):
        return [_merge_core(*r)], []

    (om,), _ = _rows("attn_merge", merge_fwd_body, list(og + lse), [], [(RW, BF16)])
    y2 = _mm("dil_out", om, W["dil_w_out"])
    x3, u3 = post_fwd(2, x2, y2)

    h3, a3 = _mm("mlp1_up_1", u3, W["mlp_w1"][1], out=(F32, BF16), epi=relu2)
    y3 = _mm("mlp1_down_1", a3, W["mlp_w2"][1])

    def last_body(r, pp):
        x, y, tg = r
        xn, vjp = jax.vjp(_post_ln, x, y, *pp)
        err = xn - tg
        dx, dy, dg, dlg, dlb = vjp(err * (1.0 / D))
        loss = jnp.full((1, 128), (0.5 / D) * jnp.sum(err * err), F32)
        return [dx, dy], [loss, dg, dlg, dlb]

    (dxp, dy3), (loss_acc, dg3, dlng3, dlnb3) = _rows(
        "final_ln_loss", last_body, [x3, y3, tgt], [gt[3], lng[3], lnb[3]],
        [(D, F32), (D, BF16)], [(1, 128), (1, D), (1, D), (1, D)])

    G = {}
    dsc, dsh, dgt = [None] * 4, [None] * 4, [None] * 4
    dlng, dlnb = [None] * 4, [None] * 4
    dgt[3], dlng[3], dlnb[3] = dg3, dlng3, dlnb3

    def mlp_bwd(i, u, h, a, dy):
        dh = _mm(f"mlp_dh_{i}", dy, W["mlp_w2"][i], tb=True, out=(BF16,), epi=relu2_bwd, extras=(h,))
        gw2 = _mm(f"mlp_dw2_{i}", a.T, dy)
        du = _mm(f"mlp_du_{i}", dh, W["mlp_w1"][i], tb=True)
        gw1 = _mm(f"mlp_dw1_{i}", u.T, dh)
        return du, gw1, gw2

    def post_bwd_body(r, pp):
        x, y, dxn, dun = r
        _, vjp = jax.vjp(_post_ln_mod, x, y, *pp)
        dx, dy, dg, dlg, dlb, dscn, dshn = vjp((dxn, dun))
        return [dx, dy], [dg, dlg, dlb, dscn, dshn]

    def post_bwd(s, x, y, dxn, dun):
        (dx, dy), (dgt[s], dlng[s], dlnb[s], dsc[s + 1], dsh[s + 1]) = _rows(
            f"post_ln_bwd_{s}", post_bwd_body, [x, y, dxn, dun],
            [gt[s], lng[s], lnb[s], sc[s + 1], sh[s + 1]], [(D, F32), (D, BF16)], [(1, D)] * 5)
        return dx, dy

    du3, gw1_1, gw2_1 = mlp_bwd(1, u3, h3, a3, dy3)
    dxp, dy2 = post_bwd(2, x2, y2, dxp, du3)

    G["dil_w_out"] = _mm("dil_out_dw", om.T, dy2)[None]
    do = _mm("dil_out_dx", dy2, W["dil_w_out"], tb=True)

    def merge_bwd_body(r, pp):
        _, vjp = jax.vjp(_merge_core, *r[:6])
        d = vjp(r[6])
        return list(d[:3]) + [_headsum(d[3 + g], pp[0]) for g in range(3)], []

    mb, _ = _rows("attn_merge_bwd", merge_bwd_body, list(og + lse) + [do], [E],
                  [(RW, F32)] * 6)
    back = [_attn_bwd(pq, bias[g], mb[g], og[g], lse[g], mb[3 + g], g) for g in range(3)]
    dpq = jnp.concatenate([t for dq, dk, dv, _ in back for t in (dq, dk, dv)], axis=1).astype(BF16)
    rb = _relbias_grad(jnp.stack([b[3] for b in back]).reshape(3, HEADS, NPAIR), onehotT)
    G["rel_bias"] = rb.transpose(2, 0, 1).reshape(NBUCKET, 3 * HEADS)
    G["dil_w_qkv"] = _mm("qkv_dw", u2.T, dpq)[None]
    du2 = _mm("qkv_dx", dpq, W["dil_w_qkv"], tb=True)
    dxp, dy1 = post_bwd(1, x1, y1, dxp, du2)

    du1, gw1_0, gw2_0 = mlp_bwd(0, u1, h1, a1, dy1)
    G["mlp_w1"] = jnp.stack([gw1_0, gw1_1])
    G["mlp_w2"] = jnp.stack([gw2_0, gw2_1])
    dxp, dy0 = post_bwd(0, x0, y0, dxp, du1)

    G["ab_w_out"] = _mm("ab_out_dw", cat.T, dy0)[None]
    dcat = _mm("ab_out_dx", dy0, W["ab_w_out"], tb=True)
    post_params = [post_params[0] + early_grads(G)] + post_params[1:]

    def postmix_bwd_body(r, pp):
        _, vjp = jax.vjp(functools.partial(_post_core, pp[0]), *r[:5], *pp[1:])
        d = vjp(r[5])
        return list(d[:5]), list(d[5:])

    (dysc, dr1, dkh1, dv1, dgate), (G["rw_lnx_g"], G["rw_lnx_b"], drk) = _rows(
        "rwkv_post_bwd", postmix_bwd_body, [ysc, r_, kh_, v_, gate_, (dcat, 512, 0)], [E] + post_params,
        [(RW, F32)] * 5, [(1, RW)] * 3, tm=256)
    G["rw_r_k"] = drk.reshape(1, HEADS, HD)
    dr2, dw2, dk2, da2, db2, dv2 = _scan_bwd(*scan_in, _cols3(dysc, "rwkv_dy_columns"), ck)

    def pre_bwd_body(r, pp):
        prim, ct = pre_args(r[:len(pre_rows)]), r[len(pre_rows):]
        _, vjp = jax.vjp(functools.partial(_pre_core, pp[0]), *prim, *pp[1:])
        cts = (ct[0] + ct[1], ct[2], ct[3] + ct[4], ct[5] + ct[6], ct[7], ct[8], ct[9], ct[10])
        d = vjp(cts)
        z = jnp.zeros_like(d[12])
        dp = jnp.concatenate([d[0], d[1], d[2], d[3], d[4], d[5], d[12], d[13], d[14]], axis=1)
        dp1 = jnp.concatenate([d[6], d[7], d[8], d[9], d[10], d[11], d[15], z, d[16]], axis=1)
        dp2 = jnp.concatenate([d[17], z, d[18]], axis=1)
        return [dp, dp1, dp2], list(d[NPR:])

    acc_shapes = [a.shape for a in pre_params]
    (dp, dp1, dp2), pacc = _rows(
        "rwkv_pre_bwd", pre_bwd_body,
        pre_rows + [dr1, dr2, dw2, dkh1, dk2, dv1, dv2, da2, db2, dgate, (dcat, 512, 1)],
        [E] + pre_params, [(PAB, F32), (PAB, F32), (PB, F32)], acc_shapes, tm=256)
    G["rw_mu"] = _unpad_pa(jnp.concatenate(pacc[:6], axis=1))
    G["rw_w0"], G["rw_a0"], G["rw_k_k"], G["rw_k_a"] = pacc[6], pacc[8], pacc[11], pacc[12]
    G["rw_w_up"] = pacc[7][None, :64]
    G["rw_a_up"] = pacc[9][None, :64]
    G["rw_g_up"] = pacc[10][None, :160]
    G["sc_conv_w"] = jnp.concatenate(pacc[13:16], axis=0)[None]

    def shift_merge_body(r, pp):
        d0, d1, d1_next, d2, d2_next = r
        d = d0 + _shift_up(d1, d1_next, 1)
        return [jnp.concatenate([d[:, :PA], d[:, PA:] + _shift_up(d2, d2_next, 2)], axis=1)], []

    (dpt,), _ = _rows("shift_merge", shift_merge_body,
                      [dp, dp1, (dp1, PAB, 0, "next"), dp2, (dp2, PB, 0, "next")], [], [(PAB, BF16)])
    gin = _mm("ab_in_dw", u0.T, dpt)
    G["ab_w_in"] = jnp.concatenate([_unpad_pa(gin[:, :PA]), gin[:, PA:]], axis=1)[None]
    du0 = _mm("ab_in_dx", dpt, W["ab_w_in"], tb=True)

    def mod_bwd_body(r, pp):
        du, dx, x = r
        return [dx + du * (1.0 + pp[0])], [jnp.sum(du * x, axis=0, keepdims=True), jnp.sum(du, axis=0, keepdims=True)]

    (grad_x,), (dsc[0], dsh[0]) = _rows("modulate_bwd", mod_bwd_body, [du0, dxp, x0], [sc[0]], [(D, F32)],
                                        [(1, D), (1, D)])

    G["ln_g"] = jnp.concatenate(dlng, axis=0).reshape(2, 2, D)
    G["ln_b"] = jnp.concatenate(dlnb, axis=0).reshape(2, 2, D)
    dmod = jnp.concatenate([dsh[0], dsc[0], dgt[0], dsh[1], dsc[1], dgt[1],
                            dsh[2], dsc[2], dgt[2], dsh[3], dsc[3], dgt[3]], axis=1).reshape(2, 6 * D)
    return loss_acc[0, 0], grad_x, dmod, G


def kernel(x, c, ada_w, ada_b, ln_g, ln_b, ab_w_in, rw_mu, rw_w0, rw_w_up, rw_a0, rw_a_up, rw_g_up, rw_k_k, rw_k_a, rw_r_k, rw_lnx_g, rw_lnx_b, sc_conv_w, ab_w_out, dil_w_qkv, dil_w_out, rel_bias, mlp_w1, mlp_w2, loss_target, m_ada_w, m_ada_b, m_ln_g, m_ln_b, m_ab_w_in, m_rw_mu, m_rw_w0, m_rw_w_up, m_rw_a0, m_rw_a_up, m_rw_g_up, m_rw_k_k, m_rw_k_a, m_rw_r_k, m_rw_lnx_g, m_rw_lnx_b, m_sc_conv_w, m_ab_w_out, m_dil_w_qkv, m_dil_w_out, m_rel_bias, m_mlp_w1, m_mlp_w2, v_ada_w, v_ada_b, v_ln_g, v_ln_b, v_ab_w_in, v_rw_mu, v_rw_w0, v_rw_w_up, v_rw_a0, v_rw_a_up, v_rw_g_up, v_rw_k_k, v_rw_k_a, v_rw_r_k, v_rw_lnx_g, v_rw_lnx_b, v_sc_conv_w, v_ab_w_out, v_dil_w_qkv, v_dil_w_out, v_rel_bias, v_mlp_w1, v_mlp_w2):
    w = dict(ada_w=ada_w, ada_b=ada_b, ln_g=ln_g, ln_b=ln_b, ab_w_in=ab_w_in, rw_mu=rw_mu, rw_w0=rw_w0,
             rw_w_up=rw_w_up, rw_a0=rw_a0, rw_a_up=rw_a_up, rw_g_up=rw_g_up, rw_k_k=rw_k_k, rw_k_a=rw_k_a,
             rw_r_k=rw_r_k, rw_lnx_g=rw_lnx_g, rw_lnx_b=rw_lnx_b, sc_conv_w=sc_conv_w, ab_w_out=ab_w_out,
             dil_w_qkv=dil_w_qkv, dil_w_out=dil_w_out, rel_bias=rel_bias, mlp_w1=mlp_w1, mlp_w2=mlp_w2)
    m = dict(ada_w=m_ada_w, ada_b=m_ada_b, ln_g=m_ln_g, ln_b=m_ln_b, ab_w_in=m_ab_w_in, rw_mu=m_rw_mu,
             rw_w0=m_rw_w0, rw_w_up=m_rw_w_up, rw_a0=m_rw_a0, rw_a_up=m_rw_a_up, rw_g_up=m_rw_g_up,
             rw_k_k=m_rw_k_k, rw_k_a=m_rw_k_a, rw_r_k=m_rw_r_k, rw_lnx_g=m_rw_lnx_g, rw_lnx_b=m_rw_lnx_b,
             sc_conv_w=m_sc_conv_w, ab_w_out=m_ab_w_out, dil_w_qkv=m_dil_w_qkv, dil_w_out=m_dil_w_out,
             rel_bias=m_rel_bias, mlp_w1=m_mlp_w1, mlp_w2=m_mlp_w2)
    v = dict(ada_w=v_ada_w, ada_b=v_ada_b, ln_g=v_ln_g, ln_b=v_ln_b, ab_w_in=v_ab_w_in, rw_mu=v_rw_mu,
             rw_w0=v_rw_w0, rw_w_up=v_rw_w_up, rw_a0=v_rw_a0, rw_a_up=v_rw_a_up, rw_g_up=v_rw_g_up,
             rw_k_k=v_rw_k_k, rw_k_a=v_rw_k_a, rw_r_k=v_rw_r_k, rw_lnx_g=v_rw_lnx_g, rw_lnx_b=v_rw_lnx_b,
             sc_conv_w=v_sc_conv_w, ab_w_out=v_ab_w_out, dil_w_qkv=v_dil_w_qkv, dil_w_out=v_dil_w_out,
             rel_bias=v_rel_bias, mlp_w1=v_mlp_w1, mlp_w2=v_mlp_w2)
    kinds = dict(SHARDED)
    me = 4 * lax.axis_index("x") + 2 * lax.axis_index("y") + lax.axis_index("c")
    ncol = ada_w.shape[2]

    small = _all_gather(_pack([c] + [w[n] for n in GATHER_F32], F32, 8), "gather_small")
    parts = _unpack(small, [c.shape] + [w[n].shape for n in GATHER_F32], (NDEV,))
    c_all = parts[0].reshape(NDEV, D)
    P = {n: _from_chunks(t, kinds[n]) for n, t in zip(GATHER_F32, parts[1:])}
    P = {n: (t if n in ("ln_g", "ln_b") else t[0]) for n, t in P.items()}
    for n in REPLICATED[1:]:
        P[n] = w[n]
    def full(n, t):
        t = _from_chunks(t, kinds[n])
        return t if n in ("mlp_w1", "mlp_w2") else t[0]

    parts = _all_gather_many([w[n].astype(BF16) for n in FIRST], "gather_first_weights")
    W = {n: full(n, t) for n, t in zip(FIRST, parts)}
    W["ab_w_in"] = jnp.concatenate([_pad_pa(W["ab_w_in"][:, :1824]), W["ab_w_in"][:, 1824:]], axis=1)

    ada_b_loc = lax.dynamic_slice(ada_b, (0, ncol * me), (2, ncol))
    mod_part = _ada_mod(c_all, ada_w, ada_b_loc)
    mod_all = _all_gather(mod_part.reshape(-1, 128), "gather_mod").reshape(NDEV, 2, NDEV, ncol)
    mod = lax.dynamic_index_in_dim(mod_all, me, axis=2, keepdims=False)
    mod = mod.transpose(1, 0, 2).reshape(2, 6 * D)

    behind = (mod[0, 0] * 0.0).astype(BF16)
    later = _exchange_start([w[n].astype(BF16) + (behind if n == LATER[0] else 0) for n in LATER], True,
                            "gather_later_weights_start")
    mod = mod + later[-1][0, 0]

    def later_weights(after):
        lands = _exchange_wait(later, True, after, "gather_later_weights_wait")
        return {n: full(n, t) for n, t in zip(LATER, lands)}

    sent = []

    def early_grads(G):
        sent.append(_exchange_start([_to_chunks(G[n], kinds[n]).astype(BF16) for n in LATER], False,
                                    "exchange_later_grads_start"))
        return sent[0][-1][0, 0]

    loss_part, grad_x, dmod, G = _local_step(x[0], loss_target[0], mod, W, P, later_weights, early_grads)
    G["ada_b"] = dmod

    rep_shapes = [w[n].shape for n in REPLICATED] + [(1,)]
    rep_all = _all_gather(_pack([G[n] for n in REPLICATED] + [loss_part], F32, 8), "gather_replicated_grads")
    pk = lambda d: _pack([d[n] for n in REPLICATED] + [jnp.zeros((1,), F32)], F32, 8)
    rep_out = _sum_adamw(rep_all, pk(w), pk(m), pk(v), "sum_adamw_replicated", rep_all.shape[1])
    loss = _unpack(rep_out[0], rep_shapes)[-1][0]
    rep_out = [dict(zip(REPLICATED, _unpack(o, rep_shapes))) for o in rep_out]

    dmod_all = _unpack(rep_all, [(2, 6 * D)], (NDEV,))[0]
    dmod_loc = lax.dynamic_slice(dmod_all, (0, 0, ncol * me), (NDEV, 2, ncol)).transpose(1, 0, 2)
    ada_out = _ada_grad_adamw(c_all.T, dmod_loc, ada_w, m_ada_w, v_ada_w)

    names = [n for n, _ in SHARDED if n not in GATHER_BF16]
    shard_shapes = [w[n].shape for n in names]
    chunks = _pack8([_to_chunks(G[n], kinds[n]) for n in names], F32, 8)
    recv = _all_to_all(chunks, "exchange_small_grads")
    pk = lambda d: _pack([d[n] for n in names], F32, 8)
    sh_out = _sum_adamw(recv, pk(w), pk(m), pk(v), "sum_adamw_small", recv.shape[1])
    sh_out = [dict(zip(names, _unpack(o, shard_shapes))) for o in sh_out]

    big_out = {}

    def update(n, contributions):
        cols = w[n].shape[-1]
        flat = lambda t: t.reshape(-1, cols)
        rows = flat(w[n]).shape[0]
        outs = _sum_adamw(contributions.reshape(-1, rows, cols), flat(w[n]), flat(m[n]), flat(v[n]),
                          f"sum_adamw_{n}", min(rows, 256))
        big_out[n] = [o.reshape(w[n].shape) for o in outs]

    ci = lax.axis_index("c")
    mine_l, sib_l = [], []
    for n in FIRST:
        g8 = _to_chunks(G[n], kinds[n])
        g42 = g8.reshape((4, 2) + g8.shape[1:])
        mine_l.append(lax.dynamic_index_in_dim(g42, ci, 1, keepdims=False))
        sib_l.append(lax.dynamic_index_in_dim(g42, 1 - ci, 1, keepdims=False))
    from_sib = _swap_sibling(sib_l, "swap_sibling_grads")

    def add2_body(r, pp):
        return [r[0] + r[1]], []

    partials = []
    for n, a, b in zip(FIRST, mine_l, from_sib):
        cols = a.shape[-1]
        (p,), _ = _rows(f"pair_sum_{n}", add2_body, [a.reshape(-1, cols), b.reshape(-1, cols)], [],
                        [(cols, BF16)], tm=512)
        partials.append(p.reshape(a.shape))
    for n, r in zip(FIRST, _exchange_chips(partials, "exchange_chip_grads")):
        update(n, r)

    for n, r in zip(LATER, _exchange_wait(sent[0], False, partials[0], "exchange_later_grads_wait")):
        update(n, r)
    sh_out = [{**d, **{n: big_out[n][i] for n in GATHER_BF16}} for i, d in enumerate(sh_out)]

    def pick(i, n):
        if n == "ada_w":
            return ada_out[i]
        return rep_out[i][n] if n in REPLICATED else sh_out[i][n]

    outs = [loss, grad_x[None]]
    for i in range(4):
        outs += [pick(i, n) for n in WEIGHTS]
    return tuple(outs)
```

```python
import functools
import math

import jax
import jax.numpy as jnp
from jax import lax
from jax.experimental import pallas as pl
from jax.experimental.pallas import tpu as pltpu

F32 = jnp.float32
BF16 = jnp.bfloat16
HI = lax.Precision.HIGHEST

NDEV = 8
T = 2048
D = 1024
DFF = 4096
HEADS = 8
HD = 64
RW = 512
PA = 2048
PB = 1536
PAB = PA + PB
QKV = 4608
DILS = (1, 4, 16)
BLK = 128
ALPHA = 4.0 ** 0.25
LN_EPS = 1e-5
GN_EPS = 64e-5
ADAM_LR, ADAM_B1, ADAM_B2, ADAM_EPS, ADAM_WD, ADAM_STEP = 0.001, 0.9, 0.999, 1e-8, 0.01, 10
VMEM_LIMIT = 56 * 1024 * 1024


def _cp(sem):
    return pltpu.CompilerParams(dimension_semantics=sem, vmem_limit_bytes=VMEM_LIMIT)


def _slot(px, py, pc):
    return 4 * px + 2 * py + pc


def _all_gather(x, name):
    R, C = x.shape

    def body(x_ref, out_ref, send_sems, recv_sems, local_sem):
        xi, yi, ci = lax.axis_index("x"), lax.axis_index("y"), lax.axis_index("c")
        me, sibling = (xi, yi, ci), (xi, yi, 1 - ci)
        chips = [(1 - xi, yi), (xi, 1 - yi), (1 - xi, 1 - yi)]

        def rows(px, py, pc):
            return out_ref.at[_slot(px, py, pc)]

        def copy(k, block, to, src=None):
            return pltpu.make_async_remote_copy(
                src_ref=rows(*block) if src is None else src, dst_ref=rows(*block),
                send_sem=send_sems.at[k], recv_sem=recv_sems.at[k],
                device_id=to, device_id_type=pl.DeviceIdType.MESH)

        mine = pltpu.make_async_copy(x_ref, rows(*me), local_sem)
        mine.start()
        first = [copy(0, me, sibling, src=x_ref)]
        first += [copy(1 + j, me, (*chip, ci), src=x_ref) for j, chip in enumerate(chips)]
        for cp in first:
            cp.start()
        passed = [copy(4 + j, (*chip, ci), sibling) for j, chip in enumerate(chips)]
        for j, chip in enumerate(chips):
            copy(1 + j, (*chip, ci), me).wait_recv()
            passed[j].start()
        copy(0, sibling, me).wait_recv()
        for j, chip in enumerate(chips):
            copy(4 + j, (*chip, 1 - ci), me).wait_recv()
        for cp in first + passed:
            cp.wait_send()
        mine.wait()

    return pl.pallas_call(
        body, name=name,
        out_shape=jax.ShapeDtypeStruct((NDEV, R, C), x.dtype),
        in_specs=[pl.BlockSpec(memory_space=pl.ANY)],
        out_specs=pl.BlockSpec(memory_space=pl.ANY),
        scratch_shapes=[pltpu.SemaphoreType.DMA((7,)), pltpu.SemaphoreType.DMA((7,)),
                        pltpu.SemaphoreType.DMA(())],
    )(x)


def _all_to_all(g, name):
    _, R, C = g.shape

    def body(g_ref, out_ref, send_sems, recv_sems, local_sem):
        xi, yi, ci = lax.axis_index("x"), lax.axis_index("y"), lax.axis_index("c")
        my_slot = _slot(xi, yi, ci)
        mine = pltpu.make_async_copy(g_ref.at[my_slot], out_ref.at[my_slot], local_sem)
        mine.start()
        copies = []
        for k in range(1, 8):
            px = 1 - xi if k & 4 else xi
            py = 1 - yi if k & 2 else yi
            pc = 1 - ci if k & 1 else ci
            peer_slot = _slot(px, py, pc)
            copies.append((
                pltpu.make_async_remote_copy(
                    src_ref=g_ref.at[peer_slot], dst_ref=out_ref.at[my_slot],
                    send_sem=send_sems.at[k - 1], recv_sem=recv_sems.at[k - 1],
                    device_id=(px, py, pc), device_id_type=pl.DeviceIdType.MESH),
                pltpu.make_async_remote_copy(
                    src_ref=g_ref.at[peer_slot], dst_ref=out_ref.at[peer_slot],
                    send_sem=send_sems.at[k - 1], recv_sem=recv_sems.at[k - 1],
                    device_id=(px, py, pc), device_id_type=pl.DeviceIdType.MESH)))
        for send, _ in copies:
            send.start()
        for _, recv in copies:
            recv.wait_recv()
        for send, _ in copies:
            send.wait_send()
        mine.wait()

    return pl.pallas_call(
        body, name=name,
        out_shape=jax.ShapeDtypeStruct((NDEV, R, C), g.dtype),
        in_specs=[pl.BlockSpec(memory_space=pl.ANY)],
        out_specs=pl.BlockSpec(memory_space=pl.ANY),
        scratch_shapes=[pltpu.SemaphoreType.DMA((7,)), pltpu.SemaphoreType.DMA((7,)),
                        pltpu.SemaphoreType.DMA(())],
    )(g)


def _my_slot():
    return _slot(lax.axis_index("x"), lax.axis_index("y"), lax.axis_index("c"))


def _put_own(buf, own, slot):
    return lax.dynamic_update_index_in_dim(buf, own, slot, 0)


def _hbm_call(body, name, ins, out_shapes, n_sems):
    anyspec = pl.BlockSpec(memory_space=pl.ANY)
    return pl.pallas_call(
        body, name=name, out_shape=out_shapes,
        in_specs=[anyspec] * len(ins), out_specs=[anyspec] * len(out_shapes),
        scratch_shapes=[pltpu.SemaphoreType.DMA(s) for s in n_sems],
    )(*ins)


def _all_gather_many(xs, name):
    n = len(xs)

    def body(*refs):
        x_refs, o_refs = refs[:n], refs[n:2 * n]
        send_sems, recv_sems = refs[2 * n:]
        xi, yi, ci = lax.axis_index("x"), lax.axis_index("y"), lax.axis_index("c")
        me, sibling = (xi, yi, ci), (xi, yi, 1 - ci)
        chips = [(1 - xi, yi), (xi, 1 - yi), (1 - xi, 1 - yi)]

        def copy(i, k, block, to, src=None):
            dst = o_refs[i].at[_slot(*block)]
            return pltpu.make_async_remote_copy(
                src_ref=dst if src is None else src, dst_ref=dst,
                send_sem=send_sems.at[i, k], recv_sem=recv_sems.at[i, k],
                device_id=to, device_id_type=pl.DeviceIdType.MESH)

        sends = []
        for i in range(n):
            sends += [copy(i, 1 + j, me, (*chip, ci), src=x_refs[i]) for j, chip in enumerate(chips)]
            sends.append(copy(i, 0, me, sibling, src=x_refs[i]))
        for cp in sends:
            cp.start()
        for j, chip in enumerate(chips):
            for i in range(n):
                copy(i, 1 + j, (*chip, ci), me).wait_recv()
                passed = copy(i, 4 + j, (*chip, ci), sibling)
                passed.start()
                sends.append(passed)
        for i in range(n):
            copy(i, 0, sibling, me).wait_recv()
            for j, chip in enumerate(chips):
                copy(i, 4 + j, (*chip, 1 - ci), me).wait_recv()
        for cp in sends:
            cp.wait_send()

    outs = _hbm_call(body, name, xs, [jax.ShapeDtypeStruct((NDEV,) + x.shape, x.dtype) for x in xs],
                     [(n, 7), (n, 7)])
    return [_put_own(o, x[None], _my_slot()) for o, x in zip(outs, xs)]


def _swap_sibling(gs, name):
    n = len(gs)

    def body(*refs):
        g_refs, o_refs = refs[:n], refs[n:2 * n]
        send_sems, recv_sems = refs[2 * n:]
        sibling = (lax.axis_index("x"), lax.axis_index("y"), 1 - lax.axis_index("c"))
        copies = [pltpu.make_async_remote_copy(
            src_ref=g_refs[i], dst_ref=o_refs[i], send_sem=send_sems.at[i], recv_sem=recv_sems.at[i],
            device_id=sibling, device_id_type=pl.DeviceIdType.MESH) for i in range(n)]
        for cp in copies:
            cp.start()
        for cp in copies:
            cp.wait_recv()
        for cp in copies:
            cp.wait_send()

    return _hbm_call(body, name, gs, [jax.ShapeDtypeStruct(g.shape, g.dtype) for g in gs], [(n,), (n,)])


def _exchange_chips(ps, name):
    n = len(ps)

    def body(*refs):
        p_refs, o_refs = refs[:n], refs[n:2 * n]
        send_sems, recv_sems = refs[2 * n:]
        xi, yi, ci = lax.axis_index("x"), lax.axis_index("y"), lax.axis_index("c")
        q_me = 2 * xi + yi
        sends, recvs = [], []
        for k in range(1, 4):
            px = 1 - xi if k & 2 else xi
            py = 1 - yi if k & 1 else yi
            q_peer = 2 * px + py
            for i in range(n):
                sends.append(pltpu.make_async_remote_copy(
                    src_ref=p_refs[i].at[q_peer], dst_ref=o_refs[i].at[q_me],
                    send_sem=send_sems.at[i, k - 1], recv_sem=recv_sems.at[i, k - 1],
                    device_id=(px, py, ci), device_id_type=pl.DeviceIdType.MESH))
                recvs.append(pltpu.make_async_remote_copy(
                    src_ref=p_refs[i].at[q_peer], dst_ref=o_refs[i].at[q_peer],
                    send_sem=send_sems.at[i, k - 1], recv_sem=recv_sems.at[i, k - 1],
                    device_id=(px, py, ci), device_id_type=pl.DeviceIdType.MESH))
        for cp in sends:
            cp.start()
        for cp in recvs:
            cp.wait_recv()
        for cp in sends:
            cp.wait_send()

    outs = _hbm_call(body, name, ps, [jax.ShapeDtypeStruct(p.shape, p.dtype) for p in ps], [(n, 3), (n, 3)])
    q_me = 2 * lax.axis_index("x") + lax.axis_index("y")
    return [_put_own(o, lax.dynamic_index_in_dim(p, q_me, 0, keepdims=True), q_me) for o, p in zip(outs, ps)]


def _peers(xi, yi, ci):
    return [(1 - xi if k & 4 else xi, 1 - yi if k & 2 else yi, 1 - ci if k & 1 else ci) for k in range(1, 8)]


def _direct_copy(src_refs, land_refs, send_sems, recv_sems, i, k, peer, my_slot, gather):
    src = src_refs[i] if gather else src_refs[i].at[_slot(*peer)]
    return pltpu.make_async_remote_copy(
        src_ref=src, dst_ref=land_refs[i].at[my_slot], send_sem=send_sems.at[7 * i + k], recv_sem=recv_sems.at[7 * i + k],
        device_id=peer, device_id_type=pl.DeviceIdType.MESH)


def _exchange_start(srcs, gather, name):
    n = len(srcs)
    lands = [lax.empty(((NDEV,) + s.shape) if gather else s.shape, s.dtype) for s in srcs]

    def body(*refs):
        s_refs, l_refs = refs[:n], refs[n:2 * n]
        send_sems, recv_sems = refs[2 * n], refs[2 * n + 1]
        token = refs[2 * n + 2 + 2 * n]
        xi, yi, ci = lax.axis_index("x"), lax.axis_index("y"), lax.axis_index("c")
        my_slot = _slot(xi, yi, ci)
        for k, peer in enumerate(_peers(xi, yi, ci)):
            for i in range(n):
                _direct_copy(s_refs, l_refs, send_sems, recv_sems, i, k, peer, my_slot, gather).start()
        token[...] = jnp.zeros_like(token)

    hbm = pl.BlockSpec(memory_space=pltpu.HBM)
    sem = pl.BlockSpec(memory_space=pltpu.SEMAPHORE)
    both = list(srcs) + lands
    return pl.pallas_call(
        body, name=name,
        out_shape=(pltpu.SemaphoreType.DMA((7 * n,)), pltpu.SemaphoreType.DMA((7 * n,)),
                   *[pltpu.HBM(t.shape, t.dtype) for t in both], jax.ShapeDtypeStruct((8, 128), F32)),
        in_specs=[hbm] * (2 * n),
        out_specs=(sem, sem, *[hbm] * (2 * n), pl.BlockSpec(memory_space=pltpu.VMEM)),
        input_output_aliases={i: 2 + i for i in range(2 * n)},
        compiler_params=pltpu.CompilerParams(has_side_effects=pltpu.SideEffectType.DATAFLOW_SIDE_EFFECTING),
    )(*[pltpu.with_memory_space_constraint(t, pltpu.HBM) for t in both])


def _exchange_wait(started, gather, after, name):
    send_sems, recv_sems, *thru, _ = started
    n = len(thru) // 2

    def body(*refs):
        s_refs, l_refs = refs[:n], refs[n:2 * n]
        send_sems, recv_sems = refs[2 * n], refs[2 * n + 1]
        xi, yi, ci = lax.axis_index("x"), lax.axis_index("y"), lax.axis_index("c")
        my_slot = _slot(xi, yi, ci)
        for k, peer in enumerate(_peers(xi, yi, ci)):
            for i in range(n):
                _direct_copy(s_refs, l_refs, send_sems, recv_sems, i, k, peer, my_slot, gather).wait_send()
                _direct_copy(s_refs, l_refs, send_sems, recv_sems, i, k, peer, _slot(*peer), gather).wait_recv()

    hbm = pl.BlockSpec(memory_space=pltpu.HBM)
    sem = pl.BlockSpec(memory_space=pltpu.SEMAPHORE)
    outs = pl.pallas_call(
        body, name=name,
        out_shape=tuple(pltpu.HBM(t.shape, t.dtype) for t in thru),
        in_specs=[hbm] * (2 * n) + [sem, sem, pl.BlockSpec(memory_space=pl.ANY)],
        out_specs=tuple([hbm] * (2 * n)),
        input_output_aliases={i: i for i in range(2 * n)},
        compiler_params=pltpu.CompilerParams(has_side_effects=pltpu.SideEffectType.DATAFLOW_SIDE_EFFECTING),
    )(*thru, send_sems, recv_sems, after)
    slot = _my_slot()
    own = [s[None] if gather else lax.dynamic_index_in_dim(s, slot, 0, keepdims=True) for s in outs[:n]]
    return [_put_own(land, o, slot) for land, o in zip(outs[n:], own)]


def _mm(name, a, b, tb=False, out=(F32,), epi=None, extras=(), tm=1024, tn=512, tk_cap=2048):
    M, K = a.shape
    N = b.shape[0] if tb else b.shape[1]
    tm, tn = min(tm, M), min(tn, N)
    tk = max(t for t in range(128, min(K, tk_cap) + 1, 128) if K % t == 0)
    assert M % tm == 0 and N % tn == 0 and K % tk == 0, (name, M, N, K)
    nk = K // tk
    ne, no = len(extras), len(out)
    dims = (((1,), (1 if tb else 0,)), ((), ()))

    def kern(*refs):
        a_ref, b_ref = refs[:2]
        e_refs = refs[2:2 + ne]
        o_refs = refs[2 + ne:2 + ne + no]

        def finish(acc):
            outs = epi(acc, *[e[...] for e in e_refs]) if epi is not None else (acc,)
            for o_ref, o in zip(o_refs, outs):
                o_ref[...] = o.astype(o_ref.dtype)

        part = lax.dot_general(a_ref[...], b_ref[...], dims, preferred_element_type=F32)
        if nk == 1:
            finish(part)
            return
        acc_ref = refs[-1]
        k = pl.program_id(2)

        @pl.when(k == 0)
        def _():
            acc_ref[...] = part

        @pl.when(k > 0)
        def _():
            acc_ref[...] += part

        @pl.when(k == nk - 1)
        def _():
            finish(acc_ref[...])

    b_spec = (pl.BlockSpec((tn, tk), lambda i, j, k: (j, k)) if tb
              else pl.BlockSpec((tk, tn), lambda i, j, k: (k, j)))
    tile = pl.BlockSpec((tm, tn), lambda i, j, k: (i, j))
    res = pl.pallas_call(
        kern, name=name, grid=(M // tm, N // tn, nk),
        in_specs=[pl.BlockSpec((tm, tk), lambda i, j, k: (i, k)), b_spec] + [tile] * ne,
        out_specs=[tile] * no,
        out_shape=[jax.ShapeDtypeStruct((M, N), dt) for dt in out],
        scratch_shapes=[pltpu.VMEM((tm, tn), F32)] if nk > 1 else [],
        compiler_params=_cp(("parallel", "parallel", "arbitrary")),
    )(a, b, *extras)
    return res[0] if no == 1 else res


HALO = 8


def _rows(name, body, rows, params, out_rows, out_accs=(), tm=256):
    views = [r if isinstance(r, tuple) else (r, r.shape[1], 0) for r in rows]
    n = views[0][0].shape[0]
    assert n % tm == 0 and tm % HALO == 0
    nr, npar, nor, noa = len(views), len(params), len(out_rows), len(out_accs)

    def row_spec(width, cb, halo=None):
        per, last = tm // HALO, n // HALO - 1
        if halo == "prev":
            return pl.BlockSpec((HALO, width), lambda i: (jnp.maximum(i * per - 1, 0), cb))
        if halo == "next":
            return pl.BlockSpec((HALO, width), lambda i: (jnp.minimum((i + 1) * per, last), cb))
        return pl.BlockSpec((tm, width), lambda i: (i, cb))

    def kern(*refs):
        r_refs = refs[:nr]
        p_refs = refs[nr:nr + npar]
        o_refs = refs[nr + npar:nr + npar + nor]
        a_refs = refs[nr + npar + nor:]
        outs, accs = body([r[...] for r in r_refs], [p[...] for p in p_refs])
        assert len(outs) == nor and len(accs) == noa, (name, len(outs), len(accs))
        for o_ref, o in zip(o_refs, outs):
            o_ref[...] = o.astype(o_ref.dtype)
        if noa:
            @pl.when(pl.program_id(0) == 0)
            def _():
                for a_ref in a_refs:
                    a_ref[...] = jnp.zeros_like(a_ref)

            for a_ref, a in zip(a_refs, accs):
                a_ref[...] += a.astype(F32)

    def whole(shape):
        nd = len(shape)
        return pl.BlockSpec(tuple(shape), lambda i, nd=nd: (0,) * nd)

    in_specs = [row_spec(*v[1:]) for v in views]
    in_specs += [whole(p.shape) for p in params]
    out_specs = [pl.BlockSpec((tm, c), lambda i: (i, 0)) for c, _ in out_rows]
    out_specs += [whole(s) for s in out_accs]
    out_shape = [jax.ShapeDtypeStruct((n, c), dt) for c, dt in out_rows]
    out_shape += [jax.ShapeDtypeStruct(tuple(s), F32) for s in out_accs]
    res = pl.pallas_call(
        kern, name=name, grid=(n // tm,), in_specs=in_specs, out_specs=out_specs,
        out_shape=out_shape, compiler_params=_cp(("arbitrary",)),
    )(*[v[0] for v in views], *params)
    return res[:nor], res[nor:]


def _shift_down(x, prev, k):
    head = jnp.where(pl.program_id(0) == 0, 0.0, pltpu.roll(prev, k, axis=0))
    row = lax.broadcasted_iota(jnp.int32, x.shape, 0)
    return jnp.where(row < k, jnp.tile(head, (x.shape[0] // HALO, 1)), pltpu.roll(x, k, axis=0))


def _shift_up(x, nxt, k):
    n = x.shape[0]
    tail = jnp.where(pl.program_id(0) == pl.num_programs(0) - 1, 0.0, pltpu.roll(nxt, HALO - k, axis=0))
    row = lax.broadcasted_iota(jnp.int32, x.shape, 0)
    return jnp.where(row >= n - k, jnp.tile(tail, (n // HALO, 1)), pltpu.roll(x, n - k, axis=0))


@jax.custom_vjp
def _headsum(x, e):
    return sum(jnp.dot(p, e, preferred_element_type=F32) for p in _split3(x))


_headsum.defvjp(lambda x, e: (_headsum(x, e), e), lambda e, ct: (_headsum(ct, e), None))


def _softplus(z):
    return jnp.maximum(z, 0.0) + jnp.log(1.0 + jnp.exp(jnp.minimum(z, -z)))


def _post_ln(x, y, g, lng, lnb):
    z = ALPHA * x + (1.0 + g) * y
    mu = jnp.mean(z, axis=-1, keepdims=True)
    zc = z - mu
    var = jnp.mean(zc * zc, axis=-1, keepdims=True)
    return zc * lax.rsqrt(var + LN_EPS) * lng + lnb


def _post_ln_mod(x, y, g, lng, lnb, scn, shn):
    xn = _post_ln(x, y, g, lng, lnb)
    return xn, xn * (1.0 + scn) + shn


def _pre_core(E, r_, k_, v_, wd_, ad_, gd_, r1, k1, v1, wd1, ad1, gd1, h, bg, cg, h1, cg1, h2, cg2,
              mu_r, mu_k, mu_v, mu_wd, mu_ad, mu_gd, w0, w_up, a0, a_up, g_up, k_k, k_a,
              cw0, cw1, cw2):
    def mix(x, x1, mu):
        return x + mu * (x1 - x)

    r, k, v = mix(r_, r1, mu_r), mix(k_, k1, mu_k), mix(v_, v1, mu_v)
    wd, ad, gd = mix(wd_, wd1, mu_wd), mix(ad_, ad1, mu_ad), mix(gd_, gd1, mu_gd)
    logw = -_softplus(-(w0 + jnp.dot(jnp.tanh(wd), w_up, preferred_element_type=F32))) - 0.5
    decay = jnp.exp(-jnp.exp(logw))
    iclr = jax.nn.sigmoid(a0 + jnp.dot(ad, a_up, preferred_element_type=F32))
    gate = jnp.dot(jax.nn.sigmoid(gd), g_up, preferred_element_type=F32)
    kk0 = k * k_k
    nrm = jnp.sqrt(_headsum(kk0 * kk0, E))
    kk = kk0 / jnp.maximum(nrm, 1e-12)
    kh = k * (1.0 + (iclr - 1.0) * k_a)
    yb = bg * (cw2 * (cg * h) + cw1 * (cg1 * h1) + cw0 * (cg2 * h2))
    return r, decay, kh, v, -kk, kk * iclr, gate, yb


def _post_core(E, y, r, kh, v, gate, lnx_g, lnx_b, rk):
    def seg(t):
        return _headsum(t, E)

    mean = seg(y) * (1.0 / HD)
    yc = y - mean
    var = seg(yc * yc) * (1.0 / HD)
    gn = yc * lax.rsqrt(var + GN_EPS) * lnx_g + lnx_b
    bonus = seg(r * kh * rk) * v
    return (gn + bonus) * gate


def _merge_core(o0, o1, o2, l0, l1, l2):
    m = jnp.maximum(jnp.maximum(l0, l1), l2)
    e0, e1, e2 = jnp.exp(l0 - m), jnp.exp(l1 - m), jnp.exp(l2 - m)
    den = e0 + e1 + e2
    return (e0 * o0 + e1 * o1 + e2 * o2) / den


CHUNK = 128
HALF = 64
HP = HEADS // 2
LW = 2 * HD
NCHUNK = T // CHUNK


def _split3(x):
    hi = x.astype(BF16)
    r1 = x - hi.astype(F32)
    mid = r1.astype(BF16)
    return hi, mid, (r1 - mid.astype(F32)).astype(BF16)


def _cols3(x, name):
    def kern(x_ref, o_ref):
        xt = x_ref[...].T
        left = lax.broadcasted_iota(jnp.int32, (HD, CHUNK), 1) < HALF
        for p in range(HP):
            a, b = xt[p * LW:p * LW + HD], xt[p * LW + HD:(p + 1) * LW]
            halves = [jnp.where(left, a, pltpu.roll(b, HALF, axis=1)), jnp.where(left, pltpu.roll(a, HALF, axis=1), b)]
            for h, tile in enumerate(halves):
                for j, part in enumerate(_split3(tile)):
                    o_ref[p, :, (3 * h + j) * LW:(3 * h + j + 1) * LW] = part

    return pl.pallas_call(
        kern, name=name, grid=(NCHUNK,),
        in_specs=[pl.BlockSpec((CHUNK, RW), lambda c: (c, 0))],
        out_specs=pl.BlockSpec((HP, HD, 6 * CHUNK), lambda c: (0, 0, c)),
        out_shape=jax.ShapeDtypeStruct((HP, HD, 6 * T), BF16),
        compiler_params=_cp(("parallel",)),
    )(x)


def _pick_codes():
    row = lax.broadcasted_iota(jnp.int32, (6 * HALF, LW), 0)
    col = lax.broadcasted_iota(jnp.int32, (6 * HALF, LW), 1)
    same = ((row & (LW - 1)) >= HALF) == (col >= HD)
    return jnp.where(same, row & (HALF - 1), -1).astype(BF16)


def _column(block_ref, codes, half, i):
    pick = jnp.where(codes == i.astype(BF16), jnp.ones((), BF16), jnp.zeros((), BF16))
    block = block_ref[:, :, half * 6 * HALF:(half + 1) * 6 * HALF].reshape(HP * HD, 6 * HALF)
    return jnp.dot(block, pick, preferred_element_type=F32)


def _halfsums(x, row, left1):
    row_l = jnp.where(left1, row, 0.0)
    return (jnp.sum(x * row_l, axis=1, keepdims=True), jnp.sum(x * (row - row_l), axis=1, keepdims=True))


def _pair_rows(row):
    return [row[:, p * LW:(p + 1) * LW] for p in range(HP)]


def _store_columns(ref, p, t_mask, cols):
    for j, col in enumerate(cols):
        pltpu.store(ref.at[pl.ds(2 * p + j, 1)], jnp.broadcast_to(col[None], (1, HD, CHUNK)), mask=t_mask[None])


def _columns_to_rows(cols_ref, rows_ref):
    for p in range(HP):
        rows_ref[:, p * LW:(p + 1) * LW] = cols_ref[2 * p:2 * p + 2].reshape(LW, CHUNK).T


NHALF = T // HALF
HALVES = CHUNK // HALF


def _scan_fwd(r, w, k, a, b, v3):
    def kern(r_ref, w_ref, k_ref, a_ref, b_ref, v_ref, y_ref, ck_ref, st_hbm, sa_hbm,
             s_ref, vb_ref, yc_ref, st_ref, sa_ref, sems):
        c = pl.program_id(0)

        @pl.when(c == 0)
        def _():
            s_ref[...] = jnp.zeros_like(s_ref)

        lane = lax.broadcasted_iota(jnp.int32, (HD, CHUNK), 1)
        left = lane < HD
        left1 = lax.broadcasted_iota(jnp.int32, (1, LW), 1) < HD
        codes = _pick_codes()

        def flush(slot, half_index):
            return [pltpu.make_async_copy(src.at[slot], dst.at[half_index], sems.at[j, slot])
                    for j, (src, dst) in enumerate(((st_ref, st_hbm), (sa_ref, sa_hbm)))]

        for half in range(HALVES):
            ck_ref[half] = s_ref[...]
            vb_ref[...] = _column(v_ref, codes, half, jnp.int32(0))

            @pl.when(c > 0)
            def _():
                for cp in flush(half, (c - 1) * HALVES + half):
                    cp.wait()

            def step(i, carry):
                t = half * HALF + i
                row = lambda ref: _pair_rows(ref[pl.ds(t, 1), :])
                S = [s_ref[p] for p in range(HP)]
                sa = [jnp.where(left, *_halfsums(s, a, left1)) for s, a in zip(S, row(a_ref))]
                S = [s * w + c_ * b + vb_ref[pl.ds(p * HD, HD), :] * k
                     for p, (s, w, c_, b, k) in enumerate(zip(S, row(w_ref), sa, row(b_ref), row(k_ref)))]
                for p, (s, c_) in enumerate(zip(S, sa)):
                    s_ref[p] = s
                    st_ref[half, i, p] = s
                    sa_ref[half, i, p] = c_
                for p, (s, r) in enumerate(zip(S, row(r_ref))):
                    _store_columns(yc_ref, p, lane == t, _halfsums(s, r, left1))
                vb_ref[...] = _column(v_ref, codes, half, i + 1)
                return carry

            lax.fori_loop(0, HALF, step, 0, unroll=8)
            for cp in flush(half, c * HALVES + half):
                cp.start()
        _columns_to_rows(yc_ref, y_ref)

        @pl.when(c == NCHUNK - 1)
        def _():
            for half in range(HALVES):
                for cp in flush(half, c * HALVES + half):
                    cp.wait()

    rowblk = pl.BlockSpec((CHUNK, RW), lambda c: (c, 0))
    saved = jax.ShapeDtypeStruct((NHALF, HALF, HP, HD, LW), F32)
    stage = pltpu.VMEM((HALVES, HALF, HP, HD, LW), F32)
    return pl.pallas_call(
        kern, name="rwkv_scan_fwd", grid=(NCHUNK,),
        in_specs=[rowblk] * 5 + [pl.BlockSpec((HP, HD, 6 * CHUNK), lambda c: (0, 0, c))],
        out_specs=[rowblk, pl.BlockSpec((HALVES, HP, HD, LW), lambda c: (c, 0, 0, 0)),
                   pl.BlockSpec(memory_space=pl.ANY), pl.BlockSpec(memory_space=pl.ANY)],
        out_shape=[jax.ShapeDtypeStruct((T, RW), F32), jax.ShapeDtypeStruct((NHALF, HP, HD, LW), F32), saved, saved],
        scratch_shapes=[pltpu.VMEM((HP, HD, LW), F32), pltpu.VMEM((HP * HD, LW), F32),
                        pltpu.VMEM((HEADS, HD, CHUNK), F32), stage, stage, pltpu.SemaphoreType.DMA((2, HALVES))],
        compiler_params=_cp(("arbitrary",)),
    )(r, w, k, a, b, v3)


def _scan_bwd(r, w, k, a, b, v3, dy3, ck, st, sa):
    def kern(r_ref, w_ref, k_ref, a_ref, b_ref, v_ref, dy_ref, ck_ref, st_hbm, sa_hbm,
             dr_ref, dw_ref, dk_ref, da_ref, db_ref, dv_ref, ds_ref, sb_ref, sa_ref, pick_ref, dvc_ref, sems):
        c = pl.program_id(0)
        chunk = NCHUNK - 1 - c

        @pl.when(c == 0)
        def _():
            ds_ref[...] = jnp.zeros_like(ds_ref)

        lane = lax.broadcasted_iota(jnp.int32, (HD, CHUNK), 1)
        left = lane < HD
        left1 = lax.broadcasted_iota(jnp.int32, (1, LW), 1) < HD
        codes = _pick_codes()

        def rowsum(x):
            return jnp.sum(x, axis=0, keepdims=True)

        def fetch(slot, half_index):
            return [pltpu.make_async_copy(st_hbm.at[half_index], sb_ref.at[slot, pl.ds(1, HALF)], sems.at[0, slot]),
                    pltpu.make_async_copy(sa_hbm.at[half_index], sa_ref.at[slot], sems.at[1, slot])]

        def picks(half, i):
            pick_ref[pl.ds(0, HP * HD), :] = _column(v_ref, codes, half, i)
            pick_ref[pl.ds(HP * HD, HP * HD), :] = _column(dy_ref, codes, half, i)

        @pl.when(c == 0)
        def _():
            for cp in fetch(HALVES - 1, chunk * HALVES + HALVES - 1):
                cp.start()

        for half in reversed(range(HALVES)):
            base = half * HALF
            for cp in fetch(half, chunk * HALVES + half):
                cp.wait()
            if half:
                for cp in fetch(half - 1, chunk * HALVES + half - 1):
                    cp.start()
            else:
                @pl.when(chunk > 0)
                def _():
                    for cp in fetch(HALVES - 1, chunk * HALVES - 1):
                        cp.start()
            sb_ref[half, 0] = ck_ref[half]
            picks(half, jnp.int32(HALF - 1))

            def back(ii, carry):
                i = HALF - 1 - ii
                t = base + i
                row = lambda ref: _pair_rows(ref[pl.ds(t, 1), :])
                a_r, b_r, k_r, w_r, r_r = row(a_ref), row(b_ref), row(k_ref), row(w_ref), row(r_ref)
                vs = [pick_ref[pl.ds(p * HD, HD), :] for p in range(HP)]
                dys = [pick_ref[pl.ds((HP + p) * HD, HD), :] for p in range(HP)]
                picks(half, jnp.maximum(i - 1, 0))
                dr, dw, db, dk, da = [], [], [], [], []
                for p in range(HP):
                    Sp, dy = sb_ref[half, i, p], dys[p]
                    dS = ds_ref[p] + dy * r_r[p]
                    dr.append(rowsum(sb_ref[half, i + 1, p] * dy))
                    dw.append(rowsum(dS * Sp))
                    db.append(rowsum(dS * sa_ref[half, i, p]))
                    dk.append(rowsum(dS * vs[p]))
                    dsa = jnp.where(left, *_halfsums(dS, b_r[p], left1))
                    _store_columns(dvc_ref, p, lane == t, _halfsums(dS, k_r[p], left1))
                    da.append(rowsum(Sp * dsa))
                    ds_ref[p] = dS * w_r[p] + dsa * a_r[p]
                for ref, pieces in ((dr_ref, dr), (dw_ref, dw), (db_ref, db), (dk_ref, dk), (da_ref, da)):
                    ref[pl.ds(t, 1), :] = jnp.concatenate(pieces, axis=1)
                return carry

            lax.fori_loop(0, HALF, back, 0, unroll=4)
        _columns_to_rows(dvc_ref, dv_ref)

    rowblk = pl.BlockSpec((CHUNK, RW), lambda c: (NCHUNK - 1 - c, 0))
    col3blk = pl.BlockSpec((HP, HD, 6 * CHUNK), lambda c: (0, 0, NCHUNK - 1 - c))
    rowshape = jax.ShapeDtypeStruct((T, RW), F32)
    return pl.pallas_call(
        kern, name="rwkv_scan_bwd", grid=(NCHUNK,),
        in_specs=[rowblk] * 5 + [col3blk, col3blk,
                                 pl.BlockSpec((HALVES, HP, HD, LW), lambda c: (NCHUNK - 1 - c, 0, 0, 0)),
                                 pl.BlockSpec(memory_space=pl.ANY), pl.BlockSpec(memory_space=pl.ANY)],
        out_specs=[rowblk] * 6, out_shape=[rowshape] * 6,
        scratch_shapes=[pltpu.VMEM((HP, HD, LW), F32), pltpu.VMEM((HALVES, HALF + 1, HP, HD, LW), F32),
                        pltpu.VMEM((HALVES, HALF, HP, HD, LW), F32), pltpu.VMEM((2 * HP * HD, LW), F32),
                        pltpu.VMEM((HEADS, HD, CHUNK), F32), pltpu.SemaphoreType.DMA((2, HALVES))],
        compiler_params=_cp(("arbitrary",)),
    )(r, w, k, a, b, v3, dy3, ck, st, sa)


NT = (((1,), (1,)), ((), ()))
TN = (((0,), (0,)), ((), ()))
SCALE = HD ** -0.5
QKV_G = 3 * RW


def _attn_setup(g):
    dil = DILS[g]
    qkv = [pl.BlockSpec((T, LW), lambda hp, c=(g * QKV_G + s * RW) // LW: (0, c + hp)) for s in range(3)]
    tile = pl.BlockSpec((T, LW), lambda hp: (0, hp))
    bias = pl.BlockSpec((2, BLK, 2 * BLK), lambda hp: (hp, 0, 0))

    def blocks():
        for r in range(dil):
            for n in range(T // dil // BLK):
                rows = pl.ds(n * BLK * dil + r, BLK, stride=dil)
                keys = pl.ds((n - 1) * BLK * dil + r, 2 * BLK, stride=dil) if n else rows
                yield n, rows, keys

    return qkv, tile, bias, blocks


def _band(n):
    qi = lax.broadcasted_iota(jnp.int32, (BLK, 2 * BLK), 0)
    ki = lax.broadcasted_iota(jnp.int32, (BLK, 2 * BLK), 1)
    band = (ki >= qi) & (ki <= qi + BLK)
    return band if n else band[:, BLK:]


def _head_masks():
    lane = lax.broadcasted_iota(jnp.int32, (BLK, LW), 1)
    return lane < HD, [(lane < HD).astype(BF16), (lane >= HD).astype(BF16)]


def _attn_fwd(pq, bias, g):
    qkv, tile, bias_spec, blocks = _attn_setup(g)

    def kern(q_ref, k_ref, v_ref, b_ref, o_ref, l_ref):
        left, masks = _head_masks()
        for n, rows, keys in blocks():
            qb, kc, vc = q_ref[rows, :].astype(BF16), k_ref[keys, :].astype(BF16), v_ref[keys, :].astype(BF16)
            valid = _band(n)
            o, lse = [], []
            for j in range(2):
                bias_j = b_ref[j] if n else b_ref[j][:, BLK:]
                s = lax.dot_general(qb * masks[j], kc, NT, preferred_element_type=F32) * SCALE + bias_j
                s = jnp.where(valid, s, -jnp.inf)
                m = jnp.max(s, axis=1, keepdims=True)
                e = jnp.exp(s - m)
                den = jnp.sum(e, axis=1, keepdims=True)
                o.append(jnp.dot((e / den).astype(BF16), vc, preferred_element_type=F32))
                lse.append(m + jnp.log(den))
            o_ref[rows, :] = jnp.where(left, o[0], o[1])
            l_ref[rows, :] = jnp.where(left, lse[0], lse[1])

    shape = jax.ShapeDtypeStruct((T, RW), F32)
    return pl.pallas_call(
        kern, name=f"attn_fwd_{g}", grid=(HP,),
        in_specs=qkv + [bias_spec], out_specs=[tile, tile], out_shape=[shape, shape],
        compiler_params=_cp(("parallel",)),
    )(pq, pq, pq, bias)


def _attn_bwd(pq, bias, do, o, lse, dlse, g):
    qkv, tile, bias_spec, blocks = _attn_setup(g)

    def kern(q_ref, k_ref, v_ref, b_ref, do_ref, o_ref, l_ref, dl_ref, dq_ref, dk_ref, dv_ref, db_ref):
        left, masks = _head_masks()
        lane = lax.broadcasted_iota(jnp.int32, (BLK, LW), 1)
        dk_ref[...] = jnp.zeros_like(dk_ref)
        dv_ref[...] = jnp.zeros_like(dv_ref)
        db_ref[...] = jnp.zeros_like(db_ref)

        def column(tile_, j):
            return jnp.sum(jnp.where(lane == j * HD, tile_, 0.0), axis=1, keepdims=True)

        for n, rows, keys in blocks():
            qb, kc, vc = q_ref[rows, :].astype(BF16), k_ref[keys, :].astype(BF16), v_ref[keys, :].astype(BF16)
            dof, valid = do_ref[rows, :], _band(n)
            dob, prod = dof.astype(BF16), dof * o_ref[rows, :]
            dq = []
            for j in range(2):
                bias_j = b_ref[j] if n else b_ref[j][:, BLK:]
                delta = jnp.sum(prod * masks[j].astype(F32), axis=1, keepdims=True)
                qm, dom = qb * masks[j], dob * masks[j]
                s = lax.dot_general(qm, kc, NT, preferred_element_type=F32) * SCALE + bias_j
                p = jnp.where(valid, jnp.exp(s - column(l_ref[rows, :], j)), 0.0)
                dp = lax.dot_general(dom, vc, NT, preferred_element_type=F32)
                ds = p * (dp + (column(dl_ref[rows, :], j) - delta))
                if n:
                    db_ref[j] += ds
                else:
                    db_ref[j, :, BLK:] += ds
                dsb = (ds * SCALE).astype(BF16)
                dq.append(jnp.dot(dsb, kc, preferred_element_type=F32))
                dk_ref[keys, :] += lax.dot_general(dsb, qm, TN, preferred_element_type=F32)
                dv_ref[keys, :] += lax.dot_general(p.astype(BF16), dom, TN, preferred_element_type=F32)
            dq_ref[rows, :] = jnp.where(left, dq[0], dq[1])

    shape = jax.ShapeDtypeStruct((T, RW), F32)
    return pl.pallas_call(
        kern, name=f"attn_bwd_{g}", grid=(HP,),
        in_specs=qkv + [bias_spec] + [tile] * 4, out_specs=[tile] * 3 + [bias_spec],
        out_shape=[shape] * 3 + [jax.ShapeDtypeStruct((HEADS, BLK, 2 * BLK), F32)],
        compiler_params=_cp(("parallel",)),
    )(pq, pq, pq, bias, do, o, lse, dlse)


NBUCKET = 32
NPAIR = BLK * 2 * BLK


def _relbias_table(rbT, onehotT):
    def kern(rb_ref, oh_ref, out_ref):
        out_ref[0] = sum(jnp.dot(p, oh_ref[0], preferred_element_type=F32) for p in _split3(rb_ref[0]))

    return pl.pallas_call(
        kern, name="relbias_table", grid=(3,),
        in_specs=[pl.BlockSpec((1, HEADS, NBUCKET), lambda g: (g, 0, 0)),
                  pl.BlockSpec((1, NBUCKET, NPAIR), lambda g: (g, 0, 0))],
        out_specs=pl.BlockSpec((1, HEADS, NPAIR), lambda g: (g, 0, 0)),
        out_shape=jax.ShapeDtypeStruct((3, HEADS, NPAIR), F32),
        compiler_params=_cp(("parallel",)),
    )(rbT, onehotT)


def _relbias_grad(db, onehotT):
    nt = (((1,), (1,)), ((), ()))

    def kern(db_ref, oh_ref, out_ref):
        hi, mid, _ = _split3(db_ref[0])
        out_ref[0] = (lax.dot_general(hi, oh_ref[0], nt, preferred_element_type=F32)
                      + lax.dot_general(mid, oh_ref[0], nt, preferred_element_type=F32))

    return pl.pallas_call(
        kern, name="relbias_grad", grid=(3,),
        in_specs=[pl.BlockSpec((1, HEADS, NPAIR), lambda g: (g, 0, 0)),
                  pl.BlockSpec((1, NBUCKET, NPAIR), lambda g: (g, 0, 0))],
        out_specs=pl.BlockSpec((1, HEADS, NBUCKET), lambda g: (g, 0, 0)),
        out_shape=jax.ShapeDtypeStruct((3, HEADS, NBUCKET), F32),
        compiler_params=_cp(("parallel",)),
    )(db, onehotT)


def _adamw(w, g, m, v):
    m2 = ADAM_B1 * m + (1.0 - ADAM_B1) * g
    v2 = ADAM_B2 * v + (1.0 - ADAM_B2) * (g * g)
    m_hat = m2 / (1.0 - ADAM_B1 ** ADAM_STEP)
    v_hat = v2 / (1.0 - ADAM_B2 ** ADAM_STEP)
    return -ADAM_LR * (m_hat / (jnp.sqrt(v_hat) + ADAM_EPS) + ADAM_WD * w), m2, v2


def _ada_mod(c_all, ada_w, ada_b_loc):
    def kern(c_ref, w_ref, b_ref, o_ref):
        c = c_ref[...]
        cond = c * jax.nn.sigmoid(c)
        o_ref[0] = jnp.dot(cond, w_ref[0], precision=HI, preferred_element_type=F32) + b_ref[0]

    ncol = ada_w.shape[2]
    return pl.pallas_call(
        kern, name="ada_mod", grid=(2,),
        in_specs=[pl.BlockSpec((NDEV, D), lambda i: (0, 0)),
                  pl.BlockSpec((1, D, ncol), lambda i: (i, 0, 0)),
                  pl.BlockSpec((1, 1, ncol), lambda i: (i, 0, 0))],
        out_specs=pl.BlockSpec((1, NDEV, ncol), lambda i: (i, 0, 0)),
        out_shape=jax.ShapeDtypeStruct((2, NDEV, ncol), F32),
        compiler_params=_cp(("parallel",)),
    )(c_all, ada_w, ada_b_loc.reshape(2, 1, ncol))


def _ada_grad_adamw(cT_all, dmod_loc, w, m, v):
    ncol = w.shape[2]
    tr = 256

    def kern(c_ref, d_ref, w_ref, m_ref, v_ref, g_ref, dl_ref, m2_ref, v2_ref):
        c = c_ref[...]
        cond = c * jax.nn.sigmoid(c)
        g = jnp.dot(cond, d_ref[0], precision=HI, preferred_element_type=F32)
        dl, m2, v2 = _adamw(w_ref[0], g, m_ref[0], v_ref[0])
        g_ref[0], dl_ref[0], m2_ref[0], v2_ref[0] = g, dl, m2, v2

    big = pl.BlockSpec((1, tr, ncol), lambda i, j: (i, j, 0))
    shp = jax.ShapeDtypeStruct(w.shape, F32)
    return pl.pallas_call(
        kern, name="ada_grad_adamw", grid=(2, D // tr),
        in_specs=[pl.BlockSpec((tr, NDEV), lambda i, j: (j, 0)),
                  pl.BlockSpec((1, NDEV, ncol), lambda i, j: (i, 0, 0)), big, big, big],
        out_specs=[big] * 4, out_shape=[shp] * 4,
        compiler_params=_cp(("parallel", "parallel")),
    )(cT_all, dmod_loc, w, m, v)


def _sum_adamw(recv, w, m, v, name, tr):
    S = recv.shape[0]
    R, C = w.shape
    assert R % tr == 0 and recv.shape[1:] == (R, C)

    def kern(r_ref, w_ref, m_ref, v_ref, g_ref, dl_ref, m2_ref, v2_ref):
        g = r_ref[0].astype(F32)
        for s in range(1, S):
            g = g + r_ref[s].astype(F32)
        dl, m2, v2 = _adamw(w_ref[...], g, m_ref[...], v_ref[...])
        g_ref[...], dl_ref[...], m2_ref[...], v2_ref[...] = g, dl, m2, v2

    flat = pl.BlockSpec((tr, C), lambda i: (i, 0))
    shp = jax.ShapeDtypeStruct((R, C), F32)
    return pl.pallas_call(
        kern, name=name, grid=(R // tr,),
        in_specs=[pl.BlockSpec((S, tr, C), lambda i: (0, i, 0)), flat, flat, flat],
        out_specs=[flat] * 4, out_shape=[shp] * 4,
        compiler_params=_cp(("parallel",)),
    )(recv, w, m, v)


def _pack(arrs, dtype, row_mult):
    flat = jnp.concatenate([a.reshape(-1).astype(dtype) for a in arrs])
    flat = jnp.pad(flat, (0, -flat.shape[0] % (128 * row_mult)))
    return flat.reshape(-1, 128)


def _pack8(arrs, dtype, row_mult):
    flat = jnp.concatenate([a.reshape(NDEV, -1).astype(dtype) for a in arrs], axis=1)
    flat = jnp.pad(flat, ((0, 0), (0, -flat.shape[1] % (128 * row_mult))))
    return flat.reshape(NDEV, -1, 128)


def _unpack(buf, shapes, lead=()):
    flat = buf.reshape(lead + (-1,))
    out, off = [], 0
    for s in shapes:
        n = math.prod(s)
        out.append(flat[..., off:off + n].reshape(lead + tuple(s)))
        off += n
    return out


def _to_chunks(full, kind):
    if kind == "col":
        x = full.reshape(full.shape[:-1] + (NDEV, full.shape[-1] // NDEV))
        return jnp.moveaxis(x, -2, 0)
    x = full.reshape(full.shape[:-2] + (NDEV, full.shape[-2] // NDEV, full.shape[-1]))
    return jnp.moveaxis(x, -3, 0)


def _from_chunks(g8, kind):
    if kind == "col":
        x = jnp.moveaxis(g8, 0, -2)
        return x.reshape(x.shape[:-2] + (x.shape[-2] * x.shape[-1],))
    x = jnp.moveaxis(g8, 0, -3)
    return x.reshape(x.shape[:-3] + (x.shape[-3] * x.shape[-2], x.shape[-1]))


def _pad_pa(x):
    z = lambda n: jnp.zeros(x.shape[:-1] + (n,), x.dtype)
    return jnp.concatenate([x[..., :1600], z(64), x[..., 1600:1664], z(64), x[..., 1664:1824], z(96)], -1)


def _unpad_pa(x):
    return jnp.concatenate([x[..., :1600], x[..., 1664:1728], x[..., 1792:1952]], -1)


def _pad_rows(x, n):
    return jnp.pad(x, ((0, n - x.shape[0]), (0, 0)))


def _bucket_tables():
    qi = jnp.arange(BLK)[:, None]
    ki = jnp.arange(2 * BLK)[None, :]
    rel = BLK + qi - ki
    tabs = []
    for dil in DILS:
        dist = jnp.clip(rel, 0, BLK) * dil
        logd = jnp.log(jnp.maximum(dist, 1).astype(F32) / 16) / math.log(2048 / 16)
        large = jnp.minimum(16 + (logd * 16).astype(jnp.int32), 31)
        tabs.append(jnp.where(dist < 16, dist, large))
    return jnp.stack(tabs)


SHARDED = (("ln_g", "col"), ("ln_b", "col"), ("ab_w_in", "col"), ("rw_w_up", "col"), ("rw_a_up", "col"),
           ("rw_g_up", "col"), ("sc_conv_w", "col"), ("ab_w_out", "row"), ("dil_w_qkv", "col"),
           ("dil_w_out", "col"), ("mlp_w1", "col"), ("mlp_w2", "row"))
FIRST = ("ab_w_in",)
LATER = ("ab_w_out", "dil_w_qkv", "dil_w_out", "mlp_w1", "mlp_w2")
GATHER_BF16 = FIRST + LATER
GATHER_F32 = ("rw_w_up", "rw_a_up", "rw_g_up", "sc_conv_w", "ln_g", "ln_b")
REPLICATED = ("ada_b", "rw_mu", "rw_w0", "rw_a0", "rw_k_k", "rw_k_a", "rw_r_k", "rw_lnx_g", "rw_lnx_b", "rel_bias")
WEIGHTS = ("ada_w", "ada_b", "ln_g", "ln_b", "ab_w_in", "rw_mu", "rw_w0", "rw_w_up", "rw_a0", "rw_a_up",
           "rw_g_up", "rw_k_k", "rw_k_a", "rw_r_k", "rw_lnx_g", "rw_lnx_b", "sc_conv_w", "ab_w_out",
           "dil_w_qkv", "dil_w_out", "rel_bias", "mlp_w1", "mlp_w2")
FLAT_TILE = 512


def _local_step(x0, tgt, mod, W, P, later_weights, early_grads):
    row = lambda a: a.reshape(1, -1)
    W = dict(W)
    m6 = mod.reshape(2, 6, 1, D)
    sc = [m6[0, 1], m6[0, 4], m6[1, 1], m6[1, 4]]
    sh = [m6[0, 0], m6[0, 3], m6[1, 0], m6[1, 3]]
    gt = [m6[0, 2], m6[0, 5], m6[1, 2], m6[1, 5]]
    lng = [row(P["ln_g"][0, 0]), row(P["ln_g"][0, 1]), row(P["ln_g"][1, 0]), row(P["ln_g"][1, 1])]
    lnb = [row(P["ln_b"][0, 0]), row(P["ln_b"][0, 1]), row(P["ln_b"][1, 0]), row(P["ln_b"][1, 1])]
    E = jnp.kron(jnp.eye(HEADS, dtype=BF16), jnp.ones((HD, HD), BF16))

    def mod_body(r, p):
        return [r[0] * (1.0 + p[0]) + p[1]], []

    (u0,), _ = _rows("modulate", mod_body, [x0], [sc[0], sh[0]], [(D, BF16)])

    def post_fwd_body(r, p):
        xn, un = _post_ln_mod(r[0], r[1], *p)
        return [xn, un], []

    def post_fwd(s, x, y):
        (xn, un), _ = _rows(f"post_ln_{s}", post_fwd_body, [x, y],
                            [gt[s], lng[s], lnb[s], sc[s + 1], sh[s + 1]], [(D, F32), (D, BF16)])
        return xn, un

    def relu2(acc):
        a = jnp.maximum(acc, 0.0)
        return acc, a * a

    def relu2_bwd(acc, h):
        return (acc * (2.0 * jnp.maximum(h, 0.0)),)

    p = _mm("ab_in", u0, W["ab_w_in"])
    mu = _pad_pa(P["rw_mu"])
    mu_parts = [mu[:, :512], mu[:, 512:1024], mu[:, 1024:1536], mu[:, 1536:1664], mu[:, 1664:1792], mu[:, 1792:]]
    pre_params = mu_parts + [P["rw_w0"], _pad_rows(P["rw_w_up"], 128), P["rw_a0"], _pad_rows(P["rw_a_up"], 128),
                             _pad_rows(P["rw_g_up"], 256), P["rw_k_k"], P["rw_k_a"],
                             P["sc_conv_w"][0:1], P["sc_conv_w"][1:2], P["sc_conv_w"][2:3]]
    pieces = [(p, 512, 0), (p, 512, 1), (p, 512, 2), (p, 128, 12), (p, 128, 13), (p, 256, 7),
              (p, 512, 4), (p, 512, 5), (p, 512, 6)]
    shifted = [0, 1, 2, 3, 4, 5, 6, 8]
    pre_rows = pieces + [pieces[i] + ("prev",) for i in shifted]
    NPR = 19

    def pre_args(r):
        x, prev = r[:9], dict(zip(shifted, r[9:17]))
        down = lambda i, k: _shift_down(x[i], prev[i], k)
        return x[:6] + [down(i, 1) for i in range(6)] + x[6:9] + [down(6, 1), down(8, 1), down(6, 2), down(8, 2)]

    def pre_fwd_body(r, pp):
        return list(_pre_core(pp[0], *pre_args(r), *pp[1:])), []

    (r_, w_, kh_, v_, a_, b_, gate_, yb), _ = _rows(
        "rwkv_pre", pre_fwd_body, pre_rows, [E] + pre_params, [(RW, F32)] * 7 + [(RW, BF16)], tm=256)
    scan_in = [r_, w_, kh_, a_, b_, _cols3(v_, "rwkv_v_columns")]
    ysc, *saved = _scan_fwd(*scan_in)
    post_params = [P["rw_lnx_g"], P["rw_lnx_b"], P["rw_r_k"].reshape(1, RW)]

    def postmix_fwd_body(r, pp):
        return [_post_core(pp[0], *r, *pp[1:])], []

    (ya,), _ = _rows("rwkv_post", postmix_fwd_body, [ysc, r_, kh_, v_, gate_], [E] + post_params,
                     [(RW, BF16)], tm=256)
    cat = jnp.concatenate([ya, yb], axis=1)
    W.update(later_weights(cat))
    y0 = _mm("ab_out", cat, W["ab_w_out"])
    x1, u1 = post_fwd(0, x0, y0)

    h1, a1 = _mm("mlp1_up_0", u1, W["mlp_w1"][0], out=(F32, BF16), epi=relu2)
    y1 = _mm("mlp1_down_0", a1, W["mlp_w2"][0])
    x2, u2 = post_fwd(1, x1, y1)

    pq = _mm("qkv", u2, W["dil_w_qkv"])
    onehotT = (_bucket_tables().reshape(3, 1, NPAIR) == jnp.arange(NBUCKET).reshape(1, NBUCKET, 1)).astype(BF16)
    rbT = P["rel_bias"].reshape(NBUCKET, 3, HEADS).transpose(1, 2, 0)
    bias = _relbias_table(rbT, onehotT).reshape(3, HEADS, BLK, 2 * BLK)
    og, lse = zip(*[_attn_fwd(pq, bias[g], g) for g in range(3)])

    def merge_fwd_body(r, pp):
        return [_merge_core(*r)], []

    (om,), _ = _rows("attn_merge", merge_fwd_body, list(og + lse), [], [(RW, BF16)])
    y2 = _mm("dil_out", om, W["dil_w_out"])
    x3, u3 = post_fwd(2, x2, y2)

    h3, a3 = _mm("mlp1_up_1", u3, W["mlp_w1"][1], out=(F32, BF16), epi=relu2)
    y3 = _mm("mlp1_down_1", a3, W["mlp_w2"][1])

    def last_body(r, pp):
        x, y, tg = r
        xn, vjp = jax.vjp(_post_ln, x, y, *pp)
        err = xn - tg
        dx, dy, dg, dlg, dlb = vjp(err * (1.0 / D))
        loss = jnp.full((1, 128), (0.5 / D) * jnp.sum(err * err), F32)
        return [dx, dy], [loss, dg, dlg, dlb]

    (dxp, dy3), (loss_acc, dg3, dlng3, dlnb3) = _rows(
        "final_ln_loss", last_body, [x3, y3, tgt], [gt[3], lng[3], lnb[3]],
        [(D, F32), (D, BF16)], [(1, 128), (1, D), (1, D), (1, D)])

    G = {}
    dsc, dsh, dgt = [None] * 4, [None] * 4, [None] * 4
    dlng, dlnb = [None] * 4, [None] * 4
    dgt[3], dlng[3], dlnb[3] = dg3, dlng3, dlnb3

    def mlp_bwd(i, u, h, a, dy):
        dh = _mm(f"mlp_dh_{i}", dy, W["mlp_w2"][i], tb=True, out=(BF16,), epi=relu2_bwd, extras=(h,))
        gw2 = _mm(f"mlp_dw2_{i}", a.T, dy)
        du = _mm(f"mlp_du_{i}", dh, W["mlp_w1"][i], tb=True)
        gw1 = _mm(f"mlp_dw1_{i}", u.T, dh)
        return du, gw1, gw2

    def post_bwd_body(r, pp):
        x, y, dxn, dun = r
        _, vjp = jax.vjp(_post_ln_mod, x, y, *pp)
        dx, dy, dg, dlg, dlb, dscn, dshn = vjp((dxn, dun))
        return [dx, dy], [dg, dlg, dlb, dscn, dshn]

    def post_bwd(s, x, y, dxn, dun):
        (dx, dy), (dgt[s], dlng[s], dlnb[s], dsc[s + 1], dsh[s + 1]) = _rows(
            f"post_ln_bwd_{s}", post_bwd_body, [x, y, dxn, dun],
            [gt[s], lng[s], lnb[s], sc[s + 1], sh[s + 1]], [(D, F32), (D, BF16)], [(1, D)] * 5)
        return dx, dy

    du3, gw1_1, gw2_1 = mlp_bwd(1, u3, h3, a3, dy3)
    dxp, dy2 = post_bwd(2, x2, y2, dxp, du3)

    G["dil_w_out"] = _mm("dil_out_dw", om.T, dy2)[None]
    do = _mm("dil_out_dx", dy2, W["dil_w_out"], tb=True)

    def merge_bwd_body(r, pp):
        _, vjp = jax.vjp(_merge_core, *r[:6])
        d = vjp(r[6])
        return list(d[:3]) + [_headsum(d[3 + g], pp[0]) for g in range(3)], []

    mb, _ = _rows("attn_merge_bwd", merge_bwd_body, list(og + lse) + [do], [E],
                  [(RW, F32)] * 6)
    back = [_attn_bwd(pq, bias[g], mb[g], og[g], lse[g], mb[3 + g], g) for g in range(3)]
    dpq = jnp.concatenate([t for dq, dk, dv, _ in back for t in (dq, dk, dv)], axis=1).astype(BF16)
    rb = _relbias_grad(jnp.stack([b[3] for b in back]).reshape(3, HEADS, NPAIR), onehotT)
    G["rel_bias"] = rb.transpose(2, 0, 1).reshape(NBUCKET, 3 * HEADS)
    G["dil_w_qkv"] = _mm("qkv_dw", u2.T, dpq)[None]
    du2 = _mm("qkv_dx", dpq, W["dil_w_qkv"], tb=True)
    dxp, dy1 = post_bwd(1, x1, y1, dxp, du2)

    du1, gw1_0, gw2_0 = mlp_bwd(0, u1, h1, a1, dy1)
    G["mlp_w1"] = jnp.stack([gw1_0, gw1_1])
    G["mlp_w2"] = jnp.stack([gw2_0, gw2_1])
    dxp, dy0 = post_bwd(0, x0, y0, dxp, du1)

    G["ab_w_out"] = _mm("ab_out_dw", cat.T, dy0)[None]
    dcat = _mm("ab_out_dx", dy0, W["ab_w_out"], tb=True)
    post_params = [post_params[0] + early_grads(G)] + post_params[1:]

    def postmix_bwd_body(r, pp):
        _, vjp = jax.vjp(functools.partial(_post_core, pp[0]), *r[:5], *pp[1:])
        d = vjp(r[5])
        return list(d[:5]), list(d[5:])

    (dysc, dr1, dkh1, dv1, dgate), (G["rw_lnx_g"], G["rw_lnx_b"], drk) = _rows(
        "rwkv_post_bwd", postmix_bwd_body, [ysc, r_, kh_, v_, gate_, (dcat, 512, 0)], [E] + post_params,
        [(RW, F32)] * 5, [(1, RW)] * 3, tm=256)
    G["rw_r_k"] = drk.reshape(1, HEADS, HD)
    dr2, dw2, dk2, da2, db2, dv2 = _scan_bwd(*scan_in, _cols3(dysc, "rwkv_dy_columns"), *saved)

    def pre_bwd_body(r, pp):
        prim, ct = pre_args(r[:len(pre_rows)]), r[len(pre_rows):]
        _, vjp = jax.vjp(functools.partial(_pre_core, pp[0]), *prim, *pp[1:])
        cts = (ct[0] + ct[1], ct[2], ct[3] + ct[4], ct[5] + ct[6], ct[7], ct[8], ct[9], ct[10])
        d = vjp(cts)
        z = jnp.zeros_like(d[12])
        dp = jnp.concatenate([d[0], d[1], d[2], d[3], d[4], d[5], d[12], d[13], d[14]], axis=1)
        dp1 = jnp.concatenate([d[6], d[7], d[8], d[9], d[10], d[11], d[15], z, d[16]], axis=1)
        dp2 = jnp.concatenate([d[17], z, d[18]], axis=1)
        return [dp, dp1, dp2], list(d[NPR:])

    acc_shapes = [a.shape for a in pre_params]
    (dp, dp1, dp2), pacc = _rows(
        "rwkv_pre_bwd", pre_bwd_body,
        pre_rows + [dr1, dr2, dw2, dkh1, dk2, dv1, dv2, da2, db2, dgate, (dcat, 512, 1)],
        [E] + pre_params, [(PAB, F32), (PAB, F32), (PB, F32)], acc_shapes, tm=256)
    G["rw_mu"] = _unpad_pa(jnp.concatenate(pacc[:6], axis=1))
    G["rw_w0"], G["rw_a0"], G["rw_k_k"], G["rw_k_a"] = pacc[6], pacc[8], pacc[11], pacc[12]
    G["rw_w_up"] = pacc[7][None, :64]
    G["rw_a_up"] = pacc[9][None, :64]
    G["rw_g_up"] = pacc[10][None, :160]
    G["sc_conv_w"] = jnp.concatenate(pacc[13:16], axis=0)[None]

    def shift_merge_body(r, pp):
        d0, d1, d1_next, d2, d2_next = r
        d = d0 + _shift_up(d1, d1_next, 1)
        return [jnp.concatenate([d[:, :PA], d[:, PA:] + _shift_up(d2, d2_next, 2)], axis=1)], []

    (dpt,), _ = _rows("shift_merge", shift_merge_body,
                      [dp, dp1, (dp1, PAB, 0, "next"), dp2, (dp2, PB, 0, "next")], [], [(PAB, BF16)])
    gin = _mm("ab_in_dw", u0.T, dpt)
    G["ab_w_in"] = jnp.concatenate([_unpad_pa(gin[:, :PA]), gin[:, PA:]], axis=1)[None]
    du0 = _mm("ab_in_dx", dpt, W["ab_w_in"], tb=True)

    def mod_bwd_body(r, pp):
        du, dx, x = r
        return [dx + du * (1.0 + pp[0])], [jnp.sum(du * x, axis=0, keepdims=True), jnp.sum(du, axis=0, keepdims=True)]

    (grad_x,), (dsc[0], dsh[0]) = _rows("modulate_bwd", mod_bwd_body, [du0, dxp, x0], [sc[0]], [(D, F32)],
                                        [(1, D), (1, D)])

    G["ln_g"] = jnp.concatenate(dlng, axis=0).reshape(2, 2, D)
    G["ln_b"] = jnp.concatenate(dlnb, axis=0).reshape(2, 2, D)
    dmod = jnp.concatenate([dsh[0], dsc[0], dgt[0], dsh[1], dsc[1], dgt[1],
                            dsh[2], dsc[2], dgt[2], dsh[3], dsc[3], dgt[3]], axis=1).reshape(2, 6 * D)
    return loss_acc[0, 0], grad_x, dmod, G


def kernel(x, c, ada_w, ada_b, ln_g, ln_b, ab_w_in, rw_mu, rw_w0, rw_w_up, rw_a0, rw_a_up, rw_g_up, rw_k_k, rw_k_a, rw_r_k, rw_lnx_g, rw_lnx_b, sc_conv_w, ab_w_out, dil_w_qkv, dil_w_out, rel_bias, mlp_w1, mlp_w2, loss_target, m_ada_w, m_ada_b, m_ln_g, m_ln_b, m_ab_w_in, m_rw_mu, m_rw_w0, m_rw_w_up, m_rw_a0, m_rw_a_up, m_rw_g_up, m_rw_k_k, m_rw_k_a, m_rw_r_k, m_rw_lnx_g, m_rw_lnx_b, m_sc_conv_w, m_ab_w_out, m_dil_w_qkv, m_dil_w_out, m_rel_bias, m_mlp_w1, m_mlp_w2, v_ada_w, v_ada_b, v_ln_g, v_ln_b, v_ab_w_in, v_rw_mu, v_rw_w0, v_rw_w_up, v_rw_a0, v_rw_a_up, v_rw_g_up, v_rw_k_k, v_rw_k_a, v_rw_r_k, v_rw_lnx_g, v_rw_lnx_b, v_sc_conv_w, v_ab_w_out, v_dil_w_qkv, v_dil_w_out, v_rel_bias, v_mlp_w1, v_mlp_w2):
    w = dict(ada_w=ada_w, ada_b=ada_b, ln_g=ln_g, ln_b=ln_b, ab_w_in=ab_w_in, rw_mu=rw_mu, rw_w0=rw_w0,
             rw_w_up=rw_w_up, rw_a0=rw_a0, rw_a_up=rw_a_up, rw_g_up=rw_g_up, rw_k_k=rw_k_k, rw_k_a=rw_k_a,
             rw_r_k=rw_r_k, rw_lnx_g=rw_lnx_g, rw_lnx_b=rw_lnx_b, sc_conv_w=sc_conv_w, ab_w_out=ab_w_out,
             dil_w_qkv=dil_w_qkv, dil_w_out=dil_w_out, rel_bias=rel_bias, mlp_w1=mlp_w1, mlp_w2=mlp_w2)
    m = dict(ada_w=m_ada_w, ada_b=m_ada_b, ln_g=m_ln_g, ln_b=m_ln_b, ab_w_in=m_ab_w_in, rw_mu=m_rw_mu,
             rw_w0=m_rw_w0, rw_w_up=m_rw_w_up, rw_a0=m_rw_a0, rw_a_up=m_rw_a_up, rw_g_up=m_rw_g_up,
             rw_k_k=m_rw_k_k, rw_k_a=m_rw_k_a, rw_r_k=m_rw_r_k, rw_lnx_g=m_rw_lnx_g, rw_lnx_b=m_rw_lnx_b,
             sc_conv_w=m_sc_conv_w, ab_w_out=m_ab_w_out, dil_w_qkv=m_dil_w_qkv, dil_w_out=m_dil_w_out,
             rel_bias=m_rel_bias, mlp_w1=m_mlp_w1, mlp_w2=m_mlp_w2)
    v = dict(ada_w=v_ada_w, ada_b=v_ada_b, ln_g=v_ln_g, ln_b=v_ln_b, ab_w_in=v_ab_w_in, rw_mu=v_rw_mu,
             rw_w0=v_rw_w0, rw_w_up=v_rw_w_up, rw_a0=v_rw_a0, rw_a_up=v_rw_a_up, rw_g_up=v_rw_g_up,
             rw_k_k=v_rw_k_k, rw_k_a=v_rw_k_a, rw_r_k=v_rw_r_k, rw_lnx_g=v_rw_lnx_g, rw_lnx_b=v_rw_lnx_b,
             sc_conv_w=v_sc_conv_w, ab_w_out=v_ab_w_out, dil_w_qkv=v_dil_w_qkv, dil_w_out=v_dil_w_out,
             rel_bias=v_rel_bias, mlp_w1=v_mlp_w1, mlp_w2=v_mlp_w2)
    kinds = dict(SHARDED)
    me = 4 * lax.axis_index("x") + 2 * lax.axis_index("y") + lax.axis_index("c")
    ncol = ada_w.shape[2]

    small = _all_gather(_pack([c] + [w[n] for n in GATHER_F32], F32, 8), "gather_small")
    parts = _unpack(small, [c.shape] + [w[n].shape for n in GATHER_F32], (NDEV,))
    c_all = parts[0].reshape(NDEV, D)
    P = {n: _from_chunks(t, kinds[n]) for n, t in zip(GATHER_F32, parts[1:])}
    P = {n: (t if n in ("ln_g", "ln_b") else t[0]) for n, t in P.items()}
    for n in REPLICATED[1:]:
        P[n] = w[n]
    def full(n, t):
        t = _from_chunks(t, kinds[n])
        return t if n in ("mlp_w1", "mlp_w2") else t[0]

    parts = _all_gather_many([w[n].astype(BF16) for n in FIRST], "gather_first_weights")
    W = {n: full(n, t) for n, t in zip(FIRST, parts)}
    W["ab_w_in"] = jnp.concatenate([_pad_pa(W["ab_w_in"][:, :1824]), W["ab_w_in"][:, 1824:]], axis=1)

    ada_b_loc = lax.dynamic_slice(ada_b, (0, ncol * me), (2, ncol))
    mod_part = _ada_mod(c_all, ada_w, ada_b_loc)
    mod_all = _all_gather(mod_part.reshape(-1, 128), "gather_mod").reshape(NDEV, 2, NDEV, ncol)
    mod = lax.dynamic_index_in_dim(mod_all, me, axis=2, keepdims=False)
    mod = mod.transpose(1, 0, 2).reshape(2, 6 * D)

    behind = (mod[0, 0] * 0.0).astype(BF16)
    later = _exchange_start([w[n].astype(BF16) + (behind if n == LATER[0] else 0) for n in LATER], True,
                            "gather_later_weights_start")
    mod = mod + later[-1][0, 0]

    def later_weights(after):
        lands = _exchange_wait(later, True, after, "gather_later_weights_wait")
        return {n: full(n, t) for n, t in zip(LATER, lands)}

    sent = []

    def early_grads(G):
        sent.append(_exchange_start([_to_chunks(G[n], kinds[n]).astype(BF16) for n in LATER], False,
                                    "exchange_later_grads_start"))
        return sent[0][-1][0, 0]

    loss_part, grad_x, dmod, G = _local_step(x[0], loss_target[0], mod, W, P, later_weights, early_grads)
    G["ada_b"] = dmod

    rep_shapes = [w[n].shape for n in REPLICATED] + [(1,)]
    rep_all = _all_gather(_pack([G[n] for n in REPLICATED] + [loss_part], F32, 8), "gather_replicated_grads")
    pk = lambda d: _pack([d[n] for n in REPLICATED] + [jnp.zeros((1,), F32)], F32, 8)
    rep_out = _sum_adamw(rep_all, pk(w), pk(m), pk(v), "sum_adamw_replicated", rep_all.shape[1])
    loss = _unpack(rep_out[0], rep_shapes)[-1][0]
    rep_out = [dict(zip(REPLICATED, _unpack(o, rep_shapes))) for o in rep_out]

    dmod_all = _unpack(rep_all, [(2, 6 * D)], (NDEV,))[0]
    dmod_loc = lax.dynamic_slice(dmod_all, (0, 0, ncol * me), (NDEV, 2, ncol)).transpose(1, 0, 2)
    ada_out = _ada_grad_adamw(c_all.T, dmod_loc, ada_w, m_ada_w, v_ada_w)

    names = [n for n, _ in SHARDED if n not in GATHER_BF16]
    shard_shapes = [w[n].shape for n in names]
    chunks = _pack8([_to_chunks(G[n], kinds[n]) for n in names], F32, 8)
    recv = _all_to_all(chunks, "exchange_small_grads")
    pk = lambda d: _pack([d[n] for n in names], F32, 8)
    sh_out = _sum_adamw(recv, pk(w), pk(m), pk(v), "sum_adamw_small", recv.shape[1])
    sh_out = [dict(zip(names, _unpack(o, shard_shapes))) for o in sh_out]

    big_out = {}

    def update(n, contributions):
        cols = w[n].shape[-1]
        flat = lambda t: t.reshape(-1, cols)
        rows = flat(w[n]).shape[0]
        outs = _sum_adamw(contributions.reshape(-1, rows, cols), flat(w[n]), flat(m[n]), flat(v[n]),
                          f"sum_adamw_{n}", min(rows, 256))
        big_out[n] = [o.reshape(w[n].shape) for o in outs]

    ci = lax.axis_index("c")
    mine_l, sib_l = [], []
    for n in FIRST:
        g8 = _to_chunks(G[n], kinds[n])
        g42 = g8.reshape((4, 2) + g8.shape[1:])
        mine_l.append(lax.dynamic_index_in_dim(g42, ci, 1, keepdims=False))
        sib_l.append(lax.dynamic_index_in_dim(g42, 1 - ci, 1, keepdims=False))
    from_sib = _swap_sibling(sib_l, "swap_sibling_grads")

    def add2_body(r, pp):
        return [r[0] + r[1]], []

    partials = []
    for n, a, b in zip(FIRST, mine_l, from_sib):
        cols = a.shape[-1]
        (p,), _ = _rows(f"pair_sum_{n}", add2_body, [a.reshape(-1, cols), b.reshape(-1, cols)], [],
                        [(cols, BF16)], tm=512)
        partials.append(p.reshape(a.shape))
    for n, r in zip(FIRST, _exchange_chips(partials, "exchange_chip_grads")):
        update(n, r)

    for n, r in zip(LATER, _exchange_wait(sent[0], False, partials[0], "exchange_later_grads_wait")):
        update(n, r)
    sh_out = [{**d, **{n: big_out[n][i] for n in GATHER_BF16}} for i, d in enumerate(sh_out)]

    def pick(i, n):
        if n == "ada_w":
            return ada_out[i]
        return rep_out[i][n] if n in REPLICATED else sh_out[i][n]

    outs = [loss, grad_x[None]]
    for i in range(4):
        outs += [pick(i, n) for n in WEIGHTS]
    return tuple(outs)
```

```python
import functools
import math

import jax
import jax.numpy as jnp
from jax import lax
from jax.experimental import pallas as pl
from jax.experimental.pallas import tpu as pltpu

F32 = jnp.float32
BF16 = jnp.bfloat16
HI = lax.Precision.HIGHEST

NDEV = 8
T = 2048
D = 1024
DFF = 4096
HEADS = 8
HD = 64
RW = 512
PA = 2048
PB = 1536
PAB = PA + PB
QKV = 4608
DILS = (1, 4, 16)
BLK = 128
ALPHA = 4.0 ** 0.25
LN_EPS = 1e-5
GN_EPS = 64e-5
ADAM_LR, ADAM_B1, ADAM_B2, ADAM_EPS, ADAM_WD, ADAM_STEP = 0.001, 0.9, 0.999, 1e-8, 0.01, 10
VMEM_LIMIT = 56 * 1024 * 1024


def _cp(sem):
    return pltpu.CompilerParams(dimension_semantics=sem, vmem_limit_bytes=VMEM_LIMIT)


def _slot(px, py, pc):
    return 4 * px + 2 * py + pc


def _all_gather(x, name):
    R, C = x.shape

    def body(x_ref, out_ref, send_sems, recv_sems, local_sem):
        xi, yi, ci = lax.axis_index("x"), lax.axis_index("y"), lax.axis_index("c")
        me, sibling = (xi, yi, ci), (xi, yi, 1 - ci)
        chips = [(1 - xi, yi), (xi, 1 - yi), (1 - xi, 1 - yi)]

        def rows(px, py, pc):
            return out_ref.at[_slot(px, py, pc)]

        def copy(k, block, to, src=None):
            return pltpu.make_async_remote_copy(
                src_ref=rows(*block) if src is None else src, dst_ref=rows(*block),
                send_sem=send_sems.at[k], recv_sem=recv_sems.at[k],
                device_id=to, device_id_type=pl.DeviceIdType.MESH)

        mine = pltpu.make_async_copy(x_ref, rows(*me), local_sem)
        mine.start()
        first = [copy(0, me, sibling, src=x_ref)]
        first += [copy(1 + j, me, (*chip, ci), src=x_ref) for j, chip in enumerate(chips)]
        for cp in first:
            cp.start()
        passed = [copy(4 + j, (*chip, ci), sibling) for j, chip in enumerate(chips)]
        for j, chip in enumerate(chips):
            copy(1 + j, (*chip, ci), me).wait_recv()
            passed[j].start()
        copy(0, sibling, me).wait_recv()
        for j, chip in enumerate(chips):
            copy(4 + j, (*chip, 1 - ci), me).wait_recv()
        for cp in first + passed:
            cp.wait_send()
        mine.wait()

    return pl.pallas_call(
        body, name=name,
        out_shape=jax.ShapeDtypeStruct((NDEV, R, C), x.dtype),
        in_specs=[pl.BlockSpec(memory_space=pl.ANY)],
        out_specs=pl.BlockSpec(memory_space=pl.ANY),
        scratch_shapes=[pltpu.SemaphoreType.DMA((7,)), pltpu.SemaphoreType.DMA((7,)),
                        pltpu.SemaphoreType.DMA(())],
    )(x)


def _all_to_all(g, name):
    _, R, C = g.shape

    def body(g_ref, out_ref, send_sems, recv_sems, local_sem):
        xi, yi, ci = lax.axis_index("x"), lax.axis_index("y"), lax.axis_index("c")
        my_slot = _slot(xi, yi, ci)
        mine = pltpu.make_async_copy(g_ref.at[my_slot], out_ref.at[my_slot], local_sem)
        mine.start()
        copies = []
        for k in range(1, 8):
            px = 1 - xi if k & 4 else xi
            py = 1 - yi if k & 2 else yi
            pc = 1 - ci if k & 1 else ci
            peer_slot = _slot(px, py, pc)
            copies.append((
                pltpu.make_async_remote_copy(
                    src_ref=g_ref.at[peer_slot], dst_ref=out_ref.at[my_slot],
                    send_sem=send_sems.at[k - 1], recv_sem=recv_sems.at[k - 1],
                    device_id=(px, py, pc), device_id_type=pl.DeviceIdType.MESH),
                pltpu.make_async_remote_copy(
                    src_ref=g_ref.at[peer_slot], dst_ref=out_ref.at[peer_slot],
                    send_sem=send_sems.at[k - 1], recv_sem=recv_sems.at[k - 1],
                    device_id=(px, py, pc), device_id_type=pl.DeviceIdType.MESH)))
        for send, _ in copies:
            send.start()
        for _, recv in copies:
            recv.wait_recv()
        for send, _ in copies:
            send.wait_send()
        mine.wait()

    return pl.pallas_call(
        body, name=name,
        out_shape=jax.ShapeDtypeStruct((NDEV, R, C), g.dtype),
        in_specs=[pl.BlockSpec(memory_space=pl.ANY)],
        out_specs=pl.BlockSpec(memory_space=pl.ANY),
        scratch_shapes=[pltpu.SemaphoreType.DMA((7,)), pltpu.SemaphoreType.DMA((7,)),
                        pltpu.SemaphoreType.DMA(())],
    )(g)


def _my_slot():
    return _slot(lax.axis_index("x"), lax.axis_index("y"), lax.axis_index("c"))


def _put_own(buf, own, slot):
    return lax.dynamic_update_index_in_dim(buf, own, slot, 0)


def _hbm_call(body, name, ins, out_shapes, n_sems):
    anyspec = pl.BlockSpec(memory_space=pl.ANY)
    return pl.pallas_call(
        body, name=name, out_shape=out_shapes,
        in_specs=[anyspec] * len(ins), out_specs=[anyspec] * len(out_shapes),
        scratch_shapes=[pltpu.SemaphoreType.DMA(s) for s in n_sems],
    )(*ins)


def _all_gather_many(xs, name):
    n = len(xs)

    def body(*refs):
        x_refs, o_refs = refs[:n], refs[n:2 * n]
        send_sems, recv_sems = refs[2 * n:]
        xi, yi, ci = lax.axis_index("x"), lax.axis_index("y"), lax.axis_index("c")
        me, sibling = (xi, yi, ci), (xi, yi, 1 - ci)
        chips = [(1 - xi, yi), (xi, 1 - yi), (1 - xi, 1 - yi)]

        def copy(i, k, block, to, src=None):
            dst = o_refs[i].at[_slot(*block)]
            return pltpu.make_async_remote_copy(
                src_ref=dst if src is None else src, dst_ref=dst,
                send_sem=send_sems.at[i, k], recv_sem=recv_sems.at[i, k],
                device_id=to, device_id_type=pl.DeviceIdType.MESH)

        sends = []
        for i in range(n):
            sends += [copy(i, 1 + j, me, (*chip, ci), src=x_refs[i]) for j, chip in enumerate(chips)]
            sends.append(copy(i, 0, me, sibling, src=x_refs[i]))
        for cp in sends:
            cp.start()
        for j, chip in enumerate(chips):
            for i in range(n):
                copy(i, 1 + j, (*chip, ci), me).wait_recv()
                passed = copy(i, 4 + j, (*chip, ci), sibling)
                passed.start()
                sends.append(passed)
        for i in range(n):
            copy(i, 0, sibling, me).wait_recv()
            for j, chip in enumerate(chips):
                copy(i, 4 + j, (*chip, 1 - ci), me).wait_recv()
        for cp in sends:
            cp.wait_send()

    outs = _hbm_call(body, name, xs, [jax.ShapeDtypeStruct((NDEV,) + x.shape, x.dtype) for x in xs],
                     [(n, 7), (n, 7)])
    return [_put_own(o, x[None], _my_slot()) for o, x in zip(outs, xs)]


def _peers(xi, yi, ci):
    return [(1 - xi if k & 4 else xi, 1 - yi if k & 2 else yi, 1 - ci if k & 1 else ci) for k in range(1, 8)]


def _direct_copy(src_refs, land_refs, send_sems, recv_sems, i, k, peer, my_slot, gather):
    src = src_refs[i] if gather else src_refs[i].at[_slot(*peer)]
    return pltpu.make_async_remote_copy(
        src_ref=src, dst_ref=land_refs[i].at[my_slot], send_sem=send_sems.at[7 * i + k], recv_sem=recv_sems.at[7 * i + k],
        device_id=peer, device_id_type=pl.DeviceIdType.MESH)


def _exchange_start(srcs, gather, name):
    n = len(srcs)
    lands = [lax.empty(((NDEV,) + s.shape) if gather else s.shape, s.dtype) for s in srcs]

    def body(*refs):
        s_refs, l_refs = refs[:n], refs[n:2 * n]
        send_sems, recv_sems = refs[2 * n], refs[2 * n + 1]
        token = refs[2 * n + 2 + 2 * n]
        xi, yi, ci = lax.axis_index("x"), lax.axis_index("y"), lax.axis_index("c")
        my_slot = _slot(xi, yi, ci)
        for k, peer in enumerate(_peers(xi, yi, ci)):
            for i in range(n):
                _direct_copy(s_refs, l_refs, send_sems, recv_sems, i, k, peer, my_slot, gather).start()
        token[...] = jnp.zeros_like(token)

    hbm = pl.BlockSpec(memory_space=pltpu.HBM)
    sem = pl.BlockSpec(memory_space=pltpu.SEMAPHORE)
    both = list(srcs) + lands
    return pl.pallas_call(
        body, name=name,
        out_shape=(pltpu.SemaphoreType.DMA((7 * n,)), pltpu.SemaphoreType.DMA((7 * n,)),
                   *[pltpu.HBM(t.shape, t.dtype) for t in both], jax.ShapeDtypeStruct((8, 128), F32)),
        in_specs=[hbm] * (2 * n),
        out_specs=(sem, sem, *[hbm] * (2 * n), pl.BlockSpec(memory_space=pltpu.VMEM)),
        input_output_aliases={i: 2 + i for i in range(2 * n)},
        compiler_params=pltpu.CompilerParams(has_side_effects=pltpu.SideEffectType.DATAFLOW_SIDE_EFFECTING),
    )(*[pltpu.with_memory_space_constraint(t, pltpu.HBM) for t in both])


def _exchange_wait(started, gather, after, name):
    send_sems, recv_sems, *thru, _ = started
    n = len(thru) // 2

    def body(*refs):
        s_refs, l_refs = refs[:n], refs[n:2 * n]
        send_sems, recv_sems = refs[2 * n], refs[2 * n + 1]
        xi, yi, ci = lax.axis_index("x"), lax.axis_index("y"), lax.axis_index("c")
        my_slot = _slot(xi, yi, ci)
        for k, peer in enumerate(_peers(xi, yi, ci)):
            for i in range(n):
                _direct_copy(s_refs, l_refs, send_sems, recv_sems, i, k, peer, my_slot, gather).wait_send()
                _direct_copy(s_refs, l_refs, send_sems, recv_sems, i, k, peer, _slot(*peer), gather).wait_recv()

    hbm = pl.BlockSpec(memory_space=pltpu.HBM)
    sem = pl.BlockSpec(memory_space=pltpu.SEMAPHORE)
    outs = pl.pallas_call(
        body, name=name,
        out_shape=tuple(pltpu.HBM(t.shape, t.dtype) for t in thru),
        in_specs=[hbm] * (2 * n) + [sem, sem, pl.BlockSpec(memory_space=pl.ANY)],
        out_specs=tuple([hbm] * (2 * n)),
        input_output_aliases={i: i for i in range(2 * n)},
        compiler_params=pltpu.CompilerParams(has_side_effects=pltpu.SideEffectType.DATAFLOW_SIDE_EFFECTING),
    )(*thru, send_sems, recv_sems, after)
    slot = _my_slot()
    own = [s[None] if gather else lax.dynamic_index_in_dim(s, slot, 0, keepdims=True) for s in outs[:n]]
    return [_put_own(land, o, slot) for land, o in zip(outs[n:], own)]


def _mm(name, a, b, tb=False, out=(F32,), epi=None, extras=(), tm=1024, tn=512, tk_cap=2048):
    M, K = a.shape
    N = b.shape[0] if tb else b.shape[1]
    tm, tn = min(tm, M), min(tn, N)
    tk = max(t for t in range(128, min(K, tk_cap) + 1, 128) if K % t == 0)
    assert M % tm == 0 and N % tn == 0 and K % tk == 0, (name, M, N, K)
    nk = K // tk
    ne, no = len(extras), len(out)
    dims = (((1,), (1 if tb else 0,)), ((), ()))

    def kern(*refs):
        a_ref, b_ref = refs[:2]
        e_refs = refs[2:2 + ne]
        o_refs = refs[2 + ne:2 + ne + no]

        def finish(acc):
            outs = epi(acc, *[e[...] for e in e_refs]) if epi is not None else (acc,)
            for o_ref, o in zip(o_refs, outs):
                o_ref[...] = o.astype(o_ref.dtype)

        part = lax.dot_general(a_ref[...], b_ref[...], dims, preferred_element_type=F32)
        if nk == 1:
            finish(part)
            return
        acc_ref = refs[-1]
        k = pl.program_id(2)

        @pl.when(k == 0)
        def _():
            acc_ref[...] = part

        @pl.when(k > 0)
        def _():
            acc_ref[...] += part

        @pl.when(k == nk - 1)
        def _():
            finish(acc_ref[...])

    b_spec = (pl.BlockSpec((tn, tk), lambda i, j, k: (j, k)) if tb
              else pl.BlockSpec((tk, tn), lambda i, j, k: (k, j)))
    tile = pl.BlockSpec((tm, tn), lambda i, j, k: (i, j))
    res = pl.pallas_call(
        kern, name=name, grid=(M // tm, N // tn, nk),
        in_specs=[pl.BlockSpec((tm, tk), lambda i, j, k: (i, k)), b_spec] + [tile] * ne,
        out_specs=[tile] * no,
        out_shape=[jax.ShapeDtypeStruct((M, N), dt) for dt in out],
        scratch_shapes=[pltpu.VMEM((tm, tn), F32)] if nk > 1 else [],
        compiler_params=_cp(("parallel", "parallel", "arbitrary")),
    )(a, b, *extras)
    return res[0] if no == 1 else res


HALO = 8


def _rows(name, body, rows, params, out_rows, out_accs=(), tm=256):
    views = [r if isinstance(r, tuple) else (r, r.shape[1], 0) for r in rows]
    n = views[0][0].shape[0]
    assert n % tm == 0 and tm % HALO == 0
    nr, npar, nor, noa = len(views), len(params), len(out_rows), len(out_accs)

    def row_spec(width, cb, halo=None):
        per, last = tm // HALO, n // HALO - 1
        if halo == "prev":
            return pl.BlockSpec((HALO, width), lambda i: (jnp.maximum(i * per - 1, 0), cb))
        if halo == "next":
            return pl.BlockSpec((HALO, width), lambda i: (jnp.minimum((i + 1) * per, last), cb))
        return pl.BlockSpec((tm, width), lambda i: (i, cb))

    def kern(*refs):
        r_refs = refs[:nr]
        p_refs = refs[nr:nr + npar]
        o_refs = refs[nr + npar:nr + npar + nor]
        a_refs = refs[nr + npar + nor:]
        outs, accs = body([r[...] for r in r_refs], [p[...] for p in p_refs])
        assert len(outs) == nor and len(accs) == noa, (name, len(outs), len(accs))
        for o_ref, o in zip(o_refs, outs):
            o_ref[...] = o.astype(o_ref.dtype)
        if noa:
            @pl.when(pl.program_id(0) == 0)
            def _():
                for a_ref in a_refs:
                    a_ref[...] = jnp.zeros_like(a_ref)

            for a_ref, a in zip(a_refs, accs):
                a_ref[...] += a.astype(F32)

    def whole(shape):
        nd = len(shape)
        return pl.BlockSpec(tuple(shape), lambda i, nd=nd: (0,) * nd)

    in_specs = [row_spec(*v[1:]) for v in views]
    in_specs += [whole(p.shape) for p in params]
    out_specs = [pl.BlockSpec((tm, c), lambda i: (i, 0)) for c, _ in out_rows]
    out_specs += [whole(s) for s in out_accs]
    out_shape = [jax.ShapeDtypeStruct((n, c), dt) for c, dt in out_rows]
    out_shape += [jax.ShapeDtypeStruct(tuple(s), F32) for s in out_accs]
    res = pl.pallas_call(
        kern, name=name, grid=(n // tm,), in_specs=in_specs, out_specs=out_specs,
        out_shape=out_shape, compiler_params=_cp(("arbitrary",)),
    )(*[v[0] for v in views], *params)
    return res[:nor], res[nor:]


def _shift_down(x, prev, k):
    head = jnp.where(pl.program_id(0) == 0, 0.0, pltpu.roll(prev, k, axis=0))
    row = lax.broadcasted_iota(jnp.int32, x.shape, 0)
    return jnp.where(row < k, jnp.tile(head, (x.shape[0] // HALO, 1)), pltpu.roll(x, k, axis=0))


def _shift_up(x, nxt, k):
    n = x.shape[0]
    tail = jnp.where(pl.program_id(0) == pl.num_programs(0) - 1, 0.0, pltpu.roll(nxt, HALO - k, axis=0))
    row = lax.broadcasted_iota(jnp.int32, x.shape, 0)
    return jnp.where(row >= n - k, jnp.tile(tail, (n // HALO, 1)), pltpu.roll(x, n - k, axis=0))


@jax.custom_vjp
def _headsum(x, e):
    return sum(jnp.dot(p, e, preferred_element_type=F32) for p in _split3(x))


_headsum.defvjp(lambda x, e: (_headsum(x, e), e), lambda e, ct: (_headsum(ct, e), None))


def _softplus(z):
    return jnp.maximum(z, 0.0) + jnp.log(1.0 + jnp.exp(jnp.minimum(z, -z)))


def _post_ln(x, y, g, lng, lnb):
    z = ALPHA * x + (1.0 + g) * y
    mu = jnp.mean(z, axis=-1, keepdims=True)
    zc = z - mu
    var = jnp.mean(zc * zc, axis=-1, keepdims=True)
    return zc * lax.rsqrt(var + LN_EPS) * lng + lnb


def _post_ln_mod(x, y, g, lng, lnb, scn, shn):
    xn = _post_ln(x, y, g, lng, lnb)
    return xn, xn * (1.0 + scn) + shn


def _pre_core(E, r_, k_, v_, wd_, ad_, gd_, r1, k1, v1, wd1, ad1, gd1, h, bg, cg, h1, cg1, h2, cg2,
              mu_r, mu_k, mu_v, mu_wd, mu_ad, mu_gd, w0, w_up, a0, a_up, g_up, k_k, k_a,
              cw0, cw1, cw2):
    def mix(x, x1, mu):
        return x + mu * (x1 - x)

    r, k, v = mix(r_, r1, mu_r), mix(k_, k1, mu_k), mix(v_, v1, mu_v)
    wd, ad, gd = mix(wd_, wd1, mu_wd), mix(ad_, ad1, mu_ad), mix(gd_, gd1, mu_gd)
    logw = -_softplus(-(w0 + jnp.dot(jnp.tanh(wd), w_up, preferred_element_type=F32))) - 0.5
    decay = jnp.exp(-jnp.exp(logw))
    iclr = jax.nn.sigmoid(a0 + jnp.dot(ad, a_up, preferred_element_type=F32))
    gate = jnp.dot(jax.nn.sigmoid(gd), g_up, preferred_element_type=F32)
    kk0 = k * k_k
    nrm = jnp.sqrt(_headsum(kk0 * kk0, E))
    kk = kk0 / jnp.maximum(nrm, 1e-12)
    kh = k * (1.0 + (iclr - 1.0) * k_a)
    yb = bg * (cw2 * (cg * h) + cw1 * (cg1 * h1) + cw0 * (cg2 * h2))
    return r, decay, kh, v, -kk, kk * iclr, gate, yb


def _post_core(E, y, r, kh, v, gate, lnx_g, lnx_b, rk):
    def seg(t):
        return _headsum(t, E)

    mean = seg(y) * (1.0 / HD)
    yc = y - mean
    var = seg(yc * yc) * (1.0 / HD)
    gn = yc * lax.rsqrt(var + GN_EPS) * lnx_g + lnx_b
    bonus = seg(r * kh * rk) * v
    return (gn + bonus) * gate


def _merge_core(o0, o1, o2, l0, l1, l2):
    m = jnp.maximum(jnp.maximum(l0, l1), l2)
    e0, e1, e2 = jnp.exp(l0 - m), jnp.exp(l1 - m), jnp.exp(l2 - m)
    den = e0 + e1 + e2
    return (e0 * o0 + e1 * o1 + e2 * o2) / den


CHUNK = 128
HALF = 64
HP = HEADS // 2
LW = 2 * HD
NCHUNK = T // CHUNK


def _split3(x):
    hi = x.astype(BF16)
    r1 = x - hi.astype(F32)
    mid = r1.astype(BF16)
    return hi, mid, (r1 - mid.astype(F32)).astype(BF16)


def _cols3(x, name):
    def kern(x_ref, o_ref):
        xt = x_ref[...].T
        left = lax.broadcasted_iota(jnp.int32, (HD, CHUNK), 1) < HALF
        for p in range(HP):
            a, b = xt[p * LW:p * LW + HD], xt[p * LW + HD:(p + 1) * LW]
            halves = [jnp.where(left, a, pltpu.roll(b, HALF, axis=1)), jnp.where(left, pltpu.roll(a, HALF, axis=1), b)]
            for h, tile in enumerate(halves):
                for j, part in enumerate(_split3(tile)):
                    o_ref[p, :, (3 * h + j) * LW:(3 * h + j + 1) * LW] = part

    return pl.pallas_call(
        kern, name=name, grid=(NCHUNK,),
        in_specs=[pl.BlockSpec((CHUNK, RW), lambda c: (c, 0))],
        out_specs=pl.BlockSpec((HP, HD, 6 * CHUNK), lambda c: (0, 0, c)),
        out_shape=jax.ShapeDtypeStruct((HP, HD, 6 * T), BF16),
        compiler_params=_cp(("parallel",)),
    )(x)


def _pick_codes():
    row = lax.broadcasted_iota(jnp.int32, (6 * HALF, LW), 0)
    col = lax.broadcasted_iota(jnp.int32, (6 * HALF, LW), 1)
    same = ((row & (LW - 1)) >= HALF) == (col >= HD)
    return jnp.where(same, row & (HALF - 1), -1).astype(BF16)


def _column(block_ref, codes, half, i):
    pick = jnp.where(codes == i.astype(BF16), jnp.ones((), BF16), jnp.zeros((), BF16))
    block = block_ref[:, :, half * 6 * HALF:(half + 1) * 6 * HALF].reshape(HP * HD, 6 * HALF)
    return jnp.dot(block, pick, preferred_element_type=F32)


def _halfsums(x, row, left1):
    row_l = jnp.where(left1, row, 0.0)
    return (jnp.sum(x * row_l, axis=1, keepdims=True), jnp.sum(x * (row - row_l), axis=1, keepdims=True))


def _pair_rows(row):
    return [row[:, p * LW:(p + 1) * LW] for p in range(HP)]


def _store_columns(ref, p, t_mask, cols):
    for j, col in enumerate(cols):
        pltpu.store(ref.at[pl.ds(2 * p + j, 1)], jnp.broadcast_to(col[None], (1, HD, CHUNK)), mask=t_mask[None])


def _columns_to_rows(cols_ref, rows_ref):
    for p in range(HP):
        rows_ref[:, p * LW:(p + 1) * LW] = cols_ref[2 * p:2 * p + 2].reshape(LW, CHUNK).T


NHALF = T // HALF
HALVES = CHUNK // HALF


def _scan_fwd(r, w, k, a, b, v3):
    def kern(r_ref, w_ref, k_ref, a_ref, b_ref, v_ref, y_ref, ck_ref, st_hbm, sa_hbm,
             s_ref, vb_ref, yc_ref, st_ref, sa_ref, sems):
        c = pl.program_id(0)

        @pl.when(c == 0)
        def _():
            s_ref[...] = jnp.zeros_like(s_ref)

        lane = lax.broadcasted_iota(jnp.int32, (HD, CHUNK), 1)
        left = lane < HD
        left1 = lax.broadcasted_iota(jnp.int32, (1, LW), 1) < HD
        codes = _pick_codes()

        def flush(slot, half_index):
            return [pltpu.make_async_copy(src.at[slot], dst.at[half_index], sems.at[j, slot])
                    for j, (src, dst) in enumerate(((st_ref, st_hbm), (sa_ref, sa_hbm)))]

        for half in range(HALVES):
            ck_ref[half] = s_ref[...]
            vb_ref[...] = _column(v_ref, codes, half, jnp.int32(0))

            @pl.when(c > 0)
            def _():
                for cp in flush(half, (c - 1) * HALVES + half):
                    cp.wait()

            def step(i, carry):
                t = half * HALF + i
                row = lambda ref: _pair_rows(ref[pl.ds(t, 1), :])
                S = [s_ref[p] for p in range(HP)]
                sa = [jnp.where(left, *_halfsums(s, a, left1)) for s, a in zip(S, row(a_ref))]
                S = [s * w + c_ * b + vb_ref[pl.ds(p * HD, HD), :] * k
                     for p, (s, w, c_, b, k) in enumerate(zip(S, row(w_ref), sa, row(b_ref), row(k_ref)))]
                for p, (s, c_) in enumerate(zip(S, sa)):
                    s_ref[p] = s
                    st_ref[half, i, p] = s
                    sa_ref[half, i, p] = c_
                for p, (s, r) in enumerate(zip(S, row(r_ref))):
                    _store_columns(yc_ref, p, lane == t, _halfsums(s, r, left1))
                vb_ref[...] = _column(v_ref, codes, half, i + 1)
                return carry

            lax.fori_loop(0, HALF, step, 0, unroll=8)
            for cp in flush(half, c * HALVES + half):
                cp.start()
        _columns_to_rows(yc_ref, y_ref)

        @pl.when(c == NCHUNK - 1)
        def _():
            for half in range(HALVES):
                for cp in flush(half, c * HALVES + half):
                    cp.wait()

    rowblk = pl.BlockSpec((CHUNK, RW), lambda c: (c, 0))
    saved = jax.ShapeDtypeStruct((NHALF, HALF, HP, HD, LW), F32)
    stage = pltpu.VMEM((HALVES, HALF, HP, HD, LW), F32)
    return pl.pallas_call(
        kern, name="rwkv_scan_fwd", grid=(NCHUNK,),
        in_specs=[rowblk] * 5 + [pl.BlockSpec((HP, HD, 6 * CHUNK), lambda c: (0, 0, c))],
        out_specs=[rowblk, pl.BlockSpec((HALVES, HP, HD, LW), lambda c: (c, 0, 0, 0)),
                   pl.BlockSpec(memory_space=pl.ANY), pl.BlockSpec(memory_space=pl.ANY)],
        out_shape=[jax.ShapeDtypeStruct((T, RW), F32), jax.ShapeDtypeStruct((NHALF, HP, HD, LW), F32), saved, saved],
        scratch_shapes=[pltpu.VMEM((HP, HD, LW), F32), pltpu.VMEM((HP * HD, LW), F32),
                        pltpu.VMEM((HEADS, HD, CHUNK), F32), stage, stage, pltpu.SemaphoreType.DMA((2, HALVES))],
        compiler_params=_cp(("arbitrary",)),
    )(r, w, k, a, b, v3)


def _scan_bwd(r, w, k, a, b, v3, dy3, ck, st, sa):
    def kern(r_ref, w_ref, k_ref, a_ref, b_ref, v_ref, dy_ref, ck_ref, st_hbm, sa_hbm,
             dr_ref, dw_ref, dk_ref, da_ref, db_ref, dv_ref, ds_ref, sb_ref, sa_ref, pick_ref, dvc_ref, sems):
        c = pl.program_id(0)
        chunk = NCHUNK - 1 - c

        @pl.when(c == 0)
        def _():
            ds_ref[...] = jnp.zeros_like(ds_ref)

        lane = lax.broadcasted_iota(jnp.int32, (HD, CHUNK), 1)
        left = lane < HD
        left1 = lax.broadcasted_iota(jnp.int32, (1, LW), 1) < HD
        codes = _pick_codes()

        def rowsum(x):
            return jnp.sum(x, axis=0, keepdims=True)

        def fetch(slot, half_index):
            return [pltpu.make_async_copy(st_hbm.at[half_index], sb_ref.at[slot, pl.ds(1, HALF)], sems.at[0, slot]),
                    pltpu.make_async_copy(sa_hbm.at[half_index], sa_ref.at[slot], sems.at[1, slot])]

        def picks(half, i):
            pick_ref[pl.ds(0, HP * HD), :] = _column(v_ref, codes, half, i)
            pick_ref[pl.ds(HP * HD, HP * HD), :] = _column(dy_ref, codes, half, i)

        @pl.when(c == 0)
        def _():
            for cp in fetch(HALVES - 1, chunk * HALVES + HALVES - 1):
                cp.start()

        for half in reversed(range(HALVES)):
            base = half * HALF
            for cp in fetch(half, chunk * HALVES + half):
                cp.wait()
            if half:
                for cp in fetch(half - 1, chunk * HALVES + half - 1):
                    cp.start()
            else:
                @pl.when(chunk > 0)
                def _():
                    for cp in fetch(HALVES - 1, chunk * HALVES - 1):
                        cp.start()
            sb_ref[half, 0] = ck_ref[half]
            picks(half, jnp.int32(HALF - 1))

            def back(ii, carry):
                i = HALF - 1 - ii
                t = base + i
                row = lambda ref: _pair_rows(ref[pl.ds(t, 1), :])
                a_r, b_r, k_r, w_r, r_r = row(a_ref), row(b_ref), row(k_ref), row(w_ref), row(r_ref)
                vs = [pick_ref[pl.ds(p * HD, HD), :] for p in range(HP)]
                dys = [pick_ref[pl.ds((HP + p) * HD, HD), :] for p in range(HP)]
                picks(half, jnp.maximum(i - 1, 0))
                dr, dw, db, dk, da = [], [], [], [], []
                for p in range(HP):
                    Sp, dy = sb_ref[half, i, p], dys[p]
                    dS = ds_ref[p] + dy * r_r[p]
                    dr.append(rowsum(sb_ref[half, i + 1, p] * dy))
                    dw.append(rowsum(dS * Sp))
                    db.append(rowsum(dS * sa_ref[half, i, p]))
                    dk.append(rowsum(dS * vs[p]))
                    dsa = jnp.where(left, *_halfsums(dS, b_r[p], left1))
                    _store_columns(dvc_ref, p, lane == t, _halfsums(dS, k_r[p], left1))
                    da.append(rowsum(Sp * dsa))
                    ds_ref[p] = dS * w_r[p] + dsa * a_r[p]
                for ref, pieces in ((dr_ref, dr), (dw_ref, dw), (db_ref, db), (dk_ref, dk), (da_ref, da)):
                    ref[pl.ds(t, 1), :] = jnp.concatenate(pieces, axis=1)
                return carry

            lax.fori_loop(0, HALF, back, 0, unroll=4)
        _columns_to_rows(dvc_ref, dv_ref)

    rowblk = pl.BlockSpec((CHUNK, RW), lambda c: (NCHUNK - 1 - c, 0))
    col3blk = pl.BlockSpec((HP, HD, 6 * CHUNK), lambda c: (0, 0, NCHUNK - 1 - c))
    rowshape = jax.ShapeDtypeStruct((T, RW), F32)
    return pl.pallas_call(
        kern, name="rwkv_scan_bwd", grid=(NCHUNK,),
        in_specs=[rowblk] * 5 + [col3blk, col3blk,
                                 pl.BlockSpec((HALVES, HP, HD, LW), lambda c: (NCHUNK - 1 - c, 0, 0, 0)),
                                 pl.BlockSpec(memory_space=pl.ANY), pl.BlockSpec(memory_space=pl.ANY)],
        out_specs=[rowblk] * 6, out_shape=[rowshape] * 6,
        scratch_shapes=[pltpu.VMEM((HP, HD, LW), F32), pltpu.VMEM((HALVES, HALF + 1, HP, HD, LW), F32),
                        pltpu.VMEM((HALVES, HALF, HP, HD, LW), F32), pltpu.VMEM((2 * HP * HD, LW), F32),
                        pltpu.VMEM((HEADS, HD, CHUNK), F32), pltpu.SemaphoreType.DMA((2, HALVES))],
        compiler_params=_cp(("arbitrary",)),
    )(r, w, k, a, b, v3, dy3, ck, st, sa)


NT = (((1,), (1,)), ((), ()))
TN = (((0,), (0,)), ((), ()))
SCALE = HD ** -0.5
QKV_G = 3 * RW


def _attn_setup(g):
    dil = DILS[g]
    qkv = [pl.BlockSpec((T, LW), lambda hp, c=(g * QKV_G + s * RW) // LW: (0, c + hp)) for s in range(3)]
    tile = pl.BlockSpec((T, LW), lambda hp: (0, hp))
    bias = pl.BlockSpec((2, BLK, 2 * BLK), lambda hp: (hp, 0, 0))

    def blocks():
        for r in range(dil):
            for n in range(T // dil // BLK):
                rows = pl.ds(n * BLK * dil + r, BLK, stride=dil)
                keys = pl.ds((n - 1) * BLK * dil + r, 2 * BLK, stride=dil) if n else rows
                yield n, rows, keys

    return qkv, tile, bias, blocks


def _band(n):
    qi = lax.broadcasted_iota(jnp.int32, (BLK, 2 * BLK), 0)
    ki = lax.broadcasted_iota(jnp.int32, (BLK, 2 * BLK), 1)
    band = (ki >= qi) & (ki <= qi + BLK)
    return band if n else band[:, BLK:]


def _head_masks():
    lane = lax.broadcasted_iota(jnp.int32, (BLK, LW), 1)
    return lane < HD, [(lane < HD).astype(BF16), (lane >= HD).astype(BF16)]


def _attn_fwd(pq, bias, g):
    qkv, tile, bias_spec, blocks = _attn_setup(g)

    def kern(q_ref, k_ref, v_ref, b_ref, o_ref, l_ref):
        left, masks = _head_masks()
        for n, rows, keys in blocks():
            qb, kc, vc = q_ref[rows, :].astype(BF16), k_ref[keys, :].astype(BF16), v_ref[keys, :].astype(BF16)
            valid = _band(n)
            o, lse = [], []
            for j in range(2):
                bias_j = b_ref[j] if n else b_ref[j][:, BLK:]
                s = lax.dot_general(qb * masks[j], kc, NT, preferred_element_type=F32) * SCALE + bias_j
                s = jnp.where(valid, s, -jnp.inf)
                m = jnp.max(s, axis=1, keepdims=True)
                e = jnp.exp(s - m)
                den = jnp.sum(e, axis=1, keepdims=True)
                o.append(jnp.dot((e / den).astype(BF16), vc, preferred_element_type=F32))
                lse.append(m + jnp.log(den))
            o_ref[rows, :] = jnp.where(left, o[0], o[1])
            l_ref[rows, :] = jnp.where(left, lse[0], lse[1])

    shape = jax.ShapeDtypeStruct((T, RW), F32)
    return pl.pallas_call(
        kern, name=f"attn_fwd_{g}", grid=(HP,),
        in_specs=qkv + [bias_spec], out_specs=[tile, tile], out_shape=[shape, shape],
        compiler_params=_cp(("parallel",)),
    )(pq, pq, pq, bias)


def _attn_bwd(pq, bias, do, o, lse, dlse, g):
    qkv, tile, bias_spec, blocks = _attn_setup(g)

    def kern(q_ref, k_ref, v_ref, b_ref, do_ref, o_ref, l_ref, dl_ref, dq_ref, dk_ref, dv_ref, db_ref):
        left, masks = _head_masks()
        lane = lax.broadcasted_iota(jnp.int32, (BLK, LW), 1)
        dk_ref[...] = jnp.zeros_like(dk_ref)
        dv_ref[...] = jnp.zeros_like(dv_ref)
        db_ref[...] = jnp.zeros_like(db_ref)

        def column(tile_, j):
            return jnp.sum(jnp.where(lane == j * HD, tile_, 0.0), axis=1, keepdims=True)

        for n, rows, keys in blocks():
            qb, kc, vc = q_ref[rows, :].astype(BF16), k_ref[keys, :].astype(BF16), v_ref[keys, :].astype(BF16)
            dof, valid = do_ref[rows, :], _band(n)
            dob, prod = dof.astype(BF16), dof * o_ref[rows, :]
            dq = []
            for j in range(2):
                bias_j = b_ref[j] if n else b_ref[j][:, BLK:]
                delta = jnp.sum(prod * masks[j].astype(F32), axis=1, keepdims=True)
                qm, dom = qb * masks[j], dob * masks[j]
                s = lax.dot_general(qm, kc, NT, preferred_element_type=F32) * SCALE + bias_j
                p = jnp.where(valid, jnp.exp(s - column(l_ref[rows, :], j)), 0.0)
                dp = lax.dot_general(dom, vc, NT, preferred_element_type=F32)
                ds = p * (dp + (column(dl_ref[rows, :], j) - delta))
                if n:
                    db_ref[j] += ds
                else:
                    db_ref[j, :, BLK:] += ds
                dsb = (ds * SCALE).astype(BF16)
                dq.append(jnp.dot(dsb, kc, preferred_element_type=F32))
                dk_ref[keys, :] += lax.dot_general(dsb, qm, TN, preferred_element_type=F32)
                dv_ref[keys, :] += lax.dot_general(p.astype(BF16), dom, TN, preferred_element_type=F32)
            dq_ref[rows, :] = jnp.where(left, dq[0], dq[1])

    shape = jax.ShapeDtypeStruct((T, RW), F32)
    return pl.pallas_call(
        kern, name=f"attn_bwd_{g}", grid=(HP,),
        in_specs=qkv + [bias_spec] + [tile] * 4, out_specs=[tile] * 3 + [bias_spec],
        out_shape=[shape] * 3 + [jax.ShapeDtypeStruct((HEADS, BLK, 2 * BLK), F32)],
        compiler_params=_cp(("parallel",)),
    )(pq, pq, pq, bias, do, o, lse, dlse)


NBUCKET = 32
NPAIR = BLK * 2 * BLK


def _relbias_table(rbT, onehotT):
    def kern(rb_ref, oh_ref, out_ref):
        out_ref[0] = sum(jnp.dot(p, oh_ref[0], preferred_element_type=F32) for p in _split3(rb_ref[0]))

    return pl.pallas_call(
        kern, name="relbias_table", grid=(3,),
        in_specs=[pl.BlockSpec((1, HEADS, NBUCKET), lambda g: (g, 0, 0)),
                  pl.BlockSpec((1, NBUCKET, NPAIR), lambda g: (g, 0, 0))],
        out_specs=pl.BlockSpec((1, HEADS, NPAIR), lambda g: (g, 0, 0)),
        out_shape=jax.ShapeDtypeStruct((3, HEADS, NPAIR), F32),
        compiler_params=_cp(("parallel",)),
    )(rbT, onehotT)


def _relbias_grad(db, onehotT):
    nt = (((1,), (1,)), ((), ()))

    def kern(db_ref, oh_ref, out_ref):
        hi, mid, _ = _split3(db_ref[0])
        out_ref[0] = (lax.dot_general(hi, oh_ref[0], nt, preferred_element_type=F32)
                      + lax.dot_general(mid, oh_ref[0], nt, preferred_element_type=F32))

    return pl.pallas_call(
        kern, name="relbias_grad", grid=(3,),
        in_specs=[pl.BlockSpec((1, HEADS, NPAIR), lambda g: (g, 0, 0)),
                  pl.BlockSpec((1, NBUCKET, NPAIR), lambda g: (g, 0, 0))],
        out_specs=pl.BlockSpec((1, HEADS, NBUCKET), lambda g: (g, 0, 0)),
        out_shape=jax.ShapeDtypeStruct((3, HEADS, NBUCKET), F32),
        compiler_params=_cp(("parallel",)),
    )(db, onehotT)


def _adamw(w, g, m, v):
    m2 = ADAM_B1 * m + (1.0 - ADAM_B1) * g
    v2 = ADAM_B2 * v + (1.0 - ADAM_B2) * (g * g)
    m_hat = m2 / (1.0 - ADAM_B1 ** ADAM_STEP)
    v_hat = v2 / (1.0 - ADAM_B2 ** ADAM_STEP)
    return -ADAM_LR * (m_hat / (jnp.sqrt(v_hat) + ADAM_EPS) + ADAM_WD * w), m2, v2


def _ada_mod(c_all, ada_w, ada_b_loc):
    def kern(c_ref, w_ref, b_ref, o_ref):
        c = c_ref[...]
        cond = c * jax.nn.sigmoid(c)
        o_ref[0] = jnp.dot(cond, w_ref[0], precision=HI, preferred_element_type=F32) + b_ref[0]

    ncol = ada_w.shape[2]
    return pl.pallas_call(
        kern, name="ada_mod", grid=(2,),
        in_specs=[pl.BlockSpec((NDEV, D), lambda i: (0, 0)),
                  pl.BlockSpec((1, D, ncol), lambda i: (i, 0, 0)),
                  pl.BlockSpec((1, 1, ncol), lambda i: (i, 0, 0))],
        out_specs=pl.BlockSpec((1, NDEV, ncol), lambda i: (i, 0, 0)),
        out_shape=jax.ShapeDtypeStruct((2, NDEV, ncol), F32),
        compiler_params=_cp(("parallel",)),
    )(c_all, ada_w, ada_b_loc.reshape(2, 1, ncol))


def _ada_grad_adamw(cT_all, dmod_loc, w, m, v):
    ncol = w.shape[2]
    tr = 256

    def kern(c_ref, d_ref, w_ref, m_ref, v_ref, g_ref, dl_ref, m2_ref, v2_ref):
        c = c_ref[...]
        cond = c * jax.nn.sigmoid(c)
        g = jnp.dot(cond, d_ref[0], precision=HI, preferred_element_type=F32)
        dl, m2, v2 = _adamw(w_ref[0], g, m_ref[0], v_ref[0])
        g_ref[0], dl_ref[0], m2_ref[0], v2_ref[0] = g, dl, m2, v2

    big = pl.BlockSpec((1, tr, ncol), lambda i, j: (i, j, 0))
    shp = jax.ShapeDtypeStruct(w.shape, F32)
    return pl.pallas_call(
        kern, name="ada_grad_adamw", grid=(2, D // tr),
        in_specs=[pl.BlockSpec((tr, NDEV), lambda i, j: (j, 0)),
                  pl.BlockSpec((1, NDEV, ncol), lambda i, j: (i, 0, 0)), big, big, big],
        out_specs=[big] * 4, out_shape=[shp] * 4,
        compiler_params=_cp(("parallel", "parallel")),
    )(cT_all, dmod_loc, w, m, v)


def _sum_adamw(recv, w, m, v, name, tr):
    S = recv.shape[0]
    R, C = w.shape
    assert R % tr == 0 and recv.shape[1:] == (R, C)

    def kern(r_ref, w_ref, m_ref, v_ref, g_ref, dl_ref, m2_ref, v2_ref):
        g = r_ref[0].astype(F32)
        for s in range(1, S):
            g = g + r_ref[s].astype(F32)
        dl, m2, v2 = _adamw(w_ref[...], g, m_ref[...], v_ref[...])
        g_ref[...], dl_ref[...], m2_ref[...], v2_ref[...] = g, dl, m2, v2

    flat = pl.BlockSpec((tr, C), lambda i: (i, 0))
    shp = jax.ShapeDtypeStruct((R, C), F32)
    return pl.pallas_call(
        kern, name=name, grid=(R // tr,),
        in_specs=[pl.BlockSpec((S, tr, C), lambda i: (0, i, 0)), flat, flat, flat],
        out_specs=[flat] * 4, out_shape=[shp] * 4,
        compiler_params=_cp(("parallel",)),
    )(recv, w, m, v)


def _pack(arrs, dtype, row_mult):
    flat = jnp.concatenate([a.reshape(-1).astype(dtype) for a in arrs])
    flat = jnp.pad(flat, (0, -flat.shape[0] % (128 * row_mult)))
    return flat.reshape(-1, 128)


def _pack8(arrs, dtype, row_mult):
    flat = jnp.concatenate([a.reshape(NDEV, -1).astype(dtype) for a in arrs], axis=1)
    flat = jnp.pad(flat, ((0, 0), (0, -flat.shape[1] % (128 * row_mult))))
    return flat.reshape(NDEV, -1, 128)


def _unpack(buf, shapes, lead=()):
    flat = buf.reshape(lead + (-1,))
    out, off = [], 0
    for s in shapes:
        n = math.prod(s)
        out.append(flat[..., off:off + n].reshape(lead + tuple(s)))
        off += n
    return out


def _to_chunks(full, kind):
    if kind == "col":
        x = full.reshape(full.shape[:-1] + (NDEV, full.shape[-1] // NDEV))
        return jnp.moveaxis(x, -2, 0)
    x = full.reshape(full.shape[:-2] + (NDEV, full.shape[-2] // NDEV, full.shape[-1]))
    return jnp.moveaxis(x, -3, 0)


def _from_chunks(g8, kind):
    if kind == "col":
        x = jnp.moveaxis(g8, 0, -2)
        return x.reshape(x.shape[:-2] + (x.shape[-2] * x.shape[-1],))
    x = jnp.moveaxis(g8, 0, -3)
    return x.reshape(x.shape[:-3] + (x.shape[-3] * x.shape[-2], x.shape[-1]))


def _pad_pa(x):
    z = lambda n: jnp.zeros(x.shape[:-1] + (n,), x.dtype)
    return jnp.concatenate([x[..., :1600], z(64), x[..., 1600:1664], z(64), x[..., 1664:1824], z(96)], -1)


def _unpad_pa(x):
    return jnp.concatenate([x[..., :1600], x[..., 1664:1728], x[..., 1792:1952]], -1)


AB_SEGMENTS = ((0, 1600, 0), (1600, 1664, 64), (1664, 1824, 128), (1824, 3360, PAB - 3360))
AB_SHARD = 3360 // NDEV


def _ab_in_padded(g8):
    blocks, at = [], 0
    for start, end, shift in AB_SEGMENTS:
        if start + shift > at:
            blocks.append(jnp.zeros((g8.shape[1], start + shift - at), g8.dtype))
        for j in range(start // AB_SHARD, (end - 1) // AB_SHARD + 1):
            lo, hi = max(start, j * AB_SHARD), min(end, (j + 1) * AB_SHARD)
            blocks.append(g8[j, :, lo - j * AB_SHARD:hi - j * AB_SHARD])
        at = end + shift
    return jnp.concatenate(blocks, axis=1)


def _ab_in_shards(padded):
    shards = []
    for j in range(NDEV):
        pieces = [padded[:, max(start, j * AB_SHARD) + shift:min(end, (j + 1) * AB_SHARD) + shift]
                  for start, end, shift in AB_SEGMENTS if max(start, j * AB_SHARD) < min(end, (j + 1) * AB_SHARD)]
        shards.append(jnp.concatenate(pieces, axis=1))
    return jnp.stack(shards)


def _pad_rows(x, n):
    return jnp.pad(x, ((0, n - x.shape[0]), (0, 0)))


def _bucket_tables():
    qi = jnp.arange(BLK)[:, None]
    ki = jnp.arange(2 * BLK)[None, :]
    rel = BLK + qi - ki
    tabs = []
    for dil in DILS:
        dist = jnp.clip(rel, 0, BLK) * dil
        logd = jnp.log(jnp.maximum(dist, 1).astype(F32) / 16) / math.log(2048 / 16)
        large = jnp.minimum(16 + (logd * 16).astype(jnp.int32), 31)
        tabs.append(jnp.where(dist < 16, dist, large))
    return jnp.stack(tabs)


SHARDED = (("ln_g", "col"), ("ln_b", "col"), ("ab_w_in", "col"), ("rw_w_up", "col"), ("rw_a_up", "col"),
           ("rw_g_up", "col"), ("sc_conv_w", "col"), ("ab_w_out", "row"), ("dil_w_qkv", "col"),
           ("dil_w_out", "col"), ("mlp_w1", "col"), ("mlp_w2", "row"))
FIRST = ("ab_w_in",)
LATER = ("ab_w_out", "dil_w_qkv", "dil_w_out", "mlp_w1", "mlp_w2")
GATHER_BF16 = FIRST + LATER
GATHER_F32 = ("rw_w_up", "rw_a_up", "rw_g_up", "sc_conv_w", "ln_g", "ln_b")
REPLICATED = ("ada_b", "rw_mu", "rw_w0", "rw_a0", "rw_k_k", "rw_k_a", "rw_r_k", "rw_lnx_g", "rw_lnx_b", "rel_bias")
WEIGHTS = ("ada_w", "ada_b", "ln_g", "ln_b", "ab_w_in", "rw_mu", "rw_w0", "rw_w_up", "rw_a0", "rw_a_up",
           "rw_g_up", "rw_k_k", "rw_k_a", "rw_r_k", "rw_lnx_g", "rw_lnx_b", "sc_conv_w", "ab_w_out",
           "dil_w_qkv", "dil_w_out", "rel_bias", "mlp_w1", "mlp_w2")


def _local_step(x0, tgt, mod, W, P, later_weights, early_grads, last_grad):
    row = lambda a: a.reshape(1, -1)
    W = dict(W)
    m6 = mod.reshape(2, 6, 1, D)
    sc = [m6[0, 1], m6[0, 4], m6[1, 1], m6[1, 4]]
    sh = [m6[0, 0], m6[0, 3], m6[1, 0], m6[1, 3]]
    gt = [m6[0, 2], m6[0, 5], m6[1, 2], m6[1, 5]]
    lng = [row(P["ln_g"][0, 0]), row(P["ln_g"][0, 1]), row(P["ln_g"][1, 0]), row(P["ln_g"][1, 1])]
    lnb = [row(P["ln_b"][0, 0]), row(P["ln_b"][0, 1]), row(P["ln_b"][1, 0]), row(P["ln_b"][1, 1])]
    E = jnp.kron(jnp.eye(HEADS, dtype=BF16), jnp.ones((HD, HD), BF16))

    def mod_body(r, p):
        return [r[0] * (1.0 + p[0]) + p[1]], []

    (u0,), _ = _rows("modulate", mod_body, [x0], [sc[0], sh[0]], [(D, BF16)])

    def post_fwd_body(r, p):
        xn, un = _post_ln_mod(r[0], r[1], *p)
        return [xn, un], []

    def post_fwd(s, x, y):
        (xn, un), _ = _rows(f"post_ln_{s}", post_fwd_body, [x, y],
                            [gt[s], lng[s], lnb[s], sc[s + 1], sh[s + 1]], [(D, F32), (D, BF16)])
        return xn, un

    def relu2(acc):
        a = jnp.maximum(acc, 0.0)
        return acc, a * a

    def relu2_bwd(acc, h):
        return (acc * (2.0 * jnp.maximum(h, 0.0)),)

    p = _mm("ab_in", u0, W["ab_w_in"])
    mu = _pad_pa(P["rw_mu"])
    mu_parts = [mu[:, :512], mu[:, 512:1024], mu[:, 1024:1536], mu[:, 1536:1664], mu[:, 1664:1792], mu[:, 1792:]]
    pre_params = mu_parts + [P["rw_w0"], _pad_rows(P["rw_w_up"], 128), P["rw_a0"], _pad_rows(P["rw_a_up"], 128),
                             _pad_rows(P["rw_g_up"], 256), P["rw_k_k"], P["rw_k_a"],
                             P["sc_conv_w"][0:1], P["sc_conv_w"][1:2], P["sc_conv_w"][2:3]]
    pieces = [(p, 512, 0), (p, 512, 1), (p, 512, 2), (p, 128, 12), (p, 128, 13), (p, 256, 7),
              (p, 512, 4), (p, 512, 5), (p, 512, 6)]
    shifted = [0, 1, 2, 3, 4, 5, 6, 8]
    pre_rows = pieces + [pieces[i] + ("prev",) for i in shifted]
    NPR = 19

    def pre_args(r):
        x, prev = r[:9], dict(zip(shifted, r[9:17]))
        down = lambda i, k: _shift_down(x[i], prev[i], k)
        return x[:6] + [down(i, 1) for i in range(6)] + x[6:9] + [down(6, 1), down(8, 1), down(6, 2), down(8, 2)]

    def pre_fwd_body(r, pp):
        return list(_pre_core(pp[0], *pre_args(r), *pp[1:])), []

    (r_, w_, kh_, v_, a_, b_, gate_, yb), _ = _rows(
        "rwkv_pre", pre_fwd_body, pre_rows, [E] + pre_params, [(RW, F32)] * 7 + [(RW, BF16)], tm=256)
    scan_in = [r_, w_, kh_, a_, b_, _cols3(v_, "rwkv_v_columns")]
    ysc, *saved = _scan_fwd(*scan_in)
    post_params = [P["rw_lnx_g"], P["rw_lnx_b"], P["rw_r_k"].reshape(1, RW)]

    def postmix_fwd_body(r, pp):
        return [_post_core(pp[0], *r, *pp[1:])], []

    (ya,), _ = _rows("rwkv_post", postmix_fwd_body, [ysc, r_, kh_, v_, gate_], [E] + post_params,
                     [(RW, BF16)], tm=256)
    cat = jnp.concatenate([ya, yb], axis=1)
    W.update(later_weights(cat))
    y0 = _mm("ab_out", cat, W["ab_w_out"])
    x1, u1 = post_fwd(0, x0, y0)

    h1, a1 = _mm("mlp1_up_0", u1, W["mlp_w1"][0], out=(F32, BF16), epi=relu2)
    y1 = _mm("mlp1_down_0", a1, W["mlp_w2"][0])
    x2, u2 = post_fwd(1, x1, y1)

    pq = _mm("qkv", u2, W["dil_w_qkv"])
    onehotT = (_bucket_tables().reshape(3, 1, NPAIR) == jnp.arange(NBUCKET).reshape(1, NBUCKET, 1)).astype(BF16)
    rbT = P["rel_bias"].reshape(NBUCKET, 3, HEADS).transpose(1, 2, 0)
    bias = _relbias_table(rbT, onehotT).reshape(3, HEADS, BLK, 2 * BLK)
    og, lse = zip(*[_attn_fwd(pq, bias[g], g) for g in range(3)])

    def merge_fwd_body(r, pp):
        return [_merge_core(*r)], []

    (om,), _ = _rows("attn_merge", merge_fwd_body, list(og + lse), [], [(RW, BF16)])
    y2 = _mm("dil_out", om, W["dil_w_out"])
    x3, u3 = post_fwd(2, x2, y2)

    h3, a3 = _mm("mlp1_up_1", u3, W["mlp_w1"][1], out=(F32, BF16), epi=relu2)
    y3 = _mm("mlp1_down_1", a3, W["mlp_w2"][1])

    def last_body(r, pp):
        x, y, tg = r
        xn, vjp = jax.vjp(_post_ln, x, y, *pp)
        err = xn - tg
        dx, dy, dg, dlg, dlb = vjp(err * (1.0 / D))
        loss = jnp.full((1, 128), (0.5 / D) * jnp.sum(err * err), F32)
        return [dx, dy], [loss, dg, dlg, dlb]

    (dxp, dy3), (loss_acc, dg3, dlng3, dlnb3) = _rows(
        "final_ln_loss", last_body, [x3, y3, tgt], [gt[3], lng[3], lnb[3]],
        [(D, F32), (D, BF16)], [(1, 128), (1, D), (1, D), (1, D)])

    G = {}
    dsc, dsh, dgt = [None] * 4, [None] * 4, [None] * 4
    dlng, dlnb = [None] * 4, [None] * 4
    dgt[3], dlng[3], dlnb[3] = dg3, dlng3, dlnb3

    def mlp_bwd(i, u, h, a, dy):
        dh = _mm(f"mlp_dh_{i}", dy, W["mlp_w2"][i], tb=True, out=(BF16,), epi=relu2_bwd, extras=(h,))
        gw2 = _mm(f"mlp_dw2_{i}", a.T, dy)
        du = _mm(f"mlp_du_{i}", dh, W["mlp_w1"][i], tb=True)
        gw1 = _mm(f"mlp_dw1_{i}", u.T, dh)
        return du, gw1, gw2

    def post_bwd_body(r, pp):
        x, y, dxn, dun = r
        _, vjp = jax.vjp(_post_ln_mod, x, y, *pp)
        dx, dy, dg, dlg, dlb, dscn, dshn = vjp((dxn, dun))
        return [dx, dy], [dg, dlg, dlb, dscn, dshn]

    def post_bwd(s, x, y, dxn, dun):
        (dx, dy), (dgt[s], dlng[s], dlnb[s], dsc[s + 1], dsh[s + 1]) = _rows(
            f"post_ln_bwd_{s}", post_bwd_body, [x, y, dxn, dun],
            [gt[s], lng[s], lnb[s], sc[s + 1], sh[s + 1]], [(D, F32), (D, BF16)], [(1, D)] * 5)
        return dx, dy

    du3, gw1_1, gw2_1 = mlp_bwd(1, u3, h3, a3, dy3)
    dxp, dy2 = post_bwd(2, x2, y2, dxp, du3)

    G["dil_w_out"] = _mm("dil_out_dw", om.T, dy2)[None]
    do = _mm("dil_out_dx", dy2, W["dil_w_out"], tb=True)

    def merge_bwd_body(r, pp):
        _, vjp = jax.vjp(_merge_core, *r[:6])
        d = vjp(r[6])
        return list(d[:3]) + [_headsum(d[3 + g], pp[0]) for g in range(3)], []

    mb, _ = _rows("attn_merge_bwd", merge_bwd_body, list(og + lse) + [do], [E],
                  [(RW, F32)] * 6)
    back = [_attn_bwd(pq, bias[g], mb[g], og[g], lse[g], mb[3 + g], g) for g in range(3)]
    dpq = jnp.concatenate([t for dq, dk, dv, _ in back for t in (dq, dk, dv)], axis=1).astype(BF16)
    rb = _relbias_grad(jnp.stack([b[3] for b in back]).reshape(3, HEADS, NPAIR), onehotT)
    G["rel_bias"] = rb.transpose(2, 0, 1).reshape(NBUCKET, 3 * HEADS)
    G["dil_w_qkv"] = _mm("qkv_dw", u2.T, dpq)[None]
    du2 = _mm("qkv_dx", dpq, W["dil_w_qkv"], tb=True)
    dxp, dy1 = post_bwd(1, x1, y1, dxp, du2)

    du1, gw1_0, gw2_0 = mlp_bwd(0, u1, h1, a1, dy1)
    G["mlp_w1"] = jnp.stack([gw1_0, gw1_1])
    G["mlp_w2"] = jnp.stack([gw2_0, gw2_1])
    dxp, dy0 = post_bwd(0, x0, y0, dxp, du1)

    G["ab_w_out"] = _mm("ab_out_dw", cat.T, dy0)[None]
    dcat = _mm("ab_out_dx", dy0, W["ab_w_out"], tb=True)
    post_params = [post_params[0] + early_grads(G)] + post_params[1:]

    def postmix_bwd_body(r, pp):
        _, vjp = jax.vjp(functools.partial(_post_core, pp[0]), *r[:5], *pp[1:])
        d = vjp(r[5])
        return list(d[:5]), list(d[5:])

    (dysc, dr1, dkh1, dv1, dgate), (G["rw_lnx_g"], G["rw_lnx_b"], drk) = _rows(
        "rwkv_post_bwd", postmix_bwd_body, [ysc, r_, kh_, v_, gate_, (dcat, 512, 0)], [E] + post_params,
        [(RW, F32)] * 5, [(1, RW)] * 3, tm=256)
    G["rw_r_k"] = drk.reshape(1, HEADS, HD)
    dr2, dw2, dk2, da2, db2, dv2 = _scan_bwd(*scan_in, _cols3(dysc, "rwkv_dy_columns"), *saved)

    def pre_bwd_body(r, pp):
        prim, ct = pre_args(r[:len(pre_rows)]), r[len(pre_rows):]
        _, vjp = jax.vjp(functools.partial(_pre_core, pp[0]), *prim, *pp[1:])
        cts = (ct[0] + ct[1], ct[2], ct[3] + ct[4], ct[5] + ct[6], ct[7], ct[8], ct[9], ct[10])
        d = vjp(cts)
        z = jnp.zeros_like(d[12])
        dp = jnp.concatenate([d[0], d[1], d[2], d[3], d[4], d[5], d[12], d[13], d[14]], axis=1)
        dp1 = jnp.concatenate([d[6], d[7], d[8], d[9], d[10], d[11], d[15], z, d[16]], axis=1)
        dp2 = jnp.concatenate([d[17], z, d[18]], axis=1)
        return [dp, dp1, dp2], list(d[NPR:])

    acc_shapes = [a.shape for a in pre_params]
    (dp, dp1, dp2), pacc = _rows(
        "rwkv_pre_bwd", pre_bwd_body,
        pre_rows + [dr1, dr2, dw2, dkh1, dk2, dv1, dv2, da2, db2, dgate, (dcat, 512, 1)],
        [E] + pre_params, [(PAB, F32), (PAB, F32), (PB, F32)], acc_shapes, tm=256)
    G["rw_mu"] = _unpad_pa(jnp.concatenate(pacc[:6], axis=1))
    G["rw_w0"], G["rw_a0"], G["rw_k_k"], G["rw_k_a"] = pacc[6], pacc[8], pacc[11], pacc[12]
    G["rw_w_up"] = pacc[7][None, :64]
    G["rw_a_up"] = pacc[9][None, :64]
    G["rw_g_up"] = pacc[10][None, :160]
    G["sc_conv_w"] = jnp.concatenate(pacc[13:16], axis=0)[None]

    def shift_merge_body(r, pp):
        d0, d1, d1_next, d2, d2_next = r
        d = d0 + _shift_up(d1, d1_next, 1)
        return [jnp.concatenate([d[:, :PA], d[:, PA:] + _shift_up(d2, d2_next, 2)], axis=1)], []

    (dpt,), _ = _rows("shift_merge", shift_merge_body,
                      [dp, dp1, (dp1, PAB, 0, "next"), dp2, (dp2, PB, 0, "next")], [], [(PAB, BF16)])
    gin = _mm("ab_in_dw", u0.T, dpt)
    behind = last_grad(_ab_in_shards(gin)[:, None])
    du0 = _mm("ab_in_dx", dpt, W["ab_w_in"], tb=True)

    def mod_bwd_body(r, pp):
        du, dx, x = r
        return [dx + du * (1.0 + pp[0])], [jnp.sum(du * x, axis=0, keepdims=True), jnp.sum(du, axis=0, keepdims=True)]

    (grad_x,), (dsc[0], dsh[0]) = _rows("modulate_bwd", mod_bwd_body, [du0, dxp, x0], [sc[0] + behind], [(D, F32)],
                                        [(1, D), (1, D)])

    G["ln_g"] = jnp.concatenate(dlng, axis=0).reshape(2, 2, D)
    G["ln_b"] = jnp.concatenate(dlnb, axis=0).reshape(2, 2, D)
    dmod = jnp.concatenate([dsh[0], dsc[0], dgt[0], dsh[1], dsc[1], dgt[1],
                            dsh[2], dsc[2], dgt[2], dsh[3], dsc[3], dgt[3]], axis=1).reshape(2, 6 * D)
    return loss_acc[0, 0], grad_x, dmod, G


def kernel(x, c, ada_w, ada_b, ln_g, ln_b, ab_w_in, rw_mu, rw_w0, rw_w_up, rw_a0, rw_a_up, rw_g_up, rw_k_k, rw_k_a, rw_r_k, rw_lnx_g, rw_lnx_b, sc_conv_w, ab_w_out, dil_w_qkv, dil_w_out, rel_bias, mlp_w1, mlp_w2, loss_target, m_ada_w, m_ada_b, m_ln_g, m_ln_b, m_ab_w_in, m_rw_mu, m_rw_w0, m_rw_w_up, m_rw_a0, m_rw_a_up, m_rw_g_up, m_rw_k_k, m_rw_k_a, m_rw_r_k, m_rw_lnx_g, m_rw_lnx_b, m_sc_conv_w, m_ab_w_out, m_dil_w_qkv, m_dil_w_out, m_rel_bias, m_mlp_w1, m_mlp_w2, v_ada_w, v_ada_b, v_ln_g, v_ln_b, v_ab_w_in, v_rw_mu, v_rw_w0, v_rw_w_up, v_rw_a0, v_rw_a_up, v_rw_g_up, v_rw_k_k, v_rw_k_a, v_rw_r_k, v_rw_lnx_g, v_rw_lnx_b, v_sc_conv_w, v_ab_w_out, v_dil_w_qkv, v_dil_w_out, v_rel_bias, v_mlp_w1, v_mlp_w2):
    w = dict(ada_w=ada_w, ada_b=ada_b, ln_g=ln_g, ln_b=ln_b, ab_w_in=ab_w_in, rw_mu=rw_mu, rw_w0=rw_w0,
             rw_w_up=rw_w_up, rw_a0=rw_a0, rw_a_up=rw_a_up, rw_g_up=rw_g_up, rw_k_k=rw_k_k, rw_k_a=rw_k_a,
             rw_r_k=rw_r_k, rw_lnx_g=rw_lnx_g, rw_lnx_b=rw_lnx_b, sc_conv_w=sc_conv_w, ab_w_out=ab_w_out,
             dil_w_qkv=dil_w_qkv, dil_w_out=dil_w_out, rel_bias=rel_bias, mlp_w1=mlp_w1, mlp_w2=mlp_w2)
    m = dict(ada_w=m_ada_w, ada_b=m_ada_b, ln_g=m_ln_g, ln_b=m_ln_b, ab_w_in=m_ab_w_in, rw_mu=m_rw_mu,
             rw_w0=m_rw_w0, rw_w_up=m_rw_w_up, rw_a0=m_rw_a0, rw_a_up=m_rw_a_up, rw_g_up=m_rw_g_up,
             rw_k_k=m_rw_k_k, rw_k_a=m_rw_k_a, rw_r_k=m_rw_r_k, rw_lnx_g=m_rw_lnx_g, rw_lnx_b=m_rw_lnx_b,
             sc_conv_w=m_sc_conv_w, ab_w_out=m_ab_w_out, dil_w_qkv=m_dil_w_qkv, dil_w_out=m_dil_w_out,
             rel_bias=m_rel_bias, mlp_w1=m_mlp_w1, mlp_w2=m_mlp_w2)
    v = dict(ada_w=v_ada_w, ada_b=v_ada_b, ln_g=v_ln_g, ln_b=v_ln_b, ab_w_in=v_ab_w_in, rw_mu=v_rw_mu,
             rw_w0=v_rw_w0, rw_w_up=v_rw_w_up, rw_a0=v_rw_a0, rw_a_up=v_rw_a_up, rw_g_up=v_rw_g_up,
             rw_k_k=v_rw_k_k, rw_k_a=v_rw_k_a, rw_r_k=v_rw_r_k, rw_lnx_g=v_rw_lnx_g, rw_lnx_b=v_rw_lnx_b,
             sc_conv_w=v_sc_conv_w, ab_w_out=v_ab_w_out, dil_w_qkv=v_dil_w_qkv, dil_w_out=v_dil_w_out,
             rel_bias=v_rel_bias, mlp_w1=v_mlp_w1, mlp_w2=v_mlp_w2)
    kinds = dict(SHARDED)
    me = 4 * lax.axis_index("x") + 2 * lax.axis_index("y") + lax.axis_index("c")
    ncol = ada_w.shape[2]

    small = _all_gather(_pack([c] + [w[n] for n in GATHER_F32], F32, 8), "gather_small")
    parts = _unpack(small, [c.shape] + [w[n].shape for n in GATHER_F32], (NDEV,))
    c_all = parts[0].reshape(NDEV, D)
    P = {n: _from_chunks(t, kinds[n]) for n, t in zip(GATHER_F32, parts[1:])}
    P = {n: (t if n in ("ln_g", "ln_b") else t[0]) for n, t in P.items()}
    for n in REPLICATED[1:]:
        P[n] = w[n]
    def full(n, t):
        t = _from_chunks(t, kinds[n])
        return t if n in ("mlp_w1", "mlp_w2") else t[0]

    (first,) = _all_gather_many([ab_w_in.astype(BF16)], "gather_first_weight")
    W = {"ab_w_in": _ab_in_padded(first[:, 0])}

    ada_b_loc = lax.dynamic_slice(ada_b, (0, ncol * me), (2, ncol))
    mod_part = _ada_mod(c_all, ada_w, ada_b_loc)
    mod_all = _all_gather(mod_part.reshape(-1, 128), "gather_mod").reshape(NDEV, 2, NDEV, ncol)
    mod = lax.dynamic_index_in_dim(mod_all, me, axis=2, keepdims=False)
    mod = mod.transpose(1, 0, 2).reshape(2, 6 * D)

    behind = (mod[0, 0] * 0.0).astype(BF16)
    later = _exchange_start([w[n].astype(BF16) + (behind if n == LATER[0] else 0) for n in LATER], True,
                            "gather_later_weights_start")
    mod = mod + later[-1][0, 0]

    def later_weights(after):
        lands = _exchange_wait(later, True, after, "gather_later_weights_wait")
        return {n: full(n, t) for n, t in zip(LATER, lands)}

    sent = []

    def early_grads(G):
        sent.append(_exchange_start([_to_chunks(G[n], kinds[n]).astype(BF16) for n in LATER], False,
                                    "exchange_later_grads_start"))
        return sent[0][-1][0, 0]

    def last_grad(shards):
        sent.append(_exchange_start([shards.astype(BF16)], False, "exchange_last_grad_start"))
        return sent[1][-1][0, 0]

    loss_part, grad_x, dmod, G = _local_step(x[0], loss_target[0], mod, W, P, later_weights, early_grads, last_grad)
    G["ada_b"] = dmod
    big_out = {}

    def update(n, contributions):
        cols = w[n].shape[-1]
        flat = lambda t: t.reshape(-1, cols)
        rows = flat(w[n]).shape[0]
        outs = _sum_adamw(contributions.reshape(-1, rows, cols), flat(w[n]), flat(m[n]), flat(v[n]),
                          f"sum_adamw_{n}", min(rows, 256))
        big_out[n] = [o.reshape(w[n].shape) for o in outs]

    for n, r in zip(LATER, _exchange_wait(sent[0], False, grad_x, "exchange_later_grads_wait")):
        update(n, r)
    settled = big_out[LATER[-1]][0].reshape(-1)[0] * 0.0

    rep_shapes = [w[n].shape for n in REPLICATED] + [(1,), (1,)]
    rep_all = _all_gather(_pack([G[n] for n in REPLICATED] + [loss_part, settled], F32, 8), "gather_replicated_grads")
    pk = lambda d: _pack([d[n] for n in REPLICATED] + [jnp.zeros((2,), F32)], F32, 8)
    rep_out = _sum_adamw(rep_all, pk(w), pk(m), pk(v), "sum_adamw_replicated", rep_all.shape[1])
    loss = _unpack(rep_out[0], rep_shapes)[-2][0]
    rep_out = [dict(zip(REPLICATED, _unpack(o, rep_shapes))) for o in rep_out]

    dmod_all = _unpack(rep_all, [(2, 6 * D)], (NDEV,))[0]
    dmod_loc = lax.dynamic_slice(dmod_all, (0, 0, ncol * me), (NDEV, 2, ncol)).transpose(1, 0, 2)
    ada_out = _ada_grad_adamw(c_all.T, dmod_loc, ada_w, m_ada_w, v_ada_w)

    names = [n for n, _ in SHARDED if n not in GATHER_BF16]
    shard_shapes = [w[n].shape for n in names]
    chunks = _pack8([_to_chunks(G[n], kinds[n]) for n in names], F32, 8)
    recv = _all_to_all(chunks, "exchange_small_grads")
    pk = lambda d: _pack([d[n] for n in names], F32, 8)
    sh_out = _sum_adamw(recv, pk(w), pk(m), pk(v), "sum_adamw_small", recv.shape[1])
    sh_out = [dict(zip(names, _unpack(o, shard_shapes))) for o in sh_out]

    (landed,) = _exchange_wait(sent[1], False, sh_out[0][names[0]], "exchange_last_grad_wait")
    update("ab_w_in", landed)
    sh_out = [{**d, **{n: big_out[n][i] for n in GATHER_BF16}} for i, d in enumerate(sh_out)]

    def pick(i, n):
        if n == "ada_w":
            return ada_out[i]
        return rep_out[i][n] if n in REPLICATED else sh_out[i][n]

    outs = [loss, grad_x[None]]
    for i in range(4):
        outs += [pick(i, n) for n in WEIGHTS]
    return tuple(outs)
```

```python
import functools
import math

import jax
import jax.numpy as jnp
from jax import lax
from jax.experimental import pallas as pl
from jax.experimental.pallas import tpu as pltpu

F32 = jnp.float32
BF16 = jnp.bfloat16
HI = lax.Precision.HIGHEST

NDEV = 8
T = 2048
D = 1024
DFF = 4096
HEADS = 8
HD = 64
RW = 512
PA = 2048
PB = 1536
PAB = PA + PB
QKV = 4608
DILS = (1, 4, 16)
BLK = 128
ALPHA = 4.0 ** 0.25
LN_EPS = 1e-5
GN_EPS = 64e-5
ADAM_LR, ADAM_B1, ADAM_B2, ADAM_EPS, ADAM_WD, ADAM_STEP = 0.001, 0.9, 0.999, 1e-8, 0.01, 10
VMEM_LIMIT = 56 * 1024 * 1024


def _cp(sem):
    return pltpu.CompilerParams(dimension_semantics=sem, vmem_limit_bytes=VMEM_LIMIT)


def _slot(px, py, pc):
    return 4 * px + 2 * py + pc


def _all_gather(x, name):
    R, C = x.shape

    def body(x_ref, out_ref, send_sems, recv_sems, local_sem):
        xi, yi, ci = lax.axis_index("x"), lax.axis_index("y"), lax.axis_index("c")
        me, sibling = (xi, yi, ci), (xi, yi, 1 - ci)
        chips = [(1 - xi, yi), (xi, 1 - yi), (1 - xi, 1 - yi)]

        def rows(px, py, pc):
            return out_ref.at[_slot(px, py, pc)]

        def copy(k, block, to, src=None):
            return pltpu.make_async_remote_copy(
                src_ref=rows(*block) if src is None else src, dst_ref=rows(*block),
                send_sem=send_sems.at[k], recv_sem=recv_sems.at[k],
                device_id=to, device_id_type=pl.DeviceIdType.MESH)

        mine = pltpu.make_async_copy(x_ref, rows(*me), local_sem)
        mine.start()
        first = [copy(0, me, sibling, src=x_ref)]
        first += [copy(1 + j, me, (*chip, ci), src=x_ref) for j, chip in enumerate(chips)]
        for cp in first:
            cp.start()
        passed = [copy(4 + j, (*chip, ci), sibling) for j, chip in enumerate(chips)]
        for j, chip in enumerate(chips):
            copy(1 + j, (*chip, ci), me).wait_recv()
            passed[j].start()
        copy(0, sibling, me).wait_recv()
        for j, chip in enumerate(chips):
            copy(4 + j, (*chip, 1 - ci), me).wait_recv()
        for cp in first + passed:
            cp.wait_send()
        mine.wait()

    return pl.pallas_call(
        body, name=name,
        out_shape=jax.ShapeDtypeStruct((NDEV, R, C), x.dtype),
        in_specs=[pl.BlockSpec(memory_space=pl.ANY)],
        out_specs=pl.BlockSpec(memory_space=pl.ANY),
        scratch_shapes=[pltpu.SemaphoreType.DMA((7,)), pltpu.SemaphoreType.DMA((7,)),
                        pltpu.SemaphoreType.DMA(())],
    )(x)


def _all_to_all(g, name):
    _, R, C = g.shape

    def body(g_ref, out_ref, send_sems, recv_sems, local_sem):
        xi, yi, ci = lax.axis_index("x"), lax.axis_index("y"), lax.axis_index("c")
        my_slot = _slot(xi, yi, ci)
        mine = pltpu.make_async_copy(g_ref.at[my_slot], out_ref.at[my_slot], local_sem)
        mine.start()
        copies = []
        for k in range(1, 8):
            px = 1 - xi if k & 4 else xi
            py = 1 - yi if k & 2 else yi
            pc = 1 - ci if k & 1 else ci
            peer_slot = _slot(px, py, pc)
            copies.append((
                pltpu.make_async_remote_copy(
                    src_ref=g_ref.at[peer_slot], dst_ref=out_ref.at[my_slot],
                    send_sem=send_sems.at[k - 1], recv_sem=recv_sems.at[k - 1],
                    device_id=(px, py, pc), device_id_type=pl.DeviceIdType.MESH),
                pltpu.make_async_remote_copy(
                    src_ref=g_ref.at[peer_slot], dst_ref=out_ref.at[peer_slot],
                    send_sem=send_sems.at[k - 1], recv_sem=recv_sems.at[k - 1],
                    device_id=(px, py, pc), device_id_type=pl.DeviceIdType.MESH)))
        for send, _ in copies:
            send.start()
        for _, recv in copies:
            recv.wait_recv()
        for send, _ in copies:
            send.wait_send()
        mine.wait()

    return pl.pallas_call(
        body, name=name,
        out_shape=jax.ShapeDtypeStruct((NDEV, R, C), g.dtype),
        in_specs=[pl.BlockSpec(memory_space=pl.ANY)],
        out_specs=pl.BlockSpec(memory_space=pl.ANY),
        scratch_shapes=[pltpu.SemaphoreType.DMA((7,)), pltpu.SemaphoreType.DMA((7,)),
                        pltpu.SemaphoreType.DMA(())],
    )(g)


def _my_slot():
    return _slot(lax.axis_index("x"), lax.axis_index("y"), lax.axis_index("c"))


def _put_own(buf, own, slot):
    return lax.dynamic_update_index_in_dim(buf, own, slot, 0)


def _hbm_call(body, name, ins, out_shapes, n_sems):
    anyspec = pl.BlockSpec(memory_space=pl.ANY)
    return pl.pallas_call(
        body, name=name, out_shape=out_shapes,
        in_specs=[anyspec] * len(ins), out_specs=[anyspec] * len(out_shapes),
        scratch_shapes=[pltpu.SemaphoreType.DMA(s) for s in n_sems],
    )(*ins)


def _all_gather_many(xs, name):
    n = len(xs)

    def body(*refs):
        x_refs, o_refs = refs[:n], refs[n:2 * n]
        send_sems, recv_sems = refs[2 * n:]
        xi, yi, ci = lax.axis_index("x"), lax.axis_index("y"), lax.axis_index("c")
        me, sibling = (xi, yi, ci), (xi, yi, 1 - ci)
        chips = [(1 - xi, yi), (xi, 1 - yi), (1 - xi, 1 - yi)]

        def copy(i, k, block, to, src=None):
            dst = o_refs[i].at[_slot(*block)]
            return pltpu.make_async_remote_copy(
                src_ref=dst if src is None else src, dst_ref=dst,
                send_sem=send_sems.at[i, k], recv_sem=recv_sems.at[i, k],
                device_id=to, device_id_type=pl.DeviceIdType.MESH)

        sends = []
        for i in range(n):
            sends += [copy(i, 1 + j, me, (*chip, ci), src=x_refs[i]) for j, chip in enumerate(chips)]
            sends.append(copy(i, 0, me, sibling, src=x_refs[i]))
        for cp in sends:
            cp.start()
        for j, chip in enumerate(chips):
            for i in range(n):
                copy(i, 1 + j, (*chip, ci), me).wait_recv()
                passed = copy(i, 4 + j, (*chip, ci), sibling)
                passed.start()
                sends.append(passed)
        for i in range(n):
            copy(i, 0, sibling, me).wait_recv()
            for j, chip in enumerate(chips):
                copy(i, 4 + j, (*chip, 1 - ci), me).wait_recv()
        for cp in sends:
            cp.wait_send()

    outs = _hbm_call(body, name, xs, [jax.ShapeDtypeStruct((NDEV,) + x.shape, x.dtype) for x in xs],
                     [(n, 7), (n, 7)])
    return [_put_own(o, x[None], _my_slot()) for o, x in zip(outs, xs)]


def _peers(xi, yi, ci):
    return [(1 - xi if k & 4 else xi, 1 - yi if k & 2 else yi, 1 - ci if k & 1 else ci) for k in range(1, 8)]


def _direct_copy(src_refs, land_refs, send_sems, recv_sems, i, k, peer, my_slot, gather):
    src = src_refs[i] if gather else src_refs[i].at[_slot(*peer)]
    return pltpu.make_async_remote_copy(
        src_ref=src, dst_ref=land_refs[i].at[my_slot], send_sem=send_sems.at[7 * i + k], recv_sem=recv_sems.at[7 * i + k],
        device_id=peer, device_id_type=pl.DeviceIdType.MESH)


def _exchange_start(srcs, gather, name):
    n = len(srcs)
    lands = [lax.empty(((NDEV,) + s.shape) if gather else s.shape, s.dtype) for s in srcs]

    def body(*refs):
        s_refs, l_refs = refs[:n], refs[n:2 * n]
        send_sems, recv_sems = refs[2 * n], refs[2 * n + 1]
        token = refs[2 * n + 2 + 2 * n]
        xi, yi, ci = lax.axis_index("x"), lax.axis_index("y"), lax.axis_index("c")
        my_slot = _slot(xi, yi, ci)
        for k, peer in enumerate(_peers(xi, yi, ci)):
            for i in range(n):
                _direct_copy(s_refs, l_refs, send_sems, recv_sems, i, k, peer, my_slot, gather).start()
        token[...] = jnp.zeros_like(token)

    hbm = pl.BlockSpec(memory_space=pltpu.HBM)
    sem = pl.BlockSpec(memory_space=pltpu.SEMAPHORE)
    both = list(srcs) + lands
    return pl.pallas_call(
        body, name=name,
        out_shape=(pltpu.SemaphoreType.DMA((7 * n,)), pltpu.SemaphoreType.DMA((7 * n,)),
                   *[pltpu.HBM(t.shape, t.dtype) for t in both], jax.ShapeDtypeStruct((8, 128), F32)),
        in_specs=[hbm] * (2 * n),
        out_specs=(sem, sem, *[hbm] * (2 * n), pl.BlockSpec(memory_space=pltpu.VMEM)),
        input_output_aliases={i: 2 + i for i in range(2 * n)},
        compiler_params=pltpu.CompilerParams(has_side_effects=pltpu.SideEffectType.DATAFLOW_SIDE_EFFECTING),
    )(*[pltpu.with_memory_space_constraint(t, pltpu.HBM) for t in both])


def _exchange_wait(started, gather, after, name):
    send_sems, recv_sems, *thru, _ = started
    n = len(thru) // 2

    def body(*refs):
        s_refs, l_refs = refs[:n], refs[n:2 * n]
        send_sems, recv_sems = refs[2 * n], refs[2 * n + 1]
        xi, yi, ci = lax.axis_index("x"), lax.axis_index("y"), lax.axis_index("c")
        my_slot = _slot(xi, yi, ci)
        for k, peer in enumerate(_peers(xi, yi, ci)):
            for i in range(n):
                _direct_copy(s_refs, l_refs, send_sems, recv_sems, i, k, peer, my_slot, gather).wait_send()
                _direct_copy(s_refs, l_refs, send_sems, recv_sems, i, k, peer, _slot(*peer), gather).wait_recv()

    hbm = pl.BlockSpec(memory_space=pltpu.HBM)
    sem = pl.BlockSpec(memory_space=pltpu.SEMAPHORE)
    outs = pl.pallas_call(
        body, name=name,
        out_shape=tuple(pltpu.HBM(t.shape, t.dtype) for t in thru),
        in_specs=[hbm] * (2 * n) + [sem, sem, pl.BlockSpec(memory_space=pl.ANY)],
        out_specs=tuple([hbm] * (2 * n)),
        input_output_aliases={i: i for i in range(2 * n)},
        compiler_params=pltpu.CompilerParams(has_side_effects=pltpu.SideEffectType.DATAFLOW_SIDE_EFFECTING),
    )(*thru, send_sems, recv_sems, after)
    slot = _my_slot()
    own = [s[None] if gather else lax.dynamic_index_in_dim(s, slot, 0, keepdims=True) for s in outs[:n]]
    return [_put_own(land, o, slot) for land, o in zip(outs[n:], own)]


def _mm(name, a, b, tb=False, out=(F32,), epi=None, extras=(), tm=1024, tn=512, tk_cap=2048):
    M, K = a.shape
    N = b.shape[0] if tb else b.shape[1]
    tm, tn = min(tm, M), min(tn, N)
    tk = max(t for t in range(128, min(K, tk_cap) + 1, 128) if K % t == 0)
    assert M % tm == 0 and N % tn == 0 and K % tk == 0, (name, M, N, K)
    nk = K // tk
    ne, no = len(extras), len(out)
    dims = (((1,), (1 if tb else 0,)), ((), ()))
    flipped = [isinstance(o, tuple) for o in out]

    def kern(*refs):
        a_ref, b_ref = refs[:2]
        e_refs = refs[2:2 + ne]
        o_refs = refs[2 + ne:2 + ne + no]

        def finish(acc):
            outs = epi(acc, *[e[...] for e in e_refs]) if epi is not None else (acc,)
            for o_ref, o, flip in zip(o_refs, outs, flipped):
                o_ref[...] = (o.T if flip else o).astype(o_ref.dtype)

        part = lax.dot_general(a_ref[...], b_ref[...], dims, preferred_element_type=F32)
        if nk == 1:
            finish(part)
            return
        acc_ref = refs[-1]
        k = pl.program_id(2)

        @pl.when(k == 0)
        def _():
            acc_ref[...] = part

        @pl.when(k > 0)
        def _():
            acc_ref[...] += part

        @pl.when(k == nk - 1)
        def _():
            finish(acc_ref[...])

    b_spec = (pl.BlockSpec((tn, tk), lambda i, j, k: (j, k)) if tb
              else pl.BlockSpec((tk, tn), lambda i, j, k: (k, j)))
    tile = pl.BlockSpec((tm, tn), lambda i, j, k: (i, j))
    tile_t = pl.BlockSpec((tn, tm), lambda i, j, k: (j, i))
    res = pl.pallas_call(
        kern, name=name, grid=(M // tm, N // tn, nk),
        in_specs=[pl.BlockSpec((tm, tk), lambda i, j, k: (i, k)), b_spec] + [tile] * ne,
        out_specs=[tile_t if flip else tile for flip in flipped],
        out_shape=[jax.ShapeDtypeStruct((N, M), o[0]) if flip else jax.ShapeDtypeStruct((M, N), o)
                   for o, flip in zip(out, flipped)],
        scratch_shapes=[pltpu.VMEM((tm, tn), F32)] if nk > 1 else [],
        compiler_params=_cp(("parallel", "parallel", "arbitrary")),
    )(a, b, *extras)
    return res[0] if no == 1 else res


HALO = 8


def _rows(name, body, rows, params, out_rows, out_accs=(), tm=256):
    views = [r if isinstance(r, tuple) else (r, r.shape[1], 0) for r in rows]
    n = views[0][0].shape[0]
    assert n % tm == 0 and tm % HALO == 0
    nr, npar, nor, noa = len(views), len(params), len(out_rows), len(out_accs)
    flipped = [len(o) == 3 for o in out_rows]

    def row_spec(width, cb, halo=None):
        per, last = tm // HALO, n // HALO - 1
        if halo == "prev":
            return pl.BlockSpec((HALO, width), lambda i: (jnp.maximum(i * per - 1, 0), cb))
        if halo == "next":
            return pl.BlockSpec((HALO, width), lambda i: (jnp.minimum((i + 1) * per, last), cb))
        return pl.BlockSpec((tm, width), lambda i: (i, cb))

    def kern(*refs):
        r_refs = refs[:nr]
        p_refs = refs[nr:nr + npar]
        o_refs = refs[nr + npar:nr + npar + nor]
        a_refs = refs[nr + npar + nor:]
        outs, accs = body([r[...] for r in r_refs], [p[...] for p in p_refs])
        assert len(outs) == nor and len(accs) == noa, (name, len(outs), len(accs))
        for o_ref, o, flip in zip(o_refs, outs, flipped):
            o_ref[...] = (o.T if flip else o).astype(o_ref.dtype)
        if noa:
            @pl.when(pl.program_id(0) == 0)
            def _():
                for a_ref in a_refs:
                    a_ref[...] = jnp.zeros_like(a_ref)

            for a_ref, a in zip(a_refs, accs):
                a_ref[...] += a.astype(F32)

    def whole(shape):
        nd = len(shape)
        return pl.BlockSpec(tuple(shape), lambda i, nd=nd: (0,) * nd)

    in_specs = [row_spec(*v[1:]) for v in views]
    in_specs += [whole(p.shape) for p in params]
    out_specs = [pl.BlockSpec((o[0], tm), lambda i: (0, i)) if flip else pl.BlockSpec((tm, o[0]), lambda i: (i, 0))
                 for o, flip in zip(out_rows, flipped)]
    out_specs += [whole(s) for s in out_accs]
    out_shape = [jax.ShapeDtypeStruct((o[0], n) if flip else (n, o[0]), o[1]) for o, flip in zip(out_rows, flipped)]
    out_shape += [jax.ShapeDtypeStruct(tuple(s), F32) for s in out_accs]
    res = pl.pallas_call(
        kern, name=name, grid=(n // tm,), in_specs=in_specs, out_specs=out_specs,
        out_shape=out_shape, compiler_params=_cp(("arbitrary",)),
    )(*[v[0] for v in views], *params)
    return res[:nor], res[nor:]


def _shift_down(x, prev, k):
    head = jnp.where(pl.program_id(0) == 0, 0.0, pltpu.roll(prev, k, axis=0))
    row = lax.broadcasted_iota(jnp.int32, x.shape, 0)
    return jnp.where(row < k, jnp.tile(head, (x.shape[0] // HALO, 1)), pltpu.roll(x, k, axis=0))


def _shift_up(x, nxt, k):
    n = x.shape[0]
    tail = jnp.where(pl.program_id(0) == pl.num_programs(0) - 1, 0.0, pltpu.roll(nxt, HALO - k, axis=0))
    row = lax.broadcasted_iota(jnp.int32, x.shape, 0)
    return jnp.where(row >= n - k, jnp.tile(tail, (n // HALO, 1)), pltpu.roll(x, n - k, axis=0))


@jax.custom_vjp
def _headsum(x, e):
    return sum(jnp.dot(p, e, preferred_element_type=F32) for p in _split3(x))


_headsum.defvjp(lambda x, e: (_headsum(x, e), e), lambda e, ct: (_headsum(ct, e), None))


def _softplus(z):
    return jnp.maximum(z, 0.0) + jnp.log(1.0 + jnp.exp(jnp.minimum(z, -z)))


def _post_ln(x, y, g, lng, lnb):
    z = ALPHA * x + (1.0 + g) * y
    mu = jnp.mean(z, axis=-1, keepdims=True)
    zc = z - mu
    var = jnp.mean(zc * zc, axis=-1, keepdims=True)
    return zc * lax.rsqrt(var + LN_EPS) * lng + lnb


def _post_ln_mod(x, y, g, lng, lnb, scn, shn):
    xn = _post_ln(x, y, g, lng, lnb)
    return xn, xn * (1.0 + scn) + shn


def _pre_core(E, r_, k_, v_, wd_, ad_, gd_, r1, k1, v1, wd1, ad1, gd1, h, bg, cg, h1, cg1, h2, cg2,
              mu_r, mu_k, mu_v, mu_wd, mu_ad, mu_gd, w0, w_up, a0, a_up, g_up, k_k, k_a,
              cw0, cw1, cw2):
    def mix(x, x1, mu):
        return x + mu * (x1 - x)

    r, k, v = mix(r_, r1, mu_r), mix(k_, k1, mu_k), mix(v_, v1, mu_v)
    wd, ad, gd = mix(wd_, wd1, mu_wd), mix(ad_, ad1, mu_ad), mix(gd_, gd1, mu_gd)
    logw = -_softplus(-(w0 + jnp.dot(jnp.tanh(wd), w_up, preferred_element_type=F32))) - 0.5
    decay = jnp.exp(-jnp.exp(logw))
    iclr = jax.nn.sigmoid(a0 + jnp.dot(ad, a_up, preferred_element_type=F32))
    gate = jnp.dot(jax.nn.sigmoid(gd), g_up, preferred_element_type=F32)
    kk0 = k * k_k
    nrm = jnp.sqrt(_headsum(kk0 * kk0, E))
    kk = kk0 / jnp.maximum(nrm, 1e-12)
    kh = k * (1.0 + (iclr - 1.0) * k_a)
    yb = bg * (cw2 * (cg * h) + cw1 * (cg1 * h1) + cw0 * (cg2 * h2))
    return r, decay, kh, v, -kk, kk * iclr, gate, yb


def _post_core(E, y, r, kh, v, gate, lnx_g, lnx_b, rk):
    def seg(t):
        return _headsum(t, E)

    mean = seg(y) * (1.0 / HD)
    yc = y - mean
    var = seg(yc * yc) * (1.0 / HD)
    gn = yc * lax.rsqrt(var + GN_EPS) * lnx_g + lnx_b
    bonus = seg(r * kh * rk) * v
    return (gn + bonus) * gate


def _merge_core(o0, o1, o2, l0, l1, l2):
    m = jnp.maximum(jnp.maximum(l0, l1), l2)
    e0, e1, e2 = jnp.exp(l0 - m), jnp.exp(l1 - m), jnp.exp(l2 - m)
    den = e0 + e1 + e2
    return (e0 * o0 + e1 * o1 + e2 * o2) / den


CHUNK = 128
HALF = 64
HP = HEADS // 2
LW = 2 * HD
NCHUNK = T // CHUNK


def _split3(x):
    hi = x.astype(BF16)
    r1 = x - hi.astype(F32)
    mid = r1.astype(BF16)
    return hi, mid, (r1 - mid.astype(F32)).astype(BF16)


def _cols3(x, name):
    def kern(x_ref, o_ref):
        xt = x_ref[...].T
        left = lax.broadcasted_iota(jnp.int32, (HD, CHUNK), 1) < HALF
        for p in range(HP):
            a, b = xt[p * LW:p * LW + HD], xt[p * LW + HD:(p + 1) * LW]
            halves = [jnp.where(left, a, pltpu.roll(b, HALF, axis=1)), jnp.where(left, pltpu.roll(a, HALF, axis=1), b)]
            for h, tile in enumerate(halves):
                for j, part in enumerate(_split3(tile)):
                    o_ref[p, :, (3 * h + j) * LW:(3 * h + j + 1) * LW] = part

    return pl.pallas_call(
        kern, name=name, grid=(NCHUNK,),
        in_specs=[pl.BlockSpec((CHUNK, RW), lambda c: (c, 0))],
        out_specs=pl.BlockSpec((HP, HD, 6 * CHUNK), lambda c: (0, 0, c)),
        out_shape=jax.ShapeDtypeStruct((HP, HD, 6 * T), BF16),
        compiler_params=_cp(("parallel",)),
    )(x)


def _pick_codes():
    row = lax.broadcasted_iota(jnp.int32, (6 * HALF, LW), 0)
    col = lax.broadcasted_iota(jnp.int32, (6 * HALF, LW), 1)
    same = ((row & (LW - 1)) >= HALF) == (col >= HD)
    return jnp.where(same, row & (HALF - 1), -1).astype(BF16)


def _column(block_ref, codes, half, i):
    pick = jnp.where(codes == i.astype(BF16), jnp.ones((), BF16), jnp.zeros((), BF16))
    block = block_ref[:, :, half * 6 * HALF:(half + 1) * 6 * HALF].reshape(HP * HD, 6 * HALF)
    return jnp.dot(block, pick, preferred_element_type=F32)


def _halfsums(x, row, left1):
    row_l = jnp.where(left1, row, 0.0)
    return (jnp.sum(x * row_l, axis=1, keepdims=True), jnp.sum(x * (row - row_l), axis=1, keepdims=True))


def _pair_rows(row):
    return [row[:, p * LW:(p + 1) * LW] for p in range(HP)]


def _store_columns(ref, p, t_mask, cols):
    for j, col in enumerate(cols):
        pltpu.store(ref.at[pl.ds(2 * p + j, 1)], jnp.broadcast_to(col[None], (1, HD, CHUNK)), mask=t_mask[None])


def _columns_to_rows(cols_ref, rows_ref):
    for p in range(HP):
        rows_ref[:, p * LW:(p + 1) * LW] = cols_ref[2 * p:2 * p + 2].reshape(LW, CHUNK).T


NHALF = T // HALF
HALVES = CHUNK // HALF


def _scan_fwd(r, w, k, a, b, v3):
    def kern(r_ref, w_ref, k_ref, a_ref, b_ref, v_ref, y_ref, ck_ref, st_hbm, sa_hbm,
             s_ref, vb_ref, yc_ref, st_ref, sa_ref, sems):
        c = pl.program_id(0)

        @pl.when(c == 0)
        def _():
            s_ref[...] = jnp.zeros_like(s_ref)

        lane = lax.broadcasted_iota(jnp.int32, (HD, CHUNK), 1)
        left = lane < HD
        left1 = lax.broadcasted_iota(jnp.int32, (1, LW), 1) < HD
        codes = _pick_codes()

        def flush(slot, half_index):
            return [pltpu.make_async_copy(src.at[slot], dst.at[half_index], sems.at[j, slot])
                    for j, (src, dst) in enumerate(((st_ref, st_hbm), (sa_ref, sa_hbm)))]

        for half in range(HALVES):
            ck_ref[half] = s_ref[...]
            vb_ref[...] = _column(v_ref, codes, half, jnp.int32(0))

            @pl.when(c > 0)
            def _():
                for cp in flush(half, (c - 1) * HALVES + half):
                    cp.wait()

            def step(i, carry):
                t = half * HALF + i
                row = lambda ref: _pair_rows(ref[pl.ds(t, 1), :])
                S = [s_ref[p] for p in range(HP)]
                sa = [jnp.where(left, *_halfsums(s, a, left1)) for s, a in zip(S, row(a_ref))]
                S = [s * w + c_ * b + vb_ref[pl.ds(p * HD, HD), :] * k
                     for p, (s, w, c_, b, k) in enumerate(zip(S, row(w_ref), sa, row(b_ref), row(k_ref)))]
                for p, (s, c_) in enumerate(zip(S, sa)):
                    s_ref[p] = s
                    st_ref[half, i, p] = s
                    sa_ref[half, i, p] = c_
                for p, (s, r) in enumerate(zip(S, row(r_ref))):
                    _store_columns(yc_ref, p, lane == t, _halfsums(s, r, left1))
                vb_ref[...] = _column(v_ref, codes, half, i + 1)
                return carry

            lax.fori_loop(0, HALF, step, 0, unroll=8)
            for cp in flush(half, c * HALVES + half):
                cp.start()
        _columns_to_rows(yc_ref, y_ref)

        @pl.when(c == NCHUNK - 1)
        def _():
            for half in range(HALVES):
                for cp in flush(half, c * HALVES + half):
                    cp.wait()

    rowblk = pl.BlockSpec((CHUNK, RW), lambda c: (c, 0))
    saved = jax.ShapeDtypeStruct((NHALF, HALF, HP, HD, LW), F32)
    stage = pltpu.VMEM((HALVES, HALF, HP, HD, LW), F32)
    return pl.pallas_call(
        kern, name="rwkv_scan_fwd", grid=(NCHUNK,),
        in_specs=[rowblk] * 5 + [pl.BlockSpec((HP, HD, 6 * CHUNK), lambda c: (0, 0, c))],
        out_specs=[rowblk, pl.BlockSpec((HALVES, HP, HD, LW), lambda c: (c, 0, 0, 0)),
                   pl.BlockSpec(memory_space=pl.ANY), pl.BlockSpec(memory_space=pl.ANY)],
        out_shape=[jax.ShapeDtypeStruct((T, RW), F32), jax.ShapeDtypeStruct((NHALF, HP, HD, LW), F32), saved, saved],
        scratch_shapes=[pltpu.VMEM((HP, HD, LW), F32), pltpu.VMEM((HP * HD, LW), F32),
                        pltpu.VMEM((HEADS, HD, CHUNK), F32), stage, stage, pltpu.SemaphoreType.DMA((2, HALVES))],
        compiler_params=_cp(("arbitrary",)),
    )(r, w, k, a, b, v3)


def _scan_bwd(r, w, k, a, b, v3, dy3, ck, st, sa):
    def kern(r_ref, w_ref, k_ref, a_ref, b_ref, v_ref, dy_ref, ck_ref, st_hbm, sa_hbm,
             dr_ref, dw_ref, dk_ref, da_ref, db_ref, dv_ref, ds_ref, sb_ref, sa_ref, pick_ref, dvc_ref, sems):
        c = pl.program_id(0)
        chunk = NCHUNK - 1 - c

        @pl.when(c == 0)
        def _():
            ds_ref[...] = jnp.zeros_like(ds_ref)

        lane = lax.broadcasted_iota(jnp.int32, (HD, CHUNK), 1)
        left = lane < HD
        left1 = lax.broadcasted_iota(jnp.int32, (1, LW), 1) < HD
        codes = _pick_codes()

        def rowsum(x):
            return jnp.sum(x, axis=0, keepdims=True)

        def fetch(slot, half_index):
            return [pltpu.make_async_copy(st_hbm.at[half_index], sb_ref.at[slot, pl.ds(1, HALF)], sems.at[0, slot]),
                    pltpu.make_async_copy(sa_hbm.at[half_index], sa_ref.at[slot], sems.at[1, slot])]

        def picks(half, i):
            pick_ref[pl.ds(0, HP * HD), :] = _column(v_ref, codes, half, i)
            pick_ref[pl.ds(HP * HD, HP * HD), :] = _column(dy_ref, codes, half, i)

        @pl.when(c == 0)
        def _():
            for cp in fetch(HALVES - 1, chunk * HALVES + HALVES - 1):
                cp.start()

        for half in reversed(range(HALVES)):
            base = half * HALF
            for cp in fetch(half, chunk * HALVES + half):
                cp.wait()
            if half:
                for cp in fetch(half - 1, chunk * HALVES + half - 1):
                    cp.start()
            else:
                @pl.when(chunk > 0)
                def _():
                    for cp in fetch(HALVES - 1, chunk * HALVES - 1):
                        cp.start()
            sb_ref[half, 0] = ck_ref[half]
            picks(half, jnp.int32(HALF - 1))

            def back(ii, carry):
                i = HALF - 1 - ii
                t = base + i
                row = lambda ref: _pair_rows(ref[pl.ds(t, 1), :])
                a_r, b_r, k_r, w_r, r_r = row(a_ref), row(b_ref), row(k_ref), row(w_ref), row(r_ref)
                vs = [pick_ref[pl.ds(p * HD, HD), :] for p in range(HP)]
                dys = [pick_ref[pl.ds((HP + p) * HD, HD), :] for p in range(HP)]
                picks(half, jnp.maximum(i - 1, 0))
                dr, dw, db, dk, da = [], [], [], [], []
                for p in range(HP):
                    Sp, dy = sb_ref[half, i, p], dys[p]
                    dS = ds_ref[p] + dy * r_r[p]
                    dr.append(rowsum(sb_ref[half, i + 1, p] * dy))
                    dw.append(rowsum(dS * Sp))
                    db.append(rowsum(dS * sa_ref[half, i, p]))
                    dk.append(rowsum(dS * vs[p]))
                    dsa = jnp.where(left, *_halfsums(dS, b_r[p], left1))
                    _store_columns(dvc_ref, p, lane == t, _halfsums(dS, k_r[p], left1))
                    da.append(rowsum(Sp * dsa))
                    ds_ref[p] = dS * w_r[p] + dsa * a_r[p]
                for ref, pieces in ((dr_ref, dr), (dw_ref, dw), (db_ref, db), (dk_ref, dk), (da_ref, da)):
                    ref[pl.ds(t, 1), :] = jnp.concatenate(pieces, axis=1)
                return carry

            lax.fori_loop(0, HALF, back, 0, unroll=4)
        _columns_to_rows(dvc_ref, dv_ref)

    rowblk = pl.BlockSpec((CHUNK, RW), lambda c: (NCHUNK - 1 - c, 0))
    col3blk = pl.BlockSpec((HP, HD, 6 * CHUNK), lambda c: (0, 0, NCHUNK - 1 - c))
    rowshape = jax.ShapeDtypeStruct((T, RW), F32)
    return pl.pallas_call(
        kern, name="rwkv_scan_bwd", grid=(NCHUNK,),
        in_specs=[rowblk] * 5 + [col3blk, col3blk,
                                 pl.BlockSpec((HALVES, HP, HD, LW), lambda c: (NCHUNK - 1 - c, 0, 0, 0)),
                                 pl.BlockSpec(memory_space=pl.ANY), pl.BlockSpec(memory_space=pl.ANY)],
        out_specs=[rowblk] * 6, out_shape=[rowshape] * 6,
        scratch_shapes=[pltpu.VMEM((HP, HD, LW), F32), pltpu.VMEM((HALVES, HALF + 1, HP, HD, LW), F32),
                        pltpu.VMEM((HALVES, HALF, HP, HD, LW), F32), pltpu.VMEM((2 * HP * HD, LW), F32),
                        pltpu.VMEM((HEADS, HD, CHUNK), F32), pltpu.SemaphoreType.DMA((2, HALVES))],
        compiler_params=_cp(("arbitrary",)),
    )(r, w, k, a, b, v3, dy3, ck, st, sa)


NT = (((1,), (1,)), ((), ()))
TN = (((0,), (0,)), ((), ()))
SCALE = HD ** -0.5
QKV_G = 3 * RW


def _attn_setup(g):
    dil = DILS[g]
    qkv = [pl.BlockSpec((T, LW), lambda hp, c=(g * QKV_G + s * RW) // LW: (0, c + hp)) for s in range(3)]
    tile = pl.BlockSpec((T, LW), lambda hp: (0, hp))
    bias = pl.BlockSpec((2, BLK, 2 * BLK), lambda hp: (hp, 0, 0))

    def blocks():
        for r in range(dil):
            for n in range(T // dil // BLK):
                rows = pl.ds(n * BLK * dil + r, BLK, stride=dil)
                keys = pl.ds((n - 1) * BLK * dil + r, 2 * BLK, stride=dil) if n else rows
                yield n, rows, keys

    return qkv, tile, bias, blocks


def _band(n):
    qi = lax.broadcasted_iota(jnp.int32, (BLK, 2 * BLK), 0)
    ki = lax.broadcasted_iota(jnp.int32, (BLK, 2 * BLK), 1)
    band = (ki >= qi) & (ki <= qi + BLK)
    return band if n else band[:, BLK:]


def _head_masks():
    lane = lax.broadcasted_iota(jnp.int32, (BLK, LW), 1)
    return lane < HD, [(lane < HD).astype(BF16), (lane >= HD).astype(BF16)]


def _attn_fwd(pq, bias, g):
    qkv, tile, bias_spec, blocks = _attn_setup(g)

    def kern(q_ref, k_ref, v_ref, b_ref, o_ref, l_ref):
        left, masks = _head_masks()
        for n, rows, keys in blocks():
            qb, kc, vc = q_ref[rows, :].astype(BF16), k_ref[keys, :].astype(BF16), v_ref[keys, :].astype(BF16)
            valid = _band(n)
            o, lse = [], []
            for j in range(2):
                bias_j = b_ref[j] if n else b_ref[j][:, BLK:]
                s = lax.dot_general(qb * masks[j], kc, NT, preferred_element_type=F32) * SCALE + bias_j
                s = jnp.where(valid, s, -jnp.inf)
                m = jnp.max(s, axis=1, keepdims=True)
                e = jnp.exp(s - m)
                den = jnp.sum(e, axis=1, keepdims=True)
                o.append(jnp.dot((e / den).astype(BF16), vc, preferred_element_type=F32))
                lse.append(m + jnp.log(den))
            o_ref[rows, :] = jnp.where(left, o[0], o[1])
            l_ref[rows, :] = jnp.where(left, lse[0], lse[1])

    shape = jax.ShapeDtypeStruct((T, RW), F32)
    return pl.pallas_call(
        kern, name=f"attn_fwd_{g}", grid=(HP,),
        in_specs=qkv + [bias_spec], out_specs=[tile, tile], out_shape=[shape, shape],
        compiler_params=_cp(("parallel",)),
    )(pq, pq, pq, bias)


def _attn_bwd(pq, bias, do, o, lse, dlse, g):
    qkv, tile, bias_spec, blocks = _attn_setup(g)

    def kern(q_ref, k_ref, v_ref, b_ref, do_ref, o_ref, l_ref, dl_ref, dq_ref, dk_ref, dv_ref, db_ref):
        left, masks = _head_masks()
        lane = lax.broadcasted_iota(jnp.int32, (BLK, LW), 1)
        dk_ref[...] = jnp.zeros_like(dk_ref)
        dv_ref[...] = jnp.zeros_like(dv_ref)
        db_ref[...] = jnp.zeros_like(db_ref)

        def column(tile_, j):
            return jnp.sum(jnp.where(lane == j * HD, tile_, 0.0), axis=1, keepdims=True)

        for n, rows, keys in blocks():
            qb, kc, vc = q_ref[rows, :].astype(BF16), k_ref[keys, :].astype(BF16), v_ref[keys, :].astype(BF16)
            dof, valid = do_ref[rows, :], _band(n)
            dob, prod = dof.astype(BF16), dof * o_ref[rows, :]
            dq = []
            for j in range(2):
                bias_j = b_ref[j] if n else b_ref[j][:, BLK:]
                delta = jnp.sum(prod * masks[j].astype(F32), axis=1, keepdims=True)
                qm, dom = qb * masks[j], dob * masks[j]
                s = lax.dot_general(qm, kc, NT, preferred_element_type=F32) * SCALE + bias_j
                p = jnp.where(valid, jnp.exp(s - column(l_ref[rows, :], j)), 0.0)
                dp = lax.dot_general(dom, vc, NT, preferred_element_type=F32)
                ds = p * (dp + (column(dl_ref[rows, :], j) - delta))
                if n:
                    db_ref[j] += ds
                else:
                    db_ref[j, :, BLK:] += ds
                dsb = (ds * SCALE).astype(BF16)
                dq.append(jnp.dot(dsb, kc, preferred_element_type=F32))
                dk_ref[keys, :] += lax.dot_general(dsb, qm, TN, preferred_element_type=F32)
                dv_ref[keys, :] += lax.dot_general(p.astype(BF16), dom, TN, preferred_element_type=F32)
            dq_ref[rows, :] = jnp.where(left, dq[0], dq[1])

    shape = jax.ShapeDtypeStruct((T, RW), F32)
    return pl.pallas_call(
        kern, name=f"attn_bwd_{g}", grid=(HP,),
        in_specs=qkv + [bias_spec] + [tile] * 4, out_specs=[tile] * 3 + [bias_spec],
        out_shape=[shape] * 3 + [jax.ShapeDtypeStruct((HEADS, BLK, 2 * BLK), F32)],
        compiler_params=_cp(("parallel",)),
    )(pq, pq, pq, bias, do, o, lse, dlse)


NBUCKET = 32
NPAIR = BLK * 2 * BLK


def _relbias_table(rbT, onehotT):
    def kern(rb_ref, oh_ref, out_ref):
        out_ref[0] = sum(jnp.dot(p, oh_ref[0], preferred_element_type=F32) for p in _split3(rb_ref[0]))

    return pl.pallas_call(
        kern, name="relbias_table", grid=(3,),
        in_specs=[pl.BlockSpec((1, HEADS, NBUCKET), lambda g: (g, 0, 0)),
                  pl.BlockSpec((1, NBUCKET, NPAIR), lambda g: (g, 0, 0))],
        out_specs=pl.BlockSpec((1, HEADS, NPAIR), lambda g: (g, 0, 0)),
        out_shape=jax.ShapeDtypeStruct((3, HEADS, NPAIR), F32),
        compiler_params=_cp(("parallel",)),
    )(rbT, onehotT)


def _relbias_grad(db, onehotT):
    nt = (((1,), (1,)), ((), ()))

    def kern(db_ref, oh_ref, out_ref):
        hi, mid, _ = _split3(db_ref[0])
        out_ref[0] = (lax.dot_general(hi, oh_ref[0], nt, preferred_element_type=F32)
                      + lax.dot_general(mid, oh_ref[0], nt, preferred_element_type=F32))

    return pl.pallas_call(
        kern, name="relbias_grad", grid=(3,),
        in_specs=[pl.BlockSpec((1, HEADS, NPAIR), lambda g: (g, 0, 0)),
                  pl.BlockSpec((1, NBUCKET, NPAIR), lambda g: (g, 0, 0))],
        out_specs=pl.BlockSpec((1, HEADS, NBUCKET), lambda g: (g, 0, 0)),
        out_shape=jax.ShapeDtypeStruct((3, HEADS, NBUCKET), F32),
        compiler_params=_cp(("parallel",)),
    )(db, onehotT)


def _adamw(w, g, m, v):
    m2 = ADAM_B1 * m + (1.0 - ADAM_B1) * g
    v2 = ADAM_B2 * v + (1.0 - ADAM_B2) * (g * g)
    m_hat = m2 / (1.0 - ADAM_B1 ** ADAM_STEP)
    v_hat = v2 / (1.0 - ADAM_B2 ** ADAM_STEP)
    return -ADAM_LR * (m_hat / (jnp.sqrt(v_hat) + ADAM_EPS) + ADAM_WD * w), m2, v2


def _ada_mod(c_all, ada_w, ada_b_loc):
    def kern(c_ref, w_ref, b_ref, o_ref):
        c = c_ref[...]
        cond = c * jax.nn.sigmoid(c)
        o_ref[0] = jnp.dot(cond, w_ref[0], precision=HI, preferred_element_type=F32) + b_ref[0]

    ncol = ada_w.shape[2]
    return pl.pallas_call(
        kern, name="ada_mod", grid=(2,),
        in_specs=[pl.BlockSpec((NDEV, D), lambda i: (0, 0)),
                  pl.BlockSpec((1, D, ncol), lambda i: (i, 0, 0)),
                  pl.BlockSpec((1, 1, ncol), lambda i: (i, 0, 0))],
        out_specs=pl.BlockSpec((1, NDEV, ncol), lambda i: (i, 0, 0)),
        out_shape=jax.ShapeDtypeStruct((2, NDEV, ncol), F32),
        compiler_params=_cp(("parallel",)),
    )(c_all, ada_w, ada_b_loc.reshape(2, 1, ncol))


def _ada_grad_adamw(cT_all, dmod_loc, w, m, v):
    ncol = w.shape[2]
    tr = 256

    def kern(c_ref, d_ref, w_ref, m_ref, v_ref, g_ref, dl_ref, m2_ref, v2_ref):
        c = c_ref[...]
        cond = c * jax.nn.sigmoid(c)
        g = jnp.dot(cond, d_ref[0], precision=HI, preferred_element_type=F32)
        dl, m2, v2 = _adamw(w_ref[0], g, m_ref[0], v_ref[0])
        g_ref[0], dl_ref[0], m2_ref[0], v2_ref[0] = g, dl, m2, v2

    big = pl.BlockSpec((1, tr, ncol), lambda i, j: (i, j, 0))
    shp = jax.ShapeDtypeStruct(w.shape, F32)
    return pl.pallas_call(
        kern, name="ada_grad_adamw", grid=(2, D // tr),
        in_specs=[pl.BlockSpec((tr, NDEV), lambda i, j: (j, 0)),
                  pl.BlockSpec((1, NDEV, ncol), lambda i, j: (i, 0, 0)), big, big, big],
        out_specs=[big] * 4, out_shape=[shp] * 4,
        compiler_params=_cp(("parallel", "parallel")),
    )(cT_all, dmod_loc, w, m, v)


def _sum_adamw(recv, w, m, v, name, tr):
    S = recv.shape[0]
    R, C = w.shape
    assert R % tr == 0 and recv.shape[1:] == (R, C)

    def kern(r_ref, w_ref, m_ref, v_ref, g_ref, dl_ref, m2_ref, v2_ref):
        g = r_ref[0].astype(F32)
        for s in range(1, S):
            g = g + r_ref[s].astype(F32)
        dl, m2, v2 = _adamw(w_ref[...], g, m_ref[...], v_ref[...])
        g_ref[...], dl_ref[...], m2_ref[...], v2_ref[...] = g, dl, m2, v2

    flat = pl.BlockSpec((tr, C), lambda i: (i, 0))
    shp = jax.ShapeDtypeStruct((R, C), F32)
    return pl.pallas_call(
        kern, name=name, grid=(R // tr,),
        in_specs=[pl.BlockSpec((S, tr, C), lambda i: (0, i, 0)), flat, flat, flat],
        out_specs=[flat] * 4, out_shape=[shp] * 4,
        compiler_params=_cp(("parallel",)),
    )(recv, w, m, v)


def _pack(arrs, dtype, row_mult):
    flat = jnp.concatenate([a.reshape(-1).astype(dtype) for a in arrs])
    flat = jnp.pad(flat, (0, -flat.shape[0] % (128 * row_mult)))
    return flat.reshape(-1, 128)


def _pack8(arrs, dtype, row_mult):
    flat = jnp.concatenate([a.reshape(NDEV, -1).astype(dtype) for a in arrs], axis=1)
    flat = jnp.pad(flat, ((0, 0), (0, -flat.shape[1] % (128 * row_mult))))
    return flat.reshape(NDEV, -1, 128)


def _unpack(buf, shapes, lead=()):
    flat = buf.reshape(lead + (-1,))
    out, off = [], 0
    for s in shapes:
        n = math.prod(s)
        out.append(flat[..., off:off + n].reshape(lead + tuple(s)))
        off += n
    return out


def _to_chunks(full, kind):
    if kind == "col":
        x = full.reshape(full.shape[:-1] + (NDEV, full.shape[-1] // NDEV))
        return jnp.moveaxis(x, -2, 0)
    x = full.reshape(full.shape[:-2] + (NDEV, full.shape[-2] // NDEV, full.shape[-1]))
    return jnp.moveaxis(x, -3, 0)


def _from_chunks(g8, kind):
    if kind == "col":
        x = jnp.moveaxis(g8, 0, -2)
        return x.reshape(x.shape[:-2] + (x.shape[-2] * x.shape[-1],))
    x = jnp.moveaxis(g8, 0, -3)
    return x.reshape(x.shape[:-3] + (x.shape[-3] * x.shape[-2], x.shape[-1]))


def _pad_pa(x):
    z = lambda n: jnp.zeros(x.shape[:-1] + (n,), x.dtype)
    return jnp.concatenate([x[..., :1600], z(64), x[..., 1600:1664], z(64), x[..., 1664:1824], z(96)], -1)


def _unpad_pa(x):
    return jnp.concatenate([x[..., :1600], x[..., 1664:1728], x[..., 1792:1952]], -1)


AB_SEGMENTS = ((0, 1600, 0), (1600, 1664, 64), (1664, 1824, 128), (1824, 3360, PAB - 3360))
AB_SHARD = 3360 // NDEV


def _ab_in_padded(g8):
    blocks, at = [], 0
    for start, end, shift in AB_SEGMENTS:
        if start + shift > at:
            blocks.append(jnp.zeros((g8.shape[1], start + shift - at), g8.dtype))
        for j in range(start // AB_SHARD, (end - 1) // AB_SHARD + 1):
            lo, hi = max(start, j * AB_SHARD), min(end, (j + 1) * AB_SHARD)
            blocks.append(g8[j, :, lo - j * AB_SHARD:hi - j * AB_SHARD])
        at = end + shift
    return jnp.concatenate(blocks, axis=1)


def _ab_in_shards(padded):
    shards = []
    for j in range(NDEV):
        pieces = [padded[:, max(start, j * AB_SHARD) + shift:min(end, (j + 1) * AB_SHARD) + shift]
                  for start, end, shift in AB_SEGMENTS if max(start, j * AB_SHARD) < min(end, (j + 1) * AB_SHARD)]
        shards.append(jnp.concatenate(pieces, axis=1))
    return jnp.stack(shards)


def _pad_rows(x, n):
    return jnp.pad(x, ((0, n - x.shape[0]), (0, 0)))


def _bucket_tables():
    qi = jnp.arange(BLK)[:, None]
    ki = jnp.arange(2 * BLK)[None, :]
    rel = BLK + qi - ki
    tabs = []
    for dil in DILS:
        dist = jnp.clip(rel, 0, BLK) * dil
        logd = jnp.log(jnp.maximum(dist, 1).astype(F32) / 16) / math.log(2048 / 16)
        large = jnp.minimum(16 + (logd * 16).astype(jnp.int32), 31)
        tabs.append(jnp.where(dist < 16, dist, large))
    return jnp.stack(tabs)


SHARDED = (("ln_g", "col"), ("ln_b", "col"), ("ab_w_in", "col"), ("rw_w_up", "col"), ("rw_a_up", "col"),
           ("rw_g_up", "col"), ("sc_conv_w", "col"), ("ab_w_out", "row"), ("dil_w_qkv", "col"),
           ("dil_w_out", "col"), ("mlp_w1", "col"), ("mlp_w2", "row"))
FIRST = ("ab_w_in",)
LATER = ("ab_w_out", "dil_w_qkv", "dil_w_out", "mlp_w1", "mlp_w2")
GATHER_BF16 = FIRST + LATER
GATHER_F32 = ("rw_w_up", "rw_a_up", "rw_g_up", "sc_conv_w", "ln_g", "ln_b")
REPLICATED = ("ada_b", "rw_mu", "rw_w0", "rw_a0", "rw_k_k", "rw_k_a", "rw_r_k", "rw_lnx_g", "rw_lnx_b", "rel_bias")
WEIGHTS = ("ada_w", "ada_b", "ln_g", "ln_b", "ab_w_in", "rw_mu", "rw_w0", "rw_w_up", "rw_a0", "rw_a_up",
           "rw_g_up", "rw_k_k", "rw_k_a", "rw_r_k", "rw_lnx_g", "rw_lnx_b", "sc_conv_w", "ab_w_out",
           "dil_w_qkv", "dil_w_out", "rel_bias", "mlp_w1", "mlp_w2")


def _local_step(x0, tgt, mod, W, P, later_weights, early_grads):
    row = lambda a: a.reshape(1, -1)
    W = dict(W)
    m6 = mod.reshape(2, 6, 1, D)
    sc = [m6[0, 1], m6[0, 4], m6[1, 1], m6[1, 4]]
    sh = [m6[0, 0], m6[0, 3], m6[1, 0], m6[1, 3]]
    gt = [m6[0, 2], m6[0, 5], m6[1, 2], m6[1, 5]]
    lng = [row(P["ln_g"][0, 0]), row(P["ln_g"][0, 1]), row(P["ln_g"][1, 0]), row(P["ln_g"][1, 1])]
    lnb = [row(P["ln_b"][0, 0]), row(P["ln_b"][0, 1]), row(P["ln_b"][1, 0]), row(P["ln_b"][1, 1])]
    E = jnp.kron(jnp.eye(HEADS, dtype=BF16), jnp.ones((HD, HD), BF16))

    def mod_body(r, p):
        u = r[0] * (1.0 + p[0]) + p[1]
        return [u, u], []

    (u0, u0T), _ = _rows("modulate", mod_body, [x0], [sc[0], sh[0]], [(D, BF16), (D, BF16, "T")])

    def post_fwd_body(r, p):
        xn, un = _post_ln_mod(r[0], r[1], *p)
        return [xn, un, un], []

    def post_fwd(s, x, y):
        (xn, un, unT), _ = _rows(f"post_ln_{s}", post_fwd_body, [x, y],
                                 [gt[s], lng[s], lnb[s], sc[s + 1], sh[s + 1]],
                                 [(D, F32), (D, BF16), (D, BF16, "T")])
        return xn, un, unT

    def relu2(acc):
        a = jnp.maximum(acc, 0.0)
        return acc, a * a, a * a

    def relu2_bwd(acc, h):
        return (acc * (2.0 * jnp.maximum(h, 0.0)),)

    p = _mm("ab_in", u0, W["ab_w_in"])
    mu = _pad_pa(P["rw_mu"])
    mu_parts = [mu[:, :512], mu[:, 512:1024], mu[:, 1024:1536], mu[:, 1536:1664], mu[:, 1664:1792], mu[:, 1792:]]
    pre_params = mu_parts + [P["rw_w0"], _pad_rows(P["rw_w_up"], 128), P["rw_a0"], _pad_rows(P["rw_a_up"], 128),
                             _pad_rows(P["rw_g_up"], 256), P["rw_k_k"], P["rw_k_a"],
                             P["sc_conv_w"][0:1], P["sc_conv_w"][1:2], P["sc_conv_w"][2:3]]
    pieces = [(p, 512, 0), (p, 512, 1), (p, 512, 2), (p, 128, 12), (p, 128, 13), (p, 256, 7),
              (p, 512, 4), (p, 512, 5), (p, 512, 6)]
    shifted = [0, 1, 2, 3, 4, 5, 6, 8]
    pre_rows = pieces + [pieces[i] + ("prev",) for i in shifted]
    NPR = 19

    def pre_args(r):
        x, prev = r[:9], dict(zip(shifted, r[9:17]))
        down = lambda i, k: _shift_down(x[i], prev[i], k)
        return x[:6] + [down(i, 1) for i in range(6)] + x[6:9] + [down(6, 1), down(8, 1), down(6, 2), down(8, 2)]

    def pre_fwd_body(r, pp):
        return list(_pre_core(pp[0], *pre_args(r), *pp[1:])), []

    (r_, w_, kh_, v_, a_, b_, gate_, yb), _ = _rows(
        "rwkv_pre", pre_fwd_body, pre_rows, [E] + pre_params, [(RW, F32)] * 7 + [(RW, BF16)], tm=256)
    scan_in = [r_, w_, kh_, a_, b_, _cols3(v_, "rwkv_v_columns")]
    ysc, *saved = _scan_fwd(*scan_in)
    post_params = [P["rw_lnx_g"], P["rw_lnx_b"], P["rw_r_k"].reshape(1, RW)]

    def postmix_fwd_body(r, pp):
        return [_post_core(pp[0], *r, *pp[1:])], []

    (ya,), _ = _rows("rwkv_post", postmix_fwd_body, [ysc, r_, kh_, v_, gate_], [E] + post_params,
                     [(RW, BF16)], tm=256)
    cat = jnp.concatenate([ya, yb], axis=1)
    W.update(later_weights(cat))
    y0 = _mm("ab_out", cat, W["ab_w_out"])
    x1, u1, u1T = post_fwd(0, x0, y0)

    h1, a1, a1T = _mm("mlp1_up_0", u1, W["mlp_w1"][0], out=(F32, BF16, (BF16, "T")), epi=relu2)
    y1 = _mm("mlp1_down_0", a1, W["mlp_w2"][0])
    x2, u2, u2T = post_fwd(1, x1, y1)

    pq = _mm("qkv", u2, W["dil_w_qkv"])
    onehotT = (_bucket_tables().reshape(3, 1, NPAIR) == jnp.arange(NBUCKET).reshape(1, NBUCKET, 1)).astype(BF16)
    rbT = P["rel_bias"].reshape(NBUCKET, 3, HEADS).transpose(1, 2, 0)
    bias = _relbias_table(rbT, onehotT).reshape(3, HEADS, BLK, 2 * BLK)
    og, lse = zip(*[_attn_fwd(pq, bias[g], g) for g in range(3)])

    def merge_fwd_body(r, pp):
        return [_merge_core(*r)], []

    (om,), _ = _rows("attn_merge", merge_fwd_body, list(og + lse), [], [(RW, BF16)])
    y2 = _mm("dil_out", om, W["dil_w_out"])
    x3, u3, u3T = post_fwd(2, x2, y2)

    h3, a3, a3T = _mm("mlp1_up_1", u3, W["mlp_w1"][1], out=(F32, BF16, (BF16, "T")), epi=relu2)
    y3 = _mm("mlp1_down_1", a3, W["mlp_w2"][1])

    def last_body(r, pp):
        x, y, tg = r
        xn, vjp = jax.vjp(_post_ln, x, y, *pp)
        err = xn - tg
        dx, dy, dg, dlg, dlb = vjp(err * (1.0 / D))
        loss = jnp.full((1, 128), (0.5 / D) * jnp.sum(err * err), F32)
        return [dx, dy], [loss, dg, dlg, dlb]

    (dxp, dy3), (loss_acc, dg3, dlng3, dlnb3) = _rows(
        "final_ln_loss", last_body, [x3, y3, tgt], [gt[3], lng[3], lnb[3]],
        [(D, F32), (D, BF16)], [(1, 128), (1, D), (1, D), (1, D)])

    G = {}
    dsc, dsh, dgt = [None] * 4, [None] * 4, [None] * 4
    dlng, dlnb = [None] * 4, [None] * 4
    dgt[3], dlng[3], dlnb[3] = dg3, dlng3, dlnb3

    def mlp_bwd(i, uT, h, aT, dy):
        dh = _mm(f"mlp_dh_{i}", dy, W["mlp_w2"][i], tb=True, out=(BF16,), epi=relu2_bwd, extras=(h,))
        gw2 = _mm(f"mlp_dw2_{i}", aT, dy)
        du = _mm(f"mlp_du_{i}", dh, W["mlp_w1"][i], tb=True)
        gw1 = _mm(f"mlp_dw1_{i}", uT, dh)
        return du, gw1, gw2

    def post_bwd_body(r, pp):
        x, y, dxn, dun = r
        _, vjp = jax.vjp(_post_ln_mod, x, y, *pp)
        dx, dy, dg, dlg, dlb, dscn, dshn = vjp((dxn, dun))
        return [dx, dy], [dg, dlg, dlb, dscn, dshn]

    def post_bwd(s, x, y, dxn, dun):
        (dx, dy), (dgt[s], dlng[s], dlnb[s], dsc[s + 1], dsh[s + 1]) = _rows(
            f"post_ln_bwd_{s}", post_bwd_body, [x, y, dxn, dun],
            [gt[s], lng[s], lnb[s], sc[s + 1], sh[s + 1]], [(D, F32), (D, BF16)], [(1, D)] * 5)
        return dx, dy

    du3, gw1_1, gw2_1 = mlp_bwd(1, u3T, h3, a3T, dy3)
    dxp, dy2 = post_bwd(2, x2, y2, dxp, du3)

    G["dil_w_out"] = _mm("dil_out_dw", om.T, dy2)[None]
    do = _mm("dil_out_dx", dy2, W["dil_w_out"], tb=True)

    def merge_bwd_body(r, pp):
        _, vjp = jax.vjp(_merge_core, *r[:6])
        d = vjp(r[6])
        return list(d[:3]) + [_headsum(d[3 + g], pp[0]) for g in range(3)], []

    mb, _ = _rows("attn_merge_bwd", merge_bwd_body, list(og + lse) + [do], [E],
                  [(RW, F32)] * 6)
    back = [_attn_bwd(pq, bias[g], mb[g], og[g], lse[g], mb[3 + g], g) for g in range(3)]
    dpq = jnp.concatenate([t for dq, dk, dv, _ in back for t in (dq, dk, dv)], axis=1).astype(BF16)
    rb = _relbias_grad(jnp.stack([b[3] for b in back]).reshape(3, HEADS, NPAIR), onehotT)
    G["rel_bias"] = rb.transpose(2, 0, 1).reshape(NBUCKET, 3 * HEADS)
    G["dil_w_qkv"] = _mm("qkv_dw", u2T, dpq)[None]
    du2 = _mm("qkv_dx", dpq, W["dil_w_qkv"], tb=True)
    dxp, dy1 = post_bwd(1, x1, y1, dxp, du2)

    du1, gw1_0, gw2_0 = mlp_bwd(0, u1T, h1, a1T, dy1)
    G["mlp_w1"] = jnp.stack([gw1_0, gw1_1])
    G["mlp_w2"] = jnp.stack([gw2_0, gw2_1])
    dxp, dy0 = post_bwd(0, x0, y0, dxp, du1)

    G["ab_w_out"] = _mm("ab_out_dw", cat.T, dy0)[None]
    dcat = _mm("ab_out_dx", dy0, W["ab_w_out"], tb=True)
    post_params = [post_params[0] + early_grads(G)] + post_params[1:]

    def postmix_bwd_body(r, pp):
        _, vjp = jax.vjp(functools.partial(_post_core, pp[0]), *r[:5], *pp[1:])
        d = vjp(r[5])
        return list(d[:5]), list(d[5:])

    (dysc, dr1, dkh1, dv1, dgate), (G["rw_lnx_g"], G["rw_lnx_b"], drk) = _rows(
        "rwkv_post_bwd", postmix_bwd_body, [ysc, r_, kh_, v_, gate_, (dcat, 512, 0)], [E] + post_params,
        [(RW, F32)] * 5, [(1, RW)] * 3, tm=256)
    G["rw_r_k"] = drk.reshape(1, HEADS, HD)
    dr2, dw2, dk2, da2, db2, dv2 = _scan_bwd(*scan_in, _cols3(dysc, "rwkv_dy_columns"), *saved)

    def pre_bwd_body(r, pp):
        prim, ct = pre_args(r[:len(pre_rows)]), r[len(pre_rows):]
        _, vjp = jax.vjp(functools.partial(_pre_core, pp[0]), *prim, *pp[1:])
        cts = (ct[0] + ct[1], ct[2], ct[3] + ct[4], ct[5] + ct[6], ct[7], ct[8], ct[9], ct[10])
        d = vjp(cts)
        z = jnp.zeros_like(d[12])
        dp = jnp.concatenate([d[0], d[1], d[2], d[3], d[4], d[5], d[12], d[13], d[14]], axis=1)
        dp1 = jnp.concatenate([d[6], d[7], d[8], d[9], d[10], d[11], d[15], z, d[16]], axis=1)
        dp2 = jnp.concatenate([d[17], z, d[18]], axis=1)
        return [dp, dp1, dp2], list(d[NPR:])

    acc_shapes = [a.shape for a in pre_params]
    (dp, dp1, dp2), pacc = _rows(
        "rwkv_pre_bwd", pre_bwd_body,
        pre_rows + [dr1, dr2, dw2, dkh1, dk2, dv1, dv2, da2, db2, dgate, (dcat, 512, 1)],
        [E] + pre_params, [(PAB, F32), (PAB, F32), (PB, F32)], acc_shapes, tm=256)
    G["rw_mu"] = _unpad_pa(jnp.concatenate(pacc[:6], axis=1))
    G["rw_w0"], G["rw_a0"], G["rw_k_k"], G["rw_k_a"] = pacc[6], pacc[8], pacc[11], pacc[12]
    G["rw_w_up"] = pacc[7][None, :64]
    G["rw_a_up"] = pacc[9][None, :64]
    G["rw_g_up"] = pacc[10][None, :160]
    G["sc_conv_w"] = jnp.concatenate(pacc[13:16], axis=0)[None]

    def shift_merge_body(r, pp):
        d0, d1, d1_next, d2, d2_next = r
        d = d0 + _shift_up(d1, d1_next, 1)
        return [jnp.concatenate([d[:, :PA], d[:, PA:] + _shift_up(d2, d2_next, 2)], axis=1)], []

    (dpt,), _ = _rows("shift_merge", shift_merge_body,
                      [dp, dp1, (dp1, PAB, 0, "next"), dp2, (dp2, PB, 0, "next")], [], [(PAB, BF16)])
    du0 = _mm("ab_in_dx", dpt, W["ab_w_in"], tb=True)

    def mod_bwd_body(r, pp):
        du, dx, x = r
        return [dx + du * (1.0 + pp[0])], [jnp.sum(du * x, axis=0, keepdims=True), jnp.sum(du, axis=0, keepdims=True)]

    (grad_x,), (dsc[0], dsh[0]) = _rows("modulate_bwd", mod_bwd_body, [du0, dxp, x0], [sc[0]], [(D, F32)],
                                        [(1, D), (1, D)])

    G["ln_g"] = jnp.concatenate(dlng, axis=0).reshape(2, 2, D)
    G["ln_b"] = jnp.concatenate(dlnb, axis=0).reshape(2, 2, D)
    dmod = jnp.concatenate([dsh[0], dsc[0], dgt[0], dsh[1], dsc[1], dgt[1],
                            dsh[2], dsc[2], dgt[2], dsh[3], dsc[3], dgt[3]], axis=1).reshape(2, 6 * D)
    return loss_acc[0, 0], grad_x, dmod, G, lambda: _ab_in_shards(_mm("ab_in_dw", u0T, dpt))[:, None]


def kernel(x, c, ada_w, ada_b, ln_g, ln_b, ab_w_in, rw_mu, rw_w0, rw_w_up, rw_a0, rw_a_up, rw_g_up, rw_k_k, rw_k_a, rw_r_k, rw_lnx_g, rw_lnx_b, sc_conv_w, ab_w_out, dil_w_qkv, dil_w_out, rel_bias, mlp_w1, mlp_w2, loss_target, m_ada_w, m_ada_b, m_ln_g, m_ln_b, m_ab_w_in, m_rw_mu, m_rw_w0, m_rw_w_up, m_rw_a0, m_rw_a_up, m_rw_g_up, m_rw_k_k, m_rw_k_a, m_rw_r_k, m_rw_lnx_g, m_rw_lnx_b, m_sc_conv_w, m_ab_w_out, m_dil_w_qkv, m_dil_w_out, m_rel_bias, m_mlp_w1, m_mlp_w2, v_ada_w, v_ada_b, v_ln_g, v_ln_b, v_ab_w_in, v_rw_mu, v_rw_w0, v_rw_w_up, v_rw_a0, v_rw_a_up, v_rw_g_up, v_rw_k_k, v_rw_k_a, v_rw_r_k, v_rw_lnx_g, v_rw_lnx_b, v_sc_conv_w, v_ab_w_out, v_dil_w_qkv, v_dil_w_out, v_rel_bias, v_mlp_w1, v_mlp_w2):
    w = dict(ada_w=ada_w, ada_b=ada_b, ln_g=ln_g, ln_b=ln_b, ab_w_in=ab_w_in, rw_mu=rw_mu, rw_w0=rw_w0,
             rw_w_up=rw_w_up, rw_a0=rw_a0, rw_a_up=rw_a_up, rw_g_up=rw_g_up, rw_k_k=rw_k_k, rw_k_a=rw_k_a,
             rw_r_k=rw_r_k, rw_lnx_g=rw_lnx_g, rw_lnx_b=rw_lnx_b, sc_conv_w=sc_conv_w, ab_w_out=ab_w_out,
             dil_w_qkv=dil_w_qkv, dil_w_out=dil_w_out, rel_bias=rel_bias, mlp_w1=mlp_w1, mlp_w2=mlp_w2)
    m = dict(ada_w=m_ada_w, ada_b=m_ada_b, ln_g=m_ln_g, ln_b=m_ln_b, ab_w_in=m_ab_w_in, rw_mu=m_rw_mu,
             rw_w0=m_rw_w0, rw_w_up=m_rw_w_up, rw_a0=m_rw_a0, rw_a_up=m_rw_a_up, rw_g_up=m_rw_g_up,
             rw_k_k=m_rw_k_k, rw_k_a=m_rw_k_a, rw_r_k=m_rw_r_k, rw_lnx_g=m_rw_lnx_g, rw_lnx_b=m_rw_lnx_b,
             sc_conv_w=m_sc_conv_w, ab_w_out=m_ab_w_out, dil_w_qkv=m_dil_w_qkv, dil_w_out=m_dil_w_out,
             rel_bias=m_rel_bias, mlp_w1=m_mlp_w1, mlp_w2=m_mlp_w2)
    v = dict(ada_w=v_ada_w, ada_b=v_ada_b, ln_g=v_ln_g, ln_b=v_ln_b, ab_w_in=v_ab_w_in, rw_mu=v_rw_mu,
             rw_w0=v_rw_w0, rw_w_up=v_rw_w_up, rw_a0=v_rw_a0, rw_a_up=v_rw_a_up, rw_g_up=v_rw_g_up,
             rw_k_k=v_rw_k_k, rw_k_a=v_rw_k_a, rw_r_k=v_rw_r_k, rw_lnx_g=v_rw_lnx_g, rw_lnx_b=v_rw_lnx_b,
             sc_conv_w=v_sc_conv_w, ab_w_out=v_ab_w_out, dil_w_qkv=v_dil_w_qkv, dil_w_out=v_dil_w_out,
             rel_bias=v_rel_bias, mlp_w1=v_mlp_w1, mlp_w2=v_mlp_w2)
    kinds = dict(SHARDED)
    me = 4 * lax.axis_index("x") + 2 * lax.axis_index("y") + lax.axis_index("c")
    ncol = ada_w.shape[2]

    small = _all_gather(_pack([c] + [w[n] for n in GATHER_F32], F32, 8), "gather_small")
    parts = _unpack(small, [c.shape] + [w[n].shape for n in GATHER_F32], (NDEV,))
    c_all = parts[0].reshape(NDEV, D)
    P = {n: _from_chunks(t, kinds[n]) for n, t in zip(GATHER_F32, parts[1:])}
    P = {n: (t if n in ("ln_g", "ln_b") else t[0]) for n, t in P.items()}
    for n in REPLICATED[1:]:
        P[n] = w[n]
    def full(n, t):
        t = _from_chunks(t, kinds[n])
        return t if n in ("mlp_w1", "mlp_w2") else t[0]

    (first,) = _all_gather_many([ab_w_in.astype(BF16)], "gather_first_weight")
    W = {"ab_w_in": _ab_in_padded(first[:, 0])}

    ada_b_loc = lax.dynamic_slice(ada_b, (0, ncol * me), (2, ncol))
    mod_part = _ada_mod(c_all, ada_w, ada_b_loc)
    mod_all = _all_gather(mod_part.reshape(-1, 128), "gather_mod").reshape(NDEV, 2, NDEV, ncol)
    mod = lax.dynamic_index_in_dim(mod_all, me, axis=2, keepdims=False)
    mod = mod.transpose(1, 0, 2).reshape(2, 6 * D)

    behind = (mod[0, 0] * 0.0).astype(BF16)
    later = _exchange_start([w[n].astype(BF16) + (behind if n == LATER[0] else 0) for n in LATER], True,
                            "gather_later_weights_start")
    mod = mod + later[-1][0, 0]

    def later_weights(after):
        lands = _exchange_wait(later, True, after, "gather_later_weights_wait")
        return {n: full(n, t) for n, t in zip(LATER, lands)}

    sent = []

    def early_grads(G):
        sent.append(_exchange_start([_to_chunks(G[n], kinds[n]).astype(BF16) for n in LATER], False,
                                    "exchange_later_grads_start"))
        return sent[0][-1][0, 0]

    loss_part, grad_x, dmod, G, in_grad = _local_step(x[0], loss_target[0], mod, W, P, later_weights, early_grads)
    G["ada_b"] = dmod
    big_out = {}

    def update(n, contributions):
        cols = w[n].shape[-1]
        flat = lambda t: t.reshape(-1, cols)
        rows = flat(w[n]).shape[0]
        outs = _sum_adamw(contributions.reshape(-1, rows, cols), flat(w[n]), flat(m[n]), flat(v[n]),
                          f"sum_adamw_{n}", min(rows, 256))
        big_out[n] = [o.reshape(w[n].shape) for o in outs]

    rep_shapes = [w[n].shape for n in REPLICATED] + [(1,)]
    rep_all = _all_gather(_pack([G[n] for n in REPLICATED] + [loss_part], F32, 8), "gather_replicated_grads")
    names = [n for n, _ in SHARDED if n not in GATHER_BF16]
    shard_shapes = [w[n].shape for n in names]
    recv = _all_to_all(_pack8([_to_chunks(G[n], kinds[n]) for n in names], F32, 8), "exchange_small_grads")

    behind = (recv[0, 0, 0] * 0.0 + rep_all[0, 0, 0] * 0.0).astype(BF16)
    last = _exchange_start([in_grad().astype(BF16) + behind], False, "exchange_last_grad_start")

    zero = last[-1][0:1, 0]
    pk = lambda d: _pack([d[n] for n in REPLICATED] + [zero], F32, 8)
    rep_out = _sum_adamw(rep_all, pk(w), pk(m), pk(v), "sum_adamw_replicated", rep_all.shape[1])
    loss = _unpack(rep_out[0], rep_shapes)[-1][0]
    rep_out = [dict(zip(REPLICATED, _unpack(o, rep_shapes))) for o in rep_out]
    dmod_all = _unpack(rep_all, [(2, 6 * D)], (NDEV,))[0]
    dmod_loc = lax.dynamic_slice(dmod_all, (0, 0, ncol * me), (NDEV, 2, ncol)).transpose(1, 0, 2)
    ada_out = _ada_grad_adamw(c_all.T + zero, dmod_loc, ada_w, m_ada_w, v_ada_w)
    pk = lambda d: _pack([d[n] for n in names], F32, 8)
    sh_out = _sum_adamw(recv, pk(w), pk(m), pk(v), "sum_adamw_small", recv.shape[1])
    sh_out = [dict(zip(names, _unpack(o, shard_shapes))) for o in sh_out]
    for n, r in zip(LATER, _exchange_wait(sent[0], False, last[-1], "exchange_later_grads_wait")):
        update(n, r)
    (landed,) = _exchange_wait(last, False, big_out[LATER[-1]][0], "exchange_last_grad_wait")
    update("ab_w_in", landed)
    sh_out = [{**d, **{n: big_out[n][i] for n in GATHER_BF16}} for i, d in enumerate(sh_out)]

    def pick(i, n):
        if n == "ada_w":
            return ada_out[i]
        return rep_out[i][n] if n in REPLICATED else sh_out[i][n]

    outs = [loss, grad_x[None]]
    for i in range(4):
        outs += [pick(i, n) for n in WEIGHTS]
    return tuple(outs)
```

```python
import functools
import math

import jax
import jax.numpy as jnp
from jax import lax
from jax.experimental import pallas as pl
from jax.experimental.pallas import tpu as pltpu

F32 = jnp.float32
BF16 = jnp.bfloat16
HI = lax.Precision.HIGHEST

NDEV = 8
T = 2048
D = 1024
DFF = 4096
HEADS = 8
HD = 64
RW = 512
PA = 2048
PB = 1536
PAB = PA + PB
QKV = 4608
DILS = (1, 4, 16)
BLK = 128
ALPHA = 4.0 ** 0.25
LN_EPS = 1e-5
GN_EPS = 64e-5
ADAM_LR, ADAM_B1, ADAM_B2, ADAM_EPS, ADAM_WD, ADAM_STEP = 0.001, 0.9, 0.999, 1e-8, 0.01, 10
VMEM_LIMIT = 56 * 1024 * 1024


def _cp(sem):
    return pltpu.CompilerParams(dimension_semantics=sem, vmem_limit_bytes=VMEM_LIMIT)


def _slot(px, py, pc):
    return 4 * px + 2 * py + pc


def _all_gather(x, name):
    R, C = x.shape

    def body(x_ref, out_ref, send_sems, recv_sems, local_sem):
        xi, yi, ci = lax.axis_index("x"), lax.axis_index("y"), lax.axis_index("c")
        me, sibling = (xi, yi, ci), (xi, yi, 1 - ci)
        chips = [(1 - xi, yi), (xi, 1 - yi), (1 - xi, 1 - yi)]

        def rows(px, py, pc):
            return out_ref.at[_slot(px, py, pc)]

        def copy(k, block, to, src=None):
            return pltpu.make_async_remote_copy(
                src_ref=rows(*block) if src is None else src, dst_ref=rows(*block),
                send_sem=send_sems.at[k], recv_sem=recv_sems.at[k],
                device_id=to, device_id_type=pl.DeviceIdType.MESH)

        mine = pltpu.make_async_copy(x_ref, rows(*me), local_sem)
        mine.start()
        first = [copy(0, me, sibling, src=x_ref)]
        first += [copy(1 + j, me, (*chip, ci), src=x_ref) for j, chip in enumerate(chips)]
        for cp in first:
            cp.start()
        passed = [copy(4 + j, (*chip, ci), sibling) for j, chip in enumerate(chips)]
        for j, chip in enumerate(chips):
            copy(1 + j, (*chip, ci), me).wait_recv()
            passed[j].start()
        copy(0, sibling, me).wait_recv()
        for j, chip in enumerate(chips):
            copy(4 + j, (*chip, 1 - ci), me).wait_recv()
        for cp in first + passed:
            cp.wait_send()
        mine.wait()

    return pl.pallas_call(
        body, name=name,
        out_shape=jax.ShapeDtypeStruct((NDEV, R, C), x.dtype),
        in_specs=[pl.BlockSpec(memory_space=pl.ANY)],
        out_specs=pl.BlockSpec(memory_space=pl.ANY),
        scratch_shapes=[pltpu.SemaphoreType.DMA((7,)), pltpu.SemaphoreType.DMA((7,)),
                        pltpu.SemaphoreType.DMA(())],
    )(x)


def _all_to_all(g, name):
    _, R, C = g.shape

    def body(g_ref, out_ref, send_sems, recv_sems, local_sem):
        xi, yi, ci = lax.axis_index("x"), lax.axis_index("y"), lax.axis_index("c")
        my_slot = _slot(xi, yi, ci)
        mine = pltpu.make_async_copy(g_ref.at[my_slot], out_ref.at[my_slot], local_sem)
        mine.start()
        copies = []
        for k in range(1, 8):
            px = 1 - xi if k & 4 else xi
            py = 1 - yi if k & 2 else yi
            pc = 1 - ci if k & 1 else ci
            peer_slot = _slot(px, py, pc)
            copies.append((
                pltpu.make_async_remote_copy(
                    src_ref=g_ref.at[peer_slot], dst_ref=out_ref.at[my_slot],
                    send_sem=send_sems.at[k - 1], recv_sem=recv_sems.at[k - 1],
                    device_id=(px, py, pc), device_id_type=pl.DeviceIdType.MESH),
                pltpu.make_async_remote_copy(
                    src_ref=g_ref.at[peer_slot], dst_ref=out_ref.at[peer_slot],
                    send_sem=send_sems.at[k - 1], recv_sem=recv_sems.at[k - 1],
                    device_id=(px, py, pc), device_id_type=pl.DeviceIdType.MESH)))
        for send, _ in copies:
            send.start()
        for _, recv in copies:
            recv.wait_recv()
        for send, _ in copies:
            send.wait_send()
        mine.wait()

    return pl.pallas_call(
        body, name=name,
        out_shape=jax.ShapeDtypeStruct((NDEV, R, C), g.dtype),
        in_specs=[pl.BlockSpec(memory_space=pl.ANY)],
        out_specs=pl.BlockSpec(memory_space=pl.ANY),
        scratch_shapes=[pltpu.SemaphoreType.DMA((7,)), pltpu.SemaphoreType.DMA((7,)),
                        pltpu.SemaphoreType.DMA(())],
    )(g)


def _my_slot():
    return _slot(lax.axis_index("x"), lax.axis_index("y"), lax.axis_index("c"))


def _put_own(buf, own, slot):
    return lax.dynamic_update_index_in_dim(buf, own, slot, 0)


def _hbm_call(body, name, ins, out_shapes, n_sems):
    anyspec = pl.BlockSpec(memory_space=pl.ANY)
    return pl.pallas_call(
        body, name=name, out_shape=out_shapes,
        in_specs=[anyspec] * len(ins), out_specs=[anyspec] * len(out_shapes),
        scratch_shapes=[pltpu.SemaphoreType.DMA(s) for s in n_sems],
    )(*ins)


def _all_gather_many(xs, name):
    n = len(xs)

    def body(*refs):
        x_refs, o_refs = refs[:n], refs[n:2 * n]
        send_sems, recv_sems = refs[2 * n:]
        xi, yi, ci = lax.axis_index("x"), lax.axis_index("y"), lax.axis_index("c")
        me, sibling = (xi, yi, ci), (xi, yi, 1 - ci)
        chips = [(1 - xi, yi), (xi, 1 - yi), (1 - xi, 1 - yi)]

        def copy(i, k, block, to, src=None):
            dst = o_refs[i].at[_slot(*block)]
            return pltpu.make_async_remote_copy(
                src_ref=dst if src is None else src, dst_ref=dst,
                send_sem=send_sems.at[i, k], recv_sem=recv_sems.at[i, k],
                device_id=to, device_id_type=pl.DeviceIdType.MESH)

        sends = []
        for i in range(n):
            sends += [copy(i, 1 + j, me, (*chip, ci), src=x_refs[i]) for j, chip in enumerate(chips)]
            sends.append(copy(i, 0, me, sibling, src=x_refs[i]))
        for cp in sends:
            cp.start()
        for j, chip in enumerate(chips):
            for i in range(n):
                copy(i, 1 + j, (*chip, ci), me).wait_recv()
                passed = copy(i, 4 + j, (*chip, ci), sibling)
                passed.start()
                sends.append(passed)
        for i in range(n):
            copy(i, 0, sibling, me).wait_recv()
            for j, chip in enumerate(chips):
                copy(i, 4 + j, (*chip, 1 - ci), me).wait_recv()
        for cp in sends:
            cp.wait_send()

    outs = _hbm_call(body, name, xs, [jax.ShapeDtypeStruct((NDEV,) + x.shape, x.dtype) for x in xs],
                     [(n, 7), (n, 7)])
    return [_put_own(o, x[None], _my_slot()) for o, x in zip(outs, xs)]


def _peers(xi, yi, ci):
    return [(1 - xi if k & 4 else xi, 1 - yi if k & 2 else yi, 1 - ci if k & 1 else ci) for k in range(1, 8)]


def _direct_copy(src_refs, land_refs, send_sems, recv_sems, i, k, peer, my_slot, gather):
    src = src_refs[i] if gather else src_refs[i].at[_slot(*peer)]
    return pltpu.make_async_remote_copy(
        src_ref=src, dst_ref=land_refs[i].at[my_slot], send_sem=send_sems.at[7 * i + k], recv_sem=recv_sems.at[7 * i + k],
        device_id=peer, device_id_type=pl.DeviceIdType.MESH)


def _exchange_start(srcs, gather, name):
    n = len(srcs)
    lands = [lax.empty(((NDEV,) + s.shape) if gather else s.shape, s.dtype) for s in srcs]

    def body(*refs):
        s_refs, l_refs = refs[:n], refs[n:2 * n]
        send_sems, recv_sems = refs[2 * n], refs[2 * n + 1]
        token = refs[2 * n + 2 + 2 * n]
        xi, yi, ci = lax.axis_index("x"), lax.axis_index("y"), lax.axis_index("c")
        my_slot = _slot(xi, yi, ci)
        for k, peer in enumerate(_peers(xi, yi, ci)):
            for i in range(n):
                _direct_copy(s_refs, l_refs, send_sems, recv_sems, i, k, peer, my_slot, gather).start()
        token[...] = jnp.zeros_like(token)

    hbm = pl.BlockSpec(memory_space=pltpu.HBM)
    sem = pl.BlockSpec(memory_space=pltpu.SEMAPHORE)
    both = list(srcs) + lands
    return pl.pallas_call(
        body, name=name,
        out_shape=(pltpu.SemaphoreType.DMA((7 * n,)), pltpu.SemaphoreType.DMA((7 * n,)),
                   *[pltpu.HBM(t.shape, t.dtype) for t in both], jax.ShapeDtypeStruct((8, 128), F32)),
        in_specs=[hbm] * (2 * n),
        out_specs=(sem, sem, *[hbm] * (2 * n), pl.BlockSpec(memory_space=pltpu.VMEM)),
        input_output_aliases={i: 2 + i for i in range(2 * n)},
        compiler_params=pltpu.CompilerParams(has_side_effects=pltpu.SideEffectType.DATAFLOW_SIDE_EFFECTING),
    )(*[pltpu.with_memory_space_constraint(t, pltpu.HBM) for t in both])


def _exchange_wait(started, gather, after, name):
    send_sems, recv_sems, *thru, _ = started
    n = len(thru) // 2

    def body(*refs):
        s_refs, l_refs = refs[:n], refs[n:2 * n]
        send_sems, recv_sems = refs[2 * n], refs[2 * n + 1]
        xi, yi, ci = lax.axis_index("x"), lax.axis_index("y"), lax.axis_index("c")
        my_slot = _slot(xi, yi, ci)
        for k, peer in enumerate(_peers(xi, yi, ci)):
            for i in range(n):
                _direct_copy(s_refs, l_refs, send_sems, recv_sems, i, k, peer, my_slot, gather).wait_send()
                _direct_copy(s_refs, l_refs, send_sems, recv_sems, i, k, peer, _slot(*peer), gather).wait_recv()

    hbm = pl.BlockSpec(memory_space=pltpu.HBM)
    sem = pl.BlockSpec(memory_space=pltpu.SEMAPHORE)
    outs = pl.pallas_call(
        body, name=name,
        out_shape=tuple(pltpu.HBM(t.shape, t.dtype) for t in thru),
        in_specs=[hbm] * (2 * n) + [sem, sem, pl.BlockSpec(memory_space=pl.ANY)],
        out_specs=tuple([hbm] * (2 * n)),
        input_output_aliases={i: i for i in range(2 * n)},
        compiler_params=pltpu.CompilerParams(has_side_effects=pltpu.SideEffectType.DATAFLOW_SIDE_EFFECTING),
    )(*thru, send_sems, recv_sems, after)
    slot = _my_slot()
    own = [s[None] if gather else lax.dynamic_index_in_dim(s, slot, 0, keepdims=True) for s in outs[:n]]
    return [_put_own(land, o, slot) for land, o in zip(outs[n:], own)]


def _mm(name, a, b, tb=False, out=(F32,), epi=None, extras=(), tm=1024, tn=512, tk_cap=2048):
    M, K = a.shape
    N = b.shape[0] if tb else b.shape[1]
    tm, tn = min(tm, M), min(tn, N)
    tk = max(t for t in range(128, min(K, tk_cap) + 1, 128) if K % t == 0)
    assert M % tm == 0 and N % tn == 0 and K % tk == 0, (name, M, N, K)
    nk = K // tk
    ne, no = len(extras), len(out)
    dims = (((1,), (1 if tb else 0,)), ((), ()))
    flipped = [isinstance(o, tuple) for o in out]

    def kern(*refs):
        a_ref, b_ref = refs[:2]
        e_refs = refs[2:2 + ne]
        o_refs = refs[2 + ne:2 + ne + no]

        def finish(acc):
            outs = epi(acc, *[e[...] for e in e_refs]) if epi is not None else (acc,)
            for o_ref, o, flip in zip(o_refs, outs, flipped):
                o_ref[...] = (o.T if flip else o).astype(o_ref.dtype)

        part = lax.dot_general(a_ref[...], b_ref[...], dims, preferred_element_type=F32)
        if nk == 1:
            finish(part)
            return
        acc_ref = refs[-1]
        k = pl.program_id(2)

        @pl.when(k == 0)
        def _():
            acc_ref[...] = part

        @pl.when(k > 0)
        def _():
            acc_ref[...] += part

        @pl.when(k == nk - 1)
        def _():
            finish(acc_ref[...])

    b_spec = (pl.BlockSpec((tn, tk), lambda i, j, k: (j, k)) if tb
              else pl.BlockSpec((tk, tn), lambda i, j, k: (k, j)))
    tile = pl.BlockSpec((tm, tn), lambda i, j, k: (i, j))
    tile_t = pl.BlockSpec((tn, tm), lambda i, j, k: (j, i))
    res = pl.pallas_call(
        kern, name=name, grid=(M // tm, N // tn, nk),
        in_specs=[pl.BlockSpec((tm, tk), lambda i, j, k: (i, k)), b_spec] + [tile] * ne,
        out_specs=[tile_t if flip else tile for flip in flipped],
        out_shape=[jax.ShapeDtypeStruct((N, M), o[0]) if flip else jax.ShapeDtypeStruct((M, N), o)
                   for o, flip in zip(out, flipped)],
        scratch_shapes=[pltpu.VMEM((tm, tn), F32)] if nk > 1 else [],
        compiler_params=_cp(("parallel", "parallel", "arbitrary")),
    )(a, b, *extras)
    return res[0] if no == 1 else res


HALO = 8


def _rows(name, body, rows, params, out_rows, out_accs=(), tm=256):
    views = [r if isinstance(r, tuple) else (r, r.shape[1], 0) for r in rows]
    n = views[0][0].shape[0]
    assert n % tm == 0 and tm % HALO == 0
    nr, npar, nor, noa = len(views), len(params), len(out_rows), len(out_accs)
    flipped = [len(o) == 3 for o in out_rows]

    def row_spec(width, cb, halo=None):
        per, last = tm // HALO, n // HALO - 1
        if halo == "prev":
            return pl.BlockSpec((HALO, width), lambda i: (jnp.maximum(i * per - 1, 0), cb))
        if halo == "next":
            return pl.BlockSpec((HALO, width), lambda i: (jnp.minimum((i + 1) * per, last), cb))
        return pl.BlockSpec((tm, width), lambda i: (i, cb))

    def kern(*refs):
        r_refs = refs[:nr]
        p_refs = refs[nr:nr + npar]
        o_refs = refs[nr + npar:nr + npar + nor]
        a_refs = refs[nr + npar + nor:]
        outs, accs = body([r[...] for r in r_refs], [p[...] for p in p_refs])
        assert len(outs) == nor and len(accs) == noa, (name, len(outs), len(accs))
        for o_ref, o, flip in zip(o_refs, outs, flipped):
            o_ref[...] = (o.T if flip else o).astype(o_ref.dtype)
        if noa:
            @pl.when(pl.program_id(0) == 0)
            def _():
                for a_ref in a_refs:
                    a_ref[...] = jnp.zeros_like(a_ref)

            for a_ref, a in zip(a_refs, accs):
                a_ref[...] += a.astype(F32)

    def whole(shape):
        nd = len(shape)
        return pl.BlockSpec(tuple(shape), lambda i, nd=nd: (0,) * nd)

    in_specs = [row_spec(*v[1:]) for v in views]
    in_specs += [whole(p.shape) for p in params]
    out_specs = [pl.BlockSpec((o[0], tm), lambda i: (0, i)) if flip else pl.BlockSpec((tm, o[0]), lambda i: (i, 0))
                 for o, flip in zip(out_rows, flipped)]
    out_specs += [whole(s) for s in out_accs]
    out_shape = [jax.ShapeDtypeStruct((o[0], n) if flip else (n, o[0]), o[1]) for o, flip in zip(out_rows, flipped)]
    out_shape += [jax.ShapeDtypeStruct(tuple(s), F32) for s in out_accs]
    res = pl.pallas_call(
        kern, name=name, grid=(n // tm,), in_specs=in_specs, out_specs=out_specs,
        out_shape=out_shape, compiler_params=_cp(("arbitrary",)),
    )(*[v[0] for v in views], *params)
    return res[:nor], res[nor:]


def _shift_down(x, prev, k):
    head = jnp.where(pl.program_id(0) == 0, 0.0, pltpu.roll(prev, k, axis=0))
    row = lax.broadcasted_iota(jnp.int32, x.shape, 0)
    return jnp.where(row < k, jnp.tile(head, (x.shape[0] // HALO, 1)), pltpu.roll(x, k, axis=0))


def _shift_up(x, nxt, k):
    n = x.shape[0]
    tail = jnp.where(pl.program_id(0) == pl.num_programs(0) - 1, 0.0, pltpu.roll(nxt, HALO - k, axis=0))
    row = lax.broadcasted_iota(jnp.int32, x.shape, 0)
    return jnp.where(row >= n - k, jnp.tile(tail, (n // HALO, 1)), pltpu.roll(x, n - k, axis=0))


@jax.custom_vjp
def _headsum(x, e):
    return sum(jnp.dot(p, e, preferred_element_type=F32) for p in _split3(x))


_headsum.defvjp(lambda x, e: (_headsum(x, e), e), lambda e, ct: (_headsum(ct, e), None))


def _softplus(z):
    return jnp.maximum(z, 0.0) + jnp.log(1.0 + jnp.exp(jnp.minimum(z, -z)))


def _post_ln(x, y, g, lng, lnb):
    z = ALPHA * x + (1.0 + g) * y
    mu = jnp.mean(z, axis=-1, keepdims=True)
    zc = z - mu
    var = jnp.mean(zc * zc, axis=-1, keepdims=True)
    return zc * lax.rsqrt(var + LN_EPS) * lng + lnb


def _post_ln_mod(x, y, g, lng, lnb, scn, shn):
    xn = _post_ln(x, y, g, lng, lnb)
    return xn, xn * (1.0 + scn) + shn


def _pre_core(E, r_, k_, v_, wd_, ad_, gd_, r1, k1, v1, wd1, ad1, gd1, h, bg, cg, h1, cg1, h2, cg2,
              mu_r, mu_k, mu_v, mu_wd, mu_ad, mu_gd, w0, w_up, a0, a_up, g_up, k_k, k_a,
              cw0, cw1, cw2):
    def mix(x, x1, mu):
        return x + mu * (x1 - x)

    r, k, v = mix(r_, r1, mu_r), mix(k_, k1, mu_k), mix(v_, v1, mu_v)
    wd, ad, gd = mix(wd_, wd1, mu_wd), mix(ad_, ad1, mu_ad), mix(gd_, gd1, mu_gd)
    logw = -_softplus(-(w0 + jnp.dot(jnp.tanh(wd), w_up, preferred_element_type=F32))) - 0.5
    decay = jnp.exp(-jnp.exp(logw))
    iclr = jax.nn.sigmoid(a0 + jnp.dot(ad, a_up, preferred_element_type=F32))
    gate = jnp.dot(jax.nn.sigmoid(gd), g_up, preferred_element_type=F32)
    kk0 = k * k_k
    nrm = jnp.sqrt(_headsum(kk0 * kk0, E))
    kk = kk0 / jnp.maximum(nrm, 1e-12)
    kh = k * (1.0 + (iclr - 1.0) * k_a)
    yb = bg * (cw2 * (cg * h) + cw1 * (cg1 * h1) + cw0 * (cg2 * h2))
    return r, decay, kh, v, -kk, kk * iclr, gate, yb


def _post_core(E, y, r, kh, v, gate, lnx_g, lnx_b, rk):
    def seg(t):
        return _headsum(t, E)

    mean = seg(y) * (1.0 / HD)
    yc = y - mean
    var = seg(yc * yc) * (1.0 / HD)
    gn = yc * lax.rsqrt(var + GN_EPS) * lnx_g + lnx_b
    bonus = seg(r * kh * rk) * v
    return (gn + bonus) * gate


def _merge_core(o0, o1, o2, l0, l1, l2):
    m = jnp.maximum(jnp.maximum(l0, l1), l2)
    e0, e1, e2 = jnp.exp(l0 - m), jnp.exp(l1 - m), jnp.exp(l2 - m)
    den = e0 + e1 + e2
    return (e0 * o0 + e1 * o1 + e2 * o2) / den


CHUNK = 128
HALF = 64
HP = HEADS // 2
LW = 2 * HD
NCHUNK = T // CHUNK


def _split3(x):
    hi = x.astype(BF16)
    r1 = x - hi.astype(F32)
    mid = r1.astype(BF16)
    return hi, mid, (r1 - mid.astype(F32)).astype(BF16)


def _cols3(x, name):
    def kern(x_ref, o_ref):
        xt = x_ref[...].T
        left = lax.broadcasted_iota(jnp.int32, (HD, CHUNK), 1) < HALF
        for p in range(HP):
            a, b = xt[p * LW:p * LW + HD], xt[p * LW + HD:(p + 1) * LW]
            halves = [jnp.where(left, a, pltpu.roll(b, HALF, axis=1)), jnp.where(left, pltpu.roll(a, HALF, axis=1), b)]
            for h, tile in enumerate(halves):
                for j, part in enumerate(_split3(tile)):
                    o_ref[p, :, (3 * h + j) * LW:(3 * h + j + 1) * LW] = part

    return pl.pallas_call(
        kern, name=name, grid=(NCHUNK,),
        in_specs=[pl.BlockSpec((CHUNK, RW), lambda c: (c, 0))],
        out_specs=pl.BlockSpec((HP, HD, 6 * CHUNK), lambda c: (0, 0, c)),
        out_shape=jax.ShapeDtypeStruct((HP, HD, 6 * T), BF16),
        compiler_params=_cp(("parallel",)),
    )(x)


def _pick_codes():
    row = lax.broadcasted_iota(jnp.int32, (6 * HALF, LW), 0)
    col = lax.broadcasted_iota(jnp.int32, (6 * HALF, LW), 1)
    same = ((row & (LW - 1)) >= HALF) == (col >= HD)
    return jnp.where(same, row & (HALF - 1), -1).astype(BF16)


def _column(block_ref, codes, half, i):
    pick = jnp.where(codes == i.astype(BF16), jnp.ones((), BF16), jnp.zeros((), BF16))
    block = block_ref[:, :, half * 6 * HALF:(half + 1) * 6 * HALF].reshape(HP * HD, 6 * HALF)
    return jnp.dot(block, pick, preferred_element_type=F32)


def _halfsums(x, row, left1):
    row_l = jnp.where(left1, row, 0.0)
    return (jnp.sum(x * row_l, axis=1, keepdims=True), jnp.sum(x * (row - row_l), axis=1, keepdims=True))


def _pair_rows(row):
    return [row[:, p * LW:(p + 1) * LW] for p in range(HP)]


def _store_columns(ref, p, t_mask, cols):
    for j, col in enumerate(cols):
        pltpu.store(ref.at[pl.ds(2 * p + j, 1)], jnp.broadcast_to(col[None], (1, HD, CHUNK)), mask=t_mask[None])


def _columns_to_rows(cols_ref, rows_ref):
    for p in range(HP):
        rows_ref[:, p * LW:(p + 1) * LW] = cols_ref[2 * p:2 * p + 2].reshape(LW, CHUNK).T


NHALF = T // HALF
HALVES = CHUNK // HALF


def _scan_fwd(r, w, k, a, b, v3):
    def kern(r_ref, w_ref, k_ref, a_ref, b_ref, v_ref, y_ref, ck_ref, st_hbm, sa_hbm,
             s_ref, vb_ref, yc_ref, st_ref, sa_ref, sems):
        c = pl.program_id(0)

        @pl.when(c == 0)
        def _():
            s_ref[...] = jnp.zeros_like(s_ref)

        lane = lax.broadcasted_iota(jnp.int32, (HD, CHUNK), 1)
        left = lane < HD
        left1 = lax.broadcasted_iota(jnp.int32, (1, LW), 1) < HD
        codes = _pick_codes()

        def flush(slot, half_index):
            return [pltpu.make_async_copy(src.at[slot], dst.at[half_index], sems.at[j, slot])
                    for j, (src, dst) in enumerate(((st_ref, st_hbm), (sa_ref, sa_hbm)))]

        for half in range(HALVES):
            ck_ref[half] = s_ref[...]
            vb_ref[...] = _column(v_ref, codes, half, jnp.int32(0))

            @pl.when(c > 0)
            def _():
                for cp in flush(half, (c - 1) * HALVES + half):
                    cp.wait()

            def step(i, carry):
                t = half * HALF + i
                row = lambda ref: _pair_rows(ref[pl.ds(t, 1), :])
                S = [s_ref[p] for p in range(HP)]
                sa = [jnp.where(left, *_halfsums(s, a, left1)) for s, a in zip(S, row(a_ref))]
                S = [s * w + c_ * b + vb_ref[pl.ds(p * HD, HD), :] * k
                     for p, (s, w, c_, b, k) in enumerate(zip(S, row(w_ref), sa, row(b_ref), row(k_ref)))]
                for p, (s, c_) in enumerate(zip(S, sa)):
                    s_ref[p] = s
                    st_ref[half, i, p] = s
                    sa_ref[half, i, p] = c_
                for p, (s, r) in enumerate(zip(S, row(r_ref))):
                    _store_columns(yc_ref, p, lane == t, _halfsums(s, r, left1))
                vb_ref[...] = _column(v_ref, codes, half, i + 1)
                return carry

            lax.fori_loop(0, HALF, step, 0, unroll=8)
            for cp in flush(half, c * HALVES + half):
                cp.start()
        _columns_to_rows(yc_ref, y_ref)

        @pl.when(c == NCHUNK - 1)
        def _():
            for half in range(HALVES):
                for cp in flush(half, c * HALVES + half):
                    cp.wait()

    rowblk = pl.BlockSpec((CHUNK, RW), lambda c: (c, 0))
    saved = jax.ShapeDtypeStruct((NHALF, HALF, HP, HD, LW), F32)
    stage = pltpu.VMEM((HALVES, HALF, HP, HD, LW), F32)
    return pl.pallas_call(
        kern, name="rwkv_scan_fwd", grid=(NCHUNK,),
        in_specs=[rowblk] * 5 + [pl.BlockSpec((HP, HD, 6 * CHUNK), lambda c: (0, 0, c))],
        out_specs=[rowblk, pl.BlockSpec((HALVES, HP, HD, LW), lambda c: (c, 0, 0, 0)),
                   pl.BlockSpec(memory_space=pl.ANY), pl.BlockSpec(memory_space=pl.ANY)],
        out_shape=[jax.ShapeDtypeStruct((T, RW), F32), jax.ShapeDtypeStruct((NHALF, HP, HD, LW), F32), saved, saved],
        scratch_shapes=[pltpu.VMEM((HP, HD, LW), F32), pltpu.VMEM((HP * HD, LW), F32),
                        pltpu.VMEM((HEADS, HD, CHUNK), F32), stage, stage, pltpu.SemaphoreType.DMA((2, HALVES))],
        compiler_params=_cp(("arbitrary",)),
    )(r, w, k, a, b, v3)


def _scan_bwd(r, w, k, a, b, v3, dy3, ck, st, sa):
    def kern(r_ref, w_ref, k_ref, a_ref, b_ref, v_ref, dy_ref, ck_ref, st_hbm, sa_hbm,
             dr_ref, dw_ref, dk_ref, da_ref, db_ref, dv_ref, ds_ref, sb_ref, sa_ref, pick_ref, dvc_ref, sems):
        c = pl.program_id(0)
        chunk = NCHUNK - 1 - c

        @pl.when(c == 0)
        def _():
            ds_ref[...] = jnp.zeros_like(ds_ref)

        lane = lax.broadcasted_iota(jnp.int32, (HD, CHUNK), 1)
        left = lane < HD
        left1 = lax.broadcasted_iota(jnp.int32, (1, LW), 1) < HD
        codes = _pick_codes()

        def rowsum(x):
            return jnp.sum(x, axis=0, keepdims=True)

        def fetch(slot, half_index):
            return [pltpu.make_async_copy(st_hbm.at[half_index], sb_ref.at[slot, pl.ds(1, HALF)], sems.at[0, slot]),
                    pltpu.make_async_copy(sa_hbm.at[half_index], sa_ref.at[slot], sems.at[1, slot])]

        def picks(half, i):
            pick_ref[pl.ds(0, HP * HD), :] = _column(v_ref, codes, half, i)
            pick_ref[pl.ds(HP * HD, HP * HD), :] = _column(dy_ref, codes, half, i)

        @pl.when(c == 0)
        def _():
            for cp in fetch(HALVES - 1, chunk * HALVES + HALVES - 1):
                cp.start()

        for half in reversed(range(HALVES)):
            base = half * HALF
            for cp in fetch(half, chunk * HALVES + half):
                cp.wait()
            if half:
                for cp in fetch(half - 1, chunk * HALVES + half - 1):
                    cp.start()
            else:
                @pl.when(chunk > 0)
                def _():
                    for cp in fetch(HALVES - 1, chunk * HALVES - 1):
                        cp.start()
            sb_ref[half, 0] = ck_ref[half]
            picks(half, jnp.int32(HALF - 1))

            def back(ii, carry):
                i = HALF - 1 - ii
                t = base + i
                row = lambda ref: _pair_rows(ref[pl.ds(t, 1), :])
                a_r, b_r, k_r, w_r, r_r = row(a_ref), row(b_ref), row(k_ref), row(w_ref), row(r_ref)
                vs = [pick_ref[pl.ds(p * HD, HD), :] for p in range(HP)]
                dys = [pick_ref[pl.ds((HP + p) * HD, HD), :] for p in range(HP)]
                picks(half, jnp.maximum(i - 1, 0))
                dr, dw, db, dk, da = [], [], [], [], []
                for p in range(HP):
                    Sp, dy = sb_ref[half, i, p], dys[p]
                    dS = ds_ref[p] + dy * r_r[p]
                    dr.append(rowsum(sb_ref[half, i + 1, p] * dy))
                    dw.append(rowsum(dS * Sp))
                    db.append(rowsum(dS * sa_ref[half, i, p]))
                    dk.append(rowsum(dS * vs[p]))
                    dsa = jnp.where(left, *_halfsums(dS, b_r[p], left1))
                    _store_columns(dvc_ref, p, lane == t, _halfsums(dS, k_r[p], left1))
                    da.append(rowsum(Sp * dsa))
                    ds_ref[p] = dS * w_r[p] + dsa * a_r[p]
                for ref, pieces in ((dr_ref, dr), (dw_ref, dw), (db_ref, db), (dk_ref, dk), (da_ref, da)):
                    ref[pl.ds(t, 1), :] = jnp.concatenate(pieces, axis=1)
                return carry

            lax.fori_loop(0, HALF, back, 0, unroll=8)
        _columns_to_rows(dvc_ref, dv_ref)

    rowblk = pl.BlockSpec((CHUNK, RW), lambda c: (NCHUNK - 1 - c, 0))
    col3blk = pl.BlockSpec((HP, HD, 6 * CHUNK), lambda c: (0, 0, NCHUNK - 1 - c))
    rowshape = jax.ShapeDtypeStruct((T, RW), F32)
    return pl.pallas_call(
        kern, name="rwkv_scan_bwd", grid=(NCHUNK,),
        in_specs=[rowblk] * 5 + [col3blk, col3blk,
                                 pl.BlockSpec((HALVES, HP, HD, LW), lambda c: (NCHUNK - 1 - c, 0, 0, 0)),
                                 pl.BlockSpec(memory_space=pl.ANY), pl.BlockSpec(memory_space=pl.ANY)],
        out_specs=[rowblk] * 6, out_shape=[rowshape] * 6,
        scratch_shapes=[pltpu.VMEM((HP, HD, LW), F32), pltpu.VMEM((HALVES, HALF + 1, HP, HD, LW), F32),
                        pltpu.VMEM((HALVES, HALF, HP, HD, LW), F32), pltpu.VMEM((2 * HP * HD, LW), F32),
                        pltpu.VMEM((HEADS, HD, CHUNK), F32), pltpu.SemaphoreType.DMA((2, HALVES))],
        compiler_params=_cp(("arbitrary",)),
    )(r, w, k, a, b, v3, dy3, ck, st, sa)


NT = (((1,), (1,)), ((), ()))
TN = (((0,), (0,)), ((), ()))
SCALE = HD ** -0.5
QKV_G = 3 * RW


def _attn_setup(g):
    dil = DILS[g]
    qkv = [pl.BlockSpec((T, LW), lambda hp, c=(g * QKV_G + s * RW) // LW: (0, c + hp)) for s in range(3)]
    tile = pl.BlockSpec((T, LW), lambda hp: (0, hp))
    bias = pl.BlockSpec((2, BLK, 2 * BLK), lambda hp: (hp, 0, 0))

    def blocks():
        for r in range(dil):
            for n in range(T // dil // BLK):
                rows = pl.ds(n * BLK * dil + r, BLK, stride=dil)
                keys = pl.ds((n - 1) * BLK * dil + r, 2 * BLK, stride=dil) if n else rows
                yield n, rows, keys

    return qkv, tile, bias, blocks


def _band(n):
    qi = lax.broadcasted_iota(jnp.int32, (BLK, 2 * BLK), 0)
    ki = lax.broadcasted_iota(jnp.int32, (BLK, 2 * BLK), 1)
    band = (ki >= qi) & (ki <= qi + BLK)
    return band if n else band[:, BLK:]


def _head_masks():
    lane = lax.broadcasted_iota(jnp.int32, (BLK, LW), 1)
    return lane < HD, [(lane < HD).astype(BF16), (lane >= HD).astype(BF16)]


def _attn_fwd(pq, bias, g):
    qkv, tile, bias_spec, blocks = _attn_setup(g)

    def kern(q_ref, k_ref, v_ref, b_ref, o_ref, l_ref):
        left, masks = _head_masks()
        for n, rows, keys in blocks():
            qb, kc, vc = q_ref[rows, :].astype(BF16), k_ref[keys, :].astype(BF16), v_ref[keys, :].astype(BF16)
            valid = _band(n)
            o, lse = [], []
            for j in range(2):
                bias_j = b_ref[j] if n else b_ref[j][:, BLK:]
                s = lax.dot_general(qb * masks[j], kc, NT, preferred_element_type=F32) * SCALE + bias_j
                s = jnp.where(valid, s, -jnp.inf)
                m = jnp.max(s, axis=1, keepdims=True)
                e = jnp.exp(s - m)
                den = jnp.sum(e, axis=1, keepdims=True)
                o.append(jnp.dot((e / den).astype(BF16), vc, preferred_element_type=F32))
                lse.append(m + jnp.log(den))
            o_ref[rows, :] = jnp.where(left, o[0], o[1])
            l_ref[rows, :] = jnp.where(left, lse[0], lse[1])

    shape = jax.ShapeDtypeStruct((T, RW), F32)
    return pl.pallas_call(
        kern, name=f"attn_fwd_{g}", grid=(HP,),
        in_specs=qkv + [bias_spec], out_specs=[tile, tile], out_shape=[shape, shape],
        compiler_params=_cp(("parallel",)),
    )(pq, pq, pq, bias)


def _attn_bwd(pq, bias, do, o, lse, dlse, g):
    qkv, tile, bias_spec, blocks = _attn_setup(g)

    def kern(q_ref, k_ref, v_ref, b_ref, do_ref, o_ref, l_ref, dl_ref, dq_ref, dk_ref, dv_ref, db_ref):
        left, masks = _head_masks()
        lane = lax.broadcasted_iota(jnp.int32, (BLK, LW), 1)
        dk_ref[...] = jnp.zeros_like(dk_ref)
        dv_ref[...] = jnp.zeros_like(dv_ref)
        db_ref[...] = jnp.zeros_like(db_ref)

        def column(tile_, j):
            return jnp.sum(jnp.where(lane == j * HD, tile_, 0.0), axis=1, keepdims=True)

        for n, rows, keys in blocks():
            qb, kc, vc = q_ref[rows, :].astype(BF16), k_ref[keys, :].astype(BF16), v_ref[keys, :].astype(BF16)
            dof, valid = do_ref[rows, :], _band(n)
            dob, prod = dof.astype(BF16), dof * o_ref[rows, :]
            dq = []
            for j in range(2):
                bias_j = b_ref[j] if n else b_ref[j][:, BLK:]
                delta = jnp.sum(prod * masks[j].astype(F32), axis=1, keepdims=True)
                qm, dom = qb * masks[j], dob * masks[j]
                s = lax.dot_general(qm, kc, NT, preferred_element_type=F32) * SCALE + bias_j
                p = jnp.where(valid, jnp.exp(s - column(l_ref[rows, :], j)), 0.0)
                dp = lax.dot_general(dom, vc, NT, preferred_element_type=F32)
                ds = p * (dp + (column(dl_ref[rows, :], j) - delta))
                if n:
                    db_ref[j] += ds
                else:
                    db_ref[j, :, BLK:] += ds
                dsb = (ds * SCALE).astype(BF16)
                dq.append(jnp.dot(dsb, kc, preferred_element_type=F32))
                dk_ref[keys, :] += lax.dot_general(dsb, qm, TN, preferred_element_type=F32)
                dv_ref[keys, :] += lax.dot_general(p.astype(BF16), dom, TN, preferred_element_type=F32)
            dq_ref[rows, :] = jnp.where(left, dq[0], dq[1])

    shape = jax.ShapeDtypeStruct((T, RW), F32)
    return pl.pallas_call(
        kern, name=f"attn_bwd_{g}", grid=(HP,),
        in_specs=qkv + [bias_spec] + [tile] * 4, out_specs=[tile] * 3 + [bias_spec],
        out_shape=[shape] * 3 + [jax.ShapeDtypeStruct((HEADS, BLK, 2 * BLK), F32)],
        compiler_params=_cp(("parallel",)),
    )(pq, pq, pq, bias, do, o, lse, dlse)


NBUCKET = 32
NPAIR = BLK * 2 * BLK


def _relbias_table(rbT, onehotT):
    def kern(rb_ref, oh_ref, out_ref):
        out_ref[0] = sum(jnp.dot(p, oh_ref[0], preferred_element_type=F32) for p in _split3(rb_ref[0]))

    return pl.pallas_call(
        kern, name="relbias_table", grid=(3,),
        in_specs=[pl.BlockSpec((1, HEADS, NBUCKET), lambda g: (g, 0, 0)),
                  pl.BlockSpec((1, NBUCKET, NPAIR), lambda g: (g, 0, 0))],
        out_specs=pl.BlockSpec((1, HEADS, NPAIR), lambda g: (g, 0, 0)),
        out_shape=jax.ShapeDtypeStruct((3, HEADS, NPAIR), F32),
        compiler_params=_cp(("parallel",)),
    )(rbT, onehotT)


def _relbias_grad(db, onehotT):
    nt = (((1,), (1,)), ((), ()))

    def kern(db_ref, oh_ref, out_ref):
        hi, mid, _ = _split3(db_ref[0])
        out_ref[0] = (lax.dot_general(hi, oh_ref[0], nt, preferred_element_type=F32)
                      + lax.dot_general(mid, oh_ref[0], nt, preferred_element_type=F32))

    return pl.pallas_call(
        kern, name="relbias_grad", grid=(3,),
        in_specs=[pl.BlockSpec((1, HEADS, NPAIR), lambda g: (g, 0, 0)),
                  pl.BlockSpec((1, NBUCKET, NPAIR), lambda g: (g, 0, 0))],
        out_specs=pl.BlockSpec((1, HEADS, NBUCKET), lambda g: (g, 0, 0)),
        out_shape=jax.ShapeDtypeStruct((3, HEADS, NBUCKET), F32),
        compiler_params=_cp(("parallel",)),
    )(db, onehotT)


def _adamw(w, g, m, v):
    m2 = ADAM_B1 * m + (1.0 - ADAM_B1) * g
    v2 = ADAM_B2 * v + (1.0 - ADAM_B2) * (g * g)
    m_hat = m2 / (1.0 - ADAM_B1 ** ADAM_STEP)
    v_hat = v2 / (1.0 - ADAM_B2 ** ADAM_STEP)
    return -ADAM_LR * (m_hat / (jnp.sqrt(v_hat) + ADAM_EPS) + ADAM_WD * w), m2, v2


def _ada_mod(c_all, ada_w, ada_b_loc):
    def kern(c_ref, w_ref, b_ref, o_ref):
        c = c_ref[...]
        cond = c * jax.nn.sigmoid(c)
        o_ref[0] = jnp.dot(cond, w_ref[0], precision=HI, preferred_element_type=F32) + b_ref[0]

    ncol = ada_w.shape[2]
    return pl.pallas_call(
        kern, name="ada_mod", grid=(2,),
        in_specs=[pl.BlockSpec((NDEV, D), lambda i: (0, 0)),
                  pl.BlockSpec((1, D, ncol), lambda i: (i, 0, 0)),
                  pl.BlockSpec((1, 1, ncol), lambda i: (i, 0, 0))],
        out_specs=pl.BlockSpec((1, NDEV, ncol), lambda i: (i, 0, 0)),
        out_shape=jax.ShapeDtypeStruct((2, NDEV, ncol), F32),
        compiler_params=_cp(("parallel",)),
    )(c_all, ada_w, ada_b_loc.reshape(2, 1, ncol))


def _ada_grad_adamw(cT_all, dmod_loc, w, m, v):
    ncol = w.shape[2]
    tr = 256

    def kern(c_ref, d_ref, w_ref, m_ref, v_ref, g_ref, dl_ref, m2_ref, v2_ref):
        c = c_ref[...]
        cond = c * jax.nn.sigmoid(c)
        g = jnp.dot(cond, d_ref[0], precision=HI, preferred_element_type=F32)
        dl, m2, v2 = _adamw(w_ref[0], g, m_ref[0], v_ref[0])
        g_ref[0], dl_ref[0], m2_ref[0], v2_ref[0] = g, dl, m2, v2

    big = pl.BlockSpec((1, tr, ncol), lambda i, j: (i, j, 0))
    shp = jax.ShapeDtypeStruct(w.shape, F32)
    return pl.pallas_call(
        kern, name="ada_grad_adamw", grid=(2, D // tr),
        in_specs=[pl.BlockSpec((tr, NDEV), lambda i, j: (j, 0)),
                  pl.BlockSpec((1, NDEV, ncol), lambda i, j: (i, 0, 0)), big, big, big],
        out_specs=[big] * 4, out_shape=[shp] * 4,
        compiler_params=_cp(("parallel", "parallel")),
    )(cT_all, dmod_loc, w, m, v)


def _sum_adamw(recv, w, m, v, name, tr):
    S = recv.shape[0]
    R, C = w.shape
    assert R % tr == 0 and recv.shape[1:] == (R, C)

    def kern(r_ref, w_ref, m_ref, v_ref, g_ref, dl_ref, m2_ref, v2_ref):
        g = r_ref[0].astype(F32)
        for s in range(1, S):
            g = g + r_ref[s].astype(F32)
        dl, m2, v2 = _adamw(w_ref[...], g, m_ref[...], v_ref[...])
        g_ref[...], dl_ref[...], m2_ref[...], v2_ref[...] = g, dl, m2, v2

    flat = pl.BlockSpec((tr, C), lambda i: (i, 0))
    shp = jax.ShapeDtypeStruct((R, C), F32)
    return pl.pallas_call(
        kern, name=name, grid=(R // tr,),
        in_specs=[pl.BlockSpec((S, tr, C), lambda i: (0, i, 0)), flat, flat, flat],
        out_specs=[flat] * 4, out_shape=[shp] * 4,
        compiler_params=_cp(("parallel",)),
    )(recv, w, m, v)


def _pack(arrs, dtype, row_mult):
    flat = jnp.concatenate([a.reshape(-1).astype(dtype) for a in arrs])
    flat = jnp.pad(flat, (0, -flat.shape[0] % (128 * row_mult)))
    return flat.reshape(-1, 128)


def _pack8(arrs, dtype, row_mult):
    flat = jnp.concatenate([a.reshape(NDEV, -1).astype(dtype) for a in arrs], axis=1)
    flat = jnp.pad(flat, ((0, 0), (0, -flat.shape[1] % (128 * row_mult))))
    return flat.reshape(NDEV, -1, 128)


def _unpack(buf, shapes, lead=()):
    flat = buf.reshape(lead + (-1,))
    out, off = [], 0
    for s in shapes:
        n = math.prod(s)
        out.append(flat[..., off:off + n].reshape(lead + tuple(s)))
        off += n
    return out


def _to_chunks(full, kind):
    if kind == "col":
        x = full.reshape(full.shape[:-1] + (NDEV, full.shape[-1] // NDEV))
        return jnp.moveaxis(x, -2, 0)
    x = full.reshape(full.shape[:-2] + (NDEV, full.shape[-2] // NDEV, full.shape[-1]))
    return jnp.moveaxis(x, -3, 0)


def _from_chunks(g8, kind):
    if kind == "col":
        x = jnp.moveaxis(g8, 0, -2)
        return x.reshape(x.shape[:-2] + (x.shape[-2] * x.shape[-1],))
    x = jnp.moveaxis(g8, 0, -3)
    return x.reshape(x.shape[:-3] + (x.shape[-3] * x.shape[-2], x.shape[-1]))


def _pad_pa(x):
    z = lambda n: jnp.zeros(x.shape[:-1] + (n,), x.dtype)
    return jnp.concatenate([x[..., :1600], z(64), x[..., 1600:1664], z(64), x[..., 1664:1824], z(96)], -1)


def _unpad_pa(x):
    return jnp.concatenate([x[..., :1600], x[..., 1664:1728], x[..., 1792:1952]], -1)


AB_SEGMENTS = ((0, 1600, 0), (1600, 1664, 64), (1664, 1824, 128), (1824, 3360, PAB - 3360))
AB_SHARD = 3360 // NDEV


def _ab_in_padded(g8):
    blocks, at = [], 0
    for start, end, shift in AB_SEGMENTS:
        if start + shift > at:
            blocks.append(jnp.zeros((g8.shape[1], start + shift - at), g8.dtype))
        for j in range(start // AB_SHARD, (end - 1) // AB_SHARD + 1):
            lo, hi = max(start, j * AB_SHARD), min(end, (j + 1) * AB_SHARD)
            blocks.append(g8[j, :, lo - j * AB_SHARD:hi - j * AB_SHARD])
        at = end + shift
    return jnp.concatenate(blocks, axis=1)


def _ab_in_shards(padded):
    shards = []
    for j in range(NDEV):
        pieces = [padded[:, max(start, j * AB_SHARD) + shift:min(end, (j + 1) * AB_SHARD) + shift]
                  for start, end, shift in AB_SEGMENTS if max(start, j * AB_SHARD) < min(end, (j + 1) * AB_SHARD)]
        shards.append(jnp.concatenate(pieces, axis=1))
    return jnp.stack(shards)


def _pad_rows(x, n):
    return jnp.pad(x, ((0, n - x.shape[0]), (0, 0)))


def _bucket_tables():
    qi = jnp.arange(BLK)[:, None]
    ki = jnp.arange(2 * BLK)[None, :]
    rel = BLK + qi - ki
    tabs = []
    for dil in DILS:
        dist = jnp.clip(rel, 0, BLK) * dil
        logd = jnp.log(jnp.maximum(dist, 1).astype(F32) / 16) / math.log(2048 / 16)
        large = jnp.minimum(16 + (logd * 16).astype(jnp.int32), 31)
        tabs.append(jnp.where(dist < 16, dist, large))
    return jnp.stack(tabs)


SHARDED = (("ln_g", "col"), ("ln_b", "col"), ("ab_w_in", "col"), ("rw_w_up", "col"), ("rw_a_up", "col"),
           ("rw_g_up", "col"), ("sc_conv_w", "col"), ("ab_w_out", "row"), ("dil_w_qkv", "col"),
           ("dil_w_out", "col"), ("mlp_w1", "col"), ("mlp_w2", "row"))
FIRST = ("ab_w_in",)
LATER = ("ab_w_out", "dil_w_qkv", "dil_w_out", "mlp_w1", "mlp_w2")
GATHER_BF16 = FIRST + LATER
GATHER_F32 = ("rw_w_up", "rw_a_up", "rw_g_up", "sc_conv_w", "ln_g", "ln_b")
REPLICATED = ("ada_b", "rw_mu", "rw_w0", "rw_a0", "rw_k_k", "rw_k_a", "rw_r_k", "rw_lnx_g", "rw_lnx_b", "rel_bias")
WEIGHTS = ("ada_w", "ada_b", "ln_g", "ln_b", "ab_w_in", "rw_mu", "rw_w0", "rw_w_up", "rw_a0", "rw_a_up",
           "rw_g_up", "rw_k_k", "rw_k_a", "rw_r_k", "rw_lnx_g", "rw_lnx_b", "sc_conv_w", "ab_w_out",
           "dil_w_qkv", "dil_w_out", "rel_bias", "mlp_w1", "mlp_w2")


def _local_step(x0, tgt, mod, W, P, later_weights, early_grads):
    row = lambda a: a.reshape(1, -1)
    W = dict(W)
    m6 = mod.reshape(2, 6, 1, D)
    sc = [m6[0, 1], m6[0, 4], m6[1, 1], m6[1, 4]]
    sh = [m6[0, 0], m6[0, 3], m6[1, 0], m6[1, 3]]
    gt = [m6[0, 2], m6[0, 5], m6[1, 2], m6[1, 5]]
    lng = [row(P["ln_g"][0, 0]), row(P["ln_g"][0, 1]), row(P["ln_g"][1, 0]), row(P["ln_g"][1, 1])]
    lnb = [row(P["ln_b"][0, 0]), row(P["ln_b"][0, 1]), row(P["ln_b"][1, 0]), row(P["ln_b"][1, 1])]
    E = jnp.kron(jnp.eye(HEADS, dtype=BF16), jnp.ones((HD, HD), BF16))

    def mod_body(r, p):
        u = r[0] * (1.0 + p[0]) + p[1]
        return [u, u], []

    (u0, u0T), _ = _rows("modulate", mod_body, [x0], [sc[0], sh[0]], [(D, BF16), (D, BF16, "T")])

    def post_fwd_body(r, p):
        xn, un = _post_ln_mod(r[0], r[1], *p)
        return [xn, un, un], []

    def post_fwd(s, x, y):
        (xn, un, unT), _ = _rows(f"post_ln_{s}", post_fwd_body, [x, y],
                                 [gt[s], lng[s], lnb[s], sc[s + 1], sh[s + 1]],
                                 [(D, F32), (D, BF16), (D, BF16, "T")])
        return xn, un, unT

    def relu2(acc):
        a = jnp.maximum(acc, 0.0)
        return acc, a * a, a * a

    def relu2_bwd(acc, h):
        return (acc * (2.0 * jnp.maximum(h, 0.0)),)

    p = _mm("ab_in", u0, W["ab_w_in"])
    mu = _pad_pa(P["rw_mu"])
    mu_parts = [mu[:, :512], mu[:, 512:1024], mu[:, 1024:1536], mu[:, 1536:1664], mu[:, 1664:1792], mu[:, 1792:]]
    pre_params = mu_parts + [P["rw_w0"], _pad_rows(P["rw_w_up"], 128), P["rw_a0"], _pad_rows(P["rw_a_up"], 128),
                             _pad_rows(P["rw_g_up"], 256), P["rw_k_k"], P["rw_k_a"],
                             P["sc_conv_w"][0:1], P["sc_conv_w"][1:2], P["sc_conv_w"][2:3]]
    pieces = [(p, 512, 0), (p, 512, 1), (p, 512, 2), (p, 128, 12), (p, 128, 13), (p, 256, 7),
              (p, 512, 4), (p, 512, 5), (p, 512, 6)]
    shifted = [0, 1, 2, 3, 4, 5, 6, 8]
    pre_rows = pieces + [pieces[i] + ("prev",) for i in shifted]
    NPR = 19

    def pre_args(r):
        x, prev = r[:9], dict(zip(shifted, r[9:17]))
        down = lambda i, k: _shift_down(x[i], prev[i], k)
        return x[:6] + [down(i, 1) for i in range(6)] + x[6:9] + [down(6, 1), down(8, 1), down(6, 2), down(8, 2)]

    def pre_fwd_body(r, pp):
        return list(_pre_core(pp[0], *pre_args(r), *pp[1:])), []

    (r_, w_, kh_, v_, a_, b_, gate_, yb), _ = _rows(
        "rwkv_pre", pre_fwd_body, pre_rows, [E] + pre_params, [(RW, F32)] * 7 + [(RW, BF16)], tm=256)
    scan_in = [r_, w_, kh_, a_, b_, _cols3(v_, "rwkv_v_columns")]
    ysc, *saved = _scan_fwd(*scan_in)
    post_params = [P["rw_lnx_g"], P["rw_lnx_b"], P["rw_r_k"].reshape(1, RW)]

    def postmix_fwd_body(r, pp):
        return [_post_core(pp[0], *r, *pp[1:])], []

    (ya,), _ = _rows("rwkv_post", postmix_fwd_body, [ysc, r_, kh_, v_, gate_], [E] + post_params,
                     [(RW, BF16)], tm=256)
    cat = jnp.concatenate([ya, yb], axis=1)
    W.update(later_weights(cat))
    y0 = _mm("ab_out", cat, W["ab_w_out"])
    x1, u1, u1T = post_fwd(0, x0, y0)

    h1, a1, a1T = _mm("mlp1_up_0", u1, W["mlp_w1"][0], out=(F32, BF16, (BF16, "T")), epi=relu2)
    y1 = _mm("mlp1_down_0", a1, W["mlp_w2"][0])
    x2, u2, u2T = post_fwd(1, x1, y1)

    pq = _mm("qkv", u2, W["dil_w_qkv"])
    onehotT = (_bucket_tables().reshape(3, 1, NPAIR) == jnp.arange(NBUCKET).reshape(1, NBUCKET, 1)).astype(BF16)
    rbT = P["rel_bias"].reshape(NBUCKET, 3, HEADS).transpose(1, 2, 0)
    bias = _relbias_table(rbT, onehotT).reshape(3, HEADS, BLK, 2 * BLK)
    og, lse = zip(*[_attn_fwd(pq, bias[g], g) for g in range(3)])

    def merge_fwd_body(r, pp):
        return [_merge_core(*r)], []

    (om,), _ = _rows("attn_merge", merge_fwd_body, list(og + lse), [], [(RW, BF16)])
    y2 = _mm("dil_out", om, W["dil_w_out"])
    x3, u3, u3T = post_fwd(2, x2, y2)

    h3, a3, a3T = _mm("mlp1_up_1", u3, W["mlp_w1"][1], out=(F32, BF16, (BF16, "T")), epi=relu2)
    y3 = _mm("mlp1_down_1", a3, W["mlp_w2"][1])

    def last_body(r, pp):
        x, y, tg = r
        xn, vjp = jax.vjp(_post_ln, x, y, *pp)
        err = xn - tg
        dx, dy, dg, dlg, dlb = vjp(err * (1.0 / D))
        loss = jnp.full((1, 128), (0.5 / D) * jnp.sum(err * err), F32)
        return [dx, dy], [loss, dg, dlg, dlb]

    (dxp, dy3), (loss_acc, dg3, dlng3, dlnb3) = _rows(
        "final_ln_loss", last_body, [x3, y3, tgt], [gt[3], lng[3], lnb[3]],
        [(D, F32), (D, BF16)], [(1, 128), (1, D), (1, D), (1, D)])

    G = {}
    dsc, dsh, dgt = [None] * 4, [None] * 4, [None] * 4
    dlng, dlnb = [None] * 4, [None] * 4
    dgt[3], dlng[3], dlnb[3] = dg3, dlng3, dlnb3

    def mlp_bwd(i, uT, h, aT, dy):
        dh = _mm(f"mlp_dh_{i}", dy, W["mlp_w2"][i], tb=True, out=(BF16,), epi=relu2_bwd, extras=(h,))
        gw2 = _mm(f"mlp_dw2_{i}", aT, dy)
        du = _mm(f"mlp_du_{i}", dh, W["mlp_w1"][i], tb=True)
        gw1 = _mm(f"mlp_dw1_{i}", uT, dh)
        return du, gw1, gw2

    def post_bwd_body(r, pp):
        x, y, dxn, dun = r
        _, vjp = jax.vjp(_post_ln_mod, x, y, *pp)
        dx, dy, dg, dlg, dlb, dscn, dshn = vjp((dxn, dun))
        return [dx, dy], [dg, dlg, dlb, dscn, dshn]

    def post_bwd(s, x, y, dxn, dun):
        (dx, dy), (dgt[s], dlng[s], dlnb[s], dsc[s + 1], dsh[s + 1]) = _rows(
            f"post_ln_bwd_{s}", post_bwd_body, [x, y, dxn, dun],
            [gt[s], lng[s], lnb[s], sc[s + 1], sh[s + 1]], [(D, F32), (D, BF16)], [(1, D)] * 5)
        return dx, dy

    du3, gw1_1, gw2_1 = mlp_bwd(1, u3T, h3, a3T, dy3)
    dxp, dy2 = post_bwd(2, x2, y2, dxp, du3)

    G["dil_w_out"] = _mm("dil_out_dw", om.T, dy2)[None]
    do = _mm("dil_out_dx", dy2, W["dil_w_out"], tb=True)

    def merge_bwd_body(r, pp):
        _, vjp = jax.vjp(_merge_core, *r[:6])
        d = vjp(r[6])
        return list(d[:3]) + [_headsum(d[3 + g], pp[0]) for g in range(3)], []

    mb, _ = _rows("attn_merge_bwd", merge_bwd_body, list(og + lse) + [do], [E],
                  [(RW, F32)] * 6)
    back = [_attn_bwd(pq, bias[g], mb[g], og[g], lse[g], mb[3 + g], g) for g in range(3)]
    dpq = jnp.concatenate([t for dq, dk, dv, _ in back for t in (dq, dk, dv)], axis=1).astype(BF16)
    rb = _relbias_grad(jnp.stack([b[3] for b in back]).reshape(3, HEADS, NPAIR), onehotT)
    G["rel_bias"] = rb.transpose(2, 0, 1).reshape(NBUCKET, 3 * HEADS)
    G["dil_w_qkv"] = _mm("qkv_dw", u2T, dpq)[None]
    du2 = _mm("qkv_dx", dpq, W["dil_w_qkv"], tb=True)
    dxp, dy1 = post_bwd(1, x1, y1, dxp, du2)

    du1, gw1_0, gw2_0 = mlp_bwd(0, u1T, h1, a1T, dy1)
    G["mlp_w1"] = jnp.stack([gw1_0, gw1_1])
    G["mlp_w2"] = jnp.stack([gw2_0, gw2_1])
    dxp, dy0 = post_bwd(0, x0, y0, dxp, du1)

    G["ab_w_out"] = _mm("ab_out_dw", cat.T, dy0)[None]
    dcat = _mm("ab_out_dx", dy0, W["ab_w_out"], tb=True)
    post_params = [post_params[0] + early_grads(G)] + post_params[1:]

    def postmix_bwd_body(r, pp):
        _, vjp = jax.vjp(functools.partial(_post_core, pp[0]), *r[:5], *pp[1:])
        d = vjp(r[5])
        return list(d[:5]), list(d[5:])

    (dysc, dr1, dkh1, dv1, dgate), (G["rw_lnx_g"], G["rw_lnx_b"], drk) = _rows(
        "rwkv_post_bwd", postmix_bwd_body, [ysc, r_, kh_, v_, gate_, (dcat, 512, 0)], [E] + post_params,
        [(RW, F32)] * 5, [(1, RW)] * 3, tm=256)
    G["rw_r_k"] = drk.reshape(1, HEADS, HD)
    dr2, dw2, dk2, da2, db2, dv2 = _scan_bwd(*scan_in, _cols3(dysc, "rwkv_dy_columns"), *saved)

    def pre_bwd_body(r, pp):
        prim, ct = pre_args(r[:len(pre_rows)]), r[len(pre_rows):]
        _, vjp = jax.vjp(functools.partial(_pre_core, pp[0]), *prim, *pp[1:])
        cts = (ct[0] + ct[1], ct[2], ct[3] + ct[4], ct[5] + ct[6], ct[7], ct[8], ct[9], ct[10])
        d = vjp(cts)
        z = jnp.zeros_like(d[12])
        dp = jnp.concatenate([d[0], d[1], d[2], d[3], d[4], d[5], d[12], d[13], d[14]], axis=1)
        dp1 = jnp.concatenate([d[6], d[7], d[8], d[9], d[10], d[11], d[15], z, d[16]], axis=1)
        dp2 = jnp.concatenate([d[17], z, d[18]], axis=1)
        return [dp, dp1, dp2], list(d[NPR:])

    acc_shapes = [a.shape for a in pre_params]
    (dp, dp1, dp2), pacc = _rows(
        "rwkv_pre_bwd", pre_bwd_body,
        pre_rows + [dr1, dr2, dw2, dkh1, dk2, dv1, dv2, da2, db2, dgate, (dcat, 512, 1)],
        [E] + pre_params, [(PAB, F32), (PAB, F32), (PB, F32)], acc_shapes, tm=256)
    G["rw_mu"] = _unpad_pa(jnp.concatenate(pacc[:6], axis=1))
    G["rw_w0"], G["rw_a0"], G["rw_k_k"], G["rw_k_a"] = pacc[6], pacc[8], pacc[11], pacc[12]
    G["rw_w_up"] = pacc[7][None, :64]
    G["rw_a_up"] = pacc[9][None, :64]
    G["rw_g_up"] = pacc[10][None, :160]
    G["sc_conv_w"] = jnp.concatenate(pacc[13:16], axis=0)[None]

    def shift_merge_body(r, pp):
        d0, d1, d1_next, d2, d2_next = r
        d = d0 + _shift_up(d1, d1_next, 1)
        return [jnp.concatenate([d[:, :PA], d[:, PA:] + _shift_up(d2, d2_next, 2)], axis=1)], []

    (dpt,), _ = _rows("shift_merge", shift_merge_body,
                      [dp, dp1, (dp1, PAB, 0, "next"), dp2, (dp2, PB, 0, "next")], [], [(PAB, BF16)])
    du0 = _mm("ab_in_dx", dpt, W["ab_w_in"], tb=True)

    def mod_bwd_body(r, pp):
        du, dx, x = r
        return [dx + du * (1.0 + pp[0])], [jnp.sum(du * x, axis=0, keepdims=True), jnp.sum(du, axis=0, keepdims=True)]

    (grad_x,), (dsc[0], dsh[0]) = _rows("modulate_bwd", mod_bwd_body, [du0, dxp, x0], [sc[0]], [(D, F32)],
                                        [(1, D), (1, D)])

    G["ln_g"] = jnp.concatenate(dlng, axis=0).reshape(2, 2, D)
    G["ln_b"] = jnp.concatenate(dlnb, axis=0).reshape(2, 2, D)
    dmod = jnp.concatenate([dsh[0], dsc[0], dgt[0], dsh[1], dsc[1], dgt[1],
                            dsh[2], dsc[2], dgt[2], dsh[3], dsc[3], dgt[3]], axis=1).reshape(2, 6 * D)
    return loss_acc[0, 0], grad_x, dmod, G, lambda: _ab_in_shards(_mm("ab_in_dw", u0T, dpt))[:, None]


def kernel(x, c, ada_w, ada_b, ln_g, ln_b, ab_w_in, rw_mu, rw_w0, rw_w_up, rw_a0, rw_a_up, rw_g_up, rw_k_k, rw_k_a, rw_r_k, rw_lnx_g, rw_lnx_b, sc_conv_w, ab_w_out, dil_w_qkv, dil_w_out, rel_bias, mlp_w1, mlp_w2, loss_target, m_ada_w, m_ada_b, m_ln_g, m_ln_b, m_ab_w_in, m_rw_mu, m_rw_w0, m_rw_w_up, m_rw_a0, m_rw_a_up, m_rw_g_up, m_rw_k_k, m_rw_k_a, m_rw_r_k, m_rw_lnx_g, m_rw_lnx_b, m_sc_conv_w, m_ab_w_out, m_dil_w_qkv, m_dil_w_out, m_rel_bias, m_mlp_w1, m_mlp_w2, v_ada_w, v_ada_b, v_ln_g, v_ln_b, v_ab_w_in, v_rw_mu, v_rw_w0, v_rw_w_up, v_rw_a0, v_rw_a_up, v_rw_g_up, v_rw_k_k, v_rw_k_a, v_rw_r_k, v_rw_lnx_g, v_rw_lnx_b, v_sc_conv_w, v_ab_w_out, v_dil_w_qkv, v_dil_w_out, v_rel_bias, v_mlp_w1, v_mlp_w2):
    w = dict(ada_w=ada_w, ada_b=ada_b, ln_g=ln_g, ln_b=ln_b, ab_w_in=ab_w_in, rw_mu=rw_mu, rw_w0=rw_w0,
             rw_w_up=rw_w_up, rw_a0=rw_a0, rw_a_up=rw_a_up, rw_g_up=rw_g_up, rw_k_k=rw_k_k, rw_k_a=rw_k_a,
             rw_r_k=rw_r_k, rw_lnx_g=rw_lnx_g, rw_lnx_b=rw_lnx_b, sc_conv_w=sc_conv_w, ab_w_out=ab_w_out,
             dil_w_qkv=dil_w_qkv, dil_w_out=dil_w_out, rel_bias=rel_bias, mlp_w1=mlp_w1, mlp_w2=mlp_w2)
    m = dict(ada_w=m_ada_w, ada_b=m_ada_b, ln_g=m_ln_g, ln_b=m_ln_b, ab_w_in=m_ab_w_in, rw_mu=m_rw_mu,
             rw_w0=m_rw_w0, rw_w_up=m_rw_w_up, rw_a0=m_rw_a0, rw_a_up=m_rw_a_up, rw_g_up=m_rw_g_up,
             rw_k_k=m_rw_k_k, rw_k_a=m_rw_k_a, rw_r_k=m_rw_r_k, rw_lnx_g=m_rw_lnx_g, rw_lnx_b=m_rw_lnx_b,
             sc_conv_w=m_sc_conv_w, ab_w_out=m_ab_w_out, dil_w_qkv=m_dil_w_qkv, dil_w_out=m_dil_w_out,
             rel_bias=m_rel_bias, mlp_w1=m_mlp_w1, mlp_w2=m_mlp_w2)
    v = dict(ada_w=v_ada_w, ada_b=v_ada_b, ln_g=v_ln_g, ln_b=v_ln_b, ab_w_in=v_ab_w_in, rw_mu=v_rw_mu,
             rw_w0=v_rw_w0, rw_w_up=v_rw_w_up, rw_a0=v_rw_a0, rw_a_up=v_rw_a_up, rw_g_up=v_rw_g_up,
             rw_k_k=v_rw_k_k, rw_k_a=v_rw_k_a, rw_r_k=v_rw_r_k, rw_lnx_g=v_rw_lnx_g, rw_lnx_b=v_rw_lnx_b,
             sc_conv_w=v_sc_conv_w, ab_w_out=v_ab_w_out, dil_w_qkv=v_dil_w_qkv, dil_w_out=v_dil_w_out,
             rel_bias=v_rel_bias, mlp_w1=v_mlp_w1, mlp_w2=v_mlp_w2)
    kinds = dict(SHARDED)
    me = 4 * lax.axis_index("x") + 2 * lax.axis_index("y") + lax.axis_index("c")
    ncol = ada_w.shape[2]

    small = _all_gather(_pack([c] + [w[n] for n in GATHER_F32], F32, 8), "gather_small")
    parts = _unpack(small, [c.shape] + [w[n].shape for n in GATHER_F32], (NDEV,))
    c_all = parts[0].reshape(NDEV, D)
    P = {n: _from_chunks(t, kinds[n]) for n, t in zip(GATHER_F32, parts[1:])}
    P = {n: (t if n in ("ln_g", "ln_b") else t[0]) for n, t in P.items()}
    for n in REPLICATED[1:]:
        P[n] = w[n]
    def full(n, t):
        t = _from_chunks(t, kinds[n])
        return t if n in ("mlp_w1", "mlp_w2") else t[0]

    (first,) = _all_gather_many([ab_w_in.astype(BF16)], "gather_first_weight")
    W = {"ab_w_in": _ab_in_padded(first[:, 0])}

    ada_b_loc = lax.dynamic_slice(ada_b, (0, ncol * me), (2, ncol))
    mod_part = _ada_mod(c_all, ada_w, ada_b_loc)
    mod_all = _all_gather(mod_part.reshape(-1, 128), "gather_mod").reshape(NDEV, 2, NDEV, ncol)
    mod = lax.dynamic_index_in_dim(mod_all, me, axis=2, keepdims=False)
    mod = mod.transpose(1, 0, 2).reshape(2, 6 * D)

    behind = (mod[0, 0] * 0.0).astype(BF16)
    later = _exchange_start([w[n].astype(BF16) + (behind if n == LATER[0] else 0) for n in LATER], True,
                            "gather_later_weights_start")
    mod = mod + later[-1][0, 0]

    def later_weights(after):
        lands = _exchange_wait(later, True, after, "gather_later_weights_wait")
        return {n: full(n, t) for n, t in zip(LATER, lands)}

    sent = []

    def early_grads(G):
        sent.append(_exchange_start([_to_chunks(G[n], kinds[n]).astype(BF16) for n in LATER], False,
                                    "exchange_later_grads_start"))
        return sent[0][-1][0, 0]

    loss_part, grad_x, dmod, G, in_grad = _local_step(x[0], loss_target[0], mod, W, P, later_weights, early_grads)
    G["ada_b"] = dmod
    big_out = {}

    def update(n, contributions):
        cols = w[n].shape[-1]
        flat = lambda t: t.reshape(-1, cols)
        rows = flat(w[n]).shape[0]
        outs = _sum_adamw(contributions.reshape(-1, rows, cols), flat(w[n]), flat(m[n]), flat(v[n]),
                          f"sum_adamw_{n}", min(rows, 256))
        big_out[n] = [o.reshape(w[n].shape) for o in outs]

    rep_shapes = [w[n].shape for n in REPLICATED] + [(1,)]
    rep_all = _all_gather(_pack([G[n] for n in REPLICATED] + [loss_part], F32, 8), "gather_replicated_grads")
    names = [n for n, _ in SHARDED if n not in GATHER_BF16]
    shard_shapes = [w[n].shape for n in names]
    recv = _all_to_all(_pack8([_to_chunks(G[n], kinds[n]) for n in names], F32, 8), "exchange_small_grads")

    behind = (recv[0, 0, 0] * 0.0 + rep_all[0, 0, 0] * 0.0).astype(BF16)
    last = _exchange_start([in_grad().astype(BF16) + behind], False, "exchange_last_grad_start")

    zero = last[-1][0:1, 0]
    pk = lambda d: _pack([d[n] for n in REPLICATED] + [zero], F32, 8)
    rep_out = _sum_adamw(rep_all, pk(w), pk(m), pk(v), "sum_adamw_replicated", rep_all.shape[1])
    loss = _unpack(rep_out[0], rep_shapes)[-1][0]
    rep_out = [dict(zip(REPLICATED, _unpack(o, rep_shapes))) for o in rep_out]
    dmod_all = _unpack(rep_all, [(2, 6 * D)], (NDEV,))[0]
    dmod_loc = lax.dynamic_slice(dmod_all, (0, 0, ncol * me), (NDEV, 2, ncol)).transpose(1, 0, 2)
    ada_out = _ada_grad_adamw(c_all.T + zero, dmod_loc, ada_w, m_ada_w, v_ada_w)
    pk = lambda d: _pack([d[n] for n in names], F32, 8)
    sh_out = _sum_adamw(recv, pk(w), pk(m), pk(v), "sum_adamw_small", recv.shape[1])
    sh_out = [dict(zip(names, _unpack(o, shard_shapes))) for o in sh_out]
    for n, r in zip(LATER, _exchange_wait(sent[0], False, last[-1], "exchange_later_grads_wait")):
        update(n, r)
    (landed,) = _exchange_wait(last, False, big_out[LATER[-1]][0], "exchange_last_grad_wait")
    update("ab_w_in", landed)
    sh_out = [{**d, **{n: big_out[n][i] for n in GATHER_BF16}} for i, d in enumerate(sh_out)]

    def pick(i, n):
        if n == "ada_w":
            return ada_out[i]
        return rep_out[i][n] if n in REPLICATED else sh_out[i][n]

    outs = [loss, grad_x[None]]
    for i in range(4):
        outs += [pick(i, n) for n in WEIGHTS]
    return tuple(outs)
```

```python
import functools
import math

import jax
import jax.numpy as jnp
from jax import lax
from jax.experimental import pallas as pl
from jax.experimental.pallas import tpu as pltpu

F32 = jnp.float32
BF16 = jnp.bfloat16
HI = lax.Precision.HIGHEST

NDEV = 8
T = 2048
D = 1024
DFF = 4096
HEADS = 8
HD = 64
RW = 512
PA = 2048
PB = 1536
PAB = PA + PB
QKV = 4608
DILS = (1, 4, 16)
BLK = 128
ALPHA = 4.0 ** 0.25
LN_EPS = 1e-5
GN_EPS = 64e-5
ADAM_LR, ADAM_B1, ADAM_B2, ADAM_EPS, ADAM_WD, ADAM_STEP = 0.001, 0.9, 0.999, 1e-8, 0.01, 10
VMEM_LIMIT = 56 * 1024 * 1024


def _cp(sem):
    return pltpu.CompilerParams(dimension_semantics=sem, vmem_limit_bytes=VMEM_LIMIT)


def _slot(px, py, pc):
    return 4 * px + 2 * py + pc


def _all_gather(x, name):
    R, C = x.shape

    def body(x_ref, out_ref, send_sems, recv_sems, local_sem):
        xi, yi, ci = lax.axis_index("x"), lax.axis_index("y"), lax.axis_index("c")
        me, sibling = (xi, yi, ci), (xi, yi, 1 - ci)
        chips = [(1 - xi, yi), (xi, 1 - yi), (1 - xi, 1 - yi)]

        def rows(px, py, pc):
            return out_ref.at[_slot(px, py, pc)]

        def copy(k, block, to, src=None):
            return pltpu.make_async_remote_copy(
                src_ref=rows(*block) if src is None else src, dst_ref=rows(*block),
                send_sem=send_sems.at[k], recv_sem=recv_sems.at[k],
                device_id=to, device_id_type=pl.DeviceIdType.MESH)

        mine = pltpu.make_async_copy(x_ref, rows(*me), local_sem)
        mine.start()
        first = [copy(0, me, sibling, src=x_ref)]
        first += [copy(1 + j, me, (*chip, ci), src=x_ref) for j, chip in enumerate(chips)]
        for cp in first:
            cp.start()
        passed = [copy(4 + j, (*chip, ci), sibling) for j, chip in enumerate(chips)]
        for j, chip in enumerate(chips):
            copy(1 + j, (*chip, ci), me).wait_recv()
            passed[j].start()
        copy(0, sibling, me).wait_recv()
        for j, chip in enumerate(chips):
            copy(4 + j, (*chip, 1 - ci), me).wait_recv()
        for cp in first + passed:
            cp.wait_send()
        mine.wait()

    return pl.pallas_call(
        body, name=name,
        out_shape=jax.ShapeDtypeStruct((NDEV, R, C), x.dtype),
        in_specs=[pl.BlockSpec(memory_space=pl.ANY)],
        out_specs=pl.BlockSpec(memory_space=pl.ANY),
        scratch_shapes=[pltpu.SemaphoreType.DMA((7,)), pltpu.SemaphoreType.DMA((7,)),
                        pltpu.SemaphoreType.DMA(())],
    )(x)


def _all_to_all(g, name):
    _, R, C = g.shape

    def body(g_ref, out_ref, send_sems, recv_sems, local_sem):
        xi, yi, ci = lax.axis_index("x"), lax.axis_index("y"), lax.axis_index("c")
        my_slot = _slot(xi, yi, ci)
        mine = pltpu.make_async_copy(g_ref.at[my_slot], out_ref.at[my_slot], local_sem)
        mine.start()
        copies = []
        for k in range(1, 8):
            px = 1 - xi if k & 4 else xi
            py = 1 - yi if k & 2 else yi
            pc = 1 - ci if k & 1 else ci
            peer_slot = _slot(px, py, pc)
            copies.append((
                pltpu.make_async_remote_copy(
                    src_ref=g_ref.at[peer_slot], dst_ref=out_ref.at[my_slot],
                    send_sem=send_sems.at[k - 1], recv_sem=recv_sems.at[k - 1],
                    device_id=(px, py, pc), device_id_type=pl.DeviceIdType.MESH),
                pltpu.make_async_remote_copy(
                    src_ref=g_ref.at[peer_slot], dst_ref=out_ref.at[peer_slot],
                    send_sem=send_sems.at[k - 1], recv_sem=recv_sems.at[k - 1],
                    device_id=(px, py, pc), device_id_type=pl.DeviceIdType.MESH)))
        for send, _ in copies:
            send.start()
        for _, recv in copies:
            recv.wait_recv()
        for send, _ in copies:
            send.wait_send()
        mine.wait()

    return pl.pallas_call(
        body, name=name,
        out_shape=jax.ShapeDtypeStruct((NDEV, R, C), g.dtype),
        in_specs=[pl.BlockSpec(memory_space=pl.ANY)],
        out_specs=pl.BlockSpec(memory_space=pl.ANY),
        scratch_shapes=[pltpu.SemaphoreType.DMA((7,)), pltpu.SemaphoreType.DMA((7,)),
                        pltpu.SemaphoreType.DMA(())],
    )(g)


def _my_slot():
    return _slot(lax.axis_index("x"), lax.axis_index("y"), lax.axis_index("c"))


def _put_own(buf, own, slot):
    return lax.dynamic_update_index_in_dim(buf, own, slot, 0)


def _hbm_call(body, name, ins, out_shapes, n_sems):
    anyspec = pl.BlockSpec(memory_space=pl.ANY)
    return pl.pallas_call(
        body, name=name, out_shape=out_shapes,
        in_specs=[anyspec] * len(ins), out_specs=[anyspec] * len(out_shapes),
        scratch_shapes=[pltpu.SemaphoreType.DMA(s) for s in n_sems],
    )(*ins)


def _all_gather_many(xs, name):
    n = len(xs)

    def body(*refs):
        x_refs, o_refs = refs[:n], refs[n:2 * n]
        send_sems, recv_sems = refs[2 * n:]
        xi, yi, ci = lax.axis_index("x"), lax.axis_index("y"), lax.axis_index("c")
        me, sibling = (xi, yi, ci), (xi, yi, 1 - ci)
        chips = [(1 - xi, yi), (xi, 1 - yi), (1 - xi, 1 - yi)]

        def copy(i, k, block, to, src=None):
            dst = o_refs[i].at[_slot(*block)]
            return pltpu.make_async_remote_copy(
                src_ref=dst if src is None else src, dst_ref=dst,
                send_sem=send_sems.at[i, k], recv_sem=recv_sems.at[i, k],
                device_id=to, device_id_type=pl.DeviceIdType.MESH)

        sends = []
        for i in range(n):
            sends += [copy(i, 1 + j, me, (*chip, ci), src=x_refs[i]) for j, chip in enumerate(chips)]
            sends.append(copy(i, 0, me, sibling, src=x_refs[i]))
        for cp in sends:
            cp.start()
        for j, chip in enumerate(chips):
            for i in range(n):
                copy(i, 1 + j, (*chip, ci), me).wait_recv()
                passed = copy(i, 4 + j, (*chip, ci), sibling)
                passed.start()
                sends.append(passed)
        for i in range(n):
            copy(i, 0, sibling, me).wait_recv()
            for j, chip in enumerate(chips):
                copy(i, 4 + j, (*chip, 1 - ci), me).wait_recv()
        for cp in sends:
            cp.wait_send()

    outs = _hbm_call(body, name, xs, [jax.ShapeDtypeStruct((NDEV,) + x.shape, x.dtype) for x in xs],
                     [(n, 7), (n, 7)])
    return [_put_own(o, x[None], _my_slot()) for o, x in zip(outs, xs)]


def _peers(xi, yi, ci):
    return [(1 - xi if k & 4 else xi, 1 - yi if k & 2 else yi, 1 - ci if k & 1 else ci) for k in range(1, 8)]


def _direct_copy(src_refs, land_refs, send_sems, recv_sems, i, k, peer, my_slot, gather):
    src = src_refs[i] if gather else src_refs[i].at[_slot(*peer)]
    return pltpu.make_async_remote_copy(
        src_ref=src, dst_ref=land_refs[i].at[my_slot], send_sem=send_sems.at[7 * i + k], recv_sem=recv_sems.at[7 * i + k],
        device_id=peer, device_id_type=pl.DeviceIdType.MESH)


def _exchange_start(srcs, gather, name):
    n = len(srcs)
    lands = [lax.empty(((NDEV,) + s.shape) if gather else s.shape, s.dtype) for s in srcs]

    def body(*refs):
        s_refs, l_refs = refs[:n], refs[n:2 * n]
        send_sems, recv_sems = refs[2 * n], refs[2 * n + 1]
        token = refs[2 * n + 2 + 2 * n]
        xi, yi, ci = lax.axis_index("x"), lax.axis_index("y"), lax.axis_index("c")
        my_slot = _slot(xi, yi, ci)
        for k, peer in enumerate(_peers(xi, yi, ci)):
            for i in range(n):
                _direct_copy(s_refs, l_refs, send_sems, recv_sems, i, k, peer, my_slot, gather).start()
        token[...] = jnp.zeros_like(token)

    hbm = pl.BlockSpec(memory_space=pltpu.HBM)
    sem = pl.BlockSpec(memory_space=pltpu.SEMAPHORE)
    both = list(srcs) + lands
    return pl.pallas_call(
        body, name=name,
        out_shape=(pltpu.SemaphoreType.DMA((7 * n,)), pltpu.SemaphoreType.DMA((7 * n,)),
                   *[pltpu.HBM(t.shape, t.dtype) for t in both], jax.ShapeDtypeStruct((8, 128), F32)),
        in_specs=[hbm] * (2 * n),
        out_specs=(sem, sem, *[hbm] * (2 * n), pl.BlockSpec(memory_space=pltpu.VMEM)),
        input_output_aliases={i: 2 + i for i in range(2 * n)},
        compiler_params=pltpu.CompilerParams(has_side_effects=pltpu.SideEffectType.DATAFLOW_SIDE_EFFECTING),
    )(*[pltpu.with_memory_space_constraint(t, pltpu.HBM) for t in both])


def _exchange_wait(started, gather, after, name):
    send_sems, recv_sems, *thru, _ = started
    n = len(thru) // 2

    def body(*refs):
        s_refs, l_refs = refs[:n], refs[n:2 * n]
        send_sems, recv_sems = refs[2 * n], refs[2 * n + 1]
        xi, yi, ci = lax.axis_index("x"), lax.axis_index("y"), lax.axis_index("c")
        my_slot = _slot(xi, yi, ci)
        for k, peer in enumerate(_peers(xi, yi, ci)):
            for i in range(n):
                _direct_copy(s_refs, l_refs, send_sems, recv_sems, i, k, peer, my_slot, gather).wait_send()
                _direct_copy(s_refs, l_refs, send_sems, recv_sems, i, k, peer, _slot(*peer), gather).wait_recv()

    hbm = pl.BlockSpec(memory_space=pltpu.HBM)
    sem = pl.BlockSpec(memory_space=pltpu.SEMAPHORE)
    outs = pl.pallas_call(
        body, name=name,
        out_shape=tuple(pltpu.HBM(t.shape, t.dtype) for t in thru),
        in_specs=[hbm] * (2 * n) + [sem, sem, pl.BlockSpec(memory_space=pl.ANY)],
        out_specs=tuple([hbm] * (2 * n)),
        input_output_aliases={i: i for i in range(2 * n)},
        compiler_params=pltpu.CompilerParams(has_side_effects=pltpu.SideEffectType.DATAFLOW_SIDE_EFFECTING),
    )(*thru, send_sems, recv_sems, after)
    slot = _my_slot()
    own = [s[None] if gather else lax.dynamic_index_in_dim(s, slot, 0, keepdims=True) for s in outs[:n]]
    return [_put_own(land, o, slot) for land, o in zip(outs[n:], own)]


def _mm(name, a, b, tb=False, out=(F32,), epi=None, extras=(), tm=1024, tn=512, tk_cap=2048):
    M, K = a.shape
    N = b.shape[0] if tb else b.shape[1]
    tm, tn = min(tm, M), min(tn, N)
    tk = max(t for t in range(128, min(K, tk_cap) + 1, 128) if K % t == 0)
    assert M % tm == 0 and N % tn == 0 and K % tk == 0, (name, M, N, K)
    nk = K // tk
    ne, no = len(extras), len(out)
    dims = (((1,), (1 if tb else 0,)), ((), ()))
    flipped = [isinstance(o, tuple) for o in out]

    def kern(*refs):
        a_ref, b_ref = refs[:2]
        e_refs = refs[2:2 + ne]
        o_refs = refs[2 + ne:2 + ne + no]

        def finish(acc):
            outs = epi(acc, *[e[...] for e in e_refs]) if epi is not None else (acc,)
            for o_ref, o, flip in zip(o_refs, outs, flipped):
                o_ref[...] = (o.T if flip else o).astype(o_ref.dtype)

        part = lax.dot_general(a_ref[...], b_ref[...], dims, preferred_element_type=F32)
        if nk == 1:
            finish(part)
            return
        acc_ref = refs[-1]
        k = pl.program_id(2)

        @pl.when(k == 0)
        def _():
            acc_ref[...] = part

        @pl.when(k > 0)
        def _():
            acc_ref[...] += part

        @pl.when(k == nk - 1)
        def _():
            finish(acc_ref[...])

    b_spec = (pl.BlockSpec((tn, tk), lambda i, j, k: (j, k)) if tb
              else pl.BlockSpec((tk, tn), lambda i, j, k: (k, j)))
    tile = pl.BlockSpec((tm, tn), lambda i, j, k: (i, j))
    tile_t = pl.BlockSpec((tn, tm), lambda i, j, k: (j, i))
    res = pl.pallas_call(
        kern, name=name, grid=(M // tm, N // tn, nk),
        in_specs=[pl.BlockSpec((tm, tk), lambda i, j, k: (i, k)), b_spec] + [tile] * ne,
        out_specs=[tile_t if flip else tile for flip in flipped],
        out_shape=[jax.ShapeDtypeStruct((N, M), o[0]) if flip else jax.ShapeDtypeStruct((M, N), o)
                   for o, flip in zip(out, flipped)],
        scratch_shapes=[pltpu.VMEM((tm, tn), F32)] if nk > 1 else [],
        compiler_params=_cp(("parallel", "parallel", "arbitrary")),
    )(a, b, *extras)
    return res[0] if no == 1 else res


HALO = 8


def _rows(name, body, rows, params, out_rows, out_accs=(), tm=256):
    views = [r if isinstance(r, tuple) else (r, r.shape[1], 0) for r in rows]
    n = views[0][0].shape[0]
    assert n % tm == 0 and tm % HALO == 0
    nr, npar, nor, noa = len(views), len(params), len(out_rows), len(out_accs)
    flipped = [len(o) == 3 for o in out_rows]

    def row_spec(width, cb, halo=None):
        per, last = tm // HALO, n // HALO - 1
        if halo == "prev":
            return pl.BlockSpec((HALO, width), lambda i: (jnp.maximum(i * per - 1, 0), cb))
        if halo == "next":
            return pl.BlockSpec((HALO, width), lambda i: (jnp.minimum((i + 1) * per, last), cb))
        return pl.BlockSpec((tm, width), lambda i: (i, cb))

    def kern(*refs):
        r_refs = refs[:nr]
        p_refs = refs[nr:nr + npar]
        o_refs = refs[nr + npar:nr + npar + nor]
        a_refs = refs[nr + npar + nor:]
        outs, accs = body([r[...] for r in r_refs], [p[...] for p in p_refs])
        assert len(outs) == nor and len(accs) == noa, (name, len(outs), len(accs))
        for o_ref, o, flip in zip(o_refs, outs, flipped):
            o_ref[...] = (o.T if flip else o).astype(o_ref.dtype)
        if noa:
            @pl.when(pl.program_id(0) == 0)
            def _():
                for a_ref in a_refs:
                    a_ref[...] = jnp.zeros_like(a_ref)

            for a_ref, a in zip(a_refs, accs):
                a_ref[...] += a.astype(F32)

    def whole(shape):
        nd = len(shape)
        return pl.BlockSpec(tuple(shape), lambda i, nd=nd: (0,) * nd)

    in_specs = [row_spec(*v[1:]) for v in views]
    in_specs += [whole(p.shape) for p in params]
    out_specs = [pl.BlockSpec((o[0], tm), lambda i: (0, i)) if flip else pl.BlockSpec((tm, o[0]), lambda i: (i, 0))
                 for o, flip in zip(out_rows, flipped)]
    out_specs += [whole(s) for s in out_accs]
    out_shape = [jax.ShapeDtypeStruct((o[0], n) if flip else (n, o[0]), o[1]) for o, flip in zip(out_rows, flipped)]
    out_shape += [jax.ShapeDtypeStruct(tuple(s), F32) for s in out_accs]
    res = pl.pallas_call(
        kern, name=name, grid=(n // tm,), in_specs=in_specs, out_specs=out_specs,
        out_shape=out_shape, compiler_params=_cp(("arbitrary",)),
    )(*[v[0] for v in views], *params)
    return res[:nor], res[nor:]


def _shift_down(x, prev, k):
    head = jnp.where(pl.program_id(0) == 0, 0.0, pltpu.roll(prev, k, axis=0))
    row = lax.broadcasted_iota(jnp.int32, x.shape, 0)
    return jnp.where(row < k, jnp.tile(head, (x.shape[0] // HALO, 1)), pltpu.roll(x, k, axis=0))


def _shift_up(x, nxt, k):
    n = x.shape[0]
    tail = jnp.where(pl.program_id(0) == pl.num_programs(0) - 1, 0.0, pltpu.roll(nxt, HALO - k, axis=0))
    row = lax.broadcasted_iota(jnp.int32, x.shape, 0)
    return jnp.where(row >= n - k, jnp.tile(tail, (n // HALO, 1)), pltpu.roll(x, n - k, axis=0))


@jax.custom_vjp
def _headsum(x, e):
    return sum(jnp.dot(p, e, preferred_element_type=F32) for p in _split3(x))


_headsum.defvjp(lambda x, e: (_headsum(x, e), e), lambda e, ct: (_headsum(ct, e), None))


def _softplus(z):
    return jnp.maximum(z, 0.0) + jnp.log(1.0 + jnp.exp(jnp.minimum(z, -z)))


def _post_ln(x, y, g, lng, lnb):
    z = ALPHA * x + (1.0 + g) * y
    mu = jnp.mean(z, axis=-1, keepdims=True)
    zc = z - mu
    var = jnp.mean(zc * zc, axis=-1, keepdims=True)
    return zc * lax.rsqrt(var + LN_EPS) * lng + lnb


def _post_ln_mod(x, y, g, lng, lnb, scn, shn):
    xn = _post_ln(x, y, g, lng, lnb)
    return xn, xn * (1.0 + scn) + shn


def _pre_core(E, r_, k_, v_, wd_, ad_, gd_, r1, k1, v1, wd1, ad1, gd1, h, bg, cg, h1, cg1, h2, cg2,
              mu_r, mu_k, mu_v, mu_wd, mu_ad, mu_gd, w0, w_up, a0, a_up, g_up, k_k, k_a,
              cw0, cw1, cw2):
    def mix(x, x1, mu):
        return x + mu * (x1 - x)

    r, k, v = mix(r_, r1, mu_r), mix(k_, k1, mu_k), mix(v_, v1, mu_v)
    wd, ad, gd = mix(wd_, wd1, mu_wd), mix(ad_, ad1, mu_ad), mix(gd_, gd1, mu_gd)
    logw = -_softplus(-(w0 + jnp.dot(jnp.tanh(wd), w_up, preferred_element_type=F32))) - 0.5
    decay = jnp.exp(-jnp.exp(logw))
    iclr = jax.nn.sigmoid(a0 + jnp.dot(ad, a_up, preferred_element_type=F32))
    gate = jnp.dot(jax.nn.sigmoid(gd), g_up, preferred_element_type=F32)
    kk0 = k * k_k
    nrm = jnp.sqrt(_headsum(kk0 * kk0, E))
    kk = kk0 / jnp.maximum(nrm, 1e-12)
    kh = k * (1.0 + (iclr - 1.0) * k_a)
    yb = bg * (cw2 * (cg * h) + cw1 * (cg1 * h1) + cw0 * (cg2 * h2))
    return r, decay, kh, v, -kk, kk * iclr, gate, yb


def _post_core(E, y, r, kh, v, gate, lnx_g, lnx_b, rk):
    def seg(t):
        return _headsum(t, E)

    mean = seg(y) * (1.0 / HD)
    yc = y - mean
    var = seg(yc * yc) * (1.0 / HD)
    gn = yc * lax.rsqrt(var + GN_EPS) * lnx_g + lnx_b
    bonus = seg(r * kh * rk) * v
    return (gn + bonus) * gate


def _merge_core(o0, o1, o2, l0, l1, l2):
    m = jnp.maximum(jnp.maximum(l0, l1), l2)
    e0, e1, e2 = jnp.exp(l0 - m), jnp.exp(l1 - m), jnp.exp(l2 - m)
    den = e0 + e1 + e2
    return (e0 * o0 + e1 * o1 + e2 * o2) / den


CHUNK = 128
HALF = 64
HP = HEADS // 2
LW = 2 * HD
NCHUNK = T // CHUNK


def _split3(x):
    hi = x.astype(BF16)
    r1 = x - hi.astype(F32)
    mid = r1.astype(BF16)
    return hi, mid, (r1 - mid.astype(F32)).astype(BF16)


def _cols3(x, name):
    def kern(x_ref, o_ref):
        xt = x_ref[...].T
        left = lax.broadcasted_iota(jnp.int32, (HD, CHUNK), 1) < HALF
        for p in range(HP):
            a, b = xt[p * LW:p * LW + HD], xt[p * LW + HD:(p + 1) * LW]
            halves = [jnp.where(left, a, pltpu.roll(b, HALF, axis=1)), jnp.where(left, pltpu.roll(a, HALF, axis=1), b)]
            for h, tile in enumerate(halves):
                for j, part in enumerate(_split3(tile)):
                    o_ref[p, :, (3 * h + j) * LW:(3 * h + j + 1) * LW] = part

    return pl.pallas_call(
        kern, name=name, grid=(NCHUNK,),
        in_specs=[pl.BlockSpec((CHUNK, RW), lambda c: (c, 0))],
        out_specs=pl.BlockSpec((HP, HD, 6 * CHUNK), lambda c: (0, 0, c)),
        out_shape=jax.ShapeDtypeStruct((HP, HD, 6 * T), BF16),
        compiler_params=_cp(("parallel",)),
    )(x)


def _pick_codes():
    row = lax.broadcasted_iota(jnp.int32, (6 * HALF, LW), 0)
    col = lax.broadcasted_iota(jnp.int32, (6 * HALF, LW), 1)
    same = ((row & (LW - 1)) >= HALF) == (col >= HD)
    return jnp.where(same, row & (HALF - 1), -1).astype(BF16)


def _column(block_ref, codes, half, i):
    pick = jnp.where(codes == i.astype(BF16), jnp.ones((), BF16), jnp.zeros((), BF16))
    block = block_ref[:, :, half * 6 * HALF:(half + 1) * 6 * HALF].reshape(HP * HD, 6 * HALF)
    return jnp.dot(block, pick, preferred_element_type=F32)


def _halfsums(x, row, left1):
    row_l = jnp.where(left1, row, 0.0)
    return (jnp.sum(x * row_l, axis=1, keepdims=True), jnp.sum(x * (row - row_l), axis=1, keepdims=True))


def _pair_rows(row):
    return [row[:, p * LW:(p + 1) * LW] for p in range(HP)]


def _store_columns(ref, p, t_mask, cols):
    for j, col in enumerate(cols):
        pltpu.store(ref.at[pl.ds(2 * p + j, 1)], jnp.broadcast_to(col[None], (1, HD, CHUNK)), mask=t_mask[None])


def _columns_to_rows(cols_ref, rows_ref):
    for p in range(HP):
        rows_ref[:, p * LW:(p + 1) * LW] = cols_ref[2 * p:2 * p + 2].reshape(LW, CHUNK).T


NHALF = T // HALF
HALVES = CHUNK // HALF


def _scan_fwd(r, w, k, a, b, v3):
    def kern(r_ref, w_ref, k_ref, a_ref, b_ref, v_ref, y_ref, ck_ref, st_hbm, sa_hbm,
             s_ref, vb_ref, yc_ref, st_ref, sa_ref, sems):
        c = pl.program_id(0)

        @pl.when(c == 0)
        def _():
            s_ref[...] = jnp.zeros_like(s_ref)

        lane = lax.broadcasted_iota(jnp.int32, (HD, CHUNK), 1)
        left = lane < HD
        left1 = lax.broadcasted_iota(jnp.int32, (1, LW), 1) < HD
        codes = _pick_codes()

        def flush(slot, half_index):
            return [pltpu.make_async_copy(src.at[slot], dst.at[half_index], sems.at[j, slot])
                    for j, (src, dst) in enumerate(((st_ref, st_hbm), (sa_ref, sa_hbm)))]

        for half in range(HALVES):
            ck_ref[half] = s_ref[...]
            vb_ref[...] = _column(v_ref, codes, half, jnp.int32(0))

            @pl.when(c > 0)
            def _():
                for cp in flush(half, (c - 1) * HALVES + half):
                    cp.wait()

            def step(i, carry):
                t = half * HALF + i
                row = lambda ref: _pair_rows(ref[pl.ds(t, 1), :])
                S = [s_ref[p] for p in range(HP)]
                sa = [jnp.where(left, *_halfsums(s, a, left1)) for s, a in zip(S, row(a_ref))]
                S = [s * w + c_ * b + vb_ref[pl.ds(p * HD, HD), :] * k
                     for p, (s, w, c_, b, k) in enumerate(zip(S, row(w_ref), sa, row(b_ref), row(k_ref)))]
                for p, (s, c_) in enumerate(zip(S, sa)):
                    s_ref[p] = s
                    st_ref[half, i, p] = s
                    sa_ref[half, i, p] = c_
                for p, (s, r) in enumerate(zip(S, row(r_ref))):
                    _store_columns(yc_ref, p, lane == t, _halfsums(s, r, left1))
                vb_ref[...] = _column(v_ref, codes, half, i + 1)
                return carry

            lax.fori_loop(0, HALF, step, 0, unroll=16)
            for cp in flush(half, c * HALVES + half):
                cp.start()
        _columns_to_rows(yc_ref, y_ref)

        @pl.when(c == NCHUNK - 1)
        def _():
            for half in range(HALVES):
                for cp in flush(half, c * HALVES + half):
                    cp.wait()

    rowblk = pl.BlockSpec((CHUNK, RW), lambda c: (c, 0))
    saved = jax.ShapeDtypeStruct((NHALF, HALF, HP, HD, LW), F32)
    stage = pltpu.VMEM((HALVES, HALF, HP, HD, LW), F32)
    return pl.pallas_call(
        kern, name="rwkv_scan_fwd", grid=(NCHUNK,),
        in_specs=[rowblk] * 5 + [pl.BlockSpec((HP, HD, 6 * CHUNK), lambda c: (0, 0, c))],
        out_specs=[rowblk, pl.BlockSpec((HALVES, HP, HD, LW), lambda c: (c, 0, 0, 0)),
                   pl.BlockSpec(memory_space=pl.ANY), pl.BlockSpec(memory_space=pl.ANY)],
        out_shape=[jax.ShapeDtypeStruct((T, RW), F32), jax.ShapeDtypeStruct((NHALF, HP, HD, LW), F32), saved, saved],
        scratch_shapes=[pltpu.VMEM((HP, HD, LW), F32), pltpu.VMEM((HP * HD, LW), F32),
                        pltpu.VMEM((HEADS, HD, CHUNK), F32), stage, stage, pltpu.SemaphoreType.DMA((2, HALVES))],
        compiler_params=_cp(("arbitrary",)),
    )(r, w, k, a, b, v3)


def _scan_bwd(r, w, k, a, b, v3, dy3, ck, st, sa):
    def kern(r_ref, w_ref, k_ref, a_ref, b_ref, v_ref, dy_ref, ck_ref, st_hbm, sa_hbm,
             dr_ref, dw_ref, dk_ref, da_ref, db_ref, dv_ref, ds_ref, sb_ref, sa_ref, pick_ref, dvc_ref, sems):
        c = pl.program_id(0)
        chunk = NCHUNK - 1 - c

        @pl.when(c == 0)
        def _():
            ds_ref[...] = jnp.zeros_like(ds_ref)

        lane = lax.broadcasted_iota(jnp.int32, (HD, CHUNK), 1)
        left = lane < HD
        left1 = lax.broadcasted_iota(jnp.int32, (1, LW), 1) < HD
        codes = _pick_codes()

        def rowsum(x):
            return jnp.sum(x, axis=0, keepdims=True)

        def fetch(slot, half_index):
            return [pltpu.make_async_copy(st_hbm.at[half_index], sb_ref.at[slot, pl.ds(1, HALF)], sems.at[0, slot]),
                    pltpu.make_async_copy(sa_hbm.at[half_index], sa_ref.at[slot], sems.at[1, slot])]

        def picks(half, i):
            pick_ref[pl.ds(0, HP * HD), :] = _column(v_ref, codes, half, i)
            pick_ref[pl.ds(HP * HD, HP * HD), :] = _column(dy_ref, codes, half, i)

        @pl.when(c == 0)
        def _():
            for cp in fetch(HALVES - 1, chunk * HALVES + HALVES - 1):
                cp.start()

        for half in reversed(range(HALVES)):
            base = half * HALF
            for cp in fetch(half, chunk * HALVES + half):
                cp.wait()
            if half:
                for cp in fetch(half - 1, chunk * HALVES + half - 1):
                    cp.start()
            else:
                @pl.when(chunk > 0)
                def _():
                    for cp in fetch(HALVES - 1, chunk * HALVES - 1):
                        cp.start()
            sb_ref[half, 0] = ck_ref[half]
            picks(half, jnp.int32(HALF - 1))

            def back(ii, carry):
                i = HALF - 1 - ii
                t = base + i
                row = lambda ref: _pair_rows(ref[pl.ds(t, 1), :])
                a_r, b_r, k_r, w_r, r_r = row(a_ref), row(b_ref), row(k_ref), row(w_ref), row(r_ref)
                vs = [pick_ref[pl.ds(p * HD, HD), :] for p in range(HP)]
                dys = [pick_ref[pl.ds((HP + p) * HD, HD), :] for p in range(HP)]
                picks(half, jnp.maximum(i - 1, 0))
                dr, dw, db, dk, da = [], [], [], [], []
                for p in range(HP):
                    Sp, dy = sb_ref[half, i, p], dys[p]
                    dS = ds_ref[p] + dy * r_r[p]
                    dr.append(rowsum(sb_ref[half, i + 1, p] * dy))
                    dw.append(rowsum(dS * Sp))
                    db.append(rowsum(dS * sa_ref[half, i, p]))
                    dk.append(rowsum(dS * vs[p]))
                    dsa = jnp.where(left, *_halfsums(dS, b_r[p], left1))
                    _store_columns(dvc_ref, p, lane == t, _halfsums(dS, k_r[p], left1))
                    da.append(rowsum(Sp * dsa))
                    ds_ref[p] = dS * w_r[p] + dsa * a_r[p]
                for ref, pieces in ((dr_ref, dr), (dw_ref, dw), (db_ref, db), (dk_ref, dk), (da_ref, da)):
                    ref[pl.ds(t, 1), :] = jnp.concatenate(pieces, axis=1)
                return carry

            lax.fori_loop(0, HALF, back, 0, unroll=8)
        _columns_to_rows(dvc_ref, dv_ref)

    rowblk = pl.BlockSpec((CHUNK, RW), lambda c: (NCHUNK - 1 - c, 0))
    col3blk = pl.BlockSpec((HP, HD, 6 * CHUNK), lambda c: (0, 0, NCHUNK - 1 - c))
    rowshape = jax.ShapeDtypeStruct((T, RW), F32)
    return pl.pallas_call(
        kern, name="rwkv_scan_bwd", grid=(NCHUNK,),
        in_specs=[rowblk] * 5 + [col3blk, col3blk,
                                 pl.BlockSpec((HALVES, HP, HD, LW), lambda c: (NCHUNK - 1 - c, 0, 0, 0)),
                                 pl.BlockSpec(memory_space=pl.ANY), pl.BlockSpec(memory_space=pl.ANY)],
        out_specs=[rowblk] * 6, out_shape=[rowshape] * 6,
        scratch_shapes=[pltpu.VMEM((HP, HD, LW), F32), pltpu.VMEM((HALVES, HALF + 1, HP, HD, LW), F32),
                        pltpu.VMEM((HALVES, HALF, HP, HD, LW), F32), pltpu.VMEM((2 * HP * HD, LW), F32),
                        pltpu.VMEM((HEADS, HD, CHUNK), F32), pltpu.SemaphoreType.DMA((2, HALVES))],
        compiler_params=_cp(("arbitrary",)),
    )(r, w, k, a, b, v3, dy3, ck, st, sa)


NT = (((1,), (1,)), ((), ()))
TN = (((0,), (0,)), ((), ()))
SCALE = HD ** -0.5
QKV_G = 3 * RW


def _attn_setup(g):
    dil = DILS[g]
    qkv = [pl.BlockSpec((T, LW), lambda hp, c=(g * QKV_G + s * RW) // LW: (0, c + hp)) for s in range(3)]
    tile = pl.BlockSpec((T, LW), lambda hp: (0, hp))
    bias = pl.BlockSpec((2, BLK, 2 * BLK), lambda hp: (hp, 0, 0))

    def blocks():
        for r in range(dil):
            for n in range(T // dil // BLK):
                rows = pl.ds(n * BLK * dil + r, BLK, stride=dil)
                keys = pl.ds((n - 1) * BLK * dil + r, 2 * BLK, stride=dil) if n else rows
                yield n, rows, keys

    return qkv, tile, bias, blocks


def _band(n):
    qi = lax.broadcasted_iota(jnp.int32, (BLK, 2 * BLK), 0)
    ki = lax.broadcasted_iota(jnp.int32, (BLK, 2 * BLK), 1)
    band = (ki >= qi) & (ki <= qi + BLK)
    return band if n else band[:, BLK:]


def _head_masks():
    lane = lax.broadcasted_iota(jnp.int32, (BLK, LW), 1)
    return lane < HD, [(lane < HD).astype(BF16), (lane >= HD).astype(BF16)]


def _attn_fwd(pq, bias, g):
    qkv, tile, bias_spec, blocks = _attn_setup(g)

    def kern(q_ref, k_ref, v_ref, b_ref, o_ref, l_ref):
        left, masks = _head_masks()
        for n, rows, keys in blocks():
            qb, kc, vc = q_ref[rows, :].astype(BF16), k_ref[keys, :].astype(BF16), v_ref[keys, :].astype(BF16)
            valid = _band(n)
            o, lse = [], []
            for j in range(2):
                bias_j = b_ref[j] if n else b_ref[j][:, BLK:]
                s = lax.dot_general(qb * masks[j], kc, NT, preferred_element_type=F32) * SCALE + bias_j
                s = jnp.where(valid, s, -jnp.inf)
                m = jnp.max(s, axis=1, keepdims=True)
                e = jnp.exp(s - m)
                den = jnp.sum(e, axis=1, keepdims=True)
                o.append(jnp.dot((e / den).astype(BF16), vc, preferred_element_type=F32))
                lse.append(m + jnp.log(den))
            o_ref[rows, :] = jnp.where(left, o[0], o[1])
            l_ref[rows, :] = jnp.where(left, lse[0], lse[1])

    shape = jax.ShapeDtypeStruct((T, RW), F32)
    return pl.pallas_call(
        kern, name=f"attn_fwd_{g}", grid=(HP,),
        in_specs=qkv + [bias_spec], out_specs=[tile, tile], out_shape=[shape, shape],
        compiler_params=_cp(("parallel",)),
    )(pq, pq, pq, bias)


def _attn_bwd(pq, bias, do, o, lse, dlse, g):
    qkv, tile, bias_spec, blocks = _attn_setup(g)

    def kern(q_ref, k_ref, v_ref, b_ref, do_ref, o_ref, l_ref, dl_ref, dq_ref, dk_ref, dv_ref, db_ref):
        left, masks = _head_masks()
        lane = lax.broadcasted_iota(jnp.int32, (BLK, LW), 1)
        dk_ref[...] = jnp.zeros_like(dk_ref)
        dv_ref[...] = jnp.zeros_like(dv_ref)
        db_ref[...] = jnp.zeros_like(db_ref)

        def column(tile_, j):
            return jnp.sum(jnp.where(lane == j * HD, tile_, 0.0), axis=1, keepdims=True)

        for n, rows, keys in blocks():
            qb, kc, vc = q_ref[rows, :].astype(BF16), k_ref[keys, :].astype(BF16), v_ref[keys, :].astype(BF16)
            dof, valid = do_ref[rows, :], _band(n)
            dob, prod = dof.astype(BF16), dof * o_ref[rows, :]
            dq = []
            for j in range(2):
                bias_j = b_ref[j] if n else b_ref[j][:, BLK:]
                delta = jnp.sum(prod * masks[j].astype(F32), axis=1, keepdims=True)
                qm, dom = qb * masks[j], dob * masks[j]
                s = lax.dot_general(qm, kc, NT, preferred_element_type=F32) * SCALE + bias_j
                p = jnp.where(valid, jnp.exp(s - column(l_ref[rows, :], j)), 0.0)
                dp = lax.dot_general(dom, vc, NT, preferred_element_type=F32)
                ds = p * (dp + (column(dl_ref[rows, :], j) - delta))
                if n:
                    db_ref[j] += ds
                else:
                    db_ref[j, :, BLK:] += ds
                dsb = (ds * SCALE).astype(BF16)
                dq.append(jnp.dot(dsb, kc, preferred_element_type=F32))
                dk_ref[keys, :] += lax.dot_general(dsb, qm, TN, preferred_element_type=F32)
                dv_ref[keys, :] += lax.dot_general(p.astype(BF16), dom, TN, preferred_element_type=F32)
            dq_ref[rows, :] = jnp.where(left, dq[0], dq[1])

    shape = jax.ShapeDtypeStruct((T, RW), F32)
    return pl.pallas_call(
        kern, name=f"attn_bwd_{g}", grid=(HP,),
        in_specs=qkv + [bias_spec] + [tile] * 4, out_specs=[tile] * 3 + [bias_spec],
        out_shape=[shape] * 3 + [jax.ShapeDtypeStruct((HEADS, BLK, 2 * BLK), F32)],
        compiler_params=_cp(("parallel",)),
    )(pq, pq, pq, bias, do, o, lse, dlse)


NBUCKET = 32
NPAIR = BLK * 2 * BLK


def _relbias_table(rbT, onehotT):
    def kern(rb_ref, oh_ref, out_ref):
        out_ref[0] = sum(jnp.dot(p, oh_ref[0], preferred_element_type=F32) for p in _split3(rb_ref[0]))

    return pl.pallas_call(
        kern, name="relbias_table", grid=(3,),
        in_specs=[pl.BlockSpec((1, HEADS, NBUCKET), lambda g: (g, 0, 0)),
                  pl.BlockSpec((1, NBUCKET, NPAIR), lambda g: (g, 0, 0))],
        out_specs=pl.BlockSpec((1, HEADS, NPAIR), lambda g: (g, 0, 0)),
        out_shape=jax.ShapeDtypeStruct((3, HEADS, NPAIR), F32),
        compiler_params=_cp(("parallel",)),
    )(rbT, onehotT)


def _relbias_grad(db, onehotT):
    nt = (((1,), (1,)), ((), ()))

    def kern(db_ref, oh_ref, out_ref):
        hi, mid, _ = _split3(db_ref[0])
        out_ref[0] = (lax.dot_general(hi, oh_ref[0], nt, preferred_element_type=F32)
                      + lax.dot_general(mid, oh_ref[0], nt, preferred_element_type=F32))

    return pl.pallas_call(
        kern, name="relbias_grad", grid=(3,),
        in_specs=[pl.BlockSpec((1, HEADS, NPAIR), lambda g: (g, 0, 0)),
                  pl.BlockSpec((1, NBUCKET, NPAIR), lambda g: (g, 0, 0))],
        out_specs=pl.BlockSpec((1, HEADS, NBUCKET), lambda g: (g, 0, 0)),
        out_shape=jax.ShapeDtypeStruct((3, HEADS, NBUCKET), F32),
        compiler_params=_cp(("parallel",)),
    )(db, onehotT)


def _adamw(w, g, m, v):
    m2 = ADAM_B1 * m + (1.0 - ADAM_B1) * g
    v2 = ADAM_B2 * v + (1.0 - ADAM_B2) * (g * g)
    m_hat = m2 / (1.0 - ADAM_B1 ** ADAM_STEP)
    v_hat = v2 / (1.0 - ADAM_B2 ** ADAM_STEP)
    return -ADAM_LR * (m_hat / (jnp.sqrt(v_hat) + ADAM_EPS) + ADAM_WD * w), m2, v2


def _ada_mod(c_all, ada_w, ada_b_loc):
    def kern(c_ref, w_ref, b_ref, o_ref):
        c = c_ref[...]
        cond = c * jax.nn.sigmoid(c)
        o_ref[0] = jnp.dot(cond, w_ref[0], precision=HI, preferred_element_type=F32) + b_ref[0]

    ncol = ada_w.shape[2]
    return pl.pallas_call(
        kern, name="ada_mod", grid=(2,),
        in_specs=[pl.BlockSpec((NDEV, D), lambda i: (0, 0)),
                  pl.BlockSpec((1, D, ncol), lambda i: (i, 0, 0)),
                  pl.BlockSpec((1, 1, ncol), lambda i: (i, 0, 0))],
        out_specs=pl.BlockSpec((1, NDEV, ncol), lambda i: (i, 0, 0)),
        out_shape=jax.ShapeDtypeStruct((2, NDEV, ncol), F32),
        compiler_params=_cp(("parallel",)),
    )(c_all, ada_w, ada_b_loc.reshape(2, 1, ncol))


def _ada_grad_adamw(cT_all, dmod_loc, w, m, v):
    ncol = w.shape[2]
    tr = 256

    def kern(c_ref, d_ref, w_ref, m_ref, v_ref, g_ref, dl_ref, m2_ref, v2_ref):
        c = c_ref[...]
        cond = c * jax.nn.sigmoid(c)
        g = jnp.dot(cond, d_ref[0], precision=HI, preferred_element_type=F32)
        dl, m2, v2 = _adamw(w_ref[0], g, m_ref[0], v_ref[0])
        g_ref[0], dl_ref[0], m2_ref[0], v2_ref[0] = g, dl, m2, v2

    big = pl.BlockSpec((1, tr, ncol), lambda i, j: (i, j, 0))
    shp = jax.ShapeDtypeStruct(w.shape, F32)
    return pl.pallas_call(
        kern, name="ada_grad_adamw", grid=(2, D // tr),
        in_specs=[pl.BlockSpec((tr, NDEV), lambda i, j: (j, 0)),
                  pl.BlockSpec((1, NDEV, ncol), lambda i, j: (i, 0, 0)), big, big, big],
        out_specs=[big] * 4, out_shape=[shp] * 4,
        compiler_params=_cp(("parallel", "parallel")),
    )(cT_all, dmod_loc, w, m, v)


def _sum_adamw(recv, w, m, v, name, tr):
    S = recv.shape[0]
    R, C = w.shape
    assert R % tr == 0 and recv.shape[1:] == (R, C)

    def kern(r_ref, w_ref, m_ref, v_ref, g_ref, dl_ref, m2_ref, v2_ref):
        g = r_ref[0].astype(F32)
        for s in range(1, S):
            g = g + r_ref[s].astype(F32)
        dl, m2, v2 = _adamw(w_ref[...], g, m_ref[...], v_ref[...])
        g_ref[...], dl_ref[...], m2_ref[...], v2_ref[...] = g, dl, m2, v2

    flat = pl.BlockSpec((tr, C), lambda i: (i, 0))
    shp = jax.ShapeDtypeStruct((R, C), F32)
    return pl.pallas_call(
        kern, name=name, grid=(R // tr,),
        in_specs=[pl.BlockSpec((S, tr, C), lambda i: (0, i, 0)), flat, flat, flat],
        out_specs=[flat] * 4, out_shape=[shp] * 4,
        compiler_params=_cp(("parallel",)),
    )(recv, w, m, v)


def _pack(arrs, dtype, row_mult):
    flat = jnp.concatenate([a.reshape(-1).astype(dtype) for a in arrs])
    flat = jnp.pad(flat, (0, -flat.shape[0] % (128 * row_mult)))
    return flat.reshape(-1, 128)


def _pack8(arrs, dtype, row_mult):
    flat = jnp.concatenate([a.reshape(NDEV, -1).astype(dtype) for a in arrs], axis=1)
    flat = jnp.pad(flat, ((0, 0), (0, -flat.shape[1] % (128 * row_mult))))
    return flat.reshape(NDEV, -1, 128)


def _unpack(buf, shapes, lead=()):
    flat = buf.reshape(lead + (-1,))
    out, off = [], 0
    for s in shapes:
        n = math.prod(s)
        out.append(flat[..., off:off + n].reshape(lead + tuple(s)))
        off += n
    return out


def _to_chunks(full, kind):
    if kind == "col":
        x = full.reshape(full.shape[:-1] + (NDEV, full.shape[-1] // NDEV))
        return jnp.moveaxis(x, -2, 0)
    x = full.reshape(full.shape[:-2] + (NDEV, full.shape[-2] // NDEV, full.shape[-1]))
    return jnp.moveaxis(x, -3, 0)


def _from_chunks(g8, kind):
    if kind == "col":
        x = jnp.moveaxis(g8, 0, -2)
        return x.reshape(x.shape[:-2] + (x.shape[-2] * x.shape[-1],))
    x = jnp.moveaxis(g8, 0, -3)
    return x.reshape(x.shape[:-3] + (x.shape[-3] * x.shape[-2], x.shape[-1]))


def _pad_pa(x):
    z = lambda n: jnp.zeros(x.shape[:-1] + (n,), x.dtype)
    return jnp.concatenate([x[..., :1600], z(64), x[..., 1600:1664], z(64), x[..., 1664:1824], z(96)], -1)


def _unpad_pa(x):
    return jnp.concatenate([x[..., :1600], x[..., 1664:1728], x[..., 1792:1952]], -1)


AB_SEGMENTS = ((0, 1600, 0), (1600, 1664, 64), (1664, 1824, 128), (1824, 3360, PAB - 3360))
AB_SHARD = 3360 // NDEV


def _ab_in_padded(g8):
    blocks, at = [], 0
    for start, end, shift in AB_SEGMENTS:
        if start + shift > at:
            blocks.append(jnp.zeros((g8.shape[1], start + shift - at), g8.dtype))
        for j in range(start // AB_SHARD, (end - 1) // AB_SHARD + 1):
            lo, hi = max(start, j * AB_SHARD), min(end, (j + 1) * AB_SHARD)
            blocks.append(g8[j, :, lo - j * AB_SHARD:hi - j * AB_SHARD])
        at = end + shift
    return jnp.concatenate(blocks, axis=1)


def _ab_in_shards(padded):
    shards = []
    for j in range(NDEV):
        pieces = [padded[:, max(start, j * AB_SHARD) + shift:min(end, (j + 1) * AB_SHARD) + shift]
                  for start, end, shift in AB_SEGMENTS if max(start, j * AB_SHARD) < min(end, (j + 1) * AB_SHARD)]
        shards.append(jnp.concatenate(pieces, axis=1))
    return jnp.stack(shards)


def _pad_rows(x, n):
    return jnp.pad(x, ((0, n - x.shape[0]), (0, 0)))


def _bucket_tables():
    qi = jnp.arange(BLK)[:, None]
    ki = jnp.arange(2 * BLK)[None, :]
    rel = BLK + qi - ki
    tabs = []
    for dil in DILS:
        dist = jnp.clip(rel, 0, BLK) * dil
        logd = jnp.log(jnp.maximum(dist, 1).astype(F32) / 16) / math.log(2048 / 16)
        large = jnp.minimum(16 + (logd * 16).astype(jnp.int32), 31)
        tabs.append(jnp.where(dist < 16, dist, large))
    return jnp.stack(tabs)


SHARDED = (("ln_g", "col"), ("ln_b", "col"), ("ab_w_in", "col"), ("rw_w_up", "col"), ("rw_a_up", "col"),
           ("rw_g_up", "col"), ("sc_conv_w", "col"), ("ab_w_out", "row"), ("dil_w_qkv", "col"),
           ("dil_w_out", "col"), ("mlp_w1", "col"), ("mlp_w2", "row"))
FIRST = ("ab_w_in",)
LATER = ("ab_w_out", "dil_w_qkv", "dil_w_out", "mlp_w1", "mlp_w2")
GATHER_BF16 = FIRST + LATER
GATHER_F32 = ("rw_w_up", "rw_a_up", "rw_g_up", "sc_conv_w", "ln_g", "ln_b")
REPLICATED = ("ada_b", "rw_mu", "rw_w0", "rw_a0", "rw_k_k", "rw_k_a", "rw_r_k", "rw_lnx_g", "rw_lnx_b", "rel_bias")
WEIGHTS = ("ada_w", "ada_b", "ln_g", "ln_b", "ab_w_in", "rw_mu", "rw_w0", "rw_w_up", "rw_a0", "rw_a_up",
           "rw_g_up", "rw_k_k", "rw_k_a", "rw_r_k", "rw_lnx_g", "rw_lnx_b", "sc_conv_w", "ab_w_out",
           "dil_w_qkv", "dil_w_out", "rel_bias", "mlp_w1", "mlp_w2")


def _local_step(x0, tgt, mod, W, P, later_weights, early_grads):
    row = lambda a: a.reshape(1, -1)
    W = dict(W)
    m6 = mod.reshape(2, 6, 1, D)
    sc = [m6[0, 1], m6[0, 4], m6[1, 1], m6[1, 4]]
    sh = [m6[0, 0], m6[0, 3], m6[1, 0], m6[1, 3]]
    gt = [m6[0, 2], m6[0, 5], m6[1, 2], m6[1, 5]]
    lng = [row(P["ln_g"][0, 0]), row(P["ln_g"][0, 1]), row(P["ln_g"][1, 0]), row(P["ln_g"][1, 1])]
    lnb = [row(P["ln_b"][0, 0]), row(P["ln_b"][0, 1]), row(P["ln_b"][1, 0]), row(P["ln_b"][1, 1])]
    E = jnp.kron(jnp.eye(HEADS, dtype=BF16), jnp.ones((HD, HD), BF16))

    def mod_body(r, p):
        u = r[0] * (1.0 + p[0]) + p[1]
        return [u, u], []

    (u0, u0T), _ = _rows("modulate", mod_body, [x0], [sc[0], sh[0]], [(D, BF16), (D, BF16, "T")])

    def post_fwd_body(r, p):
        xn, un = _post_ln_mod(r[0], r[1], *p)
        return [xn, un, un], []

    def post_fwd(s, x, y):
        (xn, un, unT), _ = _rows(f"post_ln_{s}", post_fwd_body, [x, y],
                                 [gt[s], lng[s], lnb[s], sc[s + 1], sh[s + 1]],
                                 [(D, F32), (D, BF16), (D, BF16, "T")])
        return xn, un, unT

    def relu2(acc):
        a = jnp.maximum(acc, 0.0)
        return acc, a * a, a * a

    def relu2_bwd(acc, h):
        return (acc * (2.0 * jnp.maximum(h, 0.0)),)

    p = _mm("ab_in", u0, W["ab_w_in"])
    mu = _pad_pa(P["rw_mu"])
    mu_parts = [mu[:, :512], mu[:, 512:1024], mu[:, 1024:1536], mu[:, 1536:1664], mu[:, 1664:1792], mu[:, 1792:]]
    pre_params = mu_parts + [P["rw_w0"], _pad_rows(P["rw_w_up"], 128), P["rw_a0"], _pad_rows(P["rw_a_up"], 128),
                             _pad_rows(P["rw_g_up"], 256), P["rw_k_k"], P["rw_k_a"],
                             P["sc_conv_w"][0:1], P["sc_conv_w"][1:2], P["sc_conv_w"][2:3]]
    pieces = [(p, 512, 0), (p, 512, 1), (p, 512, 2), (p, 128, 12), (p, 128, 13), (p, 256, 7),
              (p, 512, 4), (p, 512, 5), (p, 512, 6)]
    shifted = [0, 1, 2, 3, 4, 5, 6, 8]
    pre_rows = pieces + [pieces[i] + ("prev",) for i in shifted]
    NPR = 19

    def pre_args(r):
        x, prev = r[:9], dict(zip(shifted, r[9:17]))
        down = lambda i, k: _shift_down(x[i], prev[i], k)
        return x[:6] + [down(i, 1) for i in range(6)] + x[6:9] + [down(6, 1), down(8, 1), down(6, 2), down(8, 2)]

    def pre_fwd_body(r, pp):
        return list(_pre_core(pp[0], *pre_args(r), *pp[1:])), []

    (r_, w_, kh_, v_, a_, b_, gate_, yb), _ = _rows(
        "rwkv_pre", pre_fwd_body, pre_rows, [E] + pre_params, [(RW, F32)] * 7 + [(RW, BF16)], tm=256)
    scan_in = [r_, w_, kh_, a_, b_, _cols3(v_, "rwkv_v_columns")]
    ysc, *saved = _scan_fwd(*scan_in)
    post_params = [P["rw_lnx_g"], P["rw_lnx_b"], P["rw_r_k"].reshape(1, RW)]

    def postmix_fwd_body(r, pp):
        return [_post_core(pp[0], *r, *pp[1:])], []

    (ya,), _ = _rows("rwkv_post", postmix_fwd_body, [ysc, r_, kh_, v_, gate_], [E] + post_params,
                     [(RW, BF16)], tm=256)
    cat = jnp.concatenate([ya, yb], axis=1)
    W.update(later_weights(cat))
    y0 = _mm("ab_out", cat, W["ab_w_out"])
    x1, u1, u1T = post_fwd(0, x0, y0)

    h1, a1, a1T = _mm("mlp1_up_0", u1, W["mlp_w1"][0], out=(F32, BF16, (BF16, "T")), epi=relu2)
    y1 = _mm("mlp1_down_0", a1, W["mlp_w2"][0])
    x2, u2, u2T = post_fwd(1, x1, y1)

    pq = _mm("qkv", u2, W["dil_w_qkv"])
    onehotT = (_bucket_tables().reshape(3, 1, NPAIR) == jnp.arange(NBUCKET).reshape(1, NBUCKET, 1)).astype(BF16)
    rbT = P["rel_bias"].reshape(NBUCKET, 3, HEADS).transpose(1, 2, 0)
    bias = _relbias_table(rbT, onehotT).reshape(3, HEADS, BLK, 2 * BLK)
    og, lse = zip(*[_attn_fwd(pq, bias[g], g) for g in range(3)])

    def merge_fwd_body(r, pp):
        return [_merge_core(*r)], []

    (om,), _ = _rows("attn_merge", merge_fwd_body, list(og + lse), [], [(RW, BF16)])
    y2 = _mm("dil_out", om, W["dil_w_out"])
    x3, u3, u3T = post_fwd(2, x2, y2)

    h3, a3, a3T = _mm("mlp1_up_1", u3, W["mlp_w1"][1], out=(F32, BF16, (BF16, "T")), epi=relu2)
    y3 = _mm("mlp1_down_1", a3, W["mlp_w2"][1])

    def last_body(r, pp):
        x, y, tg = r
        xn, vjp = jax.vjp(_post_ln, x, y, *pp)
        err = xn - tg
        dx, dy, dg, dlg, dlb = vjp(err * (1.0 / D))
        loss = jnp.full((1, 128), (0.5 / D) * jnp.sum(err * err), F32)
        return [dx, dy], [loss, dg, dlg, dlb]

    (dxp, dy3), (loss_acc, dg3, dlng3, dlnb3) = _rows(
        "final_ln_loss", last_body, [x3, y3, tgt], [gt[3], lng[3], lnb[3]],
        [(D, F32), (D, BF16)], [(1, 128), (1, D), (1, D), (1, D)])

    G = {}
    dsc, dsh, dgt = [None] * 4, [None] * 4, [None] * 4
    dlng, dlnb = [None] * 4, [None] * 4
    dgt[3], dlng[3], dlnb[3] = dg3, dlng3, dlnb3

    def mlp_bwd(i, uT, h, aT, dy):
        dh = _mm(f"mlp_dh_{i}", dy, W["mlp_w2"][i], tb=True, out=(BF16,), epi=relu2_bwd, extras=(h,))
        gw2 = _mm(f"mlp_dw2_{i}", aT, dy)
        du = _mm(f"mlp_du_{i}", dh, W["mlp_w1"][i], tb=True)
        gw1 = _mm(f"mlp_dw1_{i}", uT, dh)
        return du, gw1, gw2

    def post_bwd_body(r, pp):
        x, y, dxn, dun = r
        _, vjp = jax.vjp(_post_ln_mod, x, y, *pp)
        dx, dy, dg, dlg, dlb, dscn, dshn = vjp((dxn, dun))
        return [dx, dy], [dg, dlg, dlb, dscn, dshn]

    def post_bwd(s, x, y, dxn, dun):
        (dx, dy), (dgt[s], dlng[s], dlnb[s], dsc[s + 1], dsh[s + 1]) = _rows(
            f"post_ln_bwd_{s}", post_bwd_body, [x, y, dxn, dun],
            [gt[s], lng[s], lnb[s], sc[s + 1], sh[s + 1]], [(D, F32), (D, BF16)], [(1, D)] * 5)
        return dx, dy

    du3, gw1_1, gw2_1 = mlp_bwd(1, u3T, h3, a3T, dy3)
    dxp, dy2 = post_bwd(2, x2, y2, dxp, du3)

    G["dil_w_out"] = _mm("dil_out_dw", om.T, dy2)[None]
    do = _mm("dil_out_dx", dy2, W["dil_w_out"], tb=True)

    def merge_bwd_body(r, pp):
        _, vjp = jax.vjp(_merge_core, *r[:6])
        d = vjp(r[6])
        return list(d[:3]) + [_headsum(d[3 + g], pp[0]) for g in range(3)], []

    mb, _ = _rows("attn_merge_bwd", merge_bwd_body, list(og + lse) + [do], [E],
                  [(RW, F32)] * 6)
    back = [_attn_bwd(pq, bias[g], mb[g], og[g], lse[g], mb[3 + g], g) for g in range(3)]
    dpq = jnp.concatenate([t for dq, dk, dv, _ in back for t in (dq, dk, dv)], axis=1).astype(BF16)
    rb = _relbias_grad(jnp.stack([b[3] for b in back]).reshape(3, HEADS, NPAIR), onehotT)
    G["rel_bias"] = rb.transpose(2, 0, 1).reshape(NBUCKET, 3 * HEADS)
    G["dil_w_qkv"] = _mm("qkv_dw", u2T, dpq)[None]
    du2 = _mm("qkv_dx", dpq, W["dil_w_qkv"], tb=True)
    dxp, dy1 = post_bwd(1, x1, y1, dxp, du2)

    du1, gw1_0, gw2_0 = mlp_bwd(0, u1T, h1, a1T, dy1)
    G["mlp_w1"] = jnp.stack([gw1_0, gw1_1])
    G["mlp_w2"] = jnp.stack([gw2_0, gw2_1])
    dxp, dy0 = post_bwd(0, x0, y0, dxp, du1)

    G["ab_w_out"] = _mm("ab_out_dw", cat.T, dy0)[None]
    dcat = _mm("ab_out_dx", dy0, W["ab_w_out"], tb=True)
    post_params = [post_params[0] + early_grads(G)] + post_params[1:]

    def postmix_bwd_body(r, pp):
        _, vjp = jax.vjp(functools.partial(_post_core, pp[0]), *r[:5], *pp[1:])
        d = vjp(r[5])
        return list(d[:5]), list(d[5:])

    (dysc, dr1, dkh1, dv1, dgate), (G["rw_lnx_g"], G["rw_lnx_b"], drk) = _rows(
        "rwkv_post_bwd", postmix_bwd_body, [ysc, r_, kh_, v_, gate_, (dcat, 512, 0)], [E] + post_params,
        [(RW, F32)] * 5, [(1, RW)] * 3, tm=256)
    G["rw_r_k"] = drk.reshape(1, HEADS, HD)
    dr2, dw2, dk2, da2, db2, dv2 = _scan_bwd(*scan_in, _cols3(dysc, "rwkv_dy_columns"), *saved)

    def pre_bwd_body(r, pp):
        prim, ct = pre_args(r[:len(pre_rows)]), r[len(pre_rows):]
        _, vjp = jax.vjp(functools.partial(_pre_core, pp[0]), *prim, *pp[1:])
        cts = (ct[0] + ct[1], ct[2], ct[3] + ct[4], ct[5] + ct[6], ct[7], ct[8], ct[9], ct[10])
        d = vjp(cts)
        z = jnp.zeros_like(d[12])
        dp = jnp.concatenate([d[0], d[1], d[2], d[3], d[4], d[5], d[12], d[13], d[14]], axis=1)
        dp1 = jnp.concatenate([d[6], d[7], d[8], d[9], d[10], d[11], d[15], z, d[16]], axis=1)
        dp2 = jnp.concatenate([d[17], z, d[18]], axis=1)
        return [dp, dp1, dp2], list(d[NPR:])

    acc_shapes = [a.shape for a in pre_params]
    (dp, dp1, dp2), pacc = _rows(
        "rwkv_pre_bwd", pre_bwd_body,
        pre_rows + [dr1, dr2, dw2, dkh1, dk2, dv1, dv2, da2, db2, dgate, (dcat, 512, 1)],
        [E] + pre_params, [(PAB, F32), (PAB, F32), (PB, F32)], acc_shapes, tm=256)
    G["rw_mu"] = _unpad_pa(jnp.concatenate(pacc[:6], axis=1))
    G["rw_w0"], G["rw_a0"], G["rw_k_k"], G["rw_k_a"] = pacc[6], pacc[8], pacc[11], pacc[12]
    G["rw_w_up"] = pacc[7][None, :64]
    G["rw_a_up"] = pacc[9][None, :64]
    G["rw_g_up"] = pacc[10][None, :160]
    G["sc_conv_w"] = jnp.concatenate(pacc[13:16], axis=0)[None]

    def shift_merge_body(r, pp):
        d0, d1, d1_next, d2, d2_next = r
        d = d0 + _shift_up(d1, d1_next, 1)
        return [jnp.concatenate([d[:, :PA], d[:, PA:] + _shift_up(d2, d2_next, 2)], axis=1)], []

    (dpt,), _ = _rows("shift_merge", shift_merge_body,
                      [dp, dp1, (dp1, PAB, 0, "next"), dp2, (dp2, PB, 0, "next")], [], [(PAB, BF16)])
    du0 = _mm("ab_in_dx", dpt, W["ab_w_in"], tb=True)

    def mod_bwd_body(r, pp):
        du, dx, x = r
        return [dx + du * (1.0 + pp[0])], [jnp.sum(du * x, axis=0, keepdims=True), jnp.sum(du, axis=0, keepdims=True)]

    (grad_x,), (dsc[0], dsh[0]) = _rows("modulate_bwd", mod_bwd_body, [du0, dxp, x0], [sc[0]], [(D, F32)],
                                        [(1, D), (1, D)])

    G["ln_g"] = jnp.concatenate(dlng, axis=0).reshape(2, 2, D)
    G["ln_b"] = jnp.concatenate(dlnb, axis=0).reshape(2, 2, D)
    dmod = jnp.concatenate([dsh[0], dsc[0], dgt[0], dsh[1], dsc[1], dgt[1],
                            dsh[2], dsc[2], dgt[2], dsh[3], dsc[3], dgt[3]], axis=1).reshape(2, 6 * D)
    return loss_acc[0, 0], grad_x, dmod, G, lambda: _ab_in_shards(_mm("ab_in_dw", u0T, dpt))[:, None]


def kernel(x, c, ada_w, ada_b, ln_g, ln_b, ab_w_in, rw_mu, rw_w0, rw_w_up, rw_a0, rw_a_up, rw_g_up, rw_k_k, rw_k_a, rw_r_k, rw_lnx_g, rw_lnx_b, sc_conv_w, ab_w_out, dil_w_qkv, dil_w_out, rel_bias, mlp_w1, mlp_w2, loss_target, m_ada_w, m_ada_b, m_ln_g, m_ln_b, m_ab_w_in, m_rw_mu, m_rw_w0, m_rw_w_up, m_rw_a0, m_rw_a_up, m_rw_g_up, m_rw_k_k, m_rw_k_a, m_rw_r_k, m_rw_lnx_g, m_rw_lnx_b, m_sc_conv_w, m_ab_w_out, m_dil_w_qkv, m_dil_w_out, m_rel_bias, m_mlp_w1, m_mlp_w2, v_ada_w, v_ada_b, v_ln_g, v_ln_b, v_ab_w_in, v_rw_mu, v_rw_w0, v_rw_w_up, v_rw_a0, v_rw_a_up, v_rw_g_up, v_rw_k_k, v_rw_k_a, v_rw_r_k, v_rw_lnx_g, v_rw_lnx_b, v_sc_conv_w, v_ab_w_out, v_dil_w_qkv, v_dil_w_out, v_rel_bias, v_mlp_w1, v_mlp_w2):
    w = dict(ada_w=ada_w, ada_b=ada_b, ln_g=ln_g, ln_b=ln_b, ab_w_in=ab_w_in, rw_mu=rw_mu, rw_w0=rw_w0,
             rw_w_up=rw_w_up, rw_a0=rw_a0, rw_a_up=rw_a_up, rw_g_up=rw_g_up, rw_k_k=rw_k_k, rw_k_a=rw_k_a,
             rw_r_k=rw_r_k, rw_lnx_g=rw_lnx_g, rw_lnx_b=rw_lnx_b, sc_conv_w=sc_conv_w, ab_w_out=ab_w_out,
             dil_w_qkv=dil_w_qkv, dil_w_out=dil_w_out, rel_bias=rel_bias, mlp_w1=mlp_w1, mlp_w2=mlp_w2)
    m = dict(ada_w=m_ada_w, ada_b=m_ada_b, ln_g=m_ln_g, ln_b=m_ln_b, ab_w_in=m_ab_w_in, rw_mu=m_rw_mu,
             rw_w0=m_rw_w0, rw_w_up=m_rw_w_up, rw_a0=m_rw_a0, rw_a_up=m_rw_a_up, rw_g_up=m_rw_g_up,
             rw_k_k=m_rw_k_k, rw_k_a=m_rw_k_a, rw_r_k=m_rw_r_k, rw_lnx_g=m_rw_lnx_g, rw_lnx_b=m_rw_lnx_b,
             sc_conv_w=m_sc_conv_w, ab_w_out=m_ab_w_out, dil_w_qkv=m_dil_w_qkv, dil_w_out=m_dil_w_out,
             rel_bias=m_rel_bias, mlp_w1=m_mlp_w1, mlp_w2=m_mlp_w2)
    v = dict(ada_w=v_ada_w, ada_b=v_ada_b, ln_g=v_ln_g, ln_b=v_ln_b, ab_w_in=v_ab_w_in, rw_mu=v_rw_mu,
             rw_w0=v_rw_w0, rw_w_up=v_rw_w_up, rw_a0=v_rw_a0, rw_a_up=v_rw_a_up, rw_g_up=v_rw_g_up,
             rw_k_k=v_rw_k_k, rw_k_a=v_rw_k_a, rw_r_k=v_rw_r_k, rw_lnx_g=v_rw_lnx_g, rw_lnx_b=v_rw_lnx_b,
             sc_conv_w=v_sc_conv_w, ab_w_out=v_ab_w_out, dil_w_qkv=v_dil_w_qkv, dil_w_out=v_dil_w_out,
             rel_bias=v_rel_bias, mlp_w1=v_mlp_w1, mlp_w2=v_mlp_w2)
    kinds = dict(SHARDED)
    me = 4 * lax.axis_index("x") + 2 * lax.axis_index("y") + lax.axis_index("c")
    ncol = ada_w.shape[2]

    small = _all_gather(_pack([c] + [w[n] for n in GATHER_F32], F32, 8), "gather_small")
    parts = _unpack(small, [c.shape] + [w[n].shape for n in GATHER_F32], (NDEV,))
    c_all = parts[0].reshape(NDEV, D)
    P = {n: _from_chunks(t, kinds[n]) for n, t in zip(GATHER_F32, parts[1:])}
    P = {n: (t if n in ("ln_g", "ln_b") else t[0]) for n, t in P.items()}
    for n in REPLICATED[1:]:
        P[n] = w[n]
    def full(n, t):
        t = _from_chunks(t, kinds[n])
        return t if n in ("mlp_w1", "mlp_w2") else t[0]

    (first,) = _all_gather_many([ab_w_in.astype(BF16)], "gather_first_weight")
    W = {"ab_w_in": _ab_in_padded(first[:, 0])}

    ada_b_loc = lax.dynamic_slice(ada_b, (0, ncol * me), (2, ncol))
    mod_part = _ada_mod(c_all, ada_w, ada_b_loc)
    mod_all = _all_gather(mod_part.reshape(-1, 128), "gather_mod").reshape(NDEV, 2, NDEV, ncol)
    mod = lax.dynamic_index_in_dim(mod_all, me, axis=2, keepdims=False)
    mod = mod.transpose(1, 0, 2).reshape(2, 6 * D)

    behind = (mod[0, 0] * 0.0).astype(BF16)
    later = _exchange_start([w[n].astype(BF16) + (behind if n == LATER[0] else 0) for n in LATER], True,
                            "gather_later_weights_start")
    mod = mod + later[-1][0, 0]

    def later_weights(after):
        lands = _exchange_wait(later, True, after, "gather_later_weights_wait")
        return {n: full(n, t) for n, t in zip(LATER, lands)}

    sent = []

    def early_grads(G):
        sent.append(_exchange_start([_to_chunks(G[n], kinds[n]).astype(BF16) for n in LATER], False,
                                    "exchange_later_grads_start"))
        return sent[0][-1][0, 0]

    loss_part, grad_x, dmod, G, in_grad = _local_step(x[0], loss_target[0], mod, W, P, later_weights, early_grads)
    G["ada_b"] = dmod
    big_out = {}

    def update(n, contributions):
        cols = w[n].shape[-1]
        flat = lambda t: t.reshape(-1, cols)
        rows = flat(w[n]).shape[0]
        outs = _sum_adamw(contributions.reshape(-1, rows, cols), flat(w[n]), flat(m[n]), flat(v[n]),
                          f"sum_adamw_{n}", min(rows, 256))
        big_out[n] = [o.reshape(w[n].shape) for o in outs]

    rep_shapes = [w[n].shape for n in REPLICATED] + [(1,)]
    rep_all = _all_gather(_pack([G[n] for n in REPLICATED] + [loss_part], F32, 8), "gather_replicated_grads")
    names = [n for n, _ in SHARDED if n not in GATHER_BF16]
    shard_shapes = [w[n].shape for n in names]
    recv = _all_to_all(_pack8([_to_chunks(G[n], kinds[n]) for n in names], F32, 8), "exchange_small_grads")

    behind = (recv[0, 0, 0] * 0.0 + rep_all[0, 0, 0] * 0.0).astype(BF16)
    last = _exchange_start([in_grad().astype(BF16) + behind], False, "exchange_last_grad_start")

    zero = last[-1][0:1, 0]
    pk = lambda d: _pack([d[n] for n in REPLICATED] + [zero], F32, 8)
    rep_out = _sum_adamw(rep_all, pk(w), pk(m), pk(v), "sum_adamw_replicated", rep_all.shape[1])
    loss = _unpack(rep_out[0], rep_shapes)[-1][0]
    rep_out = [dict(zip(REPLICATED, _unpack(o, rep_shapes))) for o in rep_out]
    dmod_all = _unpack(rep_all, [(2, 6 * D)], (NDEV,))[0]
    dmod_loc = lax.dynamic_slice(dmod_all, (0, 0, ncol * me), (NDEV, 2, ncol)).transpose(1, 0, 2)
    ada_out = _ada_grad_adamw(c_all.T + zero, dmod_loc, ada_w, m_ada_w, v_ada_w)
    pk = lambda d: _pack([d[n] for n in names], F32, 8)
    sh_out = _sum_adamw(recv, pk(w), pk(m), pk(v), "sum_adamw_small", recv.shape[1])
    sh_out = [dict(zip(names, _unpack(o, shard_shapes))) for o in sh_out]
    for n, r in zip(LATER, _exchange_wait(sent[0], False, last[-1], "exchange_later_grads_wait")):
        update(n, r)
    (landed,) = _exchange_wait(last, False, big_out[LATER[-1]][0], "exchange_last_grad_wait")
    update("ab_w_in", landed)
    sh_out = [{**d, **{n: big_out[n][i] for n in GATHER_BF16}} for i, d in enumerate(sh_out)]

    def pick(i, n):
        if n == "ada_w":
            return ada_out[i]
        return rep_out[i][n] if n in REPLICATED else sh_out[i][n]

    outs = [loss, grad_x[None]]
    for i in range(4):
        outs += [pick(i, n) for n in WEIGHTS]
    return tuple(outs)
```

```python
import functools
import math

import jax
import jax.numpy as jnp
from jax import lax
from jax.experimental import pallas as pl
from jax.experimental.pallas import tpu as pltpu

F32 = jnp.float32
BF16 = jnp.bfloat16
HI = lax.Precision.HIGHEST

NDEV = 8
T = 2048
D = 1024
DFF = 4096
HEADS = 8
HD = 64
RW = 512
PA = 2048
PB = 1536
PAB = PA + PB
QKV = 4608
DILS = (1, 4, 16)
BLK = 128
ALPHA = 4.0 ** 0.25
LN_EPS = 1e-5
GN_EPS = 64e-5
ADAM_LR, ADAM_B1, ADAM_B2, ADAM_EPS, ADAM_WD, ADAM_STEP = 0.001, 0.9, 0.999, 1e-8, 0.01, 10
VMEM_LIMIT = 56 * 1024 * 1024


def _cp(sem):
    return pltpu.CompilerParams(dimension_semantics=sem, vmem_limit_bytes=VMEM_LIMIT)


def _slot(px, py, pc):
    return 4 * px + 2 * py + pc


def _all_gather(x, name):
    R, C = x.shape

    def body(x_ref, out_ref, send_sems, recv_sems, local_sem):
        xi, yi, ci = lax.axis_index("x"), lax.axis_index("y"), lax.axis_index("c")
        me, sibling = (xi, yi, ci), (xi, yi, 1 - ci)
        chips = [(1 - xi, yi), (xi, 1 - yi), (1 - xi, 1 - yi)]

        def rows(px, py, pc):
            return out_ref.at[_slot(px, py, pc)]

        def copy(k, block, to, src=None):
            return pltpu.make_async_remote_copy(
                src_ref=rows(*block) if src is None else src, dst_ref=rows(*block),
                send_sem=send_sems.at[k], recv_sem=recv_sems.at[k],
                device_id=to, device_id_type=pl.DeviceIdType.MESH)

        mine = pltpu.make_async_copy(x_ref, rows(*me), local_sem)
        mine.start()
        first = [copy(0, me, sibling, src=x_ref)]
        first += [copy(1 + j, me, (*chip, ci), src=x_ref) for j, chip in enumerate(chips)]
        for cp in first:
            cp.start()
        passed = [copy(4 + j, (*chip, ci), sibling) for j, chip in enumerate(chips)]
        for j, chip in enumerate(chips):
            copy(1 + j, (*chip, ci), me).wait_recv()
            passed[j].start()
        copy(0, sibling, me).wait_recv()
        for j, chip in enumerate(chips):
            copy(4 + j, (*chip, 1 - ci), me).wait_recv()
        for cp in first + passed:
            cp.wait_send()
        mine.wait()

    return pl.pallas_call(
        body, name=name,
        out_shape=jax.ShapeDtypeStruct((NDEV, R, C), x.dtype),
        in_specs=[pl.BlockSpec(memory_space=pl.ANY)],
        out_specs=pl.BlockSpec(memory_space=pl.ANY),
        scratch_shapes=[pltpu.SemaphoreType.DMA((7,)), pltpu.SemaphoreType.DMA((7,)),
                        pltpu.SemaphoreType.DMA(())],
    )(x)


def _all_to_all(g, name):
    _, R, C = g.shape

    def body(g_ref, out_ref, send_sems, recv_sems, local_sem):
        xi, yi, ci = lax.axis_index("x"), lax.axis_index("y"), lax.axis_index("c")
        my_slot = _slot(xi, yi, ci)
        mine = pltpu.make_async_copy(g_ref.at[my_slot], out_ref.at[my_slot], local_sem)
        mine.start()
        copies = []
        for k in range(1, 8):
            px = 1 - xi if k & 4 else xi
            py = 1 - yi if k & 2 else yi
            pc = 1 - ci if k & 1 else ci
            peer_slot = _slot(px, py, pc)
            copies.append((
                pltpu.make_async_remote_copy(
                    src_ref=g_ref.at[peer_slot], dst_ref=out_ref.at[my_slot],
                    send_sem=send_sems.at[k - 1], recv_sem=recv_sems.at[k - 1],
                    device_id=(px, py, pc), device_id_type=pl.DeviceIdType.MESH),
                pltpu.make_async_remote_copy(
                    src_ref=g_ref.at[peer_slot], dst_ref=out_ref.at[peer_slot],
                    send_sem=send_sems.at[k - 1], recv_sem=recv_sems.at[k - 1],
                    device_id=(px, py, pc), device_id_type=pl.DeviceIdType.MESH)))
        for send, _ in copies:
            send.start()
        for _, recv in copies:
            recv.wait_recv()
        for send, _ in copies:
            send.wait_send()
        mine.wait()

    return pl.pallas_call(
        body, name=name,
        out_shape=jax.ShapeDtypeStruct((NDEV, R, C), g.dtype),
        in_specs=[pl.BlockSpec(memory_space=pl.ANY)],
        out_specs=pl.BlockSpec(memory_space=pl.ANY),
        scratch_shapes=[pltpu.SemaphoreType.DMA((7,)), pltpu.SemaphoreType.DMA((7,)),
                        pltpu.SemaphoreType.DMA(())],
    )(g)


def _my_slot():
    return _slot(lax.axis_index("x"), lax.axis_index("y"), lax.axis_index("c"))


def _put_own(buf, own, slot):
    return lax.dynamic_update_index_in_dim(buf, own, slot, 0)


def _hbm_call(body, name, ins, out_shapes, n_sems):
    anyspec = pl.BlockSpec(memory_space=pl.ANY)
    return pl.pallas_call(
        body, name=name, out_shape=out_shapes,
        in_specs=[anyspec] * len(ins), out_specs=[anyspec] * len(out_shapes),
        scratch_shapes=[pltpu.SemaphoreType.DMA(s) for s in n_sems],
    )(*ins)


def _all_gather_many(xs, name):
    n = len(xs)

    def body(*refs):
        x_refs, o_refs = refs[:n], refs[n:2 * n]
        send_sems, recv_sems = refs[2 * n:]
        xi, yi, ci = lax.axis_index("x"), lax.axis_index("y"), lax.axis_index("c")
        me, sibling = (xi, yi, ci), (xi, yi, 1 - ci)
        chips = [(1 - xi, yi), (xi, 1 - yi), (1 - xi, 1 - yi)]

        def copy(i, k, block, to, src=None):
            dst = o_refs[i].at[_slot(*block)]
            return pltpu.make_async_remote_copy(
                src_ref=dst if src is None else src, dst_ref=dst,
                send_sem=send_sems.at[i, k], recv_sem=recv_sems.at[i, k],
                device_id=to, device_id_type=pl.DeviceIdType.MESH)

        sends = []
        for i in range(n):
            sends += [copy(i, 1 + j, me, (*chip, ci), src=x_refs[i]) for j, chip in enumerate(chips)]
            sends.append(copy(i, 0, me, sibling, src=x_refs[i]))
        for cp in sends:
            cp.start()
        for j, chip in enumerate(chips):
            for i in range(n):
                copy(i, 1 + j, (*chip, ci), me).wait_recv()
                passed = copy(i, 4 + j, (*chip, ci), sibling)
                passed.start()
                sends.append(passed)
        for i in range(n):
            copy(i, 0, sibling, me).wait_recv()
            for j, chip in enumerate(chips):
                copy(i, 4 + j, (*chip, 1 - ci), me).wait_recv()
        for cp in sends:
            cp.wait_send()

    outs = _hbm_call(body, name, xs, [jax.ShapeDtypeStruct((NDEV,) + x.shape, x.dtype) for x in xs],
                     [(n, 7), (n, 7)])
    return [_put_own(o, x[None], _my_slot()) for o, x in zip(outs, xs)]


def _peers(xi, yi, ci):
    return [(1 - xi if k & 4 else xi, 1 - yi if k & 2 else yi, 1 - ci if k & 1 else ci) for k in range(1, 8)]


def _direct_copy(src_refs, land_refs, send_sems, recv_sems, i, k, peer, my_slot, gather):
    src = src_refs[i] if gather else src_refs[i].at[_slot(*peer)]
    return pltpu.make_async_remote_copy(
        src_ref=src, dst_ref=land_refs[i].at[my_slot], send_sem=send_sems.at[7 * i + k], recv_sem=recv_sems.at[7 * i + k],
        device_id=peer, device_id_type=pl.DeviceIdType.MESH)


def _exchange_start(srcs, gather, name):
    n = len(srcs)
    lands = [lax.empty(((NDEV,) + s.shape) if gather else s.shape, s.dtype) for s in srcs]

    def body(*refs):
        s_refs, l_refs = refs[:n], refs[n:2 * n]
        send_sems, recv_sems = refs[2 * n], refs[2 * n + 1]
        token = refs[2 * n + 2 + 2 * n]
        xi, yi, ci = lax.axis_index("x"), lax.axis_index("y"), lax.axis_index("c")
        my_slot = _slot(xi, yi, ci)
        for k, peer in enumerate(_peers(xi, yi, ci)):
            for i in range(n):
                _direct_copy(s_refs, l_refs, send_sems, recv_sems, i, k, peer, my_slot, gather).start()
        token[...] = jnp.zeros_like(token)

    hbm = pl.BlockSpec(memory_space=pltpu.HBM)
    sem = pl.BlockSpec(memory_space=pltpu.SEMAPHORE)
    both = list(srcs) + lands
    return pl.pallas_call(
        body, name=name,
        out_shape=(pltpu.SemaphoreType.DMA((7 * n,)), pltpu.SemaphoreType.DMA((7 * n,)),
                   *[pltpu.HBM(t.shape, t.dtype) for t in both], jax.ShapeDtypeStruct((8, 128), F32)),
        in_specs=[hbm] * (2 * n),
        out_specs=(sem, sem, *[hbm] * (2 * n), pl.BlockSpec(memory_space=pltpu.VMEM)),
        input_output_aliases={i: 2 + i for i in range(2 * n)},
        compiler_params=pltpu.CompilerParams(has_side_effects=pltpu.SideEffectType.DATAFLOW_SIDE_EFFECTING),
    )(*[pltpu.with_memory_space_constraint(t, pltpu.HBM) for t in both])


def _exchange_wait(started, gather, after, name):
    send_sems, recv_sems, *thru, _ = started
    n = len(thru) // 2

    def body(*refs):
        s_refs, l_refs = refs[:n], refs[n:2 * n]
        send_sems, recv_sems = refs[2 * n], refs[2 * n + 1]
        xi, yi, ci = lax.axis_index("x"), lax.axis_index("y"), lax.axis_index("c")
        my_slot = _slot(xi, yi, ci)
        for k, peer in enumerate(_peers(xi, yi, ci)):
            for i in range(n):
                _direct_copy(s_refs, l_refs, send_sems, recv_sems, i, k, peer, my_slot, gather).wait_send()
                _direct_copy(s_refs, l_refs, send_sems, recv_sems, i, k, peer, _slot(*peer), gather).wait_recv()

    hbm = pl.BlockSpec(memory_space=pltpu.HBM)
    sem = pl.BlockSpec(memory_space=pltpu.SEMAPHORE)
    outs = pl.pallas_call(
        body, name=name,
        out_shape=tuple(pltpu.HBM(t.shape, t.dtype) for t in thru),
        in_specs=[hbm] * (2 * n) + [sem, sem, pl.BlockSpec(memory_space=pl.ANY)],
        out_specs=tuple([hbm] * (2 * n)),
        input_output_aliases={i: i for i in range(2 * n)},
        compiler_params=pltpu.CompilerParams(has_side_effects=pltpu.SideEffectType.DATAFLOW_SIDE_EFFECTING),
    )(*thru, send_sems, recv_sems, after)
    slot = _my_slot()
    own = [s[None] if gather else lax.dynamic_index_in_dim(s, slot, 0, keepdims=True) for s in outs[:n]]
    return [_put_own(land, o, slot) for land, o in zip(outs[n:], own)]


def _mm(name, a, b, tb=False, out=(F32,), epi=None, extras=(), tm=2048, tn=512, tk_cap=2048):
    M, K = a.shape
    N = b.shape[0] if tb else b.shape[1]
    tm, tn = min(tm, M), min(tn, N)
    tk = max(t for t in range(128, min(K, tk_cap) + 1, 128) if K % t == 0)
    assert M % tm == 0 and N % tn == 0 and K % tk == 0, (name, M, N, K)
    nk = K // tk
    ne, no = len(extras), len(out)
    dims = (((1,), (1 if tb else 0,)), ((), ()))
    flipped = [isinstance(o, tuple) for o in out]

    def kern(*refs):
        a_ref, b_ref = refs[:2]
        e_refs = refs[2:2 + ne]
        o_refs = refs[2 + ne:2 + ne + no]

        def finish(acc):
            outs = epi(acc, *[e[...] for e in e_refs]) if epi is not None else (acc,)
            for o_ref, o, flip in zip(o_refs, outs, flipped):
                o_ref[...] = (o.T if flip else o).astype(o_ref.dtype)

        part = lax.dot_general(a_ref[...], b_ref[...], dims, preferred_element_type=F32)
        if nk == 1:
            finish(part)
            return
        acc_ref = refs[-1]
        k = pl.program_id(2)

        @pl.when(k == 0)
        def _():
            acc_ref[...] = part

        @pl.when(k > 0)
        def _():
            acc_ref[...] += part

        @pl.when(k == nk - 1)
        def _():
            finish(acc_ref[...])

    b_spec = (pl.BlockSpec((tn, tk), lambda i, j, k: (j, k)) if tb
              else pl.BlockSpec((tk, tn), lambda i, j, k: (k, j)))
    tile = pl.BlockSpec((tm, tn), lambda i, j, k: (i, j))
    tile_t = pl.BlockSpec((tn, tm), lambda i, j, k: (j, i))
    res = pl.pallas_call(
        kern, name=name, grid=(M // tm, N // tn, nk),
        in_specs=[pl.BlockSpec((tm, tk), lambda i, j, k: (i, k)), b_spec] + [tile] * ne,
        out_specs=[tile_t if flip else tile for flip in flipped],
        out_shape=[jax.ShapeDtypeStruct((N, M), o[0]) if flip else jax.ShapeDtypeStruct((M, N), o)
                   for o, flip in zip(out, flipped)],
        scratch_shapes=[pltpu.VMEM((tm, tn), F32)] if nk > 1 else [],
        compiler_params=_cp(("parallel", "parallel", "arbitrary")),
    )(a, b, *extras)
    return res[0] if no == 1 else res


HALO = 8


def _rows(name, body, rows, params, out_rows, out_accs=(), tm=256):
    views = [r if isinstance(r, tuple) else (r, r.shape[1], 0) for r in rows]
    n = views[0][0].shape[0]
    assert n % tm == 0 and tm % HALO == 0
    nr, npar, nor, noa = len(views), len(params), len(out_rows), len(out_accs)
    flipped = [len(o) == 3 for o in out_rows]

    def row_spec(width, cb, halo=None):
        per, last = tm // HALO, n // HALO - 1
        if halo == "prev":
            return pl.BlockSpec((HALO, width), lambda i: (jnp.maximum(i * per - 1, 0), cb))
        if halo == "next":
            return pl.BlockSpec((HALO, width), lambda i: (jnp.minimum((i + 1) * per, last), cb))
        return pl.BlockSpec((tm, width), lambda i: (i, cb))

    def kern(*refs):
        r_refs = refs[:nr]
        p_refs = refs[nr:nr + npar]
        o_refs = refs[nr + npar:nr + npar + nor]
        a_refs = refs[nr + npar + nor:]
        outs, accs = body([r[...] for r in r_refs], [p[...] for p in p_refs])
        assert len(outs) == nor and len(accs) == noa, (name, len(outs), len(accs))
        for o_ref, o, flip in zip(o_refs, outs, flipped):
            o_ref[...] = (o.T if flip else o).astype(o_ref.dtype)
        if noa:
            @pl.when(pl.program_id(0) == 0)
            def _():
                for a_ref in a_refs:
                    a_ref[...] = jnp.zeros_like(a_ref)

            for a_ref, a in zip(a_refs, accs):
                a_ref[...] += a.astype(F32)

    def whole(shape):
        nd = len(shape)
        return pl.BlockSpec(tuple(shape), lambda i, nd=nd: (0,) * nd)

    in_specs = [row_spec(*v[1:]) for v in views]
    in_specs += [whole(p.shape) for p in params]
    out_specs = [pl.BlockSpec((o[0], tm), lambda i: (0, i)) if flip else pl.BlockSpec((tm, o[0]), lambda i: (i, 0))
                 for o, flip in zip(out_rows, flipped)]
    out_specs += [whole(s) for s in out_accs]
    out_shape = [jax.ShapeDtypeStruct((o[0], n) if flip else (n, o[0]), o[1]) for o, flip in zip(out_rows, flipped)]
    out_shape += [jax.ShapeDtypeStruct(tuple(s), F32) for s in out_accs]
    res = pl.pallas_call(
        kern, name=name, grid=(n // tm,), in_specs=in_specs, out_specs=out_specs,
        out_shape=out_shape, compiler_params=_cp(("arbitrary",)),
    )(*[v[0] for v in views], *params)
    return res[:nor], res[nor:]


def _shift_down(x, prev, k):
    head = jnp.where(pl.program_id(0) == 0, 0.0, pltpu.roll(prev, k, axis=0))
    row = lax.broadcasted_iota(jnp.int32, x.shape, 0)
    return jnp.where(row < k, jnp.tile(head, (x.shape[0] // HALO, 1)), pltpu.roll(x, k, axis=0))


def _shift_up(x, nxt, k):
    n = x.shape[0]
    tail = jnp.where(pl.program_id(0) == pl.num_programs(0) - 1, 0.0, pltpu.roll(nxt, HALO - k, axis=0))
    row = lax.broadcasted_iota(jnp.int32, x.shape, 0)
    return jnp.where(row >= n - k, jnp.tile(tail, (n // HALO, 1)), pltpu.roll(x, n - k, axis=0))


@jax.custom_vjp
def _headsum(x, e):
    return sum(jnp.dot(p, e, preferred_element_type=F32) for p in _split3(x))


_headsum.defvjp(lambda x, e: (_headsum(x, e), e), lambda e, ct: (_headsum(ct, e), None))


def _softplus(z):
    return jnp.maximum(z, 0.0) + jnp.log(1.0 + jnp.exp(jnp.minimum(z, -z)))


def _post_ln(x, y, g, lng, lnb):
    z = ALPHA * x + (1.0 + g) * y
    mu = jnp.mean(z, axis=-1, keepdims=True)
    zc = z - mu
    var = jnp.mean(zc * zc, axis=-1, keepdims=True)
    return zc * lax.rsqrt(var + LN_EPS) * lng + lnb


def _post_ln_mod(x, y, g, lng, lnb, scn, shn):
    xn = _post_ln(x, y, g, lng, lnb)
    return xn, xn * (1.0 + scn) + shn


def _pre_core(E, r_, k_, v_, wd_, ad_, gd_, r1, k1, v1, wd1, ad1, gd1, h, bg, cg, h1, cg1, h2, cg2,
              mu_r, mu_k, mu_v, mu_wd, mu_ad, mu_gd, w0, w_up, a0, a_up, g_up, k_k, k_a,
              cw0, cw1, cw2):
    def mix(x, x1, mu):
        return x + mu * (x1 - x)

    r, k, v = mix(r_, r1, mu_r), mix(k_, k1, mu_k), mix(v_, v1, mu_v)
    wd, ad, gd = mix(wd_, wd1, mu_wd), mix(ad_, ad1, mu_ad), mix(gd_, gd1, mu_gd)
    logw = -_softplus(-(w0 + jnp.dot(jnp.tanh(wd), w_up, preferred_element_type=F32))) - 0.5
    decay = jnp.exp(-jnp.exp(logw))
    iclr = jax.nn.sigmoid(a0 + jnp.dot(ad, a_up, preferred_element_type=F32))
    gate = jnp.dot(jax.nn.sigmoid(gd), g_up, preferred_element_type=F32)
    kk0 = k * k_k
    nrm = jnp.sqrt(_headsum(kk0 * kk0, E))
    kk = kk0 / jnp.maximum(nrm, 1e-12)
    kh = k * (1.0 + (iclr - 1.0) * k_a)
    yb = bg * (cw2 * (cg * h) + cw1 * (cg1 * h1) + cw0 * (cg2 * h2))
    return r, decay, kh, v, -kk, kk * iclr, gate, yb


def _post_core(E, y, r, kh, v, gate, lnx_g, lnx_b, rk):
    def seg(t):
        return _headsum(t, E)

    mean = seg(y) * (1.0 / HD)
    yc = y - mean
    var = seg(yc * yc) * (1.0 / HD)
    gn = yc * lax.rsqrt(var + GN_EPS) * lnx_g + lnx_b
    bonus = seg(r * kh * rk) * v
    return (gn + bonus) * gate


def _merge_core(o0, o1, o2, l0, l1, l2):
    m = jnp.maximum(jnp.maximum(l0, l1), l2)
    e0, e1, e2 = jnp.exp(l0 - m), jnp.exp(l1 - m), jnp.exp(l2 - m)
    den = e0 + e1 + e2
    return (e0 * o0 + e1 * o1 + e2 * o2) / den


CHUNK = 128
HALF = 64
HP = HEADS // 2
LW = 2 * HD
NCHUNK = T // CHUNK


def _split3(x):
    hi = x.astype(BF16)
    r1 = x - hi.astype(F32)
    mid = r1.astype(BF16)
    return hi, mid, (r1 - mid.astype(F32)).astype(BF16)


def _cols3(x, name):
    def kern(x_ref, o_ref):
        xt = x_ref[...].T
        left = lax.broadcasted_iota(jnp.int32, (HD, CHUNK), 1) < HALF
        for p in range(HP):
            a, b = xt[p * LW:p * LW + HD], xt[p * LW + HD:(p + 1) * LW]
            halves = [jnp.where(left, a, pltpu.roll(b, HALF, axis=1)), jnp.where(left, pltpu.roll(a, HALF, axis=1), b)]
            for h, tile in enumerate(halves):
                for j, part in enumerate(_split3(tile)):
                    o_ref[p, :, (3 * h + j) * LW:(3 * h + j + 1) * LW] = part

    return pl.pallas_call(
        kern, name=name, grid=(NCHUNK,),
        in_specs=[pl.BlockSpec((CHUNK, RW), lambda c: (c, 0))],
        out_specs=pl.BlockSpec((HP, HD, 6 * CHUNK), lambda c: (0, 0, c)),
        out_shape=jax.ShapeDtypeStruct((HP, HD, 6 * T), BF16),
        compiler_params=_cp(("parallel",)),
    )(x)


def _pick_codes():
    row = lax.broadcasted_iota(jnp.int32, (6 * HALF, LW), 0)
    col = lax.broadcasted_iota(jnp.int32, (6 * HALF, LW), 1)
    same = ((row & (LW - 1)) >= HALF) == (col >= HD)
    return jnp.where(same, row & (HALF - 1), -1).astype(BF16)


def _column(block_ref, codes, half, i):
    pick = jnp.where(codes == i.astype(BF16), jnp.ones((), BF16), jnp.zeros((), BF16))
    block = block_ref[:, :, half * 6 * HALF:(half + 1) * 6 * HALF].reshape(HP * HD, 6 * HALF)
    return jnp.dot(block, pick, preferred_element_type=F32)


def _halfsums(x, row, left1):
    row_l = jnp.where(left1, row, 0.0)
    return (jnp.sum(x * row_l, axis=1, keepdims=True), jnp.sum(x * (row - row_l), axis=1, keepdims=True))


def _pair_rows(row):
    return [row[:, p * LW:(p + 1) * LW] for p in range(HP)]


def _store_columns(ref, p, t_mask, cols):
    for j, col in enumerate(cols):
        pltpu.store(ref.at[pl.ds(2 * p + j, 1)], jnp.broadcast_to(col[None], (1, HD, CHUNK)), mask=t_mask[None])


def _columns_to_rows(cols_ref, rows_ref):
    for p in range(HP):
        rows_ref[:, p * LW:(p + 1) * LW] = cols_ref[2 * p:2 * p + 2].reshape(LW, CHUNK).T


NHALF = T // HALF
HALVES = CHUNK // HALF


def _scan_fwd(r, w, k, a, b, v3):
    def kern(r_ref, w_ref, k_ref, a_ref, b_ref, v_ref, y_ref, ck_ref, st_hbm, sa_hbm,
             s_ref, vb_ref, yc_ref, st_ref, sa_ref, sems):
        c = pl.program_id(0)

        @pl.when(c == 0)
        def _():
            s_ref[...] = jnp.zeros_like(s_ref)

        lane = lax.broadcasted_iota(jnp.int32, (HD, CHUNK), 1)
        left = lane < HD
        left1 = lax.broadcasted_iota(jnp.int32, (1, LW), 1) < HD
        codes = _pick_codes()

        def flush(slot, half_index):
            return [pltpu.make_async_copy(src.at[slot], dst.at[half_index], sems.at[j, slot])
                    for j, (src, dst) in enumerate(((st_ref, st_hbm), (sa_ref, sa_hbm)))]

        for half in range(HALVES):
            ck_ref[half] = s_ref[...]
            vb_ref[...] = _column(v_ref, codes, half, jnp.int32(0))

            @pl.when(c > 0)
            def _():
                for cp in flush(half, (c - 1) * HALVES + half):
                    cp.wait()

            def step(i, carry):
                t = half * HALF + i
                row = lambda ref: _pair_rows(ref[pl.ds(t, 1), :])
                S = [s_ref[p] for p in range(HP)]
                sa = [jnp.where(left, *_halfsums(s, a, left1)) for s, a in zip(S, row(a_ref))]
                S = [s * w + c_ * b + vb_ref[pl.ds(p * HD, HD), :] * k
                     for p, (s, w, c_, b, k) in enumerate(zip(S, row(w_ref), sa, row(b_ref), row(k_ref)))]
                for p, (s, c_) in enumerate(zip(S, sa)):
                    s_ref[p] = s
                    st_ref[half, i, p] = s
                    sa_ref[half, i, p] = c_
                for p, (s, r) in enumerate(zip(S, row(r_ref))):
                    _store_columns(yc_ref, p, lane == t, _halfsums(s, r, left1))
                vb_ref[...] = _column(v_ref, codes, half, i + 1)
                return carry

            lax.fori_loop(0, HALF, step, 0, unroll=16)
            for cp in flush(half, c * HALVES + half):
                cp.start()
        _columns_to_rows(yc_ref, y_ref)

        @pl.when(c == NCHUNK - 1)
        def _():
            for half in range(HALVES):
                for cp in flush(half, c * HALVES + half):
                    cp.wait()

    rowblk = pl.BlockSpec((CHUNK, RW), lambda c: (c, 0))
    saved = jax.ShapeDtypeStruct((NHALF, HALF, HP, HD, LW), F32)
    stage = pltpu.VMEM((HALVES, HALF, HP, HD, LW), F32)
    return pl.pallas_call(
        kern, name="rwkv_scan_fwd", grid=(NCHUNK,),
        in_specs=[rowblk] * 5 + [pl.BlockSpec((HP, HD, 6 * CHUNK), lambda c: (0, 0, c))],
        out_specs=[rowblk, pl.BlockSpec((HALVES, HP, HD, LW), lambda c: (c, 0, 0, 0)),
                   pl.BlockSpec(memory_space=pl.ANY), pl.BlockSpec(memory_space=pl.ANY)],
        out_shape=[jax.ShapeDtypeStruct((T, RW), F32), jax.ShapeDtypeStruct((NHALF, HP, HD, LW), F32), saved, saved],
        scratch_shapes=[pltpu.VMEM((HP, HD, LW), F32), pltpu.VMEM((HP * HD, LW), F32),
                        pltpu.VMEM((HEADS, HD, CHUNK), F32), stage, stage, pltpu.SemaphoreType.DMA((2, HALVES))],
        compiler_params=_cp(("arbitrary",)),
    )(r, w, k, a, b, v3)


def _scan_bwd(r, w, k, a, b, v3, dy3, ck, st, sa):
    def kern(r_ref, w_ref, k_ref, a_ref, b_ref, v_ref, dy_ref, ck_ref, st_hbm, sa_hbm,
             dr_ref, dw_ref, dk_ref, da_ref, db_ref, dv_ref, ds_ref, sb_ref, sa_ref, pick_ref, dvc_ref, sems):
        c = pl.program_id(0)
        chunk = NCHUNK - 1 - c

        @pl.when(c == 0)
        def _():
            ds_ref[...] = jnp.zeros_like(ds_ref)

        lane = lax.broadcasted_iota(jnp.int32, (HD, CHUNK), 1)
        left = lane < HD
        left1 = lax.broadcasted_iota(jnp.int32, (1, LW), 1) < HD
        codes = _pick_codes()

        def rowsum(x):
            return jnp.sum(x, axis=0, keepdims=True)

        def fetch(slot, half_index):
            return [pltpu.make_async_copy(st_hbm.at[half_index], sb_ref.at[slot, pl.ds(1, HALF)], sems.at[0, slot]),
                    pltpu.make_async_copy(sa_hbm.at[half_index], sa_ref.at[slot], sems.at[1, slot])]

        def picks(half, i):
            pick_ref[pl.ds(0, HP * HD), :] = _column(v_ref, codes, half, i)
            pick_ref[pl.ds(HP * HD, HP * HD), :] = _column(dy_ref, codes, half, i)

        @pl.when(c == 0)
        def _():
            for cp in fetch(HALVES - 1, chunk * HALVES + HALVES - 1):
                cp.start()

        for half in reversed(range(HALVES)):
            base = half * HALF
            for cp in fetch(half, chunk * HALVES + half):
                cp.wait()
            if half:
                for cp in fetch(half - 1, chunk * HALVES + half - 1):
                    cp.start()
            else:
                @pl.when(chunk > 0)
                def _():
                    for cp in fetch(HALVES - 1, chunk * HALVES - 1):
                        cp.start()
            sb_ref[half, 0] = ck_ref[half]
            picks(half, jnp.int32(HALF - 1))

            def back(ii, carry):
                i = HALF - 1 - ii
                t = base + i
                row = lambda ref: _pair_rows(ref[pl.ds(t, 1), :])
                a_r, b_r, k_r, w_r, r_r = row(a_ref), row(b_ref), row(k_ref), row(w_ref), row(r_ref)
                vs = [pick_ref[pl.ds(p * HD, HD), :] for p in range(HP)]
                dys = [pick_ref[pl.ds((HP + p) * HD, HD), :] for p in range(HP)]
                picks(half, jnp.maximum(i - 1, 0))
                dr, dw, db, dk, da = [], [], [], [], []
                for p in range(HP):
                    Sp, dy = sb_ref[half, i, p], dys[p]
                    dS = ds_ref[p] + dy * r_r[p]
                    dr.append(rowsum(sb_ref[half, i + 1, p] * dy))
                    dw.append(rowsum(dS * Sp))
                    db.append(rowsum(dS * sa_ref[half, i, p]))
                    dk.append(rowsum(dS * vs[p]))
                    dsa = jnp.where(left, *_halfsums(dS, b_r[p], left1))
                    _store_columns(dvc_ref, p, lane == t, _halfsums(dS, k_r[p], left1))
                    da.append(rowsum(Sp * dsa))
                    ds_ref[p] = dS * w_r[p] + dsa * a_r[p]
                for ref, pieces in ((dr_ref, dr), (dw_ref, dw), (db_ref, db), (dk_ref, dk), (da_ref, da)):
                    ref[pl.ds(t, 1), :] = jnp.concatenate(pieces, axis=1)
                return carry

            lax.fori_loop(0, HALF, back, 0, unroll=8)
        _columns_to_rows(dvc_ref, dv_ref)

    rowblk = pl.BlockSpec((CHUNK, RW), lambda c: (NCHUNK - 1 - c, 0))
    col3blk = pl.BlockSpec((HP, HD, 6 * CHUNK), lambda c: (0, 0, NCHUNK - 1 - c))
    rowshape = jax.ShapeDtypeStruct((T, RW), F32)
    return pl.pallas_call(
        kern, name="rwkv_scan_bwd", grid=(NCHUNK,),
        in_specs=[rowblk] * 5 + [col3blk, col3blk,
                                 pl.BlockSpec((HALVES, HP, HD, LW), lambda c: (NCHUNK - 1 - c, 0, 0, 0)),
                                 pl.BlockSpec(memory_space=pl.ANY), pl.BlockSpec(memory_space=pl.ANY)],
        out_specs=[rowblk] * 6, out_shape=[rowshape] * 6,
        scratch_shapes=[pltpu.VMEM((HP, HD, LW), F32), pltpu.VMEM((HALVES, HALF + 1, HP, HD, LW), F32),
                        pltpu.VMEM((HALVES, HALF, HP, HD, LW), F32), pltpu.VMEM((2 * HP * HD, LW), F32),
                        pltpu.VMEM((HEADS, HD, CHUNK), F32), pltpu.SemaphoreType.DMA((2, HALVES))],
        compiler_params=_cp(("arbitrary",)),
    )(r, w, k, a, b, v3, dy3, ck, st, sa)


NT = (((1,), (1,)), ((), ()))
TN = (((0,), (0,)), ((), ()))
SCALE = HD ** -0.5
QKV_G = 3 * RW


def _attn_setup(g):
    dil = DILS[g]
    qkv = [pl.BlockSpec((T, LW), lambda hp, c=(g * QKV_G + s * RW) // LW: (0, c + hp)) for s in range(3)]
    tile = pl.BlockSpec((T, LW), lambda hp: (0, hp))
    bias = pl.BlockSpec((2, BLK, 2 * BLK), lambda hp: (hp, 0, 0))

    def blocks():
        for r in range(dil):
            for n in range(T // dil // BLK):
                rows = pl.ds(n * BLK * dil + r, BLK, stride=dil)
                keys = pl.ds((n - 1) * BLK * dil + r, 2 * BLK, stride=dil) if n else rows
                yield n, rows, keys

    return qkv, tile, bias, blocks


def _band(n):
    qi = lax.broadcasted_iota(jnp.int32, (BLK, 2 * BLK), 0)
    ki = lax.broadcasted_iota(jnp.int32, (BLK, 2 * BLK), 1)
    band = (ki >= qi) & (ki <= qi + BLK)
    return band if n else band[:, BLK:]


def _head_masks():
    lane = lax.broadcasted_iota(jnp.int32, (BLK, LW), 1)
    return lane < HD, [(lane < HD).astype(BF16), (lane >= HD).astype(BF16)]


def _attn_fwd(pq, bias, g):
    qkv, tile, bias_spec, blocks = _attn_setup(g)

    def kern(q_ref, k_ref, v_ref, b_ref, o_ref, l_ref):
        left, masks = _head_masks()
        for n, rows, keys in blocks():
            qb, kc, vc = q_ref[rows, :].astype(BF16), k_ref[keys, :].astype(BF16), v_ref[keys, :].astype(BF16)
            valid = _band(n)
            o, lse = [], []
            for j in range(2):
                bias_j = b_ref[j] if n else b_ref[j][:, BLK:]
                s = lax.dot_general(qb * masks[j], kc, NT, preferred_element_type=F32) * SCALE + bias_j
                s = jnp.where(valid, s, -jnp.inf)
                m = jnp.max(s, axis=1, keepdims=True)
                e = jnp.exp(s - m)
                den = jnp.sum(e, axis=1, keepdims=True)
                o.append(jnp.dot((e / den).astype(BF16), vc, preferred_element_type=F32))
                lse.append(m + jnp.log(den))
            o_ref[rows, :] = jnp.where(left, o[0], o[1])
            l_ref[rows, :] = jnp.where(left, lse[0], lse[1])

    shape = jax.ShapeDtypeStruct((T, RW), F32)
    return pl.pallas_call(
        kern, name=f"attn_fwd_{g}", grid=(HP,),
        in_specs=qkv + [bias_spec], out_specs=[tile, tile], out_shape=[shape, shape],
        compiler_params=_cp(("parallel",)),
    )(pq, pq, pq, bias)


def _attn_bwd(pq, bias, do, o, lse, dlse, g):
    qkv, tile, bias_spec, blocks = _attn_setup(g)

    def kern(q_ref, k_ref, v_ref, b_ref, do_ref, o_ref, l_ref, dl_ref, dq_ref, dk_ref, dv_ref, db_ref):
        left, masks = _head_masks()
        lane = lax.broadcasted_iota(jnp.int32, (BLK, LW), 1)
        dk_ref[...] = jnp.zeros_like(dk_ref)
        dv_ref[...] = jnp.zeros_like(dv_ref)
        db_ref[...] = jnp.zeros_like(db_ref)

        def column(tile_, j):
            return jnp.sum(jnp.where(lane == j * HD, tile_, 0.0), axis=1, keepdims=True)

        for n, rows, keys in blocks():
            qb, kc, vc = q_ref[rows, :].astype(BF16), k_ref[keys, :].astype(BF16), v_ref[keys, :].astype(BF16)
            dof, valid = do_ref[rows, :], _band(n)
            dob, prod = dof.astype(BF16), dof * o_ref[rows, :]
            dq = []
            for j in range(2):
                bias_j = b_ref[j] if n else b_ref[j][:, BLK:]
                delta = jnp.sum(prod * masks[j].astype(F32), axis=1, keepdims=True)
                qm, dom = qb * masks[j], dob * masks[j]
                s = lax.dot_general(qm, kc, NT, preferred_element_type=F32) * SCALE + bias_j
                p = jnp.where(valid, jnp.exp(s - column(l_ref[rows, :], j)), 0.0)
                dp = lax.dot_general(dom, vc, NT, preferred_element_type=F32)
                ds = p * (dp + (column(dl_ref[rows, :], j) - delta))
                if n:
                    db_ref[j] += ds
                else:
                    db_ref[j, :, BLK:] += ds
                dsb = (ds * SCALE).astype(BF16)
                dq.append(jnp.dot(dsb, kc, preferred_element_type=F32))
                dk_ref[keys, :] += lax.dot_general(dsb, qm, TN, preferred_element_type=F32)
                dv_ref[keys, :] += lax.dot_general(p.astype(BF16), dom, TN, preferred_element_type=F32)
            dq_ref[rows, :] = jnp.where(left, dq[0], dq[1])

    shape = jax.ShapeDtypeStruct((T, RW), F32)
    return pl.pallas_call(
        kern, name=f"attn_bwd_{g}", grid=(HP,),
        in_specs=qkv + [bias_spec] + [tile] * 4, out_specs=[tile] * 3 + [bias_spec],
        out_shape=[shape] * 3 + [jax.ShapeDtypeStruct((HEADS, BLK, 2 * BLK), F32)],
        compiler_params=_cp(("parallel",)),
    )(pq, pq, pq, bias, do, o, lse, dlse)


NBUCKET = 32
NPAIR = BLK * 2 * BLK


def _relbias_table(rbT, onehotT):
    def kern(rb_ref, oh_ref, out_ref):
        out_ref[0] = sum(jnp.dot(p, oh_ref[0], preferred_element_type=F32) for p in _split3(rb_ref[0]))

    return pl.pallas_call(
        kern, name="relbias_table", grid=(3,),
        in_specs=[pl.BlockSpec((1, HEADS, NBUCKET), lambda g: (g, 0, 0)),
                  pl.BlockSpec((1, NBUCKET, NPAIR), lambda g: (g, 0, 0))],
        out_specs=pl.BlockSpec((1, HEADS, NPAIR), lambda g: (g, 0, 0)),
        out_shape=jax.ShapeDtypeStruct((3, HEADS, NPAIR), F32),
        compiler_params=_cp(("parallel",)),
    )(rbT, onehotT)


def _relbias_grad(db, onehotT):
    nt = (((1,), (1,)), ((), ()))

    def kern(db_ref, oh_ref, out_ref):
        hi, mid, _ = _split3(db_ref[0])
        out_ref[0] = (lax.dot_general(hi, oh_ref[0], nt, preferred_element_type=F32)
                      + lax.dot_general(mid, oh_ref[0], nt, preferred_element_type=F32))

    return pl.pallas_call(
        kern, name="relbias_grad", grid=(3,),
        in_specs=[pl.BlockSpec((1, HEADS, NPAIR), lambda g: (g, 0, 0)),
                  pl.BlockSpec((1, NBUCKET, NPAIR), lambda g: (g, 0, 0))],
        out_specs=pl.BlockSpec((1, HEADS, NBUCKET), lambda g: (g, 0, 0)),
        out_shape=jax.ShapeDtypeStruct((3, HEADS, NBUCKET), F32),
        compiler_params=_cp(("parallel",)),
    )(db, onehotT)


def _adamw(w, g, m, v):
    m2 = ADAM_B1 * m + (1.0 - ADAM_B1) * g
    v2 = ADAM_B2 * v + (1.0 - ADAM_B2) * (g * g)
    m_hat = m2 / (1.0 - ADAM_B1 ** ADAM_STEP)
    v_hat = v2 / (1.0 - ADAM_B2 ** ADAM_STEP)
    return -ADAM_LR * (m_hat / (jnp.sqrt(v_hat) + ADAM_EPS) + ADAM_WD * w), m2, v2


def _ada_mod(c_all, ada_w, ada_b_loc):
    def kern(c_ref, w_ref, b_ref, o_ref):
        c = c_ref[...]
        cond = c * jax.nn.sigmoid(c)
        o_ref[0] = jnp.dot(cond, w_ref[0], precision=HI, preferred_element_type=F32) + b_ref[0]

    ncol = ada_w.shape[2]
    return pl.pallas_call(
        kern, name="ada_mod", grid=(2,),
        in_specs=[pl.BlockSpec((NDEV, D), lambda i: (0, 0)),
                  pl.BlockSpec((1, D, ncol), lambda i: (i, 0, 0)),
                  pl.BlockSpec((1, 1, ncol), lambda i: (i, 0, 0))],
        out_specs=pl.BlockSpec((1, NDEV, ncol), lambda i: (i, 0, 0)),
        out_shape=jax.ShapeDtypeStruct((2, NDEV, ncol), F32),
        compiler_params=_cp(("parallel",)),
    )(c_all, ada_w, ada_b_loc.reshape(2, 1, ncol))


def _ada_grad_adamw(cT_all, dmod_loc, w, m, v):
    ncol = w.shape[2]
    tr = 256

    def kern(c_ref, d_ref, w_ref, m_ref, v_ref, g_ref, dl_ref, m2_ref, v2_ref):
        c = c_ref[...]
        cond = c * jax.nn.sigmoid(c)
        g = jnp.dot(cond, d_ref[0], precision=HI, preferred_element_type=F32)
        dl, m2, v2 = _adamw(w_ref[0], g, m_ref[0], v_ref[0])
        g_ref[0], dl_ref[0], m2_ref[0], v2_ref[0] = g, dl, m2, v2

    big = pl.BlockSpec((1, tr, ncol), lambda i, j: (i, j, 0))
    shp = jax.ShapeDtypeStruct(w.shape, F32)
    return pl.pallas_call(
        kern, name="ada_grad_adamw", grid=(2, D // tr),
        in_specs=[pl.BlockSpec((tr, NDEV), lambda i, j: (j, 0)),
                  pl.BlockSpec((1, NDEV, ncol), lambda i, j: (i, 0, 0)), big, big, big],
        out_specs=[big] * 4, out_shape=[shp] * 4,
        compiler_params=_cp(("parallel", "parallel")),
    )(cT_all, dmod_loc, w, m, v)


def _sum_adamw(recv, w, m, v, name, tr):
    S = recv.shape[0]
    R, C = w.shape
    assert R % tr == 0 and recv.shape[1:] == (R, C)

    def kern(r_ref, w_ref, m_ref, v_ref, g_ref, dl_ref, m2_ref, v2_ref):
        g = r_ref[0].astype(F32)
        for s in range(1, S):
            g = g + r_ref[s].astype(F32)
        dl, m2, v2 = _adamw(w_ref[...], g, m_ref[...], v_ref[...])
        g_ref[...], dl_ref[...], m2_ref[...], v2_ref[...] = g, dl, m2, v2

    flat = pl.BlockSpec((tr, C), lambda i: (i, 0))
    shp = jax.ShapeDtypeStruct((R, C), F32)
    return pl.pallas_call(
        kern, name=name, grid=(R // tr,),
        in_specs=[pl.BlockSpec((S, tr, C), lambda i: (0, i, 0)), flat, flat, flat],
        out_specs=[flat] * 4, out_shape=[shp] * 4,
        compiler_params=_cp(("parallel",)),
    )(recv, w, m, v)


def _pack(arrs, dtype, row_mult):
    flat = jnp.concatenate([a.reshape(-1).astype(dtype) for a in arrs])
    flat = jnp.pad(flat, (0, -flat.shape[0] % (128 * row_mult)))
    return flat.reshape(-1, 128)


def _pack8(arrs, dtype, row_mult):
    flat = jnp.concatenate([a.reshape(NDEV, -1).astype(dtype) for a in arrs], axis=1)
    flat = jnp.pad(flat, ((0, 0), (0, -flat.shape[1] % (128 * row_mult))))
    return flat.reshape(NDEV, -1, 128)


def _unpack(buf, shapes, lead=()):
    flat = buf.reshape(lead + (-1,))
    out, off = [], 0
    for s in shapes:
        n = math.prod(s)
        out.append(flat[..., off:off + n].reshape(lead + tuple(s)))
        off += n
    return out


def _to_chunks(full, kind):
    if kind == "col":
        x = full.reshape(full.shape[:-1] + (NDEV, full.shape[-1] // NDEV))
        return jnp.moveaxis(x, -2, 0)
    x = full.reshape(full.shape[:-2] + (NDEV, full.shape[-2] // NDEV, full.shape[-1]))
    return jnp.moveaxis(x, -3, 0)


def _from_chunks(g8, kind):
    if kind == "col":
        x = jnp.moveaxis(g8, 0, -2)
        return x.reshape(x.shape[:-2] + (x.shape[-2] * x.shape[-1],))
    x = jnp.moveaxis(g8, 0, -3)
    return x.reshape(x.shape[:-3] + (x.shape[-3] * x.shape[-2], x.shape[-1]))


def _pad_pa(x):
    z = lambda n: jnp.zeros(x.shape[:-1] + (n,), x.dtype)
    return jnp.concatenate([x[..., :1600], z(64), x[..., 1600:1664], z(64), x[..., 1664:1824], z(96)], -1)


def _unpad_pa(x):
    return jnp.concatenate([x[..., :1600], x[..., 1664:1728], x[..., 1792:1952]], -1)


AB_SEGMENTS = ((0, 1600, 0), (1600, 1664, 64), (1664, 1824, 128), (1824, 3360, PAB - 3360))
AB_SHARD = 3360 // NDEV


def _ab_in_padded(g8):
    blocks, at = [], 0
    for start, end, shift in AB_SEGMENTS:
        if start + shift > at:
            blocks.append(jnp.zeros((g8.shape[1], start + shift - at), g8.dtype))
        for j in range(start // AB_SHARD, (end - 1) // AB_SHARD + 1):
            lo, hi = max(start, j * AB_SHARD), min(end, (j + 1) * AB_SHARD)
            blocks.append(g8[j, :, lo - j * AB_SHARD:hi - j * AB_SHARD])
        at = end + shift
    return jnp.concatenate(blocks, axis=1)


def _ab_in_shards(padded):
    shards = []
    for j in range(NDEV):
        pieces = [padded[:, max(start, j * AB_SHARD) + shift:min(end, (j + 1) * AB_SHARD) + shift]
                  for start, end, shift in AB_SEGMENTS if max(start, j * AB_SHARD) < min(end, (j + 1) * AB_SHARD)]
        shards.append(jnp.concatenate(pieces, axis=1))
    return jnp.stack(shards)


def _pad_rows(x, n):
    return jnp.pad(x, ((0, n - x.shape[0]), (0, 0)))


def _bucket_tables():
    qi = jnp.arange(BLK)[:, None]
    ki = jnp.arange(2 * BLK)[None, :]
    rel = BLK + qi - ki
    tabs = []
    for dil in DILS:
        dist = jnp.clip(rel, 0, BLK) * dil
        logd = jnp.log(jnp.maximum(dist, 1).astype(F32) / 16) / math.log(2048 / 16)
        large = jnp.minimum(16 + (logd * 16).astype(jnp.int32), 31)
        tabs.append(jnp.where(dist < 16, dist, large))
    return jnp.stack(tabs)


SHARDED = (("ln_g", "col"), ("ln_b", "col"), ("ab_w_in", "col"), ("rw_w_up", "col"), ("rw_a_up", "col"),
           ("rw_g_up", "col"), ("sc_conv_w", "col"), ("ab_w_out", "row"), ("dil_w_qkv", "col"),
           ("dil_w_out", "col"), ("mlp_w1", "col"), ("mlp_w2", "row"))
FIRST = ("ab_w_in",)
LATER = ("ab_w_out", "dil_w_qkv", "dil_w_out", "mlp_w1", "mlp_w2")
GATHER_BF16 = FIRST + LATER
GATHER_F32 = ("rw_w_up", "rw_a_up", "rw_g_up", "sc_conv_w", "ln_g", "ln_b")
REPLICATED = ("ada_b", "rw_mu", "rw_w0", "rw_a0", "rw_k_k", "rw_k_a", "rw_r_k", "rw_lnx_g", "rw_lnx_b", "rel_bias")
WEIGHTS = ("ada_w", "ada_b", "ln_g", "ln_b", "ab_w_in", "rw_mu", "rw_w0", "rw_w_up", "rw_a0", "rw_a_up",
           "rw_g_up", "rw_k_k", "rw_k_a", "rw_r_k", "rw_lnx_g", "rw_lnx_b", "sc_conv_w", "ab_w_out",
           "dil_w_qkv", "dil_w_out", "rel_bias", "mlp_w1", "mlp_w2")


def _local_step(x0, tgt, mod, W, P, later_weights, early_grads):
    row = lambda a: a.reshape(1, -1)
    W = dict(W)
    m6 = mod.reshape(2, 6, 1, D)
    sc = [m6[0, 1], m6[0, 4], m6[1, 1], m6[1, 4]]
    sh = [m6[0, 0], m6[0, 3], m6[1, 0], m6[1, 3]]
    gt = [m6[0, 2], m6[0, 5], m6[1, 2], m6[1, 5]]
    lng = [row(P["ln_g"][0, 0]), row(P["ln_g"][0, 1]), row(P["ln_g"][1, 0]), row(P["ln_g"][1, 1])]
    lnb = [row(P["ln_b"][0, 0]), row(P["ln_b"][0, 1]), row(P["ln_b"][1, 0]), row(P["ln_b"][1, 1])]
    E = jnp.kron(jnp.eye(HEADS, dtype=BF16), jnp.ones((HD, HD), BF16))

    def mod_body(r, p):
        u = r[0] * (1.0 + p[0]) + p[1]
        return [u, u], []

    (u0, u0T), _ = _rows("modulate", mod_body, [x0], [sc[0], sh[0]], [(D, BF16), (D, BF16, "T")])

    def post_fwd_body(r, p):
        xn, un = _post_ln_mod(r[0], r[1], *p)
        return [xn, un, un], []

    def post_fwd(s, x, y):
        (xn, un, unT), _ = _rows(f"post_ln_{s}", post_fwd_body, [x, y],
                                 [gt[s], lng[s], lnb[s], sc[s + 1], sh[s + 1]],
                                 [(D, F32), (D, BF16), (D, BF16, "T")])
        return xn, un, unT

    def relu2(acc):
        a = jnp.maximum(acc, 0.0)
        return acc, a * a, a * a

    def relu2_bwd(acc, h):
        return (acc * (2.0 * jnp.maximum(h, 0.0)),)

    p = _mm("ab_in", u0, W["ab_w_in"])
    mu = _pad_pa(P["rw_mu"])
    mu_parts = [mu[:, :512], mu[:, 512:1024], mu[:, 1024:1536], mu[:, 1536:1664], mu[:, 1664:1792], mu[:, 1792:]]
    pre_params = mu_parts + [P["rw_w0"], _pad_rows(P["rw_w_up"], 128), P["rw_a0"], _pad_rows(P["rw_a_up"], 128),
                             _pad_rows(P["rw_g_up"], 256), P["rw_k_k"], P["rw_k_a"],
                             P["sc_conv_w"][0:1], P["sc_conv_w"][1:2], P["sc_conv_w"][2:3]]
    pieces = [(p, 512, 0), (p, 512, 1), (p, 512, 2), (p, 128, 12), (p, 128, 13), (p, 256, 7),
              (p, 512, 4), (p, 512, 5), (p, 512, 6)]
    shifted = [0, 1, 2, 3, 4, 5, 6, 8]
    pre_rows = pieces + [pieces[i] + ("prev",) for i in shifted]
    NPR = 19

    def pre_args(r):
        x, prev = r[:9], dict(zip(shifted, r[9:17]))
        down = lambda i, k: _shift_down(x[i], prev[i], k)
        return x[:6] + [down(i, 1) for i in range(6)] + x[6:9] + [down(6, 1), down(8, 1), down(6, 2), down(8, 2)]

    def pre_fwd_body(r, pp):
        return list(_pre_core(pp[0], *pre_args(r), *pp[1:])), []

    (r_, w_, kh_, v_, a_, b_, gate_, yb), _ = _rows(
        "rwkv_pre", pre_fwd_body, pre_rows, [E] + pre_params, [(RW, F32)] * 7 + [(RW, BF16)], tm=256)
    scan_in = [r_, w_, kh_, a_, b_, _cols3(v_, "rwkv_v_columns")]
    ysc, *saved = _scan_fwd(*scan_in)
    post_params = [P["rw_lnx_g"], P["rw_lnx_b"], P["rw_r_k"].reshape(1, RW)]

    def postmix_fwd_body(r, pp):
        return [_post_core(pp[0], *r, *pp[1:])], []

    (ya,), _ = _rows("rwkv_post", postmix_fwd_body, [ysc, r_, kh_, v_, gate_], [E] + post_params,
                     [(RW, BF16)], tm=256)
    cat = jnp.concatenate([ya, yb], axis=1)
    W.update(later_weights(cat))
    y0 = _mm("ab_out", cat, W["ab_w_out"])
    x1, u1, u1T = post_fwd(0, x0, y0)

    h1, a1, a1T = _mm("mlp1_up_0", u1, W["mlp_w1"][0], out=(F32, BF16, (BF16, "T")), epi=relu2)
    y1 = _mm("mlp1_down_0", a1, W["mlp_w2"][0])
    x2, u2, u2T = post_fwd(1, x1, y1)

    pq = _mm("qkv", u2, W["dil_w_qkv"])
    onehotT = (_bucket_tables().reshape(3, 1, NPAIR) == jnp.arange(NBUCKET).reshape(1, NBUCKET, 1)).astype(BF16)
    rbT = P["rel_bias"].reshape(NBUCKET, 3, HEADS).transpose(1, 2, 0)
    bias = _relbias_table(rbT, onehotT).reshape(3, HEADS, BLK, 2 * BLK)
    og, lse = zip(*[_attn_fwd(pq, bias[g], g) for g in range(3)])

    def merge_fwd_body(r, pp):
        return [_merge_core(*r)], []

    (om,), _ = _rows("attn_merge", merge_fwd_body, list(og + lse), [], [(RW, BF16)])
    y2 = _mm("dil_out", om, W["dil_w_out"])
    x3, u3, u3T = post_fwd(2, x2, y2)

    h3, a3, a3T = _mm("mlp1_up_1", u3, W["mlp_w1"][1], out=(F32, BF16, (BF16, "T")), epi=relu2)
    y3 = _mm("mlp1_down_1", a3, W["mlp_w2"][1])

    def last_body(r, pp):
        x, y, tg = r
        xn, vjp = jax.vjp(_post_ln, x, y, *pp)
        err = xn - tg
        dx, dy, dg, dlg, dlb = vjp(err * (1.0 / D))
        loss = jnp.full((1, 128), (0.5 / D) * jnp.sum(err * err), F32)
        return [dx, dy], [loss, dg, dlg, dlb]

    (dxp, dy3), (loss_acc, dg3, dlng3, dlnb3) = _rows(
        "final_ln_loss", last_body, [x3, y3, tgt], [gt[3], lng[3], lnb[3]],
        [(D, F32), (D, BF16)], [(1, 128), (1, D), (1, D), (1, D)])

    G = {}
    dsc, dsh, dgt = [None] * 4, [None] * 4, [None] * 4
    dlng, dlnb = [None] * 4, [None] * 4
    dgt[3], dlng[3], dlnb[3] = dg3, dlng3, dlnb3

    def mlp_bwd(i, uT, h, aT, dy):
        dh = _mm(f"mlp_dh_{i}", dy, W["mlp_w2"][i], tb=True, out=(BF16,), epi=relu2_bwd, extras=(h,))
        gw2 = _mm(f"mlp_dw2_{i}", aT, dy)
        du = _mm(f"mlp_du_{i}", dh, W["mlp_w1"][i], tb=True)
        gw1 = _mm(f"mlp_dw1_{i}", uT, dh)
        return du, gw1, gw2

    def post_bwd_body(r, pp):
        x, y, dxn, dun = r
        _, vjp = jax.vjp(_post_ln_mod, x, y, *pp)
        dx, dy, dg, dlg, dlb, dscn, dshn = vjp((dxn, dun))
        return [dx, dy], [dg, dlg, dlb, dscn, dshn]

    def post_bwd(s, x, y, dxn, dun):
        (dx, dy), (dgt[s], dlng[s], dlnb[s], dsc[s + 1], dsh[s + 1]) = _rows(
            f"post_ln_bwd_{s}", post_bwd_body, [x, y, dxn, dun],
            [gt[s], lng[s], lnb[s], sc[s + 1], sh[s + 1]], [(D, F32), (D, BF16)], [(1, D)] * 5)
        return dx, dy

    du3, gw1_1, gw2_1 = mlp_bwd(1, u3T, h3, a3T, dy3)
    dxp, dy2 = post_bwd(2, x2, y2, dxp, du3)

    G["dil_w_out"] = _mm("dil_out_dw", om.T, dy2)[None]
    do = _mm("dil_out_dx", dy2, W["dil_w_out"], tb=True)

    def merge_bwd_body(r, pp):
        _, vjp = jax.vjp(_merge_core, *r[:6])
        d = vjp(r[6])
        return list(d[:3]) + [_headsum(d[3 + g], pp[0]) for g in range(3)], []

    mb, _ = _rows("attn_merge_bwd", merge_bwd_body, list(og + lse) + [do], [E],
                  [(RW, F32)] * 6)
    back = [_attn_bwd(pq, bias[g], mb[g], og[g], lse[g], mb[3 + g], g) for g in range(3)]
    dpq = jnp.concatenate([t for dq, dk, dv, _ in back for t in (dq, dk, dv)], axis=1).astype(BF16)
    rb = _relbias_grad(jnp.stack([b[3] for b in back]).reshape(3, HEADS, NPAIR), onehotT)
    G["rel_bias"] = rb.transpose(2, 0, 1).reshape(NBUCKET, 3 * HEADS)
    G["dil_w_qkv"] = _mm("qkv_dw", u2T, dpq)[None]
    du2 = _mm("qkv_dx", dpq, W["dil_w_qkv"], tb=True)
    dxp, dy1 = post_bwd(1, x1, y1, dxp, du2)

    du1, gw1_0, gw2_0 = mlp_bwd(0, u1T, h1, a1T, dy1)
    G["mlp_w1"] = jnp.stack([gw1_0, gw1_1])
    G["mlp_w2"] = jnp.stack([gw2_0, gw2_1])
    dxp, dy0 = post_bwd(0, x0, y0, dxp, du1)

    G["ab_w_out"] = _mm("ab_out_dw", cat.T, dy0)[None]
    dcat = _mm("ab_out_dx", dy0, W["ab_w_out"], tb=True)
    post_params = [post_params[0] + early_grads(G)] + post_params[1:]

    def postmix_bwd_body(r, pp):
        _, vjp = jax.vjp(functools.partial(_post_core, pp[0]), *r[:5], *pp[1:])
        d = vjp(r[5])
        return list(d[:5]), list(d[5:])

    (dysc, dr1, dkh1, dv1, dgate), (G["rw_lnx_g"], G["rw_lnx_b"], drk) = _rows(
        "rwkv_post_bwd", postmix_bwd_body, [ysc, r_, kh_, v_, gate_, (dcat, 512, 0)], [E] + post_params,
        [(RW, F32)] * 5, [(1, RW)] * 3, tm=256)
    G["rw_r_k"] = drk.reshape(1, HEADS, HD)
    dr2, dw2, dk2, da2, db2, dv2 = _scan_bwd(*scan_in, _cols3(dysc, "rwkv_dy_columns"), *saved)

    def pre_bwd_body(r, pp):
        prim, ct = pre_args(r[:len(pre_rows)]), r[len(pre_rows):]
        _, vjp = jax.vjp(functools.partial(_pre_core, pp[0]), *prim, *pp[1:])
        cts = (ct[0] + ct[1], ct[2], ct[3] + ct[4], ct[5] + ct[6], ct[7], ct[8], ct[9], ct[10])
        d = vjp(cts)
        z = jnp.zeros_like(d[12])
        dp = jnp.concatenate([d[0], d[1], d[2], d[3], d[4], d[5], d[12], d[13], d[14]], axis=1)
        dp1 = jnp.concatenate([d[6], d[7], d[8], d[9], d[10], d[11], d[15], z, d[16]], axis=1)
        dp2 = jnp.concatenate([d[17], z, d[18]], axis=1)
        return [dp, dp1, dp2], list(d[NPR:])

    acc_shapes = [a.shape for a in pre_params]
    (dp, dp1, dp2), pacc = _rows(
        "rwkv_pre_bwd", pre_bwd_body,
        pre_rows + [dr1, dr2, dw2, dkh1, dk2, dv1, dv2, da2, db2, dgate, (dcat, 512, 1)],
        [E] + pre_params, [(PAB, F32), (PAB, F32), (PB, F32)], acc_shapes, tm=256)
    G["rw_mu"] = _unpad_pa(jnp.concatenate(pacc[:6], axis=1))
    G["rw_w0"], G["rw_a0"], G["rw_k_k"], G["rw_k_a"] = pacc[6], pacc[8], pacc[11], pacc[12]
    G["rw_w_up"] = pacc[7][None, :64]
    G["rw_a_up"] = pacc[9][None, :64]
    G["rw_g_up"] = pacc[10][None, :160]
    G["sc_conv_w"] = jnp.concatenate(pacc[13:16], axis=0)[None]

    def shift_merge_body(r, pp):
        d0, d1, d1_next, d2, d2_next = r
        d = d0 + _shift_up(d1, d1_next, 1)
        return [jnp.concatenate([d[:, :PA], d[:, PA:] + _shift_up(d2, d2_next, 2)], axis=1)], []

    (dpt,), _ = _rows("shift_merge", shift_merge_body,
                      [dp, dp1, (dp1, PAB, 0, "next"), dp2, (dp2, PB, 0, "next")], [], [(PAB, BF16)])
    du0 = _mm("ab_in_dx", dpt, W["ab_w_in"], tb=True)

    def mod_bwd_body(r, pp):
        du, dx, x = r
        return [dx + du * (1.0 + pp[0])], [jnp.sum(du * x, axis=0, keepdims=True), jnp.sum(du, axis=0, keepdims=True)]

    (grad_x,), (dsc[0], dsh[0]) = _rows("modulate_bwd", mod_bwd_body, [du0, dxp, x0], [sc[0]], [(D, F32)],
                                        [(1, D), (1, D)])

    G["ln_g"] = jnp.concatenate(dlng, axis=0).reshape(2, 2, D)
    G["ln_b"] = jnp.concatenate(dlnb, axis=0).reshape(2, 2, D)
    dmod = jnp.concatenate([dsh[0], dsc[0], dgt[0], dsh[1], dsc[1], dgt[1],
                            dsh[2], dsc[2], dgt[2], dsh[3], dsc[3], dgt[3]], axis=1).reshape(2, 6 * D)
    return loss_acc[0, 0], grad_x, dmod, G, lambda: _ab_in_shards(_mm("ab_in_dw", u0T, dpt))[:, None]


def kernel(x, c, ada_w, ada_b, ln_g, ln_b, ab_w_in, rw_mu, rw_w0, rw_w_up, rw_a0, rw_a_up, rw_g_up, rw_k_k, rw_k_a, rw_r_k, rw_lnx_g, rw_lnx_b, sc_conv_w, ab_w_out, dil_w_qkv, dil_w_out, rel_bias, mlp_w1, mlp_w2, loss_target, m_ada_w, m_ada_b, m_ln_g, m_ln_b, m_ab_w_in, m_rw_mu, m_rw_w0, m_rw_w_up, m_rw_a0, m_rw_a_up, m_rw_g_up, m_rw_k_k, m_rw_k_a, m_rw_r_k, m_rw_lnx_g, m_rw_lnx_b, m_sc_conv_w, m_ab_w_out, m_dil_w_qkv, m_dil_w_out, m_rel_bias, m_mlp_w1, m_mlp_w2, v_ada_w, v_ada_b, v_ln_g, v_ln_b, v_ab_w_in, v_rw_mu, v_rw_w0, v_rw_w_up, v_rw_a0, v_rw_a_up, v_rw_g_up, v_rw_k_k, v_rw_k_a, v_rw_r_k, v_rw_lnx_g, v_rw_lnx_b, v_sc_conv_w, v_ab_w_out, v_dil_w_qkv, v_dil_w_out, v_rel_bias, v_mlp_w1, v_mlp_w2):
    w = dict(ada_w=ada_w, ada_b=ada_b, ln_g=ln_g, ln_b=ln_b, ab_w_in=ab_w_in, rw_mu=rw_mu, rw_w0=rw_w0,
             rw_w_up=rw_w_up, rw_a0=rw_a0, rw_a_up=rw_a_up, rw_g_up=rw_g_up, rw_k_k=rw_k_k, rw_k_a=rw_k_a,
             rw_r_k=rw_r_k, rw_lnx_g=rw_lnx_g, rw_lnx_b=rw_lnx_b, sc_conv_w=sc_conv_w, ab_w_out=ab_w_out,
             dil_w_qkv=dil_w_qkv, dil_w_out=dil_w_out, rel_bias=rel_bias, mlp_w1=mlp_w1, mlp_w2=mlp_w2)
    m = dict(ada_w=m_ada_w, ada_b=m_ada_b, ln_g=m_ln_g, ln_b=m_ln_b, ab_w_in=m_ab_w_in, rw_mu=m_rw_mu,
             rw_w0=m_rw_w0, rw_w_up=m_rw_w_up, rw_a0=m_rw_a0, rw_a_up=m_rw_a_up, rw_g_up=m_rw_g_up,
             rw_k_k=m_rw_k_k, rw_k_a=m_rw_k_a, rw_r_k=m_rw_r_k, rw_lnx_g=m_rw_lnx_g, rw_lnx_b=m_rw_lnx_b,
             sc_conv_w=m_sc_conv_w, ab_w_out=m_ab_w_out, dil_w_qkv=m_dil_w_qkv, dil_w_out=m_dil_w_out,
             rel_bias=m_rel_bias, mlp_w1=m_mlp_w1, mlp_w2=m_mlp_w2)
    v = dict(ada_w=v_ada_w, ada_b=v_ada_b, ln_g=v_ln_g, ln_b=v_ln_b, ab_w_in=v_ab_w_in, rw_mu=v_rw_mu,
             rw_w0=v_rw_w0, rw_w_up=v_rw_w_up, rw_a0=v_rw_a0, rw_a_up=v_rw_a_up, rw_g_up=v_rw_g_up,
             rw_k_k=v_rw_k_k, rw_k_a=v_rw_k_a, rw_r_k=v_rw_r_k, rw_lnx_g=v_rw_lnx_g, rw_lnx_b=v_rw_lnx_b,
             sc_conv_w=v_sc_conv_w, ab_w_out=v_ab_w_out, dil_w_qkv=v_dil_w_qkv, dil_w_out=v_dil_w_out,
             rel_bias=v_rel_bias, mlp_w1=v_mlp_w1, mlp_w2=v_mlp_w2)
    kinds = dict(SHARDED)
    me = 4 * lax.axis_index("x") + 2 * lax.axis_index("y") + lax.axis_index("c")
    ncol = ada_w.shape[2]

    small = _all_gather(_pack([c] + [w[n] for n in GATHER_F32], F32, 8), "gather_small")
    parts = _unpack(small, [c.shape] + [w[n].shape for n in GATHER_F32], (NDEV,))
    c_all = parts[0].reshape(NDEV, D)
    P = {n: _from_chunks(t, kinds[n]) for n, t in zip(GATHER_F32, parts[1:])}
    P = {n: (t if n in ("ln_g", "ln_b") else t[0]) for n, t in P.items()}
    for n in REPLICATED[1:]:
        P[n] = w[n]
    def full(n, t):
        t = _from_chunks(t, kinds[n])
        return t if n in ("mlp_w1", "mlp_w2") else t[0]

    (first,) = _all_gather_many([ab_w_in.astype(BF16)], "gather_first_weight")
    W = {"ab_w_in": _ab_in_padded(first[:, 0])}

    ada_b_loc = lax.dynamic_slice(ada_b, (0, ncol * me), (2, ncol))
    mod_part = _ada_mod(c_all, ada_w, ada_b_loc)
    mod_all = _all_gather(mod_part.reshape(-1, 128), "gather_mod").reshape(NDEV, 2, NDEV, ncol)
    mod = lax.dynamic_index_in_dim(mod_all, me, axis=2, keepdims=False)
    mod = mod.transpose(1, 0, 2).reshape(2, 6 * D)

    behind = (mod[0, 0] * 0.0).astype(BF16)
    later = _exchange_start([w[n].astype(BF16) + (behind if n == LATER[0] else 0) for n in LATER], True,
                            "gather_later_weights_start")
    mod = mod + later[-1][0, 0]

    def later_weights(after):
        lands = _exchange_wait(later, True, after, "gather_later_weights_wait")
        return {n: full(n, t) for n, t in zip(LATER, lands)}

    sent = []

    def early_grads(G):
        sent.append(_exchange_start([_to_chunks(G[n], kinds[n]).astype(BF16) for n in LATER], False,
                                    "exchange_later_grads_start"))
        return sent[0][-1][0, 0]

    loss_part, grad_x, dmod, G, in_grad = _local_step(x[0], loss_target[0], mod, W, P, later_weights, early_grads)
    G["ada_b"] = dmod
    big_out = {}

    def update(n, contributions):
        cols = w[n].shape[-1]
        flat = lambda t: t.reshape(-1, cols)
        rows = flat(w[n]).shape[0]
        outs = _sum_adamw(contributions.reshape(-1, rows, cols), flat(w[n]), flat(m[n]), flat(v[n]),
                          f"sum_adamw_{n}", min(rows, 256))
        big_out[n] = [o.reshape(w[n].shape) for o in outs]

    rep_shapes = [w[n].shape for n in REPLICATED] + [(1,)]
    rep_all = _all_gather(_pack([G[n] for n in REPLICATED] + [loss_part], F32, 8), "gather_replicated_grads")
    names = [n for n, _ in SHARDED if n not in GATHER_BF16]
    shard_shapes = [w[n].shape for n in names]
    recv = _all_to_all(_pack8([_to_chunks(G[n], kinds[n]) for n in names], F32, 8), "exchange_small_grads")

    behind = (recv[0, 0, 0] * 0.0 + rep_all[0, 0, 0] * 0.0).astype(BF16)
    last = _exchange_start([in_grad().astype(BF16) + behind], False, "exchange_last_grad_start")

    zero = last[-1][0:1, 0]
    pk = lambda d: _pack([d[n] for n in REPLICATED] + [zero], F32, 8)
    rep_out = _sum_adamw(rep_all, pk(w), pk(m), pk(v), "sum_adamw_replicated", rep_all.shape[1])
    loss = _unpack(rep_out[0], rep_shapes)[-1][0]
    rep_out = [dict(zip(REPLICATED, _unpack(o, rep_shapes))) for o in rep_out]
    dmod_all = _unpack(rep_all, [(2, 6 * D)], (NDEV,))[0]
    dmod_loc = lax.dynamic_slice(dmod_all, (0, 0, ncol * me), (NDEV, 2, ncol)).transpose(1, 0, 2)
    ada_out = _ada_grad_adamw(c_all.T + zero, dmod_loc, ada_w, m_ada_w, v_ada_w)
    pk = lambda d: _pack([d[n] for n in names], F32, 8)
    sh_out = _sum_adamw(recv, pk(w), pk(m), pk(v), "sum_adamw_small", recv.shape[1])
    sh_out = [dict(zip(names, _unpack(o, shard_shapes))) for o in sh_out]
    for n, r in zip(LATER, _exchange_wait(sent[0], False, last[-1], "exchange_later_grads_wait")):
        update(n, r)
    (landed,) = _exchange_wait(last, False, big_out[LATER[-1]][0], "exchange_last_grad_wait")
    update("ab_w_in", landed)
    sh_out = [{**d, **{n: big_out[n][i] for n in GATHER_BF16}} for i, d in enumerate(sh_out)]

    def pick(i, n):
        if n == "ada_w":
            return ada_out[i]
        return rep_out[i][n] if n in REPLICATED else sh_out[i][n]

    outs = [loss, grad_x[None]]
    for i in range(4):
        outs += [pick(i, n) for n in WEIGHTS]
    return tuple(outs)
```

```python
import functools
import math

import jax
import jax.numpy as jnp
from jax import lax
from jax.experimental import pallas as pl
from jax.experimental.pallas import tpu as pltpu

F32 = jnp.float32
BF16 = jnp.bfloat16
HI = lax.Precision.HIGHEST

NDEV = 8
T = 2048
D = 1024
DFF = 4096
HEADS = 8
HD = 64
RW = 512
PA = 2048
PB = 1536
PAB = PA + PB
QKV = 4608
DILS = (1, 4, 16)
BLK = 128
ALPHA = 4.0 ** 0.25
LN_EPS = 1e-5
GN_EPS = 64e-5
ADAM_LR, ADAM_B1, ADAM_B2, ADAM_EPS, ADAM_WD, ADAM_STEP = 0.001, 0.9, 0.999, 1e-8, 0.01, 10
VMEM_LIMIT = 56 * 1024 * 1024


def _cp(sem):
    return pltpu.CompilerParams(dimension_semantics=sem, vmem_limit_bytes=VMEM_LIMIT)


def _slot(px, py, pc):
    return 4 * px + 2 * py + pc


def _all_gather(x, name):
    R, C = x.shape

    def body(x_ref, out_ref, send_sems, recv_sems, local_sem):
        xi, yi, ci = lax.axis_index("x"), lax.axis_index("y"), lax.axis_index("c")
        me, sibling = (xi, yi, ci), (xi, yi, 1 - ci)
        chips = [(1 - xi, yi), (xi, 1 - yi), (1 - xi, 1 - yi)]

        def rows(px, py, pc):
            return out_ref.at[_slot(px, py, pc)]

        def copy(k, block, to, src=None):
            return pltpu.make_async_remote_copy(
                src_ref=rows(*block) if src is None else src, dst_ref=rows(*block),
                send_sem=send_sems.at[k], recv_sem=recv_sems.at[k],
                device_id=to, device_id_type=pl.DeviceIdType.MESH)

        mine = pltpu.make_async_copy(x_ref, rows(*me), local_sem)
        mine.start()
        first = [copy(0, me, sibling, src=x_ref)]
        first += [copy(1 + j, me, (*chip, ci), src=x_ref) for j, chip in enumerate(chips)]
        for cp in first:
            cp.start()
        passed = [copy(4 + j, (*chip, ci), sibling) for j, chip in enumerate(chips)]
        for j, chip in enumerate(chips):
            copy(1 + j, (*chip, ci), me).wait_recv()
            passed[j].start()
        copy(0, sibling, me).wait_recv()
        for j, chip in enumerate(chips):
            copy(4 + j, (*chip, 1 - ci), me).wait_recv()
        for cp in first + passed:
            cp.wait_send()
        mine.wait()

    return pl.pallas_call(
        body, name=name,
        out_shape=jax.ShapeDtypeStruct((NDEV, R, C), x.dtype),
        in_specs=[pl.BlockSpec(memory_space=pl.ANY)],
        out_specs=pl.BlockSpec(memory_space=pl.ANY),
        scratch_shapes=[pltpu.SemaphoreType.DMA((7,)), pltpu.SemaphoreType.DMA((7,)),
                        pltpu.SemaphoreType.DMA(())],
    )(x)


def _all_to_all(g, name):
    _, R, C = g.shape

    def body(g_ref, out_ref, send_sems, recv_sems, local_sem):
        xi, yi, ci = lax.axis_index("x"), lax.axis_index("y"), lax.axis_index("c")
        my_slot = _slot(xi, yi, ci)
        mine = pltpu.make_async_copy(g_ref.at[my_slot], out_ref.at[my_slot], local_sem)
        mine.start()
        copies = []
        for k in range(1, 8):
            px = 1 - xi if k & 4 else xi
            py = 1 - yi if k & 2 else yi
            pc = 1 - ci if k & 1 else ci
            peer_slot = _slot(px, py, pc)
            copies.append((
                pltpu.make_async_remote_copy(
                    src_ref=g_ref.at[peer_slot], dst_ref=out_ref.at[my_slot],
                    send_sem=send_sems.at[k - 1], recv_sem=recv_sems.at[k - 1],
                    device_id=(px, py, pc), device_id_type=pl.DeviceIdType.MESH),
                pltpu.make_async_remote_copy(
                    src_ref=g_ref.at[peer_slot], dst_ref=out_ref.at[peer_slot],
                    send_sem=send_sems.at[k - 1], recv_sem=recv_sems.at[k - 1],
                    device_id=(px, py, pc), device_id_type=pl.DeviceIdType.MESH)))
        for send, _ in copies:
            send.start()
        for _, recv in copies:
            recv.wait_recv()
        for send, _ in copies:
            send.wait_send()
        mine.wait()

    return pl.pallas_call(
        body, name=name,
        out_shape=jax.ShapeDtypeStruct((NDEV, R, C), g.dtype),
        in_specs=[pl.BlockSpec(memory_space=pl.ANY)],
        out_specs=pl.BlockSpec(memory_space=pl.ANY),
        scratch_shapes=[pltpu.SemaphoreType.DMA((7,)), pltpu.SemaphoreType.DMA((7,)),
                        pltpu.SemaphoreType.DMA(())],
    )(g)


def _my_slot():
    return _slot(lax.axis_index("x"), lax.axis_index("y"), lax.axis_index("c"))


def _put_own(buf, own, slot):
    return lax.dynamic_update_index_in_dim(buf, own, slot, 0)


def _hbm_call(body, name, ins, out_shapes, n_sems):
    anyspec = pl.BlockSpec(memory_space=pl.ANY)
    return pl.pallas_call(
        body, name=name, out_shape=out_shapes,
        in_specs=[anyspec] * len(ins), out_specs=[anyspec] * len(out_shapes),
        scratch_shapes=[pltpu.SemaphoreType.DMA(s) for s in n_sems],
    )(*ins)


def _all_gather_many(xs, name):
    n = len(xs)

    def body(*refs):
        x_refs, o_refs = refs[:n], refs[n:2 * n]
        send_sems, recv_sems = refs[2 * n:]
        xi, yi, ci = lax.axis_index("x"), lax.axis_index("y"), lax.axis_index("c")
        me, sibling = (xi, yi, ci), (xi, yi, 1 - ci)
        chips = [(1 - xi, yi), (xi, 1 - yi), (1 - xi, 1 - yi)]

        def copy(i, k, block, to, src=None):
            dst = o_refs[i].at[_slot(*block)]
            return pltpu.make_async_remote_copy(
                src_ref=dst if src is None else src, dst_ref=dst,
                send_sem=send_sems.at[i, k], recv_sem=recv_sems.at[i, k],
                device_id=to, device_id_type=pl.DeviceIdType.MESH)

        sends = []
        for i in range(n):
            sends += [copy(i, 1 + j, me, (*chip, ci), src=x_refs[i]) for j, chip in enumerate(chips)]
            sends.append(copy(i, 0, me, sibling, src=x_refs[i]))
        for cp in sends:
            cp.start()
        for j, chip in enumerate(chips):
            for i in range(n):
                copy(i, 1 + j, (*chip, ci), me).wait_recv()
                passed = copy(i, 4 + j, (*chip, ci), sibling)
                passed.start()
                sends.append(passed)
        for i in range(n):
            copy(i, 0, sibling, me).wait_recv()
            for j, chip in enumerate(chips):
                copy(i, 4 + j, (*chip, 1 - ci), me).wait_recv()
        for cp in sends:
            cp.wait_send()

    outs = _hbm_call(body, name, xs, [jax.ShapeDtypeStruct((NDEV,) + x.shape, x.dtype) for x in xs],
                     [(n, 7), (n, 7)])
    return [_put_own(o, x[None], _my_slot()) for o, x in zip(outs, xs)]


def _peers(xi, yi, ci):
    return [(1 - xi if k & 4 else xi, 1 - yi if k & 2 else yi, 1 - ci if k & 1 else ci) for k in range(1, 8)]


def _direct_copy(src_refs, land_refs, send_sems, recv_sems, i, k, peer, my_slot, gather):
    src = src_refs[i] if gather else src_refs[i].at[_slot(*peer)]
    return pltpu.make_async_remote_copy(
        src_ref=src, dst_ref=land_refs[i].at[my_slot], send_sem=send_sems.at[7 * i + k], recv_sem=recv_sems.at[7 * i + k],
        device_id=peer, device_id_type=pl.DeviceIdType.MESH)


def _exchange_start(srcs, gather, name):
    n = len(srcs)
    lands = [lax.empty(((NDEV,) + s.shape) if gather else s.shape, s.dtype) for s in srcs]

    def body(*refs):
        s_refs, l_refs = refs[:n], refs[n:2 * n]
        send_sems, recv_sems = refs[2 * n], refs[2 * n + 1]
        token = refs[2 * n + 2 + 2 * n]
        xi, yi, ci = lax.axis_index("x"), lax.axis_index("y"), lax.axis_index("c")
        my_slot = _slot(xi, yi, ci)
        for k, peer in enumerate(_peers(xi, yi, ci)):
            for i in range(n):
                _direct_copy(s_refs, l_refs, send_sems, recv_sems, i, k, peer, my_slot, gather).start()
        token[...] = jnp.zeros_like(token)

    hbm = pl.BlockSpec(memory_space=pltpu.HBM)
    sem = pl.BlockSpec(memory_space=pltpu.SEMAPHORE)
    both = list(srcs) + lands
    return pl.pallas_call(
        body, name=name,
        out_shape=(pltpu.SemaphoreType.DMA((7 * n,)), pltpu.SemaphoreType.DMA((7 * n,)),
                   *[pltpu.HBM(t.shape, t.dtype) for t in both], jax.ShapeDtypeStruct((8, 128), F32)),
        in_specs=[hbm] * (2 * n),
        out_specs=(sem, sem, *[hbm] * (2 * n), pl.BlockSpec(memory_space=pltpu.VMEM)),
        input_output_aliases={i: 2 + i for i in range(2 * n)},
        compiler_params=pltpu.CompilerParams(has_side_effects=pltpu.SideEffectType.DATAFLOW_SIDE_EFFECTING),
    )(*[pltpu.with_memory_space_constraint(t, pltpu.HBM) for t in both])


def _exchange_wait(started, gather, after, name):
    send_sems, recv_sems, *thru, _ = started
    n = len(thru) // 2

    def body(*refs):
        s_refs, l_refs = refs[:n], refs[n:2 * n]
        send_sems, recv_sems = refs[2 * n], refs[2 * n + 1]
        xi, yi, ci = lax.axis_index("x"), lax.axis_index("y"), lax.axis_index("c")
        my_slot = _slot(xi, yi, ci)
        for k, peer in enumerate(_peers(xi, yi, ci)):
            for i in range(n):
                _direct_copy(s_refs, l_refs, send_sems, recv_sems, i, k, peer, my_slot, gather).wait_send()
                _direct_copy(s_refs, l_refs, send_sems, recv_sems, i, k, peer, _slot(*peer), gather).wait_recv()

    hbm = pl.BlockSpec(memory_space=pltpu.HBM)
    sem = pl.BlockSpec(memory_space=pltpu.SEMAPHORE)
    outs = pl.pallas_call(
        body, name=name,
        out_shape=tuple(pltpu.HBM(t.shape, t.dtype) for t in thru),
        in_specs=[hbm] * (2 * n) + [sem, sem, pl.BlockSpec(memory_space=pl.ANY)],
        out_specs=tuple([hbm] * (2 * n)),
        input_output_aliases={i: i for i in range(2 * n)},
        compiler_params=pltpu.CompilerParams(has_side_effects=pltpu.SideEffectType.DATAFLOW_SIDE_EFFECTING),
    )(*thru, send_sems, recv_sems, after)
    slot = _my_slot()
    own = [s[None] if gather else lax.dynamic_index_in_dim(s, slot, 0, keepdims=True) for s in outs[:n]]
    return [_put_own(land, o, slot) for land, o in zip(outs[n:], own)]


def _mm(name, a, b, tb=False, out=(F32,), epi=None, extras=(), tm=2048, tn=512, tk_cap=2048):
    M, K = a.shape
    N = b.shape[0] if tb else b.shape[1]
    tm, tn = min(tm, M), min(tn, N)
    tk = max(t for t in range(128, min(K, tk_cap) + 1, 128) if K % t == 0)
    assert M % tm == 0 and N % tn == 0 and K % tk == 0, (name, M, N, K)
    nk = K // tk
    ne, no = len(extras), len(out)
    dims = (((1,), (1 if tb else 0,)), ((), ()))
    flipped = [isinstance(o, tuple) for o in out]

    def kern(*refs):
        a_ref, b_ref = refs[:2]
        e_refs = refs[2:2 + ne]
        o_refs = refs[2 + ne:2 + ne + no]

        def finish(acc):
            outs = epi(acc, *[e[...] for e in e_refs]) if epi is not None else (acc,)
            for o_ref, o, flip in zip(o_refs, outs, flipped):
                o_ref[...] = (o.T if flip else o).astype(o_ref.dtype)

        part = lax.dot_general(a_ref[...], b_ref[...], dims, preferred_element_type=F32)
        if nk == 1:
            finish(part)
            return
        acc_ref = refs[-1]
        k = pl.program_id(2)

        @pl.when(k == 0)
        def _():
            acc_ref[...] = part

        @pl.when(k > 0)
        def _():
            acc_ref[...] += part

        @pl.when(k == nk - 1)
        def _():
            finish(acc_ref[...])

    b_spec = (pl.BlockSpec((tn, tk), lambda i, j, k: (j, k)) if tb
              else pl.BlockSpec((tk, tn), lambda i, j, k: (k, j)))
    tile = pl.BlockSpec((tm, tn), lambda i, j, k: (i, j))
    tile_t = pl.BlockSpec((tn, tm), lambda i, j, k: (j, i))
    res = pl.pallas_call(
        kern, name=name, grid=(M // tm, N // tn, nk),
        in_specs=[pl.BlockSpec((tm, tk), lambda i, j, k: (i, k)), b_spec] + [tile] * ne,
        out_specs=[tile_t if flip else tile for flip in flipped],
        out_shape=[jax.ShapeDtypeStruct((N, M), o[0]) if flip else jax.ShapeDtypeStruct((M, N), o)
                   for o, flip in zip(out, flipped)],
        scratch_shapes=[pltpu.VMEM((tm, tn), F32)] if nk > 1 else [],
        compiler_params=_cp(("parallel", "parallel", "arbitrary")),
    )(a, b, *extras)
    return res[0] if no == 1 else res


HALO = 8


def _rows(name, body, rows, params, out_rows, out_accs=(), tm=256):
    views = [r if isinstance(r, tuple) else (r, r.shape[1], 0) for r in rows]
    n = views[0][0].shape[0]
    assert n % tm == 0 and tm % HALO == 0
    nr, npar, nor, noa = len(views), len(params), len(out_rows), len(out_accs)
    flipped = [len(o) == 3 for o in out_rows]

    def row_spec(width, cb, halo=None):
        per, last = tm // HALO, n // HALO - 1
        if halo == "prev":
            return pl.BlockSpec((HALO, width), lambda i: (jnp.maximum(i * per - 1, 0), cb))
        if halo == "next":
            return pl.BlockSpec((HALO, width), lambda i: (jnp.minimum((i + 1) * per, last), cb))
        return pl.BlockSpec((tm, width), lambda i: (i, cb))

    def kern(*refs):
        r_refs = refs[:nr]
        p_refs = refs[nr:nr + npar]
        o_refs = refs[nr + npar:nr + npar + nor]
        a_refs = refs[nr + npar + nor:]
        outs, accs = body([r[...] for r in r_refs], [p[...] for p in p_refs])
        assert len(outs) == nor and len(accs) == noa, (name, len(outs), len(accs))
        for o_ref, o, flip in zip(o_refs, outs, flipped):
            o_ref[...] = (o.T if flip else o).astype(o_ref.dtype)
        if noa:
            @pl.when(pl.program_id(0) == 0)
            def _():
                for a_ref in a_refs:
                    a_ref[...] = jnp.zeros_like(a_ref)

            for a_ref, a in zip(a_refs, accs):
                a_ref[...] += a.astype(F32)

    def whole(shape):
        nd = len(shape)
        return pl.BlockSpec(tuple(shape), lambda i, nd=nd: (0,) * nd)

    in_specs = [row_spec(*v[1:]) for v in views]
    in_specs += [whole(p.shape) for p in params]
    out_specs = [pl.BlockSpec((o[0], tm), lambda i: (0, i)) if flip else pl.BlockSpec((tm, o[0]), lambda i: (i, 0))
                 for o, flip in zip(out_rows, flipped)]
    out_specs += [whole(s) for s in out_accs]
    out_shape = [jax.ShapeDtypeStruct((o[0], n) if flip else (n, o[0]), o[1]) for o, flip in zip(out_rows, flipped)]
    out_shape += [jax.ShapeDtypeStruct(tuple(s), F32) for s in out_accs]
    res = pl.pallas_call(
        kern, name=name, grid=(n // tm,), in_specs=in_specs, out_specs=out_specs,
        out_shape=out_shape, compiler_params=_cp(("arbitrary",)),
    )(*[v[0] for v in views], *params)
    return res[:nor], res[nor:]


def _shift_down(x, prev, k):
    head = jnp.where(pl.program_id(0) == 0, 0.0, pltpu.roll(prev, k, axis=0))
    row = lax.broadcasted_iota(jnp.int32, x.shape, 0)
    return jnp.where(row < k, jnp.tile(head, (x.shape[0] // HALO, 1)), pltpu.roll(x, k, axis=0))


def _shift_up(x, nxt, k):
    n = x.shape[0]
    tail = jnp.where(pl.program_id(0) == pl.num_programs(0) - 1, 0.0, pltpu.roll(nxt, HALO - k, axis=0))
    row = lax.broadcasted_iota(jnp.int32, x.shape, 0)
    return jnp.where(row >= n - k, jnp.tile(tail, (n // HALO, 1)), pltpu.roll(x, n - k, axis=0))


@jax.custom_vjp
def _headsum(x, e):
    return sum(jnp.dot(p, e, preferred_element_type=F32) for p in _split3(x))


_headsum.defvjp(lambda x, e: (_headsum(x, e), e), lambda e, ct: (_headsum(ct, e), None))


def _softplus(z):
    return jnp.maximum(z, 0.0) + jnp.log(1.0 + jnp.exp(jnp.minimum(z, -z)))


def _post_ln(x, y, g, lng, lnb):
    z = ALPHA * x + (1.0 + g) * y
    mu = jnp.mean(z, axis=-1, keepdims=True)
    zc = z - mu
    var = jnp.mean(zc * zc, axis=-1, keepdims=True)
    return zc * lax.rsqrt(var + LN_EPS) * lng + lnb


def _post_ln_mod(x, y, g, lng, lnb, scn, shn):
    xn = _post_ln(x, y, g, lng, lnb)
    return xn, xn * (1.0 + scn) + shn


def _pre_core(E, r_, k_, v_, wd_, ad_, gd_, r1, k1, v1, wd1, ad1, gd1, h, bg, cg, h1, cg1, h2, cg2,
              mu_r, mu_k, mu_v, mu_wd, mu_ad, mu_gd, w0, w_up, a0, a_up, g_up, k_k, k_a,
              cw0, cw1, cw2):
    def mix(x, x1, mu):
        return x + mu * (x1 - x)

    r, k, v = mix(r_, r1, mu_r), mix(k_, k1, mu_k), mix(v_, v1, mu_v)
    wd, ad, gd = mix(wd_, wd1, mu_wd), mix(ad_, ad1, mu_ad), mix(gd_, gd1, mu_gd)
    logw = -_softplus(-(w0 + jnp.dot(jnp.tanh(wd), w_up, preferred_element_type=F32))) - 0.5
    decay = jnp.exp(-jnp.exp(logw))
    iclr = jax.nn.sigmoid(a0 + jnp.dot(ad, a_up, preferred_element_type=F32))
    gate = jnp.dot(jax.nn.sigmoid(gd), g_up, preferred_element_type=F32)
    kk0 = k * k_k
    nrm = jnp.sqrt(_headsum(kk0 * kk0, E))
    kk = kk0 / jnp.maximum(nrm, 1e-12)
    kh = k * (1.0 + (iclr - 1.0) * k_a)
    yb = bg * (cw2 * (cg * h) + cw1 * (cg1 * h1) + cw0 * (cg2 * h2))
    return r, decay, kh, v, -kk, kk * iclr, gate, yb


def _post_core(E, y, r, kh, v, gate, lnx_g, lnx_b, rk):
    def seg(t):
        return _headsum(t, E)

    mean = seg(y) * (1.0 / HD)
    yc = y - mean
    var = seg(yc * yc) * (1.0 / HD)
    gn = yc * lax.rsqrt(var + GN_EPS) * lnx_g + lnx_b
    bonus = seg(r * kh * rk) * v
    return (gn + bonus) * gate


def _merge_core(o0, o1, o2, l0, l1, l2):
    m = jnp.maximum(jnp.maximum(l0, l1), l2)
    e0, e1, e2 = jnp.exp(l0 - m), jnp.exp(l1 - m), jnp.exp(l2 - m)
    den = e0 + e1 + e2
    return (e0 * o0 + e1 * o1 + e2 * o2) / den


CHUNK = 128
HALF = 64
HP = HEADS // 2
LW = 2 * HD
NCHUNK = T // CHUNK


def _split3(x):
    hi = x.astype(BF16)
    r1 = x - hi.astype(F32)
    mid = r1.astype(BF16)
    return hi, mid, (r1 - mid.astype(F32)).astype(BF16)


def _cols3(x, name):
    def kern(x_ref, o_ref):
        xt = x_ref[...].T
        left = lax.broadcasted_iota(jnp.int32, (HD, CHUNK), 1) < HALF
        for p in range(HP):
            a, b = xt[p * LW:p * LW + HD], xt[p * LW + HD:(p + 1) * LW]
            halves = [jnp.where(left, a, pltpu.roll(b, HALF, axis=1)), jnp.where(left, pltpu.roll(a, HALF, axis=1), b)]
            for h, tile in enumerate(halves):
                for j, part in enumerate(_split3(tile)):
                    o_ref[p, :, (3 * h + j) * LW:(3 * h + j + 1) * LW] = part

    return pl.pallas_call(
        kern, name=name, grid=(NCHUNK,),
        in_specs=[pl.BlockSpec((CHUNK, RW), lambda c: (c, 0))],
        out_specs=pl.BlockSpec((HP, HD, 6 * CHUNK), lambda c: (0, 0, c)),
        out_shape=jax.ShapeDtypeStruct((HP, HD, 6 * T), BF16),
        compiler_params=_cp(("parallel",)),
    )(x)


def _pick_codes():
    row = lax.broadcasted_iota(jnp.int32, (6 * HALF, LW), 0)
    col = lax.broadcasted_iota(jnp.int32, (6 * HALF, LW), 1)
    same = ((row & (LW - 1)) >= HALF) == (col >= HD)
    return jnp.where(same, row & (HALF - 1), -1).astype(BF16)


def _column(block_ref, codes, half, i):
    pick = jnp.where(codes == i.astype(BF16), jnp.ones((), BF16), jnp.zeros((), BF16))
    block = block_ref[:, :, half * 6 * HALF:(half + 1) * 6 * HALF].reshape(HP * HD, 6 * HALF)
    return jnp.dot(block, pick, preferred_element_type=F32)


def _halfsums(x, row, left1):
    row_l = jnp.where(left1, row, 0.0)
    return (jnp.sum(x * row_l, axis=1, keepdims=True), jnp.sum(x * (row - row_l), axis=1, keepdims=True))


def _pair_rows(row):
    return [row[:, p * LW:(p + 1) * LW] for p in range(HP)]


def _store_columns(ref, p, t_mask, cols):
    for j, col in enumerate(cols):
        pltpu.store(ref.at[pl.ds(2 * p + j, 1)], jnp.broadcast_to(col[None], (1, HD, CHUNK)), mask=t_mask[None])


def _columns_to_rows(cols_ref, rows_ref):
    for p in range(HP):
        rows_ref[:, p * LW:(p + 1) * LW] = cols_ref[2 * p:2 * p + 2].reshape(LW, CHUNK).T


NHALF = T // HALF
HALVES = CHUNK // HALF


def _scan_fwd(r, w, k, a, b, v3):
    def kern(r_ref, w_ref, k_ref, a_ref, b_ref, v_ref, y_ref, ck_ref, st_hbm, sa_hbm,
             s_ref, vb_ref, yc_ref, st_ref, sa_ref, sems):
        c = pl.program_id(0)

        @pl.when(c == 0)
        def _():
            s_ref[...] = jnp.zeros_like(s_ref)

        lane = lax.broadcasted_iota(jnp.int32, (HD, CHUNK), 1)
        left = lane < HD
        left1 = lax.broadcasted_iota(jnp.int32, (1, LW), 1) < HD
        codes = _pick_codes()

        def flush(slot, half_index):
            return [pltpu.make_async_copy(src.at[slot], dst.at[half_index], sems.at[j, slot])
                    for j, (src, dst) in enumerate(((st_ref, st_hbm), (sa_ref, sa_hbm)))]

        for half in range(HALVES):
            ck_ref[half] = s_ref[...]
            vb_ref[...] = _column(v_ref, codes, half, jnp.int32(0))

            @pl.when(c > 0)
            def _():
                for cp in flush(half, (c - 1) * HALVES + half):
                    cp.wait()

            def step(i, carry):
                t = half * HALF + i
                row = lambda ref: _pair_rows(ref[pl.ds(t, 1), :])
                S = [s_ref[p] for p in range(HP)]
                sa = [jnp.where(left, *_halfsums(s, a, left1)) for s, a in zip(S, row(a_ref))]
                S = [s * w + c_ * b + vb_ref[pl.ds(p * HD, HD), :] * k
                     for p, (s, w, c_, b, k) in enumerate(zip(S, row(w_ref), sa, row(b_ref), row(k_ref)))]
                for p, (s, c_) in enumerate(zip(S, sa)):
                    s_ref[p] = s
                    st_ref[half, i, p] = s
                    sa_ref[half, i, p] = c_
                for p, (s, r) in enumerate(zip(S, row(r_ref))):
                    _store_columns(yc_ref, p, lane == t, _halfsums(s, r, left1))
                vb_ref[...] = _column(v_ref, codes, half, i + 1)
                return carry

            lax.fori_loop(0, HALF, step, 0, unroll=16)
            for cp in flush(half, c * HALVES + half):
                cp.start()
        _columns_to_rows(yc_ref, y_ref)

        @pl.when(c == NCHUNK - 1)
        def _():
            for half in range(HALVES):
                for cp in flush(half, c * HALVES + half):
                    cp.wait()

    rowblk = pl.BlockSpec((CHUNK, RW), lambda c: (c, 0))
    saved = jax.ShapeDtypeStruct((NHALF, HALF, HP, HD, LW), F32)
    stage = pltpu.VMEM((HALVES, HALF, HP, HD, LW), F32)
    return pl.pallas_call(
        kern, name="rwkv_scan_fwd", grid=(NCHUNK,),
        in_specs=[rowblk] * 5 + [pl.BlockSpec((HP, HD, 6 * CHUNK), lambda c: (0, 0, c))],
        out_specs=[rowblk, pl.BlockSpec((HALVES, HP, HD, LW), lambda c: (c, 0, 0, 0)),
                   pl.BlockSpec(memory_space=pl.ANY), pl.BlockSpec(memory_space=pl.ANY)],
        out_shape=[jax.ShapeDtypeStruct((T, RW), F32), jax.ShapeDtypeStruct((NHALF, HP, HD, LW), F32), saved, saved],
        scratch_shapes=[pltpu.VMEM((HP, HD, LW), F32), pltpu.VMEM((HP * HD, LW), F32),
                        pltpu.VMEM((HEADS, HD, CHUNK), F32), stage, stage, pltpu.SemaphoreType.DMA((2, HALVES))],
        compiler_params=_cp(("arbitrary",)),
    )(r, w, k, a, b, v3)


def _scan_bwd(r, w, k, a, b, v3, dy3, ck, st, sa):
    def kern(r_ref, w_ref, k_ref, a_ref, b_ref, v_ref, dy_ref, ck_ref, st_hbm, sa_hbm,
             dr_ref, dw_ref, dk_ref, da_ref, db_ref, dv_ref, ds_ref, sb_ref, sa_ref, pick_ref, dvc_ref, sems):
        c = pl.program_id(0)
        chunk = NCHUNK - 1 - c

        @pl.when(c == 0)
        def _():
            ds_ref[...] = jnp.zeros_like(ds_ref)

        lane = lax.broadcasted_iota(jnp.int32, (HD, CHUNK), 1)
        left = lane < HD
        left1 = lax.broadcasted_iota(jnp.int32, (1, LW), 1) < HD
        codes = _pick_codes()

        def rowsum(x):
            return jnp.sum(x, axis=0, keepdims=True)

        def fetch(slot, half_index):
            return [pltpu.make_async_copy(st_hbm.at[half_index], sb_ref.at[slot, pl.ds(1, HALF)], sems.at[0, slot]),
                    pltpu.make_async_copy(sa_hbm.at[half_index], sa_ref.at[slot], sems.at[1, slot])]

        def picks(half, i):
            pick_ref[pl.ds(0, HP * HD), :] = _column(v_ref, codes, half, i)
            pick_ref[pl.ds(HP * HD, HP * HD), :] = _column(dy_ref, codes, half, i)

        @pl.when(c == 0)
        def _():
            for cp in fetch(HALVES - 1, chunk * HALVES + HALVES - 1):
                cp.start()

        for half in reversed(range(HALVES)):
            base = half * HALF
            for cp in fetch(half, chunk * HALVES + half):
                cp.wait()
            if half:
                for cp in fetch(half - 1, chunk * HALVES + half - 1):
                    cp.start()
            else:
                @pl.when(chunk > 0)
                def _():
                    for cp in fetch(HALVES - 1, chunk * HALVES - 1):
                        cp.start()
            sb_ref[half, 0] = ck_ref[half]
            picks(half, jnp.int32(HALF - 1))

            def back(ii, carry):
                i = HALF - 1 - ii
                t = base + i
                row = lambda ref: _pair_rows(ref[pl.ds(t, 1), :])
                a_r, b_r, k_r, w_r, r_r = row(a_ref), row(b_ref), row(k_ref), row(w_ref), row(r_ref)
                vs = [pick_ref[pl.ds(p * HD, HD), :] for p in range(HP)]
                dys = [pick_ref[pl.ds((HP + p) * HD, HD), :] for p in range(HP)]
                picks(half, jnp.maximum(i - 1, 0))
                dr, dw, db, dk, da = [], [], [], [], []
                for p in range(HP):
                    Sp, dy = sb_ref[half, i, p], dys[p]
                    dS = ds_ref[p] + dy * r_r[p]
                    dr.append(rowsum(sb_ref[half, i + 1, p] * dy))
                    dw.append(rowsum(dS * Sp))
                    db.append(rowsum(dS * sa_ref[half, i, p]))
                    dk.append(rowsum(dS * vs[p]))
                    dsa = jnp.where(left, *_halfsums(dS, b_r[p], left1))
                    _store_columns(dvc_ref, p, lane == t, _halfsums(dS, k_r[p], left1))
                    da.append(rowsum(Sp * dsa))
                    ds_ref[p] = dS * w_r[p] + dsa * a_r[p]
                for ref, pieces in ((dr_ref, dr), (dw_ref, dw), (db_ref, db), (dk_ref, dk), (da_ref, da)):
                    ref[pl.ds(t, 1), :] = jnp.concatenate(pieces, axis=1)
                return carry

            lax.fori_loop(0, HALF, back, 0, unroll=16)
        _columns_to_rows(dvc_ref, dv_ref)

    rowblk = pl.BlockSpec((CHUNK, RW), lambda c: (NCHUNK - 1 - c, 0))
    col3blk = pl.BlockSpec((HP, HD, 6 * CHUNK), lambda c: (0, 0, NCHUNK - 1 - c))
    rowshape = jax.ShapeDtypeStruct((T, RW), F32)
    return pl.pallas_call(
        kern, name="rwkv_scan_bwd", grid=(NCHUNK,),
        in_specs=[rowblk] * 5 + [col3blk, col3blk,
                                 pl.BlockSpec((HALVES, HP, HD, LW), lambda c: (NCHUNK - 1 - c, 0, 0, 0)),
                                 pl.BlockSpec(memory_space=pl.ANY), pl.BlockSpec(memory_space=pl.ANY)],
        out_specs=[rowblk] * 6, out_shape=[rowshape] * 6,
        scratch_shapes=[pltpu.VMEM((HP, HD, LW), F32), pltpu.VMEM((HALVES, HALF + 1, HP, HD, LW), F32),
                        pltpu.VMEM((HALVES, HALF, HP, HD, LW), F32), pltpu.VMEM((2 * HP * HD, LW), F32),
                        pltpu.VMEM((HEADS, HD, CHUNK), F32), pltpu.SemaphoreType.DMA((2, HALVES))],
        compiler_params=_cp(("arbitrary",)),
    )(r, w, k, a, b, v3, dy3, ck, st, sa)


NT = (((1,), (1,)), ((), ()))
TN = (((0,), (0,)), ((), ()))
SCALE = HD ** -0.5
QKV_G = 3 * RW


def _attn_setup(g):
    dil = DILS[g]
    qkv = [pl.BlockSpec((T, LW), lambda hp, c=(g * QKV_G + s * RW) // LW: (0, c + hp)) for s in range(3)]
    tile = pl.BlockSpec((T, LW), lambda hp: (0, hp))
    bias = pl.BlockSpec((2, BLK, 2 * BLK), lambda hp: (hp, 0, 0))

    def blocks():
        for r in range(dil):
            for n in range(T // dil // BLK):
                rows = pl.ds(n * BLK * dil + r, BLK, stride=dil)
                keys = pl.ds((n - 1) * BLK * dil + r, 2 * BLK, stride=dil) if n else rows
                yield n, rows, keys

    return qkv, tile, bias, blocks


def _band(n):
    qi = lax.broadcasted_iota(jnp.int32, (BLK, 2 * BLK), 0)
    ki = lax.broadcasted_iota(jnp.int32, (BLK, 2 * BLK), 1)
    band = (ki >= qi) & (ki <= qi + BLK)
    return band if n else band[:, BLK:]


def _head_masks():
    lane = lax.broadcasted_iota(jnp.int32, (BLK, LW), 1)
    return lane < HD, [(lane < HD).astype(BF16), (lane >= HD).astype(BF16)]


def _attn_fwd(pq, bias, g):
    qkv, tile, bias_spec, blocks = _attn_setup(g)

    def kern(q_ref, k_ref, v_ref, b_ref, o_ref, l_ref):
        left, masks = _head_masks()
        for n, rows, keys in blocks():
            qb, kc, vc = q_ref[rows, :].astype(BF16), k_ref[keys, :].astype(BF16), v_ref[keys, :].astype(BF16)
            valid = _band(n)
            o, lse = [], []
            for j in range(2):
                bias_j = b_ref[j] if n else b_ref[j][:, BLK:]
                s = lax.dot_general(qb * masks[j], kc, NT, preferred_element_type=F32) * SCALE + bias_j
                s = jnp.where(valid, s, -jnp.inf)
                m = jnp.max(s, axis=1, keepdims=True)
                e = jnp.exp(s - m)
                den = jnp.sum(e, axis=1, keepdims=True)
                o.append(jnp.dot((e / den).astype(BF16), vc, preferred_element_type=F32))
                lse.append(m + jnp.log(den))
            o_ref[rows, :] = jnp.where(left, o[0], o[1])
            l_ref[rows, :] = jnp.where(left, lse[0], lse[1])

    shape = jax.ShapeDtypeStruct((T, RW), F32)
    return pl.pallas_call(
        kern, name=f"attn_fwd_{g}", grid=(HP,),
        in_specs=qkv + [bias_spec], out_specs=[tile, tile], out_shape=[shape, shape],
        compiler_params=_cp(("parallel",)),
    )(pq, pq, pq, bias)


def _attn_bwd(pq, bias, do, o, lse, dlse, g):
    qkv, tile, bias_spec, blocks = _attn_setup(g)

    def kern(q_ref, k_ref, v_ref, b_ref, do_ref, o_ref, l_ref, dl_ref, dq_ref, dk_ref, dv_ref, db_ref):
        left, masks = _head_masks()
        lane = lax.broadcasted_iota(jnp.int32, (BLK, LW), 1)
        dk_ref[...] = jnp.zeros_like(dk_ref)
        dv_ref[...] = jnp.zeros_like(dv_ref)
        db_ref[...] = jnp.zeros_like(db_ref)

        def column(tile_, j):
            return jnp.sum(jnp.where(lane == j * HD, tile_, 0.0), axis=1, keepdims=True)

        for n, rows, keys in blocks():
            qb, kc, vc = q_ref[rows, :].astype(BF16), k_ref[keys, :].astype(BF16), v_ref[keys, :].astype(BF16)
            dof, valid = do_ref[rows, :], _band(n)
            dob, prod = dof.astype(BF16), dof * o_ref[rows, :]
            dq = []
            for j in range(2):
                bias_j = b_ref[j] if n else b_ref[j][:, BLK:]
                delta = jnp.sum(prod * masks[j].astype(F32), axis=1, keepdims=True)
                qm, dom = qb * masks[j], dob * masks[j]
                s = lax.dot_general(qm, kc, NT, preferred_element_type=F32) * SCALE + bias_j
                p = jnp.where(valid, jnp.exp(s - column(l_ref[rows, :], j)), 0.0)
                dp = lax.dot_general(dom, vc, NT, preferred_element_type=F32)
                ds = p * (dp + (column(dl_ref[rows, :], j) - delta))
                if n:
                    db_ref[j] += ds
                else:
                    db_ref[j, :, BLK:] += ds
                dsb = (ds * SCALE).astype(BF16)
                dq.append(jnp.dot(dsb, kc, preferred_element_type=F32))
                dk_ref[keys, :] += lax.dot_general(dsb, qm, TN, preferred_element_type=F32)
                dv_ref[keys, :] += lax.dot_general(p.astype(BF16), dom, TN, preferred_element_type=F32)
            dq_ref[rows, :] = jnp.where(left, dq[0], dq[1])

    shape = jax.ShapeDtypeStruct((T, RW), F32)
    return pl.pallas_call(
        kern, name=f"attn_bwd_{g}", grid=(HP,),
        in_specs=qkv + [bias_spec] + [tile] * 4, out_specs=[tile] * 3 + [bias_spec],
        out_shape=[shape] * 3 + [jax.ShapeDtypeStruct((HEADS, BLK, 2 * BLK), F32)],
        compiler_params=_cp(("parallel",)),
    )(pq, pq, pq, bias, do, o, lse, dlse)


NBUCKET = 32
NPAIR = BLK * 2 * BLK


def _relbias_table(rbT, onehotT):
    def kern(rb_ref, oh_ref, out_ref):
        out_ref[0] = sum(jnp.dot(p, oh_ref[0], preferred_element_type=F32) for p in _split3(rb_ref[0]))

    return pl.pallas_call(
        kern, name="relbias_table", grid=(3,),
        in_specs=[pl.BlockSpec((1, HEADS, NBUCKET), lambda g: (g, 0, 0)),
                  pl.BlockSpec((1, NBUCKET, NPAIR), lambda g: (g, 0, 0))],
        out_specs=pl.BlockSpec((1, HEADS, NPAIR), lambda g: (g, 0, 0)),
        out_shape=jax.ShapeDtypeStruct((3, HEADS, NPAIR), F32),
        compiler_params=_cp(("parallel",)),
    )(rbT, onehotT)


def _relbias_grad(db, onehotT):
    nt = (((1,), (1,)), ((), ()))

    def kern(db_ref, oh_ref, out_ref):
        hi, mid, _ = _split3(db_ref[0])
        out_ref[0] = (lax.dot_general(hi, oh_ref[0], nt, preferred_element_type=F32)
                      + lax.dot_general(mid, oh_ref[0], nt, preferred_element_type=F32))

    return pl.pallas_call(
        kern, name="relbias_grad", grid=(3,),
        in_specs=[pl.BlockSpec((1, HEADS, NPAIR), lambda g: (g, 0, 0)),
                  pl.BlockSpec((1, NBUCKET, NPAIR), lambda g: (g, 0, 0))],
        out_specs=pl.BlockSpec((1, HEADS, NBUCKET), lambda g: (g, 0, 0)),
        out_shape=jax.ShapeDtypeStruct((3, HEADS, NBUCKET), F32),
        compiler_params=_cp(("parallel",)),
    )(db, onehotT)


def _adamw(w, g, m, v):
    m2 = ADAM_B1 * m + (1.0 - ADAM_B1) * g
    v2 = ADAM_B2 * v + (1.0 - ADAM_B2) * (g * g)
    m_hat = m2 / (1.0 - ADAM_B1 ** ADAM_STEP)
    v_hat = v2 / (1.0 - ADAM_B2 ** ADAM_STEP)
    return -ADAM_LR * (m_hat / (jnp.sqrt(v_hat) + ADAM_EPS) + ADAM_WD * w), m2, v2


def _ada_mod(c_all, ada_w, ada_b_loc):
    def kern(c_ref, w_ref, b_ref, o_ref):
        c = c_ref[...]
        cond = c * jax.nn.sigmoid(c)
        o_ref[0] = jnp.dot(cond, w_ref[0], precision=HI, preferred_element_type=F32) + b_ref[0]

    ncol = ada_w.shape[2]
    return pl.pallas_call(
        kern, name="ada_mod", grid=(2,),
        in_specs=[pl.BlockSpec((NDEV, D), lambda i: (0, 0)),
                  pl.BlockSpec((1, D, ncol), lambda i: (i, 0, 0)),
                  pl.BlockSpec((1, 1, ncol), lambda i: (i, 0, 0))],
        out_specs=pl.BlockSpec((1, NDEV, ncol), lambda i: (i, 0, 0)),
        out_shape=jax.ShapeDtypeStruct((2, NDEV, ncol), F32),
        compiler_params=_cp(("parallel",)),
    )(c_all, ada_w, ada_b_loc.reshape(2, 1, ncol))


def _ada_grad_adamw(cT_all, dmod_loc, w, m, v):
    ncol = w.shape[2]
    tr = 256

    def kern(c_ref, d_ref, w_ref, m_ref, v_ref, g_ref, dl_ref, m2_ref, v2_ref):
        c = c_ref[...]
        cond = c * jax.nn.sigmoid(c)
        g = jnp.dot(cond, d_ref[0], precision=HI, preferred_element_type=F32)
        dl, m2, v2 = _adamw(w_ref[0], g, m_ref[0], v_ref[0])
        g_ref[0], dl_ref[0], m2_ref[0], v2_ref[0] = g, dl, m2, v2

    big = pl.BlockSpec((1, tr, ncol), lambda i, j: (i, j, 0))
    shp = jax.ShapeDtypeStruct(w.shape, F32)
    return pl.pallas_call(
        kern, name="ada_grad_adamw", grid=(2, D // tr),
        in_specs=[pl.BlockSpec((tr, NDEV), lambda i, j: (j, 0)),
                  pl.BlockSpec((1, NDEV, ncol), lambda i, j: (i, 0, 0)), big, big, big],
        out_specs=[big] * 4, out_shape=[shp] * 4,
        compiler_params=_cp(("parallel", "parallel")),
    )(cT_all, dmod_loc, w, m, v)


def _sum_adamw(recv, w, m, v, name, tr):
    S = recv.shape[0]
    R, C = w.shape
    assert R % tr == 0 and recv.shape[1:] == (R, C)

    def kern(r_ref, w_ref, m_ref, v_ref, g_ref, dl_ref, m2_ref, v2_ref):
        g = r_ref[0].astype(F32)
        for s in range(1, S):
            g = g + r_ref[s].astype(F32)
        dl, m2, v2 = _adamw(w_ref[...], g, m_ref[...], v_ref[...])
        g_ref[...], dl_ref[...], m2_ref[...], v2_ref[...] = g, dl, m2, v2

    flat = pl.BlockSpec((tr, C), lambda i: (i, 0))
    shp = jax.ShapeDtypeStruct((R, C), F32)
    return pl.pallas_call(
        kern, name=name, grid=(R // tr,),
        in_specs=[pl.BlockSpec((S, tr, C), lambda i: (0, i, 0)), flat, flat, flat],
        out_specs=[flat] * 4, out_shape=[shp] * 4,
        compiler_params=_cp(("parallel",)),
    )(recv, w, m, v)


def _pack(arrs, dtype, row_mult):
    flat = jnp.concatenate([a.reshape(-1).astype(dtype) for a in arrs])
    flat = jnp.pad(flat, (0, -flat.shape[0] % (128 * row_mult)))
    return flat.reshape(-1, 128)


def _pack8(arrs, dtype, row_mult):
    flat = jnp.concatenate([a.reshape(NDEV, -1).astype(dtype) for a in arrs], axis=1)
    flat = jnp.pad(flat, ((0, 0), (0, -flat.shape[1] % (128 * row_mult))))
    return flat.reshape(NDEV, -1, 128)


def _unpack(buf, shapes, lead=()):
    flat = buf.reshape(lead + (-1,))
    out, off = [], 0
    for s in shapes:
        n = math.prod(s)
        out.append(flat[..., off:off + n].reshape(lead + tuple(s)))
        off += n
    return out


def _to_chunks(full, kind):
    if kind == "col":
        x = full.reshape(full.shape[:-1] + (NDEV, full.shape[-1] // NDEV))
        return jnp.moveaxis(x, -2, 0)
    x = full.reshape(full.shape[:-2] + (NDEV, full.shape[-2] // NDEV, full.shape[-1]))
    return jnp.moveaxis(x, -3, 0)


def _from_chunks(g8, kind):
    if kind == "col":
        x = jnp.moveaxis(g8, 0, -2)
        return x.reshape(x.shape[:-2] + (x.shape[-2] * x.shape[-1],))
    x = jnp.moveaxis(g8, 0, -3)
    return x.reshape(x.shape[:-3] + (x.shape[-3] * x.shape[-2], x.shape[-1]))


def _pad_pa(x):
    z = lambda n: jnp.zeros(x.shape[:-1] + (n,), x.dtype)
    return jnp.concatenate([x[..., :1600], z(64), x[..., 1600:1664], z(64), x[..., 1664:1824], z(96)], -1)


def _unpad_pa(x):
    return jnp.concatenate([x[..., :1600], x[..., 1664:1728], x[..., 1792:1952]], -1)


AB_SEGMENTS = ((0, 1600, 0), (1600, 1664, 64), (1664, 1824, 128), (1824, 3360, PAB - 3360))
AB_SHARD = 3360 // NDEV


def _ab_in_padded(g8):
    blocks, at = [], 0
    for start, end, shift in AB_SEGMENTS:
        if start + shift > at:
            blocks.append(jnp.zeros((g8.shape[1], start + shift - at), g8.dtype))
        for j in range(start // AB_SHARD, (end - 1) // AB_SHARD + 1):
            lo, hi = max(start, j * AB_SHARD), min(end, (j + 1) * AB_SHARD)
            blocks.append(g8[j, :, lo - j * AB_SHARD:hi - j * AB_SHARD])
        at = end + shift
    return jnp.concatenate(blocks, axis=1)


def _ab_in_shards(padded):
    shards = []
    for j in range(NDEV):
        pieces = [padded[:, max(start, j * AB_SHARD) + shift:min(end, (j + 1) * AB_SHARD) + shift]
                  for start, end, shift in AB_SEGMENTS if max(start, j * AB_SHARD) < min(end, (j + 1) * AB_SHARD)]
        shards.append(jnp.concatenate(pieces, axis=1))
    return jnp.stack(shards)


def _pad_rows(x, n):
    return jnp.pad(x, ((0, n - x.shape[0]), (0, 0)))


def _bucket_tables():
    qi = jnp.arange(BLK)[:, None]
    ki = jnp.arange(2 * BLK)[None, :]
    rel = BLK + qi - ki
    tabs = []
    for dil in DILS:
        dist = jnp.clip(rel, 0, BLK) * dil
        logd = jnp.log(jnp.maximum(dist, 1).astype(F32) / 16) / math.log(2048 / 16)
        large = jnp.minimum(16 + (logd * 16).astype(jnp.int32), 31)
        tabs.append(jnp.where(dist < 16, dist, large))
    return jnp.stack(tabs)


SHARDED = (("ln_g", "col"), ("ln_b", "col"), ("ab_w_in", "col"), ("rw_w_up", "col"), ("rw_a_up", "col"),
           ("rw_g_up", "col"), ("sc_conv_w", "col"), ("ab_w_out", "row"), ("dil_w_qkv", "col"),
           ("dil_w_out", "col"), ("mlp_w1", "col"), ("mlp_w2", "row"))
FIRST = ("ab_w_in",)
LATER = ("ab_w_out", "dil_w_qkv", "dil_w_out", "mlp_w1", "mlp_w2")
GATHER_BF16 = FIRST + LATER
GATHER_F32 = ("rw_w_up", "rw_a_up", "rw_g_up", "sc_conv_w", "ln_g", "ln_b")
REPLICATED = ("ada_b", "rw_mu", "rw_w0", "rw_a0", "rw_k_k", "rw_k_a", "rw_r_k", "rw_lnx_g", "rw_lnx_b", "rel_bias")
WEIGHTS = ("ada_w", "ada_b", "ln_g", "ln_b", "ab_w_in", "rw_mu", "rw_w0", "rw_w_up", "rw_a0", "rw_a_up",
           "rw_g_up", "rw_k_k", "rw_k_a", "rw_r_k", "rw_lnx_g", "rw_lnx_b", "sc_conv_w", "ab_w_out",
           "dil_w_qkv", "dil_w_out", "rel_bias", "mlp_w1", "mlp_w2")


def _local_step(x0, tgt, mod, W, P, later_weights, early_grads):
    row = lambda a: a.reshape(1, -1)
    W = dict(W)
    m6 = mod.reshape(2, 6, 1, D)
    sc = [m6[0, 1], m6[0, 4], m6[1, 1], m6[1, 4]]
    sh = [m6[0, 0], m6[0, 3], m6[1, 0], m6[1, 3]]
    gt = [m6[0, 2], m6[0, 5], m6[1, 2], m6[1, 5]]
    lng = [row(P["ln_g"][0, 0]), row(P["ln_g"][0, 1]), row(P["ln_g"][1, 0]), row(P["ln_g"][1, 1])]
    lnb = [row(P["ln_b"][0, 0]), row(P["ln_b"][0, 1]), row(P["ln_b"][1, 0]), row(P["ln_b"][1, 1])]
    E = jnp.kron(jnp.eye(HEADS, dtype=BF16), jnp.ones((HD, HD), BF16))

    def mod_body(r, p):
        u = r[0] * (1.0 + p[0]) + p[1]
        return [u, u], []

    (u0, u0T), _ = _rows("modulate", mod_body, [x0], [sc[0], sh[0]], [(D, BF16), (D, BF16, "T")])

    def post_fwd_body(r, p):
        xn, un = _post_ln_mod(r[0], r[1], *p)
        return [xn, un, un], []

    def post_fwd(s, x, y):
        (xn, un, unT), _ = _rows(f"post_ln_{s}", post_fwd_body, [x, y],
                                 [gt[s], lng[s], lnb[s], sc[s + 1], sh[s + 1]],
                                 [(D, F32), (D, BF16), (D, BF16, "T")])
        return xn, un, unT

    def relu2(acc):
        a = jnp.maximum(acc, 0.0)
        return acc, a * a, a * a

    def relu2_bwd(acc, h):
        return (acc * (2.0 * jnp.maximum(h, 0.0)),)

    p = _mm("ab_in", u0, W["ab_w_in"])
    mu = _pad_pa(P["rw_mu"])
    mu_parts = [mu[:, :512], mu[:, 512:1024], mu[:, 1024:1536], mu[:, 1536:1664], mu[:, 1664:1792], mu[:, 1792:]]
    pre_params = mu_parts + [P["rw_w0"], _pad_rows(P["rw_w_up"], 128), P["rw_a0"], _pad_rows(P["rw_a_up"], 128),
                             _pad_rows(P["rw_g_up"], 256), P["rw_k_k"], P["rw_k_a"],
                             P["sc_conv_w"][0:1], P["sc_conv_w"][1:2], P["sc_conv_w"][2:3]]
    pieces = [(p, 512, 0), (p, 512, 1), (p, 512, 2), (p, 128, 12), (p, 128, 13), (p, 256, 7),
              (p, 512, 4), (p, 512, 5), (p, 512, 6)]
    shifted = [0, 1, 2, 3, 4, 5, 6, 8]
    pre_rows = pieces + [pieces[i] + ("prev",) for i in shifted]
    NPR = 19

    def pre_args(r):
        x, prev = r[:9], dict(zip(shifted, r[9:17]))
        down = lambda i, k: _shift_down(x[i], prev[i], k)
        return x[:6] + [down(i, 1) for i in range(6)] + x[6:9] + [down(6, 1), down(8, 1), down(6, 2), down(8, 2)]

    def pre_fwd_body(r, pp):
        return list(_pre_core(pp[0], *pre_args(r), *pp[1:])), []

    (r_, w_, kh_, v_, a_, b_, gate_, yb), _ = _rows(
        "rwkv_pre", pre_fwd_body, pre_rows, [E] + pre_params, [(RW, F32)] * 7 + [(RW, BF16)], tm=256)
    scan_in = [r_, w_, kh_, a_, b_, _cols3(v_, "rwkv_v_columns")]
    ysc, *saved = _scan_fwd(*scan_in)
    post_params = [P["rw_lnx_g"], P["rw_lnx_b"], P["rw_r_k"].reshape(1, RW)]

    def postmix_fwd_body(r, pp):
        return [_post_core(pp[0], *r, *pp[1:])], []

    (ya,), _ = _rows("rwkv_post", postmix_fwd_body, [ysc, r_, kh_, v_, gate_], [E] + post_params,
                     [(RW, BF16)], tm=256)
    cat = jnp.concatenate([ya, yb], axis=1)
    W.update(later_weights(cat))
    y0 = _mm("ab_out", cat, W["ab_w_out"])
    x1, u1, u1T = post_fwd(0, x0, y0)

    h1, a1, a1T = _mm("mlp1_up_0", u1, W["mlp_w1"][0], out=(F32, BF16, (BF16, "T")), epi=relu2)
    y1 = _mm("mlp1_down_0", a1, W["mlp_w2"][0])
    x2, u2, u2T = post_fwd(1, x1, y1)

    pq = _mm("qkv", u2, W["dil_w_qkv"])
    onehotT = (_bucket_tables().reshape(3, 1, NPAIR) == jnp.arange(NBUCKET).reshape(1, NBUCKET, 1)).astype(BF16)
    rbT = P["rel_bias"].reshape(NBUCKET, 3, HEADS).transpose(1, 2, 0)
    bias = _relbias_table(rbT, onehotT).reshape(3, HEADS, BLK, 2 * BLK)
    og, lse = zip(*[_attn_fwd(pq, bias[g], g) for g in range(3)])

    def merge_fwd_body(r, pp):
        return [_merge_core(*r)], []

    (om,), _ = _rows("attn_merge", merge_fwd_body, list(og + lse), [], [(RW, BF16)])
    y2 = _mm("dil_out", om, W["dil_w_out"])
    x3, u3, u3T = post_fwd(2, x2, y2)

    h3, a3, a3T = _mm("mlp1_up_1", u3, W["mlp_w1"][1], out=(F32, BF16, (BF16, "T")), epi=relu2)
    y3 = _mm("mlp1_down_1", a3, W["mlp_w2"][1])

    def last_body(r, pp):
        x, y, tg = r
        xn, vjp = jax.vjp(_post_ln, x, y, *pp)
        err = xn - tg
        dx, dy, dg, dlg, dlb = vjp(err * (1.0 / D))
        loss = jnp.full((1, 128), (0.5 / D) * jnp.sum(err * err), F32)
        return [dx, dy], [loss, dg, dlg, dlb]

    (dxp, dy3), (loss_acc, dg3, dlng3, dlnb3) = _rows(
        "final_ln_loss", last_body, [x3, y3, tgt], [gt[3], lng[3], lnb[3]],
        [(D, F32), (D, BF16)], [(1, 128), (1, D), (1, D), (1, D)])

    G = {}
    dsc, dsh, dgt = [None] * 4, [None] * 4, [None] * 4
    dlng, dlnb = [None] * 4, [None] * 4
    dgt[3], dlng[3], dlnb[3] = dg3, dlng3, dlnb3

    def mlp_bwd(i, uT, h, aT, dy):
        dh = _mm(f"mlp_dh_{i}", dy, W["mlp_w2"][i], tb=True, out=(BF16,), epi=relu2_bwd, extras=(h,))
        gw2 = _mm(f"mlp_dw2_{i}", aT, dy)
        du = _mm(f"mlp_du_{i}", dh, W["mlp_w1"][i], tb=True)
        gw1 = _mm(f"mlp_dw1_{i}", uT, dh)
        return du, gw1, gw2

    def post_bwd_body(r, pp):
        x, y, dxn, dun = r
        _, vjp = jax.vjp(_post_ln_mod, x, y, *pp)
        dx, dy, dg, dlg, dlb, dscn, dshn = vjp((dxn, dun))
        return [dx, dy], [dg, dlg, dlb, dscn, dshn]

    def post_bwd(s, x, y, dxn, dun):
        (dx, dy), (dgt[s], dlng[s], dlnb[s], dsc[s + 1], dsh[s + 1]) = _rows(
            f"post_ln_bwd_{s}", post_bwd_body, [x, y, dxn, dun],
            [gt[s], lng[s], lnb[s], sc[s + 1], sh[s + 1]], [(D, F32), (D, BF16)], [(1, D)] * 5)
        return dx, dy

    du3, gw1_1, gw2_1 = mlp_bwd(1, u3T, h3, a3T, dy3)
    dxp, dy2 = post_bwd(2, x2, y2, dxp, du3)

    G["dil_w_out"] = _mm("dil_out_dw", om.T, dy2)[None]
    do = _mm("dil_out_dx", dy2, W["dil_w_out"], tb=True)

    def merge_bwd_body(r, pp):
        _, vjp = jax.vjp(_merge_core, *r[:6])
        d = vjp(r[6])
        return list(d[:3]) + [_headsum(d[3 + g], pp[0]) for g in range(3)], []

    mb, _ = _rows("attn_merge_bwd", merge_bwd_body, list(og + lse) + [do], [E],
                  [(RW, F32)] * 6)
    back = [_attn_bwd(pq, bias[g], mb[g], og[g], lse[g], mb[3 + g], g) for g in range(3)]
    dpq = jnp.concatenate([t for dq, dk, dv, _ in back for t in (dq, dk, dv)], axis=1).astype(BF16)
    rb = _relbias_grad(jnp.stack([b[3] for b in back]).reshape(3, HEADS, NPAIR), onehotT)
    G["rel_bias"] = rb.transpose(2, 0, 1).reshape(NBUCKET, 3 * HEADS)
    G["dil_w_qkv"] = _mm("qkv_dw", u2T, dpq)[None]
    du2 = _mm("qkv_dx", dpq, W["dil_w_qkv"], tb=True)
    dxp, dy1 = post_bwd(1, x1, y1, dxp, du2)

    du1, gw1_0, gw2_0 = mlp_bwd(0, u1T, h1, a1T, dy1)
    G["mlp_w1"] = jnp.stack([gw1_0, gw1_1])
    G["mlp_w2"] = jnp.stack([gw2_0, gw2_1])
    dxp, dy0 = post_bwd(0, x0, y0, dxp, du1)

    G["ab_w_out"] = _mm("ab_out_dw", cat.T, dy0)[None]
    dcat = _mm("ab_out_dx", dy0, W["ab_w_out"], tb=True)
    post_params = [post_params[0] + early_grads(G)] + post_params[1:]

    def postmix_bwd_body(r, pp):
        _, vjp = jax.vjp(functools.partial(_post_core, pp[0]), *r[:5], *pp[1:])
        d = vjp(r[5])
        return list(d[:5]), list(d[5:])

    (dysc, dr1, dkh1, dv1, dgate), (G["rw_lnx_g"], G["rw_lnx_b"], drk) = _rows(
        "rwkv_post_bwd", postmix_bwd_body, [ysc, r_, kh_, v_, gate_, (dcat, 512, 0)], [E] + post_params,
        [(RW, F32)] * 5, [(1, RW)] * 3, tm=256)
    G["rw_r_k"] = drk.reshape(1, HEADS, HD)
    dr2, dw2, dk2, da2, db2, dv2 = _scan_bwd(*scan_in, _cols3(dysc, "rwkv_dy_columns"), *saved)

    def pre_bwd_body(r, pp):
        prim, ct = pre_args(r[:len(pre_rows)]), r[len(pre_rows):]
        _, vjp = jax.vjp(functools.partial(_pre_core, pp[0]), *prim, *pp[1:])
        cts = (ct[0] + ct[1], ct[2], ct[3] + ct[4], ct[5] + ct[6], ct[7], ct[8], ct[9], ct[10])
        d = vjp(cts)
        z = jnp.zeros_like(d[12])
        dp = jnp.concatenate([d[0], d[1], d[2], d[3], d[4], d[5], d[12], d[13], d[14]], axis=1)
        dp1 = jnp.concatenate([d[6], d[7], d[8], d[9], d[10], d[11], d[15], z, d[16]], axis=1)
        dp2 = jnp.concatenate([d[17], z, d[18]], axis=1)
        return [dp, dp1, dp2], list(d[NPR:])

    acc_shapes = [a.shape for a in pre_params]
    (dp, dp1, dp2), pacc = _rows(
        "rwkv_pre_bwd", pre_bwd_body,
        pre_rows + [dr1, dr2, dw2, dkh1, dk2, dv1, dv2, da2, db2, dgate, (dcat, 512, 1)],
        [E] + pre_params, [(PAB, F32), (PAB, F32), (PB, F32)], acc_shapes, tm=256)
    G["rw_mu"] = _unpad_pa(jnp.concatenate(pacc[:6], axis=1))
    G["rw_w0"], G["rw_a0"], G["rw_k_k"], G["rw_k_a"] = pacc[6], pacc[8], pacc[11], pacc[12]
    G["rw_w_up"] = pacc[7][None, :64]
    G["rw_a_up"] = pacc[9][None, :64]
    G["rw_g_up"] = pacc[10][None, :160]
    G["sc_conv_w"] = jnp.concatenate(pacc[13:16], axis=0)[None]

    def shift_merge_body(r, pp):
        d0, d1, d1_next, d2, d2_next = r
        d = d0 + _shift_up(d1, d1_next, 1)
        return [jnp.concatenate([d[:, :PA], d[:, PA:] + _shift_up(d2, d2_next, 2)], axis=1)], []

    (dpt,), _ = _rows("shift_merge", shift_merge_body,
                      [dp, dp1, (dp1, PAB, 0, "next"), dp2, (dp2, PB, 0, "next")], [], [(PAB, BF16)])
    du0 = _mm("ab_in_dx", dpt, W["ab_w_in"], tb=True)

    def mod_bwd_body(r, pp):
        du, dx, x = r
        return [dx + du * (1.0 + pp[0])], [jnp.sum(du * x, axis=0, keepdims=True), jnp.sum(du, axis=0, keepdims=True)]

    (grad_x,), (dsc[0], dsh[0]) = _rows("modulate_bwd", mod_bwd_body, [du0, dxp, x0], [sc[0]], [(D, F32)],
                                        [(1, D), (1, D)])

    G["ln_g"] = jnp.concatenate(dlng, axis=0).reshape(2, 2, D)
    G["ln_b"] = jnp.concatenate(dlnb, axis=0).reshape(2, 2, D)
    dmod = jnp.concatenate([dsh[0], dsc[0], dgt[0], dsh[1], dsc[1], dgt[1],
                            dsh[2], dsc[2], dgt[2], dsh[3], dsc[3], dgt[3]], axis=1).reshape(2, 6 * D)
    return loss_acc[0, 0], grad_x, dmod, G, lambda: _ab_in_shards(_mm("ab_in_dw", u0T, dpt))[:, None]


def kernel(x, c, ada_w, ada_b, ln_g, ln_b, ab_w_in, rw_mu, rw_w0, rw_w_up, rw_a0, rw_a_up, rw_g_up, rw_k_k, rw_k_a, rw_r_k, rw_lnx_g, rw_lnx_b, sc_conv_w, ab_w_out, dil_w_qkv, dil_w_out, rel_bias, mlp_w1, mlp_w2, loss_target, m_ada_w, m_ada_b, m_ln_g, m_ln_b, m_ab_w_in, m_rw_mu, m_rw_w0, m_rw_w_up, m_rw_a0, m_rw_a_up, m_rw_g_up, m_rw_k_k, m_rw_k_a, m_rw_r_k, m_rw_lnx_g, m_rw_lnx_b, m_sc_conv_w, m_ab_w_out, m_dil_w_qkv, m_dil_w_out, m_rel_bias, m_mlp_w1, m_mlp_w2, v_ada_w, v_ada_b, v_ln_g, v_ln_b, v_ab_w_in, v_rw_mu, v_rw_w0, v_rw_w_up, v_rw_a0, v_rw_a_up, v_rw_g_up, v_rw_k_k, v_rw_k_a, v_rw_r_k, v_rw_lnx_g, v_rw_lnx_b, v_sc_conv_w, v_ab_w_out, v_dil_w_qkv, v_dil_w_out, v_rel_bias, v_mlp_w1, v_mlp_w2):
    w = dict(ada_w=ada_w, ada_b=ada_b, ln_g=ln_g, ln_b=ln_b, ab_w_in=ab_w_in, rw_mu=rw_mu, rw_w0=rw_w0,
             rw_w_up=rw_w_up, rw_a0=rw_a0, rw_a_up=rw_a_up, rw_g_up=rw_g_up, rw_k_k=rw_k_k, rw_k_a=rw_k_a,
             rw_r_k=rw_r_k, rw_lnx_g=rw_lnx_g, rw_lnx_b=rw_lnx_b, sc_conv_w=sc_conv_w, ab_w_out=ab_w_out,
             dil_w_qkv=dil_w_qkv, dil_w_out=dil_w_out, rel_bias=rel_bias, mlp_w1=mlp_w1, mlp_w2=mlp_w2)
    m = dict(ada_w=m_ada_w, ada_b=m_ada_b, ln_g=m_ln_g, ln_b=m_ln_b, ab_w_in=m_ab_w_in, rw_mu=m_rw_mu,
             rw_w0=m_rw_w0, rw_w_up=m_rw_w_up, rw_a0=m_rw_a0, rw_a_up=m_rw_a_up, rw_g_up=m_rw_g_up,
             rw_k_k=m_rw_k_k, rw_k_a=m_rw_k_a, rw_r_k=m_rw_r_k, rw_lnx_g=m_rw_lnx_g, rw_lnx_b=m_rw_lnx_b,
             sc_conv_w=m_sc_conv_w, ab_w_out=m_ab_w_out, dil_w_qkv=m_dil_w_qkv, dil_w_out=m_dil_w_out,
             rel_bias=m_rel_bias, mlp_w1=m_mlp_w1, mlp_w2=m_mlp_w2)
    v = dict(ada_w=v_ada_w, ada_b=v_ada_b, ln_g=v_ln_g, ln_b=v_ln_b, ab_w_in=v_ab_w_in, rw_mu=v_rw_mu,
             rw_w0=v_rw_w0, rw_w_up=v_rw_w_up, rw_a0=v_rw_a0, rw_a_up=v_rw_a_up, rw_g_up=v_rw_g_up,
             rw_k_k=v_rw_k_k, rw_k_a=v_rw_k_a, rw_r_k=v_rw_r_k, rw_lnx_g=v_rw_lnx_g, rw_lnx_b=v_rw_lnx_b,
             sc_conv_w=v_sc_conv_w, ab_w_out=v_ab_w_out, dil_w_qkv=v_dil_w_qkv, dil_w_out=v_dil_w_out,
             rel_bias=v_rel_bias, mlp_w1=v_mlp_w1, mlp_w2=v_mlp_w2)
    kinds = dict(SHARDED)
    me = 4 * lax.axis_index("x") + 2 * lax.axis_index("y") + lax.axis_index("c")
    ncol = ada_w.shape[2]

    small = _all_gather(_pack([c] + [w[n] for n in GATHER_F32], F32, 8), "gather_small")
    parts = _unpack(small, [c.shape] + [w[n].shape for n in GATHER_F32], (NDEV,))
    c_all = parts[0].reshape(NDEV, D)
    P = {n: _from_chunks(t, kinds[n]) for n, t in zip(GATHER_F32, parts[1:])}
    P = {n: (t if n in ("ln_g", "ln_b") else t[0]) for n, t in P.items()}
    for n in REPLICATED[1:]:
        P[n] = w[n]
    def full(n, t):
        t = _from_chunks(t, kinds[n])
        return t if n in ("mlp_w1", "mlp_w2") else t[0]

    (first,) = _all_gather_many([ab_w_in.astype(BF16)], "gather_first_weight")
    W = {"ab_w_in": _ab_in_padded(first[:, 0])}

    ada_b_loc = lax.dynamic_slice(ada_b, (0, ncol * me), (2, ncol))
    mod_part = _ada_mod(c_all, ada_w, ada_b_loc)
    mod_all = _all_gather(mod_part.reshape(-1, 128), "gather_mod").reshape(NDEV, 2, NDEV, ncol)
    mod = lax.dynamic_index_in_dim(mod_all, me, axis=2, keepdims=False)
    mod = mod.transpose(1, 0, 2).reshape(2, 6 * D)

    behind = (mod[0, 0] * 0.0).astype(BF16)
    later = _exchange_start([w[n].astype(BF16) + (behind if n == LATER[0] else 0) for n in LATER], True,
                            "gather_later_weights_start")
    mod = mod + later[-1][0, 0]

    def later_weights(after):
        lands = _exchange_wait(later, True, after, "gather_later_weights_wait")
        return {n: full(n, t) for n, t in zip(LATER, lands)}

    sent = []

    def early_grads(G):
        sent.append(_exchange_start([_to_chunks(G[n], kinds[n]).astype(BF16) for n in LATER], False,
                                    "exchange_later_grads_start"))
        return sent[0][-1][0, 0]

    loss_part, grad_x, dmod, G, in_grad = _local_step(x[0], loss_target[0], mod, W, P, later_weights, early_grads)
    G["ada_b"] = dmod
    big_out = {}

    def update(n, contributions):
        cols = w[n].shape[-1]
        flat = lambda t: t.reshape(-1, cols)
        rows = flat(w[n]).shape[0]
        outs = _sum_adamw(contributions.reshape(-1, rows, cols), flat(w[n]), flat(m[n]), flat(v[n]),
                          f"sum_adamw_{n}", min(rows, 256))
        big_out[n] = [o.reshape(w[n].shape) for o in outs]

    rep_shapes = [w[n].shape for n in REPLICATED] + [(1,)]
    rep_all = _all_gather(_pack([G[n] for n in REPLICATED] + [loss_part], F32, 8), "gather_replicated_grads")
    names = [n for n, _ in SHARDED if n not in GATHER_BF16]
    shard_shapes = [w[n].shape for n in names]
    recv = _all_to_all(_pack8([_to_chunks(G[n], kinds[n]) for n in names], F32, 8), "exchange_small_grads")

    behind = (recv[0, 0, 0] * 0.0 + rep_all[0, 0, 0] * 0.0).astype(BF16)
    last = _exchange_start([in_grad().astype(BF16) + behind], False, "exchange_last_grad_start")

    zero = last[-1][0:1, 0]
    pk = lambda d: _pack([d[n] for n in REPLICATED] + [zero], F32, 8)
    rep_out = _sum_adamw(rep_all, pk(w), pk(m), pk(v), "sum_adamw_replicated", rep_all.shape[1])
    loss = _unpack(rep_out[0], rep_shapes)[-1][0]
    rep_out = [dict(zip(REPLICATED, _unpack(o, rep_shapes))) for o in rep_out]
    dmod_all = _unpack(rep_all, [(2, 6 * D)], (NDEV,))[0]
    dmod_loc = lax.dynamic_slice(dmod_all, (0, 0, ncol * me), (NDEV, 2, ncol)).transpose(1, 0, 2)
    ada_out = _ada_grad_adamw(c_all.T + zero, dmod_loc, ada_w, m_ada_w, v_ada_w)
    pk = lambda d: _pack([d[n] for n in names], F32, 8)
    sh_out = _sum_adamw(recv, pk(w), pk(m), pk(v), "sum_adamw_small", recv.shape[1])
    sh_out = [dict(zip(names, _unpack(o, shard_shapes))) for o in sh_out]
    for n, r in zip(LATER, _exchange_wait(sent[0], False, last[-1], "exchange_later_grads_wait")):
        update(n, r)
    (landed,) = _exchange_wait(last, False, big_out[LATER[-1]][0], "exchange_last_grad_wait")
    update("ab_w_in", landed)
    sh_out = [{**d, **{n: big_out[n][i] for n in GATHER_BF16}} for i, d in enumerate(sh_out)]

    def pick(i, n):
        if n == "ada_w":
            return ada_out[i]
        return rep_out[i][n] if n in REPLICATED else sh_out[i][n]

    outs = [loss, grad_x[None]]
    for i in range(4):
        outs += [pick(i, n) for n in WEIGHTS]
    return tuple(outs)
```

```python
import functools
import math

import jax
import jax.numpy as jnp
from jax import lax
from jax.experimental import pallas as pl
from jax.experimental.pallas import tpu as pltpu

F32 = jnp.float32
BF16 = jnp.bfloat16
HI = lax.Precision.HIGHEST

NDEV = 8
T = 2048
D = 1024
DFF = 4096
HEADS = 8
HD = 64
RW = 512
PA = 2048
PB = 1536
PAB = PA + PB
QKV = 4608
DILS = (1, 4, 16)
BLK = 128
ALPHA = 4.0 ** 0.25
LN_EPS = 1e-5
GN_EPS = 64e-5
ADAM_LR, ADAM_B1, ADAM_B2, ADAM_EPS, ADAM_WD, ADAM_STEP = 0.001, 0.9, 0.999, 1e-8, 0.01, 10
VMEM_LIMIT = 56 * 1024 * 1024


def _cp(sem):
    return pltpu.CompilerParams(dimension_semantics=sem, vmem_limit_bytes=VMEM_LIMIT)


def _slot(px, py, pc):
    return 4 * px + 2 * py + pc


def _all_gather(x, name):
    R, C = x.shape

    def body(x_ref, out_ref, send_sems, recv_sems, local_sem):
        xi, yi, ci = lax.axis_index("x"), lax.axis_index("y"), lax.axis_index("c")
        me, sibling = (xi, yi, ci), (xi, yi, 1 - ci)
        chips = [(1 - xi, yi), (xi, 1 - yi), (1 - xi, 1 - yi)]

        def rows(px, py, pc):
            return out_ref.at[_slot(px, py, pc)]

        def copy(k, block, to, src=None):
            return pltpu.make_async_remote_copy(
                src_ref=rows(*block) if src is None else src, dst_ref=rows(*block),
                send_sem=send_sems.at[k], recv_sem=recv_sems.at[k],
                device_id=to, device_id_type=pl.DeviceIdType.MESH)

        mine = pltpu.make_async_copy(x_ref, rows(*me), local_sem)
        mine.start()
        first = [copy(0, me, sibling, src=x_ref)]
        first += [copy(1 + j, me, (*chip, ci), src=x_ref) for j, chip in enumerate(chips)]
        for cp in first:
            cp.start()
        passed = [copy(4 + j, (*chip, ci), sibling) for j, chip in enumerate(chips)]
        for j, chip in enumerate(chips):
            copy(1 + j, (*chip, ci), me).wait_recv()
            passed[j].start()
        copy(0, sibling, me).wait_recv()
        for j, chip in enumerate(chips):
            copy(4 + j, (*chip, 1 - ci), me).wait_recv()
        for cp in first + passed:
            cp.wait_send()
        mine.wait()

    return pl.pallas_call(
        body, name=name,
        out_shape=jax.ShapeDtypeStruct((NDEV, R, C), x.dtype),
        in_specs=[pl.BlockSpec(memory_space=pl.ANY)],
        out_specs=pl.BlockSpec(memory_space=pl.ANY),
        scratch_shapes=[pltpu.SemaphoreType.DMA((7,)), pltpu.SemaphoreType.DMA((7,)),
                        pltpu.SemaphoreType.DMA(())],
    )(x)


def _all_to_all(g, name):
    _, R, C = g.shape

    def body(g_ref, out_ref, send_sems, recv_sems, local_sem):
        xi, yi, ci = lax.axis_index("x"), lax.axis_index("y"), lax.axis_index("c")
        my_slot = _slot(xi, yi, ci)
        mine = pltpu.make_async_copy(g_ref.at[my_slot], out_ref.at[my_slot], local_sem)
        mine.start()
        copies = []
        for k in range(1, 8):
            px = 1 - xi if k & 4 else xi
            py = 1 - yi if k & 2 else yi
            pc = 1 - ci if k & 1 else ci
            peer_slot = _slot(px, py, pc)
            copies.append((
                pltpu.make_async_remote_copy(
                    src_ref=g_ref.at[peer_slot], dst_ref=out_ref.at[my_slot],
                    send_sem=send_sems.at[k - 1], recv_sem=recv_sems.at[k - 1],
                    device_id=(px, py, pc), device_id_type=pl.DeviceIdType.MESH),
                pltpu.make_async_remote_copy(
                    src_ref=g_ref.at[peer_slot], dst_ref=out_ref.at[peer_slot],
                    send_sem=send_sems.at[k - 1], recv_sem=recv_sems.at[k - 1],
                    device_id=(px, py, pc), device_id_type=pl.DeviceIdType.MESH)))
        for send, _ in copies:
            send.start()
        for _, recv in copies:
            recv.wait_recv()
        for send, _ in copies:
            send.wait_send()
        mine.wait()

    return pl.pallas_call(
        body, name=name,
        out_shape=jax.ShapeDtypeStruct((NDEV, R, C), g.dtype),
        in_specs=[pl.BlockSpec(memory_space=pl.ANY)],
        out_specs=pl.BlockSpec(memory_space=pl.ANY),
        scratch_shapes=[pltpu.SemaphoreType.DMA((7,)), pltpu.SemaphoreType.DMA((7,)),
                        pltpu.SemaphoreType.DMA(())],
    )(g)


def _my_slot():
    return _slot(lax.axis_index("x"), lax.axis_index("y"), lax.axis_index("c"))


def _put_own(buf, own, slot):
    return lax.dynamic_update_index_in_dim(buf, own, slot, 0)


def _hbm_call(body, name, ins, out_shapes, n_sems):
    anyspec = pl.BlockSpec(memory_space=pl.ANY)
    return pl.pallas_call(
        body, name=name, out_shape=out_shapes,
        in_specs=[anyspec] * len(ins), out_specs=[anyspec] * len(out_shapes),
        scratch_shapes=[pltpu.SemaphoreType.DMA(s) for s in n_sems],
    )(*ins)


def _all_gather_many(xs, name):
    n = len(xs)

    def body(*refs):
        x_refs, o_refs = refs[:n], refs[n:2 * n]
        send_sems, recv_sems = refs[2 * n:]
        xi, yi, ci = lax.axis_index("x"), lax.axis_index("y"), lax.axis_index("c")
        me, sibling = (xi, yi, ci), (xi, yi, 1 - ci)
        chips = [(1 - xi, yi), (xi, 1 - yi), (1 - xi, 1 - yi)]

        def copy(i, k, block, to, src=None):
            dst = o_refs[i].at[_slot(*block)]
            return pltpu.make_async_remote_copy(
                src_ref=dst if src is None else src, dst_ref=dst,
                send_sem=send_sems.at[i, k], recv_sem=recv_sems.at[i, k],
                device_id=to, device_id_type=pl.DeviceIdType.MESH)

        sends = []
        for i in range(n):
            sends += [copy(i, 1 + j, me, (*chip, ci), src=x_refs[i]) for j, chip in enumerate(chips)]
            sends.append(copy(i, 0, me, sibling, src=x_refs[i]))
        for cp in sends:
            cp.start()
        for j, chip in enumerate(chips):
            for i in range(n):
                copy(i, 1 + j, (*chip, ci), me).wait_recv()
                passed = copy(i, 4 + j, (*chip, ci), sibling)
                passed.start()
                sends.append(passed)
        for i in range(n):
            copy(i, 0, sibling, me).wait_recv()
            for j, chip in enumerate(chips):
                copy(i, 4 + j, (*chip, 1 - ci), me).wait_recv()
        for cp in sends:
            cp.wait_send()

    outs = _hbm_call(body, name, xs, [jax.ShapeDtypeStruct((NDEV,) + x.shape, x.dtype) for x in xs],
                     [(n, 7), (n, 7)])
    return [_put_own(o, x[None], _my_slot()) for o, x in zip(outs, xs)]


def _peers(xi, yi, ci):
    return [(1 - xi if k & 4 else xi, 1 - yi if k & 2 else yi, 1 - ci if k & 1 else ci) for k in range(1, 8)]


def _direct_copy(src_refs, land_refs, send_sems, recv_sems, i, k, peer, my_slot, gather):
    src = src_refs[i] if gather else src_refs[i].at[_slot(*peer)]
    return pltpu.make_async_remote_copy(
        src_ref=src, dst_ref=land_refs[i].at[my_slot], send_sem=send_sems.at[7 * i + k], recv_sem=recv_sems.at[7 * i + k],
        device_id=peer, device_id_type=pl.DeviceIdType.MESH)


def _exchange_start(srcs, gather, name):
    n = len(srcs)
    lands = [lax.empty(((NDEV,) + s.shape) if gather else s.shape, s.dtype) for s in srcs]

    def body(*refs):
        s_refs, l_refs = refs[:n], refs[n:2 * n]
        send_sems, recv_sems = refs[2 * n], refs[2 * n + 1]
        token = refs[2 * n + 2 + 2 * n]
        xi, yi, ci = lax.axis_index("x"), lax.axis_index("y"), lax.axis_index("c")
        my_slot = _slot(xi, yi, ci)
        for k, peer in enumerate(_peers(xi, yi, ci)):
            for i in range(n):
                _direct_copy(s_refs, l_refs, send_sems, recv_sems, i, k, peer, my_slot, gather).start()
        token[...] = jnp.zeros_like(token)

    hbm = pl.BlockSpec(memory_space=pltpu.HBM)
    sem = pl.BlockSpec(memory_space=pltpu.SEMAPHORE)
    both = list(srcs) + lands
    return pl.pallas_call(
        body, name=name,
        out_shape=(pltpu.SemaphoreType.DMA((7 * n,)), pltpu.SemaphoreType.DMA((7 * n,)),
                   *[pltpu.HBM(t.shape, t.dtype) for t in both], jax.ShapeDtypeStruct((8, 128), F32)),
        in_specs=[hbm] * (2 * n),
        out_specs=(sem, sem, *[hbm] * (2 * n), pl.BlockSpec(memory_space=pltpu.VMEM)),
        input_output_aliases={i: 2 + i for i in range(2 * n)},
        compiler_params=pltpu.CompilerParams(has_side_effects=pltpu.SideEffectType.DATAFLOW_SIDE_EFFECTING),
    )(*[pltpu.with_memory_space_constraint(t, pltpu.HBM) for t in both])


def _exchange_wait(started, gather, after, name):
    send_sems, recv_sems, *thru, _ = started
    n = len(thru) // 2

    def body(*refs):
        s_refs, l_refs = refs[:n], refs[n:2 * n]
        send_sems, recv_sems = refs[2 * n], refs[2 * n + 1]
        xi, yi, ci = lax.axis_index("x"), lax.axis_index("y"), lax.axis_index("c")
        my_slot = _slot(xi, yi, ci)
        for k, peer in enumerate(_peers(xi, yi, ci)):
            for i in range(n):
                _direct_copy(s_refs, l_refs, send_sems, recv_sems, i, k, peer, my_slot, gather).wait_send()
                _direct_copy(s_refs, l_refs, send_sems, recv_sems, i, k, peer, _slot(*peer), gather).wait_recv()

    hbm = pl.BlockSpec(memory_space=pltpu.HBM)
    sem = pl.BlockSpec(memory_space=pltpu.SEMAPHORE)
    outs = pl.pallas_call(
        body, name=name,
        out_shape=tuple(pltpu.HBM(t.shape, t.dtype) for t in thru),
        in_specs=[hbm] * (2 * n) + [sem, sem, pl.BlockSpec(memory_space=pl.ANY)],
        out_specs=tuple([hbm] * (2 * n)),
        input_output_aliases={i: i for i in range(2 * n)},
        compiler_params=pltpu.CompilerParams(has_side_effects=pltpu.SideEffectType.DATAFLOW_SIDE_EFFECTING),
    )(*thru, send_sems, recv_sems, after)
    slot = _my_slot()
    own = [s[None] if gather else lax.dynamic_index_in_dim(s, slot, 0, keepdims=True) for s in outs[:n]]
    return [_put_own(land, o, slot) for land, o in zip(outs[n:], own)]


def _mm(name, a, b, tb=False, out=(F32,), epi=None, extras=(), tm=2048, tn=512, tk_cap=2048):
    M, K = a.shape
    N = b.shape[0] if tb else b.shape[1]
    tm, tn = min(tm, M), min(tn, N)
    tk = max(t for t in range(128, min(K, tk_cap) + 1, 128) if K % t == 0)
    assert M % tm == 0 and N % tn == 0 and K % tk == 0, (name, M, N, K)
    nk = K // tk
    ne, no = len(extras), len(out)
    dims = (((1,), (1 if tb else 0,)), ((), ()))
    flipped = [isinstance(o, tuple) for o in out]

    def kern(*refs):
        a_ref, b_ref = refs[:2]
        e_refs = refs[2:2 + ne]
        o_refs = refs[2 + ne:2 + ne + no]

        def finish(acc):
            outs = epi(acc, *[e[...] for e in e_refs]) if epi is not None else (acc,)
            for o_ref, o, flip in zip(o_refs, outs, flipped):
                o_ref[...] = (o.T if flip else o).astype(o_ref.dtype)

        part = lax.dot_general(a_ref[...], b_ref[...], dims, preferred_element_type=F32)
        if nk == 1:
            finish(part)
            return
        acc_ref = refs[-1]
        k = pl.program_id(2)

        @pl.when(k == 0)
        def _():
            acc_ref[...] = part

        @pl.when(k > 0)
        def _():
            acc_ref[...] += part

        @pl.when(k == nk - 1)
        def _():
            finish(acc_ref[...])

    b_spec = (pl.BlockSpec((tn, tk), lambda i, j, k: (j, k)) if tb
              else pl.BlockSpec((tk, tn), lambda i, j, k: (k, j)))
    tile = pl.BlockSpec((tm, tn), lambda i, j, k: (i, j))
    tile_t = pl.BlockSpec((tn, tm), lambda i, j, k: (j, i))
    res = pl.pallas_call(
        kern, name=name, grid=(M // tm, N // tn, nk),
        in_specs=[pl.BlockSpec((tm, tk), lambda i, j, k: (i, k)), b_spec] + [tile] * ne,
        out_specs=[tile_t if flip else tile for flip in flipped],
        out_shape=[jax.ShapeDtypeStruct((N, M), o[0]) if flip else jax.ShapeDtypeStruct((M, N), o)
                   for o, flip in zip(out, flipped)],
        scratch_shapes=[pltpu.VMEM((tm, tn), F32)] if nk > 1 else [],
        compiler_params=_cp(("parallel", "parallel", "arbitrary")),
    )(a, b, *extras)
    return res[0] if no == 1 else res


HALO = 8


def _rows(name, body, rows, params, out_rows, out_accs=(), tm=512):
    views = [r if isinstance(r, tuple) else (r, r.shape[1], 0) for r in rows]
    n = views[0][0].shape[0]
    assert n % tm == 0 and tm % HALO == 0
    nr, npar, nor, noa = len(views), len(params), len(out_rows), len(out_accs)
    flipped = [len(o) == 3 for o in out_rows]

    def row_spec(width, cb, halo=None):
        per, last = tm // HALO, n // HALO - 1
        if halo == "prev":
            return pl.BlockSpec((HALO, width), lambda i: (jnp.maximum(i * per - 1, 0), cb))
        if halo == "next":
            return pl.BlockSpec((HALO, width), lambda i: (jnp.minimum((i + 1) * per, last), cb))
        return pl.BlockSpec((tm, width), lambda i: (i, cb))

    def kern(*refs):
        r_refs = refs[:nr]
        p_refs = refs[nr:nr + npar]
        o_refs = refs[nr + npar:nr + npar + nor]
        a_refs = refs[nr + npar + nor:]
        outs, accs = body([r[...] for r in r_refs], [p[...] for p in p_refs])
        assert len(outs) == nor and len(accs) == noa, (name, len(outs), len(accs))
        for o_ref, o, flip in zip(o_refs, outs, flipped):
            o_ref[...] = (o.T if flip else o).astype(o_ref.dtype)
        if noa:
            @pl.when(pl.program_id(0) == 0)
            def _():
                for a_ref in a_refs:
                    a_ref[...] = jnp.zeros_like(a_ref)

            for a_ref, a in zip(a_refs, accs):
                a_ref[...] += a.astype(F32)

    def whole(shape):
        nd = len(shape)
        return pl.BlockSpec(tuple(shape), lambda i, nd=nd: (0,) * nd)

    in_specs = [row_spec(*v[1:]) for v in views]
    in_specs += [whole(p.shape) for p in params]
    out_specs = [pl.BlockSpec((o[0], tm), lambda i: (0, i)) if flip else pl.BlockSpec((tm, o[0]), lambda i: (i, 0))
                 for o, flip in zip(out_rows, flipped)]
    out_specs += [whole(s) for s in out_accs]
    out_shape = [jax.ShapeDtypeStruct((o[0], n) if flip else (n, o[0]), o[1]) for o, flip in zip(out_rows, flipped)]
    out_shape += [jax.ShapeDtypeStruct(tuple(s), F32) for s in out_accs]
    res = pl.pallas_call(
        kern, name=name, grid=(n // tm,), in_specs=in_specs, out_specs=out_specs,
        out_shape=out_shape, compiler_params=_cp(("arbitrary",)),
    )(*[v[0] for v in views], *params)
    return res[:nor], res[nor:]


def _shift_down(x, prev, k):
    head = jnp.where(pl.program_id(0) == 0, 0.0, pltpu.roll(prev, k, axis=0))
    row = lax.broadcasted_iota(jnp.int32, x.shape, 0)
    return jnp.where(row < k, jnp.tile(head, (x.shape[0] // HALO, 1)), pltpu.roll(x, k, axis=0))


def _shift_up(x, nxt, k):
    n = x.shape[0]
    tail = jnp.where(pl.program_id(0) == pl.num_programs(0) - 1, 0.0, pltpu.roll(nxt, HALO - k, axis=0))
    row = lax.broadcasted_iota(jnp.int32, x.shape, 0)
    return jnp.where(row >= n - k, jnp.tile(tail, (n // HALO, 1)), pltpu.roll(x, n - k, axis=0))


@jax.custom_vjp
def _headsum(x, e):
    return sum(jnp.dot(p, e, preferred_element_type=F32) for p in _split3(x))


_headsum.defvjp(lambda x, e: (_headsum(x, e), e), lambda e, ct: (_headsum(ct, e), None))


def _softplus(z):
    return jnp.maximum(z, 0.0) + jnp.log(1.0 + jnp.exp(jnp.minimum(z, -z)))


def _post_ln(x, y, g, lng, lnb):
    z = ALPHA * x + (1.0 + g) * y
    mu = jnp.mean(z, axis=-1, keepdims=True)
    zc = z - mu
    var = jnp.mean(zc * zc, axis=-1, keepdims=True)
    return zc * lax.rsqrt(var + LN_EPS) * lng + lnb


def _post_ln_mod(x, y, g, lng, lnb, scn, shn):
    xn = _post_ln(x, y, g, lng, lnb)
    return xn, xn * (1.0 + scn) + shn


def _pre_core(E, r_, k_, v_, wd_, ad_, gd_, r1, k1, v1, wd1, ad1, gd1, h, bg, cg, h1, cg1, h2, cg2,
              mu_r, mu_k, mu_v, mu_wd, mu_ad, mu_gd, w0, w_up, a0, a_up, g_up, k_k, k_a,
              cw0, cw1, cw2):
    def mix(x, x1, mu):
        return x + mu * (x1 - x)

    r, k, v = mix(r_, r1, mu_r), mix(k_, k1, mu_k), mix(v_, v1, mu_v)
    wd, ad, gd = mix(wd_, wd1, mu_wd), mix(ad_, ad1, mu_ad), mix(gd_, gd1, mu_gd)
    logw = -_softplus(-(w0 + jnp.dot(jnp.tanh(wd), w_up, preferred_element_type=F32))) - 0.5
    decay = jnp.exp(-jnp.exp(logw))
    iclr = jax.nn.sigmoid(a0 + jnp.dot(ad, a_up, preferred_element_type=F32))
    gate = jnp.dot(jax.nn.sigmoid(gd), g_up, preferred_element_type=F32)
    kk0 = k * k_k
    nrm = jnp.sqrt(_headsum(kk0 * kk0, E))
    kk = kk0 / jnp.maximum(nrm, 1e-12)
    kh = k * (1.0 + (iclr - 1.0) * k_a)
    yb = bg * (cw2 * (cg * h) + cw1 * (cg1 * h1) + cw0 * (cg2 * h2))
    return r, decay, kh, v, -kk, kk * iclr, gate, yb


def _post_core(E, y, r, kh, v, gate, lnx_g, lnx_b, rk):
    def seg(t):
        return _headsum(t, E)

    mean = seg(y) * (1.0 / HD)
    yc = y - mean
    var = seg(yc * yc) * (1.0 / HD)
    gn = yc * lax.rsqrt(var + GN_EPS) * lnx_g + lnx_b
    bonus = seg(r * kh * rk) * v
    return (gn + bonus) * gate


def _merge_core(o0, o1, o2, l0, l1, l2):
    m = jnp.maximum(jnp.maximum(l0, l1), l2)
    e0, e1, e2 = jnp.exp(l0 - m), jnp.exp(l1 - m), jnp.exp(l2 - m)
    den = e0 + e1 + e2
    return (e0 * o0 + e1 * o1 + e2 * o2) / den


CHUNK = 128
HALF = 64
HP = HEADS // 2
LW = 2 * HD
NCHUNK = T // CHUNK


def _split3(x):
    hi = x.astype(BF16)
    r1 = x - hi.astype(F32)
    mid = r1.astype(BF16)
    return hi, mid, (r1 - mid.astype(F32)).astype(BF16)


def _cols3(x, name):
    def kern(x_ref, o_ref):
        xt = x_ref[...].T
        left = lax.broadcasted_iota(jnp.int32, (HD, CHUNK), 1) < HALF
        for p in range(HP):
            a, b = xt[p * LW:p * LW + HD], xt[p * LW + HD:(p + 1) * LW]
            halves = [jnp.where(left, a, pltpu.roll(b, HALF, axis=1)), jnp.where(left, pltpu.roll(a, HALF, axis=1), b)]
            for h, tile in enumerate(halves):
                for j, part in enumerate(_split3(tile)):
                    o_ref[p, :, (3 * h + j) * LW:(3 * h + j + 1) * LW] = part

    return pl.pallas_call(
        kern, name=name, grid=(NCHUNK,),
        in_specs=[pl.BlockSpec((CHUNK, RW), lambda c: (c, 0))],
        out_specs=pl.BlockSpec((HP, HD, 6 * CHUNK), lambda c: (0, 0, c)),
        out_shape=jax.ShapeDtypeStruct((HP, HD, 6 * T), BF16),
        compiler_params=_cp(("parallel",)),
    )(x)


def _pick_codes():
    row = lax.broadcasted_iota(jnp.int32, (6 * HALF, LW), 0)
    col = lax.broadcasted_iota(jnp.int32, (6 * HALF, LW), 1)
    same = ((row & (LW - 1)) >= HALF) == (col >= HD)
    return jnp.where(same, row & (HALF - 1), -1).astype(BF16)


def _column(block_ref, codes, half, i):
    pick = jnp.where(codes == i.astype(BF16), jnp.ones((), BF16), jnp.zeros((), BF16))
    block = block_ref[:, :, half * 6 * HALF:(half + 1) * 6 * HALF].reshape(HP * HD, 6 * HALF)
    return jnp.dot(block, pick, preferred_element_type=F32)


def _halfsums(x, row, left1):
    row_l = jnp.where(left1, row, 0.0)
    return (jnp.sum(x * row_l, axis=1, keepdims=True), jnp.sum(x * (row - row_l), axis=1, keepdims=True))


def _pair_rows(row):
    return [row[:, p * LW:(p + 1) * LW] for p in range(HP)]


def _store_columns(ref, p, t_mask, cols):
    for j, col in enumerate(cols):
        pltpu.store(ref.at[pl.ds(2 * p + j, 1)], jnp.broadcast_to(col[None], (1, HD, CHUNK)), mask=t_mask[None])


def _columns_to_rows(cols_ref, rows_ref):
    for p in range(HP):
        rows_ref[:, p * LW:(p + 1) * LW] = cols_ref[2 * p:2 * p + 2].reshape(LW, CHUNK).T


NHALF = T // HALF
HALVES = CHUNK // HALF


def _scan_fwd(r, w, k, a, b, v3):
    def kern(r_ref, w_ref, k_ref, a_ref, b_ref, v_ref, y_ref, ck_ref, st_hbm, sa_hbm,
             s_ref, vb_ref, yc_ref, st_ref, sa_ref, sems):
        c = pl.program_id(0)

        @pl.when(c == 0)
        def _():
            s_ref[...] = jnp.zeros_like(s_ref)

        lane = lax.broadcasted_iota(jnp.int32, (HD, CHUNK), 1)
        left = lane < HD
        left1 = lax.broadcasted_iota(jnp.int32, (1, LW), 1) < HD
        codes = _pick_codes()

        def flush(slot, half_index):
            return [pltpu.make_async_copy(src.at[slot], dst.at[half_index], sems.at[j, slot])
                    for j, (src, dst) in enumerate(((st_ref, st_hbm), (sa_ref, sa_hbm)))]

        for half in range(HALVES):
            ck_ref[half] = s_ref[...]
            vb_ref[...] = _column(v_ref, codes, half, jnp.int32(0))

            @pl.when(c > 0)
            def _():
                for cp in flush(half, (c - 1) * HALVES + half):
                    cp.wait()

            def step(i, carry):
                t = half * HALF + i
                row = lambda ref: _pair_rows(ref[pl.ds(t, 1), :])
                S = [s_ref[p] for p in range(HP)]
                sa = [jnp.where(left, *_halfsums(s, a, left1)) for s, a in zip(S, row(a_ref))]
                S = [s * w + c_ * b + vb_ref[pl.ds(p * HD, HD), :] * k
                     for p, (s, w, c_, b, k) in enumerate(zip(S, row(w_ref), sa, row(b_ref), row(k_ref)))]
                for p, (s, c_) in enumerate(zip(S, sa)):
                    s_ref[p] = s
                    st_ref[half, i, p] = s
                    sa_ref[half, i, p] = c_
                for p, (s, r) in enumerate(zip(S, row(r_ref))):
                    _store_columns(yc_ref, p, lane == t, _halfsums(s, r, left1))
                vb_ref[...] = _column(v_ref, codes, half, i + 1)
                return carry

            lax.fori_loop(0, HALF, step, 0, unroll=16)
            for cp in flush(half, c * HALVES + half):
                cp.start()
        _columns_to_rows(yc_ref, y_ref)

        @pl.when(c == NCHUNK - 1)
        def _():
            for half in range(HALVES):
                for cp in flush(half, c * HALVES + half):
                    cp.wait()

    rowblk = pl.BlockSpec((CHUNK, RW), lambda c: (c, 0))
    saved = jax.ShapeDtypeStruct((NHALF, HALF, HP, HD, LW), F32)
    stage = pltpu.VMEM((HALVES, HALF, HP, HD, LW), F32)
    return pl.pallas_call(
        kern, name="rwkv_scan_fwd", grid=(NCHUNK,),
        in_specs=[rowblk] * 5 + [pl.BlockSpec((HP, HD, 6 * CHUNK), lambda c: (0, 0, c))],
        out_specs=[rowblk, pl.BlockSpec((HALVES, HP, HD, LW), lambda c: (c, 0, 0, 0)),
                   pl.BlockSpec(memory_space=pl.ANY), pl.BlockSpec(memory_space=pl.ANY)],
        out_shape=[jax.ShapeDtypeStruct((T, RW), F32), jax.ShapeDtypeStruct((NHALF, HP, HD, LW), F32), saved, saved],
        scratch_shapes=[pltpu.VMEM((HP, HD, LW), F32), pltpu.VMEM((HP * HD, LW), F32),
                        pltpu.VMEM((HEADS, HD, CHUNK), F32), stage, stage, pltpu.SemaphoreType.DMA((2, HALVES))],
        compiler_params=_cp(("arbitrary",)),
    )(r, w, k, a, b, v3)


def _scan_bwd(r, w, k, a, b, v3, dy3, ck, st, sa):
    def kern(r_ref, w_ref, k_ref, a_ref, b_ref, v_ref, dy_ref, ck_ref, st_hbm, sa_hbm,
             dr_ref, dw_ref, dk_ref, da_ref, db_ref, dv_ref, ds_ref, sb_ref, sa_ref, pick_ref, dvc_ref, sems):
        c = pl.program_id(0)
        chunk = NCHUNK - 1 - c

        @pl.when(c == 0)
        def _():
            ds_ref[...] = jnp.zeros_like(ds_ref)

        lane = lax.broadcasted_iota(jnp.int32, (HD, CHUNK), 1)
        left = lane < HD
        left1 = lax.broadcasted_iota(jnp.int32, (1, LW), 1) < HD
        codes = _pick_codes()

        def rowsum(x):
            return jnp.sum(x, axis=0, keepdims=True)

        def fetch(slot, half_index):
            return [pltpu.make_async_copy(st_hbm.at[half_index], sb_ref.at[slot, pl.ds(1, HALF)], sems.at[0, slot]),
                    pltpu.make_async_copy(sa_hbm.at[half_index], sa_ref.at[slot], sems.at[1, slot])]

        def picks(half, i):
            pick_ref[pl.ds(0, HP * HD), :] = _column(v_ref, codes, half, i)
            pick_ref[pl.ds(HP * HD, HP * HD), :] = _column(dy_ref, codes, half, i)

        @pl.when(c == 0)
        def _():
            for cp in fetch(HALVES - 1, chunk * HALVES + HALVES - 1):
                cp.start()

        for half in reversed(range(HALVES)):
            base = half * HALF
            for cp in fetch(half, chunk * HALVES + half):
                cp.wait()
            if half:
                for cp in fetch(half - 1, chunk * HALVES + half - 1):
                    cp.start()
            else:
                @pl.when(chunk > 0)
                def _():
                    for cp in fetch(HALVES - 1, chunk * HALVES - 1):
                        cp.start()
            sb_ref[half, 0] = ck_ref[half]
            picks(half, jnp.int32(HALF - 1))

            def back(ii, carry):
                i = HALF - 1 - ii
                t = base + i
                row = lambda ref: _pair_rows(ref[pl.ds(t, 1), :])
                a_r, b_r, k_r, w_r, r_r = row(a_ref), row(b_ref), row(k_ref), row(w_ref), row(r_ref)
                vs = [pick_ref[pl.ds(p * HD, HD), :] for p in range(HP)]
                dys = [pick_ref[pl.ds((HP + p) * HD, HD), :] for p in range(HP)]
                picks(half, jnp.maximum(i - 1, 0))
                dr, dw, db, dk, da = [], [], [], [], []
                for p in range(HP):
                    Sp, dy = sb_ref[half, i, p], dys[p]
                    dS = ds_ref[p] + dy * r_r[p]
                    dr.append(rowsum(sb_ref[half, i + 1, p] * dy))
                    dw.append(rowsum(dS * Sp))
                    db.append(rowsum(dS * sa_ref[half, i, p]))
                    dk.append(rowsum(dS * vs[p]))
                    dsa = jnp.where(left, *_halfsums(dS, b_r[p], left1))
                    _store_columns(dvc_ref, p, lane == t, _halfsums(dS, k_r[p], left1))
                    da.append(rowsum(Sp * dsa))
                    ds_ref[p] = dS * w_r[p] + dsa * a_r[p]
                for ref, pieces in ((dr_ref, dr), (dw_ref, dw), (db_ref, db), (dk_ref, dk), (da_ref, da)):
                    ref[pl.ds(t, 1), :] = jnp.concatenate(pieces, axis=1)
                return carry

            lax.fori_loop(0, HALF, back, 0, unroll=16)
        _columns_to_rows(dvc_ref, dv_ref)

    rowblk = pl.BlockSpec((CHUNK, RW), lambda c: (NCHUNK - 1 - c, 0))
    col3blk = pl.BlockSpec((HP, HD, 6 * CHUNK), lambda c: (0, 0, NCHUNK - 1 - c))
    rowshape = jax.ShapeDtypeStruct((T, RW), F32)
    return pl.pallas_call(
        kern, name="rwkv_scan_bwd", grid=(NCHUNK,),
        in_specs=[rowblk] * 5 + [col3blk, col3blk,
                                 pl.BlockSpec((HALVES, HP, HD, LW), lambda c: (NCHUNK - 1 - c, 0, 0, 0)),
                                 pl.BlockSpec(memory_space=pl.ANY), pl.BlockSpec(memory_space=pl.ANY)],
        out_specs=[rowblk] * 6, out_shape=[rowshape] * 6,
        scratch_shapes=[pltpu.VMEM((HP, HD, LW), F32), pltpu.VMEM((HALVES, HALF + 1, HP, HD, LW), F32),
                        pltpu.VMEM((HALVES, HALF, HP, HD, LW), F32), pltpu.VMEM((2 * HP * HD, LW), F32),
                        pltpu.VMEM((HEADS, HD, CHUNK), F32), pltpu.SemaphoreType.DMA((2, HALVES))],
        compiler_params=_cp(("arbitrary",)),
    )(r, w, k, a, b, v3, dy3, ck, st, sa)


NT = (((1,), (1,)), ((), ()))
TN = (((0,), (0,)), ((), ()))
SCALE = HD ** -0.5
QKV_G = 3 * RW


def _attn_setup(g):
    dil = DILS[g]
    qkv = [pl.BlockSpec((T, LW), lambda hp, c=(g * QKV_G + s * RW) // LW: (0, c + hp)) for s in range(3)]
    tile = pl.BlockSpec((T, LW), lambda hp: (0, hp))
    bias = pl.BlockSpec((2, BLK, 2 * BLK), lambda hp: (hp, 0, 0))

    def blocks():
        for r in range(dil):
            for n in range(T // dil // BLK):
                rows = pl.ds(n * BLK * dil + r, BLK, stride=dil)
                keys = pl.ds((n - 1) * BLK * dil + r, 2 * BLK, stride=dil) if n else rows
                yield n, rows, keys

    return qkv, tile, bias, blocks


def _band(n):
    qi = lax.broadcasted_iota(jnp.int32, (BLK, 2 * BLK), 0)
    ki = lax.broadcasted_iota(jnp.int32, (BLK, 2 * BLK), 1)
    band = (ki >= qi) & (ki <= qi + BLK)
    return band if n else band[:, BLK:]


def _head_masks():
    lane = lax.broadcasted_iota(jnp.int32, (BLK, LW), 1)
    return lane < HD, [(lane < HD).astype(BF16), (lane >= HD).astype(BF16)]


def _attn_fwd(pq, bias, g):
    qkv, tile, bias_spec, blocks = _attn_setup(g)

    def kern(q_ref, k_ref, v_ref, b_ref, o_ref, l_ref):
        left, masks = _head_masks()
        for n, rows, keys in blocks():
            qb, kc, vc = q_ref[rows, :].astype(BF16), k_ref[keys, :].astype(BF16), v_ref[keys, :].astype(BF16)
            valid = _band(n)
            o, lse = [], []
            for j in range(2):
                bias_j = b_ref[j] if n else b_ref[j][:, BLK:]
                s = lax.dot_general(qb * masks[j], kc, NT, preferred_element_type=F32) * SCALE + bias_j
                s = jnp.where(valid, s, -jnp.inf)
                m = jnp.max(s, axis=1, keepdims=True)
                e = jnp.exp(s - m)
                den = jnp.sum(e, axis=1, keepdims=True)
                o.append(jnp.dot((e / den).astype(BF16), vc, preferred_element_type=F32))
                lse.append(m + jnp.log(den))
            o_ref[rows, :] = jnp.where(left, o[0], o[1])
            l_ref[rows, :] = jnp.where(left, lse[0], lse[1])

    shape = jax.ShapeDtypeStruct((T, RW), F32)
    return pl.pallas_call(
        kern, name=f"attn_fwd_{g}", grid=(HP,),
        in_specs=qkv + [bias_spec], out_specs=[tile, tile], out_shape=[shape, shape],
        compiler_params=_cp(("parallel",)),
    )(pq, pq, pq, bias)


def _attn_bwd(pq, bias, do, o, lse, dlse, g):
    qkv, tile, bias_spec, blocks = _attn_setup(g)

    def kern(q_ref, k_ref, v_ref, b_ref, do_ref, o_ref, l_ref, dl_ref, dq_ref, dk_ref, dv_ref, db_ref):
        left, masks = _head_masks()
        lane = lax.broadcasted_iota(jnp.int32, (BLK, LW), 1)
        dk_ref[...] = jnp.zeros_like(dk_ref)
        dv_ref[...] = jnp.zeros_like(dv_ref)
        db_ref[...] = jnp.zeros_like(db_ref)

        def column(tile_, j):
            return jnp.sum(jnp.where(lane == j * HD, tile_, 0.0), axis=1, keepdims=True)

        for n, rows, keys in blocks():
            qb, kc, vc = q_ref[rows, :].astype(BF16), k_ref[keys, :].astype(BF16), v_ref[keys, :].astype(BF16)
            dof, valid = do_ref[rows, :], _band(n)
            dob, prod = dof.astype(BF16), dof * o_ref[rows, :]
            dq = []
            for j in range(2):
                bias_j = b_ref[j] if n else b_ref[j][:, BLK:]
                delta = jnp.sum(prod * masks[j].astype(F32), axis=1, keepdims=True)
                qm, dom = qb * masks[j], dob * masks[j]
                s = lax.dot_general(qm, kc, NT, preferred_element_type=F32) * SCALE + bias_j
                p = jnp.where(valid, jnp.exp(s - column(l_ref[rows, :], j)), 0.0)
                dp = lax.dot_general(dom, vc, NT, preferred_element_type=F32)
                ds = p * (dp + (column(dl_ref[rows, :], j) - delta))
                if n:
                    db_ref[j] += ds
                else:
                    db_ref[j, :, BLK:] += ds
                dsb = (ds * SCALE).astype(BF16)
                dq.append(jnp.dot(dsb, kc, preferred_element_type=F32))
                dk_ref[keys, :] += lax.dot_general(dsb, qm, TN, preferred_element_type=F32)
                dv_ref[keys, :] += lax.dot_general(p.astype(BF16), dom, TN, preferred_element_type=F32)
            dq_ref[rows, :] = jnp.where(left, dq[0], dq[1])

    shape = jax.ShapeDtypeStruct((T, RW), F32)
    return pl.pallas_call(
        kern, name=f"attn_bwd_{g}", grid=(HP,),
        in_specs=qkv + [bias_spec] + [tile] * 4, out_specs=[tile] * 3 + [bias_spec],
        out_shape=[shape] * 3 + [jax.ShapeDtypeStruct((HEADS, BLK, 2 * BLK), F32)],
        compiler_params=_cp(("parallel",)),
    )(pq, pq, pq, bias, do, o, lse, dlse)


NBUCKET = 32
NPAIR = BLK * 2 * BLK


def _relbias_table(rbT, onehotT):
    def kern(rb_ref, oh_ref, out_ref):
        out_ref[0] = sum(jnp.dot(p, oh_ref[0], preferred_element_type=F32) for p in _split3(rb_ref[0]))

    return pl.pallas_call(
        kern, name="relbias_table", grid=(3,),
        in_specs=[pl.BlockSpec((1, HEADS, NBUCKET), lambda g: (g, 0, 0)),
                  pl.BlockSpec((1, NBUCKET, NPAIR), lambda g: (g, 0, 0))],
        out_specs=pl.BlockSpec((1, HEADS, NPAIR), lambda g: (g, 0, 0)),
        out_shape=jax.ShapeDtypeStruct((3, HEADS, NPAIR), F32),
        compiler_params=_cp(("parallel",)),
    )(rbT, onehotT)


def _relbias_grad(db, onehotT):
    nt = (((1,), (1,)), ((), ()))

    def kern(db_ref, oh_ref, out_ref):
        hi, mid, _ = _split3(db_ref[0])
        out_ref[0] = (lax.dot_general(hi, oh_ref[0], nt, preferred_element_type=F32)
                      + lax.dot_general(mid, oh_ref[0], nt, preferred_element_type=F32))

    return pl.pallas_call(
        kern, name="relbias_grad", grid=(3,),
        in_specs=[pl.BlockSpec((1, HEADS, NPAIR), lambda g: (g, 0, 0)),
                  pl.BlockSpec((1, NBUCKET, NPAIR), lambda g: (g, 0, 0))],
        out_specs=pl.BlockSpec((1, HEADS, NBUCKET), lambda g: (g, 0, 0)),
        out_shape=jax.ShapeDtypeStruct((3, HEADS, NBUCKET), F32),
        compiler_params=_cp(("parallel",)),
    )(db, onehotT)


def _adamw(w, g, m, v):
    m2 = ADAM_B1 * m + (1.0 - ADAM_B1) * g
    v2 = ADAM_B2 * v + (1.0 - ADAM_B2) * (g * g)
    m_hat = m2 / (1.0 - ADAM_B1 ** ADAM_STEP)
    v_hat = v2 / (1.0 - ADAM_B2 ** ADAM_STEP)
    return -ADAM_LR * (m_hat / (jnp.sqrt(v_hat) + ADAM_EPS) + ADAM_WD * w), m2, v2


def _ada_mod(c_all, ada_w, ada_b_loc):
    def kern(c_ref, w_ref, b_ref, o_ref):
        c = c_ref[...]
        cond = c * jax.nn.sigmoid(c)
        o_ref[0] = jnp.dot(cond, w_ref[0], precision=HI, preferred_element_type=F32) + b_ref[0]

    ncol = ada_w.shape[2]
    return pl.pallas_call(
        kern, name="ada_mod", grid=(2,),
        in_specs=[pl.BlockSpec((NDEV, D), lambda i: (0, 0)),
                  pl.BlockSpec((1, D, ncol), lambda i: (i, 0, 0)),
                  pl.BlockSpec((1, 1, ncol), lambda i: (i, 0, 0))],
        out_specs=pl.BlockSpec((1, NDEV, ncol), lambda i: (i, 0, 0)),
        out_shape=jax.ShapeDtypeStruct((2, NDEV, ncol), F32),
        compiler_params=_cp(("parallel",)),
    )(c_all, ada_w, ada_b_loc.reshape(2, 1, ncol))


def _ada_grad_adamw(cT_all, dmod_loc, w, m, v):
    ncol = w.shape[2]
    tr = 256

    def kern(c_ref, d_ref, w_ref, m_ref, v_ref, g_ref, dl_ref, m2_ref, v2_ref):
        c = c_ref[...]
        cond = c * jax.nn.sigmoid(c)
        g = jnp.dot(cond, d_ref[0], precision=HI, preferred_element_type=F32)
        dl, m2, v2 = _adamw(w_ref[0], g, m_ref[0], v_ref[0])
        g_ref[0], dl_ref[0], m2_ref[0], v2_ref[0] = g, dl, m2, v2

    big = pl.BlockSpec((1, tr, ncol), lambda i, j: (i, j, 0))
    shp = jax.ShapeDtypeStruct(w.shape, F32)
    return pl.pallas_call(
        kern, name="ada_grad_adamw", grid=(2, D // tr),
        in_specs=[pl.BlockSpec((tr, NDEV), lambda i, j: (j, 0)),
                  pl.BlockSpec((1, NDEV, ncol), lambda i, j: (i, 0, 0)), big, big, big],
        out_specs=[big] * 4, out_shape=[shp] * 4,
        compiler_params=_cp(("parallel", "parallel")),
    )(cT_all, dmod_loc, w, m, v)


def _sum_adamw(recv, w, m, v, name, tr):
    S = recv.shape[0]
    R, C = w.shape
    assert R % tr == 0 and recv.shape[1:] == (R, C)

    def kern(r_ref, w_ref, m_ref, v_ref, g_ref, dl_ref, m2_ref, v2_ref):
        g = r_ref[0].astype(F32)
        for s in range(1, S):
            g = g + r_ref[s].astype(F32)
        dl, m2, v2 = _adamw(w_ref[...], g, m_ref[...], v_ref[...])
        g_ref[...], dl_ref[...], m2_ref[...], v2_ref[...] = g, dl, m2, v2

    flat = pl.BlockSpec((tr, C), lambda i: (i, 0))
    shp = jax.ShapeDtypeStruct((R, C), F32)
    return pl.pallas_call(
        kern, name=name, grid=(R // tr,),
        in_specs=[pl.BlockSpec((S, tr, C), lambda i: (0, i, 0)), flat, flat, flat],
        out_specs=[flat] * 4, out_shape=[shp] * 4,
        compiler_params=_cp(("parallel",)),
    )(recv, w, m, v)


def _pack(arrs, dtype, row_mult):
    flat = jnp.concatenate([a.reshape(-1).astype(dtype) for a in arrs])
    flat = jnp.pad(flat, (0, -flat.shape[0] % (128 * row_mult)))
    return flat.reshape(-1, 128)


def _pack8(arrs, dtype, row_mult):
    flat = jnp.concatenate([a.reshape(NDEV, -1).astype(dtype) for a in arrs], axis=1)
    flat = jnp.pad(flat, ((0, 0), (0, -flat.shape[1] % (128 * row_mult))))
    return flat.reshape(NDEV, -1, 128)


def _unpack(buf, shapes, lead=()):
    flat = buf.reshape(lead + (-1,))
    out, off = [], 0
    for s in shapes:
        n = math.prod(s)
        out.append(flat[..., off:off + n].reshape(lead + tuple(s)))
        off += n
    return out


def _to_chunks(full, kind):
    if kind == "col":
        x = full.reshape(full.shape[:-1] + (NDEV, full.shape[-1] // NDEV))
        return jnp.moveaxis(x, -2, 0)
    x = full.reshape(full.shape[:-2] + (NDEV, full.shape[-2] // NDEV, full.shape[-1]))
    return jnp.moveaxis(x, -3, 0)


def _from_chunks(g8, kind):
    if kind == "col":
        x = jnp.moveaxis(g8, 0, -2)
        return x.reshape(x.shape[:-2] + (x.shape[-2] * x.shape[-1],))
    x = jnp.moveaxis(g8, 0, -3)
    return x.reshape(x.shape[:-3] + (x.shape[-3] * x.shape[-2], x.shape[-1]))


def _pad_pa(x):
    z = lambda n: jnp.zeros(x.shape[:-1] + (n,), x.dtype)
    return jnp.concatenate([x[..., :1600], z(64), x[..., 1600:1664], z(64), x[..., 1664:1824], z(96)], -1)


def _unpad_pa(x):
    return jnp.concatenate([x[..., :1600], x[..., 1664:1728], x[..., 1792:1952]], -1)


AB_SEGMENTS = ((0, 1600, 0), (1600, 1664, 64), (1664, 1824, 128), (1824, 3360, PAB - 3360))
AB_SHARD = 3360 // NDEV


def _ab_in_padded(g8):
    blocks, at = [], 0
    for start, end, shift in AB_SEGMENTS:
        if start + shift > at:
            blocks.append(jnp.zeros((g8.shape[1], start + shift - at), g8.dtype))
        for j in range(start // AB_SHARD, (end - 1) // AB_SHARD + 1):
            lo, hi = max(start, j * AB_SHARD), min(end, (j + 1) * AB_SHARD)
            blocks.append(g8[j, :, lo - j * AB_SHARD:hi - j * AB_SHARD])
        at = end + shift
    return jnp.concatenate(blocks, axis=1)


def _ab_in_shards(padded):
    shards = []
    for j in range(NDEV):
        pieces = [padded[:, max(start, j * AB_SHARD) + shift:min(end, (j + 1) * AB_SHARD) + shift]
                  for start, end, shift in AB_SEGMENTS if max(start, j * AB_SHARD) < min(end, (j + 1) * AB_SHARD)]
        shards.append(jnp.concatenate(pieces, axis=1))
    return jnp.stack(shards)


def _pad_rows(x, n):
    return jnp.pad(x, ((0, n - x.shape[0]), (0, 0)))


def _bucket_tables():
    qi = jnp.arange(BLK)[:, None]
    ki = jnp.arange(2 * BLK)[None, :]
    rel = BLK + qi - ki
    tabs = []
    for dil in DILS:
        dist = jnp.clip(rel, 0, BLK) * dil
        logd = jnp.log(jnp.maximum(dist, 1).astype(F32) / 16) / math.log(2048 / 16)
        large = jnp.minimum(16 + (logd * 16).astype(jnp.int32), 31)
        tabs.append(jnp.where(dist < 16, dist, large))
    return jnp.stack(tabs)


SHARDED = (("ln_g", "col"), ("ln_b", "col"), ("ab_w_in", "col"), ("rw_w_up", "col"), ("rw_a_up", "col"),
           ("rw_g_up", "col"), ("sc_conv_w", "col"), ("ab_w_out", "row"), ("dil_w_qkv", "col"),
           ("dil_w_out", "col"), ("mlp_w1", "col"), ("mlp_w2", "row"))
FIRST = ("ab_w_in",)
LATER = ("ab_w_out", "dil_w_qkv", "dil_w_out", "mlp_w1", "mlp_w2")
GATHER_BF16 = FIRST + LATER
GATHER_F32 = ("rw_w_up", "rw_a_up", "rw_g_up", "sc_conv_w", "ln_g", "ln_b")
REPLICATED = ("ada_b", "rw_mu", "rw_w0", "rw_a0", "rw_k_k", "rw_k_a", "rw_r_k", "rw_lnx_g", "rw_lnx_b", "rel_bias")
WEIGHTS = ("ada_w", "ada_b", "ln_g", "ln_b", "ab_w_in", "rw_mu", "rw_w0", "rw_w_up", "rw_a0", "rw_a_up",
           "rw_g_up", "rw_k_k", "rw_k_a", "rw_r_k", "rw_lnx_g", "rw_lnx_b", "sc_conv_w", "ab_w_out",
           "dil_w_qkv", "dil_w_out", "rel_bias", "mlp_w1", "mlp_w2")


def _local_step(x0, tgt, mod, W, P, later_weights, early_grads):
    row = lambda a: a.reshape(1, -1)
    W = dict(W)
    m6 = mod.reshape(2, 6, 1, D)
    sc = [m6[0, 1], m6[0, 4], m6[1, 1], m6[1, 4]]
    sh = [m6[0, 0], m6[0, 3], m6[1, 0], m6[1, 3]]
    gt = [m6[0, 2], m6[0, 5], m6[1, 2], m6[1, 5]]
    lng = [row(P["ln_g"][0, 0]), row(P["ln_g"][0, 1]), row(P["ln_g"][1, 0]), row(P["ln_g"][1, 1])]
    lnb = [row(P["ln_b"][0, 0]), row(P["ln_b"][0, 1]), row(P["ln_b"][1, 0]), row(P["ln_b"][1, 1])]
    E = jnp.kron(jnp.eye(HEADS, dtype=BF16), jnp.ones((HD, HD), BF16))

    def mod_body(r, p):
        u = r[0] * (1.0 + p[0]) + p[1]
        return [u, u], []

    (u0, u0T), _ = _rows("modulate", mod_body, [x0], [sc[0], sh[0]], [(D, BF16), (D, BF16, "T")])

    def post_fwd_body(r, p):
        xn, un = _post_ln_mod(r[0], r[1], *p)
        return [xn, un, un], []

    def post_fwd(s, x, y):
        (xn, un, unT), _ = _rows(f"post_ln_{s}", post_fwd_body, [x, y],
                                 [gt[s], lng[s], lnb[s], sc[s + 1], sh[s + 1]],
                                 [(D, F32), (D, BF16), (D, BF16, "T")])
        return xn, un, unT

    def relu2(acc):
        a = jnp.maximum(acc, 0.0)
        return acc, a * a, a * a

    def relu2_bwd(acc, h):
        return (acc * (2.0 * jnp.maximum(h, 0.0)),)

    p = _mm("ab_in", u0, W["ab_w_in"])
    mu = _pad_pa(P["rw_mu"])
    mu_parts = [mu[:, :512], mu[:, 512:1024], mu[:, 1024:1536], mu[:, 1536:1664], mu[:, 1664:1792], mu[:, 1792:]]
    pre_params = mu_parts + [P["rw_w0"], _pad_rows(P["rw_w_up"], 128), P["rw_a0"], _pad_rows(P["rw_a_up"], 128),
                             _pad_rows(P["rw_g_up"], 256), P["rw_k_k"], P["rw_k_a"],
                             P["sc_conv_w"][0:1], P["sc_conv_w"][1:2], P["sc_conv_w"][2:3]]
    pieces = [(p, 512, 0), (p, 512, 1), (p, 512, 2), (p, 128, 12), (p, 128, 13), (p, 256, 7),
              (p, 512, 4), (p, 512, 5), (p, 512, 6)]
    shifted = [0, 1, 2, 3, 4, 5, 6, 8]
    pre_rows = pieces + [pieces[i] + ("prev",) for i in shifted]
    NPR = 19

    def pre_args(r):
        x, prev = r[:9], dict(zip(shifted, r[9:17]))
        down = lambda i, k: _shift_down(x[i], prev[i], k)
        return x[:6] + [down(i, 1) for i in range(6)] + x[6:9] + [down(6, 1), down(8, 1), down(6, 2), down(8, 2)]

    def pre_fwd_body(r, pp):
        return list(_pre_core(pp[0], *pre_args(r), *pp[1:])), []

    (r_, w_, kh_, v_, a_, b_, gate_, yb), _ = _rows(
        "rwkv_pre", pre_fwd_body, pre_rows, [E] + pre_params, [(RW, F32)] * 7 + [(RW, BF16)], tm=256)
    scan_in = [r_, w_, kh_, a_, b_, _cols3(v_, "rwkv_v_columns")]
    ysc, *saved = _scan_fwd(*scan_in)
    post_params = [P["rw_lnx_g"], P["rw_lnx_b"], P["rw_r_k"].reshape(1, RW)]

    def postmix_fwd_body(r, pp):
        return [_post_core(pp[0], *r, *pp[1:])], []

    (ya,), _ = _rows("rwkv_post", postmix_fwd_body, [ysc, r_, kh_, v_, gate_], [E] + post_params,
                     [(RW, BF16)], tm=256)
    cat = jnp.concatenate([ya, yb], axis=1)
    W.update(later_weights(cat))
    y0 = _mm("ab_out", cat, W["ab_w_out"])
    x1, u1, u1T = post_fwd(0, x0, y0)

    h1, a1, a1T = _mm("mlp1_up_0", u1, W["mlp_w1"][0], out=(F32, BF16, (BF16, "T")), epi=relu2)
    y1 = _mm("mlp1_down_0", a1, W["mlp_w2"][0])
    x2, u2, u2T = post_fwd(1, x1, y1)

    pq = _mm("qkv", u2, W["dil_w_qkv"])
    onehotT = (_bucket_tables().reshape(3, 1, NPAIR) == jnp.arange(NBUCKET).reshape(1, NBUCKET, 1)).astype(BF16)
    rbT = P["rel_bias"].reshape(NBUCKET, 3, HEADS).transpose(1, 2, 0)
    bias = _relbias_table(rbT, onehotT).reshape(3, HEADS, BLK, 2 * BLK)
    og, lse = zip(*[_attn_fwd(pq, bias[g], g) for g in range(3)])

    def merge_fwd_body(r, pp):
        return [_merge_core(*r)], []

    (om,), _ = _rows("attn_merge", merge_fwd_body, list(og + lse), [], [(RW, BF16)])
    y2 = _mm("dil_out", om, W["dil_w_out"])
    x3, u3, u3T = post_fwd(2, x2, y2)

    h3, a3, a3T = _mm("mlp1_up_1", u3, W["mlp_w1"][1], out=(F32, BF16, (BF16, "T")), epi=relu2)
    y3 = _mm("mlp1_down_1", a3, W["mlp_w2"][1])

    def last_body(r, pp):
        x, y, tg = r
        xn, vjp = jax.vjp(_post_ln, x, y, *pp)
        err = xn - tg
        dx, dy, dg, dlg, dlb = vjp(err * (1.0 / D))
        loss = jnp.full((1, 128), (0.5 / D) * jnp.sum(err * err), F32)
        return [dx, dy], [loss, dg, dlg, dlb]

    (dxp, dy3), (loss_acc, dg3, dlng3, dlnb3) = _rows(
        "final_ln_loss", last_body, [x3, y3, tgt], [gt[3], lng[3], lnb[3]],
        [(D, F32), (D, BF16)], [(1, 128), (1, D), (1, D), (1, D)])

    G = {}
    dsc, dsh, dgt = [None] * 4, [None] * 4, [None] * 4
    dlng, dlnb = [None] * 4, [None] * 4
    dgt[3], dlng[3], dlnb[3] = dg3, dlng3, dlnb3

    def mlp_bwd(i, uT, h, aT, dy):
        dh = _mm(f"mlp_dh_{i}", dy, W["mlp_w2"][i], tb=True, out=(BF16,), epi=relu2_bwd, extras=(h,))
        gw2 = _mm(f"mlp_dw2_{i}", aT, dy)
        du = _mm(f"mlp_du_{i}", dh, W["mlp_w1"][i], tb=True)
        gw1 = _mm(f"mlp_dw1_{i}", uT, dh)
        return du, gw1, gw2

    def post_bwd_body(r, pp):
        x, y, dxn, dun = r
        _, vjp = jax.vjp(_post_ln_mod, x, y, *pp)
        dx, dy, dg, dlg, dlb, dscn, dshn = vjp((dxn, dun))
        return [dx, dy], [dg, dlg, dlb, dscn, dshn]

    def post_bwd(s, x, y, dxn, dun):
        (dx, dy), (dgt[s], dlng[s], dlnb[s], dsc[s + 1], dsh[s + 1]) = _rows(
            f"post_ln_bwd_{s}", post_bwd_body, [x, y, dxn, dun],
            [gt[s], lng[s], lnb[s], sc[s + 1], sh[s + 1]], [(D, F32), (D, BF16)], [(1, D)] * 5)
        return dx, dy

    du3, gw1_1, gw2_1 = mlp_bwd(1, u3T, h3, a3T, dy3)
    dxp, dy2 = post_bwd(2, x2, y2, dxp, du3)

    G["dil_w_out"] = _mm("dil_out_dw", om.T, dy2)[None]
    do = _mm("dil_out_dx", dy2, W["dil_w_out"], tb=True)

    def merge_bwd_body(r, pp):
        _, vjp = jax.vjp(_merge_core, *r[:6])
        d = vjp(r[6])
        return list(d[:3]) + [_headsum(d[3 + g], pp[0]) for g in range(3)], []

    mb, _ = _rows("attn_merge_bwd", merge_bwd_body, list(og + lse) + [do], [E],
                  [(RW, F32)] * 6)
    back = [_attn_bwd(pq, bias[g], mb[g], og[g], lse[g], mb[3 + g], g) for g in range(3)]
    dpq = jnp.concatenate([t for dq, dk, dv, _ in back for t in (dq, dk, dv)], axis=1).astype(BF16)
    rb = _relbias_grad(jnp.stack([b[3] for b in back]).reshape(3, HEADS, NPAIR), onehotT)
    G["rel_bias"] = rb.transpose(2, 0, 1).reshape(NBUCKET, 3 * HEADS)
    G["dil_w_qkv"] = _mm("qkv_dw", u2T, dpq)[None]
    du2 = _mm("qkv_dx", dpq, W["dil_w_qkv"], tb=True)
    dxp, dy1 = post_bwd(1, x1, y1, dxp, du2)

    du1, gw1_0, gw2_0 = mlp_bwd(0, u1T, h1, a1T, dy1)
    G["mlp_w1"] = jnp.stack([gw1_0, gw1_1])
    G["mlp_w2"] = jnp.stack([gw2_0, gw2_1])
    dxp, dy0 = post_bwd(0, x0, y0, dxp, du1)

    G["ab_w_out"] = _mm("ab_out_dw", cat.T, dy0)[None]
    dcat = _mm("ab_out_dx", dy0, W["ab_w_out"], tb=True)
    post_params = [post_params[0] + early_grads(G)] + post_params[1:]

    def postmix_bwd_body(r, pp):
        _, vjp = jax.vjp(functools.partial(_post_core, pp[0]), *r[:5], *pp[1:])
        d = vjp(r[5])
        return list(d[:5]), list(d[5:])

    (dysc, dr1, dkh1, dv1, dgate), (G["rw_lnx_g"], G["rw_lnx_b"], drk) = _rows(
        "rwkv_post_bwd", postmix_bwd_body, [ysc, r_, kh_, v_, gate_, (dcat, 512, 0)], [E] + post_params,
        [(RW, F32)] * 5, [(1, RW)] * 3, tm=256)
    G["rw_r_k"] = drk.reshape(1, HEADS, HD)
    dr2, dw2, dk2, da2, db2, dv2 = _scan_bwd(*scan_in, _cols3(dysc, "rwkv_dy_columns"), *saved)

    def pre_bwd_body(r, pp):
        prim, ct = pre_args(r[:len(pre_rows)]), r[len(pre_rows):]
        _, vjp = jax.vjp(functools.partial(_pre_core, pp[0]), *prim, *pp[1:])
        cts = (ct[0] + ct[1], ct[2], ct[3] + ct[4], ct[5] + ct[6], ct[7], ct[8], ct[9], ct[10])
        d = vjp(cts)
        z = jnp.zeros_like(d[12])
        dp = jnp.concatenate([d[0], d[1], d[2], d[3], d[4], d[5], d[12], d[13], d[14]], axis=1)
        dp1 = jnp.concatenate([d[6], d[7], d[8], d[9], d[10], d[11], d[15], z, d[16]], axis=1)
        dp2 = jnp.concatenate([d[17], z, d[18]], axis=1)
        return [dp, dp1, dp2], list(d[NPR:])

    acc_shapes = [a.shape for a in pre_params]
    (dp, dp1, dp2), pacc = _rows(
        "rwkv_pre_bwd", pre_bwd_body,
        pre_rows + [dr1, dr2, dw2, dkh1, dk2, dv1, dv2, da2, db2, dgate, (dcat, 512, 1)],
        [E] + pre_params, [(PAB, F32), (PAB, F32), (PB, F32)], acc_shapes, tm=256)
    G["rw_mu"] = _unpad_pa(jnp.concatenate(pacc[:6], axis=1))
    G["rw_w0"], G["rw_a0"], G["rw_k_k"], G["rw_k_a"] = pacc[6], pacc[8], pacc[11], pacc[12]
    G["rw_w_up"] = pacc[7][None, :64]
    G["rw_a_up"] = pacc[9][None, :64]
    G["rw_g_up"] = pacc[10][None, :160]
    G["sc_conv_w"] = jnp.concatenate(pacc[13:16], axis=0)[None]

    def shift_merge_body(r, pp):
        d0, d1, d1_next, d2, d2_next = r
        d = d0 + _shift_up(d1, d1_next, 1)
        return [jnp.concatenate([d[:, :PA], d[:, PA:] + _shift_up(d2, d2_next, 2)], axis=1)], []

    (dpt,), _ = _rows("shift_merge", shift_merge_body,
                      [dp, dp1, (dp1, PAB, 0, "next"), dp2, (dp2, PB, 0, "next")], [], [(PAB, BF16)], tm=256)
    du0 = _mm("ab_in_dx", dpt, W["ab_w_in"], tb=True)

    def mod_bwd_body(r, pp):
        du, dx, x = r
        return [dx + du * (1.0 + pp[0])], [jnp.sum(du * x, axis=0, keepdims=True), jnp.sum(du, axis=0, keepdims=True)]

    (grad_x,), (dsc[0], dsh[0]) = _rows("modulate_bwd", mod_bwd_body, [du0, dxp, x0], [sc[0]], [(D, F32)],
                                        [(1, D), (1, D)])

    G["ln_g"] = jnp.concatenate(dlng, axis=0).reshape(2, 2, D)
    G["ln_b"] = jnp.concatenate(dlnb, axis=0).reshape(2, 2, D)
    dmod = jnp.concatenate([dsh[0], dsc[0], dgt[0], dsh[1], dsc[1], dgt[1],
                            dsh[2], dsc[2], dgt[2], dsh[3], dsc[3], dgt[3]], axis=1).reshape(2, 6 * D)
    return loss_acc[0, 0], grad_x, dmod, G, lambda: _ab_in_shards(_mm("ab_in_dw", u0T, dpt))[:, None]


def kernel(x, c, ada_w, ada_b, ln_g, ln_b, ab_w_in, rw_mu, rw_w0, rw_w_up, rw_a0, rw_a_up, rw_g_up, rw_k_k, rw_k_a, rw_r_k, rw_lnx_g, rw_lnx_b, sc_conv_w, ab_w_out, dil_w_qkv, dil_w_out, rel_bias, mlp_w1, mlp_w2, loss_target, m_ada_w, m_ada_b, m_ln_g, m_ln_b, m_ab_w_in, m_rw_mu, m_rw_w0, m_rw_w_up, m_rw_a0, m_rw_a_up, m_rw_g_up, m_rw_k_k, m_rw_k_a, m_rw_r_k, m_rw_lnx_g, m_rw_lnx_b, m_sc_conv_w, m_ab_w_out, m_dil_w_qkv, m_dil_w_out, m_rel_bias, m_mlp_w1, m_mlp_w2, v_ada_w, v_ada_b, v_ln_g, v_ln_b, v_ab_w_in, v_rw_mu, v_rw_w0, v_rw_w_up, v_rw_a0, v_rw_a_up, v_rw_g_up, v_rw_k_k, v_rw_k_a, v_rw_r_k, v_rw_lnx_g, v_rw_lnx_b, v_sc_conv_w, v_ab_w_out, v_dil_w_qkv, v_dil_w_out, v_rel_bias, v_mlp_w1, v_mlp_w2):
    w = dict(ada_w=ada_w, ada_b=ada_b, ln_g=ln_g, ln_b=ln_b, ab_w_in=ab_w_in, rw_mu=rw_mu, rw_w0=rw_w0,
             rw_w_up=rw_w_up, rw_a0=rw_a0, rw_a_up=rw_a_up, rw_g_up=rw_g_up, rw_k_k=rw_k_k, rw_k_a=rw_k_a,
             rw_r_k=rw_r_k, rw_lnx_g=rw_lnx_g, rw_lnx_b=rw_lnx_b, sc_conv_w=sc_conv_w, ab_w_out=ab_w_out,
             dil_w_qkv=dil_w_qkv, dil_w_out=dil_w_out, rel_bias=rel_bias, mlp_w1=mlp_w1, mlp_w2=mlp_w2)
    m = dict(ada_w=m_ada_w, ada_b=m_ada_b, ln_g=m_ln_g, ln_b=m_ln_b, ab_w_in=m_ab_w_in, rw_mu=m_rw_mu,
             rw_w0=m_rw_w0, rw_w_up=m_rw_w_up, rw_a0=m_rw_a0, rw_a_up=m_rw_a_up, rw_g_up=m_rw_g_up,
             rw_k_k=m_rw_k_k, rw_k_a=m_rw_k_a, rw_r_k=m_rw_r_k, rw_lnx_g=m_rw_lnx_g, rw_lnx_b=m_rw_lnx_b,
             sc_conv_w=m_sc_conv_w, ab_w_out=m_ab_w_out, dil_w_qkv=m_dil_w_qkv, dil_w_out=m_dil_w_out,
             rel_bias=m_rel_bias, mlp_w1=m_mlp_w1, mlp_w2=m_mlp_w2)
    v = dict(ada_w=v_ada_w, ada_b=v_ada_b, ln_g=v_ln_g, ln_b=v_ln_b, ab_w_in=v_ab_w_in, rw_mu=v_rw_mu,
             rw_w0=v_rw_w0, rw_w_up=v_rw_w_up, rw_a0=v_rw_a0, rw_a_up=v_rw_a_up, rw_g_up=v_rw_g_up,
             rw_k_k=v_rw_k_k, rw_k_a=v_rw_k_a, rw_r_k=v_rw_r_k, rw_lnx_g=v_rw_lnx_g, rw_lnx_b=v_rw_lnx_b,
             sc_conv_w=v_sc_conv_w, ab_w_out=v_ab_w_out, dil_w_qkv=v_dil_w_qkv, dil_w_out=v_dil_w_out,
             rel_bias=v_rel_bias, mlp_w1=v_mlp_w1, mlp_w2=v_mlp_w2)
    kinds = dict(SHARDED)
    me = 4 * lax.axis_index("x") + 2 * lax.axis_index("y") + lax.axis_index("c")
    ncol = ada_w.shape[2]

    small = _all_gather(_pack([c] + [w[n] for n in GATHER_F32], F32, 8), "gather_small")
    parts = _unpack(small, [c.shape] + [w[n].shape for n in GATHER_F32], (NDEV,))
    c_all = parts[0].reshape(NDEV, D)
    P = {n: _from_chunks(t, kinds[n]) for n, t in zip(GATHER_F32, parts[1:])}
    P = {n: (t if n in ("ln_g", "ln_b") else t[0]) for n, t in P.items()}
    for n in REPLICATED[1:]:
        P[n] = w[n]
    def full(n, t):
        t = _from_chunks(t, kinds[n])
        return t if n in ("mlp_w1", "mlp_w2") else t[0]

    (first,) = _all_gather_many([ab_w_in.astype(BF16)], "gather_first_weight")
    W = {"ab_w_in": _ab_in_padded(first[:, 0])}

    ada_b_loc = lax.dynamic_slice(ada_b, (0, ncol * me), (2, ncol))
    mod_part = _ada_mod(c_all, ada_w, ada_b_loc)
    mod_all = _all_gather(mod_part.reshape(-1, 128), "gather_mod").reshape(NDEV, 2, NDEV, ncol)
    mod = lax.dynamic_index_in_dim(mod_all, me, axis=2, keepdims=False)
    mod = mod.transpose(1, 0, 2).reshape(2, 6 * D)

    behind = (mod[0, 0] * 0.0).astype(BF16)
    later = _exchange_start([w[n].astype(BF16) + (behind if n == LATER[0] else 0) for n in LATER], True,
                            "gather_later_weights_start")
    mod = mod + later[-1][0, 0]

    def later_weights(after):
        lands = _exchange_wait(later, True, after, "gather_later_weights_wait")
        return {n: full(n, t) for n, t in zip(LATER, lands)}

    sent = []

    def early_grads(G):
        sent.append(_exchange_start([_to_chunks(G[n], kinds[n]).astype(BF16) for n in LATER], False,
                                    "exchange_later_grads_start"))
        return sent[0][-1][0, 0]

    loss_part, grad_x, dmod, G, in_grad = _local_step(x[0], loss_target[0], mod, W, P, later_weights, early_grads)
    G["ada_b"] = dmod
    big_out = {}

    def update(n, contributions):
        cols = w[n].shape[-1]
        flat = lambda t: t.reshape(-1, cols)
        rows = flat(w[n]).shape[0]
        outs = _sum_adamw(contributions.reshape(-1, rows, cols), flat(w[n]), flat(m[n]), flat(v[n]),
                          f"sum_adamw_{n}", min(rows, 256))
        big_out[n] = [o.reshape(w[n].shape) for o in outs]

    rep_shapes = [w[n].shape for n in REPLICATED] + [(1,)]
    rep_all = _all_gather(_pack([G[n] for n in REPLICATED] + [loss_part], F32, 8), "gather_replicated_grads")
    names = [n for n, _ in SHARDED if n not in GATHER_BF16]
    shard_shapes = [w[n].shape for n in names]
    recv = _all_to_all(_pack8([_to_chunks(G[n], kinds[n]) for n in names], F32, 8), "exchange_small_grads")

    behind = (recv[0, 0, 0] * 0.0 + rep_all[0, 0, 0] * 0.0).astype(BF16)
    last = _exchange_start([in_grad().astype(BF16) + behind], False, "exchange_last_grad_start")

    zero = last[-1][0:1, 0]
    pk = lambda d: _pack([d[n] for n in REPLICATED] + [zero], F32, 8)
    rep_out = _sum_adamw(rep_all, pk(w), pk(m), pk(v), "sum_adamw_replicated", rep_all.shape[1])
    loss = _unpack(rep_out[0], rep_shapes)[-1][0]
    rep_out = [dict(zip(REPLICATED, _unpack(o, rep_shapes))) for o in rep_out]
    dmod_all = _unpack(rep_all, [(2, 6 * D)], (NDEV,))[0]
    dmod_loc = lax.dynamic_slice(dmod_all, (0, 0, ncol * me), (NDEV, 2, ncol)).transpose(1, 0, 2)
    ada_out = _ada_grad_adamw(c_all.T + zero, dmod_loc, ada_w, m_ada_w, v_ada_w)
    pk = lambda d: _pack([d[n] for n in names], F32, 8)
    sh_out = _sum_adamw(recv, pk(w), pk(m), pk(v), "sum_adamw_small", recv.shape[1])
    sh_out = [dict(zip(names, _unpack(o, shard_shapes))) for o in sh_out]
    for n, r in zip(LATER, _exchange_wait(sent[0], False, last[-1], "exchange_later_grads_wait")):
        update(n, r)
    (landed,) = _exchange_wait(last, False, big_out[LATER[-1]][0], "exchange_last_grad_wait")
    update("ab_w_in", landed)
    sh_out = [{**d, **{n: big_out[n][i] for n in GATHER_BF16}} for i, d in enumerate(sh_out)]

    def pick(i, n):
        if n == "ada_w":
            return ada_out[i]
        return rep_out[i][n] if n in REPLICATED else sh_out[i][n]

    outs = [loss, grad_x[None]]
    for i in range(4):
        outs += [pick(i, n) for n in WEIGHTS]
    return tuple(outs)
```

```python
import functools
import math

import jax
import jax.numpy as jnp
from jax import lax
from jax.experimental import pallas as pl
from jax.experimental.pallas import tpu as pltpu

F32 = jnp.float32
BF16 = jnp.bfloat16
HI = lax.Precision.HIGHEST

NDEV = 8
T = 2048
D = 1024
DFF = 4096
HEADS = 8
HD = 64
RW = 512
PA = 2048
PB = 1536
PAB = PA + PB
QKV = 4608
DILS = (1, 4, 16)
BLK = 128
ALPHA = 4.0 ** 0.25
LN_EPS = 1e-5
GN_EPS = 64e-5
ADAM_LR, ADAM_B1, ADAM_B2, ADAM_EPS, ADAM_WD, ADAM_STEP = 0.001, 0.9, 0.999, 1e-8, 0.01, 10
VMEM_LIMIT = 56 * 1024 * 1024


def _cp(sem):
    return pltpu.CompilerParams(dimension_semantics=sem, vmem_limit_bytes=VMEM_LIMIT)


def _slot(px, py, pc):
    return 4 * px + 2 * py + pc


def _all_gather(x, name):
    R, C = x.shape

    def body(x_ref, out_ref, send_sems, recv_sems, local_sem):
        xi, yi, ci = lax.axis_index("x"), lax.axis_index("y"), lax.axis_index("c")
        me, sibling = (xi, yi, ci), (xi, yi, 1 - ci)
        chips = [(1 - xi, yi), (xi, 1 - yi), (1 - xi, 1 - yi)]

        def rows(px, py, pc):
            return out_ref.at[_slot(px, py, pc)]

        def copy(k, block, to, src=None):
            return pltpu.make_async_remote_copy(
                src_ref=rows(*block) if src is None else src, dst_ref=rows(*block),
                send_sem=send_sems.at[k], recv_sem=recv_sems.at[k],
                device_id=to, device_id_type=pl.DeviceIdType.MESH)

        mine = pltpu.make_async_copy(x_ref, rows(*me), local_sem)
        mine.start()
        first = [copy(0, me, sibling, src=x_ref)]
        first += [copy(1 + j, me, (*chip, ci), src=x_ref) for j, chip in enumerate(chips)]
        for cp in first:
            cp.start()
        passed = [copy(4 + j, (*chip, ci), sibling) for j, chip in enumerate(chips)]
        for j, chip in enumerate(chips):
            copy(1 + j, (*chip, ci), me).wait_recv()
            passed[j].start()
        copy(0, sibling, me).wait_recv()
        for j, chip in enumerate(chips):
            copy(4 + j, (*chip, 1 - ci), me).wait_recv()
        for cp in first + passed:
            cp.wait_send()
        mine.wait()

    return pl.pallas_call(
        body, name=name,
        out_shape=jax.ShapeDtypeStruct((NDEV, R, C), x.dtype),
        in_specs=[pl.BlockSpec(memory_space=pl.ANY)],
        out_specs=pl.BlockSpec(memory_space=pl.ANY),
        scratch_shapes=[pltpu.SemaphoreType.DMA((7,)), pltpu.SemaphoreType.DMA((7,)),
                        pltpu.SemaphoreType.DMA(())],
    )(x)


def _all_to_all(g, name):
    _, R, C = g.shape

    def body(g_ref, out_ref, send_sems, recv_sems, local_sem):
        xi, yi, ci = lax.axis_index("x"), lax.axis_index("y"), lax.axis_index("c")
        my_slot = _slot(xi, yi, ci)
        mine = pltpu.make_async_copy(g_ref.at[my_slot], out_ref.at[my_slot], local_sem)
        mine.start()
        copies = []
        for k in range(1, 8):
            px = 1 - xi if k & 4 else xi
            py = 1 - yi if k & 2 else yi
            pc = 1 - ci if k & 1 else ci
            peer_slot = _slot(px, py, pc)
            copies.append((
                pltpu.make_async_remote_copy(
                    src_ref=g_ref.at[peer_slot], dst_ref=out_ref.at[my_slot],
                    send_sem=send_sems.at[k - 1], recv_sem=recv_sems.at[k - 1],
                    device_id=(px, py, pc), device_id_type=pl.DeviceIdType.MESH),
                pltpu.make_async_remote_copy(
                    src_ref=g_ref.at[peer_slot], dst_ref=out_ref.at[peer_slot],
                    send_sem=send_sems.at[k - 1], recv_sem=recv_sems.at[k - 1],
                    device_id=(px, py, pc), device_id_type=pl.DeviceIdType.MESH)))
        for send, _ in copies:
            send.start()
        for _, recv in copies:
            recv.wait_recv()
        for send, _ in copies:
            send.wait_send()
        mine.wait()

    return pl.pallas_call(
        body, name=name,
        out_shape=jax.ShapeDtypeStruct((NDEV, R, C), g.dtype),
        in_specs=[pl.BlockSpec(memory_space=pl.ANY)],
        out_specs=pl.BlockSpec(memory_space=pl.ANY),
        scratch_shapes=[pltpu.SemaphoreType.DMA((7,)), pltpu.SemaphoreType.DMA((7,)),
                        pltpu.SemaphoreType.DMA(())],
    )(g)


def _my_slot():
    return _slot(lax.axis_index("x"), lax.axis_index("y"), lax.axis_index("c"))


def _put_own(buf, own, slot):
    return lax.dynamic_update_index_in_dim(buf, own, slot, 0)


def _hbm_call(body, name, ins, out_shapes, n_sems):
    anyspec = pl.BlockSpec(memory_space=pl.ANY)
    return pl.pallas_call(
        body, name=name, out_shape=out_shapes,
        in_specs=[anyspec] * len(ins), out_specs=[anyspec] * len(out_shapes),
        scratch_shapes=[pltpu.SemaphoreType.DMA(s) for s in n_sems],
    )(*ins)


def _all_gather_many(xs, name):
    n = len(xs)

    def body(*refs):
        x_refs, o_refs = refs[:n], refs[n:2 * n]
        send_sems, recv_sems = refs[2 * n:]
        xi, yi, ci = lax.axis_index("x"), lax.axis_index("y"), lax.axis_index("c")
        me, sibling = (xi, yi, ci), (xi, yi, 1 - ci)
        chips = [(1 - xi, yi), (xi, 1 - yi), (1 - xi, 1 - yi)]

        def copy(i, k, block, to, src=None):
            dst = o_refs[i].at[_slot(*block)]
            return pltpu.make_async_remote_copy(
                src_ref=dst if src is None else src, dst_ref=dst,
                send_sem=send_sems.at[i, k], recv_sem=recv_sems.at[i, k],
                device_id=to, device_id_type=pl.DeviceIdType.MESH)

        sends = []
        for i in range(n):
            sends += [copy(i, 1 + j, me, (*chip, ci), src=x_refs[i]) for j, chip in enumerate(chips)]
            sends.append(copy(i, 0, me, sibling, src=x_refs[i]))
        for cp in sends:
            cp.start()
        for j, chip in enumerate(chips):
            for i in range(n):
                copy(i, 1 + j, (*chip, ci), me).wait_recv()
                passed = copy(i, 4 + j, (*chip, ci), sibling)
                passed.start()
                sends.append(passed)
        for i in range(n):
            copy(i, 0, sibling, me).wait_recv()
            for j, chip in enumerate(chips):
                copy(i, 4 + j, (*chip, 1 - ci), me).wait_recv()
        for cp in sends:
            cp.wait_send()

    outs = _hbm_call(body, name, xs, [jax.ShapeDtypeStruct((NDEV,) + x.shape, x.dtype) for x in xs],
                     [(n, 7), (n, 7)])
    return [_put_own(o, x[None], _my_slot()) for o, x in zip(outs, xs)]


def _peers(xi, yi, ci):
    return [(1 - xi if k & 4 else xi, 1 - yi if k & 2 else yi, 1 - ci if k & 1 else ci) for k in range(1, 8)]


def _direct_copy(src_refs, land_refs, send_sems, recv_sems, i, k, peer, my_slot, gather):
    src = src_refs[i] if gather else src_refs[i].at[_slot(*peer)]
    return pltpu.make_async_remote_copy(
        src_ref=src, dst_ref=land_refs[i].at[my_slot], send_sem=send_sems.at[7 * i + k], recv_sem=recv_sems.at[7 * i + k],
        device_id=peer, device_id_type=pl.DeviceIdType.MESH)


def _exchange_start(srcs, gather, name):
    n = len(srcs)
    lands = [lax.empty(((NDEV,) + s.shape) if gather else s.shape, s.dtype) for s in srcs]

    def body(*refs):
        s_refs, l_refs = refs[:n], refs[n:2 * n]
        send_sems, recv_sems = refs[2 * n], refs[2 * n + 1]
        token = refs[2 * n + 2 + 2 * n]
        xi, yi, ci = lax.axis_index("x"), lax.axis_index("y"), lax.axis_index("c")
        my_slot = _slot(xi, yi, ci)
        for k, peer in enumerate(_peers(xi, yi, ci)):
            for i in range(n):
                _direct_copy(s_refs, l_refs, send_sems, recv_sems, i, k, peer, my_slot, gather).start()
        token[...] = jnp.zeros_like(token)

    hbm = pl.BlockSpec(memory_space=pltpu.HBM)
    sem = pl.BlockSpec(memory_space=pltpu.SEMAPHORE)
    both = list(srcs) + lands
    return pl.pallas_call(
        body, name=name,
        out_shape=(pltpu.SemaphoreType.DMA((7 * n,)), pltpu.SemaphoreType.DMA((7 * n,)),
                   *[pltpu.HBM(t.shape, t.dtype) for t in both], jax.ShapeDtypeStruct((8, 128), F32)),
        in_specs=[hbm] * (2 * n),
        out_specs=(sem, sem, *[hbm] * (2 * n), pl.BlockSpec(memory_space=pltpu.VMEM)),
        input_output_aliases={i: 2 + i for i in range(2 * n)},
        compiler_params=pltpu.CompilerParams(has_side_effects=pltpu.SideEffectType.DATAFLOW_SIDE_EFFECTING),
    )(*[pltpu.with_memory_space_constraint(t, pltpu.HBM) for t in both])


def _exchange_wait(started, gather, after, name):
    send_sems, recv_sems, *thru, _ = started
    n = len(thru) // 2

    def body(*refs):
        s_refs, l_refs = refs[:n], refs[n:2 * n]
        send_sems, recv_sems = refs[2 * n], refs[2 * n + 1]
        xi, yi, ci = lax.axis_index("x"), lax.axis_index("y"), lax.axis_index("c")
        my_slot = _slot(xi, yi, ci)
        for k, peer in enumerate(_peers(xi, yi, ci)):
            for i in range(n):
                _direct_copy(s_refs, l_refs, send_sems, recv_sems, i, k, peer, my_slot, gather).wait_send()
                _direct_copy(s_refs, l_refs, send_sems, recv_sems, i, k, peer, _slot(*peer), gather).wait_recv()

    hbm = pl.BlockSpec(memory_space=pltpu.HBM)
    sem = pl.BlockSpec(memory_space=pltpu.SEMAPHORE)
    outs = pl.pallas_call(
        body, name=name,
        out_shape=tuple(pltpu.HBM(t.shape, t.dtype) for t in thru),
        in_specs=[hbm] * (2 * n) + [sem, sem, pl.BlockSpec(memory_space=pl.ANY)],
        out_specs=tuple([hbm] * (2 * n)),
        input_output_aliases={i: i for i in range(2 * n)},
        compiler_params=pltpu.CompilerParams(has_side_effects=pltpu.SideEffectType.DATAFLOW_SIDE_EFFECTING),
    )(*thru, send_sems, recv_sems, after)
    slot = _my_slot()
    own = [s[None] if gather else lax.dynamic_index_in_dim(s, slot, 0, keepdims=True) for s in outs[:n]]
    return [_put_own(land, o, slot) for land, o in zip(outs[n:], own)]


def _mm(name, a, b, tb=False, out=(F32,), epi=None, extras=(), tm=2048, tn=512, tk_cap=2048):
    M, K = a.shape
    N = b.shape[0] if tb else b.shape[1]
    tm, tn = min(tm, M), min(tn, N)
    tk = max(t for t in range(128, min(K, tk_cap) + 1, 128) if K % t == 0)
    assert M % tm == 0 and N % tn == 0 and K % tk == 0, (name, M, N, K)
    nk = K // tk
    ne, no = len(extras), len(out)
    dims = (((1,), (1 if tb else 0,)), ((), ()))
    flipped = [isinstance(o, tuple) for o in out]

    def kern(*refs):
        a_ref, b_ref = refs[:2]
        e_refs = refs[2:2 + ne]
        o_refs = refs[2 + ne:2 + ne + no]

        def finish(acc):
            outs = epi(acc, *[e[...] for e in e_refs]) if epi is not None else (acc,)
            for o_ref, o, flip in zip(o_refs, outs, flipped):
                o_ref[...] = (o.T if flip else o).astype(o_ref.dtype)

        part = lax.dot_general(a_ref[...], b_ref[...], dims, preferred_element_type=F32)
        if nk == 1:
            finish(part)
            return
        acc_ref = refs[-1]
        k = pl.program_id(2)

        @pl.when(k == 0)
        def _():
            acc_ref[...] = part

        @pl.when(k > 0)
        def _():
            acc_ref[...] += part

        @pl.when(k == nk - 1)
        def _():
            finish(acc_ref[...])

    b_spec = (pl.BlockSpec((tn, tk), lambda i, j, k: (j, k)) if tb
              else pl.BlockSpec((tk, tn), lambda i, j, k: (k, j)))
    tile = pl.BlockSpec((tm, tn), lambda i, j, k: (i, j))
    tile_t = pl.BlockSpec((tn, tm), lambda i, j, k: (j, i))
    res = pl.pallas_call(
        kern, name=name, grid=(M // tm, N // tn, nk),
        in_specs=[pl.BlockSpec((tm, tk), lambda i, j, k: (i, k)), b_spec] + [tile] * ne,
        out_specs=[tile_t if flip else tile for flip in flipped],
        out_shape=[jax.ShapeDtypeStruct((N, M), o[0]) if flip else jax.ShapeDtypeStruct((M, N), o)
                   for o, flip in zip(out, flipped)],
        scratch_shapes=[pltpu.VMEM((tm, tn), F32)] if nk > 1 else [],
        compiler_params=_cp(("parallel", "parallel", "arbitrary")),
    )(a, b, *extras)
    return res[0] if no == 1 else res


HALO = 8


def _rows(name, body, rows, params, out_rows, out_accs=(), tm=512):
    views = [r if isinstance(r, tuple) else (r, r.shape[1], 0) for r in rows]
    n = views[0][0].shape[0]
    assert n % tm == 0 and tm % HALO == 0
    nr, npar, nor, noa = len(views), len(params), len(out_rows), len(out_accs)
    kinds = [o[2] if len(o) == 3 else None for o in out_rows]
    store = {None: lambda o: o, "T": lambda o: o.T, "cols3": _cols3_tile}
    spec = {None: lambda w: pl.BlockSpec((tm, w), lambda i: (i, 0)),
            "T": lambda w: pl.BlockSpec((w, tm), lambda i: (0, i)),
            "cols3": lambda w: pl.BlockSpec((HP, HD, 6 * tm), lambda i: (0, 0, i))}
    shape = {None: lambda w: (n, w), "T": lambda w: (w, n), "cols3": lambda w: (HP, HD, 6 * n)}

    def row_spec(width, cb, halo=None):
        per, last = tm // HALO, n // HALO - 1
        if halo == "prev":
            return pl.BlockSpec((HALO, width), lambda i: (jnp.maximum(i * per - 1, 0), cb))
        if halo == "next":
            return pl.BlockSpec((HALO, width), lambda i: (jnp.minimum((i + 1) * per, last), cb))
        return pl.BlockSpec((tm, width), lambda i: (i, cb))

    def kern(*refs):
        r_refs = refs[:nr]
        p_refs = refs[nr:nr + npar]
        o_refs = refs[nr + npar:nr + npar + nor]
        a_refs = refs[nr + npar + nor:]
        outs, accs = body([r[...] for r in r_refs], [p[...] for p in p_refs])
        assert len(outs) == nor and len(accs) == noa, (name, len(outs), len(accs))
        for o_ref, o, kind in zip(o_refs, outs, kinds):
            o_ref[...] = store[kind](o).astype(o_ref.dtype)
        if noa:
            @pl.when(pl.program_id(0) == 0)
            def _():
                for a_ref in a_refs:
                    a_ref[...] = jnp.zeros_like(a_ref)

            for a_ref, a in zip(a_refs, accs):
                a_ref[...] += a.astype(F32)

    def whole(shape):
        nd = len(shape)
        return pl.BlockSpec(tuple(shape), lambda i, nd=nd: (0,) * nd)

    in_specs = [row_spec(*v[1:]) for v in views]
    in_specs += [whole(p.shape) for p in params]
    out_specs = [spec[kind](o[0]) for o, kind in zip(out_rows, kinds)]
    out_specs += [whole(s) for s in out_accs]
    out_shape = [jax.ShapeDtypeStruct(shape[kind](o[0]), o[1]) for o, kind in zip(out_rows, kinds)]
    out_shape += [jax.ShapeDtypeStruct(tuple(s), F32) for s in out_accs]
    res = pl.pallas_call(
        kern, name=name, grid=(n // tm,), in_specs=in_specs, out_specs=out_specs,
        out_shape=out_shape, compiler_params=_cp(("arbitrary",)),
    )(*[v[0] for v in views], *params)
    return res[:nor], res[nor:]


def _shift_down(x, prev, k):
    head = jnp.where(pl.program_id(0) == 0, 0.0, pltpu.roll(prev, k, axis=0))
    row = lax.broadcasted_iota(jnp.int32, x.shape, 0)
    return jnp.where(row < k, jnp.tile(head, (x.shape[0] // HALO, 1)), pltpu.roll(x, k, axis=0))


def _shift_up(x, nxt, k):
    n = x.shape[0]
    tail = jnp.where(pl.program_id(0) == pl.num_programs(0) - 1, 0.0, pltpu.roll(nxt, HALO - k, axis=0))
    row = lax.broadcasted_iota(jnp.int32, x.shape, 0)
    return jnp.where(row >= n - k, jnp.tile(tail, (n // HALO, 1)), pltpu.roll(x, n - k, axis=0))


@jax.custom_vjp
def _headsum(x, e):
    return sum(jnp.dot(p, e, preferred_element_type=F32) for p in _split3(x))


_headsum.defvjp(lambda x, e: (_headsum(x, e), e), lambda e, ct: (_headsum(ct, e), None))


def _softplus(z):
    return jnp.maximum(z, 0.0) + jnp.log(1.0 + jnp.exp(jnp.minimum(z, -z)))


def _post_ln(x, y, g, lng, lnb):
    z = ALPHA * x + (1.0 + g) * y
    mu = jnp.mean(z, axis=-1, keepdims=True)
    zc = z - mu
    var = jnp.mean(zc * zc, axis=-1, keepdims=True)
    return zc * lax.rsqrt(var + LN_EPS) * lng + lnb


def _post_ln_mod(x, y, g, lng, lnb, scn, shn):
    xn = _post_ln(x, y, g, lng, lnb)
    return xn, xn * (1.0 + scn) + shn


def _pre_core(E, r_, k_, v_, wd_, ad_, gd_, r1, k1, v1, wd1, ad1, gd1, h, bg, cg, h1, cg1, h2, cg2,
              mu_r, mu_k, mu_v, mu_wd, mu_ad, mu_gd, w0, w_up, a0, a_up, g_up, k_k, k_a,
              cw0, cw1, cw2):
    def mix(x, x1, mu):
        return x + mu * (x1 - x)

    r, k, v = mix(r_, r1, mu_r), mix(k_, k1, mu_k), mix(v_, v1, mu_v)
    wd, ad, gd = mix(wd_, wd1, mu_wd), mix(ad_, ad1, mu_ad), mix(gd_, gd1, mu_gd)
    logw = -_softplus(-(w0 + jnp.dot(jnp.tanh(wd), w_up, preferred_element_type=F32))) - 0.5
    decay = jnp.exp(-jnp.exp(logw))
    iclr = jax.nn.sigmoid(a0 + jnp.dot(ad, a_up, preferred_element_type=F32))
    gate = jnp.dot(jax.nn.sigmoid(gd), g_up, preferred_element_type=F32)
    kk0 = k * k_k
    nrm = jnp.sqrt(_headsum(kk0 * kk0, E))
    kk = kk0 / jnp.maximum(nrm, 1e-12)
    kh = k * (1.0 + (iclr - 1.0) * k_a)
    yb = bg * (cw2 * (cg * h) + cw1 * (cg1 * h1) + cw0 * (cg2 * h2))
    return r, decay, kh, v, -kk, kk * iclr, gate, yb


def _post_core(E, y, r, kh, v, gate, lnx_g, lnx_b, rk):
    def seg(t):
        return _headsum(t, E)

    mean = seg(y) * (1.0 / HD)
    yc = y - mean
    var = seg(yc * yc) * (1.0 / HD)
    gn = yc * lax.rsqrt(var + GN_EPS) * lnx_g + lnx_b
    bonus = seg(r * kh * rk) * v
    return (gn + bonus) * gate


def _merge_core(o0, o1, o2, l0, l1, l2):
    m = jnp.maximum(jnp.maximum(l0, l1), l2)
    e0, e1, e2 = jnp.exp(l0 - m), jnp.exp(l1 - m), jnp.exp(l2 - m)
    den = e0 + e1 + e2
    return (e0 * o0 + e1 * o1 + e2 * o2) / den


CHUNK = 128
HALF = 64
HP = HEADS // 2
LW = 2 * HD
NCHUNK = T // CHUNK


def _split3(x):
    hi = x.astype(BF16)
    r1 = x - hi.astype(F32)
    mid = r1.astype(BF16)
    return hi, mid, (r1 - mid.astype(F32)).astype(BF16)


def _cols3_tile(x):
    left = lax.broadcasted_iota(jnp.int32, (HD, CHUNK), 1) < HALF
    xts = [x[c * CHUNK:(c + 1) * CHUNK].T for c in range(x.shape[0] // CHUNK)]
    pairs = []
    for p in range(HP):
        groups = []
        for xt in xts:
            a, b = xt[p * LW:p * LW + HD], xt[p * LW + HD:(p + 1) * LW]
            for half in (jnp.where(left, a, pltpu.roll(b, HALF, axis=1)), jnp.where(left, pltpu.roll(a, HALF, axis=1), b)):
                groups += list(_split3(half))
        pairs.append(jnp.concatenate(groups, axis=1))
    return jnp.stack(pairs)


def _pick_codes():
    row = lax.broadcasted_iota(jnp.int32, (6 * HALF, LW), 0)
    col = lax.broadcasted_iota(jnp.int32, (6 * HALF, LW), 1)
    same = ((row & (LW - 1)) >= HALF) == (col >= HD)
    return jnp.where(same, row & (HALF - 1), -1).astype(BF16)


def _column(block_ref, codes, half, i):
    pick = jnp.where(codes == i.astype(BF16), jnp.ones((), BF16), jnp.zeros((), BF16))
    block = block_ref[:, :, half * 6 * HALF:(half + 1) * 6 * HALF].reshape(HP * HD, 6 * HALF)
    return jnp.dot(block, pick, preferred_element_type=F32)


def _halfsums(x, row, left1):
    row_l = jnp.where(left1, row, 0.0)
    return (jnp.sum(x * row_l, axis=1, keepdims=True), jnp.sum(x * (row - row_l), axis=1, keepdims=True))


def _pair_rows(row):
    return [row[:, p * LW:(p + 1) * LW] for p in range(HP)]


def _store_columns(ref, p, t_mask, cols):
    for j, col in enumerate(cols):
        pltpu.store(ref.at[pl.ds(2 * p + j, 1)], jnp.broadcast_to(col[None], (1, HD, CHUNK)), mask=t_mask[None])


def _columns_to_rows(cols_ref, rows_ref):
    for p in range(HP):
        rows_ref[:, p * LW:(p + 1) * LW] = cols_ref[2 * p:2 * p + 2].reshape(LW, CHUNK).T


NHALF = T // HALF
HALVES = CHUNK // HALF


def _scan_fwd(r, w, k, a, b, v3):
    def kern(r_ref, w_ref, k_ref, a_ref, b_ref, v_ref, y_ref, ck_ref, st_hbm, sa_hbm,
             s_ref, vb_ref, yc_ref, st_ref, sa_ref, sems):
        c = pl.program_id(0)

        @pl.when(c == 0)
        def _():
            s_ref[...] = jnp.zeros_like(s_ref)

        lane = lax.broadcasted_iota(jnp.int32, (HD, CHUNK), 1)
        left = lane < HD
        left1 = lax.broadcasted_iota(jnp.int32, (1, LW), 1) < HD
        codes = _pick_codes()

        def flush(slot, half_index):
            return [pltpu.make_async_copy(src.at[slot], dst.at[half_index], sems.at[j, slot])
                    for j, (src, dst) in enumerate(((st_ref, st_hbm), (sa_ref, sa_hbm)))]

        for half in range(HALVES):
            ck_ref[half] = s_ref[...]
            vb_ref[...] = _column(v_ref, codes, half, jnp.int32(0))

            @pl.when(c > 0)
            def _():
                for cp in flush(half, (c - 1) * HALVES + half):
                    cp.wait()

            def step(i, carry):
                t = half * HALF + i
                row = lambda ref: _pair_rows(ref[pl.ds(t, 1), :])
                S = [s_ref[p] for p in range(HP)]
                sa = [jnp.where(left, *_halfsums(s, a, left1)) for s, a in zip(S, row(a_ref))]
                S = [s * w + c_ * b + vb_ref[pl.ds(p * HD, HD), :] * k
                     for p, (s, w, c_, b, k) in enumerate(zip(S, row(w_ref), sa, row(b_ref), row(k_ref)))]
                for p, (s, c_) in enumerate(zip(S, sa)):
                    s_ref[p] = s
                    st_ref[half, i, p] = s
                    sa_ref[half, i, p] = c_
                for p, (s, r) in enumerate(zip(S, row(r_ref))):
                    _store_columns(yc_ref, p, lane == t, _halfsums(s, r, left1))
                vb_ref[...] = _column(v_ref, codes, half, i + 1)
                return carry

            lax.fori_loop(0, HALF, step, 0, unroll=16)
            for cp in flush(half, c * HALVES + half):
                cp.start()
        _columns_to_rows(yc_ref, y_ref)

        @pl.when(c == NCHUNK - 1)
        def _():
            for half in range(HALVES):
                for cp in flush(half, c * HALVES + half):
                    cp.wait()

    rowblk = pl.BlockSpec((CHUNK, RW), lambda c: (c, 0))
    saved = jax.ShapeDtypeStruct((NHALF, HALF, HP, HD, LW), F32)
    stage = pltpu.VMEM((HALVES, HALF, HP, HD, LW), F32)
    return pl.pallas_call(
        kern, name="rwkv_scan_fwd", grid=(NCHUNK,),
        in_specs=[rowblk] * 5 + [pl.BlockSpec((HP, HD, 6 * CHUNK), lambda c: (0, 0, c))],
        out_specs=[rowblk, pl.BlockSpec((HALVES, HP, HD, LW), lambda c: (c, 0, 0, 0)),
                   pl.BlockSpec(memory_space=pl.ANY), pl.BlockSpec(memory_space=pl.ANY)],
        out_shape=[jax.ShapeDtypeStruct((T, RW), F32), jax.ShapeDtypeStruct((NHALF, HP, HD, LW), F32), saved, saved],
        scratch_shapes=[pltpu.VMEM((HP, HD, LW), F32), pltpu.VMEM((HP * HD, LW), F32),
                        pltpu.VMEM((HEADS, HD, CHUNK), F32), stage, stage, pltpu.SemaphoreType.DMA((2, HALVES))],
        compiler_params=_cp(("arbitrary",)),
    )(r, w, k, a, b, v3)


def _scan_bwd(r, w, k, a, b, v3, dy3, ck, st, sa):
    def kern(r_ref, w_ref, k_ref, a_ref, b_ref, v_ref, dy_ref, ck_ref, st_hbm, sa_hbm,
             dr_ref, dw_ref, dk_ref, da_ref, db_ref, dv_ref, ds_ref, sb_ref, sa_ref, pick_ref, dvc_ref, sems):
        c = pl.program_id(0)
        chunk = NCHUNK - 1 - c

        @pl.when(c == 0)
        def _():
            ds_ref[...] = jnp.zeros_like(ds_ref)

        lane = lax.broadcasted_iota(jnp.int32, (HD, CHUNK), 1)
        left = lane < HD
        left1 = lax.broadcasted_iota(jnp.int32, (1, LW), 1) < HD
        codes = _pick_codes()

        def rowsum(x):
            return jnp.sum(x, axis=0, keepdims=True)

        def fetch(slot, half_index):
            return [pltpu.make_async_copy(st_hbm.at[half_index], sb_ref.at[slot, pl.ds(1, HALF)], sems.at[0, slot]),
                    pltpu.make_async_copy(sa_hbm.at[half_index], sa_ref.at[slot], sems.at[1, slot])]

        def picks(half, i):
            pick_ref[pl.ds(0, HP * HD), :] = _column(v_ref, codes, half, i)
            pick_ref[pl.ds(HP * HD, HP * HD), :] = _column(dy_ref, codes, half, i)

        @pl.when(c == 0)
        def _():
            for cp in fetch(HALVES - 1, chunk * HALVES + HALVES - 1):
                cp.start()

        for half in reversed(range(HALVES)):
            base = half * HALF
            for cp in fetch(half, chunk * HALVES + half):
                cp.wait()
            if half:
                for cp in fetch(half - 1, chunk * HALVES + half - 1):
                    cp.start()
            else:
                @pl.when(chunk > 0)
                def _():
                    for cp in fetch(HALVES - 1, chunk * HALVES - 1):
                        cp.start()
            sb_ref[half, 0] = ck_ref[half]
            picks(half, jnp.int32(HALF - 1))

            def back(ii, carry):
                i = HALF - 1 - ii
                t = base + i
                row = lambda ref: _pair_rows(ref[pl.ds(t, 1), :])
                a_r, b_r, k_r, w_r, r_r = row(a_ref), row(b_ref), row(k_ref), row(w_ref), row(r_ref)
                vs = [pick_ref[pl.ds(p * HD, HD), :] for p in range(HP)]
                dys = [pick_ref[pl.ds((HP + p) * HD, HD), :] for p in range(HP)]
                picks(half, jnp.maximum(i - 1, 0))
                dr, dw, db, dk, da = [], [], [], [], []
                for p in range(HP):
                    Sp, dy = sb_ref[half, i, p], dys[p]
                    dS = ds_ref[p] + dy * r_r[p]
                    dr.append(rowsum(sb_ref[half, i + 1, p] * dy))
                    dw.append(rowsum(dS * Sp))
                    db.append(rowsum(dS * sa_ref[half, i, p]))
                    dk.append(rowsum(dS * vs[p]))
                    dsa = jnp.where(left, *_halfsums(dS, b_r[p], left1))
                    _store_columns(dvc_ref, p, lane == t, _halfsums(dS, k_r[p], left1))
                    da.append(rowsum(Sp * dsa))
                    ds_ref[p] = dS * w_r[p] + dsa * a_r[p]
                for ref, pieces in ((dr_ref, dr), (dw_ref, dw), (db_ref, db), (dk_ref, dk), (da_ref, da)):
                    ref[pl.ds(t, 1), :] = jnp.concatenate(pieces, axis=1)
                return carry

            lax.fori_loop(0, HALF, back, 0, unroll=16)
        _columns_to_rows(dvc_ref, dv_ref)

    rowblk = pl.BlockSpec((CHUNK, RW), lambda c: (NCHUNK - 1 - c, 0))
    col3blk = pl.BlockSpec((HP, HD, 6 * CHUNK), lambda c: (0, 0, NCHUNK - 1 - c))
    rowshape = jax.ShapeDtypeStruct((T, RW), F32)
    return pl.pallas_call(
        kern, name="rwkv_scan_bwd", grid=(NCHUNK,),
        in_specs=[rowblk] * 5 + [col3blk, col3blk,
                                 pl.BlockSpec((HALVES, HP, HD, LW), lambda c: (NCHUNK - 1 - c, 0, 0, 0)),
                                 pl.BlockSpec(memory_space=pl.ANY), pl.BlockSpec(memory_space=pl.ANY)],
        out_specs=[rowblk] * 6, out_shape=[rowshape] * 6,
        scratch_shapes=[pltpu.VMEM((HP, HD, LW), F32), pltpu.VMEM((HALVES, HALF + 1, HP, HD, LW), F32),
                        pltpu.VMEM((HALVES, HALF, HP, HD, LW), F32), pltpu.VMEM((2 * HP * HD, LW), F32),
                        pltpu.VMEM((HEADS, HD, CHUNK), F32), pltpu.SemaphoreType.DMA((2, HALVES))],
        compiler_params=_cp(("arbitrary",)),
    )(r, w, k, a, b, v3, dy3, ck, st, sa)


NT = (((1,), (1,)), ((), ()))
TN = (((0,), (0,)), ((), ()))
SCALE = HD ** -0.5
QKV_G = 3 * RW


def _attn_setup(g):
    dil = DILS[g]
    qkv = [pl.BlockSpec((T, LW), lambda hp, c=(g * QKV_G + s * RW) // LW: (0, c + hp)) for s in range(3)]
    tile = pl.BlockSpec((T, LW), lambda hp: (0, hp))
    bias = pl.BlockSpec((2, BLK, 2 * BLK), lambda hp: (hp, 0, 0))

    def blocks():
        for r in range(dil):
            for n in range(T // dil // BLK):
                rows = pl.ds(n * BLK * dil + r, BLK, stride=dil)
                keys = pl.ds((n - 1) * BLK * dil + r, 2 * BLK, stride=dil) if n else rows
                yield n, rows, keys

    return qkv, tile, bias, blocks


def _band(n):
    qi = lax.broadcasted_iota(jnp.int32, (BLK, 2 * BLK), 0)
    ki = lax.broadcasted_iota(jnp.int32, (BLK, 2 * BLK), 1)
    band = (ki >= qi) & (ki <= qi + BLK)
    return band if n else band[:, BLK:]


def _head_masks():
    lane = lax.broadcasted_iota(jnp.int32, (BLK, LW), 1)
    return lane < HD, [(lane < HD).astype(BF16), (lane >= HD).astype(BF16)]


def _attn_fwd(pq, bias, g):
    qkv, tile, bias_spec, blocks = _attn_setup(g)

    def kern(q_ref, k_ref, v_ref, b_ref, o_ref, l_ref):
        left, masks = _head_masks()
        for n, rows, keys in blocks():
            qb, kc, vc = q_ref[rows, :].astype(BF16), k_ref[keys, :].astype(BF16), v_ref[keys, :].astype(BF16)
            valid = _band(n)
            o, lse = [], []
            for j in range(2):
                bias_j = b_ref[j] if n else b_ref[j][:, BLK:]
                s = lax.dot_general(qb * masks[j], kc, NT, preferred_element_type=F32) * SCALE + bias_j
                s = jnp.where(valid, s, -jnp.inf)
                m = jnp.max(s, axis=1, keepdims=True)
                e = jnp.exp(s - m)
                den = jnp.sum(e, axis=1, keepdims=True)
                o.append(jnp.dot((e / den).astype(BF16), vc, preferred_element_type=F32))
                lse.append(m + jnp.log(den))
            o_ref[rows, :] = jnp.where(left, o[0], o[1])
            l_ref[rows, :] = jnp.where(left, lse[0], lse[1])

    shape = jax.ShapeDtypeStruct((T, RW), F32)
    return pl.pallas_call(
        kern, name=f"attn_fwd_{g}", grid=(HP,),
        in_specs=qkv + [bias_spec], out_specs=[tile, tile], out_shape=[shape, shape],
        compiler_params=_cp(("parallel",)),
    )(pq, pq, pq, bias)


def _attn_bwd(pq, bias, do, o, lse, dlse, g):
    qkv, tile, bias_spec, blocks = _attn_setup(g)

    def kern(q_ref, k_ref, v_ref, b_ref, do_ref, o_ref, l_ref, dl_ref, dq_ref, dk_ref, dv_ref, db_ref):
        left, masks = _head_masks()
        lane = lax.broadcasted_iota(jnp.int32, (BLK, LW), 1)
        dk_ref[...] = jnp.zeros_like(dk_ref)
        dv_ref[...] = jnp.zeros_like(dv_ref)
        db_ref[...] = jnp.zeros_like(db_ref)

        def column(tile_, j):
            return jnp.sum(jnp.where(lane == j * HD, tile_, 0.0), axis=1, keepdims=True)

        for n, rows, keys in blocks():
            qb, kc, vc = q_ref[rows, :].astype(BF16), k_ref[keys, :].astype(BF16), v_ref[keys, :].astype(BF16)
            dof, valid = do_ref[rows, :], _band(n)
            dob, terms = dof.astype(BF16), dl_ref[rows, :] - dof * o_ref[rows, :]
            dq = []
            for j in range(2):
                bias_j = b_ref[j] if n else b_ref[j][:, BLK:]
                corr = jnp.sum(terms * masks[j].astype(F32), axis=1, keepdims=True)
                qm, dom = qb * masks[j], dob * masks[j]
                s = lax.dot_general(qm, kc, NT, preferred_element_type=F32) * SCALE + bias_j
                p = jnp.where(valid, jnp.exp(s - column(l_ref[rows, :], j)), 0.0)
                dp = lax.dot_general(dom, vc, NT, preferred_element_type=F32)
                ds = p * (dp + corr)
                if n:
                    db_ref[j] += ds
                else:
                    db_ref[j, :, BLK:] += ds
                dsb = (ds * SCALE).astype(BF16)
                dq.append(jnp.dot(dsb, kc, preferred_element_type=F32))
                dk_ref[keys, :] += lax.dot_general(dsb, qm, TN, preferred_element_type=F32)
                dv_ref[keys, :] += lax.dot_general(p.astype(BF16), dom, TN, preferred_element_type=F32)
            dq_ref[rows, :] = jnp.where(left, dq[0], dq[1])

    shape = jax.ShapeDtypeStruct((T, RW), F32)
    return pl.pallas_call(
        kern, name=f"attn_bwd_{g}", grid=(HP,),
        in_specs=qkv + [bias_spec] + [tile] * 4, out_specs=[tile] * 3 + [bias_spec],
        out_shape=[shape] * 3 + [jax.ShapeDtypeStruct((HEADS, BLK, 2 * BLK), F32)],
        compiler_params=_cp(("parallel",)),
    )(pq, pq, pq, bias, do, o, lse, dlse)


NBUCKET = 32
NPAIR = BLK * 2 * BLK


def _relbias_table(rbT, onehotT):
    def kern(rb_ref, oh_ref, out_ref):
        out_ref[0] = sum(jnp.dot(p, oh_ref[0], preferred_element_type=F32) for p in _split3(rb_ref[0]))

    return pl.pallas_call(
        kern, name="relbias_table", grid=(3,),
        in_specs=[pl.BlockSpec((1, HEADS, NBUCKET), lambda g: (g, 0, 0)),
                  pl.BlockSpec((1, NBUCKET, NPAIR), lambda g: (g, 0, 0))],
        out_specs=pl.BlockSpec((1, HEADS, NPAIR), lambda g: (g, 0, 0)),
        out_shape=jax.ShapeDtypeStruct((3, HEADS, NPAIR), F32),
        compiler_params=_cp(("parallel",)),
    )(rbT, onehotT)


def _relbias_grad(db, onehotT):
    nt = (((1,), (1,)), ((), ()))

    def kern(db_ref, oh_ref, out_ref):
        hi, mid, _ = _split3(db_ref[0])
        out_ref[0] = (lax.dot_general(hi, oh_ref[0], nt, preferred_element_type=F32)
                      + lax.dot_general(mid, oh_ref[0], nt, preferred_element_type=F32))

    return pl.pallas_call(
        kern, name="relbias_grad", grid=(3,),
        in_specs=[pl.BlockSpec((1, HEADS, NPAIR), lambda g: (g, 0, 0)),
                  pl.BlockSpec((1, NBUCKET, NPAIR), lambda g: (g, 0, 0))],
        out_specs=pl.BlockSpec((1, HEADS, NBUCKET), lambda g: (g, 0, 0)),
        out_shape=jax.ShapeDtypeStruct((3, HEADS, NBUCKET), F32),
        compiler_params=_cp(("parallel",)),
    )(db, onehotT)


def _adamw(w, g, m, v):
    m2 = ADAM_B1 * m + (1.0 - ADAM_B1) * g
    v2 = ADAM_B2 * v + (1.0 - ADAM_B2) * (g * g)
    m_hat = m2 / (1.0 - ADAM_B1 ** ADAM_STEP)
    v_hat = v2 / (1.0 - ADAM_B2 ** ADAM_STEP)
    return -ADAM_LR * (m_hat / (jnp.sqrt(v_hat) + ADAM_EPS) + ADAM_WD * w), m2, v2


def _ada_mod(c_all, ada_w, ada_b_loc):
    def kern(c_ref, w_ref, b_ref, o_ref):
        c = c_ref[...]
        cond = c * jax.nn.sigmoid(c)
        o_ref[0] = jnp.dot(cond, w_ref[0], precision=HI, preferred_element_type=F32) + b_ref[0]

    ncol = ada_w.shape[2]
    return pl.pallas_call(
        kern, name="ada_mod", grid=(2,),
        in_specs=[pl.BlockSpec((NDEV, D), lambda i: (0, 0)),
                  pl.BlockSpec((1, D, ncol), lambda i: (i, 0, 0)),
                  pl.BlockSpec((1, 1, ncol), lambda i: (i, 0, 0))],
        out_specs=pl.BlockSpec((1, NDEV, ncol), lambda i: (i, 0, 0)),
        out_shape=jax.ShapeDtypeStruct((2, NDEV, ncol), F32),
        compiler_params=_cp(("parallel",)),
    )(c_all, ada_w, ada_b_loc.reshape(2, 1, ncol))


def _ada_grad_adamw(cT_all, dmod_loc, w, m, v):
    ncol = w.shape[2]
    tr = 256

    def kern(c_ref, d_ref, w_ref, m_ref, v_ref, g_ref, dl_ref, m2_ref, v2_ref):
        c = c_ref[...]
        cond = c * jax.nn.sigmoid(c)
        g = jnp.dot(cond, d_ref[0], precision=HI, preferred_element_type=F32)
        dl, m2, v2 = _adamw(w_ref[0], g, m_ref[0], v_ref[0])
        g_ref[0], dl_ref[0], m2_ref[0], v2_ref[0] = g, dl, m2, v2

    big = pl.BlockSpec((1, tr, ncol), lambda i, j: (i, j, 0))
    shp = jax.ShapeDtypeStruct(w.shape, F32)
    return pl.pallas_call(
        kern, name="ada_grad_adamw", grid=(2, D // tr),
        in_specs=[pl.BlockSpec((tr, NDEV), lambda i, j: (j, 0)),
                  pl.BlockSpec((1, NDEV, ncol), lambda i, j: (i, 0, 0)), big, big, big],
        out_specs=[big] * 4, out_shape=[shp] * 4,
        compiler_params=_cp(("parallel", "parallel")),
    )(cT_all, dmod_loc, w, m, v)


def _sum_adamw(recv, w, m, v, name, tr):
    S = recv.shape[0]
    R, C = w.shape
    assert R % tr == 0 and recv.shape[1:] == (R, C)

    def kern(r_ref, w_ref, m_ref, v_ref, g_ref, dl_ref, m2_ref, v2_ref):
        g = r_ref[0].astype(F32)
        for s in range(1, S):
            g = g + r_ref[s].astype(F32)
        dl, m2, v2 = _adamw(w_ref[...], g, m_ref[...], v_ref[...])
        g_ref[...], dl_ref[...], m2_ref[...], v2_ref[...] = g, dl, m2, v2

    flat = pl.BlockSpec((tr, C), lambda i: (i, 0))
    shp = jax.ShapeDtypeStruct((R, C), F32)
    return pl.pallas_call(
        kern, name=name, grid=(R // tr,),
        in_specs=[pl.BlockSpec((S, tr, C), lambda i: (0, i, 0)), flat, flat, flat],
        out_specs=[flat] * 4, out_shape=[shp] * 4,
        compiler_params=_cp(("parallel",)),
    )(recv, w, m, v)


def _pack(arrs, dtype, row_mult):
    flat = jnp.concatenate([a.reshape(-1).astype(dtype) for a in arrs])
    flat = jnp.pad(flat, (0, -flat.shape[0] % (128 * row_mult)))
    return flat.reshape(-1, 128)


def _pack8(arrs, dtype, row_mult):
    flat = jnp.concatenate([a.reshape(NDEV, -1).astype(dtype) for a in arrs], axis=1)
    flat = jnp.pad(flat, ((0, 0), (0, -flat.shape[1] % (128 * row_mult))))
    return flat.reshape(NDEV, -1, 128)


def _unpack(buf, shapes, lead=()):
    flat = buf.reshape(lead + (-1,))
    out, off = [], 0
    for s in shapes:
        n = math.prod(s)
        out.append(flat[..., off:off + n].reshape(lead + tuple(s)))
        off += n
    return out


def _to_chunks(full, kind):
    if kind == "col":
        x = full.reshape(full.shape[:-1] + (NDEV, full.shape[-1] // NDEV))
        return jnp.moveaxis(x, -2, 0)
    x = full.reshape(full.shape[:-2] + (NDEV, full.shape[-2] // NDEV, full.shape[-1]))
    return jnp.moveaxis(x, -3, 0)


def _from_chunks(g8, kind):
    if kind == "col":
        x = jnp.moveaxis(g8, 0, -2)
        return x.reshape(x.shape[:-2] + (x.shape[-2] * x.shape[-1],))
    x = jnp.moveaxis(g8, 0, -3)
    return x.reshape(x.shape[:-3] + (x.shape[-3] * x.shape[-2], x.shape[-1]))


def _pad_pa(x):
    z = lambda n: jnp.zeros(x.shape[:-1] + (n,), x.dtype)
    return jnp.concatenate([x[..., :1600], z(64), x[..., 1600:1664], z(64), x[..., 1664:1824], z(96)], -1)


def _unpad_pa(x):
    return jnp.concatenate([x[..., :1600], x[..., 1664:1728], x[..., 1792:1952]], -1)


AB_SEGMENTS = ((0, 1600, 0), (1600, 1664, 64), (1664, 1824, 128), (1824, 3360, PAB - 3360))
AB_SHARD = 3360 // NDEV


def _ab_in_padded(g8):
    blocks, at = [], 0
    for start, end, shift in AB_SEGMENTS:
        if start + shift > at:
            blocks.append(jnp.zeros((g8.shape[1], start + shift - at), g8.dtype))
        for j in range(start // AB_SHARD, (end - 1) // AB_SHARD + 1):
            lo, hi = max(start, j * AB_SHARD), min(end, (j + 1) * AB_SHARD)
            blocks.append(g8[j, :, lo - j * AB_SHARD:hi - j * AB_SHARD])
        at = end + shift
    return jnp.concatenate(blocks, axis=1)


def _ab_in_shards(padded):
    shards = []
    for j in range(NDEV):
        pieces = [padded[:, max(start, j * AB_SHARD) + shift:min(end, (j + 1) * AB_SHARD) + shift]
                  for start, end, shift in AB_SEGMENTS if max(start, j * AB_SHARD) < min(end, (j + 1) * AB_SHARD)]
        shards.append(jnp.concatenate(pieces, axis=1))
    return jnp.stack(shards)


def _pad_rows(x, n):
    return jnp.pad(x, ((0, n - x.shape[0]), (0, 0)))


def _bucket_tables():
    qi = jnp.arange(BLK)[:, None]
    ki = jnp.arange(2 * BLK)[None, :]
    rel = BLK + qi - ki
    tabs = []
    for dil in DILS:
        dist = jnp.clip(rel, 0, BLK) * dil
        logd = jnp.log(jnp.maximum(dist, 1).astype(F32) / 16) / math.log(2048 / 16)
        large = jnp.minimum(16 + (logd * 16).astype(jnp.int32), 31)
        tabs.append(jnp.where(dist < 16, dist, large))
    return jnp.stack(tabs)


SHARDED = (("ln_g", "col"), ("ln_b", "col"), ("ab_w_in", "col"), ("rw_w_up", "col"), ("rw_a_up", "col"),
           ("rw_g_up", "col"), ("sc_conv_w", "col"), ("ab_w_out", "row"), ("dil_w_qkv", "col"),
           ("dil_w_out", "col"), ("mlp_w1", "col"), ("mlp_w2", "row"))
FIRST = ("ab_w_in",)
LATER = ("ab_w_out", "dil_w_qkv", "dil_w_out", "mlp_w1", "mlp_w2")
GATHER_BF16 = FIRST + LATER
GATHER_F32 = ("rw_w_up", "rw_a_up", "rw_g_up", "sc_conv_w", "ln_g", "ln_b")
REPLICATED = ("ada_b", "rw_mu", "rw_w0", "rw_a0", "rw_k_k", "rw_k_a", "rw_r_k", "rw_lnx_g", "rw_lnx_b", "rel_bias")
WEIGHTS = ("ada_w", "ada_b", "ln_g", "ln_b", "ab_w_in", "rw_mu", "rw_w0", "rw_w_up", "rw_a0", "rw_a_up",
           "rw_g_up", "rw_k_k", "rw_k_a", "rw_r_k", "rw_lnx_g", "rw_lnx_b", "sc_conv_w", "ab_w_out",
           "dil_w_qkv", "dil_w_out", "rel_bias", "mlp_w1", "mlp_w2")


def _local_step(x0, tgt, mod, W, P, later_weights, early_grads):
    row = lambda a: a.reshape(1, -1)
    W = dict(W)
    m6 = mod.reshape(2, 6, 1, D)
    sc = [m6[0, 1], m6[0, 4], m6[1, 1], m6[1, 4]]
    sh = [m6[0, 0], m6[0, 3], m6[1, 0], m6[1, 3]]
    gt = [m6[0, 2], m6[0, 5], m6[1, 2], m6[1, 5]]
    lng = [row(P["ln_g"][0, 0]), row(P["ln_g"][0, 1]), row(P["ln_g"][1, 0]), row(P["ln_g"][1, 1])]
    lnb = [row(P["ln_b"][0, 0]), row(P["ln_b"][0, 1]), row(P["ln_b"][1, 0]), row(P["ln_b"][1, 1])]
    E = jnp.kron(jnp.eye(HEADS, dtype=BF16), jnp.ones((HD, HD), BF16))

    def mod_body(r, p):
        u = r[0] * (1.0 + p[0]) + p[1]
        return [u, u], []

    (u0, u0T), _ = _rows("modulate", mod_body, [x0], [sc[0], sh[0]], [(D, BF16), (D, BF16, "T")])

    def post_fwd_body(r, p):
        xn, un = _post_ln_mod(r[0], r[1], *p)
        return [xn, un, un], []

    def post_fwd(s, x, y):
        (xn, un, unT), _ = _rows(f"post_ln_{s}", post_fwd_body, [x, y],
                                 [gt[s], lng[s], lnb[s], sc[s + 1], sh[s + 1]],
                                 [(D, F32), (D, BF16), (D, BF16, "T")])
        return xn, un, unT

    def relu2(acc):
        a = jnp.maximum(acc, 0.0)
        return acc, a * a, a * a

    def relu2_bwd(acc, h):
        return (acc * (2.0 * jnp.maximum(h, 0.0)),)

    p = _mm("ab_in", u0, W["ab_w_in"])
    mu = _pad_pa(P["rw_mu"])
    mu_parts = [mu[:, :512], mu[:, 512:1024], mu[:, 1024:1536], mu[:, 1536:1664], mu[:, 1664:1792], mu[:, 1792:]]
    pre_params = mu_parts + [P["rw_w0"], _pad_rows(P["rw_w_up"], 128), P["rw_a0"], _pad_rows(P["rw_a_up"], 128),
                             _pad_rows(P["rw_g_up"], 256), P["rw_k_k"], P["rw_k_a"],
                             P["sc_conv_w"][0:1], P["sc_conv_w"][1:2], P["sc_conv_w"][2:3]]
    pieces = [(p, 512, 0), (p, 512, 1), (p, 512, 2), (p, 128, 12), (p, 128, 13), (p, 256, 7),
              (p, 512, 4), (p, 512, 5), (p, 512, 6)]
    shifted = [0, 1, 2, 3, 4, 5, 6, 8]
    pre_rows = pieces + [pieces[i] + ("prev",) for i in shifted]
    NPR = 19

    def pre_args(r):
        x, prev = r[:9], dict(zip(shifted, r[9:17]))
        down = lambda i, k: _shift_down(x[i], prev[i], k)
        return x[:6] + [down(i, 1) for i in range(6)] + x[6:9] + [down(6, 1), down(8, 1), down(6, 2), down(8, 2)]

    def pre_fwd_body(r, pp):
        outs = list(_pre_core(pp[0], *pre_args(r), *pp[1:]))
        return outs + [outs[3]], []

    (r_, w_, kh_, v_, a_, b_, gate_, yb, v_cols), _ = _rows(
        "rwkv_pre", pre_fwd_body, pre_rows, [E] + pre_params,
        [(RW, F32)] * 7 + [(RW, BF16), (RW, BF16, "cols3")], tm=256)
    scan_in = [r_, w_, kh_, a_, b_, v_cols]
    ysc, *saved = _scan_fwd(*scan_in)
    post_params = [P["rw_lnx_g"], P["rw_lnx_b"], P["rw_r_k"].reshape(1, RW)]

    def postmix_fwd_body(r, pp):
        return [_post_core(pp[0], *r, *pp[1:])], []

    (ya,), _ = _rows("rwkv_post", postmix_fwd_body, [ysc, r_, kh_, v_, gate_], [E] + post_params,
                     [(RW, BF16)], tm=256)
    cat = jnp.concatenate([ya, yb], axis=1)
    W.update(later_weights(cat))
    y0 = _mm("ab_out", cat, W["ab_w_out"])
    x1, u1, u1T = post_fwd(0, x0, y0)

    h1, a1, a1T = _mm("mlp1_up_0", u1, W["mlp_w1"][0], out=(F32, BF16, (BF16, "T")), epi=relu2)
    y1 = _mm("mlp1_down_0", a1, W["mlp_w2"][0])
    x2, u2, u2T = post_fwd(1, x1, y1)

    pq = _mm("qkv", u2, W["dil_w_qkv"])
    onehotT = (_bucket_tables().reshape(3, 1, NPAIR) == jnp.arange(NBUCKET).reshape(1, NBUCKET, 1)).astype(BF16)
    rbT = P["rel_bias"].reshape(NBUCKET, 3, HEADS).transpose(1, 2, 0)
    bias = _relbias_table(rbT, onehotT).reshape(3, HEADS, BLK, 2 * BLK)
    og, lse = zip(*[_attn_fwd(pq, bias[g], g) for g in range(3)])

    def merge_fwd_body(r, pp):
        return [_merge_core(*r)], []

    (om,), _ = _rows("attn_merge", merge_fwd_body, list(og + lse), [], [(RW, BF16)])
    y2 = _mm("dil_out", om, W["dil_w_out"])
    x3, u3, u3T = post_fwd(2, x2, y2)

    h3, a3, a3T = _mm("mlp1_up_1", u3, W["mlp_w1"][1], out=(F32, BF16, (BF16, "T")), epi=relu2)
    y3 = _mm("mlp1_down_1", a3, W["mlp_w2"][1])

    def last_body(r, pp):
        x, y, tg = r
        xn, vjp = jax.vjp(_post_ln, x, y, *pp)
        err = xn - tg
        dx, dy, dg, dlg, dlb = vjp(err * (1.0 / D))
        loss = jnp.full((1, 128), (0.5 / D) * jnp.sum(err * err), F32)
        return [dx, dy], [loss, dg, dlg, dlb]

    (dxp, dy3), (loss_acc, dg3, dlng3, dlnb3) = _rows(
        "final_ln_loss", last_body, [x3, y3, tgt], [gt[3], lng[3], lnb[3]],
        [(D, F32), (D, BF16)], [(1, 128), (1, D), (1, D), (1, D)])

    G = {}
    dsc, dsh, dgt = [None] * 4, [None] * 4, [None] * 4
    dlng, dlnb = [None] * 4, [None] * 4
    dgt[3], dlng[3], dlnb[3] = dg3, dlng3, dlnb3

    def mlp_bwd(i, uT, h, aT, dy):
        dh = _mm(f"mlp_dh_{i}", dy, W["mlp_w2"][i], tb=True, out=(BF16,), epi=relu2_bwd, extras=(h,))
        gw2 = _mm(f"mlp_dw2_{i}", aT, dy)
        du = _mm(f"mlp_du_{i}", dh, W["mlp_w1"][i], tb=True)
        gw1 = _mm(f"mlp_dw1_{i}", uT, dh)
        return du, gw1, gw2

    def post_bwd_body(r, pp):
        x, y, dxn, dun = r
        _, vjp = jax.vjp(_post_ln_mod, x, y, *pp)
        dx, dy, dg, dlg, dlb, dscn, dshn = vjp((dxn, dun))
        return [dx, dy], [dg, dlg, dlb, dscn, dshn]

    def post_bwd(s, x, y, dxn, dun):
        (dx, dy), (dgt[s], dlng[s], dlnb[s], dsc[s + 1], dsh[s + 1]) = _rows(
            f"post_ln_bwd_{s}", post_bwd_body, [x, y, dxn, dun],
            [gt[s], lng[s], lnb[s], sc[s + 1], sh[s + 1]], [(D, F32), (D, BF16)], [(1, D)] * 5)
        return dx, dy

    du3, gw1_1, gw2_1 = mlp_bwd(1, u3T, h3, a3T, dy3)
    dxp, dy2 = post_bwd(2, x2, y2, dxp, du3)

    G["dil_w_out"] = _mm("dil_out_dw", om.T, dy2)[None]
    do = _mm("dil_out_dx", dy2, W["dil_w_out"], tb=True)

    def merge_bwd_body(r, pp):
        _, vjp = jax.vjp(_merge_core, *r[:6])
        return list(vjp(r[6])), []

    mb, _ = _rows("attn_merge_bwd", merge_bwd_body, list(og + lse) + [do], [], [(RW, F32)] * 6)
    back = [_attn_bwd(pq, bias[g], mb[g], og[g], lse[g], mb[3 + g], g) for g in range(3)]
    dpq = jnp.concatenate([t for dq, dk, dv, _ in back for t in (dq, dk, dv)], axis=1).astype(BF16)
    rb = _relbias_grad(jnp.stack([b[3] for b in back]).reshape(3, HEADS, NPAIR), onehotT)
    G["rel_bias"] = rb.transpose(2, 0, 1).reshape(NBUCKET, 3 * HEADS)
    G["dil_w_qkv"] = _mm("qkv_dw", u2T, dpq)[None]
    du2 = _mm("qkv_dx", dpq, W["dil_w_qkv"], tb=True)
    dxp, dy1 = post_bwd(1, x1, y1, dxp, du2)

    du1, gw1_0, gw2_0 = mlp_bwd(0, u1T, h1, a1T, dy1)
    G["mlp_w1"] = jnp.stack([gw1_0, gw1_1])
    G["mlp_w2"] = jnp.stack([gw2_0, gw2_1])
    dxp, dy0 = post_bwd(0, x0, y0, dxp, du1)

    G["ab_w_out"] = _mm("ab_out_dw", cat.T, dy0)[None]
    dcat = _mm("ab_out_dx", dy0, W["ab_w_out"], tb=True)
    post_params = [post_params[0] + early_grads(G)] + post_params[1:]

    def postmix_bwd_body(r, pp):
        _, vjp = jax.vjp(functools.partial(_post_core, pp[0]), *r[:5], *pp[1:])
        d = vjp(r[5])
        return list(d[:5]), list(d[5:])

    (dy_cols, dr1, dkh1, dv1, dgate), (G["rw_lnx_g"], G["rw_lnx_b"], drk) = _rows(
        "rwkv_post_bwd", postmix_bwd_body, [ysc, r_, kh_, v_, gate_, (dcat, 512, 0)], [E] + post_params,
        [(RW, BF16, "cols3")] + [(RW, F32)] * 4, [(1, RW)] * 3, tm=256)
    G["rw_r_k"] = drk.reshape(1, HEADS, HD)
    dr2, dw2, dk2, da2, db2, dv2 = _scan_bwd(*scan_in, dy_cols, *saved)

    def pre_bwd_body(r, pp):
        prim, ct = pre_args(r[:len(pre_rows)]), r[len(pre_rows):]
        _, vjp = jax.vjp(functools.partial(_pre_core, pp[0]), *prim, *pp[1:])
        cts = (ct[0] + ct[1], ct[2], ct[3] + ct[4], ct[5] + ct[6], ct[7], ct[8], ct[9], ct[10])
        d = vjp(cts)
        z = jnp.zeros_like(d[12])
        dp = jnp.concatenate([d[0], d[1], d[2], d[3], d[4], d[5], d[12], d[13], d[14]], axis=1)
        dp1 = jnp.concatenate([d[6], d[7], d[8], d[9], d[10], d[11], d[15], z, d[16]], axis=1)
        dp2 = jnp.concatenate([d[17], z, d[18]], axis=1)
        return [dp, dp1, dp2], list(d[NPR:])

    acc_shapes = [a.shape for a in pre_params]
    (dp, dp1, dp2), pacc = _rows(
        "rwkv_pre_bwd", pre_bwd_body,
        pre_rows + [dr1, dr2, dw2, dkh1, dk2, dv1, dv2, da2, db2, dgate, (dcat, 512, 1)],
        [E] + pre_params, [(PAB, F32), (PAB, F32), (PB, F32)], acc_shapes, tm=256)
    G["rw_mu"] = _unpad_pa(jnp.concatenate(pacc[:6], axis=1))
    G["rw_w0"], G["rw_a0"], G["rw_k_k"], G["rw_k_a"] = pacc[6], pacc[8], pacc[11], pacc[12]
    G["rw_w_up"] = pacc[7][None, :64]
    G["rw_a_up"] = pacc[9][None, :64]
    G["rw_g_up"] = pacc[10][None, :160]
    G["sc_conv_w"] = jnp.concatenate(pacc[13:16], axis=0)[None]

    def shift_merge_body(r, pp):
        d0, d1, d1_next, d2, d2_next = r
        d = d0 + _shift_up(d1, d1_next, 1)
        return [jnp.concatenate([d[:, :PA], d[:, PA:] + _shift_up(d2, d2_next, 2)], axis=1)], []

    (dpt,), _ = _rows("shift_merge", shift_merge_body,
                      [dp, dp1, (dp1, PAB, 0, "next"), dp2, (dp2, PB, 0, "next")], [], [(PAB, BF16)], tm=256)
    du0 = _mm("ab_in_dx", dpt, W["ab_w_in"], tb=True)

    def mod_bwd_body(r, pp):
        du, dx, x = r
        return [dx + du * (1.0 + pp[0])], [jnp.sum(du * x, axis=0, keepdims=True), jnp.sum(du, axis=0, keepdims=True)]

    (grad_x,), (dsc[0], dsh[0]) = _rows("modulate_bwd", mod_bwd_body, [du0, dxp, x0], [sc[0]], [(D, F32)],
                                        [(1, D), (1, D)])

    G["ln_g"] = jnp.concatenate(dlng, axis=0).reshape(2, 2, D)
    G["ln_b"] = jnp.concatenate(dlnb, axis=0).reshape(2, 2, D)
    dmod = jnp.concatenate([dsh[0], dsc[0], dgt[0], dsh[1], dsc[1], dgt[1],
                            dsh[2], dsc[2], dgt[2], dsh[3], dsc[3], dgt[3]], axis=1).reshape(2, 6 * D)
    return loss_acc[0, 0], grad_x, dmod, G, lambda: _ab_in_shards(_mm("ab_in_dw", u0T, dpt))[:, None]


def kernel(x, c, ada_w, ada_b, ln_g, ln_b, ab_w_in, rw_mu, rw_w0, rw_w_up, rw_a0, rw_a_up, rw_g_up, rw_k_k, rw_k_a, rw_r_k, rw_lnx_g, rw_lnx_b, sc_conv_w, ab_w_out, dil_w_qkv, dil_w_out, rel_bias, mlp_w1, mlp_w2, loss_target, m_ada_w, m_ada_b, m_ln_g, m_ln_b, m_ab_w_in, m_rw_mu, m_rw_w0, m_rw_w_up, m_rw_a0, m_rw_a_up, m_rw_g_up, m_rw_k_k, m_rw_k_a, m_rw_r_k, m_rw_lnx_g, m_rw_lnx_b, m_sc_conv_w, m_ab_w_out, m_dil_w_qkv, m_dil_w_out, m_rel_bias, m_mlp_w1, m_mlp_w2, v_ada_w, v_ada_b, v_ln_g, v_ln_b, v_ab_w_in, v_rw_mu, v_rw_w0, v_rw_w_up, v_rw_a0, v_rw_a_up, v_rw_g_up, v_rw_k_k, v_rw_k_a, v_rw_r_k, v_rw_lnx_g, v_rw_lnx_b, v_sc_conv_w, v_ab_w_out, v_dil_w_qkv, v_dil_w_out, v_rel_bias, v_mlp_w1, v_mlp_w2):
    w = dict(ada_w=ada_w, ada_b=ada_b, ln_g=ln_g, ln_b=ln_b, ab_w_in=ab_w_in, rw_mu=rw_mu, rw_w0=rw_w0,
             rw_w_up=rw_w_up, rw_a0=rw_a0, rw_a_up=rw_a_up, rw_g_up=rw_g_up, rw_k_k=rw_k_k, rw_k_a=rw_k_a,
             rw_r_k=rw_r_k, rw_lnx_g=rw_lnx_g, rw_lnx_b=rw_lnx_b, sc_conv_w=sc_conv_w, ab_w_out=ab_w_out,
             dil_w_qkv=dil_w_qkv, dil_w_out=dil_w_out, rel_bias=rel_bias, mlp_w1=mlp_w1, mlp_w2=mlp_w2)
    m = dict(ada_w=m_ada_w, ada_b=m_ada_b, ln_g=m_ln_g, ln_b=m_ln_b, ab_w_in=m_ab_w_in, rw_mu=m_rw_mu,
             rw_w0=m_rw_w0, rw_w_up=m_rw_w_up, rw_a0=m_rw_a0, rw_a_up=m_rw_a_up, rw_g_up=m_rw_g_up,
             rw_k_k=m_rw_k_k, rw_k_a=m_rw_k_a, rw_r_k=m_rw_r_k, rw_lnx_g=m_rw_lnx_g, rw_lnx_b=m_rw_lnx_b,
             sc_conv_w=m_sc_conv_w, ab_w_out=m_ab_w_out, dil_w_qkv=m_dil_w_qkv, dil_w_out=m_dil_w_out,
             rel_bias=m_rel_bias, mlp_w1=m_mlp_w1, mlp_w2=m_mlp_w2)
    v = dict(ada_w=v_ada_w, ada_b=v_ada_b, ln_g=v_ln_g, ln_b=v_ln_b, ab_w_in=v_ab_w_in, rw_mu=v_rw_mu,
             rw_w0=v_rw_w0, rw_w_up=v_rw_w_up, rw_a0=v_rw_a0, rw_a_up=v_rw_a_up, rw_g_up=v_rw_g_up,
             rw_k_k=v_rw_k_k, rw_k_a=v_rw_k_a, rw_r_k=v_rw_r_k, rw_lnx_g=v_rw_lnx_g, rw_lnx_b=v_rw_lnx_b,
             sc_conv_w=v_sc_conv_w, ab_w_out=v_ab_w_out, dil_w_qkv=v_dil_w_qkv, dil_w_out=v_dil_w_out,
             rel_bias=v_rel_bias, mlp_w1=v_mlp_w1, mlp_w2=v_mlp_w2)
    kinds = dict(SHARDED)
    me = 4 * lax.axis_index("x") + 2 * lax.axis_index("y") + lax.axis_index("c")
    ncol = ada_w.shape[2]

    small = _all_gather(_pack([c] + [w[n] for n in GATHER_F32], F32, 8), "gather_small")
    parts = _unpack(small, [c.shape] + [w[n].shape for n in GATHER_F32], (NDEV,))
    c_all = parts[0].reshape(NDEV, D)
    P = {n: _from_chunks(t, kinds[n]) for n, t in zip(GATHER_F32, parts[1:])}
    P = {n: (t if n in ("ln_g", "ln_b") else t[0]) for n, t in P.items()}
    for n in REPLICATED[1:]:
        P[n] = w[n]
    def full(n, t):
        t = _from_chunks(t, kinds[n])
        return t if n in ("mlp_w1", "mlp_w2") else t[0]

    (first,) = _all_gather_many([ab_w_in.astype(BF16)], "gather_first_weight")
    W = {"ab_w_in": _ab_in_padded(first[:, 0])}

    ada_b_loc = lax.dynamic_slice(ada_b, (0, ncol * me), (2, ncol))
    mod_part = _ada_mod(c_all, ada_w, ada_b_loc)
    mod_all = _all_gather(mod_part.reshape(-1, 128), "gather_mod").reshape(NDEV, 2, NDEV, ncol)
    mod = lax.dynamic_index_in_dim(mod_all, me, axis=2, keepdims=False)
    mod = mod.transpose(1, 0, 2).reshape(2, 6 * D)

    behind = (mod[0, 0] * 0.0).astype(BF16)
    later = _exchange_start([w[n].astype(BF16) + (behind if n == LATER[0] else 0) for n in LATER], True,
                            "gather_later_weights_start")
    mod = mod + later[-1][0, 0]

    def later_weights(after):
        lands = _exchange_wait(later, True, after, "gather_later_weights_wait")
        return {n: full(n, t) for n, t in zip(LATER, lands)}

    sent = []

    def early_grads(G):
        sent.append(_exchange_start([_to_chunks(G[n], kinds[n]).astype(BF16) for n in LATER], False,
                                    "exchange_later_grads_start"))
        return sent[0][-1][0, 0]

    loss_part, grad_x, dmod, G, in_grad = _local_step(x[0], loss_target[0], mod, W, P, later_weights, early_grads)
    G["ada_b"] = dmod
    big_out = {}

    def update(n, contributions):
        cols = w[n].shape[-1]
        flat = lambda t: t.reshape(-1, cols)
        rows = flat(w[n]).shape[0]
        outs = _sum_adamw(contributions.reshape(-1, rows, cols), flat(w[n]), flat(m[n]), flat(v[n]),
                          f"sum_adamw_{n}", min(rows, 256))
        big_out[n] = [o.reshape(w[n].shape) for o in outs]

    rep_shapes = [w[n].shape for n in REPLICATED] + [(1,)]
    rep_all = _all_gather(_pack([G[n] for n in REPLICATED] + [loss_part], F32, 8), "gather_replicated_grads")
    names = [n for n, _ in SHARDED if n not in GATHER_BF16]
    shard_shapes = [w[n].shape for n in names]
    recv = _all_to_all(_pack8([_to_chunks(G[n], kinds[n]) for n in names], F32, 8), "exchange_small_grads")

    behind = (recv[0, 0, 0] * 0.0 + rep_all[0, 0, 0] * 0.0).astype(BF16)
    last = _exchange_start([in_grad().astype(BF16) + behind], False, "exchange_last_grad_start")

    zero = last[-1][0:1, 0]
    pk = lambda d: _pack([d[n] for n in REPLICATED] + [zero], F32, 8)
    rep_out = _sum_adamw(rep_all, pk(w), pk(m), pk(v), "sum_adamw_replicated", rep_all.shape[1])
    loss = _unpack(rep_out[0], rep_shapes)[-1][0]
    rep_out = [dict(zip(REPLICATED, _unpack(o, rep_shapes))) for o in rep_out]
    dmod_all = _unpack(rep_all, [(2, 6 * D)], (NDEV,))[0]
    dmod_loc = lax.dynamic_slice(dmod_all, (0, 0, ncol * me), (NDEV, 2, ncol)).transpose(1, 0, 2)
    ada_out = _ada_grad_adamw(c_all.T + zero, dmod_loc, ada_w, m_ada_w, v_ada_w)
    pk = lambda d: _pack([d[n] for n in names], F32, 8)
    sh_out = _sum_adamw(recv, pk(w), pk(m), pk(v), "sum_adamw_small", recv.shape[1])
    sh_out = [dict(zip(names, _unpack(o, shard_shapes))) for o in sh_out]
    for n, r in zip(LATER, _exchange_wait(sent[0], False, last[-1], "exchange_later_grads_wait")):
        update(n, r)
    (landed,) = _exchange_wait(last, False, big_out[LATER[-1]][0], "exchange_last_grad_wait")
    update("ab_w_in", landed)
    sh_out = [{**d, **{n: big_out[n][i] for n in GATHER_BF16}} for i, d in enumerate(sh_out)]

    def pick(i, n):
        if n == "ada_w":
            return ada_out[i]
        return rep_out[i][n] if n in REPLICATED else sh_out[i][n]

    outs = [loss, grad_x[None]]
    for i in range(4):
        outs += [pick(i, n) for n in WEIGHTS]
    return tuple(outs)
```

```python
import functools
import math

import jax
import jax.numpy as jnp
from jax import lax
from jax.experimental import pallas as pl
from jax.experimental.pallas import tpu as pltpu

F32 = jnp.float32
BF16 = jnp.bfloat16
HI = lax.Precision.HIGHEST

NDEV = 8
T = 2048
D = 1024
DFF = 4096
HEADS = 8
HD = 64
RW = 512
PA = 2048
PB = 1536
PAB = PA + PB
QKV = 4608
DILS = (1, 4, 16)
BLK = 128
ALPHA = 4.0 ** 0.25
LN_EPS = 1e-5
GN_EPS = 64e-5
ADAM_LR, ADAM_B1, ADAM_B2, ADAM_EPS, ADAM_WD, ADAM_STEP = 0.001, 0.9, 0.999, 1e-8, 0.01, 10
VMEM_LIMIT = 56 * 1024 * 1024


def _cp(sem):
    return pltpu.CompilerParams(dimension_semantics=sem, vmem_limit_bytes=VMEM_LIMIT)


def _slot(px, py, pc):
    return 4 * px + 2 * py + pc


def _all_gather(x, name):
    R, C = x.shape

    def body(x_ref, out_ref, send_sems, recv_sems, local_sem):
        xi, yi, ci = lax.axis_index("x"), lax.axis_index("y"), lax.axis_index("c")
        me, sibling = (xi, yi, ci), (xi, yi, 1 - ci)
        chips = [(1 - xi, yi), (xi, 1 - yi), (1 - xi, 1 - yi)]

        def rows(px, py, pc):
            return out_ref.at[_slot(px, py, pc)]

        def copy(k, block, to, src=None):
            return pltpu.make_async_remote_copy(
                src_ref=rows(*block) if src is None else src, dst_ref=rows(*block),
                send_sem=send_sems.at[k], recv_sem=recv_sems.at[k],
                device_id=to, device_id_type=pl.DeviceIdType.MESH)

        mine = pltpu.make_async_copy(x_ref, rows(*me), local_sem)
        mine.start()
        first = [copy(0, me, sibling, src=x_ref)]
        first += [copy(1 + j, me, (*chip, ci), src=x_ref) for j, chip in enumerate(chips)]
        for cp in first:
            cp.start()
        passed = [copy(4 + j, (*chip, ci), sibling) for j, chip in enumerate(chips)]
        for j, chip in enumerate(chips):
            copy(1 + j, (*chip, ci), me).wait_recv()
            passed[j].start()
        copy(0, sibling, me).wait_recv()
        for j, chip in enumerate(chips):
            copy(4 + j, (*chip, 1 - ci), me).wait_recv()
        for cp in first + passed:
            cp.wait_send()
        mine.wait()

    return pl.pallas_call(
        body, name=name,
        out_shape=jax.ShapeDtypeStruct((NDEV, R, C), x.dtype),
        in_specs=[pl.BlockSpec(memory_space=pl.ANY)],
        out_specs=pl.BlockSpec(memory_space=pl.ANY),
        scratch_shapes=[pltpu.SemaphoreType.DMA((7,)), pltpu.SemaphoreType.DMA((7,)),
                        pltpu.SemaphoreType.DMA(())],
    )(x)


def _all_to_all(g, name):
    _, R, C = g.shape

    def body(g_ref, out_ref, send_sems, recv_sems, local_sem):
        xi, yi, ci = lax.axis_index("x"), lax.axis_index("y"), lax.axis_index("c")
        my_slot = _slot(xi, yi, ci)
        mine = pltpu.make_async_copy(g_ref.at[my_slot], out_ref.at[my_slot], local_sem)
        mine.start()
        copies = []
        for k in range(1, 8):
            px = 1 - xi if k & 4 else xi
            py = 1 - yi if k & 2 else yi
            pc = 1 - ci if k & 1 else ci
            peer_slot = _slot(px, py, pc)
            copies.append((
                pltpu.make_async_remote_copy(
                    src_ref=g_ref.at[peer_slot], dst_ref=out_ref.at[my_slot],
                    send_sem=send_sems.at[k - 1], recv_sem=recv_sems.at[k - 1],
                    device_id=(px, py, pc), device_id_type=pl.DeviceIdType.MESH),
                pltpu.make_async_remote_copy(
                    src_ref=g_ref.at[peer_slot], dst_ref=out_ref.at[peer_slot],
                    send_sem=send_sems.at[k - 1], recv_sem=recv_sems.at[k - 1],
                    device_id=(px, py, pc), device_id_type=pl.DeviceIdType.MESH)))
        for send, _ in copies:
            send.start()
        for _, recv in copies:
            recv.wait_recv()
        for send, _ in copies:
            send.wait_send()
        mine.wait()

    return pl.pallas_call(
        body, name=name,
        out_shape=jax.ShapeDtypeStruct((NDEV, R, C), g.dtype),
        in_specs=[pl.BlockSpec(memory_space=pl.ANY)],
        out_specs=pl.BlockSpec(memory_space=pl.ANY),
        scratch_shapes=[pltpu.SemaphoreType.DMA((7,)), pltpu.SemaphoreType.DMA((7,)),
                        pltpu.SemaphoreType.DMA(())],
    )(g)


def _my_slot():
    return _slot(lax.axis_index("x"), lax.axis_index("y"), lax.axis_index("c"))


def _put_own(buf, own, slot):
    return lax.dynamic_update_index_in_dim(buf, own, slot, 0)


def _hbm_call(body, name, ins, out_shapes, n_sems):
    anyspec = pl.BlockSpec(memory_space=pl.ANY)
    return pl.pallas_call(
        body, name=name, out_shape=out_shapes,
        in_specs=[anyspec] * len(ins), out_specs=[anyspec] * len(out_shapes),
        scratch_shapes=[pltpu.SemaphoreType.DMA(s) for s in n_sems],
    )(*ins)


def _all_gather_many(xs, name):
    n = len(xs)

    def body(*refs):
        x_refs, o_refs = refs[:n], refs[n:2 * n]
        send_sems, recv_sems = refs[2 * n:]
        xi, yi, ci = lax.axis_index("x"), lax.axis_index("y"), lax.axis_index("c")
        me, sibling = (xi, yi, ci), (xi, yi, 1 - ci)
        chips = [(1 - xi, yi), (xi, 1 - yi), (1 - xi, 1 - yi)]

        def copy(i, k, block, to, src=None):
            dst = o_refs[i].at[_slot(*block)]
            return pltpu.make_async_remote_copy(
                src_ref=dst if src is None else src, dst_ref=dst,
                send_sem=send_sems.at[i, k], recv_sem=recv_sems.at[i, k],
                device_id=to, device_id_type=pl.DeviceIdType.MESH)

        sends = []
        for i in range(n):
            sends += [copy(i, 1 + j, me, (*chip, ci), src=x_refs[i]) for j, chip in enumerate(chips)]
            sends.append(copy(i, 0, me, sibling, src=x_refs[i]))
        for cp in sends:
            cp.start()
        for j, chip in enumerate(chips):
            for i in range(n):
                copy(i, 1 + j, (*chip, ci), me).wait_recv()
                passed = copy(i, 4 + j, (*chip, ci), sibling)
                passed.start()
                sends.append(passed)
        for i in range(n):
            copy(i, 0, sibling, me).wait_recv()
            for j, chip in enumerate(chips):
                copy(i, 4 + j, (*chip, 1 - ci), me).wait_recv()
        for cp in sends:
            cp.wait_send()

    outs = _hbm_call(body, name, xs, [jax.ShapeDtypeStruct((NDEV,) + x.shape, x.dtype) for x in xs],
                     [(n, 7), (n, 7)])
    return [_put_own(o, x[None], _my_slot()) for o, x in zip(outs, xs)]


def _peers(xi, yi, ci):
    return [(1 - xi if k & 4 else xi, 1 - yi if k & 2 else yi, 1 - ci if k & 1 else ci) for k in range(1, 8)]


def _direct_copy(src_refs, land_refs, send_sems, recv_sems, i, k, peer, my_slot, gather):
    src = src_refs[i] if gather else src_refs[i].at[_slot(*peer)]
    return pltpu.make_async_remote_copy(
        src_ref=src, dst_ref=land_refs[i].at[my_slot], send_sem=send_sems.at[7 * i + k], recv_sem=recv_sems.at[7 * i + k],
        device_id=peer, device_id_type=pl.DeviceIdType.MESH)


def _exchange_start(srcs, gather, name):
    n = len(srcs)
    lands = [lax.empty(((NDEV,) + s.shape) if gather else s.shape, s.dtype) for s in srcs]

    def body(*refs):
        s_refs, l_refs = refs[:n], refs[n:2 * n]
        send_sems, recv_sems = refs[2 * n], refs[2 * n + 1]
        token = refs[2 * n + 2 + 2 * n]
        xi, yi, ci = lax.axis_index("x"), lax.axis_index("y"), lax.axis_index("c")
        my_slot = _slot(xi, yi, ci)
        for k, peer in enumerate(_peers(xi, yi, ci)):
            for i in range(n):
                _direct_copy(s_refs, l_refs, send_sems, recv_sems, i, k, peer, my_slot, gather).start()
        token[...] = jnp.zeros_like(token)

    hbm = pl.BlockSpec(memory_space=pltpu.HBM)
    sem = pl.BlockSpec(memory_space=pltpu.SEMAPHORE)
    both = list(srcs) + lands
    return pl.pallas_call(
        body, name=name,
        out_shape=(pltpu.SemaphoreType.DMA((7 * n,)), pltpu.SemaphoreType.DMA((7 * n,)),
                   *[pltpu.HBM(t.shape, t.dtype) for t in both], jax.ShapeDtypeStruct((8, 128), F32)),
        in_specs=[hbm] * (2 * n),
        out_specs=(sem, sem, *[hbm] * (2 * n), pl.BlockSpec(memory_space=pltpu.VMEM)),
        input_output_aliases={i: 2 + i for i in range(2 * n)},
        compiler_params=pltpu.CompilerParams(has_side_effects=pltpu.SideEffectType.DATAFLOW_SIDE_EFFECTING),
    )(*[pltpu.with_memory_space_constraint(t, pltpu.HBM) for t in both])


def _exchange_wait(started, gather, after, name):
    send_sems, recv_sems, *thru, _ = started
    n = len(thru) // 2

    def body(*refs):
        s_refs, l_refs = refs[:n], refs[n:2 * n]
        send_sems, recv_sems = refs[2 * n], refs[2 * n + 1]
        xi, yi, ci = lax.axis_index("x"), lax.axis_index("y"), lax.axis_index("c")
        my_slot = _slot(xi, yi, ci)
        for k, peer in enumerate(_peers(xi, yi, ci)):
            for i in range(n):
                _direct_copy(s_refs, l_refs, send_sems, recv_sems, i, k, peer, my_slot, gather).wait_send()
                _direct_copy(s_refs, l_refs, send_sems, recv_sems, i, k, peer, _slot(*peer), gather).wait_recv()

    hbm = pl.BlockSpec(memory_space=pltpu.HBM)
    sem = pl.BlockSpec(memory_space=pltpu.SEMAPHORE)
    outs = pl.pallas_call(
        body, name=name,
        out_shape=tuple(pltpu.HBM(t.shape, t.dtype) for t in thru),
        in_specs=[hbm] * (2 * n) + [sem, sem, pl.BlockSpec(memory_space=pl.ANY)],
        out_specs=tuple([hbm] * (2 * n)),
        input_output_aliases={i: i for i in range(2 * n)},
        compiler_params=pltpu.CompilerParams(has_side_effects=pltpu.SideEffectType.DATAFLOW_SIDE_EFFECTING),
    )(*thru, send_sems, recv_sems, after)
    slot = _my_slot()
    own = [s[None] if gather else lax.dynamic_index_in_dim(s, slot, 0, keepdims=True) for s in outs[:n]]
    return [_put_own(land, o, slot) for land, o in zip(outs[n:], own)]


def _mm(name, a, b, tb=False, out=(F32,), epi=None, extras=(), tm=2048, tn=512, tk_cap=2048):
    M, K = a.shape
    N = b.shape[0] if tb else b.shape[1]
    tm, tn = min(tm, M), min(tn, N)
    tk = max(t for t in range(128, min(K, tk_cap) + 1, 128) if K % t == 0)
    assert M % tm == 0 and N % tn == 0 and K % tk == 0, (name, M, N, K)
    nk = K // tk
    ne, no = len(extras), len(out)
    dims = (((1,), (1 if tb else 0,)), ((), ()))
    flipped = [isinstance(o, tuple) for o in out]

    def kern(*refs):
        a_ref, b_ref = refs[:2]
        e_refs = refs[2:2 + ne]
        o_refs = refs[2 + ne:2 + ne + no]

        def finish(acc):
            outs = epi(acc, *[e[...] for e in e_refs]) if epi is not None else (acc,)
            for o_ref, o, flip in zip(o_refs, outs, flipped):
                o_ref[...] = (o.T if flip else o).astype(o_ref.dtype)

        part = lax.dot_general(a_ref[...], b_ref[...], dims, preferred_element_type=F32)
        if nk == 1:
            finish(part)
            return
        acc_ref = refs[-1]
        k = pl.program_id(2)

        @pl.when(k == 0)
        def _():
            acc_ref[...] = part

        @pl.when(k > 0)
        def _():
            acc_ref[...] += part

        @pl.when(k == nk - 1)
        def _():
            finish(acc_ref[...])

    b_spec = (pl.BlockSpec((tn, tk), lambda i, j, k: (j, k)) if tb
              else pl.BlockSpec((tk, tn), lambda i, j, k: (k, j)))
    tile = pl.BlockSpec((tm, tn), lambda i, j, k: (i, j))
    tile_t = pl.BlockSpec((tn, tm), lambda i, j, k: (j, i))
    res = pl.pallas_call(
        kern, name=name, grid=(M // tm, N // tn, nk),
        in_specs=[pl.BlockSpec((tm, tk), lambda i, j, k: (i, k)), b_spec] + [tile] * ne,
        out_specs=[tile_t if flip else tile for flip in flipped],
        out_shape=[jax.ShapeDtypeStruct((N, M), o[0]) if flip else jax.ShapeDtypeStruct((M, N), o)
                   for o, flip in zip(out, flipped)],
        scratch_shapes=[pltpu.VMEM((tm, tn), F32)] if nk > 1 else [],
        compiler_params=_cp(("parallel", "parallel", "arbitrary")),
    )(a, b, *extras)
    return res[0] if no == 1 else res


HALO = 8


def _rows(name, body, rows, params, out_rows, out_accs=(), tm=512):
    views = [r if isinstance(r, tuple) else (r, r.shape[1], 0) for r in rows]
    n = views[0][0].shape[0]
    assert n % tm == 0 and tm % HALO == 0
    nr, npar, nor, noa = len(views), len(params), len(out_rows), len(out_accs)
    kinds = [o[2] if len(o) == 3 else None for o in out_rows]
    store = {None: lambda o: o, "T": lambda o: o.T, "cols3": _cols3_tile}
    spec = {None: lambda w: pl.BlockSpec((tm, w), lambda i: (i, 0)),
            "T": lambda w: pl.BlockSpec((w, tm), lambda i: (0, i)),
            "cols3": lambda w: pl.BlockSpec((HP, HD, 6 * tm), lambda i: (0, 0, i))}
    shape = {None: lambda w: (n, w), "T": lambda w: (w, n), "cols3": lambda w: (HP, HD, 6 * n)}

    def row_spec(width, cb, halo=None):
        per, last = tm // HALO, n // HALO - 1
        if halo == "prev":
            return pl.BlockSpec((HALO, width), lambda i: (jnp.maximum(i * per - 1, 0), cb))
        if halo == "next":
            return pl.BlockSpec((HALO, width), lambda i: (jnp.minimum((i + 1) * per, last), cb))
        return pl.BlockSpec((tm, width), lambda i: (i, cb))

    def kern(*refs):
        r_refs = refs[:nr]
        p_refs = refs[nr:nr + npar]
        o_refs = refs[nr + npar:nr + npar + nor]
        a_refs = refs[nr + npar + nor:]
        outs, accs = body([r[...] for r in r_refs], [p[...] for p in p_refs])
        assert len(outs) == nor and len(accs) == noa, (name, len(outs), len(accs))
        for o_ref, o, kind in zip(o_refs, outs, kinds):
            o_ref[...] = store[kind](o).astype(o_ref.dtype)
        if noa:
            @pl.when(pl.program_id(0) == 0)
            def _():
                for a_ref in a_refs:
                    a_ref[...] = jnp.zeros_like(a_ref)

            for a_ref, a in zip(a_refs, accs):
                a_ref[...] += a.astype(F32)

    def whole(shape):
        nd = len(shape)
        return pl.BlockSpec(tuple(shape), lambda i, nd=nd: (0,) * nd)

    in_specs = [row_spec(*v[1:]) for v in views]
    in_specs += [whole(p.shape) for p in params]
    out_specs = [spec[kind](o[0]) for o, kind in zip(out_rows, kinds)]
    out_specs += [whole(s) for s in out_accs]
    out_shape = [jax.ShapeDtypeStruct(shape[kind](o[0]), o[1]) for o, kind in zip(out_rows, kinds)]
    out_shape += [jax.ShapeDtypeStruct(tuple(s), F32) for s in out_accs]
    res = pl.pallas_call(
        kern, name=name, grid=(n // tm,), in_specs=in_specs, out_specs=out_specs,
        out_shape=out_shape, compiler_params=_cp(("arbitrary",)),
    )(*[v[0] for v in views], *params)
    return res[:nor], res[nor:]


def _shift_down(x, prev, k):
    head = jnp.where(pl.program_id(0) == 0, 0.0, pltpu.roll(prev, k, axis=0))
    row = lax.broadcasted_iota(jnp.int32, x.shape, 0)
    return jnp.where(row < k, jnp.tile(head, (x.shape[0] // HALO, 1)), pltpu.roll(x, k, axis=0))


def _shift_up(x, nxt, k):
    n = x.shape[0]
    tail = jnp.where(pl.program_id(0) == pl.num_programs(0) - 1, 0.0, pltpu.roll(nxt, HALO - k, axis=0))
    row = lax.broadcasted_iota(jnp.int32, x.shape, 0)
    return jnp.where(row >= n - k, jnp.tile(tail, (n // HALO, 1)), pltpu.roll(x, n - k, axis=0))


@jax.custom_vjp
def _headsum(x, e):
    return sum(jnp.dot(p, e, preferred_element_type=F32) for p in _split3(x))


_headsum.defvjp(lambda x, e: (_headsum(x, e), e), lambda e, ct: (_headsum(ct, e), None))


def _softplus(z):
    return jnp.maximum(z, 0.0) + jnp.log(1.0 + jnp.exp(jnp.minimum(z, -z)))


def _post_ln(x, y, g, lng, lnb):
    z = ALPHA * x + (1.0 + g) * y
    mu = jnp.mean(z, axis=-1, keepdims=True)
    zc = z - mu
    var = jnp.mean(zc * zc, axis=-1, keepdims=True)
    return zc * lax.rsqrt(var + LN_EPS) * lng + lnb


def _post_ln_mod(x, y, g, lng, lnb, scn, shn):
    xn = _post_ln(x, y, g, lng, lnb)
    return xn, xn * (1.0 + scn) + shn


def _pre_core(E, r_, k_, v_, wd_, ad_, gd_, r1, k1, v1, wd1, ad1, gd1, h, bg, cg, h1, cg1, h2, cg2,
              mu_r, mu_k, mu_v, mu_wd, mu_ad, mu_gd, w0, w_up, a0, a_up, g_up, k_k, k_a,
              cw0, cw1, cw2):
    def mix(x, x1, mu):
        return x + mu * (x1 - x)

    r, k, v = mix(r_, r1, mu_r), mix(k_, k1, mu_k), mix(v_, v1, mu_v)
    wd, ad, gd = mix(wd_, wd1, mu_wd), mix(ad_, ad1, mu_ad), mix(gd_, gd1, mu_gd)
    logw = -_softplus(-(w0 + jnp.dot(jnp.tanh(wd), w_up, preferred_element_type=F32))) - 0.5
    decay = jnp.exp(-jnp.exp(logw))
    iclr = jax.nn.sigmoid(a0 + jnp.dot(ad, a_up, preferred_element_type=F32))
    gate = jnp.dot(jax.nn.sigmoid(gd), g_up, preferred_element_type=F32)
    kk0 = k * k_k
    nrm = jnp.sqrt(_headsum(kk0 * kk0, E))
    kk = kk0 / jnp.maximum(nrm, 1e-12)
    kh = k * (1.0 + (iclr - 1.0) * k_a)
    yb = bg * (cw2 * (cg * h) + cw1 * (cg1 * h1) + cw0 * (cg2 * h2))
    return r, decay, kh, v, -kk, kk * iclr, gate, yb


def _post_core(E, y, r, kh, v, gate, lnx_g, lnx_b, rk):
    def seg(t):
        return _headsum(t, E)

    mean = seg(y) * (1.0 / HD)
    yc = y - mean
    var = seg(yc * yc) * (1.0 / HD)
    gn = yc * lax.rsqrt(var + GN_EPS) * lnx_g + lnx_b
    bonus = seg(r * kh * rk) * v
    return (gn + bonus) * gate


def _merge_core(o0, o1, o2, l0, l1, l2):
    m = jnp.maximum(jnp.maximum(l0, l1), l2)
    e0, e1, e2 = jnp.exp(l0 - m), jnp.exp(l1 - m), jnp.exp(l2 - m)
    den = e0 + e1 + e2
    return (e0 * o0 + e1 * o1 + e2 * o2) / den


CHUNK = 128
HALF = 64
HP = HEADS // 2
LW = 2 * HD
NCHUNK = T // CHUNK


def _split3(x):
    hi = x.astype(BF16)
    r1 = x - hi.astype(F32)
    mid = r1.astype(BF16)
    return hi, mid, (r1 - mid.astype(F32)).astype(BF16)


def _cols3_tile(x):
    left = lax.broadcasted_iota(jnp.int32, (HD, CHUNK), 1) < HALF
    xts = [x[c * CHUNK:(c + 1) * CHUNK].T for c in range(x.shape[0] // CHUNK)]
    pairs = []
    for p in range(HP):
        groups = []
        for xt in xts:
            a, b = xt[p * LW:p * LW + HD], xt[p * LW + HD:(p + 1) * LW]
            for half in (jnp.where(left, a, pltpu.roll(b, HALF, axis=1)), jnp.where(left, pltpu.roll(a, HALF, axis=1), b)):
                groups += list(_split3(half))
        pairs.append(jnp.concatenate(groups, axis=1))
    return jnp.stack(pairs)


def _pick_codes():
    row = lax.broadcasted_iota(jnp.int32, (6 * HALF, LW), 0)
    col = lax.broadcasted_iota(jnp.int32, (6 * HALF, LW), 1)
    same = ((row & (LW - 1)) >= HALF) == (col >= HD)
    return jnp.where(same, row & (HALF - 1), -1).astype(BF16)


def _column(block_ref, codes, half, i):
    pick = jnp.where(codes == i.astype(BF16), jnp.ones((), BF16), jnp.zeros((), BF16))
    block = block_ref[:, :, half * 6 * HALF:(half + 1) * 6 * HALF].reshape(HP * HD, 6 * HALF)
    return jnp.dot(block, pick, preferred_element_type=F32)


def _halfsums(x, row, left1):
    row_l = jnp.where(left1, row, 0.0)
    return (jnp.sum(x * row_l, axis=1, keepdims=True), jnp.sum(x * (row - row_l), axis=1, keepdims=True))


def _pair_rows(row):
    return [row[:, p * LW:(p + 1) * LW] for p in range(HP)]


def _store_columns(ref, p, t_mask, cols):
    for j, col in enumerate(cols):
        pltpu.store(ref.at[pl.ds(2 * p + j, 1)], jnp.broadcast_to(col[None], (1, HD, CHUNK)), mask=t_mask[None])


def _columns_to_rows(cols_ref, rows_ref):
    for p in range(HP):
        rows_ref[:, p * LW:(p + 1) * LW] = cols_ref[2 * p:2 * p + 2].reshape(LW, CHUNK).T


NHALF = T // HALF
HALVES = CHUNK // HALF


def _scan_fwd(r, w, k, a, b, v3):
    def kern(r_ref, w_ref, k_ref, a_ref, b_ref, v_ref, y_ref, ck_ref, st_hbm, sa_hbm,
             s_ref, vb_ref, yc_ref, st_ref, sa_ref, sems):
        c = pl.program_id(0)

        @pl.when(c == 0)
        def _():
            s_ref[...] = jnp.zeros_like(s_ref)

        lane = lax.broadcasted_iota(jnp.int32, (HD, CHUNK), 1)
        left = lane < HD
        left1 = lax.broadcasted_iota(jnp.int32, (1, LW), 1) < HD
        codes = _pick_codes()

        def flush(slot, half_index):
            return [pltpu.make_async_copy(src.at[slot], dst.at[half_index], sems.at[j, slot])
                    for j, (src, dst) in enumerate(((st_ref, st_hbm), (sa_ref, sa_hbm)))]

        for half in range(HALVES):
            ck_ref[half] = s_ref[...]
            vb_ref[...] = _column(v_ref, codes, half, jnp.int32(0))

            @pl.when(c > 0)
            def _():
                for cp in flush(half, (c - 1) * HALVES + half):
                    cp.wait()

            def step(i, carry):
                t = half * HALF + i
                row = lambda ref: _pair_rows(ref[pl.ds(t, 1), :])
                S = [s_ref[p] for p in range(HP)]
                sa = [jnp.where(left, *_halfsums(s, a, left1)) for s, a in zip(S, row(a_ref))]
                S = [s * w + c_ * b + vb_ref[pl.ds(p * HD, HD), :] * k
                     for p, (s, w, c_, b, k) in enumerate(zip(S, row(w_ref), sa, row(b_ref), row(k_ref)))]
                for p, (s, c_) in enumerate(zip(S, sa)):
                    s_ref[p] = s
                    st_ref[half, i, p] = s
                    sa_ref[half, i, p] = c_
                for p, (s, r) in enumerate(zip(S, row(r_ref))):
                    _store_columns(yc_ref, p, lane == t, _halfsums(s, r, left1))
                vb_ref[...] = _column(v_ref, codes, half, i + 1)
                return carry

            lax.fori_loop(0, HALF, step, 0, unroll=16)
            for cp in flush(half, c * HALVES + half):
                cp.start()
        _columns_to_rows(yc_ref, y_ref)

        @pl.when(c == NCHUNK - 1)
        def _():
            for half in range(HALVES):
                for cp in flush(half, c * HALVES + half):
                    cp.wait()

    rowblk = pl.BlockSpec((CHUNK, RW), lambda c: (c, 0))
    saved = jax.ShapeDtypeStruct((NHALF, HALF, HP, HD, LW), F32)
    stage = pltpu.VMEM((HALVES, HALF, HP, HD, LW), F32)
    return pl.pallas_call(
        kern, name="rwkv_scan_fwd", grid=(NCHUNK,),
        in_specs=[rowblk] * 5 + [pl.BlockSpec((HP, HD, 6 * CHUNK), lambda c: (0, 0, c))],
        out_specs=[rowblk, pl.BlockSpec((HALVES, HP, HD, LW), lambda c: (c, 0, 0, 0)),
                   pl.BlockSpec(memory_space=pl.ANY), pl.BlockSpec(memory_space=pl.ANY)],
        out_shape=[jax.ShapeDtypeStruct((T, RW), F32), jax.ShapeDtypeStruct((NHALF, HP, HD, LW), F32), saved, saved],
        scratch_shapes=[pltpu.VMEM((HP, HD, LW), F32), pltpu.VMEM((HP * HD, LW), F32),
                        pltpu.VMEM((HEADS, HD, CHUNK), F32), stage, stage, pltpu.SemaphoreType.DMA((2, HALVES))],
        compiler_params=_cp(("arbitrary",)),
    )(r, w, k, a, b, v3)


def _scan_bwd(r, w, k, a, b, v3, dy3, ck, st, sa):
    def kern(r_ref, w_ref, k_ref, a_ref, b_ref, v_ref, dy_ref, ck_ref, st_hbm, sa_hbm,
             dr_ref, dw_ref, dk_ref, da_ref, db_ref, dv_ref, ds_ref, sb_ref, sa_ref, pick_ref, dvc_ref, sems):
        c = pl.program_id(0)
        chunk = NCHUNK - 1 - c

        @pl.when(c == 0)
        def _():
            ds_ref[...] = jnp.zeros_like(ds_ref)

        lane = lax.broadcasted_iota(jnp.int32, (HD, CHUNK), 1)
        left = lane < HD
        left1 = lax.broadcasted_iota(jnp.int32, (1, LW), 1) < HD
        codes = _pick_codes()

        def rowsum(x):
            return jnp.sum(x, axis=0, keepdims=True)

        def fetch(slot, half_index):
            return [pltpu.make_async_copy(st_hbm.at[half_index], sb_ref.at[slot, pl.ds(1, HALF)], sems.at[0, slot]),
                    pltpu.make_async_copy(sa_hbm.at[half_index], sa_ref.at[slot], sems.at[1, slot])]

        def picks(half, i):
            pick_ref[pl.ds(0, HP * HD), :] = _column(v_ref, codes, half, i)
            pick_ref[pl.ds(HP * HD, HP * HD), :] = _column(dy_ref, codes, half, i)

        @pl.when(c == 0)
        def _():
            for cp in fetch(HALVES - 1, chunk * HALVES + HALVES - 1):
                cp.start()

        for half in reversed(range(HALVES)):
            base = half * HALF
            for cp in fetch(half, chunk * HALVES + half):
                cp.wait()
            if half:
                for cp in fetch(half - 1, chunk * HALVES + half - 1):
                    cp.start()
            else:
                @pl.when(chunk > 0)
                def _():
                    for cp in fetch(HALVES - 1, chunk * HALVES - 1):
                        cp.start()
            sb_ref[half, 0] = ck_ref[half]
            picks(half, jnp.int32(HALF - 1))

            def back(ii, carry):
                i = HALF - 1 - ii
                t = base + i
                row = lambda ref: _pair_rows(ref[pl.ds(t, 1), :])
                a_r, b_r, k_r, w_r, r_r = row(a_ref), row(b_ref), row(k_ref), row(w_ref), row(r_ref)
                vs = [pick_ref[pl.ds(p * HD, HD), :] for p in range(HP)]
                dys = [pick_ref[pl.ds((HP + p) * HD, HD), :] for p in range(HP)]
                picks(half, jnp.maximum(i - 1, 0))
                dr, dw, db, dk, da = [], [], [], [], []
                for p in range(HP):
                    Sp, dy = sb_ref[half, i, p], dys[p]
                    dS = ds_ref[p] + dy * r_r[p]
                    dr.append(rowsum(sb_ref[half, i + 1, p] * dy))
                    dw.append(rowsum(dS * Sp))
                    db.append(rowsum(dS * sa_ref[half, i, p]))
                    dk.append(rowsum(dS * vs[p]))
                    dsa = jnp.where(left, *_halfsums(dS, b_r[p], left1))
                    _store_columns(dvc_ref, p, lane == t, _halfsums(dS, k_r[p], left1))
                    da.append(rowsum(Sp * dsa))
                    ds_ref[p] = dS * w_r[p] + dsa * a_r[p]
                for ref, pieces in ((dr_ref, dr), (dw_ref, dw), (db_ref, db), (dk_ref, dk), (da_ref, da)):
                    ref[pl.ds(t, 1), :] = jnp.concatenate(pieces, axis=1)
                return carry

            lax.fori_loop(0, HALF, back, 0, unroll=16)
        _columns_to_rows(dvc_ref, dv_ref)

    rowblk = pl.BlockSpec((CHUNK, RW), lambda c: (NCHUNK - 1 - c, 0))
    col3blk = pl.BlockSpec((HP, HD, 6 * CHUNK), lambda c: (0, 0, NCHUNK - 1 - c))
    rowshape = jax.ShapeDtypeStruct((T, RW), F32)
    return pl.pallas_call(
        kern, name="rwkv_scan_bwd", grid=(NCHUNK,),
        in_specs=[rowblk] * 5 + [col3blk, col3blk,
                                 pl.BlockSpec((HALVES, HP, HD, LW), lambda c: (NCHUNK - 1 - c, 0, 0, 0)),
                                 pl.BlockSpec(memory_space=pl.ANY), pl.BlockSpec(memory_space=pl.ANY)],
        out_specs=[rowblk] * 6, out_shape=[rowshape] * 6,
        scratch_shapes=[pltpu.VMEM((HP, HD, LW), F32), pltpu.VMEM((HALVES, HALF + 1, HP, HD, LW), F32),
                        pltpu.VMEM((HALVES, HALF, HP, HD, LW), F32), pltpu.VMEM((2 * HP * HD, LW), F32),
                        pltpu.VMEM((HEADS, HD, CHUNK), F32), pltpu.SemaphoreType.DMA((2, HALVES))],
        compiler_params=_cp(("arbitrary",)),
    )(r, w, k, a, b, v3, dy3, ck, st, sa)


NT = (((1,), (1,)), ((), ()))
TN = (((0,), (0,)), ((), ()))
SCALE = HD ** -0.5
QKV_G = 3 * RW


def _attn_setup(g):
    dil = DILS[g]
    qkv = [pl.BlockSpec((T, LW), lambda hp, c=(g * QKV_G + s * RW) // LW: (0, c + hp)) for s in range(3)]
    tile = pl.BlockSpec((T, LW), lambda hp: (0, hp))
    bias = pl.BlockSpec((2, BLK, 2 * BLK), lambda hp: (hp, 0, 0))

    def blocks():
        for r in range(dil):
            for n in range(T // dil // BLK):
                rows = pl.ds(n * BLK * dil + r, BLK, stride=dil)
                keys = pl.ds((n - 1) * BLK * dil + r, 2 * BLK, stride=dil) if n else rows
                yield n, rows, keys

    return qkv, tile, bias, blocks


def _band(n):
    qi = lax.broadcasted_iota(jnp.int32, (BLK, 2 * BLK), 0)
    ki = lax.broadcasted_iota(jnp.int32, (BLK, 2 * BLK), 1)
    band = (ki >= qi) & (ki <= qi + BLK)
    return band if n else band[:, BLK:]


def _head_masks():
    lane = lax.broadcasted_iota(jnp.int32, (BLK, LW), 1)
    return lane < HD, [(lane < HD).astype(BF16), (lane >= HD).astype(BF16)]


def _attn_fwd(pq, bias, g):
    qkv, tile, bias_spec, blocks = _attn_setup(g)

    def kern(q_ref, k_ref, v_ref, b_ref, o_ref, l_ref):
        left, masks = _head_masks()
        for n, rows, keys in blocks():
            qb, kc, vc = q_ref[rows, :].astype(BF16), k_ref[keys, :].astype(BF16), v_ref[keys, :].astype(BF16)
            valid = _band(n)
            o, lse = [], []
            for j in range(2):
                bias_j = b_ref[j] if n else b_ref[j][:, BLK:]
                s = lax.dot_general(qb * masks[j], kc, NT, preferred_element_type=F32) * SCALE + bias_j
                s = jnp.where(valid, s, -jnp.inf)
                m = jnp.max(s, axis=1, keepdims=True)
                e = jnp.exp(s - m)
                den = jnp.sum(e, axis=1, keepdims=True)
                o.append(jnp.dot((e / den).astype(BF16), vc, preferred_element_type=F32))
                lse.append(m + jnp.log(den))
            o_ref[rows, :] = jnp.where(left, o[0], o[1])
            l_ref[rows, :] = jnp.where(left, lse[0], lse[1])

    shape = jax.ShapeDtypeStruct((T, RW), F32)
    return pl.pallas_call(
        kern, name=f"attn_fwd_{g}", grid=(HP,),
        in_specs=qkv + [bias_spec], out_specs=[tile, tile], out_shape=[shape, shape],
        compiler_params=_cp(("parallel",)),
    )(pq, pq, pq, bias)


def _attn_bwd(pq, bias, do, o, lse, dlse, g):
    qkv, tile, bias_spec, blocks = _attn_setup(g)

    def kern(q_ref, k_ref, v_ref, b_ref, do_ref, o_ref, l_ref, dl_ref, dq_ref, dk_ref, dv_ref, db_ref):
        left, masks = _head_masks()
        lane = lax.broadcasted_iota(jnp.int32, (BLK, LW), 1)
        dk_ref[...] = jnp.zeros_like(dk_ref)
        dv_ref[...] = jnp.zeros_like(dv_ref)
        db_ref[...] = jnp.zeros_like(db_ref)

        def column(tile_, j):
            return jnp.sum(jnp.where(lane == j * HD, tile_, 0.0), axis=1, keepdims=True)

        for n, rows, keys in blocks():
            qb, kc, vc = q_ref[rows, :].astype(BF16), k_ref[keys, :].astype(BF16), v_ref[keys, :].astype(BF16)
            dof, valid = do_ref[rows, :], _band(n)
            dob, terms = dof.astype(BF16), dl_ref[rows, :] - dof * o_ref[rows, :]
            dq = []
            for j in range(2):
                bias_j = b_ref[j] if n else b_ref[j][:, BLK:]
                corr = jnp.sum(terms * masks[j].astype(F32), axis=1, keepdims=True)
                qm, dom = qb * masks[j], dob * masks[j]
                s = lax.dot_general(qm, kc, NT, preferred_element_type=F32) * SCALE + bias_j
                p = jnp.where(valid, jnp.exp(s - column(l_ref[rows, :], j)), 0.0)
                dp = lax.dot_general(dom, vc, NT, preferred_element_type=F32)
                ds = p * (dp + corr)
                if n:
                    db_ref[j] += ds
                else:
                    db_ref[j, :, BLK:] += ds
                dsb = (ds * SCALE).astype(BF16)
                dq.append(jnp.dot(dsb, kc, preferred_element_type=F32))
                dk_ref[keys, :] += lax.dot_general(dsb, qm, TN, preferred_element_type=F32)
                dv_ref[keys, :] += lax.dot_general(p.astype(BF16), dom, TN, preferred_element_type=F32)
            dq_ref[rows, :] = jnp.where(left, dq[0], dq[1])

    shape = jax.ShapeDtypeStruct((T, RW), F32)
    return pl.pallas_call(
        kern, name=f"attn_bwd_{g}", grid=(HP,),
        in_specs=qkv + [bias_spec] + [tile] * 4, out_specs=[tile] * 3 + [bias_spec],
        out_shape=[shape] * 3 + [jax.ShapeDtypeStruct((HEADS, BLK, 2 * BLK), F32)],
        compiler_params=_cp(("parallel",)),
    )(pq, pq, pq, bias, do, o, lse, dlse)


NBUCKET = 32
NPAIR = BLK * 2 * BLK


def _relbias_table(rbT, onehotT):
    def kern(rb_ref, oh_ref, out_ref):
        out_ref[0] = sum(jnp.dot(p, oh_ref[0], preferred_element_type=F32) for p in _split3(rb_ref[0]))

    return pl.pallas_call(
        kern, name="relbias_table", grid=(3,),
        in_specs=[pl.BlockSpec((1, HEADS, NBUCKET), lambda g: (g, 0, 0)),
                  pl.BlockSpec((1, NBUCKET, NPAIR), lambda g: (g, 0, 0))],
        out_specs=pl.BlockSpec((1, HEADS, NPAIR), lambda g: (g, 0, 0)),
        out_shape=jax.ShapeDtypeStruct((3, HEADS, NPAIR), F32),
        compiler_params=_cp(("parallel",)),
    )(rbT, onehotT)


def _relbias_grad(db, onehotT):
    nt = (((1,), (1,)), ((), ()))

    def kern(db_ref, oh_ref, out_ref):
        hi, mid, _ = _split3(db_ref[0])
        out_ref[0] = (lax.dot_general(hi, oh_ref[0], nt, preferred_element_type=F32)
                      + lax.dot_general(mid, oh_ref[0], nt, preferred_element_type=F32))

    return pl.pallas_call(
        kern, name="relbias_grad", grid=(3,),
        in_specs=[pl.BlockSpec((1, HEADS, NPAIR), lambda g: (g, 0, 0)),
                  pl.BlockSpec((1, NBUCKET, NPAIR), lambda g: (g, 0, 0))],
        out_specs=pl.BlockSpec((1, HEADS, NBUCKET), lambda g: (g, 0, 0)),
        out_shape=jax.ShapeDtypeStruct((3, HEADS, NBUCKET), F32),
        compiler_params=_cp(("parallel",)),
    )(db, onehotT)


def _adamw(w, g, m, v):
    m2 = ADAM_B1 * m + (1.0 - ADAM_B1) * g
    v2 = ADAM_B2 * v + (1.0 - ADAM_B2) * (g * g)
    m_hat = m2 / (1.0 - ADAM_B1 ** ADAM_STEP)
    v_hat = v2 / (1.0 - ADAM_B2 ** ADAM_STEP)
    return -ADAM_LR * (m_hat / (jnp.sqrt(v_hat) + ADAM_EPS) + ADAM_WD * w), m2, v2


def _ada_mod(c_all, ada_w, ada_b_loc):
    def kern(c_ref, w_ref, b_ref, o_ref):
        c = c_ref[...]
        cond = c * jax.nn.sigmoid(c)
        o_ref[0] = jnp.dot(cond, w_ref[0], precision=HI, preferred_element_type=F32) + b_ref[0]

    ncol = ada_w.shape[2]
    return pl.pallas_call(
        kern, name="ada_mod", grid=(2,),
        in_specs=[pl.BlockSpec((NDEV, D), lambda i: (0, 0)),
                  pl.BlockSpec((1, D, ncol), lambda i: (i, 0, 0)),
                  pl.BlockSpec((1, 1, ncol), lambda i: (i, 0, 0))],
        out_specs=pl.BlockSpec((1, NDEV, ncol), lambda i: (i, 0, 0)),
        out_shape=jax.ShapeDtypeStruct((2, NDEV, ncol), F32),
        compiler_params=_cp(("parallel",)),
    )(c_all, ada_w, ada_b_loc.reshape(2, 1, ncol))


def _ada_grad_adamw(cT_all, dmod_loc, w, m, v):
    ncol = w.shape[2]
    tr = 256

    def kern(c_ref, d_ref, w_ref, m_ref, v_ref, g_ref, dl_ref, m2_ref, v2_ref):
        c = c_ref[...]
        cond = c * jax.nn.sigmoid(c)
        g = jnp.dot(cond, d_ref[0], precision=HI, preferred_element_type=F32)
        dl, m2, v2 = _adamw(w_ref[0], g, m_ref[0], v_ref[0])
        g_ref[0], dl_ref[0], m2_ref[0], v2_ref[0] = g, dl, m2, v2

    big = pl.BlockSpec((1, tr, ncol), lambda i, j: (i, j, 0))
    shp = jax.ShapeDtypeStruct(w.shape, F32)
    return pl.pallas_call(
        kern, name="ada_grad_adamw", grid=(2, D // tr),
        in_specs=[pl.BlockSpec((tr, NDEV), lambda i, j: (j, 0)),
                  pl.BlockSpec((1, NDEV, ncol), lambda i, j: (i, 0, 0)), big, big, big],
        out_specs=[big] * 4, out_shape=[shp] * 4,
        compiler_params=_cp(("parallel", "parallel")),
    )(cT_all, dmod_loc, w, m, v)


def _sum_adamw(recv, w, m, v, name, tr):
    S = recv.shape[0]
    R, C = w.shape
    assert R % tr == 0 and recv.shape[1:] == (R, C)

    def kern(r_ref, w_ref, m_ref, v_ref, g_ref, dl_ref, m2_ref, v2_ref):
        g = r_ref[0].astype(F32)
        for s in range(1, S):
            g = g + r_ref[s].astype(F32)
        dl, m2, v2 = _adamw(w_ref[...], g, m_ref[...], v_ref[...])
        g_ref[...], dl_ref[...], m2_ref[...], v2_ref[...] = g, dl, m2, v2

    flat = pl.BlockSpec((tr, C), lambda i: (i, 0))
    shp = jax.ShapeDtypeStruct((R, C), F32)
    return pl.pallas_call(
        kern, name=name, grid=(R // tr,),
        in_specs=[pl.BlockSpec((S, tr, C), lambda i: (0, i, 0)), flat, flat, flat],
        out_specs=[flat] * 4, out_shape=[shp] * 4,
        compiler_params=_cp(("parallel",)),
    )(recv, w, m, v)


def _pack(arrs, dtype, row_mult):
    flat = jnp.concatenate([a.reshape(-1).astype(dtype) for a in arrs])
    flat = jnp.pad(flat, (0, -flat.shape[0] % (128 * row_mult)))
    return flat.reshape(-1, 128)


def _pack8(arrs, dtype, row_mult):
    flat = jnp.concatenate([a.reshape(NDEV, -1).astype(dtype) for a in arrs], axis=1)
    flat = jnp.pad(flat, ((0, 0), (0, -flat.shape[1] % (128 * row_mult))))
    return flat.reshape(NDEV, -1, 128)


def _unpack(buf, shapes, lead=()):
    flat = buf.reshape(lead + (-1,))
    out, off = [], 0
    for s in shapes:
        n = math.prod(s)
        out.append(flat[..., off:off + n].reshape(lead + tuple(s)))
        off += n
    return out


def _to_chunks(full, kind):
    if kind == "col":
        x = full.reshape(full.shape[:-1] + (NDEV, full.shape[-1] // NDEV))
        return jnp.moveaxis(x, -2, 0)
    x = full.reshape(full.shape[:-2] + (NDEV, full.shape[-2] // NDEV, full.shape[-1]))
    return jnp.moveaxis(x, -3, 0)


def _from_chunks(g8, kind):
    if kind == "col":
        x = jnp.moveaxis(g8, 0, -2)
        return x.reshape(x.shape[:-2] + (x.shape[-2] * x.shape[-1],))
    x = jnp.moveaxis(g8, 0, -3)
    return x.reshape(x.shape[:-3] + (x.shape[-3] * x.shape[-2], x.shape[-1]))


def _pad_pa(x):
    z = lambda n: jnp.zeros(x.shape[:-1] + (n,), x.dtype)
    return jnp.concatenate([x[..., :1600], z(64), x[..., 1600:1664], z(64), x[..., 1664:1824], z(96)], -1)


def _unpad_pa(x):
    return jnp.concatenate([x[..., :1600], x[..., 1664:1728], x[..., 1792:1952]], -1)


AB_SEGMENTS = ((0, 1600, 0), (1600, 1664, 64), (1664, 1824, 128), (1824, 3360, PAB - 3360))
AB_SHARD = 3360 // NDEV


def _ab_in_padded(g8):
    blocks, at = [], 0
    for start, end, shift in AB_SEGMENTS:
        if start + shift > at:
            blocks.append(jnp.zeros((g8.shape[1], start + shift - at), g8.dtype))
        for j in range(start // AB_SHARD, (end - 1) // AB_SHARD + 1):
            lo, hi = max(start, j * AB_SHARD), min(end, (j + 1) * AB_SHARD)
            blocks.append(g8[j, :, lo - j * AB_SHARD:hi - j * AB_SHARD])
        at = end + shift
    return jnp.concatenate(blocks, axis=1)


def _ab_in_shards(padded):
    shards = []
    for j in range(NDEV):
        pieces = [padded[:, max(start, j * AB_SHARD) + shift:min(end, (j + 1) * AB_SHARD) + shift]
                  for start, end, shift in AB_SEGMENTS if max(start, j * AB_SHARD) < min(end, (j + 1) * AB_SHARD)]
        shards.append(jnp.concatenate(pieces, axis=1))
    return jnp.stack(shards)


def _pad_rows(x, n):
    return jnp.pad(x, ((0, n - x.shape[0]), (0, 0)))


def _bucket_tables():
    qi = jnp.arange(BLK)[:, None]
    ki = jnp.arange(2 * BLK)[None, :]
    rel = BLK + qi - ki
    tabs = []
    for dil in DILS:
        dist = jnp.clip(rel, 0, BLK) * dil
        logd = jnp.log(jnp.maximum(dist, 1).astype(F32) / 16) / math.log(2048 / 16)
        large = jnp.minimum(16 + (logd * 16).astype(jnp.int32), 31)
        tabs.append(jnp.where(dist < 16, dist, large))
    return jnp.stack(tabs)


SHARDED = (("ln_g", "col"), ("ln_b", "col"), ("ab_w_in", "col"), ("rw_w_up", "col"), ("rw_a_up", "col"),
           ("rw_g_up", "col"), ("sc_conv_w", "col"), ("ab_w_out", "row"), ("dil_w_qkv", "col"),
           ("dil_w_out", "col"), ("mlp_w1", "col"), ("mlp_w2", "row"))
FIRST = ("ab_w_in",)
LATER = ("ab_w_out", "dil_w_qkv", "dil_w_out", "mlp_w1", "mlp_w2")
GATHER_BF16 = FIRST + LATER
GATHER_F32 = ("rw_w_up", "rw_a_up", "rw_g_up", "sc_conv_w", "ln_g", "ln_b")
REPLICATED = ("ada_b", "rw_mu", "rw_w0", "rw_a0", "rw_k_k", "rw_k_a", "rw_r_k", "rw_lnx_g", "rw_lnx_b", "rel_bias")
WEIGHTS = ("ada_w", "ada_b", "ln_g", "ln_b", "ab_w_in", "rw_mu", "rw_w0", "rw_w_up", "rw_a0", "rw_a_up",
           "rw_g_up", "rw_k_k", "rw_k_a", "rw_r_k", "rw_lnx_g", "rw_lnx_b", "sc_conv_w", "ab_w_out",
           "dil_w_qkv", "dil_w_out", "rel_bias", "mlp_w1", "mlp_w2")


def _local_step(x0, tgt, mod, W, P, later_weights, early_grads):
    row = lambda a: a.reshape(1, -1)
    W = dict(W)
    m6 = mod.reshape(2, 6, 1, D)
    sc = [m6[0, 1], m6[0, 4], m6[1, 1], m6[1, 4]]
    sh = [m6[0, 0], m6[0, 3], m6[1, 0], m6[1, 3]]
    gt = [m6[0, 2], m6[0, 5], m6[1, 2], m6[1, 5]]
    lng = [row(P["ln_g"][0, 0]), row(P["ln_g"][0, 1]), row(P["ln_g"][1, 0]), row(P["ln_g"][1, 1])]
    lnb = [row(P["ln_b"][0, 0]), row(P["ln_b"][0, 1]), row(P["ln_b"][1, 0]), row(P["ln_b"][1, 1])]
    E = jnp.kron(jnp.eye(HEADS, dtype=BF16), jnp.ones((HD, HD), BF16))

    def mod_body(r, p):
        u = r[0] * (1.0 + p[0]) + p[1]
        return [u, u], []

    (u0, u0T), _ = _rows("modulate", mod_body, [x0], [sc[0], sh[0]], [(D, BF16), (D, BF16, "T")])

    def post_fwd_body(r, p):
        xn, un = _post_ln_mod(r[0], r[1], *p)
        return [xn, un, un], []

    def post_fwd(s, x, y):
        (xn, un, unT), _ = _rows(f"post_ln_{s}", post_fwd_body, [x, y],
                                 [gt[s], lng[s], lnb[s], sc[s + 1], sh[s + 1]],
                                 [(D, F32), (D, BF16), (D, BF16, "T")])
        return xn, un, unT

    def relu2(acc):
        a = jnp.maximum(acc, 0.0)
        return acc, a * a, a * a

    def relu2_bwd(acc, h):
        return (acc * (2.0 * jnp.maximum(h, 0.0)),)

    p = _mm("ab_in", u0, W["ab_w_in"])
    mu = _pad_pa(P["rw_mu"])
    mu_parts = [mu[:, :512], mu[:, 512:1024], mu[:, 1024:1536], mu[:, 1536:1664], mu[:, 1664:1792], mu[:, 1792:]]
    pre_params = mu_parts + [P["rw_w0"], _pad_rows(P["rw_w_up"], 128), P["rw_a0"], _pad_rows(P["rw_a_up"], 128),
                             _pad_rows(P["rw_g_up"], 256), P["rw_k_k"], P["rw_k_a"],
                             P["sc_conv_w"][0:1], P["sc_conv_w"][1:2], P["sc_conv_w"][2:3]]
    pieces = [(p, 512, 0), (p, 512, 1), (p, 512, 2), (p, 128, 12), (p, 128, 13), (p, 256, 7),
              (p, 512, 4), (p, 512, 5), (p, 512, 6)]
    shifted = [0, 1, 2, 3, 4, 5, 6, 8]
    pre_rows = pieces + [pieces[i] + ("prev",) for i in shifted]
    NPR = 19

    def pre_args(r):
        x, prev = r[:9], dict(zip(shifted, r[9:17]))
        down = lambda i, k: _shift_down(x[i], prev[i], k)
        return x[:6] + [down(i, 1) for i in range(6)] + x[6:9] + [down(6, 1), down(8, 1), down(6, 2), down(8, 2)]

    def pre_fwd_body(r, pp):
        outs = list(_pre_core(pp[0], *pre_args(r), *pp[1:]))
        return outs + [outs[3]], []

    (r_, w_, kh_, v_, a_, b_, gate_, yb, v_cols), _ = _rows(
        "rwkv_pre", pre_fwd_body, pre_rows, [E] + pre_params,
        [(RW, F32)] * 7 + [(RW, BF16), (RW, BF16, "cols3")], tm=256)
    scan_in = [r_, w_, kh_, a_, b_, v_cols]
    ysc, *saved = _scan_fwd(*scan_in)
    post_params = [P["rw_lnx_g"], P["rw_lnx_b"], P["rw_r_k"].reshape(1, RW)]

    def postmix_fwd_body(r, pp):
        return [_post_core(pp[0], *r, *pp[1:])], []

    (ya,), _ = _rows("rwkv_post", postmix_fwd_body, [ysc, r_, kh_, v_, gate_], [E] + post_params,
                     [(RW, BF16)])
    cat = jnp.concatenate([ya, yb], axis=1)
    W.update(later_weights(cat))
    y0 = _mm("ab_out", cat, W["ab_w_out"])
    x1, u1, u1T = post_fwd(0, x0, y0)

    h1, a1, a1T = _mm("mlp1_up_0", u1, W["mlp_w1"][0], out=(F32, BF16, (BF16, "T")), epi=relu2)
    y1 = _mm("mlp1_down_0", a1, W["mlp_w2"][0])
    x2, u2, u2T = post_fwd(1, x1, y1)

    pq = _mm("qkv", u2, W["dil_w_qkv"])
    onehotT = (_bucket_tables().reshape(3, 1, NPAIR) == jnp.arange(NBUCKET).reshape(1, NBUCKET, 1)).astype(BF16)
    rbT = P["rel_bias"].reshape(NBUCKET, 3, HEADS).transpose(1, 2, 0)
    bias = _relbias_table(rbT, onehotT).reshape(3, HEADS, BLK, 2 * BLK)
    og, lse = zip(*[_attn_fwd(pq, bias[g], g) for g in range(3)])

    def merge_fwd_body(r, pp):
        return [_merge_core(*r)], []

    (om,), _ = _rows("attn_merge", merge_fwd_body, list(og + lse), [], [(RW, BF16)])
    y2 = _mm("dil_out", om, W["dil_w_out"])
    x3, u3, u3T = post_fwd(2, x2, y2)

    h3, a3, a3T = _mm("mlp1_up_1", u3, W["mlp_w1"][1], out=(F32, BF16, (BF16, "T")), epi=relu2)
    y3 = _mm("mlp1_down_1", a3, W["mlp_w2"][1])

    def last_body(r, pp):
        x, y, tg = r
        xn, vjp = jax.vjp(_post_ln, x, y, *pp)
        err = xn - tg
        dx, dy, dg, dlg, dlb = vjp(err * (1.0 / D))
        loss = jnp.full((1, 128), (0.5 / D) * jnp.sum(err * err), F32)
        return [dx, dy], [loss, dg, dlg, dlb]

    (dxp, dy3), (loss_acc, dg3, dlng3, dlnb3) = _rows(
        "final_ln_loss", last_body, [x3, y3, tgt], [gt[3], lng[3], lnb[3]],
        [(D, F32), (D, BF16)], [(1, 128), (1, D), (1, D), (1, D)])

    G = {}
    dsc, dsh, dgt = [None] * 4, [None] * 4, [None] * 4
    dlng, dlnb = [None] * 4, [None] * 4
    dgt[3], dlng[3], dlnb[3] = dg3, dlng3, dlnb3

    def mlp_bwd(i, uT, h, aT, dy):
        dh = _mm(f"mlp_dh_{i}", dy, W["mlp_w2"][i], tb=True, out=(BF16,), epi=relu2_bwd, extras=(h,))
        gw2 = _mm(f"mlp_dw2_{i}", aT, dy)
        du = _mm(f"mlp_du_{i}", dh, W["mlp_w1"][i], tb=True)
        gw1 = _mm(f"mlp_dw1_{i}", uT, dh)
        return du, gw1, gw2

    def post_bwd_body(r, pp):
        x, y, dxn, dun = r
        _, vjp = jax.vjp(_post_ln_mod, x, y, *pp)
        dx, dy, dg, dlg, dlb, dscn, dshn = vjp((dxn, dun))
        return [dx, dy], [dg, dlg, dlb, dscn, dshn]

    def post_bwd(s, x, y, dxn, dun):
        (dx, dy), (dgt[s], dlng[s], dlnb[s], dsc[s + 1], dsh[s + 1]) = _rows(
            f"post_ln_bwd_{s}", post_bwd_body, [x, y, dxn, dun],
            [gt[s], lng[s], lnb[s], sc[s + 1], sh[s + 1]], [(D, F32), (D, BF16)], [(1, D)] * 5)
        return dx, dy

    du3, gw1_1, gw2_1 = mlp_bwd(1, u3T, h3, a3T, dy3)
    dxp, dy2 = post_bwd(2, x2, y2, dxp, du3)

    G["dil_w_out"] = _mm("dil_out_dw", om.T, dy2)[None]
    do = _mm("dil_out_dx", dy2, W["dil_w_out"], tb=True)

    def merge_bwd_body(r, pp):
        _, vjp = jax.vjp(_merge_core, *r[:6])
        return list(vjp(r[6])), []

    mb, _ = _rows("attn_merge_bwd", merge_bwd_body, list(og + lse) + [do], [], [(RW, F32)] * 6)
    back = [_attn_bwd(pq, bias[g], mb[g], og[g], lse[g], mb[3 + g], g) for g in range(3)]
    dpq = jnp.concatenate([t for dq, dk, dv, _ in back for t in (dq, dk, dv)], axis=1).astype(BF16)
    rb = _relbias_grad(jnp.stack([b[3] for b in back]).reshape(3, HEADS, NPAIR), onehotT)
    G["rel_bias"] = rb.transpose(2, 0, 1).reshape(NBUCKET, 3 * HEADS)
    G["dil_w_qkv"] = _mm("qkv_dw", u2T, dpq)[None]
    du2 = _mm("qkv_dx", dpq, W["dil_w_qkv"], tb=True)
    dxp, dy1 = post_bwd(1, x1, y1, dxp, du2)

    du1, gw1_0, gw2_0 = mlp_bwd(0, u1T, h1, a1T, dy1)
    G["mlp_w1"] = jnp.stack([gw1_0, gw1_1])
    G["mlp_w2"] = jnp.stack([gw2_0, gw2_1])
    dxp, dy0 = post_bwd(0, x0, y0, dxp, du1)

    G["ab_w_out"] = _mm("ab_out_dw", cat.T, dy0)[None]
    dcat = _mm("ab_out_dx", dy0, W["ab_w_out"], tb=True)
    post_params = [post_params[0] + early_grads(G)] + post_params[1:]

    def postmix_bwd_body(r, pp):
        _, vjp = jax.vjp(functools.partial(_post_core, pp[0]), *r[:5], *pp[1:])
        d = vjp(r[5])
        return list(d[:5]), list(d[5:])

    (dy_cols, dr1, dkh1, dv1, dgate), (G["rw_lnx_g"], G["rw_lnx_b"], drk) = _rows(
        "rwkv_post_bwd", postmix_bwd_body, [ysc, r_, kh_, v_, gate_, (dcat, 512, 0)], [E] + post_params,
        [(RW, BF16, "cols3")] + [(RW, F32)] * 4, [(1, RW)] * 3)
    G["rw_r_k"] = drk.reshape(1, HEADS, HD)
    dr2, dw2, dk2, da2, db2, dv2 = _scan_bwd(*scan_in, dy_cols, *saved)

    def pre_bwd_body(r, pp):
        prim, ct = pre_args(r[:len(pre_rows)]), r[len(pre_rows):]
        _, vjp = jax.vjp(functools.partial(_pre_core, pp[0]), *prim, *pp[1:])
        cts = (ct[0] + ct[1], ct[2], ct[3] + ct[4], ct[5] + ct[6], ct[7], ct[8], ct[9], ct[10])
        d = vjp(cts)
        z = jnp.zeros_like(d[12])
        dp = jnp.concatenate([d[0], d[1], d[2], d[3], d[4], d[5], d[12], d[13], d[14]], axis=1)
        dp1 = jnp.concatenate([d[6], d[7], d[8], d[9], d[10], d[11], d[15], z, d[16]], axis=1)
        dp2 = jnp.concatenate([d[17], z, d[18]], axis=1)
        return [dp, dp1, dp2], list(d[NPR:])

    acc_shapes = [a.shape for a in pre_params]
    (dp, dp1, dp2), pacc = _rows(
        "rwkv_pre_bwd", pre_bwd_body,
        pre_rows + [dr1, dr2, dw2, dkh1, dk2, dv1, dv2, da2, db2, dgate, (dcat, 512, 1)],
        [E] + pre_params, [(PAB, F32), (PAB, F32), (PB, F32)], acc_shapes, tm=256)
    G["rw_mu"] = _unpad_pa(jnp.concatenate(pacc[:6], axis=1))
    G["rw_w0"], G["rw_a0"], G["rw_k_k"], G["rw_k_a"] = pacc[6], pacc[8], pacc[11], pacc[12]
    G["rw_w_up"] = pacc[7][None, :64]
    G["rw_a_up"] = pacc[9][None, :64]
    G["rw_g_up"] = pacc[10][None, :160]
    G["sc_conv_w"] = jnp.concatenate(pacc[13:16], axis=0)[None]

    def shift_merge_body(r, pp):
        d0, d1, d1_next, d2, d2_next = r
        d = d0 + _shift_up(d1, d1_next, 1)
        return [jnp.concatenate([d[:, :PA], d[:, PA:] + _shift_up(d2, d2_next, 2)], axis=1)], []

    (dpt,), _ = _rows("shift_merge", shift_merge_body,
                      [dp, dp1, (dp1, PAB, 0, "next"), dp2, (dp2, PB, 0, "next")], [], [(PAB, BF16)], tm=256)
    du0 = _mm("ab_in_dx", dpt, W["ab_w_in"], tb=True)

    def mod_bwd_body(r, pp):
        du, dx, x = r
        return [dx + du * (1.0 + pp[0])], [jnp.sum(du * x, axis=0, keepdims=True), jnp.sum(du, axis=0, keepdims=True)]

    (grad_x,), (dsc[0], dsh[0]) = _rows("modulate_bwd", mod_bwd_body, [du0, dxp, x0], [sc[0]], [(D, F32)],
                                        [(1, D), (1, D)])

    G["ln_g"] = jnp.concatenate(dlng, axis=0).reshape(2, 2, D)
    G["ln_b"] = jnp.concatenate(dlnb, axis=0).reshape(2, 2, D)
    dmod = jnp.concatenate([dsh[0], dsc[0], dgt[0], dsh[1], dsc[1], dgt[1],
                            dsh[2], dsc[2], dgt[2], dsh[3], dsc[3], dgt[3]], axis=1).reshape(2, 6 * D)
    return loss_acc[0, 0], grad_x, dmod, G, lambda: _ab_in_shards(_mm("ab_in_dw", u0T, dpt))[:, None]


def kernel(x, c, ada_w, ada_b, ln_g, ln_b, ab_w_in, rw_mu, rw_w0, rw_w_up, rw_a0, rw_a_up, rw_g_up, rw_k_k, rw_k_a, rw_r_k, rw_lnx_g, rw_lnx_b, sc_conv_w, ab_w_out, dil_w_qkv, dil_w_out, rel_bias, mlp_w1, mlp_w2, loss_target, m_ada_w, m_ada_b, m_ln_g, m_ln_b, m_ab_w_in, m_rw_mu, m_rw_w0, m_rw_w_up, m_rw_a0, m_rw_a_up, m_rw_g_up, m_rw_k_k, m_rw_k_a, m_rw_r_k, m_rw_lnx_g, m_rw_lnx_b, m_sc_conv_w, m_ab_w_out, m_dil_w_qkv, m_dil_w_out, m_rel_bias, m_mlp_w1, m_mlp_w2, v_ada_w, v_ada_b, v_ln_g, v_ln_b, v_ab_w_in, v_rw_mu, v_rw_w0, v_rw_w_up, v_rw_a0, v_rw_a_up, v_rw_g_up, v_rw_k_k, v_rw_k_a, v_rw_r_k, v_rw_lnx_g, v_rw_lnx_b, v_sc_conv_w, v_ab_w_out, v_dil_w_qkv, v_dil_w_out, v_rel_bias, v_mlp_w1, v_mlp_w2):
    w = dict(ada_w=ada_w, ada_b=ada_b, ln_g=ln_g, ln_b=ln_b, ab_w_in=ab_w_in, rw_mu=rw_mu, rw_w0=rw_w0,
             rw_w_up=rw_w_up, rw_a0=rw_a0, rw_a_up=rw_a_up, rw_g_up=rw_g_up, rw_k_k=rw_k_k, rw_k_a=rw_k_a,
             rw_r_k=rw_r_k, rw_lnx_g=rw_lnx_g, rw_lnx_b=rw_lnx_b, sc_conv_w=sc_conv_w, ab_w_out=ab_w_out,
             dil_w_qkv=dil_w_qkv, dil_w_out=dil_w_out, rel_bias=rel_bias, mlp_w1=mlp_w1, mlp_w2=mlp_w2)
    m = dict(ada_w=m_ada_w, ada_b=m_ada_b, ln_g=m_ln_g, ln_b=m_ln_b, ab_w_in=m_ab_w_in, rw_mu=m_rw_mu,
             rw_w0=m_rw_w0, rw_w_up=m_rw_w_up, rw_a0=m_rw_a0, rw_a_up=m_rw_a_up, rw_g_up=m_rw_g_up,
             rw_k_k=m_rw_k_k, rw_k_a=m_rw_k_a, rw_r_k=m_rw_r_k, rw_lnx_g=m_rw_lnx_g, rw_lnx_b=m_rw_lnx_b,
             sc_conv_w=m_sc_conv_w, ab_w_out=m_ab_w_out, dil_w_qkv=m_dil_w_qkv, dil_w_out=m_dil_w_out,
             rel_bias=m_rel_bias, mlp_w1=m_mlp_w1, mlp_w2=m_mlp_w2)
    v = dict(ada_w=v_ada_w, ada_b=v_ada_b, ln_g=v_ln_g, ln_b=v_ln_b, ab_w_in=v_ab_w_in, rw_mu=v_rw_mu,
             rw_w0=v_rw_w0, rw_w_up=v_rw_w_up, rw_a0=v_rw_a0, rw_a_up=v_rw_a_up, rw_g_up=v_rw_g_up,
             rw_k_k=v_rw_k_k, rw_k_a=v_rw_k_a, rw_r_k=v_rw_r_k, rw_lnx_g=v_rw_lnx_g, rw_lnx_b=v_rw_lnx_b,
             sc_conv_w=v_sc_conv_w, ab_w_out=v_ab_w_out, dil_w_qkv=v_dil_w_qkv, dil_w_out=v_dil_w_out,
             rel_bias=v_rel_bias, mlp_w1=v_mlp_w1, mlp_w2=v_mlp_w2)
    kinds = dict(SHARDED)
    me = 4 * lax.axis_index("x") + 2 * lax.axis_index("y") + lax.axis_index("c")
    ncol = ada_w.shape[2]

    small = _all_gather(_pack([c] + [w[n] for n in GATHER_F32], F32, 8), "gather_small")
    parts = _unpack(small, [c.shape] + [w[n].shape for n in GATHER_F32], (NDEV,))
    c_all = parts[0].reshape(NDEV, D)
    P = {n: _from_chunks(t, kinds[n]) for n, t in zip(GATHER_F32, parts[1:])}
    P = {n: (t if n in ("ln_g", "ln_b") else t[0]) for n, t in P.items()}
    for n in REPLICATED[1:]:
        P[n] = w[n]
    def full(n, t):
        t = _from_chunks(t, kinds[n])
        return t if n in ("mlp_w1", "mlp_w2") else t[0]

    (first,) = _all_gather_many([ab_w_in.astype(BF16)], "gather_first_weight")
    W = {"ab_w_in": _ab_in_padded(first[:, 0])}

    ada_b_loc = lax.dynamic_slice(ada_b, (0, ncol * me), (2, ncol))
    mod_part = _ada_mod(c_all, ada_w, ada_b_loc)
    mod_all = _all_gather(mod_part.reshape(-1, 128), "gather_mod").reshape(NDEV, 2, NDEV, ncol)
    mod = lax.dynamic_index_in_dim(mod_all, me, axis=2, keepdims=False)
    mod = mod.transpose(1, 0, 2).reshape(2, 6 * D)

    behind = (mod[0, 0] * 0.0).astype(BF16)
    later = _exchange_start([w[n].astype(BF16) + (behind if n == LATER[0] else 0) for n in LATER], True,
                            "gather_later_weights_start")
    mod = mod + later[-1][0, 0]

    def later_weights(after):
        lands = _exchange_wait(later, True, after, "gather_later_weights_wait")
        return {n: full(n, t) for n, t in zip(LATER, lands)}

    sent = []

    def early_grads(G):
        sent.append(_exchange_start([_to_chunks(G[n], kinds[n]).astype(BF16) for n in LATER], False,
                                    "exchange_later_grads_start"))
        return sent[0][-1][0, 0]

    loss_part, grad_x, dmod, G, in_grad = _local_step(x[0], loss_target[0], mod, W, P, later_weights, early_grads)
    G["ada_b"] = dmod
    big_out = {}

    def update(n, contributions):
        cols = w[n].shape[-1]
        flat = lambda t: t.reshape(-1, cols)
        rows = flat(w[n]).shape[0]
        outs = _sum_adamw(contributions.reshape(-1, rows, cols), flat(w[n]), flat(m[n]), flat(v[n]),
                          f"sum_adamw_{n}", min(rows, 256))
        big_out[n] = [o.reshape(w[n].shape) for o in outs]

    rep_shapes = [w[n].shape for n in REPLICATED] + [(1,)]
    rep_all = _all_gather(_pack([G[n] for n in REPLICATED] + [loss_part], F32, 8), "gather_replicated_grads")
    names = [n for n, _ in SHARDED if n not in GATHER_BF16]
    shard_shapes = [w[n].shape for n in names]
    recv = _all_to_all(_pack8([_to_chunks(G[n], kinds[n]) for n in names], F32, 8), "exchange_small_grads")

    behind = (recv[0, 0, 0] * 0.0 + rep_all[0, 0, 0] * 0.0).astype(BF16)
    last = _exchange_start([in_grad().astype(BF16) + behind], False, "exchange_last_grad_start")

    zero = last[-1][0:1, 0]
    pk = lambda d: _pack([d[n] for n in REPLICATED] + [zero], F32, 8)
    rep_out = _sum_adamw(rep_all, pk(w), pk(m), pk(v), "sum_adamw_replicated", rep_all.shape[1])
    loss = _unpack(rep_out[0], rep_shapes)[-1][0]
    rep_out = [dict(zip(REPLICATED, _unpack(o, rep_shapes))) for o in rep_out]
    dmod_all = _unpack(rep_all, [(2, 6 * D)], (NDEV,))[0]
    dmod_loc = lax.dynamic_slice(dmod_all, (0, 0, ncol * me), (NDEV, 2, ncol)).transpose(1, 0, 2)
    ada_out = _ada_grad_adamw(c_all.T + zero, dmod_loc, ada_w, m_ada_w, v_ada_w)
    pk = lambda d: _pack([d[n] for n in names], F32, 8)
    sh_out = _sum_adamw(recv, pk(w), pk(m), pk(v), "sum_adamw_small", recv.shape[1])
    sh_out = [dict(zip(names, _unpack(o, shard_shapes))) for o in sh_out]
    for n, r in zip(LATER, _exchange_wait(sent[0], False, last[-1], "exchange_later_grads_wait")):
        update(n, r)
    (landed,) = _exchange_wait(last, False, big_out[LATER[-1]][0], "exchange_last_grad_wait")
    update("ab_w_in", landed)
    sh_out = [{**d, **{n: big_out[n][i] for n in GATHER_BF16}} for i, d in enumerate(sh_out)]

    def pick(i, n):
        if n == "ada_w":
            return ada_out[i]
        return rep_out[i][n] if n in REPLICATED else sh_out[i][n]

    outs = [loss, grad_x[None]]
    for i in range(4):
        outs += [pick(i, n) for n in WEIGHTS]
    return tuple(outs)
```

```python
import functools
import math

import jax
import jax.numpy as jnp
from jax import lax
from jax.experimental import pallas as pl
from jax.experimental.pallas import tpu as pltpu

F32 = jnp.float32
BF16 = jnp.bfloat16
HI = lax.Precision.HIGHEST

NDEV = 8
T = 2048
D = 1024
DFF = 4096
HEADS = 8
HD = 64
RW = 512
PA = 2048
PB = 1536
PAB = PA + PB
QKV = 4608
DILS = (1, 4, 16)
BLK = 128
ALPHA = 4.0 ** 0.25
LN_EPS = 1e-5
GN_EPS = 64e-5
ADAM_LR, ADAM_B1, ADAM_B2, ADAM_EPS, ADAM_WD, ADAM_STEP = 0.001, 0.9, 0.999, 1e-8, 0.01, 10
VMEM_LIMIT = 56 * 1024 * 1024


def _cp(sem):
    return pltpu.CompilerParams(dimension_semantics=sem, vmem_limit_bytes=VMEM_LIMIT)


def _slot(px, py, pc):
    return 4 * px + 2 * py + pc


def _all_gather(x, name):
    R, C = x.shape

    def body(x_ref, out_ref, send_sems, recv_sems, local_sem):
        xi, yi, ci = lax.axis_index("x"), lax.axis_index("y"), lax.axis_index("c")
        me, sibling = (xi, yi, ci), (xi, yi, 1 - ci)
        chips = [(1 - xi, yi), (xi, 1 - yi), (1 - xi, 1 - yi)]

        def rows(px, py, pc):
            return out_ref.at[_slot(px, py, pc)]

        def copy(k, block, to, src=None):
            return pltpu.make_async_remote_copy(
                src_ref=rows(*block) if src is None else src, dst_ref=rows(*block),
                send_sem=send_sems.at[k], recv_sem=recv_sems.at[k],
                device_id=to, device_id_type=pl.DeviceIdType.MESH)

        mine = pltpu.make_async_copy(x_ref, rows(*me), local_sem)
        mine.start()
        first = [copy(0, me, sibling, src=x_ref)]
        first += [copy(1 + j, me, (*chip, ci), src=x_ref) for j, chip in enumerate(chips)]
        for cp in first:
            cp.start()
        passed = [copy(4 + j, (*chip, ci), sibling) for j, chip in enumerate(chips)]
        for j, chip in enumerate(chips):
            copy(1 + j, (*chip, ci), me).wait_recv()
            passed[j].start()
        copy(0, sibling, me).wait_recv()
        for j, chip in enumerate(chips):
            copy(4 + j, (*chip, 1 - ci), me).wait_recv()
        for cp in first + passed:
            cp.wait_send()
        mine.wait()

    return pl.pallas_call(
        body, name=name,
        out_shape=jax.ShapeDtypeStruct((NDEV, R, C), x.dtype),
        in_specs=[pl.BlockSpec(memory_space=pl.ANY)],
        out_specs=pl.BlockSpec(memory_space=pl.ANY),
        scratch_shapes=[pltpu.SemaphoreType.DMA((7,)), pltpu.SemaphoreType.DMA((7,)),
                        pltpu.SemaphoreType.DMA(())],
    )(x)


def _all_to_all(g, name):
    _, R, C = g.shape

    def body(g_ref, out_ref, send_sems, recv_sems, local_sem):
        xi, yi, ci = lax.axis_index("x"), lax.axis_index("y"), lax.axis_index("c")
        my_slot = _slot(xi, yi, ci)
        mine = pltpu.make_async_copy(g_ref.at[my_slot], out_ref.at[my_slot], local_sem)
        mine.start()
        copies = []
        for k in range(1, 8):
            px = 1 - xi if k & 4 else xi
            py = 1 - yi if k & 2 else yi
            pc = 1 - ci if k & 1 else ci
            peer_slot = _slot(px, py, pc)
            copies.append((
                pltpu.make_async_remote_copy(
                    src_ref=g_ref.at[peer_slot], dst_ref=out_ref.at[my_slot],
                    send_sem=send_sems.at[k - 1], recv_sem=recv_sems.at[k - 1],
                    device_id=(px, py, pc), device_id_type=pl.DeviceIdType.MESH),
                pltpu.make_async_remote_copy(
                    src_ref=g_ref.at[peer_slot], dst_ref=out_ref.at[peer_slot],
                    send_sem=send_sems.at[k - 1], recv_sem=recv_sems.at[k - 1],
                    device_id=(px, py, pc), device_id_type=pl.DeviceIdType.MESH)))
        for send, _ in copies:
            send.start()
        for _, recv in copies:
            recv.wait_recv()
        for send, _ in copies:
            send.wait_send()
        mine.wait()

    return pl.pallas_call(
        body, name=name,
        out_shape=jax.ShapeDtypeStruct((NDEV, R, C), g.dtype),
        in_specs=[pl.BlockSpec(memory_space=pl.ANY)],
        out_specs=pl.BlockSpec(memory_space=pl.ANY),
        scratch_shapes=[pltpu.SemaphoreType.DMA((7,)), pltpu.SemaphoreType.DMA((7,)),
                        pltpu.SemaphoreType.DMA(())],
    )(g)


def _my_slot():
    return _slot(lax.axis_index("x"), lax.axis_index("y"), lax.axis_index("c"))


def _put_own(buf, own, slot):
    return lax.dynamic_update_index_in_dim(buf, own, slot, 0)


def _hbm_call(body, name, ins, out_shapes, n_sems):
    anyspec = pl.BlockSpec(memory_space=pl.ANY)
    return pl.pallas_call(
        body, name=name, out_shape=out_shapes,
        in_specs=[anyspec] * len(ins), out_specs=[anyspec] * len(out_shapes),
        scratch_shapes=[pltpu.SemaphoreType.DMA(s) for s in n_sems],
    )(*ins)


def _all_gather_many(xs, name):
    n = len(xs)

    def body(*refs):
        x_refs, o_refs = refs[:n], refs[n:2 * n]
        send_sems, recv_sems = refs[2 * n:]
        xi, yi, ci = lax.axis_index("x"), lax.axis_index("y"), lax.axis_index("c")
        me, sibling = (xi, yi, ci), (xi, yi, 1 - ci)
        chips = [(1 - xi, yi), (xi, 1 - yi), (1 - xi, 1 - yi)]

        def copy(i, k, block, to, src=None):
            dst = o_refs[i].at[_slot(*block)]
            return pltpu.make_async_remote_copy(
                src_ref=dst if src is None else src, dst_ref=dst,
                send_sem=send_sems.at[i, k], recv_sem=recv_sems.at[i, k],
                device_id=to, device_id_type=pl.DeviceIdType.MESH)

        sends = []
        for i in range(n):
            sends += [copy(i, 1 + j, me, (*chip, ci), src=x_refs[i]) for j, chip in enumerate(chips)]
            sends.append(copy(i, 0, me, sibling, src=x_refs[i]))
        for cp in sends:
            cp.start()
        for j, chip in enumerate(chips):
            for i in range(n):
                copy(i, 1 + j, (*chip, ci), me).wait_recv()
                passed = copy(i, 4 + j, (*chip, ci), sibling)
                passed.start()
                sends.append(passed)
        for i in range(n):
            copy(i, 0, sibling, me).wait_recv()
            for j, chip in enumerate(chips):
                copy(i, 4 + j, (*chip, 1 - ci), me).wait_recv()
        for cp in sends:
            cp.wait_send()

    outs = _hbm_call(body, name, xs, [jax.ShapeDtypeStruct((NDEV,) + x.shape, x.dtype) for x in xs],
                     [(n, 7), (n, 7)])
    return [_put_own(o, x[None], _my_slot()) for o, x in zip(outs, xs)]


def _peers(xi, yi, ci):
    return [(1 - xi if k & 4 else xi, 1 - yi if k & 2 else yi, 1 - ci if k & 1 else ci) for k in range(1, 8)]


def _direct_copy(src_refs, land_refs, send_sems, recv_sems, i, k, peer, my_slot, gather):
    src = src_refs[i] if gather else src_refs[i].at[_slot(*peer)]
    return pltpu.make_async_remote_copy(
        src_ref=src, dst_ref=land_refs[i].at[my_slot], send_sem=send_sems.at[7 * i + k], recv_sem=recv_sems.at[7 * i + k],
        device_id=peer, device_id_type=pl.DeviceIdType.MESH)


def _exchange_start(srcs, gather, name):
    n = len(srcs)
    lands = [lax.empty(((NDEV,) + s.shape) if gather else s.shape, s.dtype) for s in srcs]

    def body(*refs):
        s_refs, l_refs = refs[:n], refs[n:2 * n]
        send_sems, recv_sems = refs[2 * n], refs[2 * n + 1]
        token = refs[2 * n + 2 + 2 * n]
        xi, yi, ci = lax.axis_index("x"), lax.axis_index("y"), lax.axis_index("c")
        my_slot = _slot(xi, yi, ci)
        for k, peer in enumerate(_peers(xi, yi, ci)):
            for i in range(n):
                _direct_copy(s_refs, l_refs, send_sems, recv_sems, i, k, peer, my_slot, gather).start()
        token[...] = jnp.zeros_like(token)

    hbm = pl.BlockSpec(memory_space=pltpu.HBM)
    sem = pl.BlockSpec(memory_space=pltpu.SEMAPHORE)
    both = list(srcs) + lands
    return pl.pallas_call(
        body, name=name,
        out_shape=(pltpu.SemaphoreType.DMA((7 * n,)), pltpu.SemaphoreType.DMA((7 * n,)),
                   *[pltpu.HBM(t.shape, t.dtype) for t in both], jax.ShapeDtypeStruct((8, 128), F32)),
        in_specs=[hbm] * (2 * n),
        out_specs=(sem, sem, *[hbm] * (2 * n), pl.BlockSpec(memory_space=pltpu.VMEM)),
        input_output_aliases={i: 2 + i for i in range(2 * n)},
        compiler_params=pltpu.CompilerParams(has_side_effects=pltpu.SideEffectType.DATAFLOW_SIDE_EFFECTING),
    )(*[pltpu.with_memory_space_constraint(t, pltpu.HBM) for t in both])


def _exchange_wait(started, gather, after, name):
    send_sems, recv_sems, *thru, _ = started
    n = len(thru) // 2

    def body(*refs):
        s_refs, l_refs = refs[:n], refs[n:2 * n]
        send_sems, recv_sems = refs[2 * n], refs[2 * n + 1]
        xi, yi, ci = lax.axis_index("x"), lax.axis_index("y"), lax.axis_index("c")
        my_slot = _slot(xi, yi, ci)
        for k, peer in enumerate(_peers(xi, yi, ci)):
            for i in range(n):
                _direct_copy(s_refs, l_refs, send_sems, recv_sems, i, k, peer, my_slot, gather).wait_send()
                _direct_copy(s_refs, l_refs, send_sems, recv_sems, i, k, peer, _slot(*peer), gather).wait_recv()

    hbm = pl.BlockSpec(memory_space=pltpu.HBM)
    sem = pl.BlockSpec(memory_space=pltpu.SEMAPHORE)
    outs = pl.pallas_call(
        body, name=name,
        out_shape=tuple(pltpu.HBM(t.shape, t.dtype) for t in thru),
        in_specs=[hbm] * (2 * n) + [sem, sem, pl.BlockSpec(memory_space=pl.ANY)],
        out_specs=tuple([hbm] * (2 * n)),
        input_output_aliases={i: i for i in range(2 * n)},
        compiler_params=pltpu.CompilerParams(has_side_effects=pltpu.SideEffectType.DATAFLOW_SIDE_EFFECTING),
    )(*thru, send_sems, recv_sems, after)
    slot = _my_slot()
    own = [s[None] if gather else lax.dynamic_index_in_dim(s, slot, 0, keepdims=True) for s in outs[:n]]
    return [_put_own(land, o, slot) for land, o in zip(outs[n:], own)]


def _mm(name, a, b, tb=False, out=(F32,), epi=None, extras=(), tm=2048, tn=512, tk_cap=2048):
    M, K = a.shape
    N = b.shape[0] if tb else b.shape[1]
    tm, tn = min(tm, M), min(tn, N)
    tk = max(t for t in range(128, min(K, tk_cap) + 1, 128) if K % t == 0)
    assert M % tm == 0 and N % tn == 0 and K % tk == 0, (name, M, N, K)
    nk = K // tk
    ne, no = len(extras), len(out)
    dims = (((1,), (1 if tb else 0,)), ((), ()))
    flipped = [isinstance(o, tuple) for o in out]

    def kern(*refs):
        a_ref, b_ref = refs[:2]
        e_refs = refs[2:2 + ne]
        o_refs = refs[2 + ne:2 + ne + no]

        def finish(acc):
            outs = epi(acc, *[e[...] for e in e_refs]) if epi is not None else (acc,)
            for o_ref, o, flip in zip(o_refs, outs, flipped):
                o_ref[...] = (o.T if flip else o).astype(o_ref.dtype)

        part = lax.dot_general(a_ref[...], b_ref[...], dims, preferred_element_type=F32)
        if nk == 1:
            finish(part)
            return
        acc_ref = refs[-1]
        k = pl.program_id(2)

        @pl.when(k == 0)
        def _():
            acc_ref[...] = part

        @pl.when(k > 0)
        def _():
            acc_ref[...] += part

        @pl.when(k == nk - 1)
        def _():
            finish(acc_ref[...])

    piped = nk == 1 and (M // tm) * (N // tn) >= 3
    deep = dict(pipeline_mode=pl.Buffered(3)) if piped else {}
    b_spec = (pl.BlockSpec((tn, tk), lambda i, j, k: (j, k), **deep) if tb
              else pl.BlockSpec((tk, tn), lambda i, j, k: (k, j), **deep))
    tile = pl.BlockSpec((tm, tn), lambda i, j, k: (i, j))
    tile_t = pl.BlockSpec((tn, tm), lambda i, j, k: (j, i))
    if piped:
        in_specs = [pl.BlockSpec((tm, tk), lambda i, j, k: (i, k)), b_spec] + [tile] * ne
        out_specs = [tile_t if flip else tile for flip in flipped]

        def outer(*refs):
            pltpu.emit_pipeline(kern, grid=(M // tm, N // tn, 1), in_specs=in_specs,
                                out_specs=out_specs)(*refs)

        res = pl.pallas_call(
            outer, name=name,
            in_specs=[pl.BlockSpec(memory_space=pl.ANY)] * (2 + ne),
            out_specs=[pl.BlockSpec(memory_space=pl.ANY)] * no,
            out_shape=[jax.ShapeDtypeStruct((N, M), o[0]) if flip else jax.ShapeDtypeStruct((M, N), o)
                       for o, flip in zip(out, flipped)],
            compiler_params=_cp(None),
        )(a, b, *extras)
        return res[0] if no == 1 else res
    res = pl.pallas_call(
        kern, name=name, grid=(M // tm, N // tn, nk),
        in_specs=[pl.BlockSpec((tm, tk), lambda i, j, k: (i, k)), b_spec] + [tile] * ne,
        out_specs=[tile_t if flip else tile for flip in flipped],
        out_shape=[jax.ShapeDtypeStruct((N, M), o[0]) if flip else jax.ShapeDtypeStruct((M, N), o)
                   for o, flip in zip(out, flipped)],
        scratch_shapes=[pltpu.VMEM((tm, tn), F32)] if nk > 1 else [],
        compiler_params=_cp(("parallel", "parallel", "arbitrary")),
    )(a, b, *extras)
    return res[0] if no == 1 else res


HALO = 8


def _rows(name, body, rows, params, out_rows, out_accs=(), tm=512):
    views = [r if isinstance(r, tuple) else (r, r.shape[1], 0) for r in rows]
    n = views[0][0].shape[0]
    assert n % tm == 0 and tm % HALO == 0
    nr, npar, nor, noa = len(views), len(params), len(out_rows), len(out_accs)
    kinds = [o[2] if len(o) == 3 else None for o in out_rows]
    store = {None: lambda o: o, "T": lambda o: o.T, "cols3": _cols3_tile}
    spec = {None: lambda w: pl.BlockSpec((tm, w), lambda i: (i, 0)),
            "T": lambda w: pl.BlockSpec((w, tm), lambda i: (0, i)),
            "cols3": lambda w: pl.BlockSpec((HP, HD, 6 * tm), lambda i: (0, 0, i))}
    shape = {None: lambda w: (n, w), "T": lambda w: (w, n), "cols3": lambda w: (HP, HD, 6 * n)}

    def row_spec(width, cb, halo=None):
        per, last = tm // HALO, n // HALO - 1
        if halo == "prev":
            return pl.BlockSpec((HALO, width), lambda i: (jnp.maximum(i * per - 1, 0), cb))
        if halo == "next":
            return pl.BlockSpec((HALO, width), lambda i: (jnp.minimum((i + 1) * per, last), cb))
        return pl.BlockSpec((tm, width), lambda i: (i, cb))

    def kern(*refs):
        r_refs = refs[:nr]
        p_refs = refs[nr:nr + npar]
        o_refs = refs[nr + npar:nr + npar + nor]
        a_refs = refs[nr + npar + nor:]
        outs, accs = body([r[...] for r in r_refs], [p[...] for p in p_refs])
        assert len(outs) == nor and len(accs) == noa, (name, len(outs), len(accs))
        for o_ref, o, kind in zip(o_refs, outs, kinds):
            o_ref[...] = store[kind](o).astype(o_ref.dtype)
        if noa:
            @pl.when(pl.program_id(0) == 0)
            def _():
                for a_ref in a_refs:
                    a_ref[...] = jnp.zeros_like(a_ref)

            for a_ref, a in zip(a_refs, accs):
                a_ref[...] += a.astype(F32)

    def whole(shape):
        nd = len(shape)
        return pl.BlockSpec(tuple(shape), lambda i, nd=nd: (0,) * nd)

    in_specs = [row_spec(*v[1:]) for v in views]
    in_specs += [whole(p.shape) for p in params]
    out_specs = [spec[kind](o[0]) for o, kind in zip(out_rows, kinds)]
    out_specs += [whole(s) for s in out_accs]
    out_shape = [jax.ShapeDtypeStruct(shape[kind](o[0]), o[1]) for o, kind in zip(out_rows, kinds)]
    out_shape += [jax.ShapeDtypeStruct(tuple(s), F32) for s in out_accs]
    res = pl.pallas_call(
        kern, name=name, grid=(n // tm,), in_specs=in_specs, out_specs=out_specs,
        out_shape=out_shape, compiler_params=_cp(("arbitrary",)),
    )(*[v[0] for v in views], *params)
    return res[:nor], res[nor:]


def _shift_down(x, prev, k):
    head = jnp.where(pl.program_id(0) == 0, 0.0, pltpu.roll(prev, k, axis=0))
    row = lax.broadcasted_iota(jnp.int32, x.shape, 0)
    return jnp.where(row < k, jnp.tile(head, (x.shape[0] // HALO, 1)), pltpu.roll(x, k, axis=0))


def _shift_up(x, nxt, k):
    n = x.shape[0]
    tail = jnp.where(pl.program_id(0) == pl.num_programs(0) - 1, 0.0, pltpu.roll(nxt, HALO - k, axis=0))
    row = lax.broadcasted_iota(jnp.int32, x.shape, 0)
    return jnp.where(row >= n - k, jnp.tile(tail, (n // HALO, 1)), pltpu.roll(x, n - k, axis=0))


@jax.custom_vjp
def _headsum(x, e):
    return sum(jnp.dot(p, e, preferred_element_type=F32) for p in _split3(x))


_headsum.defvjp(lambda x, e: (_headsum(x, e), e), lambda e, ct: (_headsum(ct, e), None))


def _softplus(z):
    return jnp.maximum(z, 0.0) + jnp.log(1.0 + jnp.exp(jnp.minimum(z, -z)))


def _post_ln(x, y, g, lng, lnb):
    z = ALPHA * x + (1.0 + g) * y
    mu = jnp.mean(z, axis=-1, keepdims=True)
    zc = z - mu
    var = jnp.mean(zc * zc, axis=-1, keepdims=True)
    return zc * lax.rsqrt(var + LN_EPS) * lng + lnb


def _post_ln_mod(x, y, g, lng, lnb, scn, shn):
    xn = _post_ln(x, y, g, lng, lnb)
    return xn, xn * (1.0 + scn) + shn


def _pre_core(E, r_, k_, v_, wd_, ad_, gd_, r1, k1, v1, wd1, ad1, gd1, h, bg, cg, h1, cg1, h2, cg2,
              mu_r, mu_k, mu_v, mu_wd, mu_ad, mu_gd, w0, w_up, a0, a_up, g_up, k_k, k_a,
              cw0, cw1, cw2):
    def mix(x, x1, mu):
        return x + mu * (x1 - x)

    r, k, v = mix(r_, r1, mu_r), mix(k_, k1, mu_k), mix(v_, v1, mu_v)
    wd, ad, gd = mix(wd_, wd1, mu_wd), mix(ad_, ad1, mu_ad), mix(gd_, gd1, mu_gd)
    logw = -_softplus(-(w0 + jnp.dot(jnp.tanh(wd), w_up, preferred_element_type=F32))) - 0.5
    decay = jnp.exp(-jnp.exp(logw))
    iclr = jax.nn.sigmoid(a0 + jnp.dot(ad, a_up, preferred_element_type=F32))
    gate = jnp.dot(jax.nn.sigmoid(gd), g_up, preferred_element_type=F32)
    kk0 = k * k_k
    nrm = jnp.sqrt(_headsum(kk0 * kk0, E))
    kk = kk0 / jnp.maximum(nrm, 1e-12)
    kh = k * (1.0 + (iclr - 1.0) * k_a)
    yb = bg * (cw2 * (cg * h) + cw1 * (cg1 * h1) + cw0 * (cg2 * h2))
    return r, decay, kh, v, -kk, kk * iclr, gate, yb


def _post_core(E, y, r, kh, v, gate, lnx_g, lnx_b, rk):
    def seg(t):
        return _headsum(t, E)

    mean = seg(y) * (1.0 / HD)
    yc = y - mean
    var = seg(yc * yc) * (1.0 / HD)
    gn = yc * lax.rsqrt(var + GN_EPS) * lnx_g + lnx_b
    bonus = seg(r * kh * rk) * v
    return (gn + bonus) * gate


def _merge_core(o0, o1, o2, l0, l1, l2):
    m = jnp.maximum(jnp.maximum(l0, l1), l2)
    e0, e1, e2 = jnp.exp(l0 - m), jnp.exp(l1 - m), jnp.exp(l2 - m)
    den = e0 + e1 + e2
    return (e0 * o0 + e1 * o1 + e2 * o2) / den


CHUNK = 128
HALF = 64
HP = HEADS // 2
LW = 2 * HD
NCHUNK = T // CHUNK


def _split3(x):
    hi = x.astype(BF16)
    r1 = x - hi.astype(F32)
    mid = r1.astype(BF16)
    return hi, mid, (r1 - mid.astype(F32)).astype(BF16)


def _cols3_tile(x):
    left = lax.broadcasted_iota(jnp.int32, (HD, CHUNK), 1) < HALF
    xts = [x[c * CHUNK:(c + 1) * CHUNK].T for c in range(x.shape[0] // CHUNK)]
    pairs = []
    for p in range(HP):
        groups = []
        for xt in xts:
            a, b = xt[p * LW:p * LW + HD], xt[p * LW + HD:(p + 1) * LW]
            for half in (jnp.where(left, a, pltpu.roll(b, HALF, axis=1)), jnp.where(left, pltpu.roll(a, HALF, axis=1), b)):
                groups += list(_split3(half))
        pairs.append(jnp.concatenate(groups, axis=1))
    return jnp.stack(pairs)


def _pick_codes():
    row = lax.broadcasted_iota(jnp.int32, (6 * HALF, LW), 0)
    col = lax.broadcasted_iota(jnp.int32, (6 * HALF, LW), 1)
    same = ((row & (LW - 1)) >= HALF) == (col >= HD)
    return jnp.where(same, row & (HALF - 1), -1).astype(BF16)


def _column(block_ref, codes, half, i):
    pick = jnp.where(codes == i.astype(BF16), jnp.ones((), BF16), jnp.zeros((), BF16))
    block = block_ref[:, :, half * 6 * HALF:(half + 1) * 6 * HALF].reshape(HP * HD, 6 * HALF)
    return jnp.dot(block, pick, preferred_element_type=F32)


def _halfsums(x, row, left1):
    row_l = jnp.where(left1, row, 0.0)
    return (jnp.sum(x * row_l, axis=1, keepdims=True), jnp.sum(x * (row - row_l), axis=1, keepdims=True))


def _pair_rows(row):
    return [row[:, p * LW:(p + 1) * LW] for p in range(HP)]


def _store_columns(ref, p, t_mask, cols):
    for j, col in enumerate(cols):
        pltpu.store(ref.at[pl.ds(2 * p + j, 1)], jnp.broadcast_to(col[None], (1, HD, CHUNK)), mask=t_mask[None])


def _columns_to_rows(cols_ref, rows_ref):
    for p in range(HP):
        rows_ref[:, p * LW:(p + 1) * LW] = cols_ref[2 * p:2 * p + 2].reshape(LW, CHUNK).T


NHALF = T // HALF
HALVES = CHUNK // HALF


def _scan_fwd(r, w, k, a, b, v3):
    def kern(r_ref, w_ref, k_ref, a_ref, b_ref, v_ref, y_ref, ck_ref, st_hbm, sa_hbm,
             s_ref, vb_ref, yc_ref, st_ref, sa_ref, sems):
        c = pl.program_id(0)

        @pl.when(c == 0)
        def _():
            s_ref[...] = jnp.zeros_like(s_ref)

        lane = lax.broadcasted_iota(jnp.int32, (HD, CHUNK), 1)
        left = lane < HD
        left1 = lax.broadcasted_iota(jnp.int32, (1, LW), 1) < HD
        codes = _pick_codes()

        def flush(slot, half_index):
            return [pltpu.make_async_copy(src.at[slot], dst.at[half_index], sems.at[j, slot])
                    for j, (src, dst) in enumerate(((st_ref, st_hbm), (sa_ref, sa_hbm)))]

        for half in range(HALVES):
            ck_ref[half] = s_ref[...]
            vb_ref[...] = _column(v_ref, codes, half, jnp.int32(0))

            @pl.when(c > 0)
            def _():
                for cp in flush(half, (c - 1) * HALVES + half):
                    cp.wait()

            def step(i, carry):
                t = half * HALF + i
                row = lambda ref: _pair_rows(ref[pl.ds(t, 1), :])
                S = [s_ref[p] for p in range(HP)]
                sa = [jnp.where(left, *_halfsums(s, a, left1)) for s, a in zip(S, row(a_ref))]
                S = [s * w + c_ * b + vb_ref[pl.ds(p * HD, HD), :] * k
                     for p, (s, w, c_, b, k) in enumerate(zip(S, row(w_ref), sa, row(b_ref), row(k_ref)))]
                for p, (s, c_) in enumerate(zip(S, sa)):
                    s_ref[p] = s
                    st_ref[half, i, p] = s
                    sa_ref[half, i, p] = c_
                for p, (s, r) in enumerate(zip(S, row(r_ref))):
                    _store_columns(yc_ref, p, lane == t, _halfsums(s, r, left1))
                vb_ref[...] = _column(v_ref, codes, half, i + 1)
                return carry

            lax.fori_loop(0, HALF, step, 0, unroll=16)
            for cp in flush(half, c * HALVES + half):
                cp.start()
        _columns_to_rows(yc_ref, y_ref)

        @pl.when(c == NCHUNK - 1)
        def _():
            for half in range(HALVES):
                for cp in flush(half, c * HALVES + half):
                    cp.wait()

    rowblk = pl.BlockSpec((CHUNK, RW), lambda c: (c, 0))
    saved = jax.ShapeDtypeStruct((NHALF, HALF, HP, HD, LW), F32)
    stage = pltpu.VMEM((HALVES, HALF, HP, HD, LW), F32)
    return pl.pallas_call(
        kern, name="rwkv_scan_fwd", grid=(NCHUNK,),
        in_specs=[rowblk] * 5 + [pl.BlockSpec((HP, HD, 6 * CHUNK), lambda c: (0, 0, c))],
        out_specs=[rowblk, pl.BlockSpec((HALVES, HP, HD, LW), lambda c: (c, 0, 0, 0)),
                   pl.BlockSpec(memory_space=pl.ANY), pl.BlockSpec(memory_space=pl.ANY)],
        out_shape=[jax.ShapeDtypeStruct((T, RW), F32), jax.ShapeDtypeStruct((NHALF, HP, HD, LW), F32), saved, saved],
        scratch_shapes=[pltpu.VMEM((HP, HD, LW), F32), pltpu.VMEM((HP * HD, LW), F32),
                        pltpu.VMEM((HEADS, HD, CHUNK), F32), stage, stage, pltpu.SemaphoreType.DMA((2, HALVES))],
        compiler_params=_cp(("arbitrary",)),
    )(r, w, k, a, b, v3)


def _scan_bwd(r, w, k, a, b, v3, dy3, ck, st, sa):
    def kern(r_ref, w_ref, k_ref, a_ref, b_ref, v_ref, dy_ref, ck_ref, st_hbm, sa_hbm,
             dr_ref, dw_ref, dk_ref, da_ref, db_ref, dv_ref, ds_ref, sb_ref, sa_ref, pick_ref, dvc_ref, sems):
        c = pl.program_id(0)
        chunk = NCHUNK - 1 - c

        @pl.when(c == 0)
        def _():
            ds_ref[...] = jnp.zeros_like(ds_ref)

        lane = lax.broadcasted_iota(jnp.int32, (HD, CHUNK), 1)
        left = lane < HD
        left1 = lax.broadcasted_iota(jnp.int32, (1, LW), 1) < HD
        codes = _pick_codes()

        def rowsum(x):
            return jnp.sum(x, axis=0, keepdims=True)

        def fetch(slot, half_index):
            return [pltpu.make_async_copy(st_hbm.at[half_index], sb_ref.at[slot, pl.ds(1, HALF)], sems.at[0, slot]),
                    pltpu.make_async_copy(sa_hbm.at[half_index], sa_ref.at[slot], sems.at[1, slot])]

        def picks(half, i):
            pick_ref[pl.ds(0, HP * HD), :] = _column(v_ref, codes, half, i)
            pick_ref[pl.ds(HP * HD, HP * HD), :] = _column(dy_ref, codes, half, i)

        @pl.when(c == 0)
        def _():
            for cp in fetch(HALVES - 1, chunk * HALVES + HALVES - 1):
                cp.start()

        for half in reversed(range(HALVES)):
            base = half * HALF
            for cp in fetch(half, chunk * HALVES + half):
                cp.wait()
            if half:
                for cp in fetch(half - 1, chunk * HALVES + half - 1):
                    cp.start()
            else:
                @pl.when(chunk > 0)
                def _():
                    for cp in fetch(HALVES - 1, chunk * HALVES - 1):
                        cp.start()
            sb_ref[half, 0] = ck_ref[half]
            picks(half, jnp.int32(HALF - 1))

            def back(ii, carry):
                i = HALF - 1 - ii
                t = base + i
                row = lambda ref: _pair_rows(ref[pl.ds(t, 1), :])
                a_r, b_r, k_r, w_r, r_r = row(a_ref), row(b_ref), row(k_ref), row(w_ref), row(r_ref)
                vs = [pick_ref[pl.ds(p * HD, HD), :] for p in range(HP)]
                dys = [pick_ref[pl.ds((HP + p) * HD, HD), :] for p in range(HP)]
                picks(half, jnp.maximum(i - 1, 0))
                dr, dw, db, dk, da = [], [], [], [], []
                for p in range(HP):
                    Sp, dy = sb_ref[half, i, p], dys[p]
                    dS = ds_ref[p] + dy * r_r[p]
                    dr.append(rowsum(sb_ref[half, i + 1, p] * dy))
                    dw.append(rowsum(dS * Sp))
                    db.append(rowsum(dS * sa_ref[half, i, p]))
                    dk.append(rowsum(dS * vs[p]))
                    dsa = jnp.where(left, *_halfsums(dS, b_r[p], left1))
                    _store_columns(dvc_ref, p, lane == t, _halfsums(dS, k_r[p], left1))
                    da.append(rowsum(Sp * dsa))
                    ds_ref[p] = dS * w_r[p] + dsa * a_r[p]
                for ref, pieces in ((dr_ref, dr), (dw_ref, dw), (db_ref, db), (dk_ref, dk), (da_ref, da)):
                    ref[pl.ds(t, 1), :] = jnp.concatenate(pieces, axis=1)
                return carry

            lax.fori_loop(0, HALF, back, 0, unroll=16)
        _columns_to_rows(dvc_ref, dv_ref)

    rowblk = pl.BlockSpec((CHUNK, RW), lambda c: (NCHUNK - 1 - c, 0))
    col3blk = pl.BlockSpec((HP, HD, 6 * CHUNK), lambda c: (0, 0, NCHUNK - 1 - c))
    rowshape = jax.ShapeDtypeStruct((T, RW), F32)
    return pl.pallas_call(
        kern, name="rwkv_scan_bwd", grid=(NCHUNK,),
        in_specs=[rowblk] * 5 + [col3blk, col3blk,
                                 pl.BlockSpec((HALVES, HP, HD, LW), lambda c: (NCHUNK - 1 - c, 0, 0, 0)),
                                 pl.BlockSpec(memory_space=pl.ANY), pl.BlockSpec(memory_space=pl.ANY)],
        out_specs=[rowblk] * 6, out_shape=[rowshape] * 6,
        scratch_shapes=[pltpu.VMEM((HP, HD, LW), F32), pltpu.VMEM((HALVES, HALF + 1, HP, HD, LW), F32),
                        pltpu.VMEM((HALVES, HALF, HP, HD, LW), F32), pltpu.VMEM((2 * HP * HD, LW), F32),
                        pltpu.VMEM((HEADS, HD, CHUNK), F32), pltpu.SemaphoreType.DMA((2, HALVES))],
        compiler_params=_cp(("arbitrary",)),
    )(r, w, k, a, b, v3, dy3, ck, st, sa)


NT = (((1,), (1,)), ((), ()))
TN = (((0,), (0,)), ((), ()))
SCALE = HD ** -0.5
QKV_G = 3 * RW


def _attn_setup(g):
    dil = DILS[g]
    qkv = [pl.BlockSpec((T, LW), lambda hp, c=(g * QKV_G + s * RW) // LW: (0, c + hp)) for s in range(3)]
    tile = pl.BlockSpec((T, LW), lambda hp: (0, hp))
    bias = pl.BlockSpec((2, BLK, 2 * BLK), lambda hp: (hp, 0, 0))

    def blocks():
        for r in range(dil):
            for n in range(T // dil // BLK):
                rows = pl.ds(n * BLK * dil + r, BLK, stride=dil)
                keys = pl.ds((n - 1) * BLK * dil + r, 2 * BLK, stride=dil) if n else rows
                yield n, rows, keys

    return qkv, tile, bias, blocks


def _band(n):
    qi = lax.broadcasted_iota(jnp.int32, (BLK, 2 * BLK), 0)
    ki = lax.broadcasted_iota(jnp.int32, (BLK, 2 * BLK), 1)
    band = (ki >= qi) & (ki <= qi + BLK)
    return band if n else band[:, BLK:]


def _head_masks():
    lane = lax.broadcasted_iota(jnp.int32, (BLK, LW), 1)
    return lane < HD, [(lane < HD).astype(BF16), (lane >= HD).astype(BF16)]


def _attn_fwd(pq, bias, g):
    qkv, tile, bias_spec, blocks = _attn_setup(g)

    def kern(q_ref, k_ref, v_ref, b_ref, o_ref, l_ref):
        left, masks = _head_masks()
        for n, rows, keys in blocks():
            qb, kc, vc = q_ref[rows, :].astype(BF16), k_ref[keys, :].astype(BF16), v_ref[keys, :].astype(BF16)
            valid = _band(n)
            o, lse = [], []
            for j in range(2):
                bias_j = b_ref[j] if n else b_ref[j][:, BLK:]
                s = lax.dot_general(qb * masks[j], kc, NT, preferred_element_type=F32) * SCALE + bias_j
                s = jnp.where(valid, s, -jnp.inf)
                m = jnp.max(s, axis=1, keepdims=True)
                e = jnp.exp(s - m)
                den = jnp.sum(e, axis=1, keepdims=True)
                o.append(jnp.dot((e / den).astype(BF16), vc, preferred_element_type=F32))
                lse.append(m + jnp.log(den))
            o_ref[rows, :] = jnp.where(left, o[0], o[1])
            l_ref[rows, :] = jnp.where(left, lse[0], lse[1])

    shape = jax.ShapeDtypeStruct((T, RW), F32)
    return pl.pallas_call(
        kern, name=f"attn_fwd_{g}", grid=(HP,),
        in_specs=qkv + [bias_spec], out_specs=[tile, tile], out_shape=[shape, shape],
        compiler_params=_cp(("parallel",)),
    )(pq, pq, pq, bias)


def _attn_bwd(pq, bias, do, o, lse, dlse, g):
    qkv, tile, bias_spec, blocks = _attn_setup(g)

    def kern(q_ref, k_ref, v_ref, b_ref, do_ref, o_ref, l_ref, dl_ref, dq_ref, dk_ref, dv_ref, db_ref):
        left, masks = _head_masks()
        lane = lax.broadcasted_iota(jnp.int32, (BLK, LW), 1)
        dk_ref[...] = jnp.zeros_like(dk_ref)
        dv_ref[...] = jnp.zeros_like(dv_ref)
        db_ref[...] = jnp.zeros_like(db_ref)

        def column(tile_, j):
            return jnp.sum(jnp.where(lane == j * HD, tile_, 0.0), axis=1, keepdims=True)

        for n, rows, keys in blocks():
            qb, kc, vc = q_ref[rows, :].astype(BF16), k_ref[keys, :].astype(BF16), v_ref[keys, :].astype(BF16)
            dof, valid = do_ref[rows, :], _band(n)
            dob, terms = dof.astype(BF16), dl_ref[rows, :] - dof * o_ref[rows, :]
            dq = []
            for j in range(2):
                bias_j = b_ref[j] if n else b_ref[j][:, BLK:]
                corr = jnp.sum(terms * masks[j].astype(F32), axis=1, keepdims=True)
                qm, dom = qb * masks[j], dob * masks[j]
                s = lax.dot_general(qm, kc, NT, preferred_element_type=F32) * SCALE + bias_j
                p = jnp.where(valid, jnp.exp(s - column(l_ref[rows, :], j)), 0.0)
                dp = lax.dot_general(dom, vc, NT, preferred_element_type=F32)
                ds = p * (dp + corr)
                if n:
                    db_ref[j] += ds
                else:
                    db_ref[j, :, BLK:] += ds
                dsb = (ds * SCALE).astype(BF16)
                dq.append(jnp.dot(dsb, kc, preferred_element_type=F32))
                dk_ref[keys, :] += lax.dot_general(dsb, qm, TN, preferred_element_type=F32)
                dv_ref[keys, :] += lax.dot_general(p.astype(BF16), dom, TN, preferred_element_type=F32)
            dq_ref[rows, :] = jnp.where(left, dq[0], dq[1])

    shape = jax.ShapeDtypeStruct((T, RW), F32)
    return pl.pallas_call(
        kern, name=f"attn_bwd_{g}", grid=(HP,),
        in_specs=qkv + [bias_spec] + [tile] * 4, out_specs=[tile] * 3 + [bias_spec],
        out_shape=[shape] * 3 + [jax.ShapeDtypeStruct((HEADS, BLK, 2 * BLK), F32)],
        compiler_params=_cp(("parallel",)),
    )(pq, pq, pq, bias, do, o, lse, dlse)


NBUCKET = 32
NPAIR = BLK * 2 * BLK


def _relbias_table(rbT, onehotT):
    def kern(rb_ref, oh_ref, out_ref):
        out_ref[0] = sum(jnp.dot(p, oh_ref[0], preferred_element_type=F32) for p in _split3(rb_ref[0]))

    return pl.pallas_call(
        kern, name="relbias_table", grid=(3,),
        in_specs=[pl.BlockSpec((1, HEADS, NBUCKET), lambda g: (g, 0, 0)),
                  pl.BlockSpec((1, NBUCKET, NPAIR), lambda g: (g, 0, 0))],
        out_specs=pl.BlockSpec((1, HEADS, NPAIR), lambda g: (g, 0, 0)),
        out_shape=jax.ShapeDtypeStruct((3, HEADS, NPAIR), F32),
        compiler_params=_cp(("parallel",)),
    )(rbT, onehotT)


def _relbias_grad(db, onehotT):
    nt = (((1,), (1,)), ((), ()))

    def kern(db_ref, oh_ref, out_ref):
        hi, mid, _ = _split3(db_ref[0])
        out_ref[0] = (lax.dot_general(hi, oh_ref[0], nt, preferred_element_type=F32)
                      + lax.dot_general(mid, oh_ref[0], nt, preferred_element_type=F32))

    return pl.pallas_call(
        kern, name="relbias_grad", grid=(3,),
        in_specs=[pl.BlockSpec((1, HEADS, NPAIR), lambda g: (g, 0, 0)),
                  pl.BlockSpec((1, NBUCKET, NPAIR), lambda g: (g, 0, 0))],
        out_specs=pl.BlockSpec((1, HEADS, NBUCKET), lambda g: (g, 0, 0)),
        out_shape=jax.ShapeDtypeStruct((3, HEADS, NBUCKET), F32),
        compiler_params=_cp(("parallel",)),
    )(db, onehotT)


def _adamw(w, g, m, v):
    m2 = ADAM_B1 * m + (1.0 - ADAM_B1) * g
    v2 = ADAM_B2 * v + (1.0 - ADAM_B2) * (g * g)
    m_hat = m2 / (1.0 - ADAM_B1 ** ADAM_STEP)
    v_hat = v2 / (1.0 - ADAM_B2 ** ADAM_STEP)
    return -ADAM_LR * (m_hat / (jnp.sqrt(v_hat) + ADAM_EPS) + ADAM_WD * w), m2, v2


def _ada_mod(c_all, ada_w, ada_b_loc):
    def kern(c_ref, w_ref, b_ref, o_ref):
        c = c_ref[...]
        cond = c * jax.nn.sigmoid(c)
        o_ref[0] = jnp.dot(cond, w_ref[0], precision=HI, preferred_element_type=F32) + b_ref[0]

    ncol = ada_w.shape[2]
    return pl.pallas_call(
        kern, name="ada_mod", grid=(2,),
        in_specs=[pl.BlockSpec((NDEV, D), lambda i: (0, 0)),
                  pl.BlockSpec((1, D, ncol), lambda i: (i, 0, 0)),
                  pl.BlockSpec((1, 1, ncol), lambda i: (i, 0, 0))],
        out_specs=pl.BlockSpec((1, NDEV, ncol), lambda i: (i, 0, 0)),
        out_shape=jax.ShapeDtypeStruct((2, NDEV, ncol), F32),
        compiler_params=_cp(("parallel",)),
    )(c_all, ada_w, ada_b_loc.reshape(2, 1, ncol))


def _ada_grad_adamw(cT_all, dmod_loc, w, m, v):
    ncol = w.shape[2]
    tr = 256

    def kern(c_ref, d_ref, w_ref, m_ref, v_ref, g_ref, dl_ref, m2_ref, v2_ref):
        c = c_ref[...]
        cond = c * jax.nn.sigmoid(c)
        g = jnp.dot(cond, d_ref[0], precision=HI, preferred_element_type=F32)
        dl, m2, v2 = _adamw(w_ref[0], g, m_ref[0], v_ref[0])
        g_ref[0], dl_ref[0], m2_ref[0], v2_ref[0] = g, dl, m2, v2

    big = pl.BlockSpec((1, tr, ncol), lambda i, j: (i, j, 0))
    shp = jax.ShapeDtypeStruct(w.shape, F32)
    return pl.pallas_call(
        kern, name="ada_grad_adamw", grid=(2, D // tr),
        in_specs=[pl.BlockSpec((tr, NDEV), lambda i, j: (j, 0)),
                  pl.BlockSpec((1, NDEV, ncol), lambda i, j: (i, 0, 0)), big, big, big],
        out_specs=[big] * 4, out_shape=[shp] * 4,
        compiler_params=_cp(("parallel", "parallel")),
    )(cT_all, dmod_loc, w, m, v)


def _sum_adamw(recv, w, m, v, name, tr):
    S = recv.shape[0]
    R, C = w.shape
    assert R % tr == 0 and recv.shape[1:] == (R, C)

    def kern(r_ref, w_ref, m_ref, v_ref, g_ref, dl_ref, m2_ref, v2_ref):
        g = r_ref[0].astype(F32)
        for s in range(1, S):
            g = g + r_ref[s].astype(F32)
        dl, m2, v2 = _adamw(w_ref[...], g, m_ref[...], v_ref[...])
        g_ref[...], dl_ref[...], m2_ref[...], v2_ref[...] = g, dl, m2, v2

    flat = pl.BlockSpec((tr, C), lambda i: (i, 0))
    shp = jax.ShapeDtypeStruct((R, C), F32)
    return pl.pallas_call(
        kern, name=name, grid=(R // tr,),
        in_specs=[pl.BlockSpec((S, tr, C), lambda i: (0, i, 0)), flat, flat, flat],
        out_specs=[flat] * 4, out_shape=[shp] * 4,
        compiler_params=_cp(("parallel",)),
    )(recv, w, m, v)


def _pack(arrs, dtype, row_mult):
    flat = jnp.concatenate([a.reshape(-1).astype(dtype) for a in arrs])
    flat = jnp.pad(flat, (0, -flat.shape[0] % (128 * row_mult)))
    return flat.reshape(-1, 128)


def _pack8(arrs, dtype, row_mult):
    flat = jnp.concatenate([a.reshape(NDEV, -1).astype(dtype) for a in arrs], axis=1)
    flat = jnp.pad(flat, ((0, 0), (0, -flat.shape[1] % (128 * row_mult))))
    return flat.reshape(NDEV, -1, 128)


def _unpack(buf, shapes, lead=()):
    flat = buf.reshape(lead + (-1,))
    out, off = [], 0
    for s in shapes:
        n = math.prod(s)
        out.append(flat[..., off:off + n].reshape(lead + tuple(s)))
        off += n
    return out


def _to_chunks(full, kind):
    if kind == "col":
        x = full.reshape(full.shape[:-1] + (NDEV, full.shape[-1] // NDEV))
        return jnp.moveaxis(x, -2, 0)
    x = full.reshape(full.shape[:-2] + (NDEV, full.shape[-2] // NDEV, full.shape[-1]))
    return jnp.moveaxis(x, -3, 0)


def _from_chunks(g8, kind):
    if kind == "col":
        x = jnp.moveaxis(g8, 0, -2)
        return x.reshape(x.shape[:-2] + (x.shape[-2] * x.shape[-1],))
    x = jnp.moveaxis(g8, 0, -3)
    return x.reshape(x.shape[:-3] + (x.shape[-3] * x.shape[-2], x.shape[-1]))


def _pad_pa(x):
    z = lambda n: jnp.zeros(x.shape[:-1] + (n,), x.dtype)
    return jnp.concatenate([x[..., :1600], z(64), x[..., 1600:1664], z(64), x[..., 1664:1824], z(96)], -1)


def _unpad_pa(x):
    return jnp.concatenate([x[..., :1600], x[..., 1664:1728], x[..., 1792:1952]], -1)


AB_SEGMENTS = ((0, 1600, 0), (1600, 1664, 64), (1664, 1824, 128), (1824, 3360, PAB - 3360))
AB_SHARD = 3360 // NDEV


def _ab_in_padded(g8):
    blocks, at = [], 0
    for start, end, shift in AB_SEGMENTS:
        if start + shift > at:
            blocks.append(jnp.zeros((g8.shape[1], start + shift - at), g8.dtype))
        for j in range(start // AB_SHARD, (end - 1) // AB_SHARD + 1):
            lo, hi = max(start, j * AB_SHARD), min(end, (j + 1) * AB_SHARD)
            blocks.append(g8[j, :, lo - j * AB_SHARD:hi - j * AB_SHARD])
        at = end + shift
    return jnp.concatenate(blocks, axis=1)


def _ab_in_shards(padded):
    shards = []
    for j in range(NDEV):
        pieces = [padded[:, max(start, j * AB_SHARD) + shift:min(end, (j + 1) * AB_SHARD) + shift]
                  for start, end, shift in AB_SEGMENTS if max(start, j * AB_SHARD) < min(end, (j + 1) * AB_SHARD)]
        shards.append(jnp.concatenate(pieces, axis=1))
    return jnp.stack(shards)


def _pad_rows(x, n):
    return jnp.pad(x, ((0, n - x.shape[0]), (0, 0)))


def _bucket_tables():
    qi = jnp.arange(BLK)[:, None]
    ki = jnp.arange(2 * BLK)[None, :]
    rel = BLK + qi - ki
    tabs = []
    for dil in DILS:
        dist = jnp.clip(rel, 0, BLK) * dil
        logd = jnp.log(jnp.maximum(dist, 1).astype(F32) / 16) / math.log(2048 / 16)
        large = jnp.minimum(16 + (logd * 16).astype(jnp.int32), 31)
        tabs.append(jnp.where(dist < 16, dist, large))
    return jnp.stack(tabs)


SHARDED = (("ln_g", "col"), ("ln_b", "col"), ("ab_w_in", "col"), ("rw_w_up", "col"), ("rw_a_up", "col"),
           ("rw_g_up", "col"), ("sc_conv_w", "col"), ("ab_w_out", "row"), ("dil_w_qkv", "col"),
           ("dil_w_out", "col"), ("mlp_w1", "col"), ("mlp_w2", "row"))
FIRST = ("ab_w_in",)
LATER = ("ab_w_out", "dil_w_qkv", "dil_w_out", "mlp_w1", "mlp_w2")
GATHER_BF16 = FIRST + LATER
GATHER_F32 = ("rw_w_up", "rw_a_up", "rw_g_up", "sc_conv_w", "ln_g", "ln_b")
REPLICATED = ("ada_b", "rw_mu", "rw_w0", "rw_a0", "rw_k_k", "rw_k_a", "rw_r_k", "rw_lnx_g", "rw_lnx_b", "rel_bias")
WEIGHTS = ("ada_w", "ada_b", "ln_g", "ln_b", "ab_w_in", "rw_mu", "rw_w0", "rw_w_up", "rw_a0", "rw_a_up",
           "rw_g_up", "rw_k_k", "rw_k_a", "rw_r_k", "rw_lnx_g", "rw_lnx_b", "sc_conv_w", "ab_w_out",
           "dil_w_qkv", "dil_w_out", "rel_bias", "mlp_w1", "mlp_w2")


def _local_step(x0, tgt, mod, W, P, later_weights, early_grads):
    row = lambda a: a.reshape(1, -1)
    W = dict(W)
    m6 = mod.reshape(2, 6, 1, D)
    sc = [m6[0, 1], m6[0, 4], m6[1, 1], m6[1, 4]]
    sh = [m6[0, 0], m6[0, 3], m6[1, 0], m6[1, 3]]
    gt = [m6[0, 2], m6[0, 5], m6[1, 2], m6[1, 5]]
    lng = [row(P["ln_g"][0, 0]), row(P["ln_g"][0, 1]), row(P["ln_g"][1, 0]), row(P["ln_g"][1, 1])]
    lnb = [row(P["ln_b"][0, 0]), row(P["ln_b"][0, 1]), row(P["ln_b"][1, 0]), row(P["ln_b"][1, 1])]
    E = jnp.kron(jnp.eye(HEADS, dtype=BF16), jnp.ones((HD, HD), BF16))

    def mod_body(r, p):
        u = r[0] * (1.0 + p[0]) + p[1]
        return [u, u], []

    (u0, u0T), _ = _rows("modulate", mod_body, [x0], [sc[0], sh[0]], [(D, BF16), (D, BF16, "T")])

    def post_fwd_body(r, p):
        xn, un = _post_ln_mod(r[0], r[1], *p)
        return [xn, un, un], []

    def post_fwd(s, x, y):
        (xn, un, unT), _ = _rows(f"post_ln_{s}", post_fwd_body, [x, y],
                                 [gt[s], lng[s], lnb[s], sc[s + 1], sh[s + 1]],
                                 [(D, F32), (D, BF16), (D, BF16, "T")])
        return xn, un, unT

    def relu2(acc):
        a = jnp.maximum(acc, 0.0)
        return acc, a * a, a * a

    def relu2_bwd(acc, h):
        return (acc * (2.0 * jnp.maximum(h, 0.0)),)

    p = _mm("ab_in", u0, W["ab_w_in"])
    mu = _pad_pa(P["rw_mu"])
    mu_parts = [mu[:, :512], mu[:, 512:1024], mu[:, 1024:1536], mu[:, 1536:1664], mu[:, 1664:1792], mu[:, 1792:]]
    pre_params = mu_parts + [P["rw_w0"], _pad_rows(P["rw_w_up"], 128), P["rw_a0"], _pad_rows(P["rw_a_up"], 128),
                             _pad_rows(P["rw_g_up"], 256), P["rw_k_k"], P["rw_k_a"],
                             P["sc_conv_w"][0:1], P["sc_conv_w"][1:2], P["sc_conv_w"][2:3]]
    pieces = [(p, 512, 0), (p, 512, 1), (p, 512, 2), (p, 128, 12), (p, 128, 13), (p, 256, 7),
              (p, 512, 4), (p, 512, 5), (p, 512, 6)]
    shifted = [0, 1, 2, 3, 4, 5, 6, 8]
    pre_rows = pieces + [pieces[i] + ("prev",) for i in shifted]
    NPR = 19

    def pre_args(r):
        x, prev = r[:9], dict(zip(shifted, r[9:17]))
        down = lambda i, k: _shift_down(x[i], prev[i], k)
        return x[:6] + [down(i, 1) for i in range(6)] + x[6:9] + [down(6, 1), down(8, 1), down(6, 2), down(8, 2)]

    def pre_fwd_body(r, pp):
        outs = list(_pre_core(pp[0], *pre_args(r), *pp[1:]))
        return outs + [outs[3]], []

    (r_, w_, kh_, v_, a_, b_, gate_, yb, v_cols), _ = _rows(
        "rwkv_pre", pre_fwd_body, pre_rows, [E] + pre_params,
        [(RW, F32)] * 7 + [(RW, BF16), (RW, BF16, "cols3")], tm=256)
    scan_in = [r_, w_, kh_, a_, b_, v_cols]
    ysc, *saved = _scan_fwd(*scan_in)
    post_params = [P["rw_lnx_g"], P["rw_lnx_b"], P["rw_r_k"].reshape(1, RW)]

    def postmix_fwd_body(r, pp):
        return [_post_core(pp[0], *r, *pp[1:])], []

    (ya,), _ = _rows("rwkv_post", postmix_fwd_body, [ysc, r_, kh_, v_, gate_], [E] + post_params,
                     [(RW, BF16)])
    cat = jnp.concatenate([ya, yb], axis=1)
    W.update(later_weights(cat))
    y0 = _mm("ab_out", cat, W["ab_w_out"])
    x1, u1, u1T = post_fwd(0, x0, y0)

    h1, a1, a1T = _mm("mlp1_up_0", u1, W["mlp_w1"][0], out=(F32, BF16, (BF16, "T")), epi=relu2)
    y1 = _mm("mlp1_down_0", a1, W["mlp_w2"][0])
    x2, u2, u2T = post_fwd(1, x1, y1)

    pq = _mm("qkv", u2, W["dil_w_qkv"])
    onehotT = (_bucket_tables().reshape(3, 1, NPAIR) == jnp.arange(NBUCKET).reshape(1, NBUCKET, 1)).astype(BF16)
    rbT = P["rel_bias"].reshape(NBUCKET, 3, HEADS).transpose(1, 2, 0)
    bias = _relbias_table(rbT, onehotT).reshape(3, HEADS, BLK, 2 * BLK)
    og, lse = zip(*[_attn_fwd(pq, bias[g], g) for g in range(3)])

    def merge_fwd_body(r, pp):
        return [_merge_core(*r)], []

    (om,), _ = _rows("attn_merge", merge_fwd_body, list(og + lse), [], [(RW, BF16)])
    y2 = _mm("dil_out", om, W["dil_w_out"])
    x3, u3, u3T = post_fwd(2, x2, y2)

    h3, a3, a3T = _mm("mlp1_up_1", u3, W["mlp_w1"][1], out=(F32, BF16, (BF16, "T")), epi=relu2)
    y3 = _mm("mlp1_down_1", a3, W["mlp_w2"][1])

    def last_body(r, pp):
        x, y, tg = r
        xn, vjp = jax.vjp(_post_ln, x, y, *pp)
        err = xn - tg
        dx, dy, dg, dlg, dlb = vjp(err * (1.0 / D))
        loss = jnp.full((1, 128), (0.5 / D) * jnp.sum(err * err), F32)
        return [dx, dy], [loss, dg, dlg, dlb]

    (dxp, dy3), (loss_acc, dg3, dlng3, dlnb3) = _rows(
        "final_ln_loss", last_body, [x3, y3, tgt], [gt[3], lng[3], lnb[3]],
        [(D, F32), (D, BF16)], [(1, 128), (1, D), (1, D), (1, D)])

    G = {}
    dsc, dsh, dgt = [None] * 4, [None] * 4, [None] * 4
    dlng, dlnb = [None] * 4, [None] * 4
    dgt[3], dlng[3], dlnb[3] = dg3, dlng3, dlnb3

    def mlp_bwd(i, uT, h, aT, dy):
        dh = _mm(f"mlp_dh_{i}", dy, W["mlp_w2"][i], tb=True, out=(BF16,), epi=relu2_bwd, extras=(h,))
        gw2 = _mm(f"mlp_dw2_{i}", aT, dy)
        du = _mm(f"mlp_du_{i}", dh, W["mlp_w1"][i], tb=True)
        gw1 = _mm(f"mlp_dw1_{i}", uT, dh)
        return du, gw1, gw2

    def post_bwd_body(r, pp):
        x, y, dxn, dun = r
        _, vjp = jax.vjp(_post_ln_mod, x, y, *pp)
        dx, dy, dg, dlg, dlb, dscn, dshn = vjp((dxn, dun))
        return [dx, dy], [dg, dlg, dlb, dscn, dshn]

    def post_bwd(s, x, y, dxn, dun):
        (dx, dy), (dgt[s], dlng[s], dlnb[s], dsc[s + 1], dsh[s + 1]) = _rows(
            f"post_ln_bwd_{s}", post_bwd_body, [x, y, dxn, dun],
            [gt[s], lng[s], lnb[s], sc[s + 1], sh[s + 1]], [(D, F32), (D, BF16)], [(1, D)] * 5)
        return dx, dy

    du3, gw1_1, gw2_1 = mlp_bwd(1, u3T, h3, a3T, dy3)
    dxp, dy2 = post_bwd(2, x2, y2, dxp, du3)

    G["dil_w_out"] = _mm("dil_out_dw", om.T, dy2)[None]
    do = _mm("dil_out_dx", dy2, W["dil_w_out"], tb=True)

    def merge_bwd_body(r, pp):
        _, vjp = jax.vjp(_merge_core, *r[:6])
        return list(vjp(r[6])), []

    mb, _ = _rows("attn_merge_bwd", merge_bwd_body, list(og + lse) + [do], [], [(RW, F32)] * 6)
    back = [_attn_bwd(pq, bias[g], mb[g], og[g], lse[g], mb[3 + g], g) for g in range(3)]
    dpq = jnp.concatenate([t for dq, dk, dv, _ in back for t in (dq, dk, dv)], axis=1).astype(BF16)
    rb = _relbias_grad(jnp.stack([b[3] for b in back]).reshape(3, HEADS, NPAIR), onehotT)
    G["rel_bias"] = rb.transpose(2, 0, 1).reshape(NBUCKET, 3 * HEADS)
    G["dil_w_qkv"] = _mm("qkv_dw", u2T, dpq)[None]
    du2 = _mm("qkv_dx", dpq, W["dil_w_qkv"], tb=True)
    dxp, dy1 = post_bwd(1, x1, y1, dxp, du2)

    du1, gw1_0, gw2_0 = mlp_bwd(0, u1T, h1, a1T, dy1)
    G["mlp_w1"] = jnp.stack([gw1_0, gw1_1])
    G["mlp_w2"] = jnp.stack([gw2_0, gw2_1])
    dxp, dy0 = post_bwd(0, x0, y0, dxp, du1)

    G["ab_w_out"] = _mm("ab_out_dw", cat.T, dy0)[None]
    dcat = _mm("ab_out_dx", dy0, W["ab_w_out"], tb=True)
    post_params = [post_params[0] + early_grads(G)] + post_params[1:]

    def postmix_bwd_body(r, pp):
        _, vjp = jax.vjp(functools.partial(_post_core, pp[0]), *r[:5], *pp[1:])
        d = vjp(r[5])
        return list(d[:5]), list(d[5:])

    (dy_cols, dr1, dkh1, dv1, dgate), (G["rw_lnx_g"], G["rw_lnx_b"], drk) = _rows(
        "rwkv_post_bwd", postmix_bwd_body, [ysc, r_, kh_, v_, gate_, (dcat, 512, 0)], [E] + post_params,
        [(RW, BF16, "cols3")] + [(RW, F32)] * 4, [(1, RW)] * 3)
    G["rw_r_k"] = drk.reshape(1, HEADS, HD)
    dr2, dw2, dk2, da2, db2, dv2 = _scan_bwd(*scan_in, dy_cols, *saved)

    def pre_bwd_body(r, pp):
        prim, ct = pre_args(r[:len(pre_rows)]), r[len(pre_rows):]
        _, vjp = jax.vjp(functools.partial(_pre_core, pp[0]), *prim, *pp[1:])
        cts = (ct[0] + ct[1], ct[2], ct[3] + ct[4], ct[5] + ct[6], ct[7], ct[8], ct[9], ct[10])
        d = vjp(cts)
        z = jnp.zeros_like(d[12])
        dp = jnp.concatenate([d[0], d[1], d[2], d[3], d[4], d[5], d[12], d[13], d[14]], axis=1)
        dp1 = jnp.concatenate([d[6], d[7], d[8], d[9], d[10], d[11], d[15], z, d[16]], axis=1)
        dp2 = jnp.concatenate([d[17], z, d[18]], axis=1)
        return [dp, dp1, dp2], list(d[NPR:])

    acc_shapes = [a.shape for a in pre_params]
    (dp, dp1, dp2), pacc = _rows(
        "rwkv_pre_bwd", pre_bwd_body,
        pre_rows + [dr1, dr2, dw2, dkh1, dk2, dv1, dv2, da2, db2, dgate, (dcat, 512, 1)],
        [E] + pre_params, [(PAB, F32), (PAB, F32), (PB, F32)], acc_shapes, tm=256)
    G["rw_mu"] = _unpad_pa(jnp.concatenate(pacc[:6], axis=1))
    G["rw_w0"], G["rw_a0"], G["rw_k_k"], G["rw_k_a"] = pacc[6], pacc[8], pacc[11], pacc[12]
    G["rw_w_up"] = pacc[7][None, :64]
    G["rw_a_up"] = pacc[9][None, :64]
    G["rw_g_up"] = pacc[10][None, :160]
    G["sc_conv_w"] = jnp.concatenate(pacc[13:16], axis=0)[None]

    def shift_merge_body(r, pp):
        d0, d1, d1_next, d2, d2_next = r
        d = d0 + _shift_up(d1, d1_next, 1)
        return [jnp.concatenate([d[:, :PA], d[:, PA:] + _shift_up(d2, d2_next, 2)], axis=1)], []

    (dpt,), _ = _rows("shift_merge", shift_merge_body,
                      [dp, dp1, (dp1, PAB, 0, "next"), dp2, (dp2, PB, 0, "next")], [], [(PAB, BF16)], tm=256)
    du0 = _mm("ab_in_dx", dpt, W["ab_w_in"], tb=True)

    def mod_bwd_body(r, pp):
        du, dx, x = r
        return [dx + du * (1.0 + pp[0])], [jnp.sum(du * x, axis=0, keepdims=True), jnp.sum(du, axis=0, keepdims=True)]

    (grad_x,), (dsc[0], dsh[0]) = _rows("modulate_bwd", mod_bwd_body, [du0, dxp, x0], [sc[0]], [(D, F32)],
                                        [(1, D), (1, D)])

    G["ln_g"] = jnp.concatenate(dlng, axis=0).reshape(2, 2, D)
    G["ln_b"] = jnp.concatenate(dlnb, axis=0).reshape(2, 2, D)
    dmod = jnp.concatenate([dsh[0], dsc[0], dgt[0], dsh[1], dsc[1], dgt[1],
                            dsh[2], dsc[2], dgt[2], dsh[3], dsc[3], dgt[3]], axis=1).reshape(2, 6 * D)
    return loss_acc[0, 0], grad_x, dmod, G, lambda: _ab_in_shards(_mm("ab_in_dw", u0T, dpt))[:, None]


def kernel(x, c, ada_w, ada_b, ln_g, ln_b, ab_w_in, rw_mu, rw_w0, rw_w_up, rw_a0, rw_a_up, rw_g_up, rw_k_k, rw_k_a, rw_r_k, rw_lnx_g, rw_lnx_b, sc_conv_w, ab_w_out, dil_w_qkv, dil_w_out, rel_bias, mlp_w1, mlp_w2, loss_target, m_ada_w, m_ada_b, m_ln_g, m_ln_b, m_ab_w_in, m_rw_mu, m_rw_w0, m_rw_w_up, m_rw_a0, m_rw_a_up, m_rw_g_up, m_rw_k_k, m_rw_k_a, m_rw_r_k, m_rw_lnx_g, m_rw_lnx_b, m_sc_conv_w, m_ab_w_out, m_dil_w_qkv, m_dil_w_out, m_rel_bias, m_mlp_w1, m_mlp_w2, v_ada_w, v_ada_b, v_ln_g, v_ln_b, v_ab_w_in, v_rw_mu, v_rw_w0, v_rw_w_up, v_rw_a0, v_rw_a_up, v_rw_g_up, v_rw_k_k, v_rw_k_a, v_rw_r_k, v_rw_lnx_g, v_rw_lnx_b, v_sc_conv_w, v_ab_w_out, v_dil_w_qkv, v_dil_w_out, v_rel_bias, v_mlp_w1, v_mlp_w2):
    w = dict(ada_w=ada_w, ada_b=ada_b, ln_g=ln_g, ln_b=ln_b, ab_w_in=ab_w_in, rw_mu=rw_mu, rw_w0=rw_w0,
             rw_w_up=rw_w_up, rw_a0=rw_a0, rw_a_up=rw_a_up, rw_g_up=rw_g_up, rw_k_k=rw_k_k, rw_k_a=rw_k_a,
             rw_r_k=rw_r_k, rw_lnx_g=rw_lnx_g, rw_lnx_b=rw_lnx_b, sc_conv_w=sc_conv_w, ab_w_out=ab_w_out,
             dil_w_qkv=dil_w_qkv, dil_w_out=dil_w_out, rel_bias=rel_bias, mlp_w1=mlp_w1, mlp_w2=mlp_w2)
    m = dict(ada_w=m_ada_w, ada_b=m_ada_b, ln_g=m_ln_g, ln_b=m_ln_b, ab_w_in=m_ab_w_in, rw_mu=m_rw_mu,
             rw_w0=m_rw_w0, rw_w_up=m_rw_w_up, rw_a0=m_rw_a0, rw_a_up=m_rw_a_up, rw_g_up=m_rw_g_up,
             rw_k_k=m_rw_k_k, rw_k_a=m_rw_k_a, rw_r_k=m_rw_r_k, rw_lnx_g=m_rw_lnx_g, rw_lnx_b=m_rw_lnx_b,
             sc_conv_w=m_sc_conv_w, ab_w_out=m_ab_w_out, dil_w_qkv=m_dil_w_qkv, dil_w_out=m_dil_w_out,
             rel_bias=m_rel_bias, mlp_w1=m_mlp_w1, mlp_w2=m_mlp_w2)
    v = dict(ada_w=v_ada_w, ada_b=v_ada_b, ln_g=v_ln_g, ln_b=v_ln_b, ab_w_in=v_ab_w_in, rw_mu=v_rw_mu,
             rw_w0=v_rw_w0, rw_w_up=v_rw_w_up, rw_a0=v_rw_a0, rw_a_up=v_rw_a_up, rw_g_up=v_rw_g_up,
             rw_k_k=v_rw_k_k, rw_k_a=v_rw_k_a, rw_r_k=v_rw_r_k, rw_lnx_g=v_rw_lnx_g, rw_lnx_b=v_rw_lnx_b,
             sc_conv_w=v_sc_conv_w, ab_w_out=v_ab_w_out, dil_w_qkv=v_dil_w_qkv, dil_w_out=v_dil_w_out,
             rel_bias=v_rel_bias, mlp_w1=v_mlp_w1, mlp_w2=v_mlp_w2)
    kinds = dict(SHARDED)
    me = 4 * lax.axis_index("x") + 2 * lax.axis_index("y") + lax.axis_index("c")
    ncol = ada_w.shape[2]

    small = _all_gather(_pack([c] + [w[n] for n in GATHER_F32], F32, 8), "gather_small")
    parts = _unpack(small, [c.shape] + [w[n].shape for n in GATHER_F32], (NDEV,))
    c_all = parts[0].reshape(NDEV, D)
    P = {n: _from_chunks(t, kinds[n]) for n, t in zip(GATHER_F32, parts[1:])}
    P = {n: (t if n in ("ln_g", "ln_b") else t[0]) for n, t in P.items()}
    for n in REPLICATED[1:]:
        P[n] = w[n]
    def full(n, t):
        t = _from_chunks(t, kinds[n])
        return t if n in ("mlp_w1", "mlp_w2") else t[0]

    (first,) = _all_gather_many([ab_w_in.astype(BF16)], "gather_first_weight")
    W = {"ab_w_in": _ab_in_padded(first[:, 0])}

    ada_b_loc = lax.dynamic_slice(ada_b, (0, ncol * me), (2, ncol))
    mod_part = _ada_mod(c_all, ada_w, ada_b_loc)
    mod_all = _all_gather(mod_part.reshape(-1, 128), "gather_mod").reshape(NDEV, 2, NDEV, ncol)
    mod = lax.dynamic_index_in_dim(mod_all, me, axis=2, keepdims=False)
    mod = mod.transpose(1, 0, 2).reshape(2, 6 * D)

    behind = (mod[0, 0] * 0.0).astype(BF16)
    later = _exchange_start([w[n].astype(BF16) + (behind if n == LATER[0] else 0) for n in LATER], True,
                            "gather_later_weights_start")
    mod = mod + later[-1][0, 0]

    def later_weights(after):
        lands = _exchange_wait(later, True, after, "gather_later_weights_wait")
        return {n: full(n, t) for n, t in zip(LATER, lands)}

    sent = []

    def early_grads(G):
        sent.append(_exchange_start([_to_chunks(G[n], kinds[n]).astype(BF16) for n in LATER], False,
                                    "exchange_later_grads_start"))
        return sent[0][-1][0, 0]

    loss_part, grad_x, dmod, G, in_grad = _local_step(x[0], loss_target[0], mod, W, P, later_weights, early_grads)
    G["ada_b"] = dmod
    big_out = {}

    def update(n, contributions):
        cols = w[n].shape[-1]
        flat = lambda t: t.reshape(-1, cols)
        rows = flat(w[n]).shape[0]
        outs = _sum_adamw(contributions.reshape(-1, rows, cols), flat(w[n]), flat(m[n]), flat(v[n]),
                          f"sum_adamw_{n}", min(rows, 256))
        big_out[n] = [o.reshape(w[n].shape) for o in outs]

    rep_shapes = [w[n].shape for n in REPLICATED] + [(1,)]
    rep_all = _all_gather(_pack([G[n] for n in REPLICATED] + [loss_part], F32, 8), "gather_replicated_grads")
    names = [n for n, _ in SHARDED if n not in GATHER_BF16]
    shard_shapes = [w[n].shape for n in names]
    recv = _all_to_all(_pack8([_to_chunks(G[n], kinds[n]) for n in names], F32, 8), "exchange_small_grads")

    behind = (recv[0, 0, 0] * 0.0 + rep_all[0, 0, 0] * 0.0).astype(BF16)
    last = _exchange_start([in_grad().astype(BF16) + behind], False, "exchange_last_grad_start")

    zero = last[-1][0:1, 0]
    pk = lambda d: _pack([d[n] for n in REPLICATED] + [zero], F32, 8)
    rep_out = _sum_adamw(rep_all, pk(w), pk(m), pk(v), "sum_adamw_replicated", rep_all.shape[1])
    loss = _unpack(rep_out[0], rep_shapes)[-1][0]
    rep_out = [dict(zip(REPLICATED, _unpack(o, rep_shapes))) for o in rep_out]
    dmod_all = _unpack(rep_all, [(2, 6 * D)], (NDEV,))[0]
    dmod_loc = lax.dynamic_slice(dmod_all, (0, 0, ncol * me), (NDEV, 2, ncol)).transpose(1, 0, 2)
    ada_out = _ada_grad_adamw(c_all.T + zero, dmod_loc, ada_w, m_ada_w, v_ada_w)
    pk = lambda d: _pack([d[n] for n in names], F32, 8)
    sh_out = _sum_adamw(recv, pk(w), pk(m), pk(v), "sum_adamw_small", recv.shape[1])
    sh_out = [dict(zip(names, _unpack(o, shard_shapes))) for o in sh_out]
    for n, r in zip(LATER, _exchange_wait(sent[0], False, last[-1], "exchange_later_grads_wait")):
        update(n, r)
    (landed,) = _exchange_wait(last, False, big_out[LATER[-1]][0], "exchange_last_grad_wait")
    update("ab_w_in", landed)
    sh_out = [{**d, **{n: big_out[n][i] for n in GATHER_BF16}} for i, d in enumerate(sh_out)]

    def pick(i, n):
        if n == "ada_w":
            return ada_out[i]
        return rep_out[i][n] if n in REPLICATED else sh_out[i][n]

    outs = [loss, grad_x[None]]
    for i in range(4):
        outs += [pick(i, n) for n in WEIGHTS]
    return tuple(outs)
```

```python
import functools
import math

import jax
import jax.numpy as jnp
from jax import lax
from jax.experimental import pallas as pl
from jax.experimental.pallas import tpu as pltpu

F32 = jnp.float32
BF16 = jnp.bfloat16
HI = lax.Precision.HIGHEST

NDEV = 8
T = 2048
D = 1024
DFF = 4096
HEADS = 8
HD = 64
RW = 512
PA = 2048
PB = 1536
PAB = PA + PB
QKV = 4608
DILS = (1, 4, 16)
BLK = 128
ALPHA = 4.0 ** 0.25
LN_EPS = 1e-5
GN_EPS = 64e-5
ADAM_LR, ADAM_B1, ADAM_B2, ADAM_EPS, ADAM_WD, ADAM_STEP = 0.001, 0.9, 0.999, 1e-8, 0.01, 10
VMEM_LIMIT = 56 * 1024 * 1024


def _cp(sem):
    return pltpu.CompilerParams(dimension_semantics=sem, vmem_limit_bytes=VMEM_LIMIT)


def _slot(px, py, pc):
    return 4 * px + 2 * py + pc


def _all_gather(x, name):
    R, C = x.shape

    def body(x_ref, out_ref, send_sems, recv_sems, local_sem):
        xi, yi, ci = lax.axis_index("x"), lax.axis_index("y"), lax.axis_index("c")
        me, sibling = (xi, yi, ci), (xi, yi, 1 - ci)
        chips = [(1 - xi, yi), (xi, 1 - yi), (1 - xi, 1 - yi)]

        def rows(px, py, pc):
            return out_ref.at[_slot(px, py, pc)]

        def copy(k, block, to, src=None):
            return pltpu.make_async_remote_copy(
                src_ref=rows(*block) if src is None else src, dst_ref=rows(*block),
                send_sem=send_sems.at[k], recv_sem=recv_sems.at[k],
                device_id=to, device_id_type=pl.DeviceIdType.MESH)

        mine = pltpu.make_async_copy(x_ref, rows(*me), local_sem)
        mine.start()
        first = [copy(0, me, sibling, src=x_ref)]
        first += [copy(1 + j, me, (*chip, ci), src=x_ref) for j, chip in enumerate(chips)]
        for cp in first:
            cp.start()
        passed = [copy(4 + j, (*chip, ci), sibling) for j, chip in enumerate(chips)]
        for j, chip in enumerate(chips):
            copy(1 + j, (*chip, ci), me).wait_recv()
            passed[j].start()
        copy(0, sibling, me).wait_recv()
        for j, chip in enumerate(chips):
            copy(4 + j, (*chip, 1 - ci), me).wait_recv()
        for cp in first + passed:
            cp.wait_send()
        mine.wait()

    return pl.pallas_call(
        body, name=name,
        out_shape=jax.ShapeDtypeStruct((NDEV, R, C), x.dtype),
        in_specs=[pl.BlockSpec(memory_space=pl.ANY)],
        out_specs=pl.BlockSpec(memory_space=pl.ANY),
        scratch_shapes=[pltpu.SemaphoreType.DMA((7,)), pltpu.SemaphoreType.DMA((7,)),
                        pltpu.SemaphoreType.DMA(())],
    )(x)


def _all_to_all(g, name):
    _, R, C = g.shape

    def body(g_ref, out_ref, send_sems, recv_sems, local_sem):
        xi, yi, ci = lax.axis_index("x"), lax.axis_index("y"), lax.axis_index("c")
        my_slot = _slot(xi, yi, ci)
        mine = pltpu.make_async_copy(g_ref.at[my_slot], out_ref.at[my_slot], local_sem)
        mine.start()
        copies = []
        for k in range(1, 8):
            px = 1 - xi if k & 4 else xi
            py = 1 - yi if k & 2 else yi
            pc = 1 - ci if k & 1 else ci
            peer_slot = _slot(px, py, pc)
            copies.append((
                pltpu.make_async_remote_copy(
                    src_ref=g_ref.at[peer_slot], dst_ref=out_ref.at[my_slot],
                    send_sem=send_sems.at[k - 1], recv_sem=recv_sems.at[k - 1],
                    device_id=(px, py, pc), device_id_type=pl.DeviceIdType.MESH),
                pltpu.make_async_remote_copy(
                    src_ref=g_ref.at[peer_slot], dst_ref=out_ref.at[peer_slot],
                    send_sem=send_sems.at[k - 1], recv_sem=recv_sems.at[k - 1],
                    device_id=(px, py, pc), device_id_type=pl.DeviceIdType.MESH)))
        for send, _ in copies:
            send.start()
        for _, recv in copies:
            recv.wait_recv()
        for send, _ in copies:
            send.wait_send()
        mine.wait()

    return pl.pallas_call(
        body, name=name,
        out_shape=jax.ShapeDtypeStruct((NDEV, R, C), g.dtype),
        in_specs=[pl.BlockSpec(memory_space=pl.ANY)],
        out_specs=pl.BlockSpec(memory_space=pl.ANY),
        scratch_shapes=[pltpu.SemaphoreType.DMA((7,)), pltpu.SemaphoreType.DMA((7,)),
                        pltpu.SemaphoreType.DMA(())],
    )(g)


def _my_slot():
    return _slot(lax.axis_index("x"), lax.axis_index("y"), lax.axis_index("c"))


def _put_own(buf, own, slot):
    return lax.dynamic_update_index_in_dim(buf, own, slot, 0)


def _hbm_call(body, name, ins, out_shapes, n_sems):
    anyspec = pl.BlockSpec(memory_space=pl.ANY)
    return pl.pallas_call(
        body, name=name, out_shape=out_shapes,
        in_specs=[anyspec] * len(ins), out_specs=[anyspec] * len(out_shapes),
        scratch_shapes=[pltpu.SemaphoreType.DMA(s) for s in n_sems],
    )(*ins)


def _all_gather_many(xs, name):
    n = len(xs)

    def body(*refs):
        x_refs, o_refs = refs[:n], refs[n:2 * n]
        send_sems, recv_sems = refs[2 * n:]
        xi, yi, ci = lax.axis_index("x"), lax.axis_index("y"), lax.axis_index("c")
        me, sibling = (xi, yi, ci), (xi, yi, 1 - ci)
        chips = [(1 - xi, yi), (xi, 1 - yi), (1 - xi, 1 - yi)]

        def copy(i, k, block, to, src=None):
            dst = o_refs[i].at[_slot(*block)]
            return pltpu.make_async_remote_copy(
                src_ref=dst if src is None else src, dst_ref=dst,
                send_sem=send_sems.at[i, k], recv_sem=recv_sems.at[i, k],
                device_id=to, device_id_type=pl.DeviceIdType.MESH)

        sends = []
        for i in range(n):
            sends += [copy(i, 1 + j, me, (*chip, ci), src=x_refs[i]) for j, chip in enumerate(chips)]
            sends.append(copy(i, 0, me, sibling, src=x_refs[i]))
        for cp in sends:
            cp.start()
        for j, chip in enumerate(chips):
            for i in range(n):
                copy(i, 1 + j, (*chip, ci), me).wait_recv()
                passed = copy(i, 4 + j, (*chip, ci), sibling)
                passed.start()
                sends.append(passed)
        for i in range(n):
            copy(i, 0, sibling, me).wait_recv()
            for j, chip in enumerate(chips):
                copy(i, 4 + j, (*chip, 1 - ci), me).wait_recv()
        for cp in sends:
            cp.wait_send()

    outs = _hbm_call(body, name, xs, [jax.ShapeDtypeStruct((NDEV,) + x.shape, x.dtype) for x in xs],
                     [(n, 7), (n, 7)])
    return [_put_own(o, x[None], _my_slot()) for o, x in zip(outs, xs)]


def _peers(xi, yi, ci):
    return [(1 - xi if k & 4 else xi, 1 - yi if k & 2 else yi, 1 - ci if k & 1 else ci) for k in range(1, 8)]


def _direct_copy(src_refs, land_refs, send_sems, recv_sems, i, k, peer, my_slot, gather):
    src = src_refs[i] if gather else src_refs[i].at[_slot(*peer)]
    return pltpu.make_async_remote_copy(
        src_ref=src, dst_ref=land_refs[i].at[my_slot], send_sem=send_sems.at[7 * i + k], recv_sem=recv_sems.at[7 * i + k],
        device_id=peer, device_id_type=pl.DeviceIdType.MESH)


def _exchange_start(srcs, gather, name):
    n = len(srcs)
    lands = [lax.empty(((NDEV,) + s.shape) if gather else s.shape, s.dtype) for s in srcs]

    def body(*refs):
        s_refs, l_refs = refs[:n], refs[n:2 * n]
        send_sems, recv_sems = refs[2 * n], refs[2 * n + 1]
        token = refs[2 * n + 2 + 2 * n]
        xi, yi, ci = lax.axis_index("x"), lax.axis_index("y"), lax.axis_index("c")
        my_slot = _slot(xi, yi, ci)
        for k, peer in enumerate(_peers(xi, yi, ci)):
            for i in range(n):
                _direct_copy(s_refs, l_refs, send_sems, recv_sems, i, k, peer, my_slot, gather).start()
        token[...] = jnp.zeros_like(token)

    hbm = pl.BlockSpec(memory_space=pltpu.HBM)
    sem = pl.BlockSpec(memory_space=pltpu.SEMAPHORE)
    both = list(srcs) + lands
    return pl.pallas_call(
        body, name=name,
        out_shape=(pltpu.SemaphoreType.DMA((7 * n,)), pltpu.SemaphoreType.DMA((7 * n,)),
                   *[pltpu.HBM(t.shape, t.dtype) for t in both], jax.ShapeDtypeStruct((8, 128), F32)),
        in_specs=[hbm] * (2 * n),
        out_specs=(sem, sem, *[hbm] * (2 * n), pl.BlockSpec(memory_space=pltpu.VMEM)),
        input_output_aliases={i: 2 + i for i in range(2 * n)},
        compiler_params=pltpu.CompilerParams(has_side_effects=pltpu.SideEffectType.DATAFLOW_SIDE_EFFECTING),
    )(*[pltpu.with_memory_space_constraint(t, pltpu.HBM) for t in both])


def _exchange_wait(started, gather, after, name):
    send_sems, recv_sems, *thru, _ = started
    n = len(thru) // 2

    def body(*refs):
        s_refs, l_refs = refs[:n], refs[n:2 * n]
        send_sems, recv_sems = refs[2 * n], refs[2 * n + 1]
        xi, yi, ci = lax.axis_index("x"), lax.axis_index("y"), lax.axis_index("c")
        my_slot = _slot(xi, yi, ci)
        for k, peer in enumerate(_peers(xi, yi, ci)):
            for i in range(n):
                _direct_copy(s_refs, l_refs, send_sems, recv_sems, i, k, peer, my_slot, gather).wait_send()
                _direct_copy(s_refs, l_refs, send_sems, recv_sems, i, k, peer, _slot(*peer), gather).wait_recv()

    hbm = pl.BlockSpec(memory_space=pltpu.HBM)
    sem = pl.BlockSpec(memory_space=pltpu.SEMAPHORE)
    outs = pl.pallas_call(
        body, name=name,
        out_shape=tuple(pltpu.HBM(t.shape, t.dtype) for t in thru),
        in_specs=[hbm] * (2 * n) + [sem, sem, pl.BlockSpec(memory_space=pl.ANY)],
        out_specs=tuple([hbm] * (2 * n)),
        input_output_aliases={i: i for i in range(2 * n)},
        compiler_params=pltpu.CompilerParams(has_side_effects=pltpu.SideEffectType.DATAFLOW_SIDE_EFFECTING),
    )(*thru, send_sems, recv_sems, after)
    slot = _my_slot()
    own = [s[None] if gather else lax.dynamic_index_in_dim(s, slot, 0, keepdims=True) for s in outs[:n]]
    return [_put_own(land, o, slot) for land, o in zip(outs[n:], own)]


def _mm(name, a, b, tb=False, out=(F32,), epi=None, extras=(), tm=2048, tn=512, tk_cap=2048):
    M, K = a.shape
    N = b.shape[0] if tb else b.shape[1]
    tm, tn = min(tm, M), min(tn, N)
    tk = max(t for t in range(128, min(K, tk_cap) + 1, 128) if K % t == 0)
    assert M % tm == 0 and N % tn == 0 and K % tk == 0, (name, M, N, K)
    nk = K // tk
    if nk == 1 and N % 256 == 0:
        tn = min(tn, 256)
    ne, no = len(extras), len(out)
    dims = (((1,), (1 if tb else 0,)), ((), ()))
    flipped = [isinstance(o, tuple) for o in out]

    def kern(*refs):
        a_ref, b_ref = refs[:2]
        e_refs = refs[2:2 + ne]
        o_refs = refs[2 + ne:2 + ne + no]

        def finish(acc):
            outs = epi(acc, *[e[...] for e in e_refs]) if epi is not None else (acc,)
            for o_ref, o, flip in zip(o_refs, outs, flipped):
                o_ref[...] = (o.T if flip else o).astype(o_ref.dtype)

        part = lax.dot_general(a_ref[...], b_ref[...], dims, preferred_element_type=F32)
        if nk == 1:
            finish(part)
            return
        acc_ref = refs[-1]
        k = pl.program_id(2)

        @pl.when(k == 0)
        def _():
            acc_ref[...] = part

        @pl.when(k > 0)
        def _():
            acc_ref[...] += part

        @pl.when(k == nk - 1)
        def _():
            finish(acc_ref[...])

    piped = nk == 1 and (M // tm) * (N // tn) >= 3
    deep = dict(pipeline_mode=pl.Buffered(3)) if piped else {}
    b_spec = (pl.BlockSpec((tn, tk), lambda i, j, k: (j, k), **deep) if tb
              else pl.BlockSpec((tk, tn), lambda i, j, k: (k, j), **deep))
    tile = pl.BlockSpec((tm, tn), lambda i, j, k: (i, j))
    tile_t = pl.BlockSpec((tn, tm), lambda i, j, k: (j, i))
    if piped:
        in_specs = [pl.BlockSpec((tm, tk), lambda i, j, k: (i, k)), b_spec] + [tile] * ne
        out_specs = [tile_t if flip else tile for flip in flipped]

        def outer(*refs):
            pltpu.emit_pipeline(kern, grid=(M // tm, N // tn, 1), in_specs=in_specs,
                                out_specs=out_specs)(*refs)

        res = pl.pallas_call(
            outer, name=name,
            in_specs=[pl.BlockSpec(memory_space=pl.ANY)] * (2 + ne),
            out_specs=[pl.BlockSpec(memory_space=pl.ANY)] * no,
            out_shape=[jax.ShapeDtypeStruct((N, M), o[0]) if flip else jax.ShapeDtypeStruct((M, N), o)
                       for o, flip in zip(out, flipped)],
            compiler_params=_cp(None),
        )(a, b, *extras)
        return res[0] if no == 1 else res
    res = pl.pallas_call(
        kern, name=name, grid=(M // tm, N // tn, nk),
        in_specs=[pl.BlockSpec((tm, tk), lambda i, j, k: (i, k)), b_spec] + [tile] * ne,
        out_specs=[tile_t if flip else tile for flip in flipped],
        out_shape=[jax.ShapeDtypeStruct((N, M), o[0]) if flip else jax.ShapeDtypeStruct((M, N), o)
                   for o, flip in zip(out, flipped)],
        scratch_shapes=[pltpu.VMEM((tm, tn), F32)] if nk > 1 else [],
        compiler_params=_cp(("parallel", "parallel", "arbitrary")),
    )(a, b, *extras)
    return res[0] if no == 1 else res


HALO = 8


def _rows(name, body, rows, params, out_rows, out_accs=(), tm=512):
    views = [r if isinstance(r, tuple) else (r, r.shape[1], 0) for r in rows]
    n = views[0][0].shape[0]
    assert n % tm == 0 and tm % HALO == 0
    nr, npar, nor, noa = len(views), len(params), len(out_rows), len(out_accs)
    kinds = [o[2] if len(o) == 3 else None for o in out_rows]
    store = {None: lambda o: o, "T": lambda o: o.T, "cols3": _cols3_tile}
    spec = {None: lambda w: pl.BlockSpec((tm, w), lambda i: (i, 0)),
            "T": lambda w: pl.BlockSpec((w, tm), lambda i: (0, i)),
            "cols3": lambda w: pl.BlockSpec((HP, HD, 6 * tm), lambda i: (0, 0, i))}
    shape = {None: lambda w: (n, w), "T": lambda w: (w, n), "cols3": lambda w: (HP, HD, 6 * n)}

    def row_spec(width, cb, halo=None):
        per, last = tm // HALO, n // HALO - 1
        if halo == "prev":
            return pl.BlockSpec((HALO, width), lambda i: (jnp.maximum(i * per - 1, 0), cb))
        if halo == "next":
            return pl.BlockSpec((HALO, width), lambda i: (jnp.minimum((i + 1) * per, last), cb))
        return pl.BlockSpec((tm, width), lambda i: (i, cb))

    def kern(*refs):
        r_refs = refs[:nr]
        p_refs = refs[nr:nr + npar]
        o_refs = refs[nr + npar:nr + npar + nor]
        a_refs = refs[nr + npar + nor:]
        outs, accs = body([r[...] for r in r_refs], [p[...] for p in p_refs])
        assert len(outs) == nor and len(accs) == noa, (name, len(outs), len(accs))
        for o_ref, o, kind in zip(o_refs, outs, kinds):
            o_ref[...] = store[kind](o).astype(o_ref.dtype)
        if noa:
            @pl.when(pl.program_id(0) == 0)
            def _():
                for a_ref in a_refs:
                    a_ref[...] = jnp.zeros_like(a_ref)

            for a_ref, a in zip(a_refs, accs):
                a_ref[...] += a.astype(F32)

    def whole(shape):
        nd = len(shape)
        return pl.BlockSpec(tuple(shape), lambda i, nd=nd: (0,) * nd)

    in_specs = [row_spec(*v[1:]) for v in views]
    in_specs += [whole(p.shape) for p in params]
    out_specs = [spec[kind](o[0]) for o, kind in zip(out_rows, kinds)]
    out_specs += [whole(s) for s in out_accs]
    out_shape = [jax.ShapeDtypeStruct(shape[kind](o[0]), o[1]) for o, kind in zip(out_rows, kinds)]
    out_shape += [jax.ShapeDtypeStruct(tuple(s), F32) for s in out_accs]
    res = pl.pallas_call(
        kern, name=name, grid=(n // tm,), in_specs=in_specs, out_specs=out_specs,
        out_shape=out_shape, compiler_params=_cp(("arbitrary",)),
    )(*[v[0] for v in views], *params)
    return res[:nor], res[nor:]


def _shift_down(x, prev, k):
    head = jnp.where(pl.program_id(0) == 0, 0.0, pltpu.roll(prev, k, axis=0))
    row = lax.broadcasted_iota(jnp.int32, x.shape, 0)
    return jnp.where(row < k, jnp.tile(head, (x.shape[0] // HALO, 1)), pltpu.roll(x, k, axis=0))


def _shift_up(x, nxt, k):
    n = x.shape[0]
    tail = jnp.where(pl.program_id(0) == pl.num_programs(0) - 1, 0.0, pltpu.roll(nxt, HALO - k, axis=0))
    row = lax.broadcasted_iota(jnp.int32, x.shape, 0)
    return jnp.where(row >= n - k, jnp.tile(tail, (n // HALO, 1)), pltpu.roll(x, n - k, axis=0))


@jax.custom_vjp
def _headsum(x, e):
    return sum(jnp.dot(p, e, preferred_element_type=F32) for p in _split3(x))


_headsum.defvjp(lambda x, e: (_headsum(x, e), e), lambda e, ct: (_headsum(ct, e), None))


def _softplus(z):
    return jnp.maximum(z, 0.0) + jnp.log(1.0 + jnp.exp(jnp.minimum(z, -z)))


def _post_ln(x, y, g, lng, lnb):
    z = ALPHA * x + (1.0 + g) * y
    mu = jnp.mean(z, axis=-1, keepdims=True)
    zc = z - mu
    var = jnp.mean(zc * zc, axis=-1, keepdims=True)
    return zc * lax.rsqrt(var + LN_EPS) * lng + lnb


def _post_ln_mod(x, y, g, lng, lnb, scn, shn):
    xn = _post_ln(x, y, g, lng, lnb)
    return xn, xn * (1.0 + scn) + shn


def _pre_core(E, r_, k_, v_, wd_, ad_, gd_, r1, k1, v1, wd1, ad1, gd1, h, bg, cg, h1, cg1, h2, cg2,
              mu_r, mu_k, mu_v, mu_wd, mu_ad, mu_gd, w0, w_up, a0, a_up, g_up, k_k, k_a,
              cw0, cw1, cw2):
    def mix(x, x1, mu):
        return x + mu * (x1 - x)

    r, k, v = mix(r_, r1, mu_r), mix(k_, k1, mu_k), mix(v_, v1, mu_v)
    wd, ad, gd = mix(wd_, wd1, mu_wd), mix(ad_, ad1, mu_ad), mix(gd_, gd1, mu_gd)
    logw = -_softplus(-(w0 + jnp.dot(jnp.tanh(wd), w_up, preferred_element_type=F32))) - 0.5
    decay = jnp.exp(-jnp.exp(logw))
    iclr = jax.nn.sigmoid(a0 + jnp.dot(ad, a_up, preferred_element_type=F32))
    gate = jnp.dot(jax.nn.sigmoid(gd), g_up, preferred_element_type=F32)
    kk0 = k * k_k
    nrm = jnp.sqrt(_headsum(kk0 * kk0, E))
    kk = kk0 / jnp.maximum(nrm, 1e-12)
    kh = k * (1.0 + (iclr - 1.0) * k_a)
    yb = bg * (cw2 * (cg * h) + cw1 * (cg1 * h1) + cw0 * (cg2 * h2))
    return r, decay, kh, v, -kk, kk * iclr, gate, yb


def _post_core(E, y, r, kh, v, gate, lnx_g, lnx_b, rk):
    def seg(t):
        return _headsum(t, E)

    mean = seg(y) * (1.0 / HD)
    yc = y - mean
    var = seg(yc * yc) * (1.0 / HD)
    gn = yc * lax.rsqrt(var + GN_EPS) * lnx_g + lnx_b
    bonus = seg(r * kh * rk) * v
    return (gn + bonus) * gate


def _merge_core(o0, o1, o2, l0, l1, l2):
    m = jnp.maximum(jnp.maximum(l0, l1), l2)
    e0, e1, e2 = jnp.exp(l0 - m), jnp.exp(l1 - m), jnp.exp(l2 - m)
    den = e0 + e1 + e2
    return (e0 * o0 + e1 * o1 + e2 * o2) / den


CHUNK = 128
HALF = 64
HP = HEADS // 2
LW = 2 * HD
NCHUNK = T // CHUNK


def _split3(x):
    hi = x.astype(BF16)
    r1 = x - hi.astype(F32)
    mid = r1.astype(BF16)
    return hi, mid, (r1 - mid.astype(F32)).astype(BF16)


def _cols3_tile(x):
    left = lax.broadcasted_iota(jnp.int32, (HD, CHUNK), 1) < HALF
    xts = [x[c * CHUNK:(c + 1) * CHUNK].T for c in range(x.shape[0] // CHUNK)]
    pairs = []
    for p in range(HP):
        groups = []
        for xt in xts:
            a, b = xt[p * LW:p * LW + HD], xt[p * LW + HD:(p + 1) * LW]
            for half in (jnp.where(left, a, pltpu.roll(b, HALF, axis=1)), jnp.where(left, pltpu.roll(a, HALF, axis=1), b)):
                groups += list(_split3(half))
        pairs.append(jnp.concatenate(groups, axis=1))
    return jnp.stack(pairs)


def _pick_codes():
    row = lax.broadcasted_iota(jnp.int32, (6 * HALF, LW), 0)
    col = lax.broadcasted_iota(jnp.int32, (6 * HALF, LW), 1)
    same = ((row & (LW - 1)) >= HALF) == (col >= HD)
    return jnp.where(same, row & (HALF - 1), -1).astype(BF16)


def _column(block_ref, codes, half, i):
    pick = jnp.where(codes == i.astype(BF16), jnp.ones((), BF16), jnp.zeros((), BF16))
    block = block_ref[:, :, half * 6 * HALF:(half + 1) * 6 * HALF].reshape(HP * HD, 6 * HALF)
    return jnp.dot(block, pick, preferred_element_type=F32)


def _halfsums(x, row, left1):
    row_l = jnp.where(left1, row, 0.0)
    return (jnp.sum(x * row_l, axis=1, keepdims=True), jnp.sum(x * (row - row_l), axis=1, keepdims=True))


def _pair_rows(row):
    return [row[:, p * LW:(p + 1) * LW] for p in range(HP)]


def _store_columns(ref, p, t_mask, cols):
    for j, col in enumerate(cols):
        pltpu.store(ref.at[pl.ds(2 * p + j, 1)], jnp.broadcast_to(col[None], (1, HD, CHUNK)), mask=t_mask[None])


def _columns_to_rows(cols_ref, rows_ref):
    for p in range(HP):
        rows_ref[:, p * LW:(p + 1) * LW] = cols_ref[2 * p:2 * p + 2].reshape(LW, CHUNK).T


NHALF = T // HALF
HALVES = CHUNK // HALF


def _scan_fwd(r, w, k, a, b, v3):
    def kern(r_ref, w_ref, k_ref, a_ref, b_ref, v_ref, y_ref, ck_ref, st_hbm, sa_hbm,
             s_ref, vb_ref, yc_ref, st_ref, sa_ref, sems):
        c = pl.program_id(0)

        @pl.when(c == 0)
        def _():
            s_ref[...] = jnp.zeros_like(s_ref)

        lane = lax.broadcasted_iota(jnp.int32, (HD, CHUNK), 1)
        left = lane < HD
        left1 = lax.broadcasted_iota(jnp.int32, (1, LW), 1) < HD
        codes = _pick_codes()

        def flush(slot, half_index):
            return [pltpu.make_async_copy(src.at[slot], dst.at[half_index], sems.at[j, slot])
                    for j, (src, dst) in enumerate(((st_ref, st_hbm), (sa_ref, sa_hbm)))]

        for half in range(HALVES):
            ck_ref[half] = s_ref[...]
            vb_ref[...] = _column(v_ref, codes, half, jnp.int32(0))

            @pl.when(c > 0)
            def _():
                for cp in flush(half, (c - 1) * HALVES + half):
                    cp.wait()

            def step(i, carry):
                t = half * HALF + i
                row = lambda ref: _pair_rows(ref[pl.ds(t, 1), :])
                S = [s_ref[p] for p in range(HP)]
                sa = [jnp.where(left, *_halfsums(s, a, left1)) for s, a in zip(S, row(a_ref))]
                S = [s * w + c_ * b + vb_ref[pl.ds(p * HD, HD), :] * k
                     for p, (s, w, c_, b, k) in enumerate(zip(S, row(w_ref), sa, row(b_ref), row(k_ref)))]
                for p, (s, c_) in enumerate(zip(S, sa)):
                    s_ref[p] = s
                    st_ref[half, i, p] = s
                    sa_ref[half, i, p] = c_
                for p, (s, r) in enumerate(zip(S, row(r_ref))):
                    _store_columns(yc_ref, p, lane == t, _halfsums(s, r, left1))
                vb_ref[...] = _column(v_ref, codes, half, i + 1)
                return carry

            lax.fori_loop(0, HALF, step, 0, unroll=16)
            for cp in flush(half, c * HALVES + half):
                cp.start()
        _columns_to_rows(yc_ref, y_ref)

        @pl.when(c == NCHUNK - 1)
        def _():
            for half in range(HALVES):
                for cp in flush(half, c * HALVES + half):
                    cp.wait()

    rowblk = pl.BlockSpec((CHUNK, RW), lambda c: (c, 0))
    saved = jax.ShapeDtypeStruct((NHALF, HALF, HP, HD, LW), F32)
    stage = pltpu.VMEM((HALVES, HALF, HP, HD, LW), F32)
    return pl.pallas_call(
        kern, name="rwkv_scan_fwd", grid=(NCHUNK,),
        in_specs=[rowblk] * 5 + [pl.BlockSpec((HP, HD, 6 * CHUNK), lambda c: (0, 0, c))],
        out_specs=[rowblk, pl.BlockSpec((HALVES, HP, HD, LW), lambda c: (c, 0, 0, 0)),
                   pl.BlockSpec(memory_space=pl.ANY), pl.BlockSpec(memory_space=pl.ANY)],
        out_shape=[jax.ShapeDtypeStruct((T, RW), F32), jax.ShapeDtypeStruct((NHALF, HP, HD, LW), F32), saved, saved],
        scratch_shapes=[pltpu.VMEM((HP, HD, LW), F32), pltpu.VMEM((HP * HD, LW), F32),
                        pltpu.VMEM((HEADS, HD, CHUNK), F32), stage, stage, pltpu.SemaphoreType.DMA((2, HALVES))],
        compiler_params=_cp(("arbitrary",)),
    )(r, w, k, a, b, v3)


def _scan_bwd(r, w, k, a, b, v3, dy3, ck, st, sa):
    def kern(r_ref, w_ref, k_ref, a_ref, b_ref, v_ref, dy_ref, ck_ref, st_hbm, sa_hbm,
             dr_ref, dw_ref, dk_ref, da_ref, db_ref, dv_ref, ds_ref, sb_ref, sa_ref, pick_ref, dvc_ref, sems):
        c = pl.program_id(0)
        chunk = NCHUNK - 1 - c

        @pl.when(c == 0)
        def _():
            ds_ref[...] = jnp.zeros_like(ds_ref)

        lane = lax.broadcasted_iota(jnp.int32, (HD, CHUNK), 1)
        left = lane < HD
        left1 = lax.broadcasted_iota(jnp.int32, (1, LW), 1) < HD
        codes = _pick_codes()

        def rowsum(x):
            return jnp.sum(x, axis=0, keepdims=True)

        def fetch(slot, half_index):
            return [pltpu.make_async_copy(st_hbm.at[half_index], sb_ref.at[slot, pl.ds(1, HALF)], sems.at[0, slot]),
                    pltpu.make_async_copy(sa_hbm.at[half_index], sa_ref.at[slot], sems.at[1, slot])]

        def picks(half, i):
            pick_ref[pl.ds(0, HP * HD), :] = _column(v_ref, codes, half, i)
            pick_ref[pl.ds(HP * HD, HP * HD), :] = _column(dy_ref, codes, half, i)

        @pl.when(c == 0)
        def _():
            for cp in fetch(HALVES - 1, chunk * HALVES + HALVES - 1):
                cp.start()

        for half in reversed(range(HALVES)):
            base = half * HALF
            for cp in fetch(half, chunk * HALVES + half):
                cp.wait()
            if half:
                for cp in fetch(half - 1, chunk * HALVES + half - 1):
                    cp.start()
            else:
                @pl.when(chunk > 0)
                def _():
                    for cp in fetch(HALVES - 1, chunk * HALVES - 1):
                        cp.start()
            sb_ref[half, 0] = ck_ref[half]
            picks(half, jnp.int32(HALF - 1))

            def back(ii, carry):
                i = HALF - 1 - ii
                t = base + i
                row = lambda ref: _pair_rows(ref[pl.ds(t, 1), :])
                a_r, b_r, k_r, w_r, r_r = row(a_ref), row(b_ref), row(k_ref), row(w_ref), row(r_ref)
                vs = [pick_ref[pl.ds(p * HD, HD), :] for p in range(HP)]
                dys = [pick_ref[pl.ds((HP + p) * HD, HD), :] for p in range(HP)]
                picks(half, jnp.maximum(i - 1, 0))
                dr, dw, db, dk, da = [], [], [], [], []
                for p in range(HP):
                    Sp, dy = sb_ref[half, i, p], dys[p]
                    dS = ds_ref[p] + dy * r_r[p]
                    dr.append(rowsum(sb_ref[half, i + 1, p] * dy))
                    dw.append(rowsum(dS * Sp))
                    db.append(rowsum(dS * sa_ref[half, i, p]))
                    dk.append(rowsum(dS * vs[p]))
                    dsa = jnp.where(left, *_halfsums(dS, b_r[p], left1))
                    _store_columns(dvc_ref, p, lane == t, _halfsums(dS, k_r[p], left1))
                    da.append(rowsum(Sp * dsa))
                    ds_ref[p] = dS * w_r[p] + dsa * a_r[p]
                for ref, pieces in ((dr_ref, dr), (dw_ref, dw), (db_ref, db), (dk_ref, dk), (da_ref, da)):
                    ref[pl.ds(t, 1), :] = jnp.concatenate(pieces, axis=1)
                return carry

            lax.fori_loop(0, HALF, back, 0, unroll=16)
        _columns_to_rows(dvc_ref, dv_ref)

    rowblk = pl.BlockSpec((CHUNK, RW), lambda c: (NCHUNK - 1 - c, 0))
    col3blk = pl.BlockSpec((HP, HD, 6 * CHUNK), lambda c: (0, 0, NCHUNK - 1 - c))
    rowshape = jax.ShapeDtypeStruct((T, RW), F32)
    return pl.pallas_call(
        kern, name="rwkv_scan_bwd", grid=(NCHUNK,),
        in_specs=[rowblk] * 5 + [col3blk, col3blk,
                                 pl.BlockSpec((HALVES, HP, HD, LW), lambda c: (NCHUNK - 1 - c, 0, 0, 0)),
                                 pl.BlockSpec(memory_space=pl.ANY), pl.BlockSpec(memory_space=pl.ANY)],
        out_specs=[rowblk] * 6, out_shape=[rowshape] * 6,
        scratch_shapes=[pltpu.VMEM((HP, HD, LW), F32), pltpu.VMEM((HALVES, HALF + 1, HP, HD, LW), F32),
                        pltpu.VMEM((HALVES, HALF, HP, HD, LW), F32), pltpu.VMEM((2 * HP * HD, LW), F32),
                        pltpu.VMEM((HEADS, HD, CHUNK), F32), pltpu.SemaphoreType.DMA((2, HALVES))],
        compiler_params=_cp(("arbitrary",)),
    )(r, w, k, a, b, v3, dy3, ck, st, sa)


NT = (((1,), (1,)), ((), ()))
TN = (((0,), (0,)), ((), ()))
SCALE = HD ** -0.5
QKV_G = 3 * RW


def _attn_setup(g):
    dil = DILS[g]
    qkv = [pl.BlockSpec((T, LW), lambda hp, c=(g * QKV_G + s * RW) // LW: (0, c + hp)) for s in range(3)]
    tile = pl.BlockSpec((T, LW), lambda hp: (0, hp))
    bias = pl.BlockSpec((2, BLK, 2 * BLK), lambda hp: (hp, 0, 0))

    def blocks():
        for r in range(dil):
            for n in range(T // dil // BLK):
                rows = pl.ds(n * BLK * dil + r, BLK, stride=dil)
                keys = pl.ds((n - 1) * BLK * dil + r, 2 * BLK, stride=dil) if n else rows
                yield n, rows, keys

    return qkv, tile, bias, blocks


def _band(n):
    qi = lax.broadcasted_iota(jnp.int32, (BLK, 2 * BLK), 0)
    ki = lax.broadcasted_iota(jnp.int32, (BLK, 2 * BLK), 1)
    band = (ki >= qi) & (ki <= qi + BLK)
    return band if n else band[:, BLK:]


def _head_masks():
    lane = lax.broadcasted_iota(jnp.int32, (BLK, LW), 1)
    return lane < HD, [(lane < HD).astype(BF16), (lane >= HD).astype(BF16)]


def _attn_fwd(pq, bias, g):
    qkv, tile, bias_spec, blocks = _attn_setup(g)

    def kern(q_ref, k_ref, v_ref, b_ref, o_ref, l_ref):
        left, masks = _head_masks()
        for n, rows, keys in blocks():
            qb, kc, vc = q_ref[rows, :].astype(BF16), k_ref[keys, :].astype(BF16), v_ref[keys, :].astype(BF16)
            valid = _band(n)
            o, lse = [], []
            for j in range(2):
                bias_j = b_ref[j] if n else b_ref[j][:, BLK:]
                s = lax.dot_general(qb * masks[j], kc, NT, preferred_element_type=F32) * SCALE + bias_j
                s = jnp.where(valid, s, -jnp.inf)
                m = jnp.max(s, axis=1, keepdims=True)
                e = jnp.exp(s - m)
                den = jnp.sum(e, axis=1, keepdims=True)
                o.append(jnp.dot((e / den).astype(BF16), vc, preferred_element_type=F32))
                lse.append(m + jnp.log(den))
            o_ref[rows, :] = jnp.where(left, o[0], o[1])
            l_ref[rows, :] = jnp.where(left, lse[0], lse[1])

    shape = jax.ShapeDtypeStruct((T, RW), F32)
    return pl.pallas_call(
        kern, name=f"attn_fwd_{g}", grid=(HP,),
        in_specs=qkv + [bias_spec], out_specs=[tile, tile], out_shape=[shape, shape],
        compiler_params=_cp(("parallel",)),
    )(pq, pq, pq, bias)


def _attn_bwd(pq, bias, do, o, lse, dlse, g):
    qkv, tile, bias_spec, blocks = _attn_setup(g)

    def kern(q_ref, k_ref, v_ref, b_ref, do_ref, o_ref, l_ref, dl_ref, dq_ref, dk_ref, dv_ref, db_ref):
        left, masks = _head_masks()
        lane = lax.broadcasted_iota(jnp.int32, (BLK, LW), 1)
        dk_ref[...] = jnp.zeros_like(dk_ref)
        dv_ref[...] = jnp.zeros_like(dv_ref)
        db_ref[...] = jnp.zeros_like(db_ref)

        def column(tile_, j):
            return jnp.sum(jnp.where(lane == j * HD, tile_, 0.0), axis=1, keepdims=True)

        for n, rows, keys in blocks():
            qb, kc, vc = q_ref[rows, :].astype(BF16), k_ref[keys, :].astype(BF16), v_ref[keys, :].astype(BF16)
            dof, valid = do_ref[rows, :], _band(n)
            dob, terms = dof.astype(BF16), dl_ref[rows, :] - dof * o_ref[rows, :]
            dq = []
            for j in range(2):
                bias_j = b_ref[j] if n else b_ref[j][:, BLK:]
                corr = jnp.sum(terms * masks[j].astype(F32), axis=1, keepdims=True)
                qm, dom = qb * masks[j], dob * masks[j]
                s = lax.dot_general(qm, kc, NT, preferred_element_type=F32) * SCALE + bias_j
                p = jnp.where(valid, jnp.exp(s - column(l_ref[rows, :], j)), 0.0)
                dp = lax.dot_general(dom, vc, NT, preferred_element_type=F32)
                ds = p * (dp + corr)
                if n:
                    db_ref[j] += ds
                else:
                    db_ref[j, :, BLK:] += ds
                dsb = (ds * SCALE).astype(BF16)
                dq.append(jnp.dot(dsb, kc, preferred_element_type=F32))
                dk_ref[keys, :] += lax.dot_general(dsb, qm, TN, preferred_element_type=F32)
                dv_ref[keys, :] += lax.dot_general(p.astype(BF16), dom, TN, preferred_element_type=F32)
            dq_ref[rows, :] = jnp.where(left, dq[0], dq[1])

    shape = jax.ShapeDtypeStruct((T, RW), F32)
    return pl.pallas_call(
        kern, name=f"attn_bwd_{g}", grid=(HP,),
        in_specs=qkv + [bias_spec] + [tile] * 4, out_specs=[tile] * 3 + [bias_spec],
        out_shape=[shape] * 3 + [jax.ShapeDtypeStruct((HEADS, BLK, 2 * BLK), F32)],
        compiler_params=_cp(("parallel",)),
    )(pq, pq, pq, bias, do, o, lse, dlse)


NBUCKET = 32
NPAIR = BLK * 2 * BLK


def _relbias_table(rbT, onehotT):
    def kern(rb_ref, oh_ref, out_ref):
        out_ref[0] = sum(jnp.dot(p, oh_ref[0], preferred_element_type=F32) for p in _split3(rb_ref[0]))

    return pl.pallas_call(
        kern, name="relbias_table", grid=(3,),
        in_specs=[pl.BlockSpec((1, HEADS, NBUCKET), lambda g: (g, 0, 0)),
                  pl.BlockSpec((1, NBUCKET, NPAIR), lambda g: (g, 0, 0))],
        out_specs=pl.BlockSpec((1, HEADS, NPAIR), lambda g: (g, 0, 0)),
        out_shape=jax.ShapeDtypeStruct((3, HEADS, NPAIR), F32),
        compiler_params=_cp(("parallel",)),
    )(rbT, onehotT)


def _relbias_grad(db, onehotT):
    nt = (((1,), (1,)), ((), ()))

    def kern(db_ref, oh_ref, out_ref):
        hi, mid, _ = _split3(db_ref[0])
        out_ref[0] = (lax.dot_general(hi, oh_ref[0], nt, preferred_element_type=F32)
                      + lax.dot_general(mid, oh_ref[0], nt, preferred_element_type=F32))

    return pl.pallas_call(
        kern, name="relbias_grad", grid=(3,),
        in_specs=[pl.BlockSpec((1, HEADS, NPAIR), lambda g: (g, 0, 0)),
                  pl.BlockSpec((1, NBUCKET, NPAIR), lambda g: (g, 0, 0))],
        out_specs=pl.BlockSpec((1, HEADS, NBUCKET), lambda g: (g, 0, 0)),
        out_shape=jax.ShapeDtypeStruct((3, HEADS, NBUCKET), F32),
        compiler_params=_cp(("parallel",)),
    )(db, onehotT)


def _adamw(w, g, m, v):
    m2 = ADAM_B1 * m + (1.0 - ADAM_B1) * g
    v2 = ADAM_B2 * v + (1.0 - ADAM_B2) * (g * g)
    m_hat = m2 / (1.0 - ADAM_B1 ** ADAM_STEP)
    v_hat = v2 / (1.0 - ADAM_B2 ** ADAM_STEP)
    return -ADAM_LR * (m_hat / (jnp.sqrt(v_hat) + ADAM_EPS) + ADAM_WD * w), m2, v2


def _ada_mod(c_all, ada_w, ada_b_loc):
    def kern(c_ref, w_ref, b_ref, o_ref):
        c = c_ref[...]
        cond = c * jax.nn.sigmoid(c)
        o_ref[0] = jnp.dot(cond, w_ref[0], precision=HI, preferred_element_type=F32) + b_ref[0]

    ncol = ada_w.shape[2]
    return pl.pallas_call(
        kern, name="ada_mod", grid=(2,),
        in_specs=[pl.BlockSpec((NDEV, D), lambda i: (0, 0)),
                  pl.BlockSpec((1, D, ncol), lambda i: (i, 0, 0)),
                  pl.BlockSpec((1, 1, ncol), lambda i: (i, 0, 0))],
        out_specs=pl.BlockSpec((1, NDEV, ncol), lambda i: (i, 0, 0)),
        out_shape=jax.ShapeDtypeStruct((2, NDEV, ncol), F32),
        compiler_params=_cp(("parallel",)),
    )(c_all, ada_w, ada_b_loc.reshape(2, 1, ncol))


def _ada_grad_adamw(cT_all, dmod_loc, w, m, v):
    ncol = w.shape[2]
    tr = 256

    def kern(c_ref, d_ref, w_ref, m_ref, v_ref, g_ref, dl_ref, m2_ref, v2_ref):
        c = c_ref[...]
        cond = c * jax.nn.sigmoid(c)
        g = jnp.dot(cond, d_ref[0], precision=HI, preferred_element_type=F32)
        dl, m2, v2 = _adamw(w_ref[0], g, m_ref[0], v_ref[0])
        g_ref[0], dl_ref[0], m2_ref[0], v2_ref[0] = g, dl, m2, v2

    big = pl.BlockSpec((1, tr, ncol), lambda i, j: (i, j, 0))
    shp = jax.ShapeDtypeStruct(w.shape, F32)
    return pl.pallas_call(
        kern, name="ada_grad_adamw", grid=(2, D // tr),
        in_specs=[pl.BlockSpec((tr, NDEV), lambda i, j: (j, 0)),
                  pl.BlockSpec((1, NDEV, ncol), lambda i, j: (i, 0, 0)), big, big, big],
        out_specs=[big] * 4, out_shape=[shp] * 4,
        compiler_params=_cp(("parallel", "parallel")),
    )(cT_all, dmod_loc, w, m, v)


def _sum_adamw(recv, w, m, v, name, tr):
    S = recv.shape[0]
    R, C = w.shape
    assert R % tr == 0 and recv.shape[1:] == (R, C)

    def kern(r_ref, w_ref, m_ref, v_ref, g_ref, dl_ref, m2_ref, v2_ref):
        g = r_ref[0].astype(F32)
        for s in range(1, S):
            g = g + r_ref[s].astype(F32)
        dl, m2, v2 = _adamw(w_ref[...], g, m_ref[...], v_ref[...])
        g_ref[...], dl_ref[...], m2_ref[...], v2_ref[...] = g, dl, m2, v2

    flat = pl.BlockSpec((tr, C), lambda i: (i, 0))
    shp = jax.ShapeDtypeStruct((R, C), F32)
    return pl.pallas_call(
        kern, name=name, grid=(R // tr,),
        in_specs=[pl.BlockSpec((S, tr, C), lambda i: (0, i, 0)), flat, flat, flat],
        out_specs=[flat] * 4, out_shape=[shp] * 4,
        compiler_params=_cp(("parallel",)),
    )(recv, w, m, v)


def _pack(arrs, dtype, row_mult):
    flat = jnp.concatenate([a.reshape(-1).astype(dtype) for a in arrs])
    flat = jnp.pad(flat, (0, -flat.shape[0] % (128 * row_mult)))
    return flat.reshape(-1, 128)


def _pack8(arrs, dtype, row_mult):
    flat = jnp.concatenate([a.reshape(NDEV, -1).astype(dtype) for a in arrs], axis=1)
    flat = jnp.pad(flat, ((0, 0), (0, -flat.shape[1] % (128 * row_mult))))
    return flat.reshape(NDEV, -1, 128)


def _unpack(buf, shapes, lead=()):
    flat = buf.reshape(lead + (-1,))
    out, off = [], 0
    for s in shapes:
        n = math.prod(s)
        out.append(flat[..., off:off + n].reshape(lead + tuple(s)))
        off += n
    return out


def _to_chunks(full, kind):
    if kind == "col":
        x = full.reshape(full.shape[:-1] + (NDEV, full.shape[-1] // NDEV))
        return jnp.moveaxis(x, -2, 0)
    x = full.reshape(full.shape[:-2] + (NDEV, full.shape[-2] // NDEV, full.shape[-1]))
    return jnp.moveaxis(x, -3, 0)


def _from_chunks(g8, kind):
    if kind == "col":
        x = jnp.moveaxis(g8, 0, -2)
        return x.reshape(x.shape[:-2] + (x.shape[-2] * x.shape[-1],))
    x = jnp.moveaxis(g8, 0, -3)
    return x.reshape(x.shape[:-3] + (x.shape[-3] * x.shape[-2], x.shape[-1]))


def _pad_pa(x):
    z = lambda n: jnp.zeros(x.shape[:-1] + (n,), x.dtype)
    return jnp.concatenate([x[..., :1600], z(64), x[..., 1600:1664], z(64), x[..., 1664:1824], z(96)], -1)


def _unpad_pa(x):
    return jnp.concatenate([x[..., :1600], x[..., 1664:1728], x[..., 1792:1952]], -1)


AB_SEGMENTS = ((0, 1600, 0), (1600, 1664, 64), (1664, 1824, 128), (1824, 3360, PAB - 3360))
AB_SHARD = 3360 // NDEV


def _ab_in_padded(g8):
    blocks, at = [], 0
    for start, end, shift in AB_SEGMENTS:
        if start + shift > at:
            blocks.append(jnp.zeros((g8.shape[1], start + shift - at), g8.dtype))
        for j in range(start // AB_SHARD, (end - 1) // AB_SHARD + 1):
            lo, hi = max(start, j * AB_SHARD), min(end, (j + 1) * AB_SHARD)
            blocks.append(g8[j, :, lo - j * AB_SHARD:hi - j * AB_SHARD])
        at = end + shift
    return jnp.concatenate(blocks, axis=1)


def _ab_in_shards(padded):
    shards = []
    for j in range(NDEV):
        pieces = [padded[:, max(start, j * AB_SHARD) + shift:min(end, (j + 1) * AB_SHARD) + shift]
                  for start, end, shift in AB_SEGMENTS if max(start, j * AB_SHARD) < min(end, (j + 1) * AB_SHARD)]
        shards.append(jnp.concatenate(pieces, axis=1))
    return jnp.stack(shards)


def _pad_rows(x, n):
    return jnp.pad(x, ((0, n - x.shape[0]), (0, 0)))


def _bucket_tables():
    qi = jnp.arange(BLK)[:, None]
    ki = jnp.arange(2 * BLK)[None, :]
    rel = BLK + qi - ki
    tabs = []
    for dil in DILS:
        dist = jnp.clip(rel, 0, BLK) * dil
        logd = jnp.log(jnp.maximum(dist, 1).astype(F32) / 16) / math.log(2048 / 16)
        large = jnp.minimum(16 + (logd * 16).astype(jnp.int32), 31)
        tabs.append(jnp.where(dist < 16, dist, large))
    return jnp.stack(tabs)


SHARDED = (("ln_g", "col"), ("ln_b", "col"), ("ab_w_in", "col"), ("rw_w_up", "col"), ("rw_a_up", "col"),
           ("rw_g_up", "col"), ("sc_conv_w", "col"), ("ab_w_out", "row"), ("dil_w_qkv", "col"),
           ("dil_w_out", "col"), ("mlp_w1", "col"), ("mlp_w2", "row"))
FIRST = ("ab_w_in",)
LATER = ("ab_w_out", "dil_w_qkv", "dil_w_out", "mlp_w1", "mlp_w2")
GATHER_BF16 = FIRST + LATER
GATHER_F32 = ("rw_w_up", "rw_a_up", "rw_g_up", "sc_conv_w", "ln_g", "ln_b")
REPLICATED = ("ada_b", "rw_mu", "rw_w0", "rw_a0", "rw_k_k", "rw_k_a", "rw_r_k", "rw_lnx_g", "rw_lnx_b", "rel_bias")
WEIGHTS = ("ada_w", "ada_b", "ln_g", "ln_b", "ab_w_in", "rw_mu", "rw_w0", "rw_w_up", "rw_a0", "rw_a_up",
           "rw_g_up", "rw_k_k", "rw_k_a", "rw_r_k", "rw_lnx_g", "rw_lnx_b", "sc_conv_w", "ab_w_out",
           "dil_w_qkv", "dil_w_out", "rel_bias", "mlp_w1", "mlp_w2")


def _local_step(x0, tgt, mod, W, P, later_weights, early_grads):
    row = lambda a: a.reshape(1, -1)
    W = dict(W)
    m6 = mod.reshape(2, 6, 1, D)
    sc = [m6[0, 1], m6[0, 4], m6[1, 1], m6[1, 4]]
    sh = [m6[0, 0], m6[0, 3], m6[1, 0], m6[1, 3]]
    gt = [m6[0, 2], m6[0, 5], m6[1, 2], m6[1, 5]]
    lng = [row(P["ln_g"][0, 0]), row(P["ln_g"][0, 1]), row(P["ln_g"][1, 0]), row(P["ln_g"][1, 1])]
    lnb = [row(P["ln_b"][0, 0]), row(P["ln_b"][0, 1]), row(P["ln_b"][1, 0]), row(P["ln_b"][1, 1])]
    E = jnp.kron(jnp.eye(HEADS, dtype=BF16), jnp.ones((HD, HD), BF16))

    def mod_body(r, p):
        u = r[0] * (1.0 + p[0]) + p[1]
        return [u, u], []

    (u0, u0T), _ = _rows("modulate", mod_body, [x0], [sc[0], sh[0]], [(D, BF16), (D, BF16, "T")])

    def post_fwd_body(r, p):
        xn, un = _post_ln_mod(r[0], r[1], *p)
        return [xn, un, un], []

    def post_fwd(s, x, y):
        (xn, un, unT), _ = _rows(f"post_ln_{s}", post_fwd_body, [x, y],
                                 [gt[s], lng[s], lnb[s], sc[s + 1], sh[s + 1]],
                                 [(D, F32), (D, BF16), (D, BF16, "T")])
        return xn, un, unT

    def relu2(acc):
        a = jnp.maximum(acc, 0.0)
        return acc, a * a, a * a

    def relu2_bwd(acc, h):
        return (acc * (2.0 * jnp.maximum(h, 0.0)),)

    p = _mm("ab_in", u0, W["ab_w_in"])
    mu = _pad_pa(P["rw_mu"])
    mu_parts = [mu[:, :512], mu[:, 512:1024], mu[:, 1024:1536], mu[:, 1536:1664], mu[:, 1664:1792], mu[:, 1792:]]
    pre_params = mu_parts + [P["rw_w0"], _pad_rows(P["rw_w_up"], 128), P["rw_a0"], _pad_rows(P["rw_a_up"], 128),
                             _pad_rows(P["rw_g_up"], 256), P["rw_k_k"], P["rw_k_a"],
                             P["sc_conv_w"][0:1], P["sc_conv_w"][1:2], P["sc_conv_w"][2:3]]
    pieces = [(p, 512, 0), (p, 512, 1), (p, 512, 2), (p, 128, 12), (p, 128, 13), (p, 256, 7),
              (p, 512, 4), (p, 512, 5), (p, 512, 6)]
    shifted = [0, 1, 2, 3, 4, 5, 6, 8]
    pre_rows = pieces + [pieces[i] + ("prev",) for i in shifted]
    NPR = 19

    def pre_args(r):
        x, prev = r[:9], dict(zip(shifted, r[9:17]))
        down = lambda i, k: _shift_down(x[i], prev[i], k)
        return x[:6] + [down(i, 1) for i in range(6)] + x[6:9] + [down(6, 1), down(8, 1), down(6, 2), down(8, 2)]

    def pre_fwd_body(r, pp):
        outs = list(_pre_core(pp[0], *pre_args(r), *pp[1:]))
        return outs + [outs[3]], []

    (r_, w_, kh_, v_, a_, b_, gate_, yb, v_cols), _ = _rows(
        "rwkv_pre", pre_fwd_body, pre_rows, [E] + pre_params,
        [(RW, F32)] * 7 + [(RW, BF16), (RW, BF16, "cols3")], tm=256)
    scan_in = [r_, w_, kh_, a_, b_, v_cols]
    ysc, *saved = _scan_fwd(*scan_in)
    post_params = [P["rw_lnx_g"], P["rw_lnx_b"], P["rw_r_k"].reshape(1, RW)]

    def postmix_fwd_body(r, pp):
        return [_post_core(pp[0], *r, *pp[1:])], []

    (ya,), _ = _rows("rwkv_post", postmix_fwd_body, [ysc, r_, kh_, v_, gate_], [E] + post_params,
                     [(RW, BF16)])
    cat = jnp.concatenate([ya, yb], axis=1)
    W.update(later_weights(cat))
    y0 = _mm("ab_out", cat, W["ab_w_out"])
    x1, u1, u1T = post_fwd(0, x0, y0)

    h1, a1, a1T = _mm("mlp1_up_0", u1, W["mlp_w1"][0], out=(F32, BF16, (BF16, "T")), epi=relu2)
    y1 = _mm("mlp1_down_0", a1, W["mlp_w2"][0])
    x2, u2, u2T = post_fwd(1, x1, y1)

    pq = _mm("qkv", u2, W["dil_w_qkv"])
    onehotT = (_bucket_tables().reshape(3, 1, NPAIR) == jnp.arange(NBUCKET).reshape(1, NBUCKET, 1)).astype(BF16)
    rbT = P["rel_bias"].reshape(NBUCKET, 3, HEADS).transpose(1, 2, 0)
    bias = _relbias_table(rbT, onehotT).reshape(3, HEADS, BLK, 2 * BLK)
    og, lse = zip(*[_attn_fwd(pq, bias[g], g) for g in range(3)])

    def merge_fwd_body(r, pp):
        return [_merge_core(*r)], []

    (om,), _ = _rows("attn_merge", merge_fwd_body, list(og + lse), [], [(RW, BF16)])
    y2 = _mm("dil_out", om, W["dil_w_out"])
    x3, u3, u3T = post_fwd(2, x2, y2)

    h3, a3, a3T = _mm("mlp1_up_1", u3, W["mlp_w1"][1], out=(F32, BF16, (BF16, "T")), epi=relu2)
    y3 = _mm("mlp1_down_1", a3, W["mlp_w2"][1])

    def last_body(r, pp):
        x, y, tg = r
        xn, vjp = jax.vjp(_post_ln, x, y, *pp)
        err = xn - tg
        dx, dy, dg, dlg, dlb = vjp(err * (1.0 / D))
        loss = jnp.full((1, 128), (0.5 / D) * jnp.sum(err * err), F32)
        return [dx, dy], [loss, dg, dlg, dlb]

    (dxp, dy3), (loss_acc, dg3, dlng3, dlnb3) = _rows(
        "final_ln_loss", last_body, [x3, y3, tgt], [gt[3], lng[3], lnb[3]],
        [(D, F32), (D, BF16)], [(1, 128), (1, D), (1, D), (1, D)])

    G = {}
    dsc, dsh, dgt = [None] * 4, [None] * 4, [None] * 4
    dlng, dlnb = [None] * 4, [None] * 4
    dgt[3], dlng[3], dlnb[3] = dg3, dlng3, dlnb3

    def mlp_bwd(i, uT, h, aT, dy):
        dh = _mm(f"mlp_dh_{i}", dy, W["mlp_w2"][i], tb=True, out=(BF16,), epi=relu2_bwd, extras=(h,))
        gw2 = _mm(f"mlp_dw2_{i}", aT, dy)
        du = _mm(f"mlp_du_{i}", dh, W["mlp_w1"][i], tb=True)
        gw1 = _mm(f"mlp_dw1_{i}", uT, dh)
        return du, gw1, gw2

    def post_bwd_body(r, pp):
        x, y, dxn, dun = r
        _, vjp = jax.vjp(_post_ln_mod, x, y, *pp)
        dx, dy, dg, dlg, dlb, dscn, dshn = vjp((dxn, dun))
        return [dx, dy], [dg, dlg, dlb, dscn, dshn]

    def post_bwd(s, x, y, dxn, dun):
        (dx, dy), (dgt[s], dlng[s], dlnb[s], dsc[s + 1], dsh[s + 1]) = _rows(
            f"post_ln_bwd_{s}", post_bwd_body, [x, y, dxn, dun],
            [gt[s], lng[s], lnb[s], sc[s + 1], sh[s + 1]], [(D, F32), (D, BF16)], [(1, D)] * 5)
        return dx, dy

    du3, gw1_1, gw2_1 = mlp_bwd(1, u3T, h3, a3T, dy3)
    dxp, dy2 = post_bwd(2, x2, y2, dxp, du3)

    G["dil_w_out"] = _mm("dil_out_dw", om.T, dy2)[None]
    do = _mm("dil_out_dx", dy2, W["dil_w_out"], tb=True)

    def merge_bwd_body(r, pp):
        _, vjp = jax.vjp(_merge_core, *r[:6])
        return list(vjp(r[6])), []

    mb, _ = _rows("attn_merge_bwd", merge_bwd_body, list(og + lse) + [do], [], [(RW, F32)] * 6)
    back = [_attn_bwd(pq, bias[g], mb[g], og[g], lse[g], mb[3 + g], g) for g in range(3)]
    dpq = jnp.concatenate([t for dq, dk, dv, _ in back for t in (dq, dk, dv)], axis=1).astype(BF16)
    rb = _relbias_grad(jnp.stack([b[3] for b in back]).reshape(3, HEADS, NPAIR), onehotT)
    G["rel_bias"] = rb.transpose(2, 0, 1).reshape(NBUCKET, 3 * HEADS)
    G["dil_w_qkv"] = _mm("qkv_dw", u2T, dpq)[None]
    du2 = _mm("qkv_dx", dpq, W["dil_w_qkv"], tb=True)
    dxp, dy1 = post_bwd(1, x1, y1, dxp, du2)

    du1, gw1_0, gw2_0 = mlp_bwd(0, u1T, h1, a1T, dy1)
    G["mlp_w1"] = jnp.stack([gw1_0, gw1_1])
    G["mlp_w2"] = jnp.stack([gw2_0, gw2_1])
    dxp, dy0 = post_bwd(0, x0, y0, dxp, du1)

    G["ab_w_out"] = _mm("ab_out_dw", cat.T, dy0)[None]
    dcat = _mm("ab_out_dx", dy0, W["ab_w_out"], tb=True)
    post_params = [post_params[0] + early_grads(G)] + post_params[1:]

    def postmix_bwd_body(r, pp):
        _, vjp = jax.vjp(functools.partial(_post_core, pp[0]), *r[:5], *pp[1:])
        d = vjp(r[5])
        return list(d[:5]), list(d[5:])

    (dy_cols, dr1, dkh1, dv1, dgate), (G["rw_lnx_g"], G["rw_lnx_b"], drk) = _rows(
        "rwkv_post_bwd", postmix_bwd_body, [ysc, r_, kh_, v_, gate_, (dcat, 512, 0)], [E] + post_params,
        [(RW, BF16, "cols3")] + [(RW, F32)] * 4, [(1, RW)] * 3)
    G["rw_r_k"] = drk.reshape(1, HEADS, HD)
    dr2, dw2, dk2, da2, db2, dv2 = _scan_bwd(*scan_in, dy_cols, *saved)

    def pre_bwd_body(r, pp):
        prim, ct = pre_args(r[:len(pre_rows)]), r[len(pre_rows):]
        _, vjp = jax.vjp(functools.partial(_pre_core, pp[0]), *prim, *pp[1:])
        cts = (ct[0] + ct[1], ct[2], ct[3] + ct[4], ct[5] + ct[6], ct[7], ct[8], ct[9], ct[10])
        d = vjp(cts)
        z = jnp.zeros_like(d[12])
        dp = jnp.concatenate([d[0], d[1], d[2], d[3], d[4], d[5], d[12], d[13], d[14]], axis=1)
        dp1 = jnp.concatenate([d[6], d[7], d[8], d[9], d[10], d[11], d[15], z, d[16]], axis=1)
        dp2 = jnp.concatenate([d[17], z, d[18]], axis=1)
        return [dp, dp1, dp2], list(d[NPR:])

    acc_shapes = [a.shape for a in pre_params]
    (dp, dp1, dp2), pacc = _rows(
        "rwkv_pre_bwd", pre_bwd_body,
        pre_rows + [dr1, dr2, dw2, dkh1, dk2, dv1, dv2, da2, db2, dgate, (dcat, 512, 1)],
        [E] + pre_params, [(PAB, F32), (PAB, F32), (PB, F32)], acc_shapes, tm=256)
    G["rw_mu"] = _unpad_pa(jnp.concatenate(pacc[:6], axis=1))
    G["rw_w0"], G["rw_a0"], G["rw_k_k"], G["rw_k_a"] = pacc[6], pacc[8], pacc[11], pacc[12]
    G["rw_w_up"] = pacc[7][None, :64]
    G["rw_a_up"] = pacc[9][None, :64]
    G["rw_g_up"] = pacc[10][None, :160]
    G["sc_conv_w"] = jnp.concatenate(pacc[13:16], axis=0)[None]

    def shift_merge_body(r, pp):
        d0, d1, d1_next, d2, d2_next = r
        d = d0 + _shift_up(d1, d1_next, 1)
        return [jnp.concatenate([d[:, :PA], d[:, PA:] + _shift_up(d2, d2_next, 2)], axis=1)], []

    (dpt,), _ = _rows("shift_merge", shift_merge_body,
                      [dp, dp1, (dp1, PAB, 0, "next"), dp2, (dp2, PB, 0, "next")], [], [(PAB, BF16)], tm=256)
    du0 = _mm("ab_in_dx", dpt, W["ab_w_in"], tb=True)

    def mod_bwd_body(r, pp):
        du, dx, x = r
        return [dx + du * (1.0 + pp[0])], [jnp.sum(du * x, axis=0, keepdims=True), jnp.sum(du, axis=0, keepdims=True)]

    (grad_x,), (dsc[0], dsh[0]) = _rows("modulate_bwd", mod_bwd_body, [du0, dxp, x0], [sc[0]], [(D, F32)],
                                        [(1, D), (1, D)])

    G["ln_g"] = jnp.concatenate(dlng, axis=0).reshape(2, 2, D)
    G["ln_b"] = jnp.concatenate(dlnb, axis=0).reshape(2, 2, D)
    dmod = jnp.concatenate([dsh[0], dsc[0], dgt[0], dsh[1], dsc[1], dgt[1],
                            dsh[2], dsc[2], dgt[2], dsh[3], dsc[3], dgt[3]], axis=1).reshape(2, 6 * D)
    return loss_acc[0, 0], grad_x, dmod, G, lambda: _ab_in_shards(_mm("ab_in_dw", u0T, dpt))[:, None]


def kernel(x, c, ada_w, ada_b, ln_g, ln_b, ab_w_in, rw_mu, rw_w0, rw_w_up, rw_a0, rw_a_up, rw_g_up, rw_k_k, rw_k_a, rw_r_k, rw_lnx_g, rw_lnx_b, sc_conv_w, ab_w_out, dil_w_qkv, dil_w_out, rel_bias, mlp_w1, mlp_w2, loss_target, m_ada_w, m_ada_b, m_ln_g, m_ln_b, m_ab_w_in, m_rw_mu, m_rw_w0, m_rw_w_up, m_rw_a0, m_rw_a_up, m_rw_g_up, m_rw_k_k, m_rw_k_a, m_rw_r_k, m_rw_lnx_g, m_rw_lnx_b, m_sc_conv_w, m_ab_w_out, m_dil_w_qkv, m_dil_w_out, m_rel_bias, m_mlp_w1, m_mlp_w2, v_ada_w, v_ada_b, v_ln_g, v_ln_b, v_ab_w_in, v_rw_mu, v_rw_w0, v_rw_w_up, v_rw_a0, v_rw_a_up, v_rw_g_up, v_rw_k_k, v_rw_k_a, v_rw_r_k, v_rw_lnx_g, v_rw_lnx_b, v_sc_conv_w, v_ab_w_out, v_dil_w_qkv, v_dil_w_out, v_rel_bias, v_mlp_w1, v_mlp_w2):
    w = dict(ada_w=ada_w, ada_b=ada_b, ln_g=ln_g, ln_b=ln_b, ab_w_in=ab_w_in, rw_mu=rw_mu, rw_w0=rw_w0,
             rw_w_up=rw_w_up, rw_a0=rw_a0, rw_a_up=rw_a_up, rw_g_up=rw_g_up, rw_k_k=rw_k_k, rw_k_a=rw_k_a,
             rw_r_k=rw_r_k, rw_lnx_g=rw_lnx_g, rw_lnx_b=rw_lnx_b, sc_conv_w=sc_conv_w, ab_w_out=ab_w_out,
             dil_w_qkv=dil_w_qkv, dil_w_out=dil_w_out, rel_bias=rel_bias, mlp_w1=mlp_w1, mlp_w2=mlp_w2)
    m = dict(ada_w=m_ada_w, ada_b=m_ada_b, ln_g=m_ln_g, ln_b=m_ln_b, ab_w_in=m_ab_w_in, rw_mu=m_rw_mu,
             rw_w0=m_rw_w0, rw_w_up=m_rw_w_up, rw_a0=m_rw_a0, rw_a_up=m_rw_a_up, rw_g_up=m_rw_g_up,
             rw_k_k=m_rw_k_k, rw_k_a=m_rw_k_a, rw_r_k=m_rw_r_k, rw_lnx_g=m_rw_lnx_g, rw_lnx_b=m_rw_lnx_b,
             sc_conv_w=m_sc_conv_w, ab_w_out=m_ab_w_out, dil_w_qkv=m_dil_w_qkv, dil_w_out=m_dil_w_out,
             rel_bias=m_rel_bias, mlp_w1=m_mlp_w1, mlp_w2=m_mlp_w2)
    v = dict(ada_w=v_ada_w, ada_b=v_ada_b, ln_g=v_ln_g, ln_b=v_ln_b, ab_w_in=v_ab_w_in, rw_mu=v_rw_mu,
             rw_w0=v_rw_w0, rw_w_up=v_rw_w_up, rw_a0=v_rw_a0, rw_a_up=v_rw_a_up, rw_g_up=v_rw_g_up,
             rw_k_k=v_rw_k_k, rw_k_a=v_rw_k_a, rw_r_k=v_rw_r_k, rw_lnx_g=v_rw_lnx_g, rw_lnx_b=v_rw_lnx_b,
             sc_conv_w=v_sc_conv_w, ab_w_out=v_ab_w_out, dil_w_qkv=v_dil_w_qkv, dil_w_out=v_dil_w_out,
             rel_bias=v_rel_bias, mlp_w1=v_mlp_w1, mlp_w2=v_mlp_w2)
    kinds = dict(SHARDED)
    me = 4 * lax.axis_index("x") + 2 * lax.axis_index("y") + lax.axis_index("c")
    ncol = ada_w.shape[2]

    small = _all_gather(_pack([c] + [w[n] for n in GATHER_F32], F32, 8), "gather_small")
    parts = _unpack(small, [c.shape] + [w[n].shape for n in GATHER_F32], (NDEV,))
    c_all = parts[0].reshape(NDEV, D)
    P = {n: _from_chunks(t, kinds[n]) for n, t in zip(GATHER_F32, parts[1:])}
    P = {n: (t if n in ("ln_g", "ln_b") else t[0]) for n, t in P.items()}
    for n in REPLICATED[1:]:
        P[n] = w[n]
    def full(n, t):
        t = _from_chunks(t, kinds[n])
        return t if n in ("mlp_w1", "mlp_w2") else t[0]

    (first,) = _all_gather_many([ab_w_in.astype(BF16)], "gather_first_weight")
    W = {"ab_w_in": _ab_in_padded(first[:, 0])}

    ada_b_loc = lax.dynamic_slice(ada_b, (0, ncol * me), (2, ncol))
    mod_part = _ada_mod(c_all, ada_w, ada_b_loc)
    mod_all = _all_gather(mod_part.reshape(-1, 128), "gather_mod").reshape(NDEV, 2, NDEV, ncol)
    mod = lax.dynamic_index_in_dim(mod_all, me, axis=2, keepdims=False)
    mod = mod.transpose(1, 0, 2).reshape(2, 6 * D)

    behind = (mod[0, 0] * 0.0).astype(BF16)
    later = _exchange_start([w[n].astype(BF16) + (behind if n == LATER[0] else 0) for n in LATER], True,
                            "gather_later_weights_start")
    mod = mod + later[-1][0, 0]

    def later_weights(after):
        lands = _exchange_wait(later, True, after, "gather_later_weights_wait")
        return {n: full(n, t) for n, t in zip(LATER, lands)}

    sent = []

    def early_grads(G):
        sent.append(_exchange_start([_to_chunks(G[n], kinds[n]).astype(BF16) for n in LATER], False,
                                    "exchange_later_grads_start"))
        return sent[0][-1][0, 0]

    loss_part, grad_x, dmod, G, in_grad = _local_step(x[0], loss_target[0], mod, W, P, later_weights, early_grads)
    G["ada_b"] = dmod
    big_out = {}

    def update(n, contributions):
        cols = w[n].shape[-1]
        flat = lambda t: t.reshape(-1, cols)
        rows = flat(w[n]).shape[0]
        outs = _sum_adamw(contributions.reshape(-1, rows, cols), flat(w[n]), flat(m[n]), flat(v[n]),
                          f"sum_adamw_{n}", min(rows, 256))
        big_out[n] = [o.reshape(w[n].shape) for o in outs]

    rep_shapes = [w[n].shape for n in REPLICATED] + [(1,)]
    rep_all = _all_gather(_pack([G[n] for n in REPLICATED] + [loss_part], F32, 8), "gather_replicated_grads")
    names = [n for n, _ in SHARDED if n not in GATHER_BF16]
    shard_shapes = [w[n].shape for n in names]
    recv = _all_to_all(_pack8([_to_chunks(G[n], kinds[n]) for n in names], F32, 8), "exchange_small_grads")

    behind = (recv[0, 0, 0] * 0.0 + rep_all[0, 0, 0] * 0.0).astype(BF16)
    last = _exchange_start([in_grad().astype(BF16) + behind], False, "exchange_last_grad_start")

    zero = last[-1][0:1, 0]
    pk = lambda d: _pack([d[n] for n in REPLICATED] + [zero], F32, 8)
    rep_out = _sum_adamw(rep_all, pk(w), pk(m), pk(v), "sum_adamw_replicated", rep_all.shape[1])
    loss = _unpack(rep_out[0], rep_shapes)[-1][0]
    rep_out = [dict(zip(REPLICATED, _unpack(o, rep_shapes))) for o in rep_out]
    dmod_all = _unpack(rep_all, [(2, 6 * D)], (NDEV,))[0]
    dmod_loc = lax.dynamic_slice(dmod_all, (0, 0, ncol * me), (NDEV, 2, ncol)).transpose(1, 0, 2)
    ada_out = _ada_grad_adamw(c_all.T + zero, dmod_loc, ada_w, m_ada_w, v_ada_w)
    pk = lambda d: _pack([d[n] for n in names], F32, 8)
    sh_out = _sum_adamw(recv, pk(w), pk(m), pk(v), "sum_adamw_small", recv.shape[1])
    sh_out = [dict(zip(names, _unpack(o, shard_shapes))) for o in sh_out]
    for n, r in zip(LATER, _exchange_wait(sent[0], False, last[-1], "exchange_later_grads_wait")):
        update(n, r)
    (landed,) = _exchange_wait(last, False, big_out[LATER[-1]][0], "exchange_last_grad_wait")
    update("ab_w_in", landed)
    sh_out = [{**d, **{n: big_out[n][i] for n in GATHER_BF16}} for i, d in enumerate(sh_out)]

    def pick(i, n):
        if n == "ada_w":
            return ada_out[i]
        return rep_out[i][n] if n in REPLICATED else sh_out[i][n]

    outs = [loss, grad_x[None]]
    for i in range(4):
        outs += [pick(i, n) for n in WEIGHTS]
    return tuple(outs)
```
